```python
import jax
import jax.numpy as jnp
from jax import lax
import numpy as np

D_MODEL = 1024
BATCH = 8
SEQ = 2048
DEPTH = 4

CHUNK = 64
Q_BLOCK = 128
N_HEADS = 4
HEAD_DIM = 128
BRANCH_W = N_HEADS * HEAD_DIM
RET_DK = HEAD_DIM
RET_DV = HEAD_DIM
GLA_DK = HEAD_DIM // 2
GLA_DV = HEAD_DIM
GLA_LOWRANK = 16
GLA_TAU = 16.0
FOX_D = HEAD_DIM
N_BRANCH = 3
D_FF = 2816
CONV_W = 3
ROPE_BASE = 10000.0
EPS = 1e-6

IN_SPLITS = (
    N_HEADS * RET_DK, N_HEADS * RET_DK, N_HEADS * RET_DV, N_HEADS * RET_DV,
    N_HEADS * GLA_DK, N_HEADS * GLA_DK, N_HEADS * GLA_DV, GLA_LOWRANK, N_HEADS * GLA_DV,
    N_HEADS * FOX_D, N_HEADS * FOX_D, N_HEADS * FOX_D, N_HEADS,
)
IN_W = sum(IN_SPLITS)

kernel_name = 'hybrid_ret_gla_fox_adaln_convffn'

F32 = jnp.float32


def rms_norm(x, g):
    xf = x.astype(F32)
    y = xf * lax.rsqrt(jnp.mean(xf * xf, axis=-1, keepdims=True) + EPS)
    return (y * g).astype(x.dtype)


def head_group_norm(x, g):
    xf = x.astype(F32)
    mu = jnp.mean(xf, axis=-1, keepdims=True)
    xc = xf - mu
    return xc * lax.rsqrt(jnp.mean(xc * xc, axis=-1, keepdims=True) + EPS) * g


def modulate(h, shift, scale):
    return h * (1.0 + scale[:, None, :]) + shift[:, None, :]


def rotary(x, pos):
    half = x.shape[-1] // 2
    inv_freq = ROPE_BASE ** (-jnp.arange(half, dtype=F32) / half)
    ang = pos[:, None] * inv_freq[None, :]
    cos = jnp.cos(ang)[None, :, None, :]
    sin = jnp.sin(ang)[None, :, None, :]
    x1, x2 = x[..., :half], x[..., half:]
    return jnp.concatenate([x1 * cos - x2 * sin, x1 * sin + x2 * cos], axis=-1)


def retention(q, k, v):
    b, s, h, dk = q.shape
    dv = v.shape[-1]
    n = s // CHUNK
    pos = jnp.arange(s, dtype=F32)
    q = rotary(q.astype(F32), pos)
    k = rotary(k.astype(F32), pos) * dk ** -0.5
    v = v.astype(F32)
    log_g = jnp.log1p(-jnp.exp2(-5.0 - jnp.arange(h, dtype=F32)))
    idx = jnp.arange(CHUNK, dtype=F32)
    d_intra = jnp.exp(jnp.abs(idx[:, None] - idx[None, :])[None] * log_g[:, None, None])
    qc = q.reshape(b, n, CHUNK, h, dk)
    kc = k.reshape(b, n, CHUNK, h, dk)
    vc = v.reshape(b, n, CHUNK, h, dv)
    scores = jnp.einsum('bnihd,bnjhd->bnhij', qc, kc) * d_intra
    o_intra = jnp.einsum('bnhij,bnjhe->bnihe', scores, vc)
    k_w = jnp.exp((CHUNK - 1.0 - idx)[:, None] * log_g[None, :])
    kv = jnp.einsum('bnjhd,bnjhe->nbhde', kc * k_w[:, :, None], vc)
    g_chunk = jnp.exp(CHUNK * log_g)[None, :, None, None]

    def step(r, kv_n):
        return g_chunk * r + kv_n, r

    _, r_prev = lax.scan(step, jnp.zeros((b, h, dk, dv), F32), kv)
    q_w = jnp.exp((idx + 1.0)[:, None] * log_g[None, :])
    o_cross = jnp.einsum('bnihd,nbhde->bnihe', qc * q_w[:, :, None], r_prev)
    return (o_intra + o_cross).reshape(b, s, h, dv)


def gla(q, k, v, log_a):
    b, s, h, dk = q.shape
    dv = v.shape[-1]
    n = s // CHUNK
    qc = (q.astype(F32) * dk ** -0.5).reshape(b, n, CHUNK, h, dk)
    kc = k.astype(F32).reshape(b, n, CHUNK, h, dk)
    vc = v.astype(F32).reshape(b, n, CHUNK, h, dv)
    la = log_a.astype(F32).reshape(b, n, CHUNK, h, dk)
    b_cum = jnp.cumsum(la, axis=2)
    b_end = b_cum[:, :, -1:]
    kv = jnp.einsum('bnjhd,bnjhe->nbhde', kc * jnp.exp(b_end - b_cum), vc)
    a = jnp.exp(b_end[:, :, 0]).transpose(1, 0, 2, 3)

    def step(st, inp):
        a_n, kv_n = inp
        st = a_n[..., None] * st + kv_n
        return st, st

    _, s_all = lax.scan(step, jnp.zeros((b, h, dk, dv), F32), (a, kv))
    o = jnp.einsum('bnihd,nbhde->bnihe', qc, s_all)
    return o.reshape(b, s, h, dv)


def forgetting_attention(q, k, v, f_logit):
    b, s, h, d = q.shape
    log_f = jax.nn.log_sigmoid(f_logit.astype(F32))
    cum = jnp.cumsum(log_f, axis=1).transpose(0, 2, 1)
    outs = []
    for blk in range(s // Q_BLOCK):
        q0 = blk * Q_BLOCK
        q1 = q0 + Q_BLOCK
        logits = jnp.einsum('bihd,bjhd->bhij', q[:, q0:q1], k[:, :q1]).astype(F32) * d ** -0.5
        logits = logits + cum[:, :, q0:q1, None] - cum[:, :, None, :q1]
        mask = jnp.arange(q0, q1)[:, None] >= jnp.arange(q1)[None, :]
        p = jax.nn.softmax(jnp.where(mask, logits, -jnp.inf), axis=-1).astype(v.dtype)
        outs.append(jnp.einsum('bhij,bjhe->bihe', p, v[:, :q1]))
    return jnp.concatenate(outs, axis=1)


def causal_dwconv(u, w, bias):
    s = u.shape[1]
    up = jnp.pad(u, ((0, 0), (CONV_W - 1, 0), (0, 0)))
    out = bias
    for j in range(CONV_W):
        out = out + w[j] * up[:, j:j + s]
    return out


def hybrid_layer(x, c_act, norm1_g, norm2_g, w_ada, b_ada, w_in, w_gla_a2, b_gla_a, b_fox_f,
                 ret_norm_g, gla_norm_g, q_norm_g, k_norm_g, w_br, w_mg, b_mg, w_o,
                 w_up, w_conv, b_conv, w_down):
    b, s = x.shape[0], x.shape[1]
    mod = c_act @ w_ada + b_ada
    shift1, scale1, gate1, shift2, scale2, gate2 = jnp.split(mod, 6, axis=-1)

    h = modulate(rms_norm(x, norm1_g), shift1, scale1)
    split_points = np.cumsum(IN_SPLITS)[:-1].tolist()
    (rq, rk, rv, rg, gq, gk, gv, glr, gg, fq, fk, fv, ff) = jnp.split(h @ w_in, split_points, axis=-1)

    def heads(t, d):
        return t.reshape(b, s, N_HEADS, d)

    ret = retention(heads(rq, RET_DK), heads(rk, RET_DK), heads(rv, RET_DV))
    ret = head_group_norm(ret, ret_norm_g.reshape(N_HEADS, RET_DV)).reshape(b, s, BRANCH_W)
    ret = jax.nn.silu(rg) * ret

    log_a = jax.nn.log_sigmoid((glr @ w_gla_a2 + b_gla_a).astype(F32)) / GLA_TAU
    gla_o = gla(heads(gq, GLA_DK), heads(gk, GLA_DK), heads(gv, GLA_DV), heads(log_a, GLA_DK))
    gla_o = rms_norm(gla_o, gla_norm_g).reshape(b, s, BRANCH_W)
    gla_o = jax.nn.silu(gg) * gla_o

    fox_o = forgetting_attention(rms_norm(heads(fq, FOX_D), q_norm_g),
                                 rms_norm(heads(fk, FOX_D), k_norm_g),
                                 heads(fv, FOX_D), ff + b_fox_f).reshape(b, s, BRANCH_W)

    branches = jnp.stack([ret, gla_o, fox_o], axis=2).astype(h.dtype)
    y_br = jnp.einsum('bsnw,nwd->bsnd', branches, w_br)
    gates = jax.nn.sigmoid(h @ w_mg + b_mg).reshape(b, s, N_BRANCH, D_MODEL)
    mixed = jnp.sum(gates * y_br, axis=2) @ w_o
    x = x + (gate1[:, None, :] * mixed).astype(x.dtype)

    h2 = modulate(rms_norm(x, norm2_g), shift2, scale2)
    u, g = jnp.split(h2 @ w_up, 2, axis=-1)
    u = causal_dwconv(u, w_conv, b_conv)
    y = (jax.nn.silu(u) * g) @ w_down
    x = x + (gate2[:, None, :] * y).astype(x.dtype)
    return x


def _fwd_setup_inputs(seed: int = 0) -> dict:
    key = jax.random.key(seed)
    ks = jax.random.split(key, 24)

    def nrm(k, shape, scale):
        return jax.random.normal(k, shape, F32) * scale

    L, D = DEPTH, D_MODEL
    return {
        'x': nrm(ks[0], (BATCH, SEQ, D), 1.0),
        'c': nrm(ks[1], (BATCH, D), 1.0),
        'norm1_g': 1.0 + nrm(ks[2], (L, D), 0.02),
        'norm2_g': 1.0 + nrm(ks[3], (L, D), 0.02),
        'w_ada': nrm(ks[4], (L, D, 6 * D), 0.5 * D ** -0.5),
        'b_ada': nrm(ks[5], (L, 6 * D), 0.02),
        'w_in': nrm(ks[6], (L, D, IN_W), D ** -0.5),
        'w_gla_a2': nrm(ks[7], (L, GLA_LOWRANK, N_HEADS * GLA_DK), GLA_LOWRANK ** -0.5),
        'b_gla_a': nrm(ks[8], (L, N_HEADS * GLA_DK), 0.1),
        'b_fox_f': 1.0 + nrm(ks[9], (L, N_HEADS), 0.1),
        'ret_norm_g': 1.0 + nrm(ks[10], (L, N_HEADS * RET_DV), 0.02),
        'gla_norm_g': 1.0 + nrm(ks[11], (L, GLA_DV), 0.02),
        'q_norm_g': 1.0 + nrm(ks[12], (L, FOX_D), 0.02),
        'k_norm_g': 1.0 + nrm(ks[13], (L, FOX_D), 0.02),
        'w_br': nrm(ks[14], (L, N_BRANCH, BRANCH_W, D), BRANCH_W ** -0.5),
        'w_mg': nrm(ks[15], (L, D, N_BRANCH * D), D ** -0.5),
        'b_mg': nrm(ks[16], (L, N_BRANCH * D), 0.02),
        'w_o': nrm(ks[17], (L, D, D), D ** -0.5),
        'w_up': nrm(ks[18], (L, D, 2 * D_FF), D ** -0.5),
        'w_conv': nrm(ks[19], (L, CONV_W, D_FF), CONV_W ** -0.5),
        'b_conv': nrm(ks[20], (L, D_FF), 0.02),
        'w_down': nrm(ks[21], (L, D_FF, D), D_FF ** -0.5),
    }


def _fwd_reference(x, c, norm1_g, norm2_g, w_ada, b_ada, w_in, w_gla_a2, b_gla_a, b_fox_f,
              ret_norm_g, gla_norm_g, q_norm_g, k_norm_g, w_br, w_mg, b_mg, w_o,
              w_up, w_conv, b_conv, w_down):
    c_act = jax.nn.silu(c)
    for l in range(DEPTH):
        x = hybrid_layer(x, c_act, norm1_g[l], norm2_g[l], w_ada[l], b_ada[l], w_in[l],
                         w_gla_a2[l], b_gla_a[l], b_fox_f[l], ret_norm_g[l], gla_norm_g[l],
                         q_norm_g[l], k_norm_g[l], w_br[l], w_mg[l], b_mg[l], w_o[l],
                         w_up[l], w_conv[l], b_conv[l], w_down[l])
    return x


import jax as _jax
import jax.numpy as _jnp

TWIN_FORMAT = 'train_step'
FWD_PARAMS = ['x', 'c', 'norm1_g', 'norm2_g', 'w_ada', 'b_ada', 'w_in', 'w_gla_a2', 'b_gla_a', 'b_fox_f', 'ret_norm_g', 'gla_norm_g', 'q_norm_g', 'k_norm_g', 'w_br', 'w_mg', 'b_mg', 'w_o', 'w_up', 'w_conv', 'b_conv', 'w_down']
TWIN_WEIGHTS = ['norm1_g', 'norm2_g', 'w_ada', 'b_ada', 'w_in', 'w_gla_a2', 'b_gla_a', 'b_fox_f', 'ret_norm_g', 'gla_norm_g', 'q_norm_g', 'k_norm_g', 'w_br', 'w_mg', 'b_mg', 'w_o', 'w_up', 'w_conv', 'b_conv', 'w_down']
TWIN_DIFF_INPUT = 'x'
TWIN_INPUTS = ['x', 'c', 'norm1_g', 'norm2_g', 'w_ada', 'b_ada', 'w_in', 'w_gla_a2', 'b_gla_a', 'b_fox_f', 'ret_norm_g', 'gla_norm_g', 'q_norm_g', 'k_norm_g', 'w_br', 'w_mg', 'b_mg', 'w_o', 'w_up', 'w_conv', 'b_conv', 'w_down', 'loss_target', 'm_norm1_g', 'm_norm2_g', 'm_w_ada', 'm_b_ada', 'm_w_in', 'm_w_gla_a2', 'm_b_gla_a', 'm_b_fox_f', 'm_ret_norm_g', 'm_gla_norm_g', 'm_q_norm_g', 'm_k_norm_g', 'm_w_br', 'm_w_mg', 'm_b_mg', 'm_w_o', 'm_w_up', 'm_w_conv', 'm_b_conv', 'm_w_down', 'v_norm1_g', 'v_norm2_g', 'v_w_ada', 'v_b_ada', 'v_w_in', 'v_w_gla_a2', 'v_b_gla_a', 'v_b_fox_f', 'v_ret_norm_g', 'v_gla_norm_g', 'v_q_norm_g', 'v_k_norm_g', 'v_w_br', 'v_w_mg', 'v_b_mg', 'v_w_o', 'v_w_up', 'v_w_conv', 'v_b_conv', 'v_w_down']
TWIN_OUTPUTS = ['loss', 'grad_x', 'grad_norm1_g', 'grad_norm2_g', 'grad_w_ada', 'grad_b_ada', 'grad_w_in', 'grad_w_gla_a2', 'grad_b_gla_a', 'grad_b_fox_f', 'grad_ret_norm_g', 'grad_gla_norm_g', 'grad_q_norm_g', 'grad_k_norm_g', 'grad_w_br', 'grad_w_mg', 'grad_b_mg', 'grad_w_o', 'grad_w_up', 'grad_w_conv', 'grad_b_conv', 'grad_w_down', 'delta_norm1_g', 'delta_norm2_g', 'delta_w_ada', 'delta_b_ada', 'delta_w_in', 'delta_w_gla_a2', 'delta_b_gla_a', 'delta_b_fox_f', 'delta_ret_norm_g', 'delta_gla_norm_g', 'delta_q_norm_g', 'delta_k_norm_g', 'delta_w_br', 'delta_w_mg', 'delta_b_mg', 'delta_w_o', 'delta_w_up', 'delta_w_conv', 'delta_b_conv', 'delta_w_down', 'new_m_norm1_g', 'new_m_norm2_g', 'new_m_w_ada', 'new_m_b_ada', 'new_m_w_in', 'new_m_w_gla_a2', 'new_m_b_gla_a', 'new_m_b_fox_f', 'new_m_ret_norm_g', 'new_m_gla_norm_g', 'new_m_q_norm_g', 'new_m_k_norm_g', 'new_m_w_br', 'new_m_w_mg', 'new_m_b_mg', 'new_m_w_o', 'new_m_w_up', 'new_m_w_conv', 'new_m_b_conv', 'new_m_w_down', 'new_v_norm1_g', 'new_v_norm2_g', 'new_v_w_ada', 'new_v_b_ada', 'new_v_w_in', 'new_v_w_gla_a2', 'new_v_b_gla_a', 'new_v_b_fox_f', 'new_v_ret_norm_g', 'new_v_gla_norm_g', 'new_v_q_norm_g', 'new_v_k_norm_g', 'new_v_w_br', 'new_v_w_mg', 'new_v_b_mg', 'new_v_w_o', 'new_v_w_up', 'new_v_w_conv', 'new_v_b_conv', 'new_v_w_down']
TWIN_LEAF_KINDS = {'loss': 'loss', 'grad_x': 'grad_x', 'grad_norm1_g': 'grad_w', 'grad_norm2_g': 'grad_w', 'grad_w_ada': 'grad_w', 'grad_b_ada': 'grad_w', 'grad_w_in': 'grad_w', 'grad_w_gla_a2': 'grad_w', 'grad_b_gla_a': 'grad_w', 'grad_b_fox_f': 'grad_w', 'grad_ret_norm_g': 'grad_w', 'grad_gla_norm_g': 'grad_w', 'grad_q_norm_g': 'grad_w', 'grad_k_norm_g': 'grad_w', 'grad_w_br': 'grad_w', 'grad_w_mg': 'grad_w', 'grad_b_mg': 'grad_w', 'grad_w_o': 'grad_w', 'grad_w_up': 'grad_w', 'grad_w_conv': 'grad_w', 'grad_b_conv': 'grad_w', 'grad_w_down': 'grad_w', 'delta_norm1_g': 'delta_w', 'delta_norm2_g': 'delta_w', 'delta_w_ada': 'delta_w', 'delta_b_ada': 'delta_w', 'delta_w_in': 'delta_w', 'delta_w_gla_a2': 'delta_w', 'delta_b_gla_a': 'delta_w', 'delta_b_fox_f': 'delta_w', 'delta_ret_norm_g': 'delta_w', 'delta_gla_norm_g': 'delta_w', 'delta_q_norm_g': 'delta_w', 'delta_k_norm_g': 'delta_w', 'delta_w_br': 'delta_w', 'delta_w_mg': 'delta_w', 'delta_b_mg': 'delta_w', 'delta_w_o': 'delta_w', 'delta_w_up': 'delta_w', 'delta_w_conv': 'delta_w', 'delta_b_conv': 'delta_w', 'delta_w_down': 'delta_w', 'new_m_norm1_g': 'new_m', 'new_m_norm2_g': 'new_m', 'new_m_w_ada': 'new_m', 'new_m_b_ada': 'new_m', 'new_m_w_in': 'new_m', 'new_m_w_gla_a2': 'new_m', 'new_m_b_gla_a': 'new_m', 'new_m_b_fox_f': 'new_m', 'new_m_ret_norm_g': 'new_m', 'new_m_gla_norm_g': 'new_m', 'new_m_q_norm_g': 'new_m', 'new_m_k_norm_g': 'new_m', 'new_m_w_br': 'new_m', 'new_m_w_mg': 'new_m', 'new_m_b_mg': 'new_m', 'new_m_w_o': 'new_m', 'new_m_w_up': 'new_m', 'new_m_w_conv': 'new_m', 'new_m_b_conv': 'new_m', 'new_m_w_down': 'new_m', 'new_v_norm1_g': 'new_v', 'new_v_norm2_g': 'new_v', 'new_v_w_ada': 'new_v', 'new_v_b_ada': 'new_v', 'new_v_w_in': 'new_v', 'new_v_w_gla_a2': 'new_v', 'new_v_b_gla_a': 'new_v', 'new_v_b_fox_f': 'new_v', 'new_v_ret_norm_g': 'new_v', 'new_v_gla_norm_g': 'new_v', 'new_v_q_norm_g': 'new_v', 'new_v_k_norm_g': 'new_v', 'new_v_w_br': 'new_v', 'new_v_w_mg': 'new_v', 'new_v_b_mg': 'new_v', 'new_v_w_o': 'new_v', 'new_v_w_up': 'new_v', 'new_v_w_conv': 'new_v', 'new_v_b_conv': 'new_v', 'new_v_w_down': 'new_v'}


def _forward(args):
    return _fwd_reference(*[args[k] for k in FWD_PARAMS])


def _output_shape():
    out = _jax.eval_shape(lambda: _forward(_fwd_setup_inputs(0)))
    return out.shape, out.dtype

N_MICROBATCH = 1
ADAM_LR = 0.001
ADAM_B1 = 0.9
ADAM_B2 = 0.999
ADAM_EPS = 1e-08
ADAM_WD = 0.01
ADAM_STEP = 10
PER_EXAMPLE_BATCH_AXIS = {'x': 0, 'c': 0, 'loss_target': 0}
SHARED_INPUTS = []
_WEIGHT_DTYPES = {'norm1_g': _jnp.float32, 'norm2_g': _jnp.float32, 'w_ada': _jnp.float32, 'b_ada': _jnp.float32, 'w_in': _jnp.float32, 'w_gla_a2': _jnp.float32, 'b_gla_a': _jnp.float32, 'b_fox_f': _jnp.float32, 'ret_norm_g': _jnp.float32, 'gla_norm_g': _jnp.float32, 'q_norm_g': _jnp.float32, 'k_norm_g': _jnp.float32, 'w_br': _jnp.float32, 'w_mg': _jnp.float32, 'b_mg': _jnp.float32, 'w_o': _jnp.float32, 'w_up': _jnp.float32, 'w_conv': _jnp.float32, 'b_conv': _jnp.float32, 'w_down': _jnp.float32}
MOMENT_SCALE = {'norm1_g': 6.640551e-01, 'norm2_g': 1.718923e+00, 'w_ada': 5.009120e-01, 'b_ada': 1.085426e+00, 'w_in': 6.369768e-02, 'w_gla_a2': 1.116629e-02, 'b_gla_a': 3.095998e-02, 'b_fox_f': 8.654185e+00, 'ret_norm_g': 4.121928e-01, 'gla_norm_g': 1.601625e+00, 'q_norm_g': 2.355477e-01, 'k_norm_g': 2.366331e-01, 'w_br': 4.375415e-02, 'w_mg': 1.471065e-02, 'b_mg': 6.519418e-02, 'w_o': 7.458424e-02, 'w_up': 5.399354e-02, 'w_conv': 2.066458e-01, 'b_conv': 2.264912e-01, 'w_down': 6.568705e-02}


def _to_microbatches(a, axis):
    t = _jnp.moveaxis(a, axis, 0)
    t = t.reshape((N_MICROBATCH, t.shape[0] // N_MICROBATCH) + t.shape[1:])
    return _jnp.moveaxis(t, 1, axis + 1)


def setup_inputs(seed: int = 0) -> dict:
    inp = _fwd_setup_inputs(seed)
    key = _jax.random.fold_in(_jax.random.key(seed), 7919)
    shape, _ = _output_shape()
    out = dict(inp)
    out["loss_target"] = _jax.random.normal(_jax.random.fold_in(key, 0), shape, _jnp.float32)
    for i, name in enumerate(TWIN_WEIGHTS):
        w = inp[name].astype(_jnp.float32)
        if MOMENT_SCALE is None:
            s = _jnp.sqrt(_jnp.mean(_jnp.square(w)) + 1e-30)
        else:
            s = MOMENT_SCALE[name]
        km, kv = _jax.random.split(_jax.random.fold_in(key, i + 1))
        out[name] = w
        out["m_" + name] = s * _jax.random.normal(km, w.shape, _jnp.float32)
        out["v_" + name] = (s * s) * _jax.random.uniform(kv, w.shape, _jnp.float32, 0.5, 1.5)
    if N_MICROBATCH > 1:
        for name, axis in PER_EXAMPLE_BATCH_AXIS.items():
            out[name] = _to_microbatches(out[name], axis)
    return {'x': out['x'], 'c': out['c'], 'norm1_g': out['norm1_g'], 'norm2_g': out['norm2_g'], 'w_ada': out['w_ada'], 'b_ada': out['b_ada'], 'w_in': out['w_in'], 'w_gla_a2': out['w_gla_a2'], 'b_gla_a': out['b_gla_a'], 'b_fox_f': out['b_fox_f'], 'ret_norm_g': out['ret_norm_g'], 'gla_norm_g': out['gla_norm_g'], 'q_norm_g': out['q_norm_g'], 'k_norm_g': out['k_norm_g'], 'w_br': out['w_br'], 'w_mg': out['w_mg'], 'b_mg': out['b_mg'], 'w_o': out['w_o'], 'w_up': out['w_up'], 'w_conv': out['w_conv'], 'b_conv': out['b_conv'], 'w_down': out['w_down'], 'loss_target': out['loss_target'], 'm_norm1_g': out['m_norm1_g'], 'm_norm2_g': out['m_norm2_g'], 'm_w_ada': out['m_w_ada'], 'm_b_ada': out['m_b_ada'], 'm_w_in': out['m_w_in'], 'm_w_gla_a2': out['m_w_gla_a2'], 'm_b_gla_a': out['m_b_gla_a'], 'm_b_fox_f': out['m_b_fox_f'], 'm_ret_norm_g': out['m_ret_norm_g'], 'm_gla_norm_g': out['m_gla_norm_g'], 'm_q_norm_g': out['m_q_norm_g'], 'm_k_norm_g': out['m_k_norm_g'], 'm_w_br': out['m_w_br'], 'm_w_mg': out['m_w_mg'], 'm_b_mg': out['m_b_mg'], 'm_w_o': out['m_w_o'], 'm_w_up': out['m_w_up'], 'm_w_conv': out['m_w_conv'], 'm_b_conv': out['m_b_conv'], 'm_w_down': out['m_w_down'], 'v_norm1_g': out['v_norm1_g'], 'v_norm2_g': out['v_norm2_g'], 'v_w_ada': out['v_w_ada'], 'v_b_ada': out['v_b_ada'], 'v_w_in': out['v_w_in'], 'v_w_gla_a2': out['v_w_gla_a2'], 'v_b_gla_a': out['v_b_gla_a'], 'v_b_fox_f': out['v_b_fox_f'], 'v_ret_norm_g': out['v_ret_norm_g'], 'v_gla_norm_g': out['v_gla_norm_g'], 'v_q_norm_g': out['v_q_norm_g'], 'v_k_norm_g': out['v_k_norm_g'], 'v_w_br': out['v_w_br'], 'v_w_mg': out['v_w_mg'], 'v_b_mg': out['v_b_mg'], 'v_w_o': out['v_w_o'], 'v_w_up': out['v_w_up'], 'v_w_conv': out['v_w_conv'], 'v_b_conv': out['v_b_conv'], 'v_w_down': out['v_w_down']}


def _loss(weights, diff, rest, loss_target):
    with _jax.named_scope("forward"):
        args = {**rest, TWIN_DIFF_INPUT: diff, **{k: w.astype(_WEIGHT_DTYPES[k]) for k, w in weights.items()}}
        y = _forward(args)
    with _jax.named_scope("loss_head"):
        err = _jnp.square(y.astype(_jnp.float32) - loss_target)
        return 0.5 * _jnp.sum(_jnp.mean(err, axis=-1)) if err.ndim else 0.5 * err


def _adamw(w, g, m, v):
    m = ADAM_B1 * m + (1.0 - ADAM_B1) * g
    v = ADAM_B2 * v + (1.0 - ADAM_B2) * _jnp.square(g)
    m_hat = m / (1.0 - ADAM_B1 ** ADAM_STEP)
    v_hat = v / (1.0 - ADAM_B2 ** ADAM_STEP)
    delta = -ADAM_LR * (m_hat / (_jnp.sqrt(v_hat) + ADAM_EPS) + ADAM_WD * w)
    return delta, m, v


def reference(x, c, norm1_g, norm2_g, w_ada, b_ada, w_in, w_gla_a2, b_gla_a, b_fox_f, ret_norm_g, gla_norm_g, q_norm_g, k_norm_g, w_br, w_mg, b_mg, w_o, w_up, w_conv, b_conv, w_down, loss_target, m_norm1_g, m_norm2_g, m_w_ada, m_b_ada, m_w_in, m_w_gla_a2, m_b_gla_a, m_b_fox_f, m_ret_norm_g, m_gla_norm_g, m_q_norm_g, m_k_norm_g, m_w_br, m_w_mg, m_b_mg, m_w_o, m_w_up, m_w_conv, m_b_conv, m_w_down, v_norm1_g, v_norm2_g, v_w_ada, v_b_ada, v_w_in, v_w_gla_a2, v_b_gla_a, v_b_fox_f, v_ret_norm_g, v_gla_norm_g, v_q_norm_g, v_k_norm_g, v_w_br, v_w_mg, v_b_mg, v_w_o, v_w_up, v_w_conv, v_b_conv, v_w_down):
    given = dict(x=x, c=c, norm1_g=norm1_g, norm2_g=norm2_g, w_ada=w_ada, b_ada=b_ada, w_in=w_in, w_gla_a2=w_gla_a2, b_gla_a=b_gla_a, b_fox_f=b_fox_f, ret_norm_g=ret_norm_g, gla_norm_g=gla_norm_g, q_norm_g=q_norm_g, k_norm_g=k_norm_g, w_br=w_br, w_mg=w_mg, b_mg=b_mg, w_o=w_o, w_up=w_up, w_conv=w_conv, b_conv=b_conv, w_down=w_down, loss_target=loss_target, m_norm1_g=m_norm1_g, m_norm2_g=m_norm2_g, m_w_ada=m_w_ada, m_b_ada=m_b_ada, m_w_in=m_w_in, m_w_gla_a2=m_w_gla_a2, m_b_gla_a=m_b_gla_a, m_b_fox_f=m_b_fox_f, m_ret_norm_g=m_ret_norm_g, m_gla_norm_g=m_gla_norm_g, m_q_norm_g=m_q_norm_g, m_k_norm_g=m_k_norm_g, m_w_br=m_w_br, m_w_mg=m_w_mg, m_b_mg=m_b_mg, m_w_o=m_w_o, m_w_up=m_w_up, m_w_conv=m_w_conv, m_b_conv=m_b_conv, m_w_down=m_w_down, v_norm1_g=v_norm1_g, v_norm2_g=v_norm2_g, v_w_ada=v_w_ada, v_b_ada=v_b_ada, v_w_in=v_w_in, v_w_gla_a2=v_w_gla_a2, v_b_gla_a=v_b_gla_a, v_b_fox_f=v_b_fox_f, v_ret_norm_g=v_ret_norm_g, v_gla_norm_g=v_gla_norm_g, v_q_norm_g=v_q_norm_g, v_k_norm_g=v_k_norm_g, v_w_br=v_w_br, v_w_mg=v_w_mg, v_b_mg=v_b_mg, v_w_o=v_w_o, v_w_up=v_w_up, v_w_conv=v_w_conv, v_b_conv=v_b_conv, v_w_down=v_w_down)
    weights = {n: given[n] for n in TWIN_WEIGHTS}
    shared = {n: given[n] for n in SHARED_INPUTS}
    per_example = {n: given[n] for n in ['x', 'c']}
    grad_fn = _jax.value_and_grad(_loss, argnums=(0, 1))

    def one_microbatch(ex, loss_target):
        ex = dict(ex)
        diff = ex.pop(TWIN_DIFF_INPUT)
        return grad_fn(weights, diff, {**shared, **ex}, loss_target)

    if N_MICROBATCH == 1:
        loss, (grad_w, grad_x) = one_microbatch(per_example, given["loss_target"])
    else:
        def body(carry, xs):
            loss_sum, grad_sum = carry
            l_k, (gw_k, gx_k) = one_microbatch(xs[0], xs[1])
            with _jax.named_scope("update"):
                return (loss_sum + l_k, _jax.tree.map(_jnp.add, grad_sum, gw_k)), gx_k

        init = (_jnp.zeros((), _jnp.float32), _jax.tree.map(_jnp.zeros_like, weights))
        (loss, grad_w), grad_x = _jax.lax.scan(body, init, (per_example, given["loss_target"]))
    with _jax.named_scope("update"):
        delta_w, new_m, new_v = {}, {}, {}
        for n in TWIN_WEIGHTS:
            delta_w[n], new_m[n], new_v[n] = _adamw(weights[n], grad_w[n], given["m_" + n], given["v_" + n])
    return (loss, grad_x, *[grad_w[n] for n in TWIN_WEIGHTS], *[delta_w[n] for n in TWIN_WEIGHTS],
            *[new_m[n] for n in TWIN_WEIGHTS], *[new_v[n] for n in TWIN_WEIGHTS])
```

```python
import functools

import numpy as np
import jax
import jax.numpy as jnp
from jax import lax
from jax.experimental import pallas as pl
from jax.experimental.pallas import tpu as pltpu

F32 = jnp.float32
MMT = jnp.bfloat16
HI = lax.Precision.HIGHEST

D = 1024
DEPTH = 4
NH = 4
HD = 128
BW = NH * HD
CH = 64
GDK = 64
GLR = 16
DFF = 2816
EPS = 1e-6
ROPE_BASE = 10000.0

GP, RQ, RK, RV, RG, GQ, GK, GV, GG, FQ, FK, FV, LR, FF = (
    0, 3072, 3584, 4096, 4608, 5120, 5376, 5632, 6144, 6656, 7168, 7680, 8192, 8320)
NZZ = 8448
IN_W = 5140

VMEM_LIMIT = 56 * 1024 * 1024

ADAM_LR, ADAM_B1, ADAM_B2, ADAM_EPS, ADAM_WD, ADAM_STEP = 0.001, 0.9, 0.999, 1e-08, 0.01, 10


def _cparams(sem=None):
    return pltpu.CompilerParams(dimension_semantics=sem, vmem_limit_bytes=VMEM_LIMIT)


def _sds(shape, dtype):
    return jax.ShapeDtypeStruct(tuple(shape), dtype)


def _dot(a, b, precision=None):
    return lax.dot_general(a, b, (((1,), (0,)), ((), ())), precision=precision, preferred_element_type=F32)


def _dot_nt(a, b, precision=None):
    return lax.dot_general(a, b, (((1,), (1,)), ((), ())), precision=precision, preferred_element_type=F32)


def _dot_tn(a, b, precision=None):
    return lax.dot_general(a, b, (((0,), (0,)), ((), ())), precision=precision, preferred_element_type=F32)


def _silu(x):
    return x * jax.nn.sigmoid(x)


def _log_sigmoid(x):
    return jnp.minimum(x, 0.0) - jnp.log(1.0 + jnp.exp(jnp.minimum(x, -x)))


@jax.custom_vjp
def _swap_halves(x):
    return pltpu.roll(x, HD // 2, 1)


_swap_halves.defvjp(lambda x: (_swap_halves(x), None), lambda _, g: (_swap_halves(g),))


def mm_nn(a, b, *, tm, tn, out_dtype, name):
    m, k = a.shape
    n = b.shape[1]

    def body(a_ref, b_ref, o_ref):
        o_ref[...] = _dot(a_ref[...], b_ref[...]).astype(o_ref.dtype)

    return pl.pallas_call(
        body, grid=(m // tm, n // tn),
        in_specs=[pl.BlockSpec((tm, k), lambda i, j: (i, 0)), pl.BlockSpec((k, tn), lambda i, j: (0, j))],
        out_specs=pl.BlockSpec((tm, tn), lambda i, j: (i, j)),
        out_shape=_sds((m, n), out_dtype), compiler_params=_cparams(("parallel", "parallel")), name=name)(a, b)


def mm_nn_residual(a, b, res, gate, *, tm, tn, name):
    m, k = a.shape
    n = b.shape[1]

    def body(a_ref, b_ref, r_ref, g_ref, x_ref, y_ref):
        acc = _dot(a_ref[...], b_ref[...])
        y_ref[...] = acc
        x_ref[...] = r_ref[...] + g_ref[...] * acc

    return pl.pallas_call(
        body, grid=(m // tm, n // tn),
        in_specs=[pl.BlockSpec((tm, k), lambda i, j: (i, 0)), pl.BlockSpec((k, tn), lambda i, j: (0, j)),
                  pl.BlockSpec((tm, tn), lambda i, j: (i, j)), pl.BlockSpec((1, tn), lambda i, j: (0, j))],
        out_specs=[pl.BlockSpec((tm, tn), lambda i, j: (i, j)), pl.BlockSpec((tm, tn), lambda i, j: (i, j))],
        out_shape=[_sds((m, n), F32), _sds((m, n), F32)],
        compiler_params=_cparams(("parallel", "parallel")), name=name)(a, b, res, gate)


def mm_nt(a, b, *, tm, tn, out_dtype, name):
    m, k = a.shape
    n = b.shape[0]

    def body(a_ref, b_ref, o_ref):
        o_ref[...] = _dot_nt(a_ref[...], b_ref[...]).astype(o_ref.dtype)

    return pl.pallas_call(
        body, grid=(m // tm, n // tn),
        in_specs=[pl.BlockSpec((tm, k), lambda i, j: (i, 0)), pl.BlockSpec((tn, k), lambda i, j: (j, 0))],
        out_specs=pl.BlockSpec((tm, tn), lambda i, j: (i, j)),
        out_shape=_sds((m, n), out_dtype), compiler_params=_cparams(("parallel", "parallel")), name=name)(a, b)


def mm_tn(a, b, *, tm, tn, out_dtype, name):
    s, m = a.shape
    n = b.shape[1]

    def body(a_ref, b_ref, o_ref):
        o_ref[...] = _dot_tn(a_ref[...], b_ref[...]).astype(o_ref.dtype)

    return pl.pallas_call(
        body, grid=(m // tm, n // tn),
        in_specs=[pl.BlockSpec((s, tm), lambda i, j: (0, i)), pl.BlockSpec((s, tn), lambda i, j: (0, j))],
        out_specs=pl.BlockSpec((tm, tn), lambda i, j: (i, j)),
        out_shape=_sds((m, n), out_dtype), compiler_params=_cparams(("parallel", "parallel")), name=name)(a, b)


def _row_tile(s):
    return min(256, s)


def _norm_mod_f(x, g, scale, shift):
    r = lax.rsqrt(jnp.mean(x * x, axis=-1, keepdims=True) + EPS)
    return (x * r * g) * (1.0 + scale) + shift


def norm_mod(x, g, scale, shift, *, name):
    s = x.shape[0]
    t = _row_tile(s)

    def body(x_ref, g_ref, sc_ref, sh_ref, o_ref):
        o_ref[...] = _norm_mod_f(x_ref[...], g_ref[...], sc_ref[...], sh_ref[...]).astype(o_ref.dtype)

    vec = pl.BlockSpec((1, D), lambda i: (0, 0))
    return pl.pallas_call(
        body, grid=(s // t,), in_specs=[pl.BlockSpec((t, D), lambda i: (i, 0)), vec, vec, vec],
        out_specs=pl.BlockSpec((t, D), lambda i: (i, 0)), out_shape=_sds((s, D), MMT),
        compiler_params=_cparams(("parallel",)), name=name)(x, g, scale, shift)


def norm_mod_bwd(x, dh, dres, g, scale, shift, *, name):
    s = x.shape[0]
    t = _row_tile(s)

    def body(x_ref, dh_ref, dr_ref, g_ref, sc_ref, sh_ref, dx_ref, dg_ref, dsc_ref, dsh_ref):
        @pl.when(pl.program_id(0) == 0)
        def _():
            dg_ref[...] = jnp.zeros_like(dg_ref)
            dsc_ref[...] = jnp.zeros_like(dsc_ref)
            dsh_ref[...] = jnp.zeros_like(dsh_ref)

        _, vjp = jax.vjp(_norm_mod_f, x_ref[...], g_ref[...], sc_ref[...], sh_ref[...])
        dx, dg, dsc, dsh = vjp(dh_ref[...])
        dx_ref[...] = dr_ref[...] + dx
        dg_ref[...] += dg
        dsc_ref[...] += dsc
        dsh_ref[...] += dsh

    row = pl.BlockSpec((t, D), lambda i: (i, 0))
    vec = pl.BlockSpec((1, D), lambda i: (0, 0))
    return pl.pallas_call(
        body, grid=(s // t,), in_specs=[row, row, row, vec, vec, vec], out_specs=[row, vec, vec, vec],
        out_shape=[_sds((s, D), F32)] + [_sds((1, D), F32)] * 3,
        compiler_params=_cparams(("arbitrary",)), name=name)(x, dh, dres, g, scale, shift)


def gate_bwd(dx, y, gate, *, name):
    s = dx.shape[0]
    t = _row_tile(s)

    def body(dx_ref, y_ref, g_ref, dy_ref, dg_ref):
        @pl.when(pl.program_id(0) == 0)
        def _():
            dg_ref[...] = jnp.zeros_like(dg_ref)

        dxv = dx_ref[...]
        dy_ref[...] = (g_ref[...] * dxv).astype(dy_ref.dtype)
        dg_ref[...] += jnp.sum(dxv * y_ref[...], axis=0, keepdims=True)

    row = pl.BlockSpec((t, D), lambda i: (i, 0))
    vec = pl.BlockSpec((1, D), lambda i: (0, 0))
    return pl.pallas_call(
        body, grid=(s // t,), in_specs=[row, row, vec], out_specs=[row, vec],
        out_shape=[_sds((s, D), MMT), _sds((1, D), F32)],
        compiler_params=_cparams(("arbitrary",)), name=name)(dx, y, gate)


def loss_and_grad(xf, target, *, name):
    s = xf.shape[0]
    t = _row_tile(s)

    def body(x_ref, t_ref, l_ref, dx_ref):
        @pl.when(pl.program_id(0) == 0)
        def _():
            l_ref[...] = jnp.zeros_like(l_ref)

        e = x_ref[...] - t_ref[...]
        dx_ref[...] = e * (1.0 / D)
        l_ref[...] += 0.5 * jnp.sum(jnp.sum(e * e, axis=1, keepdims=True), axis=0, keepdims=True) * (1.0 / D)

    row = pl.BlockSpec((t, D), lambda i: (i, 0))
    return pl.pallas_call(
        body, grid=(s // t,), in_specs=[row, row], out_specs=[pl.BlockSpec((1, 1), lambda i: (0, 0)), row],
        out_shape=[_sds((1, 1), F32), _sds((s, D), F32)],
        compiler_params=_cparams(("arbitrary",)), name=name)(xf, target)


def _ret_consts():
    log_g = np.log1p(-np.exp2(-5.0 - np.arange(NH, dtype=np.float32))).astype(np.float32)
    idx = np.arange(CH, dtype=np.float32)
    d_intra = np.exp(np.abs(idx[:, None] - idx[None, :])[None] * log_g[:, None, None]).astype(np.float32)
    k_w = np.exp((CH - 1.0 - idx)[None, :] * log_g[:, None]).astype(np.float32)
    q_w = np.exp((idx + 1.0)[None, :] * log_g[:, None]).astype(np.float32)
    g_chunk = [float(v) for v in np.exp(np.float32(CH) * log_g).astype(np.float32)]
    bc = lambda a: np.ascontiguousarray(np.broadcast_to(a[:, :, None], (NH, CH, HD)))
    return jnp.asarray(d_intra), jnp.asarray(bc(k_w)), jnp.asarray(bc(q_w)), g_chunk


def _rope_tables(s):
    half = HD // 2
    inv_freq = ROPE_BASE ** (-jnp.arange(half, dtype=F32) / half)
    ang = jnp.arange(s, dtype=F32)[:, None] * inv_freq[None, :]
    cos, sin = jnp.cos(ang), jnp.sin(ang)
    return jnp.concatenate([cos, cos], axis=1), jnp.concatenate([-sin, sin], axis=1)


def _ret_chunk(qs, ks, vs, rs, cos, sin, dintra, kw, qw, g_chunk):
    outs, rn = [], []
    for h in range(NH):
        q = qs[h] * cos + _swap_halves(qs[h]) * sin
        k = (ks[h] * cos + _swap_halves(ks[h]) * sin) * (HD ** -0.5)
        sc = _dot_nt(q, k, HI) * dintra[h]
        outs.append(_dot(sc, vs[h], HI) + _dot(q * qw[h], rs[h], HI))
        rn.append(g_chunk[h] * rs[h] + _dot_tn(k * kw[h], vs[h], HI))
    return outs, rn


def _heads(x):
    return [x[:, h * HD:(h + 1) * HD] for h in range(NH)]


def retention_fwd(zz, cosf, sinf, *, name):
    s = zz.shape[0]
    n = s // CH
    dintra, kw, qw, g_chunk = _ret_consts()

    def body(q_ref, k_ref, v_ref, c_ref, s_ref, di_ref, kw_ref, qw_ref, o_ref, rp_ref, r_scr):
        @pl.when(pl.program_id(0) == 0)
        def _():
            r_scr[...] = jnp.zeros_like(r_scr)

        rprev = r_scr[...]
        rp_ref[0] = rprev
        outs, rn = _ret_chunk(_heads(q_ref[...]), _heads(k_ref[...]), _heads(v_ref[...]),
                              [rprev[h * HD:(h + 1) * HD] for h in range(NH)], c_ref[...], s_ref[...],
                              [di_ref[h] for h in range(NH)], [kw_ref[h] for h in range(NH)],
                              [qw_ref[h] for h in range(NH)], g_chunk)
        o_ref[...] = jnp.concatenate(outs, axis=1)
        r_scr[...] = jnp.concatenate(rn, axis=0)

    col = lambda c: pl.BlockSpec((CH, BW), lambda i: (i, c // BW))
    tab = pl.BlockSpec((CH, HD), lambda i: (i, 0))
    cst = lambda shp: pl.BlockSpec(shp, lambda i: (0,) * len(shp))
    return pl.pallas_call(
        body, grid=(n,),
        in_specs=[col(RQ), col(RK), col(RV), tab, tab, cst((NH, CH, CH)), cst((NH, CH, HD)), cst((NH, CH, HD))],
        out_specs=[pl.BlockSpec((CH, BW), lambda i: (i, 0)), pl.BlockSpec((1, BW, HD), lambda i: (i, 0, 0))],
        out_shape=[_sds((s, BW), F32), _sds((n, BW, HD), F32)],
        scratch_shapes=[pltpu.VMEM((BW, HD), F32)],
        compiler_params=_cparams(("arbitrary",)), name=name)(zz, zz, zz, cosf, sinf, dintra, kw, qw)


def retention_bwd(zz, cosf, sinf, rprev, do, *, name):
    s = zz.shape[0]
    n = s // CH
    dintra, kw, qw, g_chunk = _ret_consts()

    def body(q_ref, k_ref, v_ref, c_ref, s_ref, di_ref, kw_ref, qw_ref, rp_ref, do_ref, dz_ref, dr_scr):
        @pl.when(pl.program_id(0) == 0)
        def _():
            dr_scr[...] = jnp.zeros_like(dr_scr)

        rprev_v = rp_ref[0]
        f = functools.partial(_ret_chunk, cos=c_ref[...], sin=s_ref[...],
                              dintra=[di_ref[h] for h in range(NH)], kw=[kw_ref[h] for h in range(NH)],
                              qw=[qw_ref[h] for h in range(NH)], g_chunk=g_chunk)
        _, vjp = jax.vjp(f, _heads(q_ref[...]), _heads(k_ref[...]), _heads(v_ref[...]),
                         [rprev_v[h * HD:(h + 1) * HD] for h in range(NH)])
        dr = dr_scr[...]
        dq, dk, dv, drp = vjp((_heads(do_ref[...]), [dr[h * HD:(h + 1) * HD] for h in range(NH)]))
        dz_ref[...] = jnp.concatenate(dq + dk + dv, axis=1).astype(dz_ref.dtype)
        dr_scr[...] = jnp.concatenate(drp, axis=0)

    col = lambda c: pl.BlockSpec((CH, BW), lambda i: (n - 1 - i, c // BW))
    tab = pl.BlockSpec((CH, HD), lambda i: (n - 1 - i, 0))
    cst = lambda shp: pl.BlockSpec(shp, lambda i: (0,) * len(shp))
    return pl.pallas_call(
        body, grid=(n,),
        in_specs=[col(RQ), col(RK), col(RV), tab, tab, cst((NH, CH, CH)), cst((NH, CH, HD)), cst((NH, CH, HD)),
                  pl.BlockSpec((1, BW, HD), lambda i: (n - 1 - i, 0, 0)), pl.BlockSpec((CH, BW), lambda i: (n - 1 - i, 0))],
        out_specs=pl.BlockSpec((CH, 3 * BW), lambda i: (n - 1 - i, 0)),
        out_shape=_sds((s, 3 * BW), MMT),
        scratch_shapes=[pltpu.VMEM((BW, HD), F32)],
        compiler_params=_cparams(("arbitrary",)), name=name)(zz, zz, zz, cosf, sinf, dintra, kw, qw, rprev, do)


GKW = NH * GDK


def _gla_consts():
    tri = np.tril(np.ones((CH, CH), np.float32))
    mask_t = np.zeros((BW, GKW), np.float32)
    for h in range(NH):
        mask_t[h * HD:(h + 1) * HD, h * GDK:(h + 1) * GDK] = 1.0
    return jnp.asarray(tri), jnp.asarray(mask_t)


def _gla_chunk(q, k, v, glr, w_a2, b_a, st, tri, mask_t):
    la = _log_sigmoid(_dot(glr, w_a2, HI) + b_a) * (1.0 / 16.0)
    bc = _dot(tri, la, HI)
    be = jnp.sum(la, axis=0, keepdims=True)
    kv_t = _dot_tn(v, k * jnp.exp(be - bc), HI) * mask_t
    sn = jnp.exp(be) * st + kv_t
    return _dot_nt(q * (GDK ** -0.5), sn, HI), sn


def gla_fwd(zz, w_a2p, b_a, *, name):
    s = zz.shape[0]
    n = s // CH
    tri, mask_t = _gla_consts()

    def body(q_ref, k_ref, v_ref, lr_ref, w_ref, b_ref, tri_ref, m_ref, o_ref, sp_ref, st_scr):
        @pl.when(pl.program_id(0) == 0)
        def _():
            st_scr[...] = jnp.zeros_like(st_scr)

        sp = st_scr[...]
        sp_ref[0] = sp
        o, sn = _gla_chunk(q_ref[...], k_ref[...], v_ref[...], lr_ref[...], w_ref[...], b_ref[...], sp,
                           tri_ref[...], m_ref[...])
        o_ref[...] = o
        st_scr[...] = sn

    cst = lambda shp: pl.BlockSpec(shp, lambda i: (0,) * len(shp))
    return pl.pallas_call(
        body, grid=(n,),
        in_specs=[pl.BlockSpec((CH, GKW), lambda i: (i, GQ // GKW)), pl.BlockSpec((CH, GKW), lambda i: (i, GK // GKW)),
                  pl.BlockSpec((CH, BW), lambda i: (i, GV // BW)), pl.BlockSpec((CH, HD), lambda i: (i, LR // HD)),
                  cst((HD, GKW)), cst((1, GKW)), cst((CH, CH)), cst((BW, GKW))],
        out_specs=[pl.BlockSpec((CH, BW), lambda i: (i, 0)), pl.BlockSpec((1, BW, GKW), lambda i: (i, 0, 0))],
        out_shape=[_sds((s, BW), F32), _sds((n, BW, GKW), F32)],
        scratch_shapes=[pltpu.VMEM((BW, GKW), F32)],
        compiler_params=_cparams(("arbitrary",)), name=name)(zz, zz, zz, zz, w_a2p, b_a, tri, mask_t)


def gla_bwd(zz, w_a2p, b_a, sprev, do, *, name):
    s = zz.shape[0]
    n = s // CH
    tri, mask_t = _gla_consts()

    def body(q_ref, k_ref, v_ref, lr_ref, w_ref, b_ref, tri_ref, m_ref, sp_ref, do_ref,
             dz_ref, dlr_ref, dw_ref, db_ref, ds_scr):
        @pl.when(pl.program_id(0) == 0)
        def _():
            ds_scr[...] = jnp.zeros_like(ds_scr)
            dw_ref[...] = jnp.zeros_like(dw_ref)
            db_ref[...] = jnp.zeros_like(db_ref)

        f = functools.partial(_gla_chunk, tri=tri_ref[...], mask_t=m_ref[...])
        _, vjp = jax.vjp(f, q_ref[...], k_ref[...], v_ref[...], lr_ref[...], w_ref[...], b_ref[...], sp_ref[0])
        dq, dk, dv, dlr, dw, db, dsp = vjp((do_ref[...], ds_scr[...]))
        dz_ref[...] = jnp.concatenate([dq, dk, dv], axis=1).astype(dz_ref.dtype)
        dlr_ref[...] = dlr.astype(dlr_ref.dtype)
        dw_ref[...] += dw
        db_ref[...] += db
        ds_scr[...] = dsp

    cst = lambda shp: pl.BlockSpec(shp, lambda i: (0,) * len(shp))
    r = lambda i: n - 1 - i
    return pl.pallas_call(
        body, grid=(n,),
        in_specs=[pl.BlockSpec((CH, GKW), lambda i: (r(i), GQ // GKW)), pl.BlockSpec((CH, GKW), lambda i: (r(i), GK // GKW)),
                  pl.BlockSpec((CH, BW), lambda i: (r(i), GV // BW)), pl.BlockSpec((CH, HD), lambda i: (r(i), LR // HD)),
                  cst((HD, GKW)), cst((1, GKW)), cst((CH, CH)), cst((BW, GKW)),
                  pl.BlockSpec((1, BW, GKW), lambda i: (r(i), 0, 0)), pl.BlockSpec((CH, BW), lambda i: (r(i), 0))],
        out_specs=[pl.BlockSpec((CH, 2 * GKW + BW), lambda i: (r(i), 0)), pl.BlockSpec((CH, HD), lambda i: (r(i), 0)),
                   cst((HD, GKW)), cst((1, GKW))],
        out_shape=[_sds((s, 2 * GKW + BW), MMT), _sds((s, HD), MMT), _sds((HD, GKW), F32), _sds((1, GKW), F32)],
        scratch_shapes=[pltpu.VMEM((BW, GKW), F32)],
        compiler_params=_cparams(("arbitrary",)), name=name)(zz, zz, zz, zz, w_a2p, b_a, tri, mask_t, sprev, do)


def _fox_pre_f(fqs, fks, ff, gq, gk, bf):
    def rms(x, g):
        return x * lax.rsqrt(jnp.mean(x * x, axis=-1, keepdims=True) + EPS) * g

    qn = [rms(x, gq) * (HD ** -0.5) for x in fqs]
    kn = [rms(x, gk) for x in fks]
    return qn, kn, _log_sigmoid(ff + bf)


def fox_pre(zz, gq, gk, bf, *, name):
    s = zz.shape[0]
    t = _row_tile(s)
    tri = jnp.asarray(np.tril(np.ones((t, t), np.float32)))

    def body(q_ref, k_ref, f_ref, gq_ref, gk_ref, b_ref, tri_ref, qn_ref, kn_ref, cum_ref, carry):
        @pl.when(pl.program_id(0) == 0)
        def _():
            carry[...] = jnp.zeros_like(carry)

        qn, kn, lf = _fox_pre_f(_heads(q_ref[...]), _heads(k_ref[...]), f_ref[...], gq_ref[...], gk_ref[...], b_ref[...])
        qn_ref[...] = jnp.concatenate(qn, axis=1).astype(qn_ref.dtype)
        kn_ref[...] = jnp.concatenate(kn, axis=1).astype(kn_ref.dtype)
        cum_ref[...] = _dot(tri_ref[...], lf, HI) + carry[...]
        carry[...] += jnp.sum(lf, axis=0, keepdims=True)

    vec = pl.BlockSpec((1, HD), lambda i: (0, 0))
    return pl.pallas_call(
        body, grid=(s // t,),
        in_specs=[pl.BlockSpec((t, BW), lambda i: (i, FQ // BW)), pl.BlockSpec((t, BW), lambda i: (i, FK // BW)),
                  pl.BlockSpec((t, HD), lambda i: (i, FF // HD)), vec, vec, vec, pl.BlockSpec((t, t), lambda i: (0, 0))],
        out_specs=[pl.BlockSpec((t, BW), lambda i: (i, 0)), pl.BlockSpec((t, BW), lambda i: (i, 0)),
                   pl.BlockSpec((t, HD), lambda i: (i, 0))],
        out_shape=[_sds((s, BW), MMT), _sds((s, BW), MMT), _sds((s, HD), F32)],
        scratch_shapes=[pltpu.VMEM((1, HD), F32)],
        compiler_params=_cparams(("arbitrary",)), name=name)(zz, zz, zz, gq, gk, bf, tri)


def fox_pre_bwd(zz, gq, gk, bf, dqn, dkn, dcum, *, name):
    s = zz.shape[0]
    t = _row_tile(s)
    nt = s // t
    triu = jnp.asarray(np.triu(np.ones((t, t), np.float32)))

    def body(q_ref, k_ref, f_ref, gq_ref, gk_ref, b_ref, tri_ref, dqn_ref, dkn_ref, dcum_ref,
             dz_ref, dff_ref, dgq_ref, dgk_ref, db_ref, carry):
        @pl.when(pl.program_id(0) == 0)
        def _():
            carry[...] = jnp.zeros_like(carry)
            dgq_ref[...] = jnp.zeros_like(dgq_ref)
            dgk_ref[...] = jnp.zeros_like(dgk_ref)
            db_ref[...] = jnp.zeros_like(db_ref)

        dcum_v = dcum_ref[...]
        dlf = _dot(tri_ref[...], dcum_v, HI) + carry[...]
        carry[...] += jnp.sum(dcum_v, axis=0, keepdims=True)
        _, vjp = jax.vjp(_fox_pre_f, _heads(q_ref[...]), _heads(k_ref[...]), f_ref[...], gq_ref[...], gk_ref[...], b_ref[...])
        dq, dk, dff, dgq, dgk, db = vjp((_heads(dqn_ref[...]), _heads(dkn_ref[...]), dlf))
        dz_ref[...] = jnp.concatenate(dq + dk, axis=1).astype(dz_ref.dtype)
        dff_ref[...] = dff.astype(dff_ref.dtype)
        dgq_ref[...] += dgq
        dgk_ref[...] += dgk
        db_ref[...] += db

    r = lambda i: nt - 1 - i
    vec = pl.BlockSpec((1, HD), lambda i: (0, 0))
    return pl.pallas_call(
        body, grid=(nt,),
        in_specs=[pl.BlockSpec((t, BW), lambda i: (r(i), FQ // BW)), pl.BlockSpec((t, BW), lambda i: (r(i), FK // BW)),
                  pl.BlockSpec((t, HD), lambda i: (r(i), FF // HD)), vec, vec, vec, pl.BlockSpec((t, t), lambda i: (0, 0)),
                  pl.BlockSpec((t, BW), lambda i: (r(i), 0)), pl.BlockSpec((t, BW), lambda i: (r(i), 0)),
                  pl.BlockSpec((t, HD), lambda i: (r(i), 0))],
        out_specs=[pl.BlockSpec((t, 2 * BW), lambda i: (r(i), 0)), pl.BlockSpec((t, HD), lambda i: (r(i), 0)), vec, vec, vec],
        out_shape=[_sds((s, 2 * BW), MMT), _sds((s, HD), MMT), _sds((1, HD), F32), _sds((1, HD), F32), _sds((1, HD), F32)],
        scratch_shapes=[pltpu.VMEM((1, HD), F32)],
        compiler_params=_cparams(("arbitrary",)), name=name)(zz, zz, zz, gq, gk, bf, triu, dqn, dkn, dcum)


def _fox_blocks(s):
    return min(256, s), min(512, s)


NEG = -1e30


def fox_attn_fwd(qn, kn, zz, cum_col, cum_row, *, name):
    s = qn.shape[0]
    bq, bk = _fox_blocks(s)

    def body(q_ref, k_ref, v_ref, cc_ref, cr_ref, o_ref, lse_ref):
        qi = pl.program_id(1)
        q = q_ref[...]
        cq = cc_ref[...]
        rows = qi * bq + lax.broadcasted_iota(jnp.int32, (bq, bk), 0)
        cols0 = lax.broadcasted_iota(jnp.int32, (bq, bk), 1)

        def step(j, carry):
            m, l, acc = carry
            off = pl.multiple_of(j * bk, bk)
            k = k_ref[pl.ds(off, bk), :]
            v = v_ref[pl.ds(off, bk), :].astype(MMT)
            sc = _dot_nt(q, k) + cq - cr_ref[pl.ds(j, 1), :]
            sc = jnp.where(rows >= cols0 + j * bk, sc, NEG)
            m_new = jnp.maximum(m, jnp.max(sc, axis=1, keepdims=True))
            alpha = jnp.exp(m - m_new)
            p = jnp.exp(sc - m_new)
            return m_new, alpha * l + jnp.sum(p, axis=1, keepdims=True), alpha * acc + _dot(p.astype(MMT), v)

        nk = ((qi + 1) * bq + bk - 1) // bk
        m, l, acc = lax.fori_loop(0, nk, step, (jnp.full((bq, 1), NEG, F32), jnp.zeros((bq, 1), F32),
                                                jnp.zeros((bq, HD), F32)))
        o_ref[...] = acc / l
        lse_ref[...] = m + jnp.log(l)

    return pl.pallas_call(
        body, grid=(NH, s // bq),
        in_specs=[pl.BlockSpec((bq, HD), lambda h, i: (i, h)), pl.BlockSpec((s, HD), lambda h, i: (0, h)),
                  pl.BlockSpec((s, HD), lambda h, i: (0, FV // HD + h)),
                  pl.BlockSpec((None, bq, 1), lambda h, i: (h, i, 0)), pl.BlockSpec((None, s // bk, bk), lambda h, i: (h, 0, 0))],
        out_specs=[pl.BlockSpec((bq, HD), lambda h, i: (i, h)), pl.BlockSpec((None, bq, 1), lambda h, i: (h, i, 0))],
        out_shape=[_sds((s, BW), F32), _sds((NH, s, 1), F32)],
        compiler_params=_cparams(("parallel", "parallel")), name=name)(qn, kn, zz, cum_col, cum_row)


def fox_attn_bwd(qn, kn, zz, cum_col, cum_row, lse, do, *, name):
    s = qn.shape[0]
    bq, bk = _fox_blocks(s)
    nkc = s // bk

    def body(q_ref, k_ref, v_ref, cc_ref, cr_ref, lse_ref, do_ref, dq_ref, dk_ref, dv_ref, dc_ref, p_scr, dp_scr):
        qi = pl.program_id(1)

        @pl.when(qi == 0)
        def _():
            dk_ref[...] = jnp.zeros_like(dk_ref)
            dv_ref[...] = jnp.zeros_like(dv_ref)
            dc_ref[...] = jnp.zeros_like(dc_ref)

        q = q_ref[...]
        dob = do_ref[...].astype(MMT)
        cq = cc_ref[...]
        lse_v = lse_ref[...]
        rows = qi * bq + lax.broadcasted_iota(jnp.int32, (bq, bk), 0)
        cols0 = lax.broadcasted_iota(jnp.int32, (bq, bk), 1)
        nk = ((qi + 1) * bq + bk - 1) // bk

        def probs(j, delta):
            off = pl.multiple_of(j * bk, bk)
            sc = _dot_nt(q, k_ref[pl.ds(off, bk), :]) + cq - cr_ref[pl.ds(j, 1), :]
            p = jnp.where(rows >= cols0 + j * bk, jnp.exp(sc - lse_v), 0.0)
            dp = _dot_nt(dob, v_ref[pl.ds(off, bk), :].astype(MMT))
            p_scr[j] = p
            dp_scr[j] = dp
            return delta + jnp.sum(p * dp, axis=1, keepdims=True)

        delta = lax.fori_loop(0, nk, probs, jnp.zeros((bq, 1), F32))

        def grads(j, dq):
            off = pl.multiple_of(j * bk, bk)
            p = p_scr[j]
            ds = p * (dp_scr[j] - delta)
            dsm = ds.astype(MMT)
            dv_ref[pl.ds(off, bk), :] += _dot_tn(p.astype(MMT), dob)
            dk_ref[pl.ds(off, bk), :] += _dot_tn(dsm, q)
            dc_ref[pl.ds(j, 1), :] -= jnp.sum(ds, axis=0, keepdims=True)
            return dq + _dot(dsm, k_ref[pl.ds(off, bk), :])

        dq_ref[...] = lax.fori_loop(0, nk, grads, jnp.zeros((bq, HD), F32))

    full = lambda c0=0: pl.BlockSpec((s, HD), lambda h, i: (0, c0 + h))
    blk = lambda: pl.BlockSpec((bq, HD), lambda h, i: (i, h))
    colv = lambda: pl.BlockSpec((None, bq, 1), lambda h, i: (h, i, 0))
    rowv = lambda: pl.BlockSpec((None, nkc, bk), lambda h, i: (h, 0, 0))
    return pl.pallas_call(
        body, grid=(NH, s // bq),
        in_specs=[blk(), full(), full(FV // HD), colv(), rowv(), colv(), blk()],
        out_specs=[blk(), full(), full(), rowv()],
        out_shape=[_sds((s, BW), F32), _sds((s, BW), F32), _sds((s, BW), F32), _sds((NH, nkc, bk), F32)],
        scratch_shapes=[pltpu.VMEM((nkc, bq, bk), F32), pltpu.VMEM((nkc, bq, bk), F32)],
        compiler_params=_cparams(("parallel", "arbitrary")), name=name)(qn, kn, zz, cum_col, cum_row, lse, do)


def _branch_f(rets, rgs, glas, ggs, ret_g, gla_g):
    out_r, out_g = [], []
    for h in range(NH):
        xc = rets[h] - jnp.mean(rets[h], axis=-1, keepdims=True)
        y = xc * lax.rsqrt(jnp.mean(xc * xc, axis=-1, keepdims=True) + EPS) * ret_g[h]
        out_r.append(_silu(rgs[h]) * y)
        x = glas[h]
        y = x * lax.rsqrt(jnp.mean(x * x, axis=-1, keepdims=True) + EPS) * gla_g
        out_g.append(_silu(ggs[h]) * y)
    return out_r, out_g


def mix_fwd(ret_raw, gla_raw, fox_o, zz, ret_g, gla_g, b_mg, w_br, *, name):
    s = zz.shape[0]
    t = _row_tile(s)

    def body(r_ref, g_ref, f_ref, rg_ref, gg_ref, gp_ref, rgn_ref, ggn_ref, bmg_ref, w_ref, o_ref):
        rgn = rgn_ref[...]
        br_r, br_g = _branch_f(_heads(r_ref[...]), _heads(rg_ref[...]), _heads(g_ref[...]), _heads(gg_ref[...]),
                               _heads(rgn), ggn_ref[...])
        brs = [jnp.concatenate(br_r, axis=1), jnp.concatenate(br_g, axis=1), f_ref[...]]
        acc = jnp.zeros((t, D), F32)
        for b in range(3):
            gate = jax.nn.sigmoid(gp_ref[:, b * D:(b + 1) * D] + bmg_ref[:, b * D:(b + 1) * D])
            acc = acc + gate * _dot(brs[b].astype(MMT), w_ref[b])
        o_ref[...] = acc.astype(o_ref.dtype)

    row = lambda w, c=0: pl.BlockSpec((t, w), lambda i: (i, c // w))
    cst = lambda shp: pl.BlockSpec(shp, lambda i: (0,) * len(shp))
    return pl.pallas_call(
        body, grid=(s // t,),
        in_specs=[row(BW), row(BW), row(BW), row(BW, RG), row(BW, GG), row(3 * D, GP), cst((1, BW)), cst((1, HD)),
                  cst((1, 3 * D)), cst((3, BW, D))],
        out_specs=row(D), out_shape=_sds((s, D), MMT),
        compiler_params=_cparams(("parallel",)), name=name)(ret_raw, gla_raw, fox_o, zz, zz, zz, ret_g, gla_g, b_mg, w_br)


def mix_bwd(ret_raw, gla_raw, fox_o, zz, ret_g, gla_g, b_mg, w_br, dmi, *, name):
    s = zz.shape[0]
    t = _row_tile(s)

    def body(r_ref, g_ref, f_ref, rg_ref, gg_ref, gp_ref, rgn_ref, ggn_ref, bmg_ref, w_ref, dmi_ref,
             dr_ref, dg_ref, df_ref, drg_ref, dgg_ref, dgp_ref, dw_ref, drgn_ref, dggn_ref, dbmg_ref):
        @pl.when(pl.program_id(0) == 0)
        def _():
            dw_ref[...] = jnp.zeros_like(dw_ref)
            drgn_ref[...] = jnp.zeros_like(drgn_ref)
            dggn_ref[...] = jnp.zeros_like(dggn_ref)
            dbmg_ref[...] = jnp.zeros_like(dbmg_ref)

        (br_r, br_g), vjp = jax.vjp(_branch_f, _heads(r_ref[...]), _heads(rg_ref[...]), _heads(g_ref[...]),
                                    _heads(gg_ref[...]), _heads(rgn_ref[...]), ggn_ref[...])
        brs = [jnp.concatenate(br_r, axis=1).astype(MMT), jnp.concatenate(br_g, axis=1).astype(MMT),
               f_ref[...].astype(MMT)]
        dmi_v = dmi_ref[...].astype(F32)
        dbr = []
        for b in range(3):
            w = w_ref[b]
            ybr = _dot(brs[b], w)
            gate = jax.nn.sigmoid(gp_ref[:, b * D:(b + 1) * D] + bmg_ref[:, b * D:(b + 1) * D])
            dgp = dmi_v * ybr * gate * (1.0 - gate)
            dgp_ref[:, b * D:(b + 1) * D] = dgp.astype(dgp_ref.dtype)
            dbmg_ref[:, b * D:(b + 1) * D] += jnp.sum(dgp, axis=0, keepdims=True)
            dy = (dmi_v * gate).astype(MMT)
            dw_ref[b] += _dot_tn(brs[b], dy)
            dbr.append(_dot_nt(dy, w))
        dr, drg, dg, dgg, drgn, dggn = vjp((_heads(dbr[0]), _heads(dbr[1])))
        dr_ref[...] = jnp.concatenate(dr, axis=1)
        dg_ref[...] = jnp.concatenate(dg, axis=1)
        df_ref[...] = dbr[2]
        drg_ref[...] = jnp.concatenate(drg, axis=1).astype(drg_ref.dtype)
        dgg_ref[...] = jnp.concatenate(dgg, axis=1).astype(dgg_ref.dtype)
        drgn_ref[...] += jnp.concatenate(drgn, axis=1)
        dggn_ref[...] += dggn

    row = lambda w, c=0: pl.BlockSpec((t, w), lambda i: (i, c // w))
    cst = lambda shp: pl.BlockSpec(shp, lambda i: (0,) * len(shp))
    return pl.pallas_call(
        body, grid=(s // t,),
        in_specs=[row(BW), row(BW), row(BW), row(BW, RG), row(BW, GG), row(3 * D, GP), cst((1, BW)), cst((1, HD)),
                  cst((1, 3 * D)), cst((3, BW, D)), row(D)],
        out_specs=[row(BW), row(BW), row(BW), row(BW), row(BW), row(3 * D), cst((3, BW, D)), cst((1, BW)), cst((1, HD)),
                   cst((1, 3 * D))],
        out_shape=[_sds((s, BW), F32)] * 3 + [_sds((s, BW), MMT)] * 2 + [_sds((s, 3 * D), MMT), _sds((3, BW, D), F32),
                                                                          _sds((1, BW), F32), _sds((1, HD), F32),
                                                                          _sds((1, 3 * D), F32)],
        compiler_params=_cparams(("arbitrary",)), name=name)(ret_raw, gla_raw, fox_o, zz, zz, zz, ret_g, gla_g, b_mg, w_br, dmi)


CT = 256


def _shift_down(x, k, rows):
    return jnp.where(rows >= k, pltpu.roll(x, k, 0), 0.0)


def _shift_up(x, k, rows, s):
    return jnp.where(rows < s - k, pltpu.roll(x, s - k, 0), 0.0)


def conv_fwd(ug, w_conv, b_conv, *, name):
    s = ug.shape[0]
    nt = DFF // CT

    def body(u_ref, g_ref, w_ref, b_ref, a_ref):
        u = u_ref[...]
        rows = lax.broadcasted_iota(jnp.int32, u.shape, 0)
        uc = b_ref[...] + w_ref[0:1, :] * _shift_down(u, 2, rows) + w_ref[1:2, :] * _shift_down(u, 1, rows) + w_ref[2:3, :] * u
        a_ref[...] = (_silu(uc) * g_ref[...]).astype(a_ref.dtype)

    return pl.pallas_call(
        body, grid=(nt,),
        in_specs=[pl.BlockSpec((s, CT), lambda j: (0, j)), pl.BlockSpec((s, CT), lambda j: (0, nt + j)),
                  pl.BlockSpec((3, CT), lambda j: (0, j)), pl.BlockSpec((1, CT), lambda j: (0, j))],
        out_specs=pl.BlockSpec((s, CT), lambda j: (0, j)), out_shape=_sds((s, DFF), MMT),
        compiler_params=_cparams(("parallel",)), name=name)(ug, ug, w_conv, b_conv)


def conv_bwd(ug, w_conv, b_conv, da, *, name):
    s = ug.shape[0]
    nt = DFF // CT

    def body(u_ref, g_ref, w_ref, b_ref, da_ref, du_ref, dg_ref, dw_ref, db_ref):
        u = u_ref[...]
        rows = lax.broadcasted_iota(jnp.int32, u.shape, 0)
        u2, u1 = _shift_down(u, 2, rows), _shift_down(u, 1, rows)
        uc = b_ref[...] + w_ref[0:1, :] * u2 + w_ref[1:2, :] * u1 + w_ref[2:3, :] * u
        sg = jax.nn.sigmoid(uc)
        da_v = da_ref[...]
        dg_ref[...] = (da_v * uc * sg).astype(dg_ref.dtype)
        duc = da_v * g_ref[...] * sg * (1.0 + uc * (1.0 - sg))
        du = w_ref[2:3, :] * duc + w_ref[1:2, :] * _shift_up(duc, 1, rows, s) + w_ref[0:1, :] * _shift_up(duc, 2, rows, s)
        du_ref[...] = du.astype(du_ref.dtype)
        dw_ref[0:1, :] = jnp.sum(duc * u2, axis=0, keepdims=True)
        dw_ref[1:2, :] = jnp.sum(duc * u1, axis=0, keepdims=True)
        dw_ref[2:3, :] = jnp.sum(duc * u, axis=0, keepdims=True)
        db_ref[...] = jnp.sum(duc, axis=0, keepdims=True)

    col = lambda: pl.BlockSpec((s, CT), lambda j: (0, j))
    return pl.pallas_call(
        body, grid=(nt,),
        in_specs=[col(), pl.BlockSpec((s, CT), lambda j: (0, nt + j)), pl.BlockSpec((3, CT), lambda j: (0, j)),
                  pl.BlockSpec((1, CT), lambda j: (0, j)), col()],
        out_specs=[col(), col(), pl.BlockSpec((3, CT), lambda j: (0, j)), pl.BlockSpec((1, CT), lambda j: (0, j))],
        out_shape=[_sds((s, DFF), MMT), _sds((s, DFF), MMT), _sds((3, DFF), F32), _sds((1, DFF), F32)],
        compiler_params=_cparams(("parallel",)), name=name)(ug, ug, w_conv, b_conv, da)


def _tiles(s):
    return min(1024, s)


def layer_fwd(x, mod, p, cosf, sinf):
    s = x.shape[0]
    tm = _tiles(s)
    shift1, scale1, gate1, shift2, scale2, gate2 = mod
    h = norm_mod(x, p["norm1_g"], scale1, shift1, name="norm_mod")
    zz = mm_nn(h, p["w1"], tm=tm, tn=768, out_dtype=F32, name="mm_w1")
    ret_raw, rprev = retention_fwd(zz, cosf, sinf, name="ret_fwd")
    gla_raw, sprev = gla_fwd(zz, p["w_a2p"], p["b_gla_a"], name="gla_fwd")
    qn, kn, cum = fox_pre(zz, p["q_norm_g"], p["k_norm_g"], p["b_foxp"], name="fox_pre")
    bq, bk = _fox_blocks(s)
    cum_t = cum[:, :NH].T
    cum_col, cum_row = cum_t[:, :, None], cum_t.reshape(NH, s // bk, bk)
    fox_o, lse = fox_attn_fwd(qn, kn, zz, cum_col, cum_row, name="fox_fwd")
    mi = mix_fwd(ret_raw, gla_raw, fox_o, zz, p["ret_norm_g"], p["gla_norm_g"], p["b_mg"], p["w_br"], name="mix_fwd")
    x1, mixed = mm_nn_residual(mi, p["w_o"], x, gate1, tm=tm, tn=512, name="mm_wo")
    h2 = norm_mod(x1, p["norm2_g"], scale2, shift2, name="norm_mod")
    ug = mm_nn(h2, p["w_up"], tm=tm, tn=512, out_dtype=F32, name="mm_wup")
    a = conv_fwd(ug, p["w_conv"], p["b_conv"], name="conv_fwd")
    x2, y = mm_nn_residual(a, p["w_down"], x1, gate2, tm=tm, tn=512, name="mm_wdown")
    saved = dict(x=x, h=h, zz=zz, ret_raw=ret_raw, rprev=rprev, gla_raw=gla_raw, sprev=sprev, qn=qn, kn=kn,
                 cum_col=cum_col, cum_row=cum_row, fox_o=fox_o, lse=lse, mi=mi, mixed=mixed, x1=x1, h2=h2, ug=ug, a=a, y=y)
    return x2, saved


def layer_bwd(dx2, mod, p, sv, cosf, sinf):
    s = dx2.shape[0]
    tm = _tiles(s)
    shift1, scale1, gate1, shift2, scale2, gate2 = mod
    g = {}
    dy, dgate2 = gate_bwd(dx2, sv["y"], gate2, name="gate_bwd")
    g["w_down"] = mm_tn(sv["a"], dy, tm=1408, tn=512, out_dtype=MMT, name="mm_dwdown")
    da = mm_nt(dy, p["w_down"], tm=tm, tn=1408, out_dtype=F32, name="mm_da")
    du, dg, g["w_conv"], g["b_conv"] = conv_bwd(sv["ug"], p["w_conv"], p["b_conv"], da, name="conv_bwd")
    dug = jnp.concatenate([du, dg], axis=1)
    g["w_up"] = mm_tn(sv["h2"], dug, tm=512, tn=512, out_dtype=MMT, name="mm_dwup")
    dh2 = mm_nt(dug, p["w_up"], tm=min(512, s), tn=512, out_dtype=F32, name="mm_dh2")
    dx1, g["norm2_g"], dscale2, dshift2 = norm_mod_bwd(sv["x1"], dh2, dx2, p["norm2_g"], scale2, shift2, name="norm_mod_bwd")
    dmixed, dgate1 = gate_bwd(dx1, sv["mixed"], gate1, name="gate_bwd")
    g["w_o"] = mm_tn(sv["mi"], dmixed, tm=512, tn=512, out_dtype=MMT, name="mm_dwo")
    dmi = mm_nt(dmixed, p["w_o"], tm=tm, tn=512, out_dtype=MMT, name="mm_dmi")
    zz = sv["zz"]
    (dret, dgla, dfox, drg, dgg, dgp, g["w_br"], g["ret_norm_g"], g["gla_norm_g"], g["b_mg"]) = mix_bwd(
        sv["ret_raw"], sv["gla_raw"], sv["fox_o"], zz, p["ret_norm_g"], p["gla_norm_g"], p["b_mg"], p["w_br"], dmi,
        name="mix_bwd")
    dqn, dkn, dfv, dcum_row = fox_attn_bwd(sv["qn"], sv["kn"], zz, sv["cum_col"], sv["cum_row"], sv["lse"], dfox,
                                           name="fox_bwd")
    dfv = dfv.astype(MMT)
    dcum = jnp.pad(dcum_row.reshape(NH, s).T, ((0, 0), (0, HD - NH)))
    dfqk, dff, g["q_norm_g"], g["k_norm_g"], g["b_foxp"] = fox_pre_bwd(
        zz, p["q_norm_g"], p["k_norm_g"], p["b_foxp"], dqn, dkn, dcum, name="fox_pre_bwd")
    dgqkv, dlr, g["w_a2p"], g["b_gla_a"] = gla_bwd(zz, p["w_a2p"], p["b_gla_a"], sv["sprev"], dgla, name="gla_bwd")
    drqkv = retention_bwd(zz, cosf, sinf, sv["rprev"], dret, name="ret_bwd")
    dzz = jnp.concatenate([dgp, drqkv, drg, dgqkv, dgg, dfqk, dfv, dlr, dff], axis=1)
    g["w1"] = mm_tn(sv["h"], dzz, tm=512, tn=768, out_dtype=MMT, name="mm_dw1")
    dh = mm_nt(dzz, p["w1"], tm=min(512, s), tn=512, out_dtype=F32, name="mm_dh")
    dx, g["norm1_g"], dscale1, dshift1 = norm_mod_bwd(sv["x"], dh, dx1, p["norm1_g"], scale1, shift1, name="norm_mod_bwd")
    dmod = jnp.concatenate([dshift1, dscale1, dgate1, dshift2, dscale2, dgate2], axis=1)
    return dx, g, dmod


def build_w1(w_in, w_mg):
    z = lambda n: jnp.zeros((D, n), w_in.dtype)
    return jnp.concatenate([w_mg, w_in[:, :3072], w_in[:, 3088:5136], w_in[:, 3072:3088], z(HD - GLR),
                            w_in[:, 5136:5140], z(HD - NH)], axis=1)


def split_dw1(dw1):
    dw_in = jnp.concatenate([dw1[:, RQ:GG], dw1[:, LR:LR + GLR], dw1[:, GG:LR], dw1[:, FF:FF + NH]], axis=1)
    return dw_in, dw1[:, :RQ]


def layer_params(w, l):
    row = lambda v: v[l][None, :]
    return dict(
        norm1_g=row(w["norm1_g"]), norm2_g=row(w["norm2_g"]), w1=build_w1(w["w_in"][l], w["w_mg"][l]),
        w_a2p=jnp.pad(w["w_gla_a2"][l], ((0, HD - GLR), (0, 0))), b_gla_a=row(w["b_gla_a"]),
        b_foxp=jnp.pad(row(w["b_fox_f"]), ((0, 0), (0, HD - NH))), ret_norm_g=row(w["ret_norm_g"]),
        gla_norm_g=row(w["gla_norm_g"]), q_norm_g=row(w["q_norm_g"]), k_norm_g=row(w["k_norm_g"]),
        w_br=w["w_br"][l], b_mg=row(w["b_mg"]), w_o=w["w_o"][l], w_up=w["w_up"][l], w_conv=w["w_conv"][l],
        b_conv=row(w["b_conv"]), w_down=w["w_down"][l])


def layer_grads(g):
    dw_in, dw_mg = split_dw1(g["w1"])
    vec = lambda v: v[0]
    return dict(
        norm1_g=vec(g["norm1_g"]), norm2_g=vec(g["norm2_g"]), w_in=dw_in, w_gla_a2=g["w_a2p"][:GLR], b_gla_a=vec(g["b_gla_a"]),
        b_fox_f=g["b_foxp"][0, :NH], ret_norm_g=vec(g["ret_norm_g"]), gla_norm_g=vec(g["gla_norm_g"]),
        q_norm_g=vec(g["q_norm_g"]), k_norm_g=vec(g["k_norm_g"]), w_br=g["w_br"], w_mg=dw_mg, b_mg=vec(g["b_mg"]),
        w_o=g["w_o"], w_up=g["w_up"], w_conv=g["w_conv"], b_conv=vec(g["b_conv"]), w_down=g["w_down"])


def ada_mod(c_all, w_ada, b_ada):
    nl, _, n = w_ada.shape

    def body(c_ref, w_ref, b_ref, o_ref):
        o_ref[...] = _dot(_silu(c_ref[...]), w_ref[...], HI) + b_ref[...]

    return pl.pallas_call(
        body, grid=(nl,),
        in_specs=[pl.BlockSpec((8, D), lambda l: (0, 0)), pl.BlockSpec((None, D, n), lambda l: (l, 0, 0)),
                  pl.BlockSpec((None, 1, n), lambda l: (l, 0, 0))],
        out_specs=pl.BlockSpec((None, 8, n), lambda l: (l, 0, 0)), out_shape=_sds((nl, 8, n), F32),
        compiler_params=_cparams(("parallel",)), name="ada_mod")(c_all, w_ada, b_ada)


def ada_dw(c_all, dmod):
    nl, _, n = dmod.shape

    def body(c_ref, d_ref, o_ref):
        o_ref[...] = _dot_tn(_silu(c_ref[...]), d_ref[...], HI)

    return pl.pallas_call(
        body, grid=(nl,),
        in_specs=[pl.BlockSpec((8, D), lambda l: (0, 0)), pl.BlockSpec((None, 8, n), lambda l: (l, 0, 0))],
        out_specs=pl.BlockSpec((None, D, n), lambda l: (l, 0, 0)), out_shape=_sds((nl, D, n), F32),
        compiler_params=_cparams(("parallel",)), name="ada_dw")(c_all, dmod)


def sum_devices(g):
    def body(g_ref, o_ref):
        acc = g_ref[0]
        for d in range(1, 8):
            acc = acc + g_ref[d]
        o_ref[...] = acc

    return pl.pallas_call(body, out_shape=_sds(g.shape[1:], F32), name="sum_devices")(g)


def adamw(w, g, m, v, *, block, name):
    nd = w.ndim
    grid = tuple(w.shape[i] // block[i] for i in range(nd))
    bc1 = 1.0 - ADAM_B1 ** ADAM_STEP
    bc2 = 1.0 - ADAM_B2 ** ADAM_STEP

    def body(w_ref, g_ref, m_ref, v_ref, d_ref, nm_ref, nv_ref):
        gv = g_ref[...]
        nm = ADAM_B1 * m_ref[...] + (1.0 - ADAM_B1) * gv
        nv = ADAM_B2 * v_ref[...] + (1.0 - ADAM_B2) * (gv * gv)
        nm_ref[...] = nm
        nv_ref[...] = nv
        d_ref[...] = -ADAM_LR * ((nm / bc1) / (jnp.sqrt(nv / bc2) + ADAM_EPS) + ADAM_WD * w_ref[...])

    spec = pl.BlockSpec(tuple(block), lambda *i: i)
    return pl.pallas_call(
        body, grid=grid, in_specs=[spec] * 4, out_specs=[spec] * 3, out_shape=[_sds(w.shape, F32)] * 3,
        compiler_params=_cparams(("parallel",) * nd), name=name)(w, g, m, v)


MESH = pl.DeviceIdType.MESH
ANY = pl.BlockSpec(memory_space=pl.ANY)
VM = pl.BlockSpec(memory_space=pltpu.VMEM)


def _place():
    x, y, c = lax.axis_index("x"), lax.axis_index("y"), lax.axis_index("c")
    return x, y, c, [(1 - x, y), (x, 1 - y), (1 - x, 1 - y)]


def small_allgather(v, *, name):
    m_per, n = v.shape

    def body(x_ref, out_ref, send_sems, recv_sems, local_sem):
        x, y, c, chips = _place()
        me, sibling = (x, y, c), (x, y, 1 - c)

        def rows(px, py, pc):
            return out_ref.at[pl.ds((4 * px + 2 * py + pc) * m_per, m_per), :]

        def copy(k, block, to, src=None):
            return pltpu.make_async_remote_copy(
                src_ref=rows(*block) if src is None else src, dst_ref=rows(*block),
                send_sem=send_sems.at[k], recv_sem=recv_sems.at[k], device_id=to, device_id_type=MESH)

        mine = pltpu.make_async_copy(x_ref, rows(*me), local_sem)
        mine.start()
        first = [copy(0, me, sibling, src=x_ref)]
        first += [copy(1 + j, me, (*chip, c), src=x_ref) for j, chip in enumerate(chips)]
        for cp in first:
            cp.start()
        passed = [copy(4 + j, (*chip, c), sibling) for j, chip in enumerate(chips)]
        for j, chip in enumerate(chips):
            copy(1 + j, (*chip, c), me).wait_recv()
            passed[j].start()
        copy(0, sibling, me).wait_recv()
        for j, chip in enumerate(chips):
            copy(4 + j, (*chip, 1 - c), me).wait_recv()
        for cp in first + passed:
            cp.wait_send()
        mine.wait()

    return pl.pallas_call(
        body, out_shape=_sds((8 * m_per, n), v.dtype), in_specs=[VM], out_specs=VM,
        scratch_shapes=[pltpu.SemaphoreType.DMA((7,)), pltpu.SemaphoreType.DMA((7,)), pltpu.SemaphoreType.DMA],
        name=name)(v)


def big_gather(p, *, name):
    nl = p.shape[0]
    hl = nl // 2

    def body(p_ref, out_ref, send_sems, recv_sems, local_sem):
        x, y, c, chips = _place()
        mine = 2 * x + y
        sibling = (x, y, 1 - c)

        def remote(k, src, dst, to):
            return pltpu.make_async_remote_copy(src_ref=src, dst_ref=dst, send_sem=send_sems.at[k],
                                                recv_sem=recv_sems.at[k], device_id=to, device_id_type=MESH)

        local = pltpu.make_async_copy(p_ref, out_ref.at[mine], local_sem)
        local.start()
        sends = []
        for i in range(hl):
            for j, chip in enumerate(chips):
                layer = hl * c + i
                sends.append(remote(j * hl + i, p_ref.at[layer], out_ref.at[mine, layer], (*chip, c)))
                sends[-1].start()
        passed = []
        for i in range(hl):
            for j, chip in enumerate(chips):
                kj = 2 * chip[0] + chip[1]
                landed = out_ref.at[kj, hl * c + i]
                remote(j * hl + i, landed, landed, (x, y, c)).wait_recv()
                passed.append(remote(3 * hl + j * hl + i, landed, landed, sibling))
                passed[-1].start()
        for i in range(hl):
            for j, chip in enumerate(chips):
                kj = 2 * chip[0] + chip[1]
                landed = out_ref.at[kj, hl * (1 - c) + i]
                remote(3 * hl + j * hl + i, landed, landed, (x, y, c)).wait_recv()
        for cp in sends + passed:
            cp.wait_send()
        local.wait()

    return pl.pallas_call(
        body, out_shape=_sds((4,) + p.shape, p.dtype), in_specs=[ANY], out_specs=ANY,
        scratch_shapes=[pltpu.SemaphoreType.DMA((6 * hl,)), pltpu.SemaphoreType.DMA((6 * hl,)), pltpu.SemaphoreType.DMA],
        name=name)(p)


def pair_exchange(p, *, name):
    nk, nl = p.shape[:2]
    hl = nl // 2

    def body(p_ref, out_ref, send_sems, recv_sems):
        x, y, c, _ = _place()
        copies = [pltpu.make_async_remote_copy(
            src_ref=p_ref.at[k, pl.ds(hl * (1 - c), hl)], dst_ref=out_ref.at[k], send_sem=send_sems.at[k],
            recv_sem=recv_sems.at[k], device_id=(x, y, 1 - c), device_id_type=MESH) for k in range(nk)]
        for cp in copies:
            cp.start()
        for cp in copies:
            cp.wait()

    return pl.pallas_call(
        body, out_shape=_sds((nk, hl) + p.shape[2:], p.dtype), in_specs=[ANY], out_specs=ANY,
        scratch_shapes=[pltpu.SemaphoreType.DMA((nk,)), pltpu.SemaphoreType.DMA((nk,))], name=name)(p)


def chip_exchange(s, *, name):
    def body(s_ref, out_ref, send_sems, recv_sems):
        x, y, c, chips = _place()
        copies = [pltpu.make_async_remote_copy(
            src_ref=s_ref.at[2 * chip[0] + chip[1]], dst_ref=out_ref.at[j], send_sem=send_sems.at[j],
            recv_sem=recv_sems.at[j], device_id=(*chip, c), device_id_type=MESH) for j, chip in enumerate(chips)]
        for cp in copies:
            cp.start()
        for cp in copies:
            cp.wait()

    return pl.pallas_call(
        body, out_shape=_sds((3,) + s.shape[1:], s.dtype), in_specs=[ANY], out_specs=ANY,
        scratch_shapes=[pltpu.SemaphoreType.DMA((3,)), pltpu.SemaphoreType.DMA((3,))], name=name)(s)


def pair_share(f, *, name):
    hl = f.shape[0]

    def body(f_ref, out_ref, send_sem, recv_sem, local_sem):
        x, y, c, _ = _place()
        local = pltpu.make_async_copy(f_ref, out_ref.at[pl.ds(hl * c, hl)], local_sem)
        local.start()
        cp = pltpu.make_async_remote_copy(src_ref=f_ref, dst_ref=out_ref.at[pl.ds(hl * c, hl)], send_sem=send_sem,
                                          recv_sem=recv_sem, device_id=(x, y, 1 - c), device_id_type=MESH)
        cp.start()
        cp.wait_send()
        pltpu.make_async_remote_copy(src_ref=f_ref, dst_ref=out_ref.at[pl.ds(hl * (1 - c), hl)], send_sem=send_sem,
                                     recv_sem=recv_sem, device_id=(x, y, c), device_id_type=MESH).wait_recv()
        local.wait()

    return pl.pallas_call(
        body, out_shape=_sds((2 * hl,) + f.shape[1:], f.dtype), in_specs=[ANY], out_specs=ANY,
        scratch_shapes=[pltpu.SemaphoreType.DMA, pltpu.SemaphoreType.DMA, pltpu.SemaphoreType.DMA], name=name)(f)


RT = 688


def pair_add(p, r, c_idx, *, name):
    nk, hl, rows, cols = r.shape
    rt = min(RT, rows)

    def body(c_ref, p_ref, r_ref, o_ref):
        o_ref[...] = (p_ref[...].astype(F32) + r_ref[...].astype(F32)).astype(o_ref.dtype)

    blk = (None, None, rt, cols)
    return pl.pallas_call(
        body, out_shape=_sds(r.shape, r.dtype),
        grid_spec=pltpu.PrefetchScalarGridSpec(
            num_scalar_prefetch=1, grid=(nk, hl, rows // rt),
            in_specs=[pl.BlockSpec(blk, lambda k, i, t, c: (k, hl * c[0] + i, t, 0)),
                      pl.BlockSpec(blk, lambda k, i, t, c: (k, i, t, 0))],
            out_specs=pl.BlockSpec(blk, lambda k, i, t, c: (k, i, t, 0))),
        compiler_params=_cparams(("parallel",) * 3), name=name)(c_idx, p, r)


def chip_add(s, r, k_idx, *, name):
    _, hl, rows, cols = s.shape
    rt = min(RT, rows)

    def body(k_ref, s_ref, r0_ref, r1_ref, r2_ref, o_ref):
        o_ref[...] = ((s_ref[...].astype(F32) + r0_ref[...].astype(F32)) + r1_ref[...].astype(F32)) + r2_ref[...].astype(F32)

    blk = (None, None, rt, cols)
    peer = lambda j: pl.BlockSpec(blk, lambda i, t, k: (j, i, t, 0))
    return pl.pallas_call(
        body, out_shape=_sds((hl, rows, cols), F32),
        grid_spec=pltpu.PrefetchScalarGridSpec(
            num_scalar_prefetch=1, grid=(hl, rows // rt),
            in_specs=[pl.BlockSpec(blk, lambda i, t, k: (k[0], i, t, 0)), peer(0), peer(1), peer(2)],
            out_specs=pl.BlockSpec((None, rt, cols), lambda i, t, k: (i, t, 0))),
        compiler_params=_cparams(("parallel",) * 2), name=name)(k_idx, s, r, r, r)


BIG = (("w_in", (D, 1285), 1), ("w_mg", (D, 768), 1), ("w_br", (3, BW, 256), 2), ("w_o", (256, D), 0),
       ("w_up", (D, 1408), 1), ("w_down", (704, D), 0))
PACK_C = 1024
BF16_ROWS = 16


def _seg_rows(shape):
    n = int(np.prod(shape)) // PACK_C
    return n, -(-n // BF16_ROWS) * BF16_ROWS


PACK_R = sum(_seg_rows(shape)[1] for _, shape, _ in BIG)


def pack_shards(sh):
    parts = []
    for name, shape, _ in BIG:
        n, npad = _seg_rows(shape)
        a = sh[name]
        parts.append(jnp.pad(a.reshape(a.shape[0], n, PACK_C), ((0, 0), (0, npad - n), (0, 0))))
    return jnp.concatenate(parts, axis=1)


def unpack_shards(buf):
    out, r0 = {}, 0
    for name, shape, _ in BIG:
        n, npad = _seg_rows(shape)
        out[name] = buf[:, r0:r0 + n].reshape((buf.shape[0],) + shape)
        r0 += npad
    return out


def split_shards(full, shape, axis):
    sp = full.shape[:axis] + (4, shape[axis]) + full.shape[axis + 1:]
    return jnp.moveaxis(full.reshape(sp), axis, 0)


def join_shards(sh, axis):
    m = jnp.moveaxis(sh, 0, axis)
    return m.reshape(m.shape[:axis] + (4 * m.shape[axis + 1],) + m.shape[axis + 2:])


def _flat_rows(arrs):
    v = jnp.concatenate([a.reshape(-1) for a in arrs])
    n = -(-v.shape[0] // 1024) * 1024
    return jnp.pad(v, (0, n - v.shape[0])).reshape(n // HD, HD)


def _unflat(buf, shapes):
    v, out, o = buf.reshape(-1), [], 0
    for s in shapes:
        n = int(np.prod(s))
        out.append(v[o:o + n].reshape(s))
        o += n
    return out


WEIGHTS = ("norm1_g", "norm2_g", "w_ada", "b_ada", "w_in", "w_gla_a2", "b_gla_a", "b_fox_f", "ret_norm_g", "gla_norm_g",
           "q_norm_g", "k_norm_g", "w_br", "w_mg", "b_mg", "w_o", "w_up", "w_conv", "b_conv", "w_down")
REPLICATED = ("norm1_g", "norm2_g", "b_gla_a", "b_fox_f", "ret_norm_g", "gla_norm_g", "q_norm_g", "k_norm_g", "b_mg", "b_conv")
ADAM_BLOCKS = dict(w_ada=(1, 256, 1536), w_in=(1, 256, 1285), w_br=(1, 3, BW, 256), w_mg=(1, 512, 768), w_o=(2, 256, D),
                   w_up=(1, 256, 1408), w_down=(1, 352, D))
ALL_AXES = ("x", "y", "c")


def kernel(x, c, norm1_g, norm2_g, w_ada, b_ada, w_in, w_gla_a2, b_gla_a, b_fox_f, ret_norm_g, gla_norm_g, q_norm_g, k_norm_g, w_br, w_mg, b_mg, w_o, w_up, w_conv, b_conv, w_down, loss_target, m_norm1_g, m_norm2_g, m_w_ada, m_b_ada, m_w_in, m_w_gla_a2, m_b_gla_a, m_b_fox_f, m_ret_norm_g, m_gla_norm_g, m_q_norm_g, m_k_norm_g, m_w_br, m_w_mg, m_b_mg, m_w_o, m_w_up, m_w_conv, m_b_conv, m_w_down, v_norm1_g, v_norm2_g, v_w_ada, v_b_ada, v_w_in, v_w_gla_a2, v_b_gla_a, v_b_fox_f, v_ret_norm_g, v_gla_norm_g, v_q_norm_g, v_k_norm_g, v_w_br, v_w_mg, v_b_mg, v_w_o, v_w_up, v_w_conv, v_b_conv, v_w_down):
    w = dict(zip(WEIGHTS, (norm1_g, norm2_g, w_ada, b_ada, w_in, w_gla_a2, b_gla_a, b_fox_f, ret_norm_g, gla_norm_g,
                           q_norm_g, k_norm_g, w_br, w_mg, b_mg, w_o, w_up, w_conv, b_conv, w_down)))
    m = dict(zip(WEIGHTS, (m_norm1_g, m_norm2_g, m_w_ada, m_b_ada, m_w_in, m_w_gla_a2, m_b_gla_a, m_b_fox_f, m_ret_norm_g,
                           m_gla_norm_g, m_q_norm_g, m_k_norm_g, m_w_br, m_w_mg, m_b_mg, m_w_o, m_w_up, m_w_conv, m_b_conv,
                           m_w_down)))
    v = dict(zip(WEIGHTS, (v_norm1_g, v_norm2_g, v_w_ada, v_b_ada, v_w_in, v_w_gla_a2, v_b_gla_a, v_b_fox_f, v_ret_norm_g,
                           v_gla_norm_g, v_q_norm_g, v_k_norm_g, v_w_br, v_w_mg, v_b_mg, v_w_o, v_w_up, v_w_conv, v_b_conv,
                           v_w_down)))
    nl = norm1_g.shape[0]
    seq = x.shape[1]
    xi, yi, ci = lax.axis_index("x"), lax.axis_index("y"), lax.axis_index("c")
    k_me = 2 * xi + yi
    b_me = 4 * xi + 2 * yi + ci
    ada_n = w_ada.shape[2]
    a2_n, conv_n = w_gla_a2.shape[2], w_conv.shape[2]

    blk = _flat_rows([c, w_gla_a2, w_conv])
    g1 = small_allgather(blk, name="gather_small").reshape(8, blk.shape[0], HD)
    c_all = g1[:, :D // HD].reshape(8, D)
    by_chip = g1[0::2].reshape(4, -1)[:, D:]
    a2_sh, conv_sh = by_chip[:, :nl * GLR * a2_n], by_chip[:, nl * GLR * a2_n:nl * (GLR * a2_n + 3 * conv_n)]
    full_small = dict(
        w_gla_a2=a2_sh.reshape(4, nl, GLR, a2_n).transpose(1, 2, 0, 3).reshape(nl, GLR, 4 * a2_n),
        w_conv=conv_sh.reshape(4, nl, 3, conv_n).transpose(1, 2, 0, 3).reshape(nl, 3, 4 * conv_n))

    b_ada_sh = lax.dynamic_slice_in_dim(b_ada, k_me * ada_n, ada_n, axis=1)[:, None, :]
    mod_sh = ada_mod(c_all, w_ada, b_ada_sh)
    g2 = small_allgather(mod_sh.reshape(nl * 8, ada_n), name="gather_mod").reshape(4, 2, nl, 8, ada_n)[:, 0]
    mod_me = lax.dynamic_index_in_dim(g2, b_me, axis=2, keepdims=False).transpose(1, 0, 2).reshape(nl, 4 * ada_n)

    gathered = big_gather(pack_shards({n: w[n].astype(MMT) for n, _, _ in BIG}), name="gather_weights")
    wfull = {n: w[n] for n in REPLICATED}
    wfull.update(full_small)
    per_layer = [unpack_shards(gathered[:, l]) for l in range(nl)]
    for n, _, axis in BIG:
        wfull[n] = [join_shards(per_layer[l][n], axis) for l in range(nl)]
    params = [layer_params(wfull, l) for l in range(nl)]
    mods = [[mod_me[l:l + 1, i * D:(i + 1) * D] for i in range(6)] for l in range(nl)]

    cosf, sinf = _rope_tables(seq)
    xs, saved = x[0], []
    for l in range(nl):
        xs, sv = layer_fwd(xs, mods[l], params[l], cosf, sinf)
        saved.append(sv)
    loss_part, dx = loss_and_grad(xs, loss_target[0], name="loss")
    loss = lax.psum(loss_part[0, 0], ALL_AXES)
    grads, dmods = [None] * nl, [None] * nl
    for l in reversed(range(nl)):
        dx, g, dmods[l] = layer_bwd(dx, mods[l], params[l], saved[l], cosf, sinf)
        grads[l] = layer_grads(g)

    small_names = REPLICATED + ("w_gla_a2", "w_conv")
    small_shapes = [(nl, 6 * D)] + [(nl,) + grads[0][n].shape for n in small_names]
    vec = _flat_rows([jnp.concatenate(dmods, axis=0)] + [jnp.stack([grads[l][n] for l in range(nl)]) for n in small_names])
    gs = small_allgather(vec, name="gather_small_grads").reshape(8, vec.shape[0], HD)
    summed = _unflat(sum_devices(gs), small_shapes)
    grad = dict(zip(small_names, summed[1:]))
    grad["b_ada"] = summed[0]
    grad["w_gla_a2"] = lax.dynamic_slice_in_dim(grad["w_gla_a2"], k_me * a2_n, a2_n, axis=2)
    grad["w_conv"] = lax.dynamic_slice_in_dim(grad["w_conv"], k_me * conv_n, conv_n, axis=2)
    dmod_all = gs[:, :nl * 6 * D // HD].reshape(8, nl, 6 * D)
    dmod_sh = lax.dynamic_slice_in_dim(dmod_all, k_me * ada_n, ada_n, axis=2).transpose(1, 0, 2)
    grad["w_ada"] = ada_dw(c_all, dmod_sh)

    packed = jnp.stack([pack_shards({n: split_shards(grads[l][n].astype(MMT), shape, axis) for n, shape, axis in BIG})
                        for l in range(nl)], axis=1)
    c_idx, k_idx = jnp.reshape(ci, (1,)).astype(jnp.int32), jnp.reshape(k_me, (1,)).astype(jnp.int32)
    from_sibling = pair_exchange(packed, name="rs_pair_exchange")
    chip_sum = pair_add(packed, from_sibling, c_idx, name="rs_pair_add")
    from_chips = chip_exchange(chip_sum, name="rs_chip_exchange")
    mine = chip_add(chip_sum, from_chips, k_idx, name="rs_chip_add")
    grad.update(unpack_shards(pair_share(mine, name="rs_pair_share")))

    delta, new_m, new_v = {}, {}, {}
    for n, block in ADAM_BLOCKS.items():
        delta[n], new_m[n], new_v[n] = adamw(w[n], grad[n], m[n], v[n], block=block, name="adamw_" + n)
    rest = [n for n in WEIGHTS if n not in ADAM_BLOCKS]
    shapes = [w[n].shape for n in rest]
    flat = [_flat_rows([t[n] for n in rest]) for t in (w, grad, m, v)]
    outs = adamw(*flat, block=flat[0].shape, name="adamw_small")
    for t, o in zip((delta, new_m, new_v), outs):
        t.update(zip(rest, _unflat(o, shapes)))

    return (loss, dx[None], *[grad[n] for n in WEIGHTS], *[delta[n] for n in WEIGHTS], *[new_m[n] for n in WEIGHTS],
            *[new_v[n] for n in WEIGHTS])
```

```python
import functools

import numpy as np
import jax
import jax.numpy as jnp
from jax import lax
from jax.experimental import pallas as pl
from jax.experimental.pallas import tpu as pltpu

F32 = jnp.float32
MMT = jnp.bfloat16
HI = lax.Precision.HIGHEST

D = 1024
DEPTH = 4
NH = 4
HD = 128
BW = NH * HD
CH = 64
GDK = 64
GLR = 16
DFF = 2816
EPS = 1e-6
ROPE_BASE = 10000.0

GP, RQ, RK, RV, RG, GQ, GK, GV, GG, FQ, FK, FV, LR, FF = (
    0, 3072, 3584, 4096, 4608, 5120, 5376, 5632, 6144, 6656, 7168, 7680, 8192, 8320)
NZZ = 8448
IN_W = 5140

VMEM_LIMIT = 56 * 1024 * 1024

ADAM_LR, ADAM_B1, ADAM_B2, ADAM_EPS, ADAM_WD, ADAM_STEP = 0.001, 0.9, 0.999, 1e-08, 0.01, 10


def _cparams(sem=None):
    return pltpu.CompilerParams(dimension_semantics=sem, vmem_limit_bytes=VMEM_LIMIT)


def _sds(shape, dtype):
    return jax.ShapeDtypeStruct(tuple(shape), dtype)


def _dot(a, b, precision=None):
    return lax.dot_general(a, b, (((1,), (0,)), ((), ())), precision=precision, preferred_element_type=F32)


def _dot_nt(a, b, precision=None):
    return lax.dot_general(a, b, (((1,), (1,)), ((), ())), precision=precision, preferred_element_type=F32)


def _dot_tn(a, b, precision=None):
    return lax.dot_general(a, b, (((0,), (0,)), ((), ())), precision=precision, preferred_element_type=F32)


def _silu(x):
    return x * jax.nn.sigmoid(x)


def _log_sigmoid(x):
    return jnp.minimum(x, 0.0) - jnp.log(1.0 + jnp.exp(jnp.minimum(x, -x)))


@jax.custom_vjp
def _swap_halves(x):
    return pltpu.roll(x, HD // 2, 1)


_swap_halves.defvjp(lambda x: (_swap_halves(x), None), lambda _, g: (_swap_halves(g),))


def _stacked(blk, idx, layer):
    if layer is None:
        return pl.BlockSpec(blk, idx)
    return pl.BlockSpec((None,) + blk, lambda i, j: (layer,) + idx(i, j))


def mm_nn(a, b, *, tm, tn, out_dtype, name, layer=None):
    m, k = a.shape
    n = b.shape[-1]

    def body(a_ref, b_ref, o_ref):
        o_ref[...] = _dot(a_ref[...], b_ref[...]).astype(o_ref.dtype)

    return pl.pallas_call(
        body, grid=(m // tm, n // tn),
        in_specs=[pl.BlockSpec((tm, k), lambda i, j: (i, 0)), _stacked((k, tn), lambda i, j: (0, j), layer)],
        out_specs=pl.BlockSpec((tm, tn), lambda i, j: (i, j)),
        out_shape=_sds((m, n), out_dtype), compiler_params=_cparams(("parallel", "parallel")), name=name)(a, b)


def mm_nn_residual(a, b, res, gate, *, tm, tn, name, layer=None):
    m, k = a.shape
    n = b.shape[-1]

    def body(a_ref, b_ref, r_ref, g_ref, x_ref, y_ref):
        acc = _dot(a_ref[...], b_ref[...])
        y_ref[...] = acc
        x_ref[...] = r_ref[...] + g_ref[...] * acc

    return pl.pallas_call(
        body, grid=(m // tm, n // tn),
        in_specs=[pl.BlockSpec((tm, k), lambda i, j: (i, 0)), _stacked((k, tn), lambda i, j: (0, j), layer),
                  pl.BlockSpec((tm, tn), lambda i, j: (i, j)), pl.BlockSpec((1, tn), lambda i, j: (0, j))],
        out_specs=[pl.BlockSpec((tm, tn), lambda i, j: (i, j)), pl.BlockSpec((tm, tn), lambda i, j: (i, j))],
        out_shape=[_sds((m, n), F32), _sds((m, n), F32)],
        compiler_params=_cparams(("parallel", "parallel")), name=name)(a, b, res, gate)


def mm_nt(a, b, *, tm, tn, out_dtype, name, layer=None):
    m, k = a.shape
    n = b.shape[-2]

    def body(a_ref, b_ref, o_ref):
        o_ref[...] = _dot_nt(a_ref[...], b_ref[...]).astype(o_ref.dtype)

    return pl.pallas_call(
        body, grid=(m // tm, n // tn),
        in_specs=[pl.BlockSpec((tm, k), lambda i, j: (i, 0)), _stacked((tn, k), lambda i, j: (j, 0), layer)],
        out_specs=pl.BlockSpec((tm, tn), lambda i, j: (i, j)),
        out_shape=_sds((m, n), out_dtype), compiler_params=_cparams(("parallel", "parallel")), name=name)(a, b)


def mm_tn(a, b, *, tm, tn, out_dtype, name, col0=0, ncols=None, stack=None, layer=None):
    s, m = a.shape
    n = b.shape[1] - col0 if ncols is None else ncols
    c0 = col0 // tn

    def body(a_ref, b_ref, *rest):
        o_ref = rest[-1]
        o_ref[...] = _dot_tn(a_ref[...], b_ref[...]).astype(o_ref.dtype)

    in_specs = [pl.BlockSpec((s, tm), lambda i, j: (0, i)), pl.BlockSpec((s, tn), lambda i, j: (0, c0 + j))]
    if stack is None:
        return pl.pallas_call(
            body, grid=(m // tm, n // tn), in_specs=in_specs, out_specs=pl.BlockSpec((tm, tn), lambda i, j: (i, j)),
            out_shape=_sds((m, n), out_dtype), compiler_params=_cparams(("parallel", "parallel")), name=name)(a, b)
    return pl.pallas_call(
        body, grid=(m // tm, n // tn), in_specs=in_specs + [pl.BlockSpec(memory_space=pl.ANY)],
        out_specs=pl.BlockSpec((None, tm, tn), lambda i, j: (layer, i, j)),
        out_shape=_sds(stack.shape, stack.dtype), input_output_aliases={2: 0},
        compiler_params=_cparams(("parallel", "parallel")), name=name)(a, b, stack)


def _row_tile(s):
    return min(256, s)


def _norm_mod_f(x, g, scale, shift):
    r = lax.rsqrt(jnp.mean(x * x, axis=-1, keepdims=True) + EPS)
    return (x * r * g) * (1.0 + scale) + shift


def norm_mod(x, g, scale, shift, *, name):
    s = x.shape[0]
    t = _row_tile(s)

    def body(x_ref, g_ref, sc_ref, sh_ref, o_ref):
        o_ref[...] = _norm_mod_f(x_ref[...], g_ref[...], sc_ref[...], sh_ref[...]).astype(o_ref.dtype)

    vec = pl.BlockSpec((1, D), lambda i: (0, 0))
    return pl.pallas_call(
        body, grid=(s // t,), in_specs=[pl.BlockSpec((t, D), lambda i: (i, 0)), vec, vec, vec],
        out_specs=pl.BlockSpec((t, D), lambda i: (i, 0)), out_shape=_sds((s, D), MMT),
        compiler_params=_cparams(("parallel",)), name=name)(x, g, scale, shift)


def norm_mod_bwd(x, dh, dres, g, scale, shift, *, name):
    s = x.shape[0]
    t = _row_tile(s)

    def body(x_ref, dh_ref, dr_ref, g_ref, sc_ref, sh_ref, dx_ref, dg_ref, dsc_ref, dsh_ref):
        @pl.when(pl.program_id(0) == 0)
        def _():
            dg_ref[...] = jnp.zeros_like(dg_ref)
            dsc_ref[...] = jnp.zeros_like(dsc_ref)
            dsh_ref[...] = jnp.zeros_like(dsh_ref)

        _, vjp = jax.vjp(_norm_mod_f, x_ref[...], g_ref[...], sc_ref[...], sh_ref[...])
        dx, dg, dsc, dsh = vjp(dh_ref[...])
        dx_ref[...] = dr_ref[...] + dx
        dg_ref[...] += dg
        dsc_ref[...] += dsc
        dsh_ref[...] += dsh

    row = pl.BlockSpec((t, D), lambda i: (i, 0))
    vec = pl.BlockSpec((1, D), lambda i: (0, 0))
    return pl.pallas_call(
        body, grid=(s // t,), in_specs=[row, row, row, vec, vec, vec], out_specs=[row, vec, vec, vec],
        out_shape=[_sds((s, D), F32)] + [_sds((1, D), F32)] * 3,
        compiler_params=_cparams(("arbitrary",)), name=name)(x, dh, dres, g, scale, shift)


def gate_bwd(dx, y, gate, *, name):
    s = dx.shape[0]
    t = _row_tile(s)

    def body(dx_ref, y_ref, g_ref, dy_ref, dg_ref):
        @pl.when(pl.program_id(0) == 0)
        def _():
            dg_ref[...] = jnp.zeros_like(dg_ref)

        dxv = dx_ref[...]
        dy_ref[...] = (g_ref[...] * dxv).astype(dy_ref.dtype)
        dg_ref[...] += jnp.sum(dxv * y_ref[...], axis=0, keepdims=True)

    row = pl.BlockSpec((t, D), lambda i: (i, 0))
    vec = pl.BlockSpec((1, D), lambda i: (0, 0))
    return pl.pallas_call(
        body, grid=(s // t,), in_specs=[row, row, vec], out_specs=[row, vec],
        out_shape=[_sds((s, D), MMT), _sds((1, D), F32)],
        compiler_params=_cparams(("arbitrary",)), name=name)(dx, y, gate)


def loss_and_grad(xf, target, *, name):
    s = xf.shape[0]
    t = _row_tile(s)

    def body(x_ref, t_ref, l_ref, dx_ref):
        @pl.when(pl.program_id(0) == 0)
        def _():
            l_ref[...] = jnp.zeros_like(l_ref)

        e = x_ref[...] - t_ref[...]
        dx_ref[...] = e * (1.0 / D)
        l_ref[...] += 0.5 * jnp.sum(jnp.sum(e * e, axis=1, keepdims=True), axis=0, keepdims=True) * (1.0 / D)

    row = pl.BlockSpec((t, D), lambda i: (i, 0))
    return pl.pallas_call(
        body, grid=(s // t,), in_specs=[row, row], out_specs=[pl.BlockSpec((1, 1), lambda i: (0, 0)), row],
        out_shape=[_sds((1, 1), F32), _sds((s, D), F32)],
        compiler_params=_cparams(("arbitrary",)), name=name)(xf, target)


def _ret_consts():
    log_g = np.log1p(-np.exp2(-5.0 - np.arange(NH, dtype=np.float32))).astype(np.float32)
    idx = np.arange(CH, dtype=np.float32)
    d_intra = np.exp(np.abs(idx[:, None] - idx[None, :])[None] * log_g[:, None, None]).astype(np.float32)
    k_w = np.exp((CH - 1.0 - idx)[None, :] * log_g[:, None]).astype(np.float32)
    q_w = np.exp((idx + 1.0)[None, :] * log_g[:, None]).astype(np.float32)
    g_chunk = [float(v) for v in np.exp(np.float32(CH) * log_g).astype(np.float32)]
    bc = lambda a: np.ascontiguousarray(np.broadcast_to(a[:, :, None], (NH, CH, HD)))
    return jnp.asarray(d_intra), jnp.asarray(bc(k_w)), jnp.asarray(bc(q_w)), g_chunk


def _rope_tables(s):
    half = HD // 2
    inv_freq = ROPE_BASE ** (-jnp.arange(half, dtype=F32) / half)
    ang = jnp.arange(s, dtype=F32)[:, None] * inv_freq[None, :]
    cos, sin = jnp.cos(ang), jnp.sin(ang)
    return jnp.concatenate([cos, cos], axis=1), jnp.concatenate([-sin, sin], axis=1)


def _ret_chunk(qs, ks, vs, rs, cos, sin, dintra, kw, qw, g_chunk):
    outs, rn = [], []
    for h in range(NH):
        q = qs[h] * cos + _swap_halves(qs[h]) * sin
        k = (ks[h] * cos + _swap_halves(ks[h]) * sin) * (HD ** -0.5)
        sc = _dot_nt(q, k, HI) * dintra[h]
        outs.append(_dot(sc, vs[h], HI) + _dot(q * qw[h], rs[h], HI))
        rn.append(g_chunk[h] * rs[h] + _dot_tn(k * kw[h], vs[h], HI))
    return outs, rn


def _heads(x):
    return [x[:, h * HD:(h + 1) * HD] for h in range(NH)]


def retention_fwd(zz, cosf, sinf, *, name):
    s = zz.shape[0]
    n = s // CH
    dintra, kw, qw, g_chunk = _ret_consts()

    def body(q_ref, k_ref, v_ref, c_ref, s_ref, di_ref, kw_ref, qw_ref, o_ref, rp_ref, r_scr):
        @pl.when(pl.program_id(0) == 0)
        def _():
            r_scr[...] = jnp.zeros_like(r_scr)

        rprev = r_scr[...]
        rp_ref[0] = rprev
        outs, rn = _ret_chunk(_heads(q_ref[...]), _heads(k_ref[...]), _heads(v_ref[...]),
                              [rprev[h * HD:(h + 1) * HD] for h in range(NH)], c_ref[...], s_ref[...],
                              [di_ref[h] for h in range(NH)], [kw_ref[h] for h in range(NH)],
                              [qw_ref[h] for h in range(NH)], g_chunk)
        o_ref[...] = jnp.concatenate(outs, axis=1)
        r_scr[...] = jnp.concatenate(rn, axis=0)

    col = lambda c: pl.BlockSpec((CH, BW), lambda i: (i, c // BW))
    tab = pl.BlockSpec((CH, HD), lambda i: (i, 0))
    cst = lambda shp: pl.BlockSpec(shp, lambda i: (0,) * len(shp))
    return pl.pallas_call(
        body, grid=(n,),
        in_specs=[col(RQ), col(RK), col(RV), tab, tab, cst((NH, CH, CH)), cst((NH, CH, HD)), cst((NH, CH, HD))],
        out_specs=[pl.BlockSpec((CH, BW), lambda i: (i, 0)), pl.BlockSpec((1, BW, HD), lambda i: (i, 0, 0))],
        out_shape=[_sds((s, BW), F32), _sds((n, BW, HD), F32)],
        scratch_shapes=[pltpu.VMEM((BW, HD), F32)],
        compiler_params=_cparams(("arbitrary",)), name=name)(zz, zz, zz, cosf, sinf, dintra, kw, qw)


def retention_bwd(zz, cosf, sinf, rprev, do, *, name):
    s = zz.shape[0]
    n = s // CH
    dintra, kw, qw, g_chunk = _ret_consts()

    def body(q_ref, k_ref, v_ref, c_ref, s_ref, di_ref, kw_ref, qw_ref, rp_ref, do_ref, dz_ref, dr_scr):
        @pl.when(pl.program_id(0) == 0)
        def _():
            dr_scr[...] = jnp.zeros_like(dr_scr)

        rprev_v = rp_ref[0]
        f = functools.partial(_ret_chunk, cos=c_ref[...], sin=s_ref[...],
                              dintra=[di_ref[h] for h in range(NH)], kw=[kw_ref[h] for h in range(NH)],
                              qw=[qw_ref[h] for h in range(NH)], g_chunk=g_chunk)
        _, vjp = jax.vjp(f, _heads(q_ref[...]), _heads(k_ref[...]), _heads(v_ref[...]),
                         [rprev_v[h * HD:(h + 1) * HD] for h in range(NH)])
        dr = dr_scr[...]
        dq, dk, dv, drp = vjp((_heads(do_ref[...]), [dr[h * HD:(h + 1) * HD] for h in range(NH)]))
        dz_ref[...] = jnp.concatenate(dq + dk + dv, axis=1).astype(dz_ref.dtype)
        dr_scr[...] = jnp.concatenate(drp, axis=0)

    col = lambda c: pl.BlockSpec((CH, BW), lambda i: (n - 1 - i, c // BW))
    tab = pl.BlockSpec((CH, HD), lambda i: (n - 1 - i, 0))
    cst = lambda shp: pl.BlockSpec(shp, lambda i: (0,) * len(shp))
    return pl.pallas_call(
        body, grid=(n,),
        in_specs=[col(RQ), col(RK), col(RV), tab, tab, cst((NH, CH, CH)), cst((NH, CH, HD)), cst((NH, CH, HD)),
                  pl.BlockSpec((1, BW, HD), lambda i: (n - 1 - i, 0, 0)), pl.BlockSpec((CH, BW), lambda i: (n - 1 - i, 0))],
        out_specs=pl.BlockSpec((CH, 3 * BW), lambda i: (n - 1 - i, 0)),
        out_shape=_sds((s, 3 * BW), MMT),
        scratch_shapes=[pltpu.VMEM((BW, HD), F32)],
        compiler_params=_cparams(("arbitrary",)), name=name)(zz, zz, zz, cosf, sinf, dintra, kw, qw, rprev, do)


GKW = NH * GDK


def _gla_consts():
    tri = np.tril(np.ones((CH, CH), np.float32))
    mask_t = np.zeros((BW, GKW), np.float32)
    for h in range(NH):
        mask_t[h * HD:(h + 1) * HD, h * GDK:(h + 1) * GDK] = 1.0
    return jnp.asarray(tri), jnp.asarray(mask_t)


def _gla_chunk(q, k, v, glr, w_a2, b_a, st, tri, mask_t):
    la = _log_sigmoid(_dot(glr, w_a2, HI) + b_a) * (1.0 / 16.0)
    bc = _dot(tri, la, HI)
    be = jnp.sum(la, axis=0, keepdims=True)
    kv_t = _dot_tn(v, k * jnp.exp(be - bc), HI) * mask_t
    sn = jnp.exp(be) * st + kv_t
    return _dot_nt(q * (GDK ** -0.5), sn, HI), sn


def gla_fwd(zz, w_a2p, b_a, *, name):
    s = zz.shape[0]
    n = s // CH
    tri, mask_t = _gla_consts()

    def body(q_ref, k_ref, v_ref, lr_ref, w_ref, b_ref, tri_ref, m_ref, o_ref, sp_ref, st_scr):
        @pl.when(pl.program_id(0) == 0)
        def _():
            st_scr[...] = jnp.zeros_like(st_scr)

        sp = st_scr[...]
        sp_ref[0] = sp
        o, sn = _gla_chunk(q_ref[...], k_ref[...], v_ref[...], lr_ref[...], w_ref[...], b_ref[...], sp,
                           tri_ref[...], m_ref[...])
        o_ref[...] = o
        st_scr[...] = sn

    cst = lambda shp: pl.BlockSpec(shp, lambda i: (0,) * len(shp))
    return pl.pallas_call(
        body, grid=(n,),
        in_specs=[pl.BlockSpec((CH, GKW), lambda i: (i, GQ // GKW)), pl.BlockSpec((CH, GKW), lambda i: (i, GK // GKW)),
                  pl.BlockSpec((CH, BW), lambda i: (i, GV // BW)), pl.BlockSpec((CH, HD), lambda i: (i, LR // HD)),
                  cst((HD, GKW)), cst((1, GKW)), cst((CH, CH)), cst((BW, GKW))],
        out_specs=[pl.BlockSpec((CH, BW), lambda i: (i, 0)), pl.BlockSpec((1, BW, GKW), lambda i: (i, 0, 0))],
        out_shape=[_sds((s, BW), F32), _sds((n, BW, GKW), F32)],
        scratch_shapes=[pltpu.VMEM((BW, GKW), F32)],
        compiler_params=_cparams(("arbitrary",)), name=name)(zz, zz, zz, zz, w_a2p, b_a, tri, mask_t)


def gla_bwd(zz, w_a2p, b_a, sprev, do, *, name):
    s = zz.shape[0]
    n = s // CH
    tri, mask_t = _gla_consts()

    def body(q_ref, k_ref, v_ref, lr_ref, w_ref, b_ref, tri_ref, m_ref, sp_ref, do_ref,
             dz_ref, dlr_ref, dw_ref, db_ref, ds_scr):
        @pl.when(pl.program_id(0) == 0)
        def _():
            ds_scr[...] = jnp.zeros_like(ds_scr)
            dw_ref[...] = jnp.zeros_like(dw_ref)
            db_ref[...] = jnp.zeros_like(db_ref)

        f = functools.partial(_gla_chunk, tri=tri_ref[...], mask_t=m_ref[...])
        _, vjp = jax.vjp(f, q_ref[...], k_ref[...], v_ref[...], lr_ref[...], w_ref[...], b_ref[...], sp_ref[0])
        dq, dk, dv, dlr, dw, db, dsp = vjp((do_ref[...], ds_scr[...]))
        dz_ref[...] = jnp.concatenate([dq, dk, dv], axis=1).astype(dz_ref.dtype)
        dlr_ref[...] = dlr.astype(dlr_ref.dtype)
        dw_ref[...] += dw
        db_ref[...] += db
        ds_scr[...] = dsp

    cst = lambda shp: pl.BlockSpec(shp, lambda i: (0,) * len(shp))
    r = lambda i: n - 1 - i
    return pl.pallas_call(
        body, grid=(n,),
        in_specs=[pl.BlockSpec((CH, GKW), lambda i: (r(i), GQ // GKW)), pl.BlockSpec((CH, GKW), lambda i: (r(i), GK // GKW)),
                  pl.BlockSpec((CH, BW), lambda i: (r(i), GV // BW)), pl.BlockSpec((CH, HD), lambda i: (r(i), LR // HD)),
                  cst((HD, GKW)), cst((1, GKW)), cst((CH, CH)), cst((BW, GKW)),
                  pl.BlockSpec((1, BW, GKW), lambda i: (r(i), 0, 0)), pl.BlockSpec((CH, BW), lambda i: (r(i), 0))],
        out_specs=[pl.BlockSpec((CH, 2 * GKW + BW), lambda i: (r(i), 0)), pl.BlockSpec((CH, HD), lambda i: (r(i), 0)),
                   cst((HD, GKW)), cst((1, GKW))],
        out_shape=[_sds((s, 2 * GKW + BW), MMT), _sds((s, HD), MMT), _sds((HD, GKW), F32), _sds((1, GKW), F32)],
        scratch_shapes=[pltpu.VMEM((BW, GKW), F32)],
        compiler_params=_cparams(("arbitrary",)), name=name)(zz, zz, zz, zz, w_a2p, b_a, tri, mask_t, sprev, do)


def _fox_pre_f(fqs, fks, ff, gq, gk, bf):
    def rms(x, g):
        return x * lax.rsqrt(jnp.mean(x * x, axis=-1, keepdims=True) + EPS) * g

    qn = [rms(x, gq) * (HD ** -0.5) for x in fqs]
    kn = [rms(x, gk) for x in fks]
    return qn, kn, _log_sigmoid(ff + bf)


def fox_pre(zz, gq, gk, bf, *, name):
    s = zz.shape[0]
    t = _row_tile(s)
    tri = jnp.asarray(np.tril(np.ones((t, t), np.float32)))

    def body(q_ref, k_ref, f_ref, gq_ref, gk_ref, b_ref, tri_ref, qn_ref, kn_ref, cum_ref, carry):
        @pl.when(pl.program_id(0) == 0)
        def _():
            carry[...] = jnp.zeros_like(carry)

        qn, kn, lf = _fox_pre_f(_heads(q_ref[...]), _heads(k_ref[...]), f_ref[...], gq_ref[...], gk_ref[...], b_ref[...])
        qn_ref[...] = jnp.concatenate(qn, axis=1).astype(qn_ref.dtype)
        kn_ref[...] = jnp.concatenate(kn, axis=1).astype(kn_ref.dtype)
        cum_ref[...] = _dot(tri_ref[...], lf, HI) + carry[...]
        carry[...] += jnp.sum(lf, axis=0, keepdims=True)

    vec = pl.BlockSpec((1, HD), lambda i: (0, 0))
    return pl.pallas_call(
        body, grid=(s // t,),
        in_specs=[pl.BlockSpec((t, BW), lambda i: (i, FQ // BW)), pl.BlockSpec((t, BW), lambda i: (i, FK // BW)),
                  pl.BlockSpec((t, HD), lambda i: (i, FF // HD)), vec, vec, vec, pl.BlockSpec((t, t), lambda i: (0, 0))],
        out_specs=[pl.BlockSpec((t, BW), lambda i: (i, 0)), pl.BlockSpec((t, BW), lambda i: (i, 0)),
                   pl.BlockSpec((t, HD), lambda i: (i, 0))],
        out_shape=[_sds((s, BW), MMT), _sds((s, BW), MMT), _sds((s, HD), F32)],
        scratch_shapes=[pltpu.VMEM((1, HD), F32)],
        compiler_params=_cparams(("arbitrary",)), name=name)(zz, zz, zz, gq, gk, bf, tri)


def fox_pre_bwd(zz, gq, gk, bf, dqn, dkn, dcum, *, name):
    s = zz.shape[0]
    t = _row_tile(s)
    nt = s // t
    triu = jnp.asarray(np.triu(np.ones((t, t), np.float32)))

    def body(q_ref, k_ref, f_ref, gq_ref, gk_ref, b_ref, tri_ref, dqn_ref, dkn_ref, dcum_ref,
             dz_ref, dff_ref, dgq_ref, dgk_ref, db_ref, carry):
        @pl.when(pl.program_id(0) == 0)
        def _():
            carry[...] = jnp.zeros_like(carry)
            dgq_ref[...] = jnp.zeros_like(dgq_ref)
            dgk_ref[...] = jnp.zeros_like(dgk_ref)
            db_ref[...] = jnp.zeros_like(db_ref)

        dcum_v = dcum_ref[...]
        dlf = _dot(tri_ref[...], dcum_v, HI) + carry[...]
        carry[...] += jnp.sum(dcum_v, axis=0, keepdims=True)
        _, vjp = jax.vjp(_fox_pre_f, _heads(q_ref[...]), _heads(k_ref[...]), f_ref[...], gq_ref[...], gk_ref[...], b_ref[...])
        dq, dk, dff, dgq, dgk, db = vjp((_heads(dqn_ref[...]), _heads(dkn_ref[...]), dlf))
        dz_ref[...] = jnp.concatenate(dq + dk, axis=1).astype(dz_ref.dtype)
        dff_ref[...] = dff.astype(dff_ref.dtype)
        dgq_ref[...] += dgq
        dgk_ref[...] += dgk
        db_ref[...] += db

    r = lambda i: nt - 1 - i
    vec = pl.BlockSpec((1, HD), lambda i: (0, 0))
    return pl.pallas_call(
        body, grid=(nt,),
        in_specs=[pl.BlockSpec((t, BW), lambda i: (r(i), FQ // BW)), pl.BlockSpec((t, BW), lambda i: (r(i), FK // BW)),
                  pl.BlockSpec((t, HD), lambda i: (r(i), FF // HD)), vec, vec, vec, pl.BlockSpec((t, t), lambda i: (0, 0)),
                  pl.BlockSpec((t, BW), lambda i: (r(i), 0)), pl.BlockSpec((t, BW), lambda i: (r(i), 0)),
                  pl.BlockSpec((t, HD), lambda i: (r(i), 0))],
        out_specs=[pl.BlockSpec((t, 2 * BW), lambda i: (r(i), 0)), pl.BlockSpec((t, HD), lambda i: (r(i), 0)), vec, vec, vec],
        out_shape=[_sds((s, 2 * BW), MMT), _sds((s, HD), MMT), _sds((1, HD), F32), _sds((1, HD), F32), _sds((1, HD), F32)],
        scratch_shapes=[pltpu.VMEM((1, HD), F32)],
        compiler_params=_cparams(("arbitrary",)), name=name)(zz, zz, zz, gq, gk, bf, triu, dqn, dkn, dcum)


def _fox_blocks(s):
    return min(256, s), min(512, s)


NEG = -1e30


def fox_attn_fwd(qn, kn, zz, cum_col, cum_row, *, name):
    s = qn.shape[0]
    bq, bk = _fox_blocks(s)

    def body(q_ref, k_ref, v_ref, cc_ref, cr_ref, o_ref, lse_ref):
        qi = pl.program_id(1)
        q = q_ref[...]
        cq = cc_ref[...]
        rows = qi * bq + lax.broadcasted_iota(jnp.int32, (bq, bk), 0)
        cols0 = lax.broadcasted_iota(jnp.int32, (bq, bk), 1)

        def step(j, carry):
            m, l, acc = carry
            off = pl.multiple_of(j * bk, bk)
            k = k_ref[pl.ds(off, bk), :]
            v = v_ref[pl.ds(off, bk), :].astype(MMT)
            sc = _dot_nt(q, k) + cq - cr_ref[pl.ds(j, 1), :]
            sc = jnp.where(rows >= cols0 + j * bk, sc, NEG)
            m_new = jnp.maximum(m, jnp.max(sc, axis=1, keepdims=True))
            alpha = jnp.exp(m - m_new)
            p = jnp.exp(sc - m_new)
            return m_new, alpha * l + jnp.sum(p, axis=1, keepdims=True), alpha * acc + _dot(p.astype(MMT), v)

        nk = ((qi + 1) * bq + bk - 1) // bk
        m, l, acc = lax.fori_loop(0, nk, step, (jnp.full((bq, 1), NEG, F32), jnp.zeros((bq, 1), F32),
                                                jnp.zeros((bq, HD), F32)))
        o_ref[...] = acc / l
        lse_ref[...] = m + jnp.log(l)

    return pl.pallas_call(
        body, grid=(NH, s // bq),
        in_specs=[pl.BlockSpec((bq, HD), lambda h, i: (i, h)), pl.BlockSpec((s, HD), lambda h, i: (0, h)),
                  pl.BlockSpec((s, HD), lambda h, i: (0, FV // HD + h)),
                  pl.BlockSpec((None, bq, 1), lambda h, i: (h, i, 0)), pl.BlockSpec((None, s // bk, bk), lambda h, i: (h, 0, 0))],
        out_specs=[pl.BlockSpec((bq, HD), lambda h, i: (i, h)), pl.BlockSpec((None, bq, 1), lambda h, i: (h, i, 0))],
        out_shape=[_sds((s, BW), F32), _sds((NH, s, 1), F32)],
        compiler_params=_cparams(("parallel", "parallel")), name=name)(qn, kn, zz, cum_col, cum_row)


def fox_attn_bwd(qn, kn, zz, cum_col, cum_row, lse, do, *, name):
    s = qn.shape[0]
    bq, bk = _fox_blocks(s)
    nkc = s // bk

    def body(q_ref, k_ref, v_ref, cc_ref, cr_ref, lse_ref, do_ref, dq_ref, dk_ref, dv_ref, dc_ref, p_scr, dp_scr):
        qi = pl.program_id(1)

        @pl.when(qi == 0)
        def _():
            dk_ref[...] = jnp.zeros_like(dk_ref)
            dv_ref[...] = jnp.zeros_like(dv_ref)
            dc_ref[...] = jnp.zeros_like(dc_ref)

        q = q_ref[...]
        dob = do_ref[...].astype(MMT)
        cq = cc_ref[...]
        lse_v = lse_ref[...]
        rows = qi * bq + lax.broadcasted_iota(jnp.int32, (bq, bk), 0)
        cols0 = lax.broadcasted_iota(jnp.int32, (bq, bk), 1)
        nk = ((qi + 1) * bq + bk - 1) // bk

        def probs(j, delta):
            off = pl.multiple_of(j * bk, bk)
            sc = _dot_nt(q, k_ref[pl.ds(off, bk), :]) + cq - cr_ref[pl.ds(j, 1), :]
            p = jnp.where(rows >= cols0 + j * bk, jnp.exp(sc - lse_v), 0.0)
            dp = _dot_nt(dob, v_ref[pl.ds(off, bk), :].astype(MMT))
            p_scr[j] = p
            dp_scr[j] = dp
            return delta + jnp.sum(p * dp, axis=1, keepdims=True)

        delta = lax.fori_loop(0, nk, probs, jnp.zeros((bq, 1), F32))

        def grads(j, dq):
            off = pl.multiple_of(j * bk, bk)
            p = p_scr[j]
            ds = p * (dp_scr[j] - delta)
            dsm = ds.astype(MMT)
            dv_ref[pl.ds(off, bk), :] += _dot_tn(p.astype(MMT), dob)
            dk_ref[pl.ds(off, bk), :] += _dot_tn(dsm, q)
            dc_ref[pl.ds(j, 1), :] -= jnp.sum(ds, axis=0, keepdims=True)
            return dq + _dot(dsm, k_ref[pl.ds(off, bk), :])

        dq_ref[...] = lax.fori_loop(0, nk, grads, jnp.zeros((bq, HD), F32))

    full = lambda c0=0: pl.BlockSpec((s, HD), lambda h, i: (0, c0 + h))
    blk = lambda: pl.BlockSpec((bq, HD), lambda h, i: (i, h))
    colv = lambda: pl.BlockSpec((None, bq, 1), lambda h, i: (h, i, 0))
    rowv = lambda: pl.BlockSpec((None, nkc, bk), lambda h, i: (h, 0, 0))
    return pl.pallas_call(
        body, grid=(NH, s // bq),
        in_specs=[blk(), full(), full(FV // HD), colv(), rowv(), colv(), blk()],
        out_specs=[blk(), full(), full(), rowv()],
        out_shape=[_sds((s, BW), F32), _sds((s, BW), F32), _sds((s, BW), F32), _sds((NH, nkc, bk), F32)],
        scratch_shapes=[pltpu.VMEM((nkc, bq, bk), F32), pltpu.VMEM((nkc, bq, bk), F32)],
        compiler_params=_cparams(("parallel", "arbitrary")), name=name)(qn, kn, zz, cum_col, cum_row, lse, do)


def _branch_f(rets, rgs, glas, ggs, ret_g, gla_g):
    out_r, out_g = [], []
    for h in range(NH):
        xc = rets[h] - jnp.mean(rets[h], axis=-1, keepdims=True)
        y = xc * lax.rsqrt(jnp.mean(xc * xc, axis=-1, keepdims=True) + EPS) * ret_g[h]
        out_r.append(_silu(rgs[h]) * y)
        x = glas[h]
        y = x * lax.rsqrt(jnp.mean(x * x, axis=-1, keepdims=True) + EPS) * gla_g
        out_g.append(_silu(ggs[h]) * y)
    return out_r, out_g


def _w_br_spec(layer):
    return pl.BlockSpec((None, 3, BW, D), lambda i: (layer, 0, 0, 0))


def mix_fwd(ret_raw, gla_raw, fox_o, zz, ret_g, gla_g, b_mg, w_br, *, name, layer):
    s = zz.shape[0]
    t = _row_tile(s)

    def body(r_ref, g_ref, f_ref, rg_ref, gg_ref, gp_ref, rgn_ref, ggn_ref, bmg_ref, w_ref, o_ref):
        rgn = rgn_ref[...]
        br_r, br_g = _branch_f(_heads(r_ref[...]), _heads(rg_ref[...]), _heads(g_ref[...]), _heads(gg_ref[...]),
                               _heads(rgn), ggn_ref[...])
        brs = [jnp.concatenate(br_r, axis=1), jnp.concatenate(br_g, axis=1), f_ref[...]]
        acc = jnp.zeros((t, D), F32)
        for b in range(3):
            gate = jax.nn.sigmoid(gp_ref[:, b * D:(b + 1) * D] + bmg_ref[:, b * D:(b + 1) * D])
            acc = acc + gate * _dot(brs[b].astype(MMT), w_ref[b])
        o_ref[...] = acc.astype(o_ref.dtype)

    row = lambda w, c=0: pl.BlockSpec((t, w), lambda i: (i, c // w))
    cst = lambda shp: pl.BlockSpec(shp, lambda i: (0,) * len(shp))
    return pl.pallas_call(
        body, grid=(s // t,),
        in_specs=[row(BW), row(BW), row(BW), row(BW, RG), row(BW, GG), row(3 * D, GP), cst((1, BW)), cst((1, HD)),
                  cst((1, 3 * D)), _w_br_spec(layer)],
        out_specs=row(D), out_shape=_sds((s, D), MMT),
        compiler_params=_cparams(("parallel",)), name=name)(ret_raw, gla_raw, fox_o, zz, zz, zz, ret_g, gla_g, b_mg, w_br)


def mix_bwd(ret_raw, gla_raw, fox_o, zz, ret_g, gla_g, b_mg, w_br, dmi, *, name, layer):
    s = zz.shape[0]
    t = _row_tile(s)

    def body(r_ref, g_ref, f_ref, rg_ref, gg_ref, gp_ref, rgn_ref, ggn_ref, bmg_ref, w_ref, dmi_ref,
             dr_ref, dg_ref, df_ref, drg_ref, dgg_ref, dgp_ref, dw_ref, drgn_ref, dggn_ref, dbmg_ref):
        @pl.when(pl.program_id(0) == 0)
        def _():
            dw_ref[...] = jnp.zeros_like(dw_ref)
            drgn_ref[...] = jnp.zeros_like(drgn_ref)
            dggn_ref[...] = jnp.zeros_like(dggn_ref)
            dbmg_ref[...] = jnp.zeros_like(dbmg_ref)

        (br_r, br_g), vjp = jax.vjp(_branch_f, _heads(r_ref[...]), _heads(rg_ref[...]), _heads(g_ref[...]),
                                    _heads(gg_ref[...]), _heads(rgn_ref[...]), ggn_ref[...])
        brs = [jnp.concatenate(br_r, axis=1).astype(MMT), jnp.concatenate(br_g, axis=1).astype(MMT),
               f_ref[...].astype(MMT)]
        dmi_v = dmi_ref[...].astype(F32)
        dbr = []
        for b in range(3):
            w = w_ref[b]
            ybr = _dot(brs[b], w)
            gate = jax.nn.sigmoid(gp_ref[:, b * D:(b + 1) * D] + bmg_ref[:, b * D:(b + 1) * D])
            dgp = dmi_v * ybr * gate * (1.0 - gate)
            dgp_ref[:, b * D:(b + 1) * D] = dgp.astype(dgp_ref.dtype)
            dbmg_ref[:, b * D:(b + 1) * D] += jnp.sum(dgp, axis=0, keepdims=True)
            dy = (dmi_v * gate).astype(MMT)
            dw_ref[b] += _dot_tn(brs[b], dy)
            dbr.append(_dot_nt(dy, w))
        dr, drg, dg, dgg, drgn, dggn = vjp((_heads(dbr[0]), _heads(dbr[1])))
        dr_ref[...] = jnp.concatenate(dr, axis=1)
        dg_ref[...] = jnp.concatenate(dg, axis=1)
        df_ref[...] = dbr[2]
        drg_ref[...] = jnp.concatenate(drg, axis=1).astype(drg_ref.dtype)
        dgg_ref[...] = jnp.concatenate(dgg, axis=1).astype(dgg_ref.dtype)
        drgn_ref[...] += jnp.concatenate(drgn, axis=1)
        dggn_ref[...] += dggn

    row = lambda w, c=0: pl.BlockSpec((t, w), lambda i: (i, c // w))
    cst = lambda shp: pl.BlockSpec(shp, lambda i: (0,) * len(shp))
    return pl.pallas_call(
        body, grid=(s // t,),
        in_specs=[row(BW), row(BW), row(BW), row(BW, RG), row(BW, GG), row(3 * D, GP), cst((1, BW)), cst((1, HD)),
                  cst((1, 3 * D)), _w_br_spec(layer), row(D)],
        out_specs=[row(BW), row(BW), row(BW), row(BW), row(BW), row(3 * D), cst((3, BW, D)), cst((1, BW)), cst((1, HD)),
                   cst((1, 3 * D))],
        out_shape=[_sds((s, BW), F32)] * 3 + [_sds((s, BW), MMT)] * 2 + [_sds((s, 3 * D), MMT), _sds((3, BW, D), F32),
                                                                          _sds((1, BW), F32), _sds((1, HD), F32),
                                                                          _sds((1, 3 * D), F32)],
        compiler_params=_cparams(("arbitrary",)), name=name)(ret_raw, gla_raw, fox_o, zz, zz, zz, ret_g, gla_g, b_mg, w_br, dmi)


CT = 256


def _shift_down(x, k, rows):
    return jnp.where(rows >= k, pltpu.roll(x, k, 0), 0.0)


def _shift_up(x, k, rows, s):
    return jnp.where(rows < s - k, pltpu.roll(x, s - k, 0), 0.0)


def conv_fwd(ug, w_conv, b_conv, *, name):
    s = ug.shape[0]
    nt = DFF // CT

    def body(u_ref, g_ref, w_ref, b_ref, a_ref):
        u = u_ref[...]
        rows = lax.broadcasted_iota(jnp.int32, u.shape, 0)
        uc = b_ref[...] + w_ref[0:1, :] * _shift_down(u, 2, rows) + w_ref[1:2, :] * _shift_down(u, 1, rows) + w_ref[2:3, :] * u
        a_ref[...] = (_silu(uc) * g_ref[...]).astype(a_ref.dtype)

    return pl.pallas_call(
        body, grid=(nt,),
        in_specs=[pl.BlockSpec((s, CT), lambda j: (0, j)), pl.BlockSpec((s, CT), lambda j: (0, nt + j)),
                  pl.BlockSpec((3, CT), lambda j: (0, j)), pl.BlockSpec((1, CT), lambda j: (0, j))],
        out_specs=pl.BlockSpec((s, CT), lambda j: (0, j)), out_shape=_sds((s, DFF), MMT),
        compiler_params=_cparams(("parallel",)), name=name)(ug, ug, w_conv, b_conv)


def conv_bwd(ug, w_conv, b_conv, da, *, name):
    s = ug.shape[0]
    nt = DFF // CT

    def body(u_ref, g_ref, w_ref, b_ref, da_ref, du_ref, dg_ref, dw_ref, db_ref):
        u = u_ref[...]
        rows = lax.broadcasted_iota(jnp.int32, u.shape, 0)
        u2, u1 = _shift_down(u, 2, rows), _shift_down(u, 1, rows)
        uc = b_ref[...] + w_ref[0:1, :] * u2 + w_ref[1:2, :] * u1 + w_ref[2:3, :] * u
        sg = jax.nn.sigmoid(uc)
        da_v = da_ref[...]
        dg_ref[...] = (da_v * uc * sg).astype(dg_ref.dtype)
        duc = da_v * g_ref[...] * sg * (1.0 + uc * (1.0 - sg))
        du = w_ref[2:3, :] * duc + w_ref[1:2, :] * _shift_up(duc, 1, rows, s) + w_ref[0:1, :] * _shift_up(duc, 2, rows, s)
        du_ref[...] = du.astype(du_ref.dtype)
        dw_ref[0:1, :] = jnp.sum(duc * u2, axis=0, keepdims=True)
        dw_ref[1:2, :] = jnp.sum(duc * u1, axis=0, keepdims=True)
        dw_ref[2:3, :] = jnp.sum(duc * u, axis=0, keepdims=True)
        db_ref[...] = jnp.sum(duc, axis=0, keepdims=True)

    col = lambda: pl.BlockSpec((s, CT), lambda j: (0, j))
    return pl.pallas_call(
        body, grid=(nt,),
        in_specs=[col(), pl.BlockSpec((s, CT), lambda j: (0, nt + j)), pl.BlockSpec((3, CT), lambda j: (0, j)),
                  pl.BlockSpec((1, CT), lambda j: (0, j)), col()],
        out_specs=[col(), col(), pl.BlockSpec((3, CT), lambda j: (0, j)), pl.BlockSpec((1, CT), lambda j: (0, j))],
        out_shape=[_sds((s, DFF), MMT), _sds((s, DFF), MMT), _sds((3, DFF), F32), _sds((1, DFF), F32)],
        compiler_params=_cparams(("parallel",)), name=name)(ug, ug, w_conv, b_conv, da)


def _tiles(s):
    return min(1024, s)


def layer_fwd(x, mod, p, cosf, sinf):
    s = x.shape[0]
    tm = _tiles(s)
    l = p["l"]
    shift1, scale1, gate1, shift2, scale2, gate2 = mod
    h = norm_mod(x, p["norm1_g"], scale1, shift1, name="norm_mod")
    zz = mm_nn(h, p["w1"], tm=tm, tn=768, out_dtype=F32, name="mm_w1")
    ret_raw, rprev = retention_fwd(zz, cosf, sinf, name="ret_fwd")
    gla_raw, sprev = gla_fwd(zz, p["w_a2p"], p["b_gla_a"], name="gla_fwd")
    qn, kn, cum = fox_pre(zz, p["q_norm_g"], p["k_norm_g"], p["b_foxp"], name="fox_pre")
    bq, bk = _fox_blocks(s)
    cum_t = cum[:, :NH].T
    cum_col, cum_row = cum_t[:, :, None], cum_t.reshape(NH, s // bk, bk)
    fox_o, lse = fox_attn_fwd(qn, kn, zz, cum_col, cum_row, name="fox_fwd")
    mi = mix_fwd(ret_raw, gla_raw, fox_o, zz, p["ret_norm_g"], p["gla_norm_g"], p["b_mg"], p["w_br"], name="mix_fwd",
                 layer=l)
    x1, mixed = mm_nn_residual(mi, p["w_o"], x, gate1, tm=tm, tn=512, name="mm_wo", layer=l)
    h2 = norm_mod(x1, p["norm2_g"], scale2, shift2, name="norm_mod")
    ug = mm_nn(h2, p["w_up"], tm=tm, tn=512, out_dtype=F32, name="mm_wup", layer=l)
    a = conv_fwd(ug, p["w_conv"], p["b_conv"], name="conv_fwd")
    x2, y = mm_nn_residual(a, p["w_down"], x1, gate2, tm=tm, tn=512, name="mm_wdown", layer=l)
    saved = dict(x=x, h=h, zz=zz, ret_raw=ret_raw, rprev=rprev, gla_raw=gla_raw, sprev=sprev, qn=qn, kn=kn,
                 cum_col=cum_col, cum_row=cum_row, fox_o=fox_o, lse=lse, mi=mi, mixed=mixed, x1=x1, h2=h2, ug=ug, a=a, y=y)
    return x2, saved


def layer_bwd(dx2, mod, p, sv, cosf, sinf, stacks):
    s = dx2.shape[0]
    tm = _tiles(s)
    l = p["l"]
    shift1, scale1, gate1, shift2, scale2, gate2 = mod
    g, stacks = {}, dict(stacks)
    dy, dgate2 = gate_bwd(dx2, sv["y"], gate2, name="gate_bwd")
    stacks["w_down"] = mm_tn(sv["a"], dy, tm=min(1408, DFF), tn=512, out_dtype=MMT, name="mm_dwdown",
                             stack=stacks["w_down"], layer=l)
    da = mm_nt(dy, p["w_down"], tm=tm, tn=1408, out_dtype=F32, name="mm_da", layer=l)
    du, dg, g["w_conv"], g["b_conv"] = conv_bwd(sv["ug"], p["w_conv"], p["b_conv"], da, name="conv_bwd")
    dug = jnp.concatenate([du, dg], axis=1)
    stacks["w_up"] = mm_tn(sv["h2"], dug, tm=512, tn=512, out_dtype=MMT, name="mm_dwup", stack=stacks["w_up"], layer=l)
    dh2 = mm_nt(dug, p["w_up"], tm=min(512, s), tn=512, out_dtype=F32, name="mm_dh2", layer=l)
    dx1, g["norm2_g"], dscale2, dshift2 = norm_mod_bwd(sv["x1"], dh2, dx2, p["norm2_g"], scale2, shift2, name="norm_mod_bwd")
    dmixed, dgate1 = gate_bwd(dx1, sv["mixed"], gate1, name="gate_bwd")
    stacks["w_o"] = mm_tn(sv["mi"], dmixed, tm=512, tn=512, out_dtype=MMT, name="mm_dwo", stack=stacks["w_o"], layer=l)
    dmi = mm_nt(dmixed, p["w_o"], tm=tm, tn=512, out_dtype=MMT, name="mm_dmi", layer=l)
    zz = sv["zz"]
    (dret, dgla, dfox, drg, dgg, dgp, g["w_br"], g["ret_norm_g"], g["gla_norm_g"], g["b_mg"]) = mix_bwd(
        sv["ret_raw"], sv["gla_raw"], sv["fox_o"], zz, p["ret_norm_g"], p["gla_norm_g"], p["b_mg"], p["w_br"], dmi,
        name="mix_bwd", layer=l)
    dqn, dkn, dfv, dcum_row = fox_attn_bwd(sv["qn"], sv["kn"], zz, sv["cum_col"], sv["cum_row"], sv["lse"], dfox,
                                           name="fox_bwd")
    dfv = dfv.astype(MMT)
    dcum = jnp.pad(dcum_row.reshape(NH, s).T, ((0, 0), (0, HD - NH)))
    dfqk, dff, g["q_norm_g"], g["k_norm_g"], g["b_foxp"] = fox_pre_bwd(
        zz, p["q_norm_g"], p["k_norm_g"], p["b_foxp"], dqn, dkn, dcum, name="fox_pre_bwd")
    dgqkv, dlr, g["w_a2p"], g["b_gla_a"] = gla_bwd(zz, p["w_a2p"], p["b_gla_a"], sv["sprev"], dgla, name="gla_bwd")
    drqkv = retention_bwd(zz, cosf, sinf, sv["rprev"], dret, name="ret_bwd")
    dzz = jnp.concatenate([dgp, drqkv, drg, dgqkv, dgg, dfqk, dfv, dlr, dff], axis=1)
    stacks["w_mg"] = mm_tn(sv["h"], dzz, tm=512, tn=768, out_dtype=MMT, name="mm_dwmg", ncols=RQ, stack=stacks["w_mg"],
                           layer=l)
    g["w_z"] = mm_tn(sv["h"], dzz, tm=512, tn=768, out_dtype=MMT, name="mm_dwz", col0=RQ)
    dh = mm_nt(dzz, p["w1"], tm=min(512, s), tn=512, out_dtype=F32, name="mm_dh")
    dx, g["norm1_g"], dscale1, dshift1 = norm_mod_bwd(sv["x"], dh, dx1, p["norm1_g"], scale1, shift1, name="norm_mod_bwd")
    dmod = jnp.concatenate([dshift1, dscale1, dgate1, dshift2, dscale2, dgate2], axis=1)
    return dx, g, dmod, stacks


def build_w1(w_in, w_mg):
    z = lambda n: jnp.zeros((D, n), w_in.dtype)
    return jnp.concatenate([w_mg, w_in[:, :3072], w_in[:, 3088:5136], w_in[:, 3072:3088], z(HD - GLR),
                            w_in[:, 5136:5140], z(HD - NH)], axis=1)


def unalign_dw_in(dwz):
    o = lambda c: c - RQ
    return jnp.concatenate([dwz[:, :o(GG)], dwz[:, o(LR):o(LR) + GLR], dwz[:, o(GG):o(LR)], dwz[:, o(FF):o(FF) + NH]], axis=1)


def layer_params(w, l):
    row = lambda v: v[l][None, :]
    w_in = jnp.moveaxis(w["w_in"][l], 0, 1).reshape(D, IN_W)
    return dict(
        l=l, norm1_g=row(w["norm1_g"]), norm2_g=row(w["norm2_g"]), w1=build_w1(w_in, w["w_mg"][l]),
        w_a2p=jnp.pad(w["w_gla_a2"][l], ((0, HD - GLR), (0, 0))), b_gla_a=row(w["b_gla_a"]),
        b_foxp=jnp.pad(row(w["b_fox_f"]), ((0, 0), (0, HD - NH))), ret_norm_g=row(w["ret_norm_g"]),
        gla_norm_g=row(w["gla_norm_g"]), q_norm_g=row(w["q_norm_g"]), k_norm_g=row(w["k_norm_g"]),
        w_br=w["w_br"], b_mg=row(w["b_mg"]), w_o=w["w_o"], w_up=w["w_up"], w_conv=w["w_conv"][l],
        b_conv=row(w["b_conv"]), w_down=w["w_down"])


def layer_grads(g):
    vec = lambda v: v[0]
    w_in = jnp.moveaxis(unalign_dw_in(g["w_z"]).reshape(D, 4, IN_W // 4), 1, 0)
    return dict(
        norm1_g=vec(g["norm1_g"]), norm2_g=vec(g["norm2_g"]), w_in=w_in, w_gla_a2=g["w_a2p"][:GLR], b_gla_a=vec(g["b_gla_a"]),
        b_fox_f=g["b_foxp"][0, :NH], ret_norm_g=vec(g["ret_norm_g"]), gla_norm_g=vec(g["gla_norm_g"]),
        q_norm_g=vec(g["q_norm_g"]), k_norm_g=vec(g["k_norm_g"]), w_br=g["w_br"], b_mg=vec(g["b_mg"]),
        w_conv=g["w_conv"], b_conv=vec(g["b_conv"]))


def ada_mod(c_all, w_ada, b_ada):
    nl, _, n = w_ada.shape

    def body(c_ref, w_ref, b_ref, o_ref):
        o_ref[...] = _dot(_silu(c_ref[...]), w_ref[...], HI) + b_ref[...]

    return pl.pallas_call(
        body, grid=(nl,),
        in_specs=[pl.BlockSpec((8, D), lambda l: (0, 0)), pl.BlockSpec((None, D, n), lambda l: (l, 0, 0)),
                  pl.BlockSpec((None, 1, n), lambda l: (l, 0, 0))],
        out_specs=pl.BlockSpec((None, 8, n), lambda l: (l, 0, 0)), out_shape=_sds((nl, 8, n), F32),
        compiler_params=_cparams(("parallel",)), name="ada_mod")(c_all, w_ada, b_ada)


def ada_dw(c_all, dmod):
    nl, _, n = dmod.shape

    def body(c_ref, d_ref, o_ref):
        o_ref[...] = _dot_tn(_silu(c_ref[...]), d_ref[...], HI)

    return pl.pallas_call(
        body, grid=(nl,),
        in_specs=[pl.BlockSpec((8, D), lambda l: (0, 0)), pl.BlockSpec((None, 8, n), lambda l: (l, 0, 0))],
        out_specs=pl.BlockSpec((None, D, n), lambda l: (l, 0, 0)), out_shape=_sds((nl, D, n), F32),
        compiler_params=_cparams(("parallel",)), name="ada_dw")(c_all, dmod)


def sum_devices(g):
    def body(g_ref, o_ref):
        acc = g_ref[0]
        for d in range(1, 8):
            acc = acc + g_ref[d]
        o_ref[...] = acc

    return pl.pallas_call(body, out_shape=_sds(g.shape[1:], F32), name="sum_devices")(g)


def adamw(w, g, m, v, *, block, name):
    nd = w.ndim
    grid = tuple(w.shape[i] // block[i] for i in range(nd))
    bc1 = 1.0 - ADAM_B1 ** ADAM_STEP
    bc2 = 1.0 - ADAM_B2 ** ADAM_STEP

    def body(w_ref, g_ref, m_ref, v_ref, d_ref, nm_ref, nv_ref):
        gv = g_ref[...]
        nm = ADAM_B1 * m_ref[...] + (1.0 - ADAM_B1) * gv
        nv = ADAM_B2 * v_ref[...] + (1.0 - ADAM_B2) * (gv * gv)
        nm_ref[...] = nm
        nv_ref[...] = nv
        d_ref[...] = -ADAM_LR * ((nm / bc1) / (jnp.sqrt(nv / bc2) + ADAM_EPS) + ADAM_WD * w_ref[...])

    spec = pl.BlockSpec(tuple(block), lambda *i: i)
    return pl.pallas_call(
        body, grid=grid, in_specs=[spec] * 4, out_specs=[spec] * 3, out_shape=[_sds(w.shape, F32)] * 3,
        compiler_params=_cparams(("parallel",) * nd), name=name)(w, g, m, v)


MESH = pl.DeviceIdType.MESH
ANY = pl.BlockSpec(memory_space=pl.ANY)
VM = pl.BlockSpec(memory_space=pltpu.VMEM)


def _place():
    x, y, c = lax.axis_index("x"), lax.axis_index("y"), lax.axis_index("c")
    return x, y, c, [(1 - x, y), (x, 1 - y), (1 - x, 1 - y)]


def small_allgather(v, *, name):
    m_per, n = v.shape

    def body(x_ref, out_ref, send_sems, recv_sems, local_sem):
        x, y, c, chips = _place()
        me, sibling = (x, y, c), (x, y, 1 - c)

        def rows(px, py, pc):
            return out_ref.at[pl.ds((4 * px + 2 * py + pc) * m_per, m_per), :]

        def copy(k, block, to, src=None):
            return pltpu.make_async_remote_copy(
                src_ref=rows(*block) if src is None else src, dst_ref=rows(*block),
                send_sem=send_sems.at[k], recv_sem=recv_sems.at[k], device_id=to, device_id_type=MESH)

        mine = pltpu.make_async_copy(x_ref, rows(*me), local_sem)
        mine.start()
        first = [copy(0, me, sibling, src=x_ref)]
        first += [copy(1 + j, me, (*chip, c), src=x_ref) for j, chip in enumerate(chips)]
        for cp in first:
            cp.start()
        passed = [copy(4 + j, (*chip, c), sibling) for j, chip in enumerate(chips)]
        for j, chip in enumerate(chips):
            copy(1 + j, (*chip, c), me).wait_recv()
            passed[j].start()
        copy(0, sibling, me).wait_recv()
        for j, chip in enumerate(chips):
            copy(4 + j, (*chip, 1 - c), me).wait_recv()
        for cp in first + passed:
            cp.wait_send()
        mine.wait()

    return pl.pallas_call(
        body, out_shape=_sds((8 * m_per, n), v.dtype), in_specs=[VM], out_specs=VM,
        scratch_shapes=[pltpu.SemaphoreType.DMA((7,)), pltpu.SemaphoreType.DMA((7,)), pltpu.SemaphoreType.DMA],
        name=name)(v)


TENSORS = {
    "w_in": ("lead", None, (4, D, 1285), (1, D, 1285)),
    "w_mg": ("col", 768, (D, 3072), (512, 3072)),
    "w_br": ("col", 256, (3, BW, D), (3, BW, D)),
    "w_o": ("row", 256, (D, D), (D, D)),
    "w_up": ("col", 1408, (D, 5632), (256, 5632)),
    "w_down": ("row", 704, (DFF, D), (704, D)),
}
BIG = tuple(TENSORS)


def _shard_shape(name):
    kind, width, full, _ = TENSORS[name]
    if kind == "lead":
        return full[1:]
    return full[:-1] + (width,) if kind == "col" else (width,) + full[1:]


def _shard_view(ref, layers, name, k):
    kind, width, full, _ = TENSORS[name]
    if kind == "lead":
        return ref.at[layers, k]
    if kind == "row":
        return ref.at[layers, pl.ds(k * width, width)]
    return ref.at[(layers,) + (slice(None),) * (len(full) - 1) + (pl.ds(k * width, width),)]


def _remote(send_sems, recv_sems, k, src, dst, to):
    return pltpu.make_async_remote_copy(src_ref=src, dst_ref=dst, send_sem=send_sems.at[k], recv_sem=recv_sems.at[k],
                                        device_id=to, device_id_type=MESH)


def _dma_sems(n):
    return [pltpu.SemaphoreType.DMA((n,)), pltpu.SemaphoreType.DMA((n,))]


def gather_weights(shards, *, name):
    nl = shards[BIG[0]].shape[0]
    hl = nl // 2
    nt = len(BIG)
    per = 6 * hl + nl

    def body(*refs):
        p, o = dict(zip(BIG, refs[:nt])), dict(zip(BIG, refs[nt:2 * nt]))
        send_sems, recv_sems = refs[2 * nt:]
        x, y, c, chips = _place()
        mine, me, sibling = 2 * x + y, (x, y, c), (x, y, 1 - c)
        remote = functools.partial(_remote, send_sems, recv_sems)
        started = []

        def go(cp):
            cp.start()
            started.append(cp)

        for t, n in enumerate(BIG):
            for l in range(nl):
                go(remote(t * per + 6 * hl + l, p[n].at[l], _shard_view(o[n], l, n, mine), sibling))
        for i in range(hl):
            for t, n in enumerate(BIG):
                for j, chip in enumerate(chips):
                    layer = hl * c + i
                    go(remote(t * per + j * hl + i, p[n].at[layer], _shard_view(o[n], layer, n, mine), (*chip, c)))
        for i in range(hl):
            for t, n in enumerate(BIG):
                for j, chip in enumerate(chips):
                    landed = _shard_view(o[n], hl * c + i, n, 2 * chip[0] + chip[1])
                    remote(t * per + j * hl + i, landed, landed, me).wait_recv()
                    go(remote(t * per + 3 * hl + j * hl + i, landed, landed, sibling))
        for i in range(hl):
            for t, n in enumerate(BIG):
                for j, chip in enumerate(chips):
                    landed = _shard_view(o[n], hl * (1 - c) + i, n, 2 * chip[0] + chip[1])
                    remote(t * per + 3 * hl + j * hl + i, landed, landed, me).wait_recv()
        for t, n in enumerate(BIG):
            for l in range(nl):
                landed = _shard_view(o[n], l, n, mine)
                remote(t * per + 6 * hl + l, landed, landed, me).wait_recv()
        for cp in started:
            cp.wait_send()

    outs = pl.pallas_call(
        body, out_shape=[_sds((nl,) + TENSORS[n][2], shards[n].dtype) for n in BIG], in_specs=[ANY] * nt,
        out_specs=[ANY] * nt, scratch_shapes=_dma_sems(nt * per), name=name)(*[shards[n] for n in BIG])
    return dict(zip(BIG, outs))


def pair_exchange(g, *, name):
    nl = g[BIG[0]].shape[0]
    hl = nl // 2
    nt = len(BIG)

    def body(*refs):
        send_sems, recv_sems = refs[2 * nt:]
        x, y, c, _ = _place()
        copies = [_remote(send_sems, recv_sems, t, refs[t].at[pl.ds(hl * (1 - c), hl)], refs[nt + t], (x, y, 1 - c))
                  for t in range(nt)]
        for cp in copies:
            cp.start()
        for cp in copies:
            cp.wait()

    outs = pl.pallas_call(
        body, out_shape=[_sds((hl,) + g[n].shape[1:], g[n].dtype) for n in BIG], in_specs=[ANY] * nt, out_specs=[ANY] * nt,
        scratch_shapes=_dma_sems(nt), name=name)(*[g[n] for n in BIG])
    return dict(zip(BIG, outs))


def chip_exchange(s, *, name):
    hl = s[BIG[0]].shape[0]
    nt = len(BIG)

    def body(*refs):
        send_sems, recv_sems = refs[2 * nt:]
        x, y, c, chips = _place()
        copies = [_remote(send_sems, recv_sems, 3 * t + j, _shard_view(refs[t], pl.ds(0, hl), n, 2 * chip[0] + chip[1]),
                          refs[nt + t].at[j], (*chip, c))
                  for t, n in enumerate(BIG) for j, chip in enumerate(chips)]
        for cp in copies:
            cp.start()
        for cp in copies:
            cp.wait()

    outs = pl.pallas_call(
        body, out_shape=[_sds((3, hl) + _shard_shape(n), s[n].dtype) for n in BIG], in_specs=[ANY] * nt,
        out_specs=[ANY] * nt, scratch_shapes=_dma_sems(3 * nt), name=name)(*[s[n] for n in BIG])
    return dict(zip(BIG, outs))


def pair_share(f, *, name):
    nl = f[BIG[0]].shape[0]
    hl = nl // 2
    nt = len(BIG)

    def body(*refs):
        o = refs[nt:2 * nt]
        send_sems, recv_sems = refs[2 * nt:]
        x, y, c, _ = _place()
        copies = [_remote(send_sems, recv_sems, t, o[t].at[pl.ds(hl * c, hl)], o[t].at[pl.ds(hl * c, hl)], (x, y, 1 - c))
                  for t in range(nt)]
        for cp in copies:
            cp.start()
        for t, cp in enumerate(copies):
            cp.wait_send()
            theirs = o[t].at[pl.ds(hl * (1 - c), hl)]
            _remote(send_sems, recv_sems, t, theirs, theirs, (x, y, c)).wait_recv()

    outs = pl.pallas_call(
        body, out_shape=[_sds(f[n].shape, f[n].dtype) for n in BIG], in_specs=[ANY] * nt, out_specs=[ANY] * nt,
        input_output_aliases={t: t for t in range(nt)}, scratch_shapes=_dma_sems(nt), name=name)(*[f[n] for n in BIG])
    return dict(zip(BIG, outs))


def pair_add(g, r, idx, *, tensor, name):
    _, _, full, blk = TENSORS[tensor]
    hl = r.shape[0]

    def body(idx_ref, g_ref, r_ref, o_ref):
        o_ref[...] = (g_ref[...].astype(F32) + r_ref[...].astype(F32)).astype(o_ref.dtype)

    own = pl.BlockSpec((None,) + blk, lambda *a: (a[0],) + a[1:-1])
    return pl.pallas_call(
        body, out_shape=_sds(r.shape, r.dtype),
        grid_spec=pltpu.PrefetchScalarGridSpec(
            num_scalar_prefetch=1, grid=(hl,) + tuple(f // b for f, b in zip(full, blk)),
            in_specs=[pl.BlockSpec((None,) + blk, lambda *a: (hl * a[-1][0] + a[0],) + a[1:-1]), own], out_specs=own),
        compiler_params=_cparams(("parallel",) * (1 + len(full))), name=name)(idx, g, r)


def chip_add(s, r, idx, *, tensor, name):
    kind, width, full, _ = TENSORS[tensor]
    shard = _shard_shape(tensor)
    hl = s.shape[0]
    zeros = (0,) * len(shard)

    def body(idx_ref, s_ref, r0_ref, r1_ref, r2_ref, o_ref):
        o_ref[...] = ((s_ref[...].astype(F32) + r0_ref[...].astype(F32)) + r1_ref[...].astype(F32)) + r2_ref[...].astype(F32)

    if kind == "lead":
        mine = pl.BlockSpec((None, None) + shard, lambda i, ix: (i, ix[1]) + zeros)
    elif kind == "row":
        mine = pl.BlockSpec((None,) + shard, lambda i, ix: (i, ix[1]) + zeros[1:])
    else:
        mine = pl.BlockSpec((None,) + shard, lambda i, ix: (i,) + zeros[1:] + (ix[1],))
    peer = lambda j: pl.BlockSpec((None, None) + shard, lambda i, ix: (j, i) + zeros)
    return pl.pallas_call(
        body, out_shape=_sds((2 * hl,) + shard, F32),
        grid_spec=pltpu.PrefetchScalarGridSpec(
            num_scalar_prefetch=1, grid=(hl,), in_specs=[mine, peer(0), peer(1), peer(2)],
            out_specs=pl.BlockSpec((None,) + shard, lambda i, ix: (hl * ix[0] + i,) + zeros)),
        compiler_params=_cparams(("parallel",)), name=name)(idx, s, r, r, r)


def _flat_rows(arrs):
    v = jnp.concatenate([a.reshape(-1) for a in arrs])
    n = -(-v.shape[0] // 1024) * 1024
    return jnp.pad(v, (0, n - v.shape[0])).reshape(n // HD, HD)


def _unflat(buf, shapes):
    v, out, o = buf.reshape(-1), [], 0
    for s in shapes:
        n = int(np.prod(s))
        out.append(v[o:o + n].reshape(s))
        o += n
    return out


WEIGHTS = ("norm1_g", "norm2_g", "w_ada", "b_ada", "w_in", "w_gla_a2", "b_gla_a", "b_fox_f", "ret_norm_g", "gla_norm_g",
           "q_norm_g", "k_norm_g", "w_br", "w_mg", "b_mg", "w_o", "w_up", "w_conv", "b_conv", "w_down")
REPLICATED = ("norm1_g", "norm2_g", "b_gla_a", "b_fox_f", "ret_norm_g", "gla_norm_g", "q_norm_g", "k_norm_g", "b_mg", "b_conv")
ADAM_BLOCKS = dict(w_ada=(1, 256, 1536), w_in=(1, 256, 1285), w_br=(1, 3, BW, 256), w_mg=(1, 512, 768), w_o=(2, 256, D),
                   w_up=(1, 256, 1408), w_down=(1, 352, D))
ALL_AXES = ("x", "y", "c")


def kernel(x, c, norm1_g, norm2_g, w_ada, b_ada, w_in, w_gla_a2, b_gla_a, b_fox_f, ret_norm_g, gla_norm_g, q_norm_g, k_norm_g, w_br, w_mg, b_mg, w_o, w_up, w_conv, b_conv, w_down, loss_target, m_norm1_g, m_norm2_g, m_w_ada, m_b_ada, m_w_in, m_w_gla_a2, m_b_gla_a, m_b_fox_f, m_ret_norm_g, m_gla_norm_g, m_q_norm_g, m_k_norm_g, m_w_br, m_w_mg, m_b_mg, m_w_o, m_w_up, m_w_conv, m_b_conv, m_w_down, v_norm1_g, v_norm2_g, v_w_ada, v_b_ada, v_w_in, v_w_gla_a2, v_b_gla_a, v_b_fox_f, v_ret_norm_g, v_gla_norm_g, v_q_norm_g, v_k_norm_g, v_w_br, v_w_mg, v_b_mg, v_w_o, v_w_up, v_w_conv, v_b_conv, v_w_down):
    w = dict(zip(WEIGHTS, (norm1_g, norm2_g, w_ada, b_ada, w_in, w_gla_a2, b_gla_a, b_fox_f, ret_norm_g, gla_norm_g,
                           q_norm_g, k_norm_g, w_br, w_mg, b_mg, w_o, w_up, w_conv, b_conv, w_down)))
    m = dict(zip(WEIGHTS, (m_norm1_g, m_norm2_g, m_w_ada, m_b_ada, m_w_in, m_w_gla_a2, m_b_gla_a, m_b_fox_f, m_ret_norm_g,
                           m_gla_norm_g, m_q_norm_g, m_k_norm_g, m_w_br, m_w_mg, m_b_mg, m_w_o, m_w_up, m_w_conv, m_b_conv,
                           m_w_down)))
    v = dict(zip(WEIGHTS, (v_norm1_g, v_norm2_g, v_w_ada, v_b_ada, v_w_in, v_w_gla_a2, v_b_gla_a, v_b_fox_f, v_ret_norm_g,
                           v_gla_norm_g, v_q_norm_g, v_k_norm_g, v_w_br, v_w_mg, v_b_mg, v_w_o, v_w_up, v_w_conv, v_b_conv,
                           v_w_down)))
    nl = norm1_g.shape[0]
    seq = x.shape[1]
    xi, yi, ci = lax.axis_index("x"), lax.axis_index("y"), lax.axis_index("c")
    k_me = 2 * xi + yi
    b_me = 4 * xi + 2 * yi + ci
    ada_n = w_ada.shape[2]
    a2_n, conv_n = w_gla_a2.shape[2], w_conv.shape[2]

    blk = _flat_rows([c, w_gla_a2, w_conv])
    g1 = small_allgather(blk, name="gather_small").reshape(8, blk.shape[0], HD)
    c_all = g1[:, :D // HD].reshape(8, D)
    by_chip = g1[0::2].reshape(4, -1)[:, D:]
    a2_sh, conv_sh = by_chip[:, :nl * GLR * a2_n], by_chip[:, nl * GLR * a2_n:nl * (GLR * a2_n + 3 * conv_n)]
    full_small = dict(
        w_gla_a2=a2_sh.reshape(4, nl, GLR, a2_n).transpose(1, 2, 0, 3).reshape(nl, GLR, 4 * a2_n),
        w_conv=conv_sh.reshape(4, nl, 3, conv_n).transpose(1, 2, 0, 3).reshape(nl, 3, 4 * conv_n))

    b_ada_sh = lax.dynamic_slice_in_dim(b_ada, k_me * ada_n, ada_n, axis=1)[:, None, :]
    mod_sh = ada_mod(c_all, w_ada, b_ada_sh)
    g2 = small_allgather(mod_sh.reshape(nl * 8, ada_n), name="gather_mod").reshape(4, 2, nl, 8, ada_n)[:, 0]
    mod_me = lax.dynamic_index_in_dim(g2, b_me, axis=2, keepdims=False).transpose(1, 0, 2).reshape(nl, 4 * ada_n)

    wfull = {n: w[n] for n in REPLICATED}
    wfull.update(full_small)
    wfull.update(gather_weights({n: w[n].astype(MMT) for n in BIG}, name="gather_weights"))
    params = [layer_params(wfull, l) for l in range(nl)]
    mods = [[mod_me[l:l + 1, i * D:(i + 1) * D] for i in range(6)] for l in range(nl)]

    cosf, sinf = _rope_tables(seq)
    xs, saved = x[0], []
    for l in range(nl):
        xs, sv = layer_fwd(xs, mods[l], params[l], cosf, sinf)
        saved.append(sv)
    loss_part, dx = loss_and_grad(xs, loss_target[0], name="loss")
    loss = lax.psum(loss_part[0, 0], ALL_AXES)
    grads, dmods = [None] * nl, [None] * nl
    stacks = {n: lax.empty((nl,) + TENSORS[n][2], MMT) for n in ("w_mg", "w_o", "w_up", "w_down")}
    for l in reversed(range(nl)):
        dx, g, dmods[l], stacks = layer_bwd(dx, mods[l], params[l], saved[l], cosf, sinf, stacks)
        grads[l] = layer_grads(g)

    small_names = REPLICATED + ("w_gla_a2", "w_conv")
    small_shapes = [(nl, 6 * D)] + [(nl,) + grads[0][n].shape for n in small_names]
    vec = _flat_rows([jnp.concatenate(dmods, axis=0)] + [jnp.stack([grads[l][n] for l in range(nl)]) for n in small_names])
    gs = small_allgather(vec, name="gather_small_grads").reshape(8, vec.shape[0], HD)
    summed = _unflat(sum_devices(gs), small_shapes)
    grad = dict(zip(small_names, summed[1:]))
    grad["b_ada"] = summed[0]
    grad["w_gla_a2"] = lax.dynamic_slice_in_dim(grad["w_gla_a2"], k_me * a2_n, a2_n, axis=2)
    grad["w_conv"] = lax.dynamic_slice_in_dim(grad["w_conv"], k_me * conv_n, conv_n, axis=2)
    dmod_all = gs[:, :nl * 6 * D // HD].reshape(8, nl, 6 * D)
    dmod_sh = lax.dynamic_slice_in_dim(dmod_all, k_me * ada_n, ada_n, axis=2).transpose(1, 0, 2)
    grad["w_ada"] = ada_dw(c_all, dmod_sh)

    partial = dict(stacks)
    partial["w_in"] = jnp.stack([grads[l]["w_in"] for l in range(nl)])
    partial["w_br"] = jnp.stack([grads[l]["w_br"].astype(MMT) for l in range(nl)])
    idx = jnp.stack([ci, k_me]).astype(jnp.int32)
    from_sibling = pair_exchange(partial, name="rs_pair_exchange")
    chip_sum = {n: pair_add(partial[n], from_sibling[n], idx, tensor=n, name="rs_pair_add_" + n) for n in BIG}
    from_chips = chip_exchange(chip_sum, name="rs_chip_exchange")
    mine = {n: chip_add(chip_sum[n], from_chips[n], idx, tensor=n, name="rs_chip_add_" + n) for n in BIG}
    grad.update(pair_share(mine, name="rs_pair_share"))

    delta, new_m, new_v = {}, {}, {}
    for n, block in ADAM_BLOCKS.items():
        delta[n], new_m[n], new_v[n] = adamw(w[n], grad[n], m[n], v[n], block=block, name="adamw_" + n)
    rest = [n for n in WEIGHTS if n not in ADAM_BLOCKS]
    shapes = [w[n].shape for n in rest]
    flat = [_flat_rows([t[n] for n in rest]) for t in (w, grad, m, v)]
    outs = adamw(*flat, block=flat[0].shape, name="adamw_small")
    for t, o in zip((delta, new_m, new_v), outs):
        t.update(zip(rest, _unflat(o, shapes)))

    return (loss, dx[None], *[grad[n] for n in WEIGHTS], *[delta[n] for n in WEIGHTS], *[new_m[n] for n in WEIGHTS],
            *[new_v[n] for n in WEIGHTS])
```

```python
import functools

import numpy as np
import jax
import jax.numpy as jnp
from jax import lax
from jax.experimental import pallas as pl
from jax.experimental.pallas import tpu as pltpu

F32 = jnp.float32
MMT = jnp.bfloat16
HI = lax.Precision.HIGHEST

D = 1024
DEPTH = 4
NH = 4
HD = 128
BW = NH * HD
CH = 64
GDK = 64
GLR = 16
DFF = 2816
EPS = 1e-6
ROPE_BASE = 10000.0

GP, RQ, RK, RV, RG, GQ, GK, GV, GG, FQ, FK, FV, LR, FF = (
    0, 3072, 3584, 4096, 4608, 5120, 5376, 5632, 6144, 6656, 7168, 7680, 8192, 8320)
NZZ = 8448
IN_W = 5140

VMEM_LIMIT = 56 * 1024 * 1024

ADAM_LR, ADAM_B1, ADAM_B2, ADAM_EPS, ADAM_WD, ADAM_STEP = 0.001, 0.9, 0.999, 1e-08, 0.01, 10


def _cparams(sem=None):
    return pltpu.CompilerParams(dimension_semantics=sem, vmem_limit_bytes=VMEM_LIMIT)


def _sds(shape, dtype):
    return jax.ShapeDtypeStruct(tuple(shape), dtype)


def _dot(a, b, precision=None):
    return lax.dot_general(a, b, (((1,), (0,)), ((), ())), precision=precision, preferred_element_type=F32)


def _dot_nt(a, b, precision=None):
    return lax.dot_general(a, b, (((1,), (1,)), ((), ())), precision=precision, preferred_element_type=F32)


def _dot_tn(a, b, precision=None):
    return lax.dot_general(a, b, (((0,), (0,)), ((), ())), precision=precision, preferred_element_type=F32)


def _silu(x):
    return x * jax.nn.sigmoid(x)


def _log_sigmoid(x):
    return jnp.minimum(x, 0.0) - jnp.log(1.0 + jnp.exp(jnp.minimum(x, -x)))


@jax.custom_vjp
def _swap_halves(x):
    return pltpu.roll(x, HD // 2, 1)


_swap_halves.defvjp(lambda x: (_swap_halves(x), None), lambda _, g: (_swap_halves(g),))


@jax.custom_vjp
def _bdot(a, b):
    return _dot(a.astype(MMT), b.astype(MMT))


@jax.custom_vjp
def _bdot_nt(a, b):
    return _dot_nt(a.astype(MMT), b.astype(MMT))


@jax.custom_vjp
def _bdot_tn(a, b):
    return _dot_tn(a.astype(MMT), b.astype(MMT))


_bdot.defvjp(lambda a, b: (_bdot(a, b), (a, b)), lambda r, g: (_bdot_nt(g, r[1]), _bdot_tn(r[0], g)))
_bdot_nt.defvjp(lambda a, b: (_bdot_nt(a, b), (a, b)), lambda r, g: (_bdot(g, r[1]), _bdot_tn(g, r[0])))
_bdot_tn.defvjp(lambda a, b: (_bdot_tn(a, b), (a, b)), lambda r, g: (_bdot_nt(r[1], g), _bdot(r[0], g)))


def _stacked(blk, idx, layer):
    if layer is None:
        return pl.BlockSpec(blk, idx)
    return pl.BlockSpec((None,) + blk, lambda i, j: (layer,) + idx(i, j))


def mm_nn(a, b, *, tm, tn, out_dtype, name, layer=None):
    m, k = a.shape
    n = b.shape[-1]

    def body(a_ref, b_ref, o_ref):
        o_ref[...] = _dot(a_ref[...], b_ref[...]).astype(o_ref.dtype)

    return pl.pallas_call(
        body, grid=(m // tm, n // tn),
        in_specs=[pl.BlockSpec((tm, k), lambda i, j: (i, 0)), _stacked((k, tn), lambda i, j: (0, j), layer)],
        out_specs=pl.BlockSpec((tm, tn), lambda i, j: (i, j)),
        out_shape=_sds((m, n), out_dtype), compiler_params=_cparams(("parallel", "parallel")), name=name)(a, b)


def mm_nn_residual(a, b, res, gate, *, tm, tn, name, layer=None):
    m, k = a.shape
    n = b.shape[-1]

    def body(a_ref, b_ref, r_ref, g_ref, x_ref, y_ref):
        acc = _dot(a_ref[...], b_ref[...])
        y_ref[...] = acc
        x_ref[...] = r_ref[...] + g_ref[...] * acc

    return pl.pallas_call(
        body, grid=(m // tm, n // tn),
        in_specs=[pl.BlockSpec((tm, k), lambda i, j: (i, 0)), _stacked((k, tn), lambda i, j: (0, j), layer),
                  pl.BlockSpec((tm, tn), lambda i, j: (i, j)), pl.BlockSpec((1, tn), lambda i, j: (0, j))],
        out_specs=[pl.BlockSpec((tm, tn), lambda i, j: (i, j)), pl.BlockSpec((tm, tn), lambda i, j: (i, j))],
        out_shape=[_sds((m, n), F32), _sds((m, n), F32)],
        compiler_params=_cparams(("parallel", "parallel")), name=name)(a, b, res, gate)


def mm_nt(a, b, *, tm, tn, out_dtype, name, layer=None):
    m, k = a.shape
    n = b.shape[-2]

    def body(a_ref, b_ref, o_ref):
        o_ref[...] = _dot_nt(a_ref[...], b_ref[...]).astype(o_ref.dtype)

    return pl.pallas_call(
        body, grid=(m // tm, n // tn),
        in_specs=[pl.BlockSpec((tm, k), lambda i, j: (i, 0)), _stacked((tn, k), lambda i, j: (j, 0), layer)],
        out_specs=pl.BlockSpec((tm, tn), lambda i, j: (i, j)),
        out_shape=_sds((m, n), out_dtype), compiler_params=_cparams(("parallel", "parallel")), name=name)(a, b)


def mm_tn(a, b, *, tm, tn, out_dtype, name, col0=0, ncols=None, stack=None, layer=None):
    s, m = a.shape
    n = b.shape[1] - col0 if ncols is None else ncols
    c0 = col0 // tn

    def body(a_ref, b_ref, *rest):
        o_ref = rest[-1]
        o_ref[...] = _dot_tn(a_ref[...], b_ref[...]).astype(o_ref.dtype)

    in_specs = [pl.BlockSpec((s, tm), lambda i, j: (0, i)), pl.BlockSpec((s, tn), lambda i, j: (0, c0 + j))]
    if stack is None:
        return pl.pallas_call(
            body, grid=(m // tm, n // tn), in_specs=in_specs, out_specs=pl.BlockSpec((tm, tn), lambda i, j: (i, j)),
            out_shape=_sds((m, n), out_dtype), compiler_params=_cparams(("parallel", "parallel")), name=name)(a, b)
    return pl.pallas_call(
        body, grid=(m // tm, n // tn), in_specs=in_specs + [pl.BlockSpec(memory_space=pl.ANY)],
        out_specs=pl.BlockSpec((None, tm, tn), lambda i, j: (layer, i, j)),
        out_shape=_sds(stack.shape, stack.dtype), input_output_aliases={2: 0},
        compiler_params=_cparams(("parallel", "parallel")), name=name)(a, b, stack)


def _row_tile(s):
    return min(256, s)


def _norm_mod_f(x, g, scale, shift):
    r = lax.rsqrt(jnp.mean(x * x, axis=-1, keepdims=True) + EPS)
    return (x * r * g) * (1.0 + scale) + shift


def norm_mod(x, g, scale, shift, *, name):
    s = x.shape[0]
    t = _row_tile(s)

    def body(x_ref, g_ref, sc_ref, sh_ref, o_ref):
        o_ref[...] = _norm_mod_f(x_ref[...], g_ref[...], sc_ref[...], sh_ref[...]).astype(o_ref.dtype)

    vec = pl.BlockSpec((1, D), lambda i: (0, 0))
    return pl.pallas_call(
        body, grid=(s // t,), in_specs=[pl.BlockSpec((t, D), lambda i: (i, 0)), vec, vec, vec],
        out_specs=pl.BlockSpec((t, D), lambda i: (i, 0)), out_shape=_sds((s, D), MMT),
        compiler_params=_cparams(("parallel",)), name=name)(x, g, scale, shift)


def norm_mod_bwd(x, dh, dres, g, scale, shift, *, name):
    s = x.shape[0]
    t = _row_tile(s)

    def body(x_ref, dh_ref, dr_ref, g_ref, sc_ref, sh_ref, dx_ref, dg_ref, dsc_ref, dsh_ref):
        @pl.when(pl.program_id(0) == 0)
        def _():
            dg_ref[...] = jnp.zeros_like(dg_ref)
            dsc_ref[...] = jnp.zeros_like(dsc_ref)
            dsh_ref[...] = jnp.zeros_like(dsh_ref)

        _, vjp = jax.vjp(_norm_mod_f, x_ref[...], g_ref[...], sc_ref[...], sh_ref[...])
        dx, dg, dsc, dsh = vjp(dh_ref[...])
        dx_ref[...] = dr_ref[...] + dx
        dg_ref[...] += dg
        dsc_ref[...] += dsc
        dsh_ref[...] += dsh

    row = pl.BlockSpec((t, D), lambda i: (i, 0))
    vec = pl.BlockSpec((1, D), lambda i: (0, 0))
    return pl.pallas_call(
        body, grid=(s // t,), in_specs=[row, row, row, vec, vec, vec], out_specs=[row, vec, vec, vec],
        out_shape=[_sds((s, D), F32)] + [_sds((1, D), F32)] * 3,
        compiler_params=_cparams(("arbitrary",)), name=name)(x, dh, dres, g, scale, shift)


def gate_bwd(dx, y, gate, *, name):
    s = dx.shape[0]
    t = _row_tile(s)

    def body(dx_ref, y_ref, g_ref, dy_ref, dg_ref):
        @pl.when(pl.program_id(0) == 0)
        def _():
            dg_ref[...] = jnp.zeros_like(dg_ref)

        dxv = dx_ref[...]
        dy_ref[...] = (g_ref[...] * dxv).astype(dy_ref.dtype)
        dg_ref[...] += jnp.sum(dxv * y_ref[...], axis=0, keepdims=True)

    row = pl.BlockSpec((t, D), lambda i: (i, 0))
    vec = pl.BlockSpec((1, D), lambda i: (0, 0))
    return pl.pallas_call(
        body, grid=(s // t,), in_specs=[row, row, vec], out_specs=[row, vec],
        out_shape=[_sds((s, D), MMT), _sds((1, D), F32)],
        compiler_params=_cparams(("arbitrary",)), name=name)(dx, y, gate)


def loss_and_grad(xf, target, *, name):
    s = xf.shape[0]
    t = _row_tile(s)

    def body(x_ref, t_ref, l_ref, dx_ref):
        @pl.when(pl.program_id(0) == 0)
        def _():
            l_ref[...] = jnp.zeros_like(l_ref)

        e = x_ref[...] - t_ref[...]
        dx_ref[...] = e * (1.0 / D)
        l_ref[...] += 0.5 * jnp.sum(jnp.sum(e * e, axis=1, keepdims=True), axis=0, keepdims=True) * (1.0 / D)

    row = pl.BlockSpec((t, D), lambda i: (i, 0))
    return pl.pallas_call(
        body, grid=(s // t,), in_specs=[row, row], out_specs=[pl.BlockSpec((1, 1), lambda i: (0, 0)), row],
        out_shape=[_sds((1, 1), F32), _sds((s, D), F32)],
        compiler_params=_cparams(("arbitrary",)), name=name)(xf, target)


def _ret_consts():
    log_g = np.log1p(-np.exp2(-5.0 - np.arange(NH, dtype=np.float32))).astype(np.float32)
    idx = np.arange(CH, dtype=np.float32)
    d_intra = np.exp(np.abs(idx[:, None] - idx[None, :])[None] * log_g[:, None, None]).astype(np.float32)
    k_w = np.exp((CH - 1.0 - idx)[None, :] * log_g[:, None]).astype(np.float32)
    q_w = np.exp((idx + 1.0)[None, :] * log_g[:, None]).astype(np.float32)
    g_chunk = [float(v) for v in np.exp(np.float32(CH) * log_g).astype(np.float32)]
    bc = lambda a: np.ascontiguousarray(np.broadcast_to(a[:, :, None], (NH, CH, HD)))
    return jnp.asarray(d_intra), jnp.asarray(bc(k_w)), jnp.asarray(bc(q_w)), g_chunk


def _rope_tables(s):
    half = HD // 2
    inv_freq = (ROPE_BASE ** (-np.arange(half, dtype=np.float64) / half)).astype(np.float32)
    ang = (np.arange(s, dtype=np.float32)[:, None] * inv_freq[None, :]).astype(np.float64)
    cos, sin = np.cos(ang).astype(np.float32), np.sin(ang).astype(np.float32)
    return jnp.asarray(np.concatenate([cos, cos], axis=1)), jnp.asarray(np.concatenate([-sin, sin], axis=1))


def _ret_chunk(qs, ks, vs, rs, cos, sin, dintra, kw, qw, g_chunk):
    outs, rn = [], []
    for h in range(NH):
        q = qs[h] * cos + _swap_halves(qs[h]) * sin
        k = (ks[h] * cos + _swap_halves(ks[h]) * sin) * (HD ** -0.5)
        sc = _bdot_nt(q, k) * dintra[h]
        outs.append(_bdot(sc, vs[h]) + _bdot(q * qw[h], rs[h]))
        rn.append(g_chunk[h] * rs[h] + _bdot_tn(k * kw[h], vs[h]))
    return outs, rn


def _heads(x):
    return [x[:, h * HD:(h + 1) * HD] for h in range(NH)]


def retention_fwd(zz, cosf, sinf, *, name):
    s = zz.shape[0]
    n = s // CH
    dintra, kw, qw, g_chunk = _ret_consts()

    def body(q_ref, k_ref, v_ref, c_ref, s_ref, di_ref, kw_ref, qw_ref, o_ref, rp_ref, r_scr):
        @pl.when(pl.program_id(0) == 0)
        def _():
            r_scr[...] = jnp.zeros_like(r_scr)

        rprev = r_scr[...]
        rp_ref[0] = rprev
        outs, rn = _ret_chunk(_heads(q_ref[...]), _heads(k_ref[...]), _heads(v_ref[...]),
                              [rprev[h * HD:(h + 1) * HD] for h in range(NH)], c_ref[...], s_ref[...],
                              [di_ref[h] for h in range(NH)], [kw_ref[h] for h in range(NH)],
                              [qw_ref[h] for h in range(NH)], g_chunk)
        o_ref[...] = jnp.concatenate(outs, axis=1)
        r_scr[...] = jnp.concatenate(rn, axis=0)

    col = lambda c: pl.BlockSpec((CH, BW), lambda i: (i, c // BW))
    tab = pl.BlockSpec((CH, HD), lambda i: (i, 0))
    cst = lambda shp: pl.BlockSpec(shp, lambda i: (0,) * len(shp))
    return pl.pallas_call(
        body, grid=(n,),
        in_specs=[col(RQ), col(RK), col(RV), tab, tab, cst((NH, CH, CH)), cst((NH, CH, HD)), cst((NH, CH, HD))],
        out_specs=[pl.BlockSpec((CH, BW), lambda i: (i, 0)), pl.BlockSpec((1, BW, HD), lambda i: (i, 0, 0))],
        out_shape=[_sds((s, BW), F32), _sds((n, BW, HD), F32)],
        scratch_shapes=[pltpu.VMEM((BW, HD), F32)],
        compiler_params=_cparams(("arbitrary",)), name=name)(zz, zz, zz, cosf, sinf, dintra, kw, qw)


def retention_bwd(zz, cosf, sinf, rprev, do, *, name):
    s = zz.shape[0]
    n = s // CH
    dintra, kw, qw, g_chunk = _ret_consts()

    def body(q_ref, k_ref, v_ref, c_ref, s_ref, di_ref, kw_ref, qw_ref, rp_ref, do_ref, dz_ref, dr_scr):
        @pl.when(pl.program_id(0) == 0)
        def _():
            dr_scr[...] = jnp.zeros_like(dr_scr)

        rprev_v = rp_ref[0]
        f = functools.partial(_ret_chunk, cos=c_ref[...], sin=s_ref[...],
                              dintra=[di_ref[h] for h in range(NH)], kw=[kw_ref[h] for h in range(NH)],
                              qw=[qw_ref[h] for h in range(NH)], g_chunk=g_chunk)
        _, vjp = jax.vjp(f, _heads(q_ref[...]), _heads(k_ref[...]), _heads(v_ref[...]),
                         [rprev_v[h * HD:(h + 1) * HD] for h in range(NH)])
        dr = dr_scr[...]
        dq, dk, dv, drp = vjp((_heads(do_ref[...]), [dr[h * HD:(h + 1) * HD] for h in range(NH)]))
        dz_ref[...] = jnp.concatenate(dq + dk + dv, axis=1).astype(dz_ref.dtype)
        dr_scr[...] = jnp.concatenate(drp, axis=0)

    col = lambda c: pl.BlockSpec((CH, BW), lambda i: (n - 1 - i, c // BW))
    tab = pl.BlockSpec((CH, HD), lambda i: (n - 1 - i, 0))
    cst = lambda shp: pl.BlockSpec(shp, lambda i: (0,) * len(shp))
    return pl.pallas_call(
        body, grid=(n,),
        in_specs=[col(RQ), col(RK), col(RV), tab, tab, cst((NH, CH, CH)), cst((NH, CH, HD)), cst((NH, CH, HD)),
                  pl.BlockSpec((1, BW, HD), lambda i: (n - 1 - i, 0, 0)), pl.BlockSpec((CH, BW), lambda i: (n - 1 - i, 0))],
        out_specs=pl.BlockSpec((CH, 3 * BW), lambda i: (n - 1 - i, 0)),
        out_shape=_sds((s, 3 * BW), MMT),
        scratch_shapes=[pltpu.VMEM((BW, HD), F32)],
        compiler_params=_cparams(("arbitrary",)), name=name)(zz, zz, zz, cosf, sinf, dintra, kw, qw, rprev, do)


GKW = NH * GDK


def _gla_consts():
    tri = np.tril(np.ones((CH, CH), np.float32))
    mask_t = np.zeros((BW, GKW), np.float32)
    for h in range(NH):
        mask_t[h * HD:(h + 1) * HD, h * GDK:(h + 1) * GDK] = 1.0
    return jnp.asarray(tri), jnp.asarray(mask_t)


def _gla_chunk(q, k, v, glr, w_a2, b_a, st, tri, mask_t):
    la = _log_sigmoid(_bdot(glr, w_a2) + b_a) * (1.0 / 16.0)
    bc = _dot(tri, la, HI)
    be = jnp.sum(la, axis=0, keepdims=True)
    kv_t = _bdot_tn(v, k * jnp.exp(be - bc)) * mask_t
    sn = jnp.exp(be) * st + kv_t
    return _bdot_nt(q * (GDK ** -0.5), sn), sn


def gla_fwd(zz, w_a2p, b_a, *, name):
    s = zz.shape[0]
    n = s // CH
    tri, mask_t = _gla_consts()

    def body(q_ref, k_ref, v_ref, lr_ref, w_ref, b_ref, tri_ref, m_ref, o_ref, sp_ref, st_scr):
        @pl.when(pl.program_id(0) == 0)
        def _():
            st_scr[...] = jnp.zeros_like(st_scr)

        sp = st_scr[...]
        sp_ref[0] = sp
        o, sn = _gla_chunk(q_ref[...], k_ref[...], v_ref[...], lr_ref[...], w_ref[...], b_ref[...], sp,
                           tri_ref[...], m_ref[...])
        o_ref[...] = o
        st_scr[...] = sn

    cst = lambda shp: pl.BlockSpec(shp, lambda i: (0,) * len(shp))
    return pl.pallas_call(
        body, grid=(n,),
        in_specs=[pl.BlockSpec((CH, GKW), lambda i: (i, GQ // GKW)), pl.BlockSpec((CH, GKW), lambda i: (i, GK // GKW)),
                  pl.BlockSpec((CH, BW), lambda i: (i, GV // BW)), pl.BlockSpec((CH, HD), lambda i: (i, LR // HD)),
                  cst((HD, GKW)), cst((1, GKW)), cst((CH, CH)), cst((BW, GKW))],
        out_specs=[pl.BlockSpec((CH, BW), lambda i: (i, 0)), pl.BlockSpec((1, BW, GKW), lambda i: (i, 0, 0))],
        out_shape=[_sds((s, BW), F32), _sds((n, BW, GKW), F32)],
        scratch_shapes=[pltpu.VMEM((BW, GKW), F32)],
        compiler_params=_cparams(("arbitrary",)), name=name)(zz, zz, zz, zz, w_a2p, b_a, tri, mask_t)


def gla_bwd(zz, w_a2p, b_a, sprev, do, *, name):
    s = zz.shape[0]
    n = s // CH
    tri, mask_t = _gla_consts()

    def body(q_ref, k_ref, v_ref, lr_ref, w_ref, b_ref, tri_ref, m_ref, sp_ref, do_ref,
             dz_ref, dlr_ref, dw_ref, db_ref, ds_scr):
        @pl.when(pl.program_id(0) == 0)
        def _():
            ds_scr[...] = jnp.zeros_like(ds_scr)
            dw_ref[...] = jnp.zeros_like(dw_ref)
            db_ref[...] = jnp.zeros_like(db_ref)

        f = functools.partial(_gla_chunk, tri=tri_ref[...], mask_t=m_ref[...])
        _, vjp = jax.vjp(f, q_ref[...], k_ref[...], v_ref[...], lr_ref[...], w_ref[...], b_ref[...], sp_ref[0])
        dq, dk, dv, dlr, dw, db, dsp = vjp((do_ref[...], ds_scr[...]))
        dz_ref[...] = jnp.concatenate([dq, dk, dv], axis=1).astype(dz_ref.dtype)
        dlr_ref[...] = dlr.astype(dlr_ref.dtype)
        dw_ref[...] += dw
        db_ref[...] += db
        ds_scr[...] = dsp

    cst = lambda shp: pl.BlockSpec(shp, lambda i: (0,) * len(shp))
    r = lambda i: n - 1 - i
    return pl.pallas_call(
        body, grid=(n,),
        in_specs=[pl.BlockSpec((CH, GKW), lambda i: (r(i), GQ // GKW)), pl.BlockSpec((CH, GKW), lambda i: (r(i), GK // GKW)),
                  pl.BlockSpec((CH, BW), lambda i: (r(i), GV // BW)), pl.BlockSpec((CH, HD), lambda i: (r(i), LR // HD)),
                  cst((HD, GKW)), cst((1, GKW)), cst((CH, CH)), cst((BW, GKW)),
                  pl.BlockSpec((1, BW, GKW), lambda i: (r(i), 0, 0)), pl.BlockSpec((CH, BW), lambda i: (r(i), 0))],
        out_specs=[pl.BlockSpec((CH, 2 * GKW + BW), lambda i: (r(i), 0)), pl.BlockSpec((CH, HD), lambda i: (r(i), 0)),
                   cst((HD, GKW)), cst((1, GKW))],
        out_shape=[_sds((s, 2 * GKW + BW), MMT), _sds((s, HD), MMT), _sds((HD, GKW), F32), _sds((1, GKW), F32)],
        scratch_shapes=[pltpu.VMEM((BW, GKW), F32)],
        compiler_params=_cparams(("arbitrary",)), name=name)(zz, zz, zz, zz, w_a2p, b_a, tri, mask_t, sprev, do)


def _fox_pre_f(fqs, fks, ff, gq, gk, bf):
    def rms(x, g):
        return x * lax.rsqrt(jnp.mean(x * x, axis=-1, keepdims=True) + EPS) * g

    qn = [rms(x, gq) * (HD ** -0.5) for x in fqs]
    kn = [rms(x, gk) for x in fks]
    return qn, kn, _log_sigmoid(ff + bf)


def fox_pre(zz, gq, gk, bf, *, name):
    s = zz.shape[0]
    t = _row_tile(s)
    tri = jnp.asarray(np.tril(np.ones((t, t), np.float32)))

    def body(q_ref, k_ref, f_ref, gq_ref, gk_ref, b_ref, tri_ref, qn_ref, kn_ref, cum_ref, carry):
        @pl.when(pl.program_id(0) == 0)
        def _():
            carry[...] = jnp.zeros_like(carry)

        qn, kn, lf = _fox_pre_f(_heads(q_ref[...]), _heads(k_ref[...]), f_ref[...], gq_ref[...], gk_ref[...], b_ref[...])
        qn_ref[...] = jnp.concatenate(qn, axis=1).astype(qn_ref.dtype)
        kn_ref[...] = jnp.concatenate(kn, axis=1).astype(kn_ref.dtype)
        cum_ref[...] = _dot(tri_ref[...], lf, HI) + carry[...]
        carry[...] += jnp.sum(lf, axis=0, keepdims=True)

    vec = pl.BlockSpec((1, HD), lambda i: (0, 0))
    return pl.pallas_call(
        body, grid=(s // t,),
        in_specs=[pl.BlockSpec((t, BW), lambda i: (i, FQ // BW)), pl.BlockSpec((t, BW), lambda i: (i, FK // BW)),
                  pl.BlockSpec((t, HD), lambda i: (i, FF // HD)), vec, vec, vec, pl.BlockSpec((t, t), lambda i: (0, 0))],
        out_specs=[pl.BlockSpec((t, BW), lambda i: (i, 0)), pl.BlockSpec((t, BW), lambda i: (i, 0)),
                   pl.BlockSpec((t, HD), lambda i: (i, 0))],
        out_shape=[_sds((s, BW), MMT), _sds((s, BW), MMT), _sds((s, HD), F32)],
        scratch_shapes=[pltpu.VMEM((1, HD), F32)],
        compiler_params=_cparams(("arbitrary",)), name=name)(zz, zz, zz, gq, gk, bf, tri)


def fox_pre_bwd(zz, gq, gk, bf, dqn, dkn, dcum, *, name):
    s = zz.shape[0]
    t = _row_tile(s)
    nt = s // t
    triu = jnp.asarray(np.triu(np.ones((t, t), np.float32)))

    def body(q_ref, k_ref, f_ref, gq_ref, gk_ref, b_ref, tri_ref, dqn_ref, dkn_ref, dcum_ref,
             dz_ref, dff_ref, dgq_ref, dgk_ref, db_ref, carry):
        @pl.when(pl.program_id(0) == 0)
        def _():
            carry[...] = jnp.zeros_like(carry)
            dgq_ref[...] = jnp.zeros_like(dgq_ref)
            dgk_ref[...] = jnp.zeros_like(dgk_ref)
            db_ref[...] = jnp.zeros_like(db_ref)

        dcum_v = dcum_ref[...]
        dlf = _dot(tri_ref[...], dcum_v, HI) + carry[...]
        carry[...] += jnp.sum(dcum_v, axis=0, keepdims=True)
        _, vjp = jax.vjp(_fox_pre_f, _heads(q_ref[...]), _heads(k_ref[...]), f_ref[...], gq_ref[...], gk_ref[...], b_ref[...])
        dq, dk, dff, dgq, dgk, db = vjp((_heads(dqn_ref[...]), _heads(dkn_ref[...]), dlf))
        dz_ref[...] = jnp.concatenate(dq + dk, axis=1).astype(dz_ref.dtype)
        dff_ref[...] = dff.astype(dff_ref.dtype)
        dgq_ref[...] += dgq
        dgk_ref[...] += dgk
        db_ref[...] += db

    r = lambda i: nt - 1 - i
    vec = pl.BlockSpec((1, HD), lambda i: (0, 0))
    return pl.pallas_call(
        body, grid=(nt,),
        in_specs=[pl.BlockSpec((t, BW), lambda i: (r(i), FQ // BW)), pl.BlockSpec((t, BW), lambda i: (r(i), FK // BW)),
                  pl.BlockSpec((t, HD), lambda i: (r(i), FF // HD)), vec, vec, vec, pl.BlockSpec((t, t), lambda i: (0, 0)),
                  pl.BlockSpec((t, BW), lambda i: (r(i), 0)), pl.BlockSpec((t, BW), lambda i: (r(i), 0)),
                  pl.BlockSpec((t, HD), lambda i: (r(i), 0))],
        out_specs=[pl.BlockSpec((t, 2 * BW), lambda i: (r(i), 0)), pl.BlockSpec((t, HD), lambda i: (r(i), 0)), vec, vec, vec],
        out_shape=[_sds((s, 2 * BW), MMT), _sds((s, HD), MMT), _sds((1, HD), F32), _sds((1, HD), F32), _sds((1, HD), F32)],
        scratch_shapes=[pltpu.VMEM((1, HD), F32)],
        compiler_params=_cparams(("arbitrary",)), name=name)(zz, zz, zz, gq, gk, bf, triu, dqn, dkn, dcum)


def _fox_blocks(s):
    return min(256, s), min(512, s)


NEG = -1e30


def fox_attn_fwd(qn, kn, zz, cum_col, cum_row, *, name):
    s = qn.shape[0]
    bq, bk = _fox_blocks(s)

    def body(q_ref, k_ref, v_ref, cc_ref, cr_ref, o_ref, lse_ref):
        qi = pl.program_id(1)
        q = q_ref[...]
        cq = cc_ref[...]
        rows = qi * bq + lax.broadcasted_iota(jnp.int32, (bq, bk), 0)
        cols0 = lax.broadcasted_iota(jnp.int32, (bq, bk), 1)

        def step(j, carry):
            m, l, acc = carry
            off = pl.multiple_of(j * bk, bk)
            k = k_ref[pl.ds(off, bk), :]
            v = v_ref[pl.ds(off, bk), :].astype(MMT)
            sc = _dot_nt(q, k) + cq - cr_ref[pl.ds(j, 1), :]
            sc = jnp.where(rows >= cols0 + j * bk, sc, NEG)
            m_new = jnp.maximum(m, jnp.max(sc, axis=1, keepdims=True))
            alpha = jnp.exp(m - m_new)
            p = jnp.exp(sc - m_new)
            return m_new, alpha * l + jnp.sum(p, axis=1, keepdims=True), alpha * acc + _dot(p.astype(MMT), v)

        nk = ((qi + 1) * bq + bk - 1) // bk
        m, l, acc = lax.fori_loop(0, nk, step, (jnp.full((bq, 1), NEG, F32), jnp.zeros((bq, 1), F32),
                                                jnp.zeros((bq, HD), F32)))
        o_ref[...] = acc / l
        lse_ref[...] = m + jnp.log(l)

    return pl.pallas_call(
        body, grid=(NH, s // bq),
        in_specs=[pl.BlockSpec((bq, HD), lambda h, i: (i, h)), pl.BlockSpec((s, HD), lambda h, i: (0, h)),
                  pl.BlockSpec((s, HD), lambda h, i: (0, FV // HD + h)),
                  pl.BlockSpec((None, bq, 1), lambda h, i: (h, i, 0)), pl.BlockSpec((None, s // bk, bk), lambda h, i: (h, 0, 0))],
        out_specs=[pl.BlockSpec((bq, HD), lambda h, i: (i, h)), pl.BlockSpec((None, bq, 1), lambda h, i: (h, i, 0))],
        out_shape=[_sds((s, BW), F32), _sds((NH, s, 1), F32)],
        compiler_params=_cparams(("parallel", "parallel")), name=name)(qn, kn, zz, cum_col, cum_row)


def fox_attn_bwd(qn, kn, zz, cum_col, cum_row, lse, do, *, name):
    s = qn.shape[0]
    bq, bk = _fox_blocks(s)
    nkc = s // bk

    def body(q_ref, k_ref, v_ref, cc_ref, cr_ref, lse_ref, do_ref, dq_ref, dk_ref, dv_ref, dc_ref, p_scr, dp_scr):
        qi = pl.program_id(1)

        @pl.when(qi == 0)
        def _():
            dk_ref[...] = jnp.zeros_like(dk_ref)
            dv_ref[...] = jnp.zeros_like(dv_ref)
            dc_ref[...] = jnp.zeros_like(dc_ref)

        q = q_ref[...]
        dob = do_ref[...].astype(MMT)
        cq = cc_ref[...]
        lse_v = lse_ref[...]
        rows = qi * bq + lax.broadcasted_iota(jnp.int32, (bq, bk), 0)
        cols0 = lax.broadcasted_iota(jnp.int32, (bq, bk), 1)
        nk = ((qi + 1) * bq + bk - 1) // bk

        def probs(j, delta):
            off = pl.multiple_of(j * bk, bk)
            sc = _dot_nt(q, k_ref[pl.ds(off, bk), :]) + cq - cr_ref[pl.ds(j, 1), :]
            p = jnp.where(rows >= cols0 + j * bk, jnp.exp(sc - lse_v), 0.0)
            dp = _dot_nt(dob, v_ref[pl.ds(off, bk), :].astype(MMT))
            p_scr[j] = p
            dp_scr[j] = dp
            return delta + jnp.sum(p * dp, axis=1, keepdims=True)

        delta = lax.fori_loop(0, nk, probs, jnp.zeros((bq, 1), F32))

        def grads(j, dq):
            off = pl.multiple_of(j * bk, bk)
            p = p_scr[j]
            ds = p * (dp_scr[j] - delta)
            dsm = ds.astype(MMT)
            dv_ref[pl.ds(off, bk), :] += _dot_tn(p.astype(MMT), dob)
            dk_ref[pl.ds(off, bk), :] += _dot_tn(dsm, q)
            dc_ref[pl.ds(j, 1), :] -= jnp.sum(ds, axis=0, keepdims=True)
            return dq + _dot(dsm, k_ref[pl.ds(off, bk), :])

        dq_ref[...] = lax.fori_loop(0, nk, grads, jnp.zeros((bq, HD), F32))

    full = lambda c0=0: pl.BlockSpec((s, HD), lambda h, i: (0, c0 + h))
    blk = lambda: pl.BlockSpec((bq, HD), lambda h, i: (i, h))
    colv = lambda: pl.BlockSpec((None, bq, 1), lambda h, i: (h, i, 0))
    rowv = lambda: pl.BlockSpec((None, nkc, bk), lambda h, i: (h, 0, 0))
    return pl.pallas_call(
        body, grid=(NH, s // bq),
        in_specs=[blk(), full(), full(FV // HD), colv(), rowv(), colv(), blk()],
        out_specs=[blk(), full(), full(), rowv()],
        out_shape=[_sds((s, BW), F32), _sds((s, BW), F32), _sds((s, BW), F32), _sds((NH, nkc, bk), F32)],
        scratch_shapes=[pltpu.VMEM((nkc, bq, bk), F32), pltpu.VMEM((nkc, bq, bk), F32)],
        compiler_params=_cparams(("parallel", "arbitrary")), name=name)(qn, kn, zz, cum_col, cum_row, lse, do)


def _branch_f(rets, rgs, glas, ggs, ret_g, gla_g):
    out_r, out_g = [], []
    for h in range(NH):
        xc = rets[h] - jnp.mean(rets[h], axis=-1, keepdims=True)
        y = xc * lax.rsqrt(jnp.mean(xc * xc, axis=-1, keepdims=True) + EPS) * ret_g[h]
        out_r.append(_silu(rgs[h]) * y)
        x = glas[h]
        y = x * lax.rsqrt(jnp.mean(x * x, axis=-1, keepdims=True) + EPS) * gla_g
        out_g.append(_silu(ggs[h]) * y)
    return out_r, out_g


def _w_br_spec(layer):
    return pl.BlockSpec((None, 3, BW, D), lambda i: (layer, 0, 0, 0))


def mix_fwd(ret_raw, gla_raw, fox_o, zz, ret_g, gla_g, b_mg, w_br, *, name, layer):
    s = zz.shape[0]
    t = _row_tile(s)

    def body(r_ref, g_ref, f_ref, rg_ref, gg_ref, gp_ref, rgn_ref, ggn_ref, bmg_ref, w_ref, o_ref):
        rgn = rgn_ref[...]
        br_r, br_g = _branch_f(_heads(r_ref[...]), _heads(rg_ref[...]), _heads(g_ref[...]), _heads(gg_ref[...]),
                               _heads(rgn), ggn_ref[...])
        brs = [jnp.concatenate(br_r, axis=1), jnp.concatenate(br_g, axis=1), f_ref[...]]
        acc = jnp.zeros((t, D), F32)
        for b in range(3):
            gate = jax.nn.sigmoid(gp_ref[:, b * D:(b + 1) * D] + bmg_ref[:, b * D:(b + 1) * D])
            acc = acc + gate * _dot(brs[b].astype(MMT), w_ref[b])
        o_ref[...] = acc.astype(o_ref.dtype)

    row = lambda w, c=0: pl.BlockSpec((t, w), lambda i: (i, c // w))
    cst = lambda shp: pl.BlockSpec(shp, lambda i: (0,) * len(shp))
    return pl.pallas_call(
        body, grid=(s // t,),
        in_specs=[row(BW), row(BW), row(BW), row(BW, RG), row(BW, GG), row(3 * D, GP), cst((1, BW)), cst((1, HD)),
                  cst((1, 3 * D)), _w_br_spec(layer)],
        out_specs=row(D), out_shape=_sds((s, D), MMT),
        compiler_params=_cparams(("parallel",)), name=name)(ret_raw, gla_raw, fox_o, zz, zz, zz, ret_g, gla_g, b_mg, w_br)


def mix_bwd(ret_raw, gla_raw, fox_o, zz, ret_g, gla_g, b_mg, w_br, dmi, *, name, layer):
    s = zz.shape[0]
    t = _row_tile(s)

    def body(r_ref, g_ref, f_ref, rg_ref, gg_ref, gp_ref, rgn_ref, ggn_ref, bmg_ref, w_ref, dmi_ref,
             dr_ref, dg_ref, df_ref, drg_ref, dgg_ref, dgp_ref, dw_ref, drgn_ref, dggn_ref, dbmg_ref):
        @pl.when(pl.program_id(0) == 0)
        def _():
            dw_ref[...] = jnp.zeros_like(dw_ref)
            drgn_ref[...] = jnp.zeros_like(drgn_ref)
            dggn_ref[...] = jnp.zeros_like(dggn_ref)
            dbmg_ref[...] = jnp.zeros_like(dbmg_ref)

        (br_r, br_g), vjp = jax.vjp(_branch_f, _heads(r_ref[...]), _heads(rg_ref[...]), _heads(g_ref[...]),
                                    _heads(gg_ref[...]), _heads(rgn_ref[...]), ggn_ref[...])
        brs = [jnp.concatenate(br_r, axis=1).astype(MMT), jnp.concatenate(br_g, axis=1).astype(MMT),
               f_ref[...].astype(MMT)]
        dmi_v = dmi_ref[...].astype(F32)
        dbr = []
        for b in range(3):
            w = w_ref[b]
            ybr = _dot(brs[b], w)
            gate = jax.nn.sigmoid(gp_ref[:, b * D:(b + 1) * D] + bmg_ref[:, b * D:(b + 1) * D])
            dgp = dmi_v * ybr * gate * (1.0 - gate)
            dgp_ref[:, b * D:(b + 1) * D] = dgp.astype(dgp_ref.dtype)
            dbmg_ref[:, b * D:(b + 1) * D] += jnp.sum(dgp, axis=0, keepdims=True)
            dy = (dmi_v * gate).astype(MMT)
            dw_ref[b] += _dot_tn(brs[b], dy)
            dbr.append(_dot_nt(dy, w))
        dr, drg, dg, dgg, drgn, dggn = vjp((_heads(dbr[0]), _heads(dbr[1])))
        dr_ref[...] = jnp.concatenate(dr, axis=1)
        dg_ref[...] = jnp.concatenate(dg, axis=1)
        df_ref[...] = dbr[2]
        drg_ref[...] = jnp.concatenate(drg, axis=1).astype(drg_ref.dtype)
        dgg_ref[...] = jnp.concatenate(dgg, axis=1).astype(dgg_ref.dtype)
        drgn_ref[...] += jnp.concatenate(drgn, axis=1)
        dggn_ref[...] += dggn

    row = lambda w, c=0: pl.BlockSpec((t, w), lambda i: (i, c // w))
    cst = lambda shp: pl.BlockSpec(shp, lambda i: (0,) * len(shp))
    return pl.pallas_call(
        body, grid=(s // t,),
        in_specs=[row(BW), row(BW), row(BW), row(BW, RG), row(BW, GG), row(3 * D, GP), cst((1, BW)), cst((1, HD)),
                  cst((1, 3 * D)), _w_br_spec(layer), row(D)],
        out_specs=[row(BW), row(BW), row(BW), row(BW), row(BW), row(3 * D), cst((3, BW, D)), cst((1, BW)), cst((1, HD)),
                   cst((1, 3 * D))],
        out_shape=[_sds((s, BW), F32)] * 3 + [_sds((s, BW), MMT)] * 2 + [_sds((s, 3 * D), MMT), _sds((3, BW, D), F32),
                                                                          _sds((1, BW), F32), _sds((1, HD), F32),
                                                                          _sds((1, 3 * D), F32)],
        compiler_params=_cparams(("arbitrary",)), name=name)(ret_raw, gla_raw, fox_o, zz, zz, zz, ret_g, gla_g, b_mg, w_br, dmi)


CT = 256


def _shift_down(x, k, rows):
    return jnp.where(rows >= k, pltpu.roll(x, k, 0), 0.0)


def _shift_up(x, k, rows, s):
    return jnp.where(rows < s - k, pltpu.roll(x, s - k, 0), 0.0)


def conv_fwd(ug, w_conv, b_conv, *, name):
    s = ug.shape[0]
    nt = DFF // CT

    def body(u_ref, g_ref, w_ref, b_ref, a_ref):
        u = u_ref[...]
        rows = lax.broadcasted_iota(jnp.int32, u.shape, 0)
        uc = b_ref[...] + w_ref[0:1, :] * _shift_down(u, 2, rows) + w_ref[1:2, :] * _shift_down(u, 1, rows) + w_ref[2:3, :] * u
        a_ref[...] = (_silu(uc) * g_ref[...]).astype(a_ref.dtype)

    return pl.pallas_call(
        body, grid=(nt,),
        in_specs=[pl.BlockSpec((s, CT), lambda j: (0, j)), pl.BlockSpec((s, CT), lambda j: (0, nt + j)),
                  pl.BlockSpec((3, CT), lambda j: (0, j)), pl.BlockSpec((1, CT), lambda j: (0, j))],
        out_specs=pl.BlockSpec((s, CT), lambda j: (0, j)), out_shape=_sds((s, DFF), MMT),
        compiler_params=_cparams(("parallel",)), name=name)(ug, ug, w_conv, b_conv)


def conv_bwd(ug, w_conv, b_conv, da, *, name):
    s = ug.shape[0]
    nt = DFF // CT

    def body(u_ref, g_ref, w_ref, b_ref, da_ref, du_ref, dg_ref, dw_ref, db_ref):
        u = u_ref[...]
        rows = lax.broadcasted_iota(jnp.int32, u.shape, 0)
        u2, u1 = _shift_down(u, 2, rows), _shift_down(u, 1, rows)
        uc = b_ref[...] + w_ref[0:1, :] * u2 + w_ref[1:2, :] * u1 + w_ref[2:3, :] * u
        sg = jax.nn.sigmoid(uc)
        da_v = da_ref[...]
        dg_ref[...] = (da_v * uc * sg).astype(dg_ref.dtype)
        duc = da_v * g_ref[...] * sg * (1.0 + uc * (1.0 - sg))
        du = w_ref[2:3, :] * duc + w_ref[1:2, :] * _shift_up(duc, 1, rows, s) + w_ref[0:1, :] * _shift_up(duc, 2, rows, s)
        du_ref[...] = du.astype(du_ref.dtype)
        dw_ref[0:1, :] = jnp.sum(duc * u2, axis=0, keepdims=True)
        dw_ref[1:2, :] = jnp.sum(duc * u1, axis=0, keepdims=True)
        dw_ref[2:3, :] = jnp.sum(duc * u, axis=0, keepdims=True)
        db_ref[...] = jnp.sum(duc, axis=0, keepdims=True)

    col = lambda: pl.BlockSpec((s, CT), lambda j: (0, j))
    return pl.pallas_call(
        body, grid=(nt,),
        in_specs=[col(), pl.BlockSpec((s, CT), lambda j: (0, nt + j)), pl.BlockSpec((3, CT), lambda j: (0, j)),
                  pl.BlockSpec((1, CT), lambda j: (0, j)), col()],
        out_specs=[col(), col(), pl.BlockSpec((3, CT), lambda j: (0, j)), pl.BlockSpec((1, CT), lambda j: (0, j))],
        out_shape=[_sds((s, DFF), MMT), _sds((s, DFF), MMT), _sds((3, DFF), F32), _sds((1, DFF), F32)],
        compiler_params=_cparams(("parallel",)), name=name)(ug, ug, w_conv, b_conv, da)


def _tiles(s):
    return min(1024, s)


def layer_fwd(x, mod, p, cosf, sinf):
    s = x.shape[0]
    tm = _tiles(s)
    l = p["l"]
    shift1, scale1, gate1, shift2, scale2, gate2 = mod
    h = norm_mod(x, p["norm1_g"], scale1, shift1, name="norm_mod")
    zz = mm_nn(h, p["w1"], tm=tm, tn=768, out_dtype=F32, name="mm_w1", layer=l)
    ret_raw, rprev = retention_fwd(zz, cosf, sinf, name="ret_fwd")
    gla_raw, sprev = gla_fwd(zz, p["w_a2p"], p["b_gla_a"], name="gla_fwd")
    qn, kn, cum = fox_pre(zz, p["q_norm_g"], p["k_norm_g"], p["b_foxp"], name="fox_pre")
    bq, bk = _fox_blocks(s)
    cum_t = cum[:, :NH].T
    cum_col, cum_row = cum_t[:, :, None], cum_t.reshape(NH, s // bk, bk)
    fox_o, lse = fox_attn_fwd(qn, kn, zz, cum_col, cum_row, name="fox_fwd")
    mi = mix_fwd(ret_raw, gla_raw, fox_o, zz, p["ret_norm_g"], p["gla_norm_g"], p["b_mg"], p["w_br"], name="mix_fwd",
                 layer=l)
    x1, mixed = mm_nn_residual(mi, p["w_o"], x, gate1, tm=tm, tn=512, name="mm_wo", layer=l)
    h2 = norm_mod(x1, p["norm2_g"], scale2, shift2, name="norm_mod")
    ug = mm_nn(h2, p["w_up"], tm=tm, tn=512, out_dtype=F32, name="mm_wup", layer=l)
    a = conv_fwd(ug, p["w_conv"], p["b_conv"], name="conv_fwd")
    x2, y = mm_nn_residual(a, p["w_down"], x1, gate2, tm=tm, tn=512, name="mm_wdown", layer=l)
    saved = dict(x=x, h=h, zz=zz, ret_raw=ret_raw, rprev=rprev, gla_raw=gla_raw, sprev=sprev, qn=qn, kn=kn,
                 cum_col=cum_col, cum_row=cum_row, fox_o=fox_o, lse=lse, mi=mi, mixed=mixed, x1=x1, h2=h2, ug=ug, a=a, y=y)
    return x2, saved


def layer_bwd(dx2, mod, p, sv, cosf, sinf, stacks):
    s = dx2.shape[0]
    tm = _tiles(s)
    l = p["l"]
    shift1, scale1, gate1, shift2, scale2, gate2 = mod
    g, stacks = {}, dict(stacks)
    dy, dgate2 = gate_bwd(dx2, sv["y"], gate2, name="gate_bwd")
    stacks["w_down"] = mm_tn(sv["a"], dy, tm=min(1408, DFF), tn=512, out_dtype=MMT, name="mm_dwdown",
                             stack=stacks["w_down"], layer=l)
    da = mm_nt(dy, p["w_down"], tm=tm, tn=1408, out_dtype=F32, name="mm_da", layer=l)
    du, dg, g["w_conv"], g["b_conv"] = conv_bwd(sv["ug"], p["w_conv"], p["b_conv"], da, name="conv_bwd")
    dug = jnp.concatenate([du, dg], axis=1)
    stacks["w_up"] = mm_tn(sv["h2"], dug, tm=512, tn=512, out_dtype=MMT, name="mm_dwup", stack=stacks["w_up"], layer=l)
    dh2 = mm_nt(dug, p["w_up"], tm=min(512, s), tn=512, out_dtype=F32, name="mm_dh2", layer=l)
    dx1, g["norm2_g"], dscale2, dshift2 = norm_mod_bwd(sv["x1"], dh2, dx2, p["norm2_g"], scale2, shift2, name="norm_mod_bwd")
    dmixed, dgate1 = gate_bwd(dx1, sv["mixed"], gate1, name="gate_bwd")
    stacks["w_o"] = mm_tn(sv["mi"], dmixed, tm=512, tn=512, out_dtype=MMT, name="mm_dwo", stack=stacks["w_o"], layer=l)
    dmi = mm_nt(dmixed, p["w_o"], tm=tm, tn=512, out_dtype=MMT, name="mm_dmi", layer=l)
    zz = sv["zz"]
    (dret, dgla, dfox, drg, dgg, dgp, g["w_br"], g["ret_norm_g"], g["gla_norm_g"], g["b_mg"]) = mix_bwd(
        sv["ret_raw"], sv["gla_raw"], sv["fox_o"], zz, p["ret_norm_g"], p["gla_norm_g"], p["b_mg"], p["w_br"], dmi,
        name="mix_bwd", layer=l)
    dqn, dkn, dfv, dcum_row = fox_attn_bwd(sv["qn"], sv["kn"], zz, sv["cum_col"], sv["cum_row"], sv["lse"], dfox,
                                           name="fox_bwd")
    dfv = dfv.astype(MMT)
    dcum = jnp.pad(dcum_row.reshape(NH, s).T, ((0, 0), (0, HD - NH)))
    dfqk, dff, g["q_norm_g"], g["k_norm_g"], g["b_foxp"] = fox_pre_bwd(
        zz, p["q_norm_g"], p["k_norm_g"], p["b_foxp"], dqn, dkn, dcum, name="fox_pre_bwd")
    dgqkv, dlr, g["w_a2p"], g["b_gla_a"] = gla_bwd(zz, p["w_a2p"], p["b_gla_a"], sv["sprev"], dgla, name="gla_bwd")
    drqkv = retention_bwd(zz, cosf, sinf, sv["rprev"], dret, name="ret_bwd")
    dzz = jnp.concatenate([dgp, drqkv, drg, dgqkv, dgg, dfqk, dfv, dlr, dff], axis=1)
    stacks["w_mg"] = mm_tn(sv["h"], dzz, tm=512, tn=768, out_dtype=MMT, name="mm_dwmg", ncols=RQ, stack=stacks["w_mg"],
                           layer=l)
    dwz = mm_tn(sv["h"], dzz, tm=512, tn=768, out_dtype=MMT, name="mm_dwz", col0=RQ)
    stacks["w_in"] = unalign_dw_in(dwz, stacks["w_in"], l)
    dh = mm_nt(dzz, p["w1"], tm=min(512, s), tn=512, out_dtype=F32, name="mm_dh", layer=l)
    dx, g["norm1_g"], dscale1, dshift1 = norm_mod_bwd(sv["x"], dh, dx1, p["norm1_g"], scale1, shift1, name="norm_mod_bwd")
    dmod = jnp.concatenate([dshift1, dscale1, dgate1, dshift2, dscale2, dgate2], axis=1)
    return dx, g, dmod, stacks


def _align_cols(w_in, w_mg):
    z = lambda n: jnp.zeros((w_in.shape[0], n), w_in.dtype)
    return jnp.concatenate([w_mg, w_in[:, :3072], w_in[:, 3088:5136], w_in[:, 3072:3088], z(HD - GLR),
                            w_in[:, 5136:5140], z(HD - NH)], axis=1)


def _unalign_cols(dwz):
    o = lambda c: c - RQ
    return jnp.concatenate([dwz[:, :o(GG)], dwz[:, o(LR):o(LR) + GLR], dwz[:, o(GG):o(LR)], dwz[:, o(FF):o(FF) + NH]], axis=1)


def build_w1(w_in_sh, w_mg):
    nl = w_mg.shape[0]
    t = _row_tile(D)

    def body(s_ref, g_ref, o_ref):
        o_ref[...] = _align_cols(jnp.concatenate([s_ref[k] for k in range(4)], axis=1), g_ref[...])

    return pl.pallas_call(
        body, grid=(nl, D // t),
        in_specs=[pl.BlockSpec((None, 4, t, IN_W // 4), lambda l, i: (l, 0, i, 0)), pl.BlockSpec((None, t, RQ), lambda l, i: (l, i, 0))],
        out_specs=pl.BlockSpec((None, t, NZZ), lambda l, i: (l, i, 0)), out_shape=_sds((nl, D, NZZ), w_mg.dtype),
        compiler_params=_cparams(("parallel", "parallel")), name="build_w1")(w_in_sh, w_mg)


def unalign_dw_in(dwz, stack, layer):
    t = _row_tile(D)

    def body(z_ref, s_ref, o_ref):
        w = _unalign_cols(z_ref[...])
        for k in range(4):
            o_ref[k] = w[:, k * (IN_W // 4):(k + 1) * (IN_W // 4)]

    return pl.pallas_call(
        body, grid=(D // t,),
        in_specs=[pl.BlockSpec((t, NZZ - RQ), lambda i: (i, 0)), pl.BlockSpec(memory_space=pl.ANY)],
        out_specs=pl.BlockSpec((None, 4, t, IN_W // 4), lambda i: (layer, 0, i, 0)), out_shape=_sds(stack.shape, stack.dtype),
        input_output_aliases={1: 0}, compiler_params=_cparams(("parallel",)), name="unalign_dw_in")(dwz, stack)


def layer_params(w, l):
    row = lambda v: v[l][None, :]
    return dict(
        l=l, norm1_g=row(w["norm1_g"]), norm2_g=row(w["norm2_g"]), w1=w["w1"],
        w_a2p=jnp.pad(w["w_gla_a2"][l], ((0, HD - GLR), (0, 0))), b_gla_a=row(w["b_gla_a"]),
        b_foxp=jnp.pad(row(w["b_fox_f"]), ((0, 0), (0, HD - NH))), ret_norm_g=row(w["ret_norm_g"]),
        gla_norm_g=row(w["gla_norm_g"]), q_norm_g=row(w["q_norm_g"]), k_norm_g=row(w["k_norm_g"]),
        w_br=w["w_br"], b_mg=row(w["b_mg"]), w_o=w["w_o"], w_up=w["w_up"], w_conv=w["w_conv"][l],
        b_conv=row(w["b_conv"]), w_down=w["w_down"])


def layer_grads(g):
    vec = lambda v: v[0]
    return dict(
        norm1_g=vec(g["norm1_g"]), norm2_g=vec(g["norm2_g"]), w_gla_a2=g["w_a2p"][:GLR], b_gla_a=vec(g["b_gla_a"]),
        b_fox_f=g["b_foxp"][0, :NH], ret_norm_g=vec(g["ret_norm_g"]), gla_norm_g=vec(g["gla_norm_g"]),
        q_norm_g=vec(g["q_norm_g"]), k_norm_g=vec(g["k_norm_g"]), w_br=g["w_br"], b_mg=vec(g["b_mg"]),
        w_conv=g["w_conv"], b_conv=vec(g["b_conv"]))


def ada_mod(c_all, w_ada, b_ada):
    nl, _, n = w_ada.shape

    def body(c_ref, w_ref, b_ref, o_ref):
        o_ref[...] = _dot(_silu(c_ref[...]), w_ref[...], HI) + b_ref[...]

    return pl.pallas_call(
        body, grid=(nl,),
        in_specs=[pl.BlockSpec((8, D), lambda l: (0, 0)), pl.BlockSpec((None, D, n), lambda l: (l, 0, 0)),
                  pl.BlockSpec((None, 1, n), lambda l: (l, 0, 0))],
        out_specs=pl.BlockSpec((None, 8, n), lambda l: (l, 0, 0)), out_shape=_sds((nl, 8, n), F32),
        compiler_params=_cparams(("parallel",)), name="ada_mod")(c_all, w_ada, b_ada)


def ada_dw(c_all, dmod):
    nl, _, n = dmod.shape

    def body(c_ref, d_ref, o_ref):
        o_ref[...] = _dot_tn(_silu(c_ref[...]), d_ref[...], HI)

    return pl.pallas_call(
        body, grid=(nl,),
        in_specs=[pl.BlockSpec((8, D), lambda l: (0, 0)), pl.BlockSpec((None, 8, n), lambda l: (l, 0, 0))],
        out_specs=pl.BlockSpec((None, D, n), lambda l: (l, 0, 0)), out_shape=_sds((nl, D, n), F32),
        compiler_params=_cparams(("parallel",)), name="ada_dw")(c_all, dmod)


def sum_devices(g):
    def body(g_ref, o_ref):
        acc = g_ref[0]
        for d in range(1, 8):
            acc = acc + g_ref[d]
        o_ref[...] = acc

    return pl.pallas_call(body, out_shape=_sds(g.shape[1:], F32), name="sum_devices")(g)


def adamw(w, g, m, v, *, block, name):
    nd = w.ndim
    grid = tuple(w.shape[i] // block[i] for i in range(nd))
    bc1 = 1.0 - ADAM_B1 ** ADAM_STEP
    bc2 = 1.0 - ADAM_B2 ** ADAM_STEP

    def body(w_ref, g_ref, m_ref, v_ref, d_ref, nm_ref, nv_ref):
        gv = g_ref[...]
        nm = ADAM_B1 * m_ref[...] + (1.0 - ADAM_B1) * gv
        nv = ADAM_B2 * v_ref[...] + (1.0 - ADAM_B2) * (gv * gv)
        nm_ref[...] = nm
        nv_ref[...] = nv
        d_ref[...] = -ADAM_LR * ((nm / bc1) / (jnp.sqrt(nv / bc2) + ADAM_EPS) + ADAM_WD * w_ref[...])

    spec = pl.BlockSpec(tuple(block), lambda *i: i)
    return pl.pallas_call(
        body, grid=grid, in_specs=[spec] * 4, out_specs=[spec] * 3, out_shape=[_sds(w.shape, F32)] * 3,
        compiler_params=_cparams(("parallel",) * nd), name=name)(w, g, m, v)


MESH = pl.DeviceIdType.MESH
ANY = pl.BlockSpec(memory_space=pl.ANY)
VM = pl.BlockSpec(memory_space=pltpu.VMEM)


def _place():
    x, y, c = lax.axis_index("x"), lax.axis_index("y"), lax.axis_index("c")
    return x, y, c, [(1 - x, y), (x, 1 - y), (1 - x, 1 - y)]


def small_allgather(v, *, name):
    m_per, n = v.shape

    def body(x_ref, out_ref, send_sems, recv_sems, local_sem):
        x, y, c, chips = _place()
        me, sibling = (x, y, c), (x, y, 1 - c)

        def rows(px, py, pc):
            return out_ref.at[pl.ds((4 * px + 2 * py + pc) * m_per, m_per), :]

        def copy(k, block, to, src=None):
            return pltpu.make_async_remote_copy(
                src_ref=rows(*block) if src is None else src, dst_ref=rows(*block),
                send_sem=send_sems.at[k], recv_sem=recv_sems.at[k], device_id=to, device_id_type=MESH)

        mine = pltpu.make_async_copy(x_ref, rows(*me), local_sem)
        mine.start()
        first = [copy(0, me, sibling, src=x_ref)]
        first += [copy(1 + j, me, (*chip, c), src=x_ref) for j, chip in enumerate(chips)]
        for cp in first:
            cp.start()
        passed = [copy(4 + j, (*chip, c), sibling) for j, chip in enumerate(chips)]
        for j, chip in enumerate(chips):
            copy(1 + j, (*chip, c), me).wait_recv()
            passed[j].start()
        copy(0, sibling, me).wait_recv()
        for j, chip in enumerate(chips):
            copy(4 + j, (*chip, 1 - c), me).wait_recv()
        for cp in first + passed:
            cp.wait_send()
        mine.wait()

    return pl.pallas_call(
        body, out_shape=_sds((8 * m_per, n), v.dtype), in_specs=[VM], out_specs=VM,
        scratch_shapes=[pltpu.SemaphoreType.DMA((7,)), pltpu.SemaphoreType.DMA((7,)), pltpu.SemaphoreType.DMA],
        name=name)(v)


TENSORS = {
    "w_in": ("lead", None, (4, D, 1285), (1, D, 1285)),
    "w_mg": ("col", 768, (D, 3072), (512, 3072)),
    "w_br": ("col", 256, (3, BW, D), (3, BW, D)),
    "w_o": ("row", 256, (D, D), (D, D)),
    "w_up": ("col", 1408, (D, 5632), (256, 5632)),
    "w_down": ("row", 704, (DFF, D), (704, D)),
}
BIG = tuple(TENSORS)


def _shard_shape(name):
    kind, width, full, _ = TENSORS[name]
    if kind == "lead":
        return full[1:]
    return full[:-1] + (width,) if kind == "col" else (width,) + full[1:]


def _shard_view(ref, layers, name, k):
    kind, width, full, _ = TENSORS[name]
    if kind == "lead":
        return ref.at[layers, k]
    if kind == "row":
        return ref.at[layers, pl.ds(k * width, width)]
    return ref.at[(layers,) + (slice(None),) * (len(full) - 1) + (pl.ds(k * width, width),)]


def _remote(send_sems, recv_sems, k, src, dst, to):
    return pltpu.make_async_remote_copy(src_ref=src, dst_ref=dst, send_sem=send_sems.at[k], recv_sem=recv_sems.at[k],
                                        device_id=to, device_id_type=MESH)


def _dma_sems(n):
    return [pltpu.SemaphoreType.DMA((n,)), pltpu.SemaphoreType.DMA((n,))]


def gather_weights(shards, *, name):
    nl = shards[BIG[0]].shape[0]
    hl = nl // 2
    nt = len(BIG)
    per = 6 * hl + nl

    def body(*refs):
        p, o = dict(zip(BIG, refs[:nt])), dict(zip(BIG, refs[nt:2 * nt]))
        send_sems, recv_sems = refs[2 * nt:]
        x, y, c, chips = _place()
        mine, me, sibling = 2 * x + y, (x, y, c), (x, y, 1 - c)
        remote = functools.partial(_remote, send_sems, recv_sems)
        started = []

        def go(cp):
            cp.start()
            started.append(cp)

        for t, n in enumerate(BIG):
            for l in range(nl):
                go(remote(t * per + 6 * hl + l, p[n].at[l], _shard_view(o[n], l, n, mine), sibling))
        for i in range(hl):
            for t, n in enumerate(BIG):
                for j, chip in enumerate(chips):
                    layer = hl * c + i
                    go(remote(t * per + j * hl + i, p[n].at[layer], _shard_view(o[n], layer, n, mine), (*chip, c)))
        for i in range(hl):
            for t, n in enumerate(BIG):
                for j, chip in enumerate(chips):
                    landed = _shard_view(o[n], hl * c + i, n, 2 * chip[0] + chip[1])
                    remote(t * per + j * hl + i, landed, landed, me).wait_recv()
                    go(remote(t * per + 3 * hl + j * hl + i, landed, landed, sibling))
        for i in range(hl):
            for t, n in enumerate(BIG):
                for j, chip in enumerate(chips):
                    landed = _shard_view(o[n], hl * (1 - c) + i, n, 2 * chip[0] + chip[1])
                    remote(t * per + 3 * hl + j * hl + i, landed, landed, me).wait_recv()
        for t, n in enumerate(BIG):
            for l in range(nl):
                landed = _shard_view(o[n], l, n, mine)
                remote(t * per + 6 * hl + l, landed, landed, me).wait_recv()
        for cp in started:
            cp.wait_send()

    outs = pl.pallas_call(
        body, out_shape=[_sds((nl,) + TENSORS[n][2], shards[n].dtype) for n in BIG], in_specs=[ANY] * nt,
        out_specs=[ANY] * nt, scratch_shapes=_dma_sems(nt * per), name=name)(*[shards[n] for n in BIG])
    return dict(zip(BIG, outs))


def pair_exchange(g, *, name):
    nl = g[BIG[0]].shape[0]
    hl = nl // 2
    nt = len(BIG)

    def body(*refs):
        send_sems, recv_sems = refs[2 * nt:]
        x, y, c, _ = _place()
        copies = [_remote(send_sems, recv_sems, t, refs[t].at[pl.ds(hl * (1 - c), hl)], refs[nt + t], (x, y, 1 - c))
                  for t in range(nt)]
        for cp in copies:
            cp.start()
        for cp in copies:
            cp.wait()

    outs = pl.pallas_call(
        body, out_shape=[_sds((hl,) + g[n].shape[1:], g[n].dtype) for n in BIG], in_specs=[ANY] * nt, out_specs=[ANY] * nt,
        scratch_shapes=_dma_sems(nt), name=name)(*[g[n] for n in BIG])
    return dict(zip(BIG, outs))


def chip_exchange(s, *, name):
    hl = s[BIG[0]].shape[0]
    nt = len(BIG)

    def body(*refs):
        send_sems, recv_sems = refs[2 * nt:]
        x, y, c, chips = _place()
        copies = [_remote(send_sems, recv_sems, 3 * t + j, _shard_view(refs[t], pl.ds(0, hl), n, 2 * chip[0] + chip[1]),
                          refs[nt + t].at[j], (*chip, c))
                  for t, n in enumerate(BIG) for j, chip in enumerate(chips)]
        for cp in copies:
            cp.start()
        for cp in copies:
            cp.wait()

    outs = pl.pallas_call(
        body, out_shape=[_sds((3, hl) + _shard_shape(n), s[n].dtype) for n in BIG], in_specs=[ANY] * nt,
        out_specs=[ANY] * nt, scratch_shapes=_dma_sems(3 * nt), name=name)(*[s[n] for n in BIG])
    return dict(zip(BIG, outs))


def pair_share(f, *, name):
    nl = f[BIG[0]].shape[0]
    hl = nl // 2
    nt = len(BIG)

    def body(*refs):
        o = refs[nt:2 * nt]
        send_sems, recv_sems = refs[2 * nt:]
        x, y, c, _ = _place()
        copies = [_remote(send_sems, recv_sems, t, o[t].at[pl.ds(hl * c, hl)], o[t].at[pl.ds(hl * c, hl)], (x, y, 1 - c))
                  for t in range(nt)]
        for cp in copies:
            cp.start()
        for t, cp in enumerate(copies):
            cp.wait_send()
            theirs = o[t].at[pl.ds(hl * (1 - c), hl)]
            _remote(send_sems, recv_sems, t, theirs, theirs, (x, y, c)).wait_recv()

    outs = pl.pallas_call(
        body, out_shape=[_sds(f[n].shape, f[n].dtype) for n in BIG], in_specs=[ANY] * nt, out_specs=[ANY] * nt,
        input_output_aliases={t: t for t in range(nt)}, scratch_shapes=_dma_sems(nt), name=name)(*[f[n] for n in BIG])
    return dict(zip(BIG, outs))


def pair_add(g, r, idx, *, tensor, name):
    _, _, full, blk = TENSORS[tensor]
    hl = r.shape[0]

    def body(idx_ref, g_ref, r_ref, o_ref):
        o_ref[...] = (g_ref[...].astype(F32) + r_ref[...].astype(F32)).astype(o_ref.dtype)

    own = pl.BlockSpec((None,) + blk, lambda *a: (a[0],) + a[1:-1])
    return pl.pallas_call(
        body, out_shape=_sds(r.shape, r.dtype),
        grid_spec=pltpu.PrefetchScalarGridSpec(
            num_scalar_prefetch=1, grid=(hl,) + tuple(f // b for f, b in zip(full, blk)),
            in_specs=[pl.BlockSpec((None,) + blk, lambda *a: (hl * a[-1][0] + a[0],) + a[1:-1]), own], out_specs=own),
        compiler_params=_cparams(("parallel",) * (1 + len(full))), name=name)(idx, g, r)


def chip_add(s, r, idx, *, tensor, name):
    kind, width, full, _ = TENSORS[tensor]
    shard = _shard_shape(tensor)
    hl = s.shape[0]
    zeros = (0,) * len(shard)

    def body(idx_ref, s_ref, r0_ref, r1_ref, r2_ref, o_ref):
        o_ref[...] = ((s_ref[...].astype(F32) + r0_ref[...].astype(F32)) + r1_ref[...].astype(F32)) + r2_ref[...].astype(F32)

    if kind == "lead":
        mine = pl.BlockSpec((None, None) + shard, lambda i, ix: (i, ix[1]) + zeros)
    elif kind == "row":
        mine = pl.BlockSpec((None,) + shard, lambda i, ix: (i, ix[1]) + zeros[1:])
    else:
        mine = pl.BlockSpec((None,) + shard, lambda i, ix: (i,) + zeros[1:] + (ix[1],))
    peer = lambda j: pl.BlockSpec((None, None) + shard, lambda i, ix: (j, i) + zeros)
    return pl.pallas_call(
        body, out_shape=_sds((2 * hl,) + shard, F32),
        grid_spec=pltpu.PrefetchScalarGridSpec(
            num_scalar_prefetch=1, grid=(hl,), in_specs=[mine, peer(0), peer(1), peer(2)],
            out_specs=pl.BlockSpec((None,) + shard, lambda i, ix: (hl * ix[0] + i,) + zeros)),
        compiler_params=_cparams(("parallel",)), name=name)(idx, s, r, r, r)


def _flat_rows(arrs):
    v = jnp.concatenate([a.reshape(-1) for a in arrs])
    n = -(-v.shape[0] // 1024) * 1024
    return jnp.pad(v, (0, n - v.shape[0])).reshape(n // HD, HD)


def _unflat(buf, shapes):
    v, out, o = buf.reshape(-1), [], 0
    for s in shapes:
        n = int(np.prod(s))
        out.append(v[o:o + n].reshape(s))
        o += n
    return out


WEIGHTS = ("norm1_g", "norm2_g", "w_ada", "b_ada", "w_in", "w_gla_a2", "b_gla_a", "b_fox_f", "ret_norm_g", "gla_norm_g",
           "q_norm_g", "k_norm_g", "w_br", "w_mg", "b_mg", "w_o", "w_up", "w_conv", "b_conv", "w_down")
REPLICATED = ("norm1_g", "norm2_g", "b_gla_a", "b_fox_f", "ret_norm_g", "gla_norm_g", "q_norm_g", "k_norm_g", "b_mg", "b_conv")
ADAM_BLOCKS = dict(w_ada=(1, 256, 1536), w_in=(1, 256, 1285), w_br=(1, 3, BW, 256), w_mg=(1, 512, 768), w_o=(2, 256, D),
                   w_up=(1, 256, 1408), w_down=(1, 352, D))
ALL_AXES = ("x", "y", "c")


def kernel(x, c, norm1_g, norm2_g, w_ada, b_ada, w_in, w_gla_a2, b_gla_a, b_fox_f, ret_norm_g, gla_norm_g, q_norm_g, k_norm_g, w_br, w_mg, b_mg, w_o, w_up, w_conv, b_conv, w_down, loss_target, m_norm1_g, m_norm2_g, m_w_ada, m_b_ada, m_w_in, m_w_gla_a2, m_b_gla_a, m_b_fox_f, m_ret_norm_g, m_gla_norm_g, m_q_norm_g, m_k_norm_g, m_w_br, m_w_mg, m_b_mg, m_w_o, m_w_up, m_w_conv, m_b_conv, m_w_down, v_norm1_g, v_norm2_g, v_w_ada, v_b_ada, v_w_in, v_w_gla_a2, v_b_gla_a, v_b_fox_f, v_ret_norm_g, v_gla_norm_g, v_q_norm_g, v_k_norm_g, v_w_br, v_w_mg, v_b_mg, v_w_o, v_w_up, v_w_conv, v_b_conv, v_w_down):
    w = dict(zip(WEIGHTS, (norm1_g, norm2_g, w_ada, b_ada, w_in, w_gla_a2, b_gla_a, b_fox_f, ret_norm_g, gla_norm_g,
                           q_norm_g, k_norm_g, w_br, w_mg, b_mg, w_o, w_up, w_conv, b_conv, w_down)))
    m = dict(zip(WEIGHTS, (m_norm1_g, m_norm2_g, m_w_ada, m_b_ada, m_w_in, m_w_gla_a2, m_b_gla_a, m_b_fox_f, m_ret_norm_g,
                           m_gla_norm_g, m_q_norm_g, m_k_norm_g, m_w_br, m_w_mg, m_b_mg, m_w_o, m_w_up, m_w_conv, m_b_conv,
                           m_w_down)))
    v = dict(zip(WEIGHTS, (v_norm1_g, v_norm2_g, v_w_ada, v_b_ada, v_w_in, v_w_gla_a2, v_b_gla_a, v_b_fox_f, v_ret_norm_g,
                           v_gla_norm_g, v_q_norm_g, v_k_norm_g, v_w_br, v_w_mg, v_b_mg, v_w_o, v_w_up, v_w_conv, v_b_conv,
                           v_w_down)))
    nl = norm1_g.shape[0]
    seq = x.shape[1]
    xi, yi, ci = lax.axis_index("x"), lax.axis_index("y"), lax.axis_index("c")
    k_me = 2 * xi + yi
    b_me = 4 * xi + 2 * yi + ci
    ada_n = w_ada.shape[2]
    a2_n, conv_n = w_gla_a2.shape[2], w_conv.shape[2]

    blk = _flat_rows([c, w_gla_a2, w_conv])
    g1 = small_allgather(blk, name="gather_small").reshape(8, blk.shape[0], HD)
    c_all = g1[:, :D // HD].reshape(8, D)
    by_chip = g1[0::2].reshape(4, -1)[:, D:]
    a2_sh, conv_sh = by_chip[:, :nl * GLR * a2_n], by_chip[:, nl * GLR * a2_n:nl * (GLR * a2_n + 3 * conv_n)]
    full_small = dict(
        w_gla_a2=a2_sh.reshape(4, nl, GLR, a2_n).transpose(1, 2, 0, 3).reshape(nl, GLR, 4 * a2_n),
        w_conv=conv_sh.reshape(4, nl, 3, conv_n).transpose(1, 2, 0, 3).reshape(nl, 3, 4 * conv_n))

    b_ada_sh = lax.dynamic_slice_in_dim(b_ada, k_me * ada_n, ada_n, axis=1)[:, None, :]
    mod_sh = ada_mod(c_all, w_ada, b_ada_sh)
    g2 = small_allgather(mod_sh.reshape(nl * 8, ada_n), name="gather_mod").reshape(4, 2, nl, 8, ada_n)[:, 0]
    mod_me = lax.dynamic_index_in_dim(g2, b_me, axis=2, keepdims=False).transpose(1, 0, 2).reshape(nl, 4 * ada_n)

    wfull = {n: w[n] for n in REPLICATED}
    wfull.update(full_small)
    wfull.update(gather_weights({n: w[n].astype(MMT) for n in BIG}, name="gather_weights"))
    wfull["w1"] = build_w1(wfull["w_in"], wfull["w_mg"])
    params = [layer_params(wfull, l) for l in range(nl)]
    mods = [[mod_me[l:l + 1, i * D:(i + 1) * D] for i in range(6)] for l in range(nl)]

    cosf, sinf = _rope_tables(seq)
    xs, saved = x[0], []
    for l in range(nl):
        xs, sv = layer_fwd(xs, mods[l], params[l], cosf, sinf)
        saved.append(sv)
    loss_part, dx = loss_and_grad(xs, loss_target[0], name="loss")
    loss = lax.psum(loss_part[0, 0], ALL_AXES)
    grads, dmods = [None] * nl, [None] * nl
    stacks = {n: lax.empty((nl,) + TENSORS[n][2], MMT) for n in BIG if n != "w_br"}
    for l in reversed(range(nl)):
        dx, g, dmods[l], stacks = layer_bwd(dx, mods[l], params[l], saved[l], cosf, sinf, stacks)
        grads[l] = layer_grads(g)

    small_names = REPLICATED + ("w_gla_a2", "w_conv")
    small_shapes = [(nl, 6 * D)] + [(nl,) + grads[0][n].shape for n in small_names]
    vec = _flat_rows([jnp.concatenate(dmods, axis=0)] + [jnp.stack([grads[l][n] for l in range(nl)]) for n in small_names])
    gs = small_allgather(vec, name="gather_small_grads").reshape(8, vec.shape[0], HD)
    summed = _unflat(sum_devices(gs), small_shapes)
    grad = dict(zip(small_names, summed[1:]))
    grad["b_ada"] = summed[0]
    grad["w_gla_a2"] = lax.dynamic_slice_in_dim(grad["w_gla_a2"], k_me * a2_n, a2_n, axis=2)
    grad["w_conv"] = lax.dynamic_slice_in_dim(grad["w_conv"], k_me * conv_n, conv_n, axis=2)
    dmod_all = gs[:, :nl * 6 * D // HD].reshape(8, nl, 6 * D)
    dmod_sh = lax.dynamic_slice_in_dim(dmod_all, k_me * ada_n, ada_n, axis=2).transpose(1, 0, 2)
    grad["w_ada"] = ada_dw(c_all, dmod_sh)

    partial = dict(stacks)
    partial["w_br"] = jnp.stack([grads[l]["w_br"].astype(MMT) for l in range(nl)])
    idx = jnp.stack([ci, k_me]).astype(jnp.int32)
    from_sibling = pair_exchange(partial, name="rs_pair_exchange")
    chip_sum = {n: pair_add(partial[n], from_sibling[n], idx, tensor=n, name="rs_pair_add_" + n) for n in BIG}
    from_chips = chip_exchange(chip_sum, name="rs_chip_exchange")
    mine = {n: chip_add(chip_sum[n], from_chips[n], idx, tensor=n, name="rs_chip_add_" + n) for n in BIG}
    grad.update(pair_share(mine, name="rs_pair_share"))

    delta, new_m, new_v = {}, {}, {}
    for n, block in ADAM_BLOCKS.items():
        delta[n], new_m[n], new_v[n] = adamw(w[n], grad[n], m[n], v[n], block=block, name="adamw_" + n)
    rest = [n for n in WEIGHTS if n not in ADAM_BLOCKS]
    shapes = [w[n].shape for n in rest]
    flat = [_flat_rows([t[n] for n in rest]) for t in (w, grad, m, v)]
    outs = adamw(*flat, block=flat[0].shape, name="adamw_small")
    for t, o in zip((delta, new_m, new_v), outs):
        t.update(zip(rest, _unflat(o, shapes)))

    return (loss, dx[None], *[grad[n] for n in WEIGHTS], *[delta[n] for n in WEIGHTS], *[new_m[n] for n in WEIGHTS],
            *[new_v[n] for n in WEIGHTS])
```

```python
import functools

import numpy as np
import jax
import jax.numpy as jnp
from jax import lax
from jax.experimental import pallas as pl
from jax.experimental.pallas import tpu as pltpu

F32 = jnp.float32
MMT = jnp.bfloat16
HI = lax.Precision.HIGHEST

D = 1024
DEPTH = 4
NH = 4
HD = 128
BW = NH * HD
CH = 64
GDK = 64
GLR = 16
DFF = 2816
EPS = 1e-6
ROPE_BASE = 10000.0

GP, RQ, RK, RV, RG, GQ, GK, GV, GG, FQ, FK, FV, LR, FF = (
    0, 3072, 3584, 4096, 4608, 5120, 5376, 5632, 6144, 6656, 7168, 7680, 8192, 8320)
NZZ = 8448
IN_W = 5140

VMEM_LIMIT = 56 * 1024 * 1024

ADAM_LR, ADAM_B1, ADAM_B2, ADAM_EPS, ADAM_WD, ADAM_STEP = 0.001, 0.9, 0.999, 1e-08, 0.01, 10


def _cparams(sem=None):
    return pltpu.CompilerParams(dimension_semantics=sem, vmem_limit_bytes=VMEM_LIMIT)


def _sds(shape, dtype):
    return jax.ShapeDtypeStruct(tuple(shape), dtype)


def _dot(a, b, precision=None):
    return lax.dot_general(a, b, (((1,), (0,)), ((), ())), precision=precision, preferred_element_type=F32)


def _dot_nt(a, b, precision=None):
    return lax.dot_general(a, b, (((1,), (1,)), ((), ())), precision=precision, preferred_element_type=F32)


def _dot_tn(a, b, precision=None):
    return lax.dot_general(a, b, (((0,), (0,)), ((), ())), precision=precision, preferred_element_type=F32)


def _silu(x):
    return x * jax.nn.sigmoid(x)


def _log_sigmoid(x):
    return jnp.minimum(x, 0.0) - jnp.log(1.0 + jnp.exp(jnp.minimum(x, -x)))


@jax.custom_vjp
def _swap_halves(x):
    return pltpu.roll(x, HD // 2, 1)


_swap_halves.defvjp(lambda x: (_swap_halves(x), None), lambda _, g: (_swap_halves(g),))


@jax.custom_vjp
def _bdot(a, b):
    return _dot(a.astype(MMT), b.astype(MMT))


@jax.custom_vjp
def _bdot_nt(a, b):
    return _dot_nt(a.astype(MMT), b.astype(MMT))


@jax.custom_vjp
def _bdot_tn(a, b):
    return _dot_tn(a.astype(MMT), b.astype(MMT))


_bdot.defvjp(lambda a, b: (_bdot(a, b), (a, b)), lambda r, g: (_bdot_nt(g, r[1]), _bdot_tn(r[0], g)))
_bdot_nt.defvjp(lambda a, b: (_bdot_nt(a, b), (a, b)), lambda r, g: (_bdot(g, r[1]), _bdot_tn(g, r[0])))
_bdot_tn.defvjp(lambda a, b: (_bdot_tn(a, b), (a, b)), lambda r, g: (_bdot_nt(r[1], g), _bdot(r[0], g)))


def _stacked(blk, idx, layer):
    if layer is None:
        return pl.BlockSpec(blk, idx)
    return pl.BlockSpec((None,) + blk, lambda i, j: (layer,) + idx(i, j))


def mm_nn(a, b, *, tm, tn, out_dtype, name, layer=None):
    m, k = a.shape
    n = b.shape[-1]

    def body(a_ref, b_ref, o_ref):
        o_ref[...] = _dot(a_ref[...], b_ref[...]).astype(o_ref.dtype)

    return pl.pallas_call(
        body, grid=(m // tm, n // tn),
        in_specs=[pl.BlockSpec((tm, k), lambda i, j: (i, 0)), _stacked((k, tn), lambda i, j: (0, j), layer)],
        out_specs=pl.BlockSpec((tm, tn), lambda i, j: (i, j)),
        out_shape=_sds((m, n), out_dtype), compiler_params=_cparams(("parallel", "parallel")), name=name)(a, b)


def mm_nn_residual(a, b, res, gate, *, tm, tn, name, layer=None):
    m, k = a.shape
    n = b.shape[-1]

    def body(a_ref, b_ref, r_ref, g_ref, x_ref, y_ref):
        acc = _dot(a_ref[...], b_ref[...])
        y_ref[...] = acc
        x_ref[...] = r_ref[...] + g_ref[...] * acc

    return pl.pallas_call(
        body, grid=(m // tm, n // tn),
        in_specs=[pl.BlockSpec((tm, k), lambda i, j: (i, 0)), _stacked((k, tn), lambda i, j: (0, j), layer),
                  pl.BlockSpec((tm, tn), lambda i, j: (i, j)), pl.BlockSpec((1, tn), lambda i, j: (0, j))],
        out_specs=[pl.BlockSpec((tm, tn), lambda i, j: (i, j)), pl.BlockSpec((tm, tn), lambda i, j: (i, j))],
        out_shape=[_sds((m, n), F32), _sds((m, n), F32)],
        compiler_params=_cparams(("parallel", "parallel")), name=name)(a, b, res, gate)


def mm_nt(a, b, *, tm, tn, out_dtype, name, layer=None):
    m, k = a.shape
    n = b.shape[-2]

    def body(a_ref, b_ref, o_ref):
        o_ref[...] = _dot_nt(a_ref[...], b_ref[...]).astype(o_ref.dtype)

    return pl.pallas_call(
        body, grid=(m // tm, n // tn),
        in_specs=[pl.BlockSpec((tm, k), lambda i, j: (i, 0)), _stacked((tn, k), lambda i, j: (j, 0), layer)],
        out_specs=pl.BlockSpec((tm, tn), lambda i, j: (i, j)),
        out_shape=_sds((m, n), out_dtype), compiler_params=_cparams(("parallel", "parallel")), name=name)(a, b)


def mm_tn(a, b, *, tm, tn, out_dtype, name, col0=0, ncols=None, stack=None, layer=None):
    s, m = a.shape
    n = b.shape[1] - col0 if ncols is None else ncols
    c0 = col0 // tn

    def body(a_ref, b_ref, *rest):
        o_ref = rest[-1]
        o_ref[...] = _dot_tn(a_ref[...], b_ref[...]).astype(o_ref.dtype)

    in_specs = [pl.BlockSpec((s, tm), lambda i, j: (0, i)), pl.BlockSpec((s, tn), lambda i, j: (0, c0 + j))]
    if stack is None:
        return pl.pallas_call(
            body, grid=(m // tm, n // tn), in_specs=in_specs, out_specs=pl.BlockSpec((tm, tn), lambda i, j: (i, j)),
            out_shape=_sds((m, n), out_dtype), compiler_params=_cparams(("parallel", "parallel")), name=name)(a, b)
    return pl.pallas_call(
        body, grid=(m // tm, n // tn), in_specs=in_specs + [pl.BlockSpec(memory_space=pl.ANY)],
        out_specs=pl.BlockSpec((None, tm, tn), lambda i, j: (layer, i, j)),
        out_shape=_sds(stack.shape, stack.dtype), input_output_aliases={2: 0},
        compiler_params=_cparams(("parallel", "parallel")), name=name)(a, b, stack)


def _row_tile(s):
    return min(256, s)


def _norm_mod_f(x, g, scale, shift):
    r = lax.rsqrt(jnp.mean(x * x, axis=-1, keepdims=True) + EPS)
    return (x * r * g) * (1.0 + scale) + shift


def norm_mod(x, g, scale, shift, *, name):
    s = x.shape[0]
    t = _row_tile(s)

    def body(x_ref, g_ref, sc_ref, sh_ref, o_ref):
        o_ref[...] = _norm_mod_f(x_ref[...], g_ref[...], sc_ref[...], sh_ref[...]).astype(o_ref.dtype)

    vec = pl.BlockSpec((1, D), lambda i: (0, 0))
    return pl.pallas_call(
        body, grid=(s // t,), in_specs=[pl.BlockSpec((t, D), lambda i: (i, 0)), vec, vec, vec],
        out_specs=pl.BlockSpec((t, D), lambda i: (i, 0)), out_shape=_sds((s, D), MMT),
        compiler_params=_cparams(("parallel",)), name=name)(x, g, scale, shift)


def norm_mod_bwd(x, dh, dres, g, scale, shift, *, name):
    s = x.shape[0]
    t = _row_tile(s)

    def body(x_ref, dh_ref, dr_ref, g_ref, sc_ref, sh_ref, dx_ref, dg_ref, dsc_ref, dsh_ref):
        @pl.when(pl.program_id(0) == 0)
        def _():
            dg_ref[...] = jnp.zeros_like(dg_ref)
            dsc_ref[...] = jnp.zeros_like(dsc_ref)
            dsh_ref[...] = jnp.zeros_like(dsh_ref)

        _, vjp = jax.vjp(_norm_mod_f, x_ref[...], g_ref[...], sc_ref[...], sh_ref[...])
        dx, dg, dsc, dsh = vjp(dh_ref[...])
        dx_ref[...] = dr_ref[...] + dx
        dg_ref[...] += dg
        dsc_ref[...] += dsc
        dsh_ref[...] += dsh

    row = pl.BlockSpec((t, D), lambda i: (i, 0))
    vec = pl.BlockSpec((1, D), lambda i: (0, 0))
    return pl.pallas_call(
        body, grid=(s // t,), in_specs=[row, row, row, vec, vec, vec], out_specs=[row, vec, vec, vec],
        out_shape=[_sds((s, D), F32)] + [_sds((1, D), F32)] * 3,
        compiler_params=_cparams(("arbitrary",)), name=name)(x, dh, dres, g, scale, shift)


def gate_bwd(dx, y, gate, *, name):
    s = dx.shape[0]
    t = _row_tile(s)

    def body(dx_ref, y_ref, g_ref, dy_ref, dg_ref):
        @pl.when(pl.program_id(0) == 0)
        def _():
            dg_ref[...] = jnp.zeros_like(dg_ref)

        dxv = dx_ref[...]
        dy_ref[...] = (g_ref[...] * dxv).astype(dy_ref.dtype)
        dg_ref[...] += jnp.sum(dxv * y_ref[...], axis=0, keepdims=True)

    row = pl.BlockSpec((t, D), lambda i: (i, 0))
    vec = pl.BlockSpec((1, D), lambda i: (0, 0))
    return pl.pallas_call(
        body, grid=(s // t,), in_specs=[row, row, vec], out_specs=[row, vec],
        out_shape=[_sds((s, D), MMT), _sds((1, D), F32)],
        compiler_params=_cparams(("arbitrary",)), name=name)(dx, y, gate)


def loss_and_grad(xf, target, *, name):
    s = xf.shape[0]
    t = _row_tile(s)

    def body(x_ref, t_ref, l_ref, dx_ref):
        @pl.when(pl.program_id(0) == 0)
        def _():
            l_ref[...] = jnp.zeros_like(l_ref)

        e = x_ref[...] - t_ref[...]
        dx_ref[...] = e * (1.0 / D)
        l_ref[...] += 0.5 * jnp.sum(jnp.sum(e * e, axis=1, keepdims=True), axis=0, keepdims=True) * (1.0 / D)

    row = pl.BlockSpec((t, D), lambda i: (i, 0))
    return pl.pallas_call(
        body, grid=(s // t,), in_specs=[row, row], out_specs=[pl.BlockSpec((1, 1), lambda i: (0, 0)), row],
        out_shape=[_sds((1, 1), F32), _sds((s, D), F32)],
        compiler_params=_cparams(("arbitrary",)), name=name)(xf, target)


def _ret_consts():
    log_g = np.log1p(-np.exp2(-5.0 - np.arange(NH, dtype=np.float32))).astype(np.float32)
    idx = np.arange(CH, dtype=np.float32)
    d_intra = np.exp(np.abs(idx[:, None] - idx[None, :])[None] * log_g[:, None, None]).astype(np.float32)
    k_w = np.exp((CH - 1.0 - idx)[None, :] * log_g[:, None]).astype(np.float32)
    q_w = np.exp((idx + 1.0)[None, :] * log_g[:, None]).astype(np.float32)
    g_chunk = [float(v) for v in np.exp(np.float32(CH) * log_g).astype(np.float32)]
    bc = lambda a: np.ascontiguousarray(np.broadcast_to(a[:, :, None], (NH, CH, HD)))
    return jnp.asarray(d_intra), jnp.asarray(bc(k_w)), jnp.asarray(bc(q_w)), g_chunk


def _rope_tables(s):
    half = HD // 2
    inv_freq = (ROPE_BASE ** (-np.arange(half, dtype=np.float64) / half)).astype(np.float32)
    ang = (np.arange(s, dtype=np.float32)[:, None] * inv_freq[None, :]).astype(np.float64)
    cos, sin = np.cos(ang).astype(np.float32), np.sin(ang).astype(np.float32)
    return jnp.asarray(np.concatenate([cos, cos], axis=1)), jnp.asarray(np.concatenate([-sin, sin], axis=1))


def _ret_chunk(qs, ks, vs, rs, cos, sin, dintra, kw, qw, g_chunk):
    outs, rn = [], []
    for h in range(NH):
        q = qs[h] * cos + _swap_halves(qs[h]) * sin
        k = (ks[h] * cos + _swap_halves(ks[h]) * sin) * (HD ** -0.5)
        sc = _bdot_nt(q, k) * dintra[h]
        outs.append(_bdot(sc, vs[h]) + _bdot(q * qw[h], rs[h]))
        rn.append(g_chunk[h] * rs[h] + _bdot_tn(k * kw[h], vs[h]))
    return outs, rn


def _heads(x):
    return [x[:, h * HD:(h + 1) * HD] for h in range(NH)]


def retention_fwd(zz, cosf, sinf, *, name):
    s = zz.shape[0]
    n = s // CH
    dintra, kw, qw, g_chunk = _ret_consts()

    def body(q_ref, k_ref, v_ref, c_ref, s_ref, di_ref, kw_ref, qw_ref, o_ref, rp_ref, r_scr):
        @pl.when(pl.program_id(0) == 0)
        def _():
            r_scr[...] = jnp.zeros_like(r_scr)

        rprev = r_scr[...]
        rp_ref[0] = rprev
        outs, rn = _ret_chunk(_heads(q_ref[...]), _heads(k_ref[...]), _heads(v_ref[...]),
                              [rprev[h * HD:(h + 1) * HD] for h in range(NH)], c_ref[...], s_ref[...],
                              [di_ref[h] for h in range(NH)], [kw_ref[h] for h in range(NH)],
                              [qw_ref[h] for h in range(NH)], g_chunk)
        o_ref[...] = jnp.concatenate(outs, axis=1)
        r_scr[...] = jnp.concatenate(rn, axis=0)

    col = lambda c: pl.BlockSpec((CH, BW), lambda i: (i, c // BW))
    tab = pl.BlockSpec((CH, HD), lambda i: (i, 0))
    cst = lambda shp: pl.BlockSpec(shp, lambda i: (0,) * len(shp))
    return pl.pallas_call(
        body, grid=(n,),
        in_specs=[col(RQ), col(RK), col(RV), tab, tab, cst((NH, CH, CH)), cst((NH, CH, HD)), cst((NH, CH, HD))],
        out_specs=[pl.BlockSpec((CH, BW), lambda i: (i, 0)), pl.BlockSpec((1, BW, HD), lambda i: (i, 0, 0))],
        out_shape=[_sds((s, BW), F32), _sds((n, BW, HD), F32)],
        scratch_shapes=[pltpu.VMEM((BW, HD), F32)],
        compiler_params=_cparams(("arbitrary",)), name=name)(zz, zz, zz, cosf, sinf, dintra, kw, qw)


def retention_bwd(zz, cosf, sinf, rprev, do, *, name):
    s = zz.shape[0]
    n = s // CH
    dintra, kw, qw, g_chunk = _ret_consts()

    def body(q_ref, k_ref, v_ref, c_ref, s_ref, di_ref, kw_ref, qw_ref, rp_ref, do_ref, dz_ref, dr_scr):
        @pl.when(pl.program_id(0) == 0)
        def _():
            dr_scr[...] = jnp.zeros_like(dr_scr)

        rprev_v = rp_ref[0]
        f = functools.partial(_ret_chunk, cos=c_ref[...], sin=s_ref[...],
                              dintra=[di_ref[h] for h in range(NH)], kw=[kw_ref[h] for h in range(NH)],
                              qw=[qw_ref[h] for h in range(NH)], g_chunk=g_chunk)
        _, vjp = jax.vjp(f, _heads(q_ref[...]), _heads(k_ref[...]), _heads(v_ref[...]),
                         [rprev_v[h * HD:(h + 1) * HD] for h in range(NH)])
        dr = dr_scr[...]
        dq, dk, dv, drp = vjp((_heads(do_ref[...]), [dr[h * HD:(h + 1) * HD] for h in range(NH)]))
        dz_ref[...] = jnp.concatenate(dq + dk + dv, axis=1).astype(dz_ref.dtype)
        dr_scr[...] = jnp.concatenate(drp, axis=0)

    col = lambda c: pl.BlockSpec((CH, BW), lambda i: (n - 1 - i, c // BW))
    tab = pl.BlockSpec((CH, HD), lambda i: (n - 1 - i, 0))
    cst = lambda shp: pl.BlockSpec(shp, lambda i: (0,) * len(shp))
    return pl.pallas_call(
        body, grid=(n,),
        in_specs=[col(RQ), col(RK), col(RV), tab, tab, cst((NH, CH, CH)), cst((NH, CH, HD)), cst((NH, CH, HD)),
                  pl.BlockSpec((1, BW, HD), lambda i: (n - 1 - i, 0, 0)), pl.BlockSpec((CH, BW), lambda i: (n - 1 - i, 0))],
        out_specs=pl.BlockSpec((CH, 3 * BW), lambda i: (n - 1 - i, 0)),
        out_shape=_sds((s, 3 * BW), MMT),
        scratch_shapes=[pltpu.VMEM((BW, HD), F32)],
        compiler_params=_cparams(("arbitrary",)), name=name)(zz, zz, zz, cosf, sinf, dintra, kw, qw, rprev, do)


GKW = NH * GDK


def _gla_consts():
    tri = np.tril(np.ones((CH, CH), np.float32))
    mask_t = np.zeros((BW, GKW), np.float32)
    for h in range(NH):
        mask_t[h * HD:(h + 1) * HD, h * GDK:(h + 1) * GDK] = 1.0
    return jnp.asarray(tri), jnp.asarray(mask_t)


def _gla_chunk(q, k, v, glr, w_a2, b_a, st, tri, mask_t):
    la = _log_sigmoid(_bdot(glr, w_a2) + b_a) * (1.0 / 16.0)
    bc = _dot(tri, la, HI)
    be = jnp.sum(la, axis=0, keepdims=True)
    kv_t = _bdot_tn(v, k * jnp.exp(be - bc)) * mask_t
    sn = jnp.exp(be) * st + kv_t
    return _bdot_nt(q * (GDK ** -0.5), sn), sn


def gla_fwd(zz, w_a2p, b_a, *, name):
    s = zz.shape[0]
    n = s // CH
    tri, mask_t = _gla_consts()

    def body(q_ref, k_ref, v_ref, lr_ref, w_ref, b_ref, tri_ref, m_ref, o_ref, sp_ref, st_scr):
        @pl.when(pl.program_id(0) == 0)
        def _():
            st_scr[...] = jnp.zeros_like(st_scr)

        sp = st_scr[...]
        sp_ref[0] = sp
        o, sn = _gla_chunk(q_ref[...], k_ref[...], v_ref[...], lr_ref[...], w_ref[...], b_ref[...], sp,
                           tri_ref[...], m_ref[...])
        o_ref[...] = o
        st_scr[...] = sn

    cst = lambda shp: pl.BlockSpec(shp, lambda i: (0,) * len(shp))
    return pl.pallas_call(
        body, grid=(n,),
        in_specs=[pl.BlockSpec((CH, GKW), lambda i: (i, GQ // GKW)), pl.BlockSpec((CH, GKW), lambda i: (i, GK // GKW)),
                  pl.BlockSpec((CH, BW), lambda i: (i, GV // BW)), pl.BlockSpec((CH, HD), lambda i: (i, LR // HD)),
                  cst((HD, GKW)), cst((1, GKW)), cst((CH, CH)), cst((BW, GKW))],
        out_specs=[pl.BlockSpec((CH, BW), lambda i: (i, 0)), pl.BlockSpec((1, BW, GKW), lambda i: (i, 0, 0))],
        out_shape=[_sds((s, BW), F32), _sds((n, BW, GKW), F32)],
        scratch_shapes=[pltpu.VMEM((BW, GKW), F32)],
        compiler_params=_cparams(("arbitrary",)), name=name)(zz, zz, zz, zz, w_a2p, b_a, tri, mask_t)


def gla_bwd(zz, w_a2p, b_a, sprev, do, *, name):
    s = zz.shape[0]
    n = s // CH
    tri, mask_t = _gla_consts()

    def body(q_ref, k_ref, v_ref, lr_ref, w_ref, b_ref, tri_ref, m_ref, sp_ref, do_ref,
             dz_ref, dlr_ref, dw_ref, db_ref, ds_scr):
        @pl.when(pl.program_id(0) == 0)
        def _():
            ds_scr[...] = jnp.zeros_like(ds_scr)
            dw_ref[...] = jnp.zeros_like(dw_ref)
            db_ref[...] = jnp.zeros_like(db_ref)

        f = functools.partial(_gla_chunk, tri=tri_ref[...], mask_t=m_ref[...])
        _, vjp = jax.vjp(f, q_ref[...], k_ref[...], v_ref[...], lr_ref[...], w_ref[...], b_ref[...], sp_ref[0])
        dq, dk, dv, dlr, dw, db, dsp = vjp((do_ref[...], ds_scr[...]))
        dz_ref[...] = jnp.concatenate([dq, dk, dv], axis=1).astype(dz_ref.dtype)
        dlr_ref[...] = dlr.astype(dlr_ref.dtype)
        dw_ref[...] += dw
        db_ref[...] += db
        ds_scr[...] = dsp

    cst = lambda shp: pl.BlockSpec(shp, lambda i: (0,) * len(shp))
    r = lambda i: n - 1 - i
    return pl.pallas_call(
        body, grid=(n,),
        in_specs=[pl.BlockSpec((CH, GKW), lambda i: (r(i), GQ // GKW)), pl.BlockSpec((CH, GKW), lambda i: (r(i), GK // GKW)),
                  pl.BlockSpec((CH, BW), lambda i: (r(i), GV // BW)), pl.BlockSpec((CH, HD), lambda i: (r(i), LR // HD)),
                  cst((HD, GKW)), cst((1, GKW)), cst((CH, CH)), cst((BW, GKW)),
                  pl.BlockSpec((1, BW, GKW), lambda i: (r(i), 0, 0)), pl.BlockSpec((CH, BW), lambda i: (r(i), 0))],
        out_specs=[pl.BlockSpec((CH, 2 * GKW + BW), lambda i: (r(i), 0)), pl.BlockSpec((CH, HD), lambda i: (r(i), 0)),
                   cst((HD, GKW)), cst((1, GKW))],
        out_shape=[_sds((s, 2 * GKW + BW), MMT), _sds((s, HD), MMT), _sds((HD, GKW), F32), _sds((1, GKW), F32)],
        scratch_shapes=[pltpu.VMEM((BW, GKW), F32)],
        compiler_params=_cparams(("arbitrary",)), name=name)(zz, zz, zz, zz, w_a2p, b_a, tri, mask_t, sprev, do)


def _fox_pre_f(fqs, fks, ff, gq, gk, bf):
    def rms(x, g):
        return x * lax.rsqrt(jnp.mean(x * x, axis=-1, keepdims=True) + EPS) * g

    qn = [rms(x, gq) * (HD ** -0.5) for x in fqs]
    kn = [rms(x, gk) for x in fks]
    return qn, kn, _log_sigmoid(ff + bf)


def fox_pre(zz, gq, gk, bf, *, name):
    s = zz.shape[0]
    t = _row_tile(s)
    tri = jnp.asarray(np.tril(np.ones((t, t), np.float32)))

    def body(q_ref, k_ref, f_ref, gq_ref, gk_ref, b_ref, tri_ref, qn_ref, kn_ref, cum_ref, carry):
        @pl.when(pl.program_id(0) == 0)
        def _():
            carry[...] = jnp.zeros_like(carry)

        qn, kn, lf = _fox_pre_f(_heads(q_ref[...]), _heads(k_ref[...]), f_ref[...], gq_ref[...], gk_ref[...], b_ref[...])
        qn_ref[...] = jnp.concatenate(qn, axis=1).astype(qn_ref.dtype)
        kn_ref[...] = jnp.concatenate(kn, axis=1).astype(kn_ref.dtype)
        cum_ref[...] = _dot(tri_ref[...], lf, HI) + carry[...]
        carry[...] += jnp.sum(lf, axis=0, keepdims=True)

    vec = pl.BlockSpec((1, HD), lambda i: (0, 0))
    return pl.pallas_call(
        body, grid=(s // t,),
        in_specs=[pl.BlockSpec((t, BW), lambda i: (i, FQ // BW)), pl.BlockSpec((t, BW), lambda i: (i, FK // BW)),
                  pl.BlockSpec((t, HD), lambda i: (i, FF // HD)), vec, vec, vec, pl.BlockSpec((t, t), lambda i: (0, 0))],
        out_specs=[pl.BlockSpec((t, BW), lambda i: (i, 0)), pl.BlockSpec((t, BW), lambda i: (i, 0)),
                   pl.BlockSpec((t, HD), lambda i: (i, 0))],
        out_shape=[_sds((s, BW), MMT), _sds((s, BW), MMT), _sds((s, HD), F32)],
        scratch_shapes=[pltpu.VMEM((1, HD), F32)],
        compiler_params=_cparams(("arbitrary",)), name=name)(zz, zz, zz, gq, gk, bf, tri)


def fox_pre_bwd(zz, gq, gk, bf, dqn, dkn, dcum, *, name):
    s = zz.shape[0]
    t = _row_tile(s)
    nt = s // t
    triu = jnp.asarray(np.triu(np.ones((t, t), np.float32)))

    def body(q_ref, k_ref, f_ref, gq_ref, gk_ref, b_ref, tri_ref, dqn_ref, dkn_ref, dcum_ref,
             dz_ref, dff_ref, dgq_ref, dgk_ref, db_ref, carry):
        @pl.when(pl.program_id(0) == 0)
        def _():
            carry[...] = jnp.zeros_like(carry)
            dgq_ref[...] = jnp.zeros_like(dgq_ref)
            dgk_ref[...] = jnp.zeros_like(dgk_ref)
            db_ref[...] = jnp.zeros_like(db_ref)

        dcum_v = dcum_ref[...]
        dlf = _dot(tri_ref[...], dcum_v, HI) + carry[...]
        carry[...] += jnp.sum(dcum_v, axis=0, keepdims=True)
        _, vjp = jax.vjp(_fox_pre_f, _heads(q_ref[...]), _heads(k_ref[...]), f_ref[...], gq_ref[...], gk_ref[...], b_ref[...])
        dq, dk, dff, dgq, dgk, db = vjp((_heads(dqn_ref[...]), _heads(dkn_ref[...]), dlf))
        dz_ref[...] = jnp.concatenate(dq + dk, axis=1).astype(dz_ref.dtype)
        dff_ref[...] = dff.astype(dff_ref.dtype)
        dgq_ref[...] += dgq
        dgk_ref[...] += dgk
        db_ref[...] += db

    r = lambda i: nt - 1 - i
    vec = pl.BlockSpec((1, HD), lambda i: (0, 0))
    return pl.pallas_call(
        body, grid=(nt,),
        in_specs=[pl.BlockSpec((t, BW), lambda i: (r(i), FQ // BW)), pl.BlockSpec((t, BW), lambda i: (r(i), FK // BW)),
                  pl.BlockSpec((t, HD), lambda i: (r(i), FF // HD)), vec, vec, vec, pl.BlockSpec((t, t), lambda i: (0, 0)),
                  pl.BlockSpec((t, BW), lambda i: (r(i), 0)), pl.BlockSpec((t, BW), lambda i: (r(i), 0)),
                  pl.BlockSpec((t, HD), lambda i: (r(i), 0))],
        out_specs=[pl.BlockSpec((t, 2 * BW), lambda i: (r(i), 0)), pl.BlockSpec((t, HD), lambda i: (r(i), 0)), vec, vec, vec],
        out_shape=[_sds((s, 2 * BW), MMT), _sds((s, HD), MMT), _sds((1, HD), F32), _sds((1, HD), F32), _sds((1, HD), F32)],
        scratch_shapes=[pltpu.VMEM((1, HD), F32)],
        compiler_params=_cparams(("arbitrary",)), name=name)(zz, zz, zz, gq, gk, bf, triu, dqn, dkn, dcum)


def _fox_blocks(s):
    return min(256, s), min(512, s)


NEG = -1e30


def fox_attn_fwd(qn, kn, zz, cum_col, cum_row, *, name):
    s = qn.shape[0]
    bq, bk = _fox_blocks(s)

    def body(q_ref, k_ref, v_ref, cc_ref, cr_ref, o_ref, lse_ref):
        qi = pl.program_id(1)
        q = q_ref[...]
        cq = cc_ref[...]
        rows = qi * bq + lax.broadcasted_iota(jnp.int32, (bq, bk), 0)
        cols0 = lax.broadcasted_iota(jnp.int32, (bq, bk), 1)

        def step(j, carry):
            m, l, acc = carry
            off = pl.multiple_of(j * bk, bk)
            k = k_ref[pl.ds(off, bk), :]
            v = v_ref[pl.ds(off, bk), :].astype(MMT)
            sc = _dot_nt(q, k) + cq - cr_ref[pl.ds(j, 1), :]
            sc = jnp.where(rows >= cols0 + j * bk, sc, NEG)
            m_new = jnp.maximum(m, jnp.max(sc, axis=1, keepdims=True))
            alpha = jnp.exp(m - m_new)
            p = jnp.exp(sc - m_new)
            return m_new, alpha * l + jnp.sum(p, axis=1, keepdims=True), alpha * acc + _dot(p.astype(MMT), v)

        nk = ((qi + 1) * bq + bk - 1) // bk
        m, l, acc = lax.fori_loop(0, nk, step, (jnp.full((bq, 1), NEG, F32), jnp.zeros((bq, 1), F32),
                                                jnp.zeros((bq, HD), F32)))
        o_ref[...] = acc / l
        lse_ref[...] = m + jnp.log(l)

    return pl.pallas_call(
        body, grid=(NH, s // bq),
        in_specs=[pl.BlockSpec((bq, HD), lambda h, i: (i, h)), pl.BlockSpec((s, HD), lambda h, i: (0, h)),
                  pl.BlockSpec((s, HD), lambda h, i: (0, FV // HD + h)),
                  pl.BlockSpec((None, bq, 1), lambda h, i: (h, i, 0)), pl.BlockSpec((None, s // bk, bk), lambda h, i: (h, 0, 0))],
        out_specs=[pl.BlockSpec((bq, HD), lambda h, i: (i, h)), pl.BlockSpec((None, bq, 1), lambda h, i: (h, i, 0))],
        out_shape=[_sds((s, BW), F32), _sds((NH, s, 1), F32)],
        compiler_params=_cparams(("parallel", "parallel")), name=name)(qn, kn, zz, cum_col, cum_row)


def fox_attn_bwd(qn, kn, zz, cum_col, cum_row, lse, do, *, name):
    s = qn.shape[0]
    bq, bk = _fox_blocks(s)
    nkc = s // bk

    def body(q_ref, k_ref, v_ref, cc_ref, cr_ref, lse_ref, do_ref, dq_ref, dk_ref, dv_ref, dc_ref, p_scr, dp_scr):
        qi = pl.program_id(1)

        @pl.when(qi == 0)
        def _():
            dk_ref[...] = jnp.zeros_like(dk_ref)
            dv_ref[...] = jnp.zeros_like(dv_ref)
            dc_ref[...] = jnp.zeros_like(dc_ref)

        q = q_ref[...]
        dob = do_ref[...].astype(MMT)
        cq = cc_ref[...]
        lse_v = lse_ref[...]
        rows = qi * bq + lax.broadcasted_iota(jnp.int32, (bq, bk), 0)
        cols0 = lax.broadcasted_iota(jnp.int32, (bq, bk), 1)
        nk = ((qi + 1) * bq + bk - 1) // bk

        def probs(j, delta):
            off = pl.multiple_of(j * bk, bk)
            sc = _dot_nt(q, k_ref[pl.ds(off, bk), :]) + cq - cr_ref[pl.ds(j, 1), :]
            p = jnp.where(rows >= cols0 + j * bk, jnp.exp(sc - lse_v), 0.0)
            dp = _dot_nt(dob, v_ref[pl.ds(off, bk), :].astype(MMT))
            p_scr[j] = p
            dp_scr[j] = dp
            return delta + jnp.sum(p * dp, axis=1, keepdims=True)

        delta = lax.fori_loop(0, nk, probs, jnp.zeros((bq, 1), F32))

        def grads(j, dq):
            off = pl.multiple_of(j * bk, bk)
            p = p_scr[j]
            ds = p * (dp_scr[j] - delta)
            dsm = ds.astype(MMT)
            dv_ref[pl.ds(off, bk), :] += _dot_tn(p.astype(MMT), dob)
            dk_ref[pl.ds(off, bk), :] += _dot_tn(dsm, q)
            dc_ref[pl.ds(j, 1), :] -= jnp.sum(ds, axis=0, keepdims=True)
            return dq + _dot(dsm, k_ref[pl.ds(off, bk), :])

        dq_ref[...] = lax.fori_loop(0, nk, grads, jnp.zeros((bq, HD), F32))

    full = lambda c0=0: pl.BlockSpec((s, HD), lambda h, i: (0, c0 + h))
    blk = lambda: pl.BlockSpec((bq, HD), lambda h, i: (i, h))
    colv = lambda: pl.BlockSpec((None, bq, 1), lambda h, i: (h, i, 0))
    rowv = lambda: pl.BlockSpec((None, nkc, bk), lambda h, i: (h, 0, 0))
    return pl.pallas_call(
        body, grid=(NH, s // bq),
        in_specs=[blk(), full(), full(FV // HD), colv(), rowv(), colv(), blk()],
        out_specs=[blk(), full(), full(), rowv()],
        out_shape=[_sds((s, BW), F32), _sds((s, BW), F32), _sds((s, BW), F32), _sds((NH, nkc, bk), F32)],
        scratch_shapes=[pltpu.VMEM((nkc, bq, bk), F32), pltpu.VMEM((nkc, bq, bk), F32)],
        compiler_params=_cparams(("parallel", "arbitrary")), name=name)(qn, kn, zz, cum_col, cum_row, lse, do)


def _branch_f(rets, rgs, glas, ggs, ret_g, gla_g):
    out_r, out_g = [], []
    for h in range(NH):
        xc = rets[h] - jnp.mean(rets[h], axis=-1, keepdims=True)
        y = xc * lax.rsqrt(jnp.mean(xc * xc, axis=-1, keepdims=True) + EPS) * ret_g[h]
        out_r.append(_silu(rgs[h]) * y)
        x = glas[h]
        y = x * lax.rsqrt(jnp.mean(x * x, axis=-1, keepdims=True) + EPS) * gla_g
        out_g.append(_silu(ggs[h]) * y)
    return out_r, out_g


def _w_br_spec(layer):
    return pl.BlockSpec((None, 3, BW, D), lambda i: (layer, 0, 0, 0))


def mix_fwd(ret_raw, gla_raw, fox_o, zz, ret_g, gla_g, b_mg, w_br, *, name, layer):
    s = zz.shape[0]
    t = _row_tile(s)

    def body(r_ref, g_ref, f_ref, rg_ref, gg_ref, gp_ref, rgn_ref, ggn_ref, bmg_ref, w_ref, o_ref):
        rgn = rgn_ref[...]
        br_r, br_g = _branch_f(_heads(r_ref[...]), _heads(rg_ref[...]), _heads(g_ref[...]), _heads(gg_ref[...]),
                               _heads(rgn), ggn_ref[...])
        brs = [jnp.concatenate(br_r, axis=1), jnp.concatenate(br_g, axis=1), f_ref[...]]
        acc = jnp.zeros((t, D), F32)
        for b in range(3):
            gate = jax.nn.sigmoid(gp_ref[:, b * D:(b + 1) * D] + bmg_ref[:, b * D:(b + 1) * D])
            acc = acc + gate * _dot(brs[b].astype(MMT), w_ref[b])
        o_ref[...] = acc.astype(o_ref.dtype)

    row = lambda w, c=0: pl.BlockSpec((t, w), lambda i: (i, c // w))
    cst = lambda shp: pl.BlockSpec(shp, lambda i: (0,) * len(shp))
    return pl.pallas_call(
        body, grid=(s // t,),
        in_specs=[row(BW), row(BW), row(BW), row(BW, RG), row(BW, GG), row(3 * D, GP), cst((1, BW)), cst((1, HD)),
                  cst((1, 3 * D)), _w_br_spec(layer)],
        out_specs=row(D), out_shape=_sds((s, D), MMT),
        compiler_params=_cparams(("parallel",)), name=name)(ret_raw, gla_raw, fox_o, zz, zz, zz, ret_g, gla_g, b_mg, w_br)


def mix_bwd(ret_raw, gla_raw, fox_o, zz, ret_g, gla_g, b_mg, w_br, dmi, *, name, layer):
    s = zz.shape[0]
    t = _row_tile(s)

    def body(r_ref, g_ref, f_ref, rg_ref, gg_ref, gp_ref, rgn_ref, ggn_ref, bmg_ref, w_ref, dmi_ref,
             dr_ref, dg_ref, df_ref, drg_ref, dgg_ref, dgp_ref, dw_ref, drgn_ref, dggn_ref, dbmg_ref):
        @pl.when(pl.program_id(0) == 0)
        def _():
            dw_ref[...] = jnp.zeros_like(dw_ref)
            drgn_ref[...] = jnp.zeros_like(drgn_ref)
            dggn_ref[...] = jnp.zeros_like(dggn_ref)
            dbmg_ref[...] = jnp.zeros_like(dbmg_ref)

        (br_r, br_g), vjp = jax.vjp(_branch_f, _heads(r_ref[...]), _heads(rg_ref[...]), _heads(g_ref[...]),
                                    _heads(gg_ref[...]), _heads(rgn_ref[...]), ggn_ref[...])
        brs = [jnp.concatenate(br_r, axis=1).astype(MMT), jnp.concatenate(br_g, axis=1).astype(MMT),
               f_ref[...].astype(MMT)]
        dmi_v = dmi_ref[...].astype(F32)
        dbr = []
        for b in range(3):
            w = w_ref[b]
            ybr = _dot(brs[b], w)
            gate = jax.nn.sigmoid(gp_ref[:, b * D:(b + 1) * D] + bmg_ref[:, b * D:(b + 1) * D])
            dgp = dmi_v * ybr * gate * (1.0 - gate)
            dgp_ref[:, b * D:(b + 1) * D] = dgp.astype(dgp_ref.dtype)
            dbmg_ref[:, b * D:(b + 1) * D] += jnp.sum(dgp, axis=0, keepdims=True)
            dy = (dmi_v * gate).astype(MMT)
            dw_ref[b] += _dot_tn(brs[b], dy)
            dbr.append(_dot_nt(dy, w))
        dr, drg, dg, dgg, drgn, dggn = vjp((_heads(dbr[0]), _heads(dbr[1])))
        dr_ref[...] = jnp.concatenate(dr, axis=1)
        dg_ref[...] = jnp.concatenate(dg, axis=1)
        df_ref[...] = dbr[2]
        drg_ref[...] = jnp.concatenate(drg, axis=1).astype(drg_ref.dtype)
        dgg_ref[...] = jnp.concatenate(dgg, axis=1).astype(dgg_ref.dtype)
        drgn_ref[...] += jnp.concatenate(drgn, axis=1)
        dggn_ref[...] += dggn

    row = lambda w, c=0: pl.BlockSpec((t, w), lambda i: (i, c // w))
    cst = lambda shp: pl.BlockSpec(shp, lambda i: (0,) * len(shp))
    return pl.pallas_call(
        body, grid=(s // t,),
        in_specs=[row(BW), row(BW), row(BW), row(BW, RG), row(BW, GG), row(3 * D, GP), cst((1, BW)), cst((1, HD)),
                  cst((1, 3 * D)), _w_br_spec(layer), row(D)],
        out_specs=[row(BW), row(BW), row(BW), row(BW), row(BW), row(3 * D), cst((3, BW, D)), cst((1, BW)), cst((1, HD)),
                   cst((1, 3 * D))],
        out_shape=[_sds((s, BW), F32)] * 3 + [_sds((s, BW), MMT)] * 2 + [_sds((s, 3 * D), MMT), _sds((3, BW, D), F32),
                                                                          _sds((1, BW), F32), _sds((1, HD), F32),
                                                                          _sds((1, 3 * D), F32)],
        compiler_params=_cparams(("arbitrary",)), name=name)(ret_raw, gla_raw, fox_o, zz, zz, zz, ret_g, gla_g, b_mg, w_br, dmi)


CT = 256


def _shift_down(x, k, rows):
    return jnp.where(rows >= k, pltpu.roll(x, k, 0), 0.0)


def _shift_up(x, k, rows, s):
    return jnp.where(rows < s - k, pltpu.roll(x, s - k, 0), 0.0)


def conv_fwd(ug, w_conv, b_conv, *, name):
    s = ug.shape[0]
    nt = DFF // CT

    def body(u_ref, g_ref, w_ref, b_ref, a_ref):
        u = u_ref[...]
        rows = lax.broadcasted_iota(jnp.int32, u.shape, 0)
        uc = b_ref[...] + w_ref[0:1, :] * _shift_down(u, 2, rows) + w_ref[1:2, :] * _shift_down(u, 1, rows) + w_ref[2:3, :] * u
        a_ref[...] = (_silu(uc) * g_ref[...]).astype(a_ref.dtype)

    return pl.pallas_call(
        body, grid=(nt,),
        in_specs=[pl.BlockSpec((s, CT), lambda j: (0, j)), pl.BlockSpec((s, CT), lambda j: (0, nt + j)),
                  pl.BlockSpec((3, CT), lambda j: (0, j)), pl.BlockSpec((1, CT), lambda j: (0, j))],
        out_specs=pl.BlockSpec((s, CT), lambda j: (0, j)), out_shape=_sds((s, DFF), MMT),
        compiler_params=_cparams(("parallel",)), name=name)(ug, ug, w_conv, b_conv)


def conv_bwd(ug, w_conv, b_conv, da, *, name):
    s = ug.shape[0]
    nt = DFF // CT

    def body(u_ref, g_ref, w_ref, b_ref, da_ref, du_ref, dg_ref, dw_ref, db_ref):
        u = u_ref[...]
        rows = lax.broadcasted_iota(jnp.int32, u.shape, 0)
        u2, u1 = _shift_down(u, 2, rows), _shift_down(u, 1, rows)
        uc = b_ref[...] + w_ref[0:1, :] * u2 + w_ref[1:2, :] * u1 + w_ref[2:3, :] * u
        sg = jax.nn.sigmoid(uc)
        da_v = da_ref[...]
        dg_ref[...] = (da_v * uc * sg).astype(dg_ref.dtype)
        duc = da_v * g_ref[...] * sg * (1.0 + uc * (1.0 - sg))
        du = w_ref[2:3, :] * duc + w_ref[1:2, :] * _shift_up(duc, 1, rows, s) + w_ref[0:1, :] * _shift_up(duc, 2, rows, s)
        du_ref[...] = du.astype(du_ref.dtype)
        dw_ref[0:1, :] = jnp.sum(duc * u2, axis=0, keepdims=True)
        dw_ref[1:2, :] = jnp.sum(duc * u1, axis=0, keepdims=True)
        dw_ref[2:3, :] = jnp.sum(duc * u, axis=0, keepdims=True)
        db_ref[...] = jnp.sum(duc, axis=0, keepdims=True)

    col = lambda: pl.BlockSpec((s, CT), lambda j: (0, j))
    return pl.pallas_call(
        body, grid=(nt,),
        in_specs=[col(), pl.BlockSpec((s, CT), lambda j: (0, nt + j)), pl.BlockSpec((3, CT), lambda j: (0, j)),
                  pl.BlockSpec((1, CT), lambda j: (0, j)), col()],
        out_specs=[col(), col(), pl.BlockSpec((3, CT), lambda j: (0, j)), pl.BlockSpec((1, CT), lambda j: (0, j))],
        out_shape=[_sds((s, DFF), MMT), _sds((s, DFF), MMT), _sds((3, DFF), F32), _sds((1, DFF), F32)],
        compiler_params=_cparams(("parallel",)), name=name)(ug, ug, w_conv, b_conv, da)


def _tiles(s):
    return min(1024, s)


def layer_fwd(x, mod, p, cosf, sinf):
    s = x.shape[0]
    tm = _tiles(s)
    l = p["l"]
    shift1, scale1, gate1, shift2, scale2, gate2 = mod
    h = norm_mod(x, p["norm1_g"], scale1, shift1, name="norm_mod")
    zz = mm_nn(h, p["w1"], tm=tm, tn=768, out_dtype=F32, name="mm_w1", layer=l)
    ret_raw, rprev = retention_fwd(zz, cosf, sinf, name="ret_fwd")
    gla_raw, sprev = gla_fwd(zz, p["w_a2p"], p["b_gla_a"], name="gla_fwd")
    qn, kn, cum = fox_pre(zz, p["q_norm_g"], p["k_norm_g"], p["b_foxp"], name="fox_pre")
    bq, bk = _fox_blocks(s)
    cum_t = cum[:, :NH].T
    cum_col, cum_row = cum_t[:, :, None], cum_t.reshape(NH, s // bk, bk)
    fox_o, lse = fox_attn_fwd(qn, kn, zz, cum_col, cum_row, name="fox_fwd")
    mi = mix_fwd(ret_raw, gla_raw, fox_o, zz, p["ret_norm_g"], p["gla_norm_g"], p["b_mg"], p["w_br"], name="mix_fwd",
                 layer=l)
    x1, mixed = mm_nn_residual(mi, p["w_o"], x, gate1, tm=tm, tn=512, name="mm_wo", layer=l)
    h2 = norm_mod(x1, p["norm2_g"], scale2, shift2, name="norm_mod")
    ug = mm_nn(h2, p["w_up"], tm=tm, tn=512, out_dtype=F32, name="mm_wup", layer=l)
    a = conv_fwd(ug, p["w_conv"], p["b_conv"], name="conv_fwd")
    x2, y = mm_nn_residual(a, p["w_down"], x1, gate2, tm=tm, tn=512, name="mm_wdown", layer=l)
    saved = dict(x=x, h=h, zz=zz, ret_raw=ret_raw, rprev=rprev, gla_raw=gla_raw, sprev=sprev, qn=qn, kn=kn,
                 cum_col=cum_col, cum_row=cum_row, fox_o=fox_o, lse=lse, mi=mi, mixed=mixed, x1=x1, h2=h2, ug=ug, a=a, y=y)
    return x2, saved


def layer_bwd(dx2, mod, p, sv, cosf, sinf, stacks, slot):
    s = dx2.shape[0]
    tm = _tiles(s)
    l = p["l"]
    shift1, scale1, gate1, shift2, scale2, gate2 = mod
    g, stacks = {}, dict(stacks)
    dy, dgate2 = gate_bwd(dx2, sv["y"], gate2, name="gate_bwd")
    stacks["w_down"] = mm_tn(sv["a"], dy, tm=min(1408, DFF), tn=512, out_dtype=MMT, name="mm_dwdown",
                             stack=stacks["w_down"], layer=slot)
    da = mm_nt(dy, p["w_down"], tm=tm, tn=1408, out_dtype=F32, name="mm_da", layer=l)
    du, dg, g["w_conv"], g["b_conv"] = conv_bwd(sv["ug"], p["w_conv"], p["b_conv"], da, name="conv_bwd")
    dug = jnp.concatenate([du, dg], axis=1)
    stacks["w_up"] = mm_tn(sv["h2"], dug, tm=512, tn=512, out_dtype=MMT, name="mm_dwup", stack=stacks["w_up"], layer=slot)
    dh2 = mm_nt(dug, p["w_up"], tm=min(512, s), tn=512, out_dtype=F32, name="mm_dh2", layer=l)
    dx1, g["norm2_g"], dscale2, dshift2 = norm_mod_bwd(sv["x1"], dh2, dx2, p["norm2_g"], scale2, shift2, name="norm_mod_bwd")
    dmixed, dgate1 = gate_bwd(dx1, sv["mixed"], gate1, name="gate_bwd")
    stacks["w_o"] = mm_tn(sv["mi"], dmixed, tm=512, tn=512, out_dtype=MMT, name="mm_dwo", stack=stacks["w_o"], layer=slot)
    dmi = mm_nt(dmixed, p["w_o"], tm=tm, tn=512, out_dtype=MMT, name="mm_dmi", layer=l)
    zz = sv["zz"]
    (dret, dgla, dfox, drg, dgg, dgp, g["w_br"], g["ret_norm_g"], g["gla_norm_g"], g["b_mg"]) = mix_bwd(
        sv["ret_raw"], sv["gla_raw"], sv["fox_o"], zz, p["ret_norm_g"], p["gla_norm_g"], p["b_mg"], p["w_br"], dmi,
        name="mix_bwd", layer=l)
    dqn, dkn, dfv, dcum_row = fox_attn_bwd(sv["qn"], sv["kn"], zz, sv["cum_col"], sv["cum_row"], sv["lse"], dfox,
                                           name="fox_bwd")
    dfv = dfv.astype(MMT)
    dcum = jnp.pad(dcum_row.reshape(NH, s).T, ((0, 0), (0, HD - NH)))
    dfqk, dff, g["q_norm_g"], g["k_norm_g"], g["b_foxp"] = fox_pre_bwd(
        zz, p["q_norm_g"], p["k_norm_g"], p["b_foxp"], dqn, dkn, dcum, name="fox_pre_bwd")
    dgqkv, dlr, g["w_a2p"], g["b_gla_a"] = gla_bwd(zz, p["w_a2p"], p["b_gla_a"], sv["sprev"], dgla, name="gla_bwd")
    drqkv = retention_bwd(zz, cosf, sinf, sv["rprev"], dret, name="ret_bwd")
    dzz = jnp.concatenate([dgp, drqkv, drg, dgqkv, dgg, dfqk, dfv, dlr, dff], axis=1)
    stacks["w_mg"] = mm_tn(sv["h"], dzz, tm=512, tn=768, out_dtype=MMT, name="mm_dwmg", ncols=RQ, stack=stacks["w_mg"],
                           layer=slot)
    dwz = mm_tn(sv["h"], dzz, tm=512, tn=768, out_dtype=MMT, name="mm_dwz", col0=RQ)
    stacks["w_in"] = unalign_dw_in(dwz, stacks["w_in"], slot)
    dh = mm_nt(dzz, p["w1"], tm=min(512, s), tn=512, out_dtype=F32, name="mm_dh", layer=l)
    dx, g["norm1_g"], dscale1, dshift1 = norm_mod_bwd(sv["x"], dh, dx1, p["norm1_g"], scale1, shift1, name="norm_mod_bwd")
    dmod = jnp.concatenate([dshift1, dscale1, dgate1, dshift2, dscale2, dgate2], axis=1)
    return dx, g, dmod, stacks


def _align_cols(w_in, w_mg):
    z = lambda n: jnp.zeros((w_in.shape[0], n), w_in.dtype)
    return jnp.concatenate([w_mg, w_in[:, :3072], w_in[:, 3088:5136], w_in[:, 3072:3088], z(HD - GLR),
                            w_in[:, 5136:5140], z(HD - NH)], axis=1)


def _unalign_cols(dwz):
    o = lambda c: c - RQ
    return jnp.concatenate([dwz[:, :o(GG)], dwz[:, o(LR):o(LR) + GLR], dwz[:, o(GG):o(LR)], dwz[:, o(FF):o(FF) + NH]], axis=1)


def build_w1(w_in_sh, w_mg):
    nl = w_mg.shape[0]
    t = _row_tile(D)

    def body(s_ref, g_ref, o_ref):
        o_ref[...] = _align_cols(jnp.concatenate([s_ref[k] for k in range(4)], axis=1), g_ref[...])

    return pl.pallas_call(
        body, grid=(nl, D // t),
        in_specs=[pl.BlockSpec((None, 4, t, IN_W // 4), lambda l, i: (l, 0, i, 0)), pl.BlockSpec((None, t, RQ), lambda l, i: (l, i, 0))],
        out_specs=pl.BlockSpec((None, t, NZZ), lambda l, i: (l, i, 0)), out_shape=_sds((nl, D, NZZ), w_mg.dtype),
        compiler_params=_cparams(("parallel", "parallel")), name="build_w1")(w_in_sh, w_mg)


def unalign_dw_in(dwz, stack, layer):
    t = _row_tile(D)

    def body(z_ref, s_ref, o_ref):
        w = _unalign_cols(z_ref[...])
        for k in range(4):
            o_ref[k] = w[:, k * (IN_W // 4):(k + 1) * (IN_W // 4)]

    return pl.pallas_call(
        body, grid=(D // t,),
        in_specs=[pl.BlockSpec((t, NZZ - RQ), lambda i: (i, 0)), pl.BlockSpec(memory_space=pl.ANY)],
        out_specs=pl.BlockSpec((None, 4, t, IN_W // 4), lambda i: (layer, 0, i, 0)), out_shape=_sds(stack.shape, stack.dtype),
        input_output_aliases={1: 0}, compiler_params=_cparams(("parallel",)), name="unalign_dw_in")(dwz, stack)


def layer_params(w, l):
    row = lambda v: v[l][None, :]
    return dict(
        l=l, norm1_g=row(w["norm1_g"]), norm2_g=row(w["norm2_g"]), w1=w["w1"],
        w_a2p=jnp.pad(w["w_gla_a2"][l], ((0, HD - GLR), (0, 0))), b_gla_a=row(w["b_gla_a"]),
        b_foxp=jnp.pad(row(w["b_fox_f"]), ((0, 0), (0, HD - NH))), ret_norm_g=row(w["ret_norm_g"]),
        gla_norm_g=row(w["gla_norm_g"]), q_norm_g=row(w["q_norm_g"]), k_norm_g=row(w["k_norm_g"]),
        w_br=w["w_br"], b_mg=row(w["b_mg"]), w_o=w["w_o"], w_up=w["w_up"], w_conv=w["w_conv"][l],
        b_conv=row(w["b_conv"]), w_down=w["w_down"])


def layer_grads(g):
    vec = lambda v: v[0]
    return dict(
        norm1_g=vec(g["norm1_g"]), norm2_g=vec(g["norm2_g"]), w_gla_a2=g["w_a2p"][:GLR], b_gla_a=vec(g["b_gla_a"]),
        b_fox_f=g["b_foxp"][0, :NH], ret_norm_g=vec(g["ret_norm_g"]), gla_norm_g=vec(g["gla_norm_g"]),
        q_norm_g=vec(g["q_norm_g"]), k_norm_g=vec(g["k_norm_g"]), w_br=g["w_br"], b_mg=vec(g["b_mg"]),
        w_conv=g["w_conv"], b_conv=vec(g["b_conv"]))


def ada_mod(c_all, w_ada, b_ada):
    nl, _, n = w_ada.shape

    def body(c_ref, w_ref, b_ref, o_ref):
        o_ref[...] = _dot(_silu(c_ref[...]), w_ref[...], HI) + b_ref[...]

    return pl.pallas_call(
        body, grid=(nl,),
        in_specs=[pl.BlockSpec((8, D), lambda l: (0, 0)), pl.BlockSpec((None, D, n), lambda l: (l, 0, 0)),
                  pl.BlockSpec((None, 1, n), lambda l: (l, 0, 0))],
        out_specs=pl.BlockSpec((None, 8, n), lambda l: (l, 0, 0)), out_shape=_sds((nl, 8, n), F32),
        compiler_params=_cparams(("parallel",)), name="ada_mod")(c_all, w_ada, b_ada)


def ada_dw(c_all, dmod):
    nl, _, n = dmod.shape

    def body(c_ref, d_ref, o_ref):
        o_ref[...] = _dot_tn(_silu(c_ref[...]), d_ref[...], HI)

    return pl.pallas_call(
        body, grid=(nl,),
        in_specs=[pl.BlockSpec((8, D), lambda l: (0, 0)), pl.BlockSpec((None, 8, n), lambda l: (l, 0, 0))],
        out_specs=pl.BlockSpec((None, D, n), lambda l: (l, 0, 0)), out_shape=_sds((nl, D, n), F32),
        compiler_params=_cparams(("parallel",)), name="ada_dw")(c_all, dmod)


def sum_devices(g):
    def body(g_ref, o_ref):
        acc = g_ref[0]
        for d in range(1, 8):
            acc = acc + g_ref[d]
        o_ref[...] = acc

    return pl.pallas_call(body, out_shape=_sds(g.shape[1:], F32), name="sum_devices")(g)


def adamw(w, g, m, v, *, block, name):
    nd = w.ndim
    grid = tuple(w.shape[i] // block[i] for i in range(nd))
    bc1 = 1.0 - ADAM_B1 ** ADAM_STEP
    bc2 = 1.0 - ADAM_B2 ** ADAM_STEP

    def body(w_ref, g_ref, m_ref, v_ref, d_ref, nm_ref, nv_ref):
        gv = g_ref[...]
        nm = ADAM_B1 * m_ref[...] + (1.0 - ADAM_B1) * gv
        nv = ADAM_B2 * v_ref[...] + (1.0 - ADAM_B2) * (gv * gv)
        nm_ref[...] = nm
        nv_ref[...] = nv
        d_ref[...] = -ADAM_LR * ((nm / bc1) / (jnp.sqrt(nv / bc2) + ADAM_EPS) + ADAM_WD * w_ref[...])

    spec = pl.BlockSpec(tuple(block), lambda *i: i)
    return pl.pallas_call(
        body, grid=grid, in_specs=[spec] * 4, out_specs=[spec] * 3, out_shape=[_sds(w.shape, F32)] * 3,
        compiler_params=_cparams(("parallel",) * nd), name=name)(w, g, m, v)


MESH = pl.DeviceIdType.MESH
ANY = pl.BlockSpec(memory_space=pl.ANY)
VM = pl.BlockSpec(memory_space=pltpu.VMEM)


def _place():
    x, y, c = lax.axis_index("x"), lax.axis_index("y"), lax.axis_index("c")
    return x, y, c, [(1 - x, y), (x, 1 - y), (1 - x, 1 - y)]


def small_allgather(v, *, name):
    m_per, n = v.shape

    def body(x_ref, out_ref, send_sems, recv_sems, local_sem):
        x, y, c, chips = _place()
        me, sibling = (x, y, c), (x, y, 1 - c)

        def rows(px, py, pc):
            return out_ref.at[pl.ds((4 * px + 2 * py + pc) * m_per, m_per), :]

        def copy(k, block, to, src=None):
            return pltpu.make_async_remote_copy(
                src_ref=rows(*block) if src is None else src, dst_ref=rows(*block),
                send_sem=send_sems.at[k], recv_sem=recv_sems.at[k], device_id=to, device_id_type=MESH)

        mine = pltpu.make_async_copy(x_ref, rows(*me), local_sem)
        mine.start()
        first = [copy(0, me, sibling, src=x_ref)]
        first += [copy(1 + j, me, (*chip, c), src=x_ref) for j, chip in enumerate(chips)]
        for cp in first:
            cp.start()
        passed = [copy(4 + j, (*chip, c), sibling) for j, chip in enumerate(chips)]
        for j, chip in enumerate(chips):
            copy(1 + j, (*chip, c), me).wait_recv()
            passed[j].start()
        copy(0, sibling, me).wait_recv()
        for j, chip in enumerate(chips):
            copy(4 + j, (*chip, 1 - c), me).wait_recv()
        for cp in first + passed:
            cp.wait_send()
        mine.wait()

    return pl.pallas_call(
        body, out_shape=_sds((8 * m_per, n), v.dtype), in_specs=[VM], out_specs=VM,
        scratch_shapes=[pltpu.SemaphoreType.DMA((7,)), pltpu.SemaphoreType.DMA((7,)), pltpu.SemaphoreType.DMA],
        name=name)(v)


TENSORS = {
    "w_in": ("lead", None, (4, D, 1285), (1, D, 1285)),
    "w_mg": ("col", 768, (D, 3072), (512, 3072)),
    "w_br": ("col", 256, (3, BW, D), (3, BW, D)),
    "w_o": ("row", 256, (D, D), (D, D)),
    "w_up": ("col", 1408, (D, 5632), (256, 5632)),
    "w_down": ("row", 704, (DFF, D), (704, D)),
}
BIG = tuple(TENSORS)


def _shard_shape(name):
    kind, width, full, _ = TENSORS[name]
    if kind == "lead":
        return full[1:]
    return full[:-1] + (width,) if kind == "col" else (width,) + full[1:]


def _shard_view(ref, layers, name, k):
    kind, width, full, _ = TENSORS[name]
    if kind == "lead":
        return ref.at[layers, k]
    if kind == "row":
        return ref.at[layers, pl.ds(k * width, width)]
    return ref.at[(layers,) + (slice(None),) * (len(full) - 1) + (pl.ds(k * width, width),)]


def _remote(send_sems, recv_sems, k, src, dst, to):
    return pltpu.make_async_remote_copy(src_ref=src, dst_ref=dst, send_sem=send_sems.at[k], recv_sem=recv_sems.at[k],
                                        device_id=to, device_id_type=MESH)


def _dma_sems(n):
    return [pltpu.SemaphoreType.DMA((n,)), pltpu.SemaphoreType.DMA((n,))]


def gather_weights(shards, *, name):
    nl = shards[BIG[0]].shape[0]
    hl = nl // 2
    nt = len(BIG)
    per = 6 * hl + nl

    def body(*refs):
        p, o = dict(zip(BIG, refs[:nt])), dict(zip(BIG, refs[nt:2 * nt]))
        send_sems, recv_sems = refs[2 * nt:]
        x, y, c, chips = _place()
        mine, me, sibling = 2 * x + y, (x, y, c), (x, y, 1 - c)
        remote = functools.partial(_remote, send_sems, recv_sems)
        started = []

        def go(cp):
            cp.start()
            started.append(cp)

        for t, n in enumerate(BIG):
            for l in range(nl):
                go(remote(t * per + 6 * hl + l, p[n].at[l], _shard_view(o[n], l, n, mine), sibling))
        for i in range(hl):
            for t, n in enumerate(BIG):
                for j, chip in enumerate(chips):
                    layer = hl * c + i
                    go(remote(t * per + j * hl + i, p[n].at[layer], _shard_view(o[n], layer, n, mine), (*chip, c)))
        for i in range(hl):
            for t, n in enumerate(BIG):
                for j, chip in enumerate(chips):
                    landed = _shard_view(o[n], hl * c + i, n, 2 * chip[0] + chip[1])
                    remote(t * per + j * hl + i, landed, landed, me).wait_recv()
                    go(remote(t * per + 3 * hl + j * hl + i, landed, landed, sibling))
        for i in range(hl):
            for t, n in enumerate(BIG):
                for j, chip in enumerate(chips):
                    landed = _shard_view(o[n], hl * (1 - c) + i, n, 2 * chip[0] + chip[1])
                    remote(t * per + 3 * hl + j * hl + i, landed, landed, me).wait_recv()
        for t, n in enumerate(BIG):
            for l in range(nl):
                landed = _shard_view(o[n], l, n, mine)
                remote(t * per + 6 * hl + l, landed, landed, me).wait_recv()
        for cp in started:
            cp.wait_send()

    outs = pl.pallas_call(
        body, out_shape=[_sds((nl,) + TENSORS[n][2], shards[n].dtype) for n in BIG], in_specs=[ANY] * nt,
        out_specs=[ANY] * nt, scratch_shapes=_dma_sems(nt * per), name=name)(*[shards[n] for n in BIG])
    return dict(zip(BIG, outs))


HBM = pl.BlockSpec(memory_space=pltpu.HBM)
SEM = pl.BlockSpec(memory_space=pltpu.SEMAPHORE)


def pair_send(g, owner, *, name):
    nt = len(BIG)

    def body(*refs):
        send_sems, recv_sems = refs[2 * nt:]
        x, y, c, _ = _place()
        copies = [_remote(send_sems, recv_sems, t, refs[t], refs[nt + t], (x, y, owner)) for t in range(nt)]

        @pl.when(c != owner)
        def _():
            for cp in copies:
                cp.start()
            for cp in copies:
                cp.wait_send()

        @pl.when(c == owner)
        def _():
            for cp in copies:
                cp.wait_recv()

    outs = pl.pallas_call(
        body, out_shape=[_sds(g[n].shape, g[n].dtype) for n in BIG], in_specs=[ANY] * nt, out_specs=[ANY] * nt,
        scratch_shapes=_dma_sems(nt), name=name)(*[g[n] for n in BIG])
    return dict(zip(BIG, outs))


def _chip_copies(send_sems, recv_sems, s_refs, land_refs, c, chips):
    hl = s_refs[0].shape[0]
    return [_remote(send_sems, recv_sems, 3 * t + j, _shard_view(s_refs[t], pl.ds(0, hl), n, 2 * chip[0] + chip[1]),
                    land_refs[t].at[j], (*chip, c))
            for t, n in enumerate(BIG) for j, chip in enumerate(chips)]


def _landing_shapes(s):
    hl = s[BIG[0]].shape[0]
    return [_sds((3, hl) + _shard_shape(n), s[n].dtype) for n in BIG]


def chip_exchange(s, owner, *, name):
    nt = len(BIG)

    def body(*refs):
        send_sems, recv_sems = refs[2 * nt:]
        x, y, c, chips = _place()
        copies = _chip_copies(send_sems, recv_sems, refs[:nt], refs[nt:2 * nt], c, chips)

        @pl.when(c == owner)
        def _():
            for cp in copies:
                cp.start()
            for cp in copies:
                cp.wait()

    outs = pl.pallas_call(
        body, out_shape=_landing_shapes(s), in_specs=[ANY] * nt, out_specs=[ANY] * nt,
        scratch_shapes=_dma_sems(3 * nt), name=name)(*[s[n] for n in BIG])
    return dict(zip(BIG, outs))


def chip_exchange_start(s, owner, *, name):
    nt = len(BIG)

    def body(*refs):
        send_sems, recv_sems = refs[2 * nt:2 * nt + 2]
        x, y, c, chips = _place()
        copies = _chip_copies(send_sems, recv_sems, refs[:nt], refs[nt:2 * nt], c, chips)

        @pl.when(c == owner)
        def _():
            for cp in copies:
                cp.start()

        refs[-1][...] = jnp.zeros_like(refs[-1])

    hbm = lambda a: pltpu.with_memory_space_constraint(a, pltpu.HBM)
    lands = [hbm(lax.empty(d.shape, d.dtype)) for d in _landing_shapes(s)]
    srcs = [hbm(s[n]) for n in BIG]
    outs = pl.pallas_call(
        body,
        out_shape=(pltpu.SemaphoreType.DMA((3 * nt,)), pltpu.SemaphoreType.DMA((3 * nt,)),
                   *[pltpu.HBM(a.shape, a.dtype) for a in srcs + lands], _sds((8, HD), F32)),
        in_specs=[HBM] * (2 * nt), out_specs=(SEM, SEM, *[HBM] * (2 * nt), VM),
        input_output_aliases={t: 2 + t for t in range(2 * nt)},
        compiler_params=pltpu.CompilerParams(has_side_effects=pltpu.SideEffectType.DATAFLOW_SIDE_EFFECTING),
        name=name)(*srcs, *lands)
    return outs[0], outs[1], outs[2:2 + nt], outs[2 + nt:2 + 2 * nt], outs[-1]


def chip_exchange_wait(send_sems, recv_sems, srcs, lands, after, owner, *, name):
    nt = len(BIG)

    def body(*refs):
        x, y, c, chips = _place()
        copies = _chip_copies(refs[2 * nt], refs[2 * nt + 1], refs[:nt], refs[nt:2 * nt], c, chips)

        @pl.when(c == owner)
        def _():
            for cp in copies:
                cp.wait_send()
                cp.wait_recv()

    outs = pl.pallas_call(
        body, out_shape=tuple(pltpu.HBM(a.shape, a.dtype) for a in list(srcs) + list(lands)),
        in_specs=[HBM] * (2 * nt) + [SEM, SEM, ANY], out_specs=tuple([HBM] * (2 * nt)),
        input_output_aliases={t: t for t in range(2 * nt)},
        compiler_params=pltpu.CompilerParams(has_side_effects=pltpu.SideEffectType.DATAFLOW_SIDE_EFFECTING),
        name=name)(*srcs, *lands, send_sems, recv_sems, after)
    return dict(zip(BIG, outs[:nt])), dict(zip(BIG, outs[nt:]))


def pair_share(f, owner, *, name):
    nl = f[BIG[0]].shape[0]
    hl = nl // 2
    nt = len(BIG)

    def body(*refs):
        o = refs[nt:2 * nt]
        send_sems, recv_sems = refs[2 * nt:]
        x, y, c, _ = _place()
        group = pl.ds(hl * owner, hl)
        copies = [_remote(send_sems, recv_sems, t, o[t].at[group], o[t].at[group], (x, y, 1 - owner)) for t in range(nt)]

        @pl.when(c == owner)
        def _():
            for cp in copies:
                cp.start()
            for cp in copies:
                cp.wait_send()

        @pl.when(c != owner)
        def _():
            for cp in copies:
                cp.wait_recv()

    outs = pl.pallas_call(
        body, out_shape=[_sds(f[n].shape, f[n].dtype) for n in BIG], in_specs=[ANY] * nt, out_specs=[ANY] * nt,
        input_output_aliases={t: t for t in range(nt)}, scratch_shapes=_dma_sems(nt), name=name)(*[f[n] for n in BIG])
    return dict(zip(BIG, outs))


def pair_add(g, r, *, tensor, name):
    _, _, full, blk = TENSORS[tensor]

    def body(g_ref, r_ref, o_ref):
        o_ref[...] = (g_ref[...].astype(F32) + r_ref[...].astype(F32)).astype(o_ref.dtype)

    spec = pl.BlockSpec((None,) + blk, lambda *a: a)
    return pl.pallas_call(
        body, out_shape=_sds(r.shape, r.dtype), grid=(r.shape[0],) + tuple(f // b for f, b in zip(full, blk)),
        in_specs=[spec, spec], out_specs=spec, compiler_params=_cparams(("parallel",) * (1 + len(full))), name=name)(g, r)


def chip_add(s, r, k_idx, totals, owner, *, tensor, name):
    kind, width, full, _ = TENSORS[tensor]
    shard = _shard_shape(tensor)
    hl = s.shape[0]
    zeros = (0,) * len(shard)

    def body(k_ref, s_ref, r0_ref, r1_ref, r2_ref, t_ref, o_ref):
        o_ref[...] = ((s_ref[...].astype(F32) + r0_ref[...].astype(F32)) + r1_ref[...].astype(F32)) + r2_ref[...].astype(F32)

    if kind == "lead":
        mine = pl.BlockSpec((None, None) + shard, lambda i, k: (i, k[0]) + zeros)
    elif kind == "row":
        mine = pl.BlockSpec((None,) + shard, lambda i, k: (i, k[0]) + zeros[1:])
    else:
        mine = pl.BlockSpec((None,) + shard, lambda i, k: (i,) + zeros[1:] + (k[0],))
    peer = lambda j: pl.BlockSpec((None, None) + shard, lambda i, k: (j, i) + zeros)
    return pl.pallas_call(
        body, out_shape=_sds(totals.shape, F32),
        grid_spec=pltpu.PrefetchScalarGridSpec(
            num_scalar_prefetch=1, grid=(hl,), in_specs=[mine, peer(0), peer(1), peer(2), pl.BlockSpec(memory_space=pl.ANY)],
            out_specs=pl.BlockSpec((None,) + shard, lambda i, k: (hl * owner + i,) + zeros)),
        input_output_aliases={5: 0}, compiler_params=_cparams(("parallel",)), name=name)(k_idx, s, r, r, r, totals)


def _flat_rows(arrs):
    v = jnp.concatenate([a.reshape(-1) for a in arrs])
    n = -(-v.shape[0] // 1024) * 1024
    return jnp.pad(v, (0, n - v.shape[0])).reshape(n // HD, HD)


def _unflat(buf, shapes):
    v, out, o = buf.reshape(-1), [], 0
    for s in shapes:
        n = int(np.prod(s))
        out.append(v[o:o + n].reshape(s))
        o += n
    return out


WEIGHTS = ("norm1_g", "norm2_g", "w_ada", "b_ada", "w_in", "w_gla_a2", "b_gla_a", "b_fox_f", "ret_norm_g", "gla_norm_g",
           "q_norm_g", "k_norm_g", "w_br", "w_mg", "b_mg", "w_o", "w_up", "w_conv", "b_conv", "w_down")
REPLICATED = ("norm1_g", "norm2_g", "b_gla_a", "b_fox_f", "ret_norm_g", "gla_norm_g", "q_norm_g", "k_norm_g", "b_mg", "b_conv")
ADAM_BLOCKS = dict(w_ada=(1, 256, 1536), w_in=(1, 256, 1285), w_br=(1, 3, BW, 256), w_mg=(1, 512, 768), w_o=(2, 256, D),
                   w_up=(1, 256, 1408), w_down=(1, 352, D))
ALL_AXES = ("x", "y", "c")


def kernel(x, c, norm1_g, norm2_g, w_ada, b_ada, w_in, w_gla_a2, b_gla_a, b_fox_f, ret_norm_g, gla_norm_g, q_norm_g, k_norm_g, w_br, w_mg, b_mg, w_o, w_up, w_conv, b_conv, w_down, loss_target, m_norm1_g, m_norm2_g, m_w_ada, m_b_ada, m_w_in, m_w_gla_a2, m_b_gla_a, m_b_fox_f, m_ret_norm_g, m_gla_norm_g, m_q_norm_g, m_k_norm_g, m_w_br, m_w_mg, m_b_mg, m_w_o, m_w_up, m_w_conv, m_b_conv, m_w_down, v_norm1_g, v_norm2_g, v_w_ada, v_b_ada, v_w_in, v_w_gla_a2, v_b_gla_a, v_b_fox_f, v_ret_norm_g, v_gla_norm_g, v_q_norm_g, v_k_norm_g, v_w_br, v_w_mg, v_b_mg, v_w_o, v_w_up, v_w_conv, v_b_conv, v_w_down):
    w = dict(zip(WEIGHTS, (norm1_g, norm2_g, w_ada, b_ada, w_in, w_gla_a2, b_gla_a, b_fox_f, ret_norm_g, gla_norm_g,
                           q_norm_g, k_norm_g, w_br, w_mg, b_mg, w_o, w_up, w_conv, b_conv, w_down)))
    m = dict(zip(WEIGHTS, (m_norm1_g, m_norm2_g, m_w_ada, m_b_ada, m_w_in, m_w_gla_a2, m_b_gla_a, m_b_fox_f, m_ret_norm_g,
                           m_gla_norm_g, m_q_norm_g, m_k_norm_g, m_w_br, m_w_mg, m_b_mg, m_w_o, m_w_up, m_w_conv, m_b_conv,
                           m_w_down)))
    v = dict(zip(WEIGHTS, (v_norm1_g, v_norm2_g, v_w_ada, v_b_ada, v_w_in, v_w_gla_a2, v_b_gla_a, v_b_fox_f, v_ret_norm_g,
                           v_gla_norm_g, v_q_norm_g, v_k_norm_g, v_w_br, v_w_mg, v_b_mg, v_w_o, v_w_up, v_w_conv, v_b_conv,
                           v_w_down)))
    nl = norm1_g.shape[0]
    seq = x.shape[1]
    xi, yi, ci = lax.axis_index("x"), lax.axis_index("y"), lax.axis_index("c")
    k_me = 2 * xi + yi
    b_me = 4 * xi + 2 * yi + ci
    ada_n = w_ada.shape[2]
    a2_n, conv_n = w_gla_a2.shape[2], w_conv.shape[2]

    blk = _flat_rows([c, w_gla_a2, w_conv])
    g1 = small_allgather(blk, name="gather_small").reshape(8, blk.shape[0], HD)
    c_all = g1[:, :D // HD].reshape(8, D)
    by_chip = g1[0::2].reshape(4, -1)[:, D:]
    a2_sh, conv_sh = by_chip[:, :nl * GLR * a2_n], by_chip[:, nl * GLR * a2_n:nl * (GLR * a2_n + 3 * conv_n)]
    full_small = dict(
        w_gla_a2=a2_sh.reshape(4, nl, GLR, a2_n).transpose(1, 2, 0, 3).reshape(nl, GLR, 4 * a2_n),
        w_conv=conv_sh.reshape(4, nl, 3, conv_n).transpose(1, 2, 0, 3).reshape(nl, 3, 4 * conv_n))

    b_ada_sh = lax.dynamic_slice_in_dim(b_ada, k_me * ada_n, ada_n, axis=1)[:, None, :]
    mod_sh = ada_mod(c_all, w_ada, b_ada_sh)
    g2 = small_allgather(mod_sh.reshape(nl * 8, ada_n), name="gather_mod").reshape(4, 2, nl, 8, ada_n)[:, 0]
    mod_me = lax.dynamic_index_in_dim(g2, b_me, axis=2, keepdims=False).transpose(1, 0, 2).reshape(nl, 4 * ada_n)

    wfull = {n: w[n] for n in REPLICATED}
    wfull.update(full_small)
    wfull.update(gather_weights({n: w[n].astype(MMT) for n in BIG}, name="gather_weights"))
    wfull["w1"] = build_w1(wfull["w_in"], wfull["w_mg"])
    params = [layer_params(wfull, l) for l in range(nl)]
    mods = [[mod_me[l:l + 1, i * D:(i + 1) * D] for i in range(6)] for l in range(nl)]

    cosf, sinf = _rope_tables(seq)
    xs, saved = x[0], []
    for l in range(nl):
        xs, sv = layer_fwd(xs, mods[l], params[l], cosf, sinf)
        saved.append(sv)
    loss_part, dx = loss_and_grad(xs, loss_target[0], name="loss")
    loss = lax.psum(loss_part[0, 0], ALL_AXES)
    grads, dmods = [None] * nl, [None] * nl
    hl = nl // 2
    k_idx = jnp.reshape(k_me, (1,)).astype(jnp.int32)
    totals = {n: lax.empty((nl,) + _shard_shape(n), F32) for n in BIG}

    def backward_group(group, dx):
        stacks = {n: lax.empty((hl,) + TENSORS[n][2], MMT) for n in BIG if n != "w_br"}
        for l in reversed(range(hl * group, hl * (group + 1))):
            dx, g, dmods[l], stacks = layer_bwd(dx, mods[l], params[l], saved[l], cosf, sinf, stacks, l - hl * group)
            grads[l] = layer_grads(g)
        stacks["w_br"] = jnp.stack([grads[l]["w_br"].astype(MMT) for l in range(hl * group, hl * (group + 1))])
        from_sibling = pair_send(stacks, group, name=f"rs{group}_pair_send")
        return dx, {n: pair_add(stacks[n], from_sibling[n], tensor=n, name=f"rs{group}_pair_add_{n}") for n in BIG}

    def finish_group(group, chip_sum, from_chips, totals):
        totals = {n: chip_add(chip_sum[n], from_chips[n], k_idx, totals[n], group, tensor=n, name=f"rs{group}_chip_add_{n}")
                  for n in BIG}
        return pair_share(totals, group, name=f"rs{group}_pair_share")

    dx, chip_sum1 = backward_group(1, dx)
    send_sems, recv_sems, srcs, lands, token = chip_exchange_start(chip_sum1, 1, name="rs1_chip_exchange_start")
    dx, chip_sum0 = backward_group(0, dx + token[0, 0])
    chip_sum1, from_chips1 = chip_exchange_wait(send_sems, recv_sems, srcs, lands, dx, 1, name="rs1_chip_exchange_wait")
    totals = finish_group(1, chip_sum1, from_chips1, totals)
    totals = finish_group(0, chip_sum0, chip_exchange(chip_sum0, 0, name="rs0_chip_exchange"), totals)

    small_names = REPLICATED + ("w_gla_a2", "w_conv")
    small_shapes = [(nl, 6 * D)] + [(nl,) + grads[0][n].shape for n in small_names]
    vec = _flat_rows([jnp.concatenate(dmods, axis=0)] + [jnp.stack([grads[l][n] for l in range(nl)]) for n in small_names])
    gs = small_allgather(vec, name="gather_small_grads").reshape(8, vec.shape[0], HD)
    summed = _unflat(sum_devices(gs), small_shapes)
    grad = dict(zip(small_names, summed[1:]))
    grad["b_ada"] = summed[0]
    grad["w_gla_a2"] = lax.dynamic_slice_in_dim(grad["w_gla_a2"], k_me * a2_n, a2_n, axis=2)
    grad["w_conv"] = lax.dynamic_slice_in_dim(grad["w_conv"], k_me * conv_n, conv_n, axis=2)
    dmod_all = gs[:, :nl * 6 * D // HD].reshape(8, nl, 6 * D)
    dmod_sh = lax.dynamic_slice_in_dim(dmod_all, k_me * ada_n, ada_n, axis=2).transpose(1, 0, 2)
    grad["w_ada"] = ada_dw(c_all, dmod_sh)

    grad.update(totals)

    delta, new_m, new_v = {}, {}, {}
    for n, block in ADAM_BLOCKS.items():
        delta[n], new_m[n], new_v[n] = adamw(w[n], grad[n], m[n], v[n], block=block, name="adamw_" + n)
    rest = [n for n in WEIGHTS if n not in ADAM_BLOCKS]
    shapes = [w[n].shape for n in rest]
    flat = [_flat_rows([t[n] for n in rest]) for t in (w, grad, m, v)]
    outs = adamw(*flat, block=flat[0].shape, name="adamw_small")
    for t, o in zip((delta, new_m, new_v), outs):
        t.update(zip(rest, _unflat(o, shapes)))

    return (loss, dx[None], *[grad[n] for n in WEIGHTS], *[delta[n] for n in WEIGHTS], *[new_m[n] for n in WEIGHTS],
            *[new_v[n] for n in WEIGHTS])
```

```python
import functools

import numpy as np
import jax
import jax.numpy as jnp
from jax import lax
from jax.experimental import pallas as pl
from jax.experimental.pallas import tpu as pltpu

F32 = jnp.float32
MMT = jnp.bfloat16
HI = lax.Precision.HIGHEST

D = 1024
DEPTH = 4
NH = 4
HD = 128
BW = NH * HD
CH = 64
GDK = 64
GLR = 16
DFF = 2816
EPS = 1e-6
ROPE_BASE = 10000.0

GP, RQ, RK, RV, RG, GQ, GK, GV, GG, FQ, FK, FV, LR, FF = (
    0, 3072, 3584, 4096, 4608, 5120, 5376, 5632, 6144, 6656, 7168, 7680, 8192, 8320)
NZZ = 8448
IN_W = 5140

VMEM_LIMIT = 56 * 1024 * 1024

ADAM_LR, ADAM_B1, ADAM_B2, ADAM_EPS, ADAM_WD, ADAM_STEP = 0.001, 0.9, 0.999, 1e-08, 0.01, 10


def _cparams(sem=None):
    return pltpu.CompilerParams(dimension_semantics=sem, vmem_limit_bytes=VMEM_LIMIT)


def _sds(shape, dtype):
    return jax.ShapeDtypeStruct(tuple(shape), dtype)


def _dot(a, b, precision=None):
    return lax.dot_general(a, b, (((1,), (0,)), ((), ())), precision=precision, preferred_element_type=F32)


def _dot_nt(a, b, precision=None):
    return lax.dot_general(a, b, (((1,), (1,)), ((), ())), precision=precision, preferred_element_type=F32)


def _dot_tn(a, b, precision=None):
    return lax.dot_general(a, b, (((0,), (0,)), ((), ())), precision=precision, preferred_element_type=F32)


def _silu(x):
    return x * jax.nn.sigmoid(x)


def _log_sigmoid(x):
    return jnp.minimum(x, 0.0) - jnp.log(1.0 + jnp.exp(jnp.minimum(x, -x)))


@jax.custom_vjp
def _swap_halves(x):
    return pltpu.roll(x, HD // 2, 1)


_swap_halves.defvjp(lambda x: (_swap_halves(x), None), lambda _, g: (_swap_halves(g),))


@jax.custom_vjp
def _bdot(a, b):
    return _dot(a.astype(MMT), b.astype(MMT))


@jax.custom_vjp
def _bdot_nt(a, b):
    return _dot_nt(a.astype(MMT), b.astype(MMT))


@jax.custom_vjp
def _bdot_tn(a, b):
    return _dot_tn(a.astype(MMT), b.astype(MMT))


_bdot.defvjp(lambda a, b: (_bdot(a, b), (a, b)), lambda r, g: (_bdot_nt(g, r[1]), _bdot_tn(r[0], g)))
_bdot_nt.defvjp(lambda a, b: (_bdot_nt(a, b), (a, b)), lambda r, g: (_bdot(g, r[1]), _bdot_tn(g, r[0])))
_bdot_tn.defvjp(lambda a, b: (_bdot_tn(a, b), (a, b)), lambda r, g: (_bdot_nt(r[1], g), _bdot(r[0], g)))


def _stacked(blk, idx, layer):
    if layer is None:
        return pl.BlockSpec(blk, idx)
    return pl.BlockSpec((None,) + blk, lambda i, j: (layer,) + idx(i, j))


def mm_nn(a, b, *, tm, tn, out_dtype, name, layer=None):
    m, k = a.shape
    n = b.shape[-1]

    def body(a_ref, b_ref, o_ref):
        o_ref[...] = _dot(a_ref[...], b_ref[...]).astype(o_ref.dtype)

    return pl.pallas_call(
        body, grid=(m // tm, n // tn),
        in_specs=[pl.BlockSpec((tm, k), lambda i, j: (i, 0)), _stacked((k, tn), lambda i, j: (0, j), layer)],
        out_specs=pl.BlockSpec((tm, tn), lambda i, j: (i, j)),
        out_shape=_sds((m, n), out_dtype), compiler_params=_cparams(("parallel", "parallel")), name=name)(a, b)


def mm_nn_residual(a, b, res, gate, *, tm, tn, name, layer=None):
    m, k = a.shape
    n = b.shape[-1]

    def body(a_ref, b_ref, r_ref, g_ref, x_ref, y_ref):
        acc = _dot(a_ref[...], b_ref[...])
        y_ref[...] = acc
        x_ref[...] = r_ref[...] + g_ref[...] * acc

    return pl.pallas_call(
        body, grid=(m // tm, n // tn),
        in_specs=[pl.BlockSpec((tm, k), lambda i, j: (i, 0)), _stacked((k, tn), lambda i, j: (0, j), layer),
                  pl.BlockSpec((tm, tn), lambda i, j: (i, j)), pl.BlockSpec((1, tn), lambda i, j: (0, j))],
        out_specs=[pl.BlockSpec((tm, tn), lambda i, j: (i, j)), pl.BlockSpec((tm, tn), lambda i, j: (i, j))],
        out_shape=[_sds((m, n), F32), _sds((m, n), F32)],
        compiler_params=_cparams(("parallel", "parallel")), name=name)(a, b, res, gate)


def mm_nt(a, b, *, tm, tn, out_dtype, name, layer=None):
    m, k = a.shape
    n = b.shape[-2]

    def body(a_ref, b_ref, o_ref):
        o_ref[...] = _dot_nt(a_ref[...], b_ref[...]).astype(o_ref.dtype)

    return pl.pallas_call(
        body, grid=(m // tm, n // tn),
        in_specs=[pl.BlockSpec((tm, k), lambda i, j: (i, 0)), _stacked((tn, k), lambda i, j: (j, 0), layer)],
        out_specs=pl.BlockSpec((tm, tn), lambda i, j: (i, j)),
        out_shape=_sds((m, n), out_dtype), compiler_params=_cparams(("parallel", "parallel")), name=name)(a, b)


def mm_tn(a, b, *, tm, tn, out_dtype, name, col0=0, ncols=None, stack=None, layer=None):
    s, m = a.shape
    n = b.shape[1] - col0 if ncols is None else ncols
    c0 = col0 // tn

    def body(a_ref, b_ref, *rest):
        o_ref = rest[-1]
        o_ref[...] = _dot_tn(a_ref[...], b_ref[...]).astype(o_ref.dtype)

    in_specs = [pl.BlockSpec((s, tm), lambda i, j: (0, i)), pl.BlockSpec((s, tn), lambda i, j: (0, c0 + j))]
    if stack is None:
        return pl.pallas_call(
            body, grid=(m // tm, n // tn), in_specs=in_specs, out_specs=pl.BlockSpec((tm, tn), lambda i, j: (i, j)),
            out_shape=_sds((m, n), out_dtype), compiler_params=_cparams(("parallel", "parallel")), name=name)(a, b)
    return pl.pallas_call(
        body, grid=(m // tm, n // tn), in_specs=in_specs + [pl.BlockSpec(memory_space=pl.ANY)],
        out_specs=pl.BlockSpec((None, tm, tn), lambda i, j: (layer, i, j)),
        out_shape=_sds(stack.shape, stack.dtype), input_output_aliases={2: 0},
        compiler_params=_cparams(("parallel", "parallel")), name=name)(a, b, stack)


def _row_tile(s):
    return min(256, s)


def _norm_mod_f(x, g, scale, shift):
    r = lax.rsqrt(jnp.mean(x * x, axis=-1, keepdims=True) + EPS)
    return (x * r * g) * (1.0 + scale) + shift


def norm_mod(x, g, scale, shift, *, name):
    s = x.shape[0]
    t = _row_tile(s)

    def body(x_ref, g_ref, sc_ref, sh_ref, o_ref):
        o_ref[...] = _norm_mod_f(x_ref[...], g_ref[...], sc_ref[...], sh_ref[...]).astype(o_ref.dtype)

    vec = pl.BlockSpec((1, D), lambda i: (0, 0))
    return pl.pallas_call(
        body, grid=(s // t,), in_specs=[pl.BlockSpec((t, D), lambda i: (i, 0)), vec, vec, vec],
        out_specs=pl.BlockSpec((t, D), lambda i: (i, 0)), out_shape=_sds((s, D), MMT),
        compiler_params=_cparams(("parallel",)), name=name)(x, g, scale, shift)


def norm_mod_bwd(x, dh, dres, g, scale, shift, *, name):
    s = x.shape[0]
    t = _row_tile(s)

    def body(x_ref, dh_ref, dr_ref, g_ref, sc_ref, sh_ref, dx_ref, dg_ref, dsc_ref, dsh_ref):
        @pl.when(pl.program_id(0) == 0)
        def _():
            dg_ref[...] = jnp.zeros_like(dg_ref)
            dsc_ref[...] = jnp.zeros_like(dsc_ref)
            dsh_ref[...] = jnp.zeros_like(dsh_ref)

        _, vjp = jax.vjp(_norm_mod_f, x_ref[...], g_ref[...], sc_ref[...], sh_ref[...])
        dx, dg, dsc, dsh = vjp(dh_ref[...])
        dx_ref[...] = dr_ref[...] + dx
        dg_ref[...] += dg
        dsc_ref[...] += dsc
        dsh_ref[...] += dsh

    row = pl.BlockSpec((t, D), lambda i: (i, 0))
    vec = pl.BlockSpec((1, D), lambda i: (0, 0))
    return pl.pallas_call(
        body, grid=(s // t,), in_specs=[row, row, row, vec, vec, vec], out_specs=[row, vec, vec, vec],
        out_shape=[_sds((s, D), F32)] + [_sds((1, D), F32)] * 3,
        compiler_params=_cparams(("arbitrary",)), name=name)(x, dh, dres, g, scale, shift)


def gate_bwd(dx, y, gate, *, name):
    s = dx.shape[0]
    t = _row_tile(s)

    def body(dx_ref, y_ref, g_ref, dy_ref, dg_ref):
        @pl.when(pl.program_id(0) == 0)
        def _():
            dg_ref[...] = jnp.zeros_like(dg_ref)

        dxv = dx_ref[...]
        dy_ref[...] = (g_ref[...] * dxv).astype(dy_ref.dtype)
        dg_ref[...] += jnp.sum(dxv * y_ref[...], axis=0, keepdims=True)

    row = pl.BlockSpec((t, D), lambda i: (i, 0))
    vec = pl.BlockSpec((1, D), lambda i: (0, 0))
    return pl.pallas_call(
        body, grid=(s // t,), in_specs=[row, row, vec], out_specs=[row, vec],
        out_shape=[_sds((s, D), MMT), _sds((1, D), F32)],
        compiler_params=_cparams(("arbitrary",)), name=name)(dx, y, gate)


def loss_and_grad(xf, target, *, name):
    s = xf.shape[0]
    t = _row_tile(s)

    def body(x_ref, t_ref, l_ref, dx_ref):
        @pl.when(pl.program_id(0) == 0)
        def _():
            l_ref[...] = jnp.zeros_like(l_ref)

        e = x_ref[...] - t_ref[...]
        dx_ref[...] = e * (1.0 / D)
        l_ref[...] += 0.5 * jnp.sum(jnp.sum(e * e, axis=1, keepdims=True), axis=0, keepdims=True) * (1.0 / D)

    row = pl.BlockSpec((t, D), lambda i: (i, 0))
    return pl.pallas_call(
        body, grid=(s // t,), in_specs=[row, row], out_specs=[pl.BlockSpec((1, 1), lambda i: (0, 0)), row],
        out_shape=[_sds((1, 1), F32), _sds((s, D), F32)],
        compiler_params=_cparams(("arbitrary",)), name=name)(xf, target)


def _ret_consts():
    log_g = np.log1p(-np.exp2(-5.0 - np.arange(NH, dtype=np.float32))).astype(np.float32)
    idx = np.arange(CH, dtype=np.float32)
    d_intra = np.exp(np.abs(idx[:, None] - idx[None, :])[None] * log_g[:, None, None]).astype(np.float32)
    k_w = np.exp((CH - 1.0 - idx)[None, :] * log_g[:, None]).astype(np.float32)
    q_w = np.exp((idx + 1.0)[None, :] * log_g[:, None]).astype(np.float32)
    g_chunk = [float(v) for v in np.exp(np.float32(CH) * log_g).astype(np.float32)]
    bc = lambda a: np.ascontiguousarray(np.broadcast_to(a[:, :, None], (NH, CH, HD)))
    return jnp.asarray(d_intra), jnp.asarray(bc(k_w)), jnp.asarray(bc(q_w)), g_chunk


def _rope_tables(s):
    half = HD // 2
    inv_freq = (ROPE_BASE ** (-np.arange(half, dtype=np.float64) / half)).astype(np.float32)
    ang = (np.arange(s, dtype=np.float32)[:, None] * inv_freq[None, :]).astype(np.float64)
    cos, sin = np.cos(ang).astype(np.float32), np.sin(ang).astype(np.float32)
    return jnp.asarray(np.concatenate([cos, cos], axis=1)), jnp.asarray(np.concatenate([-sin, sin], axis=1))


def _ret_chunk(qs, ks, vs, rs, cos, sin, dintra, kw, qw, g_chunk):
    outs, rn = [], []
    for h in range(NH):
        q = qs[h] * cos + _swap_halves(qs[h]) * sin
        k = (ks[h] * cos + _swap_halves(ks[h]) * sin) * (HD ** -0.5)
        sc = _bdot_nt(q, k) * dintra[h]
        outs.append(_bdot(sc, vs[h]) + _bdot(q * qw[h], rs[h]))
        rn.append(g_chunk[h] * rs[h] + _bdot_tn(k * kw[h], vs[h]))
    return outs, rn


def _heads(x):
    return [x[:, h * HD:(h + 1) * HD] for h in range(NH)]


def retention_fwd(zz, cosf, sinf, *, name):
    s = zz.shape[0]
    n = s // CH
    dintra, kw, qw, g_chunk = _ret_consts()

    def body(q_ref, k_ref, v_ref, c_ref, s_ref, di_ref, kw_ref, qw_ref, o_ref, rp_ref, r_scr):
        @pl.when(pl.program_id(0) == 0)
        def _():
            r_scr[...] = jnp.zeros_like(r_scr)

        rprev = r_scr[...]
        rp_ref[0] = rprev
        outs, rn = _ret_chunk(_heads(q_ref[...]), _heads(k_ref[...]), _heads(v_ref[...]),
                              [rprev[h * HD:(h + 1) * HD] for h in range(NH)], c_ref[...], s_ref[...],
                              [di_ref[h] for h in range(NH)], [kw_ref[h] for h in range(NH)],
                              [qw_ref[h] for h in range(NH)], g_chunk)
        o_ref[...] = jnp.concatenate(outs, axis=1)
        r_scr[...] = jnp.concatenate(rn, axis=0)

    col = lambda c: pl.BlockSpec((CH, BW), lambda i: (i, c // BW))
    tab = pl.BlockSpec((CH, HD), lambda i: (i, 0))
    cst = lambda shp: pl.BlockSpec(shp, lambda i: (0,) * len(shp))
    return pl.pallas_call(
        body, grid=(n,),
        in_specs=[col(RQ), col(RK), col(RV), tab, tab, cst((NH, CH, CH)), cst((NH, CH, HD)), cst((NH, CH, HD))],
        out_specs=[pl.BlockSpec((CH, BW), lambda i: (i, 0)), pl.BlockSpec((1, BW, HD), lambda i: (i, 0, 0))],
        out_shape=[_sds((s, BW), F32), _sds((n, BW, HD), F32)],
        scratch_shapes=[pltpu.VMEM((BW, HD), F32)],
        compiler_params=_cparams(("arbitrary",)), name=name)(zz, zz, zz, cosf, sinf, dintra, kw, qw)


def retention_bwd(zz, cosf, sinf, rprev, do, *, name):
    s = zz.shape[0]
    n = s // CH
    dintra, kw, qw, g_chunk = _ret_consts()

    def body(q_ref, k_ref, v_ref, c_ref, s_ref, di_ref, kw_ref, qw_ref, rp_ref, do_ref, dz_ref, dr_scr):
        @pl.when(pl.program_id(0) == 0)
        def _():
            dr_scr[...] = jnp.zeros_like(dr_scr)

        rprev_v = rp_ref[0]
        f = functools.partial(_ret_chunk, cos=c_ref[...], sin=s_ref[...],
                              dintra=[di_ref[h] for h in range(NH)], kw=[kw_ref[h] for h in range(NH)],
                              qw=[qw_ref[h] for h in range(NH)], g_chunk=g_chunk)
        _, vjp = jax.vjp(f, _heads(q_ref[...]), _heads(k_ref[...]), _heads(v_ref[...]),
                         [rprev_v[h * HD:(h + 1) * HD] for h in range(NH)])
        dr = dr_scr[...]
        dq, dk, dv, drp = vjp((_heads(do_ref[...]), [dr[h * HD:(h + 1) * HD] for h in range(NH)]))
        dz_ref[...] = jnp.concatenate(dq + dk + dv, axis=1).astype(dz_ref.dtype)
        dr_scr[...] = jnp.concatenate(drp, axis=0)

    col = lambda c: pl.BlockSpec((CH, BW), lambda i: (n - 1 - i, c // BW))
    tab = pl.BlockSpec((CH, HD), lambda i: (n - 1 - i, 0))
    cst = lambda shp: pl.BlockSpec(shp, lambda i: (0,) * len(shp))
    return pl.pallas_call(
        body, grid=(n,),
        in_specs=[col(RQ), col(RK), col(RV), tab, tab, cst((NH, CH, CH)), cst((NH, CH, HD)), cst((NH, CH, HD)),
                  pl.BlockSpec((1, BW, HD), lambda i: (n - 1 - i, 0, 0)), pl.BlockSpec((CH, BW), lambda i: (n - 1 - i, 0))],
        out_specs=pl.BlockSpec((CH, 3 * BW), lambda i: (n - 1 - i, 0)),
        out_shape=_sds((s, 3 * BW), MMT),
        scratch_shapes=[pltpu.VMEM((BW, HD), F32)],
        compiler_params=_cparams(("arbitrary",)), name=name)(zz, zz, zz, cosf, sinf, dintra, kw, qw, rprev, do)


GKW = NH * GDK


def _gla_consts():
    tri = np.tril(np.ones((CH, CH), np.float32))
    mask_t = np.zeros((BW, GKW), np.float32)
    for h in range(NH):
        mask_t[h * HD:(h + 1) * HD, h * GDK:(h + 1) * GDK] = 1.0
    return jnp.asarray(tri), jnp.asarray(mask_t)


def _gla_chunk(q, k, v, glr, w_a2, b_a, st, tri, mask_t):
    la = _log_sigmoid(_bdot(glr, w_a2) + b_a) * (1.0 / 16.0)
    bc = _dot(tri, la, HI)
    be = jnp.sum(la, axis=0, keepdims=True)
    kv_t = _bdot_tn(v, k * jnp.exp(be - bc)) * mask_t
    sn = jnp.exp(be) * st + kv_t
    return _bdot_nt(q * (GDK ** -0.5), sn), sn


def gla_fwd(zz, w_a2p, b_a, *, name):
    s = zz.shape[0]
    n = s // CH
    tri, mask_t = _gla_consts()

    def body(q_ref, k_ref, v_ref, lr_ref, w_ref, b_ref, tri_ref, m_ref, o_ref, sp_ref, st_scr):
        @pl.when(pl.program_id(0) == 0)
        def _():
            st_scr[...] = jnp.zeros_like(st_scr)

        sp = st_scr[...]
        sp_ref[0] = sp
        o, sn = _gla_chunk(q_ref[...], k_ref[...], v_ref[...], lr_ref[...], w_ref[...], b_ref[...], sp,
                           tri_ref[...], m_ref[...])
        o_ref[...] = o
        st_scr[...] = sn

    cst = lambda shp: pl.BlockSpec(shp, lambda i: (0,) * len(shp))
    return pl.pallas_call(
        body, grid=(n,),
        in_specs=[pl.BlockSpec((CH, GKW), lambda i: (i, GQ // GKW)), pl.BlockSpec((CH, GKW), lambda i: (i, GK // GKW)),
                  pl.BlockSpec((CH, BW), lambda i: (i, GV // BW)), pl.BlockSpec((CH, HD), lambda i: (i, LR // HD)),
                  cst((HD, GKW)), cst((1, GKW)), cst((CH, CH)), cst((BW, GKW))],
        out_specs=[pl.BlockSpec((CH, BW), lambda i: (i, 0)), pl.BlockSpec((1, BW, GKW), lambda i: (i, 0, 0))],
        out_shape=[_sds((s, BW), F32), _sds((n, BW, GKW), F32)],
        scratch_shapes=[pltpu.VMEM((BW, GKW), F32)],
        compiler_params=_cparams(("arbitrary",)), name=name)(zz, zz, zz, zz, w_a2p, b_a, tri, mask_t)


def gla_bwd(zz, w_a2p, b_a, sprev, do, *, name):
    s = zz.shape[0]
    n = s // CH
    tri, mask_t = _gla_consts()

    def body(q_ref, k_ref, v_ref, lr_ref, w_ref, b_ref, tri_ref, m_ref, sp_ref, do_ref,
             dz_ref, dlr_ref, dw_ref, db_ref, ds_scr):
        @pl.when(pl.program_id(0) == 0)
        def _():
            ds_scr[...] = jnp.zeros_like(ds_scr)
            dw_ref[...] = jnp.zeros_like(dw_ref)
            db_ref[...] = jnp.zeros_like(db_ref)

        f = functools.partial(_gla_chunk, tri=tri_ref[...], mask_t=m_ref[...])
        _, vjp = jax.vjp(f, q_ref[...], k_ref[...], v_ref[...], lr_ref[...], w_ref[...], b_ref[...], sp_ref[0])
        dq, dk, dv, dlr, dw, db, dsp = vjp((do_ref[...], ds_scr[...]))
        dz_ref[...] = jnp.concatenate([dq, dk, dv], axis=1).astype(dz_ref.dtype)
        dlr_ref[...] = dlr.astype(dlr_ref.dtype)
        dw_ref[...] += dw
        db_ref[...] += db
        ds_scr[...] = dsp

    cst = lambda shp: pl.BlockSpec(shp, lambda i: (0,) * len(shp))
    r = lambda i: n - 1 - i
    return pl.pallas_call(
        body, grid=(n,),
        in_specs=[pl.BlockSpec((CH, GKW), lambda i: (r(i), GQ // GKW)), pl.BlockSpec((CH, GKW), lambda i: (r(i), GK // GKW)),
                  pl.BlockSpec((CH, BW), lambda i: (r(i), GV // BW)), pl.BlockSpec((CH, HD), lambda i: (r(i), LR // HD)),
                  cst((HD, GKW)), cst((1, GKW)), cst((CH, CH)), cst((BW, GKW)),
                  pl.BlockSpec((1, BW, GKW), lambda i: (r(i), 0, 0)), pl.BlockSpec((CH, BW), lambda i: (r(i), 0))],
        out_specs=[pl.BlockSpec((CH, 2 * GKW + BW), lambda i: (r(i), 0)), pl.BlockSpec((CH, HD), lambda i: (r(i), 0)),
                   cst((HD, GKW)), cst((1, GKW))],
        out_shape=[_sds((s, 2 * GKW + BW), MMT), _sds((s, HD), MMT), _sds((HD, GKW), F32), _sds((1, GKW), F32)],
        scratch_shapes=[pltpu.VMEM((BW, GKW), F32)],
        compiler_params=_cparams(("arbitrary",)), name=name)(zz, zz, zz, zz, w_a2p, b_a, tri, mask_t, sprev, do)


def _fox_pre_f(fqs, fks, ff, gq, gk, bf):
    def rms(x, g):
        return x * lax.rsqrt(jnp.mean(x * x, axis=-1, keepdims=True) + EPS) * g

    qn = [rms(x, gq) * (HD ** -0.5) for x in fqs]
    kn = [rms(x, gk) for x in fks]
    return qn, kn, _log_sigmoid(ff + bf)


def fox_pre(zz, gq, gk, bf, *, name):
    s = zz.shape[0]
    t = _row_tile(s)
    tri = jnp.asarray(np.tril(np.ones((t, t), np.float32)))

    def body(q_ref, k_ref, f_ref, gq_ref, gk_ref, b_ref, tri_ref, qn_ref, kn_ref, cum_ref, carry):
        @pl.when(pl.program_id(0) == 0)
        def _():
            carry[...] = jnp.zeros_like(carry)

        qn, kn, lf = _fox_pre_f(_heads(q_ref[...]), _heads(k_ref[...]), f_ref[...], gq_ref[...], gk_ref[...], b_ref[...])
        qn_ref[...] = jnp.concatenate(qn, axis=1).astype(qn_ref.dtype)
        kn_ref[...] = jnp.concatenate(kn, axis=1).astype(kn_ref.dtype)
        cum_ref[...] = _dot(tri_ref[...], lf, HI) + carry[...]
        carry[...] += jnp.sum(lf, axis=0, keepdims=True)

    vec = pl.BlockSpec((1, HD), lambda i: (0, 0))
    return pl.pallas_call(
        body, grid=(s // t,),
        in_specs=[pl.BlockSpec((t, BW), lambda i: (i, FQ // BW)), pl.BlockSpec((t, BW), lambda i: (i, FK // BW)),
                  pl.BlockSpec((t, HD), lambda i: (i, FF // HD)), vec, vec, vec, pl.BlockSpec((t, t), lambda i: (0, 0))],
        out_specs=[pl.BlockSpec((t, BW), lambda i: (i, 0)), pl.BlockSpec((t, BW), lambda i: (i, 0)),
                   pl.BlockSpec((t, HD), lambda i: (i, 0))],
        out_shape=[_sds((s, BW), MMT), _sds((s, BW), MMT), _sds((s, HD), F32)],
        scratch_shapes=[pltpu.VMEM((1, HD), F32)],
        compiler_params=_cparams(("arbitrary",)), name=name)(zz, zz, zz, gq, gk, bf, tri)


def fox_pre_bwd(zz, gq, gk, bf, dqn, dkn, dcum, *, name):
    s = zz.shape[0]
    t = _row_tile(s)
    nt = s // t
    triu = jnp.asarray(np.triu(np.ones((t, t), np.float32)))

    def body(q_ref, k_ref, f_ref, gq_ref, gk_ref, b_ref, tri_ref, dqn_ref, dkn_ref, dcum_ref,
             dz_ref, dff_ref, dgq_ref, dgk_ref, db_ref, carry):
        @pl.when(pl.program_id(0) == 0)
        def _():
            carry[...] = jnp.zeros_like(carry)
            dgq_ref[...] = jnp.zeros_like(dgq_ref)
            dgk_ref[...] = jnp.zeros_like(dgk_ref)
            db_ref[...] = jnp.zeros_like(db_ref)

        dcum_v = dcum_ref[...]
        dlf = _dot(tri_ref[...], dcum_v, HI) + carry[...]
        carry[...] += jnp.sum(dcum_v, axis=0, keepdims=True)
        _, vjp = jax.vjp(_fox_pre_f, _heads(q_ref[...]), _heads(k_ref[...]), f_ref[...], gq_ref[...], gk_ref[...], b_ref[...])
        dq, dk, dff, dgq, dgk, db = vjp((_heads(dqn_ref[...]), _heads(dkn_ref[...]), dlf))
        dz_ref[...] = jnp.concatenate(dq + dk, axis=1).astype(dz_ref.dtype)
        dff_ref[...] = dff.astype(dff_ref.dtype)
        dgq_ref[...] += dgq
        dgk_ref[...] += dgk
        db_ref[...] += db

    r = lambda i: nt - 1 - i
    vec = pl.BlockSpec((1, HD), lambda i: (0, 0))
    return pl.pallas_call(
        body, grid=(nt,),
        in_specs=[pl.BlockSpec((t, BW), lambda i: (r(i), FQ // BW)), pl.BlockSpec((t, BW), lambda i: (r(i), FK // BW)),
                  pl.BlockSpec((t, HD), lambda i: (r(i), FF // HD)), vec, vec, vec, pl.BlockSpec((t, t), lambda i: (0, 0)),
                  pl.BlockSpec((t, BW), lambda i: (r(i), 0)), pl.BlockSpec((t, BW), lambda i: (r(i), 0)),
                  pl.BlockSpec((t, HD), lambda i: (r(i), 0))],
        out_specs=[pl.BlockSpec((t, 2 * BW), lambda i: (r(i), 0)), pl.BlockSpec((t, HD), lambda i: (r(i), 0)), vec, vec, vec],
        out_shape=[_sds((s, 2 * BW), MMT), _sds((s, HD), MMT), _sds((1, HD), F32), _sds((1, HD), F32), _sds((1, HD), F32)],
        scratch_shapes=[pltpu.VMEM((1, HD), F32)],
        compiler_params=_cparams(("arbitrary",)), name=name)(zz, zz, zz, gq, gk, bf, triu, dqn, dkn, dcum)


def _fox_blocks(s):
    return min(256, s), min(512, s)


NEG = -1e30


def fox_attn_fwd(qn, kn, zz, cum_col, cum_row, *, name):
    s = qn.shape[0]
    bq, bk = _fox_blocks(s)

    def body(q_ref, k_ref, v_ref, cc_ref, cr_ref, o_ref, lse_ref):
        qi = pl.program_id(1)
        q = q_ref[...]
        cq = cc_ref[...]
        rows = qi * bq + lax.broadcasted_iota(jnp.int32, (bq, bk), 0)
        cols0 = lax.broadcasted_iota(jnp.int32, (bq, bk), 1)

        def step(j, carry):
            m, l, acc = carry
            off = pl.multiple_of(j * bk, bk)
            k = k_ref[pl.ds(off, bk), :]
            v = v_ref[pl.ds(off, bk), :].astype(MMT)
            sc = _dot_nt(q, k) + cq - cr_ref[pl.ds(j, 1), :]
            sc = jnp.where(rows >= cols0 + j * bk, sc, NEG)
            m_new = jnp.maximum(m, jnp.max(sc, axis=1, keepdims=True))
            alpha = jnp.exp(m - m_new)
            p = jnp.exp(sc - m_new)
            return m_new, alpha * l + jnp.sum(p, axis=1, keepdims=True), alpha * acc + _dot(p.astype(MMT), v)

        nk = ((qi + 1) * bq + bk - 1) // bk
        m, l, acc = lax.fori_loop(0, nk, step, (jnp.full((bq, 1), NEG, F32), jnp.zeros((bq, 1), F32),
                                                jnp.zeros((bq, HD), F32)))
        o_ref[...] = acc / l
        lse_ref[...] = m + jnp.log(l)

    return pl.pallas_call(
        body, grid=(NH, s // bq),
        in_specs=[pl.BlockSpec((bq, HD), lambda h, i: (i, h)), pl.BlockSpec((s, HD), lambda h, i: (0, h)),
                  pl.BlockSpec((s, HD), lambda h, i: (0, FV // HD + h)),
                  pl.BlockSpec((None, bq, 1), lambda h, i: (h, i, 0)), pl.BlockSpec((None, s // bk, bk), lambda h, i: (h, 0, 0))],
        out_specs=[pl.BlockSpec((bq, HD), lambda h, i: (i, h)), pl.BlockSpec((None, bq, 1), lambda h, i: (h, i, 0))],
        out_shape=[_sds((s, BW), F32), _sds((NH, s, 1), F32)],
        compiler_params=_cparams(("parallel", "parallel")), name=name)(qn, kn, zz, cum_col, cum_row)


def fox_attn_bwd(qn, kn, zz, cum_col, cum_row, lse, do, *, name):
    s = qn.shape[0]
    bq, bk = _fox_blocks(s)
    nkc = s // bk

    def body(q_ref, k_ref, v_ref, cc_ref, cr_ref, lse_ref, do_ref, dq_ref, dk_ref, dv_ref, dc_ref, p_scr, dp_scr):
        qi = pl.program_id(1)

        @pl.when(qi == 0)
        def _():
            dk_ref[...] = jnp.zeros_like(dk_ref)
            dv_ref[...] = jnp.zeros_like(dv_ref)
            dc_ref[...] = jnp.zeros_like(dc_ref)

        q = q_ref[...]
        dob = do_ref[...].astype(MMT)
        cq = cc_ref[...]
        lse_v = lse_ref[...]
        rows = qi * bq + lax.broadcasted_iota(jnp.int32, (bq, bk), 0)
        cols0 = lax.broadcasted_iota(jnp.int32, (bq, bk), 1)
        nk = ((qi + 1) * bq + bk - 1) // bk

        def probs(j, delta):
            off = pl.multiple_of(j * bk, bk)
            sc = _dot_nt(q, k_ref[pl.ds(off, bk), :]) + cq - cr_ref[pl.ds(j, 1), :]
            p = jnp.where(rows >= cols0 + j * bk, jnp.exp(sc - lse_v), 0.0)
            dp = _dot_nt(dob, v_ref[pl.ds(off, bk), :].astype(MMT))
            p_scr[j] = p
            dp_scr[j] = dp
            return delta + jnp.sum(p * dp, axis=1, keepdims=True)

        delta = lax.fori_loop(0, nk, probs, jnp.zeros((bq, 1), F32))

        def grads(j, dq):
            off = pl.multiple_of(j * bk, bk)
            p = p_scr[j]
            ds = p * (dp_scr[j] - delta)
            dsm = ds.astype(MMT)
            dv_ref[pl.ds(off, bk), :] += _dot_tn(p.astype(MMT), dob)
            dk_ref[pl.ds(off, bk), :] += _dot_tn(dsm, q)
            dc_ref[pl.ds(j, 1), :] -= jnp.sum(ds, axis=0, keepdims=True)
            return dq + _dot(dsm, k_ref[pl.ds(off, bk), :])

        dq_ref[...] = lax.fori_loop(0, nk, grads, jnp.zeros((bq, HD), F32))

    full = lambda c0=0: pl.BlockSpec((s, HD), lambda h, i: (0, c0 + h))
    blk = lambda: pl.BlockSpec((bq, HD), lambda h, i: (i, h))
    colv = lambda: pl.BlockSpec((None, bq, 1), lambda h, i: (h, i, 0))
    rowv = lambda: pl.BlockSpec((None, nkc, bk), lambda h, i: (h, 0, 0))
    return pl.pallas_call(
        body, grid=(NH, s // bq),
        in_specs=[blk(), full(), full(FV // HD), colv(), rowv(), colv(), blk()],
        out_specs=[blk(), full(), full(), rowv()],
        out_shape=[_sds((s, BW), F32), _sds((s, BW), F32), _sds((s, BW), F32), _sds((NH, nkc, bk), F32)],
        scratch_shapes=[pltpu.VMEM((nkc, bq, bk), F32), pltpu.VMEM((nkc, bq, bk), F32)],
        compiler_params=_cparams(("parallel", "arbitrary")), name=name)(qn, kn, zz, cum_col, cum_row, lse, do)


def _branch_f(rets, rgs, glas, ggs, ret_g, gla_g):
    out_r, out_g = [], []
    for h in range(NH):
        xc = rets[h] - jnp.mean(rets[h], axis=-1, keepdims=True)
        y = xc * lax.rsqrt(jnp.mean(xc * xc, axis=-1, keepdims=True) + EPS) * ret_g[h]
        out_r.append(_silu(rgs[h]) * y)
        x = glas[h]
        y = x * lax.rsqrt(jnp.mean(x * x, axis=-1, keepdims=True) + EPS) * gla_g
        out_g.append(_silu(ggs[h]) * y)
    return out_r, out_g


def _w_br_spec(layer):
    return pl.BlockSpec((None, 3, BW, D), lambda i: (layer, 0, 0, 0))


def mix_fwd(ret_raw, gla_raw, fox_o, zz, ret_g, gla_g, b_mg, w_br, *, name, layer):
    s = zz.shape[0]
    t = _row_tile(s)

    def body(r_ref, g_ref, f_ref, rg_ref, gg_ref, gp_ref, rgn_ref, ggn_ref, bmg_ref, w_ref, o_ref):
        rgn = rgn_ref[...]
        br_r, br_g = _branch_f(_heads(r_ref[...]), _heads(rg_ref[...]), _heads(g_ref[...]), _heads(gg_ref[...]),
                               _heads(rgn), ggn_ref[...])
        brs = [jnp.concatenate(br_r, axis=1), jnp.concatenate(br_g, axis=1), f_ref[...]]
        acc = jnp.zeros((t, D), F32)
        for b in range(3):
            gate = jax.nn.sigmoid(gp_ref[:, b * D:(b + 1) * D] + bmg_ref[:, b * D:(b + 1) * D])
            acc = acc + gate * _dot(brs[b].astype(MMT), w_ref[b])
        o_ref[...] = acc.astype(o_ref.dtype)

    row = lambda w, c=0: pl.BlockSpec((t, w), lambda i: (i, c // w))
    cst = lambda shp: pl.BlockSpec(shp, lambda i: (0,) * len(shp))
    return pl.pallas_call(
        body, grid=(s // t,),
        in_specs=[row(BW), row(BW), row(BW), row(BW, RG), row(BW, GG), row(3 * D, GP), cst((1, BW)), cst((1, HD)),
                  cst((1, 3 * D)), _w_br_spec(layer)],
        out_specs=row(D), out_shape=_sds((s, D), MMT),
        compiler_params=_cparams(("parallel",)), name=name)(ret_raw, gla_raw, fox_o, zz, zz, zz, ret_g, gla_g, b_mg, w_br)


def mix_bwd(ret_raw, gla_raw, fox_o, zz, ret_g, gla_g, b_mg, w_br, dmi, *, name, layer):
    s = zz.shape[0]
    t = _row_tile(s)

    def body(r_ref, g_ref, f_ref, rg_ref, gg_ref, gp_ref, rgn_ref, ggn_ref, bmg_ref, w_ref, dmi_ref,
             dr_ref, dg_ref, df_ref, drg_ref, dgg_ref, dgp_ref, dw_ref, drgn_ref, dggn_ref, dbmg_ref):
        @pl.when(pl.program_id(0) == 0)
        def _():
            dw_ref[...] = jnp.zeros_like(dw_ref)
            drgn_ref[...] = jnp.zeros_like(drgn_ref)
            dggn_ref[...] = jnp.zeros_like(dggn_ref)
            dbmg_ref[...] = jnp.zeros_like(dbmg_ref)

        (br_r, br_g), vjp = jax.vjp(_branch_f, _heads(r_ref[...]), _heads(rg_ref[...]), _heads(g_ref[...]),
                                    _heads(gg_ref[...]), _heads(rgn_ref[...]), ggn_ref[...])
        brs = [jnp.concatenate(br_r, axis=1).astype(MMT), jnp.concatenate(br_g, axis=1).astype(MMT),
               f_ref[...].astype(MMT)]
        dmi_v = dmi_ref[...].astype(F32)
        dbr = []
        for b in range(3):
            w = w_ref[b]
            ybr = _dot(brs[b], w)
            gate = jax.nn.sigmoid(gp_ref[:, b * D:(b + 1) * D] + bmg_ref[:, b * D:(b + 1) * D])
            dgp = dmi_v * ybr * gate * (1.0 - gate)
            dgp_ref[:, b * D:(b + 1) * D] = dgp.astype(dgp_ref.dtype)
            dbmg_ref[:, b * D:(b + 1) * D] += jnp.sum(dgp, axis=0, keepdims=True)
            dy = (dmi_v * gate).astype(MMT)
            dw_ref[b] += _dot_tn(brs[b], dy)
            dbr.append(_dot_nt(dy, w))
        dr, drg, dg, dgg, drgn, dggn = vjp((_heads(dbr[0]), _heads(dbr[1])))
        dr_ref[...] = jnp.concatenate(dr, axis=1)
        dg_ref[...] = jnp.concatenate(dg, axis=1)
        df_ref[...] = dbr[2]
        drg_ref[...] = jnp.concatenate(drg, axis=1).astype(drg_ref.dtype)
        dgg_ref[...] = jnp.concatenate(dgg, axis=1).astype(dgg_ref.dtype)
        drgn_ref[...] += jnp.concatenate(drgn, axis=1)
        dggn_ref[...] += dggn

    row = lambda w, c=0: pl.BlockSpec((t, w), lambda i: (i, c // w))
    cst = lambda shp: pl.BlockSpec(shp, lambda i: (0,) * len(shp))
    return pl.pallas_call(
        body, grid=(s // t,),
        in_specs=[row(BW), row(BW), row(BW), row(BW, RG), row(BW, GG), row(3 * D, GP), cst((1, BW)), cst((1, HD)),
                  cst((1, 3 * D)), _w_br_spec(layer), row(D)],
        out_specs=[row(BW), row(BW), row(BW), row(BW), row(BW), row(3 * D), cst((3, BW, D)), cst((1, BW)), cst((1, HD)),
                   cst((1, 3 * D))],
        out_shape=[_sds((s, BW), F32)] * 3 + [_sds((s, BW), MMT)] * 2 + [_sds((s, 3 * D), MMT), _sds((3, BW, D), F32),
                                                                          _sds((1, BW), F32), _sds((1, HD), F32),
                                                                          _sds((1, 3 * D), F32)],
        compiler_params=_cparams(("arbitrary",)), name=name)(ret_raw, gla_raw, fox_o, zz, zz, zz, ret_g, gla_g, b_mg, w_br, dmi)


CT = 256


def _shift_down(x, k, rows):
    return jnp.where(rows >= k, pltpu.roll(x, k, 0), 0.0)


def _shift_up(x, k, rows, s):
    return jnp.where(rows < s - k, pltpu.roll(x, s - k, 0), 0.0)


def conv_fwd(ug, w_conv, b_conv, *, name):
    s = ug.shape[0]
    nt = DFF // CT

    def body(u_ref, g_ref, w_ref, b_ref, a_ref):
        u = u_ref[...]
        rows = lax.broadcasted_iota(jnp.int32, u.shape, 0)
        uc = b_ref[...] + w_ref[0:1, :] * _shift_down(u, 2, rows) + w_ref[1:2, :] * _shift_down(u, 1, rows) + w_ref[2:3, :] * u
        a_ref[...] = (_silu(uc) * g_ref[...]).astype(a_ref.dtype)

    return pl.pallas_call(
        body, grid=(nt,),
        in_specs=[pl.BlockSpec((s, CT), lambda j: (0, j)), pl.BlockSpec((s, CT), lambda j: (0, nt + j)),
                  pl.BlockSpec((3, CT), lambda j: (0, j)), pl.BlockSpec((1, CT), lambda j: (0, j))],
        out_specs=pl.BlockSpec((s, CT), lambda j: (0, j)), out_shape=_sds((s, DFF), MMT),
        compiler_params=_cparams(("parallel",)), name=name)(ug, ug, w_conv, b_conv)


def conv_bwd(ug, w_conv, b_conv, da, *, name):
    s = ug.shape[0]
    nt = DFF // CT

    def body(u_ref, g_ref, w_ref, b_ref, da_ref, du_ref, dg_ref, dw_ref, db_ref):
        u = u_ref[...]
        rows = lax.broadcasted_iota(jnp.int32, u.shape, 0)
        u2, u1 = _shift_down(u, 2, rows), _shift_down(u, 1, rows)
        uc = b_ref[...] + w_ref[0:1, :] * u2 + w_ref[1:2, :] * u1 + w_ref[2:3, :] * u
        sg = jax.nn.sigmoid(uc)
        da_v = da_ref[...]
        dg_ref[...] = (da_v * uc * sg).astype(dg_ref.dtype)
        duc = da_v * g_ref[...] * sg * (1.0 + uc * (1.0 - sg))
        du = w_ref[2:3, :] * duc + w_ref[1:2, :] * _shift_up(duc, 1, rows, s) + w_ref[0:1, :] * _shift_up(duc, 2, rows, s)
        du_ref[...] = du.astype(du_ref.dtype)
        dw_ref[0:1, :] = jnp.sum(duc * u2, axis=0, keepdims=True)
        dw_ref[1:2, :] = jnp.sum(duc * u1, axis=0, keepdims=True)
        dw_ref[2:3, :] = jnp.sum(duc * u, axis=0, keepdims=True)
        db_ref[...] = jnp.sum(duc, axis=0, keepdims=True)

    col = lambda: pl.BlockSpec((s, CT), lambda j: (0, j))
    return pl.pallas_call(
        body, grid=(nt,),
        in_specs=[col(), pl.BlockSpec((s, CT), lambda j: (0, nt + j)), pl.BlockSpec((3, CT), lambda j: (0, j)),
                  pl.BlockSpec((1, CT), lambda j: (0, j)), col()],
        out_specs=[col(), col(), pl.BlockSpec((3, CT), lambda j: (0, j)), pl.BlockSpec((1, CT), lambda j: (0, j))],
        out_shape=[_sds((s, DFF), MMT), _sds((s, DFF), MMT), _sds((3, DFF), F32), _sds((1, DFF), F32)],
        compiler_params=_cparams(("parallel",)), name=name)(ug, ug, w_conv, b_conv, da)


def _tiles(s):
    return min(1024, s)


def layer_fwd(x, mod, p, cosf, sinf):
    s = x.shape[0]
    tm = _tiles(s)
    l = p["l"]
    shift1, scale1, gate1, shift2, scale2, gate2 = mod
    h = norm_mod(x, p["norm1_g"], scale1, shift1, name="norm_mod")
    zz = mm_nn(h, p["w1"], tm=tm, tn=768, out_dtype=F32, name="mm_w1", layer=l)
    ret_raw, rprev = retention_fwd(zz, cosf, sinf, name="ret_fwd")
    gla_raw, sprev = gla_fwd(zz, p["w_a2p"], p["b_gla_a"], name="gla_fwd")
    qn, kn, cum = fox_pre(zz, p["q_norm_g"], p["k_norm_g"], p["b_foxp"], name="fox_pre")
    bq, bk = _fox_blocks(s)
    cum_t = cum[:, :NH].T
    cum_col, cum_row = cum_t[:, :, None], cum_t.reshape(NH, s // bk, bk)
    fox_o, lse = fox_attn_fwd(qn, kn, zz, cum_col, cum_row, name="fox_fwd")
    mi = mix_fwd(ret_raw, gla_raw, fox_o, zz, p["ret_norm_g"], p["gla_norm_g"], p["b_mg"], p["w_br"], name="mix_fwd",
                 layer=l)
    x1, mixed = mm_nn_residual(mi, p["w_o"], x, gate1, tm=tm, tn=512, name="mm_wo", layer=l)
    h2 = norm_mod(x1, p["norm2_g"], scale2, shift2, name="norm_mod")
    ug = mm_nn(h2, p["w_up"], tm=tm, tn=512, out_dtype=F32, name="mm_wup", layer=l)
    a = conv_fwd(ug, p["w_conv"], p["b_conv"], name="conv_fwd")
    x2, y = mm_nn_residual(a, p["w_down"], x1, gate2, tm=tm, tn=512, name="mm_wdown", layer=l)
    saved = dict(x=x, h=h, zz=zz, ret_raw=ret_raw, rprev=rprev, gla_raw=gla_raw, sprev=sprev, qn=qn, kn=kn,
                 cum_col=cum_col, cum_row=cum_row, fox_o=fox_o, lse=lse, mi=mi, mixed=mixed, x1=x1, h2=h2, ug=ug, a=a, y=y)
    return x2, saved


def layer_bwd(dx2, mod, p, sv, cosf, sinf, stacks, slot):
    s = dx2.shape[0]
    tm = _tiles(s)
    l = p["l"]
    shift1, scale1, gate1, shift2, scale2, gate2 = mod
    g, stacks = {}, dict(stacks)
    dy, dgate2 = gate_bwd(dx2, sv["y"], gate2, name="gate_bwd")
    stacks["w_down"] = mm_tn(sv["a"], dy, tm=min(1408, DFF), tn=512, out_dtype=MMT, name="mm_dwdown",
                             stack=stacks["w_down"], layer=slot)
    da = mm_nt(dy, p["w_down"], tm=tm, tn=1408, out_dtype=F32, name="mm_da", layer=l)
    du, dg, g["w_conv"], g["b_conv"] = conv_bwd(sv["ug"], p["w_conv"], p["b_conv"], da, name="conv_bwd")
    dug = jnp.concatenate([du, dg], axis=1)
    stacks["w_up"] = mm_tn(sv["h2"], dug, tm=512, tn=512, out_dtype=MMT, name="mm_dwup", stack=stacks["w_up"], layer=slot)
    dh2 = mm_nt(dug, p["w_up"], tm=min(512, s), tn=512, out_dtype=F32, name="mm_dh2", layer=l)
    dx1, g["norm2_g"], dscale2, dshift2 = norm_mod_bwd(sv["x1"], dh2, dx2, p["norm2_g"], scale2, shift2, name="norm_mod_bwd")
    dmixed, dgate1 = gate_bwd(dx1, sv["mixed"], gate1, name="gate_bwd")
    stacks["w_o"] = mm_tn(sv["mi"], dmixed, tm=512, tn=512, out_dtype=MMT, name="mm_dwo", stack=stacks["w_o"], layer=slot)
    dmi = mm_nt(dmixed, p["w_o"], tm=tm, tn=512, out_dtype=MMT, name="mm_dmi", layer=l)
    zz = sv["zz"]
    (dret, dgla, dfox, drg, dgg, dgp, g["w_br"], g["ret_norm_g"], g["gla_norm_g"], g["b_mg"]) = mix_bwd(
        sv["ret_raw"], sv["gla_raw"], sv["fox_o"], zz, p["ret_norm_g"], p["gla_norm_g"], p["b_mg"], p["w_br"], dmi,
        name="mix_bwd", layer=l)
    dqn, dkn, dfv, dcum_row = fox_attn_bwd(sv["qn"], sv["kn"], zz, sv["cum_col"], sv["cum_row"], sv["lse"], dfox,
                                           name="fox_bwd")
    dfv = dfv.astype(MMT)
    dcum = jnp.pad(dcum_row.reshape(NH, s).T, ((0, 0), (0, HD - NH)))
    dfqk, dff, g["q_norm_g"], g["k_norm_g"], g["b_foxp"] = fox_pre_bwd(
        zz, p["q_norm_g"], p["k_norm_g"], p["b_foxp"], dqn, dkn, dcum, name="fox_pre_bwd")
    dgqkv, dlr, g["w_a2p"], g["b_gla_a"] = gla_bwd(zz, p["w_a2p"], p["b_gla_a"], sv["sprev"], dgla, name="gla_bwd")
    drqkv = retention_bwd(zz, cosf, sinf, sv["rprev"], dret, name="ret_bwd")
    dzz = jnp.concatenate([dgp, drqkv, drg, dgqkv, dgg, dfqk, dfv, dlr, dff], axis=1)
    stacks["w_mg"] = mm_tn(sv["h"], dzz, tm=512, tn=768, out_dtype=MMT, name="mm_dwmg", ncols=RQ, stack=stacks["w_mg"],
                           layer=slot)
    dwz = mm_tn(sv["h"], dzz, tm=512, tn=768, out_dtype=MMT, name="mm_dwz", col0=RQ)
    stacks["w_in"] = unalign_dw_in(dwz, stacks["w_in"], slot)
    dh = mm_nt(dzz, p["w1"], tm=min(512, s), tn=512, out_dtype=F32, name="mm_dh", layer=l)
    dx, g["norm1_g"], dscale1, dshift1 = norm_mod_bwd(sv["x"], dh, dx1, p["norm1_g"], scale1, shift1, name="norm_mod_bwd")
    dmod = jnp.concatenate([dshift1, dscale1, dgate1, dshift2, dscale2, dgate2], axis=1)
    return dx, g, dmod, stacks


def _align_cols(w_in, w_mg):
    z = lambda n: jnp.zeros((w_in.shape[0], n), w_in.dtype)
    return jnp.concatenate([w_mg, w_in[:, :3072], w_in[:, 3088:5136], w_in[:, 3072:3088], z(HD - GLR),
                            w_in[:, 5136:5140], z(HD - NH)], axis=1)


def _unalign_cols(dwz):
    o = lambda c: c - RQ
    return jnp.concatenate([dwz[:, :o(GG)], dwz[:, o(LR):o(LR) + GLR], dwz[:, o(GG):o(LR)], dwz[:, o(FF):o(FF) + NH]], axis=1)


def build_w1(w_in_sh, w_mg):
    nl = w_mg.shape[0]
    t = _row_tile(D)

    def body(s_ref, g_ref, o_ref):
        o_ref[...] = _align_cols(jnp.concatenate([s_ref[k] for k in range(4)], axis=1), g_ref[...])

    return pl.pallas_call(
        body, grid=(nl, D // t),
        in_specs=[pl.BlockSpec((None, 4, t, IN_W // 4), lambda l, i: (l, 0, i, 0)), pl.BlockSpec((None, t, RQ), lambda l, i: (l, i, 0))],
        out_specs=pl.BlockSpec((None, t, NZZ), lambda l, i: (l, i, 0)), out_shape=_sds((nl, D, NZZ), w_mg.dtype),
        compiler_params=_cparams(("parallel", "parallel")), name="build_w1")(w_in_sh, w_mg)


def unalign_dw_in(dwz, stack, layer):
    t = _row_tile(D)

    def body(z_ref, s_ref, o_ref):
        w = _unalign_cols(z_ref[...])
        for k in range(4):
            o_ref[k] = w[:, k * (IN_W // 4):(k + 1) * (IN_W // 4)]

    return pl.pallas_call(
        body, grid=(D // t,),
        in_specs=[pl.BlockSpec((t, NZZ - RQ), lambda i: (i, 0)), pl.BlockSpec(memory_space=pl.ANY)],
        out_specs=pl.BlockSpec((None, 4, t, IN_W // 4), lambda i: (layer, 0, i, 0)), out_shape=_sds(stack.shape, stack.dtype),
        input_output_aliases={1: 0}, compiler_params=_cparams(("parallel",)), name="unalign_dw_in")(dwz, stack)


def layer_params(w, l):
    row = lambda v: v[l][None, :]
    return dict(
        l=l, norm1_g=row(w["norm1_g"]), norm2_g=row(w["norm2_g"]), w1=w["w1"],
        w_a2p=jnp.pad(w["w_gla_a2"][l], ((0, HD - GLR), (0, 0))), b_gla_a=row(w["b_gla_a"]),
        b_foxp=jnp.pad(row(w["b_fox_f"]), ((0, 0), (0, HD - NH))), ret_norm_g=row(w["ret_norm_g"]),
        gla_norm_g=row(w["gla_norm_g"]), q_norm_g=row(w["q_norm_g"]), k_norm_g=row(w["k_norm_g"]),
        w_br=w["w_br"], b_mg=row(w["b_mg"]), w_o=w["w_o"], w_up=w["w_up"], w_conv=w["w_conv"][l],
        b_conv=row(w["b_conv"]), w_down=w["w_down"])


def layer_grads(g):
    vec = lambda v: v[0]
    return dict(
        norm1_g=vec(g["norm1_g"]), norm2_g=vec(g["norm2_g"]), w_gla_a2=g["w_a2p"][:GLR], b_gla_a=vec(g["b_gla_a"]),
        b_fox_f=g["b_foxp"][0, :NH], ret_norm_g=vec(g["ret_norm_g"]), gla_norm_g=vec(g["gla_norm_g"]),
        q_norm_g=vec(g["q_norm_g"]), k_norm_g=vec(g["k_norm_g"]), w_br=g["w_br"], b_mg=vec(g["b_mg"]),
        w_conv=g["w_conv"], b_conv=vec(g["b_conv"]))


def ada_mod(c_all, w_ada, b_ada):
    nl, _, n = w_ada.shape

    def body(c_ref, w_ref, b_ref, o_ref):
        o_ref[...] = _dot(_silu(c_ref[...]), w_ref[...], HI) + b_ref[...]

    return pl.pallas_call(
        body, grid=(nl,),
        in_specs=[pl.BlockSpec((8, D), lambda l: (0, 0)), pl.BlockSpec((None, D, n), lambda l: (l, 0, 0)),
                  pl.BlockSpec((None, 1, n), lambda l: (l, 0, 0))],
        out_specs=pl.BlockSpec((None, 8, n), lambda l: (l, 0, 0)), out_shape=_sds((nl, 8, n), F32),
        compiler_params=_cparams(("parallel",)), name="ada_mod")(c_all, w_ada, b_ada)


def ada_dw(c_all, dmod):
    nl, _, n = dmod.shape

    def body(c_ref, d_ref, o_ref):
        o_ref[...] = _dot_tn(_silu(c_ref[...]), d_ref[...], HI)

    return pl.pallas_call(
        body, grid=(nl,),
        in_specs=[pl.BlockSpec((8, D), lambda l: (0, 0)), pl.BlockSpec((None, 8, n), lambda l: (l, 0, 0))],
        out_specs=pl.BlockSpec((None, D, n), lambda l: (l, 0, 0)), out_shape=_sds((nl, D, n), F32),
        compiler_params=_cparams(("parallel",)), name="ada_dw")(c_all, dmod)


def sum_devices(g):
    def body(g_ref, o_ref):
        acc = g_ref[0]
        for d in range(1, 8):
            acc = acc + g_ref[d]
        o_ref[...] = acc

    return pl.pallas_call(body, out_shape=_sds(g.shape[1:], F32), name="sum_devices")(g)


def adamw(w, g, m, v, *, block, name):
    nd = w.ndim
    grid = tuple(w.shape[i] // block[i] for i in range(nd))
    bc1 = 1.0 - ADAM_B1 ** ADAM_STEP
    bc2 = 1.0 - ADAM_B2 ** ADAM_STEP

    def body(w_ref, g_ref, m_ref, v_ref, d_ref, nm_ref, nv_ref):
        gv = g_ref[...]
        nm = ADAM_B1 * m_ref[...] + (1.0 - ADAM_B1) * gv
        nv = ADAM_B2 * v_ref[...] + (1.0 - ADAM_B2) * (gv * gv)
        nm_ref[...] = nm
        nv_ref[...] = nv
        d_ref[...] = -ADAM_LR * ((nm / bc1) / (jnp.sqrt(nv / bc2) + ADAM_EPS) + ADAM_WD * w_ref[...])

    spec = pl.BlockSpec(tuple(block), lambda *i: i)
    return pl.pallas_call(
        body, grid=grid, in_specs=[spec] * 4, out_specs=[spec] * 3, out_shape=[_sds(w.shape, F32)] * 3,
        compiler_params=_cparams(("parallel",) * nd), name=name)(w, g, m, v)


MESH = pl.DeviceIdType.MESH
ANY = pl.BlockSpec(memory_space=pl.ANY)
VM = pl.BlockSpec(memory_space=pltpu.VMEM)


def _place():
    x, y, c = lax.axis_index("x"), lax.axis_index("y"), lax.axis_index("c")
    return x, y, c, [(1 - x, y), (x, 1 - y), (1 - x, 1 - y)]


def small_allgather(v, *, name):
    m_per, n = v.shape

    def body(x_ref, out_ref, send_sems, recv_sems, local_sem):
        x, y, c, chips = _place()
        me, sibling = (x, y, c), (x, y, 1 - c)

        def rows(px, py, pc):
            return out_ref.at[pl.ds((4 * px + 2 * py + pc) * m_per, m_per), :]

        def copy(k, block, to, src=None):
            return pltpu.make_async_remote_copy(
                src_ref=rows(*block) if src is None else src, dst_ref=rows(*block),
                send_sem=send_sems.at[k], recv_sem=recv_sems.at[k], device_id=to, device_id_type=MESH)

        mine = pltpu.make_async_copy(x_ref, rows(*me), local_sem)
        mine.start()
        first = [copy(0, me, sibling, src=x_ref)]
        first += [copy(1 + j, me, (*chip, c), src=x_ref) for j, chip in enumerate(chips)]
        for cp in first:
            cp.start()
        passed = [copy(4 + j, (*chip, c), sibling) for j, chip in enumerate(chips)]
        for j, chip in enumerate(chips):
            copy(1 + j, (*chip, c), me).wait_recv()
            passed[j].start()
        copy(0, sibling, me).wait_recv()
        for j, chip in enumerate(chips):
            copy(4 + j, (*chip, 1 - c), me).wait_recv()
        for cp in first + passed:
            cp.wait_send()
        mine.wait()

    return pl.pallas_call(
        body, out_shape=_sds((8 * m_per, n), v.dtype), in_specs=[VM], out_specs=VM,
        scratch_shapes=[pltpu.SemaphoreType.DMA((7,)), pltpu.SemaphoreType.DMA((7,)), pltpu.SemaphoreType.DMA],
        name=name)(v)


TENSORS = {
    "w_in": ("lead", None, (4, D, 1285), (1, D, 1285)),
    "w_mg": ("col", 768, (D, 3072), (512, 3072)),
    "w_br": ("col", 256, (3, BW, D), (3, BW, D)),
    "w_o": ("row", 256, (D, D), (D, D)),
    "w_up": ("col", 1408, (D, 5632), (256, 5632)),
    "w_down": ("row", 704, (DFF, D), (704, D)),
}
BIG = tuple(TENSORS)


def _shard_shape(name):
    kind, width, full, _ = TENSORS[name]
    if kind == "lead":
        return full[1:]
    return full[:-1] + (width,) if kind == "col" else (width,) + full[1:]


def _shard_view(ref, layers, name, k):
    kind, width, full, _ = TENSORS[name]
    if kind == "lead":
        return ref.at[layers, k]
    if kind == "row":
        return ref.at[layers, pl.ds(k * width, width)]
    return ref.at[(layers,) + (slice(None),) * (len(full) - 1) + (pl.ds(k * width, width),)]


def _remote(send_sems, recv_sems, k, src, dst, to):
    return pltpu.make_async_remote_copy(src_ref=src, dst_ref=dst, send_sem=send_sems.at[k], recv_sem=recv_sems.at[k],
                                        device_id=to, device_id_type=MESH)


def _dma_sems(n):
    return [pltpu.SemaphoreType.DMA((n,)), pltpu.SemaphoreType.DMA((n,))]


def gather_weights(shards, *, name):
    nl = shards[BIG[0]].shape[0]
    hl = nl // 2
    nt = len(BIG)
    per = 6 * hl + nl

    def body(*refs):
        p, o = dict(zip(BIG, refs[:nt])), dict(zip(BIG, refs[nt:2 * nt]))
        send_sems, recv_sems = refs[2 * nt:]
        x, y, c, chips = _place()
        mine, me, sibling = 2 * x + y, (x, y, c), (x, y, 1 - c)
        remote = functools.partial(_remote, send_sems, recv_sems)
        started = []

        def go(cp):
            cp.start()
            started.append(cp)

        for t, n in enumerate(BIG):
            for l in range(nl):
                go(remote(t * per + 6 * hl + l, p[n].at[l], _shard_view(o[n], l, n, mine), sibling))
        for i in range(hl):
            for t, n in enumerate(BIG):
                for j, chip in enumerate(chips):
                    layer = hl * c + i
                    go(remote(t * per + j * hl + i, p[n].at[layer], _shard_view(o[n], layer, n, mine), (*chip, c)))
        for i in range(hl):
            for t, n in enumerate(BIG):
                for j, chip in enumerate(chips):
                    landed = _shard_view(o[n], hl * c + i, n, 2 * chip[0] + chip[1])
                    remote(t * per + j * hl + i, landed, landed, me).wait_recv()
                    go(remote(t * per + 3 * hl + j * hl + i, landed, landed, sibling))
        for i in range(hl):
            for t, n in enumerate(BIG):
                for j, chip in enumerate(chips):
                    landed = _shard_view(o[n], hl * (1 - c) + i, n, 2 * chip[0] + chip[1])
                    remote(t * per + 3 * hl + j * hl + i, landed, landed, me).wait_recv()
        for t, n in enumerate(BIG):
            for l in range(nl):
                landed = _shard_view(o[n], l, n, mine)
                remote(t * per + 6 * hl + l, landed, landed, me).wait_recv()
        for cp in started:
            cp.wait_send()

    outs = pl.pallas_call(
        body, out_shape=[_sds((nl,) + TENSORS[n][2], shards[n].dtype) for n in BIG], in_specs=[ANY] * nt,
        out_specs=[ANY] * nt, scratch_shapes=_dma_sems(nt * per), name=name)(*[shards[n] for n in BIG])
    return dict(zip(BIG, outs))


RS_GROUP = 2
HBM = pl.BlockSpec(memory_space=pltpu.HBM)
SEM = pl.BlockSpec(memory_space=pltpu.SEMAPHORE)


def pair_exchange(g, *, name):
    hh = g[BIG[0]].shape[0] // 2
    nt = len(BIG)

    def body(*refs):
        send_sems, recv_sems = refs[2 * nt:]
        x, y, c, _ = _place()
        copies = [_remote(send_sems, recv_sems, t, refs[t].at[pl.ds(hh * (1 - c), hh)], refs[nt + t], (x, y, 1 - c))
                  for t in range(nt)]
        for cp in copies:
            cp.start()
        for cp in copies:
            cp.wait()

    outs = pl.pallas_call(
        body, out_shape=[_sds((hh,) + g[n].shape[1:], g[n].dtype) for n in BIG], in_specs=[ANY] * nt, out_specs=[ANY] * nt,
        scratch_shapes=_dma_sems(nt), name=name)(*[g[n] for n in BIG])
    return dict(zip(BIG, outs))


def _chip_copies(send_sems, recv_sems, s_refs, land_refs, c, chips):
    hl = s_refs[0].shape[0]
    return [_remote(send_sems, recv_sems, 3 * t + j, _shard_view(s_refs[t], pl.ds(0, hl), n, 2 * chip[0] + chip[1]),
                    land_refs[t].at[j], (*chip, c))
            for t, n in enumerate(BIG) for j, chip in enumerate(chips)]


def _landing_shapes(s):
    hl = s[BIG[0]].shape[0]
    return [_sds((3, hl) + _shard_shape(n), s[n].dtype) for n in BIG]


def chip_exchange(s, *, name):
    nt = len(BIG)

    def body(*refs):
        send_sems, recv_sems = refs[2 * nt:]
        x, y, c, chips = _place()
        copies = _chip_copies(send_sems, recv_sems, refs[:nt], refs[nt:2 * nt], c, chips)
        for cp in copies:
            cp.start()
        for cp in copies:
            cp.wait()

    outs = pl.pallas_call(
        body, out_shape=_landing_shapes(s), in_specs=[ANY] * nt, out_specs=[ANY] * nt,
        scratch_shapes=_dma_sems(3 * nt), name=name)(*[s[n] for n in BIG])
    return dict(zip(BIG, outs))


def chip_exchange_start(s, *, name):
    nt = len(BIG)

    def body(*refs):
        send_sems, recv_sems = refs[2 * nt:2 * nt + 2]
        x, y, c, chips = _place()
        for cp in _chip_copies(send_sems, recv_sems, refs[:nt], refs[nt:2 * nt], c, chips):
            cp.start()
        refs[-1][...] = jnp.zeros_like(refs[-1])

    hbm = lambda a: pltpu.with_memory_space_constraint(a, pltpu.HBM)
    lands = [hbm(lax.empty(d.shape, d.dtype)) for d in _landing_shapes(s)]
    srcs = [hbm(s[n]) for n in BIG]
    outs = pl.pallas_call(
        body,
        out_shape=(pltpu.SemaphoreType.DMA((3 * nt,)), pltpu.SemaphoreType.DMA((3 * nt,)),
                   *[pltpu.HBM(a.shape, a.dtype) for a in srcs + lands], _sds((8, HD), F32)),
        in_specs=[HBM] * (2 * nt), out_specs=(SEM, SEM, *[HBM] * (2 * nt), VM),
        input_output_aliases={t: 2 + t for t in range(2 * nt)},
        compiler_params=pltpu.CompilerParams(has_side_effects=pltpu.SideEffectType.DATAFLOW_SIDE_EFFECTING),
        name=name)(*srcs, *lands)
    return outs[0], outs[1], outs[2:2 + nt], outs[2 + nt:2 + 2 * nt], outs[-1]


def chip_exchange_wait(send_sems, recv_sems, srcs, lands, after, *, name):
    nt = len(BIG)

    def body(*refs):
        x, y, c, chips = _place()
        for cp in _chip_copies(refs[2 * nt], refs[2 * nt + 1], refs[:nt], refs[nt:2 * nt], c, chips):
            cp.wait_send()
            cp.wait_recv()

    outs = pl.pallas_call(
        body, out_shape=tuple(pltpu.HBM(a.shape, a.dtype) for a in list(srcs) + list(lands)),
        in_specs=[HBM] * (2 * nt) + [SEM, SEM, ANY], out_specs=tuple([HBM] * (2 * nt)),
        input_output_aliases={t: t for t in range(2 * nt)},
        compiler_params=pltpu.CompilerParams(has_side_effects=pltpu.SideEffectType.DATAFLOW_SIDE_EFFECTING),
        name=name)(*srcs, *lands, send_sems, recv_sems, after)
    return dict(zip(BIG, outs[:nt])), dict(zip(BIG, outs[nt:]))


def pair_share(f, l0, hh, *, name):
    nt = len(BIG)

    def body(*refs):
        o = refs[nt:2 * nt]
        send_sems, recv_sems = refs[2 * nt:]
        x, y, c, _ = _place()
        mine, theirs = pl.ds(l0 + hh * c, hh), pl.ds(l0 + hh * (1 - c), hh)
        copies = [_remote(send_sems, recv_sems, t, o[t].at[mine], o[t].at[mine], (x, y, 1 - c)) for t in range(nt)]
        for cp in copies:
            cp.start()
        for t, cp in enumerate(copies):
            cp.wait_send()
            _remote(send_sems, recv_sems, t, o[t].at[theirs], o[t].at[theirs], (x, y, c)).wait_recv()

    outs = pl.pallas_call(
        body, out_shape=[_sds(f[n].shape, f[n].dtype) for n in BIG], in_specs=[ANY] * nt, out_specs=[ANY] * nt,
        input_output_aliases={t: t for t in range(nt)}, scratch_shapes=_dma_sems(nt), name=name)(*[f[n] for n in BIG])
    return dict(zip(BIG, outs))


def pair_add(g, r, idx, *, tensor, name):
    _, _, full, blk = TENSORS[tensor]
    hh = r.shape[0]

    def body(idx_ref, g_ref, r_ref, o_ref):
        o_ref[...] = (g_ref[...].astype(F32) + r_ref[...].astype(F32)).astype(o_ref.dtype)

    own = pl.BlockSpec((None,) + blk, lambda *a: (a[0],) + a[1:-1])
    return pl.pallas_call(
        body, out_shape=_sds(r.shape, r.dtype),
        grid_spec=pltpu.PrefetchScalarGridSpec(
            num_scalar_prefetch=1, grid=(hh,) + tuple(f // b for f, b in zip(full, blk)),
            in_specs=[pl.BlockSpec((None,) + blk, lambda *a: (hh * a[-1][0] + a[0],) + a[1:-1]), own], out_specs=own),
        compiler_params=_cparams(("parallel",) * (1 + len(full))), name=name)(idx, g, r)


def chip_add(s, r, idx, totals, l0, *, tensor, name):
    kind, width, full, _ = TENSORS[tensor]
    shard = _shard_shape(tensor)
    hh = s.shape[0]
    zeros = (0,) * len(shard)

    def body(idx_ref, s_ref, r0_ref, r1_ref, r2_ref, t_ref, o_ref):
        o_ref[...] = ((s_ref[...].astype(F32) + r0_ref[...].astype(F32)) + r1_ref[...].astype(F32)) + r2_ref[...].astype(F32)

    if kind == "lead":
        mine = pl.BlockSpec((None, None) + shard, lambda i, ix: (i, ix[1]) + zeros)
    elif kind == "row":
        mine = pl.BlockSpec((None,) + shard, lambda i, ix: (i, ix[1]) + zeros[1:])
    else:
        mine = pl.BlockSpec((None,) + shard, lambda i, ix: (i,) + zeros[1:] + (ix[1],))
    peer = lambda j: pl.BlockSpec((None, None) + shard, lambda i, ix: (j, i) + zeros)
    return pl.pallas_call(
        body, out_shape=_sds(totals.shape, F32),
        grid_spec=pltpu.PrefetchScalarGridSpec(
            num_scalar_prefetch=1, grid=(hh,), in_specs=[mine, peer(0), peer(1), peer(2), pl.BlockSpec(memory_space=pl.ANY)],
            out_specs=pl.BlockSpec((None,) + shard, lambda i, ix: (l0 + hh * ix[0] + i,) + zeros)),
        input_output_aliases={5: 0}, compiler_params=_cparams(("parallel",)), name=name)(idx, s, r, r, r, totals)


def _flat_rows(arrs):
    v = jnp.concatenate([a.reshape(-1) for a in arrs])
    n = -(-v.shape[0] // 1024) * 1024
    return jnp.pad(v, (0, n - v.shape[0])).reshape(n // HD, HD)


def _unflat(buf, shapes):
    v, out, o = buf.reshape(-1), [], 0
    for s in shapes:
        n = int(np.prod(s))
        out.append(v[o:o + n].reshape(s))
        o += n
    return out


WEIGHTS = ("norm1_g", "norm2_g", "w_ada", "b_ada", "w_in", "w_gla_a2", "b_gla_a", "b_fox_f", "ret_norm_g", "gla_norm_g",
           "q_norm_g", "k_norm_g", "w_br", "w_mg", "b_mg", "w_o", "w_up", "w_conv", "b_conv", "w_down")
REPLICATED = ("norm1_g", "norm2_g", "b_gla_a", "b_fox_f", "ret_norm_g", "gla_norm_g", "q_norm_g", "k_norm_g", "b_mg", "b_conv")
ADAM_BLOCKS = dict(w_ada=(1, 256, 1536), w_in=(1, 256, 1285), w_br=(1, 3, BW, 256), w_mg=(1, 512, 768), w_o=(2, 256, D),
                   w_up=(1, 256, 1408), w_down=(1, 352, D))
ALL_AXES = ("x", "y", "c")


def kernel(x, c, norm1_g, norm2_g, w_ada, b_ada, w_in, w_gla_a2, b_gla_a, b_fox_f, ret_norm_g, gla_norm_g, q_norm_g, k_norm_g, w_br, w_mg, b_mg, w_o, w_up, w_conv, b_conv, w_down, loss_target, m_norm1_g, m_norm2_g, m_w_ada, m_b_ada, m_w_in, m_w_gla_a2, m_b_gla_a, m_b_fox_f, m_ret_norm_g, m_gla_norm_g, m_q_norm_g, m_k_norm_g, m_w_br, m_w_mg, m_b_mg, m_w_o, m_w_up, m_w_conv, m_b_conv, m_w_down, v_norm1_g, v_norm2_g, v_w_ada, v_b_ada, v_w_in, v_w_gla_a2, v_b_gla_a, v_b_fox_f, v_ret_norm_g, v_gla_norm_g, v_q_norm_g, v_k_norm_g, v_w_br, v_w_mg, v_b_mg, v_w_o, v_w_up, v_w_conv, v_b_conv, v_w_down):
    w = dict(zip(WEIGHTS, (norm1_g, norm2_g, w_ada, b_ada, w_in, w_gla_a2, b_gla_a, b_fox_f, ret_norm_g, gla_norm_g,
                           q_norm_g, k_norm_g, w_br, w_mg, b_mg, w_o, w_up, w_conv, b_conv, w_down)))
    m = dict(zip(WEIGHTS, (m_norm1_g, m_norm2_g, m_w_ada, m_b_ada, m_w_in, m_w_gla_a2, m_b_gla_a, m_b_fox_f, m_ret_norm_g,
                           m_gla_norm_g, m_q_norm_g, m_k_norm_g, m_w_br, m_w_mg, m_b_mg, m_w_o, m_w_up, m_w_conv, m_b_conv,
                           m_w_down)))
    v = dict(zip(WEIGHTS, (v_norm1_g, v_norm2_g, v_w_ada, v_b_ada, v_w_in, v_w_gla_a2, v_b_gla_a, v_b_fox_f, v_ret_norm_g,
                           v_gla_norm_g, v_q_norm_g, v_k_norm_g, v_w_br, v_w_mg, v_b_mg, v_w_o, v_w_up, v_w_conv, v_b_conv,
                           v_w_down)))
    nl = norm1_g.shape[0]
    seq = x.shape[1]
    xi, yi, ci = lax.axis_index("x"), lax.axis_index("y"), lax.axis_index("c")
    k_me = 2 * xi + yi
    b_me = 4 * xi + 2 * yi + ci
    ada_n = w_ada.shape[2]
    a2_n, conv_n = w_gla_a2.shape[2], w_conv.shape[2]

    blk = _flat_rows([c, w_gla_a2, w_conv])
    g1 = small_allgather(blk, name="gather_small").reshape(8, blk.shape[0], HD)
    c_all = g1[:, :D // HD].reshape(8, D)
    by_chip = g1[0::2].reshape(4, -1)[:, D:]
    a2_sh, conv_sh = by_chip[:, :nl * GLR * a2_n], by_chip[:, nl * GLR * a2_n:nl * (GLR * a2_n + 3 * conv_n)]
    full_small = dict(
        w_gla_a2=a2_sh.reshape(4, nl, GLR, a2_n).transpose(1, 2, 0, 3).reshape(nl, GLR, 4 * a2_n),
        w_conv=conv_sh.reshape(4, nl, 3, conv_n).transpose(1, 2, 0, 3).reshape(nl, 3, 4 * conv_n))

    b_ada_sh = lax.dynamic_slice_in_dim(b_ada, k_me * ada_n, ada_n, axis=1)[:, None, :]
    mod_sh = ada_mod(c_all, w_ada, b_ada_sh)
    g2 = small_allgather(mod_sh.reshape(nl * 8, ada_n), name="gather_mod").reshape(4, 2, nl, 8, ada_n)[:, 0]
    mod_me = lax.dynamic_index_in_dim(g2, b_me, axis=2, keepdims=False).transpose(1, 0, 2).reshape(nl, 4 * ada_n)

    wfull = {n: w[n] for n in REPLICATED}
    wfull.update(full_small)
    wfull.update(gather_weights({n: w[n].astype(MMT) for n in BIG}, name="gather_weights"))
    wfull["w1"] = build_w1(wfull["w_in"], wfull["w_mg"])
    params = [layer_params(wfull, l) for l in range(nl)]
    mods = [[mod_me[l:l + 1, i * D:(i + 1) * D] for i in range(6)] for l in range(nl)]

    cosf, sinf = _rope_tables(seq)
    xs, saved = x[0], []
    for l in range(nl):
        xs, sv = layer_fwd(xs, mods[l], params[l], cosf, sinf)
        saved.append(sv)
    loss_part, dx = loss_and_grad(xs, loss_target[0], name="loss")
    loss = lax.psum(loss_part[0, 0], ALL_AXES)
    grads, dmods = [None] * nl, [None] * nl
    idx = jnp.stack([ci, k_me]).astype(jnp.int32)
    totals = {n: lax.empty((nl,) + _shard_shape(n), F32) for n in BIG}

    def finish_group(group, chip_sum, from_chips, totals):
        totals = {n: chip_add(chip_sum[n], from_chips[n], idx, totals[n], RS_GROUP * group, tensor=n,
                              name=f"rs{group}_chip_add_{n}") for n in BIG}
        return pair_share(totals, RS_GROUP * group, RS_GROUP // 2, name=f"rs{group}_pair_share")

    pending = None
    for group in reversed(range(nl // RS_GROUP)):
        layers = range(RS_GROUP * group, RS_GROUP * (group + 1))
        stacks = {n: lax.empty((RS_GROUP,) + TENSORS[n][2], MMT) for n in BIG if n != "w_br"}
        if pending is not None:
            dx = dx + pending[-1][0, 0]
        for l in reversed(layers):
            dx, g, dmods[l], stacks = layer_bwd(dx, mods[l], params[l], saved[l], cosf, sinf, stacks, l - layers[0])
            grads[l] = layer_grads(g)
        stacks["w_br"] = jnp.stack([grads[l]["w_br"].astype(MMT) for l in layers])
        from_sibling = pair_exchange(stacks, name=f"rs{group}_pair_exchange")
        chip_sum = {n: pair_add(stacks[n], from_sibling[n], idx, tensor=n, name=f"rs{group}_pair_add_{n}") for n in BIG}
        if pending is not None:
            earlier, send_sems, recv_sems, srcs, lands, _ = pending
            sums, from_chips = chip_exchange_wait(send_sems, recv_sems, srcs, lands, dx,
                                                  name=f"rs{earlier}_chip_exchange_wait")
            totals = finish_group(earlier, sums, from_chips, totals)
        if group > 0:
            pending = (group, *chip_exchange_start(chip_sum, name=f"rs{group}_chip_exchange_start"))
        else:
            totals = finish_group(0, chip_sum, chip_exchange(chip_sum, name="rs0_chip_exchange"), totals)

    small_names = REPLICATED + ("w_gla_a2", "w_conv")
    small_shapes = [(nl, 6 * D)] + [(nl,) + grads[0][n].shape for n in small_names]
    vec = _flat_rows([jnp.concatenate(dmods, axis=0)] + [jnp.stack([grads[l][n] for l in range(nl)]) for n in small_names])
    gs = small_allgather(vec, name="gather_small_grads").reshape(8, vec.shape[0], HD)
    summed = _unflat(sum_devices(gs), small_shapes)
    grad = dict(zip(small_names, summed[1:]))
    grad["b_ada"] = summed[0]
    grad["w_gla_a2"] = lax.dynamic_slice_in_dim(grad["w_gla_a2"], k_me * a2_n, a2_n, axis=2)
    grad["w_conv"] = lax.dynamic_slice_in_dim(grad["w_conv"], k_me * conv_n, conv_n, axis=2)
    dmod_all = gs[:, :nl * 6 * D // HD].reshape(8, nl, 6 * D)
    dmod_sh = lax.dynamic_slice_in_dim(dmod_all, k_me * ada_n, ada_n, axis=2).transpose(1, 0, 2)
    grad["w_ada"] = ada_dw(c_all, dmod_sh)

    grad.update(totals)

    delta, new_m, new_v = {}, {}, {}
    for n, block in ADAM_BLOCKS.items():
        delta[n], new_m[n], new_v[n] = adamw(w[n], grad[n], m[n], v[n], block=block, name="adamw_" + n)
    rest = [n for n in WEIGHTS if n not in ADAM_BLOCKS]
    shapes = [w[n].shape for n in rest]
    flat = [_flat_rows([t[n] for n in rest]) for t in (w, grad, m, v)]
    outs = adamw(*flat, block=flat[0].shape, name="adamw_small")
    for t, o in zip((delta, new_m, new_v), outs):
        t.update(zip(rest, _unflat(o, shapes)))

    return (loss, dx[None], *[grad[n] for n in WEIGHTS], *[delta[n] for n in WEIGHTS], *[new_m[n] for n in WEIGHTS],
            *[new_v[n] for n in WEIGHTS])
```

```python
import functools

import numpy as np
import jax
import jax.numpy as jnp
from jax import lax
from jax.experimental import pallas as pl
from jax.experimental.pallas import tpu as pltpu

F32 = jnp.float32
MMT = jnp.bfloat16
HI = lax.Precision.HIGHEST

D = 1024
DEPTH = 4
NH = 4
HD = 128
BW = NH * HD
CH = 64
GDK = 64
GLR = 16
DFF = 2816
EPS = 1e-6
ROPE_BASE = 10000.0

GP, RQ, RK, RV, RG, GQ, GK, GV, GG, FQ, FK, FV, LR, FF = (
    0, 3072, 3584, 4096, 4608, 5120, 5376, 5632, 6144, 6656, 7168, 7680, 8192, 8320)
NZZ = 8448
IN_W = 5140

VMEM_LIMIT = 56 * 1024 * 1024

ADAM_LR, ADAM_B1, ADAM_B2, ADAM_EPS, ADAM_WD, ADAM_STEP = 0.001, 0.9, 0.999, 1e-08, 0.01, 10


def _cparams(sem=None):
    return pltpu.CompilerParams(dimension_semantics=sem, vmem_limit_bytes=VMEM_LIMIT)


def _sds(shape, dtype):
    return jax.ShapeDtypeStruct(tuple(shape), dtype)


def _dot(a, b, precision=None):
    return lax.dot_general(a, b, (((1,), (0,)), ((), ())), precision=precision, preferred_element_type=F32)


def _dot_nt(a, b, precision=None):
    return lax.dot_general(a, b, (((1,), (1,)), ((), ())), precision=precision, preferred_element_type=F32)


def _dot_tn(a, b, precision=None):
    return lax.dot_general(a, b, (((0,), (0,)), ((), ())), precision=precision, preferred_element_type=F32)


def _silu(x):
    return x * jax.nn.sigmoid(x)


def _log_sigmoid(x):
    return jnp.minimum(x, 0.0) - jnp.log(1.0 + jnp.exp(jnp.minimum(x, -x)))


@jax.custom_vjp
def _swap_halves(x):
    return pltpu.roll(x, HD // 2, 1)


_swap_halves.defvjp(lambda x: (_swap_halves(x), None), lambda _, g: (_swap_halves(g),))


@jax.custom_vjp
def _bdot(a, b):
    return _dot(a.astype(MMT), b.astype(MMT))


@jax.custom_vjp
def _bdot_nt(a, b):
    return _dot_nt(a.astype(MMT), b.astype(MMT))


@jax.custom_vjp
def _bdot_tn(a, b):
    return _dot_tn(a.astype(MMT), b.astype(MMT))


_bdot.defvjp(lambda a, b: (_bdot(a, b), (a, b)), lambda r, g: (_bdot_nt(g, r[1]), _bdot_tn(r[0], g)))
_bdot_nt.defvjp(lambda a, b: (_bdot_nt(a, b), (a, b)), lambda r, g: (_bdot(g, r[1]), _bdot_tn(g, r[0])))
_bdot_tn.defvjp(lambda a, b: (_bdot_tn(a, b), (a, b)), lambda r, g: (_bdot_nt(r[1], g), _bdot(r[0], g)))


def _stacked(blk, idx, layer):
    if layer is None:
        return pl.BlockSpec(blk, idx)
    return pl.BlockSpec((None,) + blk, lambda i, j: (layer,) + idx(i, j))


def mm_nn(a, b, *, tm, tn, out_dtype, name, layer=None):
    m, k = a.shape
    n = b.shape[-1]

    def body(a_ref, b_ref, o_ref):
        o_ref[...] = _dot(a_ref[...], b_ref[...]).astype(o_ref.dtype)

    return pl.pallas_call(
        body, grid=(m // tm, n // tn),
        in_specs=[pl.BlockSpec((tm, k), lambda i, j: (i, 0)), _stacked((k, tn), lambda i, j: (0, j), layer)],
        out_specs=pl.BlockSpec((tm, tn), lambda i, j: (i, j)),
        out_shape=_sds((m, n), out_dtype), compiler_params=_cparams(("parallel", "parallel")), name=name)(a, b)


def mm_nn_residual(a, b, res, gate, *, tm, tn, name, layer=None):
    m, k = a.shape
    n = b.shape[-1]

    def body(a_ref, b_ref, r_ref, g_ref, x_ref, y_ref):
        acc = _dot(a_ref[...], b_ref[...])
        y_ref[...] = acc
        x_ref[...] = r_ref[...] + g_ref[...] * acc

    return pl.pallas_call(
        body, grid=(m // tm, n // tn),
        in_specs=[pl.BlockSpec((tm, k), lambda i, j: (i, 0)), _stacked((k, tn), lambda i, j: (0, j), layer),
                  pl.BlockSpec((tm, tn), lambda i, j: (i, j)), pl.BlockSpec((1, tn), lambda i, j: (0, j))],
        out_specs=[pl.BlockSpec((tm, tn), lambda i, j: (i, j)), pl.BlockSpec((tm, tn), lambda i, j: (i, j))],
        out_shape=[_sds((m, n), F32), _sds((m, n), F32)],
        compiler_params=_cparams(("parallel", "parallel")), name=name)(a, b, res, gate)


def mm_nt(a, b, *, tm, tn, out_dtype, name, layer=None):
    m, k = a.shape
    n = b.shape[-2]

    def body(a_ref, b_ref, o_ref):
        o_ref[...] = _dot_nt(a_ref[...], b_ref[...]).astype(o_ref.dtype)

    return pl.pallas_call(
        body, grid=(m // tm, n // tn),
        in_specs=[pl.BlockSpec((tm, k), lambda i, j: (i, 0)), _stacked((tn, k), lambda i, j: (j, 0), layer)],
        out_specs=pl.BlockSpec((tm, tn), lambda i, j: (i, j)),
        out_shape=_sds((m, n), out_dtype), compiler_params=_cparams(("parallel", "parallel")), name=name)(a, b)


def mm_tn(a, b, *, tm, tn, out_dtype, name, col0=0, ncols=None, stack=None, layer=None):
    s, m = a.shape
    n = b.shape[1] - col0 if ncols is None else ncols
    c0 = col0 // tn

    def body(a_ref, b_ref, *rest):
        o_ref = rest[-1]
        o_ref[...] = _dot_tn(a_ref[...], b_ref[...]).astype(o_ref.dtype)

    in_specs = [pl.BlockSpec((s, tm), lambda i, j: (0, i)), pl.BlockSpec((s, tn), lambda i, j: (0, c0 + j))]
    if stack is None:
        return pl.pallas_call(
            body, grid=(m // tm, n // tn), in_specs=in_specs, out_specs=pl.BlockSpec((tm, tn), lambda i, j: (i, j)),
            out_shape=_sds((m, n), out_dtype), compiler_params=_cparams(("parallel", "parallel")), name=name)(a, b)
    return pl.pallas_call(
        body, grid=(m // tm, n // tn), in_specs=in_specs + [pl.BlockSpec(memory_space=pl.ANY)],
        out_specs=pl.BlockSpec((None, tm, tn), lambda i, j: (layer, i, j)),
        out_shape=_sds(stack.shape, stack.dtype), input_output_aliases={2: 0},
        compiler_params=_cparams(("parallel", "parallel")), name=name)(a, b, stack)


def _row_tile(s):
    return min(256, s)


def _norm_mod_f(x, g, scale, shift):
    r = lax.rsqrt(jnp.mean(x * x, axis=-1, keepdims=True) + EPS)
    return (x * r * g) * (1.0 + scale) + shift


def norm_mod(x, g, scale, shift, *, name):
    s = x.shape[0]
    t = _row_tile(s)

    def body(x_ref, g_ref, sc_ref, sh_ref, o_ref):
        o_ref[...] = _norm_mod_f(x_ref[...], g_ref[...], sc_ref[...], sh_ref[...]).astype(o_ref.dtype)

    vec = pl.BlockSpec((1, D), lambda i: (0, 0))
    return pl.pallas_call(
        body, grid=(s // t,), in_specs=[pl.BlockSpec((t, D), lambda i: (i, 0)), vec, vec, vec],
        out_specs=pl.BlockSpec((t, D), lambda i: (i, 0)), out_shape=_sds((s, D), MMT),
        compiler_params=_cparams(("parallel",)), name=name)(x, g, scale, shift)


def norm_mod_bwd(x, dh, dres, g, scale, shift, *, name):
    s = x.shape[0]
    t = _row_tile(s)

    def body(x_ref, dh_ref, dr_ref, g_ref, sc_ref, sh_ref, dx_ref, dg_ref, dsc_ref, dsh_ref):
        @pl.when(pl.program_id(0) == 0)
        def _():
            dg_ref[...] = jnp.zeros_like(dg_ref)
            dsc_ref[...] = jnp.zeros_like(dsc_ref)
            dsh_ref[...] = jnp.zeros_like(dsh_ref)

        _, vjp = jax.vjp(_norm_mod_f, x_ref[...], g_ref[...], sc_ref[...], sh_ref[...])
        dx, dg, dsc, dsh = vjp(dh_ref[...])
        dx_ref[...] = dr_ref[...] + dx
        dg_ref[...] += dg
        dsc_ref[...] += dsc
        dsh_ref[...] += dsh

    row = pl.BlockSpec((t, D), lambda i: (i, 0))
    vec = pl.BlockSpec((1, D), lambda i: (0, 0))
    return pl.pallas_call(
        body, grid=(s // t,), in_specs=[row, row, row, vec, vec, vec], out_specs=[row, vec, vec, vec],
        out_shape=[_sds((s, D), F32)] + [_sds((1, D), F32)] * 3,
        compiler_params=_cparams(("arbitrary",)), name=name)(x, dh, dres, g, scale, shift)


def gate_bwd(dx, y, gate, *, name):
    s = dx.shape[0]
    t = _row_tile(s)

    def body(dx_ref, y_ref, g_ref, dy_ref, dg_ref):
        @pl.when(pl.program_id(0) == 0)
        def _():
            dg_ref[...] = jnp.zeros_like(dg_ref)

        dxv = dx_ref[...]
        dy_ref[...] = (g_ref[...] * dxv).astype(dy_ref.dtype)
        dg_ref[...] += jnp.sum(dxv * y_ref[...], axis=0, keepdims=True)

    row = pl.BlockSpec((t, D), lambda i: (i, 0))
    vec = pl.BlockSpec((1, D), lambda i: (0, 0))
    return pl.pallas_call(
        body, grid=(s // t,), in_specs=[row, row, vec], out_specs=[row, vec],
        out_shape=[_sds((s, D), MMT), _sds((1, D), F32)],
        compiler_params=_cparams(("arbitrary",)), name=name)(dx, y, gate)


def loss_and_grad(xf, target, *, name):
    s = xf.shape[0]
    t = _row_tile(s)

    def body(x_ref, t_ref, l_ref, dx_ref):
        @pl.when(pl.program_id(0) == 0)
        def _():
            l_ref[...] = jnp.zeros_like(l_ref)

        e = x_ref[...] - t_ref[...]
        dx_ref[...] = e * (1.0 / D)
        l_ref[...] += 0.5 * jnp.sum(jnp.sum(e * e, axis=1, keepdims=True), axis=0, keepdims=True) * (1.0 / D)

    row = pl.BlockSpec((t, D), lambda i: (i, 0))
    return pl.pallas_call(
        body, grid=(s // t,), in_specs=[row, row], out_specs=[pl.BlockSpec((1, 1), lambda i: (0, 0)), row],
        out_shape=[_sds((1, 1), F32), _sds((s, D), F32)],
        compiler_params=_cparams(("arbitrary",)), name=name)(xf, target)


def _ret_consts():
    log_g = np.log1p(-np.exp2(-5.0 - np.arange(NH, dtype=np.float32))).astype(np.float32)
    idx = np.arange(CH, dtype=np.float32)
    d_intra = np.exp(np.abs(idx[:, None] - idx[None, :])[None] * log_g[:, None, None]).astype(np.float32)
    k_w = np.exp((CH - 1.0 - idx)[None, :] * log_g[:, None]).astype(np.float32)
    q_w = np.exp((idx + 1.0)[None, :] * log_g[:, None]).astype(np.float32)
    g_chunk = [float(v) for v in np.exp(np.float32(CH) * log_g).astype(np.float32)]
    bc = lambda a: np.ascontiguousarray(np.broadcast_to(a[:, :, None], (NH, CH, HD)))
    return jnp.asarray(d_intra), jnp.asarray(bc(k_w)), jnp.asarray(bc(q_w)), g_chunk


def _rope_tables(s):
    half = HD // 2
    inv_freq = (ROPE_BASE ** (-np.arange(half, dtype=np.float64) / half)).astype(np.float32)
    ang = (np.arange(s, dtype=np.float32)[:, None] * inv_freq[None, :]).astype(np.float64)
    cos, sin = np.cos(ang).astype(np.float32), np.sin(ang).astype(np.float32)
    return jnp.asarray(np.concatenate([cos, cos], axis=1)), jnp.asarray(np.concatenate([-sin, sin], axis=1))


def _ret_chunk(qs, ks, vs, rs, cos, sin, dintra, kw, qw, g_chunk):
    outs, rn = [], []
    for h in range(NH):
        q = qs[h] * cos + _swap_halves(qs[h]) * sin
        k = (ks[h] * cos + _swap_halves(ks[h]) * sin) * (HD ** -0.5)
        sc = _bdot_nt(q, k) * dintra[h]
        outs.append(_bdot(sc, vs[h]) + _bdot(q * qw[h], rs[h]))
        rn.append(g_chunk[h] * rs[h] + _bdot_tn(k * kw[h], vs[h]))
    return outs, rn


def _heads(x):
    return [x[:, h * HD:(h + 1) * HD] for h in range(NH)]


def retention_fwd(zz, cosf, sinf, *, name):
    s = zz.shape[0]
    n = s // CH
    dintra, kw, qw, g_chunk = _ret_consts()

    def body(q_ref, k_ref, v_ref, c_ref, s_ref, di_ref, kw_ref, qw_ref, o_ref, rp_ref, r_scr):
        @pl.when(pl.program_id(0) == 0)
        def _():
            r_scr[...] = jnp.zeros_like(r_scr)

        rprev = r_scr[...]
        rp_ref[0] = rprev
        outs, rn = _ret_chunk(_heads(q_ref[...]), _heads(k_ref[...]), _heads(v_ref[...]),
                              [rprev[h * HD:(h + 1) * HD] for h in range(NH)], c_ref[...], s_ref[...],
                              [di_ref[h] for h in range(NH)], [kw_ref[h] for h in range(NH)],
                              [qw_ref[h] for h in range(NH)], g_chunk)
        o_ref[...] = jnp.concatenate(outs, axis=1)
        r_scr[...] = jnp.concatenate(rn, axis=0)

    col = lambda c: pl.BlockSpec((CH, BW), lambda i: (i, c // BW))
    tab = pl.BlockSpec((CH, HD), lambda i: (i, 0))
    cst = lambda shp: pl.BlockSpec(shp, lambda i: (0,) * len(shp))
    return pl.pallas_call(
        body, grid=(n,),
        in_specs=[col(RQ), col(RK), col(RV), tab, tab, cst((NH, CH, CH)), cst((NH, CH, HD)), cst((NH, CH, HD))],
        out_specs=[pl.BlockSpec((CH, BW), lambda i: (i, 0)), pl.BlockSpec((1, BW, HD), lambda i: (i, 0, 0))],
        out_shape=[_sds((s, BW), F32), _sds((n, BW, HD), F32)],
        scratch_shapes=[pltpu.VMEM((BW, HD), F32)],
        compiler_params=_cparams(("arbitrary",)), name=name)(zz, zz, zz, cosf, sinf, dintra, kw, qw)


def retention_bwd(zz, cosf, sinf, rprev, do, *, name):
    s = zz.shape[0]
    n = s // CH
    dintra, kw, qw, g_chunk = _ret_consts()

    def body(q_ref, k_ref, v_ref, c_ref, s_ref, di_ref, kw_ref, qw_ref, rp_ref, do_ref, dz_ref, dr_scr):
        @pl.when(pl.program_id(0) == 0)
        def _():
            dr_scr[...] = jnp.zeros_like(dr_scr)

        rprev_v = rp_ref[0]
        f = functools.partial(_ret_chunk, cos=c_ref[...], sin=s_ref[...],
                              dintra=[di_ref[h] for h in range(NH)], kw=[kw_ref[h] for h in range(NH)],
                              qw=[qw_ref[h] for h in range(NH)], g_chunk=g_chunk)
        _, vjp = jax.vjp(f, _heads(q_ref[...]), _heads(k_ref[...]), _heads(v_ref[...]),
                         [rprev_v[h * HD:(h + 1) * HD] for h in range(NH)])
        dr = dr_scr[...]
        dq, dk, dv, drp = vjp((_heads(do_ref[...]), [dr[h * HD:(h + 1) * HD] for h in range(NH)]))
        dz_ref[...] = jnp.concatenate(dq + dk + dv, axis=1).astype(dz_ref.dtype)
        dr_scr[...] = jnp.concatenate(drp, axis=0)

    col = lambda c: pl.BlockSpec((CH, BW), lambda i: (n - 1 - i, c // BW))
    tab = pl.BlockSpec((CH, HD), lambda i: (n - 1 - i, 0))
    cst = lambda shp: pl.BlockSpec(shp, lambda i: (0,) * len(shp))
    return pl.pallas_call(
        body, grid=(n,),
        in_specs=[col(RQ), col(RK), col(RV), tab, tab, cst((NH, CH, CH)), cst((NH, CH, HD)), cst((NH, CH, HD)),
                  pl.BlockSpec((1, BW, HD), lambda i: (n - 1 - i, 0, 0)), pl.BlockSpec((CH, BW), lambda i: (n - 1 - i, 0))],
        out_specs=pl.BlockSpec((CH, 3 * BW), lambda i: (n - 1 - i, 0)),
        out_shape=_sds((s, 3 * BW), MMT),
        scratch_shapes=[pltpu.VMEM((BW, HD), F32)],
        compiler_params=_cparams(("arbitrary",)), name=name)(zz, zz, zz, cosf, sinf, dintra, kw, qw, rprev, do)


GKW = NH * GDK


def _gla_consts():
    tri = np.tril(np.ones((CH, CH), np.float32))
    mask_t = np.zeros((BW, GKW), np.float32)
    for h in range(NH):
        mask_t[h * HD:(h + 1) * HD, h * GDK:(h + 1) * GDK] = 1.0
    return jnp.asarray(tri), jnp.asarray(mask_t)


def _gla_chunk(q, k, v, glr, w_a2, b_a, st, tri, mask_t):
    la = _log_sigmoid(_bdot(glr, w_a2) + b_a) * (1.0 / 16.0)
    bc = _dot(tri, la, HI)
    be = jnp.sum(la, axis=0, keepdims=True)
    kv_t = _bdot_tn(v, k * jnp.exp(be - bc)) * mask_t
    sn = jnp.exp(be) * st + kv_t
    return _bdot_nt(q * (GDK ** -0.5), sn), sn


def gla_fwd(zz, w_a2p, b_a, *, name):
    s = zz.shape[0]
    n = s // CH
    tri, mask_t = _gla_consts()

    def body(q_ref, k_ref, v_ref, lr_ref, w_ref, b_ref, tri_ref, m_ref, o_ref, sp_ref, st_scr):
        @pl.when(pl.program_id(0) == 0)
        def _():
            st_scr[...] = jnp.zeros_like(st_scr)

        sp = st_scr[...]
        sp_ref[0] = sp
        o, sn = _gla_chunk(q_ref[...], k_ref[...], v_ref[...], lr_ref[...], w_ref[...], b_ref[...], sp,
                           tri_ref[...], m_ref[...])
        o_ref[...] = o
        st_scr[...] = sn

    cst = lambda shp: pl.BlockSpec(shp, lambda i: (0,) * len(shp))
    return pl.pallas_call(
        body, grid=(n,),
        in_specs=[pl.BlockSpec((CH, GKW), lambda i: (i, GQ // GKW)), pl.BlockSpec((CH, GKW), lambda i: (i, GK // GKW)),
                  pl.BlockSpec((CH, BW), lambda i: (i, GV // BW)), pl.BlockSpec((CH, HD), lambda i: (i, LR // HD)),
                  cst((HD, GKW)), cst((1, GKW)), cst((CH, CH)), cst((BW, GKW))],
        out_specs=[pl.BlockSpec((CH, BW), lambda i: (i, 0)), pl.BlockSpec((1, BW, GKW), lambda i: (i, 0, 0))],
        out_shape=[_sds((s, BW), F32), _sds((n, BW, GKW), F32)],
        scratch_shapes=[pltpu.VMEM((BW, GKW), F32)],
        compiler_params=_cparams(("arbitrary",)), name=name)(zz, zz, zz, zz, w_a2p, b_a, tri, mask_t)


def gla_bwd(zz, w_a2p, b_a, sprev, do, *, name):
    s = zz.shape[0]
    n = s // CH
    tri, mask_t = _gla_consts()

    def body(q_ref, k_ref, v_ref, lr_ref, w_ref, b_ref, tri_ref, m_ref, sp_ref, do_ref,
             dz_ref, dlr_ref, dw_ref, db_ref, ds_scr):
        @pl.when(pl.program_id(0) == 0)
        def _():
            ds_scr[...] = jnp.zeros_like(ds_scr)
            dw_ref[...] = jnp.zeros_like(dw_ref)
            db_ref[...] = jnp.zeros_like(db_ref)

        f = functools.partial(_gla_chunk, tri=tri_ref[...], mask_t=m_ref[...])
        _, vjp = jax.vjp(f, q_ref[...], k_ref[...], v_ref[...], lr_ref[...], w_ref[...], b_ref[...], sp_ref[0])
        dq, dk, dv, dlr, dw, db, dsp = vjp((do_ref[...], ds_scr[...]))
        dz_ref[...] = jnp.concatenate([dq, dk, dv], axis=1).astype(dz_ref.dtype)
        dlr_ref[...] = dlr.astype(dlr_ref.dtype)
        dw_ref[...] += dw
        db_ref[...] += db
        ds_scr[...] = dsp

    cst = lambda shp: pl.BlockSpec(shp, lambda i: (0,) * len(shp))
    r = lambda i: n - 1 - i
    return pl.pallas_call(
        body, grid=(n,),
        in_specs=[pl.BlockSpec((CH, GKW), lambda i: (r(i), GQ // GKW)), pl.BlockSpec((CH, GKW), lambda i: (r(i), GK // GKW)),
                  pl.BlockSpec((CH, BW), lambda i: (r(i), GV // BW)), pl.BlockSpec((CH, HD), lambda i: (r(i), LR // HD)),
                  cst((HD, GKW)), cst((1, GKW)), cst((CH, CH)), cst((BW, GKW)),
                  pl.BlockSpec((1, BW, GKW), lambda i: (r(i), 0, 0)), pl.BlockSpec((CH, BW), lambda i: (r(i), 0))],
        out_specs=[pl.BlockSpec((CH, 2 * GKW + BW), lambda i: (r(i), 0)), pl.BlockSpec((CH, HD), lambda i: (r(i), 0)),
                   cst((HD, GKW)), cst((1, GKW))],
        out_shape=[_sds((s, 2 * GKW + BW), MMT), _sds((s, HD), MMT), _sds((HD, GKW), F32), _sds((1, GKW), F32)],
        scratch_shapes=[pltpu.VMEM((BW, GKW), F32)],
        compiler_params=_cparams(("arbitrary",)), name=name)(zz, zz, zz, zz, w_a2p, b_a, tri, mask_t, sprev, do)


def _fox_pre_f(fqs, fks, ff, gq, gk, bf):
    def rms(x, g):
        return x * lax.rsqrt(jnp.mean(x * x, axis=-1, keepdims=True) + EPS) * g

    qn = [rms(x, gq) * (HD ** -0.5) for x in fqs]
    kn = [rms(x, gk) for x in fks]
    return qn, kn, _log_sigmoid(ff + bf)


def fox_pre(zz, gq, gk, bf, *, name):
    s = zz.shape[0]
    t = _row_tile(s)
    tri = jnp.asarray(np.tril(np.ones((t, t), np.float32)))

    def body(q_ref, k_ref, f_ref, gq_ref, gk_ref, b_ref, tri_ref, qn_ref, kn_ref, cum_ref, carry):
        @pl.when(pl.program_id(0) == 0)
        def _():
            carry[...] = jnp.zeros_like(carry)

        qn, kn, lf = _fox_pre_f(_heads(q_ref[...]), _heads(k_ref[...]), f_ref[...], gq_ref[...], gk_ref[...], b_ref[...])
        qn_ref[...] = jnp.concatenate(qn, axis=1).astype(qn_ref.dtype)
        kn_ref[...] = jnp.concatenate(kn, axis=1).astype(kn_ref.dtype)
        cum_ref[...] = _dot(tri_ref[...], lf, HI) + carry[...]
        carry[...] += jnp.sum(lf, axis=0, keepdims=True)

    vec = pl.BlockSpec((1, HD), lambda i: (0, 0))
    return pl.pallas_call(
        body, grid=(s // t,),
        in_specs=[pl.BlockSpec((t, BW), lambda i: (i, FQ // BW)), pl.BlockSpec((t, BW), lambda i: (i, FK // BW)),
                  pl.BlockSpec((t, HD), lambda i: (i, FF // HD)), vec, vec, vec, pl.BlockSpec((t, t), lambda i: (0, 0))],
        out_specs=[pl.BlockSpec((t, BW), lambda i: (i, 0)), pl.BlockSpec((t, BW), lambda i: (i, 0)),
                   pl.BlockSpec((t, HD), lambda i: (i, 0))],
        out_shape=[_sds((s, BW), MMT), _sds((s, BW), MMT), _sds((s, HD), F32)],
        scratch_shapes=[pltpu.VMEM((1, HD), F32)],
        compiler_params=_cparams(("arbitrary",)), name=name)(zz, zz, zz, gq, gk, bf, tri)


def fox_pre_bwd(zz, gq, gk, bf, dqn, dkn, dcum, *, name):
    s = zz.shape[0]
    t = _row_tile(s)
    nt = s // t
    triu = jnp.asarray(np.triu(np.ones((t, t), np.float32)))

    def body(q_ref, k_ref, f_ref, gq_ref, gk_ref, b_ref, tri_ref, dqn_ref, dkn_ref, dcum_ref,
             dz_ref, dff_ref, dgq_ref, dgk_ref, db_ref, carry):
        @pl.when(pl.program_id(0) == 0)
        def _():
            carry[...] = jnp.zeros_like(carry)
            dgq_ref[...] = jnp.zeros_like(dgq_ref)
            dgk_ref[...] = jnp.zeros_like(dgk_ref)
            db_ref[...] = jnp.zeros_like(db_ref)

        dcum_v = dcum_ref[...]
        dlf = _dot(tri_ref[...], dcum_v, HI) + carry[...]
        carry[...] += jnp.sum(dcum_v, axis=0, keepdims=True)
        _, vjp = jax.vjp(_fox_pre_f, _heads(q_ref[...]), _heads(k_ref[...]), f_ref[...], gq_ref[...], gk_ref[...], b_ref[...])
        dq, dk, dff, dgq, dgk, db = vjp((_heads(dqn_ref[...]), _heads(dkn_ref[...]), dlf))
        dz_ref[...] = jnp.concatenate(dq + dk, axis=1).astype(dz_ref.dtype)
        dff_ref[...] = dff.astype(dff_ref.dtype)
        dgq_ref[...] += dgq
        dgk_ref[...] += dgk
        db_ref[...] += db

    r = lambda i: nt - 1 - i
    vec = pl.BlockSpec((1, HD), lambda i: (0, 0))
    return pl.pallas_call(
        body, grid=(nt,),
        in_specs=[pl.BlockSpec((t, BW), lambda i: (r(i), FQ // BW)), pl.BlockSpec((t, BW), lambda i: (r(i), FK // BW)),
                  pl.BlockSpec((t, HD), lambda i: (r(i), FF // HD)), vec, vec, vec, pl.BlockSpec((t, t), lambda i: (0, 0)),
                  pl.BlockSpec((t, BW), lambda i: (r(i), 0)), pl.BlockSpec((t, BW), lambda i: (r(i), 0)),
                  pl.BlockSpec((t, HD), lambda i: (r(i), 0))],
        out_specs=[pl.BlockSpec((t, 2 * BW), lambda i: (r(i), 0)), pl.BlockSpec((t, HD), lambda i: (r(i), 0)), vec, vec, vec],
        out_shape=[_sds((s, 2 * BW), MMT), _sds((s, HD), MMT), _sds((1, HD), F32), _sds((1, HD), F32), _sds((1, HD), F32)],
        scratch_shapes=[pltpu.VMEM((1, HD), F32)],
        compiler_params=_cparams(("arbitrary",)), name=name)(zz, zz, zz, gq, gk, bf, triu, dqn, dkn, dcum)


def _fox_blocks(s):
    return min(256, s), min(512, s)


NEG = -1e30


def fox_attn_fwd(qn, kn, zz, cum_col, cum_row, *, name):
    s = qn.shape[0]
    bq, bk = _fox_blocks(s)

    def body(q_ref, k_ref, v_ref, cc_ref, cr_ref, o_ref, lse_ref):
        qi = pl.program_id(1)
        q = q_ref[...]
        cq = cc_ref[...]
        rows = qi * bq + lax.broadcasted_iota(jnp.int32, (bq, bk), 0)
        cols0 = lax.broadcasted_iota(jnp.int32, (bq, bk), 1)

        def step(j, carry):
            m, l, acc = carry
            off = pl.multiple_of(j * bk, bk)
            k = k_ref[pl.ds(off, bk), :]
            v = v_ref[pl.ds(off, bk), :].astype(MMT)
            sc = _dot_nt(q, k) + cq - cr_ref[pl.ds(j, 1), :]
            sc = jnp.where(rows >= cols0 + j * bk, sc, NEG)
            m_new = jnp.maximum(m, jnp.max(sc, axis=1, keepdims=True))
            alpha = jnp.exp(m - m_new)
            p = jnp.exp(sc - m_new)
            return m_new, alpha * l + jnp.sum(p, axis=1, keepdims=True), alpha * acc + _dot(p.astype(MMT), v)

        nk = ((qi + 1) * bq + bk - 1) // bk
        m, l, acc = lax.fori_loop(0, nk, step, (jnp.full((bq, 1), NEG, F32), jnp.zeros((bq, 1), F32),
                                                jnp.zeros((bq, HD), F32)))
        o_ref[...] = acc / l
        lse_ref[...] = m + jnp.log(l)

    return pl.pallas_call(
        body, grid=(NH, s // bq),
        in_specs=[pl.BlockSpec((bq, HD), lambda h, i: (i, h)), pl.BlockSpec((s, HD), lambda h, i: (0, h)),
                  pl.BlockSpec((s, HD), lambda h, i: (0, FV // HD + h)),
                  pl.BlockSpec((None, bq, 1), lambda h, i: (h, i, 0)), pl.BlockSpec((None, s // bk, bk), lambda h, i: (h, 0, 0))],
        out_specs=[pl.BlockSpec((bq, HD), lambda h, i: (i, h)), pl.BlockSpec((None, bq, 1), lambda h, i: (h, i, 0))],
        out_shape=[_sds((s, BW), F32), _sds((NH, s, 1), F32)],
        compiler_params=_cparams(("parallel", "parallel")), name=name)(qn, kn, zz, cum_col, cum_row)


def fox_attn_bwd(qn, kn, zz, cum_col, cum_row, lse, do, *, name):
    s = qn.shape[0]
    bq, bk = _fox_blocks(s)
    nkc = s // bk

    def body(q_ref, k_ref, v_ref, cc_ref, cr_ref, lse_ref, do_ref, dq_ref, dk_ref, dv_ref, dc_ref, p_scr, dp_scr):
        qi = pl.program_id(1)

        @pl.when(qi == 0)
        def _():
            dk_ref[...] = jnp.zeros_like(dk_ref)
            dv_ref[...] = jnp.zeros_like(dv_ref)
            dc_ref[...] = jnp.zeros_like(dc_ref)

        q = q_ref[...]
        dob = do_ref[...].astype(MMT)
        cq = cc_ref[...]
        lse_v = lse_ref[...]
        rows = qi * bq + lax.broadcasted_iota(jnp.int32, (bq, bk), 0)
        cols0 = lax.broadcasted_iota(jnp.int32, (bq, bk), 1)
        nk = ((qi + 1) * bq + bk - 1) // bk

        def probs(j, delta):
            off = pl.multiple_of(j * bk, bk)
            sc = _dot_nt(q, k_ref[pl.ds(off, bk), :]) + cq - cr_ref[pl.ds(j, 1), :]
            p = jnp.where(rows >= cols0 + j * bk, jnp.exp(sc - lse_v), 0.0)
            dp = _dot_nt(dob, v_ref[pl.ds(off, bk), :].astype(MMT))
            p_scr[j] = p
            dp_scr[j] = dp
            return delta + jnp.sum(p * dp, axis=1, keepdims=True)

        delta = lax.fori_loop(0, nk, probs, jnp.zeros((bq, 1), F32))

        def grads(j, dq):
            off = pl.multiple_of(j * bk, bk)
            p = p_scr[j]
            ds = p * (dp_scr[j] - delta)
            dsm = ds.astype(MMT)
            dv_ref[pl.ds(off, bk), :] += _dot_tn(p.astype(MMT), dob)
            dk_ref[pl.ds(off, bk), :] += _dot_tn(dsm, q)
            dc_ref[pl.ds(j, 1), :] -= jnp.sum(ds, axis=0, keepdims=True)
            return dq + _dot(dsm, k_ref[pl.ds(off, bk), :])

        dq_ref[...] = lax.fori_loop(0, nk, grads, jnp.zeros((bq, HD), F32))

    full = lambda c0=0: pl.BlockSpec((s, HD), lambda h, i: (0, c0 + h))
    blk = lambda: pl.BlockSpec((bq, HD), lambda h, i: (i, h))
    colv = lambda: pl.BlockSpec((None, bq, 1), lambda h, i: (h, i, 0))
    rowv = lambda: pl.BlockSpec((None, nkc, bk), lambda h, i: (h, 0, 0))
    return pl.pallas_call(
        body, grid=(NH, s // bq),
        in_specs=[blk(), full(), full(FV // HD), colv(), rowv(), colv(), blk()],
        out_specs=[blk(), full(), full(), rowv()],
        out_shape=[_sds((s, BW), F32), _sds((s, BW), F32), _sds((s, BW), F32), _sds((NH, nkc, bk), F32)],
        scratch_shapes=[pltpu.VMEM((nkc, bq, bk), F32), pltpu.VMEM((nkc, bq, bk), F32)],
        compiler_params=_cparams(("parallel", "arbitrary")), name=name)(qn, kn, zz, cum_col, cum_row, lse, do)


def _branch_f(rets, rgs, glas, ggs, ret_g, gla_g):
    out_r, out_g = [], []
    for h in range(NH):
        xc = rets[h] - jnp.mean(rets[h], axis=-1, keepdims=True)
        y = xc * lax.rsqrt(jnp.mean(xc * xc, axis=-1, keepdims=True) + EPS) * ret_g[h]
        out_r.append(_silu(rgs[h]) * y)
        x = glas[h]
        y = x * lax.rsqrt(jnp.mean(x * x, axis=-1, keepdims=True) + EPS) * gla_g
        out_g.append(_silu(ggs[h]) * y)
    return out_r, out_g


def _w_br_spec(layer):
    return pl.BlockSpec((None, 3, BW, D), lambda i: (layer, 0, 0, 0))


def mix_fwd(ret_raw, gla_raw, fox_o, zz, ret_g, gla_g, b_mg, w_br, *, name, layer):
    s = zz.shape[0]
    t = _row_tile(s)

    def body(r_ref, g_ref, f_ref, rg_ref, gg_ref, gp_ref, rgn_ref, ggn_ref, bmg_ref, w_ref, o_ref):
        rgn = rgn_ref[...]
        br_r, br_g = _branch_f(_heads(r_ref[...]), _heads(rg_ref[...]), _heads(g_ref[...]), _heads(gg_ref[...]),
                               _heads(rgn), ggn_ref[...])
        brs = [jnp.concatenate(br_r, axis=1), jnp.concatenate(br_g, axis=1), f_ref[...]]
        acc = jnp.zeros((t, D), F32)
        for b in range(3):
            gate = jax.nn.sigmoid(gp_ref[:, b * D:(b + 1) * D] + bmg_ref[:, b * D:(b + 1) * D])
            acc = acc + gate * _dot(brs[b].astype(MMT), w_ref[b])
        o_ref[...] = acc.astype(o_ref.dtype)

    row = lambda w, c=0: pl.BlockSpec((t, w), lambda i: (i, c // w))
    cst = lambda shp: pl.BlockSpec(shp, lambda i: (0,) * len(shp))
    return pl.pallas_call(
        body, grid=(s // t,),
        in_specs=[row(BW), row(BW), row(BW), row(BW, RG), row(BW, GG), row(3 * D, GP), cst((1, BW)), cst((1, HD)),
                  cst((1, 3 * D)), _w_br_spec(layer)],
        out_specs=row(D), out_shape=_sds((s, D), MMT),
        compiler_params=_cparams(("parallel",)), name=name)(ret_raw, gla_raw, fox_o, zz, zz, zz, ret_g, gla_g, b_mg, w_br)


def mix_bwd(ret_raw, gla_raw, fox_o, zz, ret_g, gla_g, b_mg, w_br, dmi, *, name, layer):
    s = zz.shape[0]
    t = _row_tile(s)

    def body(r_ref, g_ref, f_ref, rg_ref, gg_ref, gp_ref, rgn_ref, ggn_ref, bmg_ref, w_ref, dmi_ref,
             dr_ref, dg_ref, df_ref, drg_ref, dgg_ref, dgp_ref, dw_ref, drgn_ref, dggn_ref, dbmg_ref):
        @pl.when(pl.program_id(0) == 0)
        def _():
            dw_ref[...] = jnp.zeros_like(dw_ref)
            drgn_ref[...] = jnp.zeros_like(drgn_ref)
            dggn_ref[...] = jnp.zeros_like(dggn_ref)
            dbmg_ref[...] = jnp.zeros_like(dbmg_ref)

        (br_r, br_g), vjp = jax.vjp(_branch_f, _heads(r_ref[...]), _heads(rg_ref[...]), _heads(g_ref[...]),
                                    _heads(gg_ref[...]), _heads(rgn_ref[...]), ggn_ref[...])
        brs = [jnp.concatenate(br_r, axis=1).astype(MMT), jnp.concatenate(br_g, axis=1).astype(MMT),
               f_ref[...].astype(MMT)]
        dmi_v = dmi_ref[...].astype(F32)
        dbr = []
        for b in range(3):
            w = w_ref[b]
            ybr = _dot(brs[b], w)
            gate = jax.nn.sigmoid(gp_ref[:, b * D:(b + 1) * D] + bmg_ref[:, b * D:(b + 1) * D])
            dgp = dmi_v * ybr * gate * (1.0 - gate)
            dgp_ref[:, b * D:(b + 1) * D] = dgp.astype(dgp_ref.dtype)
            dbmg_ref[:, b * D:(b + 1) * D] += jnp.sum(dgp, axis=0, keepdims=True)
            dy = (dmi_v * gate).astype(MMT)
            dw_ref[b] += _dot_tn(brs[b], dy)
            dbr.append(_dot_nt(dy, w))
        dr, drg, dg, dgg, drgn, dggn = vjp((_heads(dbr[0]), _heads(dbr[1])))
        dr_ref[...] = jnp.concatenate(dr, axis=1)
        dg_ref[...] = jnp.concatenate(dg, axis=1)
        df_ref[...] = dbr[2]
        drg_ref[...] = jnp.concatenate(drg, axis=1).astype(drg_ref.dtype)
        dgg_ref[...] = jnp.concatenate(dgg, axis=1).astype(dgg_ref.dtype)
        drgn_ref[...] += jnp.concatenate(drgn, axis=1)
        dggn_ref[...] += dggn

    row = lambda w, c=0: pl.BlockSpec((t, w), lambda i: (i, c // w))
    cst = lambda shp: pl.BlockSpec(shp, lambda i: (0,) * len(shp))
    return pl.pallas_call(
        body, grid=(s // t,),
        in_specs=[row(BW), row(BW), row(BW), row(BW, RG), row(BW, GG), row(3 * D, GP), cst((1, BW)), cst((1, HD)),
                  cst((1, 3 * D)), _w_br_spec(layer), row(D)],
        out_specs=[row(BW), row(BW), row(BW), row(BW), row(BW), row(3 * D), cst((3, BW, D)), cst((1, BW)), cst((1, HD)),
                   cst((1, 3 * D))],
        out_shape=[_sds((s, BW), F32)] * 3 + [_sds((s, BW), MMT)] * 2 + [_sds((s, 3 * D), MMT), _sds((3, BW, D), F32),
                                                                          _sds((1, BW), F32), _sds((1, HD), F32),
                                                                          _sds((1, 3 * D), F32)],
        compiler_params=_cparams(("arbitrary",)), name=name)(ret_raw, gla_raw, fox_o, zz, zz, zz, ret_g, gla_g, b_mg, w_br, dmi)


CT = 256


def _shift_down(x, k, rows):
    return jnp.where(rows >= k, pltpu.roll(x, k, 0), 0.0)


def _shift_up(x, k, rows, s):
    return jnp.where(rows < s - k, pltpu.roll(x, s - k, 0), 0.0)


def conv_fwd(ug, w_conv, b_conv, *, name):
    s = ug.shape[0]
    nt = DFF // CT

    def body(u_ref, g_ref, w_ref, b_ref, a_ref):
        u = u_ref[...]
        rows = lax.broadcasted_iota(jnp.int32, u.shape, 0)
        uc = b_ref[...] + w_ref[0:1, :] * _shift_down(u, 2, rows) + w_ref[1:2, :] * _shift_down(u, 1, rows) + w_ref[2:3, :] * u
        a_ref[...] = (_silu(uc) * g_ref[...]).astype(a_ref.dtype)

    return pl.pallas_call(
        body, grid=(nt,),
        in_specs=[pl.BlockSpec((s, CT), lambda j: (0, j)), pl.BlockSpec((s, CT), lambda j: (0, nt + j)),
                  pl.BlockSpec((3, CT), lambda j: (0, j)), pl.BlockSpec((1, CT), lambda j: (0, j))],
        out_specs=pl.BlockSpec((s, CT), lambda j: (0, j)), out_shape=_sds((s, DFF), MMT),
        compiler_params=_cparams(("parallel",)), name=name)(ug, ug, w_conv, b_conv)


def conv_bwd(ug, w_conv, b_conv, da, *, name):
    s = ug.shape[0]
    nt = DFF // CT

    def body(u_ref, g_ref, w_ref, b_ref, da_ref, du_ref, dg_ref, dw_ref, db_ref):
        u = u_ref[...]
        rows = lax.broadcasted_iota(jnp.int32, u.shape, 0)
        u2, u1 = _shift_down(u, 2, rows), _shift_down(u, 1, rows)
        uc = b_ref[...] + w_ref[0:1, :] * u2 + w_ref[1:2, :] * u1 + w_ref[2:3, :] * u
        sg = jax.nn.sigmoid(uc)
        da_v = da_ref[...]
        dg_ref[...] = (da_v * uc * sg).astype(dg_ref.dtype)
        duc = da_v * g_ref[...] * sg * (1.0 + uc * (1.0 - sg))
        du = w_ref[2:3, :] * duc + w_ref[1:2, :] * _shift_up(duc, 1, rows, s) + w_ref[0:1, :] * _shift_up(duc, 2, rows, s)
        du_ref[...] = du.astype(du_ref.dtype)
        dw_ref[0:1, :] = jnp.sum(duc * u2, axis=0, keepdims=True)
        dw_ref[1:2, :] = jnp.sum(duc * u1, axis=0, keepdims=True)
        dw_ref[2:3, :] = jnp.sum(duc * u, axis=0, keepdims=True)
        db_ref[...] = jnp.sum(duc, axis=0, keepdims=True)

    col = lambda: pl.BlockSpec((s, CT), lambda j: (0, j))
    return pl.pallas_call(
        body, grid=(nt,),
        in_specs=[col(), pl.BlockSpec((s, CT), lambda j: (0, nt + j)), pl.BlockSpec((3, CT), lambda j: (0, j)),
                  pl.BlockSpec((1, CT), lambda j: (0, j)), col()],
        out_specs=[col(), col(), pl.BlockSpec((3, CT), lambda j: (0, j)), pl.BlockSpec((1, CT), lambda j: (0, j))],
        out_shape=[_sds((s, DFF), MMT), _sds((s, DFF), MMT), _sds((3, DFF), F32), _sds((1, DFF), F32)],
        compiler_params=_cparams(("parallel",)), name=name)(ug, ug, w_conv, b_conv, da)


def _tiles(s):
    return min(1024, s)


def layer_fwd(x, mod, p, cosf, sinf):
    s = x.shape[0]
    tm = _tiles(s)
    l = p["l"]
    shift1, scale1, gate1, shift2, scale2, gate2 = mod
    h = norm_mod(x, p["norm1_g"], scale1, shift1, name="norm_mod")
    zz = mm_nn(h, p["w1"], tm=tm, tn=768, out_dtype=F32, name="mm_w1", layer=l)
    ret_raw, rprev = retention_fwd(zz, cosf, sinf, name="ret_fwd")
    gla_raw, sprev = gla_fwd(zz, p["w_a2p"], p["b_gla_a"], name="gla_fwd")
    qn, kn, cum = fox_pre(zz, p["q_norm_g"], p["k_norm_g"], p["b_foxp"], name="fox_pre")
    bq, bk = _fox_blocks(s)
    cum_t = cum[:, :NH].T
    cum_col, cum_row = cum_t[:, :, None], cum_t.reshape(NH, s // bk, bk)
    fox_o, lse = fox_attn_fwd(qn, kn, zz, cum_col, cum_row, name="fox_fwd")
    mi = mix_fwd(ret_raw, gla_raw, fox_o, zz, p["ret_norm_g"], p["gla_norm_g"], p["b_mg"], p["w_br"], name="mix_fwd",
                 layer=l)
    x1, mixed = mm_nn_residual(mi, p["w_o"], x, gate1, tm=tm, tn=512, name="mm_wo", layer=l)
    h2 = norm_mod(x1, p["norm2_g"], scale2, shift2, name="norm_mod")
    ug = mm_nn(h2, p["w_up"], tm=tm, tn=512, out_dtype=F32, name="mm_wup", layer=l)
    a = conv_fwd(ug, p["w_conv"], p["b_conv"], name="conv_fwd")
    x2, y = mm_nn_residual(a, p["w_down"], x1, gate2, tm=tm, tn=512, name="mm_wdown", layer=l)
    saved = dict(x=x, h=h, zz=zz, ret_raw=ret_raw, rprev=rprev, gla_raw=gla_raw, sprev=sprev, qn=qn, kn=kn,
                 cum_col=cum_col, cum_row=cum_row, fox_o=fox_o, lse=lse, mi=mi, mixed=mixed, x1=x1, h2=h2, ug=ug, a=a, y=y)
    return x2, saved


def layer_bwd(dx2, mod, p, sv, cosf, sinf, stacks, slot):
    s = dx2.shape[0]
    tm = _tiles(s)
    l = p["l"]
    shift1, scale1, gate1, shift2, scale2, gate2 = mod
    g, stacks = {}, dict(stacks)
    dy, dgate2 = gate_bwd(dx2, sv["y"], gate2, name="gate_bwd")
    stacks["w_down"] = mm_tn(sv["a"], dy, tm=min(1408, DFF), tn=512, out_dtype=MMT, name="mm_dwdown",
                             stack=stacks["w_down"], layer=slot)
    da = mm_nt(dy, p["w_down"], tm=tm, tn=1408, out_dtype=F32, name="mm_da", layer=l)
    du, dg, g["w_conv"], g["b_conv"] = conv_bwd(sv["ug"], p["w_conv"], p["b_conv"], da, name="conv_bwd")
    dug = jnp.concatenate([du, dg], axis=1)
    stacks["w_up"] = mm_tn(sv["h2"], dug, tm=512, tn=512, out_dtype=MMT, name="mm_dwup", stack=stacks["w_up"], layer=slot)
    dh2 = mm_nt(dug, p["w_up"], tm=min(512, s), tn=512, out_dtype=F32, name="mm_dh2", layer=l)
    dx1, g["norm2_g"], dscale2, dshift2 = norm_mod_bwd(sv["x1"], dh2, dx2, p["norm2_g"], scale2, shift2, name="norm_mod_bwd")
    dmixed, dgate1 = gate_bwd(dx1, sv["mixed"], gate1, name="gate_bwd")
    stacks["w_o"] = mm_tn(sv["mi"], dmixed, tm=512, tn=512, out_dtype=MMT, name="mm_dwo", stack=stacks["w_o"], layer=slot)
    dmi = mm_nt(dmixed, p["w_o"], tm=tm, tn=512, out_dtype=MMT, name="mm_dmi", layer=l)
    zz = sv["zz"]
    (dret, dgla, dfox, drg, dgg, dgp, g["w_br"], g["ret_norm_g"], g["gla_norm_g"], g["b_mg"]) = mix_bwd(
        sv["ret_raw"], sv["gla_raw"], sv["fox_o"], zz, p["ret_norm_g"], p["gla_norm_g"], p["b_mg"], p["w_br"], dmi,
        name="mix_bwd", layer=l)
    dqn, dkn, dfv, dcum_row = fox_attn_bwd(sv["qn"], sv["kn"], zz, sv["cum_col"], sv["cum_row"], sv["lse"], dfox,
                                           name="fox_bwd")
    dfv = dfv.astype(MMT)
    dcum = jnp.pad(dcum_row.reshape(NH, s).T, ((0, 0), (0, HD - NH)))
    dfqk, dff, g["q_norm_g"], g["k_norm_g"], g["b_foxp"] = fox_pre_bwd(
        zz, p["q_norm_g"], p["k_norm_g"], p["b_foxp"], dqn, dkn, dcum, name="fox_pre_bwd")
    dgqkv, dlr, g["w_a2p"], g["b_gla_a"] = gla_bwd(zz, p["w_a2p"], p["b_gla_a"], sv["sprev"], dgla, name="gla_bwd")
    drqkv = retention_bwd(zz, cosf, sinf, sv["rprev"], dret, name="ret_bwd")
    dzz = jnp.concatenate([dgp, drqkv, drg, dgqkv, dgg, dfqk, dfv, dlr, dff], axis=1)
    stacks["w_mg"] = mm_tn(sv["h"], dzz, tm=512, tn=768, out_dtype=MMT, name="mm_dwmg", ncols=RQ, stack=stacks["w_mg"],
                           layer=slot)
    dwz = mm_tn(sv["h"], dzz, tm=512, tn=768, out_dtype=MMT, name="mm_dwz", col0=RQ)
    stacks["w_in"] = unalign_dw_in(dwz, stacks["w_in"], slot)
    dh = mm_nt(dzz, p["w1"], tm=min(512, s), tn=512, out_dtype=F32, name="mm_dh", layer=l)
    dx, g["norm1_g"], dscale1, dshift1 = norm_mod_bwd(sv["x"], dh, dx1, p["norm1_g"], scale1, shift1, name="norm_mod_bwd")
    dmod = jnp.concatenate([dshift1, dscale1, dgate1, dshift2, dscale2, dgate2], axis=1)
    return dx, g, dmod, stacks


def _align_cols(w_in, w_mg):
    z = lambda n: jnp.zeros((w_in.shape[0], n), w_in.dtype)
    return jnp.concatenate([w_mg, w_in[:, :3072], w_in[:, 3088:5136], w_in[:, 3072:3088], z(HD - GLR),
                            w_in[:, 5136:5140], z(HD - NH)], axis=1)


def _unalign_cols(dwz):
    o = lambda c: c - RQ
    return jnp.concatenate([dwz[:, :o(GG)], dwz[:, o(LR):o(LR) + GLR], dwz[:, o(GG):o(LR)], dwz[:, o(FF):o(FF) + NH]], axis=1)


def build_w1(w_in_sh, w_mg):
    nl = w_mg.shape[0]
    t = _row_tile(D)

    def body(s_ref, g_ref, o_ref):
        o_ref[...] = _align_cols(jnp.concatenate([s_ref[k] for k in range(4)], axis=1), g_ref[...])

    return pl.pallas_call(
        body, grid=(nl, D // t),
        in_specs=[pl.BlockSpec((None, 4, t, IN_W // 4), lambda l, i: (l, 0, i, 0)), pl.BlockSpec((None, t, RQ), lambda l, i: (l, i, 0))],
        out_specs=pl.BlockSpec((None, t, NZZ), lambda l, i: (l, i, 0)), out_shape=_sds((nl, D, NZZ), w_mg.dtype),
        compiler_params=_cparams(("parallel", "parallel")), name="build_w1")(w_in_sh, w_mg)


def unalign_dw_in(dwz, stack, layer):
    t = _row_tile(D)

    def body(z_ref, s_ref, o_ref):
        w = _unalign_cols(z_ref[...])
        for k in range(4):
            o_ref[k] = w[:, k * (IN_W // 4):(k + 1) * (IN_W // 4)]

    return pl.pallas_call(
        body, grid=(D // t,),
        in_specs=[pl.BlockSpec((t, NZZ - RQ), lambda i: (i, 0)), pl.BlockSpec(memory_space=pl.ANY)],
        out_specs=pl.BlockSpec((None, 4, t, IN_W // 4), lambda i: (layer, 0, i, 0)), out_shape=_sds(stack.shape, stack.dtype),
        input_output_aliases={1: 0}, compiler_params=_cparams(("parallel",)), name="unalign_dw_in")(dwz, stack)


def layer_params(w, big, l):
    row = lambda v: v[l][None, :]
    return dict(
        l=0, norm1_g=row(w["norm1_g"]), norm2_g=row(w["norm2_g"]), w1=big["w1"],
        w_a2p=jnp.pad(w["w_gla_a2"][l], ((0, HD - GLR), (0, 0))), b_gla_a=row(w["b_gla_a"]),
        b_foxp=jnp.pad(row(w["b_fox_f"]), ((0, 0), (0, HD - NH))), ret_norm_g=row(w["ret_norm_g"]),
        gla_norm_g=row(w["gla_norm_g"]), q_norm_g=row(w["q_norm_g"]), k_norm_g=row(w["k_norm_g"]),
        w_br=big["w_br"], b_mg=row(w["b_mg"]), w_o=big["w_o"], w_up=big["w_up"], w_conv=w["w_conv"][l],
        b_conv=row(w["b_conv"]), w_down=big["w_down"])


def layer_grads(g):
    vec = lambda v: v[0]
    return dict(
        norm1_g=vec(g["norm1_g"]), norm2_g=vec(g["norm2_g"]), w_gla_a2=g["w_a2p"][:GLR], b_gla_a=vec(g["b_gla_a"]),
        b_fox_f=g["b_foxp"][0, :NH], ret_norm_g=vec(g["ret_norm_g"]), gla_norm_g=vec(g["gla_norm_g"]),
        q_norm_g=vec(g["q_norm_g"]), k_norm_g=vec(g["k_norm_g"]), w_br=g["w_br"], b_mg=vec(g["b_mg"]),
        w_conv=g["w_conv"], b_conv=vec(g["b_conv"]))


def ada_mod(c_all, w_ada, b_ada):
    nl, _, n = w_ada.shape

    def body(c_ref, w_ref, b_ref, o_ref):
        o_ref[...] = _dot(_silu(c_ref[...]), w_ref[...], HI) + b_ref[...]

    return pl.pallas_call(
        body, grid=(nl,),
        in_specs=[pl.BlockSpec((8, D), lambda l: (0, 0)), pl.BlockSpec((None, D, n), lambda l: (l, 0, 0)),
                  pl.BlockSpec((None, 1, n), lambda l: (l, 0, 0))],
        out_specs=pl.BlockSpec((None, 8, n), lambda l: (l, 0, 0)), out_shape=_sds((nl, 8, n), F32),
        compiler_params=_cparams(("parallel",)), name="ada_mod")(c_all, w_ada, b_ada)


def ada_dw(c_all, dmod):
    nl, _, n = dmod.shape

    def body(c_ref, d_ref, o_ref):
        o_ref[...] = _dot_tn(_silu(c_ref[...]), d_ref[...], HI)

    return pl.pallas_call(
        body, grid=(nl,),
        in_specs=[pl.BlockSpec((8, D), lambda l: (0, 0)), pl.BlockSpec((None, 8, n), lambda l: (l, 0, 0))],
        out_specs=pl.BlockSpec((None, D, n), lambda l: (l, 0, 0)), out_shape=_sds((nl, D, n), F32),
        compiler_params=_cparams(("parallel",)), name="ada_dw")(c_all, dmod)


def sum_devices(g):
    def body(g_ref, o_ref):
        acc = g_ref[0]
        for d in range(1, 8):
            acc = acc + g_ref[d]
        o_ref[...] = acc

    return pl.pallas_call(body, out_shape=_sds(g.shape[1:], F32), name="sum_devices")(g)


def adamw(w, g, m, v, *, block, name):
    nd = w.ndim
    grid = tuple(w.shape[i] // block[i] for i in range(nd))
    bc1 = 1.0 - ADAM_B1 ** ADAM_STEP
    bc2 = 1.0 - ADAM_B2 ** ADAM_STEP

    def body(w_ref, g_ref, m_ref, v_ref, d_ref, nm_ref, nv_ref):
        gv = g_ref[...]
        nm = ADAM_B1 * m_ref[...] + (1.0 - ADAM_B1) * gv
        nv = ADAM_B2 * v_ref[...] + (1.0 - ADAM_B2) * (gv * gv)
        nm_ref[...] = nm
        nv_ref[...] = nv
        d_ref[...] = -ADAM_LR * ((nm / bc1) / (jnp.sqrt(nv / bc2) + ADAM_EPS) + ADAM_WD * w_ref[...])

    spec = pl.BlockSpec(tuple(block), lambda *i: i)
    return pl.pallas_call(
        body, grid=grid, in_specs=[spec] * 4, out_specs=[spec] * 3, out_shape=[_sds(w.shape, F32)] * 3,
        compiler_params=_cparams(("parallel",) * nd), name=name)(w, g, m, v)


MESH = pl.DeviceIdType.MESH
ANY = pl.BlockSpec(memory_space=pl.ANY)
VM = pl.BlockSpec(memory_space=pltpu.VMEM)


def _place():
    x, y, c = lax.axis_index("x"), lax.axis_index("y"), lax.axis_index("c")
    return x, y, c, [(1 - x, y), (x, 1 - y), (1 - x, 1 - y)]


def small_allgather(v, *, name):
    m_per, n = v.shape

    def body(x_ref, out_ref, send_sems, recv_sems, local_sem):
        x, y, c, chips = _place()
        me, sibling = (x, y, c), (x, y, 1 - c)

        def rows(px, py, pc):
            return out_ref.at[pl.ds((4 * px + 2 * py + pc) * m_per, m_per), :]

        def copy(k, block, to, src=None):
            return pltpu.make_async_remote_copy(
                src_ref=rows(*block) if src is None else src, dst_ref=rows(*block),
                send_sem=send_sems.at[k], recv_sem=recv_sems.at[k], device_id=to, device_id_type=MESH)

        mine = pltpu.make_async_copy(x_ref, rows(*me), local_sem)
        mine.start()
        first = [copy(0, me, sibling, src=x_ref)]
        first += [copy(1 + j, me, (*chip, c), src=x_ref) for j, chip in enumerate(chips)]
        for cp in first:
            cp.start()
        passed = [copy(4 + j, (*chip, c), sibling) for j, chip in enumerate(chips)]
        for j, chip in enumerate(chips):
            copy(1 + j, (*chip, c), me).wait_recv()
            passed[j].start()
        copy(0, sibling, me).wait_recv()
        for j, chip in enumerate(chips):
            copy(4 + j, (*chip, 1 - c), me).wait_recv()
        for cp in first + passed:
            cp.wait_send()
        mine.wait()

    return pl.pallas_call(
        body, out_shape=_sds((8 * m_per, n), v.dtype), in_specs=[VM], out_specs=VM,
        scratch_shapes=[pltpu.SemaphoreType.DMA((7,)), pltpu.SemaphoreType.DMA((7,)), pltpu.SemaphoreType.DMA],
        name=name)(v)


TENSORS = {
    "w_in": ("lead", None, (4, D, 1285), (1, D, 1285)),
    "w_mg": ("col", 768, (D, 3072), (512, 3072)),
    "w_br": ("col", 256, (3, BW, D), (3, BW, D)),
    "w_o": ("row", 256, (D, D), (D, D)),
    "w_up": ("col", 1408, (D, 5632), (256, 5632)),
    "w_down": ("row", 704, (DFF, D), (704, D)),
}
BIG = tuple(TENSORS)


def _shard_shape(name):
    kind, width, full, _ = TENSORS[name]
    if kind == "lead":
        return full[1:]
    return full[:-1] + (width,) if kind == "col" else (width,) + full[1:]


def _shard_view(ref, layers, name, k):
    kind, width, full, _ = TENSORS[name]
    if kind == "lead":
        return ref.at[layers, k]
    if kind == "row":
        return ref.at[layers, pl.ds(k * width, width)]
    return ref.at[(layers,) + (slice(None),) * (len(full) - 1) + (pl.ds(k * width, width),)]


def _remote(send_sems, recv_sems, k, src, dst, to):
    return pltpu.make_async_remote_copy(src_ref=src, dst_ref=dst, send_sem=send_sems.at[k], recv_sem=recv_sems.at[k],
                                        device_id=to, device_id_type=MESH)


def _dma_sems(n):
    return [pltpu.SemaphoreType.DMA((n,)), pltpu.SemaphoreType.DMA((n,))]


RS_GROUP = 2
HBM = pl.BlockSpec(memory_space=pltpu.HBM)
SEM = pl.BlockSpec(memory_space=pltpu.SEMAPHORE)
SPLIT_CALL = dict(compiler_params=pltpu.CompilerParams(has_side_effects=pltpu.SideEffectType.DATAFLOW_SIDE_EFFECTING))
PULL_SET = (("w_up", "w_down", "w_o"), ("w_in", "w_mg", "w_br"))


def _in_hbm(a):
    return pltpu.with_memory_space_constraint(a, pltpu.HBM)


def _pull_sends(send_sems, recv_sems, p, o, layer, core, x, y, chips):
    return [_remote(send_sems, recv_sems, 3 * BIG.index(n) + j, p[n].at[layer], _shard_view(o[n], 0, n, 2 * x + y), (*chip, core))
            for n in PULL_SET[core] for j, chip in enumerate(chips)]


def _pull_arrivals(send_sems, recv_sems, o, core, x, y, chips, to):
    views = [(3 * BIG.index(n) + j, _shard_view(o[n], 0, n, 2 * chip[0] + chip[1]))
             for n in PULL_SET[core] for j, chip in enumerate(chips)]
    return [_remote(send_sems, recv_sems, k, v, v, to) for k, v in views]


def gather_start(shards, layer, *, name):
    nt = len(BIG)

    def body(*refs):
        p, o = dict(zip(BIG, refs[:nt])), dict(zip(BIG, refs[nt:2 * nt]))
        x, y, c, chips = _place()
        for core in (0, 1):
            @pl.when(c == core)
            def _():
                for cp in _pull_sends(refs[2 * nt], refs[2 * nt + 1], p, o, layer, core, x, y, chips):
                    cp.start()
        refs[-1][...] = jnp.zeros_like(refs[-1])

    lands = [_in_hbm(lax.empty((1,) + TENSORS[n][2], shards[n].dtype)) for n in BIG]
    outs = pl.pallas_call(
        body,
        out_shape=(pltpu.SemaphoreType.DMA((3 * nt,)), pltpu.SemaphoreType.DMA((3 * nt,)),
                   *[pltpu.HBM(a.shape, a.dtype) for a in lands], _sds((8, HD), F32)),
        in_specs=[HBM] * (2 * nt), out_specs=(SEM, SEM, *[HBM] * nt, VM),
        input_output_aliases={nt + t: 2 + t for t in range(nt)}, name=name, **SPLIT_CALL)(
            *[_in_hbm(shards[n]) for n in BIG], *lands)
    return outs[0], outs[1], outs[2:2 + nt], outs[-1]


def gather_wait(send_sems, recv_sems, shards, lands, after, layer, *, name):
    nt = len(BIG)

    def body(*refs):
        p, o = dict(zip(BIG, refs[:nt])), dict(zip(BIG, refs[nt:2 * nt]))
        ss, rs = refs[2 * nt], refs[2 * nt + 1]
        x, y, c, chips = _place()
        for core in (0, 1):
            @pl.when(c == core)
            def _():
                for cp in _pull_sends(ss, rs, p, o, layer, core, x, y, chips):
                    cp.wait_send()
                for cp in _pull_arrivals(ss, rs, o, core, x, y, chips, (x, y, core)):
                    cp.wait_recv()

    return pl.pallas_call(
        body, out_shape=tuple(pltpu.HBM(a.shape, a.dtype) for a in lands),
        in_specs=[HBM] * (2 * nt) + [SEM, SEM, ANY], out_specs=tuple([HBM] * nt),
        input_output_aliases={nt + t: t for t in range(nt)}, name=name, **SPLIT_CALL)(
            *[_in_hbm(shards[n]) for n in BIG], *lands, send_sems, recv_sems, after)


def gather_forward(shards, lands, layer, *, name):
    nt = len(BIG)

    def body(*refs):
        p, o = dict(zip(BIG, refs[:nt])), dict(zip(BIG, refs[2 * nt:3 * nt]))
        ss, rs = refs[3 * nt:]
        x, y, c, chips = _place()
        for core in (0, 1):
            @pl.when(c == core)
            def _():
                me, sibling = (x, y, core), (x, y, 1 - core)
                sends = _pull_arrivals(ss, rs, o, core, x, y, chips, sibling)
                sends += [_remote(ss, rs, 3 * nt + t, p[n].at[layer], _shard_view(o[n], 0, n, 2 * x + y), sibling)
                          for t, n in enumerate(BIG)]
                for cp in sends:
                    cp.start()
                for cp in sends:
                    cp.wait_send()
                for cp in _pull_arrivals(ss, rs, o, 1 - core, x, y, chips, me):
                    cp.wait_recv()
                for t, n in enumerate(BIG):
                    own = _shard_view(o[n], 0, n, 2 * x + y)
                    _remote(ss, rs, 3 * nt + t, own, own, me).wait_recv()

    outs = pl.pallas_call(
        body, out_shape=[_sds(a.shape, a.dtype) for a in lands], in_specs=[ANY] * (2 * nt), out_specs=[ANY] * nt,
        input_output_aliases={nt + t: t for t in range(nt)}, scratch_shapes=_dma_sems(4 * nt), name=name)(
            *[shards[n] for n in BIG], *lands)
    return dict(zip(BIG, outs))


def pair_exchange(g, *, name):
    hh = g[BIG[0]].shape[0] // 2
    nt = len(BIG)

    def body(*refs):
        send_sems, recv_sems = refs[2 * nt:]
        x, y, c, _ = _place()
        copies = [_remote(send_sems, recv_sems, t, refs[t].at[pl.ds(hh * (1 - c), hh)], refs[nt + t], (x, y, 1 - c))
                  for t in range(nt)]
        for cp in copies:
            cp.start()
        for cp in copies:
            cp.wait()

    outs = pl.pallas_call(
        body, out_shape=[_sds((hh,) + g[n].shape[1:], g[n].dtype) for n in BIG], in_specs=[ANY] * nt, out_specs=[ANY] * nt,
        scratch_shapes=_dma_sems(nt), name=name)(*[g[n] for n in BIG])
    return dict(zip(BIG, outs))


def _chip_copies(send_sems, recv_sems, s_refs, land_refs, c, chips):
    hl = s_refs[0].shape[0]
    return [_remote(send_sems, recv_sems, 3 * t + j, _shard_view(s_refs[t], pl.ds(0, hl), n, 2 * chip[0] + chip[1]),
                    land_refs[t].at[j], (*chip, c))
            for t, n in enumerate(BIG) for j, chip in enumerate(chips)]


def _landing_shapes(s):
    hl = s[BIG[0]].shape[0]
    return [_sds((3, hl) + _shard_shape(n), s[n].dtype) for n in BIG]


def chip_exchange(s, *, name):
    nt = len(BIG)

    def body(*refs):
        send_sems, recv_sems = refs[2 * nt:]
        x, y, c, chips = _place()
        copies = _chip_copies(send_sems, recv_sems, refs[:nt], refs[nt:2 * nt], c, chips)
        for cp in copies:
            cp.start()
        for cp in copies:
            cp.wait()

    outs = pl.pallas_call(
        body, out_shape=_landing_shapes(s), in_specs=[ANY] * nt, out_specs=[ANY] * nt,
        scratch_shapes=_dma_sems(3 * nt), name=name)(*[s[n] for n in BIG])
    return dict(zip(BIG, outs))


def chip_exchange_start(s, *, name):
    nt = len(BIG)

    def body(*refs):
        send_sems, recv_sems = refs[2 * nt:2 * nt + 2]
        x, y, c, chips = _place()
        for cp in _chip_copies(send_sems, recv_sems, refs[:nt], refs[nt:2 * nt], c, chips):
            cp.start()
        refs[-1][...] = jnp.zeros_like(refs[-1])

    hbm = lambda a: pltpu.with_memory_space_constraint(a, pltpu.HBM)
    lands = [hbm(lax.empty(d.shape, d.dtype)) for d in _landing_shapes(s)]
    srcs = [hbm(s[n]) for n in BIG]
    outs = pl.pallas_call(
        body,
        out_shape=(pltpu.SemaphoreType.DMA((3 * nt,)), pltpu.SemaphoreType.DMA((3 * nt,)),
                   *[pltpu.HBM(a.shape, a.dtype) for a in srcs + lands], _sds((8, HD), F32)),
        in_specs=[HBM] * (2 * nt), out_specs=(SEM, SEM, *[HBM] * (2 * nt), VM),
        input_output_aliases={t: 2 + t for t in range(2 * nt)},
        compiler_params=pltpu.CompilerParams(has_side_effects=pltpu.SideEffectType.DATAFLOW_SIDE_EFFECTING),
        name=name)(*srcs, *lands)
    return outs[0], outs[1], outs[2:2 + nt], outs[2 + nt:2 + 2 * nt], outs[-1]


def chip_exchange_wait(send_sems, recv_sems, srcs, lands, after, *, name):
    nt = len(BIG)

    def body(*refs):
        x, y, c, chips = _place()
        for cp in _chip_copies(refs[2 * nt], refs[2 * nt + 1], refs[:nt], refs[nt:2 * nt], c, chips):
            cp.wait_send()
            cp.wait_recv()

    outs = pl.pallas_call(
        body, out_shape=tuple(pltpu.HBM(a.shape, a.dtype) for a in list(srcs) + list(lands)),
        in_specs=[HBM] * (2 * nt) + [SEM, SEM, ANY], out_specs=tuple([HBM] * (2 * nt)),
        input_output_aliases={t: t for t in range(2 * nt)},
        compiler_params=pltpu.CompilerParams(has_side_effects=pltpu.SideEffectType.DATAFLOW_SIDE_EFFECTING),
        name=name)(*srcs, *lands, send_sems, recv_sems, after)
    return dict(zip(BIG, outs[:nt])), dict(zip(BIG, outs[nt:]))


def pair_share(f, l0, hh, *, name):
    nt = len(BIG)

    def body(*refs):
        o = refs[nt:2 * nt]
        send_sems, recv_sems = refs[2 * nt:]
        x, y, c, _ = _place()
        mine, theirs = pl.ds(l0 + hh * c, hh), pl.ds(l0 + hh * (1 - c), hh)
        copies = [_remote(send_sems, recv_sems, t, o[t].at[mine], o[t].at[mine], (x, y, 1 - c)) for t in range(nt)]
        for cp in copies:
            cp.start()
        for t, cp in enumerate(copies):
            cp.wait_send()
            _remote(send_sems, recv_sems, t, o[t].at[theirs], o[t].at[theirs], (x, y, c)).wait_recv()

    outs = pl.pallas_call(
        body, out_shape=[_sds(f[n].shape, f[n].dtype) for n in BIG], in_specs=[ANY] * nt, out_specs=[ANY] * nt,
        input_output_aliases={t: t for t in range(nt)}, scratch_shapes=_dma_sems(nt), name=name)(*[f[n] for n in BIG])
    return dict(zip(BIG, outs))


def pair_add(g, r, idx, *, tensor, name):
    _, _, full, blk = TENSORS[tensor]
    hh = r.shape[0]

    def body(idx_ref, g_ref, r_ref, o_ref):
        o_ref[...] = (g_ref[...].astype(F32) + r_ref[...].astype(F32)).astype(o_ref.dtype)

    own = pl.BlockSpec((None,) + blk, lambda *a: (a[0],) + a[1:-1])
    return pl.pallas_call(
        body, out_shape=_sds(r.shape, r.dtype),
        grid_spec=pltpu.PrefetchScalarGridSpec(
            num_scalar_prefetch=1, grid=(hh,) + tuple(f // b for f, b in zip(full, blk)),
            in_specs=[pl.BlockSpec((None,) + blk, lambda *a: (hh * a[-1][0] + a[0],) + a[1:-1]), own], out_specs=own),
        compiler_params=_cparams(("parallel",) * (1 + len(full))), name=name)(idx, g, r)


def chip_add(s, r, idx, totals, l0, *, tensor, name):
    kind, width, full, _ = TENSORS[tensor]
    shard = _shard_shape(tensor)
    hh = s.shape[0]
    zeros = (0,) * len(shard)

    def body(idx_ref, s_ref, r0_ref, r1_ref, r2_ref, t_ref, o_ref):
        o_ref[...] = ((s_ref[...].astype(F32) + r0_ref[...].astype(F32)) + r1_ref[...].astype(F32)) + r2_ref[...].astype(F32)

    if kind == "lead":
        mine = pl.BlockSpec((None, None) + shard, lambda i, ix: (i, ix[1]) + zeros)
    elif kind == "row":
        mine = pl.BlockSpec((None,) + shard, lambda i, ix: (i, ix[1]) + zeros[1:])
    else:
        mine = pl.BlockSpec((None,) + shard, lambda i, ix: (i,) + zeros[1:] + (ix[1],))
    peer = lambda j: pl.BlockSpec((None, None) + shard, lambda i, ix: (j, i) + zeros)
    return pl.pallas_call(
        body, out_shape=_sds(totals.shape, F32),
        grid_spec=pltpu.PrefetchScalarGridSpec(
            num_scalar_prefetch=1, grid=(hh,), in_specs=[mine, peer(0), peer(1), peer(2), pl.BlockSpec(memory_space=pl.ANY)],
            out_specs=pl.BlockSpec((None,) + shard, lambda i, ix: (l0 + hh * ix[0] + i,) + zeros)),
        input_output_aliases={5: 0}, compiler_params=_cparams(("parallel",)), name=name)(idx, s, r, r, r, totals)


def _flat_rows(arrs):
    v = jnp.concatenate([a.reshape(-1) for a in arrs])
    n = -(-v.shape[0] // 1024) * 1024
    return jnp.pad(v, (0, n - v.shape[0])).reshape(n // HD, HD)


def _unflat(buf, shapes):
    v, out, o = buf.reshape(-1), [], 0
    for s in shapes:
        n = int(np.prod(s))
        out.append(v[o:o + n].reshape(s))
        o += n
    return out


WEIGHTS = ("norm1_g", "norm2_g", "w_ada", "b_ada", "w_in", "w_gla_a2", "b_gla_a", "b_fox_f", "ret_norm_g", "gla_norm_g",
           "q_norm_g", "k_norm_g", "w_br", "w_mg", "b_mg", "w_o", "w_up", "w_conv", "b_conv", "w_down")
REPLICATED = ("norm1_g", "norm2_g", "b_gla_a", "b_fox_f", "ret_norm_g", "gla_norm_g", "q_norm_g", "k_norm_g", "b_mg", "b_conv")
ADAM_BLOCKS = dict(w_ada=(1, 256, 1536), w_in=(1, 256, 1285), w_br=(1, 3, BW, 256), w_mg=(1, 512, 768), w_o=(2, 256, D),
                   w_up=(1, 256, 1408), w_down=(1, 352, D))
ALL_AXES = ("x", "y", "c")


def kernel(x, c, norm1_g, norm2_g, w_ada, b_ada, w_in, w_gla_a2, b_gla_a, b_fox_f, ret_norm_g, gla_norm_g, q_norm_g, k_norm_g, w_br, w_mg, b_mg, w_o, w_up, w_conv, b_conv, w_down, loss_target, m_norm1_g, m_norm2_g, m_w_ada, m_b_ada, m_w_in, m_w_gla_a2, m_b_gla_a, m_b_fox_f, m_ret_norm_g, m_gla_norm_g, m_q_norm_g, m_k_norm_g, m_w_br, m_w_mg, m_b_mg, m_w_o, m_w_up, m_w_conv, m_b_conv, m_w_down, v_norm1_g, v_norm2_g, v_w_ada, v_b_ada, v_w_in, v_w_gla_a2, v_b_gla_a, v_b_fox_f, v_ret_norm_g, v_gla_norm_g, v_q_norm_g, v_k_norm_g, v_w_br, v_w_mg, v_b_mg, v_w_o, v_w_up, v_w_conv, v_b_conv, v_w_down):
    w = dict(zip(WEIGHTS, (norm1_g, norm2_g, w_ada, b_ada, w_in, w_gla_a2, b_gla_a, b_fox_f, ret_norm_g, gla_norm_g,
                           q_norm_g, k_norm_g, w_br, w_mg, b_mg, w_o, w_up, w_conv, b_conv, w_down)))
    m = dict(zip(WEIGHTS, (m_norm1_g, m_norm2_g, m_w_ada, m_b_ada, m_w_in, m_w_gla_a2, m_b_gla_a, m_b_fox_f, m_ret_norm_g,
                           m_gla_norm_g, m_q_norm_g, m_k_norm_g, m_w_br, m_w_mg, m_b_mg, m_w_o, m_w_up, m_w_conv, m_b_conv,
                           m_w_down)))
    v = dict(zip(WEIGHTS, (v_norm1_g, v_norm2_g, v_w_ada, v_b_ada, v_w_in, v_w_gla_a2, v_b_gla_a, v_b_fox_f, v_ret_norm_g,
                           v_gla_norm_g, v_q_norm_g, v_k_norm_g, v_w_br, v_w_mg, v_b_mg, v_w_o, v_w_up, v_w_conv, v_b_conv,
                           v_w_down)))
    nl = norm1_g.shape[0]
    seq = x.shape[1]
    xi, yi, ci = lax.axis_index("x"), lax.axis_index("y"), lax.axis_index("c")
    k_me = 2 * xi + yi
    b_me = 4 * xi + 2 * yi + ci
    ada_n = w_ada.shape[2]
    a2_n, conv_n = w_gla_a2.shape[2], w_conv.shape[2]

    shards = {n: w[n].astype(MMT) for n in BIG}
    started = gather_start(shards, 0, name="gather0_start")

    def gather_finish(l, started, after):
        send_sems, recv_sems, lands, _ = started
        lands = gather_wait(send_sems, recv_sems, shards, lands, after, l, name=f"gather{l}_wait")
        big = gather_forward(shards, lands, l, name=f"gather{l}_forward")
        big["w1"] = build_w1(big["w_in"], big["w_mg"])
        return big

    blk = _flat_rows([c, w_gla_a2, w_conv]) + started[-1][0, 0]
    g1 = small_allgather(blk, name="gather_small").reshape(8, blk.shape[0], HD)
    c_all = g1[:, :D // HD].reshape(8, D)
    by_chip = g1[0::2].reshape(4, -1)[:, D:]
    a2_sh, conv_sh = by_chip[:, :nl * GLR * a2_n], by_chip[:, nl * GLR * a2_n:nl * (GLR * a2_n + 3 * conv_n)]
    full_small = dict(
        w_gla_a2=a2_sh.reshape(4, nl, GLR, a2_n).transpose(1, 2, 0, 3).reshape(nl, GLR, 4 * a2_n),
        w_conv=conv_sh.reshape(4, nl, 3, conv_n).transpose(1, 2, 0, 3).reshape(nl, 3, 4 * conv_n))

    b_ada_sh = lax.dynamic_slice_in_dim(b_ada, k_me * ada_n, ada_n, axis=1)[:, None, :]
    mod_sh = ada_mod(c_all, w_ada, b_ada_sh)
    g2 = small_allgather(mod_sh.reshape(nl * 8, ada_n), name="gather_mod").reshape(4, 2, nl, 8, ada_n)[:, 0]
    mod_me = lax.dynamic_index_in_dim(g2, b_me, axis=2, keepdims=False).transpose(1, 0, 2).reshape(nl, 4 * ada_n)

    wsmall = {n: w[n] for n in REPLICATED}
    wsmall.update(full_small)
    mods = [[mod_me[l:l + 1, i * D:(i + 1) * D] for i in range(6)] for l in range(nl)]
    big = gather_finish(0, started, mod_me)

    cosf, sinf = _rope_tables(seq)
    xs, saved, params = x[0], [], []
    for l in range(nl):
        if l + 1 < nl:
            started = gather_start(shards, l + 1, name=f"gather{l + 1}_start")
            xs = xs + started[-1][0, 0]
        params.append(layer_params(wsmall, big, l))
        xs, sv = layer_fwd(xs, mods[l], params[l], cosf, sinf)
        saved.append(sv)
        if l + 1 < nl:
            big = gather_finish(l + 1, started, xs)
    loss_part, dx = loss_and_grad(xs, loss_target[0], name="loss")
    loss = lax.psum(loss_part[0, 0], ALL_AXES)
    grads, dmods = [None] * nl, [None] * nl
    idx = jnp.stack([ci, k_me]).astype(jnp.int32)
    totals = {n: lax.empty((nl,) + _shard_shape(n), F32) for n in BIG}

    def finish_group(group, chip_sum, from_chips, totals):
        totals = {n: chip_add(chip_sum[n], from_chips[n], idx, totals[n], RS_GROUP * group, tensor=n,
                              name=f"rs{group}_chip_add_{n}") for n in BIG}
        return pair_share(totals, RS_GROUP * group, RS_GROUP // 2, name=f"rs{group}_pair_share")

    pending = None
    for group in reversed(range(nl // RS_GROUP)):
        layers = range(RS_GROUP * group, RS_GROUP * (group + 1))
        stacks = {n: lax.empty((RS_GROUP,) + TENSORS[n][2], MMT) for n in BIG if n != "w_br"}
        if pending is not None:
            dx = dx + pending[-1][0, 0]
        for l in reversed(layers):
            dx, g, dmods[l], stacks = layer_bwd(dx, mods[l], params[l], saved[l], cosf, sinf, stacks, l - layers[0])
            grads[l] = layer_grads(g)
        stacks["w_br"] = jnp.stack([grads[l]["w_br"].astype(MMT) for l in layers])
        from_sibling = pair_exchange(stacks, name=f"rs{group}_pair_exchange")
        chip_sum = {n: pair_add(stacks[n], from_sibling[n], idx, tensor=n, name=f"rs{group}_pair_add_{n}") for n in BIG}
        if pending is not None:
            earlier, send_sems, recv_sems, srcs, lands, _ = pending
            sums, from_chips = chip_exchange_wait(send_sems, recv_sems, srcs, lands, dx,
                                                  name=f"rs{earlier}_chip_exchange_wait")
            totals = finish_group(earlier, sums, from_chips, totals)
        if group > 0:
            pending = (group, *chip_exchange_start(chip_sum, name=f"rs{group}_chip_exchange_start"))
        else:
            totals = finish_group(0, chip_sum, chip_exchange(chip_sum, name="rs0_chip_exchange"), totals)

    small_names = REPLICATED + ("w_gla_a2", "w_conv")
    small_shapes = [(nl, 6 * D)] + [(nl,) + grads[0][n].shape for n in small_names]
    vec = _flat_rows([jnp.concatenate(dmods, axis=0)] + [jnp.stack([grads[l][n] for l in range(nl)]) for n in small_names])
    gs = small_allgather(vec, name="gather_small_grads").reshape(8, vec.shape[0], HD)
    summed = _unflat(sum_devices(gs), small_shapes)
    grad = dict(zip(small_names, summed[1:]))
    grad["b_ada"] = summed[0]
    grad["w_gla_a2"] = lax.dynamic_slice_in_dim(grad["w_gla_a2"], k_me * a2_n, a2_n, axis=2)
    grad["w_conv"] = lax.dynamic_slice_in_dim(grad["w_conv"], k_me * conv_n, conv_n, axis=2)
    dmod_all = gs[:, :nl * 6 * D // HD].reshape(8, nl, 6 * D)
    dmod_sh = lax.dynamic_slice_in_dim(dmod_all, k_me * ada_n, ada_n, axis=2).transpose(1, 0, 2)
    grad["w_ada"] = ada_dw(c_all, dmod_sh)

    grad.update(totals)

    delta, new_m, new_v = {}, {}, {}
    for n, block in ADAM_BLOCKS.items():
        delta[n], new_m[n], new_v[n] = adamw(w[n], grad[n], m[n], v[n], block=block, name="adamw_" + n)
    rest = [n for n in WEIGHTS if n not in ADAM_BLOCKS]
    shapes = [w[n].shape for n in rest]
    flat = [_flat_rows([t[n] for n in rest]) for t in (w, grad, m, v)]
    outs = adamw(*flat, block=flat[0].shape, name="adamw_small")
    for t, o in zip((delta, new_m, new_v), outs):
        t.update(zip(rest, _unflat(o, shapes)))

    return (loss, dx[None], *[grad[n] for n in WEIGHTS], *[delta[n] for n in WEIGHTS], *[new_m[n] for n in WEIGHTS],
            *[new_v[n] for n in WEIGHTS])
```

```python
import functools

import numpy as np
import jax
import jax.numpy as jnp
from jax import lax
from jax.experimental import pallas as pl
from jax.experimental.pallas import tpu as pltpu

F32 = jnp.float32
MMT = jnp.bfloat16
HI = lax.Precision.HIGHEST

D = 1024
DEPTH = 4
NH = 4
HD = 128
BW = NH * HD
CH = 64
GDK = 64
GLR = 16
DFF = 2816
EPS = 1e-6
ROPE_BASE = 10000.0

GP, RG, GG, FV, RQ, RK, RV, GQ, GK, GV, FQ, FK, LR, FF = (
    0, 3072, 3584, 4096, 4608, 5120, 5632, 6144, 6400, 6656, 7168, 7680, 8192, 8320)
NZZ = 8448
WZ0 = 3072
IN_W = 5140
W_IN_COLS = dict(rqkv=(0, 1536), rg=(1536, 2048), gqkv=(2048, 3072), lr=(3072, 3088), gg=(3088, 3600), fqk=(3600, 4624),
                 fv=(4624, 5136), ff=(5136, 5140))

VMEM_LIMIT = 56 * 1024 * 1024

ADAM_LR, ADAM_B1, ADAM_B2, ADAM_EPS, ADAM_WD, ADAM_STEP = 0.001, 0.9, 0.999, 1e-08, 0.01, 10


def _cparams(sem=None):
    return pltpu.CompilerParams(dimension_semantics=sem, vmem_limit_bytes=VMEM_LIMIT)


def _sds(shape, dtype):
    return jax.ShapeDtypeStruct(tuple(shape), dtype)


def _dot(a, b, precision=None):
    return lax.dot_general(a, b, (((1,), (0,)), ((), ())), precision=precision, preferred_element_type=F32)


def _dot_nt(a, b, precision=None):
    return lax.dot_general(a, b, (((1,), (1,)), ((), ())), precision=precision, preferred_element_type=F32)


def _dot_tn(a, b, precision=None):
    return lax.dot_general(a, b, (((0,), (0,)), ((), ())), precision=precision, preferred_element_type=F32)


def _silu(x):
    return x * jax.nn.sigmoid(x)


def _log_sigmoid(x):
    return jnp.minimum(x, 0.0) - jnp.log(1.0 + jnp.exp(jnp.minimum(x, -x)))


@jax.custom_vjp
def _swap_halves(x):
    return pltpu.roll(x, HD // 2, 1)


_swap_halves.defvjp(lambda x: (_swap_halves(x), None), lambda _, g: (_swap_halves(g),))


@jax.custom_vjp
def _bdot(a, b):
    return _dot(a.astype(MMT), b.astype(MMT))


@jax.custom_vjp
def _bdot_nt(a, b):
    return _dot_nt(a.astype(MMT), b.astype(MMT))


@jax.custom_vjp
def _bdot_tn(a, b):
    return _dot_tn(a.astype(MMT), b.astype(MMT))


_bdot.defvjp(lambda a, b: (_bdot(a, b), (a, b)), lambda r, g: (_bdot_nt(g, r[1]), _bdot_tn(r[0], g)))
_bdot_nt.defvjp(lambda a, b: (_bdot_nt(a, b), (a, b)), lambda r, g: (_bdot(g, r[1]), _bdot_tn(g, r[0])))
_bdot_tn.defvjp(lambda a, b: (_bdot_tn(a, b), (a, b)), lambda r, g: (_bdot_nt(r[1], g), _bdot(r[0], g)))


def _stacked(blk, idx, layer):
    if layer is None:
        return pl.BlockSpec(blk, idx)
    return pl.BlockSpec((None,) + blk, lambda i, j: (layer,) + idx(i, j))


def mm_nn(a, b, *, tm, tn, out_dtype, name, layer=None):
    m, k = a.shape
    n = b.shape[-1]

    def body(a_ref, b_ref, o_ref):
        o_ref[...] = _dot(a_ref[...], b_ref[...]).astype(o_ref.dtype)

    return pl.pallas_call(
        body, grid=(m // tm, n // tn),
        in_specs=[pl.BlockSpec((tm, k), lambda i, j: (i, 0)), _stacked((k, tn), lambda i, j: (0, j), layer)],
        out_specs=pl.BlockSpec((tm, tn), lambda i, j: (i, j)),
        out_shape=_sds((m, n), out_dtype), compiler_params=_cparams(("parallel", "parallel")), name=name)(a, b)


def mm_nn_residual(a, b, res, gate, *, tm, tn, name, layer=None):
    m, k = a.shape
    n = b.shape[-1]

    def body(a_ref, b_ref, r_ref, g_ref, x_ref, y_ref):
        acc = _dot(a_ref[...], b_ref[...])
        y_ref[...] = acc
        x_ref[...] = r_ref[...] + g_ref[...] * acc

    return pl.pallas_call(
        body, grid=(m // tm, n // tn),
        in_specs=[pl.BlockSpec((tm, k), lambda i, j: (i, 0)), _stacked((k, tn), lambda i, j: (0, j), layer),
                  pl.BlockSpec((tm, tn), lambda i, j: (i, j)), pl.BlockSpec((1, tn), lambda i, j: (0, j))],
        out_specs=[pl.BlockSpec((tm, tn), lambda i, j: (i, j)), pl.BlockSpec((tm, tn), lambda i, j: (i, j))],
        out_shape=[_sds((m, n), F32), _sds((m, n), F32)],
        compiler_params=_cparams(("parallel", "parallel")), name=name)(a, b, res, gate)


def mm_nt(a, b, *, tm, tn, out_dtype, name, layer=None):
    m, k = a.shape
    n = b.shape[-2]

    def body(a_ref, b_ref, o_ref):
        o_ref[...] = _dot_nt(a_ref[...], b_ref[...]).astype(o_ref.dtype)

    return pl.pallas_call(
        body, grid=(m // tm, n // tn),
        in_specs=[pl.BlockSpec((tm, k), lambda i, j: (i, 0)), _stacked((tn, k), lambda i, j: (j, 0), layer)],
        out_specs=pl.BlockSpec((tm, tn), lambda i, j: (i, j)),
        out_shape=_sds((m, n), out_dtype), compiler_params=_cparams(("parallel", "parallel")), name=name)(a, b)


def mm_tn(a, b, *, tm, tn, out_dtype, name, col0=0, ncols=None, stack=None, layer=None):
    s, m = a.shape
    n = b.shape[1] - col0 if ncols is None else ncols
    c0 = col0 // tn

    def body(a_ref, b_ref, *rest):
        o_ref = rest[-1]
        o_ref[...] = _dot_tn(a_ref[...], b_ref[...]).astype(o_ref.dtype)

    in_specs = [pl.BlockSpec((s, tm), lambda i, j: (0, i)), pl.BlockSpec((s, tn), lambda i, j: (0, c0 + j))]
    if stack is None:
        return pl.pallas_call(
            body, grid=(m // tm, n // tn), in_specs=in_specs, out_specs=pl.BlockSpec((tm, tn), lambda i, j: (i, j)),
            out_shape=_sds((m, n), out_dtype), compiler_params=_cparams(("parallel", "parallel")), name=name)(a, b)
    return pl.pallas_call(
        body, grid=(m // tm, n // tn), in_specs=in_specs + [pl.BlockSpec(memory_space=pl.ANY)],
        out_specs=pl.BlockSpec((None, tm, tn), lambda i, j: (layer, i, j)),
        out_shape=_sds(stack.shape, stack.dtype), input_output_aliases={2: 0},
        compiler_params=_cparams(("parallel", "parallel")), name=name)(a, b, stack)


def _row_tile(s):
    return min(256, s)


def _norm_mod_f(x, g, scale, shift):
    r = lax.rsqrt(jnp.mean(x * x, axis=-1, keepdims=True) + EPS)
    return (x * r * g) * (1.0 + scale) + shift


def norm_mod(x, g, scale, shift, *, name):
    s = x.shape[0]
    t = _row_tile(s)

    def body(x_ref, g_ref, sc_ref, sh_ref, o_ref):
        o_ref[...] = _norm_mod_f(x_ref[...], g_ref[...], sc_ref[...], sh_ref[...]).astype(o_ref.dtype)

    vec = pl.BlockSpec((1, D), lambda i: (0, 0))
    return pl.pallas_call(
        body, grid=(s // t,), in_specs=[pl.BlockSpec((t, D), lambda i: (i, 0)), vec, vec, vec],
        out_specs=pl.BlockSpec((t, D), lambda i: (i, 0)), out_shape=_sds((s, D), MMT),
        compiler_params=_cparams(("parallel",)), name=name)(x, g, scale, shift)


def norm_mod_bwd(x, dh, dres, g, scale, shift, *, name):
    s = x.shape[0]
    t = _row_tile(s)

    def body(x_ref, dh_ref, dr_ref, g_ref, sc_ref, sh_ref, dx_ref, dg_ref, dsc_ref, dsh_ref):
        @pl.when(pl.program_id(0) == 0)
        def _():
            dg_ref[...] = jnp.zeros_like(dg_ref)
            dsc_ref[...] = jnp.zeros_like(dsc_ref)
            dsh_ref[...] = jnp.zeros_like(dsh_ref)

        _, vjp = jax.vjp(_norm_mod_f, x_ref[...], g_ref[...], sc_ref[...], sh_ref[...])
        dx, dg, dsc, dsh = vjp(dh_ref[...])
        dx_ref[...] = dr_ref[...] + dx
        dg_ref[...] += dg
        dsc_ref[...] += dsc
        dsh_ref[...] += dsh

    row = pl.BlockSpec((t, D), lambda i: (i, 0))
    vec = pl.BlockSpec((1, D), lambda i: (0, 0))
    return pl.pallas_call(
        body, grid=(s // t,), in_specs=[row, row, row, vec, vec, vec], out_specs=[row, vec, vec, vec],
        out_shape=[_sds((s, D), F32)] + [_sds((1, D), F32)] * 3,
        compiler_params=_cparams(("arbitrary",)), name=name)(x, dh, dres, g, scale, shift)


def gate_bwd(dx, y, gate, *, name):
    s = dx.shape[0]
    t = _row_tile(s)

    def body(dx_ref, y_ref, g_ref, dy_ref, dg_ref):
        @pl.when(pl.program_id(0) == 0)
        def _():
            dg_ref[...] = jnp.zeros_like(dg_ref)

        dxv = dx_ref[...]
        dy_ref[...] = (g_ref[...] * dxv).astype(dy_ref.dtype)
        dg_ref[...] += jnp.sum(dxv * y_ref[...], axis=0, keepdims=True)

    row = pl.BlockSpec((t, D), lambda i: (i, 0))
    vec = pl.BlockSpec((1, D), lambda i: (0, 0))
    return pl.pallas_call(
        body, grid=(s // t,), in_specs=[row, row, vec], out_specs=[row, vec],
        out_shape=[_sds((s, D), MMT), _sds((1, D), F32)],
        compiler_params=_cparams(("arbitrary",)), name=name)(dx, y, gate)


def loss_and_grad(xf, target, *, name):
    s = xf.shape[0]
    t = _row_tile(s)

    def body(x_ref, t_ref, l_ref, dx_ref):
        @pl.when(pl.program_id(0) == 0)
        def _():
            l_ref[...] = jnp.zeros_like(l_ref)

        e = x_ref[...] - t_ref[...]
        dx_ref[...] = e * (1.0 / D)
        l_ref[...] += 0.5 * jnp.sum(jnp.sum(e * e, axis=1, keepdims=True), axis=0, keepdims=True) * (1.0 / D)

    row = pl.BlockSpec((t, D), lambda i: (i, 0))
    return pl.pallas_call(
        body, grid=(s // t,), in_specs=[row, row], out_specs=[pl.BlockSpec((1, 1), lambda i: (0, 0)), row],
        out_shape=[_sds((1, 1), F32), _sds((s, D), F32)],
        compiler_params=_cparams(("arbitrary",)), name=name)(xf, target)


def _ret_consts():
    log_g = np.log1p(-np.exp2(-5.0 - np.arange(NH, dtype=np.float32))).astype(np.float32)
    idx = np.arange(CH, dtype=np.float32)
    d_intra = np.exp(np.abs(idx[:, None] - idx[None, :])[None] * log_g[:, None, None]).astype(np.float32)
    k_w = np.exp((CH - 1.0 - idx)[None, :] * log_g[:, None]).astype(np.float32)
    q_w = np.exp((idx + 1.0)[None, :] * log_g[:, None]).astype(np.float32)
    g_chunk = [float(v) for v in np.exp(np.float32(CH) * log_g).astype(np.float32)]
    bc = lambda a: np.ascontiguousarray(np.broadcast_to(a[:, :, None], (NH, CH, HD)))
    return jnp.asarray(d_intra), jnp.asarray(bc(k_w)), jnp.asarray(bc(q_w)), g_chunk


def _rope_tables(s):
    half = HD // 2
    inv_freq = (ROPE_BASE ** (-np.arange(half, dtype=np.float64) / half)).astype(np.float32)
    ang = (np.arange(s, dtype=np.float32)[:, None] * inv_freq[None, :]).astype(np.float64)
    cos, sin = np.cos(ang).astype(np.float32), np.sin(ang).astype(np.float32)
    return jnp.asarray(np.concatenate([cos, cos], axis=1)), jnp.asarray(np.concatenate([-sin, sin], axis=1))


def _ret_chunk(qs, ks, vs, rs, cos, sin, dintra, kw, qw, g_chunk):
    outs, rn = [], []
    for h in range(NH):
        q = qs[h] * cos + _swap_halves(qs[h]) * sin
        k = (ks[h] * cos + _swap_halves(ks[h]) * sin) * (HD ** -0.5)
        sc = _bdot_nt(q, k) * dintra[h]
        outs.append(_bdot(sc, vs[h]) + _bdot(q * qw[h], rs[h]))
        rn.append(g_chunk[h] * rs[h] + _bdot_tn(k * kw[h], vs[h]))
    return outs, rn


def _heads(x):
    return [x[:, h * HD:(h + 1) * HD] for h in range(NH)]


def retention_fwd(zz, cosf, sinf, *, name):
    s = zz.shape[0]
    n = s // CH
    dintra, kw, qw, g_chunk = _ret_consts()

    def body(q_ref, k_ref, v_ref, c_ref, s_ref, di_ref, kw_ref, qw_ref, o_ref, rp_ref, r_scr):
        @pl.when(pl.program_id(0) == 0)
        def _():
            r_scr[...] = jnp.zeros_like(r_scr)

        rprev = r_scr[...]
        rp_ref[0] = rprev
        outs, rn = _ret_chunk(_heads(q_ref[...]), _heads(k_ref[...]), _heads(v_ref[...]),
                              [rprev[h * HD:(h + 1) * HD] for h in range(NH)], c_ref[...], s_ref[...],
                              [di_ref[h] for h in range(NH)], [kw_ref[h] for h in range(NH)],
                              [qw_ref[h] for h in range(NH)], g_chunk)
        o_ref[...] = jnp.concatenate(outs, axis=1)
        r_scr[...] = jnp.concatenate(rn, axis=0)

    col = lambda c: pl.BlockSpec((CH, BW), lambda i: (i, c // BW))
    tab = pl.BlockSpec((CH, HD), lambda i: (i, 0))
    cst = lambda shp: pl.BlockSpec(shp, lambda i: (0,) * len(shp))
    return pl.pallas_call(
        body, grid=(n,),
        in_specs=[col(RQ), col(RK), col(RV), tab, tab, cst((NH, CH, CH)), cst((NH, CH, HD)), cst((NH, CH, HD))],
        out_specs=[pl.BlockSpec((CH, BW), lambda i: (i, 0)), pl.BlockSpec((1, BW, HD), lambda i: (i, 0, 0))],
        out_shape=[_sds((s, BW), F32), _sds((n, BW, HD), F32)],
        scratch_shapes=[pltpu.VMEM((BW, HD), F32)],
        compiler_params=_cparams(("arbitrary",)), name=name)(zz, zz, zz, cosf, sinf, dintra, kw, qw)


def retention_bwd(zz, cosf, sinf, rprev, do, dzz, *, name):
    s = zz.shape[0]
    n = s // CH
    dintra, kw, qw, g_chunk = _ret_consts()

    def body(q_ref, k_ref, v_ref, c_ref, s_ref, di_ref, kw_ref, qw_ref, rp_ref, do_ref, dzz_ref, dz_ref, dr_scr):
        @pl.when(pl.program_id(0) == 0)
        def _():
            dr_scr[...] = jnp.zeros_like(dr_scr)

        rprev_v = rp_ref[0]
        f = functools.partial(_ret_chunk, cos=c_ref[...], sin=s_ref[...],
                              dintra=[di_ref[h] for h in range(NH)], kw=[kw_ref[h] for h in range(NH)],
                              qw=[qw_ref[h] for h in range(NH)], g_chunk=g_chunk)
        _, vjp = jax.vjp(f, _heads(q_ref[...]), _heads(k_ref[...]), _heads(v_ref[...]),
                         [rprev_v[h * HD:(h + 1) * HD] for h in range(NH)])
        dr = dr_scr[...]
        dq, dk, dv, drp = vjp((_heads(do_ref[...]), [dr[h * HD:(h + 1) * HD] for h in range(NH)]))
        dz_ref[...] = jnp.concatenate(dq + dk + dv, axis=1).astype(dz_ref.dtype)
        dr_scr[...] = jnp.concatenate(drp, axis=0)

    col = lambda c: pl.BlockSpec((CH, BW), lambda i: (n - 1 - i, c // BW))
    tab = pl.BlockSpec((CH, HD), lambda i: (n - 1 - i, 0))
    cst = lambda shp: pl.BlockSpec(shp, lambda i: (0,) * len(shp))
    return pl.pallas_call(
        body, grid=(n,),
        in_specs=[col(RQ), col(RK), col(RV), tab, tab, cst((NH, CH, CH)), cst((NH, CH, HD)), cst((NH, CH, HD)),
                  pl.BlockSpec((1, BW, HD), lambda i: (n - 1 - i, 0, 0)), pl.BlockSpec((CH, BW), lambda i: (n - 1 - i, 0)),
                  pl.BlockSpec(memory_space=pl.ANY)],
        out_specs=pl.BlockSpec((CH, 3 * BW), lambda i: (n - 1 - i, RQ // (3 * BW))),
        out_shape=_sds(dzz.shape, dzz.dtype), input_output_aliases={10: 0},
        scratch_shapes=[pltpu.VMEM((BW, HD), F32)],
        compiler_params=_cparams(("arbitrary",)), name=name)(zz, zz, zz, cosf, sinf, dintra, kw, qw, rprev, do, dzz)


GKW = NH * GDK


def _gla_consts():
    tri = np.tril(np.ones((CH, CH), np.float32))
    mask_t = np.zeros((BW, GKW), np.float32)
    for h in range(NH):
        mask_t[h * HD:(h + 1) * HD, h * GDK:(h + 1) * GDK] = 1.0
    return jnp.asarray(tri), jnp.asarray(mask_t)


def _gla_chunk(q, k, v, glr, w_a2, b_a, st, tri, mask_t):
    la = _log_sigmoid(_bdot(glr, w_a2) + b_a) * (1.0 / 16.0)
    bc = _dot(tri, la, HI)
    be = jnp.sum(la, axis=0, keepdims=True)
    kv_t = _bdot_tn(v, k * jnp.exp(be - bc)) * mask_t
    sn = jnp.exp(be) * st + kv_t
    return _bdot_nt(q * (GDK ** -0.5), sn), sn


def gla_fwd(zz, w_a2p, b_a, *, name):
    s = zz.shape[0]
    n = s // CH
    tri, mask_t = _gla_consts()

    def body(q_ref, k_ref, v_ref, lr_ref, w_ref, b_ref, tri_ref, m_ref, o_ref, sp_ref, st_scr):
        @pl.when(pl.program_id(0) == 0)
        def _():
            st_scr[...] = jnp.zeros_like(st_scr)

        sp = st_scr[...]
        sp_ref[0] = sp
        o, sn = _gla_chunk(q_ref[...], k_ref[...], v_ref[...], lr_ref[...], w_ref[...], b_ref[...], sp,
                           tri_ref[...], m_ref[...])
        o_ref[...] = o
        st_scr[...] = sn

    cst = lambda shp: pl.BlockSpec(shp, lambda i: (0,) * len(shp))
    return pl.pallas_call(
        body, grid=(n,),
        in_specs=[pl.BlockSpec((CH, GKW), lambda i: (i, GQ // GKW)), pl.BlockSpec((CH, GKW), lambda i: (i, GK // GKW)),
                  pl.BlockSpec((CH, BW), lambda i: (i, GV // BW)), pl.BlockSpec((CH, HD), lambda i: (i, LR // HD)),
                  cst((HD, GKW)), cst((1, GKW)), cst((CH, CH)), cst((BW, GKW))],
        out_specs=[pl.BlockSpec((CH, BW), lambda i: (i, 0)), pl.BlockSpec((1, BW, GKW), lambda i: (i, 0, 0))],
        out_shape=[_sds((s, BW), F32), _sds((n, BW, GKW), F32)],
        scratch_shapes=[pltpu.VMEM((BW, GKW), F32)],
        compiler_params=_cparams(("arbitrary",)), name=name)(zz, zz, zz, zz, w_a2p, b_a, tri, mask_t)


def gla_bwd(zz, w_a2p, b_a, sprev, do, dzz, *, name):
    s = zz.shape[0]
    n = s // CH
    tri, mask_t = _gla_consts()

    def body(q_ref, k_ref, v_ref, lr_ref, w_ref, b_ref, tri_ref, m_ref, sp_ref, do_ref, dzz_ref,
             dz_ref, dlr_ref, dw_ref, db_ref, ds_scr):
        @pl.when(pl.program_id(0) == 0)
        def _():
            ds_scr[...] = jnp.zeros_like(ds_scr)
            dw_ref[...] = jnp.zeros_like(dw_ref)
            db_ref[...] = jnp.zeros_like(db_ref)

        f = functools.partial(_gla_chunk, tri=tri_ref[...], mask_t=m_ref[...])
        _, vjp = jax.vjp(f, q_ref[...], k_ref[...], v_ref[...], lr_ref[...], w_ref[...], b_ref[...], sp_ref[0])
        dq, dk, dv, dlr, dw, db, dsp = vjp((do_ref[...], ds_scr[...]))
        dz_ref[...] = jnp.concatenate([dq, dk, dv], axis=1).astype(dz_ref.dtype)
        dlr_ref[...] = dlr.astype(dlr_ref.dtype)
        dw_ref[...] += dw
        db_ref[...] += db
        ds_scr[...] = dsp

    cst = lambda shp: pl.BlockSpec(shp, lambda i: (0,) * len(shp))
    r = lambda i: n - 1 - i
    return pl.pallas_call(
        body, grid=(n,),
        in_specs=[pl.BlockSpec((CH, GKW), lambda i: (r(i), GQ // GKW)), pl.BlockSpec((CH, GKW), lambda i: (r(i), GK // GKW)),
                  pl.BlockSpec((CH, BW), lambda i: (r(i), GV // BW)), pl.BlockSpec((CH, HD), lambda i: (r(i), LR // HD)),
                  cst((HD, GKW)), cst((1, GKW)), cst((CH, CH)), cst((BW, GKW)),
                  pl.BlockSpec((1, BW, GKW), lambda i: (r(i), 0, 0)), pl.BlockSpec((CH, BW), lambda i: (r(i), 0)),
                  pl.BlockSpec(memory_space=pl.ANY)],
        out_specs=[pl.BlockSpec((CH, 2 * GKW + BW), lambda i: (r(i), GQ // (2 * GKW + BW))),
                   pl.BlockSpec((CH, HD), lambda i: (r(i), 0)), cst((HD, GKW)), cst((1, GKW))],
        out_shape=[_sds(dzz.shape, dzz.dtype), _sds((s, HD), MMT), _sds((HD, GKW), F32), _sds((1, GKW), F32)],
        input_output_aliases={10: 0}, scratch_shapes=[pltpu.VMEM((BW, GKW), F32)],
        compiler_params=_cparams(("arbitrary",)), name=name)(zz, zz, zz, zz, w_a2p, b_a, tri, mask_t, sprev, do, dzz)


def _fox_pre_f(fqs, fks, ff, gq, gk, bf):
    def rms(x, g):
        return x * lax.rsqrt(jnp.mean(x * x, axis=-1, keepdims=True) + EPS) * g

    qn = [rms(x, gq) * (HD ** -0.5) for x in fqs]
    kn = [rms(x, gk) for x in fks]
    return qn, kn, _log_sigmoid(ff + bf)


def fox_pre(zz, gq, gk, bf, *, name):
    s = zz.shape[0]
    t = _row_tile(s)
    tri = jnp.asarray(np.tril(np.ones((t, t), np.float32)))

    def body(q_ref, k_ref, f_ref, gq_ref, gk_ref, b_ref, tri_ref, qn_ref, kn_ref, cum_ref, carry):
        @pl.when(pl.program_id(0) == 0)
        def _():
            carry[...] = jnp.zeros_like(carry)

        qn, kn, lf = _fox_pre_f(_heads(q_ref[...]), _heads(k_ref[...]), f_ref[...], gq_ref[...], gk_ref[...], b_ref[...])
        qn_ref[...] = jnp.concatenate(qn, axis=1).astype(qn_ref.dtype)
        kn_ref[...] = jnp.concatenate(kn, axis=1).astype(kn_ref.dtype)
        cum_ref[...] = _dot(tri_ref[...], lf, HI) + carry[...]
        carry[...] += jnp.sum(lf, axis=0, keepdims=True)

    vec = pl.BlockSpec((1, HD), lambda i: (0, 0))
    return pl.pallas_call(
        body, grid=(s // t,),
        in_specs=[pl.BlockSpec((t, BW), lambda i: (i, FQ // BW)), pl.BlockSpec((t, BW), lambda i: (i, FK // BW)),
                  pl.BlockSpec((t, HD), lambda i: (i, FF // HD)), vec, vec, vec, pl.BlockSpec((t, t), lambda i: (0, 0))],
        out_specs=[pl.BlockSpec((t, BW), lambda i: (i, 0)), pl.BlockSpec((t, BW), lambda i: (i, 0)),
                   pl.BlockSpec((t, HD), lambda i: (i, 0))],
        out_shape=[_sds((s, BW), MMT), _sds((s, BW), MMT), _sds((s, HD), F32)],
        scratch_shapes=[pltpu.VMEM((1, HD), F32)],
        compiler_params=_cparams(("arbitrary",)), name=name)(zz, zz, zz, gq, gk, bf, tri)


def fox_pre_bwd(zz, gq, gk, bf, dqn, dkn, dcum, dzz, *, name):
    s = zz.shape[0]
    t = _row_tile(s)
    nt = s // t
    triu = jnp.asarray(np.triu(np.ones((t, t), np.float32)))

    def body(q_ref, k_ref, f_ref, gq_ref, gk_ref, b_ref, tri_ref, dqn_ref, dkn_ref, dcum_ref, dzz_ref,
             dz_ref, dff_ref, dgq_ref, dgk_ref, db_ref, carry):
        @pl.when(pl.program_id(0) == 0)
        def _():
            carry[...] = jnp.zeros_like(carry)
            dgq_ref[...] = jnp.zeros_like(dgq_ref)
            dgk_ref[...] = jnp.zeros_like(dgk_ref)
            db_ref[...] = jnp.zeros_like(db_ref)

        dcum_v = dcum_ref[...]
        dlf = _dot(tri_ref[...], dcum_v, HI) + carry[...]
        carry[...] += jnp.sum(dcum_v, axis=0, keepdims=True)
        _, vjp = jax.vjp(_fox_pre_f, _heads(q_ref[...]), _heads(k_ref[...]), f_ref[...], gq_ref[...], gk_ref[...], b_ref[...])
        dq, dk, dff, dgq, dgk, db = vjp((_heads(dqn_ref[...]), _heads(dkn_ref[...]), dlf))
        dz_ref[...] = jnp.concatenate(dq + dk, axis=1).astype(dz_ref.dtype)
        dff_ref[...] = dff.astype(dff_ref.dtype)
        dgq_ref[...] += dgq
        dgk_ref[...] += dgk
        db_ref[...] += db

    r = lambda i: nt - 1 - i
    vec = pl.BlockSpec((1, HD), lambda i: (0, 0))
    return pl.pallas_call(
        body, grid=(nt,),
        in_specs=[pl.BlockSpec((t, BW), lambda i: (r(i), FQ // BW)), pl.BlockSpec((t, BW), lambda i: (r(i), FK // BW)),
                  pl.BlockSpec((t, HD), lambda i: (r(i), FF // HD)), vec, vec, vec, pl.BlockSpec((t, t), lambda i: (0, 0)),
                  pl.BlockSpec((t, BW), lambda i: (r(i), 0)), pl.BlockSpec((t, BW), lambda i: (r(i), 0)),
                  pl.BlockSpec((t, HD), lambda i: (r(i), 0)), pl.BlockSpec(memory_space=pl.ANY)],
        out_specs=[pl.BlockSpec((t, 2 * BW), lambda i: (r(i), FQ // (2 * BW))), pl.BlockSpec((t, HD), lambda i: (r(i), 0)),
                   vec, vec, vec],
        out_shape=[_sds(dzz.shape, dzz.dtype), _sds((s, HD), MMT), _sds((1, HD), F32), _sds((1, HD), F32), _sds((1, HD), F32)],
        input_output_aliases={10: 0}, scratch_shapes=[pltpu.VMEM((1, HD), F32)],
        compiler_params=_cparams(("arbitrary",)), name=name)(zz, zz, zz, gq, gk, bf, triu, dqn, dkn, dcum, dzz)


def _fox_blocks(s):
    return min(256, s), min(512, s)


NEG = -1e30


def fox_attn_fwd(qn, kn, zz, cum_col, cum_row, *, name):
    s = qn.shape[0]
    bq, bk = _fox_blocks(s)

    def body(q_ref, k_ref, v_ref, cc_ref, cr_ref, o_ref, lse_ref):
        qi = pl.program_id(1)
        q = q_ref[...]
        cq = cc_ref[...]
        rows = qi * bq + lax.broadcasted_iota(jnp.int32, (bq, bk), 0)
        cols0 = lax.broadcasted_iota(jnp.int32, (bq, bk), 1)

        def step(j, carry):
            m, l, acc = carry
            off = pl.multiple_of(j * bk, bk)
            k = k_ref[pl.ds(off, bk), :]
            v = v_ref[pl.ds(off, bk), :].astype(MMT)
            sc = _dot_nt(q, k) + cq - cr_ref[pl.ds(j, 1), :]
            sc = jnp.where(rows >= cols0 + j * bk, sc, NEG)
            m_new = jnp.maximum(m, jnp.max(sc, axis=1, keepdims=True))
            alpha = jnp.exp(m - m_new)
            p = jnp.exp(sc - m_new)
            return m_new, alpha * l + jnp.sum(p, axis=1, keepdims=True), alpha * acc + _dot(p.astype(MMT), v)

        nk = ((qi + 1) * bq + bk - 1) // bk
        m, l, acc = lax.fori_loop(0, nk, step, (jnp.full((bq, 1), NEG, F32), jnp.zeros((bq, 1), F32),
                                                jnp.zeros((bq, HD), F32)))
        o_ref[...] = acc / l
        lse_ref[...] = m + jnp.log(l)

    return pl.pallas_call(
        body, grid=(NH, s // bq),
        in_specs=[pl.BlockSpec((bq, HD), lambda h, i: (i, h)), pl.BlockSpec((s, HD), lambda h, i: (0, h)),
                  pl.BlockSpec((s, HD), lambda h, i: (0, FV // HD + h)),
                  pl.BlockSpec((None, bq, 1), lambda h, i: (h, i, 0)), pl.BlockSpec((None, s // bk, bk), lambda h, i: (h, 0, 0))],
        out_specs=[pl.BlockSpec((bq, HD), lambda h, i: (i, h)), pl.BlockSpec((None, bq, 1), lambda h, i: (h, i, 0))],
        out_shape=[_sds((s, BW), F32), _sds((NH, s, 1), F32)],
        compiler_params=_cparams(("parallel", "parallel")), name=name)(qn, kn, zz, cum_col, cum_row)


def fox_attn_bwd(qn, kn, zz, cum_col, cum_row, lse, do, dzz, *, name):
    s = qn.shape[0]
    bq, bk = _fox_blocks(s)
    nkc = s // bk

    def body(q_ref, k_ref, v_ref, cc_ref, cr_ref, lse_ref, do_ref, dzz_ref, dq_ref, dk_ref, dv_ref, dc_ref,
             p_scr, dp_scr, dv_scr):
        qi = pl.program_id(1)

        @pl.when(qi == 0)
        def _():
            dk_ref[...] = jnp.zeros_like(dk_ref)
            dv_scr[...] = jnp.zeros_like(dv_scr)
            dc_ref[...] = jnp.zeros_like(dc_ref)

        q = q_ref[...]
        dob = do_ref[...].astype(MMT)
        cq = cc_ref[...]
        lse_v = lse_ref[...]
        rows = qi * bq + lax.broadcasted_iota(jnp.int32, (bq, bk), 0)
        cols0 = lax.broadcasted_iota(jnp.int32, (bq, bk), 1)
        nk = ((qi + 1) * bq + bk - 1) // bk

        def probs(j, delta):
            off = pl.multiple_of(j * bk, bk)
            sc = _dot_nt(q, k_ref[pl.ds(off, bk), :]) + cq - cr_ref[pl.ds(j, 1), :]
            p = jnp.where(rows >= cols0 + j * bk, jnp.exp(sc - lse_v), 0.0)
            dp = _dot_nt(dob, v_ref[pl.ds(off, bk), :].astype(MMT))
            p_scr[j] = p
            dp_scr[j] = dp
            return delta + jnp.sum(p * dp, axis=1, keepdims=True)

        delta = lax.fori_loop(0, nk, probs, jnp.zeros((bq, 1), F32))

        def grads(j, dq):
            off = pl.multiple_of(j * bk, bk)
            p = p_scr[j]
            ds = p * (dp_scr[j] - delta)
            dsm = ds.astype(MMT)
            dv_scr[pl.ds(off, bk), :] += _dot_tn(p.astype(MMT), dob)
            dk_ref[pl.ds(off, bk), :] += _dot_tn(dsm, q)
            dc_ref[pl.ds(j, 1), :] -= jnp.sum(ds, axis=0, keepdims=True)
            return dq + _dot(dsm, k_ref[pl.ds(off, bk), :])

        dq_ref[...] = lax.fori_loop(0, nk, grads, jnp.zeros((bq, HD), F32))

        @pl.when(qi == pl.num_programs(1) - 1)
        def _():
            dv_ref[...] = dv_scr[...].astype(dv_ref.dtype)

    full = lambda c0=0: pl.BlockSpec((s, HD), lambda h, i: (0, c0 + h))
    blk = lambda: pl.BlockSpec((bq, HD), lambda h, i: (i, h))
    colv = lambda: pl.BlockSpec((None, bq, 1), lambda h, i: (h, i, 0))
    rowv = lambda: pl.BlockSpec((None, nkc, bk), lambda h, i: (h, 0, 0))
    return pl.pallas_call(
        body, grid=(NH, s // bq),
        in_specs=[blk(), full(), full(FV // HD), colv(), rowv(), colv(), blk(), pl.BlockSpec(memory_space=pl.ANY)],
        out_specs=[blk(), full(), full(FV // HD), rowv()],
        out_shape=[_sds((s, BW), F32), _sds((s, BW), F32), _sds(dzz.shape, dzz.dtype), _sds((NH, nkc, bk), F32)],
        input_output_aliases={7: 2},
        scratch_shapes=[pltpu.VMEM((nkc, bq, bk), F32), pltpu.VMEM((nkc, bq, bk), F32), pltpu.VMEM((s, HD), F32)],
        compiler_params=_cparams(("parallel", "arbitrary")), name=name)(qn, kn, zz, cum_col, cum_row, lse, do, dzz)


def _branch_f(rets, rgs, glas, ggs, ret_g, gla_g):
    out_r, out_g = [], []
    for h in range(NH):
        xc = rets[h] - jnp.mean(rets[h], axis=-1, keepdims=True)
        y = xc * lax.rsqrt(jnp.mean(xc * xc, axis=-1, keepdims=True) + EPS) * ret_g[h]
        out_r.append(_silu(rgs[h]) * y)
        x = glas[h]
        y = x * lax.rsqrt(jnp.mean(x * x, axis=-1, keepdims=True) + EPS) * gla_g
        out_g.append(_silu(ggs[h]) * y)
    return out_r, out_g


def _w_br_spec(layer):
    return pl.BlockSpec((None, 3, BW, D), lambda i: (layer, 0, 0, 0))


def mix_fwd(ret_raw, gla_raw, fox_o, zz, ret_g, gla_g, b_mg, w_br, *, name, layer):
    s = zz.shape[0]
    t = _row_tile(s)

    def body(r_ref, g_ref, f_ref, rg_ref, gg_ref, gp_ref, rgn_ref, ggn_ref, bmg_ref, w_ref, o_ref):
        rgn = rgn_ref[...]
        br_r, br_g = _branch_f(_heads(r_ref[...]), _heads(rg_ref[...]), _heads(g_ref[...]), _heads(gg_ref[...]),
                               _heads(rgn), ggn_ref[...])
        brs = [jnp.concatenate(br_r, axis=1), jnp.concatenate(br_g, axis=1), f_ref[...]]
        acc = jnp.zeros((t, D), F32)
        for b in range(3):
            gate = jax.nn.sigmoid(gp_ref[:, b * D:(b + 1) * D] + bmg_ref[:, b * D:(b + 1) * D])
            acc = acc + gate * _dot(brs[b].astype(MMT), w_ref[b])
        o_ref[...] = acc.astype(o_ref.dtype)

    row = lambda w, c=0: pl.BlockSpec((t, w), lambda i: (i, c // w))
    cst = lambda shp: pl.BlockSpec(shp, lambda i: (0,) * len(shp))
    return pl.pallas_call(
        body, grid=(s // t,),
        in_specs=[row(BW), row(BW), row(BW), row(BW, RG), row(BW, GG), row(3 * D, GP), cst((1, BW)), cst((1, HD)),
                  cst((1, 3 * D)), _w_br_spec(layer)],
        out_specs=row(D), out_shape=_sds((s, D), MMT),
        compiler_params=_cparams(("parallel",)), name=name)(ret_raw, gla_raw, fox_o, zz, zz, zz, ret_g, gla_g, b_mg, w_br)


def mix_bwd(ret_raw, gla_raw, fox_o, zz, ret_g, gla_g, b_mg, w_br, dmi, *, name, layer):
    s = zz.shape[0]
    t = _row_tile(s)

    def body(r_ref, g_ref, f_ref, rg_ref, gg_ref, gp_ref, rgn_ref, ggn_ref, bmg_ref, w_ref, dmi_ref,
             dr_ref, dg_ref, df_ref, dgp_ref, dw_ref, drgn_ref, dggn_ref, dbmg_ref):
        @pl.when(pl.program_id(0) == 0)
        def _():
            dw_ref[...] = jnp.zeros_like(dw_ref)
            drgn_ref[...] = jnp.zeros_like(drgn_ref)
            dggn_ref[...] = jnp.zeros_like(dggn_ref)
            dbmg_ref[...] = jnp.zeros_like(dbmg_ref)

        (br_r, br_g), vjp = jax.vjp(_branch_f, _heads(r_ref[...]), _heads(rg_ref[...]), _heads(g_ref[...]),
                                    _heads(gg_ref[...]), _heads(rgn_ref[...]), ggn_ref[...])
        brs = [jnp.concatenate(br_r, axis=1).astype(MMT), jnp.concatenate(br_g, axis=1).astype(MMT),
               f_ref[...].astype(MMT)]
        dmi_v = dmi_ref[...].astype(F32)
        dbr = []
        for b in range(3):
            w = w_ref[b]
            ybr = _dot(brs[b], w)
            gate = jax.nn.sigmoid(gp_ref[:, b * D:(b + 1) * D] + bmg_ref[:, b * D:(b + 1) * D])
            dgp = dmi_v * ybr * gate * (1.0 - gate)
            dgp_ref[:, b * D:(b + 1) * D] = dgp.astype(dgp_ref.dtype)
            dbmg_ref[:, b * D:(b + 1) * D] += jnp.sum(dgp, axis=0, keepdims=True)
            dy = (dmi_v * gate).astype(MMT)
            dw_ref[b] += _dot_tn(brs[b], dy)
            dbr.append(_dot_nt(dy, w))
        dr, drg, dg, dgg, drgn, dggn = vjp((_heads(dbr[0]), _heads(dbr[1])))
        dr_ref[...] = jnp.concatenate(dr, axis=1)
        dg_ref[...] = jnp.concatenate(dg, axis=1)
        df_ref[...] = dbr[2]
        dgp_ref[:, RG:RG + BW] = jnp.concatenate(drg, axis=1).astype(dgp_ref.dtype)
        dgp_ref[:, GG:GG + BW] = jnp.concatenate(dgg, axis=1).astype(dgp_ref.dtype)
        drgn_ref[...] += jnp.concatenate(drgn, axis=1)
        dggn_ref[...] += dggn

    row = lambda w, c=0: pl.BlockSpec((t, w), lambda i: (i, c // w))
    cst = lambda shp: pl.BlockSpec(shp, lambda i: (0,) * len(shp))
    return pl.pallas_call(
        body, grid=(s // t,),
        in_specs=[row(BW), row(BW), row(BW), row(BW, RG), row(BW, GG), row(3 * D, GP), cst((1, BW)), cst((1, HD)),
                  cst((1, 3 * D)), _w_br_spec(layer), row(D)],
        out_specs=[row(BW), row(BW), row(BW), row(FV), cst((3, BW, D)), cst((1, BW)), cst((1, HD)), cst((1, 3 * D))],
        out_shape=[_sds((s, BW), F32)] * 3 + [_sds((s, NZZ), MMT), _sds((3, BW, D), F32), _sds((1, BW), F32),
                                              _sds((1, HD), F32), _sds((1, 3 * D), F32)],
        compiler_params=_cparams(("arbitrary",)), name=name)(ret_raw, gla_raw, fox_o, zz, zz, zz, ret_g, gla_g, b_mg, w_br, dmi)


CT = 256


def _shift_down(x, k, rows):
    return jnp.where(rows >= k, pltpu.roll(x, k, 0), 0.0)


def _shift_up(x, k, rows, s):
    return jnp.where(rows < s - k, pltpu.roll(x, s - k, 0), 0.0)


def conv_fwd(ug, w_conv, b_conv, *, name):
    s = ug.shape[0]
    nt = DFF // CT

    def body(u_ref, g_ref, w_ref, b_ref, a_ref):
        u = u_ref[...]
        rows = lax.broadcasted_iota(jnp.int32, u.shape, 0)
        uc = b_ref[...] + w_ref[0:1, :] * _shift_down(u, 2, rows) + w_ref[1:2, :] * _shift_down(u, 1, rows) + w_ref[2:3, :] * u
        a_ref[...] = (_silu(uc) * g_ref[...]).astype(a_ref.dtype)

    return pl.pallas_call(
        body, grid=(nt,),
        in_specs=[pl.BlockSpec((s, CT), lambda j: (0, j)), pl.BlockSpec((s, CT), lambda j: (0, nt + j)),
                  pl.BlockSpec((3, CT), lambda j: (0, j)), pl.BlockSpec((1, CT), lambda j: (0, j))],
        out_specs=pl.BlockSpec((s, CT), lambda j: (0, j)), out_shape=_sds((s, DFF), MMT),
        compiler_params=_cparams(("parallel",)), name=name)(ug, ug, w_conv, b_conv)


def conv_bwd(ug, w_conv, b_conv, da, *, name):
    s = ug.shape[0]
    nt = DFF // CT

    def body(u_ref, g_ref, w_ref, b_ref, da_ref, du_ref, dg_ref, dw_ref, db_ref):
        u = u_ref[...]
        rows = lax.broadcasted_iota(jnp.int32, u.shape, 0)
        u2, u1 = _shift_down(u, 2, rows), _shift_down(u, 1, rows)
        uc = b_ref[...] + w_ref[0:1, :] * u2 + w_ref[1:2, :] * u1 + w_ref[2:3, :] * u
        sg = jax.nn.sigmoid(uc)
        da_v = da_ref[...]
        dg_ref[...] = (da_v * uc * sg).astype(dg_ref.dtype)
        duc = da_v * g_ref[...] * sg * (1.0 + uc * (1.0 - sg))
        du = w_ref[2:3, :] * duc + w_ref[1:2, :] * _shift_up(duc, 1, rows, s) + w_ref[0:1, :] * _shift_up(duc, 2, rows, s)
        du_ref[...] = du.astype(du_ref.dtype)
        dw_ref[0:1, :] = jnp.sum(duc * u2, axis=0, keepdims=True)
        dw_ref[1:2, :] = jnp.sum(duc * u1, axis=0, keepdims=True)
        dw_ref[2:3, :] = jnp.sum(duc * u, axis=0, keepdims=True)
        db_ref[...] = jnp.sum(duc, axis=0, keepdims=True)

    col = lambda: pl.BlockSpec((s, CT), lambda j: (0, j))
    return pl.pallas_call(
        body, grid=(nt,),
        in_specs=[col(), pl.BlockSpec((s, CT), lambda j: (0, nt + j)), pl.BlockSpec((3, CT), lambda j: (0, j)),
                  pl.BlockSpec((1, CT), lambda j: (0, j)), col()],
        out_specs=[col(), col(), pl.BlockSpec((3, CT), lambda j: (0, j)), pl.BlockSpec((1, CT), lambda j: (0, j))],
        out_shape=[_sds((s, DFF), MMT), _sds((s, DFF), MMT), _sds((3, DFF), F32), _sds((1, DFF), F32)],
        compiler_params=_cparams(("parallel",)), name=name)(ug, ug, w_conv, b_conv, da)


def place_tail(dzz, dlr, dff, *, name):
    s = dzz.shape[0]
    t = _row_tile(s)

    def body(a_ref, b_ref, z_ref, o_ref):
        o_ref[...] = jnp.concatenate([a_ref[...], b_ref[...]], axis=1)

    spec = pl.BlockSpec((t, HD), lambda i: (i, 0))
    return pl.pallas_call(
        body, grid=(s // t,), in_specs=[spec, spec, pl.BlockSpec(memory_space=pl.ANY)],
        out_specs=pl.BlockSpec((t, 2 * HD), lambda i: (i, LR // (2 * HD))), out_shape=_sds(dzz.shape, dzz.dtype),
        input_output_aliases={2: 0}, compiler_params=_cparams(("parallel",)), name=name)(dlr, dff, dzz)


def _tiles(s):
    return min(1024, s)


def layer_fwd(x, mod, p, cosf, sinf):
    s = x.shape[0]
    tm = _tiles(s)
    l = p["l"]
    shift1, scale1, gate1, shift2, scale2, gate2 = mod
    h = norm_mod(x, p["norm1_g"], scale1, shift1, name="norm_mod")
    zz = mm_nn(h, p["w1"], tm=tm, tn=768, out_dtype=F32, name="mm_w1", layer=l)
    ret_raw, rprev = retention_fwd(zz, cosf, sinf, name="ret_fwd")
    gla_raw, sprev = gla_fwd(zz, p["w_a2p"], p["b_gla_a"], name="gla_fwd")
    qn, kn, cum = fox_pre(zz, p["q_norm_g"], p["k_norm_g"], p["b_foxp"], name="fox_pre")
    bq, bk = _fox_blocks(s)
    cum_t = cum[:, :NH].T
    cum_col, cum_row = cum_t[:, :, None], cum_t.reshape(NH, s // bk, bk)
    fox_o, lse = fox_attn_fwd(qn, kn, zz, cum_col, cum_row, name="fox_fwd")
    mi = mix_fwd(ret_raw, gla_raw, fox_o, zz, p["ret_norm_g"], p["gla_norm_g"], p["b_mg"], p["w_br"], name="mix_fwd",
                 layer=l)
    x1, mixed = mm_nn_residual(mi, p["w_o"], x, gate1, tm=tm, tn=512, name="mm_wo", layer=l)
    h2 = norm_mod(x1, p["norm2_g"], scale2, shift2, name="norm_mod")
    ug = mm_nn(h2, p["w_up"], tm=tm, tn=512, out_dtype=F32, name="mm_wup", layer=l)
    a = conv_fwd(ug, p["w_conv"], p["b_conv"], name="conv_fwd")
    x2, y = mm_nn_residual(a, p["w_down"], x1, gate2, tm=tm, tn=512, name="mm_wdown", layer=l)
    saved = dict(x=x, h=h, zz=zz, ret_raw=ret_raw, rprev=rprev, gla_raw=gla_raw, sprev=sprev, qn=qn, kn=kn,
                 cum_col=cum_col, cum_row=cum_row, fox_o=fox_o, lse=lse, mi=mi, mixed=mixed, x1=x1, h2=h2, ug=ug, a=a, y=y)
    return x2, saved


def layer_bwd(dx2, mod, p, sv, cosf, sinf, stacks, slot):
    s = dx2.shape[0]
    tm = _tiles(s)
    l = p["l"]
    shift1, scale1, gate1, shift2, scale2, gate2 = mod
    g, stacks = {}, dict(stacks)
    dy, dgate2 = gate_bwd(dx2, sv["y"], gate2, name="gate_bwd")
    stacks["w_down"] = mm_tn(sv["a"], dy, tm=min(1408, DFF), tn=512, out_dtype=MMT, name="mm_dwdown",
                             stack=stacks["w_down"], layer=slot)
    da = mm_nt(dy, p["w_down"], tm=tm, tn=1408, out_dtype=F32, name="mm_da", layer=l)
    du, dg, g["w_conv"], g["b_conv"] = conv_bwd(sv["ug"], p["w_conv"], p["b_conv"], da, name="conv_bwd")
    dug = jnp.concatenate([du, dg], axis=1)
    stacks["w_up"] = mm_tn(sv["h2"], dug, tm=512, tn=512, out_dtype=MMT, name="mm_dwup", stack=stacks["w_up"], layer=slot)
    dh2 = mm_nt(dug, p["w_up"], tm=min(512, s), tn=512, out_dtype=F32, name="mm_dh2", layer=l)
    dx1, g["norm2_g"], dscale2, dshift2 = norm_mod_bwd(sv["x1"], dh2, dx2, p["norm2_g"], scale2, shift2, name="norm_mod_bwd")
    dmixed, dgate1 = gate_bwd(dx1, sv["mixed"], gate1, name="gate_bwd")
    stacks["w_o"] = mm_tn(sv["mi"], dmixed, tm=512, tn=512, out_dtype=MMT, name="mm_dwo", stack=stacks["w_o"], layer=slot)
    dmi = mm_nt(dmixed, p["w_o"], tm=tm, tn=512, out_dtype=MMT, name="mm_dmi", layer=l)
    zz = sv["zz"]
    (dret, dgla, dfox, dzz, g["w_br"], g["ret_norm_g"], g["gla_norm_g"], g["b_mg"]) = mix_bwd(
        sv["ret_raw"], sv["gla_raw"], sv["fox_o"], zz, p["ret_norm_g"], p["gla_norm_g"], p["b_mg"], p["w_br"], dmi,
        name="mix_bwd", layer=l)
    dqn, dkn, dzz, dcum_row = fox_attn_bwd(sv["qn"], sv["kn"], zz, sv["cum_col"], sv["cum_row"], sv["lse"], dfox, dzz,
                                           name="fox_bwd")
    dcum = jnp.pad(dcum_row.reshape(NH, s).T, ((0, 0), (0, HD - NH)))
    dzz, dff, g["q_norm_g"], g["k_norm_g"], g["b_foxp"] = fox_pre_bwd(
        zz, p["q_norm_g"], p["k_norm_g"], p["b_foxp"], dqn, dkn, dcum, dzz, name="fox_pre_bwd")
    dzz, dlr, g["w_a2p"], g["b_gla_a"] = gla_bwd(zz, p["w_a2p"], p["b_gla_a"], sv["sprev"], dgla, dzz, name="gla_bwd")
    dzz = retention_bwd(zz, cosf, sinf, sv["rprev"], dret, dzz, name="ret_bwd")
    dzz = place_tail(dzz, dlr, dff, name="place_tail")
    stacks["w_mg"] = mm_tn(sv["h"], dzz, tm=512, tn=768, out_dtype=MMT, name="mm_dwmg", ncols=WZ0, stack=stacks["w_mg"],
                           layer=slot)
    dwz = mm_tn(sv["h"], dzz, tm=512, tn=768, out_dtype=MMT, name="mm_dwz", col0=WZ0)
    stacks["w_in"] = unalign_dw_in(dwz, stacks["w_in"], slot)
    dh = mm_nt(dzz, p["w1"], tm=min(512, s), tn=512, out_dtype=F32, name="mm_dh", layer=l)
    dx, g["norm1_g"], dscale1, dshift1 = norm_mod_bwd(sv["x"], dh, dx1, p["norm1_g"], scale1, shift1, name="norm_mod_bwd")
    dmod = jnp.concatenate([dshift1, dscale1, dgate1, dshift2, dscale2, dgate2], axis=1)
    return dx, g, dmod, stacks


def _align_cols(w_in, w_mg):
    z = lambda n: jnp.zeros((w_in.shape[0], n), w_in.dtype)
    seg = lambda name: w_in[:, W_IN_COLS[name][0]:W_IN_COLS[name][1]]
    return jnp.concatenate([w_mg, seg("rg"), seg("gg"), seg("fv"), seg("rqkv"), seg("gqkv"), seg("fqk"), seg("lr"),
                            z(HD - GLR), seg("ff"), z(HD - NH)], axis=1)


def _unalign_cols(dwz):
    seg = lambda c0, name: dwz[:, c0 - WZ0:c0 - WZ0 + W_IN_COLS[name][1] - W_IN_COLS[name][0]]
    return jnp.concatenate([seg(RQ, "rqkv"), seg(RG, "rg"), seg(GQ, "gqkv"), seg(LR, "lr"), seg(GG, "gg"), seg(FQ, "fqk"),
                            seg(FV, "fv"), seg(FF, "ff")], axis=1)


def build_w1(w_in_sh, w_mg):
    nl = w_mg.shape[0]
    t = _row_tile(D)

    def body(s_ref, g_ref, o_ref):
        o_ref[...] = _align_cols(jnp.concatenate([s_ref[k] for k in range(4)], axis=1), g_ref[...])

    return pl.pallas_call(
        body, grid=(nl, D // t),
        in_specs=[pl.BlockSpec((None, 4, t, IN_W // 4), lambda l, i: (l, 0, i, 0)), pl.BlockSpec((None, t, WZ0), lambda l, i: (l, i, 0))],
        out_specs=pl.BlockSpec((None, t, NZZ), lambda l, i: (l, i, 0)), out_shape=_sds((nl, D, NZZ), w_mg.dtype),
        compiler_params=_cparams(("parallel", "parallel")), name="build_w1")(w_in_sh, w_mg)


def unalign_dw_in(dwz, stack, layer):
    t = _row_tile(D)

    def body(z_ref, s_ref, o_ref):
        w = _unalign_cols(z_ref[...])
        for k in range(4):
            o_ref[k] = w[:, k * (IN_W // 4):(k + 1) * (IN_W // 4)]

    return pl.pallas_call(
        body, grid=(D // t,),
        in_specs=[pl.BlockSpec((t, NZZ - WZ0), lambda i: (i, 0)), pl.BlockSpec(memory_space=pl.ANY)],
        out_specs=pl.BlockSpec((None, 4, t, IN_W // 4), lambda i: (layer, 0, i, 0)), out_shape=_sds(stack.shape, stack.dtype),
        input_output_aliases={1: 0}, compiler_params=_cparams(("parallel",)), name="unalign_dw_in")(dwz, stack)


def layer_params(w, big, l):
    row = lambda v: v[l][None, :]
    return dict(
        l=0, norm1_g=row(w["norm1_g"]), norm2_g=row(w["norm2_g"]), w1=big["w1"],
        w_a2p=jnp.pad(w["w_gla_a2"][l], ((0, HD - GLR), (0, 0))), b_gla_a=row(w["b_gla_a"]),
        b_foxp=jnp.pad(row(w["b_fox_f"]), ((0, 0), (0, HD - NH))), ret_norm_g=row(w["ret_norm_g"]),
        gla_norm_g=row(w["gla_norm_g"]), q_norm_g=row(w["q_norm_g"]), k_norm_g=row(w["k_norm_g"]),
        w_br=big["w_br"], b_mg=row(w["b_mg"]), w_o=big["w_o"], w_up=big["w_up"], w_conv=w["w_conv"][l],
        b_conv=row(w["b_conv"]), w_down=big["w_down"])


def layer_grads(g):
    vec = lambda v: v[0]
    return dict(
        norm1_g=vec(g["norm1_g"]), norm2_g=vec(g["norm2_g"]), w_gla_a2=g["w_a2p"][:GLR], b_gla_a=vec(g["b_gla_a"]),
        b_fox_f=g["b_foxp"][0, :NH], ret_norm_g=vec(g["ret_norm_g"]), gla_norm_g=vec(g["gla_norm_g"]),
        q_norm_g=vec(g["q_norm_g"]), k_norm_g=vec(g["k_norm_g"]), w_br=g["w_br"], b_mg=vec(g["b_mg"]),
        w_conv=g["w_conv"], b_conv=vec(g["b_conv"]))


def ada_mod(c_all, w_ada, b_ada):
    nl, _, n = w_ada.shape

    def body(c_ref, w_ref, b_ref, o_ref):
        o_ref[...] = _dot(_silu(c_ref[...]), w_ref[...], HI) + b_ref[...]

    return pl.pallas_call(
        body, grid=(nl,),
        in_specs=[pl.BlockSpec((8, D), lambda l: (0, 0)), pl.BlockSpec((None, D, n), lambda l: (l, 0, 0)),
                  pl.BlockSpec((None, 1, n), lambda l: (l, 0, 0))],
        out_specs=pl.BlockSpec((None, 8, n), lambda l: (l, 0, 0)), out_shape=_sds((nl, 8, n), F32),
        compiler_params=_cparams(("parallel",)), name="ada_mod")(c_all, w_ada, b_ada)


def ada_dw(c_all, dmod):
    nl, _, n = dmod.shape

    def body(c_ref, d_ref, o_ref):
        o_ref[...] = _dot_tn(_silu(c_ref[...]), d_ref[...], HI)

    return pl.pallas_call(
        body, grid=(nl,),
        in_specs=[pl.BlockSpec((8, D), lambda l: (0, 0)), pl.BlockSpec((None, 8, n), lambda l: (l, 0, 0))],
        out_specs=pl.BlockSpec((None, D, n), lambda l: (l, 0, 0)), out_shape=_sds((nl, D, n), F32),
        compiler_params=_cparams(("parallel",)), name="ada_dw")(c_all, dmod)


def sum_devices(g):
    def body(g_ref, o_ref):
        acc = g_ref[0]
        for d in range(1, 8):
            acc = acc + g_ref[d]
        o_ref[...] = acc

    return pl.pallas_call(body, out_shape=_sds(g.shape[1:], F32), name="sum_devices")(g)


def adamw(w, g, m, v, *, block, name):
    nd = w.ndim
    grid = tuple(w.shape[i] // block[i] for i in range(nd))
    bc1 = 1.0 - ADAM_B1 ** ADAM_STEP
    bc2 = 1.0 - ADAM_B2 ** ADAM_STEP

    def body(w_ref, g_ref, m_ref, v_ref, d_ref, nm_ref, nv_ref):
        gv = g_ref[...]
        nm = ADAM_B1 * m_ref[...] + (1.0 - ADAM_B1) * gv
        nv = ADAM_B2 * v_ref[...] + (1.0 - ADAM_B2) * (gv * gv)
        nm_ref[...] = nm
        nv_ref[...] = nv
        d_ref[...] = -ADAM_LR * ((nm / bc1) / (jnp.sqrt(nv / bc2) + ADAM_EPS) + ADAM_WD * w_ref[...])

    spec = pl.BlockSpec(tuple(block), lambda *i: i)
    return pl.pallas_call(
        body, grid=grid, in_specs=[spec] * 4, out_specs=[spec] * 3, out_shape=[_sds(w.shape, F32)] * 3,
        compiler_params=_cparams(("parallel",) * nd), name=name)(w, g, m, v)


MESH = pl.DeviceIdType.MESH
ANY = pl.BlockSpec(memory_space=pl.ANY)
VM = pl.BlockSpec(memory_space=pltpu.VMEM)


def _place():
    x, y, c = lax.axis_index("x"), lax.axis_index("y"), lax.axis_index("c")
    return x, y, c, [(1 - x, y), (x, 1 - y), (1 - x, 1 - y)]


def small_allgather(v, *, name):
    m_per, n = v.shape

    def body(x_ref, out_ref, send_sems, recv_sems, local_sem):
        x, y, c, chips = _place()
        me, sibling = (x, y, c), (x, y, 1 - c)

        def rows(px, py, pc):
            return out_ref.at[pl.ds((4 * px + 2 * py + pc) * m_per, m_per), :]

        def copy(k, block, to, src=None):
            return pltpu.make_async_remote_copy(
                src_ref=rows(*block) if src is None else src, dst_ref=rows(*block),
                send_sem=send_sems.at[k], recv_sem=recv_sems.at[k], device_id=to, device_id_type=MESH)

        mine = pltpu.make_async_copy(x_ref, rows(*me), local_sem)
        mine.start()
        first = [copy(0, me, sibling, src=x_ref)]
        first += [copy(1 + j, me, (*chip, c), src=x_ref) for j, chip in enumerate(chips)]
        for cp in first:
            cp.start()
        passed = [copy(4 + j, (*chip, c), sibling) for j, chip in enumerate(chips)]
        for j, chip in enumerate(chips):
            copy(1 + j, (*chip, c), me).wait_recv()
            passed[j].start()
        copy(0, sibling, me).wait_recv()
        for j, chip in enumerate(chips):
            copy(4 + j, (*chip, 1 - c), me).wait_recv()
        for cp in first + passed:
            cp.wait_send()
        mine.wait()

    return pl.pallas_call(
        body, out_shape=_sds((8 * m_per, n), v.dtype), in_specs=[VM], out_specs=VM,
        scratch_shapes=[pltpu.SemaphoreType.DMA((7,)), pltpu.SemaphoreType.DMA((7,)), pltpu.SemaphoreType.DMA],
        name=name)(v)


TENSORS = {
    "w_in": ("lead", None, (4, D, 1285), (1, D, 1285)),
    "w_mg": ("col", 768, (D, 3072), (512, 3072)),
    "w_br": ("col", 256, (3, BW, D), (3, BW, D)),
    "w_o": ("row", 256, (D, D), (D, D)),
    "w_up": ("col", 1408, (D, 5632), (256, 5632)),
    "w_down": ("row", 704, (DFF, D), (704, D)),
}
BIG = tuple(TENSORS)


def _shard_shape(name):
    kind, width, full, _ = TENSORS[name]
    if kind == "lead":
        return full[1:]
    return full[:-1] + (width,) if kind == "col" else (width,) + full[1:]


def _shard_view(ref, layers, name, k):
    kind, width, full, _ = TENSORS[name]
    if kind == "lead":
        return ref.at[layers, k]
    if kind == "row":
        return ref.at[layers, pl.ds(k * width, width)]
    return ref.at[(layers,) + (slice(None),) * (len(full) - 1) + (pl.ds(k * width, width),)]


def _remote(send_sems, recv_sems, k, src, dst, to):
    return pltpu.make_async_remote_copy(src_ref=src, dst_ref=dst, send_sem=send_sems.at[k], recv_sem=recv_sems.at[k],
                                        device_id=to, device_id_type=MESH)


def _dma_sems(n):
    return [pltpu.SemaphoreType.DMA((n,)), pltpu.SemaphoreType.DMA((n,))]


RS_GROUP = 2
HBM = pl.BlockSpec(memory_space=pltpu.HBM)
SEM = pl.BlockSpec(memory_space=pltpu.SEMAPHORE)
SPLIT_CALL = dict(compiler_params=pltpu.CompilerParams(has_side_effects=pltpu.SideEffectType.DATAFLOW_SIDE_EFFECTING))
PULL_SET = (("w_up", "w_down", "w_o"), ("w_in", "w_mg", "w_br"))


def _in_hbm(a):
    return pltpu.with_memory_space_constraint(a, pltpu.HBM)


def _pull_sends(send_sems, recv_sems, p, o, layer, core, x, y, chips):
    return [_remote(send_sems, recv_sems, 3 * BIG.index(n) + j, p[n].at[layer], _shard_view(o[n], 0, n, 2 * x + y), (*chip, core))
            for n in PULL_SET[core] for j, chip in enumerate(chips)]


def _pull_arrivals(send_sems, recv_sems, o, core, x, y, chips, to):
    views = [(3 * BIG.index(n) + j, _shard_view(o[n], 0, n, 2 * chip[0] + chip[1]))
             for n in PULL_SET[core] for j, chip in enumerate(chips)]
    return [_remote(send_sems, recv_sems, k, v, v, to) for k, v in views]


def gather_start(shards, layer, *, name):
    nt = len(BIG)

    def body(*refs):
        p, o = dict(zip(BIG, refs[:nt])), dict(zip(BIG, refs[nt:2 * nt]))
        x, y, c, chips = _place()
        for core in (0, 1):
            @pl.when(c == core)
            def _():
                for cp in _pull_sends(refs[2 * nt], refs[2 * nt + 1], p, o, layer, core, x, y, chips):
                    cp.start()
        refs[-1][...] = jnp.zeros_like(refs[-1])

    lands = [_in_hbm(lax.empty((1,) + TENSORS[n][2], shards[n].dtype)) for n in BIG]
    outs = pl.pallas_call(
        body,
        out_shape=(pltpu.SemaphoreType.DMA((3 * nt,)), pltpu.SemaphoreType.DMA((3 * nt,)),
                   *[pltpu.HBM(a.shape, a.dtype) for a in lands], _sds((8, HD), F32)),
        in_specs=[HBM] * (2 * nt), out_specs=(SEM, SEM, *[HBM] * nt, VM),
        input_output_aliases={nt + t: 2 + t for t in range(nt)}, name=name, **SPLIT_CALL)(
            *[_in_hbm(shards[n]) for n in BIG], *lands)
    return outs[0], outs[1], outs[2:2 + nt], outs[-1]


def gather_wait(send_sems, recv_sems, shards, lands, after, layer, *, name):
    nt = len(BIG)

    def body(*refs):
        p, o = dict(zip(BIG, refs[:nt])), dict(zip(BIG, refs[nt:2 * nt]))
        ss, rs = refs[2 * nt], refs[2 * nt + 1]
        x, y, c, chips = _place()
        for core in (0, 1):
            @pl.when(c == core)
            def _():
                for cp in _pull_sends(ss, rs, p, o, layer, core, x, y, chips):
                    cp.wait_send()
                for cp in _pull_arrivals(ss, rs, o, core, x, y, chips, (x, y, core)):
                    cp.wait_recv()

    return pl.pallas_call(
        body, out_shape=tuple(pltpu.HBM(a.shape, a.dtype) for a in lands),
        in_specs=[HBM] * (2 * nt) + [SEM, SEM, ANY], out_specs=tuple([HBM] * nt),
        input_output_aliases={nt + t: t for t in range(nt)}, name=name, **SPLIT_CALL)(
            *[_in_hbm(shards[n]) for n in BIG], *lands, send_sems, recv_sems, after)


def gather_forward(shards, lands, layer, *, name):
    nt = len(BIG)

    def body(*refs):
        p, o = dict(zip(BIG, refs[:nt])), dict(zip(BIG, refs[2 * nt:3 * nt]))
        ss, rs = refs[3 * nt:]
        x, y, c, chips = _place()
        for core in (0, 1):
            @pl.when(c == core)
            def _():
                me, sibling = (x, y, core), (x, y, 1 - core)
                sends = _pull_arrivals(ss, rs, o, core, x, y, chips, sibling)
                sends += [_remote(ss, rs, 3 * nt + t, p[n].at[layer], _shard_view(o[n], 0, n, 2 * x + y), sibling)
                          for t, n in enumerate(BIG)]
                for cp in sends:
                    cp.start()
                for cp in sends:
                    cp.wait_send()
                for cp in _pull_arrivals(ss, rs, o, 1 - core, x, y, chips, me):
                    cp.wait_recv()
                for t, n in enumerate(BIG):
                    own = _shard_view(o[n], 0, n, 2 * x + y)
                    _remote(ss, rs, 3 * nt + t, own, own, me).wait_recv()

    outs = pl.pallas_call(
        body, out_shape=[_sds(a.shape, a.dtype) for a in lands], in_specs=[ANY] * (2 * nt), out_specs=[ANY] * nt,
        input_output_aliases={nt + t: t for t in range(nt)}, scratch_shapes=_dma_sems(4 * nt), name=name)(
            *[shards[n] for n in BIG], *lands)
    return dict(zip(BIG, outs))


def pair_exchange(g, *, name):
    hh = g[BIG[0]].shape[0] // 2
    nt = len(BIG)

    def body(*refs):
        send_sems, recv_sems = refs[2 * nt:]
        x, y, c, _ = _place()
        copies = [_remote(send_sems, recv_sems, t, refs[t].at[pl.ds(hh * (1 - c), hh)], refs[nt + t], (x, y, 1 - c))
                  for t in range(nt)]
        for cp in copies:
            cp.start()
        for cp in copies:
            cp.wait()

    outs = pl.pallas_call(
        body, out_shape=[_sds((hh,) + g[n].shape[1:], g[n].dtype) for n in BIG], in_specs=[ANY] * nt, out_specs=[ANY] * nt,
        scratch_shapes=_dma_sems(nt), name=name)(*[g[n] for n in BIG])
    return dict(zip(BIG, outs))


def _chip_copies(send_sems, recv_sems, s_refs, land_refs, c, chips):
    hl = s_refs[0].shape[0]
    return [_remote(send_sems, recv_sems, 3 * t + j, _shard_view(s_refs[t], pl.ds(0, hl), n, 2 * chip[0] + chip[1]),
                    land_refs[t].at[j], (*chip, c))
            for t, n in enumerate(BIG) for j, chip in enumerate(chips)]


def _landing_shapes(s):
    hl = s[BIG[0]].shape[0]
    return [_sds((3, hl) + _shard_shape(n), s[n].dtype) for n in BIG]


def chip_exchange(s, *, name):
    nt = len(BIG)

    def body(*refs):
        send_sems, recv_sems = refs[2 * nt:]
        x, y, c, chips = _place()
        copies = _chip_copies(send_sems, recv_sems, refs[:nt], refs[nt:2 * nt], c, chips)
        for cp in copies:
            cp.start()
        for cp in copies:
            cp.wait()

    outs = pl.pallas_call(
        body, out_shape=_landing_shapes(s), in_specs=[ANY] * nt, out_specs=[ANY] * nt,
        scratch_shapes=_dma_sems(3 * nt), name=name)(*[s[n] for n in BIG])
    return dict(zip(BIG, outs))


def chip_exchange_start(s, *, name):
    nt = len(BIG)

    def body(*refs):
        send_sems, recv_sems = refs[2 * nt:2 * nt + 2]
        x, y, c, chips = _place()
        for cp in _chip_copies(send_sems, recv_sems, refs[:nt], refs[nt:2 * nt], c, chips):
            cp.start()
        refs[-1][...] = jnp.zeros_like(refs[-1])

    hbm = lambda a: pltpu.with_memory_space_constraint(a, pltpu.HBM)
    lands = [hbm(lax.empty(d.shape, d.dtype)) for d in _landing_shapes(s)]
    srcs = [hbm(s[n]) for n in BIG]
    outs = pl.pallas_call(
        body,
        out_shape=(pltpu.SemaphoreType.DMA((3 * nt,)), pltpu.SemaphoreType.DMA((3 * nt,)),
                   *[pltpu.HBM(a.shape, a.dtype) for a in srcs + lands], _sds((8, HD), F32)),
        in_specs=[HBM] * (2 * nt), out_specs=(SEM, SEM, *[HBM] * (2 * nt), VM),
        input_output_aliases={t: 2 + t for t in range(2 * nt)},
        compiler_params=pltpu.CompilerParams(has_side_effects=pltpu.SideEffectType.DATAFLOW_SIDE_EFFECTING),
        name=name)(*srcs, *lands)
    return outs[0], outs[1], outs[2:2 + nt], outs[2 + nt:2 + 2 * nt], outs[-1]


def chip_exchange_wait(send_sems, recv_sems, srcs, lands, after, *, name):
    nt = len(BIG)

    def body(*refs):
        x, y, c, chips = _place()
        for cp in _chip_copies(refs[2 * nt], refs[2 * nt + 1], refs[:nt], refs[nt:2 * nt], c, chips):
            cp.wait_send()
            cp.wait_recv()

    outs = pl.pallas_call(
        body, out_shape=tuple(pltpu.HBM(a.shape, a.dtype) for a in list(srcs) + list(lands)),
        in_specs=[HBM] * (2 * nt) + [SEM, SEM, ANY], out_specs=tuple([HBM] * (2 * nt)),
        input_output_aliases={t: t for t in range(2 * nt)},
        compiler_params=pltpu.CompilerParams(has_side_effects=pltpu.SideEffectType.DATAFLOW_SIDE_EFFECTING),
        name=name)(*srcs, *lands, send_sems, recv_sems, after)
    return dict(zip(BIG, outs[:nt])), dict(zip(BIG, outs[nt:]))


def pair_share(f, l0, hh, *, name):
    nt = len(BIG)

    def body(*refs):
        o = refs[nt:2 * nt]
        send_sems, recv_sems = refs[2 * nt:]
        x, y, c, _ = _place()
        mine, theirs = pl.ds(l0 + hh * c, hh), pl.ds(l0 + hh * (1 - c), hh)
        copies = [_remote(send_sems, recv_sems, t, o[t].at[mine], o[t].at[mine], (x, y, 1 - c)) for t in range(nt)]
        for cp in copies:
            cp.start()
        for t, cp in enumerate(copies):
            cp.wait_send()
            _remote(send_sems, recv_sems, t, o[t].at[theirs], o[t].at[theirs], (x, y, c)).wait_recv()

    outs = pl.pallas_call(
        body, out_shape=[_sds(f[n].shape, f[n].dtype) for n in BIG], in_specs=[ANY] * nt, out_specs=[ANY] * nt,
        input_output_aliases={t: t for t in range(nt)}, scratch_shapes=_dma_sems(nt), name=name)(*[f[n] for n in BIG])
    return dict(zip(BIG, outs))


def pair_add(g, r, idx, *, tensor, name):
    _, _, full, blk = TENSORS[tensor]
    hh = r.shape[0]

    def body(idx_ref, g_ref, r_ref, o_ref):
        o_ref[...] = (g_ref[...].astype(F32) + r_ref[...].astype(F32)).astype(o_ref.dtype)

    own = pl.BlockSpec((None,) + blk, lambda *a: (a[0],) + a[1:-1])
    return pl.pallas_call(
        body, out_shape=_sds(r.shape, r.dtype),
        grid_spec=pltpu.PrefetchScalarGridSpec(
            num_scalar_prefetch=1, grid=(hh,) + tuple(f // b for f, b in zip(full, blk)),
            in_specs=[pl.BlockSpec((None,) + blk, lambda *a: (hh * a[-1][0] + a[0],) + a[1:-1]), own], out_specs=own),
        compiler_params=_cparams(("parallel",) * (1 + len(full))), name=name)(idx, g, r)


def chip_add(s, r, idx, totals, l0, *, tensor, name):
    kind, width, full, _ = TENSORS[tensor]
    shard = _shard_shape(tensor)
    hh = s.shape[0]
    zeros = (0,) * len(shard)

    def body(idx_ref, s_ref, r0_ref, r1_ref, r2_ref, t_ref, o_ref):
        o_ref[...] = ((s_ref[...].astype(F32) + r0_ref[...].astype(F32)) + r1_ref[...].astype(F32)) + r2_ref[...].astype(F32)

    if kind == "lead":
        mine = pl.BlockSpec((None, None) + shard, lambda i, ix: (i, ix[1]) + zeros)
    elif kind == "row":
        mine = pl.BlockSpec((None,) + shard, lambda i, ix: (i, ix[1]) + zeros[1:])
    else:
        mine = pl.BlockSpec((None,) + shard, lambda i, ix: (i,) + zeros[1:] + (ix[1],))
    peer = lambda j: pl.BlockSpec((None, None) + shard, lambda i, ix: (j, i) + zeros)
    return pl.pallas_call(
        body, out_shape=_sds(totals.shape, F32),
        grid_spec=pltpu.PrefetchScalarGridSpec(
            num_scalar_prefetch=1, grid=(hh,), in_specs=[mine, peer(0), peer(1), peer(2), pl.BlockSpec(memory_space=pl.ANY)],
            out_specs=pl.BlockSpec((None,) + shard, lambda i, ix: (l0 + hh * ix[0] + i,) + zeros)),
        input_output_aliases={5: 0}, compiler_params=_cparams(("parallel",)), name=name)(idx, s, r, r, r, totals)


def _flat_rows(arrs):
    v = jnp.concatenate([a.reshape(-1) for a in arrs])
    n = -(-v.shape[0] // 1024) * 1024
    return jnp.pad(v, (0, n - v.shape[0])).reshape(n // HD, HD)


def _unflat(buf, shapes):
    v, out, o = buf.reshape(-1), [], 0
    for s in shapes:
        n = int(np.prod(s))
        out.append(v[o:o + n].reshape(s))
        o += n
    return out


WEIGHTS = ("norm1_g", "norm2_g", "w_ada", "b_ada", "w_in", "w_gla_a2", "b_gla_a", "b_fox_f", "ret_norm_g", "gla_norm_g",
           "q_norm_g", "k_norm_g", "w_br", "w_mg", "b_mg", "w_o", "w_up", "w_conv", "b_conv", "w_down")
REPLICATED = ("norm1_g", "norm2_g", "b_gla_a", "b_fox_f", "ret_norm_g", "gla_norm_g", "q_norm_g", "k_norm_g", "b_mg", "b_conv")
ADAM_BLOCKS = dict(w_ada=(1, 256, 1536), w_in=(1, 256, 1285), w_br=(1, 3, BW, 256), w_mg=(1, 512, 768), w_o=(2, 256, D),
                   w_up=(1, 256, 1408), w_down=(1, 352, D))
ALL_AXES = ("x", "y", "c")


def kernel(x, c, norm1_g, norm2_g, w_ada, b_ada, w_in, w_gla_a2, b_gla_a, b_fox_f, ret_norm_g, gla_norm_g, q_norm_g, k_norm_g, w_br, w_mg, b_mg, w_o, w_up, w_conv, b_conv, w_down, loss_target, m_norm1_g, m_norm2_g, m_w_ada, m_b_ada, m_w_in, m_w_gla_a2, m_b_gla_a, m_b_fox_f, m_ret_norm_g, m_gla_norm_g, m_q_norm_g, m_k_norm_g, m_w_br, m_w_mg, m_b_mg, m_w_o, m_w_up, m_w_conv, m_b_conv, m_w_down, v_norm1_g, v_norm2_g, v_w_ada, v_b_ada, v_w_in, v_w_gla_a2, v_b_gla_a, v_b_fox_f, v_ret_norm_g, v_gla_norm_g, v_q_norm_g, v_k_norm_g, v_w_br, v_w_mg, v_b_mg, v_w_o, v_w_up, v_w_conv, v_b_conv, v_w_down):
    w = dict(zip(WEIGHTS, (norm1_g, norm2_g, w_ada, b_ada, w_in, w_gla_a2, b_gla_a, b_fox_f, ret_norm_g, gla_norm_g,
                           q_norm_g, k_norm_g, w_br, w_mg, b_mg, w_o, w_up, w_conv, b_conv, w_down)))
    m = dict(zip(WEIGHTS, (m_norm1_g, m_norm2_g, m_w_ada, m_b_ada, m_w_in, m_w_gla_a2, m_b_gla_a, m_b_fox_f, m_ret_norm_g,
                           m_gla_norm_g, m_q_norm_g, m_k_norm_g, m_w_br, m_w_mg, m_b_mg, m_w_o, m_w_up, m_w_conv, m_b_conv,
                           m_w_down)))
    v = dict(zip(WEIGHTS, (v_norm1_g, v_norm2_g, v_w_ada, v_b_ada, v_w_in, v_w_gla_a2, v_b_gla_a, v_b_fox_f, v_ret_norm_g,
                           v_gla_norm_g, v_q_norm_g, v_k_norm_g, v_w_br, v_w_mg, v_b_mg, v_w_o, v_w_up, v_w_conv, v_b_conv,
                           v_w_down)))
    nl = norm1_g.shape[0]
    seq = x.shape[1]
    xi, yi, ci = lax.axis_index("x"), lax.axis_index("y"), lax.axis_index("c")
    k_me = 2 * xi + yi
    b_me = 4 * xi + 2 * yi + ci
    ada_n = w_ada.shape[2]
    a2_n, conv_n = w_gla_a2.shape[2], w_conv.shape[2]

    shards = {n: w[n].astype(MMT) for n in BIG}
    started = gather_start(shards, 0, name="gather0_start")

    def gather_finish(l, started, after):
        send_sems, recv_sems, lands, _ = started
        lands = gather_wait(send_sems, recv_sems, shards, lands, after, l, name=f"gather{l}_wait")
        big = gather_forward(shards, lands, l, name=f"gather{l}_forward")
        big["w1"] = build_w1(big["w_in"], big["w_mg"])
        return big

    blk = _flat_rows([c, w_gla_a2, w_conv]) + started[-1][0, 0]
    g1 = small_allgather(blk, name="gather_small").reshape(8, blk.shape[0], HD)
    c_all = g1[:, :D // HD].reshape(8, D)
    by_chip = g1[0::2].reshape(4, -1)[:, D:]
    a2_sh, conv_sh = by_chip[:, :nl * GLR * a2_n], by_chip[:, nl * GLR * a2_n:nl * (GLR * a2_n + 3 * conv_n)]
    full_small = dict(
        w_gla_a2=a2_sh.reshape(4, nl, GLR, a2_n).transpose(1, 2, 0, 3).reshape(nl, GLR, 4 * a2_n),
        w_conv=conv_sh.reshape(4, nl, 3, conv_n).transpose(1, 2, 0, 3).reshape(nl, 3, 4 * conv_n))

    b_ada_sh = lax.dynamic_slice_in_dim(b_ada, k_me * ada_n, ada_n, axis=1)[:, None, :]
    mod_sh = ada_mod(c_all, w_ada, b_ada_sh)
    g2 = small_allgather(mod_sh.reshape(nl * 8, ada_n), name="gather_mod").reshape(4, 2, nl, 8, ada_n)[:, 0]
    mod_me = lax.dynamic_index_in_dim(g2, b_me, axis=2, keepdims=False).transpose(1, 0, 2).reshape(nl, 4 * ada_n)

    wsmall = {n: w[n] for n in REPLICATED}
    wsmall.update(full_small)
    mods = [[mod_me[l:l + 1, i * D:(i + 1) * D] for i in range(6)] for l in range(nl)]
    big = gather_finish(0, started, mod_me)

    cosf, sinf = _rope_tables(seq)
    xs, saved, params = x[0], [], []
    for l in range(nl):
        if l + 1 < nl:
            started = gather_start(shards, l + 1, name=f"gather{l + 1}_start")
            xs = xs + started[-1][0, 0]
        params.append(layer_params(wsmall, big, l))
        xs, sv = layer_fwd(xs, mods[l], params[l], cosf, sinf)
        saved.append(sv)
        if l + 1 < nl:
            big = gather_finish(l + 1, started, xs)
    loss_part, dx = loss_and_grad(xs, loss_target[0], name="loss")
    loss = lax.psum(loss_part[0, 0], ALL_AXES)
    grads, dmods = [None] * nl, [None] * nl
    idx = jnp.stack([ci, k_me]).astype(jnp.int32)
    totals = {n: lax.empty((nl,) + _shard_shape(n), F32) for n in BIG}

    def finish_group(group, chip_sum, from_chips, totals):
        totals = {n: chip_add(chip_sum[n], from_chips[n], idx, totals[n], RS_GROUP * group, tensor=n,
                              name=f"rs{group}_chip_add_{n}") for n in BIG}
        return pair_share(totals, RS_GROUP * group, RS_GROUP // 2, name=f"rs{group}_pair_share")

    pending = None
    for group in reversed(range(nl // RS_GROUP)):
        layers = range(RS_GROUP * group, RS_GROUP * (group + 1))
        stacks = {n: lax.empty((RS_GROUP,) + TENSORS[n][2], MMT) for n in BIG if n != "w_br"}
        if pending is not None:
            dx = dx + pending[-1][0, 0]
        for l in reversed(layers):
            dx, g, dmods[l], stacks = layer_bwd(dx, mods[l], params[l], saved[l], cosf, sinf, stacks, l - layers[0])
            grads[l] = layer_grads(g)
        stacks["w_br"] = jnp.stack([grads[l]["w_br"].astype(MMT) for l in layers])
        from_sibling = pair_exchange(stacks, name=f"rs{group}_pair_exchange")
        chip_sum = {n: pair_add(stacks[n], from_sibling[n], idx, tensor=n, name=f"rs{group}_pair_add_{n}") for n in BIG}
        if pending is not None:
            earlier, send_sems, recv_sems, srcs, lands, _ = pending
            sums, from_chips = chip_exchange_wait(send_sems, recv_sems, srcs, lands, dx,
                                                  name=f"rs{earlier}_chip_exchange_wait")
            totals = finish_group(earlier, sums, from_chips, totals)
        pending = (group, *chip_exchange_start(chip_sum, name=f"rs{group}_chip_exchange_start"))

    small_names = REPLICATED + ("w_gla_a2", "w_conv")
    small_shapes = [(nl, 6 * D)] + [(nl,) + grads[0][n].shape for n in small_names]
    vec = _flat_rows([jnp.concatenate(dmods, axis=0)] + [jnp.stack([grads[l][n] for l in range(nl)]) for n in small_names])
    vec = vec + pending[-1][0, 0]
    gs = small_allgather(vec, name="gather_small_grads").reshape(8, vec.shape[0], HD)
    summed = _unflat(sum_devices(gs), small_shapes)
    grad = dict(zip(small_names, summed[1:]))
    grad["b_ada"] = summed[0]
    grad["w_gla_a2"] = lax.dynamic_slice_in_dim(grad["w_gla_a2"], k_me * a2_n, a2_n, axis=2)
    grad["w_conv"] = lax.dynamic_slice_in_dim(grad["w_conv"], k_me * conv_n, conv_n, axis=2)
    dmod_all = gs[:, :nl * 6 * D // HD].reshape(8, nl, 6 * D)
    dmod_sh = lax.dynamic_slice_in_dim(dmod_all, k_me * ada_n, ada_n, axis=2).transpose(1, 0, 2)
    grad["w_ada"] = ada_dw(c_all, dmod_sh)

    delta, new_m, new_v = {}, {}, {}
    delta["w_ada"], new_m["w_ada"], new_v["w_ada"] = adamw(w["w_ada"], grad["w_ada"], m["w_ada"], v["w_ada"],
                                                          block=ADAM_BLOCKS["w_ada"], name="adamw_w_ada")
    rest = [n for n in WEIGHTS if n not in ADAM_BLOCKS]
    shapes = [w[n].shape for n in rest]
    flat = [_flat_rows([t[n] for n in rest]) for t in (w, grad, m, v)]
    outs = adamw(*flat, block=flat[0].shape, name="adamw_small")
    for t, o in zip((delta, new_m, new_v), outs):
        t.update(zip(rest, _unflat(o, shapes)))

    earlier, send_sems, recv_sems, srcs, lands, _ = pending
    sums, from_chips = chip_exchange_wait(send_sems, recv_sems, srcs, lands, outs[0], name=f"rs{earlier}_chip_exchange_wait")
    grad.update(finish_group(earlier, sums, from_chips, totals))
    for n in BIG:
        delta[n], new_m[n], new_v[n] = adamw(w[n], grad[n], m[n], v[n], block=ADAM_BLOCKS[n], name="adamw_" + n)

    return (loss, dx[None], *[grad[n] for n in WEIGHTS], *[delta[n] for n in WEIGHTS], *[new_m[n] for n in WEIGHTS],
            *[new_v[n] for n in WEIGHTS])
```

```python
import functools

import numpy as np
import jax
import jax.numpy as jnp
from jax import lax
from jax.experimental import pallas as pl
from jax.experimental.pallas import tpu as pltpu

F32 = jnp.float32
MMT = jnp.bfloat16
HI = lax.Precision.HIGHEST

D = 1024
DEPTH = 4
NH = 4
HD = 128
BW = NH * HD
CH = 64
GDK = 64
GLR = 16
DFF = 2816
EPS = 1e-6
ROPE_BASE = 10000.0

GP, RG, GG, FV, RQ, RK, RV, GQ, GK, GV, FQ, FK, LR, FF = (
    0, 3072, 3584, 4096, 4608, 5120, 5632, 6144, 6400, 6656, 7168, 7680, 8192, 8320)
NZZ = 8448
WZ0 = 3072
IN_W = 5140
W_IN_COLS = dict(rqkv=(0, 1536), rg=(1536, 2048), gqkv=(2048, 3072), lr=(3072, 3088), gg=(3088, 3600), fqk=(3600, 4624),
                 fv=(4624, 5136), ff=(5136, 5140))

VMEM_LIMIT = 56 * 1024 * 1024

ADAM_LR, ADAM_B1, ADAM_B2, ADAM_EPS, ADAM_WD, ADAM_STEP = 0.001, 0.9, 0.999, 1e-08, 0.01, 10


def _cparams(sem=None):
    return pltpu.CompilerParams(dimension_semantics=sem, vmem_limit_bytes=VMEM_LIMIT)


def _sds(shape, dtype):
    return jax.ShapeDtypeStruct(tuple(shape), dtype)


def _dot(a, b, precision=None):
    return lax.dot_general(a, b, (((1,), (0,)), ((), ())), precision=precision, preferred_element_type=F32)


def _dot_nt(a, b, precision=None):
    return lax.dot_general(a, b, (((1,), (1,)), ((), ())), precision=precision, preferred_element_type=F32)


def _dot_tn(a, b, precision=None):
    return lax.dot_general(a, b, (((0,), (0,)), ((), ())), precision=precision, preferred_element_type=F32)


def _silu(x):
    return x * jax.nn.sigmoid(x)


def _log_sigmoid(x):
    return jnp.minimum(x, 0.0) - jnp.log(1.0 + jnp.exp(jnp.minimum(x, -x)))


@jax.custom_vjp
def _swap_halves(x):
    return pltpu.roll(x, HD // 2, 1)


_swap_halves.defvjp(lambda x: (_swap_halves(x), None), lambda _, g: (_swap_halves(g),))


@jax.custom_vjp
def _bdot(a, b):
    return _dot(a.astype(MMT), b.astype(MMT))


@jax.custom_vjp
def _bdot_nt(a, b):
    return _dot_nt(a.astype(MMT), b.astype(MMT))


@jax.custom_vjp
def _bdot_tn(a, b):
    return _dot_tn(a.astype(MMT), b.astype(MMT))


_bdot.defvjp(lambda a, b: (_bdot(a, b), (a, b)), lambda r, g: (_bdot_nt(g, r[1]), _bdot_tn(r[0], g)))
_bdot_nt.defvjp(lambda a, b: (_bdot_nt(a, b), (a, b)), lambda r, g: (_bdot(g, r[1]), _bdot_tn(g, r[0])))
_bdot_tn.defvjp(lambda a, b: (_bdot_tn(a, b), (a, b)), lambda r, g: (_bdot_nt(r[1], g), _bdot(r[0], g)))


def _stacked(blk, idx, layer):
    if layer is None:
        return pl.BlockSpec(blk, idx)
    return pl.BlockSpec((None,) + blk, lambda i, j: (layer,) + idx(i, j))


def mm_nn(a, b, *, tm, tn, out_dtype, name, layer=None):
    m, k = a.shape
    n = b.shape[-1]

    def body(a_ref, b_ref, o_ref):
        o_ref[...] = _dot(a_ref[...], b_ref[...]).astype(o_ref.dtype)

    return pl.pallas_call(
        body, grid=(m // tm, n // tn),
        in_specs=[pl.BlockSpec((tm, k), lambda i, j: (i, 0)), _stacked((k, tn), lambda i, j: (0, j), layer)],
        out_specs=pl.BlockSpec((tm, tn), lambda i, j: (i, j)),
        out_shape=_sds((m, n), out_dtype), compiler_params=_cparams(("parallel", "parallel")), name=name)(a, b)


def mm_nn_residual(a, b, res, gate, *, tm, tn, name, layer=None):
    m, k = a.shape
    n = b.shape[-1]

    def body(a_ref, b_ref, r_ref, g_ref, x_ref, y_ref):
        acc = _dot(a_ref[...], b_ref[...])
        y_ref[...] = acc
        x_ref[...] = r_ref[...] + g_ref[...] * acc

    return pl.pallas_call(
        body, grid=(m // tm, n // tn),
        in_specs=[pl.BlockSpec((tm, k), lambda i, j: (i, 0)), _stacked((k, tn), lambda i, j: (0, j), layer),
                  pl.BlockSpec((tm, tn), lambda i, j: (i, j)), pl.BlockSpec((1, tn), lambda i, j: (0, j))],
        out_specs=[pl.BlockSpec((tm, tn), lambda i, j: (i, j)), pl.BlockSpec((tm, tn), lambda i, j: (i, j))],
        out_shape=[_sds((m, n), F32), _sds((m, n), F32)],
        compiler_params=_cparams(("parallel", "parallel")), name=name)(a, b, res, gate)


def mm_nt(a, b, *, tm, tn, out_dtype, name, layer=None):
    m, k = a.shape
    n = b.shape[-2]

    def body(a_ref, b_ref, o_ref):
        o_ref[...] = _dot_nt(a_ref[...], b_ref[...]).astype(o_ref.dtype)

    return pl.pallas_call(
        body, grid=(m // tm, n // tn),
        in_specs=[pl.BlockSpec((tm, k), lambda i, j: (i, 0)), _stacked((tn, k), lambda i, j: (j, 0), layer)],
        out_specs=pl.BlockSpec((tm, tn), lambda i, j: (i, j)),
        out_shape=_sds((m, n), out_dtype), compiler_params=_cparams(("parallel", "parallel")), name=name)(a, b)


def mm_tn(a, b, *, tm, tn, out_dtype, name, col0=0, ncols=None, stack=None, layer=None):
    s, m = a.shape
    n = b.shape[1] - col0 if ncols is None else ncols
    c0 = col0 // tn

    def body(a_ref, b_ref, *rest):
        o_ref = rest[-1]
        o_ref[...] = _dot_tn(a_ref[...], b_ref[...]).astype(o_ref.dtype)

    in_specs = [pl.BlockSpec((s, tm), lambda i, j: (0, i)), pl.BlockSpec((s, tn), lambda i, j: (0, c0 + j))]
    if stack is None:
        return pl.pallas_call(
            body, grid=(m // tm, n // tn), in_specs=in_specs, out_specs=pl.BlockSpec((tm, tn), lambda i, j: (i, j)),
            out_shape=_sds((m, n), out_dtype), compiler_params=_cparams(("parallel", "parallel")), name=name)(a, b)
    return pl.pallas_call(
        body, grid=(m // tm, n // tn), in_specs=in_specs + [pl.BlockSpec(memory_space=pl.ANY)],
        out_specs=pl.BlockSpec((None, tm, tn), lambda i, j: (layer, i, j)),
        out_shape=_sds(stack.shape, stack.dtype), input_output_aliases={2: 0},
        compiler_params=_cparams(("parallel", "parallel")), name=name)(a, b, stack)


def _row_tile(s):
    return min(256, s)


def _norm_mod_f(x, g, scale, shift):
    r = lax.rsqrt(jnp.mean(x * x, axis=-1, keepdims=True) + EPS)
    return (x * r * g) * (1.0 + scale) + shift


def norm_mod(x, g, scale, shift, *, name):
    s = x.shape[0]
    t = _row_tile(s)

    def body(x_ref, g_ref, sc_ref, sh_ref, o_ref):
        o_ref[...] = _norm_mod_f(x_ref[...], g_ref[...], sc_ref[...], sh_ref[...]).astype(o_ref.dtype)

    vec = pl.BlockSpec((1, D), lambda i: (0, 0))
    return pl.pallas_call(
        body, grid=(s // t,), in_specs=[pl.BlockSpec((t, D), lambda i: (i, 0)), vec, vec, vec],
        out_specs=pl.BlockSpec((t, D), lambda i: (i, 0)), out_shape=_sds((s, D), MMT),
        compiler_params=_cparams(("parallel",)), name=name)(x, g, scale, shift)


def norm_mod_bwd(x, dh, dres, g, scale, shift, *, name):
    s = x.shape[0]
    t = _row_tile(s)

    def body(x_ref, dh_ref, dr_ref, g_ref, sc_ref, sh_ref, dx_ref, dg_ref, dsc_ref, dsh_ref):
        @pl.when(pl.program_id(0) == 0)
        def _():
            dg_ref[...] = jnp.zeros_like(dg_ref)
            dsc_ref[...] = jnp.zeros_like(dsc_ref)
            dsh_ref[...] = jnp.zeros_like(dsh_ref)

        _, vjp = jax.vjp(_norm_mod_f, x_ref[...], g_ref[...], sc_ref[...], sh_ref[...])
        dx, dg, dsc, dsh = vjp(dh_ref[...])
        dx_ref[...] = dr_ref[...] + dx
        dg_ref[...] += dg
        dsc_ref[...] += dsc
        dsh_ref[...] += dsh

    row = pl.BlockSpec((t, D), lambda i: (i, 0))
    vec = pl.BlockSpec((1, D), lambda i: (0, 0))
    return pl.pallas_call(
        body, grid=(s // t,), in_specs=[row, row, row, vec, vec, vec], out_specs=[row, vec, vec, vec],
        out_shape=[_sds((s, D), F32)] + [_sds((1, D), F32)] * 3,
        compiler_params=_cparams(("arbitrary",)), name=name)(x, dh, dres, g, scale, shift)


def gate_bwd(dx, y, gate, *, name):
    s = dx.shape[0]
    t = _row_tile(s)

    def body(dx_ref, y_ref, g_ref, dy_ref, dg_ref):
        @pl.when(pl.program_id(0) == 0)
        def _():
            dg_ref[...] = jnp.zeros_like(dg_ref)

        dxv = dx_ref[...]
        dy_ref[...] = (g_ref[...] * dxv).astype(dy_ref.dtype)
        dg_ref[...] += jnp.sum(dxv * y_ref[...], axis=0, keepdims=True)

    row = pl.BlockSpec((t, D), lambda i: (i, 0))
    vec = pl.BlockSpec((1, D), lambda i: (0, 0))
    return pl.pallas_call(
        body, grid=(s // t,), in_specs=[row, row, vec], out_specs=[row, vec],
        out_shape=[_sds((s, D), MMT), _sds((1, D), F32)],
        compiler_params=_cparams(("arbitrary",)), name=name)(dx, y, gate)


def loss_and_grad(xf, target, *, name):
    s = xf.shape[0]
    t = _row_tile(s)

    def body(x_ref, t_ref, l_ref, dx_ref):
        @pl.when(pl.program_id(0) == 0)
        def _():
            l_ref[...] = jnp.zeros_like(l_ref)

        e = x_ref[...] - t_ref[...]
        dx_ref[...] = e * (1.0 / D)
        l_ref[...] += 0.5 * jnp.sum(jnp.sum(e * e, axis=1, keepdims=True), axis=0, keepdims=True) * (1.0 / D)

    row = pl.BlockSpec((t, D), lambda i: (i, 0))
    return pl.pallas_call(
        body, grid=(s // t,), in_specs=[row, row], out_specs=[pl.BlockSpec((1, 1), lambda i: (0, 0)), row],
        out_shape=[_sds((1, 1), F32), _sds((s, D), F32)],
        compiler_params=_cparams(("arbitrary",)), name=name)(xf, target)


def _ret_consts():
    log_g = np.log1p(-np.exp2(-5.0 - np.arange(NH, dtype=np.float32))).astype(np.float32)
    idx = np.arange(CH, dtype=np.float32)
    d_intra = np.exp(np.abs(idx[:, None] - idx[None, :])[None] * log_g[:, None, None]).astype(np.float32)
    k_w = np.exp((CH - 1.0 - idx)[None, :] * log_g[:, None]).astype(np.float32)
    q_w = np.exp((idx + 1.0)[None, :] * log_g[:, None]).astype(np.float32)
    g_chunk = [float(v) for v in np.exp(np.float32(CH) * log_g).astype(np.float32)]
    bc = lambda a: np.ascontiguousarray(np.broadcast_to(a[:, :, None], (NH, CH, HD)))
    return jnp.asarray(d_intra), jnp.asarray(bc(k_w)), jnp.asarray(bc(q_w)), g_chunk


def _rope_tables(s):
    half = HD // 2
    inv_freq = (ROPE_BASE ** (-np.arange(half, dtype=np.float64) / half)).astype(np.float32)
    ang = (np.arange(s, dtype=np.float32)[:, None] * inv_freq[None, :]).astype(np.float64)
    cos, sin = np.cos(ang).astype(np.float32), np.sin(ang).astype(np.float32)
    return jnp.asarray(np.concatenate([cos, cos], axis=1)), jnp.asarray(np.concatenate([-sin, sin], axis=1))


def _ret_chunk(qs, ks, vs, rs, cos, sin, dintra, kw, qw, g_chunk):
    outs, rn = [], []
    for h in range(NH):
        q = qs[h] * cos + _swap_halves(qs[h]) * sin
        k = (ks[h] * cos + _swap_halves(ks[h]) * sin) * (HD ** -0.5)
        sc = _bdot_nt(q, k) * dintra[h]
        outs.append(_bdot(sc, vs[h]) + _bdot(q * qw[h], rs[h]))
        rn.append(g_chunk[h] * rs[h] + _bdot_tn(k * kw[h], vs[h]))
    return outs, rn


def _heads(x):
    return [x[:, h * HD:(h + 1) * HD] for h in range(NH)]


def retention_fwd(zz, cosf, sinf, *, name):
    s = zz.shape[0]
    n = s // CH
    dintra, kw, qw, g_chunk = _ret_consts()

    def body(q_ref, k_ref, v_ref, c_ref, s_ref, di_ref, kw_ref, qw_ref, o_ref, rp_ref, r_scr):
        @pl.when(pl.program_id(0) == 0)
        def _():
            r_scr[...] = jnp.zeros_like(r_scr)

        rprev = r_scr[...]
        rp_ref[0] = rprev
        outs, rn = _ret_chunk(_heads(q_ref[...]), _heads(k_ref[...]), _heads(v_ref[...]),
                              [rprev[h * HD:(h + 1) * HD] for h in range(NH)], c_ref[...], s_ref[...],
                              [di_ref[h] for h in range(NH)], [kw_ref[h] for h in range(NH)],
                              [qw_ref[h] for h in range(NH)], g_chunk)
        o_ref[...] = jnp.concatenate(outs, axis=1)
        r_scr[...] = jnp.concatenate(rn, axis=0)

    col = lambda c: pl.BlockSpec((CH, BW), lambda i: (i, c // BW))
    tab = pl.BlockSpec((CH, HD), lambda i: (i, 0))
    cst = lambda shp: pl.BlockSpec(shp, lambda i: (0,) * len(shp))
    return pl.pallas_call(
        body, grid=(n,),
        in_specs=[col(RQ), col(RK), col(RV), tab, tab, cst((NH, CH, CH)), cst((NH, CH, HD)), cst((NH, CH, HD))],
        out_specs=[pl.BlockSpec((CH, BW), lambda i: (i, 0)), pl.BlockSpec((1, BW, HD), lambda i: (i, 0, 0))],
        out_shape=[_sds((s, BW), F32), _sds((n, BW, HD), F32)],
        scratch_shapes=[pltpu.VMEM((BW, HD), F32)],
        compiler_params=_cparams(("arbitrary",)), name=name)(zz, zz, zz, cosf, sinf, dintra, kw, qw)


def retention_bwd(zz, cosf, sinf, rprev, do, dzz, *, name):
    s = zz.shape[0]
    n = s // CH
    dintra, kw, qw, g_chunk = _ret_consts()

    def body(q_ref, k_ref, v_ref, c_ref, s_ref, di_ref, kw_ref, qw_ref, rp_ref, do_ref, dzz_ref, dz_ref, dr_scr):
        @pl.when(pl.program_id(0) == 0)
        def _():
            dr_scr[...] = jnp.zeros_like(dr_scr)

        rprev_v = rp_ref[0]
        f = functools.partial(_ret_chunk, cos=c_ref[...], sin=s_ref[...],
                              dintra=[di_ref[h] for h in range(NH)], kw=[kw_ref[h] for h in range(NH)],
                              qw=[qw_ref[h] for h in range(NH)], g_chunk=g_chunk)
        _, vjp = jax.vjp(f, _heads(q_ref[...]), _heads(k_ref[...]), _heads(v_ref[...]),
                         [rprev_v[h * HD:(h + 1) * HD] for h in range(NH)])
        dr = dr_scr[...]
        dq, dk, dv, drp = vjp((_heads(do_ref[...]), [dr[h * HD:(h + 1) * HD] for h in range(NH)]))
        dz_ref[...] = jnp.concatenate(dq + dk + dv, axis=1).astype(dz_ref.dtype)
        dr_scr[...] = jnp.concatenate(drp, axis=0)

    col = lambda c: pl.BlockSpec((CH, BW), lambda i: (n - 1 - i, c // BW))
    tab = pl.BlockSpec((CH, HD), lambda i: (n - 1 - i, 0))
    cst = lambda shp: pl.BlockSpec(shp, lambda i: (0,) * len(shp))
    return pl.pallas_call(
        body, grid=(n,),
        in_specs=[col(RQ), col(RK), col(RV), tab, tab, cst((NH, CH, CH)), cst((NH, CH, HD)), cst((NH, CH, HD)),
                  pl.BlockSpec((1, BW, HD), lambda i: (n - 1 - i, 0, 0)), pl.BlockSpec((CH, BW), lambda i: (n - 1 - i, 0)),
                  pl.BlockSpec(memory_space=pl.ANY)],
        out_specs=pl.BlockSpec((CH, 3 * BW), lambda i: (n - 1 - i, RQ // (3 * BW))),
        out_shape=_sds(dzz.shape, dzz.dtype), input_output_aliases={10: 0},
        scratch_shapes=[pltpu.VMEM((BW, HD), F32)],
        compiler_params=_cparams(("arbitrary",)), name=name)(zz, zz, zz, cosf, sinf, dintra, kw, qw, rprev, do, dzz)


GKW = NH * GDK


def _gla_consts():
    tri = np.tril(np.ones((CH, CH), np.float32))
    mask_t = np.zeros((BW, GKW), np.float32)
    for h in range(NH):
        mask_t[h * HD:(h + 1) * HD, h * GDK:(h + 1) * GDK] = 1.0
    return jnp.asarray(tri), jnp.asarray(mask_t)


def _gla_chunk(q, k, v, glr, w_a2, b_a, st, tri, mask_t):
    la = _log_sigmoid(_bdot(glr, w_a2) + b_a) * (1.0 / 16.0)
    bc = _dot(tri, la, HI)
    be = jnp.sum(la, axis=0, keepdims=True)
    kv_t = _bdot_tn(v, k * jnp.exp(be - bc)) * mask_t
    sn = jnp.exp(be) * st + kv_t
    return _bdot_nt(q * (GDK ** -0.5), sn), sn


def gla_fwd(zz, w_a2p, b_a, *, name):
    s = zz.shape[0]
    n = s // CH
    tri, mask_t = _gla_consts()

    def body(q_ref, k_ref, v_ref, lr_ref, w_ref, b_ref, tri_ref, m_ref, o_ref, sp_ref, st_scr):
        @pl.when(pl.program_id(0) == 0)
        def _():
            st_scr[...] = jnp.zeros_like(st_scr)

        sp = st_scr[...]
        sp_ref[0] = sp
        o, sn = _gla_chunk(q_ref[...], k_ref[...], v_ref[...], lr_ref[...], w_ref[...], b_ref[...], sp,
                           tri_ref[...], m_ref[...])
        o_ref[...] = o
        st_scr[...] = sn

    cst = lambda shp: pl.BlockSpec(shp, lambda i: (0,) * len(shp))
    return pl.pallas_call(
        body, grid=(n,),
        in_specs=[pl.BlockSpec((CH, GKW), lambda i: (i, GQ // GKW)), pl.BlockSpec((CH, GKW), lambda i: (i, GK // GKW)),
                  pl.BlockSpec((CH, BW), lambda i: (i, GV // BW)), pl.BlockSpec((CH, HD), lambda i: (i, LR // HD)),
                  cst((HD, GKW)), cst((1, GKW)), cst((CH, CH)), cst((BW, GKW))],
        out_specs=[pl.BlockSpec((CH, BW), lambda i: (i, 0)), pl.BlockSpec((1, BW, GKW), lambda i: (i, 0, 0))],
        out_shape=[_sds((s, BW), F32), _sds((n, BW, GKW), F32)],
        scratch_shapes=[pltpu.VMEM((BW, GKW), F32)],
        compiler_params=_cparams(("arbitrary",)), name=name)(zz, zz, zz, zz, w_a2p, b_a, tri, mask_t)


def gla_bwd(zz, w_a2p, b_a, sprev, do, dzz, *, name):
    s = zz.shape[0]
    n = s // CH
    tri, mask_t = _gla_consts()

    def body(q_ref, k_ref, v_ref, lr_ref, w_ref, b_ref, tri_ref, m_ref, sp_ref, do_ref, dzz_ref,
             dz_ref, dlr_ref, dw_ref, db_ref, ds_scr):
        @pl.when(pl.program_id(0) == 0)
        def _():
            ds_scr[...] = jnp.zeros_like(ds_scr)
            dw_ref[...] = jnp.zeros_like(dw_ref)
            db_ref[...] = jnp.zeros_like(db_ref)

        f = functools.partial(_gla_chunk, tri=tri_ref[...], mask_t=m_ref[...])
        _, vjp = jax.vjp(f, q_ref[...], k_ref[...], v_ref[...], lr_ref[...], w_ref[...], b_ref[...], sp_ref[0])
        dq, dk, dv, dlr, dw, db, dsp = vjp((do_ref[...], ds_scr[...]))
        dz_ref[...] = jnp.concatenate([dq, dk, dv], axis=1).astype(dz_ref.dtype)
        dlr_ref[...] = dlr.astype(dlr_ref.dtype)
        dw_ref[...] += dw
        db_ref[...] += db
        ds_scr[...] = dsp

    cst = lambda shp: pl.BlockSpec(shp, lambda i: (0,) * len(shp))
    r = lambda i: n - 1 - i
    return pl.pallas_call(
        body, grid=(n,),
        in_specs=[pl.BlockSpec((CH, GKW), lambda i: (r(i), GQ // GKW)), pl.BlockSpec((CH, GKW), lambda i: (r(i), GK // GKW)),
                  pl.BlockSpec((CH, BW), lambda i: (r(i), GV // BW)), pl.BlockSpec((CH, HD), lambda i: (r(i), LR // HD)),
                  cst((HD, GKW)), cst((1, GKW)), cst((CH, CH)), cst((BW, GKW)),
                  pl.BlockSpec((1, BW, GKW), lambda i: (r(i), 0, 0)), pl.BlockSpec((CH, BW), lambda i: (r(i), 0)),
                  pl.BlockSpec(memory_space=pl.ANY)],
        out_specs=[pl.BlockSpec((CH, 2 * GKW + BW), lambda i: (r(i), GQ // (2 * GKW + BW))),
                   pl.BlockSpec((CH, HD), lambda i: (r(i), 0)), cst((HD, GKW)), cst((1, GKW))],
        out_shape=[_sds(dzz.shape, dzz.dtype), _sds((s, HD), MMT), _sds((HD, GKW), F32), _sds((1, GKW), F32)],
        input_output_aliases={10: 0}, scratch_shapes=[pltpu.VMEM((BW, GKW), F32)],
        compiler_params=_cparams(("arbitrary",)), name=name)(zz, zz, zz, zz, w_a2p, b_a, tri, mask_t, sprev, do, dzz)


def _fox_pre_f(fqs, fks, ff, gq, gk, bf):
    def rms(x, g):
        return x * lax.rsqrt(jnp.mean(x * x, axis=-1, keepdims=True) + EPS) * g

    qn = [rms(x, gq) * (HD ** -0.5) for x in fqs]
    kn = [rms(x, gk) for x in fks]
    return qn, kn, _log_sigmoid(ff + bf)


def fox_pre(zz, gq, gk, bf, *, name):
    s = zz.shape[0]
    t = _row_tile(s)
    tri = jnp.asarray(np.tril(np.ones((t, t), np.float32)))

    def body(q_ref, k_ref, f_ref, gq_ref, gk_ref, b_ref, tri_ref, qn_ref, kn_ref, cum_ref, carry):
        @pl.when(pl.program_id(0) == 0)
        def _():
            carry[...] = jnp.zeros_like(carry)

        qn, kn, lf = _fox_pre_f(_heads(q_ref[...]), _heads(k_ref[...]), f_ref[...], gq_ref[...], gk_ref[...], b_ref[...])
        qn_ref[...] = jnp.concatenate(qn, axis=1).astype(qn_ref.dtype)
        kn_ref[...] = jnp.concatenate(kn, axis=1).astype(kn_ref.dtype)
        cum_ref[...] = _dot(tri_ref[...], lf, HI) + carry[...]
        carry[...] += jnp.sum(lf, axis=0, keepdims=True)

    vec = pl.BlockSpec((1, HD), lambda i: (0, 0))
    return pl.pallas_call(
        body, grid=(s // t,),
        in_specs=[pl.BlockSpec((t, BW), lambda i: (i, FQ // BW)), pl.BlockSpec((t, BW), lambda i: (i, FK // BW)),
                  pl.BlockSpec((t, HD), lambda i: (i, FF // HD)), vec, vec, vec, pl.BlockSpec((t, t), lambda i: (0, 0))],
        out_specs=[pl.BlockSpec((t, BW), lambda i: (i, 0)), pl.BlockSpec((t, BW), lambda i: (i, 0)),
                   pl.BlockSpec((t, HD), lambda i: (i, 0))],
        out_shape=[_sds((s, BW), MMT), _sds((s, BW), MMT), _sds((s, HD), F32)],
        scratch_shapes=[pltpu.VMEM((1, HD), F32)],
        compiler_params=_cparams(("arbitrary",)), name=name)(zz, zz, zz, gq, gk, bf, tri)


def fox_pre_bwd(zz, gq, gk, bf, dqn, dkn, dcum, dzz, *, name):
    s = zz.shape[0]
    t = _row_tile(s)
    nt = s // t
    triu = jnp.asarray(np.triu(np.ones((t, t), np.float32)))

    def body(q_ref, k_ref, f_ref, gq_ref, gk_ref, b_ref, tri_ref, dqn_ref, dkn_ref, dcum_ref, dzz_ref,
             dz_ref, dff_ref, dgq_ref, dgk_ref, db_ref, carry):
        @pl.when(pl.program_id(0) == 0)
        def _():
            carry[...] = jnp.zeros_like(carry)
            dgq_ref[...] = jnp.zeros_like(dgq_ref)
            dgk_ref[...] = jnp.zeros_like(dgk_ref)
            db_ref[...] = jnp.zeros_like(db_ref)

        dcum_v = dcum_ref[...]
        dlf = _dot(tri_ref[...], dcum_v, HI) + carry[...]
        carry[...] += jnp.sum(dcum_v, axis=0, keepdims=True)
        _, vjp = jax.vjp(_fox_pre_f, _heads(q_ref[...]), _heads(k_ref[...]), f_ref[...], gq_ref[...], gk_ref[...], b_ref[...])
        dq, dk, dff, dgq, dgk, db = vjp((_heads(dqn_ref[...]), _heads(dkn_ref[...]), dlf))
        dz_ref[...] = jnp.concatenate(dq + dk, axis=1).astype(dz_ref.dtype)
        dff_ref[...] = dff.astype(dff_ref.dtype)
        dgq_ref[...] += dgq
        dgk_ref[...] += dgk
        db_ref[...] += db

    r = lambda i: nt - 1 - i
    vec = pl.BlockSpec((1, HD), lambda i: (0, 0))
    return pl.pallas_call(
        body, grid=(nt,),
        in_specs=[pl.BlockSpec((t, BW), lambda i: (r(i), FQ // BW)), pl.BlockSpec((t, BW), lambda i: (r(i), FK // BW)),
                  pl.BlockSpec((t, HD), lambda i: (r(i), FF // HD)), vec, vec, vec, pl.BlockSpec((t, t), lambda i: (0, 0)),
                  pl.BlockSpec((t, BW), lambda i: (r(i), 0)), pl.BlockSpec((t, BW), lambda i: (r(i), 0)),
                  pl.BlockSpec((t, HD), lambda i: (r(i), 0)), pl.BlockSpec(memory_space=pl.ANY)],
        out_specs=[pl.BlockSpec((t, 2 * BW), lambda i: (r(i), FQ // (2 * BW))), pl.BlockSpec((t, HD), lambda i: (r(i), 0)),
                   vec, vec, vec],
        out_shape=[_sds(dzz.shape, dzz.dtype), _sds((s, HD), MMT), _sds((1, HD), F32), _sds((1, HD), F32), _sds((1, HD), F32)],
        input_output_aliases={10: 0}, scratch_shapes=[pltpu.VMEM((1, HD), F32)],
        compiler_params=_cparams(("arbitrary",)), name=name)(zz, zz, zz, gq, gk, bf, triu, dqn, dkn, dcum, dzz)


def _fox_blocks(s):
    return min(256, s), min(512, s)


NEG = -1e30


def fox_attn_fwd(qn, kn, zz, cum_col, cum_row, *, name):
    s = qn.shape[0]
    bq, bk = _fox_blocks(s)

    def body(q_ref, k_ref, v_ref, cc_ref, cr_ref, o_ref, lse_ref):
        qi = pl.program_id(1)
        q = q_ref[...]
        cq = cc_ref[...]
        rows = qi * bq + lax.broadcasted_iota(jnp.int32, (bq, bk), 0)
        cols0 = lax.broadcasted_iota(jnp.int32, (bq, bk), 1)

        def step(j, carry):
            m, l, acc = carry
            off = pl.multiple_of(j * bk, bk)
            k = k_ref[pl.ds(off, bk), :]
            v = v_ref[pl.ds(off, bk), :].astype(MMT)
            sc = _dot_nt(q, k) + cq - cr_ref[pl.ds(j, 1), :]
            sc = jnp.where(rows >= cols0 + j * bk, sc, NEG)
            m_new = jnp.maximum(m, jnp.max(sc, axis=1, keepdims=True))
            alpha = jnp.exp(m - m_new)
            p = jnp.exp(sc - m_new)
            return m_new, alpha * l + jnp.sum(p, axis=1, keepdims=True), alpha * acc + _dot(p.astype(MMT), v)

        nk = ((qi + 1) * bq + bk - 1) // bk
        m, l, acc = lax.fori_loop(0, nk, step, (jnp.full((bq, 1), NEG, F32), jnp.zeros((bq, 1), F32),
                                                jnp.zeros((bq, HD), F32)))
        o_ref[...] = acc / l
        lse_ref[...] = m + jnp.log(l)

    return pl.pallas_call(
        body, grid=(NH, s // bq),
        in_specs=[pl.BlockSpec((bq, HD), lambda h, i: (i, h)), pl.BlockSpec((s, HD), lambda h, i: (0, h)),
                  pl.BlockSpec((s, HD), lambda h, i: (0, FV // HD + h)),
                  pl.BlockSpec((None, bq, 1), lambda h, i: (h, i, 0)), pl.BlockSpec((None, s // bk, bk), lambda h, i: (h, 0, 0))],
        out_specs=[pl.BlockSpec((bq, HD), lambda h, i: (i, h)), pl.BlockSpec((None, bq, 1), lambda h, i: (h, i, 0))],
        out_shape=[_sds((s, BW), F32), _sds((NH, s, 1), F32)],
        compiler_params=_cparams(("parallel", "parallel")), name=name)(qn, kn, zz, cum_col, cum_row)


def fox_attn_bwd(qn, kn, zz, cum_col, cum_row, lse, do, dzz, *, name):
    s = qn.shape[0]
    bq, bk = _fox_blocks(s)
    nkc = s // bk

    def body(q_ref, k_ref, v_ref, cc_ref, cr_ref, lse_ref, do_ref, dzz_ref, dq_ref, dk_ref, dv_ref, dc_ref,
             p_scr, dp_scr, dv_scr):
        qi = pl.program_id(1)

        @pl.when(qi == 0)
        def _():
            dk_ref[...] = jnp.zeros_like(dk_ref)
            dv_scr[...] = jnp.zeros_like(dv_scr)
            dc_ref[...] = jnp.zeros_like(dc_ref)

        q = q_ref[...]
        dob = do_ref[...].astype(MMT)
        cq = cc_ref[...]
        lse_v = lse_ref[...]
        rows = qi * bq + lax.broadcasted_iota(jnp.int32, (bq, bk), 0)
        cols0 = lax.broadcasted_iota(jnp.int32, (bq, bk), 1)
        nk = ((qi + 1) * bq + bk - 1) // bk

        def probs(j, delta):
            off = pl.multiple_of(j * bk, bk)
            sc = _dot_nt(q, k_ref[pl.ds(off, bk), :]) + cq - cr_ref[pl.ds(j, 1), :]
            p = jnp.where(rows >= cols0 + j * bk, jnp.exp(sc - lse_v), 0.0)
            dp = _dot_nt(dob, v_ref[pl.ds(off, bk), :].astype(MMT))
            p_scr[j] = p
            dp_scr[j] = dp
            return delta + jnp.sum(p * dp, axis=1, keepdims=True)

        delta = lax.fori_loop(0, nk, probs, jnp.zeros((bq, 1), F32))

        def grads(j, dq):
            off = pl.multiple_of(j * bk, bk)
            p = p_scr[j]
            ds = p * (dp_scr[j] - delta)
            dsm = ds.astype(MMT)
            dv_scr[pl.ds(off, bk), :] += _dot_tn(p.astype(MMT), dob)
            dk_ref[pl.ds(off, bk), :] += _dot_tn(dsm, q)
            dc_ref[pl.ds(j, 1), :] -= jnp.sum(ds, axis=0, keepdims=True)
            return dq + _dot(dsm, k_ref[pl.ds(off, bk), :])

        dq_ref[...] = lax.fori_loop(0, nk, grads, jnp.zeros((bq, HD), F32))

        @pl.when(qi == pl.num_programs(1) - 1)
        def _():
            dv_ref[...] = dv_scr[...].astype(dv_ref.dtype)

    full = lambda c0=0: pl.BlockSpec((s, HD), lambda h, i: (0, c0 + h))
    blk = lambda: pl.BlockSpec((bq, HD), lambda h, i: (i, h))
    colv = lambda: pl.BlockSpec((None, bq, 1), lambda h, i: (h, i, 0))
    rowv = lambda: pl.BlockSpec((None, nkc, bk), lambda h, i: (h, 0, 0))
    return pl.pallas_call(
        body, grid=(NH, s // bq),
        in_specs=[blk(), full(), full(FV // HD), colv(), rowv(), colv(), blk(), pl.BlockSpec(memory_space=pl.ANY)],
        out_specs=[blk(), full(), full(FV // HD), rowv()],
        out_shape=[_sds((s, BW), F32), _sds((s, BW), F32), _sds(dzz.shape, dzz.dtype), _sds((NH, nkc, bk), F32)],
        input_output_aliases={7: 2},
        scratch_shapes=[pltpu.VMEM((nkc, bq, bk), F32), pltpu.VMEM((nkc, bq, bk), F32), pltpu.VMEM((s, HD), F32)],
        compiler_params=_cparams(("parallel", "arbitrary")), name=name)(qn, kn, zz, cum_col, cum_row, lse, do, dzz)


def _branch_f(rets, rgs, glas, ggs, ret_g, gla_g):
    out_r, out_g = [], []
    for h in range(NH):
        xc = rets[h] - jnp.mean(rets[h], axis=-1, keepdims=True)
        y = xc * lax.rsqrt(jnp.mean(xc * xc, axis=-1, keepdims=True) + EPS) * ret_g[h]
        out_r.append(_silu(rgs[h]) * y)
        x = glas[h]
        y = x * lax.rsqrt(jnp.mean(x * x, axis=-1, keepdims=True) + EPS) * gla_g
        out_g.append(_silu(ggs[h]) * y)
    return out_r, out_g


def _w_br_spec(layer):
    return pl.BlockSpec((None, 3, BW, D), lambda i: (layer, 0, 0, 0))


def mix_fwd(ret_raw, gla_raw, fox_o, zz, ret_g, gla_g, b_mg, w_br, *, name, layer):
    s = zz.shape[0]
    t = _row_tile(s)

    def body(r_ref, g_ref, f_ref, rg_ref, gg_ref, gp_ref, rgn_ref, ggn_ref, bmg_ref, w_ref, o_ref):
        rgn = rgn_ref[...]
        br_r, br_g = _branch_f(_heads(r_ref[...]), _heads(rg_ref[...]), _heads(g_ref[...]), _heads(gg_ref[...]),
                               _heads(rgn), ggn_ref[...])
        brs = [jnp.concatenate(br_r, axis=1), jnp.concatenate(br_g, axis=1), f_ref[...]]
        acc = jnp.zeros((t, D), F32)
        for b in range(3):
            gate = jax.nn.sigmoid(gp_ref[:, b * D:(b + 1) * D] + bmg_ref[:, b * D:(b + 1) * D])
            acc = acc + gate * _dot(brs[b].astype(MMT), w_ref[b])
        o_ref[...] = acc.astype(o_ref.dtype)

    row = lambda w, c=0: pl.BlockSpec((t, w), lambda i: (i, c // w))
    cst = lambda shp: pl.BlockSpec(shp, lambda i: (0,) * len(shp))
    return pl.pallas_call(
        body, grid=(s // t,),
        in_specs=[row(BW), row(BW), row(BW), row(BW, RG), row(BW, GG), row(3 * D, GP), cst((1, BW)), cst((1, HD)),
                  cst((1, 3 * D)), _w_br_spec(layer)],
        out_specs=row(D), out_shape=_sds((s, D), MMT),
        compiler_params=_cparams(("parallel",)), name=name)(ret_raw, gla_raw, fox_o, zz, zz, zz, ret_g, gla_g, b_mg, w_br)


def mix_bwd(ret_raw, gla_raw, fox_o, zz, ret_g, gla_g, b_mg, w_br, dmi, *, name, layer):
    s = zz.shape[0]
    t = _row_tile(s)

    def body(r_ref, g_ref, f_ref, rg_ref, gg_ref, gp_ref, rgn_ref, ggn_ref, bmg_ref, w_ref, dmi_ref,
             dr_ref, dg_ref, df_ref, dgp_ref, dw_ref, drgn_ref, dggn_ref, dbmg_ref):
        @pl.when(pl.program_id(0) == 0)
        def _():
            dw_ref[...] = jnp.zeros_like(dw_ref)
            drgn_ref[...] = jnp.zeros_like(drgn_ref)
            dggn_ref[...] = jnp.zeros_like(dggn_ref)
            dbmg_ref[...] = jnp.zeros_like(dbmg_ref)

        (br_r, br_g), vjp = jax.vjp(_branch_f, _heads(r_ref[...]), _heads(rg_ref[...]), _heads(g_ref[...]),
                                    _heads(gg_ref[...]), _heads(rgn_ref[...]), ggn_ref[...])
        brs = [jnp.concatenate(br_r, axis=1).astype(MMT), jnp.concatenate(br_g, axis=1).astype(MMT),
               f_ref[...].astype(MMT)]
        dmi_v = dmi_ref[...].astype(F32)
        dbr = []
        for b in range(3):
            w = w_ref[b]
            ybr = _dot(brs[b], w)
            gate = jax.nn.sigmoid(gp_ref[:, b * D:(b + 1) * D] + bmg_ref[:, b * D:(b + 1) * D])
            dgp = dmi_v * ybr * gate * (1.0 - gate)
            dgp_ref[:, b * D:(b + 1) * D] = dgp.astype(dgp_ref.dtype)
            dbmg_ref[:, b * D:(b + 1) * D] += jnp.sum(dgp, axis=0, keepdims=True)
            dy = (dmi_v * gate).astype(MMT)
            dw_ref[b] += _dot_tn(brs[b], dy)
            dbr.append(_dot_nt(dy, w))
        dr, drg, dg, dgg, drgn, dggn = vjp((_heads(dbr[0]), _heads(dbr[1])))
        dr_ref[...] = jnp.concatenate(dr, axis=1)
        dg_ref[...] = jnp.concatenate(dg, axis=1)
        df_ref[...] = dbr[2]
        dgp_ref[:, RG:RG + BW] = jnp.concatenate(drg, axis=1).astype(dgp_ref.dtype)
        dgp_ref[:, GG:GG + BW] = jnp.concatenate(dgg, axis=1).astype(dgp_ref.dtype)
        drgn_ref[...] += jnp.concatenate(drgn, axis=1)
        dggn_ref[...] += dggn

    row = lambda w, c=0: pl.BlockSpec((t, w), lambda i: (i, c // w))
    cst = lambda shp: pl.BlockSpec(shp, lambda i: (0,) * len(shp))
    return pl.pallas_call(
        body, grid=(s // t,),
        in_specs=[row(BW), row(BW), row(BW), row(BW, RG), row(BW, GG), row(3 * D, GP), cst((1, BW)), cst((1, HD)),
                  cst((1, 3 * D)), _w_br_spec(layer), row(D)],
        out_specs=[row(BW), row(BW), row(BW), row(FV), cst((3, BW, D)), cst((1, BW)), cst((1, HD)), cst((1, 3 * D))],
        out_shape=[_sds((s, BW), F32)] * 3 + [_sds((s, NZZ), MMT), _sds((3, BW, D), F32), _sds((1, BW), F32),
                                              _sds((1, HD), F32), _sds((1, 3 * D), F32)],
        compiler_params=_cparams(("arbitrary",)), name=name)(ret_raw, gla_raw, fox_o, zz, zz, zz, ret_g, gla_g, b_mg, w_br, dmi)


CT = 256


def _shift_down(x, k, rows):
    return jnp.where(rows >= k, pltpu.roll(x, k, 0), 0.0)


def _shift_up(x, k, rows, s):
    return jnp.where(rows < s - k, pltpu.roll(x, s - k, 0), 0.0)


def conv_fwd(ug, w_conv, b_conv, *, name):
    s = ug.shape[0]
    nt = DFF // CT

    def body(u_ref, g_ref, w_ref, b_ref, a_ref):
        u = u_ref[...]
        rows = lax.broadcasted_iota(jnp.int32, u.shape, 0)
        uc = b_ref[...] + w_ref[0:1, :] * _shift_down(u, 2, rows) + w_ref[1:2, :] * _shift_down(u, 1, rows) + w_ref[2:3, :] * u
        a_ref[...] = (_silu(uc) * g_ref[...]).astype(a_ref.dtype)

    return pl.pallas_call(
        body, grid=(nt,),
        in_specs=[pl.BlockSpec((s, CT), lambda j: (0, j)), pl.BlockSpec((s, CT), lambda j: (0, nt + j)),
                  pl.BlockSpec((3, CT), lambda j: (0, j)), pl.BlockSpec((1, CT), lambda j: (0, j))],
        out_specs=pl.BlockSpec((s, CT), lambda j: (0, j)), out_shape=_sds((s, DFF), MMT),
        compiler_params=_cparams(("parallel",)), name=name)(ug, ug, w_conv, b_conv)


def conv_bwd(ug, w_conv, b_conv, da, *, name):
    s = ug.shape[0]
    nt = DFF // CT

    def body(u_ref, g_ref, w_ref, b_ref, da_ref, du_ref, dg_ref, dw_ref, db_ref):
        u = u_ref[...]
        rows = lax.broadcasted_iota(jnp.int32, u.shape, 0)
        u2, u1 = _shift_down(u, 2, rows), _shift_down(u, 1, rows)
        uc = b_ref[...] + w_ref[0:1, :] * u2 + w_ref[1:2, :] * u1 + w_ref[2:3, :] * u
        sg = jax.nn.sigmoid(uc)
        da_v = da_ref[...]
        dg_ref[...] = (da_v * uc * sg).astype(dg_ref.dtype)
        duc = da_v * g_ref[...] * sg * (1.0 + uc * (1.0 - sg))
        du = w_ref[2:3, :] * duc + w_ref[1:2, :] * _shift_up(duc, 1, rows, s) + w_ref[0:1, :] * _shift_up(duc, 2, rows, s)
        du_ref[...] = du.astype(du_ref.dtype)
        dw_ref[0:1, :] = jnp.sum(duc * u2, axis=0, keepdims=True)
        dw_ref[1:2, :] = jnp.sum(duc * u1, axis=0, keepdims=True)
        dw_ref[2:3, :] = jnp.sum(duc * u, axis=0, keepdims=True)
        db_ref[...] = jnp.sum(duc, axis=0, keepdims=True)

    col = lambda: pl.BlockSpec((s, CT), lambda j: (0, j))
    return pl.pallas_call(
        body, grid=(nt,),
        in_specs=[col(), pl.BlockSpec((s, CT), lambda j: (0, nt + j)), pl.BlockSpec((3, CT), lambda j: (0, j)),
                  pl.BlockSpec((1, CT), lambda j: (0, j)), col()],
        out_specs=[col(), col(), pl.BlockSpec((3, CT), lambda j: (0, j)), pl.BlockSpec((1, CT), lambda j: (0, j))],
        out_shape=[_sds((s, DFF), MMT), _sds((s, DFF), MMT), _sds((3, DFF), F32), _sds((1, DFF), F32)],
        compiler_params=_cparams(("parallel",)), name=name)(ug, ug, w_conv, b_conv, da)


def place_tail(dzz, dlr, dff, *, name):
    s = dzz.shape[0]
    t = _row_tile(s)

    def body(a_ref, b_ref, z_ref, o_ref):
        o_ref[...] = jnp.concatenate([a_ref[...], b_ref[...]], axis=1)

    spec = pl.BlockSpec((t, HD), lambda i: (i, 0))
    return pl.pallas_call(
        body, grid=(s // t,), in_specs=[spec, spec, pl.BlockSpec(memory_space=pl.ANY)],
        out_specs=pl.BlockSpec((t, 2 * HD), lambda i: (i, LR // (2 * HD))), out_shape=_sds(dzz.shape, dzz.dtype),
        input_output_aliases={2: 0}, compiler_params=_cparams(("parallel",)), name=name)(dlr, dff, dzz)


def _tiles(s):
    return min(1024, s)


def layer_fwd(x, mod, p, cosf, sinf):
    s = x.shape[0]
    tm = _tiles(s)
    l = p["l"]
    shift1, scale1, gate1, shift2, scale2, gate2 = mod
    h = norm_mod(x, p["norm1_g"], scale1, shift1, name="norm_mod")
    zz = mm_nn(h, p["w1"], tm=tm, tn=768, out_dtype=F32, name="mm_w1", layer=l)
    ret_raw, rprev = retention_fwd(zz, cosf, sinf, name="ret_fwd")
    gla_raw, sprev = gla_fwd(zz, p["w_a2p"], p["b_gla_a"], name="gla_fwd")
    qn, kn, cum = fox_pre(zz, p["q_norm_g"], p["k_norm_g"], p["b_foxp"], name="fox_pre")
    bq, bk = _fox_blocks(s)
    cum_t = cum[:, :NH].T
    cum_col, cum_row = cum_t[:, :, None], cum_t.reshape(NH, s // bk, bk)
    fox_o, lse = fox_attn_fwd(qn, kn, zz, cum_col, cum_row, name="fox_fwd")
    mi = mix_fwd(ret_raw, gla_raw, fox_o, zz, p["ret_norm_g"], p["gla_norm_g"], p["b_mg"], p["w_br"], name="mix_fwd",
                 layer=l)
    x1, mixed = mm_nn_residual(mi, p["w_o"], x, gate1, tm=tm, tn=512, name="mm_wo", layer=l)
    h2 = norm_mod(x1, p["norm2_g"], scale2, shift2, name="norm_mod")
    ug = mm_nn(h2, p["w_up"], tm=tm, tn=512, out_dtype=F32, name="mm_wup", layer=l)
    a = conv_fwd(ug, p["w_conv"], p["b_conv"], name="conv_fwd")
    x2, y = mm_nn_residual(a, p["w_down"], x1, gate2, tm=tm, tn=512, name="mm_wdown", layer=l)
    saved = dict(x=x, h=h, zz=zz, ret_raw=ret_raw, rprev=rprev, gla_raw=gla_raw, sprev=sprev, qn=qn, kn=kn,
                 cum_col=cum_col, cum_row=cum_row, fox_o=fox_o, lse=lse, mi=mi, mixed=mixed, x1=x1, h2=h2, ug=ug, a=a, y=y)
    return x2, saved


def layer_bwd(dx2, mod, p, sv, cosf, sinf, stacks, slot):
    s = dx2.shape[0]
    tm = _tiles(s)
    l = p["l"]
    shift1, scale1, gate1, shift2, scale2, gate2 = mod
    g, stacks = {}, dict(stacks)
    dy, dgate2 = gate_bwd(dx2, sv["y"], gate2, name="gate_bwd")
    stacks["w_down"] = mm_tn(sv["a"], dy, tm=min(1408, DFF), tn=512, out_dtype=MMT, name="mm_dwdown",
                             stack=stacks["w_down"], layer=slot)
    da = mm_nt(dy, p["w_down"], tm=tm, tn=1408, out_dtype=F32, name="mm_da", layer=l)
    du, dg, g["w_conv"], g["b_conv"] = conv_bwd(sv["ug"], p["w_conv"], p["b_conv"], da, name="conv_bwd")
    dug = jnp.concatenate([du, dg], axis=1)
    stacks["w_up"] = mm_tn(sv["h2"], dug, tm=512, tn=512, out_dtype=MMT, name="mm_dwup", stack=stacks["w_up"], layer=slot)
    dh2 = mm_nt(dug, p["w_up"], tm=min(512, s), tn=512, out_dtype=F32, name="mm_dh2", layer=l)
    dx1, g["norm2_g"], dscale2, dshift2 = norm_mod_bwd(sv["x1"], dh2, dx2, p["norm2_g"], scale2, shift2, name="norm_mod_bwd")
    dmixed, dgate1 = gate_bwd(dx1, sv["mixed"], gate1, name="gate_bwd")
    stacks["w_o"] = mm_tn(sv["mi"], dmixed, tm=512, tn=512, out_dtype=MMT, name="mm_dwo", stack=stacks["w_o"], layer=slot)
    dmi = mm_nt(dmixed, p["w_o"], tm=tm, tn=512, out_dtype=MMT, name="mm_dmi", layer=l)
    zz = sv["zz"]
    (dret, dgla, dfox, dzz, g["w_br"], g["ret_norm_g"], g["gla_norm_g"], g["b_mg"]) = mix_bwd(
        sv["ret_raw"], sv["gla_raw"], sv["fox_o"], zz, p["ret_norm_g"], p["gla_norm_g"], p["b_mg"], p["w_br"], dmi,
        name="mix_bwd", layer=l)
    dqn, dkn, dzz, dcum_row = fox_attn_bwd(sv["qn"], sv["kn"], zz, sv["cum_col"], sv["cum_row"], sv["lse"], dfox, dzz,
                                           name="fox_bwd")
    dcum = jnp.pad(dcum_row.reshape(NH, s).T, ((0, 0), (0, HD - NH)))
    dzz, dff, g["q_norm_g"], g["k_norm_g"], g["b_foxp"] = fox_pre_bwd(
        zz, p["q_norm_g"], p["k_norm_g"], p["b_foxp"], dqn, dkn, dcum, dzz, name="fox_pre_bwd")
    dzz, dlr, g["w_a2p"], g["b_gla_a"] = gla_bwd(zz, p["w_a2p"], p["b_gla_a"], sv["sprev"], dgla, dzz, name="gla_bwd")
    dzz = retention_bwd(zz, cosf, sinf, sv["rprev"], dret, dzz, name="ret_bwd")
    dzz = place_tail(dzz, dlr, dff, name="place_tail")
    stacks["w_mg"] = mm_tn(sv["h"], dzz, tm=512, tn=768, out_dtype=MMT, name="mm_dwmg", ncols=WZ0, stack=stacks["w_mg"],
                           layer=slot)
    dwz = mm_tn(sv["h"], dzz, tm=512, tn=768, out_dtype=MMT, name="mm_dwz", col0=WZ0)
    stacks["w_in"] = unalign_dw_in(dwz, stacks["w_in"], slot)
    dh = mm_nt(dzz, p["w1"], tm=min(512, s), tn=512, out_dtype=F32, name="mm_dh", layer=l)
    dx, g["norm1_g"], dscale1, dshift1 = norm_mod_bwd(sv["x"], dh, dx1, p["norm1_g"], scale1, shift1, name="norm_mod_bwd")
    dmod = jnp.concatenate([dshift1, dscale1, dgate1, dshift2, dscale2, dgate2], axis=1)
    return dx, g, dmod, stacks


def _align_cols(w_in, w_mg):
    z = lambda n: jnp.zeros((w_in.shape[0], n), w_in.dtype)
    seg = lambda name: w_in[:, W_IN_COLS[name][0]:W_IN_COLS[name][1]]
    return jnp.concatenate([w_mg, seg("rg"), seg("gg"), seg("fv"), seg("rqkv"), seg("gqkv"), seg("fqk"), seg("lr"),
                            z(HD - GLR), seg("ff"), z(HD - NH)], axis=1)


def _unalign_cols(dwz):
    seg = lambda c0, name: dwz[:, c0 - WZ0:c0 - WZ0 + W_IN_COLS[name][1] - W_IN_COLS[name][0]]
    return jnp.concatenate([seg(RQ, "rqkv"), seg(RG, "rg"), seg(GQ, "gqkv"), seg(LR, "lr"), seg(GG, "gg"), seg(FQ, "fqk"),
                            seg(FV, "fv"), seg(FF, "ff")], axis=1)


def build_w1(w_in_sh, w_mg):
    nl = w_mg.shape[0]
    t = _row_tile(D)

    def body(s_ref, g_ref, o_ref):
        o_ref[...] = _align_cols(jnp.concatenate([s_ref[k] for k in range(4)], axis=1), g_ref[...])

    return pl.pallas_call(
        body, grid=(nl, D // t),
        in_specs=[pl.BlockSpec((None, 4, t, IN_W // 4), lambda l, i: (l, 0, i, 0)), pl.BlockSpec((None, t, WZ0), lambda l, i: (l, i, 0))],
        out_specs=pl.BlockSpec((None, t, NZZ), lambda l, i: (l, i, 0)), out_shape=_sds((nl, D, NZZ), w_mg.dtype),
        compiler_params=_cparams(("parallel", "parallel")), name="build_w1")(w_in_sh, w_mg)


def unalign_dw_in(dwz, stack, layer):
    t = _row_tile(D)

    def body(z_ref, s_ref, o_ref):
        w = _unalign_cols(z_ref[...])
        for k in range(4):
            o_ref[k] = w[:, k * (IN_W // 4):(k + 1) * (IN_W // 4)]

    return pl.pallas_call(
        body, grid=(D // t,),
        in_specs=[pl.BlockSpec((t, NZZ - WZ0), lambda i: (i, 0)), pl.BlockSpec(memory_space=pl.ANY)],
        out_specs=pl.BlockSpec((None, 4, t, IN_W // 4), lambda i: (layer, 0, i, 0)), out_shape=_sds(stack.shape, stack.dtype),
        input_output_aliases={1: 0}, compiler_params=_cparams(("parallel",)), name="unalign_dw_in")(dwz, stack)


def layer_params(w, big, l):
    row = lambda v: v[l][None, :]
    return dict(
        l=0, norm1_g=row(w["norm1_g"]), norm2_g=row(w["norm2_g"]), w1=big["w1"],
        w_a2p=jnp.pad(w["w_gla_a2"][l], ((0, HD - GLR), (0, 0))), b_gla_a=row(w["b_gla_a"]),
        b_foxp=jnp.pad(row(w["b_fox_f"]), ((0, 0), (0, HD - NH))), ret_norm_g=row(w["ret_norm_g"]),
        gla_norm_g=row(w["gla_norm_g"]), q_norm_g=row(w["q_norm_g"]), k_norm_g=row(w["k_norm_g"]),
        w_br=big["w_br"], b_mg=row(w["b_mg"]), w_o=big["w_o"], w_up=big["w_up"], w_conv=w["w_conv"][l],
        b_conv=row(w["b_conv"]), w_down=big["w_down"])


def layer_grads(g):
    vec = lambda v: v[0]
    return dict(
        norm1_g=vec(g["norm1_g"]), norm2_g=vec(g["norm2_g"]), w_gla_a2=g["w_a2p"][:GLR], b_gla_a=vec(g["b_gla_a"]),
        b_fox_f=g["b_foxp"][0, :NH], ret_norm_g=vec(g["ret_norm_g"]), gla_norm_g=vec(g["gla_norm_g"]),
        q_norm_g=vec(g["q_norm_g"]), k_norm_g=vec(g["k_norm_g"]), w_br=g["w_br"], b_mg=vec(g["b_mg"]),
        w_conv=g["w_conv"], b_conv=vec(g["b_conv"]))


def ada_mod(c_all, w_ada, b_ada):
    nl, _, n = w_ada.shape

    def body(c_ref, w_ref, b_ref, o_ref):
        o_ref[...] = _dot(_silu(c_ref[...]), w_ref[...], HI) + b_ref[...]

    return pl.pallas_call(
        body, grid=(nl,),
        in_specs=[pl.BlockSpec((8, D), lambda l: (0, 0)), pl.BlockSpec((None, D, n), lambda l: (l, 0, 0)),
                  pl.BlockSpec((None, 1, n), lambda l: (l, 0, 0))],
        out_specs=pl.BlockSpec((None, 8, n), lambda l: (l, 0, 0)), out_shape=_sds((nl, 8, n), F32),
        compiler_params=_cparams(("parallel",)), name="ada_mod")(c_all, w_ada, b_ada)


def ada_dw(c_all, dmod):
    nl, _, n = dmod.shape

    def body(c_ref, d_ref, o_ref):
        o_ref[...] = _dot_tn(_silu(c_ref[...]), d_ref[...], HI)

    return pl.pallas_call(
        body, grid=(nl,),
        in_specs=[pl.BlockSpec((8, D), lambda l: (0, 0)), pl.BlockSpec((None, 8, n), lambda l: (l, 0, 0))],
        out_specs=pl.BlockSpec((None, D, n), lambda l: (l, 0, 0)), out_shape=_sds((nl, D, n), F32),
        compiler_params=_cparams(("parallel",)), name="ada_dw")(c_all, dmod)


def sum_devices(g):
    def body(g_ref, o_ref):
        acc = g_ref[0]
        for d in range(1, 8):
            acc = acc + g_ref[d]
        o_ref[...] = acc

    return pl.pallas_call(body, out_shape=_sds(g.shape[1:], F32), name="sum_devices")(g)


def adamw(w, g, m, v, *, block, name):
    nd = w.ndim
    grid = tuple(w.shape[i] // block[i] for i in range(nd))
    bc1 = 1.0 - ADAM_B1 ** ADAM_STEP
    bc2 = 1.0 - ADAM_B2 ** ADAM_STEP

    def body(w_ref, g_ref, m_ref, v_ref, d_ref, nm_ref, nv_ref):
        gv = g_ref[...]
        nm = ADAM_B1 * m_ref[...] + (1.0 - ADAM_B1) * gv
        nv = ADAM_B2 * v_ref[...] + (1.0 - ADAM_B2) * (gv * gv)
        nm_ref[...] = nm
        nv_ref[...] = nv
        d_ref[...] = -ADAM_LR * ((nm / bc1) / (jnp.sqrt(nv / bc2) + ADAM_EPS) + ADAM_WD * w_ref[...])

    spec = pl.BlockSpec(tuple(block), lambda *i: i)
    return pl.pallas_call(
        body, grid=grid, in_specs=[spec] * 4, out_specs=[spec] * 3, out_shape=[_sds(w.shape, F32)] * 3,
        compiler_params=_cparams(("parallel",) * nd), name=name)(w, g, m, v)


MESH = pl.DeviceIdType.MESH
ANY = pl.BlockSpec(memory_space=pl.ANY)
VM = pl.BlockSpec(memory_space=pltpu.VMEM)


def _place():
    x, y, c = lax.axis_index("x"), lax.axis_index("y"), lax.axis_index("c")
    return x, y, c, [(1 - x, y), (x, 1 - y), (1 - x, 1 - y)]


def small_allgather(v, *, name):
    m_per, n = v.shape

    def body(x_ref, out_ref, send_sems, recv_sems, local_sem):
        x, y, c, chips = _place()
        me, sibling = (x, y, c), (x, y, 1 - c)

        def rows(px, py, pc):
            return out_ref.at[pl.ds((4 * px + 2 * py + pc) * m_per, m_per), :]

        def copy(k, block, to, src=None):
            return pltpu.make_async_remote_copy(
                src_ref=rows(*block) if src is None else src, dst_ref=rows(*block),
                send_sem=send_sems.at[k], recv_sem=recv_sems.at[k], device_id=to, device_id_type=MESH)

        mine = pltpu.make_async_copy(x_ref, rows(*me), local_sem)
        mine.start()
        first = [copy(0, me, sibling, src=x_ref)]
        first += [copy(1 + j, me, (*chip, c), src=x_ref) for j, chip in enumerate(chips)]
        for cp in first:
            cp.start()
        passed = [copy(4 + j, (*chip, c), sibling) for j, chip in enumerate(chips)]
        for j, chip in enumerate(chips):
            copy(1 + j, (*chip, c), me).wait_recv()
            passed[j].start()
        copy(0, sibling, me).wait_recv()
        for j, chip in enumerate(chips):
            copy(4 + j, (*chip, 1 - c), me).wait_recv()
        for cp in first + passed:
            cp.wait_send()
        mine.wait()

    return pl.pallas_call(
        body, out_shape=_sds((8 * m_per, n), v.dtype), in_specs=[VM], out_specs=VM,
        scratch_shapes=[pltpu.SemaphoreType.DMA((7,)), pltpu.SemaphoreType.DMA((7,)), pltpu.SemaphoreType.DMA],
        name=name)(v)


TENSORS = {
    "w_in": ("lead", None, (4, D, 1285), (1, D, 1285)),
    "w_mg": ("col", 768, (D, 3072), (512, 3072)),
    "w_br": ("col", 256, (3, BW, D), (3, BW, D)),
    "w_o": ("row", 256, (D, D), (D, D)),
    "w_up": ("col", 1408, (D, 5632), (256, 5632)),
    "w_down": ("row", 704, (DFF, D), (704, D)),
}
BIG = tuple(TENSORS)


def _shard_shape(name):
    kind, width, full, _ = TENSORS[name]
    if kind == "lead":
        return full[1:]
    return full[:-1] + (width,) if kind == "col" else (width,) + full[1:]


def _shard_view(ref, layers, name, k):
    kind, width, full, _ = TENSORS[name]
    if kind == "lead":
        return ref.at[layers, k]
    if kind == "row":
        return ref.at[layers, pl.ds(k * width, width)]
    return ref.at[(layers,) + (slice(None),) * (len(full) - 1) + (pl.ds(k * width, width),)]


def _remote(send_sems, recv_sems, k, src, dst, to):
    return pltpu.make_async_remote_copy(src_ref=src, dst_ref=dst, send_sem=send_sems.at[k], recv_sem=recv_sems.at[k],
                                        device_id=to, device_id_type=MESH)


def _dma_sems(n):
    return [pltpu.SemaphoreType.DMA((n,)), pltpu.SemaphoreType.DMA((n,))]


RS_GROUP = 2
HBM = pl.BlockSpec(memory_space=pltpu.HBM)
SEM = pl.BlockSpec(memory_space=pltpu.SEMAPHORE)
SPLIT_CALL = dict(compiler_params=pltpu.CompilerParams(has_side_effects=pltpu.SideEffectType.DATAFLOW_SIDE_EFFECTING))
PULL_SET = (("w_up", "w_down", "w_o"), ("w_in", "w_mg", "w_br"))


def _in_hbm(a):
    return pltpu.with_memory_space_constraint(a, pltpu.HBM)


def _pull_sends(send_sems, recv_sems, p, o, layer, core, x, y, chips):
    return [_remote(send_sems, recv_sems, 3 * BIG.index(n) + j, p[n].at[layer], _shard_view(o[n], 0, n, 2 * x + y), (*chip, core))
            for n in PULL_SET[core] for j, chip in enumerate(chips)]


def _pull_arrivals(send_sems, recv_sems, o, core, x, y, chips, to):
    views = [(3 * BIG.index(n) + j, _shard_view(o[n], 0, n, 2 * chip[0] + chip[1]))
             for n in PULL_SET[core] for j, chip in enumerate(chips)]
    return [_remote(send_sems, recv_sems, k, v, v, to) for k, v in views]


def gather_start(shards, layer, *, name):
    nt = len(BIG)

    def body(*refs):
        p, o = dict(zip(BIG, refs[:nt])), dict(zip(BIG, refs[nt:2 * nt]))
        x, y, c, chips = _place()
        for core in (0, 1):
            @pl.when(c == core)
            def _():
                for cp in _pull_sends(refs[2 * nt], refs[2 * nt + 1], p, o, layer, core, x, y, chips):
                    cp.start()
        refs[-1][...] = jnp.zeros_like(refs[-1])

    lands = [_in_hbm(lax.empty((1,) + TENSORS[n][2], shards[n].dtype)) for n in BIG]
    outs = pl.pallas_call(
        body,
        out_shape=(pltpu.SemaphoreType.DMA((3 * nt,)), pltpu.SemaphoreType.DMA((3 * nt,)),
                   *[pltpu.HBM(a.shape, a.dtype) for a in lands], _sds((8, HD), F32)),
        in_specs=[HBM] * (2 * nt), out_specs=(SEM, SEM, *[HBM] * nt, VM),
        input_output_aliases={nt + t: 2 + t for t in range(nt)}, name=name, **SPLIT_CALL)(
            *[_in_hbm(shards[n]) for n in BIG], *lands)
    return outs[0], outs[1], outs[2:2 + nt], outs[-1]


def gather_wait(send_sems, recv_sems, shards, lands, after, layer, *, name):
    nt = len(BIG)

    def body(*refs):
        p, o = dict(zip(BIG, refs[:nt])), dict(zip(BIG, refs[nt:2 * nt]))
        ss, rs = refs[2 * nt], refs[2 * nt + 1]
        x, y, c, chips = _place()
        for core in (0, 1):
            @pl.when(c == core)
            def _():
                for cp in _pull_sends(ss, rs, p, o, layer, core, x, y, chips):
                    cp.wait_send()
                for cp in _pull_arrivals(ss, rs, o, core, x, y, chips, (x, y, core)):
                    cp.wait_recv()

    return pl.pallas_call(
        body, out_shape=tuple(pltpu.HBM(a.shape, a.dtype) for a in lands),
        in_specs=[HBM] * (2 * nt) + [SEM, SEM, ANY], out_specs=tuple([HBM] * nt),
        input_output_aliases={nt + t: t for t in range(nt)}, name=name, **SPLIT_CALL)(
            *[_in_hbm(shards[n]) for n in BIG], *lands, send_sems, recv_sems, after)


def gather_forward(shards, lands, layer, *, name):
    nt = len(BIG)

    def body(*refs):
        p, o = dict(zip(BIG, refs[:nt])), dict(zip(BIG, refs[2 * nt:3 * nt]))
        ss, rs = refs[3 * nt:]
        x, y, c, chips = _place()
        for core in (0, 1):
            @pl.when(c == core)
            def _():
                me, sibling = (x, y, core), (x, y, 1 - core)
                sends = _pull_arrivals(ss, rs, o, core, x, y, chips, sibling)
                sends += [_remote(ss, rs, 3 * nt + t, p[n].at[layer], _shard_view(o[n], 0, n, 2 * x + y), sibling)
                          for t, n in enumerate(BIG)]
                for cp in sends:
                    cp.start()
                for cp in sends:
                    cp.wait_send()
                for cp in _pull_arrivals(ss, rs, o, 1 - core, x, y, chips, me):
                    cp.wait_recv()
                for t, n in enumerate(BIG):
                    own = _shard_view(o[n], 0, n, 2 * x + y)
                    _remote(ss, rs, 3 * nt + t, own, own, me).wait_recv()

    outs = pl.pallas_call(
        body, out_shape=[_sds(a.shape, a.dtype) for a in lands], in_specs=[ANY] * (2 * nt), out_specs=[ANY] * nt,
        input_output_aliases={nt + t: t for t in range(nt)}, scratch_shapes=_dma_sems(4 * nt), name=name)(
            *[shards[n] for n in BIG], *lands)
    return dict(zip(BIG, outs))


def pair_exchange(g, *, name):
    hh = g[BIG[0]].shape[0] // 2
    nt = len(BIG)

    def body(*refs):
        send_sems, recv_sems = refs[2 * nt:]
        x, y, c, _ = _place()
        copies = [_remote(send_sems, recv_sems, t, refs[t].at[pl.ds(hh * (1 - c), hh)], refs[nt + t], (x, y, 1 - c))
                  for t in range(nt)]
        for cp in copies:
            cp.start()
        for cp in copies:
            cp.wait()

    outs = pl.pallas_call(
        body, out_shape=[_sds((hh,) + g[n].shape[1:], g[n].dtype) for n in BIG], in_specs=[ANY] * nt, out_specs=[ANY] * nt,
        scratch_shapes=_dma_sems(nt), name=name)(*[g[n] for n in BIG])
    return dict(zip(BIG, outs))


def _chip_copies(send_sems, recv_sems, s_refs, land_refs, c, chips):
    hl = s_refs[0].shape[0]
    return [_remote(send_sems, recv_sems, 3 * t + j, _shard_view(s_refs[t], pl.ds(0, hl), n, 2 * chip[0] + chip[1]),
                    land_refs[t].at[j], (*chip, c))
            for t, n in enumerate(BIG) for j, chip in enumerate(chips)]


def _landing_shapes(s):
    hl = s[BIG[0]].shape[0]
    return [_sds((3, hl) + _shard_shape(n), s[n].dtype) for n in BIG]


def chip_exchange(s, *, name):
    nt = len(BIG)

    def body(*refs):
        send_sems, recv_sems = refs[2 * nt:]
        x, y, c, chips = _place()
        copies = _chip_copies(send_sems, recv_sems, refs[:nt], refs[nt:2 * nt], c, chips)
        for cp in copies:
            cp.start()
        for cp in copies:
            cp.wait()

    outs = pl.pallas_call(
        body, out_shape=_landing_shapes(s), in_specs=[ANY] * nt, out_specs=[ANY] * nt,
        scratch_shapes=_dma_sems(3 * nt), name=name)(*[s[n] for n in BIG])
    return dict(zip(BIG, outs))


def chip_exchange_start(s, *, name, after=None):
    nt = len(BIG)
    first = [] if after is None else [after]

    def body(*refs):
        o = refs[2 * nt + len(first):]
        x, y, c, chips = _place()
        for cp in _chip_copies(o[0], o[1], refs[:nt], refs[nt:2 * nt], c, chips):
            cp.start()
        refs[-1][...] = jnp.zeros_like(refs[-1])

    lands = [_in_hbm(lax.empty(d.shape, d.dtype)) for d in _landing_shapes(s)]
    srcs = [_in_hbm(s[n]) for n in BIG]
    outs = pl.pallas_call(
        body,
        out_shape=(pltpu.SemaphoreType.DMA((3 * nt,)), pltpu.SemaphoreType.DMA((3 * nt,)),
                   *[pltpu.HBM(a.shape, a.dtype) for a in srcs + lands], _sds((8, HD), F32)),
        in_specs=[HBM] * (2 * nt) + [ANY] * len(first), out_specs=(SEM, SEM, *[HBM] * (2 * nt), VM),
        input_output_aliases={t: 2 + t for t in range(2 * nt)}, name=name, **SPLIT_CALL)(*srcs, *lands, *first)
    return outs[0], outs[1], outs[2:2 + nt], outs[2 + nt:2 + 2 * nt], outs[-1]


def chip_exchange_wait(send_sems, recv_sems, srcs, lands, after, *, name):
    nt = len(BIG)

    def body(*refs):
        x, y, c, chips = _place()
        for cp in _chip_copies(refs[2 * nt], refs[2 * nt + 1], refs[:nt], refs[nt:2 * nt], c, chips):
            cp.wait_send()
            cp.wait_recv()

    outs = pl.pallas_call(
        body, out_shape=tuple(pltpu.HBM(a.shape, a.dtype) for a in list(srcs) + list(lands)),
        in_specs=[HBM] * (2 * nt) + [SEM, SEM, ANY], out_specs=tuple([HBM] * (2 * nt)),
        input_output_aliases={t: t for t in range(2 * nt)},
        compiler_params=pltpu.CompilerParams(has_side_effects=pltpu.SideEffectType.DATAFLOW_SIDE_EFFECTING),
        name=name)(*srcs, *lands, send_sems, recv_sems, after)
    return dict(zip(BIG, outs[:nt])), dict(zip(BIG, outs[nt:]))


def pair_share(f, l0, hh, *, name):
    nt = len(BIG)

    def body(*refs):
        o = refs[nt:2 * nt]
        send_sems, recv_sems = refs[2 * nt:]
        x, y, c, _ = _place()
        mine, theirs = pl.ds(l0 + hh * c, hh), pl.ds(l0 + hh * (1 - c), hh)
        copies = [_remote(send_sems, recv_sems, t, o[t].at[mine], o[t].at[mine], (x, y, 1 - c)) for t in range(nt)]
        for cp in copies:
            cp.start()
        for t, cp in enumerate(copies):
            cp.wait_send()
            _remote(send_sems, recv_sems, t, o[t].at[theirs], o[t].at[theirs], (x, y, c)).wait_recv()

    outs = pl.pallas_call(
        body, out_shape=[_sds(f[n].shape, f[n].dtype) for n in BIG], in_specs=[ANY] * nt, out_specs=[ANY] * nt,
        input_output_aliases={t: t for t in range(nt)}, scratch_shapes=_dma_sems(nt), name=name)(*[f[n] for n in BIG])
    return dict(zip(BIG, outs))


def pair_add(g, r, idx, *, tensor, name):
    _, _, full, blk = TENSORS[tensor]
    hh = r.shape[0]

    def body(idx_ref, g_ref, r_ref, o_ref):
        o_ref[...] = (g_ref[...].astype(F32) + r_ref[...].astype(F32)).astype(o_ref.dtype)

    own = pl.BlockSpec((None,) + blk, lambda *a: (a[0],) + a[1:-1])
    return pl.pallas_call(
        body, out_shape=_sds(r.shape, r.dtype),
        grid_spec=pltpu.PrefetchScalarGridSpec(
            num_scalar_prefetch=1, grid=(hh,) + tuple(f // b for f, b in zip(full, blk)),
            in_specs=[pl.BlockSpec((None,) + blk, lambda *a: (hh * a[-1][0] + a[0],) + a[1:-1]), own], out_specs=own),
        compiler_params=_cparams(("parallel",) * (1 + len(full))), name=name)(idx, g, r)


def chip_add(s, r, idx, totals, l0, *, tensor, name):
    kind, width, full, _ = TENSORS[tensor]
    shard = _shard_shape(tensor)
    hh = s.shape[0]
    zeros = (0,) * len(shard)

    def body(idx_ref, s_ref, r0_ref, r1_ref, r2_ref, t_ref, o_ref):
        o_ref[...] = ((s_ref[...].astype(F32) + r0_ref[...].astype(F32)) + r1_ref[...].astype(F32)) + r2_ref[...].astype(F32)

    if kind == "lead":
        mine = pl.BlockSpec((None, None) + shard, lambda i, ix: (i, ix[1]) + zeros)
    elif kind == "row":
        mine = pl.BlockSpec((None,) + shard, lambda i, ix: (i, ix[1]) + zeros[1:])
    else:
        mine = pl.BlockSpec((None,) + shard, lambda i, ix: (i,) + zeros[1:] + (ix[1],))
    peer = lambda j: pl.BlockSpec((None, None) + shard, lambda i, ix: (j, i) + zeros)
    return pl.pallas_call(
        body, out_shape=_sds(totals.shape, F32),
        grid_spec=pltpu.PrefetchScalarGridSpec(
            num_scalar_prefetch=1, grid=(hh,), in_specs=[mine, peer(0), peer(1), peer(2), pl.BlockSpec(memory_space=pl.ANY)],
            out_specs=pl.BlockSpec((None,) + shard, lambda i, ix: (l0 + hh * ix[0] + i,) + zeros)),
        input_output_aliases={5: 0}, compiler_params=_cparams(("parallel",)), name=name)(idx, s, r, r, r, totals)


def _flat_rows(arrs):
    v = jnp.concatenate([a.reshape(-1) for a in arrs])
    n = -(-v.shape[0] // 1024) * 1024
    return jnp.pad(v, (0, n - v.shape[0])).reshape(n // HD, HD)


def _unflat(buf, shapes):
    v, out, o = buf.reshape(-1), [], 0
    for s in shapes:
        n = int(np.prod(s))
        out.append(v[o:o + n].reshape(s))
        o += n
    return out


WEIGHTS = ("norm1_g", "norm2_g", "w_ada", "b_ada", "w_in", "w_gla_a2", "b_gla_a", "b_fox_f", "ret_norm_g", "gla_norm_g",
           "q_norm_g", "k_norm_g", "w_br", "w_mg", "b_mg", "w_o", "w_up", "w_conv", "b_conv", "w_down")
REPLICATED = ("norm1_g", "norm2_g", "b_gla_a", "b_fox_f", "ret_norm_g", "gla_norm_g", "q_norm_g", "k_norm_g", "b_mg", "b_conv")
ADAM_BLOCKS = dict(w_ada=(1, 256, 1536), w_in=(1, 256, 1285), w_br=(1, 3, BW, 256), w_mg=(1, 512, 768), w_o=(2, 256, D),
                   w_up=(1, 256, 1408), w_down=(1, 352, D))
ALL_AXES = ("x", "y", "c")


def kernel(x, c, norm1_g, norm2_g, w_ada, b_ada, w_in, w_gla_a2, b_gla_a, b_fox_f, ret_norm_g, gla_norm_g, q_norm_g, k_norm_g, w_br, w_mg, b_mg, w_o, w_up, w_conv, b_conv, w_down, loss_target, m_norm1_g, m_norm2_g, m_w_ada, m_b_ada, m_w_in, m_w_gla_a2, m_b_gla_a, m_b_fox_f, m_ret_norm_g, m_gla_norm_g, m_q_norm_g, m_k_norm_g, m_w_br, m_w_mg, m_b_mg, m_w_o, m_w_up, m_w_conv, m_b_conv, m_w_down, v_norm1_g, v_norm2_g, v_w_ada, v_b_ada, v_w_in, v_w_gla_a2, v_b_gla_a, v_b_fox_f, v_ret_norm_g, v_gla_norm_g, v_q_norm_g, v_k_norm_g, v_w_br, v_w_mg, v_b_mg, v_w_o, v_w_up, v_w_conv, v_b_conv, v_w_down):
    w = dict(zip(WEIGHTS, (norm1_g, norm2_g, w_ada, b_ada, w_in, w_gla_a2, b_gla_a, b_fox_f, ret_norm_g, gla_norm_g,
                           q_norm_g, k_norm_g, w_br, w_mg, b_mg, w_o, w_up, w_conv, b_conv, w_down)))
    m = dict(zip(WEIGHTS, (m_norm1_g, m_norm2_g, m_w_ada, m_b_ada, m_w_in, m_w_gla_a2, m_b_gla_a, m_b_fox_f, m_ret_norm_g,
                           m_gla_norm_g, m_q_norm_g, m_k_norm_g, m_w_br, m_w_mg, m_b_mg, m_w_o, m_w_up, m_w_conv, m_b_conv,
                           m_w_down)))
    v = dict(zip(WEIGHTS, (v_norm1_g, v_norm2_g, v_w_ada, v_b_ada, v_w_in, v_w_gla_a2, v_b_gla_a, v_b_fox_f, v_ret_norm_g,
                           v_gla_norm_g, v_q_norm_g, v_k_norm_g, v_w_br, v_w_mg, v_b_mg, v_w_o, v_w_up, v_w_conv, v_b_conv,
                           v_w_down)))
    nl = norm1_g.shape[0]
    seq = x.shape[1]
    xi, yi, ci = lax.axis_index("x"), lax.axis_index("y"), lax.axis_index("c")
    k_me = 2 * xi + yi
    b_me = 4 * xi + 2 * yi + ci
    ada_n = w_ada.shape[2]
    a2_n, conv_n = w_gla_a2.shape[2], w_conv.shape[2]

    shards = {n: w[n].astype(MMT) for n in BIG}
    started = gather_start(shards, 0, name="gather0_start")

    def gather_finish(l, started, after):
        send_sems, recv_sems, lands, _ = started
        lands = gather_wait(send_sems, recv_sems, shards, lands, after, l, name=f"gather{l}_wait")
        big = gather_forward(shards, lands, l, name=f"gather{l}_forward")
        big["w1"] = build_w1(big["w_in"], big["w_mg"])
        return big

    blk = _flat_rows([c, w_gla_a2, w_conv]) + started[-1][0, 0]
    g1 = small_allgather(blk, name="gather_small").reshape(8, blk.shape[0], HD)
    c_all = g1[:, :D // HD].reshape(8, D)
    by_chip = g1[0::2].reshape(4, -1)[:, D:]
    a2_sh, conv_sh = by_chip[:, :nl * GLR * a2_n], by_chip[:, nl * GLR * a2_n:nl * (GLR * a2_n + 3 * conv_n)]
    full_small = dict(
        w_gla_a2=a2_sh.reshape(4, nl, GLR, a2_n).transpose(1, 2, 0, 3).reshape(nl, GLR, 4 * a2_n),
        w_conv=conv_sh.reshape(4, nl, 3, conv_n).transpose(1, 2, 0, 3).reshape(nl, 3, 4 * conv_n))

    b_ada_sh = lax.dynamic_slice_in_dim(b_ada, k_me * ada_n, ada_n, axis=1)[:, None, :]
    mod_sh = ada_mod(c_all, w_ada, b_ada_sh)
    g2 = small_allgather(mod_sh.reshape(nl * 8, ada_n), name="gather_mod").reshape(4, 2, nl, 8, ada_n)[:, 0]
    mod_me = lax.dynamic_index_in_dim(g2, b_me, axis=2, keepdims=False).transpose(1, 0, 2).reshape(nl, 4 * ada_n)

    wsmall = {n: w[n] for n in REPLICATED}
    wsmall.update(full_small)
    mods = [[mod_me[l:l + 1, i * D:(i + 1) * D] for i in range(6)] for l in range(nl)]
    big = gather_finish(0, started, mod_me)

    cosf, sinf = _rope_tables(seq)
    xs, saved, params = x[0], [], []
    for l in range(nl):
        if l + 1 < nl:
            started = gather_start(shards, l + 1, name=f"gather{l + 1}_start")
            xs = xs + started[-1][0, 0]
        params.append(layer_params(wsmall, big, l))
        xs, sv = layer_fwd(xs, mods[l], params[l], cosf, sinf)
        saved.append(sv)
        if l + 1 < nl:
            big = gather_finish(l + 1, started, xs)
    loss_part, dx = loss_and_grad(xs, loss_target[0], name="loss")
    loss = lax.psum(loss_part[0, 0], ALL_AXES)
    grads, dmods = [None] * nl, [None] * nl
    idx = jnp.stack([ci, k_me]).astype(jnp.int32)
    totals = {n: lax.empty((nl,) + _shard_shape(n), F32) for n in BIG}

    def finish_group(group, chip_sum, from_chips, totals):
        totals = {n: chip_add(chip_sum[n], from_chips[n], idx, totals[n], RS_GROUP * group, tensor=n,
                              name=f"rs{group}_chip_add_{n}") for n in BIG}
        return pair_share(totals, RS_GROUP * group, RS_GROUP // 2, name=f"rs{group}_pair_share")

    pending = None
    for group in reversed(range(nl // RS_GROUP)):
        layers = range(RS_GROUP * group, RS_GROUP * (group + 1))
        stacks = {n: lax.empty((RS_GROUP,) + TENSORS[n][2], MMT) for n in BIG if n != "w_br"}
        if pending is not None:
            dx = dx + pending[-1][0, 0]
        for l in reversed(layers):
            dx, g, dmods[l], stacks = layer_bwd(dx, mods[l], params[l], saved[l], cosf, sinf, stacks, l - layers[0])
            grads[l] = layer_grads(g)
        stacks["w_br"] = jnp.stack([grads[l]["w_br"].astype(MMT) for l in layers])
        from_sibling = pair_exchange(stacks, name=f"rs{group}_pair_exchange")
        chip_sum = {n: pair_add(stacks[n], from_sibling[n], idx, tensor=n, name=f"rs{group}_pair_add_{n}") for n in BIG}
        if pending is not None:
            earlier, send_sems, recv_sems, srcs, lands, _ = pending
            sums, from_chips = chip_exchange_wait(send_sems, recv_sems, srcs, lands, dx,
                                                  name=f"rs{earlier}_chip_exchange_wait")
            totals = finish_group(earlier, sums, from_chips, totals)
        if group > 0:
            pending = (group, *chip_exchange_start(chip_sum, name=f"rs{group}_chip_exchange_start"))

    small_names = REPLICATED + ("w_gla_a2", "w_conv")
    small_shapes = [(nl, 6 * D)] + [(nl,) + grads[0][n].shape for n in small_names]
    vec = _flat_rows([jnp.concatenate(dmods, axis=0)] + [jnp.stack([grads[l][n] for l in range(nl)]) for n in small_names])
    gs = small_allgather(vec, name="gather_small_grads")
    pending = (0, *chip_exchange_start(chip_sum, name="rs0_chip_exchange_start", after=gs))
    gs = (gs + pending[-1][0, 0]).reshape(8, vec.shape[0], HD)
    summed = _unflat(sum_devices(gs), small_shapes)
    grad = dict(zip(small_names, summed[1:]))
    grad["b_ada"] = summed[0]
    grad["w_gla_a2"] = lax.dynamic_slice_in_dim(grad["w_gla_a2"], k_me * a2_n, a2_n, axis=2)
    grad["w_conv"] = lax.dynamic_slice_in_dim(grad["w_conv"], k_me * conv_n, conv_n, axis=2)
    dmod_all = gs[:, :nl * 6 * D // HD].reshape(8, nl, 6 * D)
    dmod_sh = lax.dynamic_slice_in_dim(dmod_all, k_me * ada_n, ada_n, axis=2).transpose(1, 0, 2)
    grad["w_ada"] = ada_dw(c_all, dmod_sh)

    delta, new_m, new_v = {}, {}, {}
    delta["w_ada"], new_m["w_ada"], new_v["w_ada"] = adamw(w["w_ada"], grad["w_ada"], m["w_ada"], v["w_ada"],
                                                          block=ADAM_BLOCKS["w_ada"], name="adamw_w_ada")
    rest = [n for n in WEIGHTS if n not in ADAM_BLOCKS]
    shapes = [w[n].shape for n in rest]
    flat = [_flat_rows([t[n] for n in rest]) for t in (w, grad, m, v)]
    outs = adamw(*flat, block=flat[0].shape, name="adamw_small")
    for t, o in zip((delta, new_m, new_v), outs):
        t.update(zip(rest, _unflat(o, shapes)))

    earlier, send_sems, recv_sems, srcs, lands, _ = pending
    sums, from_chips = chip_exchange_wait(send_sems, recv_sems, srcs, lands, outs[0], name=f"rs{earlier}_chip_exchange_wait")
    grad.update(finish_group(earlier, sums, from_chips, totals))
    for n in BIG:
        delta[n], new_m[n], new_v[n] = adamw(w[n], grad[n], m[n], v[n], block=ADAM_BLOCKS[n], name="adamw_" + n)

    return (loss, dx[None], *[grad[n] for n in WEIGHTS], *[delta[n] for n in WEIGHTS], *[new_m[n] for n in WEIGHTS],
            *[new_v[n] for n in WEIGHTS])
```

```python
import functools

import numpy as np
import jax
import jax.numpy as jnp
from jax import lax
from jax.experimental import pallas as pl
from jax.experimental.pallas import tpu as pltpu

F32 = jnp.float32
MMT = jnp.bfloat16
HI = lax.Precision.HIGHEST

D = 1024
DEPTH = 4
NH = 4
HD = 128
BW = NH * HD
CH = 64
GDK = 64
GLR = 16
DFF = 2816
EPS = 1e-6
ROPE_BASE = 10000.0

GP, RG, GG, FV, RQ, RK, RV, GQ, GK, GV, FQ, FK, LR, FF = (
    0, 3072, 3584, 4096, 4608, 5120, 5632, 6144, 6400, 6656, 7168, 7680, 8192, 8320)
NZZ = 8448
WZ0 = 3072
IN_W = 5140
W_IN_COLS = dict(rqkv=(0, 1536), rg=(1536, 2048), gqkv=(2048, 3072), lr=(3072, 3088), gg=(3088, 3600), fqk=(3600, 4624),
                 fv=(4624, 5136), ff=(5136, 5140))

VMEM_LIMIT = 56 * 1024 * 1024

ADAM_LR, ADAM_B1, ADAM_B2, ADAM_EPS, ADAM_WD, ADAM_STEP = 0.001, 0.9, 0.999, 1e-08, 0.01, 10


def _cparams(sem=None):
    return pltpu.CompilerParams(dimension_semantics=sem, vmem_limit_bytes=VMEM_LIMIT)


def _sds(shape, dtype):
    return jax.ShapeDtypeStruct(tuple(shape), dtype)


def _dot(a, b, precision=None):
    return lax.dot_general(a, b, (((1,), (0,)), ((), ())), precision=precision, preferred_element_type=F32)


def _dot_nt(a, b, precision=None):
    return lax.dot_general(a, b, (((1,), (1,)), ((), ())), precision=precision, preferred_element_type=F32)


def _dot_tn(a, b, precision=None):
    return lax.dot_general(a, b, (((0,), (0,)), ((), ())), precision=precision, preferred_element_type=F32)


def _silu(x):
    return x * jax.nn.sigmoid(x)


def _log_sigmoid(x):
    return jnp.minimum(x, 0.0) - jnp.log(1.0 + jnp.exp(jnp.minimum(x, -x)))


@jax.custom_vjp
def _swap_halves(x):
    return pltpu.roll(x, HD // 2, 1)


_swap_halves.defvjp(lambda x: (_swap_halves(x), None), lambda _, g: (_swap_halves(g),))


@jax.custom_vjp
def _bdot(a, b):
    return _dot(a.astype(MMT), b.astype(MMT))


@jax.custom_vjp
def _bdot_nt(a, b):
    return _dot_nt(a.astype(MMT), b.astype(MMT))


@jax.custom_vjp
def _bdot_tn(a, b):
    return _dot_tn(a.astype(MMT), b.astype(MMT))


_bdot.defvjp(lambda a, b: (_bdot(a, b), (a, b)), lambda r, g: (_bdot_nt(g, r[1]), _bdot_tn(r[0], g)))
_bdot_nt.defvjp(lambda a, b: (_bdot_nt(a, b), (a, b)), lambda r, g: (_bdot(g, r[1]), _bdot_tn(g, r[0])))
_bdot_tn.defvjp(lambda a, b: (_bdot_tn(a, b), (a, b)), lambda r, g: (_bdot_nt(r[1], g), _bdot(r[0], g)))


def _stacked(blk, idx, layer):
    if layer is None:
        return pl.BlockSpec(blk, idx)
    return pl.BlockSpec((None,) + blk, lambda i, j: (layer,) + idx(i, j))


def mm_nn(a, b, *, tm, tn, out_dtype, name, layer=None):
    m, k = a.shape
    n = b.shape[-1]

    def body(a_ref, b_ref, o_ref):
        o_ref[...] = _dot(a_ref[...], b_ref[...]).astype(o_ref.dtype)

    return pl.pallas_call(
        body, grid=(m // tm, n // tn),
        in_specs=[pl.BlockSpec((tm, k), lambda i, j: (i, 0)), _stacked((k, tn), lambda i, j: (0, j), layer)],
        out_specs=pl.BlockSpec((tm, tn), lambda i, j: (i, j)),
        out_shape=_sds((m, n), out_dtype), compiler_params=_cparams(("parallel", "parallel")), name=name)(a, b)


def mm_nn_residual(a, b, res, gate, *, tm, tn, name, layer=None):
    m, k = a.shape
    n = b.shape[-1]

    def body(a_ref, b_ref, r_ref, g_ref, x_ref, y_ref):
        acc = _dot(a_ref[...], b_ref[...])
        y_ref[...] = acc
        x_ref[...] = r_ref[...] + g_ref[...] * acc

    return pl.pallas_call(
        body, grid=(m // tm, n // tn),
        in_specs=[pl.BlockSpec((tm, k), lambda i, j: (i, 0)), _stacked((k, tn), lambda i, j: (0, j), layer),
                  pl.BlockSpec((tm, tn), lambda i, j: (i, j)), pl.BlockSpec((1, tn), lambda i, j: (0, j))],
        out_specs=[pl.BlockSpec((tm, tn), lambda i, j: (i, j)), pl.BlockSpec((tm, tn), lambda i, j: (i, j))],
        out_shape=[_sds((m, n), F32), _sds((m, n), F32)],
        compiler_params=_cparams(("parallel", "parallel")), name=name)(a, b, res, gate)


def mm_nt(a, b, *, tm, tn, out_dtype, name, layer=None):
    m, k = a.shape
    n = b.shape[-2]

    def body(a_ref, b_ref, o_ref):
        o_ref[...] = _dot_nt(a_ref[...], b_ref[...]).astype(o_ref.dtype)

    return pl.pallas_call(
        body, grid=(m // tm, n // tn),
        in_specs=[pl.BlockSpec((tm, k), lambda i, j: (i, 0)), _stacked((tn, k), lambda i, j: (j, 0), layer)],
        out_specs=pl.BlockSpec((tm, tn), lambda i, j: (i, j)),
        out_shape=_sds((m, n), out_dtype), compiler_params=_cparams(("parallel", "parallel")), name=name)(a, b)


def mm_tn(a, b, *, tm, tn, out_dtype, name, col0=0, ncols=None, stack=None, layer=None):
    s, m = a.shape
    n = b.shape[1] - col0 if ncols is None else ncols
    c0 = col0 // tn

    def body(a_ref, b_ref, *rest):
        o_ref = rest[-1]
        o_ref[...] = _dot_tn(a_ref[...], b_ref[...]).astype(o_ref.dtype)

    in_specs = [pl.BlockSpec((s, tm), lambda i, j: (0, i)), pl.BlockSpec((s, tn), lambda i, j: (0, c0 + j))]
    if stack is None:
        return pl.pallas_call(
            body, grid=(m // tm, n // tn), in_specs=in_specs, out_specs=pl.BlockSpec((tm, tn), lambda i, j: (i, j)),
            out_shape=_sds((m, n), out_dtype), compiler_params=_cparams(("parallel", "parallel")), name=name)(a, b)
    return pl.pallas_call(
        body, grid=(m // tm, n // tn), in_specs=in_specs + [pl.BlockSpec(memory_space=pl.ANY)],
        out_specs=pl.BlockSpec((None, tm, tn), lambda i, j: (layer, i, j)),
        out_shape=_sds(stack.shape, stack.dtype), input_output_aliases={2: 0},
        compiler_params=_cparams(("parallel", "parallel")), name=name)(a, b, stack)


def _row_tile(s):
    return min(256, s)


def _norm_mod_f(x, g, scale, shift):
    r = lax.rsqrt(jnp.mean(x * x, axis=-1, keepdims=True) + EPS)
    return (x * r * g) * (1.0 + scale) + shift


def norm_mod(x, g, scale, shift, *, name):
    s = x.shape[0]
    t = _row_tile(s)

    def body(x_ref, g_ref, sc_ref, sh_ref, o_ref):
        o_ref[...] = _norm_mod_f(x_ref[...], g_ref[...], sc_ref[...], sh_ref[...]).astype(o_ref.dtype)

    vec = pl.BlockSpec((1, D), lambda i: (0, 0))
    return pl.pallas_call(
        body, grid=(s // t,), in_specs=[pl.BlockSpec((t, D), lambda i: (i, 0)), vec, vec, vec],
        out_specs=pl.BlockSpec((t, D), lambda i: (i, 0)), out_shape=_sds((s, D), MMT),
        compiler_params=_cparams(("parallel",)), name=name)(x, g, scale, shift)


def norm_mod_bwd(x, dh, dres, g, scale, shift, *, name):
    s = x.shape[0]
    t = _row_tile(s)

    def body(x_ref, dh_ref, dr_ref, g_ref, sc_ref, sh_ref, dx_ref, dg_ref, dsc_ref, dsh_ref):
        @pl.when(pl.program_id(0) == 0)
        def _():
            dg_ref[...] = jnp.zeros_like(dg_ref)
            dsc_ref[...] = jnp.zeros_like(dsc_ref)
            dsh_ref[...] = jnp.zeros_like(dsh_ref)

        _, vjp = jax.vjp(_norm_mod_f, x_ref[...], g_ref[...], sc_ref[...], sh_ref[...])
        dx, dg, dsc, dsh = vjp(dh_ref[...])
        dx_ref[...] = dr_ref[...] + dx
        dg_ref[...] += dg
        dsc_ref[...] += dsc
        dsh_ref[...] += dsh

    row = pl.BlockSpec((t, D), lambda i: (i, 0))
    vec = pl.BlockSpec((1, D), lambda i: (0, 0))
    return pl.pallas_call(
        body, grid=(s // t,), in_specs=[row, row, row, vec, vec, vec], out_specs=[row, vec, vec, vec],
        out_shape=[_sds((s, D), F32)] + [_sds((1, D), F32)] * 3,
        compiler_params=_cparams(("arbitrary",)), name=name)(x, dh, dres, g, scale, shift)


def gate_bwd(dx, y, gate, *, name):
    s = dx.shape[0]
    t = _row_tile(s)

    def body(dx_ref, y_ref, g_ref, dy_ref, dg_ref):
        @pl.when(pl.program_id(0) == 0)
        def _():
            dg_ref[...] = jnp.zeros_like(dg_ref)

        dxv = dx_ref[...]
        dy_ref[...] = (g_ref[...] * dxv).astype(dy_ref.dtype)
        dg_ref[...] += jnp.sum(dxv * y_ref[...], axis=0, keepdims=True)

    row = pl.BlockSpec((t, D), lambda i: (i, 0))
    vec = pl.BlockSpec((1, D), lambda i: (0, 0))
    return pl.pallas_call(
        body, grid=(s // t,), in_specs=[row, row, vec], out_specs=[row, vec],
        out_shape=[_sds((s, D), MMT), _sds((1, D), F32)],
        compiler_params=_cparams(("arbitrary",)), name=name)(dx, y, gate)


def loss_and_grad(xf, target, *, name):
    s = xf.shape[0]
    t = _row_tile(s)

    def body(x_ref, t_ref, l_ref, dx_ref):
        @pl.when(pl.program_id(0) == 0)
        def _():
            l_ref[...] = jnp.zeros_like(l_ref)

        e = x_ref[...] - t_ref[...]
        dx_ref[...] = e * (1.0 / D)
        l_ref[...] += 0.5 * jnp.sum(jnp.sum(e * e, axis=1, keepdims=True), axis=0, keepdims=True) * (1.0 / D)

    row = pl.BlockSpec((t, D), lambda i: (i, 0))
    return pl.pallas_call(
        body, grid=(s // t,), in_specs=[row, row], out_specs=[pl.BlockSpec((1, 1), lambda i: (0, 0)), row],
        out_shape=[_sds((1, 1), F32), _sds((s, D), F32)],
        compiler_params=_cparams(("arbitrary",)), name=name)(xf, target)


def _ret_consts():
    log_g = np.log1p(-np.exp2(-5.0 - np.arange(NH, dtype=np.float32))).astype(np.float32)
    idx = np.arange(CH, dtype=np.float32)
    d_intra = np.exp(np.abs(idx[:, None] - idx[None, :])[None] * log_g[:, None, None]).astype(np.float32)
    k_w = np.exp((CH - 1.0 - idx)[None, :] * log_g[:, None]).astype(np.float32)
    q_w = np.exp((idx + 1.0)[None, :] * log_g[:, None]).astype(np.float32)
    g_chunk = [float(v) for v in np.exp(np.float32(CH) * log_g).astype(np.float32)]
    bc = lambda a: np.ascontiguousarray(np.broadcast_to(a[:, :, None], (NH, CH, HD)))
    return jnp.asarray(d_intra), jnp.asarray(bc(k_w)), jnp.asarray(bc(q_w)), g_chunk


def _rope_tables(s):
    half = HD // 2
    inv_freq = (ROPE_BASE ** (-np.arange(half, dtype=np.float64) / half)).astype(np.float32)
    ang = (np.arange(s, dtype=np.float32)[:, None] * inv_freq[None, :]).astype(np.float64)
    cos, sin = np.cos(ang).astype(np.float32), np.sin(ang).astype(np.float32)
    return jnp.asarray(np.concatenate([cos, cos], axis=1)), jnp.asarray(np.concatenate([-sin, sin], axis=1))


def _ret_chunk(qs, ks, vs, rs, cos, sin, dintra, kw, qw, g_chunk):
    outs, rn = [], []
    for h in range(NH):
        q = qs[h] * cos + _swap_halves(qs[h]) * sin
        k = (ks[h] * cos + _swap_halves(ks[h]) * sin) * (HD ** -0.5)
        sc = _bdot_nt(q, k) * dintra[h]
        outs.append(_bdot(sc, vs[h]) + _bdot(q * qw[h], rs[h]))
        rn.append(g_chunk[h] * rs[h] + _bdot_tn(k * kw[h], vs[h]))
    return outs, rn


def _heads(x):
    return [x[:, h * HD:(h + 1) * HD] for h in range(NH)]


def retention_fwd(zz, cosf, sinf, *, name):
    s = zz.shape[0]
    n = s // CH
    dintra, kw, qw, g_chunk = _ret_consts()

    def body(q_ref, k_ref, v_ref, c_ref, s_ref, di_ref, kw_ref, qw_ref, o_ref, rp_ref, r_scr):
        @pl.when(pl.program_id(0) == 0)
        def _():
            r_scr[...] = jnp.zeros_like(r_scr)

        rprev = r_scr[...]
        rp_ref[0] = rprev
        outs, rn = _ret_chunk(_heads(q_ref[...]), _heads(k_ref[...]), _heads(v_ref[...]),
                              [rprev[h * HD:(h + 1) * HD] for h in range(NH)], c_ref[...], s_ref[...],
                              [di_ref[h] for h in range(NH)], [kw_ref[h] for h in range(NH)],
                              [qw_ref[h] for h in range(NH)], g_chunk)
        o_ref[...] = jnp.concatenate(outs, axis=1)
        r_scr[...] = jnp.concatenate(rn, axis=0)

    col = lambda c: pl.BlockSpec((CH, BW), lambda i: (i, c // BW))
    tab = pl.BlockSpec((CH, HD), lambda i: (i, 0))
    cst = lambda shp: pl.BlockSpec(shp, lambda i: (0,) * len(shp))
    return pl.pallas_call(
        body, grid=(n,),
        in_specs=[col(RQ), col(RK), col(RV), tab, tab, cst((NH, CH, CH)), cst((NH, CH, HD)), cst((NH, CH, HD))],
        out_specs=[pl.BlockSpec((CH, BW), lambda i: (i, 0)), pl.BlockSpec((1, BW, HD), lambda i: (i, 0, 0))],
        out_shape=[_sds((s, BW), F32), _sds((n, BW, HD), F32)],
        scratch_shapes=[pltpu.VMEM((BW, HD), F32)],
        compiler_params=_cparams(("arbitrary",)), name=name)(zz, zz, zz, cosf, sinf, dintra, kw, qw)


def retention_bwd(zz, cosf, sinf, rprev, do, dzz, *, name):
    s = zz.shape[0]
    n = s // CH
    dintra, kw, qw, g_chunk = _ret_consts()

    def body(q_ref, k_ref, v_ref, c_ref, s_ref, di_ref, kw_ref, qw_ref, rp_ref, do_ref, dzz_ref, dz_ref, dr_scr):
        @pl.when(pl.program_id(0) == 0)
        def _():
            dr_scr[...] = jnp.zeros_like(dr_scr)

        rprev_v = rp_ref[0]
        f = functools.partial(_ret_chunk, cos=c_ref[...], sin=s_ref[...],
                              dintra=[di_ref[h] for h in range(NH)], kw=[kw_ref[h] for h in range(NH)],
                              qw=[qw_ref[h] for h in range(NH)], g_chunk=g_chunk)
        _, vjp = jax.vjp(f, _heads(q_ref[...]), _heads(k_ref[...]), _heads(v_ref[...]),
                         [rprev_v[h * HD:(h + 1) * HD] for h in range(NH)])
        dr = dr_scr[...]
        dq, dk, dv, drp = vjp((_heads(do_ref[...]), [dr[h * HD:(h + 1) * HD] for h in range(NH)]))
        dz_ref[...] = jnp.concatenate(dq + dk + dv, axis=1).astype(dz_ref.dtype)
        dr_scr[...] = jnp.concatenate(drp, axis=0)

    col = lambda c: pl.BlockSpec((CH, BW), lambda i: (n - 1 - i, c // BW))
    tab = pl.BlockSpec((CH, HD), lambda i: (n - 1 - i, 0))
    cst = lambda shp: pl.BlockSpec(shp, lambda i: (0,) * len(shp))
    return pl.pallas_call(
        body, grid=(n,),
        in_specs=[col(RQ), col(RK), col(RV), tab, tab, cst((NH, CH, CH)), cst((NH, CH, HD)), cst((NH, CH, HD)),
                  pl.BlockSpec((1, BW, HD), lambda i: (n - 1 - i, 0, 0)), pl.BlockSpec((CH, BW), lambda i: (n - 1 - i, 0)),
                  pl.BlockSpec(memory_space=pl.ANY)],
        out_specs=pl.BlockSpec((CH, 3 * BW), lambda i: (n - 1 - i, RQ // (3 * BW))),
        out_shape=_sds(dzz.shape, dzz.dtype), input_output_aliases={10: 0},
        scratch_shapes=[pltpu.VMEM((BW, HD), F32)],
        compiler_params=_cparams(("arbitrary",)), name=name)(zz, zz, zz, cosf, sinf, dintra, kw, qw, rprev, do, dzz)


GKW = NH * GDK


def _gla_consts():
    tri = np.tril(np.ones((CH, CH), np.float32))
    mask_t = np.zeros((BW, GKW), np.float32)
    for h in range(NH):
        mask_t[h * HD:(h + 1) * HD, h * GDK:(h + 1) * GDK] = 1.0
    return jnp.asarray(tri), jnp.asarray(mask_t)


def _gla_chunk(q, k, v, glr, w_a2, b_a, st, tri, mask_t):
    la = _log_sigmoid(_bdot(glr, w_a2) + b_a) * (1.0 / 16.0)
    bc = _dot(tri, la, HI)
    be = jnp.sum(la, axis=0, keepdims=True)
    kv_t = _bdot_tn(v, k * jnp.exp(be - bc)) * mask_t
    sn = jnp.exp(be) * st + kv_t
    return _bdot_nt(q * (GDK ** -0.5), sn), sn


def gla_fwd(zz, w_a2p, b_a, *, name):
    s = zz.shape[0]
    n = s // CH
    tri, mask_t = _gla_consts()

    def body(q_ref, k_ref, v_ref, lr_ref, w_ref, b_ref, tri_ref, m_ref, o_ref, sp_ref, st_scr):
        @pl.when(pl.program_id(0) == 0)
        def _():
            st_scr[...] = jnp.zeros_like(st_scr)

        sp = st_scr[...]
        sp_ref[0] = sp
        o, sn = _gla_chunk(q_ref[...], k_ref[...], v_ref[...], lr_ref[...], w_ref[...], b_ref[...], sp,
                           tri_ref[...], m_ref[...])
        o_ref[...] = o
        st_scr[...] = sn

    cst = lambda shp: pl.BlockSpec(shp, lambda i: (0,) * len(shp))
    return pl.pallas_call(
        body, grid=(n,),
        in_specs=[pl.BlockSpec((CH, GKW), lambda i: (i, GQ // GKW)), pl.BlockSpec((CH, GKW), lambda i: (i, GK // GKW)),
                  pl.BlockSpec((CH, BW), lambda i: (i, GV // BW)), pl.BlockSpec((CH, HD), lambda i: (i, LR // HD)),
                  cst((HD, GKW)), cst((1, GKW)), cst((CH, CH)), cst((BW, GKW))],
        out_specs=[pl.BlockSpec((CH, BW), lambda i: (i, 0)), pl.BlockSpec((1, BW, GKW), lambda i: (i, 0, 0))],
        out_shape=[_sds((s, BW), F32), _sds((n, BW, GKW), F32)],
        scratch_shapes=[pltpu.VMEM((BW, GKW), F32)],
        compiler_params=_cparams(("arbitrary",)), name=name)(zz, zz, zz, zz, w_a2p, b_a, tri, mask_t)


def gla_bwd(zz, w_a2p, b_a, sprev, do, dzz, *, name):
    s = zz.shape[0]
    n = s // CH
    tri, mask_t = _gla_consts()

    def body(q_ref, k_ref, v_ref, lr_ref, w_ref, b_ref, tri_ref, m_ref, sp_ref, do_ref, dzz_ref,
             dz_ref, dlr_ref, dw_ref, db_ref, ds_scr):
        @pl.when(pl.program_id(0) == 0)
        def _():
            ds_scr[...] = jnp.zeros_like(ds_scr)
            dw_ref[...] = jnp.zeros_like(dw_ref)
            db_ref[...] = jnp.zeros_like(db_ref)

        f = functools.partial(_gla_chunk, tri=tri_ref[...], mask_t=m_ref[...])
        _, vjp = jax.vjp(f, q_ref[...], k_ref[...], v_ref[...], lr_ref[...], w_ref[...], b_ref[...], sp_ref[0])
        dq, dk, dv, dlr, dw, db, dsp = vjp((do_ref[...], ds_scr[...]))
        dz_ref[...] = jnp.concatenate([dq, dk, dv], axis=1).astype(dz_ref.dtype)
        dlr_ref[...] = dlr.astype(dlr_ref.dtype)
        dw_ref[...] += dw
        db_ref[...] += db
        ds_scr[...] = dsp

    cst = lambda shp: pl.BlockSpec(shp, lambda i: (0,) * len(shp))
    r = lambda i: n - 1 - i
    return pl.pallas_call(
        body, grid=(n,),
        in_specs=[pl.BlockSpec((CH, GKW), lambda i: (r(i), GQ // GKW)), pl.BlockSpec((CH, GKW), lambda i: (r(i), GK // GKW)),
                  pl.BlockSpec((CH, BW), lambda i: (r(i), GV // BW)), pl.BlockSpec((CH, HD), lambda i: (r(i), LR // HD)),
                  cst((HD, GKW)), cst((1, GKW)), cst((CH, CH)), cst((BW, GKW)),
                  pl.BlockSpec((1, BW, GKW), lambda i: (r(i), 0, 0)), pl.BlockSpec((CH, BW), lambda i: (r(i), 0)),
                  pl.BlockSpec(memory_space=pl.ANY)],
        out_specs=[pl.BlockSpec((CH, 2 * GKW + BW), lambda i: (r(i), GQ // (2 * GKW + BW))),
                   pl.BlockSpec((CH, HD), lambda i: (r(i), 0)), cst((HD, GKW)), cst((1, GKW))],
        out_shape=[_sds(dzz.shape, dzz.dtype), _sds((s, HD), MMT), _sds((HD, GKW), F32), _sds((1, GKW), F32)],
        input_output_aliases={10: 0}, scratch_shapes=[pltpu.VMEM((BW, GKW), F32)],
        compiler_params=_cparams(("arbitrary",)), name=name)(zz, zz, zz, zz, w_a2p, b_a, tri, mask_t, sprev, do, dzz)


def _fox_pre_f(fqs, fks, ff, gq, gk, bf):
    def rms(x, g):
        return x * lax.rsqrt(jnp.mean(x * x, axis=-1, keepdims=True) + EPS) * g

    qn = [rms(x, gq) * (HD ** -0.5) for x in fqs]
    kn = [rms(x, gk) for x in fks]
    return qn, kn, _log_sigmoid(ff + bf)


def fox_pre(zz, gq, gk, bf, *, name):
    s = zz.shape[0]
    t = _row_tile(s)
    tri = jnp.asarray(np.tril(np.ones((t, t), np.float32)))

    def body(q_ref, k_ref, f_ref, gq_ref, gk_ref, b_ref, tri_ref, qn_ref, kn_ref, cum_ref, carry):
        @pl.when(pl.program_id(0) == 0)
        def _():
            carry[...] = jnp.zeros_like(carry)

        qn, kn, lf = _fox_pre_f(_heads(q_ref[...]), _heads(k_ref[...]), f_ref[...], gq_ref[...], gk_ref[...], b_ref[...])
        qn_ref[...] = jnp.concatenate(qn, axis=1).astype(qn_ref.dtype)
        kn_ref[...] = jnp.concatenate(kn, axis=1).astype(kn_ref.dtype)
        cum_ref[...] = _dot(tri_ref[...], lf, HI) + carry[...]
        carry[...] += jnp.sum(lf, axis=0, keepdims=True)

    vec = pl.BlockSpec((1, HD), lambda i: (0, 0))
    return pl.pallas_call(
        body, grid=(s // t,),
        in_specs=[pl.BlockSpec((t, BW), lambda i: (i, FQ // BW)), pl.BlockSpec((t, BW), lambda i: (i, FK // BW)),
                  pl.BlockSpec((t, HD), lambda i: (i, FF // HD)), vec, vec, vec, pl.BlockSpec((t, t), lambda i: (0, 0))],
        out_specs=[pl.BlockSpec((t, BW), lambda i: (i, 0)), pl.BlockSpec((t, BW), lambda i: (i, 0)),
                   pl.BlockSpec((t, HD), lambda i: (i, 0))],
        out_shape=[_sds((s, BW), MMT), _sds((s, BW), MMT), _sds((s, HD), F32)],
        scratch_shapes=[pltpu.VMEM((1, HD), F32)],
        compiler_params=_cparams(("arbitrary",)), name=name)(zz, zz, zz, gq, gk, bf, tri)


def fox_pre_bwd(zz, gq, gk, bf, dqn, dkn, dcum, dzz, *, name):
    s = zz.shape[0]
    t = _row_tile(s)
    nt = s // t
    triu = jnp.asarray(np.triu(np.ones((t, t), np.float32)))

    def body(q_ref, k_ref, f_ref, gq_ref, gk_ref, b_ref, tri_ref, dqn_ref, dkn_ref, dcum_ref, dzz_ref,
             dz_ref, dff_ref, dgq_ref, dgk_ref, db_ref, carry):
        @pl.when(pl.program_id(0) == 0)
        def _():
            carry[...] = jnp.zeros_like(carry)
            dgq_ref[...] = jnp.zeros_like(dgq_ref)
            dgk_ref[...] = jnp.zeros_like(dgk_ref)
            db_ref[...] = jnp.zeros_like(db_ref)

        dcum_v = dcum_ref[...]
        dlf = _dot(tri_ref[...], dcum_v, HI) + carry[...]
        carry[...] += jnp.sum(dcum_v, axis=0, keepdims=True)
        _, vjp = jax.vjp(_fox_pre_f, _heads(q_ref[...]), _heads(k_ref[...]), f_ref[...], gq_ref[...], gk_ref[...], b_ref[...])
        dq, dk, dff, dgq, dgk, db = vjp((_heads(dqn_ref[...]), _heads(dkn_ref[...]), dlf))
        dz_ref[...] = jnp.concatenate(dq + dk, axis=1).astype(dz_ref.dtype)
        dff_ref[...] = dff.astype(dff_ref.dtype)
        dgq_ref[...] += dgq
        dgk_ref[...] += dgk
        db_ref[...] += db

    r = lambda i: nt - 1 - i
    vec = pl.BlockSpec((1, HD), lambda i: (0, 0))
    return pl.pallas_call(
        body, grid=(nt,),
        in_specs=[pl.BlockSpec((t, BW), lambda i: (r(i), FQ // BW)), pl.BlockSpec((t, BW), lambda i: (r(i), FK // BW)),
                  pl.BlockSpec((t, HD), lambda i: (r(i), FF // HD)), vec, vec, vec, pl.BlockSpec((t, t), lambda i: (0, 0)),
                  pl.BlockSpec((t, BW), lambda i: (r(i), 0)), pl.BlockSpec((t, BW), lambda i: (r(i), 0)),
                  pl.BlockSpec((t, HD), lambda i: (r(i), 0)), pl.BlockSpec(memory_space=pl.ANY)],
        out_specs=[pl.BlockSpec((t, 2 * BW), lambda i: (r(i), FQ // (2 * BW))), pl.BlockSpec((t, HD), lambda i: (r(i), 0)),
                   vec, vec, vec],
        out_shape=[_sds(dzz.shape, dzz.dtype), _sds((s, HD), MMT), _sds((1, HD), F32), _sds((1, HD), F32), _sds((1, HD), F32)],
        input_output_aliases={10: 0}, scratch_shapes=[pltpu.VMEM((1, HD), F32)],
        compiler_params=_cparams(("arbitrary",)), name=name)(zz, zz, zz, gq, gk, bf, triu, dqn, dkn, dcum, dzz)


def _fox_blocks(s):
    return min(256, s), min(512, s)


NEG = -1e30


def fox_attn_fwd(qn, kn, zz, cum_col, cum_row, *, name):
    s = qn.shape[0]
    bq, bk = _fox_blocks(s)

    def body(q_ref, k_ref, v_ref, cc_ref, cr_ref, o_ref, lse_ref):
        qi = pl.program_id(1)
        q = q_ref[...]
        cq = cc_ref[...]
        rows = qi * bq + lax.broadcasted_iota(jnp.int32, (bq, bk), 0)
        cols0 = lax.broadcasted_iota(jnp.int32, (bq, bk), 1)

        def step(j, carry, on_diagonal):
            m, l, acc = carry
            off = pl.multiple_of(j * bk, bk)
            k = k_ref[pl.ds(off, bk), :]
            v = v_ref[pl.ds(off, bk), :].astype(MMT)
            sc = _dot_nt(q, k) + cq - cr_ref[pl.ds(j, 1), :]
            if on_diagonal:
                sc = jnp.where(rows >= cols0 + j * bk, sc, NEG)
            m_new = jnp.maximum(m, jnp.max(sc, axis=1, keepdims=True))
            alpha = jnp.exp(m - m_new)
            p = jnp.exp(sc - m_new)
            return m_new, alpha * l + jnp.sum(p, axis=1, keepdims=True), alpha * acc + _dot(p.astype(MMT), v)

        nfull, nk = (qi * bq + 1) // bk, ((qi + 1) * bq + bk - 1) // bk
        carry = (jnp.full((bq, 1), NEG, F32), jnp.zeros((bq, 1), F32), jnp.zeros((bq, HD), F32))
        carry = lax.fori_loop(0, nfull, functools.partial(step, on_diagonal=False), carry)
        m, l, acc = lax.fori_loop(nfull, nk, functools.partial(step, on_diagonal=True), carry)
        o_ref[...] = acc / l
        lse_ref[...] = m + jnp.log(l)

    return pl.pallas_call(
        body, grid=(NH, s // bq),
        in_specs=[pl.BlockSpec((bq, HD), lambda h, i: (i, h)), pl.BlockSpec((s, HD), lambda h, i: (0, h)),
                  pl.BlockSpec((s, HD), lambda h, i: (0, FV // HD + h)),
                  pl.BlockSpec((None, bq, 1), lambda h, i: (h, i, 0)), pl.BlockSpec((None, s // bk, bk), lambda h, i: (h, 0, 0))],
        out_specs=[pl.BlockSpec((bq, HD), lambda h, i: (i, h)), pl.BlockSpec((None, bq, 1), lambda h, i: (h, i, 0))],
        out_shape=[_sds((s, BW), F32), _sds((NH, s, 1), F32)],
        compiler_params=_cparams(("parallel", "parallel")), name=name)(qn, kn, zz, cum_col, cum_row)


def fox_attn_bwd(qn, kn, zz, cum_col, cum_row, lse, do, dzz, *, name):
    s = qn.shape[0]
    bq, bk = _fox_blocks(s)
    nkc = s // bk

    def body(q_ref, k_ref, v_ref, cc_ref, cr_ref, lse_ref, do_ref, dzz_ref, dq_ref, dk_ref, dv_ref, dc_ref,
             p_scr, dp_scr, dv_scr):
        qi = pl.program_id(1)

        @pl.when(qi == 0)
        def _():
            dk_ref[...] = jnp.zeros_like(dk_ref)
            dv_scr[...] = jnp.zeros_like(dv_scr)
            dc_ref[...] = jnp.zeros_like(dc_ref)

        q = q_ref[...]
        dob = do_ref[...].astype(MMT)
        cq = cc_ref[...]
        lse_v = lse_ref[...]
        rows = qi * bq + lax.broadcasted_iota(jnp.int32, (bq, bk), 0)
        cols0 = lax.broadcasted_iota(jnp.int32, (bq, bk), 1)
        nfull, nk = (qi * bq + 1) // bk, ((qi + 1) * bq + bk - 1) // bk

        def probs(j, delta, on_diagonal):
            off = pl.multiple_of(j * bk, bk)
            sc = _dot_nt(q, k_ref[pl.ds(off, bk), :]) + cq - cr_ref[pl.ds(j, 1), :]
            p = jnp.exp(sc - lse_v)
            if on_diagonal:
                p = jnp.where(rows >= cols0 + j * bk, p, 0.0)
            dp = _dot_nt(dob, v_ref[pl.ds(off, bk), :].astype(MMT))
            p_scr[j] = p
            dp_scr[j] = dp
            return delta + jnp.sum(p * dp, axis=1, keepdims=True)

        delta = lax.fori_loop(0, nfull, functools.partial(probs, on_diagonal=False), jnp.zeros((bq, 1), F32))
        delta = lax.fori_loop(nfull, nk, functools.partial(probs, on_diagonal=True), delta)

        def grads(j, dq):
            off = pl.multiple_of(j * bk, bk)
            p = p_scr[j]
            ds = p * (dp_scr[j] - delta)
            dsm = ds.astype(MMT)
            dv_scr[pl.ds(off, bk), :] += _dot_tn(p.astype(MMT), dob)
            dk_ref[pl.ds(off, bk), :] += _dot_tn(dsm, q)
            dc_ref[pl.ds(j, 1), :] -= jnp.sum(ds, axis=0, keepdims=True)
            return dq + _dot(dsm, k_ref[pl.ds(off, bk), :])

        dq_ref[...] = lax.fori_loop(0, nk, grads, jnp.zeros((bq, HD), F32))

        @pl.when(qi == pl.num_programs(1) - 1)
        def _():
            dv_ref[...] = dv_scr[...].astype(dv_ref.dtype)

    full = lambda c0=0: pl.BlockSpec((s, HD), lambda h, i: (0, c0 + h))
    blk = lambda: pl.BlockSpec((bq, HD), lambda h, i: (i, h))
    colv = lambda: pl.BlockSpec((None, bq, 1), lambda h, i: (h, i, 0))
    rowv = lambda: pl.BlockSpec((None, nkc, bk), lambda h, i: (h, 0, 0))
    return pl.pallas_call(
        body, grid=(NH, s // bq),
        in_specs=[blk(), full(), full(FV // HD), colv(), rowv(), colv(), blk(), pl.BlockSpec(memory_space=pl.ANY)],
        out_specs=[blk(), full(), full(FV // HD), rowv()],
        out_shape=[_sds((s, BW), F32), _sds((s, BW), F32), _sds(dzz.shape, dzz.dtype), _sds((NH, nkc, bk), F32)],
        input_output_aliases={7: 2},
        scratch_shapes=[pltpu.VMEM((nkc, bq, bk), F32), pltpu.VMEM((nkc, bq, bk), F32), pltpu.VMEM((s, HD), F32)],
        compiler_params=_cparams(("parallel", "arbitrary")), name=name)(qn, kn, zz, cum_col, cum_row, lse, do, dzz)


def _branch_f(rets, rgs, glas, ggs, ret_g, gla_g):
    out_r, out_g = [], []
    for h in range(NH):
        xc = rets[h] - jnp.mean(rets[h], axis=-1, keepdims=True)
        y = xc * lax.rsqrt(jnp.mean(xc * xc, axis=-1, keepdims=True) + EPS) * ret_g[h]
        out_r.append(_silu(rgs[h]) * y)
        x = glas[h]
        y = x * lax.rsqrt(jnp.mean(x * x, axis=-1, keepdims=True) + EPS) * gla_g
        out_g.append(_silu(ggs[h]) * y)
    return out_r, out_g


def _w_br_spec(layer):
    return pl.BlockSpec((None, 3, BW, D), lambda i: (layer, 0, 0, 0))


def mix_fwd(ret_raw, gla_raw, fox_o, zz, ret_g, gla_g, b_mg, w_br, *, name, layer):
    s = zz.shape[0]
    t = _row_tile(s)

    def body(r_ref, g_ref, f_ref, rg_ref, gg_ref, gp_ref, rgn_ref, ggn_ref, bmg_ref, w_ref, o_ref):
        rgn = rgn_ref[...]
        br_r, br_g = _branch_f(_heads(r_ref[...]), _heads(rg_ref[...]), _heads(g_ref[...]), _heads(gg_ref[...]),
                               _heads(rgn), ggn_ref[...])
        brs = [jnp.concatenate(br_r, axis=1), jnp.concatenate(br_g, axis=1), f_ref[...]]
        acc = jnp.zeros((t, D), F32)
        for b in range(3):
            gate = jax.nn.sigmoid(gp_ref[:, b * D:(b + 1) * D] + bmg_ref[:, b * D:(b + 1) * D])
            acc = acc + gate * _dot(brs[b].astype(MMT), w_ref[b])
        o_ref[...] = acc.astype(o_ref.dtype)

    row = lambda w, c=0: pl.BlockSpec((t, w), lambda i: (i, c // w))
    cst = lambda shp: pl.BlockSpec(shp, lambda i: (0,) * len(shp))
    return pl.pallas_call(
        body, grid=(s // t,),
        in_specs=[row(BW), row(BW), row(BW), row(BW, RG), row(BW, GG), row(3 * D, GP), cst((1, BW)), cst((1, HD)),
                  cst((1, 3 * D)), _w_br_spec(layer)],
        out_specs=row(D), out_shape=_sds((s, D), MMT),
        compiler_params=_cparams(("parallel",)), name=name)(ret_raw, gla_raw, fox_o, zz, zz, zz, ret_g, gla_g, b_mg, w_br)


def mix_bwd(ret_raw, gla_raw, fox_o, zz, ret_g, gla_g, b_mg, w_br, dmi, *, name, layer):
    s = zz.shape[0]
    t = _row_tile(s)

    def body(r_ref, g_ref, f_ref, rg_ref, gg_ref, gp_ref, rgn_ref, ggn_ref, bmg_ref, w_ref, dmi_ref,
             dr_ref, dg_ref, df_ref, dgp_ref, dw_ref, drgn_ref, dggn_ref, dbmg_ref):
        @pl.when(pl.program_id(0) == 0)
        def _():
            dw_ref[...] = jnp.zeros_like(dw_ref)
            drgn_ref[...] = jnp.zeros_like(drgn_ref)
            dggn_ref[...] = jnp.zeros_like(dggn_ref)
            dbmg_ref[...] = jnp.zeros_like(dbmg_ref)

        (br_r, br_g), vjp = jax.vjp(_branch_f, _heads(r_ref[...]), _heads(rg_ref[...]), _heads(g_ref[...]),
                                    _heads(gg_ref[...]), _heads(rgn_ref[...]), ggn_ref[...])
        brs = [jnp.concatenate(br_r, axis=1).astype(MMT), jnp.concatenate(br_g, axis=1).astype(MMT),
               f_ref[...].astype(MMT)]
        dmi_v = dmi_ref[...].astype(F32)
        dbr = []
        for b in range(3):
            w = w_ref[b]
            ybr = _dot(brs[b], w)
            gate = jax.nn.sigmoid(gp_ref[:, b * D:(b + 1) * D] + bmg_ref[:, b * D:(b + 1) * D])
            dgp = dmi_v * ybr * gate * (1.0 - gate)
            dgp_ref[:, b * D:(b + 1) * D] = dgp.astype(dgp_ref.dtype)
            dbmg_ref[:, b * D:(b + 1) * D] += jnp.sum(dgp, axis=0, keepdims=True)
            dy = (dmi_v * gate).astype(MMT)
            dw_ref[b] += _dot_tn(brs[b], dy)
            dbr.append(_dot_nt(dy, w))
        dr, drg, dg, dgg, drgn, dggn = vjp((_heads(dbr[0]), _heads(dbr[1])))
        dr_ref[...] = jnp.concatenate(dr, axis=1)
        dg_ref[...] = jnp.concatenate(dg, axis=1)
        df_ref[...] = dbr[2]
        dgp_ref[:, RG:RG + BW] = jnp.concatenate(drg, axis=1).astype(dgp_ref.dtype)
        dgp_ref[:, GG:GG + BW] = jnp.concatenate(dgg, axis=1).astype(dgp_ref.dtype)
        drgn_ref[...] += jnp.concatenate(drgn, axis=1)
        dggn_ref[...] += dggn

    row = lambda w, c=0: pl.BlockSpec((t, w), lambda i: (i, c // w))
    cst = lambda shp: pl.BlockSpec(shp, lambda i: (0,) * len(shp))
    return pl.pallas_call(
        body, grid=(s // t,),
        in_specs=[row(BW), row(BW), row(BW), row(BW, RG), row(BW, GG), row(3 * D, GP), cst((1, BW)), cst((1, HD)),
                  cst((1, 3 * D)), _w_br_spec(layer), row(D)],
        out_specs=[row(BW), row(BW), row(BW), row(FV), cst((3, BW, D)), cst((1, BW)), cst((1, HD)), cst((1, 3 * D))],
        out_shape=[_sds((s, BW), F32)] * 3 + [_sds((s, NZZ), MMT), _sds((3, BW, D), F32), _sds((1, BW), F32),
                                              _sds((1, HD), F32), _sds((1, 3 * D), F32)],
        compiler_params=_cparams(("arbitrary",)), name=name)(ret_raw, gla_raw, fox_o, zz, zz, zz, ret_g, gla_g, b_mg, w_br, dmi)


CT = 256


def _shift_down(x, k, rows):
    return jnp.where(rows >= k, pltpu.roll(x, k, 0), 0.0)


def _shift_up(x, k, rows, s):
    return jnp.where(rows < s - k, pltpu.roll(x, s - k, 0), 0.0)


def conv_fwd(ug, w_conv, b_conv, *, name):
    s = ug.shape[0]
    nt = DFF // CT

    def body(u_ref, g_ref, w_ref, b_ref, a_ref):
        u = u_ref[...]
        rows = lax.broadcasted_iota(jnp.int32, u.shape, 0)
        uc = b_ref[...] + w_ref[0:1, :] * _shift_down(u, 2, rows) + w_ref[1:2, :] * _shift_down(u, 1, rows) + w_ref[2:3, :] * u
        a_ref[...] = (_silu(uc) * g_ref[...]).astype(a_ref.dtype)

    return pl.pallas_call(
        body, grid=(nt,),
        in_specs=[pl.BlockSpec((s, CT), lambda j: (0, j)), pl.BlockSpec((s, CT), lambda j: (0, nt + j)),
                  pl.BlockSpec((3, CT), lambda j: (0, j)), pl.BlockSpec((1, CT), lambda j: (0, j))],
        out_specs=pl.BlockSpec((s, CT), lambda j: (0, j)), out_shape=_sds((s, DFF), MMT),
        compiler_params=_cparams(("parallel",)), name=name)(ug, ug, w_conv, b_conv)


def conv_bwd(ug, w_conv, b_conv, da, *, name):
    s = ug.shape[0]
    nt = DFF // CT

    def body(u_ref, g_ref, w_ref, b_ref, da_ref, du_ref, dg_ref, dw_ref, db_ref):
        u = u_ref[...]
        rows = lax.broadcasted_iota(jnp.int32, u.shape, 0)
        u2, u1 = _shift_down(u, 2, rows), _shift_down(u, 1, rows)
        uc = b_ref[...] + w_ref[0:1, :] * u2 + w_ref[1:2, :] * u1 + w_ref[2:3, :] * u
        sg = jax.nn.sigmoid(uc)
        da_v = da_ref[...]
        dg_ref[...] = (da_v * uc * sg).astype(dg_ref.dtype)
        duc = da_v * g_ref[...] * sg * (1.0 + uc * (1.0 - sg))
        du = w_ref[2:3, :] * duc + w_ref[1:2, :] * _shift_up(duc, 1, rows, s) + w_ref[0:1, :] * _shift_up(duc, 2, rows, s)
        du_ref[...] = du.astype(du_ref.dtype)
        dw_ref[0:1, :] = jnp.sum(duc * u2, axis=0, keepdims=True)
        dw_ref[1:2, :] = jnp.sum(duc * u1, axis=0, keepdims=True)
        dw_ref[2:3, :] = jnp.sum(duc * u, axis=0, keepdims=True)
        db_ref[...] = jnp.sum(duc, axis=0, keepdims=True)

    col = lambda: pl.BlockSpec((s, CT), lambda j: (0, j))
    return pl.pallas_call(
        body, grid=(nt,),
        in_specs=[col(), pl.BlockSpec((s, CT), lambda j: (0, nt + j)), pl.BlockSpec((3, CT), lambda j: (0, j)),
                  pl.BlockSpec((1, CT), lambda j: (0, j)), col()],
        out_specs=[col(), col(), pl.BlockSpec((3, CT), lambda j: (0, j)), pl.BlockSpec((1, CT), lambda j: (0, j))],
        out_shape=[_sds((s, DFF), MMT), _sds((s, DFF), MMT), _sds((3, DFF), F32), _sds((1, DFF), F32)],
        compiler_params=_cparams(("parallel",)), name=name)(ug, ug, w_conv, b_conv, da)


def place_tail(dzz, dlr, dff, *, name):
    s = dzz.shape[0]
    t = _row_tile(s)

    def body(a_ref, b_ref, z_ref, o_ref):
        o_ref[...] = jnp.concatenate([a_ref[...], b_ref[...]], axis=1)

    spec = pl.BlockSpec((t, HD), lambda i: (i, 0))
    return pl.pallas_call(
        body, grid=(s // t,), in_specs=[spec, spec, pl.BlockSpec(memory_space=pl.ANY)],
        out_specs=pl.BlockSpec((t, 2 * HD), lambda i: (i, LR // (2 * HD))), out_shape=_sds(dzz.shape, dzz.dtype),
        input_output_aliases={2: 0}, compiler_params=_cparams(("parallel",)), name=name)(dlr, dff, dzz)


def _tiles(s):
    return min(1024, s)


def layer_fwd(x, mod, p, cosf, sinf):
    s = x.shape[0]
    tm = _tiles(s)
    l = p["l"]
    shift1, scale1, gate1, shift2, scale2, gate2 = mod
    h = norm_mod(x, p["norm1_g"], scale1, shift1, name="norm_mod")
    zz = mm_nn(h, p["w1"], tm=tm, tn=768, out_dtype=F32, name="mm_w1", layer=l)
    ret_raw, rprev = retention_fwd(zz, cosf, sinf, name="ret_fwd")
    gla_raw, sprev = gla_fwd(zz, p["w_a2p"], p["b_gla_a"], name="gla_fwd")
    qn, kn, cum = fox_pre(zz, p["q_norm_g"], p["k_norm_g"], p["b_foxp"], name="fox_pre")
    bq, bk = _fox_blocks(s)
    cum_t = cum[:, :NH].T
    cum_col, cum_row = cum_t[:, :, None], cum_t.reshape(NH, s // bk, bk)
    fox_o, lse = fox_attn_fwd(qn, kn, zz, cum_col, cum_row, name="fox_fwd")
    mi = mix_fwd(ret_raw, gla_raw, fox_o, zz, p["ret_norm_g"], p["gla_norm_g"], p["b_mg"], p["w_br"], name="mix_fwd",
                 layer=l)
    x1, mixed = mm_nn_residual(mi, p["w_o"], x, gate1, tm=tm, tn=512, name="mm_wo", layer=l)
    h2 = norm_mod(x1, p["norm2_g"], scale2, shift2, name="norm_mod")
    ug = mm_nn(h2, p["w_up"], tm=tm, tn=512, out_dtype=F32, name="mm_wup", layer=l)
    a = conv_fwd(ug, p["w_conv"], p["b_conv"], name="conv_fwd")
    x2, y = mm_nn_residual(a, p["w_down"], x1, gate2, tm=tm, tn=512, name="mm_wdown", layer=l)
    saved = dict(x=x, h=h, zz=zz, ret_raw=ret_raw, rprev=rprev, gla_raw=gla_raw, sprev=sprev, qn=qn, kn=kn,
                 cum_col=cum_col, cum_row=cum_row, fox_o=fox_o, lse=lse, mi=mi, mixed=mixed, x1=x1, h2=h2, ug=ug, a=a, y=y)
    return x2, saved


def layer_bwd(dx2, mod, p, sv, cosf, sinf, stacks, slot):
    s = dx2.shape[0]
    tm = _tiles(s)
    l = p["l"]
    shift1, scale1, gate1, shift2, scale2, gate2 = mod
    g, stacks = {}, dict(stacks)
    dy, dgate2 = gate_bwd(dx2, sv["y"], gate2, name="gate_bwd")
    stacks["w_down"] = mm_tn(sv["a"], dy, tm=min(1408, DFF), tn=512, out_dtype=MMT, name="mm_dwdown",
                             stack=stacks["w_down"], layer=slot)
    da = mm_nt(dy, p["w_down"], tm=tm, tn=1408, out_dtype=F32, name="mm_da", layer=l)
    du, dg, g["w_conv"], g["b_conv"] = conv_bwd(sv["ug"], p["w_conv"], p["b_conv"], da, name="conv_bwd")
    dug = jnp.concatenate([du, dg], axis=1)
    stacks["w_up"] = mm_tn(sv["h2"], dug, tm=512, tn=512, out_dtype=MMT, name="mm_dwup", stack=stacks["w_up"], layer=slot)
    dh2 = mm_nt(dug, p["w_up"], tm=min(512, s), tn=D, out_dtype=F32, name="mm_dh2", layer=l)
    dx1, g["norm2_g"], dscale2, dshift2 = norm_mod_bwd(sv["x1"], dh2, dx2, p["norm2_g"], scale2, shift2, name="norm_mod_bwd")
    dmixed, dgate1 = gate_bwd(dx1, sv["mixed"], gate1, name="gate_bwd")
    stacks["w_o"] = mm_tn(sv["mi"], dmixed, tm=512, tn=512, out_dtype=MMT, name="mm_dwo", stack=stacks["w_o"], layer=slot)
    dmi = mm_nt(dmixed, p["w_o"], tm=tm, tn=512, out_dtype=MMT, name="mm_dmi", layer=l)
    zz = sv["zz"]
    (dret, dgla, dfox, dzz, g["w_br"], g["ret_norm_g"], g["gla_norm_g"], g["b_mg"]) = mix_bwd(
        sv["ret_raw"], sv["gla_raw"], sv["fox_o"], zz, p["ret_norm_g"], p["gla_norm_g"], p["b_mg"], p["w_br"], dmi,
        name="mix_bwd", layer=l)
    dqn, dkn, dzz, dcum_row = fox_attn_bwd(sv["qn"], sv["kn"], zz, sv["cum_col"], sv["cum_row"], sv["lse"], dfox, dzz,
                                           name="fox_bwd")
    dcum = jnp.pad(dcum_row.reshape(NH, s).T, ((0, 0), (0, HD - NH)))
    dzz, dff, g["q_norm_g"], g["k_norm_g"], g["b_foxp"] = fox_pre_bwd(
        zz, p["q_norm_g"], p["k_norm_g"], p["b_foxp"], dqn, dkn, dcum, dzz, name="fox_pre_bwd")
    dzz, dlr, g["w_a2p"], g["b_gla_a"] = gla_bwd(zz, p["w_a2p"], p["b_gla_a"], sv["sprev"], dgla, dzz, name="gla_bwd")
    dzz = retention_bwd(zz, cosf, sinf, sv["rprev"], dret, dzz, name="ret_bwd")
    dzz = place_tail(dzz, dlr, dff, name="place_tail")
    stacks["w_mg"] = mm_tn(sv["h"], dzz, tm=512, tn=768, out_dtype=MMT, name="mm_dwmg", ncols=WZ0, stack=stacks["w_mg"],
                           layer=slot)
    dwz = mm_tn(sv["h"], dzz, tm=512, tn=768, out_dtype=MMT, name="mm_dwz", col0=WZ0)
    stacks["w_in"] = unalign_dw_in(dwz, stacks["w_in"], slot)
    dh = mm_nt(dzz, p["w1"], tm=min(256, s), tn=D, out_dtype=F32, name="mm_dh", layer=l)
    dx, g["norm1_g"], dscale1, dshift1 = norm_mod_bwd(sv["x"], dh, dx1, p["norm1_g"], scale1, shift1, name="norm_mod_bwd")
    dmod = jnp.concatenate([dshift1, dscale1, dgate1, dshift2, dscale2, dgate2], axis=1)
    return dx, g, dmod, stacks


def _align_cols(w_in, w_mg):
    z = lambda n: jnp.zeros((w_in.shape[0], n), w_in.dtype)
    seg = lambda name: w_in[:, W_IN_COLS[name][0]:W_IN_COLS[name][1]]
    return jnp.concatenate([w_mg, seg("rg"), seg("gg"), seg("fv"), seg("rqkv"), seg("gqkv"), seg("fqk"), seg("lr"),
                            z(HD - GLR), seg("ff"), z(HD - NH)], axis=1)


def _unalign_cols(dwz):
    seg = lambda c0, name: dwz[:, c0 - WZ0:c0 - WZ0 + W_IN_COLS[name][1] - W_IN_COLS[name][0]]
    return jnp.concatenate([seg(RQ, "rqkv"), seg(RG, "rg"), seg(GQ, "gqkv"), seg(LR, "lr"), seg(GG, "gg"), seg(FQ, "fqk"),
                            seg(FV, "fv"), seg(FF, "ff")], axis=1)


def build_w1(w_in_sh, w_mg):
    nl = w_mg.shape[0]
    t = _row_tile(D)

    def body(s_ref, g_ref, o_ref):
        o_ref[...] = _align_cols(jnp.concatenate([s_ref[k] for k in range(4)], axis=1), g_ref[...])

    return pl.pallas_call(
        body, grid=(nl, D // t),
        in_specs=[pl.BlockSpec((None, 4, t, IN_W // 4), lambda l, i: (l, 0, i, 0)), pl.BlockSpec((None, t, WZ0), lambda l, i: (l, i, 0))],
        out_specs=pl.BlockSpec((None, t, NZZ), lambda l, i: (l, i, 0)), out_shape=_sds((nl, D, NZZ), w_mg.dtype),
        compiler_params=_cparams(("parallel", "parallel")), name="build_w1")(w_in_sh, w_mg)


def unalign_dw_in(dwz, stack, layer):
    t = _row_tile(D)

    def body(z_ref, s_ref, o_ref):
        w = _unalign_cols(z_ref[...])
        for k in range(4):
            o_ref[k] = w[:, k * (IN_W // 4):(k + 1) * (IN_W // 4)]

    return pl.pallas_call(
        body, grid=(D // t,),
        in_specs=[pl.BlockSpec((t, NZZ - WZ0), lambda i: (i, 0)), pl.BlockSpec(memory_space=pl.ANY)],
        out_specs=pl.BlockSpec((None, 4, t, IN_W // 4), lambda i: (layer, 0, i, 0)), out_shape=_sds(stack.shape, stack.dtype),
        input_output_aliases={1: 0}, compiler_params=_cparams(("parallel",)), name="unalign_dw_in")(dwz, stack)


def layer_params(w, big, l):
    row = lambda v: v[l][None, :]
    return dict(
        l=0, norm1_g=row(w["norm1_g"]), norm2_g=row(w["norm2_g"]), w1=big["w1"],
        w_a2p=jnp.pad(w["w_gla_a2"][l], ((0, HD - GLR), (0, 0))), b_gla_a=row(w["b_gla_a"]),
        b_foxp=jnp.pad(row(w["b_fox_f"]), ((0, 0), (0, HD - NH))), ret_norm_g=row(w["ret_norm_g"]),
        gla_norm_g=row(w["gla_norm_g"]), q_norm_g=row(w["q_norm_g"]), k_norm_g=row(w["k_norm_g"]),
        w_br=big["w_br"], b_mg=row(w["b_mg"]), w_o=big["w_o"], w_up=big["w_up"], w_conv=w["w_conv"][l],
        b_conv=row(w["b_conv"]), w_down=big["w_down"])


def layer_grads(g):
    vec = lambda v: v[0]
    return dict(
        norm1_g=vec(g["norm1_g"]), norm2_g=vec(g["norm2_g"]), w_gla_a2=g["w_a2p"][:GLR], b_gla_a=vec(g["b_gla_a"]),
        b_fox_f=g["b_foxp"][0, :NH], ret_norm_g=vec(g["ret_norm_g"]), gla_norm_g=vec(g["gla_norm_g"]),
        q_norm_g=vec(g["q_norm_g"]), k_norm_g=vec(g["k_norm_g"]), w_br=g["w_br"], b_mg=vec(g["b_mg"]),
        w_conv=g["w_conv"], b_conv=vec(g["b_conv"]))


def ada_mod(c_all, w_ada, b_ada):
    nl, _, n = w_ada.shape

    def body(c_ref, w_ref, b_ref, o_ref):
        o_ref[...] = _dot(_silu(c_ref[...]), w_ref[...], HI) + b_ref[...]

    return pl.pallas_call(
        body, grid=(nl,),
        in_specs=[pl.BlockSpec((8, D), lambda l: (0, 0)), pl.BlockSpec((None, D, n), lambda l: (l, 0, 0)),
                  pl.BlockSpec((None, 1, n), lambda l: (l, 0, 0))],
        out_specs=pl.BlockSpec((None, 8, n), lambda l: (l, 0, 0)), out_shape=_sds((nl, 8, n), F32),
        compiler_params=_cparams(("parallel",)), name="ada_mod")(c_all, w_ada, b_ada)


def ada_dw(c_all, dmod):
    nl, _, n = dmod.shape

    def body(c_ref, d_ref, o_ref):
        o_ref[...] = _dot_tn(_silu(c_ref[...]), d_ref[...], HI)

    return pl.pallas_call(
        body, grid=(nl,),
        in_specs=[pl.BlockSpec((8, D), lambda l: (0, 0)), pl.BlockSpec((None, 8, n), lambda l: (l, 0, 0))],
        out_specs=pl.BlockSpec((None, D, n), lambda l: (l, 0, 0)), out_shape=_sds((nl, D, n), F32),
        compiler_params=_cparams(("parallel",)), name="ada_dw")(c_all, dmod)


def sum_devices(g):
    def body(g_ref, o_ref):
        acc = g_ref[0]
        for d in range(1, 8):
            acc = acc + g_ref[d]
        o_ref[...] = acc

    return pl.pallas_call(body, out_shape=_sds(g.shape[1:], F32), name="sum_devices")(g)


def adamw(w, g, m, v, *, block, name, rows=None, into=None):
    nd = w.ndim
    lo, hi = (0, w.shape[0]) if rows is None else rows
    grid = ((hi - lo) // block[0],) + tuple(w.shape[i] // block[i] for i in range(1, nd))
    first = lo // block[0]
    bc1 = 1.0 - ADAM_B1 ** ADAM_STEP
    bc2 = 1.0 - ADAM_B2 ** ADAM_STEP

    def body(w_ref, g_ref, m_ref, v_ref, *rest):
        d_ref, nm_ref, nv_ref = rest[-3:]
        gv = g_ref[...]
        nm = ADAM_B1 * m_ref[...] + (1.0 - ADAM_B1) * gv
        nv = ADAM_B2 * v_ref[...] + (1.0 - ADAM_B2) * (gv * gv)
        nm_ref[...] = nm
        nv_ref[...] = nv
        d_ref[...] = -ADAM_LR * ((nm / bc1) / (jnp.sqrt(nv / bc2) + ADAM_EPS) + ADAM_WD * w_ref[...])

    spec = pl.BlockSpec(tuple(block), lambda i, *j: (first + i,) + j)
    given = [] if into is None else list(into)
    return pl.pallas_call(
        body, grid=grid, in_specs=[spec] * 4 + [pl.BlockSpec(memory_space=pl.ANY)] * len(given), out_specs=[spec] * 3,
        out_shape=[_sds(w.shape, F32)] * 3, input_output_aliases={4 + i: i for i in range(len(given))},
        compiler_params=_cparams(("parallel",) * nd), name=name)(w, g, m, v, *given)


MESH = pl.DeviceIdType.MESH
ANY = pl.BlockSpec(memory_space=pl.ANY)
VM = pl.BlockSpec(memory_space=pltpu.VMEM)


def _place():
    x, y, c = lax.axis_index("x"), lax.axis_index("y"), lax.axis_index("c")
    return x, y, c, [(1 - x, y), (x, 1 - y), (1 - x, 1 - y)]


def small_allgather(v, *, name):
    m_per, n = v.shape

    def body(x_ref, out_ref, send_sems, recv_sems, local_sem):
        x, y, c, chips = _place()
        me, sibling = (x, y, c), (x, y, 1 - c)

        def rows(px, py, pc):
            return out_ref.at[pl.ds((4 * px + 2 * py + pc) * m_per, m_per), :]

        def copy(k, block, to, src=None):
            return pltpu.make_async_remote_copy(
                src_ref=rows(*block) if src is None else src, dst_ref=rows(*block),
                send_sem=send_sems.at[k], recv_sem=recv_sems.at[k], device_id=to, device_id_type=MESH)

        mine = pltpu.make_async_copy(x_ref, rows(*me), local_sem)
        mine.start()
        first = [copy(0, me, sibling, src=x_ref)]
        first += [copy(1 + j, me, (*chip, c), src=x_ref) for j, chip in enumerate(chips)]
        for cp in first:
            cp.start()
        passed = [copy(4 + j, (*chip, c), sibling) for j, chip in enumerate(chips)]
        for j, chip in enumerate(chips):
            copy(1 + j, (*chip, c), me).wait_recv()
            passed[j].start()
        copy(0, sibling, me).wait_recv()
        for j, chip in enumerate(chips):
            copy(4 + j, (*chip, 1 - c), me).wait_recv()
        for cp in first + passed:
            cp.wait_send()
        mine.wait()

    return pl.pallas_call(
        body, out_shape=_sds((8 * m_per, n), v.dtype), in_specs=[VM], out_specs=VM,
        scratch_shapes=[pltpu.SemaphoreType.DMA((7,)), pltpu.SemaphoreType.DMA((7,)), pltpu.SemaphoreType.DMA],
        name=name)(v)


TENSORS = {
    "w_in": ("lead", None, (4, D, 1285), (1, D, 1285)),
    "w_mg": ("col", 768, (D, 3072), (512, 3072)),
    "w_br": ("col", 256, (3, BW, D), (3, BW, D)),
    "w_o": ("row", 256, (D, D), (D, D)),
    "w_up": ("col", 1408, (D, 5632), (256, 5632)),
    "w_down": ("row", 704, (DFF, D), (704, D)),
}
BIG = tuple(TENSORS)


def _shard_shape(name):
    kind, width, full, _ = TENSORS[name]
    if kind == "lead":
        return full[1:]
    return full[:-1] + (width,) if kind == "col" else (width,) + full[1:]


def _shard_view(ref, layers, name, k):
    kind, width, full, _ = TENSORS[name]
    if kind == "lead":
        return ref.at[layers, k]
    if kind == "row":
        return ref.at[layers, pl.ds(k * width, width)]
    return ref.at[(layers,) + (slice(None),) * (len(full) - 1) + (pl.ds(k * width, width),)]


def _remote(send_sems, recv_sems, k, src, dst, to):
    return pltpu.make_async_remote_copy(src_ref=src, dst_ref=dst, send_sem=send_sems.at[k], recv_sem=recv_sems.at[k],
                                        device_id=to, device_id_type=MESH)


def _dma_sems(n):
    return [pltpu.SemaphoreType.DMA((n,)), pltpu.SemaphoreType.DMA((n,))]


RS_GROUP = 2
HBM = pl.BlockSpec(memory_space=pltpu.HBM)
SEM = pl.BlockSpec(memory_space=pltpu.SEMAPHORE)
SPLIT_CALL = dict(compiler_params=pltpu.CompilerParams(has_side_effects=pltpu.SideEffectType.DATAFLOW_SIDE_EFFECTING))
PULL_SET = (("w_up", "w_down", "w_o"), ("w_in", "w_mg", "w_br"))


def _in_hbm(a):
    return pltpu.with_memory_space_constraint(a, pltpu.HBM)


def _pull_sends(send_sems, recv_sems, p, o, layer, core, x, y, chips):
    return [_remote(send_sems, recv_sems, 3 * BIG.index(n) + j, p[n].at[layer], _shard_view(o[n], 0, n, 2 * x + y), (*chip, core))
            for n in PULL_SET[core] for j, chip in enumerate(chips)]


def _pull_arrivals(send_sems, recv_sems, o, core, x, y, chips, to):
    views = [(3 * BIG.index(n) + j, _shard_view(o[n], 0, n, 2 * chip[0] + chip[1]))
             for n in PULL_SET[core] for j, chip in enumerate(chips)]
    return [_remote(send_sems, recv_sems, k, v, v, to) for k, v in views]


def gather_start(shards, layer, *, name):
    nt = len(BIG)

    def body(*refs):
        p, o = dict(zip(BIG, refs[:nt])), dict(zip(BIG, refs[nt:2 * nt]))
        x, y, c, chips = _place()
        for core in (0, 1):
            @pl.when(c == core)
            def _():
                for cp in _pull_sends(refs[2 * nt], refs[2 * nt + 1], p, o, layer, core, x, y, chips):
                    cp.start()
        refs[-1][...] = jnp.zeros_like(refs[-1])

    lands = [_in_hbm(lax.empty((1,) + TENSORS[n][2], shards[n].dtype)) for n in BIG]
    outs = pl.pallas_call(
        body,
        out_shape=(pltpu.SemaphoreType.DMA((3 * nt,)), pltpu.SemaphoreType.DMA((3 * nt,)),
                   *[pltpu.HBM(a.shape, a.dtype) for a in lands], _sds((8, HD), F32)),
        in_specs=[HBM] * (2 * nt), out_specs=(SEM, SEM, *[HBM] * nt, VM),
        input_output_aliases={nt + t: 2 + t for t in range(nt)}, name=name, **SPLIT_CALL)(
            *[_in_hbm(shards[n]) for n in BIG], *lands)
    return outs[0], outs[1], outs[2:2 + nt], outs[-1]


def gather_wait(send_sems, recv_sems, shards, lands, after, layer, *, name):
    nt = len(BIG)

    def body(*refs):
        p, o = dict(zip(BIG, refs[:nt])), dict(zip(BIG, refs[nt:2 * nt]))
        ss, rs = refs[2 * nt], refs[2 * nt + 1]
        x, y, c, chips = _place()
        for core in (0, 1):
            @pl.when(c == core)
            def _():
                for cp in _pull_sends(ss, rs, p, o, layer, core, x, y, chips):
                    cp.wait_send()
                for cp in _pull_arrivals(ss, rs, o, core, x, y, chips, (x, y, core)):
                    cp.wait_recv()

    return pl.pallas_call(
        body, out_shape=tuple(pltpu.HBM(a.shape, a.dtype) for a in lands),
        in_specs=[HBM] * (2 * nt) + [SEM, SEM, ANY], out_specs=tuple([HBM] * nt),
        input_output_aliases={nt + t: t for t in range(nt)}, name=name, **SPLIT_CALL)(
            *[_in_hbm(shards[n]) for n in BIG], *lands, send_sems, recv_sems, after)


def gather_forward(shards, lands, layer, *, name):
    nt = len(BIG)

    def body(*refs):
        p, o = dict(zip(BIG, refs[:nt])), dict(zip(BIG, refs[2 * nt:3 * nt]))
        ss, rs = refs[3 * nt:]
        x, y, c, chips = _place()
        for core in (0, 1):
            @pl.when(c == core)
            def _():
                me, sibling = (x, y, core), (x, y, 1 - core)
                sends = _pull_arrivals(ss, rs, o, core, x, y, chips, sibling)
                sends += [_remote(ss, rs, 3 * nt + t, p[n].at[layer], _shard_view(o[n], 0, n, 2 * x + y), sibling)
                          for t, n in enumerate(BIG)]
                for cp in sends:
                    cp.start()
                for cp in sends:
                    cp.wait_send()
                for cp in _pull_arrivals(ss, rs, o, 1 - core, x, y, chips, me):
                    cp.wait_recv()
                for t, n in enumerate(BIG):
                    own = _shard_view(o[n], 0, n, 2 * x + y)
                    _remote(ss, rs, 3 * nt + t, own, own, me).wait_recv()

    outs = pl.pallas_call(
        body, out_shape=[_sds(a.shape, a.dtype) for a in lands], in_specs=[ANY] * (2 * nt), out_specs=[ANY] * nt,
        input_output_aliases={nt + t: t for t in range(nt)}, scratch_shapes=_dma_sems(4 * nt), name=name)(
            *[shards[n] for n in BIG], *lands)
    return dict(zip(BIG, outs))


def pair_exchange(g, *, name):
    hh = g[BIG[0]].shape[0] // 2
    nt = len(BIG)

    def body(*refs):
        send_sems, recv_sems = refs[2 * nt:]
        x, y, c, _ = _place()
        copies = [_remote(send_sems, recv_sems, t, refs[t].at[pl.ds(hh * (1 - c), hh)], refs[nt + t], (x, y, 1 - c))
                  for t in range(nt)]
        for cp in copies:
            cp.start()
        for cp in copies:
            cp.wait()

    outs = pl.pallas_call(
        body, out_shape=[_sds((hh,) + g[n].shape[1:], g[n].dtype) for n in BIG], in_specs=[ANY] * nt, out_specs=[ANY] * nt,
        scratch_shapes=_dma_sems(nt), name=name)(*[g[n] for n in BIG])
    return dict(zip(BIG, outs))


def _chip_copies(send_sems, recv_sems, s_refs, land_refs, c, chips):
    hl = s_refs[0].shape[0]
    return [_remote(send_sems, recv_sems, 3 * t + j, _shard_view(s_refs[t], pl.ds(0, hl), n, 2 * chip[0] + chip[1]),
                    land_refs[t].at[j], (*chip, c))
            for t, n in enumerate(BIG) for j, chip in enumerate(chips)]


def _landing_shapes(s):
    hl = s[BIG[0]].shape[0]
    return [_sds((3, hl) + _shard_shape(n), s[n].dtype) for n in BIG]


def chip_exchange(s, *, name):
    nt = len(BIG)

    def body(*refs):
        send_sems, recv_sems = refs[2 * nt:]
        x, y, c, chips = _place()
        copies = _chip_copies(send_sems, recv_sems, refs[:nt], refs[nt:2 * nt], c, chips)
        for cp in copies:
            cp.start()
        for cp in copies:
            cp.wait()

    outs = pl.pallas_call(
        body, out_shape=_landing_shapes(s), in_specs=[ANY] * nt, out_specs=[ANY] * nt,
        scratch_shapes=_dma_sems(3 * nt), name=name)(*[s[n] for n in BIG])
    return dict(zip(BIG, outs))


def chip_exchange_start(s, *, name, after=None):
    nt = len(BIG)
    first = [] if after is None else [after]

    def body(*refs):
        o = refs[2 * nt + len(first):]
        x, y, c, chips = _place()
        for cp in _chip_copies(o[0], o[1], refs[:nt], refs[nt:2 * nt], c, chips):
            cp.start()
        refs[-1][...] = jnp.zeros_like(refs[-1])

    lands = [_in_hbm(lax.empty(d.shape, d.dtype)) for d in _landing_shapes(s)]
    srcs = [_in_hbm(s[n]) for n in BIG]
    outs = pl.pallas_call(
        body,
        out_shape=(pltpu.SemaphoreType.DMA((3 * nt,)), pltpu.SemaphoreType.DMA((3 * nt,)),
                   *[pltpu.HBM(a.shape, a.dtype) for a in srcs + lands], _sds((8, HD), F32)),
        in_specs=[HBM] * (2 * nt) + [ANY] * len(first), out_specs=(SEM, SEM, *[HBM] * (2 * nt), VM),
        input_output_aliases={t: 2 + t for t in range(2 * nt)}, name=name, **SPLIT_CALL)(*srcs, *lands, *first)
    return outs[0], outs[1], outs[2:2 + nt], outs[2 + nt:2 + 2 * nt], outs[-1]


def chip_exchange_wait(send_sems, recv_sems, srcs, lands, after, *, name):
    nt = len(BIG)

    def body(*refs):
        x, y, c, chips = _place()
        for cp in _chip_copies(refs[2 * nt], refs[2 * nt + 1], refs[:nt], refs[nt:2 * nt], c, chips):
            cp.wait_send()
            cp.wait_recv()

    outs = pl.pallas_call(
        body, out_shape=tuple(pltpu.HBM(a.shape, a.dtype) for a in list(srcs) + list(lands)),
        in_specs=[HBM] * (2 * nt) + [SEM, SEM, ANY], out_specs=tuple([HBM] * (2 * nt)),
        input_output_aliases={t: t for t in range(2 * nt)},
        compiler_params=pltpu.CompilerParams(has_side_effects=pltpu.SideEffectType.DATAFLOW_SIDE_EFFECTING),
        name=name)(*srcs, *lands, send_sems, recv_sems, after)
    return dict(zip(BIG, outs[:nt])), dict(zip(BIG, outs[nt:]))


def pair_share(f, l0, hh, *, name):
    nt = len(BIG)

    def body(*refs):
        o = refs[nt:2 * nt]
        send_sems, recv_sems = refs[2 * nt:]
        x, y, c, _ = _place()
        mine, theirs = pl.ds(l0 + hh * c, hh), pl.ds(l0 + hh * (1 - c), hh)
        copies = [_remote(send_sems, recv_sems, t, o[t].at[mine], o[t].at[mine], (x, y, 1 - c)) for t in range(nt)]
        for cp in copies:
            cp.start()
        for t, cp in enumerate(copies):
            cp.wait_send()
            _remote(send_sems, recv_sems, t, o[t].at[theirs], o[t].at[theirs], (x, y, c)).wait_recv()

    outs = pl.pallas_call(
        body, out_shape=[_sds(f[n].shape, f[n].dtype) for n in BIG], in_specs=[ANY] * nt, out_specs=[ANY] * nt,
        input_output_aliases={t: t for t in range(nt)}, scratch_shapes=_dma_sems(nt), name=name)(*[f[n] for n in BIG])
    return dict(zip(BIG, outs))


def pair_add(g, r, idx, *, tensor, name):
    _, _, full, blk = TENSORS[tensor]
    hh = r.shape[0]

    def body(idx_ref, g_ref, r_ref, o_ref):
        o_ref[...] = (g_ref[...].astype(F32) + r_ref[...].astype(F32)).astype(o_ref.dtype)

    own = pl.BlockSpec((None,) + blk, lambda *a: (a[0],) + a[1:-1])
    return pl.pallas_call(
        body, out_shape=_sds(r.shape, r.dtype),
        grid_spec=pltpu.PrefetchScalarGridSpec(
            num_scalar_prefetch=1, grid=(hh,) + tuple(f // b for f, b in zip(full, blk)),
            in_specs=[pl.BlockSpec((None,) + blk, lambda *a: (hh * a[-1][0] + a[0],) + a[1:-1]), own], out_specs=own),
        compiler_params=_cparams(("parallel",) * (1 + len(full))), name=name)(idx, g, r)


def chip_add(s, r, idx, totals, l0, *, tensor, name):
    kind, width, full, _ = TENSORS[tensor]
    shard = _shard_shape(tensor)
    hh = s.shape[0]
    zeros = (0,) * len(shard)

    def body(idx_ref, s_ref, r0_ref, r1_ref, r2_ref, t_ref, o_ref):
        o_ref[...] = ((s_ref[...].astype(F32) + r0_ref[...].astype(F32)) + r1_ref[...].astype(F32)) + r2_ref[...].astype(F32)

    if kind == "lead":
        mine = pl.BlockSpec((None, None) + shard, lambda i, ix: (i, ix[1]) + zeros)
    elif kind == "row":
        mine = pl.BlockSpec((None,) + shard, lambda i, ix: (i, ix[1]) + zeros[1:])
    else:
        mine = pl.BlockSpec((None,) + shard, lambda i, ix: (i,) + zeros[1:] + (ix[1],))
    peer = lambda j: pl.BlockSpec((None, None) + shard, lambda i, ix: (j, i) + zeros)
    return pl.pallas_call(
        body, out_shape=_sds(totals.shape, F32),
        grid_spec=pltpu.PrefetchScalarGridSpec(
            num_scalar_prefetch=1, grid=(hh,), in_specs=[mine, peer(0), peer(1), peer(2), pl.BlockSpec(memory_space=pl.ANY)],
            out_specs=pl.BlockSpec((None,) + shard, lambda i, ix: (l0 + hh * ix[0] + i,) + zeros)),
        input_output_aliases={5: 0}, compiler_params=_cparams(("parallel",)), name=name)(idx, s, r, r, r, totals)


def _flat_rows(arrs):
    v = jnp.concatenate([a.reshape(-1) for a in arrs])
    n = -(-v.shape[0] // 1024) * 1024
    return jnp.pad(v, (0, n - v.shape[0])).reshape(n // HD, HD)


def _unflat(buf, shapes):
    v, out, o = buf.reshape(-1), [], 0
    for s in shapes:
        n = int(np.prod(s))
        out.append(v[o:o + n].reshape(s))
        o += n
    return out


WEIGHTS = ("norm1_g", "norm2_g", "w_ada", "b_ada", "w_in", "w_gla_a2", "b_gla_a", "b_fox_f", "ret_norm_g", "gla_norm_g",
           "q_norm_g", "k_norm_g", "w_br", "w_mg", "b_mg", "w_o", "w_up", "w_conv", "b_conv", "w_down")
REPLICATED = ("norm1_g", "norm2_g", "b_gla_a", "b_fox_f", "ret_norm_g", "gla_norm_g", "q_norm_g", "k_norm_g", "b_mg", "b_conv")
ADAM_BLOCKS = dict(w_ada=(1, 256, 1536), w_in=(1, 256, 1285), w_br=(1, 3, BW, 256), w_mg=(1, 512, 768), w_o=(2, 256, D),
                   w_up=(1, 256, 1408), w_down=(1, 352, D))
ALL_AXES = ("x", "y", "c")


def kernel(x, c, norm1_g, norm2_g, w_ada, b_ada, w_in, w_gla_a2, b_gla_a, b_fox_f, ret_norm_g, gla_norm_g, q_norm_g, k_norm_g, w_br, w_mg, b_mg, w_o, w_up, w_conv, b_conv, w_down, loss_target, m_norm1_g, m_norm2_g, m_w_ada, m_b_ada, m_w_in, m_w_gla_a2, m_b_gla_a, m_b_fox_f, m_ret_norm_g, m_gla_norm_g, m_q_norm_g, m_k_norm_g, m_w_br, m_w_mg, m_b_mg, m_w_o, m_w_up, m_w_conv, m_b_conv, m_w_down, v_norm1_g, v_norm2_g, v_w_ada, v_b_ada, v_w_in, v_w_gla_a2, v_b_gla_a, v_b_fox_f, v_ret_norm_g, v_gla_norm_g, v_q_norm_g, v_k_norm_g, v_w_br, v_w_mg, v_b_mg, v_w_o, v_w_up, v_w_conv, v_b_conv, v_w_down):
    w = dict(zip(WEIGHTS, (norm1_g, norm2_g, w_ada, b_ada, w_in, w_gla_a2, b_gla_a, b_fox_f, ret_norm_g, gla_norm_g,
                           q_norm_g, k_norm_g, w_br, w_mg, b_mg, w_o, w_up, w_conv, b_conv, w_down)))
    m = dict(zip(WEIGHTS, (m_norm1_g, m_norm2_g, m_w_ada, m_b_ada, m_w_in, m_w_gla_a2, m_b_gla_a, m_b_fox_f, m_ret_norm_g,
                           m_gla_norm_g, m_q_norm_g, m_k_norm_g, m_w_br, m_w_mg, m_b_mg, m_w_o, m_w_up, m_w_conv, m_b_conv,
                           m_w_down)))
    v = dict(zip(WEIGHTS, (v_norm1_g, v_norm2_g, v_w_ada, v_b_ada, v_w_in, v_w_gla_a2, v_b_gla_a, v_b_fox_f, v_ret_norm_g,
                           v_gla_norm_g, v_q_norm_g, v_k_norm_g, v_w_br, v_w_mg, v_b_mg, v_w_o, v_w_up, v_w_conv, v_b_conv,
                           v_w_down)))
    nl = norm1_g.shape[0]
    seq = x.shape[1]
    xi, yi, ci = lax.axis_index("x"), lax.axis_index("y"), lax.axis_index("c")
    k_me = 2 * xi + yi
    b_me = 4 * xi + 2 * yi + ci
    ada_n = w_ada.shape[2]
    a2_n, conv_n = w_gla_a2.shape[2], w_conv.shape[2]

    shards = {n: w[n].astype(MMT) for n in BIG}
    started = gather_start(shards, 0, name="gather0_start")

    def gather_finish(l, started, after):
        send_sems, recv_sems, lands, _ = started
        lands = gather_wait(send_sems, recv_sems, shards, lands, after, l, name=f"gather{l}_wait")
        big = gather_forward(shards, lands, l, name=f"gather{l}_forward")
        big["w1"] = build_w1(big["w_in"], big["w_mg"])
        return big

    blk = _flat_rows([c, w_gla_a2, w_conv]) + started[-1][0, 0]
    g1 = small_allgather(blk, name="gather_small").reshape(8, blk.shape[0], HD)
    c_all = g1[:, :D // HD].reshape(8, D)
    by_chip = g1[0::2].reshape(4, -1)[:, D:]
    a2_sh, conv_sh = by_chip[:, :nl * GLR * a2_n], by_chip[:, nl * GLR * a2_n:nl * (GLR * a2_n + 3 * conv_n)]
    full_small = dict(
        w_gla_a2=a2_sh.reshape(4, nl, GLR, a2_n).transpose(1, 2, 0, 3).reshape(nl, GLR, 4 * a2_n),
        w_conv=conv_sh.reshape(4, nl, 3, conv_n).transpose(1, 2, 0, 3).reshape(nl, 3, 4 * conv_n))

    b_ada_sh = lax.dynamic_slice_in_dim(b_ada, k_me * ada_n, ada_n, axis=1)[:, None, :]
    mod_sh = ada_mod(c_all, w_ada, b_ada_sh)
    g2 = small_allgather(mod_sh.reshape(nl * 8, ada_n), name="gather_mod").reshape(4, 2, nl, 8, ada_n)[:, 0]
    mod_me = lax.dynamic_index_in_dim(g2, b_me, axis=2, keepdims=False).transpose(1, 0, 2).reshape(nl, 4 * ada_n)

    wsmall = {n: w[n] for n in REPLICATED}
    wsmall.update(full_small)
    mods = [[mod_me[l:l + 1, i * D:(i + 1) * D] for i in range(6)] for l in range(nl)]
    big = gather_finish(0, started, mod_me)

    cosf, sinf = _rope_tables(seq)
    xs, saved, params = x[0], [], []
    for l in range(nl):
        if l + 1 < nl:
            started = gather_start(shards, l + 1, name=f"gather{l + 1}_start")
            xs = xs + started[-1][0, 0]
        params.append(layer_params(wsmall, big, l))
        xs, sv = layer_fwd(xs, mods[l], params[l], cosf, sinf)
        saved.append(sv)
        if l + 1 < nl:
            big = gather_finish(l + 1, started, xs)
    loss_part, dx = loss_and_grad(xs, loss_target[0], name="loss")
    loss = lax.psum(loss_part[0, 0], ALL_AXES)
    grads, dmods = [None] * nl, [None] * nl
    idx = jnp.stack([ci, k_me]).astype(jnp.int32)
    totals = {n: lax.empty((nl,) + _shard_shape(n), F32) for n in BIG}

    def finish_group(group, chip_sum, from_chips, totals):
        totals = {n: chip_add(chip_sum[n], from_chips[n], idx, totals[n], RS_GROUP * group, tensor=n,
                              name=f"rs{group}_chip_add_{n}") for n in BIG}
        return pair_share(totals, RS_GROUP * group, RS_GROUP // 2, name=f"rs{group}_pair_share")

    pending = None
    for group in reversed(range(nl // RS_GROUP)):
        layers = range(RS_GROUP * group, RS_GROUP * (group + 1))
        stacks = {n: lax.empty((RS_GROUP,) + TENSORS[n][2], MMT) for n in BIG if n != "w_br"}
        if pending is not None:
            dx = dx + pending[-1][0, 0]
        for l in reversed(layers):
            dx, g, dmods[l], stacks = layer_bwd(dx, mods[l], params[l], saved[l], cosf, sinf, stacks, l - layers[0])
            grads[l] = layer_grads(g)
        stacks["w_br"] = jnp.stack([grads[l]["w_br"].astype(MMT) for l in layers])
        from_sibling = pair_exchange(stacks, name=f"rs{group}_pair_exchange")
        chip_sum = {n: pair_add(stacks[n], from_sibling[n], idx, tensor=n, name=f"rs{group}_pair_add_{n}") for n in BIG}
        if pending is not None:
            earlier, send_sems, recv_sems, srcs, lands, _ = pending
            sums, from_chips = chip_exchange_wait(send_sems, recv_sems, srcs, lands, dx,
                                                  name=f"rs{earlier}_chip_exchange_wait")
            totals = finish_group(earlier, sums, from_chips, totals)
        if group > 0:
            pending = (group, *chip_exchange_start(chip_sum, name=f"rs{group}_chip_exchange_start"))

    small_names = REPLICATED + ("w_gla_a2", "w_conv")
    small_shapes = [(nl, 6 * D)] + [(nl,) + grads[0][n].shape for n in small_names]
    vec = _flat_rows([jnp.concatenate(dmods, axis=0)] + [jnp.stack([grads[l][n] for l in range(nl)]) for n in small_names])
    gs = small_allgather(vec, name="gather_small_grads")
    pending = (0, *chip_exchange_start(chip_sum, name="rs0_chip_exchange_start", after=gs))
    gs = (gs + pending[-1][0, 0]).reshape(8, vec.shape[0], HD)
    summed = _unflat(sum_devices(gs), small_shapes)
    grad = dict(zip(small_names, summed[1:]))
    grad["b_ada"] = summed[0]
    grad["w_gla_a2"] = lax.dynamic_slice_in_dim(grad["w_gla_a2"], k_me * a2_n, a2_n, axis=2)
    grad["w_conv"] = lax.dynamic_slice_in_dim(grad["w_conv"], k_me * conv_n, conv_n, axis=2)
    dmod_all = gs[:, :nl * 6 * D // HD].reshape(8, nl, 6 * D)
    dmod_sh = lax.dynamic_slice_in_dim(dmod_all, k_me * ada_n, ada_n, axis=2).transpose(1, 0, 2)
    grad["w_ada"] = ada_dw(c_all, dmod_sh)

    delta, new_m, new_v = {}, {}, {}
    delta["w_ada"], new_m["w_ada"], new_v["w_ada"] = adamw(w["w_ada"], grad["w_ada"], m["w_ada"], v["w_ada"],
                                                          block=ADAM_BLOCKS["w_ada"], name="adamw_w_ada")
    rest = [n for n in WEIGHTS if n not in ADAM_BLOCKS]
    shapes = [w[n].shape for n in rest]
    flat = [_flat_rows([t[n] for n in rest]) for t in (w, grad, m, v)]
    outs = adamw(*flat, block=flat[0].shape, name="adamw_small")
    for t, o in zip((delta, new_m, new_v), outs):
        t.update(zip(rest, _unflat(o, shapes)))

    later = {n: adamw(w[n], totals[n], m[n], v[n], block=ADAM_BLOCKS[n], name="adamw_later_" + n, rows=(RS_GROUP, nl))
             for n in BIG} if nl > RS_GROUP else {}
    earlier, send_sems, recv_sems, srcs, lands, _ = pending
    sums, from_chips = chip_exchange_wait(send_sems, recv_sems, srcs, lands, later[BIG[-1]][0] if later else outs[0],
                                          name=f"rs{earlier}_chip_exchange_wait")
    grad.update(finish_group(earlier, sums, from_chips, totals))
    for n in BIG:
        delta[n], new_m[n], new_v[n] = adamw(w[n], grad[n], m[n], v[n], block=ADAM_BLOCKS[n], name="adamw_" + n,
                                             rows=(0, min(RS_GROUP, nl)), into=later.get(n))

    return (loss, dx[None], *[grad[n] for n in WEIGHTS], *[delta[n] for n in WEIGHTS], *[new_m[n] for n in WEIGHTS],
            *[new_v[n] for n in WEIGHTS])
```

```python
import functools

import numpy as np
import jax
import jax.numpy as jnp
from jax import lax
from jax.experimental import pallas as pl
from jax.experimental.pallas import tpu as pltpu

F32 = jnp.float32
MMT = jnp.bfloat16
HI = lax.Precision.HIGHEST

D = 1024
DEPTH = 4
NH = 4
HD = 128
BW = NH * HD
CH = 64
GDK = 64
GLR = 16
DFF = 2816
EPS = 1e-6
ROPE_BASE = 10000.0

GP, RG, GG, FV, RQ, RK, RV, GQ, GK, GV, FQ, FK, LR, FF = (
    0, 3072, 3584, 4096, 4608, 5120, 5632, 6144, 6400, 6656, 7168, 7680, 8192, 8320)
NZZ = 8448
WZ0 = 3072
IN_W = 5140
W_IN_COLS = dict(rqkv=(0, 1536), rg=(1536, 2048), gqkv=(2048, 3072), lr=(3072, 3088), gg=(3088, 3600), fqk=(3600, 4624),
                 fv=(4624, 5136), ff=(5136, 5140))

VMEM_LIMIT = 56 * 1024 * 1024

ADAM_LR, ADAM_B1, ADAM_B2, ADAM_EPS, ADAM_WD, ADAM_STEP = 0.001, 0.9, 0.999, 1e-08, 0.01, 10


def _cparams(sem=None):
    return pltpu.CompilerParams(dimension_semantics=sem, vmem_limit_bytes=VMEM_LIMIT)


def _sds(shape, dtype):
    return jax.ShapeDtypeStruct(tuple(shape), dtype)


def _dot(a, b, precision=None):
    return lax.dot_general(a, b, (((1,), (0,)), ((), ())), precision=precision, preferred_element_type=F32)


def _dot_nt(a, b, precision=None):
    return lax.dot_general(a, b, (((1,), (1,)), ((), ())), precision=precision, preferred_element_type=F32)


def _dot_tn(a, b, precision=None):
    return lax.dot_general(a, b, (((0,), (0,)), ((), ())), precision=precision, preferred_element_type=F32)


def _silu(x):
    return x * jax.nn.sigmoid(x)


def _log_sigmoid(x):
    return jnp.minimum(x, 0.0) - jnp.log(1.0 + jnp.exp(jnp.minimum(x, -x)))


@jax.custom_vjp
def _swap_halves(x):
    return pltpu.roll(x, HD // 2, 1)


_swap_halves.defvjp(lambda x: (_swap_halves(x), None), lambda _, g: (_swap_halves(g),))


@jax.custom_vjp
def _bdot(a, b):
    return _dot(a.astype(MMT), b.astype(MMT))


@jax.custom_vjp
def _bdot_nt(a, b):
    return _dot_nt(a.astype(MMT), b.astype(MMT))


@jax.custom_vjp
def _bdot_tn(a, b):
    return _dot_tn(a.astype(MMT), b.astype(MMT))


_bdot.defvjp(lambda a, b: (_bdot(a, b), (a, b)), lambda r, g: (_bdot_nt(g, r[1]), _bdot_tn(r[0], g)))
_bdot_nt.defvjp(lambda a, b: (_bdot_nt(a, b), (a, b)), lambda r, g: (_bdot(g, r[1]), _bdot_tn(g, r[0])))
_bdot_tn.defvjp(lambda a, b: (_bdot_tn(a, b), (a, b)), lambda r, g: (_bdot_nt(r[1], g), _bdot(r[0], g)))


def _stacked(blk, idx, layer):
    if layer is None:
        return pl.BlockSpec(blk, idx)
    return pl.BlockSpec((None,) + blk, lambda i, j: (layer,) + idx(i, j))


def mm_nn(a, b, *, tm, tn, out_dtype, name, layer=None):
    m, k = a.shape
    n = b.shape[-1]

    def body(a_ref, b_ref, o_ref):
        o_ref[...] = _dot(a_ref[...], b_ref[...]).astype(o_ref.dtype)

    return pl.pallas_call(
        body, grid=(m // tm, n // tn),
        in_specs=[pl.BlockSpec((tm, k), lambda i, j: (i, 0)), _stacked((k, tn), lambda i, j: (0, j), layer)],
        out_specs=pl.BlockSpec((tm, tn), lambda i, j: (i, j)),
        out_shape=_sds((m, n), out_dtype), compiler_params=_cparams(("parallel", "parallel")), name=name)(a, b)


def mm_nn_residual(a, b, res, gate, *, tm, tn, name, layer=None):
    m, k = a.shape
    n = b.shape[-1]

    def body(a_ref, b_ref, r_ref, g_ref, x_ref, y_ref):
        acc = _dot(a_ref[...], b_ref[...])
        y_ref[...] = acc
        x_ref[...] = r_ref[...] + g_ref[...] * acc

    return pl.pallas_call(
        body, grid=(m // tm, n // tn),
        in_specs=[pl.BlockSpec((tm, k), lambda i, j: (i, 0)), _stacked((k, tn), lambda i, j: (0, j), layer),
                  pl.BlockSpec((tm, tn), lambda i, j: (i, j)), pl.BlockSpec((1, tn), lambda i, j: (0, j))],
        out_specs=[pl.BlockSpec((tm, tn), lambda i, j: (i, j)), pl.BlockSpec((tm, tn), lambda i, j: (i, j))],
        out_shape=[_sds((m, n), F32), _sds((m, n), F32)],
        compiler_params=_cparams(("parallel", "parallel")), name=name)(a, b, res, gate)


def mm_nt(a, b, *, tm, tn, out_dtype, name, layer=None):
    m, k = a.shape
    n = b.shape[-2]

    def body(a_ref, b_ref, o_ref):
        o_ref[...] = _dot_nt(a_ref[...], b_ref[...]).astype(o_ref.dtype)

    return pl.pallas_call(
        body, grid=(m // tm, n // tn),
        in_specs=[pl.BlockSpec((tm, k), lambda i, j: (i, 0)), _stacked((tn, k), lambda i, j: (j, 0), layer)],
        out_specs=pl.BlockSpec((tm, tn), lambda i, j: (i, j)),
        out_shape=_sds((m, n), out_dtype), compiler_params=_cparams(("parallel", "parallel")), name=name)(a, b)


def mm_tn(a, b, *, tm, tn, out_dtype, name, col0=0, ncols=None, stack=None, layer=None):
    s, m = a.shape
    n = b.shape[1] - col0 if ncols is None else ncols
    c0 = col0 // tn

    def body(a_ref, b_ref, *rest):
        o_ref = rest[-1]
        o_ref[...] = _dot_tn(a_ref[...], b_ref[...]).astype(o_ref.dtype)

    in_specs = [pl.BlockSpec((s, tm), lambda i, j: (0, i)), pl.BlockSpec((s, tn), lambda i, j: (0, c0 + j))]
    if stack is None:
        return pl.pallas_call(
            body, grid=(m // tm, n // tn), in_specs=in_specs, out_specs=pl.BlockSpec((tm, tn), lambda i, j: (i, j)),
            out_shape=_sds((m, n), out_dtype), compiler_params=_cparams(("parallel", "parallel")), name=name)(a, b)
    return pl.pallas_call(
        body, grid=(m // tm, n // tn), in_specs=in_specs + [pl.BlockSpec(memory_space=pl.ANY)],
        out_specs=pl.BlockSpec((None, tm, tn), lambda i, j: (layer, i, j)),
        out_shape=_sds(stack.shape, stack.dtype), input_output_aliases={2: 0},
        compiler_params=_cparams(("parallel", "parallel")), name=name)(a, b, stack)


def _row_tile(s):
    return min(256, s)


def _norm_mod_f(x, g, scale, shift):
    r = lax.rsqrt(jnp.mean(x * x, axis=-1, keepdims=True) + EPS)
    return (x * r * g) * (1.0 + scale) + shift


def norm_mod(x, g, scale, shift, *, name):
    s = x.shape[0]
    t = _row_tile(s)

    def body(x_ref, g_ref, sc_ref, sh_ref, o_ref):
        o_ref[...] = _norm_mod_f(x_ref[...], g_ref[...], sc_ref[...], sh_ref[...]).astype(o_ref.dtype)

    vec = pl.BlockSpec((1, D), lambda i: (0, 0))
    return pl.pallas_call(
        body, grid=(s // t,), in_specs=[pl.BlockSpec((t, D), lambda i: (i, 0)), vec, vec, vec],
        out_specs=pl.BlockSpec((t, D), lambda i: (i, 0)), out_shape=_sds((s, D), MMT),
        compiler_params=_cparams(("parallel",)), name=name)(x, g, scale, shift)


def norm_mod_bwd(x, dh, dres, g, scale, shift, *, name):
    s = x.shape[0]
    t = _row_tile(s)

    def body(x_ref, dh_ref, dr_ref, g_ref, sc_ref, sh_ref, dx_ref, dg_ref, dsc_ref, dsh_ref):
        @pl.when(pl.program_id(0) == 0)
        def _():
            dg_ref[...] = jnp.zeros_like(dg_ref)
            dsc_ref[...] = jnp.zeros_like(dsc_ref)
            dsh_ref[...] = jnp.zeros_like(dsh_ref)

        _, vjp = jax.vjp(_norm_mod_f, x_ref[...], g_ref[...], sc_ref[...], sh_ref[...])
        dx, dg, dsc, dsh = vjp(dh_ref[...])
        dx_ref[...] = dr_ref[...] + dx
        dg_ref[...] += dg
        dsc_ref[...] += dsc
        dsh_ref[...] += dsh

    row = pl.BlockSpec((t, D), lambda i: (i, 0))
    vec = pl.BlockSpec((1, D), lambda i: (0, 0))
    return pl.pallas_call(
        body, grid=(s // t,), in_specs=[row, row, row, vec, vec, vec], out_specs=[row, vec, vec, vec],
        out_shape=[_sds((s, D), F32)] + [_sds((1, D), F32)] * 3,
        compiler_params=_cparams(("arbitrary",)), name=name)(x, dh, dres, g, scale, shift)


def gate_bwd(dx, y, gate, *, name):
    s = dx.shape[0]
    t = _row_tile(s)

    def body(dx_ref, y_ref, g_ref, dy_ref, dg_ref):
        @pl.when(pl.program_id(0) == 0)
        def _():
            dg_ref[...] = jnp.zeros_like(dg_ref)

        dxv = dx_ref[...]
        dy_ref[...] = (g_ref[...] * dxv).astype(dy_ref.dtype)
        dg_ref[...] += jnp.sum(dxv * y_ref[...], axis=0, keepdims=True)

    row = pl.BlockSpec((t, D), lambda i: (i, 0))
    vec = pl.BlockSpec((1, D), lambda i: (0, 0))
    return pl.pallas_call(
        body, grid=(s // t,), in_specs=[row, row, vec], out_specs=[row, vec],
        out_shape=[_sds((s, D), MMT), _sds((1, D), F32)],
        compiler_params=_cparams(("arbitrary",)), name=name)(dx, y, gate)


def loss_and_grad(xf, target, *, name):
    s = xf.shape[0]
    t = _row_tile(s)

    def body(x_ref, t_ref, l_ref, dx_ref):
        @pl.when(pl.program_id(0) == 0)
        def _():
            l_ref[...] = jnp.zeros_like(l_ref)

        e = x_ref[...] - t_ref[...]
        dx_ref[...] = e * (1.0 / D)
        l_ref[...] += 0.5 * jnp.sum(jnp.sum(e * e, axis=1, keepdims=True), axis=0, keepdims=True) * (1.0 / D)

    row = pl.BlockSpec((t, D), lambda i: (i, 0))
    return pl.pallas_call(
        body, grid=(s // t,), in_specs=[row, row], out_specs=[pl.BlockSpec((1, 1), lambda i: (0, 0)), row],
        out_shape=[_sds((1, 1), F32), _sds((s, D), F32)],
        compiler_params=_cparams(("arbitrary",)), name=name)(xf, target)


def _ret_consts():
    log_g = np.log1p(-np.exp2(-5.0 - np.arange(NH, dtype=np.float32))).astype(np.float32)
    idx = np.arange(CH, dtype=np.float32)
    d_intra = np.exp(np.abs(idx[:, None] - idx[None, :])[None] * log_g[:, None, None]).astype(np.float32)
    k_w = np.exp((CH - 1.0 - idx)[None, :] * log_g[:, None]).astype(np.float32)
    q_w = np.exp((idx + 1.0)[None, :] * log_g[:, None]).astype(np.float32)
    g_chunk = [float(v) for v in np.exp(np.float32(CH) * log_g).astype(np.float32)]
    bc = lambda a: np.ascontiguousarray(np.broadcast_to(a[:, :, None], (NH, CH, HD)))
    return jnp.asarray(d_intra), jnp.asarray(bc(k_w)), jnp.asarray(bc(q_w)), g_chunk


def _rope_tables(s):
    half = HD // 2
    inv_freq = (ROPE_BASE ** (-np.arange(half, dtype=np.float64) / half)).astype(np.float32)
    ang = (np.arange(s, dtype=np.float32)[:, None] * inv_freq[None, :]).astype(np.float64)
    cos, sin = np.cos(ang).astype(np.float32), np.sin(ang).astype(np.float32)
    return jnp.asarray(np.concatenate([cos, cos], axis=1)), jnp.asarray(np.concatenate([-sin, sin], axis=1))


def _ret_chunk(qs, ks, vs, rs, cos, sin, dintra, kw, qw, g_chunk):
    outs, rn = [], []
    for h in range(NH):
        q = qs[h] * cos + _swap_halves(qs[h]) * sin
        k = (ks[h] * cos + _swap_halves(ks[h]) * sin) * (HD ** -0.5)
        sc = _bdot_nt(q, k) * dintra[h]
        outs.append(_bdot(sc, vs[h]) + _bdot(q * qw[h], rs[h]))
        rn.append(g_chunk[h] * rs[h] + _bdot_tn(k * kw[h], vs[h]))
    return outs, rn


def _heads(x):
    return [x[:, h * HD:(h + 1) * HD] for h in range(NH)]


def retention_fwd(zz, cosf, sinf, *, name):
    s = zz.shape[0]
    n = s // CH
    dintra, kw, qw, g_chunk = _ret_consts()

    def body(q_ref, k_ref, v_ref, c_ref, s_ref, di_ref, kw_ref, qw_ref, o_ref, rp_ref, r_scr):
        @pl.when(pl.program_id(0) == 0)
        def _():
            r_scr[...] = jnp.zeros_like(r_scr)

        rprev = r_scr[...]
        rp_ref[0] = rprev
        outs, rn = _ret_chunk(_heads(q_ref[...]), _heads(k_ref[...]), _heads(v_ref[...]),
                              [rprev[h * HD:(h + 1) * HD] for h in range(NH)], c_ref[...], s_ref[...],
                              [di_ref[h] for h in range(NH)], [kw_ref[h] for h in range(NH)],
                              [qw_ref[h] for h in range(NH)], g_chunk)
        o_ref[...] = jnp.concatenate(outs, axis=1)
        r_scr[...] = jnp.concatenate(rn, axis=0)

    col = lambda c: pl.BlockSpec((CH, BW), lambda i: (i, c // BW))
    tab = pl.BlockSpec((CH, HD), lambda i: (i, 0))
    cst = lambda shp: pl.BlockSpec(shp, lambda i: (0,) * len(shp))
    return pl.pallas_call(
        body, grid=(n,),
        in_specs=[col(RQ), col(RK), col(RV), tab, tab, cst((NH, CH, CH)), cst((NH, CH, HD)), cst((NH, CH, HD))],
        out_specs=[pl.BlockSpec((CH, BW), lambda i: (i, 0)), pl.BlockSpec((1, BW, HD), lambda i: (i, 0, 0))],
        out_shape=[_sds((s, BW), F32), _sds((n, BW, HD), F32)],
        scratch_shapes=[pltpu.VMEM((BW, HD), F32)],
        compiler_params=_cparams(("arbitrary",)), name=name)(zz, zz, zz, cosf, sinf, dintra, kw, qw)


def retention_bwd(zz, cosf, sinf, rprev, do, dzz, *, name):
    s = zz.shape[0]
    n = s // CH
    dintra, kw, qw, g_chunk = _ret_consts()

    def body(q_ref, k_ref, v_ref, c_ref, s_ref, di_ref, kw_ref, qw_ref, rp_ref, do_ref, dzz_ref, dz_ref, dr_scr):
        @pl.when(pl.program_id(0) == 0)
        def _():
            dr_scr[...] = jnp.zeros_like(dr_scr)

        rprev_v = rp_ref[0]
        f = functools.partial(_ret_chunk, cos=c_ref[...], sin=s_ref[...],
                              dintra=[di_ref[h] for h in range(NH)], kw=[kw_ref[h] for h in range(NH)],
                              qw=[qw_ref[h] for h in range(NH)], g_chunk=g_chunk)
        _, vjp = jax.vjp(f, _heads(q_ref[...]), _heads(k_ref[...]), _heads(v_ref[...]),
                         [rprev_v[h * HD:(h + 1) * HD] for h in range(NH)])
        dr = dr_scr[...]
        dq, dk, dv, drp = vjp((_heads(do_ref[...]), [dr[h * HD:(h + 1) * HD] for h in range(NH)]))
        dz_ref[...] = jnp.concatenate(dq + dk + dv, axis=1).astype(dz_ref.dtype)
        dr_scr[...] = jnp.concatenate(drp, axis=0)

    col = lambda c: pl.BlockSpec((CH, BW), lambda i: (n - 1 - i, c // BW))
    tab = pl.BlockSpec((CH, HD), lambda i: (n - 1 - i, 0))
    cst = lambda shp: pl.BlockSpec(shp, lambda i: (0,) * len(shp))
    return pl.pallas_call(
        body, grid=(n,),
        in_specs=[col(RQ), col(RK), col(RV), tab, tab, cst((NH, CH, CH)), cst((NH, CH, HD)), cst((NH, CH, HD)),
                  pl.BlockSpec((1, BW, HD), lambda i: (n - 1 - i, 0, 0)), pl.BlockSpec((CH, BW), lambda i: (n - 1 - i, 0)),
                  pl.BlockSpec(memory_space=pl.ANY)],
        out_specs=pl.BlockSpec((CH, 3 * BW), lambda i: (n - 1 - i, RQ // (3 * BW))),
        out_shape=_sds(dzz.shape, dzz.dtype), input_output_aliases={10: 0},
        scratch_shapes=[pltpu.VMEM((BW, HD), F32)],
        compiler_params=_cparams(("arbitrary",)), name=name)(zz, zz, zz, cosf, sinf, dintra, kw, qw, rprev, do, dzz)


GKW = NH * GDK


def _gla_consts():
    tri = np.tril(np.ones((CH, CH), np.float32))
    mask_t = np.zeros((BW, GKW), np.float32)
    for h in range(NH):
        mask_t[h * HD:(h + 1) * HD, h * GDK:(h + 1) * GDK] = 1.0
    return jnp.asarray(tri), jnp.asarray(mask_t)


def _gla_chunk(q, k, v, glr, w_a2, b_a, st, tri, mask_t):
    la = _log_sigmoid(_bdot(glr, w_a2) + b_a) * (1.0 / 16.0)
    bc = _dot(tri, la, HI)
    be = jnp.sum(la, axis=0, keepdims=True)
    kv_t = _bdot_tn(v, k * jnp.exp(be - bc)) * mask_t
    sn = jnp.exp(be) * st + kv_t
    return _bdot_nt(q * (GDK ** -0.5), sn), sn


def gla_fwd(zz, w_a2p, b_a, *, name):
    s = zz.shape[0]
    n = s // CH
    tri, mask_t = _gla_consts()

    def body(q_ref, k_ref, v_ref, lr_ref, w_ref, b_ref, tri_ref, m_ref, o_ref, sp_ref, st_scr):
        @pl.when(pl.program_id(0) == 0)
        def _():
            st_scr[...] = jnp.zeros_like(st_scr)

        sp = st_scr[...]
        sp_ref[0] = sp
        o, sn = _gla_chunk(q_ref[...], k_ref[...], v_ref[...], lr_ref[...], w_ref[...], b_ref[...], sp,
                           tri_ref[...], m_ref[...])
        o_ref[...] = o
        st_scr[...] = sn

    cst = lambda shp: pl.BlockSpec(shp, lambda i: (0,) * len(shp))
    return pl.pallas_call(
        body, grid=(n,),
        in_specs=[pl.BlockSpec((CH, GKW), lambda i: (i, GQ // GKW)), pl.BlockSpec((CH, GKW), lambda i: (i, GK // GKW)),
                  pl.BlockSpec((CH, BW), lambda i: (i, GV // BW)), pl.BlockSpec((CH, HD), lambda i: (i, LR // HD)),
                  cst((HD, GKW)), cst((1, GKW)), cst((CH, CH)), cst((BW, GKW))],
        out_specs=[pl.BlockSpec((CH, BW), lambda i: (i, 0)), pl.BlockSpec((1, BW, GKW), lambda i: (i, 0, 0))],
        out_shape=[_sds((s, BW), F32), _sds((n, BW, GKW), F32)],
        scratch_shapes=[pltpu.VMEM((BW, GKW), F32)],
        compiler_params=_cparams(("arbitrary",)), name=name)(zz, zz, zz, zz, w_a2p, b_a, tri, mask_t)


def gla_bwd(zz, w_a2p, b_a, sprev, do, dzz, *, name):
    s = zz.shape[0]
    n = s // CH
    tri, mask_t = _gla_consts()

    def body(q_ref, k_ref, v_ref, lr_ref, w_ref, b_ref, tri_ref, m_ref, sp_ref, do_ref, dzz_ref,
             dz_ref, dlr_ref, dw_ref, db_ref, ds_scr):
        @pl.when(pl.program_id(0) == 0)
        def _():
            ds_scr[...] = jnp.zeros_like(ds_scr)
            dw_ref[...] = jnp.zeros_like(dw_ref)
            db_ref[...] = jnp.zeros_like(db_ref)

        f = functools.partial(_gla_chunk, tri=tri_ref[...], mask_t=m_ref[...])
        _, vjp = jax.vjp(f, q_ref[...], k_ref[...], v_ref[...], lr_ref[...], w_ref[...], b_ref[...], sp_ref[0])
        dq, dk, dv, dlr, dw, db, dsp = vjp((do_ref[...], ds_scr[...]))
        dz_ref[...] = jnp.concatenate([dq, dk, dv], axis=1).astype(dz_ref.dtype)
        dlr_ref[...] = dlr.astype(dlr_ref.dtype)
        dw_ref[...] += dw
        db_ref[...] += db
        ds_scr[...] = dsp

    cst = lambda shp: pl.BlockSpec(shp, lambda i: (0,) * len(shp))
    r = lambda i: n - 1 - i
    return pl.pallas_call(
        body, grid=(n,),
        in_specs=[pl.BlockSpec((CH, GKW), lambda i: (r(i), GQ // GKW)), pl.BlockSpec((CH, GKW), lambda i: (r(i), GK // GKW)),
                  pl.BlockSpec((CH, BW), lambda i: (r(i), GV // BW)), pl.BlockSpec((CH, HD), lambda i: (r(i), LR // HD)),
                  cst((HD, GKW)), cst((1, GKW)), cst((CH, CH)), cst((BW, GKW)),
                  pl.BlockSpec((1, BW, GKW), lambda i: (r(i), 0, 0)), pl.BlockSpec((CH, BW), lambda i: (r(i), 0)),
                  pl.BlockSpec(memory_space=pl.ANY)],
        out_specs=[pl.BlockSpec((CH, 2 * GKW + BW), lambda i: (r(i), GQ // (2 * GKW + BW))),
                   pl.BlockSpec((CH, HD), lambda i: (r(i), 0)), cst((HD, GKW)), cst((1, GKW))],
        out_shape=[_sds(dzz.shape, dzz.dtype), _sds((s, HD), MMT), _sds((HD, GKW), F32), _sds((1, GKW), F32)],
        input_output_aliases={10: 0}, scratch_shapes=[pltpu.VMEM((BW, GKW), F32)],
        compiler_params=_cparams(("arbitrary",)), name=name)(zz, zz, zz, zz, w_a2p, b_a, tri, mask_t, sprev, do, dzz)


def _fox_pre_f(fqs, fks, ff, gq, gk, bf):
    def rms(x, g):
        return x * lax.rsqrt(jnp.mean(x * x, axis=-1, keepdims=True) + EPS) * g

    qn = [rms(x, gq) * (HD ** -0.5) for x in fqs]
    kn = [rms(x, gk) for x in fks]
    return qn, kn, _log_sigmoid(ff + bf)


def fox_pre(zz, gq, gk, bf, *, name):
    s = zz.shape[0]
    t = _row_tile(s)
    tri = jnp.asarray(np.tril(np.ones((t, t), np.float32)))

    def body(q_ref, k_ref, f_ref, gq_ref, gk_ref, b_ref, tri_ref, qn_ref, kn_ref, cum_ref, carry):
        @pl.when(pl.program_id(0) == 0)
        def _():
            carry[...] = jnp.zeros_like(carry)

        qn, kn, lf = _fox_pre_f(_heads(q_ref[...]), _heads(k_ref[...]), f_ref[...], gq_ref[...], gk_ref[...], b_ref[...])
        qn_ref[...] = jnp.concatenate(qn, axis=1).astype(qn_ref.dtype)
        kn_ref[...] = jnp.concatenate(kn, axis=1).astype(kn_ref.dtype)
        cum_ref[...] = _dot(tri_ref[...], lf, HI) + carry[...]
        carry[...] += jnp.sum(lf, axis=0, keepdims=True)

    vec = pl.BlockSpec((1, HD), lambda i: (0, 0))
    return pl.pallas_call(
        body, grid=(s // t,),
        in_specs=[pl.BlockSpec((t, BW), lambda i: (i, FQ // BW)), pl.BlockSpec((t, BW), lambda i: (i, FK // BW)),
                  pl.BlockSpec((t, HD), lambda i: (i, FF // HD)), vec, vec, vec, pl.BlockSpec((t, t), lambda i: (0, 0))],
        out_specs=[pl.BlockSpec((t, BW), lambda i: (i, 0)), pl.BlockSpec((t, BW), lambda i: (i, 0)),
                   pl.BlockSpec((t, HD), lambda i: (i, 0))],
        out_shape=[_sds((s, BW), MMT), _sds((s, BW), MMT), _sds((s, HD), F32)],
        scratch_shapes=[pltpu.VMEM((1, HD), F32)],
        compiler_params=_cparams(("arbitrary",)), name=name)(zz, zz, zz, gq, gk, bf, tri)


def fox_pre_bwd(zz, gq, gk, bf, dqn, dkn, dcum, dzz, *, name):
    s = zz.shape[0]
    t = _row_tile(s)
    nt = s // t
    triu = jnp.asarray(np.triu(np.ones((t, t), np.float32)))

    def body(q_ref, k_ref, f_ref, gq_ref, gk_ref, b_ref, tri_ref, dqn_ref, dkn_ref, dcum_ref, dzz_ref,
             dz_ref, dff_ref, dgq_ref, dgk_ref, db_ref, carry):
        @pl.when(pl.program_id(0) == 0)
        def _():
            carry[...] = jnp.zeros_like(carry)
            dgq_ref[...] = jnp.zeros_like(dgq_ref)
            dgk_ref[...] = jnp.zeros_like(dgk_ref)
            db_ref[...] = jnp.zeros_like(db_ref)

        dcum_v = dcum_ref[...]
        dlf = _dot(tri_ref[...], dcum_v, HI) + carry[...]
        carry[...] += jnp.sum(dcum_v, axis=0, keepdims=True)
        _, vjp = jax.vjp(_fox_pre_f, _heads(q_ref[...]), _heads(k_ref[...]), f_ref[...], gq_ref[...], gk_ref[...], b_ref[...])
        dq, dk, dff, dgq, dgk, db = vjp((_heads(dqn_ref[...]), _heads(dkn_ref[...]), dlf))
        dz_ref[...] = jnp.concatenate(dq + dk, axis=1).astype(dz_ref.dtype)
        dff_ref[...] = dff.astype(dff_ref.dtype)
        dgq_ref[...] += dgq
        dgk_ref[...] += dgk
        db_ref[...] += db

    r = lambda i: nt - 1 - i
    vec = pl.BlockSpec((1, HD), lambda i: (0, 0))
    return pl.pallas_call(
        body, grid=(nt,),
        in_specs=[pl.BlockSpec((t, BW), lambda i: (r(i), FQ // BW)), pl.BlockSpec((t, BW), lambda i: (r(i), FK // BW)),
                  pl.BlockSpec((t, HD), lambda i: (r(i), FF // HD)), vec, vec, vec, pl.BlockSpec((t, t), lambda i: (0, 0)),
                  pl.BlockSpec((t, BW), lambda i: (r(i), 0)), pl.BlockSpec((t, BW), lambda i: (r(i), 0)),
                  pl.BlockSpec((t, HD), lambda i: (r(i), 0)), pl.BlockSpec(memory_space=pl.ANY)],
        out_specs=[pl.BlockSpec((t, 2 * BW), lambda i: (r(i), FQ // (2 * BW))), pl.BlockSpec((t, HD), lambda i: (r(i), 0)),
                   vec, vec, vec],
        out_shape=[_sds(dzz.shape, dzz.dtype), _sds((s, HD), MMT), _sds((1, HD), F32), _sds((1, HD), F32), _sds((1, HD), F32)],
        input_output_aliases={10: 0}, scratch_shapes=[pltpu.VMEM((1, HD), F32)],
        compiler_params=_cparams(("arbitrary",)), name=name)(zz, zz, zz, gq, gk, bf, triu, dqn, dkn, dcum, dzz)


def _fox_blocks(s):
    return min(256, s), min(512, s)


NEG = -1e30


def fox_attn_fwd(qn, kn, zz, cum_col, cum_row, *, name):
    s = qn.shape[0]
    bq, bk = _fox_blocks(s)

    def body(q_ref, k_ref, v_ref, cc_ref, cr_ref, o_ref, lse_ref):
        qi = pl.program_id(1)
        q = q_ref[...]
        cq = cc_ref[...]
        rows = qi * bq + lax.broadcasted_iota(jnp.int32, (bq, bk), 0)
        cols0 = lax.broadcasted_iota(jnp.int32, (bq, bk), 1)

        def step(j, carry, on_diagonal):
            m, l, acc = carry
            off = pl.multiple_of(j * bk, bk)
            k = k_ref[pl.ds(off, bk), :]
            v = v_ref[pl.ds(off, bk), :].astype(MMT)
            sc = _dot_nt(q, k) + cq - cr_ref[pl.ds(j, 1), :]
            if on_diagonal:
                sc = jnp.where(rows >= cols0 + j * bk, sc, NEG)
            m_new = jnp.maximum(m, jnp.max(sc, axis=1, keepdims=True))
            alpha = jnp.exp(m - m_new)
            p = jnp.exp(sc - m_new)
            return m_new, alpha * l + jnp.sum(p, axis=1, keepdims=True), alpha * acc + _dot(p.astype(MMT), v)

        nfull, nk = (qi * bq + 1) // bk, ((qi + 1) * bq + bk - 1) // bk
        carry = (jnp.full((bq, 1), NEG, F32), jnp.zeros((bq, 1), F32), jnp.zeros((bq, HD), F32))
        carry = lax.fori_loop(0, nfull, functools.partial(step, on_diagonal=False), carry)
        m, l, acc = lax.fori_loop(nfull, nk, functools.partial(step, on_diagonal=True), carry)
        o_ref[...] = acc / l
        lse_ref[...] = m + jnp.log(l)

    return pl.pallas_call(
        body, grid=(NH, s // bq),
        in_specs=[pl.BlockSpec((bq, HD), lambda h, i: (i, h)), pl.BlockSpec((s, HD), lambda h, i: (0, h)),
                  pl.BlockSpec((s, HD), lambda h, i: (0, FV // HD + h)),
                  pl.BlockSpec((None, bq, 1), lambda h, i: (h, i, 0)), pl.BlockSpec((None, s // bk, bk), lambda h, i: (h, 0, 0))],
        out_specs=[pl.BlockSpec((bq, HD), lambda h, i: (i, h)), pl.BlockSpec((None, bq, 1), lambda h, i: (h, i, 0))],
        out_shape=[_sds((s, BW), F32), _sds((NH, s, 1), F32)],
        compiler_params=_cparams(("parallel", "parallel")), name=name)(qn, kn, zz, cum_col, cum_row)


def fox_attn_bwd(qn, kn, zz, cum_col, cum_row, lse, do, dzz, *, name):
    s = qn.shape[0]
    bq, bk = _fox_blocks(s)
    nkc = s // bk

    def body(q_ref, k_ref, v_ref, cc_ref, cr_ref, lse_ref, do_ref, dzz_ref, dq_ref, dk_ref, dv_ref, dc_ref,
             p_scr, dp_scr, dv_scr):
        qi = pl.program_id(1)

        @pl.when(qi == 0)
        def _():
            dk_ref[...] = jnp.zeros_like(dk_ref)
            dv_scr[...] = jnp.zeros_like(dv_scr)
            dc_ref[...] = jnp.zeros_like(dc_ref)

        q = q_ref[...]
        dob = do_ref[...].astype(MMT)
        cq = cc_ref[...]
        lse_v = lse_ref[...]
        rows = qi * bq + lax.broadcasted_iota(jnp.int32, (bq, bk), 0)
        cols0 = lax.broadcasted_iota(jnp.int32, (bq, bk), 1)
        nfull, nk = (qi * bq + 1) // bk, ((qi + 1) * bq + bk - 1) // bk

        def probs(j, delta, on_diagonal):
            off = pl.multiple_of(j * bk, bk)
            sc = _dot_nt(q, k_ref[pl.ds(off, bk), :]) + cq - cr_ref[pl.ds(j, 1), :]
            p = jnp.exp(sc - lse_v)
            if on_diagonal:
                p = jnp.where(rows >= cols0 + j * bk, p, 0.0)
            dp = _dot_nt(dob, v_ref[pl.ds(off, bk), :].astype(MMT))
            p_scr[j] = p
            dp_scr[j] = dp
            return delta + jnp.sum(p * dp, axis=1, keepdims=True)

        delta = lax.fori_loop(0, nfull, functools.partial(probs, on_diagonal=False), jnp.zeros((bq, 1), F32))
        delta = lax.fori_loop(nfull, nk, functools.partial(probs, on_diagonal=True), delta)

        def grads(j, dq):
            off = pl.multiple_of(j * bk, bk)
            p = p_scr[j]
            ds = p * (dp_scr[j] - delta)
            dsm = ds.astype(MMT)
            dv_scr[pl.ds(off, bk), :] += _dot_tn(p.astype(MMT), dob)
            dk_ref[pl.ds(off, bk), :] += _dot_tn(dsm, q)
            dc_ref[pl.ds(j, 1), :] -= jnp.sum(ds, axis=0, keepdims=True)
            return dq + _dot(dsm, k_ref[pl.ds(off, bk), :])

        dq_ref[...] = lax.fori_loop(0, nk, grads, jnp.zeros((bq, HD), F32))

        @pl.when(qi == pl.num_programs(1) - 1)
        def _():
            dv_ref[...] = dv_scr[...].astype(dv_ref.dtype)

    full = lambda c0=0: pl.BlockSpec((s, HD), lambda h, i: (0, c0 + h))
    blk = lambda: pl.BlockSpec((bq, HD), lambda h, i: (i, h))
    colv = lambda: pl.BlockSpec((None, bq, 1), lambda h, i: (h, i, 0))
    rowv = lambda: pl.BlockSpec((None, nkc, bk), lambda h, i: (h, 0, 0))
    return pl.pallas_call(
        body, grid=(NH, s // bq),
        in_specs=[blk(), full(), full(FV // HD), colv(), rowv(), colv(), blk(), pl.BlockSpec(memory_space=pl.ANY)],
        out_specs=[blk(), full(), full(FV // HD), rowv()],
        out_shape=[_sds((s, BW), F32), _sds((s, BW), F32), _sds(dzz.shape, dzz.dtype), _sds((NH, nkc, bk), F32)],
        input_output_aliases={7: 2},
        scratch_shapes=[pltpu.VMEM((nkc, bq, bk), F32), pltpu.VMEM((nkc, bq, bk), F32), pltpu.VMEM((s, HD), F32)],
        compiler_params=_cparams(("parallel", "arbitrary")), name=name)(qn, kn, zz, cum_col, cum_row, lse, do, dzz)


def _branch_f(rets, rgs, glas, ggs, ret_g, gla_g):
    out_r, out_g = [], []
    for h in range(NH):
        xc = rets[h] - jnp.mean(rets[h], axis=-1, keepdims=True)
        y = xc * lax.rsqrt(jnp.mean(xc * xc, axis=-1, keepdims=True) + EPS) * ret_g[h]
        out_r.append(_silu(rgs[h]) * y)
        x = glas[h]
        y = x * lax.rsqrt(jnp.mean(x * x, axis=-1, keepdims=True) + EPS) * gla_g
        out_g.append(_silu(ggs[h]) * y)
    return out_r, out_g


def _w_br_spec(layer):
    return pl.BlockSpec((None, 3, BW, D), lambda i: (layer, 0, 0, 0))


def mix_fwd(ret_raw, gla_raw, fox_o, zz, ret_g, gla_g, b_mg, w_br, *, name, layer):
    s = zz.shape[0]
    t = _row_tile(s)

    def body(r_ref, g_ref, f_ref, rg_ref, gg_ref, gp_ref, rgn_ref, ggn_ref, bmg_ref, w_ref, o_ref):
        rgn = rgn_ref[...]
        br_r, br_g = _branch_f(_heads(r_ref[...]), _heads(rg_ref[...]), _heads(g_ref[...]), _heads(gg_ref[...]),
                               _heads(rgn), ggn_ref[...])
        brs = [jnp.concatenate(br_r, axis=1), jnp.concatenate(br_g, axis=1), f_ref[...]]
        acc = jnp.zeros((t, D), F32)
        for b in range(3):
            gate = jax.nn.sigmoid(gp_ref[:, b * D:(b + 1) * D] + bmg_ref[:, b * D:(b + 1) * D])
            acc = acc + gate * _dot(brs[b].astype(MMT), w_ref[b])
        o_ref[...] = acc.astype(o_ref.dtype)

    row = lambda w, c=0: pl.BlockSpec((t, w), lambda i: (i, c // w))
    cst = lambda shp: pl.BlockSpec(shp, lambda i: (0,) * len(shp))
    return pl.pallas_call(
        body, grid=(s // t,),
        in_specs=[row(BW), row(BW), row(BW), row(BW, RG), row(BW, GG), row(3 * D, GP), cst((1, BW)), cst((1, HD)),
                  cst((1, 3 * D)), _w_br_spec(layer)],
        out_specs=row(D), out_shape=_sds((s, D), MMT),
        compiler_params=_cparams(("parallel",)), name=name)(ret_raw, gla_raw, fox_o, zz, zz, zz, ret_g, gla_g, b_mg, w_br)


def mix_bwd(ret_raw, gla_raw, fox_o, zz, ret_g, gla_g, b_mg, w_br, dmi, *, name, layer):
    s = zz.shape[0]
    t = _row_tile(s)

    def body(r_ref, g_ref, f_ref, rg_ref, gg_ref, gp_ref, rgn_ref, ggn_ref, bmg_ref, w_ref, dmi_ref,
             dr_ref, dg_ref, df_ref, dgp_ref, dw_ref, drgn_ref, dggn_ref, dbmg_ref):
        @pl.when(pl.program_id(0) == 0)
        def _():
            dw_ref[...] = jnp.zeros_like(dw_ref)
            drgn_ref[...] = jnp.zeros_like(drgn_ref)
            dggn_ref[...] = jnp.zeros_like(dggn_ref)
            dbmg_ref[...] = jnp.zeros_like(dbmg_ref)

        (br_r, br_g), vjp = jax.vjp(_branch_f, _heads(r_ref[...]), _heads(rg_ref[...]), _heads(g_ref[...]),
                                    _heads(gg_ref[...]), _heads(rgn_ref[...]), ggn_ref[...])
        brs = [jnp.concatenate(br_r, axis=1).astype(MMT), jnp.concatenate(br_g, axis=1).astype(MMT),
               f_ref[...].astype(MMT)]
        dmi_v = dmi_ref[...].astype(F32)
        dbr = []
        for b in range(3):
            w = w_ref[b]
            ybr = _dot(brs[b], w)
            gate = jax.nn.sigmoid(gp_ref[:, b * D:(b + 1) * D] + bmg_ref[:, b * D:(b + 1) * D])
            dgp = dmi_v * ybr * gate * (1.0 - gate)
            dgp_ref[:, b * D:(b + 1) * D] = dgp.astype(dgp_ref.dtype)
            dbmg_ref[:, b * D:(b + 1) * D] += jnp.sum(dgp, axis=0, keepdims=True)
            dy = (dmi_v * gate).astype(MMT)
            dw_ref[b] += _dot_tn(brs[b], dy)
            dbr.append(_dot_nt(dy, w))
        dr, drg, dg, dgg, drgn, dggn = vjp((_heads(dbr[0]), _heads(dbr[1])))
        dr_ref[...] = jnp.concatenate(dr, axis=1)
        dg_ref[...] = jnp.concatenate(dg, axis=1)
        df_ref[...] = dbr[2]
        dgp_ref[:, RG:RG + BW] = jnp.concatenate(drg, axis=1).astype(dgp_ref.dtype)
        dgp_ref[:, GG:GG + BW] = jnp.concatenate(dgg, axis=1).astype(dgp_ref.dtype)
        drgn_ref[...] += jnp.concatenate(drgn, axis=1)
        dggn_ref[...] += dggn

    row = lambda w, c=0: pl.BlockSpec((t, w), lambda i: (i, c // w))
    cst = lambda shp: pl.BlockSpec(shp, lambda i: (0,) * len(shp))
    return pl.pallas_call(
        body, grid=(s // t,),
        in_specs=[row(BW), row(BW), row(BW), row(BW, RG), row(BW, GG), row(3 * D, GP), cst((1, BW)), cst((1, HD)),
                  cst((1, 3 * D)), _w_br_spec(layer), row(D)],
        out_specs=[row(BW), row(BW), row(BW), row(FV), cst((3, BW, D)), cst((1, BW)), cst((1, HD)), cst((1, 3 * D))],
        out_shape=[_sds((s, BW), F32)] * 3 + [_sds((s, NZZ), MMT), _sds((3, BW, D), F32), _sds((1, BW), F32),
                                              _sds((1, HD), F32), _sds((1, 3 * D), F32)],
        compiler_params=_cparams(("arbitrary",)), name=name)(ret_raw, gla_raw, fox_o, zz, zz, zz, ret_g, gla_g, b_mg, w_br, dmi)


CT = 256


def _shift_down(x, k, rows):
    return jnp.where(rows >= k, pltpu.roll(x, k, 0), 0.0)


def _shift_up(x, k, rows, s):
    return jnp.where(rows < s - k, pltpu.roll(x, s - k, 0), 0.0)


def conv_fwd(ug, w_conv, b_conv, *, name):
    s = ug.shape[0]
    nt = DFF // CT

    def body(u_ref, g_ref, w_ref, b_ref, a_ref):
        u = u_ref[...]
        rows = lax.broadcasted_iota(jnp.int32, u.shape, 0)
        uc = b_ref[...] + w_ref[0:1, :] * _shift_down(u, 2, rows) + w_ref[1:2, :] * _shift_down(u, 1, rows) + w_ref[2:3, :] * u
        a_ref[...] = (_silu(uc) * g_ref[...]).astype(a_ref.dtype)

    return pl.pallas_call(
        body, grid=(nt,),
        in_specs=[pl.BlockSpec((s, CT), lambda j: (0, j)), pl.BlockSpec((s, CT), lambda j: (0, nt + j)),
                  pl.BlockSpec((3, CT), lambda j: (0, j)), pl.BlockSpec((1, CT), lambda j: (0, j))],
        out_specs=pl.BlockSpec((s, CT), lambda j: (0, j)), out_shape=_sds((s, DFF), MMT),
        compiler_params=_cparams(("parallel",)), name=name)(ug, ug, w_conv, b_conv)


def conv_bwd(ug, w_conv, b_conv, da, *, name):
    s = ug.shape[0]
    nt = DFF // CT

    def body(u_ref, g_ref, w_ref, b_ref, da_ref, du_ref, dg_ref, dw_ref, db_ref):
        u = u_ref[...]
        rows = lax.broadcasted_iota(jnp.int32, u.shape, 0)
        u2, u1 = _shift_down(u, 2, rows), _shift_down(u, 1, rows)
        uc = b_ref[...] + w_ref[0:1, :] * u2 + w_ref[1:2, :] * u1 + w_ref[2:3, :] * u
        sg = jax.nn.sigmoid(uc)
        da_v = da_ref[...]
        dg_ref[...] = (da_v * uc * sg).astype(dg_ref.dtype)
        duc = da_v * g_ref[...] * sg * (1.0 + uc * (1.0 - sg))
        du = w_ref[2:3, :] * duc + w_ref[1:2, :] * _shift_up(duc, 1, rows, s) + w_ref[0:1, :] * _shift_up(duc, 2, rows, s)
        du_ref[...] = du.astype(du_ref.dtype)
        dw_ref[0:1, :] = jnp.sum(duc * u2, axis=0, keepdims=True)
        dw_ref[1:2, :] = jnp.sum(duc * u1, axis=0, keepdims=True)
        dw_ref[2:3, :] = jnp.sum(duc * u, axis=0, keepdims=True)
        db_ref[...] = jnp.sum(duc, axis=0, keepdims=True)

    col = lambda: pl.BlockSpec((s, CT), lambda j: (0, j))
    return pl.pallas_call(
        body, grid=(nt,),
        in_specs=[col(), pl.BlockSpec((s, CT), lambda j: (0, nt + j)), pl.BlockSpec((3, CT), lambda j: (0, j)),
                  pl.BlockSpec((1, CT), lambda j: (0, j)), col()],
        out_specs=[col(), col(), pl.BlockSpec((3, CT), lambda j: (0, j)), pl.BlockSpec((1, CT), lambda j: (0, j))],
        out_shape=[_sds((s, DFF), MMT), _sds((s, DFF), MMT), _sds((3, DFF), F32), _sds((1, DFF), F32)],
        compiler_params=_cparams(("parallel",)), name=name)(ug, ug, w_conv, b_conv, da)


def place_tail(dzz, dlr, dff, *, name):
    s = dzz.shape[0]
    t = _row_tile(s)

    def body(a_ref, b_ref, z_ref, o_ref):
        o_ref[...] = jnp.concatenate([a_ref[...], b_ref[...]], axis=1)

    spec = pl.BlockSpec((t, HD), lambda i: (i, 0))
    return pl.pallas_call(
        body, grid=(s // t,), in_specs=[spec, spec, pl.BlockSpec(memory_space=pl.ANY)],
        out_specs=pl.BlockSpec((t, 2 * HD), lambda i: (i, LR // (2 * HD))), out_shape=_sds(dzz.shape, dzz.dtype),
        input_output_aliases={2: 0}, compiler_params=_cparams(("parallel",)), name=name)(dlr, dff, dzz)


def _tiles(s):
    return min(1024, s)


def layer_fwd(x, mod, p, cosf, sinf):
    s = x.shape[0]
    tm = _tiles(s)
    l = p["l"]
    shift1, scale1, gate1, shift2, scale2, gate2 = mod
    h = norm_mod(x, p["norm1_g"], scale1, shift1, name="norm_mod")
    zz = mm_nn(h, p["w1"], tm=tm, tn=768, out_dtype=F32, name="mm_w1", layer=l)
    ret_raw, rprev = retention_fwd(zz, cosf, sinf, name="ret_fwd")
    gla_raw, sprev = gla_fwd(zz, p["w_a2p"], p["b_gla_a"], name="gla_fwd")
    qn, kn, cum = fox_pre(zz, p["q_norm_g"], p["k_norm_g"], p["b_foxp"], name="fox_pre")
    bq, bk = _fox_blocks(s)
    cum_t = cum[:, :NH].T
    cum_col, cum_row = cum_t[:, :, None], cum_t.reshape(NH, s // bk, bk)
    fox_o, lse = fox_attn_fwd(qn, kn, zz, cum_col, cum_row, name="fox_fwd")
    mi = mix_fwd(ret_raw, gla_raw, fox_o, zz, p["ret_norm_g"], p["gla_norm_g"], p["b_mg"], p["w_br"], name="mix_fwd",
                 layer=l)
    x1, mixed = mm_nn_residual(mi, p["w_o"], x, gate1, tm=tm, tn=512, name="mm_wo", layer=l)
    h2 = norm_mod(x1, p["norm2_g"], scale2, shift2, name="norm_mod")
    ug = mm_nn(h2, p["w_up"], tm=tm, tn=512, out_dtype=F32, name="mm_wup", layer=l)
    a = conv_fwd(ug, p["w_conv"], p["b_conv"], name="conv_fwd")
    x2, y = mm_nn_residual(a, p["w_down"], x1, gate2, tm=tm, tn=512, name="mm_wdown", layer=l)
    saved = dict(x=x, h=h, zz=zz, ret_raw=ret_raw, rprev=rprev, gla_raw=gla_raw, sprev=sprev, qn=qn, kn=kn,
                 cum_col=cum_col, cum_row=cum_row, fox_o=fox_o, lse=lse, mi=mi, mixed=mixed, x1=x1, h2=h2, ug=ug, a=a, y=y)
    return x2, saved


def layer_bwd(dx2, mod, p, sv, cosf, sinf, stacks, slot):
    s = dx2.shape[0]
    tm = _tiles(s)
    l = p["l"]
    shift1, scale1, gate1, shift2, scale2, gate2 = mod
    g, stacks = {}, dict(stacks)
    dy, dgate2 = gate_bwd(dx2, sv["y"], gate2, name="gate_bwd")
    stacks["w_down"] = mm_tn(sv["a"], dy, tm=min(1408, DFF), tn=512, out_dtype=MMT, name="mm_dwdown",
                             stack=stacks["w_down"], layer=slot)
    da = mm_nt(dy, p["w_down"], tm=tm, tn=1408, out_dtype=F32, name="mm_da", layer=l)
    du, dg, g["w_conv"], g["b_conv"] = conv_bwd(sv["ug"], p["w_conv"], p["b_conv"], da, name="conv_bwd")
    dug = jnp.concatenate([du, dg], axis=1)
    stacks["w_up"] = mm_tn(sv["h2"], dug, tm=512, tn=512, out_dtype=MMT, name="mm_dwup", stack=stacks["w_up"], layer=slot)
    dh2 = mm_nt(dug, p["w_up"], tm=min(512, s), tn=D, out_dtype=F32, name="mm_dh2", layer=l)
    dx1, g["norm2_g"], dscale2, dshift2 = norm_mod_bwd(sv["x1"], dh2, dx2, p["norm2_g"], scale2, shift2, name="norm_mod_bwd")
    dmixed, dgate1 = gate_bwd(dx1, sv["mixed"], gate1, name="gate_bwd")
    stacks["w_o"] = mm_tn(sv["mi"], dmixed, tm=512, tn=512, out_dtype=MMT, name="mm_dwo", stack=stacks["w_o"], layer=slot)
    dmi = mm_nt(dmixed, p["w_o"], tm=tm, tn=512, out_dtype=MMT, name="mm_dmi", layer=l)
    zz = sv["zz"]
    (dret, dgla, dfox, dzz, g["w_br"], g["ret_norm_g"], g["gla_norm_g"], g["b_mg"]) = mix_bwd(
        sv["ret_raw"], sv["gla_raw"], sv["fox_o"], zz, p["ret_norm_g"], p["gla_norm_g"], p["b_mg"], p["w_br"], dmi,
        name="mix_bwd", layer=l)
    dqn, dkn, dzz, dcum_row = fox_attn_bwd(sv["qn"], sv["kn"], zz, sv["cum_col"], sv["cum_row"], sv["lse"], dfox, dzz,
                                           name="fox_bwd")
    dcum = jnp.pad(dcum_row.reshape(NH, s).T, ((0, 0), (0, HD - NH)))
    dzz, dff, g["q_norm_g"], g["k_norm_g"], g["b_foxp"] = fox_pre_bwd(
        zz, p["q_norm_g"], p["k_norm_g"], p["b_foxp"], dqn, dkn, dcum, dzz, name="fox_pre_bwd")
    dzz, dlr, g["w_a2p"], g["b_gla_a"] = gla_bwd(zz, p["w_a2p"], p["b_gla_a"], sv["sprev"], dgla, dzz, name="gla_bwd")
    dzz = retention_bwd(zz, cosf, sinf, sv["rprev"], dret, dzz, name="ret_bwd")
    dzz = place_tail(dzz, dlr, dff, name="place_tail")
    stacks["w_mg"] = mm_tn(sv["h"], dzz, tm=512, tn=768, out_dtype=MMT, name="mm_dwmg", ncols=WZ0, stack=stacks["w_mg"],
                           layer=slot)
    dwz = mm_tn(sv["h"], dzz, tm=512, tn=768, out_dtype=MMT, name="mm_dwz", col0=WZ0)
    stacks["w_in"] = unalign_dw_in(dwz, stacks["w_in"], slot)
    dh = mm_nt(dzz, p["w1"], tm=min(256, s), tn=D, out_dtype=F32, name="mm_dh", layer=l)
    dx, g["norm1_g"], dscale1, dshift1 = norm_mod_bwd(sv["x"], dh, dx1, p["norm1_g"], scale1, shift1, name="norm_mod_bwd")
    dmod = jnp.concatenate([dshift1, dscale1, dgate1, dshift2, dscale2, dgate2], axis=1)
    return dx, g, dmod, stacks


def _align_cols(w_in, w_mg):
    z = lambda n: jnp.zeros((w_in.shape[0], n), w_in.dtype)
    seg = lambda name: w_in[:, W_IN_COLS[name][0]:W_IN_COLS[name][1]]
    return jnp.concatenate([w_mg, seg("rg"), seg("gg"), seg("fv"), seg("rqkv"), seg("gqkv"), seg("fqk"), seg("lr"),
                            z(HD - GLR), seg("ff"), z(HD - NH)], axis=1)


def _unalign_cols(dwz):
    seg = lambda c0, name: dwz[:, c0 - WZ0:c0 - WZ0 + W_IN_COLS[name][1] - W_IN_COLS[name][0]]
    return jnp.concatenate([seg(RQ, "rqkv"), seg(RG, "rg"), seg(GQ, "gqkv"), seg(LR, "lr"), seg(GG, "gg"), seg(FQ, "fqk"),
                            seg(FV, "fv"), seg(FF, "ff")], axis=1)


def build_w1(w_in_sh, w_mg):
    nl = w_mg.shape[0]
    t = _row_tile(D)

    def body(s_ref, g_ref, o_ref):
        o_ref[...] = _align_cols(jnp.concatenate([s_ref[k] for k in range(4)], axis=1), g_ref[...])

    return pl.pallas_call(
        body, grid=(nl, D // t),
        in_specs=[pl.BlockSpec((None, 4, t, IN_W // 4), lambda l, i: (l, 0, i, 0)), pl.BlockSpec((None, t, WZ0), lambda l, i: (l, i, 0))],
        out_specs=pl.BlockSpec((None, t, NZZ), lambda l, i: (l, i, 0)), out_shape=_sds((nl, D, NZZ), w_mg.dtype),
        compiler_params=_cparams(("parallel", "parallel")), name="build_w1")(w_in_sh, w_mg)


def unalign_dw_in(dwz, stack, layer):
    t = _row_tile(D)

    def body(z_ref, s_ref, o_ref):
        w = _unalign_cols(z_ref[...])
        for k in range(4):
            o_ref[k] = w[:, k * (IN_W // 4):(k + 1) * (IN_W // 4)]

    return pl.pallas_call(
        body, grid=(D // t,),
        in_specs=[pl.BlockSpec((t, NZZ - WZ0), lambda i: (i, 0)), pl.BlockSpec(memory_space=pl.ANY)],
        out_specs=pl.BlockSpec((None, 4, t, IN_W // 4), lambda i: (layer, 0, i, 0)), out_shape=_sds(stack.shape, stack.dtype),
        input_output_aliases={1: 0}, compiler_params=_cparams(("parallel",)), name="unalign_dw_in")(dwz, stack)


def layer_params(w, big, l):
    row = lambda v: v[l][None, :]
    return dict(
        l=0, norm1_g=row(w["norm1_g"]), norm2_g=row(w["norm2_g"]), w1=big["w1"],
        w_a2p=jnp.pad(w["w_gla_a2"][l], ((0, HD - GLR), (0, 0))), b_gla_a=row(w["b_gla_a"]),
        b_foxp=jnp.pad(row(w["b_fox_f"]), ((0, 0), (0, HD - NH))), ret_norm_g=row(w["ret_norm_g"]),
        gla_norm_g=row(w["gla_norm_g"]), q_norm_g=row(w["q_norm_g"]), k_norm_g=row(w["k_norm_g"]),
        w_br=big["w_br"], b_mg=row(w["b_mg"]), w_o=big["w_o"], w_up=big["w_up"], w_conv=w["w_conv"][l],
        b_conv=row(w["b_conv"]), w_down=big["w_down"])


def layer_grads(g):
    vec = lambda v: v[0]
    return dict(
        norm1_g=vec(g["norm1_g"]), norm2_g=vec(g["norm2_g"]), w_gla_a2=g["w_a2p"][:GLR], b_gla_a=vec(g["b_gla_a"]),
        b_fox_f=g["b_foxp"][0, :NH], ret_norm_g=vec(g["ret_norm_g"]), gla_norm_g=vec(g["gla_norm_g"]),
        q_norm_g=vec(g["q_norm_g"]), k_norm_g=vec(g["k_norm_g"]), w_br=g["w_br"], b_mg=vec(g["b_mg"]),
        w_conv=g["w_conv"], b_conv=vec(g["b_conv"]))


def ada_mod(c_all, w_ada, b_ada):
    nl, _, n = w_ada.shape

    def body(c_ref, w_ref, b_ref, o_ref):
        o_ref[...] = _dot(_silu(c_ref[...]), w_ref[...], HI) + b_ref[...]

    return pl.pallas_call(
        body, grid=(nl,),
        in_specs=[pl.BlockSpec((8, D), lambda l: (0, 0)), pl.BlockSpec((None, D, n), lambda l: (l, 0, 0)),
                  pl.BlockSpec((None, 1, n), lambda l: (l, 0, 0))],
        out_specs=pl.BlockSpec((None, 8, n), lambda l: (l, 0, 0)), out_shape=_sds((nl, 8, n), F32),
        compiler_params=_cparams(("parallel",)), name="ada_mod")(c_all, w_ada, b_ada)


def ada_dw(c_all, dmod):
    nl, _, n = dmod.shape

    def body(c_ref, d_ref, o_ref):
        o_ref[...] = _dot_tn(_silu(c_ref[...]), d_ref[...], HI)

    return pl.pallas_call(
        body, grid=(nl,),
        in_specs=[pl.BlockSpec((8, D), lambda l: (0, 0)), pl.BlockSpec((None, 8, n), lambda l: (l, 0, 0))],
        out_specs=pl.BlockSpec((None, D, n), lambda l: (l, 0, 0)), out_shape=_sds((nl, D, n), F32),
        compiler_params=_cparams(("parallel",)), name="ada_dw")(c_all, dmod)


def sum_devices(g):
    def body(g_ref, o_ref):
        acc = g_ref[0]
        for d in range(1, 8):
            acc = acc + g_ref[d]
        o_ref[...] = acc

    return pl.pallas_call(body, out_shape=_sds(g.shape[1:], F32), name="sum_devices")(g)


def adamw(w, g, m, v, *, block, name, rows=None, into=None):
    nd = w.ndim
    lo, hi = (0, w.shape[0]) if rows is None else rows
    grid = ((hi - lo) // block[0],) + tuple(w.shape[i] // block[i] for i in range(1, nd))
    first = lo // block[0]
    bc1 = 1.0 - ADAM_B1 ** ADAM_STEP
    bc2 = 1.0 - ADAM_B2 ** ADAM_STEP

    def body(w_ref, g_ref, m_ref, v_ref, *rest):
        d_ref, nm_ref, nv_ref = rest[-3:]
        gv = g_ref[...]
        nm = ADAM_B1 * m_ref[...] + (1.0 - ADAM_B1) * gv
        nv = ADAM_B2 * v_ref[...] + (1.0 - ADAM_B2) * (gv * gv)
        nm_ref[...] = nm
        nv_ref[...] = nv
        d_ref[...] = -ADAM_LR * ((nm / bc1) / (jnp.sqrt(nv / bc2) + ADAM_EPS) + ADAM_WD * w_ref[...])

    spec = pl.BlockSpec(tuple(block), lambda i, *j: (first + i,) + j)
    given = [] if into is None else list(into)
    return pl.pallas_call(
        body, grid=grid, in_specs=[spec] * 4 + [pl.BlockSpec(memory_space=pl.ANY)] * len(given), out_specs=[spec] * 3,
        out_shape=[_sds(w.shape, F32)] * 3, input_output_aliases={4 + i: i for i in range(len(given))},
        compiler_params=_cparams(("parallel",) * nd), name=name)(w, g, m, v, *given)


MESH = pl.DeviceIdType.MESH
ANY = pl.BlockSpec(memory_space=pl.ANY)
VM = pl.BlockSpec(memory_space=pltpu.VMEM)


def _place():
    x, y, c = lax.axis_index("x"), lax.axis_index("y"), lax.axis_index("c")
    return x, y, c, [(1 - x, y), (x, 1 - y), (1 - x, 1 - y)]


def small_allgather(v, *, name):
    m_per, n = v.shape

    def body(x_ref, out_ref, send_sems, recv_sems, local_sem):
        x, y, c, chips = _place()
        me, sibling = (x, y, c), (x, y, 1 - c)

        def rows(px, py, pc):
            return out_ref.at[pl.ds((4 * px + 2 * py + pc) * m_per, m_per), :]

        def copy(k, block, to, src=None):
            return pltpu.make_async_remote_copy(
                src_ref=rows(*block) if src is None else src, dst_ref=rows(*block),
                send_sem=send_sems.at[k], recv_sem=recv_sems.at[k], device_id=to, device_id_type=MESH)

        mine = pltpu.make_async_copy(x_ref, rows(*me), local_sem)
        mine.start()
        first = [copy(0, me, sibling, src=x_ref)]
        first += [copy(1 + j, me, (*chip, c), src=x_ref) for j, chip in enumerate(chips)]
        for cp in first:
            cp.start()
        passed = [copy(4 + j, (*chip, c), sibling) for j, chip in enumerate(chips)]
        for j, chip in enumerate(chips):
            copy(1 + j, (*chip, c), me).wait_recv()
            passed[j].start()
        copy(0, sibling, me).wait_recv()
        for j, chip in enumerate(chips):
            copy(4 + j, (*chip, 1 - c), me).wait_recv()
        for cp in first + passed:
            cp.wait_send()
        mine.wait()

    return pl.pallas_call(
        body, out_shape=_sds((8 * m_per, n), v.dtype), in_specs=[VM], out_specs=VM,
        scratch_shapes=[pltpu.SemaphoreType.DMA((7,)), pltpu.SemaphoreType.DMA((7,)), pltpu.SemaphoreType.DMA],
        name=name)(v)


TENSORS = {
    "w_in": ("lead", None, (4, D, 1285), (1, D, 1285)),
    "w_mg": ("col", 768, (D, 3072), (512, 3072)),
    "w_br": ("col", 256, (3, BW, D), (3, BW, D)),
    "w_o": ("row", 256, (D, D), (D, D)),
    "w_up": ("col", 1408, (D, 5632), (256, 5632)),
    "w_down": ("row", 704, (DFF, D), (704, D)),
}
BIG = tuple(TENSORS)


def _shard_shape(name):
    kind, width, full, _ = TENSORS[name]
    if kind == "lead":
        return full[1:]
    return full[:-1] + (width,) if kind == "col" else (width,) + full[1:]


def _shard_view(ref, layers, name, k):
    kind, width, full, _ = TENSORS[name]
    if kind == "lead":
        return ref.at[layers, k]
    if kind == "row":
        return ref.at[layers, pl.ds(k * width, width)]
    return ref.at[(layers,) + (slice(None),) * (len(full) - 1) + (pl.ds(k * width, width),)]


def _remote(send_sems, recv_sems, k, src, dst, to):
    return pltpu.make_async_remote_copy(src_ref=src, dst_ref=dst, send_sem=send_sems.at[k], recv_sem=recv_sems.at[k],
                                        device_id=to, device_id_type=MESH)


def _dma_sems(n):
    return [pltpu.SemaphoreType.DMA((n,)), pltpu.SemaphoreType.DMA((n,))]


RS_GROUP = 2
HBM = pl.BlockSpec(memory_space=pltpu.HBM)
SEM = pl.BlockSpec(memory_space=pltpu.SEMAPHORE)
SPLIT_CALL = dict(compiler_params=pltpu.CompilerParams(has_side_effects=pltpu.SideEffectType.DATAFLOW_SIDE_EFFECTING))
PULL_SET = (("w_up", "w_down", "w_o"), ("w_in", "w_mg", "w_br"))


def _in_hbm(a):
    return pltpu.with_memory_space_constraint(a, pltpu.HBM)


def _pull_sends(send_sems, recv_sems, p, o, layer, core, x, y, chips):
    return [_remote(send_sems, recv_sems, 3 * BIG.index(n) + j, p[n].at[layer], _shard_view(o[n], 0, n, 2 * x + y), (*chip, core))
            for n in PULL_SET[core] for j, chip in enumerate(chips)]


def _pull_arrivals(send_sems, recv_sems, o, core, x, y, chips, to):
    views = [(3 * BIG.index(n) + j, _shard_view(o[n], 0, n, 2 * chip[0] + chip[1]))
             for n in PULL_SET[core] for j, chip in enumerate(chips)]
    return [_remote(send_sems, recv_sems, k, v, v, to) for k, v in views]


def gather_start(shards, layer, *, name):
    nt = len(BIG)

    def body(*refs):
        p, o = dict(zip(BIG, refs[:nt])), dict(zip(BIG, refs[nt:2 * nt]))
        x, y, c, chips = _place()
        for core in (0, 1):
            @pl.when(c == core)
            def _():
                for cp in _pull_sends(refs[2 * nt], refs[2 * nt + 1], p, o, layer, core, x, y, chips):
                    cp.start()
        refs[-1][...] = jnp.zeros_like(refs[-1])

    lands = [_in_hbm(lax.empty((1,) + TENSORS[n][2], shards[n].dtype)) for n in BIG]
    outs = pl.pallas_call(
        body,
        out_shape=(pltpu.SemaphoreType.DMA((3 * nt,)), pltpu.SemaphoreType.DMA((3 * nt,)),
                   *[pltpu.HBM(a.shape, a.dtype) for a in lands], _sds((8, HD), F32)),
        in_specs=[HBM] * (2 * nt), out_specs=(SEM, SEM, *[HBM] * nt, VM),
        input_output_aliases={nt + t: 2 + t for t in range(nt)}, name=name, **SPLIT_CALL)(
            *[_in_hbm(shards[n]) for n in BIG], *lands)
    return outs[0], outs[1], outs[2:2 + nt], outs[-1]


def gather_wait(send_sems, recv_sems, shards, lands, after, layer, *, name):
    nt = len(BIG)

    def body(*refs):
        p, o = dict(zip(BIG, refs[:nt])), dict(zip(BIG, refs[nt:2 * nt]))
        ss, rs = refs[2 * nt], refs[2 * nt + 1]
        x, y, c, chips = _place()
        for core in (0, 1):
            @pl.when(c == core)
            def _():
                for cp in _pull_sends(ss, rs, p, o, layer, core, x, y, chips):
                    cp.wait_send()
                for cp in _pull_arrivals(ss, rs, o, core, x, y, chips, (x, y, core)):
                    cp.wait_recv()

    return pl.pallas_call(
        body, out_shape=tuple(pltpu.HBM(a.shape, a.dtype) for a in lands),
        in_specs=[HBM] * (2 * nt) + [SEM, SEM, ANY], out_specs=tuple([HBM] * nt),
        input_output_aliases={nt + t: t for t in range(nt)}, name=name, **SPLIT_CALL)(
            *[_in_hbm(shards[n]) for n in BIG], *lands, send_sems, recv_sems, after)


def gather_forward(shards, lands, layer, *, name):
    nt = len(BIG)

    def body(*refs):
        p, o = dict(zip(BIG, refs[:nt])), dict(zip(BIG, refs[2 * nt:3 * nt]))
        ss, rs = refs[3 * nt:]
        x, y, c, chips = _place()
        for core in (0, 1):
            @pl.when(c == core)
            def _():
                me, sibling = (x, y, core), (x, y, 1 - core)
                sends = _pull_arrivals(ss, rs, o, core, x, y, chips, sibling)
                sends += [_remote(ss, rs, 3 * nt + t, p[n].at[layer], _shard_view(o[n], 0, n, 2 * x + y), sibling)
                          for t, n in enumerate(BIG)]
                for cp in sends:
                    cp.start()
                for cp in sends:
                    cp.wait_send()
                for cp in _pull_arrivals(ss, rs, o, 1 - core, x, y, chips, me):
                    cp.wait_recv()
                for t, n in enumerate(BIG):
                    own = _shard_view(o[n], 0, n, 2 * x + y)
                    _remote(ss, rs, 3 * nt + t, own, own, me).wait_recv()

    outs = pl.pallas_call(
        body, out_shape=[_sds(a.shape, a.dtype) for a in lands], in_specs=[ANY] * (2 * nt), out_specs=[ANY] * nt,
        input_output_aliases={nt + t: t for t in range(nt)}, scratch_shapes=_dma_sems(4 * nt), name=name)(
            *[shards[n] for n in BIG], *lands)
    return dict(zip(BIG, outs))


def pair_exchange(g, *, name):
    hh = g[BIG[0]].shape[0] // 2
    nt = len(BIG)

    def body(*refs):
        send_sems, recv_sems = refs[2 * nt:]
        x, y, c, _ = _place()
        copies = [_remote(send_sems, recv_sems, t, refs[t].at[pl.ds(hh * (1 - c), hh)], refs[nt + t], (x, y, 1 - c))
                  for t in range(nt)]
        for cp in copies:
            cp.start()
        for cp in copies:
            cp.wait()

    outs = pl.pallas_call(
        body, out_shape=[_sds((hh,) + g[n].shape[1:], g[n].dtype) for n in BIG], in_specs=[ANY] * nt, out_specs=[ANY] * nt,
        scratch_shapes=_dma_sems(nt), name=name)(*[g[n] for n in BIG])
    return dict(zip(BIG, outs))


def _chip_copies(send_sems, recv_sems, s_refs, land_refs, c, chips):
    hl = s_refs[0].shape[0]
    return [_remote(send_sems, recv_sems, 3 * t + j, _shard_view(s_refs[t], pl.ds(0, hl), n, 2 * chip[0] + chip[1]),
                    land_refs[t].at[j], (*chip, c))
            for t, n in enumerate(BIG) for j, chip in enumerate(chips)]


def _landing_shapes(s):
    hl = s[BIG[0]].shape[0]
    return [_sds((3, hl) + _shard_shape(n), s[n].dtype) for n in BIG]


def chip_exchange(s, *, name):
    nt = len(BIG)

    def body(*refs):
        send_sems, recv_sems = refs[2 * nt:]
        x, y, c, chips = _place()
        copies = _chip_copies(send_sems, recv_sems, refs[:nt], refs[nt:2 * nt], c, chips)
        for cp in copies:
            cp.start()
        for cp in copies:
            cp.wait()

    outs = pl.pallas_call(
        body, out_shape=_landing_shapes(s), in_specs=[ANY] * nt, out_specs=[ANY] * nt,
        scratch_shapes=_dma_sems(3 * nt), name=name)(*[s[n] for n in BIG])
    return dict(zip(BIG, outs))


def chip_exchange_start(s, *, name, after=None):
    nt = len(BIG)
    first = [] if after is None else [after]

    def body(*refs):
        o = refs[2 * nt + len(first):]
        x, y, c, chips = _place()
        for cp in _chip_copies(o[0], o[1], refs[:nt], refs[nt:2 * nt], c, chips):
            cp.start()
        refs[-1][...] = jnp.zeros_like(refs[-1])

    lands = [_in_hbm(lax.empty(d.shape, d.dtype)) for d in _landing_shapes(s)]
    srcs = [_in_hbm(s[n]) for n in BIG]
    outs = pl.pallas_call(
        body,
        out_shape=(pltpu.SemaphoreType.DMA((3 * nt,)), pltpu.SemaphoreType.DMA((3 * nt,)),
                   *[pltpu.HBM(a.shape, a.dtype) for a in srcs + lands], _sds((8, HD), F32)),
        in_specs=[HBM] * (2 * nt) + [ANY] * len(first), out_specs=(SEM, SEM, *[HBM] * (2 * nt), VM),
        input_output_aliases={t: 2 + t for t in range(2 * nt)}, name=name, **SPLIT_CALL)(*srcs, *lands, *first)
    return outs[0], outs[1], outs[2:2 + nt], outs[2 + nt:2 + 2 * nt], outs[-1]


def chip_exchange_wait(send_sems, recv_sems, srcs, lands, after, *, name):
    nt = len(BIG)

    def body(*refs):
        x, y, c, chips = _place()
        for cp in _chip_copies(refs[2 * nt], refs[2 * nt + 1], refs[:nt], refs[nt:2 * nt], c, chips):
            cp.wait_send()
            cp.wait_recv()

    outs = pl.pallas_call(
        body, out_shape=tuple(pltpu.HBM(a.shape, a.dtype) for a in list(srcs) + list(lands)),
        in_specs=[HBM] * (2 * nt) + [SEM, SEM] + [ANY] * len(after), out_specs=tuple([HBM] * (2 * nt)),
        input_output_aliases={t: t for t in range(2 * nt)}, name=name, **SPLIT_CALL)(
            *srcs, *lands, send_sems, recv_sems, *after)
    return dict(zip(BIG, outs[:nt])), dict(zip(BIG, outs[nt:]))


def pair_share(f, l0, hh, *, name):
    nt = len(BIG)

    def body(*refs):
        o = refs[nt:2 * nt]
        send_sems, recv_sems = refs[2 * nt:]
        x, y, c, _ = _place()
        mine, theirs = pl.ds(l0 + hh * c, hh), pl.ds(l0 + hh * (1 - c), hh)
        copies = [_remote(send_sems, recv_sems, t, o[t].at[mine], o[t].at[mine], (x, y, 1 - c)) for t in range(nt)]
        for cp in copies:
            cp.start()
        for t, cp in enumerate(copies):
            cp.wait_send()
            _remote(send_sems, recv_sems, t, o[t].at[theirs], o[t].at[theirs], (x, y, c)).wait_recv()

    outs = pl.pallas_call(
        body, out_shape=[_sds(f[n].shape, f[n].dtype) for n in BIG], in_specs=[ANY] * nt, out_specs=[ANY] * nt,
        input_output_aliases={t: t for t in range(nt)}, scratch_shapes=_dma_sems(nt), name=name)(*[f[n] for n in BIG])
    return dict(zip(BIG, outs))


def pair_add(g, r, idx, *, tensor, name):
    _, _, full, blk = TENSORS[tensor]
    hh = r.shape[0]

    def body(idx_ref, g_ref, r_ref, o_ref):
        o_ref[...] = (g_ref[...].astype(F32) + r_ref[...].astype(F32)).astype(o_ref.dtype)

    own = pl.BlockSpec((None,) + blk, lambda *a: (a[0],) + a[1:-1])
    return pl.pallas_call(
        body, out_shape=_sds(r.shape, r.dtype),
        grid_spec=pltpu.PrefetchScalarGridSpec(
            num_scalar_prefetch=1, grid=(hh,) + tuple(f // b for f, b in zip(full, blk)),
            in_specs=[pl.BlockSpec((None,) + blk, lambda *a: (hh * a[-1][0] + a[0],) + a[1:-1]), own], out_specs=own),
        compiler_params=_cparams(("parallel",) * (1 + len(full))), name=name)(idx, g, r)


def chip_add(s, r, idx, totals, l0, *, tensor, name):
    kind, width, full, _ = TENSORS[tensor]
    shard = _shard_shape(tensor)
    hh = s.shape[0]
    zeros = (0,) * len(shard)

    def body(idx_ref, s_ref, r0_ref, r1_ref, r2_ref, t_ref, o_ref):
        o_ref[...] = ((s_ref[...].astype(F32) + r0_ref[...].astype(F32)) + r1_ref[...].astype(F32)) + r2_ref[...].astype(F32)

    if kind == "lead":
        mine = pl.BlockSpec((None, None) + shard, lambda i, ix: (i, ix[1]) + zeros)
    elif kind == "row":
        mine = pl.BlockSpec((None,) + shard, lambda i, ix: (i, ix[1]) + zeros[1:])
    else:
        mine = pl.BlockSpec((None,) + shard, lambda i, ix: (i,) + zeros[1:] + (ix[1],))
    peer = lambda j: pl.BlockSpec((None, None) + shard, lambda i, ix: (j, i) + zeros)
    return pl.pallas_call(
        body, out_shape=_sds(totals.shape, F32),
        grid_spec=pltpu.PrefetchScalarGridSpec(
            num_scalar_prefetch=1, grid=(hh,), in_specs=[mine, peer(0), peer(1), peer(2), pl.BlockSpec(memory_space=pl.ANY)],
            out_specs=pl.BlockSpec((None,) + shard, lambda i, ix: (l0 + hh * ix[0] + i,) + zeros)),
        input_output_aliases={5: 0}, compiler_params=_cparams(("parallel",)), name=name)(idx, s, r, r, r, totals)


def _flat_rows(arrs):
    v = jnp.concatenate([a.reshape(-1) for a in arrs])
    n = -(-v.shape[0] // 1024) * 1024
    return jnp.pad(v, (0, n - v.shape[0])).reshape(n // HD, HD)


def _unflat(buf, shapes):
    v, out, o = buf.reshape(-1), [], 0
    for s in shapes:
        n = int(np.prod(s))
        out.append(v[o:o + n].reshape(s))
        o += n
    return out


WEIGHTS = ("norm1_g", "norm2_g", "w_ada", "b_ada", "w_in", "w_gla_a2", "b_gla_a", "b_fox_f", "ret_norm_g", "gla_norm_g",
           "q_norm_g", "k_norm_g", "w_br", "w_mg", "b_mg", "w_o", "w_up", "w_conv", "b_conv", "w_down")
REPLICATED = ("norm1_g", "norm2_g", "b_gla_a", "b_fox_f", "ret_norm_g", "gla_norm_g", "q_norm_g", "k_norm_g", "b_mg", "b_conv")
ADAM_BLOCKS = dict(w_ada=(1, 256, 1536), w_in=(1, 256, 1285), w_br=(1, 3, BW, 256), w_mg=(1, 512, 768), w_o=(2, 256, D),
                   w_up=(1, 256, 1408), w_down=(1, 352, D))
ALL_AXES = ("x", "y", "c")


def kernel(x, c, norm1_g, norm2_g, w_ada, b_ada, w_in, w_gla_a2, b_gla_a, b_fox_f, ret_norm_g, gla_norm_g, q_norm_g, k_norm_g, w_br, w_mg, b_mg, w_o, w_up, w_conv, b_conv, w_down, loss_target, m_norm1_g, m_norm2_g, m_w_ada, m_b_ada, m_w_in, m_w_gla_a2, m_b_gla_a, m_b_fox_f, m_ret_norm_g, m_gla_norm_g, m_q_norm_g, m_k_norm_g, m_w_br, m_w_mg, m_b_mg, m_w_o, m_w_up, m_w_conv, m_b_conv, m_w_down, v_norm1_g, v_norm2_g, v_w_ada, v_b_ada, v_w_in, v_w_gla_a2, v_b_gla_a, v_b_fox_f, v_ret_norm_g, v_gla_norm_g, v_q_norm_g, v_k_norm_g, v_w_br, v_w_mg, v_b_mg, v_w_o, v_w_up, v_w_conv, v_b_conv, v_w_down):
    w = dict(zip(WEIGHTS, (norm1_g, norm2_g, w_ada, b_ada, w_in, w_gla_a2, b_gla_a, b_fox_f, ret_norm_g, gla_norm_g,
                           q_norm_g, k_norm_g, w_br, w_mg, b_mg, w_o, w_up, w_conv, b_conv, w_down)))
    m = dict(zip(WEIGHTS, (m_norm1_g, m_norm2_g, m_w_ada, m_b_ada, m_w_in, m_w_gla_a2, m_b_gla_a, m_b_fox_f, m_ret_norm_g,
                           m_gla_norm_g, m_q_norm_g, m_k_norm_g, m_w_br, m_w_mg, m_b_mg, m_w_o, m_w_up, m_w_conv, m_b_conv,
                           m_w_down)))
    v = dict(zip(WEIGHTS, (v_norm1_g, v_norm2_g, v_w_ada, v_b_ada, v_w_in, v_w_gla_a2, v_b_gla_a, v_b_fox_f, v_ret_norm_g,
                           v_gla_norm_g, v_q_norm_g, v_k_norm_g, v_w_br, v_w_mg, v_b_mg, v_w_o, v_w_up, v_w_conv, v_b_conv,
                           v_w_down)))
    nl = norm1_g.shape[0]
    seq = x.shape[1]
    xi, yi, ci = lax.axis_index("x"), lax.axis_index("y"), lax.axis_index("c")
    k_me = 2 * xi + yi
    b_me = 4 * xi + 2 * yi + ci
    ada_n = w_ada.shape[2]
    a2_n, conv_n = w_gla_a2.shape[2], w_conv.shape[2]

    shards = {n: w[n].astype(MMT) for n in BIG}
    started = gather_start(shards, 0, name="gather0_start")

    def gather_finish(l, started, after):
        send_sems, recv_sems, lands, _ = started
        lands = gather_wait(send_sems, recv_sems, shards, lands, after, l, name=f"gather{l}_wait")
        big = gather_forward(shards, lands, l, name=f"gather{l}_forward")
        big["w1"] = build_w1(big["w_in"], big["w_mg"])
        return big

    blk = _flat_rows([c, w_gla_a2, w_conv]) + started[-1][0, 0]
    g1 = small_allgather(blk, name="gather_small").reshape(8, blk.shape[0], HD)
    c_all = g1[:, :D // HD].reshape(8, D)
    by_chip = g1[0::2].reshape(4, -1)[:, D:]
    a2_sh, conv_sh = by_chip[:, :nl * GLR * a2_n], by_chip[:, nl * GLR * a2_n:nl * (GLR * a2_n + 3 * conv_n)]
    full_small = dict(
        w_gla_a2=a2_sh.reshape(4, nl, GLR, a2_n).transpose(1, 2, 0, 3).reshape(nl, GLR, 4 * a2_n),
        w_conv=conv_sh.reshape(4, nl, 3, conv_n).transpose(1, 2, 0, 3).reshape(nl, 3, 4 * conv_n))

    b_ada_sh = lax.dynamic_slice_in_dim(b_ada, k_me * ada_n, ada_n, axis=1)[:, None, :]
    mod_sh = ada_mod(c_all, w_ada, b_ada_sh)
    g2 = small_allgather(mod_sh.reshape(nl * 8, ada_n), name="gather_mod").reshape(4, 2, nl, 8, ada_n)[:, 0]
    mod_me = lax.dynamic_index_in_dim(g2, b_me, axis=2, keepdims=False).transpose(1, 0, 2).reshape(nl, 4 * ada_n)

    wsmall = {n: w[n] for n in REPLICATED}
    wsmall.update(full_small)
    mods = [[mod_me[l:l + 1, i * D:(i + 1) * D] for i in range(6)] for l in range(nl)]
    big = gather_finish(0, started, mod_me)

    cosf, sinf = _rope_tables(seq)
    xs, saved, params = x[0], [], []
    for l in range(nl):
        if l + 1 < nl:
            started = gather_start(shards, l + 1, name=f"gather{l + 1}_start")
            xs = xs + started[-1][0, 0]
        params.append(layer_params(wsmall, big, l))
        xs, sv = layer_fwd(xs, mods[l], params[l], cosf, sinf)
        saved.append(sv)
        if l + 1 < nl:
            big = gather_finish(l + 1, started, xs)
    loss_part, dx = loss_and_grad(xs, loss_target[0], name="loss")
    loss = lax.psum(loss_part[0, 0], ALL_AXES)
    grads, dmods = [None] * nl, [None] * nl
    idx = jnp.stack([ci, k_me]).astype(jnp.int32)
    totals = {n: lax.empty((nl,) + _shard_shape(n), F32) for n in BIG}

    def finish_group(group, chip_sum, from_chips, totals):
        totals = {n: chip_add(chip_sum[n], from_chips[n], idx, totals[n], RS_GROUP * group, tensor=n,
                              name=f"rs{group}_chip_add_{n}") for n in BIG}
        return pair_share(totals, RS_GROUP * group, RS_GROUP // 2, name=f"rs{group}_pair_share")

    pending = None
    for group in reversed(range(nl // RS_GROUP)):
        layers = range(RS_GROUP * group, RS_GROUP * (group + 1))
        stacks = {n: lax.empty((RS_GROUP,) + TENSORS[n][2], MMT) for n in BIG if n != "w_br"}
        if pending is not None:
            dx = dx + pending[-1][0, 0]
        for l in reversed(layers):
            dx, g, dmods[l], stacks = layer_bwd(dx, mods[l], params[l], saved[l], cosf, sinf, stacks, l - layers[0])
            grads[l] = layer_grads(g)
        stacks["w_br"] = jnp.stack([grads[l]["w_br"].astype(MMT) for l in layers])
        from_sibling = pair_exchange(stacks, name=f"rs{group}_pair_exchange")
        chip_sum = {n: pair_add(stacks[n], from_sibling[n], idx, tensor=n, name=f"rs{group}_pair_add_{n}") for n in BIG}
        if pending is not None:
            earlier, send_sems, recv_sems, srcs, lands, _ = pending
            sums, from_chips = chip_exchange_wait(send_sems, recv_sems, srcs, lands, [dx, *chip_sum.values()],
                                                  name=f"rs{earlier}_chip_exchange_wait")
            totals = finish_group(earlier, sums, from_chips, totals)
        if group > 0:
            pending = (group, *chip_exchange_start(chip_sum, name=f"rs{group}_chip_exchange_start"))

    small_names = REPLICATED + ("w_gla_a2", "w_conv")
    small_shapes = [(nl, 6 * D)] + [(nl,) + grads[0][n].shape for n in small_names]
    vec = _flat_rows([jnp.concatenate(dmods, axis=0)] + [jnp.stack([grads[l][n] for l in range(nl)]) for n in small_names])
    gs = small_allgather(vec, name="gather_small_grads")
    pending = (0, *chip_exchange_start(chip_sum, name="rs0_chip_exchange_start", after=gs))
    gs = (gs + pending[-1][0, 0]).reshape(8, vec.shape[0], HD)
    summed = _unflat(sum_devices(gs), small_shapes)
    grad = dict(zip(small_names, summed[1:]))
    grad["b_ada"] = summed[0]
    grad["w_gla_a2"] = lax.dynamic_slice_in_dim(grad["w_gla_a2"], k_me * a2_n, a2_n, axis=2)
    grad["w_conv"] = lax.dynamic_slice_in_dim(grad["w_conv"], k_me * conv_n, conv_n, axis=2)
    dmod_all = gs[:, :nl * 6 * D // HD].reshape(8, nl, 6 * D)
    dmod_sh = lax.dynamic_slice_in_dim(dmod_all, k_me * ada_n, ada_n, axis=2).transpose(1, 0, 2)
    grad["w_ada"] = ada_dw(c_all, dmod_sh)

    delta, new_m, new_v = {}, {}, {}
    delta["w_ada"], new_m["w_ada"], new_v["w_ada"] = adamw(w["w_ada"], grad["w_ada"], m["w_ada"], v["w_ada"],
                                                          block=ADAM_BLOCKS["w_ada"], name="adamw_w_ada")
    rest = [n for n in WEIGHTS if n not in ADAM_BLOCKS]
    shapes = [w[n].shape for n in rest]
    flat = [_flat_rows([t[n] for n in rest]) for t in (w, grad, m, v)]
    outs = adamw(*flat, block=flat[0].shape, name="adamw_small")
    for t, o in zip((delta, new_m, new_v), outs):
        t.update(zip(rest, _unflat(o, shapes)))

    later = {n: adamw(w[n], totals[n], m[n], v[n], block=ADAM_BLOCKS[n], name="adamw_later_" + n, rows=(RS_GROUP, nl))
             for n in BIG} if nl > RS_GROUP else {}
    earlier, send_sems, recv_sems, srcs, lands, _ = pending
    done_first = [outs[0], delta["w_ada"]] + [later[n][0] for n in later]
    sums, from_chips = chip_exchange_wait(send_sems, recv_sems, srcs, lands, done_first,
                                          name=f"rs{earlier}_chip_exchange_wait")
    grad.update(finish_group(earlier, sums, from_chips, totals))
    for n in BIG:
        delta[n], new_m[n], new_v[n] = adamw(w[n], grad[n], m[n], v[n], block=ADAM_BLOCKS[n], name="adamw_" + n,
                                             rows=(0, min(RS_GROUP, nl)), into=later.get(n))

    return (loss, dx[None], *[grad[n] for n in WEIGHTS], *[delta[n] for n in WEIGHTS], *[new_m[n] for n in WEIGHTS],
            *[new_v[n] for n in WEIGHTS])
```

```python
import functools

import numpy as np
import jax
import jax.numpy as jnp
from jax import lax
from jax.experimental import pallas as pl
from jax.experimental.pallas import tpu as pltpu

F32 = jnp.float32
MMT = jnp.bfloat16
HI = lax.Precision.HIGHEST

D = 1024
DEPTH = 4
NH = 4
HD = 128
BW = NH * HD
CH = 64
GDK = 64
GLR = 16
DFF = 2816
EPS = 1e-6
ROPE_BASE = 10000.0

GP, RG, GG, FV, RQ, RK, RV, GQ, GK, GV, FQ, FK, LR, FF = (
    0, 3072, 3584, 4096, 4608, 5120, 5632, 6144, 6400, 6656, 7168, 7680, 8192, 8320)
NZZ = 8448
WZ0 = 3072
IN_W = 5140
W_IN_COLS = dict(rqkv=(0, 1536), rg=(1536, 2048), gqkv=(2048, 3072), lr=(3072, 3088), gg=(3088, 3600), fqk=(3600, 4624),
                 fv=(4624, 5136), ff=(5136, 5140))

VMEM_LIMIT = 56 * 1024 * 1024

ADAM_LR, ADAM_B1, ADAM_B2, ADAM_EPS, ADAM_WD, ADAM_STEP = 0.001, 0.9, 0.999, 1e-08, 0.01, 10


def _cparams(sem=None):
    return pltpu.CompilerParams(dimension_semantics=sem, vmem_limit_bytes=VMEM_LIMIT)


def _sds(shape, dtype):
    return jax.ShapeDtypeStruct(tuple(shape), dtype)


def _dot(a, b, precision=None):
    return lax.dot_general(a, b, (((1,), (0,)), ((), ())), precision=precision, preferred_element_type=F32)


def _dot_nt(a, b, precision=None):
    return lax.dot_general(a, b, (((1,), (1,)), ((), ())), precision=precision, preferred_element_type=F32)


def _dot_tn(a, b, precision=None):
    return lax.dot_general(a, b, (((0,), (0,)), ((), ())), precision=precision, preferred_element_type=F32)


def _silu(x):
    return x * jax.nn.sigmoid(x)


def _log_sigmoid(x):
    return jnp.minimum(x, 0.0) - jnp.log(1.0 + jnp.exp(jnp.minimum(x, -x)))


@jax.custom_vjp
def _swap_halves(x):
    return pltpu.roll(x, HD // 2, 1)


_swap_halves.defvjp(lambda x: (_swap_halves(x), None), lambda _, g: (_swap_halves(g),))


@jax.custom_vjp
def _bdot(a, b):
    return _dot(a.astype(MMT), b.astype(MMT))


@jax.custom_vjp
def _bdot_nt(a, b):
    return _dot_nt(a.astype(MMT), b.astype(MMT))


@jax.custom_vjp
def _bdot_tn(a, b):
    return _dot_tn(a.astype(MMT), b.astype(MMT))


_bdot.defvjp(lambda a, b: (_bdot(a, b), (a, b)), lambda r, g: (_bdot_nt(g, r[1]), _bdot_tn(r[0], g)))
_bdot_nt.defvjp(lambda a, b: (_bdot_nt(a, b), (a, b)), lambda r, g: (_bdot(g, r[1]), _bdot_tn(g, r[0])))
_bdot_tn.defvjp(lambda a, b: (_bdot_tn(a, b), (a, b)), lambda r, g: (_bdot_nt(r[1], g), _bdot(r[0], g)))


def _stacked(blk, idx, layer):
    if layer is None:
        return pl.BlockSpec(blk, idx)
    return pl.BlockSpec((None,) + blk, lambda i, j: (layer,) + idx(i, j))


def mm_nn(a, b, *, tm, tn, out_dtype, name, layer=None):
    m, k = a.shape
    n = b.shape[-1]

    def body(a_ref, b_ref, o_ref):
        o_ref[...] = _dot(a_ref[...], b_ref[...]).astype(o_ref.dtype)

    return pl.pallas_call(
        body, grid=(m // tm, n // tn),
        in_specs=[pl.BlockSpec((tm, k), lambda i, j: (i, 0)), _stacked((k, tn), lambda i, j: (0, j), layer)],
        out_specs=pl.BlockSpec((tm, tn), lambda i, j: (i, j)),
        out_shape=_sds((m, n), out_dtype), compiler_params=_cparams(("parallel", "parallel")), name=name)(a, b)


def mm_nn_residual(a, b, res, gate, *, tm, tn, name, layer=None):
    m, k = a.shape
    n = b.shape[-1]

    def body(a_ref, b_ref, r_ref, g_ref, x_ref, y_ref):
        acc = _dot(a_ref[...], b_ref[...])
        y_ref[...] = acc
        x_ref[...] = r_ref[...] + g_ref[...] * acc

    return pl.pallas_call(
        body, grid=(m // tm, n // tn),
        in_specs=[pl.BlockSpec((tm, k), lambda i, j: (i, 0)), _stacked((k, tn), lambda i, j: (0, j), layer),
                  pl.BlockSpec((tm, tn), lambda i, j: (i, j)), pl.BlockSpec((1, tn), lambda i, j: (0, j))],
        out_specs=[pl.BlockSpec((tm, tn), lambda i, j: (i, j)), pl.BlockSpec((tm, tn), lambda i, j: (i, j))],
        out_shape=[_sds((m, n), F32), _sds((m, n), F32)],
        compiler_params=_cparams(("parallel", "parallel")), name=name)(a, b, res, gate)


def mm_nt(a, b, *, tm, tn, out_dtype, name, layer=None):
    m, k = a.shape
    n = b.shape[-2]

    def body(a_ref, b_ref, o_ref):
        o_ref[...] = _dot_nt(a_ref[...], b_ref[...]).astype(o_ref.dtype)

    return pl.pallas_call(
        body, grid=(m // tm, n // tn),
        in_specs=[pl.BlockSpec((tm, k), lambda i, j: (i, 0)), _stacked((tn, k), lambda i, j: (j, 0), layer)],
        out_specs=pl.BlockSpec((tm, tn), lambda i, j: (i, j)),
        out_shape=_sds((m, n), out_dtype), compiler_params=_cparams(("parallel", "parallel")), name=name)(a, b)


def mm_nt2(a1, a2, b, *, tm, tn, name, layer):
    m, k1 = a1.shape
    k2 = a2.shape[1]
    n = b.shape[-2]

    def body(a1_ref, a2_ref, b_ref, o_ref):
        o_ref[...] = _dot_nt(a1_ref[...], b_ref[:, :k1]) + _dot_nt(a2_ref[...], b_ref[:, k1:])

    return pl.pallas_call(
        body, grid=(m // tm, n // tn),
        in_specs=[pl.BlockSpec((tm, k1), lambda i, j: (i, 0)), pl.BlockSpec((tm, k2), lambda i, j: (i, 0)),
                  _stacked((tn, k1 + k2), lambda i, j: (j, 0), layer)],
        out_specs=pl.BlockSpec((tm, tn), lambda i, j: (i, j)),
        out_shape=_sds((m, n), F32), compiler_params=_cparams(("parallel", "parallel")), name=name)(a1, a2, b)


def mm_tn(a, b, *, tm, tn, out_dtype, name, col0=0, ncols=None, stack=None, layer=None, out_col0=0):
    s, m = a.shape
    n = b.shape[1] - col0 if ncols is None else ncols
    c0, oc0 = col0 // tn, out_col0 // tn

    def body(a_ref, b_ref, *rest):
        o_ref = rest[-1]
        o_ref[...] = _dot_tn(a_ref[...], b_ref[...]).astype(o_ref.dtype)

    in_specs = [pl.BlockSpec((s, tm), lambda i, j: (0, i)), pl.BlockSpec((s, tn), lambda i, j: (0, c0 + j))]
    if stack is None:
        return pl.pallas_call(
            body, grid=(m // tm, n // tn), in_specs=in_specs, out_specs=pl.BlockSpec((tm, tn), lambda i, j: (i, j)),
            out_shape=_sds((m, n), out_dtype), compiler_params=_cparams(("parallel", "parallel")), name=name)(a, b)
    return pl.pallas_call(
        body, grid=(m // tm, n // tn), in_specs=in_specs + [pl.BlockSpec(memory_space=pl.ANY)],
        out_specs=pl.BlockSpec((None, tm, tn), lambda i, j: (layer, i, oc0 + j)),
        out_shape=_sds(stack.shape, stack.dtype), input_output_aliases={2: 0},
        compiler_params=_cparams(("parallel", "parallel")), name=name)(a, b, stack)


def _row_tile(s):
    return min(256, s)


def _norm_mod_f(x, g, scale, shift):
    r = lax.rsqrt(jnp.mean(x * x, axis=-1, keepdims=True) + EPS)
    return (x * r * g) * (1.0 + scale) + shift


def norm_mod(x, g, scale, shift, *, name):
    s = x.shape[0]
    t = _row_tile(s)

    def body(x_ref, g_ref, sc_ref, sh_ref, o_ref):
        o_ref[...] = _norm_mod_f(x_ref[...], g_ref[...], sc_ref[...], sh_ref[...]).astype(o_ref.dtype)

    vec = pl.BlockSpec((1, D), lambda i: (0, 0))
    return pl.pallas_call(
        body, grid=(s // t,), in_specs=[pl.BlockSpec((t, D), lambda i: (i, 0)), vec, vec, vec],
        out_specs=pl.BlockSpec((t, D), lambda i: (i, 0)), out_shape=_sds((s, D), MMT),
        compiler_params=_cparams(("parallel",)), name=name)(x, g, scale, shift)


def norm_mod_bwd(x, dh, dres, g, scale, shift, *, name):
    s = x.shape[0]
    t = _row_tile(s)

    def body(x_ref, dh_ref, dr_ref, g_ref, sc_ref, sh_ref, dx_ref, dg_ref, dsc_ref, dsh_ref):
        @pl.when(pl.program_id(0) == 0)
        def _():
            dg_ref[...] = jnp.zeros_like(dg_ref)
            dsc_ref[...] = jnp.zeros_like(dsc_ref)
            dsh_ref[...] = jnp.zeros_like(dsh_ref)

        _, vjp = jax.vjp(_norm_mod_f, x_ref[...], g_ref[...], sc_ref[...], sh_ref[...])
        dx, dg, dsc, dsh = vjp(dh_ref[...])
        dx_ref[...] = dr_ref[...] + dx
        dg_ref[...] += dg
        dsc_ref[...] += dsc
        dsh_ref[...] += dsh

    row = pl.BlockSpec((t, D), lambda i: (i, 0))
    vec = pl.BlockSpec((1, D), lambda i: (0, 0))
    return pl.pallas_call(
        body, grid=(s // t,), in_specs=[row, row, row, vec, vec, vec], out_specs=[row, vec, vec, vec],
        out_shape=[_sds((s, D), F32)] + [_sds((1, D), F32)] * 3,
        compiler_params=_cparams(("arbitrary",)), name=name)(x, dh, dres, g, scale, shift)


def gate_bwd(dx, y, gate, *, name):
    s = dx.shape[0]
    t = _row_tile(s)

    def body(dx_ref, y_ref, g_ref, dy_ref, dg_ref):
        @pl.when(pl.program_id(0) == 0)
        def _():
            dg_ref[...] = jnp.zeros_like(dg_ref)

        dxv = dx_ref[...]
        dy_ref[...] = (g_ref[...] * dxv).astype(dy_ref.dtype)
        dg_ref[...] += jnp.sum(dxv * y_ref[...], axis=0, keepdims=True)

    row = pl.BlockSpec((t, D), lambda i: (i, 0))
    vec = pl.BlockSpec((1, D), lambda i: (0, 0))
    return pl.pallas_call(
        body, grid=(s // t,), in_specs=[row, row, vec], out_specs=[row, vec],
        out_shape=[_sds((s, D), MMT), _sds((1, D), F32)],
        compiler_params=_cparams(("arbitrary",)), name=name)(dx, y, gate)


def loss_and_grad(xf, target, *, name):
    s = xf.shape[0]
    t = _row_tile(s)

    def body(x_ref, t_ref, l_ref, dx_ref):
        @pl.when(pl.program_id(0) == 0)
        def _():
            l_ref[...] = jnp.zeros_like(l_ref)

        e = x_ref[...] - t_ref[...]
        dx_ref[...] = e * (1.0 / D)
        l_ref[...] += 0.5 * jnp.sum(jnp.sum(e * e, axis=1, keepdims=True), axis=0, keepdims=True) * (1.0 / D)

    row = pl.BlockSpec((t, D), lambda i: (i, 0))
    return pl.pallas_call(
        body, grid=(s // t,), in_specs=[row, row], out_specs=[pl.BlockSpec((1, 1), lambda i: (0, 0)), row],
        out_shape=[_sds((1, 1), F32), _sds((s, D), F32)],
        compiler_params=_cparams(("arbitrary",)), name=name)(xf, target)


def _ret_consts():
    log_g = np.log1p(-np.exp2(-5.0 - np.arange(NH, dtype=np.float32))).astype(np.float32)
    idx = np.arange(CH, dtype=np.float32)
    d_intra = np.exp(np.abs(idx[:, None] - idx[None, :])[None] * log_g[:, None, None]).astype(np.float32)
    k_w = np.exp((CH - 1.0 - idx)[None, :] * log_g[:, None]).astype(np.float32)
    q_w = np.exp((idx + 1.0)[None, :] * log_g[:, None]).astype(np.float32)
    g_chunk = [float(v) for v in np.exp(np.float32(CH) * log_g).astype(np.float32)]
    bc = lambda a: np.ascontiguousarray(np.broadcast_to(a[:, :, None], (NH, CH, HD)))
    return jnp.asarray(d_intra), jnp.asarray(bc(k_w)), jnp.asarray(bc(q_w)), g_chunk


def _rope_tables(s):
    half = HD // 2
    inv_freq = (ROPE_BASE ** (-np.arange(half, dtype=np.float64) / half)).astype(np.float32)
    ang = (np.arange(s, dtype=np.float32)[:, None] * inv_freq[None, :]).astype(np.float64)
    cos, sin = np.cos(ang).astype(np.float32), np.sin(ang).astype(np.float32)
    return jnp.asarray(np.concatenate([cos, cos], axis=1)), jnp.asarray(np.concatenate([-sin, sin], axis=1))


def _ret_chunk(qs, ks, vs, rs, cos, sin, dintra, kw, qw, g_chunk):
    outs, rn = [], []
    for h in range(NH):
        q = qs[h] * cos + _swap_halves(qs[h]) * sin
        k = (ks[h] * cos + _swap_halves(ks[h]) * sin) * (HD ** -0.5)
        sc = _bdot_nt(q, k) * dintra[h]
        outs.append(_bdot(sc, vs[h]) + _bdot(q * qw[h], rs[h]))
        rn.append(g_chunk[h] * rs[h] + _bdot_tn(k * kw[h], vs[h]))
    return outs, rn


def _heads(x):
    return [x[:, h * HD:(h + 1) * HD] for h in range(NH)]


def retention_fwd(zz, cosf, sinf, *, name):
    s = zz.shape[0]
    n = s // CH
    dintra, kw, qw, g_chunk = _ret_consts()

    def body(q_ref, k_ref, v_ref, c_ref, s_ref, di_ref, kw_ref, qw_ref, o_ref, rp_ref, r_scr):
        @pl.when(pl.program_id(0) == 0)
        def _():
            r_scr[...] = jnp.zeros_like(r_scr)

        rprev = r_scr[...]
        rp_ref[0] = rprev
        outs, rn = _ret_chunk(_heads(q_ref[...]), _heads(k_ref[...]), _heads(v_ref[...]),
                              [rprev[h * HD:(h + 1) * HD] for h in range(NH)], c_ref[...], s_ref[...],
                              [di_ref[h] for h in range(NH)], [kw_ref[h] for h in range(NH)],
                              [qw_ref[h] for h in range(NH)], g_chunk)
        o_ref[...] = jnp.concatenate(outs, axis=1)
        r_scr[...] = jnp.concatenate(rn, axis=0)

    col = lambda c: pl.BlockSpec((CH, BW), lambda i: (i, c // BW))
    tab = pl.BlockSpec((CH, HD), lambda i: (i, 0))
    cst = lambda shp: pl.BlockSpec(shp, lambda i: (0,) * len(shp))
    return pl.pallas_call(
        body, grid=(n,),
        in_specs=[col(RQ), col(RK), col(RV), tab, tab, cst((NH, CH, CH)), cst((NH, CH, HD)), cst((NH, CH, HD))],
        out_specs=[pl.BlockSpec((CH, BW), lambda i: (i, 0)), pl.BlockSpec((1, BW, HD), lambda i: (i, 0, 0))],
        out_shape=[_sds((s, BW), F32), _sds((n, BW, HD), F32)],
        scratch_shapes=[pltpu.VMEM((BW, HD), F32)],
        compiler_params=_cparams(("arbitrary",)), name=name)(zz, zz, zz, cosf, sinf, dintra, kw, qw)


def retention_bwd(zz, cosf, sinf, rprev, do, dzz, *, name):
    s = zz.shape[0]
    n = s // CH
    dintra, kw, qw, g_chunk = _ret_consts()

    def body(q_ref, k_ref, v_ref, c_ref, s_ref, di_ref, kw_ref, qw_ref, rp_ref, do_ref, dzz_ref, dz_ref, dr_scr):
        @pl.when(pl.program_id(0) == 0)
        def _():
            dr_scr[...] = jnp.zeros_like(dr_scr)

        rprev_v = rp_ref[0]
        f = functools.partial(_ret_chunk, cos=c_ref[...], sin=s_ref[...],
                              dintra=[di_ref[h] for h in range(NH)], kw=[kw_ref[h] for h in range(NH)],
                              qw=[qw_ref[h] for h in range(NH)], g_chunk=g_chunk)
        _, vjp = jax.vjp(f, _heads(q_ref[...]), _heads(k_ref[...]), _heads(v_ref[...]),
                         [rprev_v[h * HD:(h + 1) * HD] for h in range(NH)])
        dr = dr_scr[...]
        dq, dk, dv, drp = vjp((_heads(do_ref[...]), [dr[h * HD:(h + 1) * HD] for h in range(NH)]))
        dz_ref[...] = jnp.concatenate(dq + dk + dv, axis=1).astype(dz_ref.dtype)
        dr_scr[...] = jnp.concatenate(drp, axis=0)

    col = lambda c: pl.BlockSpec((CH, BW), lambda i: (n - 1 - i, c // BW))
    tab = pl.BlockSpec((CH, HD), lambda i: (n - 1 - i, 0))
    cst = lambda shp: pl.BlockSpec(shp, lambda i: (0,) * len(shp))
    return pl.pallas_call(
        body, grid=(n,),
        in_specs=[col(RQ), col(RK), col(RV), tab, tab, cst((NH, CH, CH)), cst((NH, CH, HD)), cst((NH, CH, HD)),
                  pl.BlockSpec((1, BW, HD), lambda i: (n - 1 - i, 0, 0)), pl.BlockSpec((CH, BW), lambda i: (n - 1 - i, 0)),
                  pl.BlockSpec(memory_space=pl.ANY)],
        out_specs=pl.BlockSpec((CH, 3 * BW), lambda i: (n - 1 - i, RQ // (3 * BW))),
        out_shape=_sds(dzz.shape, dzz.dtype), input_output_aliases={10: 0},
        scratch_shapes=[pltpu.VMEM((BW, HD), F32)],
        compiler_params=_cparams(("arbitrary",)), name=name)(zz, zz, zz, cosf, sinf, dintra, kw, qw, rprev, do, dzz)


GKW = NH * GDK


def _gla_consts():
    tri = np.tril(np.ones((CH, CH), np.float32))
    mask_t = np.zeros((BW, GKW), np.float32)
    for h in range(NH):
        mask_t[h * HD:(h + 1) * HD, h * GDK:(h + 1) * GDK] = 1.0
    return jnp.asarray(tri), jnp.asarray(mask_t)


def _gla_chunk(q, k, v, glr, w_a2, b_a, st, tri, mask_t):
    la = _log_sigmoid(_bdot(glr, w_a2) + b_a) * (1.0 / 16.0)
    bc = _dot(tri, la, HI)
    be = jnp.sum(la, axis=0, keepdims=True)
    kv_t = _bdot_tn(v, k * jnp.exp(be - bc)) * mask_t
    sn = jnp.exp(be) * st + kv_t
    return _bdot_nt(q * (GDK ** -0.5), sn), sn


def gla_fwd(zz, w_a2p, b_a, *, name):
    s = zz.shape[0]
    n = s // CH
    tri, mask_t = _gla_consts()

    def body(q_ref, k_ref, v_ref, lr_ref, w_ref, b_ref, tri_ref, m_ref, o_ref, sp_ref, st_scr):
        @pl.when(pl.program_id(0) == 0)
        def _():
            st_scr[...] = jnp.zeros_like(st_scr)

        sp = st_scr[...]
        sp_ref[0] = sp
        o, sn = _gla_chunk(q_ref[...], k_ref[...], v_ref[...], lr_ref[...], w_ref[...], b_ref[...], sp,
                           tri_ref[...], m_ref[...])
        o_ref[...] = o
        st_scr[...] = sn

    cst = lambda shp: pl.BlockSpec(shp, lambda i: (0,) * len(shp))
    return pl.pallas_call(
        body, grid=(n,),
        in_specs=[pl.BlockSpec((CH, GKW), lambda i: (i, GQ // GKW)), pl.BlockSpec((CH, GKW), lambda i: (i, GK // GKW)),
                  pl.BlockSpec((CH, BW), lambda i: (i, GV // BW)), pl.BlockSpec((CH, HD), lambda i: (i, LR // HD)),
                  cst((HD, GKW)), cst((1, GKW)), cst((CH, CH)), cst((BW, GKW))],
        out_specs=[pl.BlockSpec((CH, BW), lambda i: (i, 0)), pl.BlockSpec((1, BW, GKW), lambda i: (i, 0, 0))],
        out_shape=[_sds((s, BW), F32), _sds((n, BW, GKW), F32)],
        scratch_shapes=[pltpu.VMEM((BW, GKW), F32)],
        compiler_params=_cparams(("arbitrary",)), name=name)(zz, zz, zz, zz, w_a2p, b_a, tri, mask_t)


def gla_bwd(zz, w_a2p, b_a, sprev, do, dzz, *, name):
    s = zz.shape[0]
    n = s // CH
    tri, mask_t = _gla_consts()

    def body(q_ref, k_ref, v_ref, lr_ref, w_ref, b_ref, tri_ref, m_ref, sp_ref, do_ref, dzz_ref,
             dz_ref, dlr_ref, dw_ref, db_ref, ds_scr):
        @pl.when(pl.program_id(0) == 0)
        def _():
            ds_scr[...] = jnp.zeros_like(ds_scr)
            dw_ref[...] = jnp.zeros_like(dw_ref)
            db_ref[...] = jnp.zeros_like(db_ref)

        f = functools.partial(_gla_chunk, tri=tri_ref[...], mask_t=m_ref[...])
        _, vjp = jax.vjp(f, q_ref[...], k_ref[...], v_ref[...], lr_ref[...], w_ref[...], b_ref[...], sp_ref[0])
        dq, dk, dv, dlr, dw, db, dsp = vjp((do_ref[...], ds_scr[...]))
        dz_ref[...] = jnp.concatenate([dq, dk, dv], axis=1).astype(dz_ref.dtype)
        dlr_ref[...] = dlr.astype(dlr_ref.dtype)
        dw_ref[...] += dw
        db_ref[...] += db
        ds_scr[...] = dsp

    cst = lambda shp: pl.BlockSpec(shp, lambda i: (0,) * len(shp))
    r = lambda i: n - 1 - i
    return pl.pallas_call(
        body, grid=(n,),
        in_specs=[pl.BlockSpec((CH, GKW), lambda i: (r(i), GQ // GKW)), pl.BlockSpec((CH, GKW), lambda i: (r(i), GK // GKW)),
                  pl.BlockSpec((CH, BW), lambda i: (r(i), GV // BW)), pl.BlockSpec((CH, HD), lambda i: (r(i), LR // HD)),
                  cst((HD, GKW)), cst((1, GKW)), cst((CH, CH)), cst((BW, GKW)),
                  pl.BlockSpec((1, BW, GKW), lambda i: (r(i), 0, 0)), pl.BlockSpec((CH, BW), lambda i: (r(i), 0)),
                  pl.BlockSpec(memory_space=pl.ANY)],
        out_specs=[pl.BlockSpec((CH, 2 * GKW + BW), lambda i: (r(i), GQ // (2 * GKW + BW))),
                   pl.BlockSpec((CH, HD), lambda i: (r(i), 0)), cst((HD, GKW)), cst((1, GKW))],
        out_shape=[_sds(dzz.shape, dzz.dtype), _sds((s, HD), MMT), _sds((HD, GKW), F32), _sds((1, GKW), F32)],
        input_output_aliases={10: 0}, scratch_shapes=[pltpu.VMEM((BW, GKW), F32)],
        compiler_params=_cparams(("arbitrary",)), name=name)(zz, zz, zz, zz, w_a2p, b_a, tri, mask_t, sprev, do, dzz)


def _fox_pre_f(fqs, fks, ff, gq, gk, bf):
    def rms(x, g):
        return x * lax.rsqrt(jnp.mean(x * x, axis=-1, keepdims=True) + EPS) * g

    qn = [rms(x, gq) * (HD ** -0.5) for x in fqs]
    kn = [rms(x, gk) for x in fks]
    return qn, kn, _log_sigmoid(ff + bf)


def fox_pre(zz, gq, gk, bf, *, name):
    s = zz.shape[0]
    t = _row_tile(s)
    tri = jnp.asarray(np.tril(np.ones((t, t), np.float32)))

    def body(q_ref, k_ref, f_ref, gq_ref, gk_ref, b_ref, tri_ref, qn_ref, kn_ref, cum_ref, carry):
        @pl.when(pl.program_id(0) == 0)
        def _():
            carry[...] = jnp.zeros_like(carry)

        qn, kn, lf = _fox_pre_f(_heads(q_ref[...]), _heads(k_ref[...]), f_ref[...], gq_ref[...], gk_ref[...], b_ref[...])
        qn_ref[...] = jnp.concatenate(qn, axis=1).astype(qn_ref.dtype)
        kn_ref[...] = jnp.concatenate(kn, axis=1).astype(kn_ref.dtype)
        cum_ref[...] = _dot(tri_ref[...], lf, HI) + carry[...]
        carry[...] += jnp.sum(lf, axis=0, keepdims=True)

    vec = pl.BlockSpec((1, HD), lambda i: (0, 0))
    return pl.pallas_call(
        body, grid=(s // t,),
        in_specs=[pl.BlockSpec((t, BW), lambda i: (i, FQ // BW)), pl.BlockSpec((t, BW), lambda i: (i, FK // BW)),
                  pl.BlockSpec((t, HD), lambda i: (i, FF // HD)), vec, vec, vec, pl.BlockSpec((t, t), lambda i: (0, 0))],
        out_specs=[pl.BlockSpec((t, BW), lambda i: (i, 0)), pl.BlockSpec((t, BW), lambda i: (i, 0)),
                   pl.BlockSpec((t, HD), lambda i: (i, 0))],
        out_shape=[_sds((s, BW), MMT), _sds((s, BW), MMT), _sds((s, HD), F32)],
        scratch_shapes=[pltpu.VMEM((1, HD), F32)],
        compiler_params=_cparams(("arbitrary",)), name=name)(zz, zz, zz, gq, gk, bf, tri)


def fox_pre_bwd(zz, gq, gk, bf, dqn, dkn, dcum, dzz, *, name):
    s = zz.shape[0]
    t = _row_tile(s)
    nt = s // t
    triu = jnp.asarray(np.triu(np.ones((t, t), np.float32)))

    def body(q_ref, k_ref, f_ref, gq_ref, gk_ref, b_ref, tri_ref, dqn_ref, dkn_ref, dcum_ref, dzz_ref,
             dz_ref, dff_ref, dgq_ref, dgk_ref, db_ref, carry):
        @pl.when(pl.program_id(0) == 0)
        def _():
            carry[...] = jnp.zeros_like(carry)
            dgq_ref[...] = jnp.zeros_like(dgq_ref)
            dgk_ref[...] = jnp.zeros_like(dgk_ref)
            db_ref[...] = jnp.zeros_like(db_ref)

        dcum_v = dcum_ref[...]
        dlf = _dot(tri_ref[...], dcum_v, HI) + carry[...]
        carry[...] += jnp.sum(dcum_v, axis=0, keepdims=True)
        _, vjp = jax.vjp(_fox_pre_f, _heads(q_ref[...]), _heads(k_ref[...]), f_ref[...], gq_ref[...], gk_ref[...], b_ref[...])
        dq, dk, dff, dgq, dgk, db = vjp((_heads(dqn_ref[...]), _heads(dkn_ref[...]), dlf))
        dz_ref[...] = jnp.concatenate(dq + dk, axis=1).astype(dz_ref.dtype)
        dff_ref[...] = dff.astype(dff_ref.dtype)
        dgq_ref[...] += dgq
        dgk_ref[...] += dgk
        db_ref[...] += db

    r = lambda i: nt - 1 - i
    vec = pl.BlockSpec((1, HD), lambda i: (0, 0))
    return pl.pallas_call(
        body, grid=(nt,),
        in_specs=[pl.BlockSpec((t, BW), lambda i: (r(i), FQ // BW)), pl.BlockSpec((t, BW), lambda i: (r(i), FK // BW)),
                  pl.BlockSpec((t, HD), lambda i: (r(i), FF // HD)), vec, vec, vec, pl.BlockSpec((t, t), lambda i: (0, 0)),
                  pl.BlockSpec((t, BW), lambda i: (r(i), 0)), pl.BlockSpec((t, BW), lambda i: (r(i), 0)),
                  pl.BlockSpec((t, HD), lambda i: (r(i), 0)), pl.BlockSpec(memory_space=pl.ANY)],
        out_specs=[pl.BlockSpec((t, 2 * BW), lambda i: (r(i), FQ // (2 * BW))), pl.BlockSpec((t, HD), lambda i: (r(i), 0)),
                   vec, vec, vec],
        out_shape=[_sds(dzz.shape, dzz.dtype), _sds((s, HD), MMT), _sds((1, HD), F32), _sds((1, HD), F32), _sds((1, HD), F32)],
        input_output_aliases={10: 0}, scratch_shapes=[pltpu.VMEM((1, HD), F32)],
        compiler_params=_cparams(("arbitrary",)), name=name)(zz, zz, zz, gq, gk, bf, triu, dqn, dkn, dcum, dzz)


def _fox_blocks(s):
    return min(256, s), min(512, s)


NEG = -1e30


def fox_attn_fwd(qn, kn, zz, cum_col, cum_row, *, name):
    s = qn.shape[0]
    bq, bk = _fox_blocks(s)

    def body(q_ref, k_ref, v_ref, cc_ref, cr_ref, o_ref, lse_ref):
        qi = pl.program_id(1)
        q = q_ref[...]
        cq = cc_ref[...]
        rows = qi * bq + lax.broadcasted_iota(jnp.int32, (bq, bk), 0)
        cols0 = lax.broadcasted_iota(jnp.int32, (bq, bk), 1)

        def step(j, carry, on_diagonal):
            m, l, acc = carry
            off = pl.multiple_of(j * bk, bk)
            k = k_ref[pl.ds(off, bk), :]
            v = v_ref[pl.ds(off, bk), :].astype(MMT)
            sc = _dot_nt(q, k) + cq - cr_ref[pl.ds(j, 1), :]
            if on_diagonal:
                sc = jnp.where(rows >= cols0 + j * bk, sc, NEG)
            m_new = jnp.maximum(m, jnp.max(sc, axis=1, keepdims=True))
            alpha = jnp.exp(m - m_new)
            p = jnp.exp(sc - m_new)
            return m_new, alpha * l + jnp.sum(p, axis=1, keepdims=True), alpha * acc + _dot(p.astype(MMT), v)

        nfull, nk = (qi * bq + 1) // bk, ((qi + 1) * bq + bk - 1) // bk
        carry = (jnp.full((bq, 1), NEG, F32), jnp.zeros((bq, 1), F32), jnp.zeros((bq, HD), F32))
        carry = lax.fori_loop(0, nfull, functools.partial(step, on_diagonal=False), carry)
        m, l, acc = lax.fori_loop(nfull, nk, functools.partial(step, on_diagonal=True), carry)
        o_ref[...] = acc / l
        lse_ref[...] = m + jnp.log(l)

    return pl.pallas_call(
        body, grid=(NH, s // bq),
        in_specs=[pl.BlockSpec((bq, HD), lambda h, i: (i, h)), pl.BlockSpec((s, HD), lambda h, i: (0, h)),
                  pl.BlockSpec((s, HD), lambda h, i: (0, FV // HD + h)),
                  pl.BlockSpec((None, bq, 1), lambda h, i: (h, i, 0)), pl.BlockSpec((None, s // bk, bk), lambda h, i: (h, 0, 0))],
        out_specs=[pl.BlockSpec((bq, HD), lambda h, i: (i, h)), pl.BlockSpec((None, bq, 1), lambda h, i: (h, i, 0))],
        out_shape=[_sds((s, BW), F32), _sds((NH, s, 1), F32)],
        compiler_params=_cparams(("parallel", "parallel")), name=name)(qn, kn, zz, cum_col, cum_row)


def fox_attn_bwd(qn, kn, zz, cum_col, cum_row, lse, do, dzz, *, name):
    s = qn.shape[0]
    bq, bk = _fox_blocks(s)
    nkc = s // bk

    def body(q_ref, k_ref, v_ref, cc_ref, cr_ref, lse_ref, do_ref, dzz_ref, dq_ref, dk_ref, dv_ref, dc_ref,
             p_scr, dp_scr, dv_scr):
        qi = pl.program_id(1)

        @pl.when(qi == 0)
        def _():
            dk_ref[...] = jnp.zeros_like(dk_ref)
            dv_scr[...] = jnp.zeros_like(dv_scr)
            dc_ref[...] = jnp.zeros_like(dc_ref)

        q = q_ref[...]
        dob = do_ref[...].astype(MMT)
        cq = cc_ref[...]
        lse_v = lse_ref[...]
        rows = qi * bq + lax.broadcasted_iota(jnp.int32, (bq, bk), 0)
        cols0 = lax.broadcasted_iota(jnp.int32, (bq, bk), 1)
        nfull, nk = (qi * bq + 1) // bk, ((qi + 1) * bq + bk - 1) // bk

        def probs(j, delta, on_diagonal):
            off = pl.multiple_of(j * bk, bk)
            sc = _dot_nt(q, k_ref[pl.ds(off, bk), :]) + cq - cr_ref[pl.ds(j, 1), :]
            p = jnp.exp(sc - lse_v)
            if on_diagonal:
                p = jnp.where(rows >= cols0 + j * bk, p, 0.0)
            dp = _dot_nt(dob, v_ref[pl.ds(off, bk), :].astype(MMT))
            p_scr[j] = p
            dp_scr[j] = dp
            return delta + jnp.sum(p * dp, axis=1, keepdims=True)

        delta = lax.fori_loop(0, nfull, functools.partial(probs, on_diagonal=False), jnp.zeros((bq, 1), F32))
        delta = lax.fori_loop(nfull, nk, functools.partial(probs, on_diagonal=True), delta)

        def grads(j, dq):
            off = pl.multiple_of(j * bk, bk)
            p = p_scr[j]
            ds = p * (dp_scr[j] - delta)
            dsm = ds.astype(MMT)
            dv_scr[pl.ds(off, bk), :] += _dot_tn(p.astype(MMT), dob)
            dk_ref[pl.ds(off, bk), :] += _dot_tn(dsm, q)
            dc_ref[pl.ds(j, 1), :] -= jnp.sum(ds, axis=0, keepdims=True)
            return dq + _dot(dsm, k_ref[pl.ds(off, bk), :])

        dq_ref[...] = lax.fori_loop(0, nk, grads, jnp.zeros((bq, HD), F32))

        @pl.when(qi == pl.num_programs(1) - 1)
        def _():
            dv_ref[...] = dv_scr[...].astype(dv_ref.dtype)

    full = lambda c0=0: pl.BlockSpec((s, HD), lambda h, i: (0, c0 + h))
    blk = lambda: pl.BlockSpec((bq, HD), lambda h, i: (i, h))
    colv = lambda: pl.BlockSpec((None, bq, 1), lambda h, i: (h, i, 0))
    rowv = lambda: pl.BlockSpec((None, nkc, bk), lambda h, i: (h, 0, 0))
    return pl.pallas_call(
        body, grid=(NH, s // bq),
        in_specs=[blk(), full(), full(FV // HD), colv(), rowv(), colv(), blk(), pl.BlockSpec(memory_space=pl.ANY)],
        out_specs=[blk(), full(), full(FV // HD), rowv()],
        out_shape=[_sds((s, BW), F32), _sds((s, BW), F32), _sds(dzz.shape, dzz.dtype), _sds((NH, nkc, bk), F32)],
        input_output_aliases={7: 2},
        scratch_shapes=[pltpu.VMEM((nkc, bq, bk), F32), pltpu.VMEM((nkc, bq, bk), F32), pltpu.VMEM((s, HD), F32)],
        compiler_params=_cparams(("parallel", "arbitrary")), name=name)(qn, kn, zz, cum_col, cum_row, lse, do, dzz)


def _branch_f(rets, rgs, glas, ggs, ret_g, gla_g):
    out_r, out_g = [], []
    for h in range(NH):
        xc = rets[h] - jnp.mean(rets[h], axis=-1, keepdims=True)
        y = xc * lax.rsqrt(jnp.mean(xc * xc, axis=-1, keepdims=True) + EPS) * ret_g[h]
        out_r.append(_silu(rgs[h]) * y)
        x = glas[h]
        y = x * lax.rsqrt(jnp.mean(x * x, axis=-1, keepdims=True) + EPS) * gla_g
        out_g.append(_silu(ggs[h]) * y)
    return out_r, out_g


def _w_br_spec(layer):
    return pl.BlockSpec((None, 3, BW, D), lambda i: (layer, 0, 0, 0))


def mix_fwd(ret_raw, gla_raw, fox_o, zz, ret_g, gla_g, b_mg, w_br, *, name, layer):
    s = zz.shape[0]
    t = _row_tile(s)

    def body(r_ref, g_ref, f_ref, rg_ref, gg_ref, gp_ref, rgn_ref, ggn_ref, bmg_ref, w_ref, o_ref):
        rgn = rgn_ref[...]
        br_r, br_g = _branch_f(_heads(r_ref[...]), _heads(rg_ref[...]), _heads(g_ref[...]), _heads(gg_ref[...]),
                               _heads(rgn), ggn_ref[...])
        brs = [jnp.concatenate(br_r, axis=1), jnp.concatenate(br_g, axis=1), f_ref[...]]
        acc = jnp.zeros((t, D), F32)
        for b in range(3):
            gate = jax.nn.sigmoid(gp_ref[:, b * D:(b + 1) * D] + bmg_ref[:, b * D:(b + 1) * D])
            acc = acc + gate * _dot(brs[b].astype(MMT), w_ref[b])
        o_ref[...] = acc.astype(o_ref.dtype)

    row = lambda w, c=0: pl.BlockSpec((t, w), lambda i: (i, c // w))
    cst = lambda shp: pl.BlockSpec(shp, lambda i: (0,) * len(shp))
    return pl.pallas_call(
        body, grid=(s // t,),
        in_specs=[row(BW), row(BW), row(BW), row(BW, RG), row(BW, GG), row(3 * D, GP), cst((1, BW)), cst((1, HD)),
                  cst((1, 3 * D)), _w_br_spec(layer)],
        out_specs=row(D), out_shape=_sds((s, D), MMT),
        compiler_params=_cparams(("parallel",)), name=name)(ret_raw, gla_raw, fox_o, zz, zz, zz, ret_g, gla_g, b_mg, w_br)


def mix_bwd(ret_raw, gla_raw, fox_o, zz, ret_g, gla_g, b_mg, w_br, dmi, *, name, layer):
    s = zz.shape[0]
    t = _row_tile(s)

    def body(r_ref, g_ref, f_ref, rg_ref, gg_ref, gp_ref, rgn_ref, ggn_ref, bmg_ref, w_ref, dmi_ref,
             dr_ref, dg_ref, df_ref, dgp_ref, dw_ref, drgn_ref, dggn_ref, dbmg_ref):
        @pl.when(pl.program_id(0) == 0)
        def _():
            dw_ref[...] = jnp.zeros_like(dw_ref)
            drgn_ref[...] = jnp.zeros_like(drgn_ref)
            dggn_ref[...] = jnp.zeros_like(dggn_ref)
            dbmg_ref[...] = jnp.zeros_like(dbmg_ref)

        (br_r, br_g), vjp = jax.vjp(_branch_f, _heads(r_ref[...]), _heads(rg_ref[...]), _heads(g_ref[...]),
                                    _heads(gg_ref[...]), _heads(rgn_ref[...]), ggn_ref[...])
        brs = [jnp.concatenate(br_r, axis=1).astype(MMT), jnp.concatenate(br_g, axis=1).astype(MMT),
               f_ref[...].astype(MMT)]
        dmi_v = dmi_ref[...].astype(F32)
        dbr = []
        for b in range(3):
            w = w_ref[b]
            ybr = _dot(brs[b], w)
            gate = jax.nn.sigmoid(gp_ref[:, b * D:(b + 1) * D] + bmg_ref[:, b * D:(b + 1) * D])
            dgp = dmi_v * ybr * gate * (1.0 - gate)
            dgp_ref[:, b * D:(b + 1) * D] = dgp.astype(dgp_ref.dtype)
            dbmg_ref[:, b * D:(b + 1) * D] += jnp.sum(dgp, axis=0, keepdims=True)
            dy = (dmi_v * gate).astype(MMT)
            dw_ref[b] += _dot_tn(brs[b], dy)
            dbr.append(_dot_nt(dy, w))
        dr, drg, dg, dgg, drgn, dggn = vjp((_heads(dbr[0]), _heads(dbr[1])))
        dr_ref[...] = jnp.concatenate(dr, axis=1)
        dg_ref[...] = jnp.concatenate(dg, axis=1)
        df_ref[...] = dbr[2]
        dgp_ref[:, RG:RG + BW] = jnp.concatenate(drg, axis=1).astype(dgp_ref.dtype)
        dgp_ref[:, GG:GG + BW] = jnp.concatenate(dgg, axis=1).astype(dgp_ref.dtype)
        drgn_ref[...] += jnp.concatenate(drgn, axis=1)
        dggn_ref[...] += dggn

    row = lambda w, c=0: pl.BlockSpec((t, w), lambda i: (i, c // w))
    cst = lambda shp: pl.BlockSpec(shp, lambda i: (0,) * len(shp))
    return pl.pallas_call(
        body, grid=(s // t,),
        in_specs=[row(BW), row(BW), row(BW), row(BW, RG), row(BW, GG), row(3 * D, GP), cst((1, BW)), cst((1, HD)),
                  cst((1, 3 * D)), _w_br_spec(layer), row(D)],
        out_specs=[row(BW), row(BW), row(BW), row(FV), cst((3, BW, D)), cst((1, BW)), cst((1, HD)), cst((1, 3 * D))],
        out_shape=[_sds((s, BW), F32)] * 3 + [_sds((s, NZZ), MMT), _sds((3, BW, D), F32), _sds((1, BW), F32),
                                              _sds((1, HD), F32), _sds((1, 3 * D), F32)],
        compiler_params=_cparams(("arbitrary",)), name=name)(ret_raw, gla_raw, fox_o, zz, zz, zz, ret_g, gla_g, b_mg, w_br, dmi)


CT = 256


def _shift_down(x, k, rows):
    return jnp.where(rows >= k, pltpu.roll(x, k, 0), 0.0)


def _shift_up(x, k, rows, s):
    return jnp.where(rows < s - k, pltpu.roll(x, s - k, 0), 0.0)


def conv_fwd(ug, w_conv, b_conv, *, name):
    s = ug.shape[0]
    nt = DFF // CT

    def body(u_ref, g_ref, w_ref, b_ref, a_ref):
        u = u_ref[...]
        rows = lax.broadcasted_iota(jnp.int32, u.shape, 0)
        uc = b_ref[...] + w_ref[0:1, :] * _shift_down(u, 2, rows) + w_ref[1:2, :] * _shift_down(u, 1, rows) + w_ref[2:3, :] * u
        a_ref[...] = (_silu(uc) * g_ref[...]).astype(a_ref.dtype)

    return pl.pallas_call(
        body, grid=(nt,),
        in_specs=[pl.BlockSpec((s, CT), lambda j: (0, j)), pl.BlockSpec((s, CT), lambda j: (0, nt + j)),
                  pl.BlockSpec((3, CT), lambda j: (0, j)), pl.BlockSpec((1, CT), lambda j: (0, j))],
        out_specs=pl.BlockSpec((s, CT), lambda j: (0, j)), out_shape=_sds((s, DFF), MMT),
        compiler_params=_cparams(("parallel",)), name=name)(ug, ug, w_conv, b_conv)


def conv_bwd(ug, w_conv, b_conv, da, *, name):
    s = ug.shape[0]
    nt = DFF // CT

    def body(u_ref, g_ref, w_ref, b_ref, da_ref, du_ref, dg_ref, dw_ref, db_ref):
        u = u_ref[...]
        rows = lax.broadcasted_iota(jnp.int32, u.shape, 0)
        u2, u1 = _shift_down(u, 2, rows), _shift_down(u, 1, rows)
        uc = b_ref[...] + w_ref[0:1, :] * u2 + w_ref[1:2, :] * u1 + w_ref[2:3, :] * u
        sg = jax.nn.sigmoid(uc)
        da_v = da_ref[...]
        dg_ref[...] = (da_v * uc * sg).astype(dg_ref.dtype)
        duc = da_v * g_ref[...] * sg * (1.0 + uc * (1.0 - sg))
        du = w_ref[2:3, :] * duc + w_ref[1:2, :] * _shift_up(duc, 1, rows, s) + w_ref[0:1, :] * _shift_up(duc, 2, rows, s)
        du_ref[...] = du.astype(du_ref.dtype)
        dw_ref[0:1, :] = jnp.sum(duc * u2, axis=0, keepdims=True)
        dw_ref[1:2, :] = jnp.sum(duc * u1, axis=0, keepdims=True)
        dw_ref[2:3, :] = jnp.sum(duc * u, axis=0, keepdims=True)
        db_ref[...] = jnp.sum(duc, axis=0, keepdims=True)

    col = lambda: pl.BlockSpec((s, CT), lambda j: (0, j))
    return pl.pallas_call(
        body, grid=(nt,),
        in_specs=[col(), pl.BlockSpec((s, CT), lambda j: (0, nt + j)), pl.BlockSpec((3, CT), lambda j: (0, j)),
                  pl.BlockSpec((1, CT), lambda j: (0, j)), col()],
        out_specs=[col(), col(), pl.BlockSpec((3, CT), lambda j: (0, j)), pl.BlockSpec((1, CT), lambda j: (0, j))],
        out_shape=[_sds((s, DFF), MMT), _sds((s, DFF), MMT), _sds((3, DFF), F32), _sds((1, DFF), F32)],
        compiler_params=_cparams(("parallel",)), name=name)(ug, ug, w_conv, b_conv, da)


def place_tail(dzz, dlr, dff, *, name):
    s = dzz.shape[0]
    t = _row_tile(s)

    def body(a_ref, b_ref, z_ref, o_ref):
        o_ref[...] = jnp.concatenate([a_ref[...], b_ref[...]], axis=1)

    spec = pl.BlockSpec((t, HD), lambda i: (i, 0))
    return pl.pallas_call(
        body, grid=(s // t,), in_specs=[spec, spec, pl.BlockSpec(memory_space=pl.ANY)],
        out_specs=pl.BlockSpec((t, 2 * HD), lambda i: (i, LR // (2 * HD))), out_shape=_sds(dzz.shape, dzz.dtype),
        input_output_aliases={2: 0}, compiler_params=_cparams(("parallel",)), name=name)(dlr, dff, dzz)


def _tiles(s):
    return min(1024, s)


def layer_fwd(x, mod, p, cosf, sinf):
    s = x.shape[0]
    tm = _tiles(s)
    l = p["l"]
    shift1, scale1, gate1, shift2, scale2, gate2 = mod
    h = norm_mod(x, p["norm1_g"], scale1, shift1, name="norm_mod")
    zz = mm_nn(h, p["w1"], tm=tm, tn=768, out_dtype=F32, name="mm_w1", layer=l)
    ret_raw, rprev = retention_fwd(zz, cosf, sinf, name="ret_fwd")
    gla_raw, sprev = gla_fwd(zz, p["w_a2p"], p["b_gla_a"], name="gla_fwd")
    qn, kn, cum = fox_pre(zz, p["q_norm_g"], p["k_norm_g"], p["b_foxp"], name="fox_pre")
    bq, bk = _fox_blocks(s)
    cum_t = cum[:, :NH].T
    cum_col, cum_row = cum_t[:, :, None], cum_t.reshape(NH, s // bk, bk)
    fox_o, lse = fox_attn_fwd(qn, kn, zz, cum_col, cum_row, name="fox_fwd")
    mi = mix_fwd(ret_raw, gla_raw, fox_o, zz, p["ret_norm_g"], p["gla_norm_g"], p["b_mg"], p["w_br"], name="mix_fwd",
                 layer=l)
    x1, mixed = mm_nn_residual(mi, p["w_o"], x, gate1, tm=tm, tn=512, name="mm_wo", layer=l)
    h2 = norm_mod(x1, p["norm2_g"], scale2, shift2, name="norm_mod")
    ug = mm_nn(h2, p["w_up"], tm=tm, tn=512, out_dtype=F32, name="mm_wup", layer=l)
    a = conv_fwd(ug, p["w_conv"], p["b_conv"], name="conv_fwd")
    x2, y = mm_nn_residual(a, p["w_down"], x1, gate2, tm=tm, tn=512, name="mm_wdown", layer=l)
    saved = dict(x=x, h=h, zz=zz, ret_raw=ret_raw, rprev=rprev, gla_raw=gla_raw, sprev=sprev, qn=qn, kn=kn,
                 cum_col=cum_col, cum_row=cum_row, fox_o=fox_o, lse=lse, mi=mi, mixed=mixed, x1=x1, h2=h2, ug=ug, a=a, y=y)
    return x2, saved


def layer_bwd(dx2, mod, p, sv, cosf, sinf, stacks, slot):
    s = dx2.shape[0]
    tm = _tiles(s)
    l = p["l"]
    shift1, scale1, gate1, shift2, scale2, gate2 = mod
    g, stacks = {}, dict(stacks)
    dy, dgate2 = gate_bwd(dx2, sv["y"], gate2, name="gate_bwd")
    stacks["w_down"] = mm_tn(sv["a"], dy, tm=min(1408, DFF), tn=512, out_dtype=MMT, name="mm_dwdown",
                             stack=stacks["w_down"], layer=slot)
    da = mm_nt(dy, p["w_down"], tm=tm, tn=1408, out_dtype=F32, name="mm_da", layer=l)
    du, dg, g["w_conv"], g["b_conv"] = conv_bwd(sv["ug"], p["w_conv"], p["b_conv"], da, name="conv_bwd")
    stacks["w_up"] = mm_tn(sv["h2"], du, tm=D, tn=CT, out_dtype=MMT, name="mm_dwup_u", stack=stacks["w_up"], layer=slot)
    stacks["w_up"] = mm_tn(sv["h2"], dg, tm=D, tn=CT, out_dtype=MMT, name="mm_dwup_g", stack=stacks["w_up"], layer=slot,
                           out_col0=DFF)
    dh2 = mm_nt2(du, dg, p["w_up"], tm=min(512, s), tn=D, name="mm_dh2", layer=l)
    dx1, g["norm2_g"], dscale2, dshift2 = norm_mod_bwd(sv["x1"], dh2, dx2, p["norm2_g"], scale2, shift2, name="norm_mod_bwd")
    dmixed, dgate1 = gate_bwd(dx1, sv["mixed"], gate1, name="gate_bwd")
    stacks["w_o"] = mm_tn(sv["mi"], dmixed, tm=512, tn=512, out_dtype=MMT, name="mm_dwo", stack=stacks["w_o"], layer=slot)
    dmi = mm_nt(dmixed, p["w_o"], tm=tm, tn=512, out_dtype=MMT, name="mm_dmi", layer=l)
    zz = sv["zz"]
    (dret, dgla, dfox, dzz, g["w_br"], g["ret_norm_g"], g["gla_norm_g"], g["b_mg"]) = mix_bwd(
        sv["ret_raw"], sv["gla_raw"], sv["fox_o"], zz, p["ret_norm_g"], p["gla_norm_g"], p["b_mg"], p["w_br"], dmi,
        name="mix_bwd", layer=l)
    dqn, dkn, dzz, dcum_row = fox_attn_bwd(sv["qn"], sv["kn"], zz, sv["cum_col"], sv["cum_row"], sv["lse"], dfox, dzz,
                                           name="fox_bwd")
    dcum = jnp.pad(dcum_row.reshape(NH, s).T, ((0, 0), (0, HD - NH)))
    dzz, dff, g["q_norm_g"], g["k_norm_g"], g["b_foxp"] = fox_pre_bwd(
        zz, p["q_norm_g"], p["k_norm_g"], p["b_foxp"], dqn, dkn, dcum, dzz, name="fox_pre_bwd")
    dzz, dlr, g["w_a2p"], g["b_gla_a"] = gla_bwd(zz, p["w_a2p"], p["b_gla_a"], sv["sprev"], dgla, dzz, name="gla_bwd")
    dzz = retention_bwd(zz, cosf, sinf, sv["rprev"], dret, dzz, name="ret_bwd")
    dzz = place_tail(dzz, dlr, dff, name="place_tail")
    stacks["w_mg"] = mm_tn(sv["h"], dzz, tm=512, tn=768, out_dtype=MMT, name="mm_dwmg", ncols=WZ0, stack=stacks["w_mg"],
                           layer=slot)
    dwz = mm_tn(sv["h"], dzz, tm=512, tn=768, out_dtype=MMT, name="mm_dwz", col0=WZ0)
    stacks["w_in"] = unalign_dw_in(dwz, stacks["w_in"], slot)
    dh = mm_nt(dzz, p["w1"], tm=min(256, s), tn=D, out_dtype=F32, name="mm_dh", layer=l)
    dx, g["norm1_g"], dscale1, dshift1 = norm_mod_bwd(sv["x"], dh, dx1, p["norm1_g"], scale1, shift1, name="norm_mod_bwd")
    dmod = jnp.concatenate([dshift1, dscale1, dgate1, dshift2, dscale2, dgate2], axis=1)
    return dx, g, dmod, stacks


def _align_cols(w_in, w_mg):
    z = lambda n: jnp.zeros((w_in.shape[0], n), w_in.dtype)
    seg = lambda name: w_in[:, W_IN_COLS[name][0]:W_IN_COLS[name][1]]
    return jnp.concatenate([w_mg, seg("rg"), seg("gg"), seg("fv"), seg("rqkv"), seg("gqkv"), seg("fqk"), seg("lr"),
                            z(HD - GLR), seg("ff"), z(HD - NH)], axis=1)


def _unalign_cols(dwz):
    seg = lambda c0, name: dwz[:, c0 - WZ0:c0 - WZ0 + W_IN_COLS[name][1] - W_IN_COLS[name][0]]
    return jnp.concatenate([seg(RQ, "rqkv"), seg(RG, "rg"), seg(GQ, "gqkv"), seg(LR, "lr"), seg(GG, "gg"), seg(FQ, "fqk"),
                            seg(FV, "fv"), seg(FF, "ff")], axis=1)


def build_w1(w_in_sh, w_mg):
    nl = w_mg.shape[0]
    t = _row_tile(D)

    def body(s_ref, g_ref, o_ref):
        o_ref[...] = _align_cols(jnp.concatenate([s_ref[k] for k in range(4)], axis=1), g_ref[...])

    return pl.pallas_call(
        body, grid=(nl, D // t),
        in_specs=[pl.BlockSpec((None, 4, t, IN_W // 4), lambda l, i: (l, 0, i, 0)), pl.BlockSpec((None, t, WZ0), lambda l, i: (l, i, 0))],
        out_specs=pl.BlockSpec((None, t, NZZ), lambda l, i: (l, i, 0)), out_shape=_sds((nl, D, NZZ), w_mg.dtype),
        compiler_params=_cparams(("parallel", "parallel")), name="build_w1")(w_in_sh, w_mg)


def unalign_dw_in(dwz, stack, layer):
    t = _row_tile(D)

    def body(z_ref, s_ref, o_ref):
        w = _unalign_cols(z_ref[...])
        for k in range(4):
            o_ref[k] = w[:, k * (IN_W // 4):(k + 1) * (IN_W // 4)]

    return pl.pallas_call(
        body, grid=(D // t,),
        in_specs=[pl.BlockSpec((t, NZZ - WZ0), lambda i: (i, 0)), pl.BlockSpec(memory_space=pl.ANY)],
        out_specs=pl.BlockSpec((None, 4, t, IN_W // 4), lambda i: (layer, 0, i, 0)), out_shape=_sds(stack.shape, stack.dtype),
        input_output_aliases={1: 0}, compiler_params=_cparams(("parallel",)), name="unalign_dw_in")(dwz, stack)


def layer_params(w, big, l):
    row = lambda v: v[l][None, :]
    return dict(
        l=0, norm1_g=row(w["norm1_g"]), norm2_g=row(w["norm2_g"]), w1=big["w1"],
        w_a2p=jnp.pad(w["w_gla_a2"][l], ((0, HD - GLR), (0, 0))), b_gla_a=row(w["b_gla_a"]),
        b_foxp=jnp.pad(row(w["b_fox_f"]), ((0, 0), (0, HD - NH))), ret_norm_g=row(w["ret_norm_g"]),
        gla_norm_g=row(w["gla_norm_g"]), q_norm_g=row(w["q_norm_g"]), k_norm_g=row(w["k_norm_g"]),
        w_br=big["w_br"], b_mg=row(w["b_mg"]), w_o=big["w_o"], w_up=big["w_up"], w_conv=w["w_conv"][l],
        b_conv=row(w["b_conv"]), w_down=big["w_down"])


def layer_grads(g):
    vec = lambda v: v[0]
    return dict(
        norm1_g=vec(g["norm1_g"]), norm2_g=vec(g["norm2_g"]), w_gla_a2=g["w_a2p"][:GLR], b_gla_a=vec(g["b_gla_a"]),
        b_fox_f=g["b_foxp"][0, :NH], ret_norm_g=vec(g["ret_norm_g"]), gla_norm_g=vec(g["gla_norm_g"]),
        q_norm_g=vec(g["q_norm_g"]), k_norm_g=vec(g["k_norm_g"]), w_br=g["w_br"], b_mg=vec(g["b_mg"]),
        w_conv=g["w_conv"], b_conv=vec(g["b_conv"]))


def ada_mod(c_all, w_ada, b_ada):
    nl, _, n = w_ada.shape

    def body(c_ref, w_ref, b_ref, o_ref):
        o_ref[...] = _dot(_silu(c_ref[...]), w_ref[...], HI) + b_ref[...]

    return pl.pallas_call(
        body, grid=(nl,),
        in_specs=[pl.BlockSpec((8, D), lambda l: (0, 0)), pl.BlockSpec((None, D, n), lambda l: (l, 0, 0)),
                  pl.BlockSpec((None, 1, n), lambda l: (l, 0, 0))],
        out_specs=pl.BlockSpec((None, 8, n), lambda l: (l, 0, 0)), out_shape=_sds((nl, 8, n), F32),
        compiler_params=_cparams(("parallel",)), name="ada_mod")(c_all, w_ada, b_ada)


def ada_dw(c_all, dmod):
    nl, _, n = dmod.shape

    def body(c_ref, d_ref, o_ref):
        o_ref[...] = _dot_tn(_silu(c_ref[...]), d_ref[...], HI)

    return pl.pallas_call(
        body, grid=(nl,),
        in_specs=[pl.BlockSpec((8, D), lambda l: (0, 0)), pl.BlockSpec((None, 8, n), lambda l: (l, 0, 0))],
        out_specs=pl.BlockSpec((None, D, n), lambda l: (l, 0, 0)), out_shape=_sds((nl, D, n), F32),
        compiler_params=_cparams(("parallel",)), name="ada_dw")(c_all, dmod)


def sum_devices(g):
    def body(g_ref, o_ref):
        acc = g_ref[0]
        for d in range(1, 8):
            acc = acc + g_ref[d]
        o_ref[...] = acc

    return pl.pallas_call(body, out_shape=_sds(g.shape[1:], F32), name="sum_devices")(g)


def adamw(w, g, m, v, *, block, name, rows=None, into=None):
    nd = w.ndim
    lo, hi = (0, w.shape[0]) if rows is None else rows
    grid = ((hi - lo) // block[0],) + tuple(w.shape[i] // block[i] for i in range(1, nd))
    first = lo // block[0]
    bc1 = 1.0 - ADAM_B1 ** ADAM_STEP
    bc2 = 1.0 - ADAM_B2 ** ADAM_STEP

    def body(w_ref, g_ref, m_ref, v_ref, *rest):
        d_ref, nm_ref, nv_ref = rest[-3:]
        gv = g_ref[...]
        nm = ADAM_B1 * m_ref[...] + (1.0 - ADAM_B1) * gv
        nv = ADAM_B2 * v_ref[...] + (1.0 - ADAM_B2) * (gv * gv)
        nm_ref[...] = nm
        nv_ref[...] = nv
        d_ref[...] = -ADAM_LR * ((nm / bc1) / (jnp.sqrt(nv / bc2) + ADAM_EPS) + ADAM_WD * w_ref[...])

    spec = pl.BlockSpec(tuple(block), lambda i, *j: (first + i,) + j)
    given = [] if into is None else list(into)
    return pl.pallas_call(
        body, grid=grid, in_specs=[spec] * 4 + [pl.BlockSpec(memory_space=pl.ANY)] * len(given), out_specs=[spec] * 3,
        out_shape=[_sds(w.shape, F32)] * 3, input_output_aliases={4 + i: i for i in range(len(given))},
        compiler_params=_cparams(("parallel",) * nd), name=name)(w, g, m, v, *given)


MESH = pl.DeviceIdType.MESH
ANY = pl.BlockSpec(memory_space=pl.ANY)
VM = pl.BlockSpec(memory_space=pltpu.VMEM)


def _place():
    x, y, c = lax.axis_index("x"), lax.axis_index("y"), lax.axis_index("c")
    return x, y, c, [(1 - x, y), (x, 1 - y), (1 - x, 1 - y)]


def small_allgather(v, *, name):
    m_per, n = v.shape

    def body(x_ref, out_ref, send_sems, recv_sems, local_sem):
        x, y, c, chips = _place()
        me, sibling = (x, y, c), (x, y, 1 - c)

        def rows(px, py, pc):
            return out_ref.at[pl.ds((4 * px + 2 * py + pc) * m_per, m_per), :]

        def copy(k, block, to, src=None):
            return pltpu.make_async_remote_copy(
                src_ref=rows(*block) if src is None else src, dst_ref=rows(*block),
                send_sem=send_sems.at[k], recv_sem=recv_sems.at[k], device_id=to, device_id_type=MESH)

        mine = pltpu.make_async_copy(x_ref, rows(*me), local_sem)
        mine.start()
        first = [copy(0, me, sibling, src=x_ref)]
        first += [copy(1 + j, me, (*chip, c), src=x_ref) for j, chip in enumerate(chips)]
        for cp in first:
            cp.start()
        passed = [copy(4 + j, (*chip, c), sibling) for j, chip in enumerate(chips)]
        for j, chip in enumerate(chips):
            copy(1 + j, (*chip, c), me).wait_recv()
            passed[j].start()
        copy(0, sibling, me).wait_recv()
        for j, chip in enumerate(chips):
            copy(4 + j, (*chip, 1 - c), me).wait_recv()
        for cp in first + passed:
            cp.wait_send()
        mine.wait()

    return pl.pallas_call(
        body, out_shape=_sds((8 * m_per, n), v.dtype), in_specs=[VM], out_specs=VM,
        scratch_shapes=[pltpu.SemaphoreType.DMA((7,)), pltpu.SemaphoreType.DMA((7,)), pltpu.SemaphoreType.DMA],
        name=name)(v)


TENSORS = {
    "w_in": ("lead", None, (4, D, 1285), (1, D, 1285)),
    "w_mg": ("col", 768, (D, 3072), (512, 3072)),
    "w_br": ("col", 256, (3, BW, D), (3, BW, D)),
    "w_o": ("row", 256, (D, D), (D, D)),
    "w_up": ("col", 1408, (D, 5632), (256, 5632)),
    "w_down": ("row", 704, (DFF, D), (704, D)),
}
BIG = tuple(TENSORS)


def _shard_shape(name):
    kind, width, full, _ = TENSORS[name]
    if kind == "lead":
        return full[1:]
    return full[:-1] + (width,) if kind == "col" else (width,) + full[1:]


def _shard_view(ref, layers, name, k):
    kind, width, full, _ = TENSORS[name]
    if kind == "lead":
        return ref.at[layers, k]
    if kind == "row":
        return ref.at[layers, pl.ds(k * width, width)]
    return ref.at[(layers,) + (slice(None),) * (len(full) - 1) + (pl.ds(k * width, width),)]


def _remote(send_sems, recv_sems, k, src, dst, to):
    return pltpu.make_async_remote_copy(src_ref=src, dst_ref=dst, send_sem=send_sems.at[k], recv_sem=recv_sems.at[k],
                                        device_id=to, device_id_type=MESH)


def _dma_sems(n):
    return [pltpu.SemaphoreType.DMA((n,)), pltpu.SemaphoreType.DMA((n,))]


RS_GROUP = 2
HBM = pl.BlockSpec(memory_space=pltpu.HBM)
SEM = pl.BlockSpec(memory_space=pltpu.SEMAPHORE)
SPLIT_CALL = dict(compiler_params=pltpu.CompilerParams(has_side_effects=pltpu.SideEffectType.DATAFLOW_SIDE_EFFECTING))
PULL_SET = (("w_up", "w_down", "w_o"), ("w_in", "w_mg", "w_br"))


def _in_hbm(a):
    return pltpu.with_memory_space_constraint(a, pltpu.HBM)


def _pull_sends(send_sems, recv_sems, p, o, layer, core, x, y, chips):
    return [_remote(send_sems, recv_sems, 3 * BIG.index(n) + j, p[n].at[layer], _shard_view(o[n], 0, n, 2 * x + y), (*chip, core))
            for n in PULL_SET[core] for j, chip in enumerate(chips)]


def _pull_arrivals(send_sems, recv_sems, o, core, x, y, chips, to):
    views = [(3 * BIG.index(n) + j, _shard_view(o[n], 0, n, 2 * chip[0] + chip[1]))
             for n in PULL_SET[core] for j, chip in enumerate(chips)]
    return [_remote(send_sems, recv_sems, k, v, v, to) for k, v in views]


def gather_start(shards, layer, *, name):
    nt = len(BIG)

    def body(*refs):
        p, o = dict(zip(BIG, refs[:nt])), dict(zip(BIG, refs[nt:2 * nt]))
        x, y, c, chips = _place()
        for core in (0, 1):
            @pl.when(c == core)
            def _():
                for cp in _pull_sends(refs[2 * nt], refs[2 * nt + 1], p, o, layer, core, x, y, chips):
                    cp.start()
        refs[-1][...] = jnp.zeros_like(refs[-1])

    lands = [_in_hbm(lax.empty((1,) + TENSORS[n][2], shards[n].dtype)) for n in BIG]
    outs = pl.pallas_call(
        body,
        out_shape=(pltpu.SemaphoreType.DMA((3 * nt,)), pltpu.SemaphoreType.DMA((3 * nt,)),
                   *[pltpu.HBM(a.shape, a.dtype) for a in lands], _sds((8, HD), F32)),
        in_specs=[HBM] * (2 * nt), out_specs=(SEM, SEM, *[HBM] * nt, VM),
        input_output_aliases={nt + t: 2 + t for t in range(nt)}, name=name, **SPLIT_CALL)(
            *[_in_hbm(shards[n]) for n in BIG], *lands)
    return outs[0], outs[1], outs[2:2 + nt], outs[-1]


def gather_wait(send_sems, recv_sems, shards, lands, after, layer, *, name):
    nt = len(BIG)

    def body(*refs):
        p, o = dict(zip(BIG, refs[:nt])), dict(zip(BIG, refs[nt:2 * nt]))
        ss, rs = refs[2 * nt], refs[2 * nt + 1]
        x, y, c, chips = _place()
        for core in (0, 1):
            @pl.when(c == core)
            def _():
                for cp in _pull_sends(ss, rs, p, o, layer, core, x, y, chips):
                    cp.wait_send()
                for cp in _pull_arrivals(ss, rs, o, core, x, y, chips, (x, y, core)):
                    cp.wait_recv()

    return pl.pallas_call(
        body, out_shape=tuple(pltpu.HBM(a.shape, a.dtype) for a in lands),
        in_specs=[HBM] * (2 * nt) + [SEM, SEM, ANY], out_specs=tuple([HBM] * nt),
        input_output_aliases={nt + t: t for t in range(nt)}, name=name, **SPLIT_CALL)(
            *[_in_hbm(shards[n]) for n in BIG], *lands, send_sems, recv_sems, after)


def gather_forward(shards, lands, layer, *, name):
    nt = len(BIG)

    def body(*refs):
        p, o = dict(zip(BIG, refs[:nt])), dict(zip(BIG, refs[2 * nt:3 * nt]))
        ss, rs = refs[3 * nt:]
        x, y, c, chips = _place()
        for core in (0, 1):
            @pl.when(c == core)
            def _():
                me, sibling = (x, y, core), (x, y, 1 - core)
                sends = _pull_arrivals(ss, rs, o, core, x, y, chips, sibling)
                sends += [_remote(ss, rs, 3 * nt + t, p[n].at[layer], _shard_view(o[n], 0, n, 2 * x + y), sibling)
                          for t, n in enumerate(BIG)]
                for cp in sends:
                    cp.start()
                for cp in sends:
                    cp.wait_send()
                for cp in _pull_arrivals(ss, rs, o, 1 - core, x, y, chips, me):
                    cp.wait_recv()
                for t, n in enumerate(BIG):
                    own = _shard_view(o[n], 0, n, 2 * x + y)
                    _remote(ss, rs, 3 * nt + t, own, own, me).wait_recv()

    outs = pl.pallas_call(
        body, out_shape=[_sds(a.shape, a.dtype) for a in lands], in_specs=[ANY] * (2 * nt), out_specs=[ANY] * nt,
        input_output_aliases={nt + t: t for t in range(nt)}, scratch_shapes=_dma_sems(4 * nt), name=name)(
            *[shards[n] for n in BIG], *lands)
    return dict(zip(BIG, outs))


def pair_exchange(g, *, name):
    hh = g[BIG[0]].shape[0] // 2
    nt = len(BIG)

    def body(*refs):
        send_sems, recv_sems = refs[2 * nt:]
        x, y, c, _ = _place()
        copies = [_remote(send_sems, recv_sems, t, refs[t].at[pl.ds(hh * (1 - c), hh)], refs[nt + t], (x, y, 1 - c))
                  for t in range(nt)]
        for cp in copies:
            cp.start()
        for cp in copies:
            cp.wait()

    outs = pl.pallas_call(
        body, out_shape=[_sds((hh,) + g[n].shape[1:], g[n].dtype) for n in BIG], in_specs=[ANY] * nt, out_specs=[ANY] * nt,
        scratch_shapes=_dma_sems(nt), name=name)(*[g[n] for n in BIG])
    return dict(zip(BIG, outs))


def _chip_copies(send_sems, recv_sems, s_refs, land_refs, c, chips):
    hl = s_refs[0].shape[0]
    return [_remote(send_sems, recv_sems, 3 * t + j, _shard_view(s_refs[t], pl.ds(0, hl), n, 2 * chip[0] + chip[1]),
                    land_refs[t].at[j], (*chip, c))
            for t, n in enumerate(BIG) for j, chip in enumerate(chips)]


def _landing_shapes(s):
    hl = s[BIG[0]].shape[0]
    return [_sds((3, hl) + _shard_shape(n), s[n].dtype) for n in BIG]


def chip_exchange(s, *, name):
    nt = len(BIG)

    def body(*refs):
        send_sems, recv_sems = refs[2 * nt:]
        x, y, c, chips = _place()
        copies = _chip_copies(send_sems, recv_sems, refs[:nt], refs[nt:2 * nt], c, chips)
        for cp in copies:
            cp.start()
        for cp in copies:
            cp.wait()

    outs = pl.pallas_call(
        body, out_shape=_landing_shapes(s), in_specs=[ANY] * nt, out_specs=[ANY] * nt,
        scratch_shapes=_dma_sems(3 * nt), name=name)(*[s[n] for n in BIG])
    return dict(zip(BIG, outs))


def chip_exchange_start(s, *, name, after=None):
    nt = len(BIG)
    first = [] if after is None else [after]

    def body(*refs):
        o = refs[2 * nt + len(first):]
        x, y, c, chips = _place()
        for cp in _chip_copies(o[0], o[1], refs[:nt], refs[nt:2 * nt], c, chips):
            cp.start()
        refs[-1][...] = jnp.zeros_like(refs[-1])

    lands = [_in_hbm(lax.empty(d.shape, d.dtype)) for d in _landing_shapes(s)]
    srcs = [_in_hbm(s[n]) for n in BIG]
    outs = pl.pallas_call(
        body,
        out_shape=(pltpu.SemaphoreType.DMA((3 * nt,)), pltpu.SemaphoreType.DMA((3 * nt,)),
                   *[pltpu.HBM(a.shape, a.dtype) for a in srcs + lands], _sds((8, HD), F32)),
        in_specs=[HBM] * (2 * nt) + [ANY] * len(first), out_specs=(SEM, SEM, *[HBM] * (2 * nt), VM),
        input_output_aliases={t: 2 + t for t in range(2 * nt)}, name=name, **SPLIT_CALL)(*srcs, *lands, *first)
    return outs[0], outs[1], outs[2:2 + nt], outs[2 + nt:2 + 2 * nt], outs[-1]


def chip_exchange_wait(send_sems, recv_sems, srcs, lands, after, *, name):
    nt = len(BIG)

    def body(*refs):
        x, y, c, chips = _place()
        for cp in _chip_copies(refs[2 * nt], refs[2 * nt + 1], refs[:nt], refs[nt:2 * nt], c, chips):
            cp.wait_send()
            cp.wait_recv()

    outs = pl.pallas_call(
        body, out_shape=tuple(pltpu.HBM(a.shape, a.dtype) for a in list(srcs) + list(lands)),
        in_specs=[HBM] * (2 * nt) + [SEM, SEM] + [ANY] * len(after), out_specs=tuple([HBM] * (2 * nt)),
        input_output_aliases={t: t for t in range(2 * nt)}, name=name, **SPLIT_CALL)(
            *srcs, *lands, send_sems, recv_sems, *after)
    return dict(zip(BIG, outs[:nt])), dict(zip(BIG, outs[nt:]))


def pair_share(f, l0, hh, *, name):
    nt = len(BIG)

    def body(*refs):
        o = refs[nt:2 * nt]
        send_sems, recv_sems = refs[2 * nt:]
        x, y, c, _ = _place()
        mine, theirs = pl.ds(l0 + hh * c, hh), pl.ds(l0 + hh * (1 - c), hh)
        copies = [_remote(send_sems, recv_sems, t, o[t].at[mine], o[t].at[mine], (x, y, 1 - c)) for t in range(nt)]
        for cp in copies:
            cp.start()
        for t, cp in enumerate(copies):
            cp.wait_send()
            _remote(send_sems, recv_sems, t, o[t].at[theirs], o[t].at[theirs], (x, y, c)).wait_recv()

    outs = pl.pallas_call(
        body, out_shape=[_sds(f[n].shape, f[n].dtype) for n in BIG], in_specs=[ANY] * nt, out_specs=[ANY] * nt,
        input_output_aliases={t: t for t in range(nt)}, scratch_shapes=_dma_sems(nt), name=name)(*[f[n] for n in BIG])
    return dict(zip(BIG, outs))


def pair_add(g, r, idx, *, tensor, name):
    _, _, full, blk = TENSORS[tensor]
    hh = r.shape[0]

    def body(idx_ref, g_ref, r_ref, o_ref):
        o_ref[...] = (g_ref[...].astype(F32) + r_ref[...].astype(F32)).astype(o_ref.dtype)

    own = pl.BlockSpec((None,) + blk, lambda *a: (a[0],) + a[1:-1])
    return pl.pallas_call(
        body, out_shape=_sds(r.shape, r.dtype),
        grid_spec=pltpu.PrefetchScalarGridSpec(
            num_scalar_prefetch=1, grid=(hh,) + tuple(f // b for f, b in zip(full, blk)),
            in_specs=[pl.BlockSpec((None,) + blk, lambda *a: (hh * a[-1][0] + a[0],) + a[1:-1]), own], out_specs=own),
        compiler_params=_cparams(("parallel",) * (1 + len(full))), name=name)(idx, g, r)


def chip_add(s, r, idx, totals, l0, *, tensor, name):
    kind, width, full, _ = TENSORS[tensor]
    shard = _shard_shape(tensor)
    hh = s.shape[0]
    zeros = (0,) * len(shard)

    def body(idx_ref, s_ref, r0_ref, r1_ref, r2_ref, t_ref, o_ref):
        o_ref[...] = ((s_ref[...].astype(F32) + r0_ref[...].astype(F32)) + r1_ref[...].astype(F32)) + r2_ref[...].astype(F32)

    if kind == "lead":
        mine = pl.BlockSpec((None, None) + shard, lambda i, ix: (i, ix[1]) + zeros)
    elif kind == "row":
        mine = pl.BlockSpec((None,) + shard, lambda i, ix: (i, ix[1]) + zeros[1:])
    else:
        mine = pl.BlockSpec((None,) + shard, lambda i, ix: (i,) + zeros[1:] + (ix[1],))
    peer = lambda j: pl.BlockSpec((None, None) + shard, lambda i, ix: (j, i) + zeros)
    return pl.pallas_call(
        body, out_shape=_sds(totals.shape, F32),
        grid_spec=pltpu.PrefetchScalarGridSpec(
            num_scalar_prefetch=1, grid=(hh,), in_specs=[mine, peer(0), peer(1), peer(2), pl.BlockSpec(memory_space=pl.ANY)],
            out_specs=pl.BlockSpec((None,) + shard, lambda i, ix: (l0 + hh * ix[0] + i,) + zeros)),
        input_output_aliases={5: 0}, compiler_params=_cparams(("parallel",)), name=name)(idx, s, r, r, r, totals)


def _flat_rows(arrs):
    v = jnp.concatenate([a.reshape(-1) for a in arrs])
    n = -(-v.shape[0] // 1024) * 1024
    return jnp.pad(v, (0, n - v.shape[0])).reshape(n // HD, HD)


def _unflat(buf, shapes):
    v, out, o = buf.reshape(-1), [], 0
    for s in shapes:
        n = int(np.prod(s))
        out.append(v[o:o + n].reshape(s))
        o += n
    return out


WEIGHTS = ("norm1_g", "norm2_g", "w_ada", "b_ada", "w_in", "w_gla_a2", "b_gla_a", "b_fox_f", "ret_norm_g", "gla_norm_g",
           "q_norm_g", "k_norm_g", "w_br", "w_mg", "b_mg", "w_o", "w_up", "w_conv", "b_conv", "w_down")
REPLICATED = ("norm1_g", "norm2_g", "b_gla_a", "b_fox_f", "ret_norm_g", "gla_norm_g", "q_norm_g", "k_norm_g", "b_mg", "b_conv")
ADAM_BLOCKS = dict(w_ada=(1, 256, 1536), w_in=(1, 256, 1285), w_br=(1, 3, BW, 256), w_mg=(1, 512, 768), w_o=(2, 256, D),
                   w_up=(1, 256, 1408), w_down=(1, 352, D))
ALL_AXES = ("x", "y", "c")


def kernel(x, c, norm1_g, norm2_g, w_ada, b_ada, w_in, w_gla_a2, b_gla_a, b_fox_f, ret_norm_g, gla_norm_g, q_norm_g, k_norm_g, w_br, w_mg, b_mg, w_o, w_up, w_conv, b_conv, w_down, loss_target, m_norm1_g, m_norm2_g, m_w_ada, m_b_ada, m_w_in, m_w_gla_a2, m_b_gla_a, m_b_fox_f, m_ret_norm_g, m_gla_norm_g, m_q_norm_g, m_k_norm_g, m_w_br, m_w_mg, m_b_mg, m_w_o, m_w_up, m_w_conv, m_b_conv, m_w_down, v_norm1_g, v_norm2_g, v_w_ada, v_b_ada, v_w_in, v_w_gla_a2, v_b_gla_a, v_b_fox_f, v_ret_norm_g, v_gla_norm_g, v_q_norm_g, v_k_norm_g, v_w_br, v_w_mg, v_b_mg, v_w_o, v_w_up, v_w_conv, v_b_conv, v_w_down):
    w = dict(zip(WEIGHTS, (norm1_g, norm2_g, w_ada, b_ada, w_in, w_gla_a2, b_gla_a, b_fox_f, ret_norm_g, gla_norm_g,
                           q_norm_g, k_norm_g, w_br, w_mg, b_mg, w_o, w_up, w_conv, b_conv, w_down)))
    m = dict(zip(WEIGHTS, (m_norm1_g, m_norm2_g, m_w_ada, m_b_ada, m_w_in, m_w_gla_a2, m_b_gla_a, m_b_fox_f, m_ret_norm_g,
                           m_gla_norm_g, m_q_norm_g, m_k_norm_g, m_w_br, m_w_mg, m_b_mg, m_w_o, m_w_up, m_w_conv, m_b_conv,
                           m_w_down)))
    v = dict(zip(WEIGHTS, (v_norm1_g, v_norm2_g, v_w_ada, v_b_ada, v_w_in, v_w_gla_a2, v_b_gla_a, v_b_fox_f, v_ret_norm_g,
                           v_gla_norm_g, v_q_norm_g, v_k_norm_g, v_w_br, v_w_mg, v_b_mg, v_w_o, v_w_up, v_w_conv, v_b_conv,
                           v_w_down)))
    nl = norm1_g.shape[0]
    seq = x.shape[1]
    xi, yi, ci = lax.axis_index("x"), lax.axis_index("y"), lax.axis_index("c")
    k_me = 2 * xi + yi
    b_me = 4 * xi + 2 * yi + ci
    ada_n = w_ada.shape[2]
    a2_n, conv_n = w_gla_a2.shape[2], w_conv.shape[2]

    shards = {n: w[n].astype(MMT) for n in BIG}
    started = gather_start(shards, 0, name="gather0_start")

    def gather_finish(l, started, after):
        send_sems, recv_sems, lands, _ = started
        lands = gather_wait(send_sems, recv_sems, shards, lands, after, l, name=f"gather{l}_wait")
        big = gather_forward(shards, lands, l, name=f"gather{l}_forward")
        big["w1"] = build_w1(big["w_in"], big["w_mg"])
        return big

    blk = _flat_rows([c, w_gla_a2, w_conv]) + started[-1][0, 0]
    g1 = small_allgather(blk, name="gather_small").reshape(8, blk.shape[0], HD)
    c_all = g1[:, :D // HD].reshape(8, D)
    by_chip = g1[0::2].reshape(4, -1)[:, D:]
    a2_sh, conv_sh = by_chip[:, :nl * GLR * a2_n], by_chip[:, nl * GLR * a2_n:nl * (GLR * a2_n + 3 * conv_n)]
    full_small = dict(
        w_gla_a2=a2_sh.reshape(4, nl, GLR, a2_n).transpose(1, 2, 0, 3).reshape(nl, GLR, 4 * a2_n),
        w_conv=conv_sh.reshape(4, nl, 3, conv_n).transpose(1, 2, 0, 3).reshape(nl, 3, 4 * conv_n))

    b_ada_sh = lax.dynamic_slice_in_dim(b_ada, k_me * ada_n, ada_n, axis=1)[:, None, :]
    mod_sh = ada_mod(c_all, w_ada, b_ada_sh)
    g2 = small_allgather(mod_sh.reshape(nl * 8, ada_n), name="gather_mod").reshape(4, 2, nl, 8, ada_n)[:, 0]
    mod_me = lax.dynamic_index_in_dim(g2, b_me, axis=2, keepdims=False).transpose(1, 0, 2).reshape(nl, 4 * ada_n)

    wsmall = {n: w[n] for n in REPLICATED}
    wsmall.update(full_small)
    mods = [[mod_me[l:l + 1, i * D:(i + 1) * D] for i in range(6)] for l in range(nl)]
    big = gather_finish(0, started, mod_me)

    cosf, sinf = _rope_tables(seq)
    xs, saved, params = x[0], [], []
    for l in range(nl):
        if l + 1 < nl:
            started = gather_start(shards, l + 1, name=f"gather{l + 1}_start")
            mods[l][1] = mods[l][1] + started[-1][0, 0]
        params.append(layer_params(wsmall, big, l))
        xs, sv = layer_fwd(xs, mods[l], params[l], cosf, sinf)
        saved.append(sv)
        if l + 1 < nl:
            big = gather_finish(l + 1, started, xs)
    loss_part, dx = loss_and_grad(xs, loss_target[0], name="loss")
    loss = lax.psum(loss_part[0, 0], ALL_AXES)
    grads, dmods = [None] * nl, [None] * nl
    idx = jnp.stack([ci, k_me]).astype(jnp.int32)
    totals = {n: lax.empty((nl,) + _shard_shape(n), F32) for n in BIG}

    def finish_group(group, chip_sum, from_chips, totals):
        totals = {n: chip_add(chip_sum[n], from_chips[n], idx, totals[n], RS_GROUP * group, tensor=n,
                              name=f"rs{group}_chip_add_{n}") for n in BIG}
        return pair_share(totals, RS_GROUP * group, RS_GROUP // 2, name=f"rs{group}_pair_share")

    pending = None
    for group in reversed(range(nl // RS_GROUP)):
        layers = range(RS_GROUP * group, RS_GROUP * (group + 1))
        stacks = {n: lax.empty((RS_GROUP,) + TENSORS[n][2], MMT) for n in BIG if n != "w_br"}
        if pending is not None:
            mods[layers[-1]][5] = mods[layers[-1]][5] + pending[-1][0, 0]
        for l in reversed(layers):
            dx, g, dmods[l], stacks = layer_bwd(dx, mods[l], params[l], saved[l], cosf, sinf, stacks, l - layers[0])
            grads[l] = layer_grads(g)
        stacks["w_br"] = jnp.stack([grads[l]["w_br"].astype(MMT) for l in layers])
        from_sibling = pair_exchange(stacks, name=f"rs{group}_pair_exchange")
        chip_sum = {n: pair_add(stacks[n], from_sibling[n], idx, tensor=n, name=f"rs{group}_pair_add_{n}") for n in BIG}
        if pending is not None:
            earlier, send_sems, recv_sems, srcs, lands, _ = pending
            sums, from_chips = chip_exchange_wait(send_sems, recv_sems, srcs, lands, [dx, *chip_sum.values()],
                                                  name=f"rs{earlier}_chip_exchange_wait")
            totals = finish_group(earlier, sums, from_chips, totals)
        if group > 0:
            pending = (group, *chip_exchange_start(chip_sum, name=f"rs{group}_chip_exchange_start"))

    small_names = REPLICATED + ("w_gla_a2", "w_conv")
    small_shapes = [(nl, 6 * D)] + [(nl,) + grads[0][n].shape for n in small_names]
    vec = _flat_rows([jnp.concatenate(dmods, axis=0)] + [jnp.stack([grads[l][n] for l in range(nl)]) for n in small_names])
    gs = small_allgather(vec, name="gather_small_grads")
    pending = (0, *chip_exchange_start(chip_sum, name="rs0_chip_exchange_start", after=gs))
    gs = (gs + pending[-1][0, 0]).reshape(8, vec.shape[0], HD)
    summed = _unflat(sum_devices(gs), small_shapes)
    grad = dict(zip(small_names, summed[1:]))
    grad["b_ada"] = summed[0]
    grad["w_gla_a2"] = lax.dynamic_slice_in_dim(grad["w_gla_a2"], k_me * a2_n, a2_n, axis=2)
    grad["w_conv"] = lax.dynamic_slice_in_dim(grad["w_conv"], k_me * conv_n, conv_n, axis=2)
    dmod_all = gs[:, :nl * 6 * D // HD].reshape(8, nl, 6 * D)
    dmod_sh = lax.dynamic_slice_in_dim(dmod_all, k_me * ada_n, ada_n, axis=2).transpose(1, 0, 2)
    grad["w_ada"] = ada_dw(c_all, dmod_sh)

    delta, new_m, new_v = {}, {}, {}
    delta["w_ada"], new_m["w_ada"], new_v["w_ada"] = adamw(w["w_ada"], grad["w_ada"], m["w_ada"], v["w_ada"],
                                                          block=ADAM_BLOCKS["w_ada"], name="adamw_w_ada")
    rest = [n for n in WEIGHTS if n not in ADAM_BLOCKS]
    shapes = [w[n].shape for n in rest]
    flat = [_flat_rows([t[n] for n in rest]) for t in (w, grad, m, v)]
    outs = adamw(*flat, block=flat[0].shape, name="adamw_small")
    for t, o in zip((delta, new_m, new_v), outs):
        t.update(zip(rest, _unflat(o, shapes)))

    later = {n: adamw(w[n], totals[n], m[n], v[n], block=ADAM_BLOCKS[n], name="adamw_later_" + n, rows=(RS_GROUP, nl))
             for n in BIG} if nl > RS_GROUP else {}
    earlier, send_sems, recv_sems, srcs, lands, _ = pending
    done_first = [outs[0], delta["w_ada"]] + [later[n][0] for n in later]
    sums, from_chips = chip_exchange_wait(send_sems, recv_sems, srcs, lands, done_first,
                                          name=f"rs{earlier}_chip_exchange_wait")
    grad.update(finish_group(earlier, sums, from_chips, totals))
    for n in BIG:
        delta[n], new_m[n], new_v[n] = adamw(w[n], grad[n], m[n], v[n], block=ADAM_BLOCKS[n], name="adamw_" + n,
                                             rows=(0, min(RS_GROUP, nl)), into=later.get(n))

    return (loss, dx[None], *[grad[n] for n in WEIGHTS], *[delta[n] for n in WEIGHTS], *[new_m[n] for n in WEIGHTS],
            *[new_v[n] for n in WEIGHTS])
```

```python
import functools

import numpy as np
import jax
import jax.numpy as jnp
from jax import lax
from jax.experimental import pallas as pl
from jax.experimental.pallas import tpu as pltpu

F32 = jnp.float32
MMT = jnp.bfloat16
HI = lax.Precision.HIGHEST

D = 1024
DEPTH = 4
NH = 4
HD = 128
BW = NH * HD
CH = 64
GDK = 64
GLR = 16
DFF = 2816
EPS = 1e-6
ROPE_BASE = 10000.0

GP, RG, GG, FV, RQ, RK, RV, GQ, GK, GV, FQ, FK, LR, FF = (
    0, 3072, 3584, 4096, 4608, 5120, 5632, 6144, 6400, 6656, 7168, 7680, 8192, 8320)
NZZ = 8448
WZ0 = 3072
IN_W = 5140
W_IN_COLS = dict(rqkv=(0, 1536), rg=(1536, 2048), gqkv=(2048, 3072), lr=(3072, 3088), gg=(3088, 3600), fqk=(3600, 4624),
                 fv=(4624, 5136), ff=(5136, 5140))

VMEM_LIMIT = 56 * 1024 * 1024

ADAM_LR, ADAM_B1, ADAM_B2, ADAM_EPS, ADAM_WD, ADAM_STEP = 0.001, 0.9, 0.999, 1e-08, 0.01, 10


def _cparams(sem=None):
    return pltpu.CompilerParams(dimension_semantics=sem, vmem_limit_bytes=VMEM_LIMIT)


def _sds(shape, dtype):
    return jax.ShapeDtypeStruct(tuple(shape), dtype)


def _dot(a, b, precision=None):
    return lax.dot_general(a, b, (((1,), (0,)), ((), ())), precision=precision, preferred_element_type=F32)


def _dot_nt(a, b, precision=None):
    return lax.dot_general(a, b, (((1,), (1,)), ((), ())), precision=precision, preferred_element_type=F32)


def _dot_tn(a, b, precision=None):
    return lax.dot_general(a, b, (((0,), (0,)), ((), ())), precision=precision, preferred_element_type=F32)


def _silu(x):
    return x * jax.nn.sigmoid(x)


def _log_sigmoid(x):
    return jnp.minimum(x, 0.0) - jnp.log(1.0 + jnp.exp(jnp.minimum(x, -x)))


@jax.custom_vjp
def _swap_halves(x):
    return pltpu.roll(x, HD // 2, 1)


_swap_halves.defvjp(lambda x: (_swap_halves(x), None), lambda _, g: (_swap_halves(g),))


@jax.custom_vjp
def _bdot(a, b):
    return _dot(a.astype(MMT), b.astype(MMT))


@jax.custom_vjp
def _bdot_nt(a, b):
    return _dot_nt(a.astype(MMT), b.astype(MMT))


@jax.custom_vjp
def _bdot_tn(a, b):
    return _dot_tn(a.astype(MMT), b.astype(MMT))


_bdot.defvjp(lambda a, b: (_bdot(a, b), (a, b)), lambda r, g: (_bdot_nt(g, r[1]), _bdot_tn(r[0], g)))
_bdot_nt.defvjp(lambda a, b: (_bdot_nt(a, b), (a, b)), lambda r, g: (_bdot(g, r[1]), _bdot_tn(g, r[0])))
_bdot_tn.defvjp(lambda a, b: (_bdot_tn(a, b), (a, b)), lambda r, g: (_bdot_nt(r[1], g), _bdot(r[0], g)))


def _stacked(blk, idx, layer):
    if layer is None:
        return pl.BlockSpec(blk, idx)
    return pl.BlockSpec((None,) + blk, lambda i, j: (layer,) + idx(i, j))


def mm_nn(a, b, *, tm, tn, out_dtype, name, layer=None):
    m, k = a.shape
    n = b.shape[-1]

    def body(a_ref, b_ref, o_ref):
        o_ref[...] = _dot(a_ref[...], b_ref[...]).astype(o_ref.dtype)

    return pl.pallas_call(
        body, grid=(m // tm, n // tn),
        in_specs=[pl.BlockSpec((tm, k), lambda i, j: (i, 0)), _stacked((k, tn), lambda i, j: (0, j), layer)],
        out_specs=pl.BlockSpec((tm, tn), lambda i, j: (i, j)),
        out_shape=_sds((m, n), out_dtype), compiler_params=_cparams(("parallel", "parallel")), name=name)(a, b)


def mm_nn_residual(a, b, res, gate, *, tm, tn, name, layer=None):
    m, k = a.shape
    n = b.shape[-1]

    def body(a_ref, b_ref, r_ref, g_ref, x_ref, y_ref):
        acc = _dot(a_ref[...], b_ref[...])
        y_ref[...] = acc
        x_ref[...] = r_ref[...] + g_ref[...] * acc

    return pl.pallas_call(
        body, grid=(m // tm, n // tn),
        in_specs=[pl.BlockSpec((tm, k), lambda i, j: (i, 0)), _stacked((k, tn), lambda i, j: (0, j), layer),
                  pl.BlockSpec((tm, tn), lambda i, j: (i, j)), pl.BlockSpec((1, tn), lambda i, j: (0, j))],
        out_specs=[pl.BlockSpec((tm, tn), lambda i, j: (i, j)), pl.BlockSpec((tm, tn), lambda i, j: (i, j))],
        out_shape=[_sds((m, n), F32), _sds((m, n), F32)],
        compiler_params=_cparams(("parallel", "parallel")), name=name)(a, b, res, gate)


def mm_nt(a, b, *, tm, tn, out_dtype, name, layer=None):
    m, k = a.shape
    n = b.shape[-2]

    def body(a_ref, b_ref, o_ref):
        o_ref[...] = _dot_nt(a_ref[...], b_ref[...]).astype(o_ref.dtype)

    return pl.pallas_call(
        body, grid=(m // tm, n // tn),
        in_specs=[pl.BlockSpec((tm, k), lambda i, j: (i, 0)), _stacked((tn, k), lambda i, j: (j, 0), layer)],
        out_specs=pl.BlockSpec((tm, tn), lambda i, j: (i, j)),
        out_shape=_sds((m, n), out_dtype), compiler_params=_cparams(("parallel", "parallel")), name=name)(a, b)


def mm_nt2(a1, a2, b, *, tm, tn, name, layer):
    m, k1 = a1.shape
    k2 = a2.shape[1]
    n = b.shape[-2]

    def body(a1_ref, a2_ref, b_ref, o_ref):
        o_ref[...] = _dot_nt(a1_ref[...], b_ref[:, :k1]) + _dot_nt(a2_ref[...], b_ref[:, k1:])

    return pl.pallas_call(
        body, grid=(m // tm, n // tn),
        in_specs=[pl.BlockSpec((tm, k1), lambda i, j: (i, 0)), pl.BlockSpec((tm, k2), lambda i, j: (i, 0)),
                  _stacked((tn, k1 + k2), lambda i, j: (j, 0), layer)],
        out_specs=pl.BlockSpec((tm, tn), lambda i, j: (i, j)),
        out_shape=_sds((m, n), F32), compiler_params=_cparams(("parallel", "parallel")), name=name)(a1, a2, b)


def mm_tn(a, b, *, tm, tn, out_dtype, name, col0=0, ncols=None, stack=None, layer=None, out_col0=0):
    s, m = a.shape
    n = b.shape[1] - col0 if ncols is None else ncols
    c0, oc0 = col0 // tn, out_col0 // tn

    def body(a_ref, b_ref, *rest):
        o_ref = rest[-1]
        o_ref[...] = _dot_tn(a_ref[...], b_ref[...]).astype(o_ref.dtype)

    in_specs = [pl.BlockSpec((s, tm), lambda i, j: (0, i)), pl.BlockSpec((s, tn), lambda i, j: (0, c0 + j))]
    if stack is None:
        return pl.pallas_call(
            body, grid=(m // tm, n // tn), in_specs=in_specs, out_specs=pl.BlockSpec((tm, tn), lambda i, j: (i, j)),
            out_shape=_sds((m, n), out_dtype), compiler_params=_cparams(("parallel", "parallel")), name=name)(a, b)
    return pl.pallas_call(
        body, grid=(m // tm, n // tn), in_specs=in_specs + [pl.BlockSpec(memory_space=pl.ANY)],
        out_specs=pl.BlockSpec((None, tm, tn), lambda i, j: (layer, i, oc0 + j)),
        out_shape=_sds(stack.shape, stack.dtype), input_output_aliases={2: 0},
        compiler_params=_cparams(("parallel", "parallel")), name=name)(a, b, stack)


def _row_tile(s):
    return min(256, s)


def _norm_mod_f(x, g, scale, shift):
    r = lax.rsqrt(jnp.mean(x * x, axis=-1, keepdims=True) + EPS)
    return (x * r * g) * (1.0 + scale) + shift


def norm_mod(x, g, scale, shift, *, name):
    s = x.shape[0]
    t = _row_tile(s)

    def body(x_ref, g_ref, sc_ref, sh_ref, o_ref):
        o_ref[...] = _norm_mod_f(x_ref[...], g_ref[...], sc_ref[...], sh_ref[...]).astype(o_ref.dtype)

    vec = pl.BlockSpec((1, D), lambda i: (0, 0))
    return pl.pallas_call(
        body, grid=(s // t,), in_specs=[pl.BlockSpec((t, D), lambda i: (i, 0)), vec, vec, vec],
        out_specs=pl.BlockSpec((t, D), lambda i: (i, 0)), out_shape=_sds((s, D), MMT),
        compiler_params=_cparams(("parallel",)), name=name)(x, g, scale, shift)


def norm_mod_bwd(x, dh, dres, g, scale, shift, *, name):
    s = x.shape[0]
    t = _row_tile(s)

    def body(x_ref, dh_ref, dr_ref, g_ref, sc_ref, sh_ref, dx_ref, dg_ref, dsc_ref, dsh_ref):
        @pl.when(pl.program_id(0) == 0)
        def _():
            dg_ref[...] = jnp.zeros_like(dg_ref)
            dsc_ref[...] = jnp.zeros_like(dsc_ref)
            dsh_ref[...] = jnp.zeros_like(dsh_ref)

        _, vjp = jax.vjp(_norm_mod_f, x_ref[...], g_ref[...], sc_ref[...], sh_ref[...])
        dx, dg, dsc, dsh = vjp(dh_ref[...])
        dx_ref[...] = dr_ref[...] + dx
        dg_ref[...] += dg
        dsc_ref[...] += dsc
        dsh_ref[...] += dsh

    row = pl.BlockSpec((t, D), lambda i: (i, 0))
    vec = pl.BlockSpec((1, D), lambda i: (0, 0))
    return pl.pallas_call(
        body, grid=(s // t,), in_specs=[row, row, row, vec, vec, vec], out_specs=[row, vec, vec, vec],
        out_shape=[_sds((s, D), F32)] + [_sds((1, D), F32)] * 3,
        compiler_params=_cparams(("arbitrary",)), name=name)(x, dh, dres, g, scale, shift)


def gate_bwd(dx, y, gate, *, name):
    s = dx.shape[0]
    t = _row_tile(s)

    def body(dx_ref, y_ref, g_ref, dy_ref, dg_ref):
        @pl.when(pl.program_id(0) == 0)
        def _():
            dg_ref[...] = jnp.zeros_like(dg_ref)

        dxv = dx_ref[...]
        dy_ref[...] = (g_ref[...] * dxv).astype(dy_ref.dtype)
        dg_ref[...] += jnp.sum(dxv * y_ref[...], axis=0, keepdims=True)

    row = pl.BlockSpec((t, D), lambda i: (i, 0))
    vec = pl.BlockSpec((1, D), lambda i: (0, 0))
    return pl.pallas_call(
        body, grid=(s // t,), in_specs=[row, row, vec], out_specs=[row, vec],
        out_shape=[_sds((s, D), MMT), _sds((1, D), F32)],
        compiler_params=_cparams(("arbitrary",)), name=name)(dx, y, gate)


def loss_and_grad(xf, target, *, name):
    s = xf.shape[0]
    t = _row_tile(s)

    def body(x_ref, t_ref, l_ref, dx_ref):
        @pl.when(pl.program_id(0) == 0)
        def _():
            l_ref[...] = jnp.zeros_like(l_ref)

        e = x_ref[...] - t_ref[...]
        dx_ref[...] = e * (1.0 / D)
        l_ref[...] += 0.5 * jnp.sum(jnp.sum(e * e, axis=1, keepdims=True), axis=0, keepdims=True) * (1.0 / D)

    row = pl.BlockSpec((t, D), lambda i: (i, 0))
    return pl.pallas_call(
        body, grid=(s // t,), in_specs=[row, row], out_specs=[pl.BlockSpec((1, 1), lambda i: (0, 0)), row],
        out_shape=[_sds((1, 1), F32), _sds((s, D), F32)],
        compiler_params=_cparams(("arbitrary",)), name=name)(xf, target)


def _ret_consts():
    log_g = np.log1p(-np.exp2(-5.0 - np.arange(NH, dtype=np.float32))).astype(np.float32)
    idx = np.arange(CH, dtype=np.float32)
    d_intra = np.exp(np.abs(idx[:, None] - idx[None, :])[None] * log_g[:, None, None]).astype(np.float32)
    k_w = np.exp((CH - 1.0 - idx)[None, :] * log_g[:, None]).astype(np.float32)
    q_w = np.exp((idx + 1.0)[None, :] * log_g[:, None]).astype(np.float32)
    g_chunk = [float(v) for v in np.exp(np.float32(CH) * log_g).astype(np.float32)]
    bc = lambda a: np.ascontiguousarray(np.broadcast_to(a[:, :, None], (NH, CH, HD)))
    return jnp.asarray(d_intra), jnp.asarray(bc(k_w)), jnp.asarray(bc(q_w)), g_chunk


def _rope_tables(s):
    half = HD // 2
    inv_freq = (ROPE_BASE ** (-np.arange(half, dtype=np.float64) / half)).astype(np.float32)
    ang = (np.arange(s, dtype=np.float32)[:, None] * inv_freq[None, :]).astype(np.float64)
    cos, sin = np.cos(ang).astype(np.float32), np.sin(ang).astype(np.float32)
    return jnp.asarray(np.concatenate([cos, cos], axis=1)), jnp.asarray(np.concatenate([-sin, sin], axis=1))


def _ret_chunk(qs, ks, vs, rs, cos, sin, dintra, kw, qw, g_chunk):
    outs, rn = [], []
    for h in range(NH):
        q = qs[h] * cos + _swap_halves(qs[h]) * sin
        k = (ks[h] * cos + _swap_halves(ks[h]) * sin) * (HD ** -0.5)
        sc = _bdot_nt(q, k) * dintra[h]
        outs.append(_bdot(sc, vs[h]) + _bdot(q * qw[h], rs[h]))
        rn.append(g_chunk[h] * rs[h] + _bdot_tn(k * kw[h], vs[h]))
    return outs, rn


def _heads(x):
    return [x[:, h * HD:(h + 1) * HD] for h in range(NH)]


def retention_fwd(zz, cosf, sinf, *, name):
    s = zz.shape[0]
    n = s // CH
    dintra, kw, qw, g_chunk = _ret_consts()

    def body(q_ref, k_ref, v_ref, c_ref, s_ref, di_ref, kw_ref, qw_ref, o_ref, rp_ref, r_scr):
        @pl.when(pl.program_id(0) == 0)
        def _():
            r_scr[...] = jnp.zeros_like(r_scr)

        rprev = r_scr[...]
        rp_ref[0] = rprev
        outs, rn = _ret_chunk(_heads(q_ref[...]), _heads(k_ref[...]), _heads(v_ref[...]),
                              [rprev[h * HD:(h + 1) * HD] for h in range(NH)], c_ref[...], s_ref[...],
                              [di_ref[h] for h in range(NH)], [kw_ref[h] for h in range(NH)],
                              [qw_ref[h] for h in range(NH)], g_chunk)
        o_ref[...] = jnp.concatenate(outs, axis=1)
        r_scr[...] = jnp.concatenate(rn, axis=0)

    col = lambda c: pl.BlockSpec((CH, BW), lambda i: (i, c // BW))
    tab = pl.BlockSpec((CH, HD), lambda i: (i, 0))
    cst = lambda shp: pl.BlockSpec(shp, lambda i: (0,) * len(shp))
    return pl.pallas_call(
        body, grid=(n,),
        in_specs=[col(RQ), col(RK), col(RV), tab, tab, cst((NH, CH, CH)), cst((NH, CH, HD)), cst((NH, CH, HD))],
        out_specs=[pl.BlockSpec((CH, BW), lambda i: (i, 0)), pl.BlockSpec((1, BW, HD), lambda i: (i, 0, 0))],
        out_shape=[_sds((s, BW), F32), _sds((n, BW, HD), F32)],
        scratch_shapes=[pltpu.VMEM((BW, HD), F32)],
        compiler_params=_cparams(("arbitrary",)), name=name)(zz, zz, zz, cosf, sinf, dintra, kw, qw)


def retention_bwd(zz, cosf, sinf, rprev, do, dzz, *, name):
    s = zz.shape[0]
    n = s // CH
    dintra, kw, qw, g_chunk = _ret_consts()

    def body(q_ref, k_ref, v_ref, c_ref, s_ref, di_ref, kw_ref, qw_ref, rp_ref, do_ref, dzz_ref, dz_ref, dr_scr):
        @pl.when(pl.program_id(0) == 0)
        def _():
            dr_scr[...] = jnp.zeros_like(dr_scr)

        rprev_v = rp_ref[0]
        f = functools.partial(_ret_chunk, cos=c_ref[...], sin=s_ref[...],
                              dintra=[di_ref[h] for h in range(NH)], kw=[kw_ref[h] for h in range(NH)],
                              qw=[qw_ref[h] for h in range(NH)], g_chunk=g_chunk)
        _, vjp = jax.vjp(f, _heads(q_ref[...]), _heads(k_ref[...]), _heads(v_ref[...]),
                         [rprev_v[h * HD:(h + 1) * HD] for h in range(NH)])
        dr = dr_scr[...]
        dq, dk, dv, drp = vjp((_heads(do_ref[...]), [dr[h * HD:(h + 1) * HD] for h in range(NH)]))
        dz_ref[...] = jnp.concatenate(dq + dk + dv, axis=1).astype(dz_ref.dtype)
        dr_scr[...] = jnp.concatenate(drp, axis=0)

    col = lambda c: pl.BlockSpec((CH, BW), lambda i: (n - 1 - i, c // BW))
    tab = pl.BlockSpec((CH, HD), lambda i: (n - 1 - i, 0))
    cst = lambda shp: pl.BlockSpec(shp, lambda i: (0,) * len(shp))
    return pl.pallas_call(
        body, grid=(n,),
        in_specs=[col(RQ), col(RK), col(RV), tab, tab, cst((NH, CH, CH)), cst((NH, CH, HD)), cst((NH, CH, HD)),
                  pl.BlockSpec((1, BW, HD), lambda i: (n - 1 - i, 0, 0)), pl.BlockSpec((CH, BW), lambda i: (n - 1 - i, 0)),
                  pl.BlockSpec(memory_space=pl.ANY)],
        out_specs=pl.BlockSpec((CH, 3 * BW), lambda i: (n - 1 - i, RQ // (3 * BW))),
        out_shape=_sds(dzz.shape, dzz.dtype), input_output_aliases={10: 0},
        scratch_shapes=[pltpu.VMEM((BW, HD), F32)],
        compiler_params=_cparams(("arbitrary",)), name=name)(zz, zz, zz, cosf, sinf, dintra, kw, qw, rprev, do, dzz)


GKW = NH * GDK


def _gla_consts():
    tri = np.tril(np.ones((CH, CH), np.float32))
    mask_t = np.zeros((BW, GKW), np.float32)
    for h in range(NH):
        mask_t[h * HD:(h + 1) * HD, h * GDK:(h + 1) * GDK] = 1.0
    return jnp.asarray(tri), jnp.asarray(mask_t)


def _gla_chunk(q, k, v, glr, w_a2, b_a, st, tri, mask_t):
    la = _log_sigmoid(_bdot(glr, w_a2) + b_a) * (1.0 / 16.0)
    bc = _dot(tri, la, HI)
    be = jnp.sum(la, axis=0, keepdims=True)
    kv_t = _bdot_tn(v, k * jnp.exp(be - bc)) * mask_t
    sn = jnp.exp(be) * st + kv_t
    return _bdot_nt(q * (GDK ** -0.5), sn), sn


def gla_fwd(zz, w_a2p, b_a, *, name):
    s = zz.shape[0]
    n = s // CH
    tri, mask_t = _gla_consts()

    def body(q_ref, k_ref, v_ref, lr_ref, w_ref, b_ref, tri_ref, m_ref, o_ref, sp_ref, st_scr):
        @pl.when(pl.program_id(0) == 0)
        def _():
            st_scr[...] = jnp.zeros_like(st_scr)

        sp = st_scr[...]
        sp_ref[0] = sp
        o, sn = _gla_chunk(q_ref[...], k_ref[...], v_ref[...], lr_ref[...], w_ref[...], b_ref[...], sp,
                           tri_ref[...], m_ref[...])
        o_ref[...] = o
        st_scr[...] = sn

    cst = lambda shp: pl.BlockSpec(shp, lambda i: (0,) * len(shp))
    return pl.pallas_call(
        body, grid=(n,),
        in_specs=[pl.BlockSpec((CH, GKW), lambda i: (i, GQ // GKW)), pl.BlockSpec((CH, GKW), lambda i: (i, GK // GKW)),
                  pl.BlockSpec((CH, BW), lambda i: (i, GV // BW)), pl.BlockSpec((CH, HD), lambda i: (i, LR // HD)),
                  cst((HD, GKW)), cst((1, GKW)), cst((CH, CH)), cst((BW, GKW))],
        out_specs=[pl.BlockSpec((CH, BW), lambda i: (i, 0)), pl.BlockSpec((1, BW, GKW), lambda i: (i, 0, 0))],
        out_shape=[_sds((s, BW), F32), _sds((n, BW, GKW), F32)],
        scratch_shapes=[pltpu.VMEM((BW, GKW), F32)],
        compiler_params=_cparams(("arbitrary",)), name=name)(zz, zz, zz, zz, w_a2p, b_a, tri, mask_t)


def gla_bwd(zz, w_a2p, b_a, sprev, do, dzz, *, name):
    s = zz.shape[0]
    n = s // CH
    tri, mask_t = _gla_consts()

    def body(q_ref, k_ref, v_ref, lr_ref, w_ref, b_ref, tri_ref, m_ref, sp_ref, do_ref, dzz_ref,
             dz_ref, dlr_ref, dw_ref, db_ref, ds_scr):
        @pl.when(pl.program_id(0) == 0)
        def _():
            ds_scr[...] = jnp.zeros_like(ds_scr)
            dw_ref[...] = jnp.zeros_like(dw_ref)
            db_ref[...] = jnp.zeros_like(db_ref)

        f = functools.partial(_gla_chunk, tri=tri_ref[...], mask_t=m_ref[...])
        _, vjp = jax.vjp(f, q_ref[...], k_ref[...], v_ref[...], lr_ref[...], w_ref[...], b_ref[...], sp_ref[0])
        dq, dk, dv, dlr, dw, db, dsp = vjp((do_ref[...], ds_scr[...]))
        dz_ref[...] = jnp.concatenate([dq, dk, dv], axis=1).astype(dz_ref.dtype)
        dlr_ref[...] = dlr.astype(dlr_ref.dtype)
        dw_ref[...] += dw
        db_ref[...] += db
        ds_scr[...] = dsp

    cst = lambda shp: pl.BlockSpec(shp, lambda i: (0,) * len(shp))
    r = lambda i: n - 1 - i
    return pl.pallas_call(
        body, grid=(n,),
        in_specs=[pl.BlockSpec((CH, GKW), lambda i: (r(i), GQ // GKW)), pl.BlockSpec((CH, GKW), lambda i: (r(i), GK // GKW)),
                  pl.BlockSpec((CH, BW), lambda i: (r(i), GV // BW)), pl.BlockSpec((CH, HD), lambda i: (r(i), LR // HD)),
                  cst((HD, GKW)), cst((1, GKW)), cst((CH, CH)), cst((BW, GKW)),
                  pl.BlockSpec((1, BW, GKW), lambda i: (r(i), 0, 0)), pl.BlockSpec((CH, BW), lambda i: (r(i), 0)),
                  pl.BlockSpec(memory_space=pl.ANY)],
        out_specs=[pl.BlockSpec((CH, 2 * GKW + BW), lambda i: (r(i), GQ // (2 * GKW + BW))),
                   pl.BlockSpec((CH, HD), lambda i: (r(i), 0)), cst((HD, GKW)), cst((1, GKW))],
        out_shape=[_sds(dzz.shape, dzz.dtype), _sds((s, HD), MMT), _sds((HD, GKW), F32), _sds((1, GKW), F32)],
        input_output_aliases={10: 0}, scratch_shapes=[pltpu.VMEM((BW, GKW), F32)],
        compiler_params=_cparams(("arbitrary",)), name=name)(zz, zz, zz, zz, w_a2p, b_a, tri, mask_t, sprev, do, dzz)


def _fox_pre_f(fqs, fks, ff, gq, gk, bf):
    def rms(x, g):
        return x * lax.rsqrt(jnp.mean(x * x, axis=-1, keepdims=True) + EPS) * g

    qn = [rms(x, gq) * (HD ** -0.5) for x in fqs]
    kn = [rms(x, gk) for x in fks]
    return qn, kn, _log_sigmoid(ff + bf)


def fox_pre(zz, gq, gk, bf, *, name):
    s = zz.shape[0]
    t = _row_tile(s)
    tri = jnp.asarray(np.tril(np.ones((t, t), np.float32)))

    def body(q_ref, k_ref, f_ref, gq_ref, gk_ref, b_ref, tri_ref, qn_ref, kn_ref, cum_ref, carry):
        @pl.when(pl.program_id(0) == 0)
        def _():
            carry[...] = jnp.zeros_like(carry)

        qn, kn, lf = _fox_pre_f(_heads(q_ref[...]), _heads(k_ref[...]), f_ref[...], gq_ref[...], gk_ref[...], b_ref[...])
        qn_ref[...] = jnp.concatenate(qn, axis=1).astype(qn_ref.dtype)
        kn_ref[...] = jnp.concatenate(kn, axis=1).astype(kn_ref.dtype)
        cum_ref[...] = _dot(tri_ref[...], lf, HI) + carry[...]
        carry[...] += jnp.sum(lf, axis=0, keepdims=True)

    vec = pl.BlockSpec((1, HD), lambda i: (0, 0))
    return pl.pallas_call(
        body, grid=(s // t,),
        in_specs=[pl.BlockSpec((t, BW), lambda i: (i, FQ // BW)), pl.BlockSpec((t, BW), lambda i: (i, FK // BW)),
                  pl.BlockSpec((t, HD), lambda i: (i, FF // HD)), vec, vec, vec, pl.BlockSpec((t, t), lambda i: (0, 0))],
        out_specs=[pl.BlockSpec((t, BW), lambda i: (i, 0)), pl.BlockSpec((t, BW), lambda i: (i, 0)),
                   pl.BlockSpec((t, HD), lambda i: (i, 0))],
        out_shape=[_sds((s, BW), MMT), _sds((s, BW), MMT), _sds((s, HD), F32)],
        scratch_shapes=[pltpu.VMEM((1, HD), F32)],
        compiler_params=_cparams(("arbitrary",)), name=name)(zz, zz, zz, gq, gk, bf, tri)


def fox_pre_bwd(zz, gq, gk, bf, dqn, dkn, dcum, dzz, *, name):
    s = zz.shape[0]
    t = _row_tile(s)
    nt = s // t
    triu = jnp.asarray(np.triu(np.ones((t, t), np.float32)))

    def body(q_ref, k_ref, f_ref, gq_ref, gk_ref, b_ref, tri_ref, dqn_ref, dkn_ref, dcum_ref, dzz_ref,
             dz_ref, dff_ref, dgq_ref, dgk_ref, db_ref, carry):
        @pl.when(pl.program_id(0) == 0)
        def _():
            carry[...] = jnp.zeros_like(carry)
            dgq_ref[...] = jnp.zeros_like(dgq_ref)
            dgk_ref[...] = jnp.zeros_like(dgk_ref)
            db_ref[...] = jnp.zeros_like(db_ref)

        dcum_v = dcum_ref[...]
        dlf = _dot(tri_ref[...], dcum_v, HI) + carry[...]
        carry[...] += jnp.sum(dcum_v, axis=0, keepdims=True)
        _, vjp = jax.vjp(_fox_pre_f, _heads(q_ref[...]), _heads(k_ref[...]), f_ref[...], gq_ref[...], gk_ref[...], b_ref[...])
        dq, dk, dff, dgq, dgk, db = vjp((_heads(dqn_ref[...]), _heads(dkn_ref[...]), dlf))
        dz_ref[...] = jnp.concatenate(dq + dk, axis=1).astype(dz_ref.dtype)
        dff_ref[...] = dff.astype(dff_ref.dtype)
        dgq_ref[...] += dgq
        dgk_ref[...] += dgk
        db_ref[...] += db

    r = lambda i: nt - 1 - i
    vec = pl.BlockSpec((1, HD), lambda i: (0, 0))
    return pl.pallas_call(
        body, grid=(nt,),
        in_specs=[pl.BlockSpec((t, BW), lambda i: (r(i), FQ // BW)), pl.BlockSpec((t, BW), lambda i: (r(i), FK // BW)),
                  pl.BlockSpec((t, HD), lambda i: (r(i), FF // HD)), vec, vec, vec, pl.BlockSpec((t, t), lambda i: (0, 0)),
                  pl.BlockSpec((t, BW), lambda i: (r(i), 0)), pl.BlockSpec((t, BW), lambda i: (r(i), 0)),
                  pl.BlockSpec((t, HD), lambda i: (r(i), 0)), pl.BlockSpec(memory_space=pl.ANY)],
        out_specs=[pl.BlockSpec((t, 2 * BW), lambda i: (r(i), FQ // (2 * BW))), pl.BlockSpec((t, HD), lambda i: (r(i), 0)),
                   vec, vec, vec],
        out_shape=[_sds(dzz.shape, dzz.dtype), _sds((s, HD), MMT), _sds((1, HD), F32), _sds((1, HD), F32), _sds((1, HD), F32)],
        input_output_aliases={10: 0}, scratch_shapes=[pltpu.VMEM((1, HD), F32)],
        compiler_params=_cparams(("arbitrary",)), name=name)(zz, zz, zz, gq, gk, bf, triu, dqn, dkn, dcum, dzz)


def _fox_blocks(s):
    return min(256, s), min(512, s)


NEG = -1e30


def fox_attn_fwd(qn, kn, zz, cum_col, cum_row, *, name):
    s = qn.shape[0]
    bq, bk = _fox_blocks(s)

    def body(q_ref, k_ref, v_ref, cc_ref, cr_ref, o_ref, lse_ref):
        qi = pl.program_id(1)
        q = q_ref[...]
        cq = cc_ref[...]
        rows = qi * bq + lax.broadcasted_iota(jnp.int32, (bq, bk), 0)
        cols0 = lax.broadcasted_iota(jnp.int32, (bq, bk), 1)

        def step(j, carry, on_diagonal):
            m, l, acc = carry
            off = pl.multiple_of(j * bk, bk)
            k = k_ref[pl.ds(off, bk), :]
            v = v_ref[pl.ds(off, bk), :].astype(MMT)
            sc = _dot_nt(q, k) + cq - cr_ref[pl.ds(j, 1), :]
            if on_diagonal:
                sc = jnp.where(rows >= cols0 + j * bk, sc, NEG)
            m_new = jnp.maximum(m, jnp.max(sc, axis=1, keepdims=True))
            alpha = jnp.exp(m - m_new)
            p = jnp.exp(sc - m_new)
            return m_new, alpha * l + jnp.sum(p, axis=1, keepdims=True), alpha * acc + _dot(p.astype(MMT), v)

        nfull, nk = (qi * bq + 1) // bk, ((qi + 1) * bq + bk - 1) // bk
        carry = (jnp.full((bq, 1), NEG, F32), jnp.zeros((bq, 1), F32), jnp.zeros((bq, HD), F32))
        carry = lax.fori_loop(0, nfull, functools.partial(step, on_diagonal=False), carry)
        m, l, acc = lax.fori_loop(nfull, nk, functools.partial(step, on_diagonal=True), carry)
        o_ref[...] = acc / l
        lse_ref[...] = m + jnp.log(l)

    return pl.pallas_call(
        body, grid=(NH, s // bq),
        in_specs=[pl.BlockSpec((bq, HD), lambda h, i: (i, h)), pl.BlockSpec((s, HD), lambda h, i: (0, h)),
                  pl.BlockSpec((s, HD), lambda h, i: (0, FV // HD + h)),
                  pl.BlockSpec((None, bq, 1), lambda h, i: (h, i, 0)), pl.BlockSpec((None, s // bk, bk), lambda h, i: (h, 0, 0))],
        out_specs=[pl.BlockSpec((bq, HD), lambda h, i: (i, h)), pl.BlockSpec((None, bq, 1), lambda h, i: (h, i, 0))],
        out_shape=[_sds((s, BW), F32), _sds((NH, s, 1), F32)],
        compiler_params=_cparams(("parallel", "parallel")), name=name)(qn, kn, zz, cum_col, cum_row)


def fox_attn_bwd(qn, kn, zz, cum_col, cum_row, lse, do, dzz, *, name):
    s = qn.shape[0]
    bq, bk = _fox_blocks(s)
    nkc = s // bk

    def body(q_ref, k_ref, v_ref, cc_ref, cr_ref, lse_ref, do_ref, dzz_ref, dq_ref, dk_ref, dv_ref, dc_ref,
             p_scr, dp_scr, dv_scr):
        qi = pl.program_id(1)

        @pl.when(qi == 0)
        def _():
            dk_ref[...] = jnp.zeros_like(dk_ref)
            dv_scr[...] = jnp.zeros_like(dv_scr)
            dc_ref[...] = jnp.zeros_like(dc_ref)

        q = q_ref[...]
        dob = do_ref[...].astype(MMT)
        cq = cc_ref[...]
        lse_v = lse_ref[...]
        rows = qi * bq + lax.broadcasted_iota(jnp.int32, (bq, bk), 0)
        cols0 = lax.broadcasted_iota(jnp.int32, (bq, bk), 1)
        nfull, nk = (qi * bq + 1) // bk, ((qi + 1) * bq + bk - 1) // bk

        def probs(j, delta, on_diagonal):
            off = pl.multiple_of(j * bk, bk)
            sc = _dot_nt(q, k_ref[pl.ds(off, bk), :]) + cq - cr_ref[pl.ds(j, 1), :]
            p = jnp.exp(sc - lse_v)
            if on_diagonal:
                p = jnp.where(rows >= cols0 + j * bk, p, 0.0)
            dp = _dot_nt(dob, v_ref[pl.ds(off, bk), :].astype(MMT))
            p_scr[j] = p
            dp_scr[j] = dp
            return delta + jnp.sum(p * dp, axis=1, keepdims=True)

        delta = lax.fori_loop(0, nfull, functools.partial(probs, on_diagonal=False), jnp.zeros((bq, 1), F32))
        delta = lax.fori_loop(nfull, nk, functools.partial(probs, on_diagonal=True), delta)

        def grads(j, dq):
            off = pl.multiple_of(j * bk, bk)
            p = p_scr[j]
            ds = p * (dp_scr[j] - delta)
            dsm = ds.astype(MMT)
            dv_scr[pl.ds(off, bk), :] += _dot_tn(p.astype(MMT), dob)
            dk_ref[pl.ds(off, bk), :] += _dot_tn(dsm, q)
            dc_ref[pl.ds(j, 1), :] -= jnp.sum(ds, axis=0, keepdims=True)
            return dq + _dot(dsm, k_ref[pl.ds(off, bk), :])

        dq_ref[...] = lax.fori_loop(0, nk, grads, jnp.zeros((bq, HD), F32))

        @pl.when(qi == pl.num_programs(1) - 1)
        def _():
            dv_ref[...] = dv_scr[...].astype(dv_ref.dtype)

    full = lambda c0=0: pl.BlockSpec((s, HD), lambda h, i: (0, c0 + h))
    blk = lambda: pl.BlockSpec((bq, HD), lambda h, i: (i, h))
    colv = lambda: pl.BlockSpec((None, bq, 1), lambda h, i: (h, i, 0))
    rowv = lambda: pl.BlockSpec((None, nkc, bk), lambda h, i: (h, 0, 0))
    return pl.pallas_call(
        body, grid=(NH, s // bq),
        in_specs=[blk(), full(), full(FV // HD), colv(), rowv(), colv(), blk(), pl.BlockSpec(memory_space=pl.ANY)],
        out_specs=[blk(), full(), full(FV // HD), rowv()],
        out_shape=[_sds((s, BW), F32), _sds((s, BW), F32), _sds(dzz.shape, dzz.dtype), _sds((NH, nkc, bk), F32)],
        input_output_aliases={7: 2},
        scratch_shapes=[pltpu.VMEM((nkc, bq, bk), F32), pltpu.VMEM((nkc, bq, bk), F32), pltpu.VMEM((s, HD), F32)],
        compiler_params=_cparams(("parallel", "arbitrary")), name=name)(qn, kn, zz, cum_col, cum_row, lse, do, dzz)


def _branch_f(rets, rgs, glas, ggs, ret_g, gla_g):
    out_r, out_g = [], []
    for h in range(NH):
        xc = rets[h] - jnp.mean(rets[h], axis=-1, keepdims=True)
        y = xc * lax.rsqrt(jnp.mean(xc * xc, axis=-1, keepdims=True) + EPS) * ret_g[h]
        out_r.append(_silu(rgs[h]) * y)
        x = glas[h]
        y = x * lax.rsqrt(jnp.mean(x * x, axis=-1, keepdims=True) + EPS) * gla_g
        out_g.append(_silu(ggs[h]) * y)
    return out_r, out_g


def _w_br_spec(layer):
    return pl.BlockSpec((None, 3, BW, D), lambda i: (layer, 0, 0, 0))


def mix_fwd(ret_raw, gla_raw, fox_o, zz, ret_g, gla_g, b_mg, w_br, *, name, layer):
    s = zz.shape[0]
    t = _row_tile(s)

    def body(r_ref, g_ref, f_ref, rg_ref, gg_ref, gp_ref, rgn_ref, ggn_ref, bmg_ref, w_ref, o_ref):
        rgn = rgn_ref[...]
        br_r, br_g = _branch_f(_heads(r_ref[...]), _heads(rg_ref[...]), _heads(g_ref[...]), _heads(gg_ref[...]),
                               _heads(rgn), ggn_ref[...])
        brs = [jnp.concatenate(br_r, axis=1), jnp.concatenate(br_g, axis=1), f_ref[...]]
        acc = jnp.zeros((t, D), F32)
        for b in range(3):
            gate = jax.nn.sigmoid(gp_ref[:, b * D:(b + 1) * D] + bmg_ref[:, b * D:(b + 1) * D])
            acc = acc + gate * _dot(brs[b].astype(MMT), w_ref[b])
        o_ref[...] = acc.astype(o_ref.dtype)

    row = lambda w, c=0: pl.BlockSpec((t, w), lambda i: (i, c // w))
    cst = lambda shp: pl.BlockSpec(shp, lambda i: (0,) * len(shp))
    return pl.pallas_call(
        body, grid=(s // t,),
        in_specs=[row(BW), row(BW), row(BW), row(BW, RG), row(BW, GG), row(3 * D, GP), cst((1, BW)), cst((1, HD)),
                  cst((1, 3 * D)), _w_br_spec(layer)],
        out_specs=row(D), out_shape=_sds((s, D), MMT),
        compiler_params=_cparams(("parallel",)), name=name)(ret_raw, gla_raw, fox_o, zz, zz, zz, ret_g, gla_g, b_mg, w_br)


def mix_bwd(ret_raw, gla_raw, fox_o, zz, ret_g, gla_g, b_mg, w_br, dmi, *, name, layer):
    s = zz.shape[0]
    t = _row_tile(s)

    def body(r_ref, g_ref, f_ref, rg_ref, gg_ref, gp_ref, rgn_ref, ggn_ref, bmg_ref, w_ref, dmi_ref,
             dr_ref, dg_ref, df_ref, dgp_ref, dw_ref, drgn_ref, dggn_ref, dbmg_ref):
        @pl.when(pl.program_id(0) == 0)
        def _():
            dw_ref[...] = jnp.zeros_like(dw_ref)
            drgn_ref[...] = jnp.zeros_like(drgn_ref)
            dggn_ref[...] = jnp.zeros_like(dggn_ref)
            dbmg_ref[...] = jnp.zeros_like(dbmg_ref)

        (br_r, br_g), vjp = jax.vjp(_branch_f, _heads(r_ref[...]), _heads(rg_ref[...]), _heads(g_ref[...]),
                                    _heads(gg_ref[...]), _heads(rgn_ref[...]), ggn_ref[...])
        brs = [jnp.concatenate(br_r, axis=1).astype(MMT), jnp.concatenate(br_g, axis=1).astype(MMT),
               f_ref[...].astype(MMT)]
        dmi_v = dmi_ref[...].astype(F32)
        dbr = []
        for b in range(3):
            w = w_ref[b]
            ybr = _dot(brs[b], w)
            gate = jax.nn.sigmoid(gp_ref[:, b * D:(b + 1) * D] + bmg_ref[:, b * D:(b + 1) * D])
            dgp = dmi_v * ybr * gate * (1.0 - gate)
            dgp_ref[:, b * D:(b + 1) * D] = dgp.astype(dgp_ref.dtype)
            dbmg_ref[:, b * D:(b + 1) * D] += jnp.sum(dgp, axis=0, keepdims=True)
            dy = (dmi_v * gate).astype(MMT)
            dw_ref[b] += _dot_tn(brs[b], dy)
            dbr.append(_dot_nt(dy, w))
        dr, drg, dg, dgg, drgn, dggn = vjp((_heads(dbr[0]), _heads(dbr[1])))
        dr_ref[...] = jnp.concatenate(dr, axis=1)
        dg_ref[...] = jnp.concatenate(dg, axis=1)
        df_ref[...] = dbr[2]
        dgp_ref[:, RG:RG + BW] = jnp.concatenate(drg, axis=1).astype(dgp_ref.dtype)
        dgp_ref[:, GG:GG + BW] = jnp.concatenate(dgg, axis=1).astype(dgp_ref.dtype)
        drgn_ref[...] += jnp.concatenate(drgn, axis=1)
        dggn_ref[...] += dggn

    row = lambda w, c=0: pl.BlockSpec((t, w), lambda i: (i, c // w))
    cst = lambda shp: pl.BlockSpec(shp, lambda i: (0,) * len(shp))
    return pl.pallas_call(
        body, grid=(s // t,),
        in_specs=[row(BW), row(BW), row(BW), row(BW, RG), row(BW, GG), row(3 * D, GP), cst((1, BW)), cst((1, HD)),
                  cst((1, 3 * D)), _w_br_spec(layer), row(D)],
        out_specs=[row(BW), row(BW), row(BW), row(FV), cst((3, BW, D)), cst((1, BW)), cst((1, HD)), cst((1, 3 * D))],
        out_shape=[_sds((s, BW), F32)] * 3 + [_sds((s, NZZ), MMT), _sds((3, BW, D), F32), _sds((1, BW), F32),
                                              _sds((1, HD), F32), _sds((1, 3 * D), F32)],
        compiler_params=_cparams(("arbitrary",)), name=name)(ret_raw, gla_raw, fox_o, zz, zz, zz, ret_g, gla_g, b_mg, w_br, dmi)


CT = 256


def _shift_down(x, k, rows):
    return jnp.where(rows >= k, pltpu.roll(x, k, 0), 0.0)


def _shift_up(x, k, rows, s):
    return jnp.where(rows < s - k, pltpu.roll(x, s - k, 0), 0.0)


def conv_fwd(ug, w_conv, b_conv, *, name):
    s = ug.shape[0]
    nt = DFF // CT

    def body(u_ref, g_ref, w_ref, b_ref, a_ref):
        u = u_ref[...]
        rows = lax.broadcasted_iota(jnp.int32, u.shape, 0)
        uc = b_ref[...] + w_ref[0:1, :] * _shift_down(u, 2, rows) + w_ref[1:2, :] * _shift_down(u, 1, rows) + w_ref[2:3, :] * u
        a_ref[...] = (_silu(uc) * g_ref[...]).astype(a_ref.dtype)

    return pl.pallas_call(
        body, grid=(nt,),
        in_specs=[pl.BlockSpec((s, CT), lambda j: (0, j)), pl.BlockSpec((s, CT), lambda j: (0, nt + j)),
                  pl.BlockSpec((3, CT), lambda j: (0, j)), pl.BlockSpec((1, CT), lambda j: (0, j))],
        out_specs=pl.BlockSpec((s, CT), lambda j: (0, j)), out_shape=_sds((s, DFF), MMT),
        compiler_params=_cparams(("parallel",)), name=name)(ug, ug, w_conv, b_conv)


def conv_bwd(ug, w_conv, b_conv, da, *, name):
    s = ug.shape[0]
    nt = DFF // CT

    def body(u_ref, g_ref, w_ref, b_ref, da_ref, du_ref, dg_ref, dw_ref, db_ref):
        u = u_ref[...]
        rows = lax.broadcasted_iota(jnp.int32, u.shape, 0)
        u2, u1 = _shift_down(u, 2, rows), _shift_down(u, 1, rows)
        uc = b_ref[...] + w_ref[0:1, :] * u2 + w_ref[1:2, :] * u1 + w_ref[2:3, :] * u
        sg = jax.nn.sigmoid(uc)
        da_v = da_ref[...]
        dg_ref[...] = (da_v * uc * sg).astype(dg_ref.dtype)
        duc = da_v * g_ref[...] * sg * (1.0 + uc * (1.0 - sg))
        du = w_ref[2:3, :] * duc + w_ref[1:2, :] * _shift_up(duc, 1, rows, s) + w_ref[0:1, :] * _shift_up(duc, 2, rows, s)
        du_ref[...] = du.astype(du_ref.dtype)
        dw_ref[0:1, :] = jnp.sum(duc * u2, axis=0, keepdims=True)
        dw_ref[1:2, :] = jnp.sum(duc * u1, axis=0, keepdims=True)
        dw_ref[2:3, :] = jnp.sum(duc * u, axis=0, keepdims=True)
        db_ref[...] = jnp.sum(duc, axis=0, keepdims=True)

    col = lambda: pl.BlockSpec((s, CT), lambda j: (0, j))
    return pl.pallas_call(
        body, grid=(nt,),
        in_specs=[col(), pl.BlockSpec((s, CT), lambda j: (0, nt + j)), pl.BlockSpec((3, CT), lambda j: (0, j)),
                  pl.BlockSpec((1, CT), lambda j: (0, j)), col()],
        out_specs=[col(), col(), pl.BlockSpec((3, CT), lambda j: (0, j)), pl.BlockSpec((1, CT), lambda j: (0, j))],
        out_shape=[_sds((s, DFF), MMT), _sds((s, DFF), MMT), _sds((3, DFF), F32), _sds((1, DFF), F32)],
        compiler_params=_cparams(("parallel",)), name=name)(ug, ug, w_conv, b_conv, da)


def place_tail(dzz, dlr, dff, *, name):
    s = dzz.shape[0]
    t = _row_tile(s)

    def body(a_ref, b_ref, z_ref, o_ref):
        o_ref[...] = jnp.concatenate([a_ref[...], b_ref[...]], axis=1)

    spec = pl.BlockSpec((t, HD), lambda i: (i, 0))
    return pl.pallas_call(
        body, grid=(s // t,), in_specs=[spec, spec, pl.BlockSpec(memory_space=pl.ANY)],
        out_specs=pl.BlockSpec((t, 2 * HD), lambda i: (i, LR // (2 * HD))), out_shape=_sds(dzz.shape, dzz.dtype),
        input_output_aliases={2: 0}, compiler_params=_cparams(("parallel",)), name=name)(dlr, dff, dzz)


def _tiles(s):
    return min(1024, s)


def layer_fwd(x, mod, p, cosf, sinf):
    s = x.shape[0]
    tm = _tiles(s)
    l = p["l"]
    shift1, scale1, gate1, shift2, scale2, gate2 = mod
    h = norm_mod(x, p["norm1_g"], scale1, shift1, name="norm_mod")
    zz = mm_nn(h, p["w1"], tm=tm, tn=768, out_dtype=F32, name="mm_w1", layer=l)
    ret_raw, rprev = retention_fwd(zz, cosf, sinf, name="ret_fwd")
    gla_raw, sprev = gla_fwd(zz, p["w_a2p"], p["b_gla_a"], name="gla_fwd")
    qn, kn, cum = fox_pre(zz, p["q_norm_g"], p["k_norm_g"], p["b_foxp"], name="fox_pre")
    bq, bk = _fox_blocks(s)
    cum_t = cum[:, :NH].T
    cum_col, cum_row = cum_t[:, :, None], cum_t.reshape(NH, s // bk, bk)
    fox_o, lse = fox_attn_fwd(qn, kn, zz, cum_col, cum_row, name="fox_fwd")
    mi = mix_fwd(ret_raw, gla_raw, fox_o, zz, p["ret_norm_g"], p["gla_norm_g"], p["b_mg"], p["w_br"], name="mix_fwd",
                 layer=l)
    x1, mixed = mm_nn_residual(mi, p["w_o"], x, gate1, tm=tm, tn=512, name="mm_wo", layer=l)
    h2 = norm_mod(x1, p["norm2_g"], scale2, shift2, name="norm_mod")
    ug = mm_nn(h2, p["w_up"], tm=tm, tn=512, out_dtype=F32, name="mm_wup", layer=l)
    a = conv_fwd(ug, p["w_conv"], p["b_conv"], name="conv_fwd")
    x2, y = mm_nn_residual(a, p["w_down"], x1, gate2, tm=tm, tn=512, name="mm_wdown", layer=l)
    saved = dict(x=x, h=h, zz=zz, ret_raw=ret_raw, rprev=rprev, gla_raw=gla_raw, sprev=sprev, qn=qn, kn=kn,
                 cum_col=cum_col, cum_row=cum_row, fox_o=fox_o, lse=lse, mi=mi, mixed=mixed, x1=x1, h2=h2, ug=ug, a=a, y=y)
    return x2, saved


def layer_bwd(dx2, mod, p, sv, cosf, sinf, stacks, slot):
    s = dx2.shape[0]
    tm = _tiles(s)
    l = p["l"]
    shift1, scale1, gate1, shift2, scale2, gate2 = mod
    g, stacks = {}, dict(stacks)
    dy, dgate2 = gate_bwd(dx2, sv["y"], gate2, name="gate_bwd")
    stacks["w_down"] = mm_tn(sv["a"], dy, tm=min(1408, DFF), tn=512, out_dtype=MMT, name="mm_dwdown",
                             stack=stacks["w_down"], layer=slot)
    da = mm_nt(dy, p["w_down"], tm=tm, tn=1408, out_dtype=F32, name="mm_da", layer=l)
    du, dg, g["w_conv"], g["b_conv"] = conv_bwd(sv["ug"], p["w_conv"], p["b_conv"], da, name="conv_bwd")
    stacks["w_up"] = mm_tn(sv["h2"], du, tm=D, tn=CT, out_dtype=MMT, name="mm_dwup_u", stack=stacks["w_up"], layer=slot)
    stacks["w_up"] = mm_tn(sv["h2"], dg, tm=D, tn=CT, out_dtype=MMT, name="mm_dwup_g", stack=stacks["w_up"], layer=slot,
                           out_col0=DFF)
    dh2 = mm_nt2(du, dg, p["w_up"], tm=min(512, s), tn=D, name="mm_dh2", layer=l)
    dx1, g["norm2_g"], dscale2, dshift2 = norm_mod_bwd(sv["x1"], dh2, dx2, p["norm2_g"], scale2, shift2, name="norm_mod_bwd")
    dmixed, dgate1 = gate_bwd(dx1, sv["mixed"], gate1, name="gate_bwd")
    stacks["w_o"] = mm_tn(sv["mi"], dmixed, tm=512, tn=512, out_dtype=MMT, name="mm_dwo", stack=stacks["w_o"], layer=slot)
    dmi = mm_nt(dmixed, p["w_o"], tm=tm, tn=512, out_dtype=MMT, name="mm_dmi", layer=l)
    zz = sv["zz"]
    (dret, dgla, dfox, dzz, g["w_br"], g["ret_norm_g"], g["gla_norm_g"], g["b_mg"]) = mix_bwd(
        sv["ret_raw"], sv["gla_raw"], sv["fox_o"], zz, p["ret_norm_g"], p["gla_norm_g"], p["b_mg"], p["w_br"], dmi,
        name="mix_bwd", layer=l)
    dqn, dkn, dzz, dcum_row = fox_attn_bwd(sv["qn"], sv["kn"], zz, sv["cum_col"], sv["cum_row"], sv["lse"], dfox, dzz,
                                           name="fox_bwd")
    dcum = jnp.pad(dcum_row.reshape(NH, s).T, ((0, 0), (0, HD - NH)))
    dzz, dff, g["q_norm_g"], g["k_norm_g"], g["b_foxp"] = fox_pre_bwd(
        zz, p["q_norm_g"], p["k_norm_g"], p["b_foxp"], dqn, dkn, dcum, dzz, name="fox_pre_bwd")
    dzz, dlr, g["w_a2p"], g["b_gla_a"] = gla_bwd(zz, p["w_a2p"], p["b_gla_a"], sv["sprev"], dgla, dzz, name="gla_bwd")
    dzz = retention_bwd(zz, cosf, sinf, sv["rprev"], dret, dzz, name="ret_bwd")
    dzz = place_tail(dzz, dlr, dff, name="place_tail")
    stacks["w_mg"] = mm_tn(sv["h"], dzz, tm=512, tn=768, out_dtype=MMT, name="mm_dwmg", ncols=WZ0, stack=stacks["w_mg"],
                           layer=slot)
    dwz = mm_tn(sv["h"], dzz, tm=512, tn=768, out_dtype=MMT, name="mm_dwz", col0=WZ0)
    stacks["w_in"] = unalign_dw_in(dwz, stacks["w_in"], slot)
    dh = mm_nt(dzz, p["w1"], tm=min(256, s), tn=D, out_dtype=F32, name="mm_dh", layer=l)
    dx, g["norm1_g"], dscale1, dshift1 = norm_mod_bwd(sv["x"], dh, dx1, p["norm1_g"], scale1, shift1, name="norm_mod_bwd")
    dmod = jnp.concatenate([dshift1, dscale1, dgate1, dshift2, dscale2, dgate2], axis=1)
    return dx, g, dmod, stacks


def _align_cols(w_in, w_mg):
    z = lambda n: jnp.zeros((w_in.shape[0], n), w_in.dtype)
    seg = lambda name: w_in[:, W_IN_COLS[name][0]:W_IN_COLS[name][1]]
    return jnp.concatenate([w_mg, seg("rg"), seg("gg"), seg("fv"), seg("rqkv"), seg("gqkv"), seg("fqk"), seg("lr"),
                            z(HD - GLR), seg("ff"), z(HD - NH)], axis=1)


def _unalign_cols(dwz):
    seg = lambda c0, name: dwz[:, c0 - WZ0:c0 - WZ0 + W_IN_COLS[name][1] - W_IN_COLS[name][0]]
    return jnp.concatenate([seg(RQ, "rqkv"), seg(RG, "rg"), seg(GQ, "gqkv"), seg(LR, "lr"), seg(GG, "gg"), seg(FQ, "fqk"),
                            seg(FV, "fv"), seg(FF, "ff")], axis=1)


def build_w1(w_in_sh, w_mg):
    nl = w_mg.shape[0]
    t = _row_tile(D)

    def body(s_ref, g_ref, o_ref):
        o_ref[...] = _align_cols(jnp.concatenate([s_ref[k] for k in range(4)], axis=1), g_ref[...])

    return pl.pallas_call(
        body, grid=(nl, D // t),
        in_specs=[pl.BlockSpec((None, 4, t, IN_W // 4), lambda l, i: (l, 0, i, 0)), pl.BlockSpec((None, t, WZ0), lambda l, i: (l, i, 0))],
        out_specs=pl.BlockSpec((None, t, NZZ), lambda l, i: (l, i, 0)), out_shape=_sds((nl, D, NZZ), w_mg.dtype),
        compiler_params=_cparams(("parallel", "parallel")), name="build_w1")(w_in_sh, w_mg)


def unalign_dw_in(dwz, stack, layer):
    t = _row_tile(D)

    def body(z_ref, s_ref, o_ref):
        w = _unalign_cols(z_ref[...])
        for k in range(4):
            o_ref[k] = w[:, k * (IN_W // 4):(k + 1) * (IN_W // 4)]

    return pl.pallas_call(
        body, grid=(D // t,),
        in_specs=[pl.BlockSpec((t, NZZ - WZ0), lambda i: (i, 0)), pl.BlockSpec(memory_space=pl.ANY)],
        out_specs=pl.BlockSpec((None, 4, t, IN_W // 4), lambda i: (layer, 0, i, 0)), out_shape=_sds(stack.shape, stack.dtype),
        input_output_aliases={1: 0}, compiler_params=_cparams(("parallel",)), name="unalign_dw_in")(dwz, stack)


def layer_params(w, big, l):
    row = lambda v: v[l][None, :]
    return dict(
        l=0, norm1_g=row(w["norm1_g"]), norm2_g=row(w["norm2_g"]), w1=big["w1"],
        w_a2p=jnp.pad(w["w_gla_a2"][l], ((0, HD - GLR), (0, 0))), b_gla_a=row(w["b_gla_a"]),
        b_foxp=jnp.pad(row(w["b_fox_f"]), ((0, 0), (0, HD - NH))), ret_norm_g=row(w["ret_norm_g"]),
        gla_norm_g=row(w["gla_norm_g"]), q_norm_g=row(w["q_norm_g"]), k_norm_g=row(w["k_norm_g"]),
        w_br=big["w_br"], b_mg=row(w["b_mg"]), w_o=big["w_o"], w_up=big["w_up"], w_conv=w["w_conv"][l],
        b_conv=row(w["b_conv"]), w_down=big["w_down"])


def layer_grads(g):
    vec = lambda v: v[0]
    return dict(
        norm1_g=vec(g["norm1_g"]), norm2_g=vec(g["norm2_g"]), w_gla_a2=g["w_a2p"][:GLR], b_gla_a=vec(g["b_gla_a"]),
        b_fox_f=g["b_foxp"][0, :NH], ret_norm_g=vec(g["ret_norm_g"]), gla_norm_g=vec(g["gla_norm_g"]),
        q_norm_g=vec(g["q_norm_g"]), k_norm_g=vec(g["k_norm_g"]), w_br=g["w_br"], b_mg=vec(g["b_mg"]),
        w_conv=g["w_conv"], b_conv=vec(g["b_conv"]))


def ada_mod(c_all, w_ada, b_ada):
    nl, _, n = w_ada.shape

    def body(c_ref, w_ref, b_ref, o_ref):
        o_ref[...] = _dot(_silu(c_ref[...]), w_ref[...], HI) + b_ref[...]

    return pl.pallas_call(
        body, grid=(nl,),
        in_specs=[pl.BlockSpec((8, D), lambda l: (0, 0)), pl.BlockSpec((None, D, n), lambda l: (l, 0, 0)),
                  pl.BlockSpec((None, 1, n), lambda l: (l, 0, 0))],
        out_specs=pl.BlockSpec((None, 8, n), lambda l: (l, 0, 0)), out_shape=_sds((nl, 8, n), F32),
        compiler_params=_cparams(("parallel",)), name="ada_mod")(c_all, w_ada, b_ada)


def ada_dw(c_all, dmod):
    nl, _, n = dmod.shape

    def body(c_ref, d_ref, o_ref):
        o_ref[...] = _dot_tn(_silu(c_ref[...]), d_ref[...], HI)

    return pl.pallas_call(
        body, grid=(nl,),
        in_specs=[pl.BlockSpec((8, D), lambda l: (0, 0)), pl.BlockSpec((None, 8, n), lambda l: (l, 0, 0))],
        out_specs=pl.BlockSpec((None, D, n), lambda l: (l, 0, 0)), out_shape=_sds((nl, D, n), F32),
        compiler_params=_cparams(("parallel",)), name="ada_dw")(c_all, dmod)


def sum_devices(g):
    def body(g_ref, o_ref):
        acc = g_ref[0]
        for d in range(1, 8):
            acc = acc + g_ref[d]
        o_ref[...] = acc

    return pl.pallas_call(body, out_shape=_sds(g.shape[1:], F32), name="sum_devices")(g)


def adamw(w, g, m, v, *, block, name, rows=None, into=None):
    nd = w.ndim
    lo, hi = (0, w.shape[0]) if rows is None else rows
    grid = ((hi - lo) // block[0],) + tuple(w.shape[i] // block[i] for i in range(1, nd))
    first = lo // block[0]
    bc1 = 1.0 - ADAM_B1 ** ADAM_STEP
    bc2 = 1.0 - ADAM_B2 ** ADAM_STEP

    def body(w_ref, g_ref, m_ref, v_ref, *rest):
        d_ref, nm_ref, nv_ref = rest[-3:]
        gv = g_ref[...]
        nm = ADAM_B1 * m_ref[...] + (1.0 - ADAM_B1) * gv
        nv = ADAM_B2 * v_ref[...] + (1.0 - ADAM_B2) * (gv * gv)
        nm_ref[...] = nm
        nv_ref[...] = nv
        d_ref[...] = -ADAM_LR * ((nm / bc1) / (jnp.sqrt(nv / bc2) + ADAM_EPS) + ADAM_WD * w_ref[...])

    spec = pl.BlockSpec(tuple(block), lambda i, *j: (first + i,) + j)
    given = [] if into is None else list(into)
    return pl.pallas_call(
        body, grid=grid, in_specs=[spec] * 4 + [pl.BlockSpec(memory_space=pl.ANY)] * len(given), out_specs=[spec] * 3,
        out_shape=[_sds(w.shape, F32)] * 3, input_output_aliases={4 + i: i for i in range(len(given))},
        compiler_params=_cparams(("parallel",) * nd), name=name)(w, g, m, v, *given)


MESH = pl.DeviceIdType.MESH
ANY = pl.BlockSpec(memory_space=pl.ANY)
VM = pl.BlockSpec(memory_space=pltpu.VMEM)


def _place():
    x, y, c = lax.axis_index("x"), lax.axis_index("y"), lax.axis_index("c")
    return x, y, c, [(1 - x, y), (x, 1 - y), (1 - x, 1 - y)]


def small_allgather(v, *, name):
    m_per, n = v.shape

    def body(x_ref, out_ref, send_sems, recv_sems, local_sem):
        x, y, c, chips = _place()
        me, sibling = (x, y, c), (x, y, 1 - c)

        def rows(px, py, pc):
            return out_ref.at[pl.ds((4 * px + 2 * py + pc) * m_per, m_per), :]

        def copy(k, block, to, src=None):
            return pltpu.make_async_remote_copy(
                src_ref=rows(*block) if src is None else src, dst_ref=rows(*block),
                send_sem=send_sems.at[k], recv_sem=recv_sems.at[k], device_id=to, device_id_type=MESH)

        mine = pltpu.make_async_copy(x_ref, rows(*me), local_sem)
        mine.start()
        first = [copy(0, me, sibling, src=x_ref)]
        first += [copy(1 + j, me, (*chip, c), src=x_ref) for j, chip in enumerate(chips)]
        for cp in first:
            cp.start()
        passed = [copy(4 + j, (*chip, c), sibling) for j, chip in enumerate(chips)]
        for j, chip in enumerate(chips):
            copy(1 + j, (*chip, c), me).wait_recv()
            passed[j].start()
        copy(0, sibling, me).wait_recv()
        for j, chip in enumerate(chips):
            copy(4 + j, (*chip, 1 - c), me).wait_recv()
        for cp in first + passed:
            cp.wait_send()
        mine.wait()

    return pl.pallas_call(
        body, out_shape=_sds((8 * m_per, n), v.dtype), in_specs=[VM], out_specs=VM,
        scratch_shapes=[pltpu.SemaphoreType.DMA((7,)), pltpu.SemaphoreType.DMA((7,)), pltpu.SemaphoreType.DMA],
        name=name)(v)


TENSORS = {
    "w_in": ("lead", None, (4, D, 1285), (1, D, 1285)),
    "w_mg": ("col", 768, (D, 3072), (512, 3072)),
    "w_br": ("col", 256, (3, BW, D), (3, BW, D)),
    "w_o": ("row", 256, (D, D), (D, D)),
    "w_up": ("col", 1408, (D, 5632), (256, 5632)),
    "w_down": ("row", 704, (DFF, D), (704, D)),
}
BIG = tuple(TENSORS)


def _shard_shape(name):
    kind, width, full, _ = TENSORS[name]
    if kind == "lead":
        return full[1:]
    return full[:-1] + (width,) if kind == "col" else (width,) + full[1:]


def _shard_view(ref, layers, name, k):
    kind, width, full, _ = TENSORS[name]
    if kind == "lead":
        return ref.at[layers, k]
    if kind == "row":
        return ref.at[layers, pl.ds(k * width, width)]
    return ref.at[(layers,) + (slice(None),) * (len(full) - 1) + (pl.ds(k * width, width),)]


def _remote(send_sems, recv_sems, k, src, dst, to):
    return pltpu.make_async_remote_copy(src_ref=src, dst_ref=dst, send_sem=send_sems.at[k], recv_sem=recv_sems.at[k],
                                        device_id=to, device_id_type=MESH)


def _dma_sems(n):
    return [pltpu.SemaphoreType.DMA((n,)), pltpu.SemaphoreType.DMA((n,))]


RS_GROUP = 2
HBM = pl.BlockSpec(memory_space=pltpu.HBM)
SEM = pl.BlockSpec(memory_space=pltpu.SEMAPHORE)
SPLIT_CALL = dict(compiler_params=pltpu.CompilerParams(has_side_effects=pltpu.SideEffectType.DATAFLOW_SIDE_EFFECTING))
PULL_SET = (("w_up", "w_down", "w_o"), ("w_in", "w_mg", "w_br"))


def _in_hbm(a):
    return pltpu.with_memory_space_constraint(a, pltpu.HBM)


def _pull_sends(send_sems, recv_sems, p, o, layer, core, x, y, chips):
    return [_remote(send_sems, recv_sems, 3 * BIG.index(n) + j, p[n].at[layer], _shard_view(o[n], 0, n, 2 * x + y), (*chip, core))
            for n in PULL_SET[core] for j, chip in enumerate(chips)]


def _pull_arrivals(send_sems, recv_sems, o, core, x, y, chips, to):
    views = [(3 * BIG.index(n) + j, _shard_view(o[n], 0, n, 2 * chip[0] + chip[1]))
             for n in PULL_SET[core] for j, chip in enumerate(chips)]
    return [_remote(send_sems, recv_sems, k, v, v, to) for k, v in views]


def gather_start(shards, layer, *, name):
    nt = len(BIG)

    def body(*refs):
        p, o = dict(zip(BIG, refs[:nt])), dict(zip(BIG, refs[nt:2 * nt]))
        x, y, c, chips = _place()
        for core in (0, 1):
            @pl.when(c == core)
            def _():
                for cp in _pull_sends(refs[2 * nt], refs[2 * nt + 1], p, o, layer, core, x, y, chips):
                    cp.start()
        refs[-1][...] = jnp.zeros_like(refs[-1])

    lands = [_in_hbm(lax.empty((1,) + TENSORS[n][2], shards[n].dtype)) for n in BIG]
    outs = pl.pallas_call(
        body,
        out_shape=(pltpu.SemaphoreType.DMA((3 * nt,)), pltpu.SemaphoreType.DMA((3 * nt,)),
                   *[pltpu.HBM(a.shape, a.dtype) for a in lands], _sds((8, HD), F32)),
        in_specs=[HBM] * (2 * nt), out_specs=(SEM, SEM, *[HBM] * nt, VM),
        input_output_aliases={nt + t: 2 + t for t in range(nt)}, name=name, **SPLIT_CALL)(
            *[_in_hbm(shards[n]) for n in BIG], *lands)
    return outs[0], outs[1], outs[2:2 + nt], outs[-1]


def gather_wait(send_sems, recv_sems, shards, lands, after, layer, *, name):
    nt = len(BIG)

    def body(*refs):
        p, o = dict(zip(BIG, refs[:nt])), dict(zip(BIG, refs[nt:2 * nt]))
        ss, rs = refs[2 * nt], refs[2 * nt + 1]
        x, y, c, chips = _place()
        for core in (0, 1):
            @pl.when(c == core)
            def _():
                for cp in _pull_sends(ss, rs, p, o, layer, core, x, y, chips):
                    cp.wait_send()
                for cp in _pull_arrivals(ss, rs, o, core, x, y, chips, (x, y, core)):
                    cp.wait_recv()

    return pl.pallas_call(
        body, out_shape=tuple(pltpu.HBM(a.shape, a.dtype) for a in lands),
        in_specs=[HBM] * (2 * nt) + [SEM, SEM, ANY], out_specs=tuple([HBM] * nt),
        input_output_aliases={nt + t: t for t in range(nt)}, name=name, **SPLIT_CALL)(
            *[_in_hbm(shards[n]) for n in BIG], *lands, send_sems, recv_sems, after)


def gather_forward(shards, lands, layer, *, name):
    nt = len(BIG)

    def body(*refs):
        p, o = dict(zip(BIG, refs[:nt])), dict(zip(BIG, refs[2 * nt:3 * nt]))
        ss, rs = refs[3 * nt:]
        x, y, c, chips = _place()
        for core in (0, 1):
            @pl.when(c == core)
            def _():
                me, sibling = (x, y, core), (x, y, 1 - core)
                sends = _pull_arrivals(ss, rs, o, core, x, y, chips, sibling)
                sends += [_remote(ss, rs, 3 * nt + t, p[n].at[layer], _shard_view(o[n], 0, n, 2 * x + y), sibling)
                          for t, n in enumerate(BIG)]
                for cp in sends:
                    cp.start()
                for cp in sends:
                    cp.wait_send()
                for cp in _pull_arrivals(ss, rs, o, 1 - core, x, y, chips, me):
                    cp.wait_recv()
                for t, n in enumerate(BIG):
                    own = _shard_view(o[n], 0, n, 2 * x + y)
                    _remote(ss, rs, 3 * nt + t, own, own, me).wait_recv()

    outs = pl.pallas_call(
        body, out_shape=[_sds(a.shape, a.dtype) for a in lands], in_specs=[ANY] * (2 * nt), out_specs=[ANY] * nt,
        input_output_aliases={nt + t: t for t in range(nt)}, scratch_shapes=_dma_sems(4 * nt), name=name)(
            *[shards[n] for n in BIG], *lands)
    return dict(zip(BIG, outs))


def pair_exchange(g, *, name):
    hh = g[BIG[0]].shape[0] // 2
    nt = len(BIG)

    def body(*refs):
        send_sems, recv_sems = refs[2 * nt:]
        x, y, c, _ = _place()
        copies = [_remote(send_sems, recv_sems, t, refs[t].at[pl.ds(hh * (1 - c), hh)], refs[nt + t], (x, y, 1 - c))
                  for t in range(nt)]
        for cp in copies:
            cp.start()
        for cp in copies:
            cp.wait()

    outs = pl.pallas_call(
        body, out_shape=[_sds((hh,) + g[n].shape[1:], g[n].dtype) for n in BIG], in_specs=[ANY] * nt, out_specs=[ANY] * nt,
        scratch_shapes=_dma_sems(nt), name=name)(*[g[n] for n in BIG])
    return dict(zip(BIG, outs))


def _chip_copies(send_sems, recv_sems, s_refs, land_refs, c, chips):
    hl = s_refs[0].shape[0]
    return [_remote(send_sems, recv_sems, 3 * t + j, _shard_view(s_refs[t], pl.ds(0, hl), n, 2 * chip[0] + chip[1]),
                    land_refs[t].at[j], (*chip, c))
            for t, n in enumerate(BIG) for j, chip in enumerate(chips)]


def _landing_shapes(s):
    hl = s[BIG[0]].shape[0]
    return [_sds((3, hl) + _shard_shape(n), s[n].dtype) for n in BIG]


def chip_exchange(s, *, name):
    nt = len(BIG)

    def body(*refs):
        send_sems, recv_sems = refs[2 * nt:]
        x, y, c, chips = _place()
        copies = _chip_copies(send_sems, recv_sems, refs[:nt], refs[nt:2 * nt], c, chips)
        for cp in copies:
            cp.start()
        for cp in copies:
            cp.wait()

    outs = pl.pallas_call(
        body, out_shape=_landing_shapes(s), in_specs=[ANY] * nt, out_specs=[ANY] * nt,
        scratch_shapes=_dma_sems(3 * nt), name=name)(*[s[n] for n in BIG])
    return dict(zip(BIG, outs))


def chip_exchange_start(s, *, name, after=None):
    nt = len(BIG)
    first = [] if after is None else [after]

    def body(*refs):
        o = refs[2 * nt + len(first):]
        x, y, c, chips = _place()
        for cp in _chip_copies(o[0], o[1], refs[:nt], refs[nt:2 * nt], c, chips):
            cp.start()
        refs[-1][...] = jnp.zeros_like(refs[-1])

    lands = [_in_hbm(lax.empty(d.shape, d.dtype)) for d in _landing_shapes(s)]
    srcs = [_in_hbm(s[n]) for n in BIG]
    outs = pl.pallas_call(
        body,
        out_shape=(pltpu.SemaphoreType.DMA((3 * nt,)), pltpu.SemaphoreType.DMA((3 * nt,)),
                   *[pltpu.HBM(a.shape, a.dtype) for a in srcs + lands], _sds((8, HD), F32)),
        in_specs=[HBM] * (2 * nt) + [ANY] * len(first), out_specs=(SEM, SEM, *[HBM] * (2 * nt), VM),
        input_output_aliases={t: 2 + t for t in range(2 * nt)}, name=name, **SPLIT_CALL)(*srcs, *lands, *first)
    return outs[0], outs[1], outs[2:2 + nt], outs[2 + nt:2 + 2 * nt], outs[-1]


def chip_exchange_wait(send_sems, recv_sems, srcs, lands, after, *, name):
    nt = len(BIG)

    def body(*refs):
        x, y, c, chips = _place()
        for cp in _chip_copies(refs[2 * nt], refs[2 * nt + 1], refs[:nt], refs[nt:2 * nt], c, chips):
            cp.wait_send()
            cp.wait_recv()

    outs = pl.pallas_call(
        body, out_shape=tuple(pltpu.HBM(a.shape, a.dtype) for a in list(srcs) + list(lands)),
        in_specs=[HBM] * (2 * nt) + [SEM, SEM] + [ANY] * len(after), out_specs=tuple([HBM] * (2 * nt)),
        input_output_aliases={t: t for t in range(2 * nt)}, name=name, **SPLIT_CALL)(
            *srcs, *lands, send_sems, recv_sems, *after)
    return dict(zip(BIG, outs[:nt])), dict(zip(BIG, outs[nt:]))


def pair_share(f, l0, hh, *, name):
    nt = len(BIG)

    def body(*refs):
        o = refs[nt:2 * nt]
        send_sems, recv_sems = refs[2 * nt:]
        x, y, c, _ = _place()
        mine, theirs = pl.ds(l0 + hh * c, hh), pl.ds(l0 + hh * (1 - c), hh)
        copies = [_remote(send_sems, recv_sems, t, o[t].at[mine], o[t].at[mine], (x, y, 1 - c)) for t in range(nt)]
        for cp in copies:
            cp.start()
        for t, cp in enumerate(copies):
            cp.wait_send()
            _remote(send_sems, recv_sems, t, o[t].at[theirs], o[t].at[theirs], (x, y, c)).wait_recv()

    outs = pl.pallas_call(
        body, out_shape=[_sds(f[n].shape, f[n].dtype) for n in BIG], in_specs=[ANY] * nt, out_specs=[ANY] * nt,
        input_output_aliases={t: t for t in range(nt)}, scratch_shapes=_dma_sems(nt), name=name)(*[f[n] for n in BIG])
    return dict(zip(BIG, outs))


def pair_add(g, r, idx, *, tensor, name):
    _, _, full, blk = TENSORS[tensor]
    hh = r.shape[0]

    def body(idx_ref, g_ref, r_ref, o_ref):
        o_ref[...] = (g_ref[...].astype(F32) + r_ref[...].astype(F32)).astype(o_ref.dtype)

    own = pl.BlockSpec((None,) + blk, lambda *a: (a[0],) + a[1:-1])
    return pl.pallas_call(
        body, out_shape=_sds(r.shape, r.dtype),
        grid_spec=pltpu.PrefetchScalarGridSpec(
            num_scalar_prefetch=1, grid=(hh,) + tuple(f // b for f, b in zip(full, blk)),
            in_specs=[pl.BlockSpec((None,) + blk, lambda *a: (hh * a[-1][0] + a[0],) + a[1:-1]), own], out_specs=own),
        compiler_params=_cparams(("parallel",) * (1 + len(full))), name=name)(idx, g, r)


def chip_add(s, r, idx, totals, l0, *, tensor, name):
    kind, width, full, _ = TENSORS[tensor]
    shard = _shard_shape(tensor)
    hh = s.shape[0]
    zeros = (0,) * len(shard)

    def body(idx_ref, s_ref, r0_ref, r1_ref, r2_ref, t_ref, o_ref):
        o_ref[...] = ((s_ref[...].astype(F32) + r0_ref[...].astype(F32)) + r1_ref[...].astype(F32)) + r2_ref[...].astype(F32)

    if kind == "lead":
        mine = pl.BlockSpec((None, None) + shard, lambda i, ix: (i, ix[1]) + zeros)
    elif kind == "row":
        mine = pl.BlockSpec((None,) + shard, lambda i, ix: (i, ix[1]) + zeros[1:])
    else:
        mine = pl.BlockSpec((None,) + shard, lambda i, ix: (i,) + zeros[1:] + (ix[1],))
    peer = lambda j: pl.BlockSpec((None, None) + shard, lambda i, ix: (j, i) + zeros)
    return pl.pallas_call(
        body, out_shape=_sds(totals.shape, F32),
        grid_spec=pltpu.PrefetchScalarGridSpec(
            num_scalar_prefetch=1, grid=(hh,), in_specs=[mine, peer(0), peer(1), peer(2), pl.BlockSpec(memory_space=pl.ANY)],
            out_specs=pl.BlockSpec((None,) + shard, lambda i, ix: (l0 + hh * ix[0] + i,) + zeros)),
        input_output_aliases={5: 0}, compiler_params=_cparams(("parallel",)), name=name)(idx, s, r, r, r, totals)


def _flat_rows(arrs):
    v = jnp.concatenate([a.reshape(-1) for a in arrs])
    n = -(-v.shape[0] // 1024) * 1024
    return jnp.pad(v, (0, n - v.shape[0])).reshape(n // HD, HD)


def _unflat(buf, shapes):
    v, out, o = buf.reshape(-1), [], 0
    for s in shapes:
        n = int(np.prod(s))
        out.append(v[o:o + n].reshape(s))
        o += n
    return out


WEIGHTS = ("norm1_g", "norm2_g", "w_ada", "b_ada", "w_in", "w_gla_a2", "b_gla_a", "b_fox_f", "ret_norm_g", "gla_norm_g",
           "q_norm_g", "k_norm_g", "w_br", "w_mg", "b_mg", "w_o", "w_up", "w_conv", "b_conv", "w_down")
REPLICATED = ("norm1_g", "norm2_g", "b_gla_a", "b_fox_f", "ret_norm_g", "gla_norm_g", "q_norm_g", "k_norm_g", "b_mg", "b_conv")
ADAM_BLOCKS = dict(w_ada=(1, 256, 1536), w_in=(1, 256, 1285), w_br=(1, 3, BW, 256), w_mg=(1, 512, 768), w_o=(2, 256, D),
                   w_up=(1, 256, 1408), w_down=(1, 352, D))
ALL_AXES = ("x", "y", "c")


def kernel(x, c, norm1_g, norm2_g, w_ada, b_ada, w_in, w_gla_a2, b_gla_a, b_fox_f, ret_norm_g, gla_norm_g, q_norm_g, k_norm_g, w_br, w_mg, b_mg, w_o, w_up, w_conv, b_conv, w_down, loss_target, m_norm1_g, m_norm2_g, m_w_ada, m_b_ada, m_w_in, m_w_gla_a2, m_b_gla_a, m_b_fox_f, m_ret_norm_g, m_gla_norm_g, m_q_norm_g, m_k_norm_g, m_w_br, m_w_mg, m_b_mg, m_w_o, m_w_up, m_w_conv, m_b_conv, m_w_down, v_norm1_g, v_norm2_g, v_w_ada, v_b_ada, v_w_in, v_w_gla_a2, v_b_gla_a, v_b_fox_f, v_ret_norm_g, v_gla_norm_g, v_q_norm_g, v_k_norm_g, v_w_br, v_w_mg, v_b_mg, v_w_o, v_w_up, v_w_conv, v_b_conv, v_w_down):
    w = dict(zip(WEIGHTS, (norm1_g, norm2_g, w_ada, b_ada, w_in, w_gla_a2, b_gla_a, b_fox_f, ret_norm_g, gla_norm_g,
                           q_norm_g, k_norm_g, w_br, w_mg, b_mg, w_o, w_up, w_conv, b_conv, w_down)))
    m = dict(zip(WEIGHTS, (m_norm1_g, m_norm2_g, m_w_ada, m_b_ada, m_w_in, m_w_gla_a2, m_b_gla_a, m_b_fox_f, m_ret_norm_g,
                           m_gla_norm_g, m_q_norm_g, m_k_norm_g, m_w_br, m_w_mg, m_b_mg, m_w_o, m_w_up, m_w_conv, m_b_conv,
                           m_w_down)))
    v = dict(zip(WEIGHTS, (v_norm1_g, v_norm2_g, v_w_ada, v_b_ada, v_w_in, v_w_gla_a2, v_b_gla_a, v_b_fox_f, v_ret_norm_g,
                           v_gla_norm_g, v_q_norm_g, v_k_norm_g, v_w_br, v_w_mg, v_b_mg, v_w_o, v_w_up, v_w_conv, v_b_conv,
                           v_w_down)))
    nl = norm1_g.shape[0]
    seq = x.shape[1]
    xi, yi, ci = lax.axis_index("x"), lax.axis_index("y"), lax.axis_index("c")
    k_me = 2 * xi + yi
    b_me = 4 * xi + 2 * yi + ci
    ada_n = w_ada.shape[2]
    a2_n, conv_n = w_gla_a2.shape[2], w_conv.shape[2]

    shards = {n: w[n].astype(MMT) for n in BIG}
    started = gather_start(shards, 0, name="gather0_start")

    def gather_finish(l, started, after):
        send_sems, recv_sems, lands, _ = started
        lands = gather_wait(send_sems, recv_sems, shards, lands, after, l, name=f"gather{l}_wait")
        big = gather_forward(shards, lands, l, name=f"gather{l}_forward")
        big["w1"] = build_w1(big["w_in"], big["w_mg"])
        return big

    blk = _flat_rows([c, w_gla_a2, w_conv]) + started[-1][0, 0]
    g1 = small_allgather(blk, name="gather_small").reshape(8, blk.shape[0], HD)
    c_all = g1[:, :D // HD].reshape(8, D)
    by_chip = g1[0::2].reshape(4, -1)[:, D:]
    a2_sh, conv_sh = by_chip[:, :nl * GLR * a2_n], by_chip[:, nl * GLR * a2_n:nl * (GLR * a2_n + 3 * conv_n)]
    full_small = dict(
        w_gla_a2=a2_sh.reshape(4, nl, GLR, a2_n).transpose(1, 2, 0, 3).reshape(nl, GLR, 4 * a2_n),
        w_conv=conv_sh.reshape(4, nl, 3, conv_n).transpose(1, 2, 0, 3).reshape(nl, 3, 4 * conv_n))

    b_ada_sh = lax.dynamic_slice_in_dim(b_ada, k_me * ada_n, ada_n, axis=1)[:, None, :]
    mod_sh = ada_mod(c_all, w_ada, b_ada_sh)
    g2 = small_allgather(mod_sh.reshape(nl * 8, ada_n), name="gather_mod").reshape(4, 2, nl, 8, ada_n)[:, 0]
    mod_me = lax.dynamic_index_in_dim(g2, b_me, axis=2, keepdims=False).transpose(1, 0, 2).reshape(nl, 4 * ada_n)

    wsmall = {n: w[n] for n in REPLICATED}
    wsmall.update(full_small)
    mods = [[mod_me[l:l + 1, i * D:(i + 1) * D] for i in range(6)] for l in range(nl)]
    big = gather_finish(0, started, mod_me)

    cosf, sinf = _rope_tables(seq)
    xs, saved, params = x[0], [], []
    for l in range(nl):
        if l + 1 < nl:
            started = gather_start(shards, l + 1, name=f"gather{l + 1}_start")
            mods[l][1] = mods[l][1] + started[-1][0, 0]
        params.append(layer_params(wsmall, big, l))
        xs, sv = layer_fwd(xs, mods[l], params[l], cosf, sinf)
        saved.append(sv)
        if l + 1 < nl:
            big = gather_finish(l + 1, started, xs)
    loss_part, dx = loss_and_grad(xs, loss_target[0], name="loss")
    loss = lax.psum(loss_part[0, 0], ALL_AXES)
    grads, dmods = [None] * nl, [None] * nl
    idx = jnp.stack([ci, k_me]).astype(jnp.int32)
    totals = {n: lax.empty((nl,) + _shard_shape(n), F32) for n in BIG}

    def finish_group(pending, after, totals, idx):
        group, send_sems, recv_sems, srcs, lands, _ = pending
        sums, from_chips = chip_exchange_wait(send_sems, recv_sems, srcs, lands, after, name=f"rs{group}_chip_exchange_wait")
        totals = {n: chip_add(sums[n], from_chips[n], idx, totals[n], RS_GROUP * group, tensor=n,
                              name=f"rs{group}_chip_add_{n}") for n in BIG}
        return pair_share(totals, RS_GROUP * group, RS_GROUP // 2, name=f"rs{group}_pair_share")

    pending = None
    for group in reversed(range(nl // RS_GROUP)):
        layers = range(RS_GROUP * group, RS_GROUP * (group + 1))
        stacks = {n: lax.empty((RS_GROUP,) + TENSORS[n][2], MMT) for n in BIG if n != "w_br"}
        if pending is not None:
            mods[layers[-1]][5] = mods[layers[-1]][5] + pending[-1][0, 0]
        for l in reversed(layers):
            dx, g, dmods[l], stacks = layer_bwd(dx, mods[l], params[l], saved[l], cosf, sinf, stacks, l - layers[0])
            grads[l] = layer_grads(g)
        stacks["w_br"] = jnp.stack([grads[l]["w_br"].astype(MMT) for l in layers])
        from_sibling = pair_exchange(stacks, name=f"rs{group}_pair_exchange")
        chip_sum = {n: pair_add(stacks[n], from_sibling[n], idx, tensor=n, name=f"rs{group}_pair_add_{n}") for n in BIG}
        if group > 0:
            if pending is not None:
                totals = finish_group(pending, [dx, *chip_sum.values()], totals, idx)
            pending = (group, *chip_exchange_start(chip_sum, name=f"rs{group}_chip_exchange_start"))

    small_names = REPLICATED + ("w_gla_a2", "w_conv")
    small_shapes = [(nl, 6 * D)] + [(nl,) + grads[0][n].shape for n in small_names]
    vec = _flat_rows([jnp.concatenate(dmods, axis=0)] + [jnp.stack([grads[l][n] for l in range(nl)]) for n in small_names])
    gs = small_allgather(vec, name="gather_small_grads")
    earlier = pending
    pending = (0, *chip_exchange_start(chip_sum, name="rs0_chip_exchange_start", after=gs))
    if earlier is not None:
        totals = finish_group(earlier, [dx, pending[-1]], totals, idx)
    gs = (gs + pending[-1][0, 0]).reshape(8, vec.shape[0], HD)
    summed = _unflat(sum_devices(gs), small_shapes)
    grad = dict(zip(small_names, summed[1:]))
    grad["b_ada"] = summed[0]
    grad["w_gla_a2"] = lax.dynamic_slice_in_dim(grad["w_gla_a2"], k_me * a2_n, a2_n, axis=2)
    grad["w_conv"] = lax.dynamic_slice_in_dim(grad["w_conv"], k_me * conv_n, conv_n, axis=2)
    dmod_all = gs[:, :nl * 6 * D // HD].reshape(8, nl, 6 * D)
    dmod_sh = lax.dynamic_slice_in_dim(dmod_all, k_me * ada_n, ada_n, axis=2).transpose(1, 0, 2)
    grad["w_ada"] = ada_dw(c_all, dmod_sh)

    delta, new_m, new_v = {}, {}, {}
    delta["w_ada"], new_m["w_ada"], new_v["w_ada"] = adamw(w["w_ada"], grad["w_ada"], m["w_ada"], v["w_ada"],
                                                          block=ADAM_BLOCKS["w_ada"], name="adamw_w_ada")
    rest = [n for n in WEIGHTS if n not in ADAM_BLOCKS]
    shapes = [w[n].shape for n in rest]
    flat = [_flat_rows([t[n] for n in rest]) for t in (w, grad, m, v)]
    outs = adamw(*flat, block=flat[0].shape, name="adamw_small")
    for t, o in zip((delta, new_m, new_v), outs):
        t.update(zip(rest, _unflat(o, shapes)))

    later = {n: adamw(w[n], totals[n], m[n], v[n], block=ADAM_BLOCKS[n], name="adamw_later_" + n, rows=(RS_GROUP, nl))
             for n in BIG} if nl > RS_GROUP else {}
    done_first = [outs[0], delta["w_ada"]] + [later[n][0] for n in later]
    grad.update(finish_group(pending, done_first, totals, idx))
    for n in BIG:
        delta[n], new_m[n], new_v[n] = adamw(w[n], grad[n], m[n], v[n], block=ADAM_BLOCKS[n], name="adamw_" + n,
                                             rows=(0, min(RS_GROUP, nl)), into=later.get(n))

    return (loss, dx[None], *[grad[n] for n in WEIGHTS], *[delta[n] for n in WEIGHTS], *[new_m[n] for n in WEIGHTS],
            *[new_v[n] for n in WEIGHTS])
```

```python
import functools

import numpy as np
import jax
import jax.numpy as jnp
from jax import lax
from jax.experimental import pallas as pl
from jax.experimental.pallas import tpu as pltpu

F32 = jnp.float32
MMT = jnp.bfloat16
HI = lax.Precision.HIGHEST

D = 1024
DEPTH = 4
NH = 4
HD = 128
BW = NH * HD
CH = 64
GDK = 64
GLR = 16
DFF = 2816
EPS = 1e-6
ROPE_BASE = 10000.0

GP, RG, GG, FV, RQ, RK, RV, GQ, GK, GV, FQ, FK, LR, FF = (
    0, 3072, 3584, 4096, 4608, 5120, 5632, 6144, 6400, 6656, 7168, 7680, 8192, 8320)
NZZ = 8448
WZ0 = 3072
IN_W = 5140
W_IN_COLS = dict(rqkv=(0, 1536), rg=(1536, 2048), gqkv=(2048, 3072), lr=(3072, 3088), gg=(3088, 3600), fqk=(3600, 4624),
                 fv=(4624, 5136), ff=(5136, 5140))

VMEM_LIMIT = 56 * 1024 * 1024

ADAM_LR, ADAM_B1, ADAM_B2, ADAM_EPS, ADAM_WD, ADAM_STEP = 0.001, 0.9, 0.999, 1e-08, 0.01, 10


def _cparams(sem=None):
    return pltpu.CompilerParams(dimension_semantics=sem, vmem_limit_bytes=VMEM_LIMIT)


def _sds(shape, dtype):
    return jax.ShapeDtypeStruct(tuple(shape), dtype)


def _dot(a, b, precision=None):
    return lax.dot_general(a, b, (((1,), (0,)), ((), ())), precision=precision, preferred_element_type=F32)


def _dot_nt(a, b, precision=None):
    return lax.dot_general(a, b, (((1,), (1,)), ((), ())), precision=precision, preferred_element_type=F32)


def _dot_tn(a, b, precision=None):
    return lax.dot_general(a, b, (((0,), (0,)), ((), ())), precision=precision, preferred_element_type=F32)


def _silu(x):
    return x * jax.nn.sigmoid(x)


def _log_sigmoid(x):
    return jnp.minimum(x, 0.0) - jnp.log(1.0 + jnp.exp(jnp.minimum(x, -x)))


@jax.custom_vjp
def _swap_halves(x):
    return pltpu.roll(x, HD // 2, 1)


_swap_halves.defvjp(lambda x: (_swap_halves(x), None), lambda _, g: (_swap_halves(g),))


@jax.custom_vjp
def _bdot(a, b):
    return _dot(a.astype(MMT), b.astype(MMT))


@jax.custom_vjp
def _bdot_nt(a, b):
    return _dot_nt(a.astype(MMT), b.astype(MMT))


@jax.custom_vjp
def _bdot_tn(a, b):
    return _dot_tn(a.astype(MMT), b.astype(MMT))


_bdot.defvjp(lambda a, b: (_bdot(a, b), (a, b)), lambda r, g: (_bdot_nt(g, r[1]), _bdot_tn(r[0], g)))
_bdot_nt.defvjp(lambda a, b: (_bdot_nt(a, b), (a, b)), lambda r, g: (_bdot(g, r[1]), _bdot_tn(g, r[0])))
_bdot_tn.defvjp(lambda a, b: (_bdot_tn(a, b), (a, b)), lambda r, g: (_bdot_nt(r[1], g), _bdot(r[0], g)))


def _stacked(blk, idx, layer):
    if layer is None:
        return pl.BlockSpec(blk, idx)
    return pl.BlockSpec((None,) + blk, lambda i, j: (layer,) + idx(i, j))


def mm_nn(a, b, *, tm, tn, out_dtype, name, layer=None):
    m, k = a.shape
    n = b.shape[-1]

    def body(a_ref, b_ref, o_ref):
        o_ref[...] = _dot(a_ref[...], b_ref[...]).astype(o_ref.dtype)

    return pl.pallas_call(
        body, grid=(m // tm, n // tn),
        in_specs=[pl.BlockSpec((tm, k), lambda i, j: (i, 0)), _stacked((k, tn), lambda i, j: (0, j), layer)],
        out_specs=pl.BlockSpec((tm, tn), lambda i, j: (i, j)),
        out_shape=_sds((m, n), out_dtype), compiler_params=_cparams(("parallel", "parallel")), name=name)(a, b)


def mm_nn_residual(a, b, res, gate, *, tm, tn, name, layer=None):
    m, k = a.shape
    n = b.shape[-1]

    def body(a_ref, b_ref, r_ref, g_ref, x_ref, y_ref):
        acc = _dot(a_ref[...], b_ref[...])
        y_ref[...] = acc
        x_ref[...] = r_ref[...] + g_ref[...] * acc

    return pl.pallas_call(
        body, grid=(m // tm, n // tn),
        in_specs=[pl.BlockSpec((tm, k), lambda i, j: (i, 0)), _stacked((k, tn), lambda i, j: (0, j), layer),
                  pl.BlockSpec((tm, tn), lambda i, j: (i, j)), pl.BlockSpec((1, tn), lambda i, j: (0, j))],
        out_specs=[pl.BlockSpec((tm, tn), lambda i, j: (i, j)), pl.BlockSpec((tm, tn), lambda i, j: (i, j))],
        out_shape=[_sds((m, n), F32), _sds((m, n), F32)],
        compiler_params=_cparams(("parallel", "parallel")), name=name)(a, b, res, gate)


def mm_nt(a, b, *, tm, tn, out_dtype, name, layer=None):
    m, k = a.shape
    n = b.shape[-2]

    def body(a_ref, b_ref, o_ref):
        o_ref[...] = _dot_nt(a_ref[...], b_ref[...]).astype(o_ref.dtype)

    return pl.pallas_call(
        body, grid=(m // tm, n // tn),
        in_specs=[pl.BlockSpec((tm, k), lambda i, j: (i, 0)), _stacked((tn, k), lambda i, j: (j, 0), layer)],
        out_specs=pl.BlockSpec((tm, tn), lambda i, j: (i, j)),
        out_shape=_sds((m, n), out_dtype), compiler_params=_cparams(("parallel", "parallel")), name=name)(a, b)


def mm_nt2(a1, a2, b, *, tm, tn, name, layer):
    m, k1 = a1.shape
    k2 = a2.shape[1]
    n = b.shape[-2]

    def body(a1_ref, a2_ref, b_ref, o_ref):
        o_ref[...] = _dot_nt(a1_ref[...], b_ref[:, :k1]) + _dot_nt(a2_ref[...], b_ref[:, k1:])

    return pl.pallas_call(
        body, grid=(m // tm, n // tn),
        in_specs=[pl.BlockSpec((tm, k1), lambda i, j: (i, 0)), pl.BlockSpec((tm, k2), lambda i, j: (i, 0)),
                  _stacked((tn, k1 + k2), lambda i, j: (j, 0), layer)],
        out_specs=pl.BlockSpec((tm, tn), lambda i, j: (i, j)),
        out_shape=_sds((m, n), F32), compiler_params=_cparams(("parallel", "parallel")), name=name)(a1, a2, b)


def mm_tn(a, b, *, tm, tn, out_dtype, name, col0=0, ncols=None, stack=None, layer=None, out_col0=0):
    s, m = a.shape
    n = b.shape[1] - col0 if ncols is None else ncols
    c0, oc0 = col0 // tn, out_col0 // tn

    def body(a_ref, b_ref, *rest):
        o_ref = rest[-1]
        o_ref[...] = _dot_tn(a_ref[...], b_ref[...]).astype(o_ref.dtype)

    in_specs = [pl.BlockSpec((s, tm), lambda i, j: (0, i)), pl.BlockSpec((s, tn), lambda i, j: (0, c0 + j))]
    if stack is None:
        return pl.pallas_call(
            body, grid=(m // tm, n // tn), in_specs=in_specs, out_specs=pl.BlockSpec((tm, tn), lambda i, j: (i, j)),
            out_shape=_sds((m, n), out_dtype), compiler_params=_cparams(("parallel", "parallel")), name=name)(a, b)
    return pl.pallas_call(
        body, grid=(m // tm, n // tn), in_specs=in_specs + [pl.BlockSpec(memory_space=pl.ANY)],
        out_specs=pl.BlockSpec((None, tm, tn), lambda i, j: (layer, i, oc0 + j)),
        out_shape=_sds(stack.shape, stack.dtype), input_output_aliases={2: 0},
        compiler_params=_cparams(("parallel", "parallel")), name=name)(a, b, stack)


def _row_tile(s):
    return min(256, s)


def _norm_mod_f(x, g, scale, shift):
    r = lax.rsqrt(jnp.mean(x * x, axis=-1, keepdims=True) + EPS)
    return (x * r * g) * (1.0 + scale) + shift


def norm_mod(x, g, scale, shift, *, name):
    s = x.shape[0]
    t = _row_tile(s)

    def body(x_ref, g_ref, sc_ref, sh_ref, o_ref):
        o_ref[...] = _norm_mod_f(x_ref[...], g_ref[...], sc_ref[...], sh_ref[...]).astype(o_ref.dtype)

    vec = pl.BlockSpec((1, D), lambda i: (0, 0))
    return pl.pallas_call(
        body, grid=(s // t,), in_specs=[pl.BlockSpec((t, D), lambda i: (i, 0)), vec, vec, vec],
        out_specs=pl.BlockSpec((t, D), lambda i: (i, 0)), out_shape=_sds((s, D), MMT),
        compiler_params=_cparams(("parallel",)), name=name)(x, g, scale, shift)


def norm_mod_bwd(x, dh, dres, g, scale, shift, *, name):
    s = x.shape[0]
    t = _row_tile(s)

    def body(x_ref, dh_ref, dr_ref, g_ref, sc_ref, sh_ref, dx_ref, dg_ref, dsc_ref, dsh_ref):
        @pl.when(pl.program_id(0) == 0)
        def _():
            dg_ref[...] = jnp.zeros_like(dg_ref)
            dsc_ref[...] = jnp.zeros_like(dsc_ref)
            dsh_ref[...] = jnp.zeros_like(dsh_ref)

        _, vjp = jax.vjp(_norm_mod_f, x_ref[...], g_ref[...], sc_ref[...], sh_ref[...])
        dx, dg, dsc, dsh = vjp(dh_ref[...])
        dx_ref[...] = dr_ref[...] + dx
        dg_ref[...] += dg
        dsc_ref[...] += dsc
        dsh_ref[...] += dsh

    row = pl.BlockSpec((t, D), lambda i: (i, 0))
    vec = pl.BlockSpec((1, D), lambda i: (0, 0))
    return pl.pallas_call(
        body, grid=(s // t,), in_specs=[row, row, row, vec, vec, vec], out_specs=[row, vec, vec, vec],
        out_shape=[_sds((s, D), F32)] + [_sds((1, D), F32)] * 3,
        compiler_params=_cparams(("arbitrary",)), name=name)(x, dh, dres, g, scale, shift)


def gate_bwd(dx, y, gate, *, name):
    s = dx.shape[0]
    t = _row_tile(s)

    def body(dx_ref, y_ref, g_ref, dy_ref, dg_ref):
        @pl.when(pl.program_id(0) == 0)
        def _():
            dg_ref[...] = jnp.zeros_like(dg_ref)

        dxv = dx_ref[...]
        dy_ref[...] = (g_ref[...] * dxv).astype(dy_ref.dtype)
        dg_ref[...] += jnp.sum(dxv * y_ref[...], axis=0, keepdims=True)

    row = pl.BlockSpec((t, D), lambda i: (i, 0))
    vec = pl.BlockSpec((1, D), lambda i: (0, 0))
    return pl.pallas_call(
        body, grid=(s // t,), in_specs=[row, row, vec], out_specs=[row, vec],
        out_shape=[_sds((s, D), MMT), _sds((1, D), F32)],
        compiler_params=_cparams(("arbitrary",)), name=name)(dx, y, gate)


def loss_and_grad(xf, target, *, name):
    s = xf.shape[0]
    t = _row_tile(s)

    def body(x_ref, t_ref, l_ref, dx_ref):
        @pl.when(pl.program_id(0) == 0)
        def _():
            l_ref[...] = jnp.zeros_like(l_ref)

        e = x_ref[...] - t_ref[...]
        dx_ref[...] = e * (1.0 / D)
        l_ref[...] += 0.5 * jnp.sum(jnp.sum(e * e, axis=1, keepdims=True), axis=0, keepdims=True) * (1.0 / D)

    row = pl.BlockSpec((t, D), lambda i: (i, 0))
    return pl.pallas_call(
        body, grid=(s // t,), in_specs=[row, row], out_specs=[pl.BlockSpec((1, 1), lambda i: (0, 0)), row],
        out_shape=[_sds((1, 1), F32), _sds((s, D), F32)],
        compiler_params=_cparams(("arbitrary",)), name=name)(xf, target)


def _ret_consts():
    log_g = np.log1p(-np.exp2(-5.0 - np.arange(NH, dtype=np.float32))).astype(np.float32)
    idx = np.arange(CH, dtype=np.float32)
    d_intra = np.exp(np.abs(idx[:, None] - idx[None, :])[None] * log_g[:, None, None]).astype(np.float32)
    k_w = np.exp((CH - 1.0 - idx)[None, :] * log_g[:, None]).astype(np.float32)
    q_w = np.exp((idx + 1.0)[None, :] * log_g[:, None]).astype(np.float32)
    g_chunk = [float(v) for v in np.exp(np.float32(CH) * log_g).astype(np.float32)]
    bc = lambda a: np.ascontiguousarray(np.broadcast_to(a[:, :, None], (NH, CH, HD)))
    return jnp.asarray(d_intra), jnp.asarray(bc(k_w)), jnp.asarray(bc(q_w)), g_chunk


def _rope_tables(s):
    half = HD // 2
    inv_freq = (ROPE_BASE ** (-np.arange(half, dtype=np.float64) / half)).astype(np.float32)
    ang = (np.arange(s, dtype=np.float32)[:, None] * inv_freq[None, :]).astype(np.float64)
    cos, sin = np.cos(ang).astype(np.float32), np.sin(ang).astype(np.float32)
    return jnp.asarray(np.concatenate([cos, cos], axis=1)), jnp.asarray(np.concatenate([-sin, sin], axis=1))


def _ret_chunk(qs, ks, vs, rs, cos, sin, dintra, kw, qw, g_chunk):
    outs, rn = [], []
    for h in range(NH):
        q = qs[h] * cos + _swap_halves(qs[h]) * sin
        k = (ks[h] * cos + _swap_halves(ks[h]) * sin) * (HD ** -0.5)
        sc = _bdot_nt(q, k) * dintra[h]
        outs.append(_bdot(sc, vs[h]) + _bdot(q * qw[h], rs[h]))
        rn.append(g_chunk[h] * rs[h] + _bdot_tn(k * kw[h], vs[h]))
    return outs, rn


def _heads(x):
    return [x[:, h * HD:(h + 1) * HD] for h in range(NH)]


def _chunks_per_step(n):
    return 4 if n % 4 == 0 else 1


def retention_fwd(zz, cosf, sinf, *, name):
    s = zz.shape[0]
    n = s // CH
    nb = _chunks_per_step(n)
    rb = nb * CH
    dintra, kw, qw, g_chunk = _ret_consts()

    def body(q_ref, k_ref, v_ref, c_ref, s_ref, di_ref, kw_ref, qw_ref, o_ref, rp_ref, r_scr):
        @pl.when(pl.program_id(0) == 0)
        def _():
            r_scr[...] = jnp.zeros_like(r_scr)

        r = r_scr[...]
        consts = ([di_ref[h] for h in range(NH)], [kw_ref[h] for h in range(NH)], [qw_ref[h] for h in range(NH)], g_chunk)
        for i in range(nb):
            rows = slice(i * CH, (i + 1) * CH)
            rp_ref[i] = r
            outs, rn = _ret_chunk(_heads(q_ref[rows, :]), _heads(k_ref[rows, :]), _heads(v_ref[rows, :]),
                                  [r[h * HD:(h + 1) * HD] for h in range(NH)], c_ref[rows, :], s_ref[rows, :], *consts)
            o_ref[rows, :] = jnp.concatenate(outs, axis=1)
            r = jnp.concatenate(rn, axis=0)
        r_scr[...] = r

    col = lambda c: pl.BlockSpec((rb, BW), lambda i: (i, c // BW))
    tab = pl.BlockSpec((rb, HD), lambda i: (i, 0))
    cst = lambda shp: pl.BlockSpec(shp, lambda i: (0,) * len(shp))
    return pl.pallas_call(
        body, grid=(n // nb,),
        in_specs=[col(RQ), col(RK), col(RV), tab, tab, cst((NH, CH, CH)), cst((NH, CH, HD)), cst((NH, CH, HD))],
        out_specs=[pl.BlockSpec((rb, BW), lambda i: (i, 0)), pl.BlockSpec((nb, BW, HD), lambda i: (i, 0, 0))],
        out_shape=[_sds((s, BW), F32), _sds((n, BW, HD), F32)],
        scratch_shapes=[pltpu.VMEM((BW, HD), F32)],
        compiler_params=_cparams(("arbitrary",)), name=name)(zz, zz, zz, cosf, sinf, dintra, kw, qw)


def retention_bwd(zz, cosf, sinf, rprev, do, dzz, *, name):
    s = zz.shape[0]
    n = s // CH
    nb = _chunks_per_step(n)
    rb, steps = nb * CH, n // nb
    dintra, kw, qw, g_chunk = _ret_consts()

    def body(q_ref, k_ref, v_ref, c_ref, s_ref, di_ref, kw_ref, qw_ref, rp_ref, do_ref, dzz_ref, dz_ref, dr_scr):
        @pl.when(pl.program_id(0) == 0)
        def _():
            dr_scr[...] = jnp.zeros_like(dr_scr)

        dr = dr_scr[...]
        consts = dict(dintra=[di_ref[h] for h in range(NH)], kw=[kw_ref[h] for h in range(NH)],
                      qw=[qw_ref[h] for h in range(NH)], g_chunk=g_chunk)
        for i in reversed(range(nb)):
            rows = slice(i * CH, (i + 1) * CH)
            rprev_v = rp_ref[i]
            f = functools.partial(_ret_chunk, cos=c_ref[rows, :], sin=s_ref[rows, :], **consts)
            _, vjp = jax.vjp(f, _heads(q_ref[rows, :]), _heads(k_ref[rows, :]), _heads(v_ref[rows, :]),
                             [rprev_v[h * HD:(h + 1) * HD] for h in range(NH)])
            dq, dk, dv, drp = vjp((_heads(do_ref[rows, :]), [dr[h * HD:(h + 1) * HD] for h in range(NH)]))
            dz_ref[rows, :] = jnp.concatenate(dq + dk + dv, axis=1).astype(dz_ref.dtype)
            dr = jnp.concatenate(drp, axis=0)
        dr_scr[...] = dr

    col = lambda c: pl.BlockSpec((rb, BW), lambda i: (steps - 1 - i, c // BW))
    tab = pl.BlockSpec((rb, HD), lambda i: (steps - 1 - i, 0))
    cst = lambda shp: pl.BlockSpec(shp, lambda i: (0,) * len(shp))
    return pl.pallas_call(
        body, grid=(steps,),
        in_specs=[col(RQ), col(RK), col(RV), tab, tab, cst((NH, CH, CH)), cst((NH, CH, HD)), cst((NH, CH, HD)),
                  pl.BlockSpec((nb, BW, HD), lambda i: (steps - 1 - i, 0, 0)),
                  pl.BlockSpec((rb, BW), lambda i: (steps - 1 - i, 0)), pl.BlockSpec(memory_space=pl.ANY)],
        out_specs=pl.BlockSpec((rb, 3 * BW), lambda i: (steps - 1 - i, RQ // (3 * BW))),
        out_shape=_sds(dzz.shape, dzz.dtype), input_output_aliases={10: 0},
        scratch_shapes=[pltpu.VMEM((BW, HD), F32)],
        compiler_params=_cparams(("arbitrary",)), name=name)(zz, zz, zz, cosf, sinf, dintra, kw, qw, rprev, do, dzz)


GKW = NH * GDK


def _gla_consts():
    tri = np.tril(np.ones((CH, CH), np.float32))
    mask_t = np.zeros((BW, GKW), np.float32)
    for h in range(NH):
        mask_t[h * HD:(h + 1) * HD, h * GDK:(h + 1) * GDK] = 1.0
    return jnp.asarray(tri), jnp.asarray(mask_t)


def _gla_chunk(q, k, v, glr, w_a2, b_a, st, tri, mask_t):
    la = _log_sigmoid(_bdot(glr, w_a2) + b_a) * (1.0 / 16.0)
    bc = _dot(tri, la, HI)
    be = jnp.sum(la, axis=0, keepdims=True)
    kv_t = _bdot_tn(v, k * jnp.exp(be - bc)) * mask_t
    sn = jnp.exp(be) * st + kv_t
    return _bdot_nt(q * (GDK ** -0.5), sn), sn


def gla_fwd(zz, w_a2p, b_a, *, name):
    s = zz.shape[0]
    n = s // CH
    nb = _chunks_per_step(n)
    rb = nb * CH
    tri, mask_t = _gla_consts()

    def body(q_ref, k_ref, v_ref, lr_ref, w_ref, b_ref, tri_ref, m_ref, o_ref, sp_ref, st_scr):
        @pl.when(pl.program_id(0) == 0)
        def _():
            st_scr[...] = jnp.zeros_like(st_scr)

        st = st_scr[...]
        for i in range(nb):
            rows = slice(i * CH, (i + 1) * CH)
            sp_ref[i] = st
            o_ref[rows, :], st = _gla_chunk(q_ref[rows, :], k_ref[rows, :], v_ref[rows, :], lr_ref[rows, :], w_ref[...],
                                            b_ref[...], st, tri_ref[...], m_ref[...])
        st_scr[...] = st

    cst = lambda shp: pl.BlockSpec(shp, lambda i: (0,) * len(shp))
    return pl.pallas_call(
        body, grid=(n // nb,),
        in_specs=[pl.BlockSpec((rb, GKW), lambda i: (i, GQ // GKW)), pl.BlockSpec((rb, GKW), lambda i: (i, GK // GKW)),
                  pl.BlockSpec((rb, BW), lambda i: (i, GV // BW)), pl.BlockSpec((rb, HD), lambda i: (i, LR // HD)),
                  cst((HD, GKW)), cst((1, GKW)), cst((CH, CH)), cst((BW, GKW))],
        out_specs=[pl.BlockSpec((rb, BW), lambda i: (i, 0)), pl.BlockSpec((nb, BW, GKW), lambda i: (i, 0, 0))],
        out_shape=[_sds((s, BW), F32), _sds((n, BW, GKW), F32)],
        scratch_shapes=[pltpu.VMEM((BW, GKW), F32)],
        compiler_params=_cparams(("arbitrary",)), name=name)(zz, zz, zz, zz, w_a2p, b_a, tri, mask_t)


def gla_bwd(zz, w_a2p, b_a, sprev, do, dzz, *, name):
    s = zz.shape[0]
    n = s // CH
    nb = _chunks_per_step(n)
    rb, steps = nb * CH, n // nb
    tri, mask_t = _gla_consts()

    def body(q_ref, k_ref, v_ref, lr_ref, w_ref, b_ref, tri_ref, m_ref, sp_ref, do_ref, dzz_ref,
             dz_ref, dlr_ref, dw_ref, db_ref, ds_scr):
        @pl.when(pl.program_id(0) == 0)
        def _():
            ds_scr[...] = jnp.zeros_like(ds_scr)
            dw_ref[...] = jnp.zeros_like(dw_ref)
            db_ref[...] = jnp.zeros_like(db_ref)

        f = functools.partial(_gla_chunk, tri=tri_ref[...], mask_t=m_ref[...])
        ds, dw_sum, db_sum = ds_scr[...], jnp.zeros(dw_ref.shape, F32), jnp.zeros(db_ref.shape, F32)
        for i in reversed(range(nb)):
            rows = slice(i * CH, (i + 1) * CH)
            _, vjp = jax.vjp(f, q_ref[rows, :], k_ref[rows, :], v_ref[rows, :], lr_ref[rows, :], w_ref[...], b_ref[...],
                             sp_ref[i])
            dq, dk, dv, dlr, dw, db, ds = vjp((do_ref[rows, :], ds))
            dz_ref[rows, :] = jnp.concatenate([dq, dk, dv], axis=1).astype(dz_ref.dtype)
            dlr_ref[rows, :] = dlr.astype(dlr_ref.dtype)
            dw_sum, db_sum = dw_sum + dw, db_sum + db
        dw_ref[...] += dw_sum
        db_ref[...] += db_sum
        ds_scr[...] = ds

    cst = lambda shp: pl.BlockSpec(shp, lambda i: (0,) * len(shp))
    r = lambda i: steps - 1 - i
    return pl.pallas_call(
        body, grid=(steps,),
        in_specs=[pl.BlockSpec((rb, GKW), lambda i: (r(i), GQ // GKW)), pl.BlockSpec((rb, GKW), lambda i: (r(i), GK // GKW)),
                  pl.BlockSpec((rb, BW), lambda i: (r(i), GV // BW)), pl.BlockSpec((rb, HD), lambda i: (r(i), LR // HD)),
                  cst((HD, GKW)), cst((1, GKW)), cst((CH, CH)), cst((BW, GKW)),
                  pl.BlockSpec((nb, BW, GKW), lambda i: (r(i), 0, 0)), pl.BlockSpec((rb, BW), lambda i: (r(i), 0)),
                  pl.BlockSpec(memory_space=pl.ANY)],
        out_specs=[pl.BlockSpec((rb, 2 * GKW + BW), lambda i: (r(i), GQ // (2 * GKW + BW))),
                   pl.BlockSpec((rb, HD), lambda i: (r(i), 0)), cst((HD, GKW)), cst((1, GKW))],
        out_shape=[_sds(dzz.shape, dzz.dtype), _sds((s, HD), MMT), _sds((HD, GKW), F32), _sds((1, GKW), F32)],
        input_output_aliases={10: 0}, scratch_shapes=[pltpu.VMEM((BW, GKW), F32)],
        compiler_params=_cparams(("arbitrary",)), name=name)(zz, zz, zz, zz, w_a2p, b_a, tri, mask_t, sprev, do, dzz)


def _fox_pre_f(fqs, fks, ff, gq, gk, bf):
    def rms(x, g):
        return x * lax.rsqrt(jnp.mean(x * x, axis=-1, keepdims=True) + EPS) * g

    qn = [rms(x, gq) * (HD ** -0.5) for x in fqs]
    kn = [rms(x, gk) for x in fks]
    return qn, kn, _log_sigmoid(ff + bf)


def fox_pre(zz, gq, gk, bf, *, name):
    s = zz.shape[0]
    t = _row_tile(s)
    tri = jnp.asarray(np.tril(np.ones((t, t), np.float32)))

    def body(q_ref, k_ref, f_ref, gq_ref, gk_ref, b_ref, tri_ref, qn_ref, kn_ref, cum_ref, carry):
        @pl.when(pl.program_id(0) == 0)
        def _():
            carry[...] = jnp.zeros_like(carry)

        qn, kn, lf = _fox_pre_f(_heads(q_ref[...]), _heads(k_ref[...]), f_ref[...], gq_ref[...], gk_ref[...], b_ref[...])
        qn_ref[...] = jnp.concatenate(qn, axis=1).astype(qn_ref.dtype)
        kn_ref[...] = jnp.concatenate(kn, axis=1).astype(kn_ref.dtype)
        cum_ref[...] = _dot(tri_ref[...], lf, HI) + carry[...]
        carry[...] += jnp.sum(lf, axis=0, keepdims=True)

    vec = pl.BlockSpec((1, HD), lambda i: (0, 0))
    return pl.pallas_call(
        body, grid=(s // t,),
        in_specs=[pl.BlockSpec((t, BW), lambda i: (i, FQ // BW)), pl.BlockSpec((t, BW), lambda i: (i, FK // BW)),
                  pl.BlockSpec((t, HD), lambda i: (i, FF // HD)), vec, vec, vec, pl.BlockSpec((t, t), lambda i: (0, 0))],
        out_specs=[pl.BlockSpec((t, BW), lambda i: (i, 0)), pl.BlockSpec((t, BW), lambda i: (i, 0)),
                   pl.BlockSpec((t, HD), lambda i: (i, 0))],
        out_shape=[_sds((s, BW), MMT), _sds((s, BW), MMT), _sds((s, HD), F32)],
        scratch_shapes=[pltpu.VMEM((1, HD), F32)],
        compiler_params=_cparams(("arbitrary",)), name=name)(zz, zz, zz, gq, gk, bf, tri)


def fox_pre_bwd(zz, gq, gk, bf, dqn, dkn, dcum, dzz, *, name):
    s = zz.shape[0]
    t = _row_tile(s)
    nt = s // t
    triu = jnp.asarray(np.triu(np.ones((t, t), np.float32)))

    def body(q_ref, k_ref, f_ref, gq_ref, gk_ref, b_ref, tri_ref, dqn_ref, dkn_ref, dcum_ref, dzz_ref,
             dz_ref, dff_ref, dgq_ref, dgk_ref, db_ref, carry):
        @pl.when(pl.program_id(0) == 0)
        def _():
            carry[...] = jnp.zeros_like(carry)
            dgq_ref[...] = jnp.zeros_like(dgq_ref)
            dgk_ref[...] = jnp.zeros_like(dgk_ref)
            db_ref[...] = jnp.zeros_like(db_ref)

        dcum_v = dcum_ref[...]
        dlf = _dot(tri_ref[...], dcum_v, HI) + carry[...]
        carry[...] += jnp.sum(dcum_v, axis=0, keepdims=True)
        _, vjp = jax.vjp(_fox_pre_f, _heads(q_ref[...]), _heads(k_ref[...]), f_ref[...], gq_ref[...], gk_ref[...], b_ref[...])
        dq, dk, dff, dgq, dgk, db = vjp((_heads(dqn_ref[...]), _heads(dkn_ref[...]), dlf))
        dz_ref[...] = jnp.concatenate(dq + dk, axis=1).astype(dz_ref.dtype)
        dff_ref[...] = dff.astype(dff_ref.dtype)
        dgq_ref[...] += dgq
        dgk_ref[...] += dgk
        db_ref[...] += db

    r = lambda i: nt - 1 - i
    vec = pl.BlockSpec((1, HD), lambda i: (0, 0))
    return pl.pallas_call(
        body, grid=(nt,),
        in_specs=[pl.BlockSpec((t, BW), lambda i: (r(i), FQ // BW)), pl.BlockSpec((t, BW), lambda i: (r(i), FK // BW)),
                  pl.BlockSpec((t, HD), lambda i: (r(i), FF // HD)), vec, vec, vec, pl.BlockSpec((t, t), lambda i: (0, 0)),
                  pl.BlockSpec((t, BW), lambda i: (r(i), 0)), pl.BlockSpec((t, BW), lambda i: (r(i), 0)),
                  pl.BlockSpec((t, HD), lambda i: (r(i), 0)), pl.BlockSpec(memory_space=pl.ANY)],
        out_specs=[pl.BlockSpec((t, 2 * BW), lambda i: (r(i), FQ // (2 * BW))), pl.BlockSpec((t, HD), lambda i: (r(i), 0)),
                   vec, vec, vec],
        out_shape=[_sds(dzz.shape, dzz.dtype), _sds((s, HD), MMT), _sds((1, HD), F32), _sds((1, HD), F32), _sds((1, HD), F32)],
        input_output_aliases={10: 0}, scratch_shapes=[pltpu.VMEM((1, HD), F32)],
        compiler_params=_cparams(("arbitrary",)), name=name)(zz, zz, zz, gq, gk, bf, triu, dqn, dkn, dcum, dzz)


def _fox_blocks(s):
    return min(256, s), min(512, s)


NEG = -1e30


def fox_attn_fwd(qn, kn, zz, cum_col, cum_row, *, name):
    s = qn.shape[0]
    bq, bk = _fox_blocks(s)

    def body(q_ref, k_ref, v_ref, cc_ref, cr_ref, o_ref, lse_ref):
        qi = pl.program_id(1)
        q = q_ref[...]
        cq = cc_ref[...]
        rows = qi * bq + lax.broadcasted_iota(jnp.int32, (bq, bk), 0)
        cols0 = lax.broadcasted_iota(jnp.int32, (bq, bk), 1)

        def step(j, carry, on_diagonal):
            m, l, acc = carry
            off = pl.multiple_of(j * bk, bk)
            k = k_ref[pl.ds(off, bk), :]
            v = v_ref[pl.ds(off, bk), :].astype(MMT)
            sc = _dot_nt(q, k) + cq - cr_ref[pl.ds(j, 1), :]
            if on_diagonal:
                sc = jnp.where(rows >= cols0 + j * bk, sc, NEG)
            m_new = jnp.maximum(m, jnp.max(sc, axis=1, keepdims=True))
            alpha = jnp.exp(m - m_new)
            p = jnp.exp(sc - m_new)
            return m_new, alpha * l + jnp.sum(p, axis=1, keepdims=True), alpha * acc + _dot(p.astype(MMT), v)

        nfull, nk = (qi * bq + 1) // bk, ((qi + 1) * bq + bk - 1) // bk
        carry = (jnp.full((bq, 1), NEG, F32), jnp.zeros((bq, 1), F32), jnp.zeros((bq, HD), F32))
        carry = lax.fori_loop(0, nfull, functools.partial(step, on_diagonal=False), carry)
        m, l, acc = lax.fori_loop(nfull, nk, functools.partial(step, on_diagonal=True), carry)
        o_ref[...] = acc / l
        lse_ref[...] = m + jnp.log(l)

    return pl.pallas_call(
        body, grid=(NH, s // bq),
        in_specs=[pl.BlockSpec((bq, HD), lambda h, i: (i, h)), pl.BlockSpec((s, HD), lambda h, i: (0, h)),
                  pl.BlockSpec((s, HD), lambda h, i: (0, FV // HD + h)),
                  pl.BlockSpec((None, bq, 1), lambda h, i: (h, i, 0)), pl.BlockSpec((None, s // bk, bk), lambda h, i: (h, 0, 0))],
        out_specs=[pl.BlockSpec((bq, HD), lambda h, i: (i, h)), pl.BlockSpec((None, bq, 1), lambda h, i: (h, i, 0))],
        out_shape=[_sds((s, BW), F32), _sds((NH, s, 1), F32)],
        compiler_params=_cparams(("parallel", "parallel")), name=name)(qn, kn, zz, cum_col, cum_row)


def fox_attn_bwd(qn, kn, zz, cum_col, cum_row, lse, do, dzz, *, name):
    s = qn.shape[0]
    bq, bk = _fox_blocks(s)
    nkc = s // bk

    def body(q_ref, k_ref, v_ref, cc_ref, cr_ref, lse_ref, do_ref, dzz_ref, dq_ref, dk_ref, dv_ref, dc_ref,
             p_scr, dp_scr, dv_scr):
        qi = pl.program_id(1)

        @pl.when(qi == 0)
        def _():
            dk_ref[...] = jnp.zeros_like(dk_ref)
            dv_scr[...] = jnp.zeros_like(dv_scr)
            dc_ref[...] = jnp.zeros_like(dc_ref)

        q = q_ref[...]
        dob = do_ref[...].astype(MMT)
        cq = cc_ref[...]
        lse_v = lse_ref[...]
        rows = qi * bq + lax.broadcasted_iota(jnp.int32, (bq, bk), 0)
        cols0 = lax.broadcasted_iota(jnp.int32, (bq, bk), 1)
        nfull, nk = (qi * bq + 1) // bk, ((qi + 1) * bq + bk - 1) // bk

        def probs(j, delta, on_diagonal):
            off = pl.multiple_of(j * bk, bk)
            sc = _dot_nt(q, k_ref[pl.ds(off, bk), :]) + cq - cr_ref[pl.ds(j, 1), :]
            p = jnp.exp(sc - lse_v)
            if on_diagonal:
                p = jnp.where(rows >= cols0 + j * bk, p, 0.0)
            dp = _dot_nt(dob, v_ref[pl.ds(off, bk), :].astype(MMT))
            p_scr[j] = p
            dp_scr[j] = dp
            return delta + jnp.sum(p * dp, axis=1, keepdims=True)

        delta = lax.fori_loop(0, nfull, functools.partial(probs, on_diagonal=False), jnp.zeros((bq, 1), F32))
        delta = lax.fori_loop(nfull, nk, functools.partial(probs, on_diagonal=True), delta)

        def grads(j, dq):
            off = pl.multiple_of(j * bk, bk)
            p = p_scr[j]
            ds = p * (dp_scr[j] - delta)
            dsm = ds.astype(MMT)
            dv_scr[pl.ds(off, bk), :] += _dot_tn(p.astype(MMT), dob)
            dk_ref[pl.ds(off, bk), :] += _dot_tn(dsm, q)
            dc_ref[pl.ds(j, 1), :] -= jnp.sum(ds, axis=0, keepdims=True)
            return dq + _dot(dsm, k_ref[pl.ds(off, bk), :])

        dq_ref[...] = lax.fori_loop(0, nk, grads, jnp.zeros((bq, HD), F32))

        @pl.when(qi == pl.num_programs(1) - 1)
        def _():
            dv_ref[...] = dv_scr[...].astype(dv_ref.dtype)

    full = lambda c0=0: pl.BlockSpec((s, HD), lambda h, i: (0, c0 + h))
    blk = lambda: pl.BlockSpec((bq, HD), lambda h, i: (i, h))
    colv = lambda: pl.BlockSpec((None, bq, 1), lambda h, i: (h, i, 0))
    rowv = lambda: pl.BlockSpec((None, nkc, bk), lambda h, i: (h, 0, 0))
    return pl.pallas_call(
        body, grid=(NH, s // bq),
        in_specs=[blk(), full(), full(FV // HD), colv(), rowv(), colv(), blk(), pl.BlockSpec(memory_space=pl.ANY)],
        out_specs=[blk(), full(), full(FV // HD), rowv()],
        out_shape=[_sds((s, BW), F32), _sds((s, BW), F32), _sds(dzz.shape, dzz.dtype), _sds((NH, nkc, bk), F32)],
        input_output_aliases={7: 2},
        scratch_shapes=[pltpu.VMEM((nkc, bq, bk), F32), pltpu.VMEM((nkc, bq, bk), F32), pltpu.VMEM((s, HD), F32)],
        compiler_params=_cparams(("parallel", "arbitrary")), name=name)(qn, kn, zz, cum_col, cum_row, lse, do, dzz)


def _branch_f(rets, rgs, glas, ggs, ret_g, gla_g):
    out_r, out_g = [], []
    for h in range(NH):
        xc = rets[h] - jnp.mean(rets[h], axis=-1, keepdims=True)
        y = xc * lax.rsqrt(jnp.mean(xc * xc, axis=-1, keepdims=True) + EPS) * ret_g[h]
        out_r.append(_silu(rgs[h]) * y)
        x = glas[h]
        y = x * lax.rsqrt(jnp.mean(x * x, axis=-1, keepdims=True) + EPS) * gla_g
        out_g.append(_silu(ggs[h]) * y)
    return out_r, out_g


def _w_br_spec(layer):
    return pl.BlockSpec((None, 3, BW, D), lambda i: (layer, 0, 0, 0))


def mix_fwd(ret_raw, gla_raw, fox_o, zz, ret_g, gla_g, b_mg, w_br, *, name, layer):
    s = zz.shape[0]
    t = _row_tile(s)

    def body(r_ref, g_ref, f_ref, rg_ref, gg_ref, gp_ref, rgn_ref, ggn_ref, bmg_ref, w_ref, o_ref):
        rgn = rgn_ref[...]
        br_r, br_g = _branch_f(_heads(r_ref[...]), _heads(rg_ref[...]), _heads(g_ref[...]), _heads(gg_ref[...]),
                               _heads(rgn), ggn_ref[...])
        brs = [jnp.concatenate(br_r, axis=1), jnp.concatenate(br_g, axis=1), f_ref[...]]
        acc = jnp.zeros((t, D), F32)
        for b in range(3):
            gate = jax.nn.sigmoid(gp_ref[:, b * D:(b + 1) * D] + bmg_ref[:, b * D:(b + 1) * D])
            acc = acc + gate * _dot(brs[b].astype(MMT), w_ref[b])
        o_ref[...] = acc.astype(o_ref.dtype)

    row = lambda w, c=0: pl.BlockSpec((t, w), lambda i: (i, c // w))
    cst = lambda shp: pl.BlockSpec(shp, lambda i: (0,) * len(shp))
    return pl.pallas_call(
        body, grid=(s // t,),
        in_specs=[row(BW), row(BW), row(BW), row(BW, RG), row(BW, GG), row(3 * D, GP), cst((1, BW)), cst((1, HD)),
                  cst((1, 3 * D)), _w_br_spec(layer)],
        out_specs=row(D), out_shape=_sds((s, D), MMT),
        compiler_params=_cparams(("parallel",)), name=name)(ret_raw, gla_raw, fox_o, zz, zz, zz, ret_g, gla_g, b_mg, w_br)


def mix_bwd(ret_raw, gla_raw, fox_o, zz, ret_g, gla_g, b_mg, w_br, dmi, *, name, layer):
    s = zz.shape[0]
    t = _row_tile(s)

    def body(r_ref, g_ref, f_ref, rg_ref, gg_ref, gp_ref, rgn_ref, ggn_ref, bmg_ref, w_ref, dmi_ref,
             dr_ref, dg_ref, df_ref, dgp_ref, dw_ref, drgn_ref, dggn_ref, dbmg_ref):
        @pl.when(pl.program_id(0) == 0)
        def _():
            dw_ref[...] = jnp.zeros_like(dw_ref)
            drgn_ref[...] = jnp.zeros_like(drgn_ref)
            dggn_ref[...] = jnp.zeros_like(dggn_ref)
            dbmg_ref[...] = jnp.zeros_like(dbmg_ref)

        (br_r, br_g), vjp = jax.vjp(_branch_f, _heads(r_ref[...]), _heads(rg_ref[...]), _heads(g_ref[...]),
                                    _heads(gg_ref[...]), _heads(rgn_ref[...]), ggn_ref[...])
        brs = [jnp.concatenate(br_r, axis=1).astype(MMT), jnp.concatenate(br_g, axis=1).astype(MMT),
               f_ref[...].astype(MMT)]
        dmi_v = dmi_ref[...].astype(F32)
        dbr = []
        for b in range(3):
            w = w_ref[b]
            ybr = _dot(brs[b], w)
            gate = jax.nn.sigmoid(gp_ref[:, b * D:(b + 1) * D] + bmg_ref[:, b * D:(b + 1) * D])
            dgp = dmi_v * ybr * gate * (1.0 - gate)
            dgp_ref[:, b * D:(b + 1) * D] = dgp.astype(dgp_ref.dtype)
            dbmg_ref[:, b * D:(b + 1) * D] += jnp.sum(dgp, axis=0, keepdims=True)
            dy = (dmi_v * gate).astype(MMT)
            dw_ref[b] += _dot_tn(brs[b], dy)
            dbr.append(_dot_nt(dy, w))
        dr, drg, dg, dgg, drgn, dggn = vjp((_heads(dbr[0]), _heads(dbr[1])))
        dr_ref[...] = jnp.concatenate(dr, axis=1)
        dg_ref[...] = jnp.concatenate(dg, axis=1)
        df_ref[...] = dbr[2]
        dgp_ref[:, RG:RG + BW] = jnp.concatenate(drg, axis=1).astype(dgp_ref.dtype)
        dgp_ref[:, GG:GG + BW] = jnp.concatenate(dgg, axis=1).astype(dgp_ref.dtype)
        drgn_ref[...] += jnp.concatenate(drgn, axis=1)
        dggn_ref[...] += dggn

    row = lambda w, c=0: pl.BlockSpec((t, w), lambda i: (i, c // w))
    cst = lambda shp: pl.BlockSpec(shp, lambda i: (0,) * len(shp))
    return pl.pallas_call(
        body, grid=(s // t,),
        in_specs=[row(BW), row(BW), row(BW), row(BW, RG), row(BW, GG), row(3 * D, GP), cst((1, BW)), cst((1, HD)),
                  cst((1, 3 * D)), _w_br_spec(layer), row(D)],
        out_specs=[row(BW), row(BW), row(BW), row(FV), cst((3, BW, D)), cst((1, BW)), cst((1, HD)), cst((1, 3 * D))],
        out_shape=[_sds((s, BW), F32)] * 3 + [_sds((s, NZZ), MMT), _sds((3, BW, D), F32), _sds((1, BW), F32),
                                              _sds((1, HD), F32), _sds((1, 3 * D), F32)],
        compiler_params=_cparams(("arbitrary",)), name=name)(ret_raw, gla_raw, fox_o, zz, zz, zz, ret_g, gla_g, b_mg, w_br, dmi)


CT = 256


def _shift_down(x, k, rows):
    return jnp.where(rows >= k, pltpu.roll(x, k, 0), 0.0)


def _shift_up(x, k, rows, s):
    return jnp.where(rows < s - k, pltpu.roll(x, s - k, 0), 0.0)


def conv_fwd(ug, w_conv, b_conv, *, name):
    s = ug.shape[0]
    nt = DFF // CT

    def body(u_ref, g_ref, w_ref, b_ref, a_ref):
        u = u_ref[...]
        rows = lax.broadcasted_iota(jnp.int32, u.shape, 0)
        uc = b_ref[...] + w_ref[0:1, :] * _shift_down(u, 2, rows) + w_ref[1:2, :] * _shift_down(u, 1, rows) + w_ref[2:3, :] * u
        a_ref[...] = (_silu(uc) * g_ref[...]).astype(a_ref.dtype)

    return pl.pallas_call(
        body, grid=(nt,),
        in_specs=[pl.BlockSpec((s, CT), lambda j: (0, j)), pl.BlockSpec((s, CT), lambda j: (0, nt + j)),
                  pl.BlockSpec((3, CT), lambda j: (0, j)), pl.BlockSpec((1, CT), lambda j: (0, j))],
        out_specs=pl.BlockSpec((s, CT), lambda j: (0, j)), out_shape=_sds((s, DFF), MMT),
        compiler_params=_cparams(("parallel",)), name=name)(ug, ug, w_conv, b_conv)


def conv_bwd(ug, w_conv, b_conv, da, *, name):
    s = ug.shape[0]
    nt = DFF // CT

    def body(u_ref, g_ref, w_ref, b_ref, da_ref, du_ref, dg_ref, dw_ref, db_ref):
        u = u_ref[...]
        rows = lax.broadcasted_iota(jnp.int32, u.shape, 0)
        u2, u1 = _shift_down(u, 2, rows), _shift_down(u, 1, rows)
        uc = b_ref[...] + w_ref[0:1, :] * u2 + w_ref[1:2, :] * u1 + w_ref[2:3, :] * u
        sg = jax.nn.sigmoid(uc)
        da_v = da_ref[...]
        dg_ref[...] = (da_v * uc * sg).astype(dg_ref.dtype)
        duc = da_v * g_ref[...] * sg * (1.0 + uc * (1.0 - sg))
        du = w_ref[2:3, :] * duc + w_ref[1:2, :] * _shift_up(duc, 1, rows, s) + w_ref[0:1, :] * _shift_up(duc, 2, rows, s)
        du_ref[...] = du.astype(du_ref.dtype)
        dw_ref[0:1, :] = jnp.sum(duc * u2, axis=0, keepdims=True)
        dw_ref[1:2, :] = jnp.sum(duc * u1, axis=0, keepdims=True)
        dw_ref[2:3, :] = jnp.sum(duc * u, axis=0, keepdims=True)
        db_ref[...] = jnp.sum(duc, axis=0, keepdims=True)

    col = lambda: pl.BlockSpec((s, CT), lambda j: (0, j))
    return pl.pallas_call(
        body, grid=(nt,),
        in_specs=[col(), pl.BlockSpec((s, CT), lambda j: (0, nt + j)), pl.BlockSpec((3, CT), lambda j: (0, j)),
                  pl.BlockSpec((1, CT), lambda j: (0, j)), col()],
        out_specs=[col(), col(), pl.BlockSpec((3, CT), lambda j: (0, j)), pl.BlockSpec((1, CT), lambda j: (0, j))],
        out_shape=[_sds((s, DFF), MMT), _sds((s, DFF), MMT), _sds((3, DFF), F32), _sds((1, DFF), F32)],
        compiler_params=_cparams(("parallel",)), name=name)(ug, ug, w_conv, b_conv, da)


def place_tail(dzz, dlr, dff, *, name):
    s = dzz.shape[0]
    t = _row_tile(s)

    def body(a_ref, b_ref, z_ref, o_ref):
        o_ref[...] = jnp.concatenate([a_ref[...], b_ref[...]], axis=1)

    spec = pl.BlockSpec((t, HD), lambda i: (i, 0))
    return pl.pallas_call(
        body, grid=(s // t,), in_specs=[spec, spec, pl.BlockSpec(memory_space=pl.ANY)],
        out_specs=pl.BlockSpec((t, 2 * HD), lambda i: (i, LR // (2 * HD))), out_shape=_sds(dzz.shape, dzz.dtype),
        input_output_aliases={2: 0}, compiler_params=_cparams(("parallel",)), name=name)(dlr, dff, dzz)


def _tiles(s):
    return min(1024, s)


def layer_fwd(x, mod, p, cosf, sinf):
    s = x.shape[0]
    tm = _tiles(s)
    l = p["l"]
    shift1, scale1, gate1, shift2, scale2, gate2 = mod
    h = norm_mod(x, p["norm1_g"], scale1, shift1, name="norm_mod")
    zz = mm_nn(h, p["w1"], tm=tm, tn=768, out_dtype=F32, name="mm_w1", layer=l)
    ret_raw, rprev = retention_fwd(zz, cosf, sinf, name="ret_fwd")
    gla_raw, sprev = gla_fwd(zz, p["w_a2p"], p["b_gla_a"], name="gla_fwd")
    qn, kn, cum = fox_pre(zz, p["q_norm_g"], p["k_norm_g"], p["b_foxp"], name="fox_pre")
    bq, bk = _fox_blocks(s)
    cum_t = cum[:, :NH].T
    cum_col, cum_row = cum_t[:, :, None], cum_t.reshape(NH, s // bk, bk)
    fox_o, lse = fox_attn_fwd(qn, kn, zz, cum_col, cum_row, name="fox_fwd")
    mi = mix_fwd(ret_raw, gla_raw, fox_o, zz, p["ret_norm_g"], p["gla_norm_g"], p["b_mg"], p["w_br"], name="mix_fwd",
                 layer=l)
    x1, mixed = mm_nn_residual(mi, p["w_o"], x, gate1, tm=tm, tn=512, name="mm_wo", layer=l)
    h2 = norm_mod(x1, p["norm2_g"], scale2, shift2, name="norm_mod")
    ug = mm_nn(h2, p["w_up"], tm=tm, tn=512, out_dtype=F32, name="mm_wup", layer=l)
    a = conv_fwd(ug, p["w_conv"], p["b_conv"], name="conv_fwd")
    x2, y = mm_nn_residual(a, p["w_down"], x1, gate2, tm=tm, tn=512, name="mm_wdown", layer=l)
    saved = dict(x=x, h=h, zz=zz, ret_raw=ret_raw, rprev=rprev, gla_raw=gla_raw, sprev=sprev, qn=qn, kn=kn,
                 cum_col=cum_col, cum_row=cum_row, fox_o=fox_o, lse=lse, mi=mi, mixed=mixed, x1=x1, h2=h2, ug=ug, a=a, y=y)
    return x2, saved


def layer_bwd(dx2, mod, p, sv, cosf, sinf, stacks, slot):
    s = dx2.shape[0]
    tm = _tiles(s)
    l = p["l"]
    shift1, scale1, gate1, shift2, scale2, gate2 = mod
    g, stacks = {}, dict(stacks)
    dy, dgate2 = gate_bwd(dx2, sv["y"], gate2, name="gate_bwd")
    stacks["w_down"] = mm_tn(sv["a"], dy, tm=min(1408, DFF), tn=512, out_dtype=MMT, name="mm_dwdown",
                             stack=stacks["w_down"], layer=slot)
    da = mm_nt(dy, p["w_down"], tm=tm, tn=1408, out_dtype=F32, name="mm_da", layer=l)
    du, dg, g["w_conv"], g["b_conv"] = conv_bwd(sv["ug"], p["w_conv"], p["b_conv"], da, name="conv_bwd")
    stacks["w_up"] = mm_tn(sv["h2"], du, tm=D, tn=CT, out_dtype=MMT, name="mm_dwup_u", stack=stacks["w_up"], layer=slot)
    stacks["w_up"] = mm_tn(sv["h2"], dg, tm=D, tn=CT, out_dtype=MMT, name="mm_dwup_g", stack=stacks["w_up"], layer=slot,
                           out_col0=DFF)
    dh2 = mm_nt2(du, dg, p["w_up"], tm=min(512, s), tn=D, name="mm_dh2", layer=l)
    dx1, g["norm2_g"], dscale2, dshift2 = norm_mod_bwd(sv["x1"], dh2, dx2, p["norm2_g"], scale2, shift2, name="norm_mod_bwd")
    dmixed, dgate1 = gate_bwd(dx1, sv["mixed"], gate1, name="gate_bwd")
    stacks["w_o"] = mm_tn(sv["mi"], dmixed, tm=512, tn=512, out_dtype=MMT, name="mm_dwo", stack=stacks["w_o"], layer=slot)
    dmi = mm_nt(dmixed, p["w_o"], tm=tm, tn=512, out_dtype=MMT, name="mm_dmi", layer=l)
    zz = sv["zz"]
    (dret, dgla, dfox, dzz, g["w_br"], g["ret_norm_g"], g["gla_norm_g"], g["b_mg"]) = mix_bwd(
        sv["ret_raw"], sv["gla_raw"], sv["fox_o"], zz, p["ret_norm_g"], p["gla_norm_g"], p["b_mg"], p["w_br"], dmi,
        name="mix_bwd", layer=l)
    dqn, dkn, dzz, dcum_row = fox_attn_bwd(sv["qn"], sv["kn"], zz, sv["cum_col"], sv["cum_row"], sv["lse"], dfox, dzz,
                                           name="fox_bwd")
    dcum = jnp.pad(dcum_row.reshape(NH, s).T, ((0, 0), (0, HD - NH)))
    dzz, dff, g["q_norm_g"], g["k_norm_g"], g["b_foxp"] = fox_pre_bwd(
        zz, p["q_norm_g"], p["k_norm_g"], p["b_foxp"], dqn, dkn, dcum, dzz, name="fox_pre_bwd")
    dzz, dlr, g["w_a2p"], g["b_gla_a"] = gla_bwd(zz, p["w_a2p"], p["b_gla_a"], sv["sprev"], dgla, dzz, name="gla_bwd")
    dzz = retention_bwd(zz, cosf, sinf, sv["rprev"], dret, dzz, name="ret_bwd")
    dzz = place_tail(dzz, dlr, dff, name="place_tail")
    stacks["w_mg"] = mm_tn(sv["h"], dzz, tm=512, tn=768, out_dtype=MMT, name="mm_dwmg", ncols=WZ0, stack=stacks["w_mg"],
                           layer=slot)
    dwz = mm_tn(sv["h"], dzz, tm=512, tn=768, out_dtype=MMT, name="mm_dwz", col0=WZ0)
    stacks["w_in"] = unalign_dw_in(dwz, stacks["w_in"], slot)
    dh = mm_nt(dzz, p["w1"], tm=min(256, s), tn=D, out_dtype=F32, name="mm_dh", layer=l)
    dx, g["norm1_g"], dscale1, dshift1 = norm_mod_bwd(sv["x"], dh, dx1, p["norm1_g"], scale1, shift1, name="norm_mod_bwd")
    dmod = jnp.concatenate([dshift1, dscale1, dgate1, dshift2, dscale2, dgate2], axis=1)
    return dx, g, dmod, stacks


def _align_cols(w_in, w_mg):
    z = lambda n: jnp.zeros((w_in.shape[0], n), w_in.dtype)
    seg = lambda name: w_in[:, W_IN_COLS[name][0]:W_IN_COLS[name][1]]
    return jnp.concatenate([w_mg, seg("rg"), seg("gg"), seg("fv"), seg("rqkv"), seg("gqkv"), seg("fqk"), seg("lr"),
                            z(HD - GLR), seg("ff"), z(HD - NH)], axis=1)


def _unalign_cols(dwz):
    seg = lambda c0, name: dwz[:, c0 - WZ0:c0 - WZ0 + W_IN_COLS[name][1] - W_IN_COLS[name][0]]
    return jnp.concatenate([seg(RQ, "rqkv"), seg(RG, "rg"), seg(GQ, "gqkv"), seg(LR, "lr"), seg(GG, "gg"), seg(FQ, "fqk"),
                            seg(FV, "fv"), seg(FF, "ff")], axis=1)


def build_w1(w_in_sh, w_mg):
    nl = w_mg.shape[0]
    t = _row_tile(D)

    def body(s_ref, g_ref, o_ref):
        o_ref[...] = _align_cols(jnp.concatenate([s_ref[k] for k in range(4)], axis=1), g_ref[...])

    return pl.pallas_call(
        body, grid=(nl, D // t),
        in_specs=[pl.BlockSpec((None, 4, t, IN_W // 4), lambda l, i: (l, 0, i, 0)), pl.BlockSpec((None, t, WZ0), lambda l, i: (l, i, 0))],
        out_specs=pl.BlockSpec((None, t, NZZ), lambda l, i: (l, i, 0)), out_shape=_sds((nl, D, NZZ), w_mg.dtype),
        compiler_params=_cparams(("parallel", "parallel")), name="build_w1")(w_in_sh, w_mg)


def unalign_dw_in(dwz, stack, layer):
    t = _row_tile(D)

    def body(z_ref, s_ref, o_ref):
        w = _unalign_cols(z_ref[...])
        for k in range(4):
            o_ref[k] = w[:, k * (IN_W // 4):(k + 1) * (IN_W // 4)]

    return pl.pallas_call(
        body, grid=(D // t,),
        in_specs=[pl.BlockSpec((t, NZZ - WZ0), lambda i: (i, 0)), pl.BlockSpec(memory_space=pl.ANY)],
        out_specs=pl.BlockSpec((None, 4, t, IN_W // 4), lambda i: (layer, 0, i, 0)), out_shape=_sds(stack.shape, stack.dtype),
        input_output_aliases={1: 0}, compiler_params=_cparams(("parallel",)), name="unalign_dw_in")(dwz, stack)


def layer_params(w, big, l):
    row = lambda v: v[l][None, :]
    return dict(
        l=0, norm1_g=row(w["norm1_g"]), norm2_g=row(w["norm2_g"]), w1=big["w1"],
        w_a2p=jnp.pad(w["w_gla_a2"][l], ((0, HD - GLR), (0, 0))), b_gla_a=row(w["b_gla_a"]),
        b_foxp=jnp.pad(row(w["b_fox_f"]), ((0, 0), (0, HD - NH))), ret_norm_g=row(w["ret_norm_g"]),
        gla_norm_g=row(w["gla_norm_g"]), q_norm_g=row(w["q_norm_g"]), k_norm_g=row(w["k_norm_g"]),
        w_br=big["w_br"], b_mg=row(w["b_mg"]), w_o=big["w_o"], w_up=big["w_up"], w_conv=w["w_conv"][l],
        b_conv=row(w["b_conv"]), w_down=big["w_down"])


def layer_grads(g):
    vec = lambda v: v[0]
    return dict(
        norm1_g=vec(g["norm1_g"]), norm2_g=vec(g["norm2_g"]), w_gla_a2=g["w_a2p"][:GLR], b_gla_a=vec(g["b_gla_a"]),
        b_fox_f=g["b_foxp"][0, :NH], ret_norm_g=vec(g["ret_norm_g"]), gla_norm_g=vec(g["gla_norm_g"]),
        q_norm_g=vec(g["q_norm_g"]), k_norm_g=vec(g["k_norm_g"]), w_br=g["w_br"], b_mg=vec(g["b_mg"]),
        w_conv=g["w_conv"], b_conv=vec(g["b_conv"]))


def ada_mod(c_all, w_ada, b_ada):
    nl, _, n = w_ada.shape

    def body(c_ref, w_ref, b_ref, o_ref):
        o_ref[...] = _dot(_silu(c_ref[...]), w_ref[...], HI) + b_ref[...]

    return pl.pallas_call(
        body, grid=(nl,),
        in_specs=[pl.BlockSpec((8, D), lambda l: (0, 0)), pl.BlockSpec((None, D, n), lambda l: (l, 0, 0)),
                  pl.BlockSpec((None, 1, n), lambda l: (l, 0, 0))],
        out_specs=pl.BlockSpec((None, 8, n), lambda l: (l, 0, 0)), out_shape=_sds((nl, 8, n), F32),
        compiler_params=_cparams(("parallel",)), name="ada_mod")(c_all, w_ada, b_ada)


def ada_dw(c_all, dmod):
    nl, _, n = dmod.shape

    def body(c_ref, d_ref, o_ref):
        o_ref[...] = _dot_tn(_silu(c_ref[...]), d_ref[...], HI)

    return pl.pallas_call(
        body, grid=(nl,),
        in_specs=[pl.BlockSpec((8, D), lambda l: (0, 0)), pl.BlockSpec((None, 8, n), lambda l: (l, 0, 0))],
        out_specs=pl.BlockSpec((None, D, n), lambda l: (l, 0, 0)), out_shape=_sds((nl, D, n), F32),
        compiler_params=_cparams(("parallel",)), name="ada_dw")(c_all, dmod)


def sum_devices(g):
    def body(g_ref, o_ref):
        acc = g_ref[0]
        for d in range(1, 8):
            acc = acc + g_ref[d]
        o_ref[...] = acc

    return pl.pallas_call(body, out_shape=_sds(g.shape[1:], F32), name="sum_devices")(g)


def adamw(w, g, m, v, *, block, name, rows=None, into=None):
    nd = w.ndim
    lo, hi = (0, w.shape[0]) if rows is None else rows
    grid = ((hi - lo) // block[0],) + tuple(w.shape[i] // block[i] for i in range(1, nd))
    first = lo // block[0]
    bc1 = 1.0 - ADAM_B1 ** ADAM_STEP
    bc2 = 1.0 - ADAM_B2 ** ADAM_STEP

    def body(w_ref, g_ref, m_ref, v_ref, *rest):
        d_ref, nm_ref, nv_ref = rest[-3:]
        gv = g_ref[...]
        nm = ADAM_B1 * m_ref[...] + (1.0 - ADAM_B1) * gv
        nv = ADAM_B2 * v_ref[...] + (1.0 - ADAM_B2) * (gv * gv)
        nm_ref[...] = nm
        nv_ref[...] = nv
        d_ref[...] = -ADAM_LR * ((nm / bc1) / (jnp.sqrt(nv / bc2) + ADAM_EPS) + ADAM_WD * w_ref[...])

    spec = pl.BlockSpec(tuple(block), lambda i, *j: (first + i,) + j)
    given = [] if into is None else list(into)
    return pl.pallas_call(
        body, grid=grid, in_specs=[spec] * 4 + [pl.BlockSpec(memory_space=pl.ANY)] * len(given), out_specs=[spec] * 3,
        out_shape=[_sds(w.shape, F32)] * 3, input_output_aliases={4 + i: i for i in range(len(given))},
        compiler_params=_cparams(("parallel",) * nd), name=name)(w, g, m, v, *given)


MESH = pl.DeviceIdType.MESH
ANY = pl.BlockSpec(memory_space=pl.ANY)
VM = pl.BlockSpec(memory_space=pltpu.VMEM)


def _place():
    x, y, c = lax.axis_index("x"), lax.axis_index("y"), lax.axis_index("c")
    return x, y, c, [(1 - x, y), (x, 1 - y), (1 - x, 1 - y)]


def small_allgather(v, *, name):
    m_per, n = v.shape

    def body(x_ref, out_ref, send_sems, recv_sems, local_sem):
        x, y, c, chips = _place()
        me, sibling = (x, y, c), (x, y, 1 - c)

        def rows(px, py, pc):
            return out_ref.at[pl.ds((4 * px + 2 * py + pc) * m_per, m_per), :]

        def copy(k, block, to, src=None):
            return pltpu.make_async_remote_copy(
                src_ref=rows(*block) if src is None else src, dst_ref=rows(*block),
                send_sem=send_sems.at[k], recv_sem=recv_sems.at[k], device_id=to, device_id_type=MESH)

        mine = pltpu.make_async_copy(x_ref, rows(*me), local_sem)
        mine.start()
        first = [copy(0, me, sibling, src=x_ref)]
        first += [copy(1 + j, me, (*chip, c), src=x_ref) for j, chip in enumerate(chips)]
        for cp in first:
            cp.start()
        passed = [copy(4 + j, (*chip, c), sibling) for j, chip in enumerate(chips)]
        for j, chip in enumerate(chips):
            copy(1 + j, (*chip, c), me).wait_recv()
            passed[j].start()
        copy(0, sibling, me).wait_recv()
        for j, chip in enumerate(chips):
            copy(4 + j, (*chip, 1 - c), me).wait_recv()
        for cp in first + passed:
            cp.wait_send()
        mine.wait()

    return pl.pallas_call(
        body, out_shape=_sds((8 * m_per, n), v.dtype), in_specs=[VM], out_specs=VM,
        scratch_shapes=[pltpu.SemaphoreType.DMA((7,)), pltpu.SemaphoreType.DMA((7,)), pltpu.SemaphoreType.DMA],
        name=name)(v)


TENSORS = {
    "w_in": ("lead", None, (4, D, 1285), (1, D, 1285)),
    "w_mg": ("col", 768, (D, 3072), (512, 3072)),
    "w_br": ("col", 256, (3, BW, D), (3, BW, D)),
    "w_o": ("row", 256, (D, D), (D, D)),
    "w_up": ("col", 1408, (D, 5632), (256, 5632)),
    "w_down": ("row", 704, (DFF, D), (704, D)),
}
BIG = tuple(TENSORS)


def _shard_shape(name):
    kind, width, full, _ = TENSORS[name]
    if kind == "lead":
        return full[1:]
    return full[:-1] + (width,) if kind == "col" else (width,) + full[1:]


def _shard_view(ref, layers, name, k):
    kind, width, full, _ = TENSORS[name]
    if kind == "lead":
        return ref.at[layers, k]
    if kind == "row":
        return ref.at[layers, pl.ds(k * width, width)]
    return ref.at[(layers,) + (slice(None),) * (len(full) - 1) + (pl.ds(k * width, width),)]


def _remote(send_sems, recv_sems, k, src, dst, to):
    return pltpu.make_async_remote_copy(src_ref=src, dst_ref=dst, send_sem=send_sems.at[k], recv_sem=recv_sems.at[k],
                                        device_id=to, device_id_type=MESH)


def _dma_sems(n):
    return [pltpu.SemaphoreType.DMA((n,)), pltpu.SemaphoreType.DMA((n,))]


RS_GROUP = 2
HBM = pl.BlockSpec(memory_space=pltpu.HBM)
SEM = pl.BlockSpec(memory_space=pltpu.SEMAPHORE)
SPLIT_CALL = dict(compiler_params=pltpu.CompilerParams(has_side_effects=pltpu.SideEffectType.DATAFLOW_SIDE_EFFECTING))
PULL_SET = (("w_up", "w_down", "w_o"), ("w_in", "w_mg", "w_br"))


def _in_hbm(a):
    return pltpu.with_memory_space_constraint(a, pltpu.HBM)


def _pull_sends(send_sems, recv_sems, p, o, layer, core, x, y, chips):
    return [_remote(send_sems, recv_sems, 3 * BIG.index(n) + j, p[n].at[layer], _shard_view(o[n], 0, n, 2 * x + y), (*chip, core))
            for n in PULL_SET[core] for j, chip in enumerate(chips)]


def _pull_arrivals(send_sems, recv_sems, o, core, x, y, chips, to):
    views = [(3 * BIG.index(n) + j, _shard_view(o[n], 0, n, 2 * chip[0] + chip[1]))
             for n in PULL_SET[core] for j, chip in enumerate(chips)]
    return [_remote(send_sems, recv_sems, k, v, v, to) for k, v in views]


def gather_start(shards, layer, *, name):
    nt = len(BIG)

    def body(*refs):
        p, o = dict(zip(BIG, refs[:nt])), dict(zip(BIG, refs[nt:2 * nt]))
        x, y, c, chips = _place()
        for core in (0, 1):
            @pl.when(c == core)
            def _():
                for cp in _pull_sends(refs[2 * nt], refs[2 * nt + 1], p, o, layer, core, x, y, chips):
                    cp.start()
        refs[-1][...] = jnp.zeros_like(refs[-1])

    lands = [_in_hbm(lax.empty((1,) + TENSORS[n][2], shards[n].dtype)) for n in BIG]
    outs = pl.pallas_call(
        body,
        out_shape=(pltpu.SemaphoreType.DMA((3 * nt,)), pltpu.SemaphoreType.DMA((3 * nt,)),
                   *[pltpu.HBM(a.shape, a.dtype) for a in lands], _sds((8, HD), F32)),
        in_specs=[HBM] * (2 * nt), out_specs=(SEM, SEM, *[HBM] * nt, VM),
        input_output_aliases={nt + t: 2 + t for t in range(nt)}, name=name, **SPLIT_CALL)(
            *[_in_hbm(shards[n]) for n in BIG], *lands)
    return outs[0], outs[1], outs[2:2 + nt], outs[-1]


def gather_wait(send_sems, recv_sems, shards, lands, after, layer, *, name):
    nt = len(BIG)

    def body(*refs):
        p, o = dict(zip(BIG, refs[:nt])), dict(zip(BIG, refs[nt:2 * nt]))
        ss, rs = refs[2 * nt], refs[2 * nt + 1]
        x, y, c, chips = _place()
        for core in (0, 1):
            @pl.when(c == core)
            def _():
                for cp in _pull_sends(ss, rs, p, o, layer, core, x, y, chips):
                    cp.wait_send()
                for cp in _pull_arrivals(ss, rs, o, core, x, y, chips, (x, y, core)):
                    cp.wait_recv()

    return pl.pallas_call(
        body, out_shape=tuple(pltpu.HBM(a.shape, a.dtype) for a in lands),
        in_specs=[HBM] * (2 * nt) + [SEM, SEM, ANY], out_specs=tuple([HBM] * nt),
        input_output_aliases={nt + t: t for t in range(nt)}, name=name, **SPLIT_CALL)(
            *[_in_hbm(shards[n]) for n in BIG], *lands, send_sems, recv_sems, after)


def gather_forward(shards, lands, layer, *, name):
    nt = len(BIG)

    def body(*refs):
        p, o = dict(zip(BIG, refs[:nt])), dict(zip(BIG, refs[2 * nt:3 * nt]))
        ss, rs = refs[3 * nt:]
        x, y, c, chips = _place()
        for core in (0, 1):
            @pl.when(c == core)
            def _():
                me, sibling = (x, y, core), (x, y, 1 - core)
                sends = _pull_arrivals(ss, rs, o, core, x, y, chips, sibling)
                sends += [_remote(ss, rs, 3 * nt + t, p[n].at[layer], _shard_view(o[n], 0, n, 2 * x + y), sibling)
                          for t, n in enumerate(BIG)]
                for cp in sends:
                    cp.start()
                for cp in sends:
                    cp.wait_send()
                for cp in _pull_arrivals(ss, rs, o, 1 - core, x, y, chips, me):
                    cp.wait_recv()
                for t, n in enumerate(BIG):
                    own = _shard_view(o[n], 0, n, 2 * x + y)
                    _remote(ss, rs, 3 * nt + t, own, own, me).wait_recv()

    outs = pl.pallas_call(
        body, out_shape=[_sds(a.shape, a.dtype) for a in lands], in_specs=[ANY] * (2 * nt), out_specs=[ANY] * nt,
        input_output_aliases={nt + t: t for t in range(nt)}, scratch_shapes=_dma_sems(4 * nt), name=name)(
            *[shards[n] for n in BIG], *lands)
    return dict(zip(BIG, outs))


def pair_exchange(g, *, name):
    hh = g[BIG[0]].shape[0] // 2
    nt = len(BIG)

    def body(*refs):
        send_sems, recv_sems = refs[2 * nt:]
        x, y, c, _ = _place()
        copies = [_remote(send_sems, recv_sems, t, refs[t].at[pl.ds(hh * (1 - c), hh)], refs[nt + t], (x, y, 1 - c))
                  for t in range(nt)]
        for cp in copies:
            cp.start()
        for cp in copies:
            cp.wait()

    outs = pl.pallas_call(
        body, out_shape=[_sds((hh,) + g[n].shape[1:], g[n].dtype) for n in BIG], in_specs=[ANY] * nt, out_specs=[ANY] * nt,
        scratch_shapes=_dma_sems(nt), name=name)(*[g[n] for n in BIG])
    return dict(zip(BIG, outs))


def _chip_copies(send_sems, recv_sems, s_refs, land_refs, c, chips):
    hl = s_refs[0].shape[0]
    return [_remote(send_sems, recv_sems, 3 * t + j, _shard_view(s_refs[t], pl.ds(0, hl), n, 2 * chip[0] + chip[1]),
                    land_refs[t].at[j], (*chip, c))
            for t, n in enumerate(BIG) for j, chip in enumerate(chips)]


def _landing_shapes(s):
    hl = s[BIG[0]].shape[0]
    return [_sds((3, hl) + _shard_shape(n), s[n].dtype) for n in BIG]


def chip_exchange(s, *, name):
    nt = len(BIG)

    def body(*refs):
        send_sems, recv_sems = refs[2 * nt:]
        x, y, c, chips = _place()
        copies = _chip_copies(send_sems, recv_sems, refs[:nt], refs[nt:2 * nt], c, chips)
        for cp in copies:
            cp.start()
        for cp in copies:
            cp.wait()

    outs = pl.pallas_call(
        body, out_shape=_landing_shapes(s), in_specs=[ANY] * nt, out_specs=[ANY] * nt,
        scratch_shapes=_dma_sems(3 * nt), name=name)(*[s[n] for n in BIG])
    return dict(zip(BIG, outs))


def chip_exchange_start(s, *, name, after=None):
    nt = len(BIG)
    first = [] if after is None else [after]

    def body(*refs):
        o = refs[2 * nt + len(first):]
        x, y, c, chips = _place()
        for cp in _chip_copies(o[0], o[1], refs[:nt], refs[nt:2 * nt], c, chips):
            cp.start()
        refs[-1][...] = jnp.zeros_like(refs[-1])

    lands = [_in_hbm(lax.empty(d.shape, d.dtype)) for d in _landing_shapes(s)]
    srcs = [_in_hbm(s[n]) for n in BIG]
    outs = pl.pallas_call(
        body,
        out_shape=(pltpu.SemaphoreType.DMA((3 * nt,)), pltpu.SemaphoreType.DMA((3 * nt,)),
                   *[pltpu.HBM(a.shape, a.dtype) for a in srcs + lands], _sds((8, HD), F32)),
        in_specs=[HBM] * (2 * nt) + [ANY] * len(first), out_specs=(SEM, SEM, *[HBM] * (2 * nt), VM),
        input_output_aliases={t: 2 + t for t in range(2 * nt)}, name=name, **SPLIT_CALL)(*srcs, *lands, *first)
    return outs[0], outs[1], outs[2:2 + nt], outs[2 + nt:2 + 2 * nt], outs[-1]


def chip_exchange_wait(send_sems, recv_sems, srcs, lands, after, *, name):
    nt = len(BIG)

    def body(*refs):
        x, y, c, chips = _place()
        for cp in _chip_copies(refs[2 * nt], refs[2 * nt + 1], refs[:nt], refs[nt:2 * nt], c, chips):
            cp.wait_send()
            cp.wait_recv()

    outs = pl.pallas_call(
        body, out_shape=tuple(pltpu.HBM(a.shape, a.dtype) for a in list(srcs) + list(lands)),
        in_specs=[HBM] * (2 * nt) + [SEM, SEM] + [ANY] * len(after), out_specs=tuple([HBM] * (2 * nt)),
        input_output_aliases={t: t for t in range(2 * nt)}, name=name, **SPLIT_CALL)(
            *srcs, *lands, send_sems, recv_sems, *after)
    return dict(zip(BIG, outs[:nt])), dict(zip(BIG, outs[nt:]))


def pair_share(f, l0, hh, *, name):
    nt = len(BIG)

    def body(*refs):
        o = refs[nt:2 * nt]
        send_sems, recv_sems = refs[2 * nt:]
        x, y, c, _ = _place()
        mine, theirs = pl.ds(l0 + hh * c, hh), pl.ds(l0 + hh * (1 - c), hh)
        copies = [_remote(send_sems, recv_sems, t, o[t].at[mine], o[t].at[mine], (x, y, 1 - c)) for t in range(nt)]
        for cp in copies:
            cp.start()
        for t, cp in enumerate(copies):
            cp.wait_send()
            _remote(send_sems, recv_sems, t, o[t].at[theirs], o[t].at[theirs], (x, y, c)).wait_recv()

    outs = pl.pallas_call(
        body, out_shape=[_sds(f[n].shape, f[n].dtype) for n in BIG], in_specs=[ANY] * nt, out_specs=[ANY] * nt,
        input_output_aliases={t: t for t in range(nt)}, scratch_shapes=_dma_sems(nt), name=name)(*[f[n] for n in BIG])
    return dict(zip(BIG, outs))


def pair_add(g, r, idx, *, tensor, name):
    _, _, full, blk = TENSORS[tensor]
    hh = r.shape[0]

    def body(idx_ref, g_ref, r_ref, o_ref):
        o_ref[...] = (g_ref[...].astype(F32) + r_ref[...].astype(F32)).astype(o_ref.dtype)

    own = pl.BlockSpec((None,) + blk, lambda *a: (a[0],) + a[1:-1])
    return pl.pallas_call(
        body, out_shape=_sds(r.shape, r.dtype),
        grid_spec=pltpu.PrefetchScalarGridSpec(
            num_scalar_prefetch=1, grid=(hh,) + tuple(f // b for f, b in zip(full, blk)),
            in_specs=[pl.BlockSpec((None,) + blk, lambda *a: (hh * a[-1][0] + a[0],) + a[1:-1]), own], out_specs=own),
        compiler_params=_cparams(("parallel",) * (1 + len(full))), name=name)(idx, g, r)


def chip_add(s, r, idx, totals, l0, *, tensor, name):
    kind, width, full, _ = TENSORS[tensor]
    shard = _shard_shape(tensor)
    hh = s.shape[0]
    zeros = (0,) * len(shard)

    def body(idx_ref, s_ref, r0_ref, r1_ref, r2_ref, t_ref, o_ref):
        o_ref[...] = ((s_ref[...].astype(F32) + r0_ref[...].astype(F32)) + r1_ref[...].astype(F32)) + r2_ref[...].astype(F32)

    if kind == "lead":
        mine = pl.BlockSpec((None, None) + shard, lambda i, ix: (i, ix[1]) + zeros)
    elif kind == "row":
        mine = pl.BlockSpec((None,) + shard, lambda i, ix: (i, ix[1]) + zeros[1:])
    else:
        mine = pl.BlockSpec((None,) + shard, lambda i, ix: (i,) + zeros[1:] + (ix[1],))
    peer = lambda j: pl.BlockSpec((None, None) + shard, lambda i, ix: (j, i) + zeros)
    return pl.pallas_call(
        body, out_shape=_sds(totals.shape, F32),
        grid_spec=pltpu.PrefetchScalarGridSpec(
            num_scalar_prefetch=1, grid=(hh,), in_specs=[mine, peer(0), peer(1), peer(2), pl.BlockSpec(memory_space=pl.ANY)],
            out_specs=pl.BlockSpec((None,) + shard, lambda i, ix: (l0 + hh * ix[0] + i,) + zeros)),
        input_output_aliases={5: 0}, compiler_params=_cparams(("parallel",)), name=name)(idx, s, r, r, r, totals)


def _flat_rows(arrs):
    v = jnp.concatenate([a.reshape(-1) for a in arrs])
    n = -(-v.shape[0] // 1024) * 1024
    return jnp.pad(v, (0, n - v.shape[0])).reshape(n // HD, HD)


def _unflat(buf, shapes):
    v, out, o = buf.reshape(-1), [], 0
    for s in shapes:
        n = int(np.prod(s))
        out.append(v[o:o + n].reshape(s))
        o += n
    return out


WEIGHTS = ("norm1_g", "norm2_g", "w_ada", "b_ada", "w_in", "w_gla_a2", "b_gla_a", "b_fox_f", "ret_norm_g", "gla_norm_g",
           "q_norm_g", "k_norm_g", "w_br", "w_mg", "b_mg", "w_o", "w_up", "w_conv", "b_conv", "w_down")
REPLICATED = ("norm1_g", "norm2_g", "b_gla_a", "b_fox_f", "ret_norm_g", "gla_norm_g", "q_norm_g", "k_norm_g", "b_mg", "b_conv")
ADAM_BLOCKS = dict(w_ada=(1, 256, 1536), w_in=(1, 256, 1285), w_br=(1, 3, BW, 256), w_mg=(1, 512, 768), w_o=(2, 256, D),
                   w_up=(1, 256, 1408), w_down=(1, 352, D))
ALL_AXES = ("x", "y", "c")


def kernel(x, c, norm1_g, norm2_g, w_ada, b_ada, w_in, w_gla_a2, b_gla_a, b_fox_f, ret_norm_g, gla_norm_g, q_norm_g, k_norm_g, w_br, w_mg, b_mg, w_o, w_up, w_conv, b_conv, w_down, loss_target, m_norm1_g, m_norm2_g, m_w_ada, m_b_ada, m_w_in, m_w_gla_a2, m_b_gla_a, m_b_fox_f, m_ret_norm_g, m_gla_norm_g, m_q_norm_g, m_k_norm_g, m_w_br, m_w_mg, m_b_mg, m_w_o, m_w_up, m_w_conv, m_b_conv, m_w_down, v_norm1_g, v_norm2_g, v_w_ada, v_b_ada, v_w_in, v_w_gla_a2, v_b_gla_a, v_b_fox_f, v_ret_norm_g, v_gla_norm_g, v_q_norm_g, v_k_norm_g, v_w_br, v_w_mg, v_b_mg, v_w_o, v_w_up, v_w_conv, v_b_conv, v_w_down):
    w = dict(zip(WEIGHTS, (norm1_g, norm2_g, w_ada, b_ada, w_in, w_gla_a2, b_gla_a, b_fox_f, ret_norm_g, gla_norm_g,
                           q_norm_g, k_norm_g, w_br, w_mg, b_mg, w_o, w_up, w_conv, b_conv, w_down)))
    m = dict(zip(WEIGHTS, (m_norm1_g, m_norm2_g, m_w_ada, m_b_ada, m_w_in, m_w_gla_a2, m_b_gla_a, m_b_fox_f, m_ret_norm_g,
                           m_gla_norm_g, m_q_norm_g, m_k_norm_g, m_w_br, m_w_mg, m_b_mg, m_w_o, m_w_up, m_w_conv, m_b_conv,
                           m_w_down)))
    v = dict(zip(WEIGHTS, (v_norm1_g, v_norm2_g, v_w_ada, v_b_ada, v_w_in, v_w_gla_a2, v_b_gla_a, v_b_fox_f, v_ret_norm_g,
                           v_gla_norm_g, v_q_norm_g, v_k_norm_g, v_w_br, v_w_mg, v_b_mg, v_w_o, v_w_up, v_w_conv, v_b_conv,
                           v_w_down)))
    nl = norm1_g.shape[0]
    seq = x.shape[1]
    xi, yi, ci = lax.axis_index("x"), lax.axis_index("y"), lax.axis_index("c")
    k_me = 2 * xi + yi
    b_me = 4 * xi + 2 * yi + ci
    ada_n = w_ada.shape[2]
    a2_n, conv_n = w_gla_a2.shape[2], w_conv.shape[2]

    shards = {n: w[n].astype(MMT) for n in BIG}
    started = gather_start(shards, 0, name="gather0_start")

    def gather_finish(l, started, after):
        send_sems, recv_sems, lands, _ = started
        lands = gather_wait(send_sems, recv_sems, shards, lands, after, l, name=f"gather{l}_wait")
        big = gather_forward(shards, lands, l, name=f"gather{l}_forward")
        big["w1"] = build_w1(big["w_in"], big["w_mg"])
        return big

    blk = _flat_rows([c, w_gla_a2, w_conv]) + started[-1][0, 0]
    g1 = small_allgather(blk, name="gather_small").reshape(8, blk.shape[0], HD)
    c_all = g1[:, :D // HD].reshape(8, D)
    by_chip = g1[0::2].reshape(4, -1)[:, D:]
    a2_sh, conv_sh = by_chip[:, :nl * GLR * a2_n], by_chip[:, nl * GLR * a2_n:nl * (GLR * a2_n + 3 * conv_n)]
    full_small = dict(
        w_gla_a2=a2_sh.reshape(4, nl, GLR, a2_n).transpose(1, 2, 0, 3).reshape(nl, GLR, 4 * a2_n),
        w_conv=conv_sh.reshape(4, nl, 3, conv_n).transpose(1, 2, 0, 3).reshape(nl, 3, 4 * conv_n))

    b_ada_sh = lax.dynamic_slice_in_dim(b_ada, k_me * ada_n, ada_n, axis=1)[:, None, :]
    mod_sh = ada_mod(c_all, w_ada, b_ada_sh)
    g2 = small_allgather(mod_sh.reshape(nl * 8, ada_n), name="gather_mod").reshape(4, 2, nl, 8, ada_n)[:, 0]
    mod_me = lax.dynamic_index_in_dim(g2, b_me, axis=2, keepdims=False).transpose(1, 0, 2).reshape(nl, 4 * ada_n)

    wsmall = {n: w[n] for n in REPLICATED}
    wsmall.update(full_small)
    mods = [[mod_me[l:l + 1, i * D:(i + 1) * D] for i in range(6)] for l in range(nl)]
    big = gather_finish(0, started, mod_me)

    cosf, sinf = _rope_tables(seq)
    xs, saved, params = x[0], [], []
    for l in range(nl):
        if l + 1 < nl:
            started = gather_start(shards, l + 1, name=f"gather{l + 1}_start")
            mods[l][1] = mods[l][1] + started[-1][0, 0]
        params.append(layer_params(wsmall, big, l))
        xs, sv = layer_fwd(xs, mods[l], params[l], cosf, sinf)
        saved.append(sv)
        if l + 1 < nl:
            big = gather_finish(l + 1, started, xs)
    loss_part, dx = loss_and_grad(xs, loss_target[0], name="loss")
    loss = lax.psum(loss_part[0, 0], ALL_AXES)
    grads, dmods = [None] * nl, [None] * nl
    idx = jnp.stack([ci, k_me]).astype(jnp.int32)
    totals = {n: lax.empty((nl,) + _shard_shape(n), F32) for n in BIG}

    def finish_group(pending, after, totals, idx):
        group, send_sems, recv_sems, srcs, lands, _ = pending
        sums, from_chips = chip_exchange_wait(send_sems, recv_sems, srcs, lands, after, name=f"rs{group}_chip_exchange_wait")
        totals = {n: chip_add(sums[n], from_chips[n], idx, totals[n], RS_GROUP * group, tensor=n,
                              name=f"rs{group}_chip_add_{n}") for n in BIG}
        return pair_share(totals, RS_GROUP * group, RS_GROUP // 2, name=f"rs{group}_pair_share")

    pending = None
    for group in reversed(range(nl // RS_GROUP)):
        layers = range(RS_GROUP * group, RS_GROUP * (group + 1))
        stacks = {n: lax.empty((RS_GROUP,) + TENSORS[n][2], MMT) for n in BIG if n != "w_br"}
        if pending is not None:
            mods[layers[-1]][5] = mods[layers[-1]][5] + pending[-1][0, 0]
        for l in reversed(layers):
            dx, g, dmods[l], stacks = layer_bwd(dx, mods[l], params[l], saved[l], cosf, sinf, stacks, l - layers[0])
            grads[l] = layer_grads(g)
        stacks["w_br"] = jnp.stack([grads[l]["w_br"].astype(MMT) for l in layers])
        from_sibling = pair_exchange(stacks, name=f"rs{group}_pair_exchange")
        chip_sum = {n: pair_add(stacks[n], from_sibling[n], idx, tensor=n, name=f"rs{group}_pair_add_{n}") for n in BIG}
        if group > 0:
            if pending is not None:
                totals = finish_group(pending, [dx, *chip_sum.values()], totals, idx)
            pending = (group, *chip_exchange_start(chip_sum, name=f"rs{group}_chip_exchange_start"))

    small_names = REPLICATED + ("w_gla_a2", "w_conv")
    small_shapes = [(nl, 6 * D)] + [(nl,) + grads[0][n].shape for n in small_names]
    vec = _flat_rows([jnp.concatenate(dmods, axis=0)] + [jnp.stack([grads[l][n] for l in range(nl)]) for n in small_names])
    gs = small_allgather(vec, name="gather_small_grads")
    earlier = pending
    pending = (0, *chip_exchange_start(chip_sum, name="rs0_chip_exchange_start", after=gs))
    if earlier is not None:
        totals = finish_group(earlier, [dx, pending[-1]], totals, idx)
    gs = (gs + pending[-1][0, 0]).reshape(8, vec.shape[0], HD)
    summed = _unflat(sum_devices(gs), small_shapes)
    grad = dict(zip(small_names, summed[1:]))
    grad["b_ada"] = summed[0]
    grad["w_gla_a2"] = lax.dynamic_slice_in_dim(grad["w_gla_a2"], k_me * a2_n, a2_n, axis=2)
    grad["w_conv"] = lax.dynamic_slice_in_dim(grad["w_conv"], k_me * conv_n, conv_n, axis=2)
    dmod_all = gs[:, :nl * 6 * D // HD].reshape(8, nl, 6 * D)
    dmod_sh = lax.dynamic_slice_in_dim(dmod_all, k_me * ada_n, ada_n, axis=2).transpose(1, 0, 2)
    grad["w_ada"] = ada_dw(c_all, dmod_sh)

    delta, new_m, new_v = {}, {}, {}
    delta["w_ada"], new_m["w_ada"], new_v["w_ada"] = adamw(w["w_ada"], grad["w_ada"], m["w_ada"], v["w_ada"],
                                                          block=ADAM_BLOCKS["w_ada"], name="adamw_w_ada")
    rest = [n for n in WEIGHTS if n not in ADAM_BLOCKS]
    shapes = [w[n].shape for n in rest]
    flat = [_flat_rows([t[n] for n in rest]) for t in (w, grad, m, v)]
    outs = adamw(*flat, block=flat[0].shape, name="adamw_small")
    for t, o in zip((delta, new_m, new_v), outs):
        t.update(zip(rest, _unflat(o, shapes)))

    later = {n: adamw(w[n], totals[n], m[n], v[n], block=ADAM_BLOCKS[n], name="adamw_later_" + n, rows=(RS_GROUP, nl))
             for n in BIG} if nl > RS_GROUP else {}
    done_first = [outs[0], delta["w_ada"]] + [later[n][0] for n in later]
    grad.update(finish_group(pending, done_first, totals, idx))
    for n in BIG:
        delta[n], new_m[n], new_v[n] = adamw(w[n], grad[n], m[n], v[n], block=ADAM_BLOCKS[n], name="adamw_" + n,
                                             rows=(0, min(RS_GROUP, nl)), into=later.get(n))

    return (loss, dx[None], *[grad[n] for n in WEIGHTS], *[delta[n] for n in WEIGHTS], *[new_m[n] for n in WEIGHTS],
            *[new_v[n] for n in WEIGHTS])
```

```python
import functools

import numpy as np
import jax
import jax.numpy as jnp
from jax import lax
from jax.experimental import pallas as pl
from jax.experimental.pallas import tpu as pltpu

F32 = jnp.float32
MMT = jnp.bfloat16
HI = lax.Precision.HIGHEST

D = 1024
DEPTH = 4
NH = 4
HD = 128
BW = NH * HD
CH = 64
GDK = 64
GLR = 16
DFF = 2816
EPS = 1e-6
ROPE_BASE = 10000.0

GP, RG, GG, FV, RQ, RK, RV, GQ, GK, GV, FQ, FK, LR, FF = (
    0, 3072, 3584, 4096, 4608, 5120, 5632, 6144, 6400, 6656, 7168, 7680, 8192, 8320)
NZZ = 8448
WZ0 = 3072
IN_W = 5140
W_IN_COLS = dict(rqkv=(0, 1536), rg=(1536, 2048), gqkv=(2048, 3072), lr=(3072, 3088), gg=(3088, 3600), fqk=(3600, 4624),
                 fv=(4624, 5136), ff=(5136, 5140))

VMEM_LIMIT = 56 * 1024 * 1024

ADAM_LR, ADAM_B1, ADAM_B2, ADAM_EPS, ADAM_WD, ADAM_STEP = 0.001, 0.9, 0.999, 1e-08, 0.01, 10


def _cparams(sem=None):
    return pltpu.CompilerParams(dimension_semantics=sem, vmem_limit_bytes=VMEM_LIMIT)


def _sds(shape, dtype):
    return jax.ShapeDtypeStruct(tuple(shape), dtype)


def _dot(a, b, precision=None):
    return lax.dot_general(a, b, (((1,), (0,)), ((), ())), precision=precision, preferred_element_type=F32)


def _dot_nt(a, b, precision=None):
    return lax.dot_general(a, b, (((1,), (1,)), ((), ())), precision=precision, preferred_element_type=F32)


def _dot_tn(a, b, precision=None):
    return lax.dot_general(a, b, (((0,), (0,)), ((), ())), precision=precision, preferred_element_type=F32)


def _silu(x):
    return x * jax.nn.sigmoid(x)


def _log_sigmoid(x):
    return jnp.minimum(x, 0.0) - jnp.log(1.0 + jnp.exp(jnp.minimum(x, -x)))


@jax.custom_vjp
def _swap_halves(x):
    return pltpu.roll(x, HD // 2, 1)


_swap_halves.defvjp(lambda x: (_swap_halves(x), None), lambda _, g: (_swap_halves(g),))


@jax.custom_vjp
def _bdot(a, b):
    return _dot(a.astype(MMT), b.astype(MMT))


@jax.custom_vjp
def _bdot_nt(a, b):
    return _dot_nt(a.astype(MMT), b.astype(MMT))


@jax.custom_vjp
def _bdot_tn(a, b):
    return _dot_tn(a.astype(MMT), b.astype(MMT))


_bdot.defvjp(lambda a, b: (_bdot(a, b), (a, b)), lambda r, g: (_bdot_nt(g, r[1]), _bdot_tn(r[0], g)))
_bdot_nt.defvjp(lambda a, b: (_bdot_nt(a, b), (a, b)), lambda r, g: (_bdot(g, r[1]), _bdot_tn(g, r[0])))
_bdot_tn.defvjp(lambda a, b: (_bdot_tn(a, b), (a, b)), lambda r, g: (_bdot_nt(r[1], g), _bdot(r[0], g)))


def _stacked(blk, idx, layer):
    if layer is None:
        return pl.BlockSpec(blk, idx)
    return pl.BlockSpec((None,) + blk, lambda i, j: (layer,) + idx(i, j))


def mm_nn(a, b, *, tm, tn, out_dtype, name, layer=None):
    m, k = a.shape
    n = b.shape[-1]

    def body(a_ref, b_ref, o_ref):
        o_ref[...] = _dot(a_ref[...], b_ref[...]).astype(o_ref.dtype)

    return pl.pallas_call(
        body, grid=(m // tm, n // tn),
        in_specs=[pl.BlockSpec((tm, k), lambda i, j: (i, 0)), _stacked((k, tn), lambda i, j: (0, j), layer)],
        out_specs=pl.BlockSpec((tm, tn), lambda i, j: (i, j)),
        out_shape=_sds((m, n), out_dtype), compiler_params=_cparams(("parallel", "parallel")), name=name)(a, b)


def mm_nn_residual(a, b, res, gate, *, tm, tn, name, layer=None):
    m, k = a.shape
    n = b.shape[-1]

    def body(a_ref, b_ref, r_ref, g_ref, x_ref, y_ref):
        acc = _dot(a_ref[...], b_ref[...])
        y_ref[...] = acc
        x_ref[...] = r_ref[...] + g_ref[...] * acc

    return pl.pallas_call(
        body, grid=(m // tm, n // tn),
        in_specs=[pl.BlockSpec((tm, k), lambda i, j: (i, 0)), _stacked((k, tn), lambda i, j: (0, j), layer),
                  pl.BlockSpec((tm, tn), lambda i, j: (i, j)), pl.BlockSpec((1, tn), lambda i, j: (0, j))],
        out_specs=[pl.BlockSpec((tm, tn), lambda i, j: (i, j)), pl.BlockSpec((tm, tn), lambda i, j: (i, j))],
        out_shape=[_sds((m, n), F32), _sds((m, n), F32)],
        compiler_params=_cparams(("parallel", "parallel")), name=name)(a, b, res, gate)


def mm_nt(a, b, *, tm, tn, out_dtype, name, layer=None):
    m, k = a.shape
    n = b.shape[-2]

    def body(a_ref, b_ref, o_ref):
        o_ref[...] = _dot_nt(a_ref[...], b_ref[...]).astype(o_ref.dtype)

    return pl.pallas_call(
        body, grid=(m // tm, n // tn),
        in_specs=[pl.BlockSpec((tm, k), lambda i, j: (i, 0)), _stacked((tn, k), lambda i, j: (j, 0), layer)],
        out_specs=pl.BlockSpec((tm, tn), lambda i, j: (i, j)),
        out_shape=_sds((m, n), out_dtype), compiler_params=_cparams(("parallel", "parallel")), name=name)(a, b)


def mm_nt2(a1, a2, b, *, tm, tn, name, layer):
    m, k1 = a1.shape
    k2 = a2.shape[1]
    n = b.shape[-2]

    def body(a1_ref, a2_ref, b_ref, o_ref):
        o_ref[...] = _dot_nt(a1_ref[...], b_ref[:, :k1]) + _dot_nt(a2_ref[...], b_ref[:, k1:])

    return pl.pallas_call(
        body, grid=(m // tm, n // tn),
        in_specs=[pl.BlockSpec((tm, k1), lambda i, j: (i, 0)), pl.BlockSpec((tm, k2), lambda i, j: (i, 0)),
                  _stacked((tn, k1 + k2), lambda i, j: (j, 0), layer)],
        out_specs=pl.BlockSpec((tm, tn), lambda i, j: (i, j)),
        out_shape=_sds((m, n), F32), compiler_params=_cparams(("parallel", "parallel")), name=name)(a1, a2, b)


def mm_tn(a, b, *, tm, tn, out_dtype, name, col0=0, ncols=None, stack=None, layer=None, out_col0=0):
    s, m = a.shape
    n = b.shape[1] - col0 if ncols is None else ncols
    c0, oc0 = col0 // tn, out_col0 // tn

    def body(a_ref, b_ref, *rest):
        o_ref = rest[-1]
        o_ref[...] = _dot_tn(a_ref[...], b_ref[...]).astype(o_ref.dtype)

    in_specs = [pl.BlockSpec((s, tm), lambda i, j: (0, i)), pl.BlockSpec((s, tn), lambda i, j: (0, c0 + j))]
    if stack is None:
        return pl.pallas_call(
            body, grid=(m // tm, n // tn), in_specs=in_specs, out_specs=pl.BlockSpec((tm, tn), lambda i, j: (i, j)),
            out_shape=_sds((m, n), out_dtype), compiler_params=_cparams(("parallel", "parallel")), name=name)(a, b)
    return pl.pallas_call(
        body, grid=(m // tm, n // tn), in_specs=in_specs + [pl.BlockSpec(memory_space=pl.ANY)],
        out_specs=pl.BlockSpec((None, tm, tn), lambda i, j: (layer, i, oc0 + j)),
        out_shape=_sds(stack.shape, stack.dtype), input_output_aliases={2: 0},
        compiler_params=_cparams(("parallel", "parallel")), name=name)(a, b, stack)


def _row_tile(s):
    return min(256, s)


def _norm_mod_f(x, g, scale, shift):
    r = lax.rsqrt(jnp.mean(x * x, axis=-1, keepdims=True) + EPS)
    return (x * r * g) * (1.0 + scale) + shift


def norm_mod(x, g, scale, shift, *, name):
    s = x.shape[0]
    t = _row_tile(s)

    def body(x_ref, g_ref, sc_ref, sh_ref, o_ref):
        o_ref[...] = _norm_mod_f(x_ref[...], g_ref[...], sc_ref[...], sh_ref[...]).astype(o_ref.dtype)

    vec = pl.BlockSpec((1, D), lambda i: (0, 0))
    return pl.pallas_call(
        body, grid=(s // t,), in_specs=[pl.BlockSpec((t, D), lambda i: (i, 0)), vec, vec, vec],
        out_specs=pl.BlockSpec((t, D), lambda i: (i, 0)), out_shape=_sds((s, D), MMT),
        compiler_params=_cparams(("parallel",)), name=name)(x, g, scale, shift)


def norm_mod_bwd(x, dh, dres, g, scale, shift, *, name):
    s = x.shape[0]
    t = _row_tile(s)

    def body(x_ref, dh_ref, dr_ref, g_ref, sc_ref, sh_ref, dx_ref, dg_ref, dsc_ref, dsh_ref):
        @pl.when(pl.program_id(0) == 0)
        def _():
            dg_ref[...] = jnp.zeros_like(dg_ref)
            dsc_ref[...] = jnp.zeros_like(dsc_ref)
            dsh_ref[...] = jnp.zeros_like(dsh_ref)

        _, vjp = jax.vjp(_norm_mod_f, x_ref[...], g_ref[...], sc_ref[...], sh_ref[...])
        dx, dg, dsc, dsh = vjp(dh_ref[...])
        dx_ref[...] = dr_ref[...] + dx
        dg_ref[...] += dg
        dsc_ref[...] += dsc
        dsh_ref[...] += dsh

    row = pl.BlockSpec((t, D), lambda i: (i, 0))
    vec = pl.BlockSpec((1, D), lambda i: (0, 0))
    return pl.pallas_call(
        body, grid=(s // t,), in_specs=[row, row, row, vec, vec, vec], out_specs=[row, vec, vec, vec],
        out_shape=[_sds((s, D), F32)] + [_sds((1, D), F32)] * 3,
        compiler_params=_cparams(("arbitrary",)), name=name)(x, dh, dres, g, scale, shift)


def gate_bwd(dx, y, gate, *, name):
    s = dx.shape[0]
    t = _row_tile(s)

    def body(dx_ref, y_ref, g_ref, dy_ref, dg_ref):
        @pl.when(pl.program_id(0) == 0)
        def _():
            dg_ref[...] = jnp.zeros_like(dg_ref)

        dxv = dx_ref[...]
        dy_ref[...] = (g_ref[...] * dxv).astype(dy_ref.dtype)
        dg_ref[...] += jnp.sum(dxv * y_ref[...], axis=0, keepdims=True)

    row = pl.BlockSpec((t, D), lambda i: (i, 0))
    vec = pl.BlockSpec((1, D), lambda i: (0, 0))
    return pl.pallas_call(
        body, grid=(s // t,), in_specs=[row, row, vec], out_specs=[row, vec],
        out_shape=[_sds((s, D), MMT), _sds((1, D), F32)],
        compiler_params=_cparams(("arbitrary",)), name=name)(dx, y, gate)


def loss_and_grad(xf, target, *, name):
    s = xf.shape[0]
    t = _row_tile(s)

    def body(x_ref, t_ref, l_ref, dx_ref):
        @pl.when(pl.program_id(0) == 0)
        def _():
            l_ref[...] = jnp.zeros_like(l_ref)

        e = x_ref[...] - t_ref[...]
        dx_ref[...] = e * (1.0 / D)
        l_ref[...] += 0.5 * jnp.sum(jnp.sum(e * e, axis=1, keepdims=True), axis=0, keepdims=True) * (1.0 / D)

    row = pl.BlockSpec((t, D), lambda i: (i, 0))
    return pl.pallas_call(
        body, grid=(s // t,), in_specs=[row, row], out_specs=[pl.BlockSpec((1, 1), lambda i: (0, 0)), row],
        out_shape=[_sds((1, 1), F32), _sds((s, D), F32)],
        compiler_params=_cparams(("arbitrary",)), name=name)(xf, target)


def _ret_consts():
    log_g = np.log1p(-np.exp2(-5.0 - np.arange(NH, dtype=np.float32))).astype(np.float32)
    idx = np.arange(CH, dtype=np.float32)
    d_intra = np.exp(np.abs(idx[:, None] - idx[None, :])[None] * log_g[:, None, None]).astype(np.float32)
    k_w = np.exp((CH - 1.0 - idx)[None, :] * log_g[:, None]).astype(np.float32)
    q_w = np.exp((idx + 1.0)[None, :] * log_g[:, None]).astype(np.float32)
    g_chunk = [float(v) for v in np.exp(np.float32(CH) * log_g).astype(np.float32)]
    bc = lambda a: np.ascontiguousarray(np.broadcast_to(a[:, :, None], (NH, CH, HD)))
    return jnp.asarray(d_intra), jnp.asarray(bc(k_w)), jnp.asarray(bc(q_w)), g_chunk


def _rope_tables(s):
    half = HD // 2
    inv_freq = (ROPE_BASE ** (-np.arange(half, dtype=np.float64) / half)).astype(np.float32)
    ang = (np.arange(s, dtype=np.float32)[:, None] * inv_freq[None, :]).astype(np.float64)
    cos, sin = np.cos(ang).astype(np.float32), np.sin(ang).astype(np.float32)
    return jnp.asarray(np.concatenate([cos, cos], axis=1)), jnp.asarray(np.concatenate([-sin, sin], axis=1))


def _ret_chunk(qs, ks, vs, rs, cos, sin, dintra, kw, qw, g_chunk):
    outs, rn = [], []
    for h in range(NH):
        q = qs[h] * cos + _swap_halves(qs[h]) * sin
        k = (ks[h] * cos + _swap_halves(ks[h]) * sin) * (HD ** -0.5)
        sc = _bdot_nt(q, k) * dintra[h]
        outs.append(_bdot(sc, vs[h]) + _bdot(q * qw[h], rs[h]))
        rn.append(g_chunk[h] * rs[h] + _bdot_tn(k * kw[h], vs[h]))
    return outs, rn


def _heads(x):
    return [x[:, h * HD:(h + 1) * HD] for h in range(NH)]


def _chunks_per_step(n):
    return 4 if n % 4 == 0 else 1


def retention_fwd(zz, cosf, sinf, *, name):
    s = zz.shape[0]
    n = s // CH
    nb = _chunks_per_step(n)
    rb = nb * CH
    dintra, kw, qw, g_chunk = _ret_consts()

    def body(q_ref, k_ref, v_ref, c_ref, s_ref, di_ref, kw_ref, qw_ref, o_ref, rp_ref, r_scr):
        @pl.when(pl.program_id(0) == 0)
        def _():
            r_scr[...] = jnp.zeros_like(r_scr)

        r = r_scr[...]
        consts = ([di_ref[h] for h in range(NH)], [kw_ref[h] for h in range(NH)], [qw_ref[h] for h in range(NH)], g_chunk)
        for i in range(nb):
            rows = slice(i * CH, (i + 1) * CH)
            rp_ref[i] = r
            outs, rn = _ret_chunk(_heads(q_ref[rows, :]), _heads(k_ref[rows, :]), _heads(v_ref[rows, :]),
                                  [r[h * HD:(h + 1) * HD] for h in range(NH)], c_ref[rows, :], s_ref[rows, :], *consts)
            o_ref[rows, :] = jnp.concatenate(outs, axis=1)
            r = jnp.concatenate(rn, axis=0)
        r_scr[...] = r

    col = lambda c: pl.BlockSpec((rb, BW), lambda i: (i, c // BW))
    tab = pl.BlockSpec((rb, HD), lambda i: (i, 0))
    cst = lambda shp: pl.BlockSpec(shp, lambda i: (0,) * len(shp))
    return pl.pallas_call(
        body, grid=(n // nb,),
        in_specs=[col(RQ), col(RK), col(RV), tab, tab, cst((NH, CH, CH)), cst((NH, CH, HD)), cst((NH, CH, HD))],
        out_specs=[pl.BlockSpec((rb, BW), lambda i: (i, 0)), pl.BlockSpec((nb, BW, HD), lambda i: (i, 0, 0))],
        out_shape=[_sds((s, BW), F32), _sds((n, BW, HD), F32)],
        scratch_shapes=[pltpu.VMEM((BW, HD), F32)],
        compiler_params=_cparams(("arbitrary",)), name=name)(zz, zz, zz, cosf, sinf, dintra, kw, qw)


def retention_bwd(zz, cosf, sinf, rprev, do, dzz, *, name):
    s = zz.shape[0]
    n = s // CH
    nb = _chunks_per_step(n)
    rb, steps = nb * CH, n // nb
    dintra, kw, qw, g_chunk = _ret_consts()

    def body(q_ref, k_ref, v_ref, c_ref, s_ref, di_ref, kw_ref, qw_ref, rp_ref, do_ref, dzz_ref, dz_ref, dr_scr):
        @pl.when(pl.program_id(0) == 0)
        def _():
            dr_scr[...] = jnp.zeros_like(dr_scr)

        dr = dr_scr[...]
        consts = dict(dintra=[di_ref[h] for h in range(NH)], kw=[kw_ref[h] for h in range(NH)],
                      qw=[qw_ref[h] for h in range(NH)], g_chunk=g_chunk)
        for i in reversed(range(nb)):
            rows = slice(i * CH, (i + 1) * CH)
            rprev_v = rp_ref[i]
            f = functools.partial(_ret_chunk, cos=c_ref[rows, :], sin=s_ref[rows, :], **consts)
            _, vjp = jax.vjp(f, _heads(q_ref[rows, :]), _heads(k_ref[rows, :]), _heads(v_ref[rows, :]),
                             [rprev_v[h * HD:(h + 1) * HD] for h in range(NH)])
            dq, dk, dv, drp = vjp((_heads(do_ref[rows, :]), [dr[h * HD:(h + 1) * HD] for h in range(NH)]))
            dz_ref[rows, :] = jnp.concatenate(dq + dk + dv, axis=1).astype(dz_ref.dtype)
            dr = jnp.concatenate(drp, axis=0)
        dr_scr[...] = dr

    col = lambda c: pl.BlockSpec((rb, BW), lambda i: (steps - 1 - i, c // BW))
    tab = pl.BlockSpec((rb, HD), lambda i: (steps - 1 - i, 0))
    cst = lambda shp: pl.BlockSpec(shp, lambda i: (0,) * len(shp))
    return pl.pallas_call(
        body, grid=(steps,),
        in_specs=[col(RQ), col(RK), col(RV), tab, tab, cst((NH, CH, CH)), cst((NH, CH, HD)), cst((NH, CH, HD)),
                  pl.BlockSpec((nb, BW, HD), lambda i: (steps - 1 - i, 0, 0)),
                  pl.BlockSpec((rb, BW), lambda i: (steps - 1 - i, 0)), pl.BlockSpec(memory_space=pl.ANY)],
        out_specs=pl.BlockSpec((rb, 3 * BW), lambda i: (steps - 1 - i, RQ // (3 * BW))),
        out_shape=_sds(dzz.shape, dzz.dtype), input_output_aliases={10: 0},
        scratch_shapes=[pltpu.VMEM((BW, HD), F32)],
        compiler_params=_cparams(("arbitrary",)), name=name)(zz, zz, zz, cosf, sinf, dintra, kw, qw, rprev, do, dzz)


GKW = NH * GDK


def _gla_consts():
    tri = np.tril(np.ones((CH, CH), np.float32))
    mask_t = np.zeros((BW, GKW), np.float32)
    for h in range(NH):
        mask_t[h * HD:(h + 1) * HD, h * GDK:(h + 1) * GDK] = 1.0
    return jnp.asarray(tri), jnp.asarray(mask_t)


def _gla_chunk(q, k, v, glr, w_a2, b_a, st, tri, mask_t):
    la = _log_sigmoid(_bdot(glr, w_a2) + b_a) * (1.0 / 16.0)
    bc = _dot(tri, la, HI)
    be = jnp.sum(la, axis=0, keepdims=True)
    kv_t = _bdot_tn(v, k * jnp.exp(be - bc)) * mask_t
    sn = jnp.exp(be) * st + kv_t
    return _bdot_nt(q * (GDK ** -0.5), sn), sn


def gla_fwd(zz, w_a2p, b_a, *, name):
    s = zz.shape[0]
    n = s // CH
    nb = _chunks_per_step(n)
    rb = nb * CH
    tri, mask_t = _gla_consts()

    def body(q_ref, k_ref, v_ref, lr_ref, w_ref, b_ref, tri_ref, m_ref, o_ref, sp_ref, st_scr):
        @pl.when(pl.program_id(0) == 0)
        def _():
            st_scr[...] = jnp.zeros_like(st_scr)

        st = st_scr[...]
        for i in range(nb):
            rows = slice(i * CH, (i + 1) * CH)
            sp_ref[i] = st
            o_ref[rows, :], st = _gla_chunk(q_ref[rows, :], k_ref[rows, :], v_ref[rows, :], lr_ref[rows, :], w_ref[...],
                                            b_ref[...], st, tri_ref[...], m_ref[...])
        st_scr[...] = st

    cst = lambda shp: pl.BlockSpec(shp, lambda i: (0,) * len(shp))
    return pl.pallas_call(
        body, grid=(n // nb,),
        in_specs=[pl.BlockSpec((rb, GKW), lambda i: (i, GQ // GKW)), pl.BlockSpec((rb, GKW), lambda i: (i, GK // GKW)),
                  pl.BlockSpec((rb, BW), lambda i: (i, GV // BW)), pl.BlockSpec((rb, HD), lambda i: (i, LR // HD)),
                  cst((HD, GKW)), cst((1, GKW)), cst((CH, CH)), cst((BW, GKW))],
        out_specs=[pl.BlockSpec((rb, BW), lambda i: (i, 0)), pl.BlockSpec((nb, BW, GKW), lambda i: (i, 0, 0))],
        out_shape=[_sds((s, BW), F32), _sds((n, BW, GKW), F32)],
        scratch_shapes=[pltpu.VMEM((BW, GKW), F32)],
        compiler_params=_cparams(("arbitrary",)), name=name)(zz, zz, zz, zz, w_a2p, b_a, tri, mask_t)


def gla_bwd(zz, w_a2p, b_a, sprev, do, dzz, *, name):
    s = zz.shape[0]
    n = s // CH
    nb = _chunks_per_step(n)
    rb, steps = nb * CH, n // nb
    tri, mask_t = _gla_consts()

    def body(q_ref, k_ref, v_ref, lr_ref, w_ref, b_ref, tri_ref, m_ref, sp_ref, do_ref, dzz_ref,
             dz_ref, dlr_ref, dw_ref, db_ref, ds_scr):
        @pl.when(pl.program_id(0) == 0)
        def _():
            ds_scr[...] = jnp.zeros_like(ds_scr)
            dw_ref[...] = jnp.zeros_like(dw_ref)
            db_ref[...] = jnp.zeros_like(db_ref)

        f = functools.partial(_gla_chunk, tri=tri_ref[...], mask_t=m_ref[...])
        ds, dw_sum, db_sum = ds_scr[...], jnp.zeros(dw_ref.shape, F32), jnp.zeros(db_ref.shape, F32)
        for i in reversed(range(nb)):
            rows = slice(i * CH, (i + 1) * CH)
            _, vjp = jax.vjp(f, q_ref[rows, :], k_ref[rows, :], v_ref[rows, :], lr_ref[rows, :], w_ref[...], b_ref[...],
                             sp_ref[i])
            dq, dk, dv, dlr, dw, db, ds = vjp((do_ref[rows, :], ds))
            dz_ref[rows, :] = jnp.concatenate([dq, dk, dv], axis=1).astype(dz_ref.dtype)
            dlr_ref[rows, :] = dlr.astype(dlr_ref.dtype)
            dw_sum, db_sum = dw_sum + dw, db_sum + db
        dw_ref[...] += dw_sum
        db_ref[...] += db_sum
        ds_scr[...] = ds

    cst = lambda shp: pl.BlockSpec(shp, lambda i: (0,) * len(shp))
    r = lambda i: steps - 1 - i
    return pl.pallas_call(
        body, grid=(steps,),
        in_specs=[pl.BlockSpec((rb, GKW), lambda i: (r(i), GQ // GKW)), pl.BlockSpec((rb, GKW), lambda i: (r(i), GK // GKW)),
                  pl.BlockSpec((rb, BW), lambda i: (r(i), GV // BW)), pl.BlockSpec((rb, HD), lambda i: (r(i), LR // HD)),
                  cst((HD, GKW)), cst((1, GKW)), cst((CH, CH)), cst((BW, GKW)),
                  pl.BlockSpec((nb, BW, GKW), lambda i: (r(i), 0, 0)), pl.BlockSpec((rb, BW), lambda i: (r(i), 0)),
                  pl.BlockSpec(memory_space=pl.ANY)],
        out_specs=[pl.BlockSpec((rb, 2 * GKW + BW), lambda i: (r(i), GQ // (2 * GKW + BW))),
                   pl.BlockSpec((rb, HD), lambda i: (r(i), 0)), cst((HD, GKW)), cst((1, GKW))],
        out_shape=[_sds(dzz.shape, dzz.dtype), _sds((s, HD), MMT), _sds((HD, GKW), F32), _sds((1, GKW), F32)],
        input_output_aliases={10: 0}, scratch_shapes=[pltpu.VMEM((BW, GKW), F32)],
        compiler_params=_cparams(("arbitrary",)), name=name)(zz, zz, zz, zz, w_a2p, b_a, tri, mask_t, sprev, do, dzz)


def _fox_pre_f(fqs, fks, ff, gq, gk, bf):
    def rms(x, g):
        return x * lax.rsqrt(jnp.mean(x * x, axis=-1, keepdims=True) + EPS) * g

    qn = [rms(x, gq) * (HD ** -0.5) for x in fqs]
    kn = [rms(x, gk) for x in fks]
    return qn, kn, _log_sigmoid(ff + bf)


def fox_pre(zz, gq, gk, bf, *, name):
    s = zz.shape[0]
    t = _row_tile(s)
    tri = jnp.asarray(np.tril(np.ones((t, t), np.float32)))

    def body(q_ref, k_ref, f_ref, gq_ref, gk_ref, b_ref, tri_ref, qn_ref, kn_ref, cum_ref, carry):
        @pl.when(pl.program_id(0) == 0)
        def _():
            carry[...] = jnp.zeros_like(carry)

        qn, kn, lf = _fox_pre_f(_heads(q_ref[...]), _heads(k_ref[...]), f_ref[...], gq_ref[...], gk_ref[...], b_ref[...])
        qn_ref[...] = jnp.concatenate(qn, axis=1).astype(qn_ref.dtype)
        kn_ref[...] = jnp.concatenate(kn, axis=1).astype(kn_ref.dtype)
        cum_ref[...] = _dot(tri_ref[...], lf, HI) + carry[...]
        carry[...] += jnp.sum(lf, axis=0, keepdims=True)

    vec = pl.BlockSpec((1, HD), lambda i: (0, 0))
    return pl.pallas_call(
        body, grid=(s // t,),
        in_specs=[pl.BlockSpec((t, BW), lambda i: (i, FQ // BW)), pl.BlockSpec((t, BW), lambda i: (i, FK // BW)),
                  pl.BlockSpec((t, HD), lambda i: (i, FF // HD)), vec, vec, vec, pl.BlockSpec((t, t), lambda i: (0, 0))],
        out_specs=[pl.BlockSpec((t, BW), lambda i: (i, 0)), pl.BlockSpec((t, BW), lambda i: (i, 0)),
                   pl.BlockSpec((t, HD), lambda i: (i, 0))],
        out_shape=[_sds((s, BW), MMT), _sds((s, BW), MMT), _sds((s, HD), F32)],
        scratch_shapes=[pltpu.VMEM((1, HD), F32)],
        compiler_params=_cparams(("arbitrary",)), name=name)(zz, zz, zz, gq, gk, bf, tri)


def fox_pre_bwd(zz, gq, gk, bf, dqn, dkn, dcum, dzz, *, name):
    s = zz.shape[0]
    t = _row_tile(s)
    nt = s // t
    triu = jnp.asarray(np.triu(np.ones((t, t), np.float32)))

    def body(q_ref, k_ref, f_ref, gq_ref, gk_ref, b_ref, tri_ref, dqn_ref, dkn_ref, dcum_ref, dzz_ref,
             dz_ref, dff_ref, dgq_ref, dgk_ref, db_ref, carry):
        @pl.when(pl.program_id(0) == 0)
        def _():
            carry[...] = jnp.zeros_like(carry)
            dgq_ref[...] = jnp.zeros_like(dgq_ref)
            dgk_ref[...] = jnp.zeros_like(dgk_ref)
            db_ref[...] = jnp.zeros_like(db_ref)

        dcum_v = dcum_ref[...]
        dlf = _dot(tri_ref[...], dcum_v, HI) + carry[...]
        carry[...] += jnp.sum(dcum_v, axis=0, keepdims=True)
        _, vjp = jax.vjp(_fox_pre_f, _heads(q_ref[...]), _heads(k_ref[...]), f_ref[...], gq_ref[...], gk_ref[...], b_ref[...])
        dq, dk, dff, dgq, dgk, db = vjp((_heads(dqn_ref[...]), _heads(dkn_ref[...]), dlf))
        dz_ref[...] = jnp.concatenate(dq + dk, axis=1).astype(dz_ref.dtype)
        dff_ref[...] = dff.astype(dff_ref.dtype)
        dgq_ref[...] += dgq
        dgk_ref[...] += dgk
        db_ref[...] += db

    r = lambda i: nt - 1 - i
    vec = pl.BlockSpec((1, HD), lambda i: (0, 0))
    return pl.pallas_call(
        body, grid=(nt,),
        in_specs=[pl.BlockSpec((t, BW), lambda i: (r(i), FQ // BW)), pl.BlockSpec((t, BW), lambda i: (r(i), FK // BW)),
                  pl.BlockSpec((t, HD), lambda i: (r(i), FF // HD)), vec, vec, vec, pl.BlockSpec((t, t), lambda i: (0, 0)),
                  pl.BlockSpec((t, BW), lambda i: (r(i), 0)), pl.BlockSpec((t, BW), lambda i: (r(i), 0)),
                  pl.BlockSpec((t, HD), lambda i: (r(i), 0)), pl.BlockSpec(memory_space=pl.ANY)],
        out_specs=[pl.BlockSpec((t, 2 * BW), lambda i: (r(i), FQ // (2 * BW))), pl.BlockSpec((t, HD), lambda i: (r(i), 0)),
                   vec, vec, vec],
        out_shape=[_sds(dzz.shape, dzz.dtype), _sds((s, HD), MMT), _sds((1, HD), F32), _sds((1, HD), F32), _sds((1, HD), F32)],
        input_output_aliases={10: 0}, scratch_shapes=[pltpu.VMEM((1, HD), F32)],
        compiler_params=_cparams(("arbitrary",)), name=name)(zz, zz, zz, gq, gk, bf, triu, dqn, dkn, dcum, dzz)


def _fox_blocks(s):
    return min(256, s), min(512, s)


NEG = -1e30


def fox_attn_fwd(qn, kn, zz, cum_col, cum_row, *, name):
    s = qn.shape[0]
    bq, bk = _fox_blocks(s)

    def body(q_ref, k_ref, v_ref, cc_ref, cr_ref, o_ref, lse_ref):
        qi = pl.program_id(1)
        q = q_ref[...]
        cq = cc_ref[...]
        rows = qi * bq + lax.broadcasted_iota(jnp.int32, (bq, bk), 0)
        cols0 = lax.broadcasted_iota(jnp.int32, (bq, bk), 1)

        def step(j, carry, on_diagonal):
            m, l, acc = carry
            off = pl.multiple_of(j * bk, bk)
            k = k_ref[pl.ds(off, bk), :]
            v = v_ref[pl.ds(off, bk), :].astype(MMT)
            sc = _dot_nt(q, k) + cq - cr_ref[pl.ds(j, 1), :]
            if on_diagonal:
                sc = jnp.where(rows >= cols0 + j * bk, sc, NEG)
            m_new = jnp.maximum(m, jnp.max(sc, axis=1, keepdims=True))
            alpha = jnp.exp(m - m_new)
            p = jnp.exp(sc - m_new)
            return m_new, alpha * l + jnp.sum(p, axis=1, keepdims=True), alpha * acc + _dot(p.astype(MMT), v)

        nfull, nk = (qi * bq + 1) // bk, ((qi + 1) * bq + bk - 1) // bk
        carry = (jnp.full((bq, 1), NEG, F32), jnp.zeros((bq, 1), F32), jnp.zeros((bq, HD), F32))
        carry = lax.fori_loop(0, nfull, functools.partial(step, on_diagonal=False), carry)
        m, l, acc = lax.fori_loop(nfull, nk, functools.partial(step, on_diagonal=True), carry)
        o_ref[...] = acc / l
        lse_ref[...] = m + jnp.log(l)

    return pl.pallas_call(
        body, grid=(NH, s // bq),
        in_specs=[pl.BlockSpec((bq, HD), lambda h, i: (i, h)), pl.BlockSpec((s, HD), lambda h, i: (0, h)),
                  pl.BlockSpec((s, HD), lambda h, i: (0, FV // HD + h)),
                  pl.BlockSpec((None, bq, 1), lambda h, i: (h, i, 0)), pl.BlockSpec((None, s // bk, bk), lambda h, i: (h, 0, 0))],
        out_specs=[pl.BlockSpec((bq, HD), lambda h, i: (i, h)), pl.BlockSpec((None, bq, 1), lambda h, i: (h, i, 0))],
        out_shape=[_sds((s, BW), F32), _sds((NH, s, 1), F32)],
        compiler_params=_cparams(("parallel", "parallel")), name=name)(qn, kn, zz, cum_col, cum_row)


def fox_attn_bwd(qn, kn, zz, cum_col, cum_row, lse, do, dzz, *, name):
    s = qn.shape[0]
    bq, bk = _fox_blocks(s)
    nkc = s // bk

    def body(q_ref, k_ref, v_ref, cc_ref, cr_ref, lse_ref, do_ref, dzz_ref, dq_ref, dk_ref, dv_ref, dc_ref,
             p_scr, dp_scr, dv_scr):
        qi = pl.program_id(1)

        @pl.when(qi == 0)
        def _():
            dk_ref[...] = jnp.zeros_like(dk_ref)
            dv_scr[...] = jnp.zeros_like(dv_scr)
            dc_ref[...] = jnp.zeros_like(dc_ref)

        q = q_ref[...]
        dob = do_ref[...].astype(MMT)
        cq = cc_ref[...]
        lse_v = lse_ref[...]
        rows = qi * bq + lax.broadcasted_iota(jnp.int32, (bq, bk), 0)
        cols0 = lax.broadcasted_iota(jnp.int32, (bq, bk), 1)
        nfull, nk = (qi * bq + 1) // bk, ((qi + 1) * bq + bk - 1) // bk

        def probs(j, delta, on_diagonal):
            off = pl.multiple_of(j * bk, bk)
            sc = _dot_nt(q, k_ref[pl.ds(off, bk), :]) + cq - cr_ref[pl.ds(j, 1), :]
            p = jnp.exp(sc - lse_v)
            if on_diagonal:
                p = jnp.where(rows >= cols0 + j * bk, p, 0.0)
            dp = _dot_nt(dob, v_ref[pl.ds(off, bk), :].astype(MMT))
            p_scr[j] = p
            dp_scr[j] = dp
            return delta + jnp.sum(p * dp, axis=1, keepdims=True)

        delta = lax.fori_loop(0, nfull, functools.partial(probs, on_diagonal=False), jnp.zeros((bq, 1), F32))
        delta = lax.fori_loop(nfull, nk, functools.partial(probs, on_diagonal=True), delta)

        def grads(j, dq):
            off = pl.multiple_of(j * bk, bk)
            p = p_scr[j]
            ds = p * (dp_scr[j] - delta)
            dsm = ds.astype(MMT)
            dv_scr[pl.ds(off, bk), :] += _dot_tn(p.astype(MMT), dob)
            dk_ref[pl.ds(off, bk), :] += _dot_tn(dsm, q)
            dc_ref[pl.ds(j, 1), :] -= jnp.sum(ds, axis=0, keepdims=True)
            return dq + _dot(dsm, k_ref[pl.ds(off, bk), :])

        dq_ref[...] = lax.fori_loop(0, nk, grads, jnp.zeros((bq, HD), F32))

        @pl.when(qi == pl.num_programs(1) - 1)
        def _():
            dv_ref[...] = dv_scr[...].astype(dv_ref.dtype)

    full = lambda c0=0: pl.BlockSpec((s, HD), lambda h, i: (0, c0 + h))
    blk = lambda: pl.BlockSpec((bq, HD), lambda h, i: (i, h))
    colv = lambda: pl.BlockSpec((None, bq, 1), lambda h, i: (h, i, 0))
    rowv = lambda: pl.BlockSpec((None, nkc, bk), lambda h, i: (h, 0, 0))
    return pl.pallas_call(
        body, grid=(NH, s // bq),
        in_specs=[blk(), full(), full(FV // HD), colv(), rowv(), colv(), blk(), pl.BlockSpec(memory_space=pl.ANY)],
        out_specs=[blk(), full(), full(FV // HD), rowv()],
        out_shape=[_sds((s, BW), F32), _sds((s, BW), F32), _sds(dzz.shape, dzz.dtype), _sds((NH, nkc, bk), F32)],
        input_output_aliases={7: 2},
        scratch_shapes=[pltpu.VMEM((nkc, bq, bk), F32), pltpu.VMEM((nkc, bq, bk), F32), pltpu.VMEM((s, HD), F32)],
        compiler_params=_cparams(("parallel", "arbitrary")), name=name)(qn, kn, zz, cum_col, cum_row, lse, do, dzz)


def _branch_f(rets, rgs, glas, ggs, ret_g, gla_g):
    out_r, out_g = [], []
    for h in range(NH):
        xc = rets[h] - jnp.mean(rets[h], axis=-1, keepdims=True)
        y = xc * lax.rsqrt(jnp.mean(xc * xc, axis=-1, keepdims=True) + EPS) * ret_g[h]
        out_r.append(_silu(rgs[h]) * y)
        x = glas[h]
        y = x * lax.rsqrt(jnp.mean(x * x, axis=-1, keepdims=True) + EPS) * gla_g
        out_g.append(_silu(ggs[h]) * y)
    return out_r, out_g


def _w_br_spec(layer):
    return pl.BlockSpec((None, 3, BW, D), lambda i: (layer, 0, 0, 0))


def mix_fwd(ret_raw, gla_raw, fox_o, zz, ret_g, gla_g, b_mg, w_br, *, name, layer):
    s = zz.shape[0]
    t = _row_tile(s)

    def body(r_ref, g_ref, f_ref, rg_ref, gg_ref, gp_ref, rgn_ref, ggn_ref, bmg_ref, w_ref, o_ref):
        rgn = rgn_ref[...]
        br_r, br_g = _branch_f(_heads(r_ref[...]), _heads(rg_ref[...]), _heads(g_ref[...]), _heads(gg_ref[...]),
                               _heads(rgn), ggn_ref[...])
        brs = [jnp.concatenate(br_r, axis=1), jnp.concatenate(br_g, axis=1), f_ref[...]]
        acc = jnp.zeros((t, D), F32)
        for b in range(3):
            gate = jax.nn.sigmoid(gp_ref[:, b * D:(b + 1) * D] + bmg_ref[:, b * D:(b + 1) * D])
            acc = acc + gate * _dot(brs[b].astype(MMT), w_ref[b])
        o_ref[...] = acc.astype(o_ref.dtype)

    row = lambda w, c=0: pl.BlockSpec((t, w), lambda i: (i, c // w))
    cst = lambda shp: pl.BlockSpec(shp, lambda i: (0,) * len(shp))
    return pl.pallas_call(
        body, grid=(s // t,),
        in_specs=[row(BW), row(BW), row(BW), row(BW, RG), row(BW, GG), row(3 * D, GP), cst((1, BW)), cst((1, HD)),
                  cst((1, 3 * D)), _w_br_spec(layer)],
        out_specs=row(D), out_shape=_sds((s, D), MMT),
        compiler_params=_cparams(("parallel",)), name=name)(ret_raw, gla_raw, fox_o, zz, zz, zz, ret_g, gla_g, b_mg, w_br)


def mix_bwd(ret_raw, gla_raw, fox_o, zz, ret_g, gla_g, b_mg, w_br, dmi, *, name, layer):
    s = zz.shape[0]
    t = _row_tile(s)

    def body(r_ref, g_ref, f_ref, rg_ref, gg_ref, gp_ref, rgn_ref, ggn_ref, bmg_ref, w_ref, dmi_ref,
             dr_ref, dg_ref, df_ref, dgp_ref, dw_ref, drgn_ref, dggn_ref, dbmg_ref):
        @pl.when(pl.program_id(0) == 0)
        def _():
            dw_ref[...] = jnp.zeros_like(dw_ref)
            drgn_ref[...] = jnp.zeros_like(drgn_ref)
            dggn_ref[...] = jnp.zeros_like(dggn_ref)
            dbmg_ref[...] = jnp.zeros_like(dbmg_ref)

        (br_r, br_g), vjp = jax.vjp(_branch_f, _heads(r_ref[...]), _heads(rg_ref[...]), _heads(g_ref[...]),
                                    _heads(gg_ref[...]), _heads(rgn_ref[...]), ggn_ref[...])
        brs = [jnp.concatenate(br_r, axis=1).astype(MMT), jnp.concatenate(br_g, axis=1).astype(MMT),
               f_ref[...].astype(MMT)]
        dmi_v = dmi_ref[...].astype(F32)
        dbr = []
        for b in range(3):
            w = w_ref[b]
            ybr = _dot(brs[b], w)
            gate = jax.nn.sigmoid(gp_ref[:, b * D:(b + 1) * D] + bmg_ref[:, b * D:(b + 1) * D])
            dgp = dmi_v * ybr * gate * (1.0 - gate)
            dgp_ref[:, b * D:(b + 1) * D] = dgp.astype(dgp_ref.dtype)
            dbmg_ref[:, b * D:(b + 1) * D] += jnp.sum(dgp, axis=0, keepdims=True)
            dy = (dmi_v * gate).astype(MMT)
            dw_ref[b] += _dot_tn(brs[b], dy)
            dbr.append(_dot_nt(dy, w))
        dr, drg, dg, dgg, drgn, dggn = vjp((_heads(dbr[0]), _heads(dbr[1])))
        dr_ref[...] = jnp.concatenate(dr, axis=1)
        dg_ref[...] = jnp.concatenate(dg, axis=1)
        df_ref[...] = dbr[2]
        dgp_ref[:, RG:RG + BW] = jnp.concatenate(drg, axis=1).astype(dgp_ref.dtype)
        dgp_ref[:, GG:GG + BW] = jnp.concatenate(dgg, axis=1).astype(dgp_ref.dtype)
        drgn_ref[...] += jnp.concatenate(drgn, axis=1)
        dggn_ref[...] += dggn

    row = lambda w, c=0: pl.BlockSpec((t, w), lambda i: (i, c // w))
    cst = lambda shp: pl.BlockSpec(shp, lambda i: (0,) * len(shp))
    return pl.pallas_call(
        body, grid=(s // t,),
        in_specs=[row(BW), row(BW), row(BW), row(BW, RG), row(BW, GG), row(3 * D, GP), cst((1, BW)), cst((1, HD)),
                  cst((1, 3 * D)), _w_br_spec(layer), row(D)],
        out_specs=[row(BW), row(BW), row(BW), row(FV), cst((3, BW, D)), cst((1, BW)), cst((1, HD)), cst((1, 3 * D))],
        out_shape=[_sds((s, BW), F32)] * 3 + [_sds((s, NZZ), MMT), _sds((3, BW, D), F32), _sds((1, BW), F32),
                                              _sds((1, HD), F32), _sds((1, 3 * D), F32)],
        compiler_params=_cparams(("arbitrary",)), name=name)(ret_raw, gla_raw, fox_o, zz, zz, zz, ret_g, gla_g, b_mg, w_br, dmi)


CT = 256


def _shift_down(x, k, rows):
    return jnp.where(rows >= k, pltpu.roll(x, k, 0), 0.0)


def _shift_up(x, k, rows, s):
    return jnp.where(rows < s - k, pltpu.roll(x, s - k, 0), 0.0)


def conv_fwd(ug, w_conv, b_conv, *, name):
    s = ug.shape[0]
    nt = DFF // CT

    def body(u_ref, g_ref, w_ref, b_ref, a_ref):
        u = u_ref[...]
        rows = lax.broadcasted_iota(jnp.int32, u.shape, 0)
        uc = b_ref[...] + w_ref[0:1, :] * _shift_down(u, 2, rows) + w_ref[1:2, :] * _shift_down(u, 1, rows) + w_ref[2:3, :] * u
        a_ref[...] = (_silu(uc) * g_ref[...]).astype(a_ref.dtype)

    return pl.pallas_call(
        body, grid=(nt,),
        in_specs=[pl.BlockSpec((s, CT), lambda j: (0, j)), pl.BlockSpec((s, CT), lambda j: (0, nt + j)),
                  pl.BlockSpec((3, CT), lambda j: (0, j)), pl.BlockSpec((1, CT), lambda j: (0, j))],
        out_specs=pl.BlockSpec((s, CT), lambda j: (0, j)), out_shape=_sds((s, DFF), MMT),
        compiler_params=_cparams(("parallel",)), name=name)(ug, ug, w_conv, b_conv)


def conv_bwd(ug, w_conv, b_conv, da, *, name):
    s = ug.shape[0]
    nt = DFF // CT

    def body(u_ref, g_ref, w_ref, b_ref, da_ref, du_ref, dg_ref, dw_ref, db_ref):
        u = u_ref[...]
        rows = lax.broadcasted_iota(jnp.int32, u.shape, 0)
        u2, u1 = _shift_down(u, 2, rows), _shift_down(u, 1, rows)
        uc = b_ref[...] + w_ref[0:1, :] * u2 + w_ref[1:2, :] * u1 + w_ref[2:3, :] * u
        sg = jax.nn.sigmoid(uc)
        da_v = da_ref[...]
        dg_ref[...] = (da_v * uc * sg).astype(dg_ref.dtype)
        duc = da_v * g_ref[...] * sg * (1.0 + uc * (1.0 - sg))
        du = w_ref[2:3, :] * duc + w_ref[1:2, :] * _shift_up(duc, 1, rows, s) + w_ref[0:1, :] * _shift_up(duc, 2, rows, s)
        du_ref[...] = du.astype(du_ref.dtype)
        dw_ref[0:1, :] = jnp.sum(duc * u2, axis=0, keepdims=True)
        dw_ref[1:2, :] = jnp.sum(duc * u1, axis=0, keepdims=True)
        dw_ref[2:3, :] = jnp.sum(duc * u, axis=0, keepdims=True)
        db_ref[...] = jnp.sum(duc, axis=0, keepdims=True)

    col = lambda: pl.BlockSpec((s, CT), lambda j: (0, j))
    return pl.pallas_call(
        body, grid=(nt,),
        in_specs=[col(), pl.BlockSpec((s, CT), lambda j: (0, nt + j)), pl.BlockSpec((3, CT), lambda j: (0, j)),
                  pl.BlockSpec((1, CT), lambda j: (0, j)), col()],
        out_specs=[col(), col(), pl.BlockSpec((3, CT), lambda j: (0, j)), pl.BlockSpec((1, CT), lambda j: (0, j))],
        out_shape=[_sds((s, DFF), MMT), _sds((s, DFF), MMT), _sds((3, DFF), F32), _sds((1, DFF), F32)],
        compiler_params=_cparams(("parallel",)), name=name)(ug, ug, w_conv, b_conv, da)


def place_tail(dzz, dlr, dff, *, name):
    s = dzz.shape[0]
    t = _row_tile(s)

    def body(a_ref, b_ref, z_ref, o_ref):
        o_ref[...] = jnp.concatenate([a_ref[...], b_ref[...]], axis=1)

    spec = pl.BlockSpec((t, HD), lambda i: (i, 0))
    return pl.pallas_call(
        body, grid=(s // t,), in_specs=[spec, spec, pl.BlockSpec(memory_space=pl.ANY)],
        out_specs=pl.BlockSpec((t, 2 * HD), lambda i: (i, LR // (2 * HD))), out_shape=_sds(dzz.shape, dzz.dtype),
        input_output_aliases={2: 0}, compiler_params=_cparams(("parallel",)), name=name)(dlr, dff, dzz)


def _tiles(s):
    return min(1024, s)


def layer_fwd(x, mod, p, cosf, sinf):
    s = x.shape[0]
    tm = _tiles(s)
    l = p["l"]
    shift1, scale1, gate1, shift2, scale2, gate2 = mod
    h = norm_mod(x, p["norm1_g"], scale1, shift1, name="norm_mod")
    zz = mm_nn(h, p["w1"], tm=tm, tn=768, out_dtype=F32, name="mm_w1", layer=l)
    ret_raw, rprev = retention_fwd(zz, cosf, sinf, name="ret_fwd")
    gla_raw, sprev = gla_fwd(zz, p["w_a2p"], p["b_gla_a"], name="gla_fwd")
    qn, kn, cum = fox_pre(zz, p["q_norm_g"], p["k_norm_g"], p["b_foxp"], name="fox_pre")
    bq, bk = _fox_blocks(s)
    cum_t = cum[:, :NH].T
    cum_col, cum_row = cum_t[:, :, None], cum_t.reshape(NH, s // bk, bk)
    fox_o, lse = fox_attn_fwd(qn, kn, zz, cum_col, cum_row, name="fox_fwd")
    if "later" in p:
        p = {**p, **p["later"](fox_o)}
    mi = mix_fwd(ret_raw, gla_raw, fox_o, zz, p["ret_norm_g"], p["gla_norm_g"], p["b_mg"], p["w_br"], name="mix_fwd",
                 layer=l)
    x1, mixed = mm_nn_residual(mi, p["w_o"], x, gate1, tm=tm, tn=512, name="mm_wo", layer=l)
    h2 = norm_mod(x1, p["norm2_g"], scale2, shift2, name="norm_mod")
    ug = mm_nn(h2, p["w_up"], tm=tm, tn=512, out_dtype=F32, name="mm_wup", layer=l)
    a = conv_fwd(ug, p["w_conv"], p["b_conv"], name="conv_fwd")
    x2, y = mm_nn_residual(a, p["w_down"], x1, gate2, tm=tm, tn=512, name="mm_wdown", layer=l)
    saved = dict(x=x, h=h, zz=zz, ret_raw=ret_raw, rprev=rprev, gla_raw=gla_raw, sprev=sprev, qn=qn, kn=kn,
                 cum_col=cum_col, cum_row=cum_row, fox_o=fox_o, lse=lse, mi=mi, mixed=mixed, x1=x1, h2=h2, ug=ug, a=a, y=y)
    return x2, saved, p


def layer_bwd(dx2, mod, p, sv, cosf, sinf, stacks, slot):
    s = dx2.shape[0]
    tm = _tiles(s)
    l = p["l"]
    shift1, scale1, gate1, shift2, scale2, gate2 = mod
    g, stacks = {}, dict(stacks)
    dy, dgate2 = gate_bwd(dx2, sv["y"], gate2, name="gate_bwd")
    stacks["w_down"] = mm_tn(sv["a"], dy, tm=min(1408, DFF), tn=512, out_dtype=MMT, name="mm_dwdown",
                             stack=stacks["w_down"], layer=slot)
    da = mm_nt(dy, p["w_down"], tm=tm, tn=1408, out_dtype=F32, name="mm_da", layer=l)
    du, dg, g["w_conv"], g["b_conv"] = conv_bwd(sv["ug"], p["w_conv"], p["b_conv"], da, name="conv_bwd")
    stacks["w_up"] = mm_tn(sv["h2"], du, tm=D, tn=CT, out_dtype=MMT, name="mm_dwup_u", stack=stacks["w_up"], layer=slot)
    stacks["w_up"] = mm_tn(sv["h2"], dg, tm=D, tn=CT, out_dtype=MMT, name="mm_dwup_g", stack=stacks["w_up"], layer=slot,
                           out_col0=DFF)
    dh2 = mm_nt2(du, dg, p["w_up"], tm=min(512, s), tn=D, name="mm_dh2", layer=l)
    dx1, g["norm2_g"], dscale2, dshift2 = norm_mod_bwd(sv["x1"], dh2, dx2, p["norm2_g"], scale2, shift2, name="norm_mod_bwd")
    dmixed, dgate1 = gate_bwd(dx1, sv["mixed"], gate1, name="gate_bwd")
    stacks["w_o"] = mm_tn(sv["mi"], dmixed, tm=512, tn=512, out_dtype=MMT, name="mm_dwo", stack=stacks["w_o"], layer=slot)
    dmi = mm_nt(dmixed, p["w_o"], tm=tm, tn=512, out_dtype=MMT, name="mm_dmi", layer=l)
    zz = sv["zz"]
    (dret, dgla, dfox, dzz, g["w_br"], g["ret_norm_g"], g["gla_norm_g"], g["b_mg"]) = mix_bwd(
        sv["ret_raw"], sv["gla_raw"], sv["fox_o"], zz, p["ret_norm_g"], p["gla_norm_g"], p["b_mg"], p["w_br"], dmi,
        name="mix_bwd", layer=l)
    dqn, dkn, dzz, dcum_row = fox_attn_bwd(sv["qn"], sv["kn"], zz, sv["cum_col"], sv["cum_row"], sv["lse"], dfox, dzz,
                                           name="fox_bwd")
    dcum = jnp.pad(dcum_row.reshape(NH, s).T, ((0, 0), (0, HD - NH)))
    dzz, dff, g["q_norm_g"], g["k_norm_g"], g["b_foxp"] = fox_pre_bwd(
        zz, p["q_norm_g"], p["k_norm_g"], p["b_foxp"], dqn, dkn, dcum, dzz, name="fox_pre_bwd")
    dzz, dlr, g["w_a2p"], g["b_gla_a"] = gla_bwd(zz, p["w_a2p"], p["b_gla_a"], sv["sprev"], dgla, dzz, name="gla_bwd")
    dzz = retention_bwd(zz, cosf, sinf, sv["rprev"], dret, dzz, name="ret_bwd")
    dzz = place_tail(dzz, dlr, dff, name="place_tail")
    stacks["w_mg"] = mm_tn(sv["h"], dzz, tm=512, tn=768, out_dtype=MMT, name="mm_dwmg", ncols=WZ0, stack=stacks["w_mg"],
                           layer=slot)
    dwz = mm_tn(sv["h"], dzz, tm=512, tn=768, out_dtype=MMT, name="mm_dwz", col0=WZ0)
    stacks["w_in"] = unalign_dw_in(dwz, stacks["w_in"], slot)
    dh = mm_nt(dzz, p["w1"], tm=min(256, s), tn=D, out_dtype=F32, name="mm_dh", layer=l)
    dx, g["norm1_g"], dscale1, dshift1 = norm_mod_bwd(sv["x"], dh, dx1, p["norm1_g"], scale1, shift1, name="norm_mod_bwd")
    dmod = jnp.concatenate([dshift1, dscale1, dgate1, dshift2, dscale2, dgate2], axis=1)
    return dx, g, dmod, stacks


def _align_cols(w_in, w_mg):
    z = lambda n: jnp.zeros((w_in.shape[0], n), w_in.dtype)
    seg = lambda name: w_in[:, W_IN_COLS[name][0]:W_IN_COLS[name][1]]
    return jnp.concatenate([w_mg, seg("rg"), seg("gg"), seg("fv"), seg("rqkv"), seg("gqkv"), seg("fqk"), seg("lr"),
                            z(HD - GLR), seg("ff"), z(HD - NH)], axis=1)


def _unalign_cols(dwz):
    seg = lambda c0, name: dwz[:, c0 - WZ0:c0 - WZ0 + W_IN_COLS[name][1] - W_IN_COLS[name][0]]
    return jnp.concatenate([seg(RQ, "rqkv"), seg(RG, "rg"), seg(GQ, "gqkv"), seg(LR, "lr"), seg(GG, "gg"), seg(FQ, "fqk"),
                            seg(FV, "fv"), seg(FF, "ff")], axis=1)


def build_w1(w_in_sh, w_mg):
    nl = w_mg.shape[0]
    t = _row_tile(D)

    def body(s_ref, g_ref, o_ref):
        o_ref[...] = _align_cols(jnp.concatenate([s_ref[k] for k in range(4)], axis=1), g_ref[...])

    return pl.pallas_call(
        body, grid=(nl, D // t),
        in_specs=[pl.BlockSpec((None, 4, t, IN_W // 4), lambda l, i: (l, 0, i, 0)), pl.BlockSpec((None, t, WZ0), lambda l, i: (l, i, 0))],
        out_specs=pl.BlockSpec((None, t, NZZ), lambda l, i: (l, i, 0)), out_shape=_sds((nl, D, NZZ), w_mg.dtype),
        compiler_params=_cparams(("parallel", "parallel")), name="build_w1")(w_in_sh, w_mg)


def unalign_dw_in(dwz, stack, layer):
    t = _row_tile(D)

    def body(z_ref, s_ref, o_ref):
        w = _unalign_cols(z_ref[...])
        for k in range(4):
            o_ref[k] = w[:, k * (IN_W // 4):(k + 1) * (IN_W // 4)]

    return pl.pallas_call(
        body, grid=(D // t,),
        in_specs=[pl.BlockSpec((t, NZZ - WZ0), lambda i: (i, 0)), pl.BlockSpec(memory_space=pl.ANY)],
        out_specs=pl.BlockSpec((None, 4, t, IN_W // 4), lambda i: (layer, 0, i, 0)), out_shape=_sds(stack.shape, stack.dtype),
        input_output_aliases={1: 0}, compiler_params=_cparams(("parallel",)), name="unalign_dw_in")(dwz, stack)


LATE_WEIGHTS = ("w_br", "w_o", "w_up", "w_down")


def layer_params(w, big, l, later=None):
    row = lambda v: v[l][None, :]
    p = dict(
        l=0, norm1_g=row(w["norm1_g"]), norm2_g=row(w["norm2_g"]), w1=big["w1"],
        w_a2p=jnp.pad(w["w_gla_a2"][l], ((0, HD - GLR), (0, 0))), b_gla_a=row(w["b_gla_a"]),
        b_foxp=jnp.pad(row(w["b_fox_f"]), ((0, 0), (0, HD - NH))), ret_norm_g=row(w["ret_norm_g"]),
        gla_norm_g=row(w["gla_norm_g"]), q_norm_g=row(w["q_norm_g"]), k_norm_g=row(w["k_norm_g"]),
        b_mg=row(w["b_mg"]), w_conv=w["w_conv"][l], b_conv=row(w["b_conv"]))
    if later is None:
        p.update({n: big[n] for n in LATE_WEIGHTS})
    else:
        p["later"] = later
    return p


def layer_grads(g):
    vec = lambda v: v[0]
    return dict(
        norm1_g=vec(g["norm1_g"]), norm2_g=vec(g["norm2_g"]), w_gla_a2=g["w_a2p"][:GLR], b_gla_a=vec(g["b_gla_a"]),
        b_fox_f=g["b_foxp"][0, :NH], ret_norm_g=vec(g["ret_norm_g"]), gla_norm_g=vec(g["gla_norm_g"]),
        q_norm_g=vec(g["q_norm_g"]), k_norm_g=vec(g["k_norm_g"]), w_br=g["w_br"], b_mg=vec(g["b_mg"]),
        w_conv=g["w_conv"], b_conv=vec(g["b_conv"]))


def ada_mod(c_all, w_ada, b_ada):
    nl, _, n = w_ada.shape

    def body(c_ref, w_ref, b_ref, o_ref):
        o_ref[...] = _dot(_silu(c_ref[...]), w_ref[...], HI) + b_ref[...]

    return pl.pallas_call(
        body, grid=(nl,),
        in_specs=[pl.BlockSpec((8, D), lambda l: (0, 0)), pl.BlockSpec((None, D, n), lambda l: (l, 0, 0)),
                  pl.BlockSpec((None, 1, n), lambda l: (l, 0, 0))],
        out_specs=pl.BlockSpec((None, 8, n), lambda l: (l, 0, 0)), out_shape=_sds((nl, 8, n), F32),
        compiler_params=_cparams(("parallel",)), name="ada_mod")(c_all, w_ada, b_ada)


def ada_dw(c_all, dmod):
    nl, _, n = dmod.shape

    def body(c_ref, d_ref, o_ref):
        o_ref[...] = _dot_tn(_silu(c_ref[...]), d_ref[...], HI)

    return pl.pallas_call(
        body, grid=(nl,),
        in_specs=[pl.BlockSpec((8, D), lambda l: (0, 0)), pl.BlockSpec((None, 8, n), lambda l: (l, 0, 0))],
        out_specs=pl.BlockSpec((None, D, n), lambda l: (l, 0, 0)), out_shape=_sds((nl, D, n), F32),
        compiler_params=_cparams(("parallel",)), name="ada_dw")(c_all, dmod)


def sum_devices(g):
    def body(g_ref, o_ref):
        acc = g_ref[0]
        for d in range(1, 8):
            acc = acc + g_ref[d]
        o_ref[...] = acc

    return pl.pallas_call(body, out_shape=_sds(g.shape[1:], F32), name="sum_devices")(g)


def adamw(w, g, m, v, *, block, name, rows=None, into=None):
    nd = w.ndim
    lo, hi = (0, w.shape[0]) if rows is None else rows
    grid = ((hi - lo) // block[0],) + tuple(w.shape[i] // block[i] for i in range(1, nd))
    first = lo // block[0]
    bc1 = 1.0 - ADAM_B1 ** ADAM_STEP
    bc2 = 1.0 - ADAM_B2 ** ADAM_STEP

    def body(w_ref, g_ref, m_ref, v_ref, *rest):
        d_ref, nm_ref, nv_ref = rest[-3:]
        gv = g_ref[...]
        nm = ADAM_B1 * m_ref[...] + (1.0 - ADAM_B1) * gv
        nv = ADAM_B2 * v_ref[...] + (1.0 - ADAM_B2) * (gv * gv)
        nm_ref[...] = nm
        nv_ref[...] = nv
        d_ref[...] = -ADAM_LR * ((nm / bc1) / (jnp.sqrt(nv / bc2) + ADAM_EPS) + ADAM_WD * w_ref[...])

    spec = pl.BlockSpec(tuple(block), lambda i, *j: (first + i,) + j)
    given = [] if into is None else list(into)
    return pl.pallas_call(
        body, grid=grid, in_specs=[spec] * 4 + [pl.BlockSpec(memory_space=pl.ANY)] * len(given), out_specs=[spec] * 3,
        out_shape=[_sds(w.shape, F32)] * 3, input_output_aliases={4 + i: i for i in range(len(given))},
        compiler_params=_cparams(("parallel",) * nd), name=name)(w, g, m, v, *given)


MESH = pl.DeviceIdType.MESH
ANY = pl.BlockSpec(memory_space=pl.ANY)
VM = pl.BlockSpec(memory_space=pltpu.VMEM)


def _place():
    x, y, c = lax.axis_index("x"), lax.axis_index("y"), lax.axis_index("c")
    return x, y, c, [(1 - x, y), (x, 1 - y), (1 - x, 1 - y)]


def small_allgather(v, *, name):
    m_per, n = v.shape

    def body(x_ref, out_ref, send_sems, recv_sems, local_sem):
        x, y, c, chips = _place()
        me, sibling = (x, y, c), (x, y, 1 - c)

        def rows(px, py, pc):
            return out_ref.at[pl.ds((4 * px + 2 * py + pc) * m_per, m_per), :]

        def copy(k, block, to, src=None):
            return pltpu.make_async_remote_copy(
                src_ref=rows(*block) if src is None else src, dst_ref=rows(*block),
                send_sem=send_sems.at[k], recv_sem=recv_sems.at[k], device_id=to, device_id_type=MESH)

        mine = pltpu.make_async_copy(x_ref, rows(*me), local_sem)
        mine.start()
        first = [copy(0, me, sibling, src=x_ref)]
        first += [copy(1 + j, me, (*chip, c), src=x_ref) for j, chip in enumerate(chips)]
        for cp in first:
            cp.start()
        passed = [copy(4 + j, (*chip, c), sibling) for j, chip in enumerate(chips)]
        for j, chip in enumerate(chips):
            copy(1 + j, (*chip, c), me).wait_recv()
            passed[j].start()
        copy(0, sibling, me).wait_recv()
        for j, chip in enumerate(chips):
            copy(4 + j, (*chip, 1 - c), me).wait_recv()
        for cp in first + passed:
            cp.wait_send()
        mine.wait()

    return pl.pallas_call(
        body, out_shape=_sds((8 * m_per, n), v.dtype), in_specs=[VM], out_specs=VM,
        scratch_shapes=[pltpu.SemaphoreType.DMA((7,)), pltpu.SemaphoreType.DMA((7,)), pltpu.SemaphoreType.DMA],
        name=name)(v)


TENSORS = {
    "w_in": ("lead", None, (4, D, 1285), (1, D, 1285)),
    "w_mg": ("col", 768, (D, 3072), (512, 3072)),
    "w_br": ("col", 256, (3, BW, D), (3, BW, D)),
    "w_o": ("row", 256, (D, D), (D, D)),
    "w_up": ("col", 1408, (D, 5632), (256, 5632)),
    "w_down": ("row", 704, (DFF, D), (704, D)),
}
BIG = tuple(TENSORS)


def _shard_shape(name):
    kind, width, full, _ = TENSORS[name]
    if kind == "lead":
        return full[1:]
    return full[:-1] + (width,) if kind == "col" else (width,) + full[1:]


def _shard_view(ref, layers, name, k):
    kind, width, full, _ = TENSORS[name]
    if kind == "lead":
        return ref.at[layers, k]
    if kind == "row":
        return ref.at[layers, pl.ds(k * width, width)]
    return ref.at[(layers,) + (slice(None),) * (len(full) - 1) + (pl.ds(k * width, width),)]


def _remote(send_sems, recv_sems, k, src, dst, to):
    return pltpu.make_async_remote_copy(src_ref=src, dst_ref=dst, send_sem=send_sems.at[k], recv_sem=recv_sems.at[k],
                                        device_id=to, device_id_type=MESH)


def _dma_sems(n):
    return [pltpu.SemaphoreType.DMA((n,)), pltpu.SemaphoreType.DMA((n,))]


RS_GROUP = 2
HBM = pl.BlockSpec(memory_space=pltpu.HBM)
SEM = pl.BlockSpec(memory_space=pltpu.SEMAPHORE)
SPLIT_CALL = dict(compiler_params=pltpu.CompilerParams(has_side_effects=pltpu.SideEffectType.DATAFLOW_SIDE_EFFECTING))
PULL_SET = (("w_mg", "w_up", "w_o"), ("w_in", "w_br", "w_down"))
FIRST_NEEDED = ("w_in", "w_mg")
NEEDED_LATER = tuple(n for n in BIG if n not in FIRST_NEEDED)


def _in_hbm(a):
    return pltpu.with_memory_space_constraint(a, pltpu.HBM)


def _pull_sends(send_sems, recv_sems, p, o, layer, core, x, y, chips, names=BIG):
    return [_remote(send_sems, recv_sems, 3 * BIG.index(n) + j, p[n].at[layer], _shard_view(o[n], 0, n, 2 * x + y), (*chip, core))
            for n in PULL_SET[core] if n in names for j, chip in enumerate(chips)]


def _pull_arrivals(send_sems, recv_sems, o, core, x, y, chips, to, names=BIG):
    views = [(3 * BIG.index(n) + j, _shard_view(o[n], 0, n, 2 * chip[0] + chip[1]))
             for n in PULL_SET[core] if n in names for j, chip in enumerate(chips)]
    return [_remote(send_sems, recv_sems, k, v, v, to) for k, v in views]


def gather_start(shards, layer, *, name, after=None):
    nt = len(BIG)
    first = [] if after is None else [after]

    def body(*refs):
        p, o = dict(zip(BIG, refs[:nt])), dict(zip(BIG, refs[nt:2 * nt]))
        ss, rs = refs[2 * nt + len(first)], refs[2 * nt + len(first) + 1]
        x, y, c, chips = _place()
        for core in (0, 1):
            @pl.when(c == core)
            def _():
                for cp in _pull_sends(ss, rs, p, o, layer, core, x, y, chips):
                    cp.start()
        refs[-1][...] = jnp.zeros_like(refs[-1])

    lands = [_in_hbm(lax.empty((1,) + TENSORS[n][2], shards[n].dtype)) for n in BIG]
    outs = pl.pallas_call(
        body,
        out_shape=(pltpu.SemaphoreType.DMA((3 * nt,)), pltpu.SemaphoreType.DMA((3 * nt,)),
                   *[pltpu.HBM(a.shape, a.dtype) for a in lands], _sds((8, HD), F32)),
        in_specs=[HBM] * (2 * nt) + [ANY] * len(first), out_specs=(SEM, SEM, *[HBM] * nt, VM),
        input_output_aliases={nt + t: 2 + t for t in range(nt)}, name=name, **SPLIT_CALL)(
            *[_in_hbm(shards[n]) for n in BIG], *lands, *first)
    return outs[0], outs[1], outs[2:2 + nt], outs[-1]


def gather_wait(send_sems, recv_sems, shards, lands, after, layer, *, name, names=BIG):
    nt = len(BIG)

    def body(*refs):
        p, o = dict(zip(BIG, refs[:nt])), dict(zip(BIG, refs[nt:2 * nt]))
        ss, rs = refs[2 * nt], refs[2 * nt + 1]
        x, y, c, chips = _place()
        for core in (0, 1):
            @pl.when(c == core)
            def _():
                for cp in _pull_sends(ss, rs, p, o, layer, core, x, y, chips, names):
                    cp.wait_send()
                for cp in _pull_arrivals(ss, rs, o, core, x, y, chips, (x, y, core), names):
                    cp.wait_recv()

    return pl.pallas_call(
        body, out_shape=tuple(pltpu.HBM(a.shape, a.dtype) for a in lands),
        in_specs=[HBM] * (2 * nt) + [SEM, SEM, ANY], out_specs=tuple([HBM] * nt),
        input_output_aliases={nt + t: t for t in range(nt)}, name=name, **SPLIT_CALL)(
            *[_in_hbm(shards[n]) for n in BIG], *lands, send_sems, recv_sems, after)


def gather_forward(shards, lands, layer, *, name, names=BIG):
    nt = len(BIG)

    def body(*refs):
        p, o = dict(zip(BIG, refs[:nt])), dict(zip(BIG, refs[2 * nt:3 * nt]))
        ss, rs = refs[3 * nt:]
        x, y, c, chips = _place()
        for core in (0, 1):
            @pl.when(c == core)
            def _():
                me, sibling = (x, y, core), (x, y, 1 - core)
                sends = _pull_arrivals(ss, rs, o, core, x, y, chips, sibling, names)
                sends += [_remote(ss, rs, 3 * nt + t, p[n].at[layer], _shard_view(o[n], 0, n, 2 * x + y), sibling)
                          for t, n in enumerate(BIG) if n in names]
                for cp in sends:
                    cp.start()
                for cp in sends:
                    cp.wait_send()
                for cp in _pull_arrivals(ss, rs, o, 1 - core, x, y, chips, me, names):
                    cp.wait_recv()
                for t, n in enumerate(BIG):
                    if n in names:
                        own = _shard_view(o[n], 0, n, 2 * x + y)
                        _remote(ss, rs, 3 * nt + t, own, own, me).wait_recv()

    outs = pl.pallas_call(
        body, out_shape=[_sds(a.shape, a.dtype) for a in lands], in_specs=[ANY] * (2 * nt), out_specs=[ANY] * nt,
        input_output_aliases={nt + t: t for t in range(nt)}, scratch_shapes=_dma_sems(4 * nt), name=name)(
            *[shards[n] for n in BIG], *lands)
    return dict(zip(BIG, outs))


def pair_exchange(g, *, name):
    hh = g[BIG[0]].shape[0] // 2
    nt = len(BIG)

    def body(*refs):
        send_sems, recv_sems = refs[2 * nt:]
        x, y, c, _ = _place()
        copies = [_remote(send_sems, recv_sems, t, refs[t].at[pl.ds(hh * (1 - c), hh)], refs[nt + t], (x, y, 1 - c))
                  for t in range(nt)]
        for cp in copies:
            cp.start()
        for cp in copies:
            cp.wait()

    outs = pl.pallas_call(
        body, out_shape=[_sds((hh,) + g[n].shape[1:], g[n].dtype) for n in BIG], in_specs=[ANY] * nt, out_specs=[ANY] * nt,
        scratch_shapes=_dma_sems(nt), name=name)(*[g[n] for n in BIG])
    return dict(zip(BIG, outs))


def _chip_copies(send_sems, recv_sems, s_refs, land_refs, c, chips):
    hl = s_refs[0].shape[0]
    return [_remote(send_sems, recv_sems, 3 * t + j, _shard_view(s_refs[t], pl.ds(0, hl), n, 2 * chip[0] + chip[1]),
                    land_refs[t].at[j], (*chip, c))
            for t, n in enumerate(BIG) for j, chip in enumerate(chips)]


def _landing_shapes(s):
    hl = s[BIG[0]].shape[0]
    return [_sds((3, hl) + _shard_shape(n), s[n].dtype) for n in BIG]


def chip_exchange(s, *, name):
    nt = len(BIG)

    def body(*refs):
        send_sems, recv_sems = refs[2 * nt:]
        x, y, c, chips = _place()
        copies = _chip_copies(send_sems, recv_sems, refs[:nt], refs[nt:2 * nt], c, chips)
        for cp in copies:
            cp.start()
        for cp in copies:
            cp.wait()

    outs = pl.pallas_call(
        body, out_shape=_landing_shapes(s), in_specs=[ANY] * nt, out_specs=[ANY] * nt,
        scratch_shapes=_dma_sems(3 * nt), name=name)(*[s[n] for n in BIG])
    return dict(zip(BIG, outs))


def chip_exchange_start(s, *, name, after=None):
    nt = len(BIG)
    first = [] if after is None else [after]

    def body(*refs):
        o = refs[2 * nt + len(first):]
        x, y, c, chips = _place()
        for cp in _chip_copies(o[0], o[1], refs[:nt], refs[nt:2 * nt], c, chips):
            cp.start()
        refs[-1][...] = jnp.zeros_like(refs[-1])

    lands = [_in_hbm(lax.empty(d.shape, d.dtype)) for d in _landing_shapes(s)]
    srcs = [_in_hbm(s[n]) for n in BIG]
    outs = pl.pallas_call(
        body,
        out_shape=(pltpu.SemaphoreType.DMA((3 * nt,)), pltpu.SemaphoreType.DMA((3 * nt,)),
                   *[pltpu.HBM(a.shape, a.dtype) for a in srcs + lands], _sds((8, HD), F32)),
        in_specs=[HBM] * (2 * nt) + [ANY] * len(first), out_specs=(SEM, SEM, *[HBM] * (2 * nt), VM),
        input_output_aliases={t: 2 + t for t in range(2 * nt)}, name=name, **SPLIT_CALL)(*srcs, *lands, *first)
    return outs[0], outs[1], outs[2:2 + nt], outs[2 + nt:2 + 2 * nt], outs[-1]


def chip_exchange_wait(send_sems, recv_sems, srcs, lands, after, *, name):
    nt = len(BIG)

    def body(*refs):
        x, y, c, chips = _place()
        for cp in _chip_copies(refs[2 * nt], refs[2 * nt + 1], refs[:nt], refs[nt:2 * nt], c, chips):
            cp.wait_send()
            cp.wait_recv()

    outs = pl.pallas_call(
        body, out_shape=tuple(pltpu.HBM(a.shape, a.dtype) for a in list(srcs) + list(lands)),
        in_specs=[HBM] * (2 * nt) + [SEM, SEM] + [ANY] * len(after), out_specs=tuple([HBM] * (2 * nt)),
        input_output_aliases={t: t for t in range(2 * nt)}, name=name, **SPLIT_CALL)(
            *srcs, *lands, send_sems, recv_sems, *after)
    return dict(zip(BIG, outs[:nt])), dict(zip(BIG, outs[nt:]))


def pair_share(f, l0, hh, *, name):
    nt = len(BIG)

    def body(*refs):
        o = refs[nt:2 * nt]
        send_sems, recv_sems = refs[2 * nt:]
        x, y, c, _ = _place()
        mine, theirs = pl.ds(l0 + hh * c, hh), pl.ds(l0 + hh * (1 - c), hh)
        copies = [_remote(send_sems, recv_sems, t, o[t].at[mine], o[t].at[mine], (x, y, 1 - c)) for t in range(nt)]
        for cp in copies:
            cp.start()
        for t, cp in enumerate(copies):
            cp.wait_send()
            _remote(send_sems, recv_sems, t, o[t].at[theirs], o[t].at[theirs], (x, y, c)).wait_recv()

    outs = pl.pallas_call(
        body, out_shape=[_sds(f[n].shape, f[n].dtype) for n in BIG], in_specs=[ANY] * nt, out_specs=[ANY] * nt,
        input_output_aliases={t: t for t in range(nt)}, scratch_shapes=_dma_sems(nt), name=name)(*[f[n] for n in BIG])
    return dict(zip(BIG, outs))


def pair_add(g, r, idx, *, tensor, name):
    _, _, full, blk = TENSORS[tensor]
    hh = r.shape[0]

    def body(idx_ref, g_ref, r_ref, o_ref):
        o_ref[...] = (g_ref[...].astype(F32) + r_ref[...].astype(F32)).astype(o_ref.dtype)

    own = pl.BlockSpec((None,) + blk, lambda *a: (a[0],) + a[1:-1])
    return pl.pallas_call(
        body, out_shape=_sds(r.shape, r.dtype),
        grid_spec=pltpu.PrefetchScalarGridSpec(
            num_scalar_prefetch=1, grid=(hh,) + tuple(f // b for f, b in zip(full, blk)),
            in_specs=[pl.BlockSpec((None,) + blk, lambda *a: (hh * a[-1][0] + a[0],) + a[1:-1]), own], out_specs=own),
        compiler_params=_cparams(("parallel",) * (1 + len(full))), name=name)(idx, g, r)


def chip_add(s, r, idx, totals, l0, *, tensor, name):
    kind, width, full, _ = TENSORS[tensor]
    shard = _shard_shape(tensor)
    hh = s.shape[0]
    zeros = (0,) * len(shard)

    def body(idx_ref, s_ref, r0_ref, r1_ref, r2_ref, t_ref, o_ref):
        o_ref[...] = ((s_ref[...].astype(F32) + r0_ref[...].astype(F32)) + r1_ref[...].astype(F32)) + r2_ref[...].astype(F32)

    if kind == "lead":
        mine = pl.BlockSpec((None, None) + shard, lambda i, ix: (i, ix[1]) + zeros)
    elif kind == "row":
        mine = pl.BlockSpec((None,) + shard, lambda i, ix: (i, ix[1]) + zeros[1:])
    else:
        mine = pl.BlockSpec((None,) + shard, lambda i, ix: (i,) + zeros[1:] + (ix[1],))
    peer = lambda j: pl.BlockSpec((None, None) + shard, lambda i, ix: (j, i) + zeros)
    return pl.pallas_call(
        body, out_shape=_sds(totals.shape, F32),
        grid_spec=pltpu.PrefetchScalarGridSpec(
            num_scalar_prefetch=1, grid=(hh,), in_specs=[mine, peer(0), peer(1), peer(2), pl.BlockSpec(memory_space=pl.ANY)],
            out_specs=pl.BlockSpec((None,) + shard, lambda i, ix: (l0 + hh * ix[0] + i,) + zeros)),
        input_output_aliases={5: 0}, compiler_params=_cparams(("parallel",)), name=name)(idx, s, r, r, r, totals)


def _flat_rows(arrs):
    v = jnp.concatenate([a.reshape(-1) for a in arrs])
    n = -(-v.shape[0] // 1024) * 1024
    return jnp.pad(v, (0, n - v.shape[0])).reshape(n // HD, HD)


def _unflat(buf, shapes):
    v, out, o = buf.reshape(-1), [], 0
    for s in shapes:
        n = int(np.prod(s))
        out.append(v[o:o + n].reshape(s))
        o += n
    return out


WEIGHTS = ("norm1_g", "norm2_g", "w_ada", "b_ada", "w_in", "w_gla_a2", "b_gla_a", "b_fox_f", "ret_norm_g", "gla_norm_g",
           "q_norm_g", "k_norm_g", "w_br", "w_mg", "b_mg", "w_o", "w_up", "w_conv", "b_conv", "w_down")
REPLICATED = ("norm1_g", "norm2_g", "b_gla_a", "b_fox_f", "ret_norm_g", "gla_norm_g", "q_norm_g", "k_norm_g", "b_mg", "b_conv")
ADAM_BLOCKS = dict(w_ada=(1, 256, 1536), w_in=(1, 256, 1285), w_br=(1, 3, BW, 256), w_mg=(1, 512, 768), w_o=(2, 256, D),
                   w_up=(1, 256, 1408), w_down=(1, 352, D))
ALL_AXES = ("x", "y", "c")


def kernel(x, c, norm1_g, norm2_g, w_ada, b_ada, w_in, w_gla_a2, b_gla_a, b_fox_f, ret_norm_g, gla_norm_g, q_norm_g, k_norm_g, w_br, w_mg, b_mg, w_o, w_up, w_conv, b_conv, w_down, loss_target, m_norm1_g, m_norm2_g, m_w_ada, m_b_ada, m_w_in, m_w_gla_a2, m_b_gla_a, m_b_fox_f, m_ret_norm_g, m_gla_norm_g, m_q_norm_g, m_k_norm_g, m_w_br, m_w_mg, m_b_mg, m_w_o, m_w_up, m_w_conv, m_b_conv, m_w_down, v_norm1_g, v_norm2_g, v_w_ada, v_b_ada, v_w_in, v_w_gla_a2, v_b_gla_a, v_b_fox_f, v_ret_norm_g, v_gla_norm_g, v_q_norm_g, v_k_norm_g, v_w_br, v_w_mg, v_b_mg, v_w_o, v_w_up, v_w_conv, v_b_conv, v_w_down):
    w = dict(zip(WEIGHTS, (norm1_g, norm2_g, w_ada, b_ada, w_in, w_gla_a2, b_gla_a, b_fox_f, ret_norm_g, gla_norm_g,
                           q_norm_g, k_norm_g, w_br, w_mg, b_mg, w_o, w_up, w_conv, b_conv, w_down)))
    m = dict(zip(WEIGHTS, (m_norm1_g, m_norm2_g, m_w_ada, m_b_ada, m_w_in, m_w_gla_a2, m_b_gla_a, m_b_fox_f, m_ret_norm_g,
                           m_gla_norm_g, m_q_norm_g, m_k_norm_g, m_w_br, m_w_mg, m_b_mg, m_w_o, m_w_up, m_w_conv, m_b_conv,
                           m_w_down)))
    v = dict(zip(WEIGHTS, (v_norm1_g, v_norm2_g, v_w_ada, v_b_ada, v_w_in, v_w_gla_a2, v_b_gla_a, v_b_fox_f, v_ret_norm_g,
                           v_gla_norm_g, v_q_norm_g, v_k_norm_g, v_w_br, v_w_mg, v_b_mg, v_w_o, v_w_up, v_w_conv, v_b_conv,
                           v_w_down)))
    nl = norm1_g.shape[0]
    seq = x.shape[1]
    xi, yi, ci = lax.axis_index("x"), lax.axis_index("y"), lax.axis_index("c")
    k_me = 2 * xi + yi
    b_me = 4 * xi + 2 * yi + ci
    ada_n = w_ada.shape[2]
    a2_n, conv_n = w_gla_a2.shape[2], w_conv.shape[2]

    shards = {n: w[n].astype(MMT) for n in BIG}

    def gather_finish(l, started, after):
        send_sems, recv_sems, lands, _ = started
        lands = gather_wait(send_sems, recv_sems, shards, lands, after, l, name=f"gather{l}_wait")
        big = gather_forward(shards, lands, l, name=f"gather{l}_forward")
        big["w1"] = build_w1(big["w_in"], big["w_mg"])
        return big

    blk = _flat_rows([c, w_gla_a2, w_conv])
    g1 = small_allgather(blk, name="gather_small").reshape(8, blk.shape[0], HD)
    c_all = g1[:, :D // HD].reshape(8, D)
    by_chip = g1[0::2].reshape(4, -1)[:, D:]
    a2_sh, conv_sh = by_chip[:, :nl * GLR * a2_n], by_chip[:, nl * GLR * a2_n:nl * (GLR * a2_n + 3 * conv_n)]
    full_small = dict(
        w_gla_a2=a2_sh.reshape(4, nl, GLR, a2_n).transpose(1, 2, 0, 3).reshape(nl, GLR, 4 * a2_n),
        w_conv=conv_sh.reshape(4, nl, 3, conv_n).transpose(1, 2, 0, 3).reshape(nl, 3, 4 * conv_n))

    b_ada_sh = lax.dynamic_slice_in_dim(b_ada, k_me * ada_n, ada_n, axis=1)[:, None, :]
    mod_sh = ada_mod(c_all, w_ada, b_ada_sh)
    g2 = small_allgather(mod_sh.reshape(nl * 8, ada_n), name="gather_mod").reshape(4, 2, nl, 8, ada_n)[:, 0]
    mod_me = lax.dynamic_index_in_dim(g2, b_me, axis=2, keepdims=False).transpose(1, 0, 2).reshape(nl, 4 * ada_n)

    wsmall = {n: w[n] for n in REPLICATED}
    wsmall.update(full_small)
    mods = [[mod_me[l:l + 1, i * D:(i + 1) * D] for i in range(6)] for l in range(nl)]

    send_sems, recv_sems, lands, token = gather_start(shards, 0, name="gather0_start", after=mod_me)
    lands = gather_wait(send_sems, recv_sems, shards, lands, token, 0, name="gather0_wait_first", names=FIRST_NEEDED)
    first = gather_forward(shards, lands, 0, name="gather0_forward_first", names=FIRST_NEEDED)
    big = {"w1": build_w1(first["w_in"], first["w_mg"])}

    def rest_of_layer0(after):
        rest = gather_wait(send_sems, recv_sems, shards, [first[n] for n in BIG], after, 0, name="gather0_wait_later",
                           names=NEEDED_LATER)
        return gather_forward(shards, rest, 0, name="gather0_forward_later", names=NEEDED_LATER)

    cosf, sinf = _rope_tables(seq)
    xs, saved, params = x[0], [], []
    for l in range(nl):
        if l + 1 < nl:
            started = gather_start(shards, l + 1, name=f"gather{l + 1}_start")
            mods[l][1] = mods[l][1] + started[-1][0, 0]
        xs, sv, p = layer_fwd(xs, mods[l], layer_params(wsmall, big, l, later=rest_of_layer0 if l == 0 else None), cosf, sinf)
        saved.append(sv)
        params.append(p)
        if l + 1 < nl:
            big = gather_finish(l + 1, started, xs)
    loss_part, dx = loss_and_grad(xs, loss_target[0], name="loss")
    loss = lax.psum(loss_part[0, 0], ALL_AXES)
    grads, dmods = [None] * nl, [None] * nl
    idx = jnp.stack([ci, k_me]).astype(jnp.int32)
    totals = {n: lax.empty((nl,) + _shard_shape(n), F32) for n in BIG}

    def finish_group(pending, after, totals, idx):
        group, send_sems, recv_sems, srcs, lands, _ = pending
        sums, from_chips = chip_exchange_wait(send_sems, recv_sems, srcs, lands, after, name=f"rs{group}_chip_exchange_wait")
        totals = {n: chip_add(sums[n], from_chips[n], idx, totals[n], RS_GROUP * group, tensor=n,
                              name=f"rs{group}_chip_add_{n}") for n in BIG}
        return pair_share(totals, RS_GROUP * group, RS_GROUP // 2, name=f"rs{group}_pair_share")

    pending = None
    for group in reversed(range(nl // RS_GROUP)):
        layers = range(RS_GROUP * group, RS_GROUP * (group + 1))
        stacks = {n: lax.empty((RS_GROUP,) + TENSORS[n][2], MMT) for n in BIG if n != "w_br"}
        if pending is not None:
            mods[layers[-1]][5] = mods[layers[-1]][5] + pending[-1][0, 0]
        for l in reversed(layers):
            dx, g, dmods[l], stacks = layer_bwd(dx, mods[l], params[l], saved[l], cosf, sinf, stacks, l - layers[0])
            grads[l] = layer_grads(g)
        stacks["w_br"] = jnp.stack([grads[l]["w_br"].astype(MMT) for l in layers])
        from_sibling = pair_exchange(stacks, name=f"rs{group}_pair_exchange")
        chip_sum = {n: pair_add(stacks[n], from_sibling[n], idx, tensor=n, name=f"rs{group}_pair_add_{n}") for n in BIG}
        if group > 0:
            if pending is not None:
                totals = finish_group(pending, [dx, *chip_sum.values()], totals, idx)
            pending = (group, *chip_exchange_start(chip_sum, name=f"rs{group}_chip_exchange_start"))

    small_names = REPLICATED + ("w_gla_a2", "w_conv")
    small_shapes = [(nl, 6 * D)] + [(nl,) + grads[0][n].shape for n in small_names]
    vec = _flat_rows([jnp.concatenate(dmods, axis=0)] + [jnp.stack([grads[l][n] for l in range(nl)]) for n in small_names])
    gs = small_allgather(vec, name="gather_small_grads")
    earlier = pending
    pending = (0, *chip_exchange_start(chip_sum, name="rs0_chip_exchange_start", after=gs))
    if earlier is not None:
        totals = finish_group(earlier, [dx, pending[-1]], totals, idx)
    gs = (gs + pending[-1][0, 0]).reshape(8, vec.shape[0], HD)
    summed = _unflat(sum_devices(gs), small_shapes)
    grad = dict(zip(small_names, summed[1:]))
    grad["b_ada"] = summed[0]
    grad["w_gla_a2"] = lax.dynamic_slice_in_dim(grad["w_gla_a2"], k_me * a2_n, a2_n, axis=2)
    grad["w_conv"] = lax.dynamic_slice_in_dim(grad["w_conv"], k_me * conv_n, conv_n, axis=2)
    dmod_all = gs[:, :nl * 6 * D // HD].reshape(8, nl, 6 * D)
    dmod_sh = lax.dynamic_slice_in_dim(dmod_all, k_me * ada_n, ada_n, axis=2).transpose(1, 0, 2)
    grad["w_ada"] = ada_dw(c_all, dmod_sh)

    delta, new_m, new_v = {}, {}, {}
    delta["w_ada"], new_m["w_ada"], new_v["w_ada"] = adamw(w["w_ada"], grad["w_ada"], m["w_ada"], v["w_ada"],
                                                          block=ADAM_BLOCKS["w_ada"], name="adamw_w_ada")
    rest = [n for n in WEIGHTS if n not in ADAM_BLOCKS]
    shapes = [w[n].shape for n in rest]
    flat = [_flat_rows([t[n] for n in rest]) for t in (w, grad, m, v)]
    outs = adamw(*flat, block=flat[0].shape, name="adamw_small")
    for t, o in zip((delta, new_m, new_v), outs):
        t.update(zip(rest, _unflat(o, shapes)))

    later = {n: adamw(w[n], totals[n], m[n], v[n], block=ADAM_BLOCKS[n], name="adamw_later_" + n, rows=(RS_GROUP, nl))
             for n in BIG} if nl > RS_GROUP else {}
    done_first = [outs[0], delta["w_ada"]] + [later[n][0] for n in later]
    grad.update(finish_group(pending, done_first, totals, idx))
    for n in BIG:
        delta[n], new_m[n], new_v[n] = adamw(w[n], grad[n], m[n], v[n], block=ADAM_BLOCKS[n], name="adamw_" + n,
                                             rows=(0, min(RS_GROUP, nl)), into=later.get(n))

    return (loss, dx[None], *[grad[n] for n in WEIGHTS], *[delta[n] for n in WEIGHTS], *[new_m[n] for n in WEIGHTS],
            *[new_v[n] for n in WEIGHTS])
```

```python
import functools

import numpy as np
import jax
import jax.numpy as jnp
from jax import lax
from jax.experimental import pallas as pl
from jax.experimental.pallas import tpu as pltpu

F32 = jnp.float32
MMT = jnp.bfloat16
HI = lax.Precision.HIGHEST

D = 1024
DEPTH = 4
NH = 4
HD = 128
BW = NH * HD
CH = 64
GDK = 64
GLR = 16
DFF = 2816
EPS = 1e-6
ROPE_BASE = 10000.0

GP, RG, GG, FV, RQ, RK, RV, GQ, GK, GV, FQ, FK, LR, FF = (
    0, 3072, 3584, 4096, 4608, 5120, 5632, 6144, 6400, 6656, 7168, 7680, 8192, 8320)
NZZ = 8448
WZ0 = 3072
IN_W = 5140
W_IN_COLS = dict(rqkv=(0, 1536), rg=(1536, 2048), gqkv=(2048, 3072), lr=(3072, 3088), gg=(3088, 3600), fqk=(3600, 4624),
                 fv=(4624, 5136), ff=(5136, 5140))

VMEM_LIMIT = 56 * 1024 * 1024

ADAM_LR, ADAM_B1, ADAM_B2, ADAM_EPS, ADAM_WD, ADAM_STEP = 0.001, 0.9, 0.999, 1e-08, 0.01, 10


def _cparams(sem=None):
    return pltpu.CompilerParams(dimension_semantics=sem, vmem_limit_bytes=VMEM_LIMIT)


def _sds(shape, dtype):
    return jax.ShapeDtypeStruct(tuple(shape), dtype)


def _dot(a, b, precision=None):
    return lax.dot_general(a, b, (((1,), (0,)), ((), ())), precision=precision, preferred_element_type=F32)


def _dot_nt(a, b, precision=None):
    return lax.dot_general(a, b, (((1,), (1,)), ((), ())), precision=precision, preferred_element_type=F32)


def _dot_tn(a, b, precision=None):
    return lax.dot_general(a, b, (((0,), (0,)), ((), ())), precision=precision, preferred_element_type=F32)


def _silu(x):
    return x * jax.nn.sigmoid(x)


def _log_sigmoid(x):
    return jnp.minimum(x, 0.0) - jnp.log(1.0 + jnp.exp(jnp.minimum(x, -x)))


@jax.custom_vjp
def _swap_halves(x):
    return pltpu.roll(x, HD // 2, 1)


_swap_halves.defvjp(lambda x: (_swap_halves(x), None), lambda _, g: (_swap_halves(g),))


@jax.custom_vjp
def _bdot(a, b):
    return _dot(a.astype(MMT), b.astype(MMT))


@jax.custom_vjp
def _bdot_nt(a, b):
    return _dot_nt(a.astype(MMT), b.astype(MMT))


@jax.custom_vjp
def _bdot_tn(a, b):
    return _dot_tn(a.astype(MMT), b.astype(MMT))


_bdot.defvjp(lambda a, b: (_bdot(a, b), (a, b)), lambda r, g: (_bdot_nt(g, r[1]), _bdot_tn(r[0], g)))
_bdot_nt.defvjp(lambda a, b: (_bdot_nt(a, b), (a, b)), lambda r, g: (_bdot(g, r[1]), _bdot_tn(g, r[0])))
_bdot_tn.defvjp(lambda a, b: (_bdot_tn(a, b), (a, b)), lambda r, g: (_bdot_nt(r[1], g), _bdot(r[0], g)))


def _stacked(blk, idx, layer):
    if layer is None:
        return pl.BlockSpec(blk, idx)
    return pl.BlockSpec((None,) + blk, lambda i, j: (layer,) + idx(i, j))


def mm_nn(a, b, *, tm, tn, out_dtype, name, layer=None):
    m, k = a.shape
    n = b.shape[-1]

    def body(a_ref, b_ref, o_ref):
        o_ref[...] = _dot(a_ref[...], b_ref[...]).astype(o_ref.dtype)

    return pl.pallas_call(
        body, grid=(m // tm, n // tn),
        in_specs=[pl.BlockSpec((tm, k), lambda i, j: (i, 0)), _stacked((k, tn), lambda i, j: (0, j), layer)],
        out_specs=pl.BlockSpec((tm, tn), lambda i, j: (i, j)),
        out_shape=_sds((m, n), out_dtype), compiler_params=_cparams(("parallel", "parallel")), name=name)(a, b)


def mm_nn_residual(a, b, res, gate, *, tm, tn, name, layer=None):
    m, k = a.shape
    n = b.shape[-1]

    def body(a_ref, b_ref, r_ref, g_ref, x_ref, y_ref):
        acc = _dot(a_ref[...], b_ref[...])
        y_ref[...] = acc
        x_ref[...] = r_ref[...] + g_ref[...] * acc

    return pl.pallas_call(
        body, grid=(m // tm, n // tn),
        in_specs=[pl.BlockSpec((tm, k), lambda i, j: (i, 0)), _stacked((k, tn), lambda i, j: (0, j), layer),
                  pl.BlockSpec((tm, tn), lambda i, j: (i, j)), pl.BlockSpec((1, tn), lambda i, j: (0, j))],
        out_specs=[pl.BlockSpec((tm, tn), lambda i, j: (i, j)), pl.BlockSpec((tm, tn), lambda i, j: (i, j))],
        out_shape=[_sds((m, n), F32), _sds((m, n), F32)],
        compiler_params=_cparams(("parallel", "parallel")), name=name)(a, b, res, gate)


def mm_nt(a, b, *, tm, tn, out_dtype, name, layer=None):
    m, k = a.shape
    n = b.shape[-2]

    def body(a_ref, b_ref, o_ref):
        o_ref[...] = _dot_nt(a_ref[...], b_ref[...]).astype(o_ref.dtype)

    return pl.pallas_call(
        body, grid=(m // tm, n // tn),
        in_specs=[pl.BlockSpec((tm, k), lambda i, j: (i, 0)), _stacked((tn, k), lambda i, j: (j, 0), layer)],
        out_specs=pl.BlockSpec((tm, tn), lambda i, j: (i, j)),
        out_shape=_sds((m, n), out_dtype), compiler_params=_cparams(("parallel", "parallel")), name=name)(a, b)


def mm_nt2(a1, a2, b, *, tm, tn, name, layer):
    m, k1 = a1.shape
    k2 = a2.shape[1]
    n = b.shape[-2]

    def body(a1_ref, a2_ref, b_ref, o_ref):
        o_ref[...] = _dot_nt(a1_ref[...], b_ref[:, :k1]) + _dot_nt(a2_ref[...], b_ref[:, k1:])

    return pl.pallas_call(
        body, grid=(m // tm, n // tn),
        in_specs=[pl.BlockSpec((tm, k1), lambda i, j: (i, 0)), pl.BlockSpec((tm, k2), lambda i, j: (i, 0)),
                  _stacked((tn, k1 + k2), lambda i, j: (j, 0), layer)],
        out_specs=pl.BlockSpec((tm, tn), lambda i, j: (i, j)),
        out_shape=_sds((m, n), F32), compiler_params=_cparams(("parallel", "parallel")), name=name)(a1, a2, b)


def mm_tn(a, b, *, tm, tn, out_dtype, name, col0=0, ncols=None, stack=None, layer=None, out_col0=0):
    s, m = a.shape
    n = b.shape[1] - col0 if ncols is None else ncols
    c0, oc0 = col0 // tn, out_col0 // tn

    def body(a_ref, b_ref, *rest):
        o_ref = rest[-1]
        o_ref[...] = _dot_tn(a_ref[...], b_ref[...]).astype(o_ref.dtype)

    in_specs = [pl.BlockSpec((s, tm), lambda i, j: (0, i)), pl.BlockSpec((s, tn), lambda i, j: (0, c0 + j))]
    if stack is None:
        return pl.pallas_call(
            body, grid=(m // tm, n // tn), in_specs=in_specs, out_specs=pl.BlockSpec((tm, tn), lambda i, j: (i, j)),
            out_shape=_sds((m, n), out_dtype), compiler_params=_cparams(("parallel", "parallel")), name=name)(a, b)
    return pl.pallas_call(
        body, grid=(m // tm, n // tn), in_specs=in_specs + [pl.BlockSpec(memory_space=pl.ANY)],
        out_specs=pl.BlockSpec((None, tm, tn), lambda i, j: (layer, i, oc0 + j)),
        out_shape=_sds(stack.shape, stack.dtype), input_output_aliases={2: 0},
        compiler_params=_cparams(("parallel", "parallel")), name=name)(a, b, stack)


def _row_tile(s):
    return min(256, s)


def _norm_mod_f(x, g, scale, shift):
    r = lax.rsqrt(jnp.mean(x * x, axis=-1, keepdims=True) + EPS)
    return (x * r * g) * (1.0 + scale) + shift


def norm_mod(x, g, scale, shift, *, name):
    s = x.shape[0]
    t = _row_tile(s)

    def body(x_ref, g_ref, sc_ref, sh_ref, o_ref):
        o_ref[...] = _norm_mod_f(x_ref[...], g_ref[...], sc_ref[...], sh_ref[...]).astype(o_ref.dtype)

    vec = pl.BlockSpec((1, D), lambda i: (0, 0))
    return pl.pallas_call(
        body, grid=(s // t,), in_specs=[pl.BlockSpec((t, D), lambda i: (i, 0)), vec, vec, vec],
        out_specs=pl.BlockSpec((t, D), lambda i: (i, 0)), out_shape=_sds((s, D), MMT),
        compiler_params=_cparams(("parallel",)), name=name)(x, g, scale, shift)


def norm_mod_bwd(x, dh, dres, g, scale, shift, *, name):
    s = x.shape[0]
    t = _row_tile(s)

    def body(x_ref, dh_ref, dr_ref, g_ref, sc_ref, sh_ref, dx_ref, dg_ref, dsc_ref, dsh_ref):
        @pl.when(pl.program_id(0) == 0)
        def _():
            dg_ref[...] = jnp.zeros_like(dg_ref)
            dsc_ref[...] = jnp.zeros_like(dsc_ref)
            dsh_ref[...] = jnp.zeros_like(dsh_ref)

        _, vjp = jax.vjp(_norm_mod_f, x_ref[...], g_ref[...], sc_ref[...], sh_ref[...])
        dx, dg, dsc, dsh = vjp(dh_ref[...])
        dx_ref[...] = dr_ref[...] + dx
        dg_ref[...] += dg
        dsc_ref[...] += dsc
        dsh_ref[...] += dsh

    row = pl.BlockSpec((t, D), lambda i: (i, 0))
    vec = pl.BlockSpec((1, D), lambda i: (0, 0))
    return pl.pallas_call(
        body, grid=(s // t,), in_specs=[row, row, row, vec, vec, vec], out_specs=[row, vec, vec, vec],
        out_shape=[_sds((s, D), F32)] + [_sds((1, D), F32)] * 3,
        compiler_params=_cparams(("arbitrary",)), name=name)(x, dh, dres, g, scale, shift)


def gate_bwd(dx, y, gate, *, name):
    s = dx.shape[0]
    t = _row_tile(s)

    def body(dx_ref, y_ref, g_ref, dy_ref, dg_ref):
        @pl.when(pl.program_id(0) == 0)
        def _():
            dg_ref[...] = jnp.zeros_like(dg_ref)

        dxv = dx_ref[...]
        dy_ref[...] = (g_ref[...] * dxv).astype(dy_ref.dtype)
        dg_ref[...] += jnp.sum(dxv * y_ref[...], axis=0, keepdims=True)

    row = pl.BlockSpec((t, D), lambda i: (i, 0))
    vec = pl.BlockSpec((1, D), lambda i: (0, 0))
    return pl.pallas_call(
        body, grid=(s // t,), in_specs=[row, row, vec], out_specs=[row, vec],
        out_shape=[_sds((s, D), MMT), _sds((1, D), F32)],
        compiler_params=_cparams(("arbitrary",)), name=name)(dx, y, gate)


def loss_and_grad(xf, target, *, name):
    s = xf.shape[0]
    t = _row_tile(s)

    def body(x_ref, t_ref, l_ref, dx_ref):
        @pl.when(pl.program_id(0) == 0)
        def _():
            l_ref[...] = jnp.zeros_like(l_ref)

        e = x_ref[...] - t_ref[...]
        dx_ref[...] = e * (1.0 / D)
        l_ref[...] += 0.5 * jnp.sum(jnp.sum(e * e, axis=1, keepdims=True), axis=0, keepdims=True) * (1.0 / D)

    row = pl.BlockSpec((t, D), lambda i: (i, 0))
    return pl.pallas_call(
        body, grid=(s // t,), in_specs=[row, row], out_specs=[pl.BlockSpec((1, 1), lambda i: (0, 0)), row],
        out_shape=[_sds((1, 1), F32), _sds((s, D), F32)],
        compiler_params=_cparams(("arbitrary",)), name=name)(xf, target)


def _ret_consts():
    log_g = np.log1p(-np.exp2(-5.0 - np.arange(NH, dtype=np.float32))).astype(np.float32)
    idx = np.arange(CH, dtype=np.float32)
    d_intra = np.exp(np.abs(idx[:, None] - idx[None, :])[None] * log_g[:, None, None]).astype(np.float32)
    k_w = np.exp((CH - 1.0 - idx)[None, :] * log_g[:, None]).astype(np.float32)
    q_w = np.exp((idx + 1.0)[None, :] * log_g[:, None]).astype(np.float32)
    g_chunk = [float(v) for v in np.exp(np.float32(CH) * log_g).astype(np.float32)]
    bc = lambda a: np.ascontiguousarray(np.broadcast_to(a[:, :, None], (NH, CH, HD)))
    return jnp.asarray(d_intra), jnp.asarray(bc(k_w)), jnp.asarray(bc(q_w)), g_chunk


def _rope_tables(s):
    half = HD // 2
    inv_freq = (ROPE_BASE ** (-np.arange(half, dtype=np.float64) / half)).astype(np.float32)
    ang = (np.arange(s, dtype=np.float32)[:, None] * inv_freq[None, :]).astype(np.float64)
    cos, sin = np.cos(ang).astype(np.float32), np.sin(ang).astype(np.float32)
    return jnp.asarray(np.concatenate([cos, cos], axis=1)), jnp.asarray(np.concatenate([-sin, sin], axis=1))


def _ret_chunk(qs, ks, vs, rs, cos, sin, dintra, kw, qw, g_chunk):
    outs, rn = [], []
    for h in range(NH):
        q = qs[h] * cos + _swap_halves(qs[h]) * sin
        k = (ks[h] * cos + _swap_halves(ks[h]) * sin) * (HD ** -0.5)
        sc = _bdot_nt(q, k) * dintra[h]
        outs.append(_bdot(sc, vs[h]) + _bdot(q * qw[h], rs[h]))
        rn.append(g_chunk[h] * rs[h] + _bdot_tn(k * kw[h], vs[h]))
    return outs, rn


def _heads(x):
    return [x[:, h * HD:(h + 1) * HD] for h in range(NH)]


def _chunks_per_step(n):
    return 4 if n % 4 == 0 else 1


def retention_fwd(zz, cosf, sinf, *, name):
    s = zz.shape[0]
    n = s // CH
    nb = _chunks_per_step(n)
    rb = nb * CH
    dintra, kw, qw, g_chunk = _ret_consts()

    def body(q_ref, k_ref, v_ref, c_ref, s_ref, di_ref, kw_ref, qw_ref, o_ref, rp_ref, r_scr):
        @pl.when(pl.program_id(0) == 0)
        def _():
            r_scr[...] = jnp.zeros_like(r_scr)

        r = r_scr[...]
        consts = ([di_ref[h] for h in range(NH)], [kw_ref[h] for h in range(NH)], [qw_ref[h] for h in range(NH)], g_chunk)
        for i in range(nb):
            rows = slice(i * CH, (i + 1) * CH)
            rp_ref[i] = r
            outs, rn = _ret_chunk(_heads(q_ref[rows, :]), _heads(k_ref[rows, :]), _heads(v_ref[rows, :]),
                                  [r[h * HD:(h + 1) * HD] for h in range(NH)], c_ref[rows, :], s_ref[rows, :], *consts)
            o_ref[rows, :] = jnp.concatenate(outs, axis=1)
            r = jnp.concatenate(rn, axis=0)
        r_scr[...] = r

    col = lambda c: pl.BlockSpec((rb, BW), lambda i: (i, c // BW))
    tab = pl.BlockSpec((rb, HD), lambda i: (i, 0))
    cst = lambda shp: pl.BlockSpec(shp, lambda i: (0,) * len(shp))
    return pl.pallas_call(
        body, grid=(n // nb,),
        in_specs=[col(RQ), col(RK), col(RV), tab, tab, cst((NH, CH, CH)), cst((NH, CH, HD)), cst((NH, CH, HD))],
        out_specs=[pl.BlockSpec((rb, BW), lambda i: (i, 0)), pl.BlockSpec((nb, BW, HD), lambda i: (i, 0, 0))],
        out_shape=[_sds((s, BW), F32), _sds((n, BW, HD), F32)],
        scratch_shapes=[pltpu.VMEM((BW, HD), F32)],
        compiler_params=_cparams(("arbitrary",)), name=name)(zz, zz, zz, cosf, sinf, dintra, kw, qw)


def retention_bwd(zz, cosf, sinf, rprev, do, dzz, *, name):
    s = zz.shape[0]
    n = s // CH
    nb = _chunks_per_step(n)
    rb, steps = nb * CH, n // nb
    dintra, kw, qw, g_chunk = _ret_consts()

    def body(q_ref, k_ref, v_ref, c_ref, s_ref, di_ref, kw_ref, qw_ref, rp_ref, do_ref, dzz_ref, dz_ref, dr_scr):
        @pl.when(pl.program_id(0) == 0)
        def _():
            dr_scr[...] = jnp.zeros_like(dr_scr)

        dr = dr_scr[...]
        consts = dict(dintra=[di_ref[h] for h in range(NH)], kw=[kw_ref[h] for h in range(NH)],
                      qw=[qw_ref[h] for h in range(NH)], g_chunk=g_chunk)
        for i in reversed(range(nb)):
            rows = slice(i * CH, (i + 1) * CH)
            rprev_v = rp_ref[i]
            f = functools.partial(_ret_chunk, cos=c_ref[rows, :], sin=s_ref[rows, :], **consts)
            _, vjp = jax.vjp(f, _heads(q_ref[rows, :]), _heads(k_ref[rows, :]), _heads(v_ref[rows, :]),
                             [rprev_v[h * HD:(h + 1) * HD] for h in range(NH)])
            dq, dk, dv, drp = vjp((_heads(do_ref[rows, :]), [dr[h * HD:(h + 1) * HD] for h in range(NH)]))
            dz_ref[rows, :] = jnp.concatenate(dq + dk + dv, axis=1).astype(dz_ref.dtype)
            dr = jnp.concatenate(drp, axis=0)
        dr_scr[...] = dr

    col = lambda c: pl.BlockSpec((rb, BW), lambda i: (steps - 1 - i, c // BW))
    tab = pl.BlockSpec((rb, HD), lambda i: (steps - 1 - i, 0))
    cst = lambda shp: pl.BlockSpec(shp, lambda i: (0,) * len(shp))
    return pl.pallas_call(
        body, grid=(steps,),
        in_specs=[col(RQ), col(RK), col(RV), tab, tab, cst((NH, CH, CH)), cst((NH, CH, HD)), cst((NH, CH, HD)),
                  pl.BlockSpec((nb, BW, HD), lambda i: (steps - 1 - i, 0, 0)),
                  pl.BlockSpec((rb, BW), lambda i: (steps - 1 - i, 0)), pl.BlockSpec(memory_space=pl.ANY)],
        out_specs=pl.BlockSpec((rb, 3 * BW), lambda i: (steps - 1 - i, RQ // (3 * BW))),
        out_shape=_sds(dzz.shape, dzz.dtype), input_output_aliases={10: 0},
        scratch_shapes=[pltpu.VMEM((BW, HD), F32)],
        compiler_params=_cparams(("arbitrary",)), name=name)(zz, zz, zz, cosf, sinf, dintra, kw, qw, rprev, do, dzz)


GKW = NH * GDK


def _gla_consts():
    tri = np.tril(np.ones((CH, CH), np.float32))
    mask_t = np.zeros((BW, GKW), np.float32)
    for h in range(NH):
        mask_t[h * HD:(h + 1) * HD, h * GDK:(h + 1) * GDK] = 1.0
    return jnp.asarray(tri), jnp.asarray(mask_t)


def _gla_chunk(q, k, v, glr, w_a2, b_a, st, tri, mask_t):
    la = _log_sigmoid(_bdot(glr, w_a2) + b_a) * (1.0 / 16.0)
    bc = _dot(tri, la, HI)
    be = jnp.sum(la, axis=0, keepdims=True)
    kv_t = _bdot_tn(v, k * jnp.exp(be - bc)) * mask_t
    sn = jnp.exp(be) * st + kv_t
    return _bdot_nt(q * (GDK ** -0.5), sn), sn


def gla_fwd(zz, w_a2p, b_a, *, name):
    s = zz.shape[0]
    n = s // CH
    nb = _chunks_per_step(n)
    rb = nb * CH
    tri, mask_t = _gla_consts()

    def body(q_ref, k_ref, v_ref, lr_ref, w_ref, b_ref, tri_ref, m_ref, o_ref, sp_ref, st_scr):
        @pl.when(pl.program_id(0) == 0)
        def _():
            st_scr[...] = jnp.zeros_like(st_scr)

        st = st_scr[...]
        for i in range(nb):
            rows = slice(i * CH, (i + 1) * CH)
            sp_ref[i] = st
            o_ref[rows, :], st = _gla_chunk(q_ref[rows, :], k_ref[rows, :], v_ref[rows, :], lr_ref[rows, :], w_ref[...],
                                            b_ref[...], st, tri_ref[...], m_ref[...])
        st_scr[...] = st

    cst = lambda shp: pl.BlockSpec(shp, lambda i: (0,) * len(shp))
    return pl.pallas_call(
        body, grid=(n // nb,),
        in_specs=[pl.BlockSpec((rb, GKW), lambda i: (i, GQ // GKW)), pl.BlockSpec((rb, GKW), lambda i: (i, GK // GKW)),
                  pl.BlockSpec((rb, BW), lambda i: (i, GV // BW)), pl.BlockSpec((rb, HD), lambda i: (i, LR // HD)),
                  cst((HD, GKW)), cst((1, GKW)), cst((CH, CH)), cst((BW, GKW))],
        out_specs=[pl.BlockSpec((rb, BW), lambda i: (i, 0)), pl.BlockSpec((nb, BW, GKW), lambda i: (i, 0, 0))],
        out_shape=[_sds((s, BW), F32), _sds((n, BW, GKW), F32)],
        scratch_shapes=[pltpu.VMEM((BW, GKW), F32)],
        compiler_params=_cparams(("arbitrary",)), name=name)(zz, zz, zz, zz, w_a2p, b_a, tri, mask_t)


def gla_bwd(zz, w_a2p, b_a, sprev, do, dzz, *, name):
    s = zz.shape[0]
    n = s // CH
    nb = _chunks_per_step(n)
    rb, steps = nb * CH, n // nb
    tri, mask_t = _gla_consts()

    def body(q_ref, k_ref, v_ref, lr_ref, w_ref, b_ref, tri_ref, m_ref, sp_ref, do_ref, dzz_ref,
             dz_ref, dlr_ref, dw_ref, db_ref, ds_scr):
        @pl.when(pl.program_id(0) == 0)
        def _():
            ds_scr[...] = jnp.zeros_like(ds_scr)
            dw_ref[...] = jnp.zeros_like(dw_ref)
            db_ref[...] = jnp.zeros_like(db_ref)

        f = functools.partial(_gla_chunk, tri=tri_ref[...], mask_t=m_ref[...])
        ds, dw_sum, db_sum = ds_scr[...], jnp.zeros(dw_ref.shape, F32), jnp.zeros(db_ref.shape, F32)
        for i in reversed(range(nb)):
            rows = slice(i * CH, (i + 1) * CH)
            _, vjp = jax.vjp(f, q_ref[rows, :], k_ref[rows, :], v_ref[rows, :], lr_ref[rows, :], w_ref[...], b_ref[...],
                             sp_ref[i])
            dq, dk, dv, dlr, dw, db, ds = vjp((do_ref[rows, :], ds))
            dz_ref[rows, :] = jnp.concatenate([dq, dk, dv], axis=1).astype(dz_ref.dtype)
            dlr_ref[rows, :] = dlr.astype(dlr_ref.dtype)
            dw_sum, db_sum = dw_sum + dw, db_sum + db
        dw_ref[...] += dw_sum
        db_ref[...] += db_sum
        ds_scr[...] = ds

    cst = lambda shp: pl.BlockSpec(shp, lambda i: (0,) * len(shp))
    r = lambda i: steps - 1 - i
    return pl.pallas_call(
        body, grid=(steps,),
        in_specs=[pl.BlockSpec((rb, GKW), lambda i: (r(i), GQ // GKW)), pl.BlockSpec((rb, GKW), lambda i: (r(i), GK // GKW)),
                  pl.BlockSpec((rb, BW), lambda i: (r(i), GV // BW)), pl.BlockSpec((rb, HD), lambda i: (r(i), LR // HD)),
                  cst((HD, GKW)), cst((1, GKW)), cst((CH, CH)), cst((BW, GKW)),
                  pl.BlockSpec((nb, BW, GKW), lambda i: (r(i), 0, 0)), pl.BlockSpec((rb, BW), lambda i: (r(i), 0)),
                  pl.BlockSpec(memory_space=pl.ANY)],
        out_specs=[pl.BlockSpec((rb, 2 * GKW + BW), lambda i: (r(i), GQ // (2 * GKW + BW))),
                   pl.BlockSpec((rb, HD), lambda i: (r(i), 0)), cst((HD, GKW)), cst((1, GKW))],
        out_shape=[_sds(dzz.shape, dzz.dtype), _sds((s, HD), MMT), _sds((HD, GKW), F32), _sds((1, GKW), F32)],
        input_output_aliases={10: 0}, scratch_shapes=[pltpu.VMEM((BW, GKW), F32)],
        compiler_params=_cparams(("arbitrary",)), name=name)(zz, zz, zz, zz, w_a2p, b_a, tri, mask_t, sprev, do, dzz)


def _fox_pre_f(fqs, fks, ff, gq, gk, bf):
    def rms(x, g):
        return x * lax.rsqrt(jnp.mean(x * x, axis=-1, keepdims=True) + EPS) * g

    qn = [rms(x, gq) * (HD ** -0.5) for x in fqs]
    kn = [rms(x, gk) for x in fks]
    return qn, kn, _log_sigmoid(ff + bf)


def fox_pre(zz, gq, gk, bf, *, name):
    s = zz.shape[0]
    t = _row_tile(s)
    tri = jnp.asarray(np.tril(np.ones((t, t), np.float32)))

    def body(q_ref, k_ref, f_ref, gq_ref, gk_ref, b_ref, tri_ref, qn_ref, kn_ref, cum_ref, carry):
        @pl.when(pl.program_id(0) == 0)
        def _():
            carry[...] = jnp.zeros_like(carry)

        qn, kn, lf = _fox_pre_f(_heads(q_ref[...]), _heads(k_ref[...]), f_ref[...], gq_ref[...], gk_ref[...], b_ref[...])
        qn_ref[...] = jnp.concatenate(qn, axis=1).astype(qn_ref.dtype)
        kn_ref[...] = jnp.concatenate(kn, axis=1).astype(kn_ref.dtype)
        cum_ref[...] = _dot(tri_ref[...], lf, HI) + carry[...]
        carry[...] += jnp.sum(lf, axis=0, keepdims=True)

    vec = pl.BlockSpec((1, HD), lambda i: (0, 0))
    return pl.pallas_call(
        body, grid=(s // t,),
        in_specs=[pl.BlockSpec((t, BW), lambda i: (i, FQ // BW)), pl.BlockSpec((t, BW), lambda i: (i, FK // BW)),
                  pl.BlockSpec((t, HD), lambda i: (i, FF // HD)), vec, vec, vec, pl.BlockSpec((t, t), lambda i: (0, 0))],
        out_specs=[pl.BlockSpec((t, BW), lambda i: (i, 0)), pl.BlockSpec((t, BW), lambda i: (i, 0)),
                   pl.BlockSpec((t, HD), lambda i: (i, 0))],
        out_shape=[_sds((s, BW), MMT), _sds((s, BW), MMT), _sds((s, HD), F32)],
        scratch_shapes=[pltpu.VMEM((1, HD), F32)],
        compiler_params=_cparams(("arbitrary",)), name=name)(zz, zz, zz, gq, gk, bf, tri)


def fox_pre_bwd(zz, gq, gk, bf, dqn, dkn, dcum, dzz, *, name):
    s = zz.shape[0]
    t = _row_tile(s)
    nt = s // t
    triu = jnp.asarray(np.triu(np.ones((t, t), np.float32)))

    def body(q_ref, k_ref, f_ref, gq_ref, gk_ref, b_ref, tri_ref, dqn_ref, dkn_ref, dcum_ref, dzz_ref,
             dz_ref, dff_ref, dgq_ref, dgk_ref, db_ref, carry):
        @pl.when(pl.program_id(0) == 0)
        def _():
            carry[...] = jnp.zeros_like(carry)
            dgq_ref[...] = jnp.zeros_like(dgq_ref)
            dgk_ref[...] = jnp.zeros_like(dgk_ref)
            db_ref[...] = jnp.zeros_like(db_ref)

        dcum_v = dcum_ref[...]
        dlf = _dot(tri_ref[...], dcum_v, HI) + carry[...]
        carry[...] += jnp.sum(dcum_v, axis=0, keepdims=True)
        _, vjp = jax.vjp(_fox_pre_f, _heads(q_ref[...]), _heads(k_ref[...]), f_ref[...], gq_ref[...], gk_ref[...], b_ref[...])
        dq, dk, dff, dgq, dgk, db = vjp((_heads(dqn_ref[...]), _heads(dkn_ref[...]), dlf))
        dz_ref[...] = jnp.concatenate(dq + dk, axis=1).astype(dz_ref.dtype)
        dff_ref[...] = dff.astype(dff_ref.dtype)
        dgq_ref[...] += dgq
        dgk_ref[...] += dgk
        db_ref[...] += db

    r = lambda i: nt - 1 - i
    vec = pl.BlockSpec((1, HD), lambda i: (0, 0))
    return pl.pallas_call(
        body, grid=(nt,),
        in_specs=[pl.BlockSpec((t, BW), lambda i: (r(i), FQ // BW)), pl.BlockSpec((t, BW), lambda i: (r(i), FK // BW)),
                  pl.BlockSpec((t, HD), lambda i: (r(i), FF // HD)), vec, vec, vec, pl.BlockSpec((t, t), lambda i: (0, 0)),
                  pl.BlockSpec((t, BW), lambda i: (r(i), 0)), pl.BlockSpec((t, BW), lambda i: (r(i), 0)),
                  pl.BlockSpec((t, HD), lambda i: (r(i), 0)), pl.BlockSpec(memory_space=pl.ANY)],
        out_specs=[pl.BlockSpec((t, 2 * BW), lambda i: (r(i), FQ // (2 * BW))), pl.BlockSpec((t, HD), lambda i: (r(i), 0)),
                   vec, vec, vec],
        out_shape=[_sds(dzz.shape, dzz.dtype), _sds((s, HD), MMT), _sds((1, HD), F32), _sds((1, HD), F32), _sds((1, HD), F32)],
        input_output_aliases={10: 0}, scratch_shapes=[pltpu.VMEM((1, HD), F32)],
        compiler_params=_cparams(("arbitrary",)), name=name)(zz, zz, zz, gq, gk, bf, triu, dqn, dkn, dcum, dzz)


def _fox_blocks(s):
    return min(256, s), min(512, s)


NEG = -1e30


def fox_attn_fwd(qn, kn, zz, cum_col, cum_row, *, name):
    s = qn.shape[0]
    bq, bk = _fox_blocks(s)

    def body(q_ref, k_ref, v_ref, cc_ref, cr_ref, o_ref, lse_ref):
        qi = pl.program_id(1)
        q = q_ref[...]
        cq = cc_ref[...]
        rows = qi * bq + lax.broadcasted_iota(jnp.int32, (bq, bk), 0)
        cols0 = lax.broadcasted_iota(jnp.int32, (bq, bk), 1)

        def step(j, carry, on_diagonal):
            m, l, acc = carry
            off = pl.multiple_of(j * bk, bk)
            k = k_ref[pl.ds(off, bk), :]
            v = v_ref[pl.ds(off, bk), :].astype(MMT)
            sc = _dot_nt(q, k) + cq - cr_ref[pl.ds(j, 1), :]
            if on_diagonal:
                sc = jnp.where(rows >= cols0 + j * bk, sc, NEG)
            m_new = jnp.maximum(m, jnp.max(sc, axis=1, keepdims=True))
            alpha = jnp.exp(m - m_new)
            p = jnp.exp(sc - m_new)
            return m_new, alpha * l + jnp.sum(p, axis=1, keepdims=True), alpha * acc + _dot(p.astype(MMT), v)

        nfull, nk = (qi * bq + 1) // bk, ((qi + 1) * bq + bk - 1) // bk
        carry = (jnp.full((bq, 1), NEG, F32), jnp.zeros((bq, 1), F32), jnp.zeros((bq, HD), F32))
        carry = lax.fori_loop(0, nfull, functools.partial(step, on_diagonal=False), carry)
        m, l, acc = lax.fori_loop(nfull, nk, functools.partial(step, on_diagonal=True), carry)
        o_ref[...] = acc / l
        lse_ref[...] = m + jnp.log(l)

    return pl.pallas_call(
        body, grid=(NH, s // bq),
        in_specs=[pl.BlockSpec((bq, HD), lambda h, i: (i, h)), pl.BlockSpec((s, HD), lambda h, i: (0, h)),
                  pl.BlockSpec((s, HD), lambda h, i: (0, FV // HD + h)),
                  pl.BlockSpec((None, bq, 1), lambda h, i: (h, i, 0)), pl.BlockSpec((None, s // bk, bk), lambda h, i: (h, 0, 0))],
        out_specs=[pl.BlockSpec((bq, HD), lambda h, i: (i, h)), pl.BlockSpec((None, bq, 1), lambda h, i: (h, i, 0))],
        out_shape=[_sds((s, BW), F32), _sds((NH, s, 1), F32)],
        compiler_params=_cparams(("parallel", "parallel")), name=name)(qn, kn, zz, cum_col, cum_row)


def fox_attn_bwd(qn, kn, zz, cum_col, cum_row, lse, do, dzz, *, name):
    s = qn.shape[0]
    bq, bk = _fox_blocks(s)
    nkc = s // bk

    def body(q_ref, k_ref, v_ref, cc_ref, cr_ref, lse_ref, do_ref, dzz_ref, dq_ref, dk_ref, dv_ref, dc_ref,
             p_scr, dp_scr, dv_scr):
        qi = pl.program_id(1)

        @pl.when(qi == 0)
        def _():
            dk_ref[...] = jnp.zeros_like(dk_ref)
            dv_scr[...] = jnp.zeros_like(dv_scr)
            dc_ref[...] = jnp.zeros_like(dc_ref)

        q = q_ref[...]
        dob = do_ref[...].astype(MMT)
        cq = cc_ref[...]
        lse_v = lse_ref[...]
        rows = qi * bq + lax.broadcasted_iota(jnp.int32, (bq, bk), 0)
        cols0 = lax.broadcasted_iota(jnp.int32, (bq, bk), 1)
        nfull, nk = (qi * bq + 1) // bk, ((qi + 1) * bq + bk - 1) // bk

        def probs(j, delta, on_diagonal):
            off = pl.multiple_of(j * bk, bk)
            sc = _dot_nt(q, k_ref[pl.ds(off, bk), :]) + cq - cr_ref[pl.ds(j, 1), :]
            p = jnp.exp(sc - lse_v)
            if on_diagonal:
                p = jnp.where(rows >= cols0 + j * bk, p, 0.0)
            dp = _dot_nt(dob, v_ref[pl.ds(off, bk), :].astype(MMT))
            p_scr[j] = p
            dp_scr[j] = dp
            return delta + jnp.sum(p * dp, axis=1, keepdims=True)

        delta = lax.fori_loop(0, nfull, functools.partial(probs, on_diagonal=False), jnp.zeros((bq, 1), F32))
        delta = lax.fori_loop(nfull, nk, functools.partial(probs, on_diagonal=True), delta)

        def grads(j, dq):
            off = pl.multiple_of(j * bk, bk)
            p = p_scr[j]
            ds = p * (dp_scr[j] - delta)
            dsm = ds.astype(MMT)
            dv_scr[pl.ds(off, bk), :] += _dot_tn(p.astype(MMT), dob)
            dk_ref[pl.ds(off, bk), :] += _dot_tn(dsm, q)
            dc_ref[pl.ds(j, 1), :] -= jnp.sum(ds, axis=0, keepdims=True)
            return dq + _dot(dsm, k_ref[pl.ds(off, bk), :])

        dq_ref[...] = lax.fori_loop(0, nk, grads, jnp.zeros((bq, HD), F32))

        @pl.when(qi == pl.num_programs(1) - 1)
        def _():
            dv_ref[...] = dv_scr[...].astype(dv_ref.dtype)

    full = lambda c0=0: pl.BlockSpec((s, HD), lambda h, i: (0, c0 + h))
    blk = lambda: pl.BlockSpec((bq, HD), lambda h, i: (i, h))
    colv = lambda: pl.BlockSpec((None, bq, 1), lambda h, i: (h, i, 0))
    rowv = lambda: pl.BlockSpec((None, nkc, bk), lambda h, i: (h, 0, 0))
    return pl.pallas_call(
        body, grid=(NH, s // bq),
        in_specs=[blk(), full(), full(FV // HD), colv(), rowv(), colv(), blk(), pl.BlockSpec(memory_space=pl.ANY)],
        out_specs=[blk(), full(), full(FV // HD), rowv()],
        out_shape=[_sds((s, BW), F32), _sds((s, BW), F32), _sds(dzz.shape, dzz.dtype), _sds((NH, nkc, bk), F32)],
        input_output_aliases={7: 2},
        scratch_shapes=[pltpu.VMEM((nkc, bq, bk), F32), pltpu.VMEM((nkc, bq, bk), F32), pltpu.VMEM((s, HD), F32)],
        compiler_params=_cparams(("parallel", "arbitrary")), name=name)(qn, kn, zz, cum_col, cum_row, lse, do, dzz)


def _branch_f(rets, rgs, glas, ggs, ret_g, gla_g):
    out_r, out_g = [], []
    for h in range(NH):
        xc = rets[h] - jnp.mean(rets[h], axis=-1, keepdims=True)
        y = xc * lax.rsqrt(jnp.mean(xc * xc, axis=-1, keepdims=True) + EPS) * ret_g[h]
        out_r.append(_silu(rgs[h]) * y)
        x = glas[h]
        y = x * lax.rsqrt(jnp.mean(x * x, axis=-1, keepdims=True) + EPS) * gla_g
        out_g.append(_silu(ggs[h]) * y)
    return out_r, out_g


def _w_br_spec(layer):
    return pl.BlockSpec((None, 3, BW, D), lambda i: (layer, 0, 0, 0))


def mix_fwd(ret_raw, gla_raw, fox_o, zz, ret_g, gla_g, b_mg, w_br, *, name, layer):
    s = zz.shape[0]
    t = _row_tile(s)

    def body(r_ref, g_ref, f_ref, rg_ref, gg_ref, gp_ref, rgn_ref, ggn_ref, bmg_ref, w_ref, o_ref):
        rgn = rgn_ref[...]
        br_r, br_g = _branch_f(_heads(r_ref[...]), _heads(rg_ref[...]), _heads(g_ref[...]), _heads(gg_ref[...]),
                               _heads(rgn), ggn_ref[...])
        brs = [jnp.concatenate(br_r, axis=1), jnp.concatenate(br_g, axis=1), f_ref[...]]
        acc = jnp.zeros((t, D), F32)
        for b in range(3):
            gate = jax.nn.sigmoid(gp_ref[:, b * D:(b + 1) * D] + bmg_ref[:, b * D:(b + 1) * D])
            acc = acc + gate * _dot(brs[b].astype(MMT), w_ref[b])
        o_ref[...] = acc.astype(o_ref.dtype)

    row = lambda w, c=0: pl.BlockSpec((t, w), lambda i: (i, c // w))
    cst = lambda shp: pl.BlockSpec(shp, lambda i: (0,) * len(shp))
    return pl.pallas_call(
        body, grid=(s // t,),
        in_specs=[row(BW), row(BW), row(BW), row(BW, RG), row(BW, GG), row(3 * D, GP), cst((1, BW)), cst((1, HD)),
                  cst((1, 3 * D)), _w_br_spec(layer)],
        out_specs=row(D), out_shape=_sds((s, D), MMT),
        compiler_params=_cparams(("parallel",)), name=name)(ret_raw, gla_raw, fox_o, zz, zz, zz, ret_g, gla_g, b_mg, w_br)


def mix_bwd(ret_raw, gla_raw, fox_o, zz, ret_g, gla_g, b_mg, w_br, dmi, *, name, layer):
    s = zz.shape[0]
    t = _row_tile(s)

    def body(r_ref, g_ref, f_ref, rg_ref, gg_ref, gp_ref, rgn_ref, ggn_ref, bmg_ref, w_ref, dmi_ref,
             dr_ref, dg_ref, df_ref, dgp_ref, dw_ref, drgn_ref, dggn_ref, dbmg_ref):
        @pl.when(pl.program_id(0) == 0)
        def _():
            dw_ref[...] = jnp.zeros_like(dw_ref)
            drgn_ref[...] = jnp.zeros_like(drgn_ref)
            dggn_ref[...] = jnp.zeros_like(dggn_ref)
            dbmg_ref[...] = jnp.zeros_like(dbmg_ref)

        (br_r, br_g), vjp = jax.vjp(_branch_f, _heads(r_ref[...]), _heads(rg_ref[...]), _heads(g_ref[...]),
                                    _heads(gg_ref[...]), _heads(rgn_ref[...]), ggn_ref[...])
        brs = [jnp.concatenate(br_r, axis=1).astype(MMT), jnp.concatenate(br_g, axis=1).astype(MMT),
               f_ref[...].astype(MMT)]
        dmi_v = dmi_ref[...].astype(F32)
        dbr = []
        for b in range(3):
            w = w_ref[b]
            ybr = _dot(brs[b], w)
            gate = jax.nn.sigmoid(gp_ref[:, b * D:(b + 1) * D] + bmg_ref[:, b * D:(b + 1) * D])
            dgp = dmi_v * ybr * gate * (1.0 - gate)
            dgp_ref[:, b * D:(b + 1) * D] = dgp.astype(dgp_ref.dtype)
            dbmg_ref[:, b * D:(b + 1) * D] += jnp.sum(dgp, axis=0, keepdims=True)
            dy = (dmi_v * gate).astype(MMT)
            dw_ref[b] += _dot_tn(brs[b], dy)
            dbr.append(_dot_nt(dy, w))
        dr, drg, dg, dgg, drgn, dggn = vjp((_heads(dbr[0]), _heads(dbr[1])))
        dr_ref[...] = jnp.concatenate(dr, axis=1)
        dg_ref[...] = jnp.concatenate(dg, axis=1)
        df_ref[...] = dbr[2]
        dgp_ref[:, RG:RG + BW] = jnp.concatenate(drg, axis=1).astype(dgp_ref.dtype)
        dgp_ref[:, GG:GG + BW] = jnp.concatenate(dgg, axis=1).astype(dgp_ref.dtype)
        drgn_ref[...] += jnp.concatenate(drgn, axis=1)
        dggn_ref[...] += dggn

    row = lambda w, c=0: pl.BlockSpec((t, w), lambda i: (i, c // w))
    cst = lambda shp: pl.BlockSpec(shp, lambda i: (0,) * len(shp))
    return pl.pallas_call(
        body, grid=(s // t,),
        in_specs=[row(BW), row(BW), row(BW), row(BW, RG), row(BW, GG), row(3 * D, GP), cst((1, BW)), cst((1, HD)),
                  cst((1, 3 * D)), _w_br_spec(layer), row(D)],
        out_specs=[row(BW), row(BW), row(BW), row(FV), cst((3, BW, D)), cst((1, BW)), cst((1, HD)), cst((1, 3 * D))],
        out_shape=[_sds((s, BW), F32)] * 3 + [_sds((s, NZZ), MMT), _sds((3, BW, D), F32), _sds((1, BW), F32),
                                              _sds((1, HD), F32), _sds((1, 3 * D), F32)],
        compiler_params=_cparams(("arbitrary",)), name=name)(ret_raw, gla_raw, fox_o, zz, zz, zz, ret_g, gla_g, b_mg, w_br, dmi)


CT = 256


def _shift_down(x, k, rows):
    return jnp.where(rows >= k, pltpu.roll(x, k, 0), 0.0)


def _shift_up(x, k, rows, s):
    return jnp.where(rows < s - k, pltpu.roll(x, s - k, 0), 0.0)


def conv_fwd(ug, w_conv, b_conv, *, name):
    s = ug.shape[0]
    nt = DFF // CT

    def body(u_ref, g_ref, w_ref, b_ref, a_ref):
        u = u_ref[...]
        rows = lax.broadcasted_iota(jnp.int32, u.shape, 0)
        uc = b_ref[...] + w_ref[0:1, :] * _shift_down(u, 2, rows) + w_ref[1:2, :] * _shift_down(u, 1, rows) + w_ref[2:3, :] * u
        a_ref[...] = (_silu(uc) * g_ref[...]).astype(a_ref.dtype)

    return pl.pallas_call(
        body, grid=(nt,),
        in_specs=[pl.BlockSpec((s, CT), lambda j: (0, j)), pl.BlockSpec((s, CT), lambda j: (0, nt + j)),
                  pl.BlockSpec((3, CT), lambda j: (0, j)), pl.BlockSpec((1, CT), lambda j: (0, j))],
        out_specs=pl.BlockSpec((s, CT), lambda j: (0, j)), out_shape=_sds((s, DFF), MMT),
        compiler_params=_cparams(("parallel",)), name=name)(ug, ug, w_conv, b_conv)


def conv_bwd(ug, w_conv, b_conv, da, *, name):
    s = ug.shape[0]
    nt = DFF // CT

    def body(u_ref, g_ref, w_ref, b_ref, da_ref, du_ref, dg_ref, dw_ref, db_ref):
        u = u_ref[...]
        rows = lax.broadcasted_iota(jnp.int32, u.shape, 0)
        u2, u1 = _shift_down(u, 2, rows), _shift_down(u, 1, rows)
        uc = b_ref[...] + w_ref[0:1, :] * u2 + w_ref[1:2, :] * u1 + w_ref[2:3, :] * u
        sg = jax.nn.sigmoid(uc)
        da_v = da_ref[...]
        dg_ref[...] = (da_v * uc * sg).astype(dg_ref.dtype)
        duc = da_v * g_ref[...] * sg * (1.0 + uc * (1.0 - sg))
        du = w_ref[2:3, :] * duc + w_ref[1:2, :] * _shift_up(duc, 1, rows, s) + w_ref[0:1, :] * _shift_up(duc, 2, rows, s)
        du_ref[...] = du.astype(du_ref.dtype)
        dw_ref[0:1, :] = jnp.sum(duc * u2, axis=0, keepdims=True)
        dw_ref[1:2, :] = jnp.sum(duc * u1, axis=0, keepdims=True)
        dw_ref[2:3, :] = jnp.sum(duc * u, axis=0, keepdims=True)
        db_ref[...] = jnp.sum(duc, axis=0, keepdims=True)

    col = lambda: pl.BlockSpec((s, CT), lambda j: (0, j))
    return pl.pallas_call(
        body, grid=(nt,),
        in_specs=[col(), pl.BlockSpec((s, CT), lambda j: (0, nt + j)), pl.BlockSpec((3, CT), lambda j: (0, j)),
                  pl.BlockSpec((1, CT), lambda j: (0, j)), col()],
        out_specs=[col(), col(), pl.BlockSpec((3, CT), lambda j: (0, j)), pl.BlockSpec((1, CT), lambda j: (0, j))],
        out_shape=[_sds((s, DFF), MMT), _sds((s, DFF), MMT), _sds((3, DFF), F32), _sds((1, DFF), F32)],
        compiler_params=_cparams(("parallel",)), name=name)(ug, ug, w_conv, b_conv, da)


def place_tail(dzz, dlr, dff, *, name):
    s = dzz.shape[0]
    t = _row_tile(s)

    def body(a_ref, b_ref, z_ref, o_ref):
        o_ref[...] = jnp.concatenate([a_ref[...], b_ref[...]], axis=1)

    spec = pl.BlockSpec((t, HD), lambda i: (i, 0))
    return pl.pallas_call(
        body, grid=(s // t,), in_specs=[spec, spec, pl.BlockSpec(memory_space=pl.ANY)],
        out_specs=pl.BlockSpec((t, 2 * HD), lambda i: (i, LR // (2 * HD))), out_shape=_sds(dzz.shape, dzz.dtype),
        input_output_aliases={2: 0}, compiler_params=_cparams(("parallel",)), name=name)(dlr, dff, dzz)


def _tiles(s):
    return min(1024, s)


def layer_fwd(x, mod, p, cosf, sinf):
    s = x.shape[0]
    tm = _tiles(s)
    l = p["l"]
    shift1, scale1, gate1, shift2, scale2, gate2 = mod
    h = norm_mod(x, p["norm1_g"], scale1, shift1, name="norm_mod")
    zz = mm_nn(h, p["w1"], tm=tm, tn=768, out_dtype=F32, name="mm_w1", layer=l)
    ret_raw, rprev = retention_fwd(zz, cosf, sinf, name="ret_fwd")
    gla_raw, sprev = gla_fwd(zz, p["w_a2p"], p["b_gla_a"], name="gla_fwd")
    qn, kn, cum = fox_pre(zz, p["q_norm_g"], p["k_norm_g"], p["b_foxp"], name="fox_pre")
    bq, bk = _fox_blocks(s)
    cum_t = cum[:, :NH].T
    cum_col, cum_row = cum_t[:, :, None], cum_t.reshape(NH, s // bk, bk)
    fox_o, lse = fox_attn_fwd(qn, kn, zz, cum_col, cum_row, name="fox_fwd")
    if "later" in p:
        p = {**p, **p["later"](fox_o)}
    mi = mix_fwd(ret_raw, gla_raw, fox_o, zz, p["ret_norm_g"], p["gla_norm_g"], p["b_mg"], p["w_br"], name="mix_fwd",
                 layer=l)
    x1, mixed = mm_nn_residual(mi, p["w_o"], x, gate1, tm=tm, tn=512, name="mm_wo", layer=l)
    h2 = norm_mod(x1, p["norm2_g"], scale2, shift2, name="norm_mod")
    ug = mm_nn(h2, p["w_up"], tm=tm, tn=512, out_dtype=F32, name="mm_wup", layer=l)
    a = conv_fwd(ug, p["w_conv"], p["b_conv"], name="conv_fwd")
    x2, y = mm_nn_residual(a, p["w_down"], x1, gate2, tm=tm, tn=512, name="mm_wdown", layer=l)
    saved = dict(x=x, h=h, zz=zz, ret_raw=ret_raw, rprev=rprev, gla_raw=gla_raw, sprev=sprev, qn=qn, kn=kn,
                 cum_col=cum_col, cum_row=cum_row, fox_o=fox_o, lse=lse, mi=mi, mixed=mixed, x1=x1, h2=h2, ug=ug, a=a, y=y)
    return x2, saved, p


def layer_bwd(dx2, mod, p, sv, cosf, sinf, stacks, slot):
    s = dx2.shape[0]
    tm = _tiles(s)
    l = p["l"]
    shift1, scale1, gate1, shift2, scale2, gate2 = mod
    g, stacks = {}, dict(stacks)
    dy, dgate2 = gate_bwd(dx2, sv["y"], gate2, name="gate_bwd")
    stacks["w_down"] = mm_tn(sv["a"], dy, tm=min(1408, DFF), tn=512, out_dtype=MMT, name="mm_dwdown",
                             stack=stacks["w_down"], layer=slot)
    da = mm_nt(dy, p["w_down"], tm=tm, tn=1408, out_dtype=F32, name="mm_da", layer=l)
    du, dg, g["w_conv"], g["b_conv"] = conv_bwd(sv["ug"], p["w_conv"], p["b_conv"], da, name="conv_bwd")
    stacks["w_up"] = mm_tn(sv["h2"], du, tm=D, tn=CT, out_dtype=MMT, name="mm_dwup_u", stack=stacks["w_up"], layer=slot)
    stacks["w_up"] = mm_tn(sv["h2"], dg, tm=D, tn=CT, out_dtype=MMT, name="mm_dwup_g", stack=stacks["w_up"], layer=slot,
                           out_col0=DFF)
    dh2 = mm_nt2(du, dg, p["w_up"], tm=min(512, s), tn=D, name="mm_dh2", layer=l)
    dx1, g["norm2_g"], dscale2, dshift2 = norm_mod_bwd(sv["x1"], dh2, dx2, p["norm2_g"], scale2, shift2, name="norm_mod_bwd")
    dmixed, dgate1 = gate_bwd(dx1, sv["mixed"], gate1, name="gate_bwd")
    stacks["w_o"] = mm_tn(sv["mi"], dmixed, tm=512, tn=512, out_dtype=MMT, name="mm_dwo", stack=stacks["w_o"], layer=slot)
    dmi = mm_nt(dmixed, p["w_o"], tm=tm, tn=512, out_dtype=MMT, name="mm_dmi", layer=l)
    zz = sv["zz"]
    (dret, dgla, dfox, dzz, g["w_br"], g["ret_norm_g"], g["gla_norm_g"], g["b_mg"]) = mix_bwd(
        sv["ret_raw"], sv["gla_raw"], sv["fox_o"], zz, p["ret_norm_g"], p["gla_norm_g"], p["b_mg"], p["w_br"], dmi,
        name="mix_bwd", layer=l)
    dqn, dkn, dzz, dcum_row = fox_attn_bwd(sv["qn"], sv["kn"], zz, sv["cum_col"], sv["cum_row"], sv["lse"], dfox, dzz,
                                           name="fox_bwd")
    dcum = jnp.pad(dcum_row.reshape(NH, s).T, ((0, 0), (0, HD - NH)))
    dzz, dff, g["q_norm_g"], g["k_norm_g"], g["b_foxp"] = fox_pre_bwd(
        zz, p["q_norm_g"], p["k_norm_g"], p["b_foxp"], dqn, dkn, dcum, dzz, name="fox_pre_bwd")
    dzz, dlr, g["w_a2p"], g["b_gla_a"] = gla_bwd(zz, p["w_a2p"], p["b_gla_a"], sv["sprev"], dgla, dzz, name="gla_bwd")
    dzz = retention_bwd(zz, cosf, sinf, sv["rprev"], dret, dzz, name="ret_bwd")
    dzz = place_tail(dzz, dlr, dff, name="place_tail")
    stacks["w_mg"] = mm_tn(sv["h"], dzz, tm=512, tn=768, out_dtype=MMT, name="mm_dwmg", ncols=WZ0, stack=stacks["w_mg"],
                           layer=slot)
    dwz = mm_tn(sv["h"], dzz, tm=512, tn=768, out_dtype=MMT, name="mm_dwz", col0=WZ0)
    stacks["w_in"] = unalign_dw_in(dwz, stacks["w_in"], slot)
    dh = mm_nt(dzz, p["w1"], tm=min(256, s), tn=D, out_dtype=F32, name="mm_dh", layer=l)
    dx, g["norm1_g"], dscale1, dshift1 = norm_mod_bwd(sv["x"], dh, dx1, p["norm1_g"], scale1, shift1, name="norm_mod_bwd")
    dmod = jnp.concatenate([dshift1, dscale1, dgate1, dshift2, dscale2, dgate2], axis=1)
    return dx, g, dmod, stacks


def _align_cols(w_in, w_mg):
    z = lambda n: jnp.zeros((w_in.shape[0], n), w_in.dtype)
    seg = lambda name: w_in[:, W_IN_COLS[name][0]:W_IN_COLS[name][1]]
    return jnp.concatenate([w_mg, seg("rg"), seg("gg"), seg("fv"), seg("rqkv"), seg("gqkv"), seg("fqk"), seg("lr"),
                            z(HD - GLR), seg("ff"), z(HD - NH)], axis=1)


def _unalign_cols(dwz):
    seg = lambda c0, name: dwz[:, c0 - WZ0:c0 - WZ0 + W_IN_COLS[name][1] - W_IN_COLS[name][0]]
    return jnp.concatenate([seg(RQ, "rqkv"), seg(RG, "rg"), seg(GQ, "gqkv"), seg(LR, "lr"), seg(GG, "gg"), seg(FQ, "fqk"),
                            seg(FV, "fv"), seg(FF, "ff")], axis=1)


def build_w1(w_in_sh, w_mg):
    nl = w_mg.shape[0]
    t = _row_tile(D)

    def body(s_ref, g_ref, o_ref):
        o_ref[...] = _align_cols(jnp.concatenate([s_ref[k] for k in range(4)], axis=1), g_ref[...])

    return pl.pallas_call(
        body, grid=(nl, D // t),
        in_specs=[pl.BlockSpec((None, 4, t, IN_W // 4), lambda l, i: (l, 0, i, 0)), pl.BlockSpec((None, t, WZ0), lambda l, i: (l, i, 0))],
        out_specs=pl.BlockSpec((None, t, NZZ), lambda l, i: (l, i, 0)), out_shape=_sds((nl, D, NZZ), w_mg.dtype),
        compiler_params=_cparams(("parallel", "parallel")), name="build_w1")(w_in_sh, w_mg)


def unalign_dw_in(dwz, stack, layer):
    t = _row_tile(D)

    def body(z_ref, s_ref, o_ref):
        w = _unalign_cols(z_ref[...])
        for k in range(4):
            o_ref[k] = w[:, k * (IN_W // 4):(k + 1) * (IN_W // 4)]

    return pl.pallas_call(
        body, grid=(D // t,),
        in_specs=[pl.BlockSpec((t, NZZ - WZ0), lambda i: (i, 0)), pl.BlockSpec(memory_space=pl.ANY)],
        out_specs=pl.BlockSpec((None, 4, t, IN_W // 4), lambda i: (layer, 0, i, 0)), out_shape=_sds(stack.shape, stack.dtype),
        input_output_aliases={1: 0}, compiler_params=_cparams(("parallel",)), name="unalign_dw_in")(dwz, stack)


LATE_WEIGHTS = ("w_br", "w_o", "w_up", "w_down")


def layer_params(w, big, l, later=None):
    row = lambda v: v[l][None, :]
    p = dict(
        l=0, norm1_g=row(w["norm1_g"]), norm2_g=row(w["norm2_g"]), w1=big["w1"],
        w_a2p=jnp.pad(w["w_gla_a2"][l], ((0, HD - GLR), (0, 0))), b_gla_a=row(w["b_gla_a"]),
        b_foxp=jnp.pad(row(w["b_fox_f"]), ((0, 0), (0, HD - NH))), ret_norm_g=row(w["ret_norm_g"]),
        gla_norm_g=row(w["gla_norm_g"]), q_norm_g=row(w["q_norm_g"]), k_norm_g=row(w["k_norm_g"]),
        b_mg=row(w["b_mg"]), w_conv=w["w_conv"][l], b_conv=row(w["b_conv"]))
    if later is None:
        p.update({n: big[n] for n in LATE_WEIGHTS})
    else:
        p["later"] = later
    return p


def layer_grads(g):
    vec = lambda v: v[0]
    return dict(
        norm1_g=vec(g["norm1_g"]), norm2_g=vec(g["norm2_g"]), w_gla_a2=g["w_a2p"][:GLR], b_gla_a=vec(g["b_gla_a"]),
        b_fox_f=g["b_foxp"][0, :NH], ret_norm_g=vec(g["ret_norm_g"]), gla_norm_g=vec(g["gla_norm_g"]),
        q_norm_g=vec(g["q_norm_g"]), k_norm_g=vec(g["k_norm_g"]), w_br=g["w_br"], b_mg=vec(g["b_mg"]),
        w_conv=g["w_conv"], b_conv=vec(g["b_conv"]))


def ada_mod(c_all, w_ada, b_ada):
    nl, _, n = w_ada.shape

    def body(c_ref, w_ref, b_ref, o_ref):
        o_ref[...] = _dot(_silu(c_ref[...]), w_ref[...], HI) + b_ref[...]

    return pl.pallas_call(
        body, grid=(nl,),
        in_specs=[pl.BlockSpec((8, D), lambda l: (0, 0)), pl.BlockSpec((None, D, n), lambda l: (l, 0, 0)),
                  pl.BlockSpec((None, 1, n), lambda l: (l, 0, 0))],
        out_specs=pl.BlockSpec((None, 8, n), lambda l: (l, 0, 0)), out_shape=_sds((nl, 8, n), F32),
        compiler_params=_cparams(("parallel",)), name="ada_mod")(c_all, w_ada, b_ada)


def ada_dw(c_all, dmod):
    nl, _, n = dmod.shape

    def body(c_ref, d_ref, o_ref):
        o_ref[...] = _dot_tn(_silu(c_ref[...]), d_ref[...], HI)

    return pl.pallas_call(
        body, grid=(nl,),
        in_specs=[pl.BlockSpec((8, D), lambda l: (0, 0)), pl.BlockSpec((None, 8, n), lambda l: (l, 0, 0))],
        out_specs=pl.BlockSpec((None, D, n), lambda l: (l, 0, 0)), out_shape=_sds((nl, D, n), F32),
        compiler_params=_cparams(("parallel",)), name="ada_dw")(c_all, dmod)


def sum_devices(g):
    def body(g_ref, o_ref):
        acc = g_ref[0]
        for d in range(1, 8):
            acc = acc + g_ref[d]
        o_ref[...] = acc

    return pl.pallas_call(body, out_shape=_sds(g.shape[1:], F32), name="sum_devices")(g)


def adamw(w, g, m, v, *, block, name, rows=None, into=None):
    nd = w.ndim
    lo, hi = (0, w.shape[0]) if rows is None else rows
    grid = ((hi - lo) // block[0],) + tuple(w.shape[i] // block[i] for i in range(1, nd))
    first = lo // block[0]
    bc1 = 1.0 - ADAM_B1 ** ADAM_STEP
    bc2 = 1.0 - ADAM_B2 ** ADAM_STEP

    def body(w_ref, g_ref, m_ref, v_ref, *rest):
        d_ref, nm_ref, nv_ref = rest[-3:]
        gv = g_ref[...]
        nm = ADAM_B1 * m_ref[...] + (1.0 - ADAM_B1) * gv
        nv = ADAM_B2 * v_ref[...] + (1.0 - ADAM_B2) * (gv * gv)
        nm_ref[...] = nm
        nv_ref[...] = nv
        d_ref[...] = -ADAM_LR * ((nm / bc1) / (jnp.sqrt(nv / bc2) + ADAM_EPS) + ADAM_WD * w_ref[...])

    spec = pl.BlockSpec(tuple(block), lambda i, *j: (first + i,) + j)
    given = [] if into is None else list(into)
    return pl.pallas_call(
        body, grid=grid, in_specs=[spec] * 4 + [pl.BlockSpec(memory_space=pl.ANY)] * len(given), out_specs=[spec] * 3,
        out_shape=[_sds(w.shape, F32)] * 3, input_output_aliases={4 + i: i for i in range(len(given))},
        compiler_params=_cparams(("parallel",) * nd), name=name)(w, g, m, v, *given)


MESH = pl.DeviceIdType.MESH
ANY = pl.BlockSpec(memory_space=pl.ANY)
VM = pl.BlockSpec(memory_space=pltpu.VMEM)


def _place():
    x, y, c = lax.axis_index("x"), lax.axis_index("y"), lax.axis_index("c")
    return x, y, c, [(1 - x, y), (x, 1 - y), (1 - x, 1 - y)]


def small_allgather(v, *, name):
    m_per, n = v.shape

    def body(x_ref, out_ref, send_sems, recv_sems, local_sem):
        x, y, c, chips = _place()
        me, sibling = (x, y, c), (x, y, 1 - c)

        def rows(px, py, pc):
            return out_ref.at[pl.ds((4 * px + 2 * py + pc) * m_per, m_per), :]

        def copy(k, block, to, src=None):
            return pltpu.make_async_remote_copy(
                src_ref=rows(*block) if src is None else src, dst_ref=rows(*block),
                send_sem=send_sems.at[k], recv_sem=recv_sems.at[k], device_id=to, device_id_type=MESH)

        mine = pltpu.make_async_copy(x_ref, rows(*me), local_sem)
        mine.start()
        first = [copy(0, me, sibling, src=x_ref)]
        first += [copy(1 + j, me, (*chip, c), src=x_ref) for j, chip in enumerate(chips)]
        for cp in first:
            cp.start()
        passed = [copy(4 + j, (*chip, c), sibling) for j, chip in enumerate(chips)]
        for j, chip in enumerate(chips):
            copy(1 + j, (*chip, c), me).wait_recv()
            passed[j].start()
        copy(0, sibling, me).wait_recv()
        for j, chip in enumerate(chips):
            copy(4 + j, (*chip, 1 - c), me).wait_recv()
        for cp in first + passed:
            cp.wait_send()
        mine.wait()

    return pl.pallas_call(
        body, out_shape=_sds((8 * m_per, n), v.dtype), in_specs=[VM], out_specs=VM,
        scratch_shapes=[pltpu.SemaphoreType.DMA((7,)), pltpu.SemaphoreType.DMA((7,)), pltpu.SemaphoreType.DMA],
        name=name)(v)


TENSORS = {
    "w_in": ("lead", None, (4, D, 1285), (1, D, 1285)),
    "w_mg": ("col", 768, (D, 3072), (512, 3072)),
    "w_br": ("col", 256, (3, BW, D), (3, BW, D)),
    "w_o": ("row", 256, (D, D), (D, D)),
    "w_up": ("col", 1408, (D, 5632), (256, 5632)),
    "w_down": ("row", 704, (DFF, D), (704, D)),
}
BIG = tuple(TENSORS)


def _shard_shape(name):
    kind, width, full, _ = TENSORS[name]
    if kind == "lead":
        return full[1:]
    return full[:-1] + (width,) if kind == "col" else (width,) + full[1:]


def _shard_view(ref, layers, name, k):
    kind, width, full, _ = TENSORS[name]
    if kind == "lead":
        return ref.at[layers, k]
    if kind == "row":
        return ref.at[layers, pl.ds(k * width, width)]
    return ref.at[(layers,) + (slice(None),) * (len(full) - 1) + (pl.ds(k * width, width),)]


def _remote(send_sems, recv_sems, k, src, dst, to):
    return pltpu.make_async_remote_copy(src_ref=src, dst_ref=dst, send_sem=send_sems.at[k], recv_sem=recv_sems.at[k],
                                        device_id=to, device_id_type=MESH)


def _dma_sems(n):
    return [pltpu.SemaphoreType.DMA((n,)), pltpu.SemaphoreType.DMA((n,))]


RS_GROUP = 2
HBM = pl.BlockSpec(memory_space=pltpu.HBM)
SEM = pl.BlockSpec(memory_space=pltpu.SEMAPHORE)
SPLIT_CALL = dict(compiler_params=pltpu.CompilerParams(has_side_effects=pltpu.SideEffectType.DATAFLOW_SIDE_EFFECTING))
PULL_SET = (("w_mg", "w_up", "w_o"), ("w_in", "w_br", "w_down"))
FIRST_NEEDED = ("w_in", "w_mg")
NEEDED_LATER = tuple(n for n in BIG if n not in FIRST_NEEDED)


def _in_hbm(a):
    return pltpu.with_memory_space_constraint(a, pltpu.HBM)


def _pull_sends(send_sems, recv_sems, p, o, layer, core, x, y, chips, names=BIG):
    return [_remote(send_sems, recv_sems, 3 * BIG.index(n) + j, p[n].at[layer], _shard_view(o[n], 0, n, 2 * x + y), (*chip, core))
            for n in PULL_SET[core] if n in names for j, chip in enumerate(chips)]


def _pull_arrivals(send_sems, recv_sems, o, core, x, y, chips, to, names=BIG):
    views = [(3 * BIG.index(n) + j, _shard_view(o[n], 0, n, 2 * chip[0] + chip[1]))
             for n in PULL_SET[core] if n in names for j, chip in enumerate(chips)]
    return [_remote(send_sems, recv_sems, k, v, v, to) for k, v in views]


def gather_start(shards, layer, *, name, after=None, names=BIG, lands=None):
    nt = len(BIG)
    first = [] if after is None else [after]

    def body(*refs):
        p, o = dict(zip(BIG, refs[:nt])), dict(zip(BIG, refs[nt:2 * nt]))
        ss, rs = refs[2 * nt + len(first)], refs[2 * nt + len(first) + 1]
        x, y, c, chips = _place()
        for core in (0, 1):
            @pl.when(c == core)
            def _():
                for cp in _pull_sends(ss, rs, p, o, layer, core, x, y, chips, names):
                    cp.start()
        refs[-1][...] = jnp.zeros_like(refs[-1])

    if lands is None:
        lands = [lax.empty((1,) + TENSORS[n][2], shards[n].dtype) for n in BIG]
    lands = [_in_hbm(a) for a in lands]
    outs = pl.pallas_call(
        body,
        out_shape=(pltpu.SemaphoreType.DMA((3 * nt,)), pltpu.SemaphoreType.DMA((3 * nt,)),
                   *[pltpu.HBM(a.shape, a.dtype) for a in lands], _sds((8, HD), F32)),
        in_specs=[HBM] * (2 * nt) + [ANY] * len(first), out_specs=(SEM, SEM, *[HBM] * nt, VM),
        input_output_aliases={nt + t: 2 + t for t in range(nt)}, name=name, **SPLIT_CALL)(
            *[_in_hbm(shards[n]) for n in BIG], *lands, *first)
    return outs[0], outs[1], outs[2:2 + nt], outs[-1]


def gather_wait(send_sems, recv_sems, shards, lands, after, layer, *, name, names=BIG):
    nt = len(BIG)

    def body(*refs):
        p, o = dict(zip(BIG, refs[:nt])), dict(zip(BIG, refs[nt:2 * nt]))
        ss, rs = refs[2 * nt], refs[2 * nt + 1]
        x, y, c, chips = _place()
        for core in (0, 1):
            @pl.when(c == core)
            def _():
                for cp in _pull_sends(ss, rs, p, o, layer, core, x, y, chips, names):
                    cp.wait_send()
                for cp in _pull_arrivals(ss, rs, o, core, x, y, chips, (x, y, core), names):
                    cp.wait_recv()

    return pl.pallas_call(
        body, out_shape=tuple(pltpu.HBM(a.shape, a.dtype) for a in lands),
        in_specs=[HBM] * (2 * nt) + [SEM, SEM, ANY], out_specs=tuple([HBM] * nt),
        input_output_aliases={nt + t: t for t in range(nt)}, name=name, **SPLIT_CALL)(
            *[_in_hbm(shards[n]) for n in BIG], *lands, send_sems, recv_sems, after)


def gather_forward(shards, lands, layer, *, name, names=BIG):
    nt = len(BIG)

    def body(*refs):
        p, o = dict(zip(BIG, refs[:nt])), dict(zip(BIG, refs[2 * nt:3 * nt]))
        ss, rs = refs[3 * nt:]
        x, y, c, chips = _place()
        for core in (0, 1):
            @pl.when(c == core)
            def _():
                me, sibling = (x, y, core), (x, y, 1 - core)
                sends = _pull_arrivals(ss, rs, o, core, x, y, chips, sibling, names)
                sends += [_remote(ss, rs, 3 * nt + t, p[n].at[layer], _shard_view(o[n], 0, n, 2 * x + y), sibling)
                          for t, n in enumerate(BIG) if n in names]
                for cp in sends:
                    cp.start()
                for cp in sends:
                    cp.wait_send()
                for cp in _pull_arrivals(ss, rs, o, 1 - core, x, y, chips, me, names):
                    cp.wait_recv()
                for t, n in enumerate(BIG):
                    if n in names:
                        own = _shard_view(o[n], 0, n, 2 * x + y)
                        _remote(ss, rs, 3 * nt + t, own, own, me).wait_recv()

    outs = pl.pallas_call(
        body, out_shape=[_sds(a.shape, a.dtype) for a in lands], in_specs=[ANY] * (2 * nt), out_specs=[ANY] * nt,
        input_output_aliases={nt + t: t for t in range(nt)}, scratch_shapes=_dma_sems(4 * nt), name=name)(
            *[shards[n] for n in BIG], *lands)
    return dict(zip(BIG, outs))


def pair_exchange(g, *, name):
    hh = g[BIG[0]].shape[0] // 2
    nt = len(BIG)

    def body(*refs):
        send_sems, recv_sems = refs[2 * nt:]
        x, y, c, _ = _place()
        copies = [_remote(send_sems, recv_sems, t, refs[t].at[pl.ds(hh * (1 - c), hh)], refs[nt + t], (x, y, 1 - c))
                  for t in range(nt)]
        for cp in copies:
            cp.start()
        for cp in copies:
            cp.wait()

    outs = pl.pallas_call(
        body, out_shape=[_sds((hh,) + g[n].shape[1:], g[n].dtype) for n in BIG], in_specs=[ANY] * nt, out_specs=[ANY] * nt,
        scratch_shapes=_dma_sems(nt), name=name)(*[g[n] for n in BIG])
    return dict(zip(BIG, outs))


def _chip_copies(send_sems, recv_sems, s_refs, land_refs, c, chips):
    hl = s_refs[0].shape[0]
    return [_remote(send_sems, recv_sems, 3 * t + j, _shard_view(s_refs[t], pl.ds(0, hl), n, 2 * chip[0] + chip[1]),
                    land_refs[t].at[j], (*chip, c))
            for t, n in enumerate(BIG) for j, chip in enumerate(chips)]


def _landing_shapes(s):
    hl = s[BIG[0]].shape[0]
    return [_sds((3, hl) + _shard_shape(n), s[n].dtype) for n in BIG]


def chip_exchange(s, *, name):
    nt = len(BIG)

    def body(*refs):
        send_sems, recv_sems = refs[2 * nt:]
        x, y, c, chips = _place()
        copies = _chip_copies(send_sems, recv_sems, refs[:nt], refs[nt:2 * nt], c, chips)
        for cp in copies:
            cp.start()
        for cp in copies:
            cp.wait()

    outs = pl.pallas_call(
        body, out_shape=_landing_shapes(s), in_specs=[ANY] * nt, out_specs=[ANY] * nt,
        scratch_shapes=_dma_sems(3 * nt), name=name)(*[s[n] for n in BIG])
    return dict(zip(BIG, outs))


def chip_exchange_start(s, *, name, after=None):
    nt = len(BIG)
    first = [] if after is None else [after]

    def body(*refs):
        o = refs[2 * nt + len(first):]
        x, y, c, chips = _place()
        for cp in _chip_copies(o[0], o[1], refs[:nt], refs[nt:2 * nt], c, chips):
            cp.start()
        refs[-1][...] = jnp.zeros_like(refs[-1])

    lands = [_in_hbm(lax.empty(d.shape, d.dtype)) for d in _landing_shapes(s)]
    srcs = [_in_hbm(s[n]) for n in BIG]
    outs = pl.pallas_call(
        body,
        out_shape=(pltpu.SemaphoreType.DMA((3 * nt,)), pltpu.SemaphoreType.DMA((3 * nt,)),
                   *[pltpu.HBM(a.shape, a.dtype) for a in srcs + lands], _sds((8, HD), F32)),
        in_specs=[HBM] * (2 * nt) + [ANY] * len(first), out_specs=(SEM, SEM, *[HBM] * (2 * nt), VM),
        input_output_aliases={t: 2 + t for t in range(2 * nt)}, name=name, **SPLIT_CALL)(*srcs, *lands, *first)
    return outs[0], outs[1], outs[2:2 + nt], outs[2 + nt:2 + 2 * nt], outs[-1]


def chip_exchange_wait(send_sems, recv_sems, srcs, lands, after, *, name):
    nt = len(BIG)

    def body(*refs):
        x, y, c, chips = _place()
        for cp in _chip_copies(refs[2 * nt], refs[2 * nt + 1], refs[:nt], refs[nt:2 * nt], c, chips):
            cp.wait_send()
            cp.wait_recv()

    outs = pl.pallas_call(
        body, out_shape=tuple(pltpu.HBM(a.shape, a.dtype) for a in list(srcs) + list(lands)),
        in_specs=[HBM] * (2 * nt) + [SEM, SEM] + [ANY] * len(after), out_specs=tuple([HBM] * (2 * nt)),
        input_output_aliases={t: t for t in range(2 * nt)}, name=name, **SPLIT_CALL)(
            *srcs, *lands, send_sems, recv_sems, *after)
    return dict(zip(BIG, outs[:nt])), dict(zip(BIG, outs[nt:]))


def pair_share(f, l0, hh, *, name):
    nt = len(BIG)

    def body(*refs):
        o = refs[nt:2 * nt]
        send_sems, recv_sems = refs[2 * nt:]
        x, y, c, _ = _place()
        mine, theirs = pl.ds(l0 + hh * c, hh), pl.ds(l0 + hh * (1 - c), hh)
        copies = [_remote(send_sems, recv_sems, t, o[t].at[mine], o[t].at[mine], (x, y, 1 - c)) for t in range(nt)]
        for cp in copies:
            cp.start()
        for t, cp in enumerate(copies):
            cp.wait_send()
            _remote(send_sems, recv_sems, t, o[t].at[theirs], o[t].at[theirs], (x, y, c)).wait_recv()

    outs = pl.pallas_call(
        body, out_shape=[_sds(f[n].shape, f[n].dtype) for n in BIG], in_specs=[ANY] * nt, out_specs=[ANY] * nt,
        input_output_aliases={t: t for t in range(nt)}, scratch_shapes=_dma_sems(nt), name=name)(*[f[n] for n in BIG])
    return dict(zip(BIG, outs))


def pair_add(g, r, idx, *, tensor, name):
    _, _, full, blk = TENSORS[tensor]
    hh = r.shape[0]

    def body(idx_ref, g_ref, r_ref, o_ref):
        o_ref[...] = (g_ref[...].astype(F32) + r_ref[...].astype(F32)).astype(o_ref.dtype)

    own = pl.BlockSpec((None,) + blk, lambda *a: (a[0],) + a[1:-1])
    return pl.pallas_call(
        body, out_shape=_sds(r.shape, r.dtype),
        grid_spec=pltpu.PrefetchScalarGridSpec(
            num_scalar_prefetch=1, grid=(hh,) + tuple(f // b for f, b in zip(full, blk)),
            in_specs=[pl.BlockSpec((None,) + blk, lambda *a: (hh * a[-1][0] + a[0],) + a[1:-1]), own], out_specs=own),
        compiler_params=_cparams(("parallel",) * (1 + len(full))), name=name)(idx, g, r)


def chip_add(s, r, idx, totals, l0, *, tensor, name):
    kind, width, full, _ = TENSORS[tensor]
    shard = _shard_shape(tensor)
    hh = s.shape[0]
    zeros = (0,) * len(shard)

    def body(idx_ref, s_ref, r0_ref, r1_ref, r2_ref, t_ref, o_ref):
        o_ref[...] = ((s_ref[...].astype(F32) + r0_ref[...].astype(F32)) + r1_ref[...].astype(F32)) + r2_ref[...].astype(F32)

    if kind == "lead":
        mine = pl.BlockSpec((None, None) + shard, lambda i, ix: (i, ix[1]) + zeros)
    elif kind == "row":
        mine = pl.BlockSpec((None,) + shard, lambda i, ix: (i, ix[1]) + zeros[1:])
    else:
        mine = pl.BlockSpec((None,) + shard, lambda i, ix: (i,) + zeros[1:] + (ix[1],))
    peer = lambda j: pl.BlockSpec((None, None) + shard, lambda i, ix: (j, i) + zeros)
    return pl.pallas_call(
        body, out_shape=_sds(totals.shape, F32),
        grid_spec=pltpu.PrefetchScalarGridSpec(
            num_scalar_prefetch=1, grid=(hh,), in_specs=[mine, peer(0), peer(1), peer(2), pl.BlockSpec(memory_space=pl.ANY)],
            out_specs=pl.BlockSpec((None,) + shard, lambda i, ix: (l0 + hh * ix[0] + i,) + zeros)),
        input_output_aliases={5: 0}, compiler_params=_cparams(("parallel",)), name=name)(idx, s, r, r, r, totals)


def _flat_rows(arrs):
    v = jnp.concatenate([a.reshape(-1) for a in arrs])
    n = -(-v.shape[0] // 1024) * 1024
    return jnp.pad(v, (0, n - v.shape[0])).reshape(n // HD, HD)


def _unflat(buf, shapes):
    v, out, o = buf.reshape(-1), [], 0
    for s in shapes:
        n = int(np.prod(s))
        out.append(v[o:o + n].reshape(s))
        o += n
    return out


WEIGHTS = ("norm1_g", "norm2_g", "w_ada", "b_ada", "w_in", "w_gla_a2", "b_gla_a", "b_fox_f", "ret_norm_g", "gla_norm_g",
           "q_norm_g", "k_norm_g", "w_br", "w_mg", "b_mg", "w_o", "w_up", "w_conv", "b_conv", "w_down")
REPLICATED = ("norm1_g", "norm2_g", "b_gla_a", "b_fox_f", "ret_norm_g", "gla_norm_g", "q_norm_g", "k_norm_g", "b_mg", "b_conv")
ADAM_BLOCKS = dict(w_ada=(1, 256, 1536), w_in=(1, 256, 1285), w_br=(1, 3, BW, 256), w_mg=(1, 512, 768), w_o=(2, 256, D),
                   w_up=(1, 256, 1408), w_down=(1, 352, D))
ALL_AXES = ("x", "y", "c")


def kernel(x, c, norm1_g, norm2_g, w_ada, b_ada, w_in, w_gla_a2, b_gla_a, b_fox_f, ret_norm_g, gla_norm_g, q_norm_g, k_norm_g, w_br, w_mg, b_mg, w_o, w_up, w_conv, b_conv, w_down, loss_target, m_norm1_g, m_norm2_g, m_w_ada, m_b_ada, m_w_in, m_w_gla_a2, m_b_gla_a, m_b_fox_f, m_ret_norm_g, m_gla_norm_g, m_q_norm_g, m_k_norm_g, m_w_br, m_w_mg, m_b_mg, m_w_o, m_w_up, m_w_conv, m_b_conv, m_w_down, v_norm1_g, v_norm2_g, v_w_ada, v_b_ada, v_w_in, v_w_gla_a2, v_b_gla_a, v_b_fox_f, v_ret_norm_g, v_gla_norm_g, v_q_norm_g, v_k_norm_g, v_w_br, v_w_mg, v_b_mg, v_w_o, v_w_up, v_w_conv, v_b_conv, v_w_down):
    w = dict(zip(WEIGHTS, (norm1_g, norm2_g, w_ada, b_ada, w_in, w_gla_a2, b_gla_a, b_fox_f, ret_norm_g, gla_norm_g,
                           q_norm_g, k_norm_g, w_br, w_mg, b_mg, w_o, w_up, w_conv, b_conv, w_down)))
    m = dict(zip(WEIGHTS, (m_norm1_g, m_norm2_g, m_w_ada, m_b_ada, m_w_in, m_w_gla_a2, m_b_gla_a, m_b_fox_f, m_ret_norm_g,
                           m_gla_norm_g, m_q_norm_g, m_k_norm_g, m_w_br, m_w_mg, m_b_mg, m_w_o, m_w_up, m_w_conv, m_b_conv,
                           m_w_down)))
    v = dict(zip(WEIGHTS, (v_norm1_g, v_norm2_g, v_w_ada, v_b_ada, v_w_in, v_w_gla_a2, v_b_gla_a, v_b_fox_f, v_ret_norm_g,
                           v_gla_norm_g, v_q_norm_g, v_k_norm_g, v_w_br, v_w_mg, v_b_mg, v_w_o, v_w_up, v_w_conv, v_b_conv,
                           v_w_down)))
    nl = norm1_g.shape[0]
    seq = x.shape[1]
    xi, yi, ci = lax.axis_index("x"), lax.axis_index("y"), lax.axis_index("c")
    k_me = 2 * xi + yi
    b_me = 4 * xi + 2 * yi + ci
    ada_n = w_ada.shape[2]
    a2_n, conv_n = w_gla_a2.shape[2], w_conv.shape[2]

    shards = {n: w[n].astype(MMT) for n in BIG}

    def gather_finish(l, started, after):
        send_sems, recv_sems, lands, _ = started
        lands = gather_wait(send_sems, recv_sems, shards, lands, after, l, name=f"gather{l}_wait")
        big = gather_forward(shards, lands, l, name=f"gather{l}_forward")
        big["w1"] = build_w1(big["w_in"], big["w_mg"])
        return big

    blk = _flat_rows([c, w_gla_a2, w_conv])
    g1 = small_allgather(blk, name="gather_small").reshape(8, blk.shape[0], HD)
    c_all = g1[:, :D // HD].reshape(8, D)
    by_chip = g1[0::2].reshape(4, -1)[:, D:]
    a2_sh, conv_sh = by_chip[:, :nl * GLR * a2_n], by_chip[:, nl * GLR * a2_n:nl * (GLR * a2_n + 3 * conv_n)]
    full_small = dict(
        w_gla_a2=a2_sh.reshape(4, nl, GLR, a2_n).transpose(1, 2, 0, 3).reshape(nl, GLR, 4 * a2_n),
        w_conv=conv_sh.reshape(4, nl, 3, conv_n).transpose(1, 2, 0, 3).reshape(nl, 3, 4 * conv_n))

    b_ada_sh = lax.dynamic_slice_in_dim(b_ada, k_me * ada_n, ada_n, axis=1)[:, None, :]
    mod_sh = ada_mod(c_all, w_ada, b_ada_sh)
    g2 = small_allgather(mod_sh.reshape(nl * 8, ada_n), name="gather_mod").reshape(4, 2, nl, 8, ada_n)[:, 0]
    mod_me = lax.dynamic_index_in_dim(g2, b_me, axis=2, keepdims=False).transpose(1, 0, 2).reshape(nl, 4 * ada_n)

    wsmall = {n: w[n] for n in REPLICATED}
    wsmall.update(full_small)
    mods = [[mod_me[l:l + 1, i * D:(i + 1) * D] for i in range(6)] for l in range(nl)]

    send_sems, recv_sems, lands, token = gather_start(shards, 0, name="gather0_start_first", after=mod_me, names=FIRST_NEEDED)
    lands = gather_wait(send_sems, recv_sems, shards, lands, token, 0, name="gather0_wait_first", names=FIRST_NEEDED)
    first = gather_forward(shards, lands, 0, name="gather0_forward_first", names=FIRST_NEEDED)
    later_sems = gather_start(shards, 0, name="gather0_start_later", names=NEEDED_LATER, lands=[first[n] for n in BIG])
    in_flight = dict(zip(BIG, later_sems[2]))
    big = {"w1": build_w1(in_flight["w_in"], in_flight["w_mg"])}

    def rest_of_layer0(after):
        rest = gather_wait(later_sems[0], later_sems[1], shards, later_sems[2], after, 0, name="gather0_wait_later",
                           names=NEEDED_LATER)
        return gather_forward(shards, rest, 0, name="gather0_forward_later", names=NEEDED_LATER)

    cosf, sinf = _rope_tables(seq)
    xs, saved, params = x[0], [], []
    for l in range(nl):
        if l + 1 < nl:
            started = gather_start(shards, l + 1, name=f"gather{l + 1}_start", after=xs if l else big["w1"])
            mods[l][1] = mods[l][1] + started[-1][0, 0]
        xs, sv, p = layer_fwd(xs, mods[l], layer_params(wsmall, big, l, later=rest_of_layer0 if l == 0 else None), cosf, sinf)
        saved.append(sv)
        params.append(p)
        if l + 1 < nl:
            big = gather_finish(l + 1, started, xs)
    loss_part, dx = loss_and_grad(xs, loss_target[0], name="loss")
    loss = lax.psum(loss_part[0, 0], ALL_AXES)
    grads, dmods = [None] * nl, [None] * nl
    idx = jnp.stack([ci, k_me]).astype(jnp.int32)
    totals = {n: lax.empty((nl,) + _shard_shape(n), F32) for n in BIG}

    def finish_group(pending, after, totals, idx):
        group, send_sems, recv_sems, srcs, lands, _ = pending
        sums, from_chips = chip_exchange_wait(send_sems, recv_sems, srcs, lands, after, name=f"rs{group}_chip_exchange_wait")
        totals = {n: chip_add(sums[n], from_chips[n], idx, totals[n], RS_GROUP * group, tensor=n,
                              name=f"rs{group}_chip_add_{n}") for n in BIG}
        return pair_share(totals, RS_GROUP * group, RS_GROUP // 2, name=f"rs{group}_pair_share")

    pending = None
    for group in reversed(range(nl // RS_GROUP)):
        layers = range(RS_GROUP * group, RS_GROUP * (group + 1))
        stacks = {n: lax.empty((RS_GROUP,) + TENSORS[n][2], MMT) for n in BIG if n != "w_br"}
        if pending is not None:
            mods[layers[-1]][5] = mods[layers[-1]][5] + pending[-1][0, 0]
        for l in reversed(layers):
            dx, g, dmods[l], stacks = layer_bwd(dx, mods[l], params[l], saved[l], cosf, sinf, stacks, l - layers[0])
            grads[l] = layer_grads(g)
        stacks["w_br"] = jnp.stack([grads[l]["w_br"].astype(MMT) for l in layers])
        from_sibling = pair_exchange(stacks, name=f"rs{group}_pair_exchange")
        chip_sum = {n: pair_add(stacks[n], from_sibling[n], idx, tensor=n, name=f"rs{group}_pair_add_{n}") for n in BIG}
        if group > 0:
            if pending is not None:
                totals = finish_group(pending, [dx, *chip_sum.values()], totals, idx)
            pending = (group, *chip_exchange_start(chip_sum, name=f"rs{group}_chip_exchange_start"))

    small_names = REPLICATED + ("w_gla_a2", "w_conv")
    small_shapes = [(nl, 6 * D)] + [(nl,) + grads[0][n].shape for n in small_names]
    vec = _flat_rows([jnp.concatenate(dmods, axis=0)] + [jnp.stack([grads[l][n] for l in range(nl)]) for n in small_names])
    gs = small_allgather(vec, name="gather_small_grads")
    earlier = pending
    pending = (0, *chip_exchange_start(chip_sum, name="rs0_chip_exchange_start", after=gs))
    if earlier is not None:
        totals = finish_group(earlier, [dx, pending[-1]], totals, idx)
    gs = (gs + pending[-1][0, 0]).reshape(8, vec.shape[0], HD)
    summed = _unflat(sum_devices(gs), small_shapes)
    grad = dict(zip(small_names, summed[1:]))
    grad["b_ada"] = summed[0]
    grad["w_gla_a2"] = lax.dynamic_slice_in_dim(grad["w_gla_a2"], k_me * a2_n, a2_n, axis=2)
    grad["w_conv"] = lax.dynamic_slice_in_dim(grad["w_conv"], k_me * conv_n, conv_n, axis=2)
    dmod_all = gs[:, :nl * 6 * D // HD].reshape(8, nl, 6 * D)
    dmod_sh = lax.dynamic_slice_in_dim(dmod_all, k_me * ada_n, ada_n, axis=2).transpose(1, 0, 2)
    grad["w_ada"] = ada_dw(c_all, dmod_sh)

    delta, new_m, new_v = {}, {}, {}
    delta["w_ada"], new_m["w_ada"], new_v["w_ada"] = adamw(w["w_ada"], grad["w_ada"], m["w_ada"], v["w_ada"],
                                                          block=ADAM_BLOCKS["w_ada"], name="adamw_w_ada")
    rest = [n for n in WEIGHTS if n not in ADAM_BLOCKS]
    shapes = [w[n].shape for n in rest]
    flat = [_flat_rows([t[n] for n in rest]) for t in (w, grad, m, v)]
    outs = adamw(*flat, block=flat[0].shape, name="adamw_small")
    for t, o in zip((delta, new_m, new_v), outs):
        t.update(zip(rest, _unflat(o, shapes)))

    later = {n: adamw(w[n], totals[n], m[n], v[n], block=ADAM_BLOCKS[n], name="adamw_later_" + n, rows=(RS_GROUP, nl))
             for n in BIG} if nl > RS_GROUP else {}
    done_first = [outs[0], delta["w_ada"]] + [later[n][0] for n in later]
    grad.update(finish_group(pending, done_first, totals, idx))
    for n in BIG:
        delta[n], new_m[n], new_v[n] = adamw(w[n], grad[n], m[n], v[n], block=ADAM_BLOCKS[n], name="adamw_" + n,
                                             rows=(0, min(RS_GROUP, nl)), into=later.get(n))

    return (loss, dx[None], *[grad[n] for n in WEIGHTS], *[delta[n] for n in WEIGHTS], *[new_m[n] for n in WEIGHTS],
            *[new_v[n] for n in WEIGHTS])
```

```python
import functools

import numpy as np
import jax
import jax.numpy as jnp
from jax import lax
from jax.experimental import pallas as pl
from jax.experimental.pallas import tpu as pltpu

F32 = jnp.float32
MMT = jnp.bfloat16
HI = lax.Precision.HIGHEST

D = 1024
DEPTH = 4
NH = 4
HD = 128
BW = NH * HD
CH = 64
GDK = 64
GLR = 16
DFF = 2816
EPS = 1e-6
ROPE_BASE = 10000.0

GP, RG, GG, FV, RQ, RK, RV, GQ, GK, GV, FQ, FK, LR, FF = (
    0, 3072, 3584, 4096, 4608, 5120, 5632, 6144, 6400, 6656, 7168, 7680, 8192, 8320)
NZZ = 8448
WZ0 = 3072
IN_W = 5140
W_IN_COLS = dict(rqkv=(0, 1536), rg=(1536, 2048), gqkv=(2048, 3072), lr=(3072, 3088), gg=(3088, 3600), fqk=(3600, 4624),
                 fv=(4624, 5136), ff=(5136, 5140))

VMEM_LIMIT = 56 * 1024 * 1024

ADAM_LR, ADAM_B1, ADAM_B2, ADAM_EPS, ADAM_WD, ADAM_STEP = 0.001, 0.9, 0.999, 1e-08, 0.01, 10


def _cparams(sem=None):
    return pltpu.CompilerParams(dimension_semantics=sem, vmem_limit_bytes=VMEM_LIMIT)


def _sds(shape, dtype):
    return jax.ShapeDtypeStruct(tuple(shape), dtype)


def _dot(a, b, precision=None):
    return lax.dot_general(a, b, (((1,), (0,)), ((), ())), precision=precision, preferred_element_type=F32)


def _dot_nt(a, b, precision=None):
    return lax.dot_general(a, b, (((1,), (1,)), ((), ())), precision=precision, preferred_element_type=F32)


def _dot_tn(a, b, precision=None):
    return lax.dot_general(a, b, (((0,), (0,)), ((), ())), precision=precision, preferred_element_type=F32)


def _silu(x):
    return x * jax.nn.sigmoid(x)


def _log_sigmoid(x):
    return jnp.minimum(x, 0.0) - jnp.log(1.0 + jnp.exp(jnp.minimum(x, -x)))


@jax.custom_vjp
def _swap_halves(x):
    return pltpu.roll(x, HD // 2, 1)


_swap_halves.defvjp(lambda x: (_swap_halves(x), None), lambda _, g: (_swap_halves(g),))


@jax.custom_vjp
def _bdot(a, b):
    return _dot(a.astype(MMT), b.astype(MMT))


@jax.custom_vjp
def _bdot_nt(a, b):
    return _dot_nt(a.astype(MMT), b.astype(MMT))


@jax.custom_vjp
def _bdot_tn(a, b):
    return _dot_tn(a.astype(MMT), b.astype(MMT))


_bdot.defvjp(lambda a, b: (_bdot(a, b), (a, b)), lambda r, g: (_bdot_nt(g, r[1]), _bdot_tn(r[0], g)))
_bdot_nt.defvjp(lambda a, b: (_bdot_nt(a, b), (a, b)), lambda r, g: (_bdot(g, r[1]), _bdot_tn(g, r[0])))
_bdot_tn.defvjp(lambda a, b: (_bdot_tn(a, b), (a, b)), lambda r, g: (_bdot_nt(r[1], g), _bdot(r[0], g)))


def _stacked(blk, idx, layer):
    if layer is None:
        return pl.BlockSpec(blk, idx)
    return pl.BlockSpec((None,) + blk, lambda i, j: (layer,) + idx(i, j))


def mm_nn(a, b, *, tm, tn, out_dtype, name, layer=None):
    m, k = a.shape
    n = b.shape[-1]

    def body(a_ref, b_ref, o_ref):
        o_ref[...] = _dot(a_ref[...], b_ref[...]).astype(o_ref.dtype)

    return pl.pallas_call(
        body, grid=(m // tm, n // tn),
        in_specs=[pl.BlockSpec((tm, k), lambda i, j: (i, 0)), _stacked((k, tn), lambda i, j: (0, j), layer)],
        out_specs=pl.BlockSpec((tm, tn), lambda i, j: (i, j)),
        out_shape=_sds((m, n), out_dtype), compiler_params=_cparams(("parallel", "parallel")), name=name)(a, b)


def mm_nn_residual(a, b, res, gate, *, tm, tn, name, layer=None):
    m, k = a.shape
    n = b.shape[-1]

    def body(a_ref, b_ref, r_ref, g_ref, x_ref, y_ref):
        acc = _dot(a_ref[...], b_ref[...])
        y_ref[...] = acc
        x_ref[...] = r_ref[...] + g_ref[...] * acc

    return pl.pallas_call(
        body, grid=(m // tm, n // tn),
        in_specs=[pl.BlockSpec((tm, k), lambda i, j: (i, 0)), _stacked((k, tn), lambda i, j: (0, j), layer),
                  pl.BlockSpec((tm, tn), lambda i, j: (i, j)), pl.BlockSpec((1, tn), lambda i, j: (0, j))],
        out_specs=[pl.BlockSpec((tm, tn), lambda i, j: (i, j)), pl.BlockSpec((tm, tn), lambda i, j: (i, j))],
        out_shape=[_sds((m, n), F32), _sds((m, n), F32)],
        compiler_params=_cparams(("parallel", "parallel")), name=name)(a, b, res, gate)


def mm_nt(a, b, *, tm, tn, out_dtype, name, layer=None):
    m, k = a.shape
    n = b.shape[-2]

    def body(a_ref, b_ref, o_ref):
        o_ref[...] = _dot_nt(a_ref[...], b_ref[...]).astype(o_ref.dtype)

    return pl.pallas_call(
        body, grid=(m // tm, n // tn),
        in_specs=[pl.BlockSpec((tm, k), lambda i, j: (i, 0)), _stacked((tn, k), lambda i, j: (j, 0), layer)],
        out_specs=pl.BlockSpec((tm, tn), lambda i, j: (i, j)),
        out_shape=_sds((m, n), out_dtype), compiler_params=_cparams(("parallel", "parallel")), name=name)(a, b)


def mm_nt2(a1, a2, b, *, tm, tn, name, layer):
    m, k1 = a1.shape
    k2 = a2.shape[1]
    n = b.shape[-2]

    def body(a1_ref, a2_ref, b_ref, o_ref):
        o_ref[...] = _dot_nt(a1_ref[...], b_ref[:, :k1]) + _dot_nt(a2_ref[...], b_ref[:, k1:])

    return pl.pallas_call(
        body, grid=(m // tm, n // tn),
        in_specs=[pl.BlockSpec((tm, k1), lambda i, j: (i, 0)), pl.BlockSpec((tm, k2), lambda i, j: (i, 0)),
                  _stacked((tn, k1 + k2), lambda i, j: (j, 0), layer)],
        out_specs=pl.BlockSpec((tm, tn), lambda i, j: (i, j)),
        out_shape=_sds((m, n), F32), compiler_params=_cparams(("parallel", "parallel")), name=name)(a1, a2, b)


def mm_tn(a, b, *, tm, tn, out_dtype, name, col0=0, ncols=None, stack=None, layer=None, out_col0=0):
    s, m = a.shape
    n = b.shape[1] - col0 if ncols is None else ncols
    c0, oc0 = col0 // tn, out_col0 // tn

    def body(a_ref, b_ref, *rest):
        o_ref = rest[-1]
        o_ref[...] = _dot_tn(a_ref[...], b_ref[...]).astype(o_ref.dtype)

    in_specs = [pl.BlockSpec((s, tm), lambda i, j: (0, i)), pl.BlockSpec((s, tn), lambda i, j: (0, c0 + j))]
    if stack is None:
        return pl.pallas_call(
            body, grid=(m // tm, n // tn), in_specs=in_specs, out_specs=pl.BlockSpec((tm, tn), lambda i, j: (i, j)),
            out_shape=_sds((m, n), out_dtype), compiler_params=_cparams(("parallel", "parallel")), name=name)(a, b)
    return pl.pallas_call(
        body, grid=(m // tm, n // tn), in_specs=in_specs + [pl.BlockSpec(memory_space=pl.ANY)],
        out_specs=pl.BlockSpec((None, tm, tn), lambda i, j: (layer, i, oc0 + j)),
        out_shape=_sds(stack.shape, stack.dtype), input_output_aliases={2: 0},
        compiler_params=_cparams(("parallel", "parallel")), name=name)(a, b, stack)


def _row_tile(s):
    return min(256, s)


def _norm_mod_f(x, g, scale, shift):
    r = lax.rsqrt(jnp.mean(x * x, axis=-1, keepdims=True) + EPS)
    return (x * r * g) * (1.0 + scale) + shift


def norm_mod(x, g, scale, shift, *, name):
    s = x.shape[0]
    t = _row_tile(s)

    def body(x_ref, g_ref, sc_ref, sh_ref, o_ref):
        o_ref[...] = _norm_mod_f(x_ref[...], g_ref[...], sc_ref[...], sh_ref[...]).astype(o_ref.dtype)

    vec = pl.BlockSpec((1, D), lambda i: (0, 0))
    return pl.pallas_call(
        body, grid=(s // t,), in_specs=[pl.BlockSpec((t, D), lambda i: (i, 0)), vec, vec, vec],
        out_specs=pl.BlockSpec((t, D), lambda i: (i, 0)), out_shape=_sds((s, D), MMT),
        compiler_params=_cparams(("parallel",)), name=name)(x, g, scale, shift)


def norm_mod_bwd(x, dh, dres, g, scale, shift, *, name):
    s = x.shape[0]
    t = _row_tile(s)

    def body(x_ref, dh_ref, dr_ref, g_ref, sc_ref, sh_ref, dx_ref, dg_ref, dsc_ref, dsh_ref):
        @pl.when(pl.program_id(0) == 0)
        def _():
            dg_ref[...] = jnp.zeros_like(dg_ref)
            dsc_ref[...] = jnp.zeros_like(dsc_ref)
            dsh_ref[...] = jnp.zeros_like(dsh_ref)

        _, vjp = jax.vjp(_norm_mod_f, x_ref[...], g_ref[...], sc_ref[...], sh_ref[...])
        dx, dg, dsc, dsh = vjp(dh_ref[...])
        dx_ref[...] = dr_ref[...] + dx
        dg_ref[...] += dg
        dsc_ref[...] += dsc
        dsh_ref[...] += dsh

    row = pl.BlockSpec((t, D), lambda i: (i, 0))
    vec = pl.BlockSpec((1, D), lambda i: (0, 0))
    return pl.pallas_call(
        body, grid=(s // t,), in_specs=[row, row, row, vec, vec, vec], out_specs=[row, vec, vec, vec],
        out_shape=[_sds((s, D), F32)] + [_sds((1, D), F32)] * 3,
        compiler_params=_cparams(("arbitrary",)), name=name)(x, dh, dres, g, scale, shift)


def gate_bwd(dx, y, gate, *, name):
    s = dx.shape[0]
    t = _row_tile(s)

    def body(dx_ref, y_ref, g_ref, dy_ref, dg_ref):
        @pl.when(pl.program_id(0) == 0)
        def _():
            dg_ref[...] = jnp.zeros_like(dg_ref)

        dxv = dx_ref[...]
        dy_ref[...] = (g_ref[...] * dxv).astype(dy_ref.dtype)
        dg_ref[...] += jnp.sum(dxv * y_ref[...], axis=0, keepdims=True)

    row = pl.BlockSpec((t, D), lambda i: (i, 0))
    vec = pl.BlockSpec((1, D), lambda i: (0, 0))
    return pl.pallas_call(
        body, grid=(s // t,), in_specs=[row, row, vec], out_specs=[row, vec],
        out_shape=[_sds((s, D), MMT), _sds((1, D), F32)],
        compiler_params=_cparams(("arbitrary",)), name=name)(dx, y, gate)


def loss_and_grad(xf, target, *, name):
    s = xf.shape[0]
    t = _row_tile(s)

    def body(x_ref, t_ref, l_ref, dx_ref):
        @pl.when(pl.program_id(0) == 0)
        def _():
            l_ref[...] = jnp.zeros_like(l_ref)

        e = x_ref[...] - t_ref[...]
        dx_ref[...] = e * (1.0 / D)
        l_ref[...] += 0.5 * jnp.sum(jnp.sum(e * e, axis=1, keepdims=True), axis=0, keepdims=True) * (1.0 / D)

    row = pl.BlockSpec((t, D), lambda i: (i, 0))
    return pl.pallas_call(
        body, grid=(s // t,), in_specs=[row, row], out_specs=[pl.BlockSpec((1, 1), lambda i: (0, 0)), row],
        out_shape=[_sds((1, 1), F32), _sds((s, D), F32)],
        compiler_params=_cparams(("arbitrary",)), name=name)(xf, target)


def _ret_consts():
    log_g = np.log1p(-np.exp2(-5.0 - np.arange(NH, dtype=np.float32))).astype(np.float32)
    idx = np.arange(CH, dtype=np.float32)
    d_intra = np.exp(np.abs(idx[:, None] - idx[None, :])[None] * log_g[:, None, None]).astype(np.float32)
    k_w = np.exp((CH - 1.0 - idx)[None, :] * log_g[:, None]).astype(np.float32)
    q_w = np.exp((idx + 1.0)[None, :] * log_g[:, None]).astype(np.float32)
    g_chunk = [float(v) for v in np.exp(np.float32(CH) * log_g).astype(np.float32)]
    bc = lambda a: np.ascontiguousarray(np.broadcast_to(a[:, :, None], (NH, CH, HD)))
    return jnp.asarray(d_intra), jnp.asarray(bc(k_w)), jnp.asarray(bc(q_w)), g_chunk


def _rope_tables(s):
    half = HD // 2
    inv_freq = (ROPE_BASE ** (-np.arange(half, dtype=np.float64) / half)).astype(np.float32)
    ang = (np.arange(s, dtype=np.float32)[:, None] * inv_freq[None, :]).astype(np.float64)
    cos, sin = np.cos(ang).astype(np.float32), np.sin(ang).astype(np.float32)
    return jnp.asarray(np.concatenate([cos, cos], axis=1)), jnp.asarray(np.concatenate([-sin, sin], axis=1))


def _ret_chunk(qs, ks, vs, rs, cos, sin, dintra, kw, qw, g_chunk):
    outs, rn = [], []
    for h in range(NH):
        q = qs[h] * cos + _swap_halves(qs[h]) * sin
        k = (ks[h] * cos + _swap_halves(ks[h]) * sin) * (HD ** -0.5)
        sc = _bdot_nt(q, k) * dintra[h]
        outs.append(_bdot(sc, vs[h]) + _bdot(q * qw[h], rs[h]))
        rn.append(g_chunk[h] * rs[h] + _bdot_tn(k * kw[h], vs[h]))
    return outs, rn


def _heads(x):
    return [x[:, h * HD:(h + 1) * HD] for h in range(NH)]


def _chunks_per_step(n):
    return 4 if n % 4 == 0 else 1


def retention_fwd(zz, cosf, sinf, *, name):
    s = zz.shape[0]
    n = s // CH
    nb = _chunks_per_step(n)
    rb = nb * CH
    dintra, kw, qw, g_chunk = _ret_consts()

    def body(q_ref, k_ref, v_ref, c_ref, s_ref, di_ref, kw_ref, qw_ref, o_ref, rp_ref, r_scr):
        @pl.when(pl.program_id(0) == 0)
        def _():
            r_scr[...] = jnp.zeros_like(r_scr)

        r = r_scr[...]
        consts = ([di_ref[h] for h in range(NH)], [kw_ref[h] for h in range(NH)], [qw_ref[h] for h in range(NH)], g_chunk)
        for i in range(nb):
            rows = slice(i * CH, (i + 1) * CH)
            rp_ref[i] = r
            outs, rn = _ret_chunk(_heads(q_ref[rows, :]), _heads(k_ref[rows, :]), _heads(v_ref[rows, :]),
                                  [r[h * HD:(h + 1) * HD] for h in range(NH)], c_ref[rows, :], s_ref[rows, :], *consts)
            o_ref[rows, :] = jnp.concatenate(outs, axis=1)
            r = jnp.concatenate(rn, axis=0)
        r_scr[...] = r

    col = lambda c: pl.BlockSpec((rb, BW), lambda i: (i, c // BW))
    tab = pl.BlockSpec((rb, HD), lambda i: (i, 0))
    cst = lambda shp: pl.BlockSpec(shp, lambda i: (0,) * len(shp))
    return pl.pallas_call(
        body, grid=(n // nb,),
        in_specs=[col(RQ), col(RK), col(RV), tab, tab, cst((NH, CH, CH)), cst((NH, CH, HD)), cst((NH, CH, HD))],
        out_specs=[pl.BlockSpec((rb, BW), lambda i: (i, 0)), pl.BlockSpec((nb, BW, HD), lambda i: (i, 0, 0))],
        out_shape=[_sds((s, BW), F32), _sds((n, BW, HD), F32)],
        scratch_shapes=[pltpu.VMEM((BW, HD), F32)],
        compiler_params=_cparams(("arbitrary",)), name=name)(zz, zz, zz, cosf, sinf, dintra, kw, qw)


def retention_bwd(zz, cosf, sinf, rprev, do, dzz, *, name):
    s = zz.shape[0]
    n = s // CH
    nb = _chunks_per_step(n)
    rb, steps = nb * CH, n // nb
    dintra, kw, qw, g_chunk = _ret_consts()

    def body(q_ref, k_ref, v_ref, c_ref, s_ref, di_ref, kw_ref, qw_ref, rp_ref, do_ref, dzz_ref, dz_ref, dr_scr):
        @pl.when(pl.program_id(0) == 0)
        def _():
            dr_scr[...] = jnp.zeros_like(dr_scr)

        dr = dr_scr[...]
        consts = dict(dintra=[di_ref[h] for h in range(NH)], kw=[kw_ref[h] for h in range(NH)],
                      qw=[qw_ref[h] for h in range(NH)], g_chunk=g_chunk)
        for i in reversed(range(nb)):
            rows = slice(i * CH, (i + 1) * CH)
            rprev_v = rp_ref[i]
            f = functools.partial(_ret_chunk, cos=c_ref[rows, :], sin=s_ref[rows, :], **consts)
            _, vjp = jax.vjp(f, _heads(q_ref[rows, :]), _heads(k_ref[rows, :]), _heads(v_ref[rows, :]),
                             [rprev_v[h * HD:(h + 1) * HD] for h in range(NH)])
            dq, dk, dv, drp = vjp((_heads(do_ref[rows, :]), [dr[h * HD:(h + 1) * HD] for h in range(NH)]))
            dz_ref[rows, :] = jnp.concatenate(dq + dk + dv, axis=1).astype(dz_ref.dtype)
            dr = jnp.concatenate(drp, axis=0)
        dr_scr[...] = dr

    col = lambda c: pl.BlockSpec((rb, BW), lambda i: (steps - 1 - i, c // BW))
    tab = pl.BlockSpec((rb, HD), lambda i: (steps - 1 - i, 0))
    cst = lambda shp: pl.BlockSpec(shp, lambda i: (0,) * len(shp))
    return pl.pallas_call(
        body, grid=(steps,),
        in_specs=[col(RQ), col(RK), col(RV), tab, tab, cst((NH, CH, CH)), cst((NH, CH, HD)), cst((NH, CH, HD)),
                  pl.BlockSpec((nb, BW, HD), lambda i: (steps - 1 - i, 0, 0)),
                  pl.BlockSpec((rb, BW), lambda i: (steps - 1 - i, 0)), pl.BlockSpec(memory_space=pl.ANY)],
        out_specs=pl.BlockSpec((rb, 3 * BW), lambda i: (steps - 1 - i, RQ // (3 * BW))),
        out_shape=_sds(dzz.shape, dzz.dtype), input_output_aliases={10: 0},
        scratch_shapes=[pltpu.VMEM((BW, HD), F32)],
        compiler_params=_cparams(("arbitrary",)), name=name)(zz, zz, zz, cosf, sinf, dintra, kw, qw, rprev, do, dzz)


GKW = NH * GDK


def _gla_consts():
    tri = np.tril(np.ones((CH, CH), np.float32))
    mask_t = np.zeros((BW, GKW), np.float32)
    for h in range(NH):
        mask_t[h * HD:(h + 1) * HD, h * GDK:(h + 1) * GDK] = 1.0
    return jnp.asarray(tri), jnp.asarray(mask_t)


def _gla_chunk(q, k, v, glr, w_a2, b_a, st, tri, mask_t):
    la = _log_sigmoid(_bdot(glr, w_a2) + b_a) * (1.0 / 16.0)
    bc = _dot(tri, la, HI)
    be = jnp.sum(la, axis=0, keepdims=True)
    kv_t = _bdot_tn(v, k * jnp.exp(be - bc)) * mask_t
    sn = jnp.exp(be) * st + kv_t
    return _bdot_nt(q * (GDK ** -0.5), sn), sn


def gla_fwd(zz, w_a2p, b_a, *, name):
    s = zz.shape[0]
    n = s // CH
    nb = _chunks_per_step(n)
    rb = nb * CH
    tri, mask_t = _gla_consts()

    def body(q_ref, k_ref, v_ref, lr_ref, w_ref, b_ref, tri_ref, m_ref, o_ref, sp_ref, st_scr):
        @pl.when(pl.program_id(0) == 0)
        def _():
            st_scr[...] = jnp.zeros_like(st_scr)

        st = st_scr[...]
        for i in range(nb):
            rows = slice(i * CH, (i + 1) * CH)
            sp_ref[i] = st
            o_ref[rows, :], st = _gla_chunk(q_ref[rows, :], k_ref[rows, :], v_ref[rows, :], lr_ref[rows, :], w_ref[...],
                                            b_ref[...], st, tri_ref[...], m_ref[...])
        st_scr[...] = st

    cst = lambda shp: pl.BlockSpec(shp, lambda i: (0,) * len(shp))
    return pl.pallas_call(
        body, grid=(n // nb,),
        in_specs=[pl.BlockSpec((rb, GKW), lambda i: (i, GQ // GKW)), pl.BlockSpec((rb, GKW), lambda i: (i, GK // GKW)),
                  pl.BlockSpec((rb, BW), lambda i: (i, GV // BW)), pl.BlockSpec((rb, HD), lambda i: (i, LR // HD)),
                  cst((HD, GKW)), cst((1, GKW)), cst((CH, CH)), cst((BW, GKW))],
        out_specs=[pl.BlockSpec((rb, BW), lambda i: (i, 0)), pl.BlockSpec((nb, BW, GKW), lambda i: (i, 0, 0))],
        out_shape=[_sds((s, BW), F32), _sds((n, BW, GKW), F32)],
        scratch_shapes=[pltpu.VMEM((BW, GKW), F32)],
        compiler_params=_cparams(("arbitrary",)), name=name)(zz, zz, zz, zz, w_a2p, b_a, tri, mask_t)


def gla_bwd(zz, w_a2p, b_a, sprev, do, dzz, *, name):
    s = zz.shape[0]
    n = s // CH
    nb = _chunks_per_step(n)
    rb, steps = nb * CH, n // nb
    tri, mask_t = _gla_consts()

    def body(q_ref, k_ref, v_ref, lr_ref, w_ref, b_ref, tri_ref, m_ref, sp_ref, do_ref, dzz_ref,
             dz_ref, dlr_ref, dw_ref, db_ref, ds_scr):
        @pl.when(pl.program_id(0) == 0)
        def _():
            ds_scr[...] = jnp.zeros_like(ds_scr)
            dw_ref[...] = jnp.zeros_like(dw_ref)
            db_ref[...] = jnp.zeros_like(db_ref)

        f = functools.partial(_gla_chunk, tri=tri_ref[...], mask_t=m_ref[...])
        ds, dw_sum, db_sum = ds_scr[...], jnp.zeros(dw_ref.shape, F32), jnp.zeros(db_ref.shape, F32)
        for i in reversed(range(nb)):
            rows = slice(i * CH, (i + 1) * CH)
            _, vjp = jax.vjp(f, q_ref[rows, :], k_ref[rows, :], v_ref[rows, :], lr_ref[rows, :], w_ref[...], b_ref[...],
                             sp_ref[i])
            dq, dk, dv, dlr, dw, db, ds = vjp((do_ref[rows, :], ds))
            dz_ref[rows, :] = jnp.concatenate([dq, dk, dv], axis=1).astype(dz_ref.dtype)
            dlr_ref[rows, :] = dlr.astype(dlr_ref.dtype)
            dw_sum, db_sum = dw_sum + dw, db_sum + db
        dw_ref[...] += dw_sum
        db_ref[...] += db_sum
        ds_scr[...] = ds

    cst = lambda shp: pl.BlockSpec(shp, lambda i: (0,) * len(shp))
    r = lambda i: steps - 1 - i
    return pl.pallas_call(
        body, grid=(steps,),
        in_specs=[pl.BlockSpec((rb, GKW), lambda i: (r(i), GQ // GKW)), pl.BlockSpec((rb, GKW), lambda i: (r(i), GK // GKW)),
                  pl.BlockSpec((rb, BW), lambda i: (r(i), GV // BW)), pl.BlockSpec((rb, HD), lambda i: (r(i), LR // HD)),
                  cst((HD, GKW)), cst((1, GKW)), cst((CH, CH)), cst((BW, GKW)),
                  pl.BlockSpec((nb, BW, GKW), lambda i: (r(i), 0, 0)), pl.BlockSpec((rb, BW), lambda i: (r(i), 0)),
                  pl.BlockSpec(memory_space=pl.ANY)],
        out_specs=[pl.BlockSpec((rb, 2 * GKW + BW), lambda i: (r(i), GQ // (2 * GKW + BW))),
                   pl.BlockSpec((rb, HD), lambda i: (r(i), 0)), cst((HD, GKW)), cst((1, GKW))],
        out_shape=[_sds(dzz.shape, dzz.dtype), _sds((s, HD), MMT), _sds((HD, GKW), F32), _sds((1, GKW), F32)],
        input_output_aliases={10: 0}, scratch_shapes=[pltpu.VMEM((BW, GKW), F32)],
        compiler_params=_cparams(("arbitrary",)), name=name)(zz, zz, zz, zz, w_a2p, b_a, tri, mask_t, sprev, do, dzz)


def _fox_pre_f(fqs, fks, ff, gq, gk, bf):
    def rms(x, g):
        return x * lax.rsqrt(jnp.mean(x * x, axis=-1, keepdims=True) + EPS) * g

    qn = [rms(x, gq) * (HD ** -0.5) for x in fqs]
    kn = [rms(x, gk) for x in fks]
    return qn, kn, _log_sigmoid(ff + bf)


def fox_pre(zz, gq, gk, bf, *, name):
    s = zz.shape[0]
    t = _row_tile(s)
    tri = jnp.asarray(np.tril(np.ones((t, t), np.float32)))

    def body(q_ref, k_ref, f_ref, gq_ref, gk_ref, b_ref, tri_ref, qn_ref, kn_ref, cum_ref, carry):
        @pl.when(pl.program_id(0) == 0)
        def _():
            carry[...] = jnp.zeros_like(carry)

        qn, kn, lf = _fox_pre_f(_heads(q_ref[...]), _heads(k_ref[...]), f_ref[...], gq_ref[...], gk_ref[...], b_ref[...])
        qn_ref[...] = jnp.concatenate(qn, axis=1).astype(qn_ref.dtype)
        kn_ref[...] = jnp.concatenate(kn, axis=1).astype(kn_ref.dtype)
        cum_ref[...] = _dot(tri_ref[...], lf, HI) + carry[...]
        carry[...] += jnp.sum(lf, axis=0, keepdims=True)

    vec = pl.BlockSpec((1, HD), lambda i: (0, 0))
    return pl.pallas_call(
        body, grid=(s // t,),
        in_specs=[pl.BlockSpec((t, BW), lambda i: (i, FQ // BW)), pl.BlockSpec((t, BW), lambda i: (i, FK // BW)),
                  pl.BlockSpec((t, HD), lambda i: (i, FF // HD)), vec, vec, vec, pl.BlockSpec((t, t), lambda i: (0, 0))],
        out_specs=[pl.BlockSpec((t, BW), lambda i: (i, 0)), pl.BlockSpec((t, BW), lambda i: (i, 0)),
                   pl.BlockSpec((t, HD), lambda i: (i, 0))],
        out_shape=[_sds((s, BW), MMT), _sds((s, BW), MMT), _sds((s, HD), F32)],
        scratch_shapes=[pltpu.VMEM((1, HD), F32)],
        compiler_params=_cparams(("arbitrary",)), name=name)(zz, zz, zz, gq, gk, bf, tri)


def fox_pre_bwd(zz, gq, gk, bf, dqn, dkn, dcum, dzz, *, name):
    s = zz.shape[0]
    t = _row_tile(s)
    nt = s // t
    triu = jnp.asarray(np.triu(np.ones((t, t), np.float32)))

    def body(q_ref, k_ref, f_ref, gq_ref, gk_ref, b_ref, tri_ref, dqn_ref, dkn_ref, dcum_ref, dzz_ref,
             dz_ref, dff_ref, dgq_ref, dgk_ref, db_ref, carry):
        @pl.when(pl.program_id(0) == 0)
        def _():
            carry[...] = jnp.zeros_like(carry)
            dgq_ref[...] = jnp.zeros_like(dgq_ref)
            dgk_ref[...] = jnp.zeros_like(dgk_ref)
            db_ref[...] = jnp.zeros_like(db_ref)

        dcum_v = dcum_ref[...]
        dlf = _dot(tri_ref[...], dcum_v, HI) + carry[...]
        carry[...] += jnp.sum(dcum_v, axis=0, keepdims=True)
        _, vjp = jax.vjp(_fox_pre_f, _heads(q_ref[...]), _heads(k_ref[...]), f_ref[...], gq_ref[...], gk_ref[...], b_ref[...])
        dq, dk, dff, dgq, dgk, db = vjp((_heads(dqn_ref[...]), _heads(dkn_ref[...]), dlf))
        dz_ref[...] = jnp.concatenate(dq + dk, axis=1).astype(dz_ref.dtype)
        dff_ref[...] = dff.astype(dff_ref.dtype)
        dgq_ref[...] += dgq
        dgk_ref[...] += dgk
        db_ref[...] += db

    r = lambda i: nt - 1 - i
    vec = pl.BlockSpec((1, HD), lambda i: (0, 0))
    return pl.pallas_call(
        body, grid=(nt,),
        in_specs=[pl.BlockSpec((t, BW), lambda i: (r(i), FQ // BW)), pl.BlockSpec((t, BW), lambda i: (r(i), FK // BW)),
                  pl.BlockSpec((t, HD), lambda i: (r(i), FF // HD)), vec, vec, vec, pl.BlockSpec((t, t), lambda i: (0, 0)),
                  pl.BlockSpec((t, BW), lambda i: (r(i), 0)), pl.BlockSpec((t, BW), lambda i: (r(i), 0)),
                  pl.BlockSpec((t, HD), lambda i: (r(i), 0)), pl.BlockSpec(memory_space=pl.ANY)],
        out_specs=[pl.BlockSpec((t, 2 * BW), lambda i: (r(i), FQ // (2 * BW))), pl.BlockSpec((t, HD), lambda i: (r(i), 0)),
                   vec, vec, vec],
        out_shape=[_sds(dzz.shape, dzz.dtype), _sds((s, HD), MMT), _sds((1, HD), F32), _sds((1, HD), F32), _sds((1, HD), F32)],
        input_output_aliases={10: 0}, scratch_shapes=[pltpu.VMEM((1, HD), F32)],
        compiler_params=_cparams(("arbitrary",)), name=name)(zz, zz, zz, gq, gk, bf, triu, dqn, dkn, dcum, dzz)


def _fox_blocks(s):
    return min(256, s), min(512, s)


NEG = -1e30


def fox_attn_fwd(qn, kn, zz, cum_col, cum_row, *, name):
    s = qn.shape[0]
    bq, bk = _fox_blocks(s)

    def body(q_ref, k_ref, v_ref, cc_ref, cr_ref, o_ref, lse_ref):
        qi = pl.program_id(1)
        q = q_ref[...]
        cq = cc_ref[...]
        rows = qi * bq + lax.broadcasted_iota(jnp.int32, (bq, bk), 0)
        cols0 = lax.broadcasted_iota(jnp.int32, (bq, bk), 1)

        def step(j, carry, on_diagonal):
            m, l, acc = carry
            off = pl.multiple_of(j * bk, bk)
            k = k_ref[pl.ds(off, bk), :]
            v = v_ref[pl.ds(off, bk), :].astype(MMT)
            sc = _dot_nt(q, k) + cq - cr_ref[pl.ds(j, 1), :]
            if on_diagonal:
                sc = jnp.where(rows >= cols0 + j * bk, sc, NEG)
            m_new = jnp.maximum(m, jnp.max(sc, axis=1, keepdims=True))
            alpha = jnp.exp(m - m_new)
            p = jnp.exp(sc - m_new)
            return m_new, alpha * l + jnp.sum(p, axis=1, keepdims=True), alpha * acc + _dot(p.astype(MMT), v)

        nfull, nk = (qi * bq + 1) // bk, ((qi + 1) * bq + bk - 1) // bk
        carry = (jnp.full((bq, 1), NEG, F32), jnp.zeros((bq, 1), F32), jnp.zeros((bq, HD), F32))
        carry = lax.fori_loop(0, nfull, functools.partial(step, on_diagonal=False), carry)
        m, l, acc = lax.fori_loop(nfull, nk, functools.partial(step, on_diagonal=True), carry)
        o_ref[...] = acc / l
        lse_ref[...] = m + jnp.log(l)

    return pl.pallas_call(
        body, grid=(NH, s // bq),
        in_specs=[pl.BlockSpec((bq, HD), lambda h, i: (i, h)), pl.BlockSpec((s, HD), lambda h, i: (0, h)),
                  pl.BlockSpec((s, HD), lambda h, i: (0, FV // HD + h)),
                  pl.BlockSpec((None, bq, 1), lambda h, i: (h, i, 0)), pl.BlockSpec((None, s // bk, bk), lambda h, i: (h, 0, 0))],
        out_specs=[pl.BlockSpec((bq, HD), lambda h, i: (i, h)), pl.BlockSpec((None, bq, 1), lambda h, i: (h, i, 0))],
        out_shape=[_sds((s, BW), F32), _sds((NH, s, 1), F32)],
        compiler_params=_cparams(("parallel", "parallel")), name=name)(qn, kn, zz, cum_col, cum_row)


def fox_attn_bwd(qn, kn, zz, cum_col, cum_row, lse, do, dzz, *, name):
    s = qn.shape[0]
    bq, bk = _fox_blocks(s)
    nkc = s // bk

    def body(q_ref, k_ref, v_ref, cc_ref, cr_ref, lse_ref, do_ref, dzz_ref, dq_ref, dk_ref, dv_ref, dc_ref,
             p_scr, dp_scr, dv_scr):
        qi = pl.program_id(1)

        @pl.when(qi == 0)
        def _():
            dk_ref[...] = jnp.zeros_like(dk_ref)
            dv_scr[...] = jnp.zeros_like(dv_scr)
            dc_ref[...] = jnp.zeros_like(dc_ref)

        q = q_ref[...]
        dob = do_ref[...].astype(MMT)
        cq = cc_ref[...]
        lse_v = lse_ref[...]
        rows = qi * bq + lax.broadcasted_iota(jnp.int32, (bq, bk), 0)
        cols0 = lax.broadcasted_iota(jnp.int32, (bq, bk), 1)
        nfull, nk = (qi * bq + 1) // bk, ((qi + 1) * bq + bk - 1) // bk

        def probs(j, delta, on_diagonal):
            off = pl.multiple_of(j * bk, bk)
            sc = _dot_nt(q, k_ref[pl.ds(off, bk), :]) + cq - cr_ref[pl.ds(j, 1), :]
            p = jnp.exp(sc - lse_v)
            if on_diagonal:
                p = jnp.where(rows >= cols0 + j * bk, p, 0.0)
            dp = _dot_nt(dob, v_ref[pl.ds(off, bk), :].astype(MMT))
            p_scr[j] = p
            dp_scr[j] = dp
            return delta + jnp.sum(p * dp, axis=1, keepdims=True)

        delta = lax.fori_loop(0, nfull, functools.partial(probs, on_diagonal=False), jnp.zeros((bq, 1), F32))
        delta = lax.fori_loop(nfull, nk, functools.partial(probs, on_diagonal=True), delta)

        def grads(j, dq):
            off = pl.multiple_of(j * bk, bk)
            p = p_scr[j]
            ds = p * (dp_scr[j] - delta)
            dsm = ds.astype(MMT)
            dv_scr[pl.ds(off, bk), :] += _dot_tn(p.astype(MMT), dob)
            dk_ref[pl.ds(off, bk), :] += _dot_tn(dsm, q)
            dc_ref[pl.ds(j, 1), :] -= jnp.sum(ds, axis=0, keepdims=True)
            return dq + _dot(dsm, k_ref[pl.ds(off, bk), :])

        dq_ref[...] = lax.fori_loop(0, nk, grads, jnp.zeros((bq, HD), F32))

        @pl.when(qi == pl.num_programs(1) - 1)
        def _():
            dv_ref[...] = dv_scr[...].astype(dv_ref.dtype)

    full = lambda c0=0: pl.BlockSpec((s, HD), lambda h, i: (0, c0 + h))
    blk = lambda: pl.BlockSpec((bq, HD), lambda h, i: (i, h))
    colv = lambda: pl.BlockSpec((None, bq, 1), lambda h, i: (h, i, 0))
    rowv = lambda: pl.BlockSpec((None, nkc, bk), lambda h, i: (h, 0, 0))
    return pl.pallas_call(
        body, grid=(NH, s // bq),
        in_specs=[blk(), full(), full(FV // HD), colv(), rowv(), colv(), blk(), pl.BlockSpec(memory_space=pl.ANY)],
        out_specs=[blk(), full(), full(FV // HD), rowv()],
        out_shape=[_sds((s, BW), F32), _sds((s, BW), F32), _sds(dzz.shape, dzz.dtype), _sds((NH, nkc, bk), F32)],
        input_output_aliases={7: 2},
        scratch_shapes=[pltpu.VMEM((nkc, bq, bk), F32), pltpu.VMEM((nkc, bq, bk), F32), pltpu.VMEM((s, HD), F32)],
        compiler_params=_cparams(("parallel", "arbitrary")), name=name)(qn, kn, zz, cum_col, cum_row, lse, do, dzz)


def _branch_f(rets, rgs, glas, ggs, ret_g, gla_g):
    out_r, out_g = [], []
    for h in range(NH):
        xc = rets[h] - jnp.mean(rets[h], axis=-1, keepdims=True)
        y = xc * lax.rsqrt(jnp.mean(xc * xc, axis=-1, keepdims=True) + EPS) * ret_g[h]
        out_r.append(_silu(rgs[h]) * y)
        x = glas[h]
        y = x * lax.rsqrt(jnp.mean(x * x, axis=-1, keepdims=True) + EPS) * gla_g
        out_g.append(_silu(ggs[h]) * y)
    return out_r, out_g


def _w_br_spec(layer):
    return pl.BlockSpec((None, 3, BW, D), lambda i: (layer, 0, 0, 0))


def mix_fwd(ret_raw, gla_raw, fox_o, zz, ret_g, gla_g, b_mg, w_br, *, name, layer):
    s = zz.shape[0]
    t = _row_tile(s)

    def body(r_ref, g_ref, f_ref, rg_ref, gg_ref, gp_ref, rgn_ref, ggn_ref, bmg_ref, w_ref, o_ref):
        rgn = rgn_ref[...]
        br_r, br_g = _branch_f(_heads(r_ref[...]), _heads(rg_ref[...]), _heads(g_ref[...]), _heads(gg_ref[...]),
                               _heads(rgn), ggn_ref[...])
        brs = [jnp.concatenate(br_r, axis=1), jnp.concatenate(br_g, axis=1), f_ref[...]]
        acc = jnp.zeros((t, D), F32)
        for b in range(3):
            gate = jax.nn.sigmoid(gp_ref[:, b * D:(b + 1) * D] + bmg_ref[:, b * D:(b + 1) * D])
            acc = acc + gate * _dot(brs[b].astype(MMT), w_ref[b])
        o_ref[...] = acc.astype(o_ref.dtype)

    row = lambda w, c=0: pl.BlockSpec((t, w), lambda i: (i, c // w))
    cst = lambda shp: pl.BlockSpec(shp, lambda i: (0,) * len(shp))
    return pl.pallas_call(
        body, grid=(s // t,),
        in_specs=[row(BW), row(BW), row(BW), row(BW, RG), row(BW, GG), row(3 * D, GP), cst((1, BW)), cst((1, HD)),
                  cst((1, 3 * D)), _w_br_spec(layer)],
        out_specs=row(D), out_shape=_sds((s, D), MMT),
        compiler_params=_cparams(("parallel",)), name=name)(ret_raw, gla_raw, fox_o, zz, zz, zz, ret_g, gla_g, b_mg, w_br)


def mix_bwd(ret_raw, gla_raw, fox_o, zz, ret_g, gla_g, b_mg, w_br, dmi, *, name, layer):
    s = zz.shape[0]
    t = _row_tile(s)

    def body(r_ref, g_ref, f_ref, rg_ref, gg_ref, gp_ref, rgn_ref, ggn_ref, bmg_ref, w_ref, dmi_ref,
             dr_ref, dg_ref, df_ref, dgp_ref, dw_ref, drgn_ref, dggn_ref, dbmg_ref):
        @pl.when(pl.program_id(0) == 0)
        def _():
            dw_ref[...] = jnp.zeros_like(dw_ref)
            drgn_ref[...] = jnp.zeros_like(drgn_ref)
            dggn_ref[...] = jnp.zeros_like(dggn_ref)
            dbmg_ref[...] = jnp.zeros_like(dbmg_ref)

        (br_r, br_g), vjp = jax.vjp(_branch_f, _heads(r_ref[...]), _heads(rg_ref[...]), _heads(g_ref[...]),
                                    _heads(gg_ref[...]), _heads(rgn_ref[...]), ggn_ref[...])
        brs = [jnp.concatenate(br_r, axis=1).astype(MMT), jnp.concatenate(br_g, axis=1).astype(MMT),
               f_ref[...].astype(MMT)]
        dmi_v = dmi_ref[...].astype(F32)
        dbr = []
        for b in range(3):
            w = w_ref[b]
            ybr = _dot(brs[b], w)
            gate = jax.nn.sigmoid(gp_ref[:, b * D:(b + 1) * D] + bmg_ref[:, b * D:(b + 1) * D])
            dgp = dmi_v * ybr * gate * (1.0 - gate)
            dgp_ref[:, b * D:(b + 1) * D] = dgp.astype(dgp_ref.dtype)
            dbmg_ref[:, b * D:(b + 1) * D] += jnp.sum(dgp, axis=0, keepdims=True)
            dy = (dmi_v * gate).astype(MMT)
            dw_ref[b] += _dot_tn(brs[b], dy)
            dbr.append(_dot_nt(dy, w))
        dr, drg, dg, dgg, drgn, dggn = vjp((_heads(dbr[0]), _heads(dbr[1])))
        dr_ref[...] = jnp.concatenate(dr, axis=1)
        dg_ref[...] = jnp.concatenate(dg, axis=1)
        df_ref[...] = dbr[2]
        dgp_ref[:, RG:RG + BW] = jnp.concatenate(drg, axis=1).astype(dgp_ref.dtype)
        dgp_ref[:, GG:GG + BW] = jnp.concatenate(dgg, axis=1).astype(dgp_ref.dtype)
        drgn_ref[...] += jnp.concatenate(drgn, axis=1)
        dggn_ref[...] += dggn

    row = lambda w, c=0: pl.BlockSpec((t, w), lambda i: (i, c // w))
    cst = lambda shp: pl.BlockSpec(shp, lambda i: (0,) * len(shp))
    return pl.pallas_call(
        body, grid=(s // t,),
        in_specs=[row(BW), row(BW), row(BW), row(BW, RG), row(BW, GG), row(3 * D, GP), cst((1, BW)), cst((1, HD)),
                  cst((1, 3 * D)), _w_br_spec(layer), row(D)],
        out_specs=[row(BW), row(BW), row(BW), row(FV), cst((3, BW, D)), cst((1, BW)), cst((1, HD)), cst((1, 3 * D))],
        out_shape=[_sds((s, BW), F32)] * 3 + [_sds((s, NZZ), MMT), _sds((3, BW, D), F32), _sds((1, BW), F32),
                                              _sds((1, HD), F32), _sds((1, 3 * D), F32)],
        compiler_params=_cparams(("arbitrary",)), name=name)(ret_raw, gla_raw, fox_o, zz, zz, zz, ret_g, gla_g, b_mg, w_br, dmi)


CT = 256


def _shift_down(x, k, rows):
    return jnp.where(rows >= k, pltpu.roll(x, k, 0), 0.0)


def _shift_up(x, k, rows, s):
    return jnp.where(rows < s - k, pltpu.roll(x, s - k, 0), 0.0)


def conv_fwd(ug, w_conv, b_conv, *, name):
    s = ug.shape[0]
    nt = DFF // CT

    def body(u_ref, g_ref, w_ref, b_ref, a_ref):
        u = u_ref[...]
        rows = lax.broadcasted_iota(jnp.int32, u.shape, 0)
        uc = b_ref[...] + w_ref[0:1, :] * _shift_down(u, 2, rows) + w_ref[1:2, :] * _shift_down(u, 1, rows) + w_ref[2:3, :] * u
        a_ref[...] = (_silu(uc) * g_ref[...]).astype(a_ref.dtype)

    return pl.pallas_call(
        body, grid=(nt,),
        in_specs=[pl.BlockSpec((s, CT), lambda j: (0, j)), pl.BlockSpec((s, CT), lambda j: (0, nt + j)),
                  pl.BlockSpec((3, CT), lambda j: (0, j)), pl.BlockSpec((1, CT), lambda j: (0, j))],
        out_specs=pl.BlockSpec((s, CT), lambda j: (0, j)), out_shape=_sds((s, DFF), MMT),
        compiler_params=_cparams(("parallel",)), name=name)(ug, ug, w_conv, b_conv)


def conv_bwd(ug, w_conv, b_conv, da, *, name):
    s = ug.shape[0]
    nt = DFF // CT

    def body(u_ref, g_ref, w_ref, b_ref, da_ref, du_ref, dg_ref, dw_ref, db_ref):
        u = u_ref[...]
        rows = lax.broadcasted_iota(jnp.int32, u.shape, 0)
        u2, u1 = _shift_down(u, 2, rows), _shift_down(u, 1, rows)
        uc = b_ref[...] + w_ref[0:1, :] * u2 + w_ref[1:2, :] * u1 + w_ref[2:3, :] * u
        sg = jax.nn.sigmoid(uc)
        da_v = da_ref[...]
        dg_ref[...] = (da_v * uc * sg).astype(dg_ref.dtype)
        duc = da_v * g_ref[...] * sg * (1.0 + uc * (1.0 - sg))
        du = w_ref[2:3, :] * duc + w_ref[1:2, :] * _shift_up(duc, 1, rows, s) + w_ref[0:1, :] * _shift_up(duc, 2, rows, s)
        du_ref[...] = du.astype(du_ref.dtype)
        dw_ref[0:1, :] = jnp.sum(duc * u2, axis=0, keepdims=True)
        dw_ref[1:2, :] = jnp.sum(duc * u1, axis=0, keepdims=True)
        dw_ref[2:3, :] = jnp.sum(duc * u, axis=0, keepdims=True)
        db_ref[...] = jnp.sum(duc, axis=0, keepdims=True)

    col = lambda: pl.BlockSpec((s, CT), lambda j: (0, j))
    return pl.pallas_call(
        body, grid=(nt,),
        in_specs=[col(), pl.BlockSpec((s, CT), lambda j: (0, nt + j)), pl.BlockSpec((3, CT), lambda j: (0, j)),
                  pl.BlockSpec((1, CT), lambda j: (0, j)), col()],
        out_specs=[col(), col(), pl.BlockSpec((3, CT), lambda j: (0, j)), pl.BlockSpec((1, CT), lambda j: (0, j))],
        out_shape=[_sds((s, DFF), MMT), _sds((s, DFF), MMT), _sds((3, DFF), F32), _sds((1, DFF), F32)],
        compiler_params=_cparams(("parallel",)), name=name)(ug, ug, w_conv, b_conv, da)


def place_tail(dzz, dlr, dff, *, name):
    s = dzz.shape[0]
    t = _row_tile(s)

    def body(a_ref, b_ref, z_ref, o_ref):
        o_ref[...] = jnp.concatenate([a_ref[...], b_ref[...]], axis=1)

    spec = pl.BlockSpec((t, HD), lambda i: (i, 0))
    return pl.pallas_call(
        body, grid=(s // t,), in_specs=[spec, spec, pl.BlockSpec(memory_space=pl.ANY)],
        out_specs=pl.BlockSpec((t, 2 * HD), lambda i: (i, LR // (2 * HD))), out_shape=_sds(dzz.shape, dzz.dtype),
        input_output_aliases={2: 0}, compiler_params=_cparams(("parallel",)), name=name)(dlr, dff, dzz)


def _tiles(s):
    return min(1024, s)


def layer_fwd(x, mod, p, cosf, sinf):
    s = x.shape[0]
    tm = _tiles(s)
    l = p["l"]
    shift1, scale1, gate1, shift2, scale2, gate2 = mod
    h = norm_mod(x, p["norm1_g"], scale1, shift1, name="norm_mod")
    zz = mm_nn(h, p["w1"], tm=tm, tn=768, out_dtype=F32, name="mm_w1", layer=l)
    ret_raw, rprev = retention_fwd(zz, cosf, sinf, name="ret_fwd")
    gla_raw, sprev = gla_fwd(zz, p["w_a2p"], p["b_gla_a"], name="gla_fwd")
    qn, kn, cum = fox_pre(zz, p["q_norm_g"], p["k_norm_g"], p["b_foxp"], name="fox_pre")
    bq, bk = _fox_blocks(s)
    cum_t = cum[:, :NH].T
    cum_col, cum_row = cum_t[:, :, None], cum_t.reshape(NH, s // bk, bk)
    fox_o, lse = fox_attn_fwd(qn, kn, zz, cum_col, cum_row, name="fox_fwd")
    if "later" in p:
        p = {**p, **p["later"](fox_o)}
    mi = mix_fwd(ret_raw, gla_raw, fox_o, zz, p["ret_norm_g"], p["gla_norm_g"], p["b_mg"], p["w_br"], name="mix_fwd",
                 layer=l)
    x1, mixed = mm_nn_residual(mi, p["w_o"], x, gate1, tm=tm, tn=512, name="mm_wo", layer=l)
    h2 = norm_mod(x1, p["norm2_g"], scale2, shift2, name="norm_mod")
    ug = mm_nn(h2, p["w_up"], tm=tm, tn=512, out_dtype=F32, name="mm_wup", layer=l)
    a = conv_fwd(ug, p["w_conv"], p["b_conv"], name="conv_fwd")
    x2, y = mm_nn_residual(a, p["w_down"], x1, gate2, tm=tm, tn=512, name="mm_wdown", layer=l)
    saved = dict(x=x, h=h, zz=zz, ret_raw=ret_raw, rprev=rprev, gla_raw=gla_raw, sprev=sprev, qn=qn, kn=kn,
                 cum_col=cum_col, cum_row=cum_row, fox_o=fox_o, lse=lse, mi=mi, mixed=mixed, x1=x1, h2=h2, ug=ug, a=a, y=y)
    return x2, saved, p


def layer_bwd(dx2, mod, p, sv, cosf, sinf, stacks, slot):
    s = dx2.shape[0]
    tm = _tiles(s)
    l = p["l"]
    shift1, scale1, gate1, shift2, scale2, gate2 = mod
    g, stacks = {}, dict(stacks)
    dy, dgate2 = gate_bwd(dx2, sv["y"], gate2, name="gate_bwd")
    stacks["w_down"] = mm_tn(sv["a"], dy, tm=min(1408, DFF), tn=512, out_dtype=MMT, name="mm_dwdown",
                             stack=stacks["w_down"], layer=slot)
    da = mm_nt(dy, p["w_down"], tm=tm, tn=1408, out_dtype=F32, name="mm_da", layer=l)
    du, dg, g["w_conv"], g["b_conv"] = conv_bwd(sv["ug"], p["w_conv"], p["b_conv"], da, name="conv_bwd")
    stacks["w_up"] = mm_tn(sv["h2"], du, tm=D, tn=CT, out_dtype=MMT, name="mm_dwup_u", stack=stacks["w_up"], layer=slot)
    stacks["w_up"] = mm_tn(sv["h2"], dg, tm=D, tn=CT, out_dtype=MMT, name="mm_dwup_g", stack=stacks["w_up"], layer=slot,
                           out_col0=DFF)
    dh2 = mm_nt2(du, dg, p["w_up"], tm=min(512, s), tn=D, name="mm_dh2", layer=l)
    dx1, g["norm2_g"], dscale2, dshift2 = norm_mod_bwd(sv["x1"], dh2, dx2, p["norm2_g"], scale2, shift2, name="norm_mod_bwd")
    dmixed, dgate1 = gate_bwd(dx1, sv["mixed"], gate1, name="gate_bwd")
    stacks["w_o"] = mm_tn(sv["mi"], dmixed, tm=512, tn=512, out_dtype=MMT, name="mm_dwo", stack=stacks["w_o"], layer=slot)
    dmi = mm_nt(dmixed, p["w_o"], tm=tm, tn=512, out_dtype=MMT, name="mm_dmi", layer=l)
    zz = sv["zz"]
    (dret, dgla, dfox, dzz, g["w_br"], g["ret_norm_g"], g["gla_norm_g"], g["b_mg"]) = mix_bwd(
        sv["ret_raw"], sv["gla_raw"], sv["fox_o"], zz, p["ret_norm_g"], p["gla_norm_g"], p["b_mg"], p["w_br"], dmi,
        name="mix_bwd", layer=l)
    dqn, dkn, dzz, dcum_row = fox_attn_bwd(sv["qn"], sv["kn"], zz, sv["cum_col"], sv["cum_row"], sv["lse"], dfox, dzz,
                                           name="fox_bwd")
    dcum = jnp.pad(dcum_row.reshape(NH, s).T, ((0, 0), (0, HD - NH)))
    dzz, dff, g["q_norm_g"], g["k_norm_g"], g["b_foxp"] = fox_pre_bwd(
        zz, p["q_norm_g"], p["k_norm_g"], p["b_foxp"], dqn, dkn, dcum, dzz, name="fox_pre_bwd")
    dzz, dlr, g["w_a2p"], g["b_gla_a"] = gla_bwd(zz, p["w_a2p"], p["b_gla_a"], sv["sprev"], dgla, dzz, name="gla_bwd")
    dzz = retention_bwd(zz, cosf, sinf, sv["rprev"], dret, dzz, name="ret_bwd")
    dzz = place_tail(dzz, dlr, dff, name="place_tail")
    stacks["w_mg"] = mm_tn(sv["h"], dzz, tm=512, tn=768, out_dtype=MMT, name="mm_dwmg", ncols=WZ0, stack=stacks["w_mg"],
                           layer=slot)
    dwz = mm_tn(sv["h"], dzz, tm=512, tn=768, out_dtype=MMT, name="mm_dwz", col0=WZ0)
    stacks["w_in"] = unalign_dw_in(dwz, stacks["w_in"], slot)
    dh = mm_nt(dzz, p["w1"], tm=min(256, s), tn=D, out_dtype=F32, name="mm_dh", layer=l)
    dx, g["norm1_g"], dscale1, dshift1 = norm_mod_bwd(sv["x"], dh, dx1, p["norm1_g"], scale1, shift1, name="norm_mod_bwd")
    dmod = jnp.concatenate([dshift1, dscale1, dgate1, dshift2, dscale2, dgate2], axis=1)
    return dx, g, dmod, stacks


def _align_cols(w_in, w_mg):
    z = lambda n: jnp.zeros((w_in.shape[0], n), w_in.dtype)
    seg = lambda name: w_in[:, W_IN_COLS[name][0]:W_IN_COLS[name][1]]
    return jnp.concatenate([w_mg, seg("rg"), seg("gg"), seg("fv"), seg("rqkv"), seg("gqkv"), seg("fqk"), seg("lr"),
                            z(HD - GLR), seg("ff"), z(HD - NH)], axis=1)


def _unalign_cols(dwz):
    seg = lambda c0, name: dwz[:, c0 - WZ0:c0 - WZ0 + W_IN_COLS[name][1] - W_IN_COLS[name][0]]
    return jnp.concatenate([seg(RQ, "rqkv"), seg(RG, "rg"), seg(GQ, "gqkv"), seg(LR, "lr"), seg(GG, "gg"), seg(FQ, "fqk"),
                            seg(FV, "fv"), seg(FF, "ff")], axis=1)


def build_w1(w_in_sh, w_mg):
    nl = w_mg.shape[0]
    t = _row_tile(D)

    def body(s_ref, g_ref, o_ref):
        o_ref[...] = _align_cols(jnp.concatenate([s_ref[k] for k in range(4)], axis=1), g_ref[...])

    return pl.pallas_call(
        body, grid=(nl, D // t),
        in_specs=[pl.BlockSpec((None, 4, t, IN_W // 4), lambda l, i: (l, 0, i, 0)), pl.BlockSpec((None, t, WZ0), lambda l, i: (l, i, 0))],
        out_specs=pl.BlockSpec((None, t, NZZ), lambda l, i: (l, i, 0)), out_shape=_sds((nl, D, NZZ), w_mg.dtype),
        compiler_params=_cparams(("parallel", "parallel")), name="build_w1")(w_in_sh, w_mg)


def unalign_dw_in(dwz, stack, layer):
    t = _row_tile(D)

    def body(z_ref, s_ref, o_ref):
        w = _unalign_cols(z_ref[...])
        for k in range(4):
            o_ref[k] = w[:, k * (IN_W // 4):(k + 1) * (IN_W // 4)]

    return pl.pallas_call(
        body, grid=(D // t,),
        in_specs=[pl.BlockSpec((t, NZZ - WZ0), lambda i: (i, 0)), pl.BlockSpec(memory_space=pl.ANY)],
        out_specs=pl.BlockSpec((None, 4, t, IN_W // 4), lambda i: (layer, 0, i, 0)), out_shape=_sds(stack.shape, stack.dtype),
        input_output_aliases={1: 0}, compiler_params=_cparams(("parallel",)), name="unalign_dw_in")(dwz, stack)


LATE_WEIGHTS = ("w_br", "w_o", "w_up", "w_down")


def layer_params(w, big, l, later=None):
    row = lambda v: v[l][None, :]
    p = dict(
        l=0, norm1_g=row(w["norm1_g"]), norm2_g=row(w["norm2_g"]), w1=big["w1"],
        w_a2p=jnp.pad(w["w_gla_a2"][l], ((0, HD - GLR), (0, 0))), b_gla_a=row(w["b_gla_a"]),
        b_foxp=jnp.pad(row(w["b_fox_f"]), ((0, 0), (0, HD - NH))), ret_norm_g=row(w["ret_norm_g"]),
        gla_norm_g=row(w["gla_norm_g"]), q_norm_g=row(w["q_norm_g"]), k_norm_g=row(w["k_norm_g"]),
        b_mg=row(w["b_mg"]), w_conv=w["w_conv"][l], b_conv=row(w["b_conv"]))
    if later is None:
        p.update({n: big[n] for n in LATE_WEIGHTS})
    else:
        p["later"] = later
    return p


def layer_grads(g):
    vec = lambda v: v[0]
    return dict(
        norm1_g=vec(g["norm1_g"]), norm2_g=vec(g["norm2_g"]), w_gla_a2=g["w_a2p"][:GLR], b_gla_a=vec(g["b_gla_a"]),
        b_fox_f=g["b_foxp"][0, :NH], ret_norm_g=vec(g["ret_norm_g"]), gla_norm_g=vec(g["gla_norm_g"]),
        q_norm_g=vec(g["q_norm_g"]), k_norm_g=vec(g["k_norm_g"]), w_br=g["w_br"], b_mg=vec(g["b_mg"]),
        w_conv=g["w_conv"], b_conv=vec(g["b_conv"]))


def ada_mod(c_all, w_ada, b_ada):
    nl, _, n = w_ada.shape

    def body(c_ref, w_ref, b_ref, o_ref):
        o_ref[...] = _dot(_silu(c_ref[...]), w_ref[...], HI) + b_ref[...]

    return pl.pallas_call(
        body, grid=(nl,),
        in_specs=[pl.BlockSpec((8, D), lambda l: (0, 0)), pl.BlockSpec((None, D, n), lambda l: (l, 0, 0)),
                  pl.BlockSpec((None, 1, n), lambda l: (l, 0, 0))],
        out_specs=pl.BlockSpec((None, 8, n), lambda l: (l, 0, 0)), out_shape=_sds((nl, 8, n), F32),
        compiler_params=_cparams(("parallel",)), name="ada_mod")(c_all, w_ada, b_ada)


def ada_dw(c_all, dmod):
    nl, _, n = dmod.shape

    def body(c_ref, d_ref, o_ref):
        o_ref[...] = _dot_tn(_silu(c_ref[...]), d_ref[...], HI)

    return pl.pallas_call(
        body, grid=(nl,),
        in_specs=[pl.BlockSpec((8, D), lambda l: (0, 0)), pl.BlockSpec((None, 8, n), lambda l: (l, 0, 0))],
        out_specs=pl.BlockSpec((None, D, n), lambda l: (l, 0, 0)), out_shape=_sds((nl, D, n), F32),
        compiler_params=_cparams(("parallel",)), name="ada_dw")(c_all, dmod)


def sum_devices(g):
    def body(g_ref, o_ref):
        acc = g_ref[0]
        for d in range(1, 8):
            acc = acc + g_ref[d]
        o_ref[...] = acc

    return pl.pallas_call(body, out_shape=_sds(g.shape[1:], F32), name="sum_devices")(g)


def adamw(w, g, m, v, *, block, name, rows=None, into=None, with_grad=False):
    nd = w.ndim
    lo, hi = (0, w.shape[0]) if rows is None else rows
    grid = ((hi - lo) // block[0],) + tuple(w.shape[i] // block[i] for i in range(1, nd))
    first = lo // block[0]
    nout = 4 if with_grad else 3
    bc1 = 1.0 - ADAM_B1 ** ADAM_STEP
    bc2 = 1.0 - ADAM_B2 ** ADAM_STEP

    def body(w_ref, g_ref, m_ref, v_ref, *rest):
        d_ref, nm_ref, nv_ref = rest[-nout:][:3]
        gv = g_ref[...]
        if with_grad:
            rest[-1][...] = gv
        nm = ADAM_B1 * m_ref[...] + (1.0 - ADAM_B1) * gv
        nv = ADAM_B2 * v_ref[...] + (1.0 - ADAM_B2) * (gv * gv)
        nm_ref[...] = nm
        nv_ref[...] = nv
        d_ref[...] = -ADAM_LR * ((nm / bc1) / (jnp.sqrt(nv / bc2) + ADAM_EPS) + ADAM_WD * w_ref[...])

    spec = pl.BlockSpec(tuple(block), lambda i, *j: (first + i,) + j)
    given = [] if into is None else list(into)
    return pl.pallas_call(
        body, grid=grid, in_specs=[spec] * 4 + [pl.BlockSpec(memory_space=pl.ANY)] * len(given), out_specs=[spec] * nout,
        out_shape=[_sds(w.shape, F32)] * nout, input_output_aliases={4 + i: i for i in range(len(given))},
        compiler_params=_cparams(("parallel",) * nd), name=name)(w, g, m, v, *given)


MESH = pl.DeviceIdType.MESH
ANY = pl.BlockSpec(memory_space=pl.ANY)
VM = pl.BlockSpec(memory_space=pltpu.VMEM)


def _place():
    x, y, c = lax.axis_index("x"), lax.axis_index("y"), lax.axis_index("c")
    return x, y, c, [(1 - x, y), (x, 1 - y), (1 - x, 1 - y)]


def small_allgather(v, *, name):
    m_per, n = v.shape

    def body(x_ref, out_ref, send_sems, recv_sems, local_sem):
        x, y, c, chips = _place()
        me, sibling = (x, y, c), (x, y, 1 - c)

        def rows(px, py, pc):
            return out_ref.at[pl.ds((4 * px + 2 * py + pc) * m_per, m_per), :]

        def copy(k, block, to, src=None):
            return pltpu.make_async_remote_copy(
                src_ref=rows(*block) if src is None else src, dst_ref=rows(*block),
                send_sem=send_sems.at[k], recv_sem=recv_sems.at[k], device_id=to, device_id_type=MESH)

        mine = pltpu.make_async_copy(x_ref, rows(*me), local_sem)
        mine.start()
        first = [copy(0, me, sibling, src=x_ref)]
        first += [copy(1 + j, me, (*chip, c), src=x_ref) for j, chip in enumerate(chips)]
        for cp in first:
            cp.start()
        passed = [copy(4 + j, (*chip, c), sibling) for j, chip in enumerate(chips)]
        for j, chip in enumerate(chips):
            copy(1 + j, (*chip, c), me).wait_recv()
            passed[j].start()
        copy(0, sibling, me).wait_recv()
        for j, chip in enumerate(chips):
            copy(4 + j, (*chip, 1 - c), me).wait_recv()
        for cp in first + passed:
            cp.wait_send()
        mine.wait()

    return pl.pallas_call(
        body, out_shape=_sds((8 * m_per, n), v.dtype), in_specs=[VM], out_specs=VM,
        scratch_shapes=[pltpu.SemaphoreType.DMA((7,)), pltpu.SemaphoreType.DMA((7,)), pltpu.SemaphoreType.DMA],
        name=name)(v)


TENSORS = {
    "w_in": ("lead", None, (4, D, 1285), (1, D, 1285)),
    "w_mg": ("col", 768, (D, 3072), (512, 3072)),
    "w_br": ("col", 256, (3, BW, D), (3, BW, D)),
    "w_o": ("row", 256, (D, D), (D, D)),
    "w_up": ("col", 1408, (D, 5632), (256, 5632)),
    "w_down": ("row", 704, (DFF, D), (704, D)),
}
BIG = tuple(TENSORS)


def _shard_shape(name):
    kind, width, full, _ = TENSORS[name]
    if kind == "lead":
        return full[1:]
    return full[:-1] + (width,) if kind == "col" else (width,) + full[1:]


def _shard_view(ref, layers, name, k):
    kind, width, full, _ = TENSORS[name]
    if kind == "lead":
        return ref.at[layers, k]
    if kind == "row":
        return ref.at[layers, pl.ds(k * width, width)]
    return ref.at[(layers,) + (slice(None),) * (len(full) - 1) + (pl.ds(k * width, width),)]


def _remote(send_sems, recv_sems, k, src, dst, to):
    return pltpu.make_async_remote_copy(src_ref=src, dst_ref=dst, send_sem=send_sems.at[k], recv_sem=recv_sems.at[k],
                                        device_id=to, device_id_type=MESH)


def _dma_sems(n):
    return [pltpu.SemaphoreType.DMA((n,)), pltpu.SemaphoreType.DMA((n,))]


RS_GROUP = 2
HBM = pl.BlockSpec(memory_space=pltpu.HBM)
SEM = pl.BlockSpec(memory_space=pltpu.SEMAPHORE)
SPLIT_CALL = dict(compiler_params=pltpu.CompilerParams(has_side_effects=pltpu.SideEffectType.DATAFLOW_SIDE_EFFECTING))
PULL_SET = (("w_mg", "w_up", "w_o"), ("w_in", "w_br", "w_down"))
FIRST_NEEDED = ("w_in", "w_mg")
NEEDED_LATER = tuple(n for n in BIG if n not in FIRST_NEEDED)


def _in_hbm(a):
    return pltpu.with_memory_space_constraint(a, pltpu.HBM)


def _pull_sends(send_sems, recv_sems, p, o, layer, core, x, y, chips, names=BIG):
    return [_remote(send_sems, recv_sems, 3 * BIG.index(n) + j, p[n].at[layer], _shard_view(o[n], 0, n, 2 * x + y), (*chip, core))
            for n in PULL_SET[core] if n in names for j, chip in enumerate(chips)]


def _pull_arrivals(send_sems, recv_sems, o, core, x, y, chips, to, names=BIG):
    views = [(3 * BIG.index(n) + j, _shard_view(o[n], 0, n, 2 * chip[0] + chip[1]))
             for n in PULL_SET[core] if n in names for j, chip in enumerate(chips)]
    return [_remote(send_sems, recv_sems, k, v, v, to) for k, v in views]


def gather_start(shards, layer, *, name, after=None, names=BIG, lands=None):
    nt = len(BIG)
    first = [] if after is None else [after]

    def body(*refs):
        p, o = dict(zip(BIG, refs[:nt])), dict(zip(BIG, refs[nt:2 * nt]))
        ss, rs = refs[2 * nt + len(first)], refs[2 * nt + len(first) + 1]
        x, y, c, chips = _place()
        for core in (0, 1):
            @pl.when(c == core)
            def _():
                for cp in _pull_sends(ss, rs, p, o, layer, core, x, y, chips, names):
                    cp.start()
        refs[-1][...] = jnp.zeros_like(refs[-1])

    if lands is None:
        lands = [lax.empty((1,) + TENSORS[n][2], shards[n].dtype) for n in BIG]
    lands = [_in_hbm(a) for a in lands]
    outs = pl.pallas_call(
        body,
        out_shape=(pltpu.SemaphoreType.DMA((3 * nt,)), pltpu.SemaphoreType.DMA((3 * nt,)),
                   *[pltpu.HBM(a.shape, a.dtype) for a in lands], _sds((8, HD), F32)),
        in_specs=[HBM] * (2 * nt) + [ANY] * len(first), out_specs=(SEM, SEM, *[HBM] * nt, VM),
        input_output_aliases={nt + t: 2 + t for t in range(nt)}, name=name, **SPLIT_CALL)(
            *[_in_hbm(shards[n]) for n in BIG], *lands, *first)
    return outs[0], outs[1], outs[2:2 + nt], outs[-1]


def gather_wait(send_sems, recv_sems, shards, lands, after, layer, *, name, names=BIG):
    nt = len(BIG)

    def body(*refs):
        p, o = dict(zip(BIG, refs[:nt])), dict(zip(BIG, refs[nt:2 * nt]))
        ss, rs = refs[2 * nt], refs[2 * nt + 1]
        x, y, c, chips = _place()
        for core in (0, 1):
            @pl.when(c == core)
            def _():
                for cp in _pull_sends(ss, rs, p, o, layer, core, x, y, chips, names):
                    cp.wait_send()
                for cp in _pull_arrivals(ss, rs, o, core, x, y, chips, (x, y, core), names):
                    cp.wait_recv()

    return pl.pallas_call(
        body, out_shape=tuple(pltpu.HBM(a.shape, a.dtype) for a in lands),
        in_specs=[HBM] * (2 * nt) + [SEM, SEM] + [ANY] * len(after), out_specs=tuple([HBM] * nt),
        input_output_aliases={nt + t: t for t in range(nt)}, name=name, **SPLIT_CALL)(
            *[_in_hbm(shards[n]) for n in BIG], *lands, send_sems, recv_sems, *after)


def gather_forward(shards, lands, layer, *, name, names=BIG):
    nt = len(BIG)

    def body(*refs):
        p, o = dict(zip(BIG, refs[:nt])), dict(zip(BIG, refs[2 * nt:3 * nt]))
        ss, rs = refs[3 * nt:]
        x, y, c, chips = _place()
        for core in (0, 1):
            @pl.when(c == core)
            def _():
                me, sibling = (x, y, core), (x, y, 1 - core)
                sends = _pull_arrivals(ss, rs, o, core, x, y, chips, sibling, names)
                sends += [_remote(ss, rs, 3 * nt + t, p[n].at[layer], _shard_view(o[n], 0, n, 2 * x + y), sibling)
                          for t, n in enumerate(BIG) if n in names]
                for cp in sends:
                    cp.start()
                for cp in sends:
                    cp.wait_send()
                for cp in _pull_arrivals(ss, rs, o, 1 - core, x, y, chips, me, names):
                    cp.wait_recv()
                for t, n in enumerate(BIG):
                    if n in names:
                        own = _shard_view(o[n], 0, n, 2 * x + y)
                        _remote(ss, rs, 3 * nt + t, own, own, me).wait_recv()

    outs = pl.pallas_call(
        body, out_shape=[_sds(a.shape, a.dtype) for a in lands], in_specs=[ANY] * (2 * nt), out_specs=[ANY] * nt,
        input_output_aliases={nt + t: t for t in range(nt)}, scratch_shapes=_dma_sems(4 * nt), name=name)(
            *[shards[n] for n in BIG], *lands)
    return dict(zip(BIG, outs))


def pair_exchange(g, *, name):
    hh = g[BIG[0]].shape[0] // 2
    nt = len(BIG)

    def body(*refs):
        send_sems, recv_sems = refs[2 * nt:]
        x, y, c, _ = _place()
        copies = [_remote(send_sems, recv_sems, t, refs[t].at[pl.ds(hh * (1 - c), hh)], refs[nt + t], (x, y, 1 - c))
                  for t in range(nt)]
        for cp in copies:
            cp.start()
        for cp in copies:
            cp.wait()

    outs = pl.pallas_call(
        body, out_shape=[_sds((hh,) + g[n].shape[1:], g[n].dtype) for n in BIG], in_specs=[ANY] * nt, out_specs=[ANY] * nt,
        scratch_shapes=_dma_sems(nt), name=name)(*[g[n] for n in BIG])
    return dict(zip(BIG, outs))


def _chip_copies(send_sems, recv_sems, s_refs, land_refs, c, chips):
    hl = s_refs[0].shape[0]
    return [_remote(send_sems, recv_sems, 3 * t + j, _shard_view(s_refs[t], pl.ds(0, hl), n, 2 * chip[0] + chip[1]),
                    land_refs[t].at[j], (*chip, c))
            for t, n in enumerate(BIG) for j, chip in enumerate(chips)]


def _landing_shapes(s):
    hl = s[BIG[0]].shape[0]
    return [_sds((3, hl) + _shard_shape(n), s[n].dtype) for n in BIG]


def chip_exchange(s, *, name):
    nt = len(BIG)

    def body(*refs):
        send_sems, recv_sems = refs[2 * nt:]
        x, y, c, chips = _place()
        copies = _chip_copies(send_sems, recv_sems, refs[:nt], refs[nt:2 * nt], c, chips)
        for cp in copies:
            cp.start()
        for cp in copies:
            cp.wait()

    outs = pl.pallas_call(
        body, out_shape=_landing_shapes(s), in_specs=[ANY] * nt, out_specs=[ANY] * nt,
        scratch_shapes=_dma_sems(3 * nt), name=name)(*[s[n] for n in BIG])
    return dict(zip(BIG, outs))


def chip_exchange_start(s, *, name, after=None):
    nt = len(BIG)
    first = [] if after is None else [after]

    def body(*refs):
        o = refs[2 * nt + len(first):]
        x, y, c, chips = _place()
        for cp in _chip_copies(o[0], o[1], refs[:nt], refs[nt:2 * nt], c, chips):
            cp.start()
        refs[-1][...] = jnp.zeros_like(refs[-1])

    lands = [_in_hbm(lax.empty(d.shape, d.dtype)) for d in _landing_shapes(s)]
    srcs = [_in_hbm(s[n]) for n in BIG]
    outs = pl.pallas_call(
        body,
        out_shape=(pltpu.SemaphoreType.DMA((3 * nt,)), pltpu.SemaphoreType.DMA((3 * nt,)),
                   *[pltpu.HBM(a.shape, a.dtype) for a in srcs + lands], _sds((8, HD), F32)),
        in_specs=[HBM] * (2 * nt) + [ANY] * len(first), out_specs=(SEM, SEM, *[HBM] * (2 * nt), VM),
        input_output_aliases={t: 2 + t for t in range(2 * nt)}, name=name, **SPLIT_CALL)(*srcs, *lands, *first)
    return outs[0], outs[1], outs[2:2 + nt], outs[2 + nt:2 + 2 * nt], outs[-1]


def chip_exchange_wait(send_sems, recv_sems, srcs, lands, after, *, name):
    nt = len(BIG)

    def body(*refs):
        x, y, c, chips = _place()
        for cp in _chip_copies(refs[2 * nt], refs[2 * nt + 1], refs[:nt], refs[nt:2 * nt], c, chips):
            cp.wait_send()
            cp.wait_recv()

    outs = pl.pallas_call(
        body, out_shape=tuple(pltpu.HBM(a.shape, a.dtype) for a in list(srcs) + list(lands)),
        in_specs=[HBM] * (2 * nt) + [SEM, SEM] + [ANY] * len(after), out_specs=tuple([HBM] * (2 * nt)),
        input_output_aliases={t: t for t in range(2 * nt)}, name=name, **SPLIT_CALL)(
            *srcs, *lands, send_sems, recv_sems, *after)
    return dict(zip(BIG, outs[:nt])), dict(zip(BIG, outs[nt:]))


def pair_share(f, l0, hh, *, name):
    nt = len(BIG)

    def body(*refs):
        o = refs[nt:2 * nt]
        send_sems, recv_sems = refs[2 * nt:]
        x, y, c, _ = _place()
        mine, theirs = pl.ds(l0 + hh * c, hh), pl.ds(l0 + hh * (1 - c), hh)
        copies = [_remote(send_sems, recv_sems, t, o[t].at[mine], o[t].at[mine], (x, y, 1 - c)) for t in range(nt)]
        for cp in copies:
            cp.start()
        for t, cp in enumerate(copies):
            cp.wait_send()
            _remote(send_sems, recv_sems, t, o[t].at[theirs], o[t].at[theirs], (x, y, c)).wait_recv()

    outs = pl.pallas_call(
        body, out_shape=[_sds(f[n].shape, f[n].dtype) for n in BIG], in_specs=[ANY] * nt, out_specs=[ANY] * nt,
        input_output_aliases={t: t for t in range(nt)}, scratch_shapes=_dma_sems(nt), name=name)(*[f[n] for n in BIG])
    return dict(zip(BIG, outs))


def pair_add(g, r, idx, *, tensor, name):
    _, _, full, blk = TENSORS[tensor]
    hh = r.shape[0]

    def body(idx_ref, g_ref, r_ref, o_ref):
        o_ref[...] = (g_ref[...].astype(F32) + r_ref[...].astype(F32)).astype(o_ref.dtype)

    own = pl.BlockSpec((None,) + blk, lambda *a: (a[0],) + a[1:-1])
    return pl.pallas_call(
        body, out_shape=_sds(r.shape, r.dtype),
        grid_spec=pltpu.PrefetchScalarGridSpec(
            num_scalar_prefetch=1, grid=(hh,) + tuple(f // b for f, b in zip(full, blk)),
            in_specs=[pl.BlockSpec((None,) + blk, lambda *a: (hh * a[-1][0] + a[0],) + a[1:-1]), own], out_specs=own),
        compiler_params=_cparams(("parallel",) * (1 + len(full))), name=name)(idx, g, r)


def chip_add(s, r, idx, totals, l0, *, tensor, name):
    kind, width, full, _ = TENSORS[tensor]
    shard = _shard_shape(tensor)
    hh = s.shape[0]
    zeros = (0,) * len(shard)

    def body(idx_ref, s_ref, r0_ref, r1_ref, r2_ref, t_ref, o_ref):
        o_ref[...] = ((s_ref[...].astype(F32) + r0_ref[...].astype(F32)) + r1_ref[...].astype(F32)) + r2_ref[...].astype(F32)

    if kind == "lead":
        mine = pl.BlockSpec((None, None) + shard, lambda i, ix: (i, ix[1]) + zeros)
    elif kind == "row":
        mine = pl.BlockSpec((None,) + shard, lambda i, ix: (i, ix[1]) + zeros[1:])
    else:
        mine = pl.BlockSpec((None,) + shard, lambda i, ix: (i,) + zeros[1:] + (ix[1],))
    peer = lambda j: pl.BlockSpec((None, None) + shard, lambda i, ix: (j, i) + zeros)
    return pl.pallas_call(
        body, out_shape=_sds(totals.shape, F32),
        grid_spec=pltpu.PrefetchScalarGridSpec(
            num_scalar_prefetch=1, grid=(hh,), in_specs=[mine, peer(0), peer(1), peer(2), pl.BlockSpec(memory_space=pl.ANY)],
            out_specs=pl.BlockSpec((None,) + shard, lambda i, ix: (l0 + hh * ix[0] + i,) + zeros)),
        input_output_aliases={5: 0}, compiler_params=_cparams(("parallel",)), name=name)(idx, s, r, r, r, totals)


def _flat_rows(arrs):
    v = jnp.concatenate([a.reshape(-1) for a in arrs])
    n = -(-v.shape[0] // 1024) * 1024
    return jnp.pad(v, (0, n - v.shape[0])).reshape(n // HD, HD)


def _unflat(buf, shapes):
    v, out, o = buf.reshape(-1), [], 0
    for s in shapes:
        n = int(np.prod(s))
        out.append(v[o:o + n].reshape(s))
        o += n
    return out


WEIGHTS = ("norm1_g", "norm2_g", "w_ada", "b_ada", "w_in", "w_gla_a2", "b_gla_a", "b_fox_f", "ret_norm_g", "gla_norm_g",
           "q_norm_g", "k_norm_g", "w_br", "w_mg", "b_mg", "w_o", "w_up", "w_conv", "b_conv", "w_down")
REPLICATED = ("norm1_g", "norm2_g", "b_gla_a", "b_fox_f", "ret_norm_g", "gla_norm_g", "q_norm_g", "k_norm_g", "b_mg", "b_conv")
ADAM_BLOCKS = dict(w_ada=(1, 256, 1536), w_in=(1, 256, 1285), w_br=(1, 3, BW, 256), w_mg=(1, 512, 768), w_o=(2, 256, D),
                   w_up=(1, 256, 1408), w_down=(1, 352, D))
ALL_AXES = ("x", "y", "c")


def kernel(x, c, norm1_g, norm2_g, w_ada, b_ada, w_in, w_gla_a2, b_gla_a, b_fox_f, ret_norm_g, gla_norm_g, q_norm_g, k_norm_g, w_br, w_mg, b_mg, w_o, w_up, w_conv, b_conv, w_down, loss_target, m_norm1_g, m_norm2_g, m_w_ada, m_b_ada, m_w_in, m_w_gla_a2, m_b_gla_a, m_b_fox_f, m_ret_norm_g, m_gla_norm_g, m_q_norm_g, m_k_norm_g, m_w_br, m_w_mg, m_b_mg, m_w_o, m_w_up, m_w_conv, m_b_conv, m_w_down, v_norm1_g, v_norm2_g, v_w_ada, v_b_ada, v_w_in, v_w_gla_a2, v_b_gla_a, v_b_fox_f, v_ret_norm_g, v_gla_norm_g, v_q_norm_g, v_k_norm_g, v_w_br, v_w_mg, v_b_mg, v_w_o, v_w_up, v_w_conv, v_b_conv, v_w_down):
    w = dict(zip(WEIGHTS, (norm1_g, norm2_g, w_ada, b_ada, w_in, w_gla_a2, b_gla_a, b_fox_f, ret_norm_g, gla_norm_g,
                           q_norm_g, k_norm_g, w_br, w_mg, b_mg, w_o, w_up, w_conv, b_conv, w_down)))
    m = dict(zip(WEIGHTS, (m_norm1_g, m_norm2_g, m_w_ada, m_b_ada, m_w_in, m_w_gla_a2, m_b_gla_a, m_b_fox_f, m_ret_norm_g,
                           m_gla_norm_g, m_q_norm_g, m_k_norm_g, m_w_br, m_w_mg, m_b_mg, m_w_o, m_w_up, m_w_conv, m_b_conv,
                           m_w_down)))
    v = dict(zip(WEIGHTS, (v_norm1_g, v_norm2_g, v_w_ada, v_b_ada, v_w_in, v_w_gla_a2, v_b_gla_a, v_b_fox_f, v_ret_norm_g,
                           v_gla_norm_g, v_q_norm_g, v_k_norm_g, v_w_br, v_w_mg, v_b_mg, v_w_o, v_w_up, v_w_conv, v_b_conv,
                           v_w_down)))
    nl = norm1_g.shape[0]
    seq = x.shape[1]
    xi, yi, ci = lax.axis_index("x"), lax.axis_index("y"), lax.axis_index("c")
    k_me = 2 * xi + yi
    b_me = 4 * xi + 2 * yi + ci
    ada_n = w_ada.shape[2]
    a2_n, conv_n = w_gla_a2.shape[2], w_conv.shape[2]

    shards = [{n: w[n][:1].astype(MMT) for n in BIG}]

    def gather_finish(l, started, after):
        send_sems, recv_sems, lands, _ = started
        lands = gather_wait(send_sems, recv_sems, shards[l], lands, [after], 0, name=f"gather{l}_wait")
        big = gather_forward(shards[l], lands, 0, name=f"gather{l}_forward")
        big["w1"] = build_w1(big["w_in"], big["w_mg"])
        return big

    blk = _flat_rows([c, w_gla_a2, w_conv])
    g1 = small_allgather(blk, name="gather_small").reshape(8, blk.shape[0], HD)
    c_all = g1[:, :D // HD].reshape(8, D)
    by_chip = g1[0::2].reshape(4, -1)[:, D:]
    a2_sh, conv_sh = by_chip[:, :nl * GLR * a2_n], by_chip[:, nl * GLR * a2_n:nl * (GLR * a2_n + 3 * conv_n)]
    full_small = dict(
        w_gla_a2=a2_sh.reshape(4, nl, GLR, a2_n).transpose(1, 2, 0, 3).reshape(nl, GLR, 4 * a2_n),
        w_conv=conv_sh.reshape(4, nl, 3, conv_n).transpose(1, 2, 0, 3).reshape(nl, 3, 4 * conv_n))

    b_ada_sh = lax.dynamic_slice_in_dim(b_ada, k_me * ada_n, ada_n, axis=1)[:, None, :]
    mod_sh = ada_mod(c_all, w_ada, b_ada_sh)
    g2 = small_allgather(mod_sh.reshape(nl * 8, ada_n), name="gather_mod").reshape(4, 2, nl, 8, ada_n)[:, 0]
    mod_me = lax.dynamic_index_in_dim(g2, b_me, axis=2, keepdims=False).transpose(1, 0, 2).reshape(nl, 4 * ada_n)

    wsmall = {n: w[n] for n in REPLICATED}
    wsmall.update(full_small)
    mods = [[mod_me[l:l + 1, i * D:(i + 1) * D] for i in range(6)] for l in range(nl)]

    send_sems, recv_sems, lands, token = gather_start(shards[0], 0, name="gather0_start_first", after=mod_me,
                                                      names=FIRST_NEEDED)
    shards += [{n: (w[n][l:l + 1] + token[0, 0]).astype(MMT) for n in BIG} for l in range(1, nl)]
    cast_meanwhile = [a for sh in shards[1:] for a in sh.values()]
    lands = gather_wait(send_sems, recv_sems, shards[0], lands, [token, *cast_meanwhile], 0, name="gather0_wait_first",
                        names=FIRST_NEEDED)
    first = gather_forward(shards[0], lands, 0, name="gather0_forward_first", names=FIRST_NEEDED)
    later_sems = gather_start(shards[0], 0, name="gather0_start_later", names=NEEDED_LATER, lands=[first[n] for n in BIG])
    in_flight = dict(zip(BIG, later_sems[2]))
    big = {"w1": build_w1(in_flight["w_in"], in_flight["w_mg"])}

    def rest_of_layer0(after):
        rest = gather_wait(later_sems[0], later_sems[1], shards[0], later_sems[2], [after], 0, name="gather0_wait_later",
                           names=NEEDED_LATER)
        return gather_forward(shards[0], rest, 0, name="gather0_forward_later", names=NEEDED_LATER)

    cosf, sinf = _rope_tables(seq)
    xs, saved, params = x[0], [], []
    for l in range(nl):
        if l + 1 < nl:
            started = gather_start(shards[l + 1], 0, name=f"gather{l + 1}_start", after=xs if l else big["w1"])
            mods[l][1] = mods[l][1] + started[-1][0, 0]
        xs, sv, p = layer_fwd(xs, mods[l], layer_params(wsmall, big, l, later=rest_of_layer0 if l == 0 else None), cosf, sinf)
        saved.append(sv)
        params.append(p)
        if l + 1 < nl:
            big = gather_finish(l + 1, started, xs)
    loss_part, dx = loss_and_grad(xs, loss_target[0], name="loss")
    loss = lax.psum(loss_part[0, 0], ALL_AXES)
    grads, dmods = [None] * nl, [None] * nl
    idx = jnp.stack([ci, k_me]).astype(jnp.int32)
    totals = {n: lax.empty((nl,) + _shard_shape(n), F32) for n in BIG}

    def finish_group(pending, after, totals, idx):
        group, send_sems, recv_sems, srcs, lands, _ = pending
        sums, from_chips = chip_exchange_wait(send_sems, recv_sems, srcs, lands, after, name=f"rs{group}_chip_exchange_wait")
        totals = {n: chip_add(sums[n], from_chips[n], idx, totals[n], RS_GROUP * group, tensor=n,
                              name=f"rs{group}_chip_add_{n}") for n in BIG}
        return pair_share(totals, RS_GROUP * group, RS_GROUP // 2, name=f"rs{group}_pair_share")

    pending = None
    for group in reversed(range(nl // RS_GROUP)):
        layers = range(RS_GROUP * group, RS_GROUP * (group + 1))
        stacks = {n: lax.empty((RS_GROUP,) + TENSORS[n][2], MMT) for n in BIG if n != "w_br"}
        if pending is not None:
            mods[layers[-1]][5] = mods[layers[-1]][5] + pending[-1][0, 0]
        for l in reversed(layers):
            dx, g, dmods[l], stacks = layer_bwd(dx, mods[l], params[l], saved[l], cosf, sinf, stacks, l - layers[0])
            grads[l] = layer_grads(g)
        stacks["w_br"] = jnp.stack([grads[l]["w_br"].astype(MMT) for l in layers])
        from_sibling = pair_exchange(stacks, name=f"rs{group}_pair_exchange")
        chip_sum = {n: pair_add(stacks[n], from_sibling[n], idx, tensor=n, name=f"rs{group}_pair_add_{n}") for n in BIG}
        if group > 0:
            if pending is not None:
                totals = finish_group(pending, [dx, *chip_sum.values()], totals, idx)
            pending = (group, *chip_exchange_start(chip_sum, name=f"rs{group}_chip_exchange_start"))

    small_names = REPLICATED + ("w_gla_a2", "w_conv")
    small_shapes = [(nl, 6 * D)] + [(nl,) + grads[0][n].shape for n in small_names]
    vec = _flat_rows([jnp.concatenate(dmods, axis=0)] + [jnp.stack([grads[l][n] for l in range(nl)]) for n in small_names])
    gs = small_allgather(vec, name="gather_small_grads")
    earlier = pending
    pending = (0, *chip_exchange_start(chip_sum, name="rs0_chip_exchange_start", after=gs))
    if earlier is not None:
        totals = finish_group(earlier, [dx, pending[-1]], totals, idx)
    gs = (gs + pending[-1][0, 0]).reshape(8, vec.shape[0], HD)
    summed = _unflat(sum_devices(gs), small_shapes)
    grad = dict(zip(small_names, summed[1:]))
    grad["b_ada"] = summed[0]
    grad["w_gla_a2"] = lax.dynamic_slice_in_dim(grad["w_gla_a2"], k_me * a2_n, a2_n, axis=2)
    grad["w_conv"] = lax.dynamic_slice_in_dim(grad["w_conv"], k_me * conv_n, conv_n, axis=2)
    dmod_all = gs[:, :nl * 6 * D // HD].reshape(8, nl, 6 * D)
    dmod_sh = lax.dynamic_slice_in_dim(dmod_all, k_me * ada_n, ada_n, axis=2).transpose(1, 0, 2)
    grad["w_ada"] = ada_dw(c_all, dmod_sh)

    delta, new_m, new_v = {}, {}, {}
    delta["w_ada"], new_m["w_ada"], new_v["w_ada"] = adamw(w["w_ada"], grad["w_ada"], m["w_ada"], v["w_ada"],
                                                          block=ADAM_BLOCKS["w_ada"], name="adamw_w_ada")
    rest = [n for n in WEIGHTS if n not in ADAM_BLOCKS]
    shapes = [w[n].shape for n in rest]
    flat = [_flat_rows([t[n] for n in rest]) for t in (w, grad, m, v)]
    outs = adamw(*flat, block=flat[0].shape, name="adamw_small")
    for t, o in zip((delta, new_m, new_v), outs):
        t.update(zip(rest, _unflat(o, shapes)))

    later = {n: adamw(w[n], totals[n], m[n], v[n], block=ADAM_BLOCKS[n], name="adamw_later_" + n, rows=(RS_GROUP, nl),
                      with_grad=True) for n in BIG} if nl > RS_GROUP else {}
    done_first = [outs[0], delta["w_ada"]] + [later[n][0] for n in later]
    totals = finish_group(pending, done_first, totals, idx)
    for n in BIG:
        delta[n], new_m[n], new_v[n], grad[n] = adamw(w[n], totals[n], m[n], v[n], block=ADAM_BLOCKS[n], name="adamw_" + n,
                                                      rows=(0, min(RS_GROUP, nl)), into=later.get(n), with_grad=True)

    return (loss, dx[None], *[grad[n] for n in WEIGHTS], *[delta[n] for n in WEIGHTS], *[new_m[n] for n in WEIGHTS],
            *[new_v[n] for n in WEIGHTS])
```

```python
import functools

import numpy as np
import jax
import jax.numpy as jnp
from jax import lax
from jax.experimental import pallas as pl
from jax.experimental.pallas import tpu as pltpu

F32 = jnp.float32
MMT = jnp.bfloat16
HI = lax.Precision.HIGHEST

D = 1024
DEPTH = 4
NH = 4
HD = 128
BW = NH * HD
CH = 64
GDK = 64
GLR = 16
DFF = 2816
EPS = 1e-6
ROPE_BASE = 10000.0

GP, RG, GG, FV, RQ, RK, RV, GQ, GK, GV, FQ, FK, LR, FF = (
    0, 3072, 3584, 4096, 4608, 5120, 5632, 6144, 6400, 6656, 7168, 7680, 8192, 8320)
NZZ = 8448
WZ0 = 3072
IN_W = 5140
W_IN_COLS = dict(rqkv=(0, 1536), rg=(1536, 2048), gqkv=(2048, 3072), lr=(3072, 3088), gg=(3088, 3600), fqk=(3600, 4624),
                 fv=(4624, 5136), ff=(5136, 5140))

VMEM_LIMIT = 56 * 1024 * 1024

ADAM_LR, ADAM_B1, ADAM_B2, ADAM_EPS, ADAM_WD, ADAM_STEP = 0.001, 0.9, 0.999, 1e-08, 0.01, 10


def _cparams(sem=None):
    return pltpu.CompilerParams(dimension_semantics=sem, vmem_limit_bytes=VMEM_LIMIT)


def _sds(shape, dtype):
    return jax.ShapeDtypeStruct(tuple(shape), dtype)


def _dot(a, b, precision=None):
    return lax.dot_general(a, b, (((1,), (0,)), ((), ())), precision=precision, preferred_element_type=F32)


def _dot_nt(a, b, precision=None):
    return lax.dot_general(a, b, (((1,), (1,)), ((), ())), precision=precision, preferred_element_type=F32)


def _dot_tn(a, b, precision=None):
    return lax.dot_general(a, b, (((0,), (0,)), ((), ())), precision=precision, preferred_element_type=F32)


def _silu(x):
    return x * jax.nn.sigmoid(x)


def _log_sigmoid(x):
    return jnp.minimum(x, 0.0) - jnp.log(1.0 + jnp.exp(jnp.minimum(x, -x)))


@jax.custom_vjp
def _swap_halves(x):
    return pltpu.roll(x, HD // 2, 1)


_swap_halves.defvjp(lambda x: (_swap_halves(x), None), lambda _, g: (_swap_halves(g),))


@jax.custom_vjp
def _bdot(a, b):
    return _dot(a.astype(MMT), b.astype(MMT))


@jax.custom_vjp
def _bdot_nt(a, b):
    return _dot_nt(a.astype(MMT), b.astype(MMT))


@jax.custom_vjp
def _bdot_tn(a, b):
    return _dot_tn(a.astype(MMT), b.astype(MMT))


_bdot.defvjp(lambda a, b: (_bdot(a, b), (a, b)), lambda r, g: (_bdot_nt(g, r[1]), _bdot_tn(r[0], g)))
_bdot_nt.defvjp(lambda a, b: (_bdot_nt(a, b), (a, b)), lambda r, g: (_bdot(g, r[1]), _bdot_tn(g, r[0])))
_bdot_tn.defvjp(lambda a, b: (_bdot_tn(a, b), (a, b)), lambda r, g: (_bdot_nt(r[1], g), _bdot(r[0], g)))


def _stacked(blk, idx, layer):
    if layer is None:
        return pl.BlockSpec(blk, idx)
    return pl.BlockSpec((None,) + blk, lambda i, j: (layer,) + idx(i, j))


def mm_nn(a, b, *, tm, tn, out_dtype, name, layer=None):
    m, k = a.shape
    n = b.shape[-1]

    def body(a_ref, b_ref, o_ref):
        o_ref[...] = _dot(a_ref[...], b_ref[...]).astype(o_ref.dtype)

    return pl.pallas_call(
        body, grid=(m // tm, n // tn),
        in_specs=[pl.BlockSpec((tm, k), lambda i, j: (i, 0)), _stacked((k, tn), lambda i, j: (0, j), layer)],
        out_specs=pl.BlockSpec((tm, tn), lambda i, j: (i, j)),
        out_shape=_sds((m, n), out_dtype), compiler_params=_cparams(("parallel", "parallel")), name=name)(a, b)


def mm_nn_residual(a, b, res, gate, *, tm, tn, name, layer=None):
    m, k = a.shape
    n = b.shape[-1]

    def body(a_ref, b_ref, r_ref, g_ref, x_ref, y_ref):
        acc = _dot(a_ref[...], b_ref[...])
        y_ref[...] = acc
        x_ref[...] = r_ref[...] + g_ref[...] * acc

    return pl.pallas_call(
        body, grid=(m // tm, n // tn),
        in_specs=[pl.BlockSpec((tm, k), lambda i, j: (i, 0)), _stacked((k, tn), lambda i, j: (0, j), layer),
                  pl.BlockSpec((tm, tn), lambda i, j: (i, j)), pl.BlockSpec((1, tn), lambda i, j: (0, j))],
        out_specs=[pl.BlockSpec((tm, tn), lambda i, j: (i, j)), pl.BlockSpec((tm, tn), lambda i, j: (i, j))],
        out_shape=[_sds((m, n), F32), _sds((m, n), F32)],
        compiler_params=_cparams(("parallel", "parallel")), name=name)(a, b, res, gate)


def mm_nt(a, b, *, tm, tn, out_dtype, name, layer=None):
    m, k = a.shape
    n = b.shape[-2]

    def body(a_ref, b_ref, o_ref):
        o_ref[...] = _dot_nt(a_ref[...], b_ref[...]).astype(o_ref.dtype)

    return pl.pallas_call(
        body, grid=(m // tm, n // tn),
        in_specs=[pl.BlockSpec((tm, k), lambda i, j: (i, 0)), _stacked((tn, k), lambda i, j: (j, 0), layer)],
        out_specs=pl.BlockSpec((tm, tn), lambda i, j: (i, j)),
        out_shape=_sds((m, n), out_dtype), compiler_params=_cparams(("parallel", "parallel")), name=name)(a, b)


def mm_nt2(a1, a2, b, *, tm, tn, name, layer):
    m, k1 = a1.shape
    k2 = a2.shape[1]
    n = b.shape[-2]

    def body(a1_ref, a2_ref, b_ref, o_ref):
        o_ref[...] = _dot_nt(a1_ref[...], b_ref[:, :k1]) + _dot_nt(a2_ref[...], b_ref[:, k1:])

    return pl.pallas_call(
        body, grid=(m // tm, n // tn),
        in_specs=[pl.BlockSpec((tm, k1), lambda i, j: (i, 0)), pl.BlockSpec((tm, k2), lambda i, j: (i, 0)),
                  _stacked((tn, k1 + k2), lambda i, j: (j, 0), layer)],
        out_specs=pl.BlockSpec((tm, tn), lambda i, j: (i, j)),
        out_shape=_sds((m, n), F32), compiler_params=_cparams(("parallel", "parallel")), name=name)(a1, a2, b)


def mm_tn(a, b, *, tm, tn, out_dtype, name, col0=0, ncols=None, stack=None, layer=None, out_col0=0):
    s, m = a.shape
    n = b.shape[1] - col0 if ncols is None else ncols
    c0, oc0 = col0 // tn, out_col0 // tn

    def body(a_ref, b_ref, *rest):
        o_ref = rest[-1]
        o_ref[...] = _dot_tn(a_ref[...], b_ref[...]).astype(o_ref.dtype)

    in_specs = [pl.BlockSpec((s, tm), lambda i, j: (0, i)), pl.BlockSpec((s, tn), lambda i, j: (0, c0 + j))]
    if stack is None:
        return pl.pallas_call(
            body, grid=(m // tm, n // tn), in_specs=in_specs, out_specs=pl.BlockSpec((tm, tn), lambda i, j: (i, j)),
            out_shape=_sds((m, n), out_dtype), compiler_params=_cparams(("parallel", "parallel")), name=name)(a, b)
    return pl.pallas_call(
        body, grid=(m // tm, n // tn), in_specs=in_specs + [pl.BlockSpec(memory_space=pl.ANY)],
        out_specs=pl.BlockSpec((None, tm, tn), lambda i, j: (layer, i, oc0 + j)),
        out_shape=_sds(stack.shape, stack.dtype), input_output_aliases={2: 0},
        compiler_params=_cparams(("parallel", "parallel")), name=name)(a, b, stack)


def _row_tile(s):
    return min(256, s)


def _norm_mod_f(x, g, scale, shift):
    r = lax.rsqrt(jnp.mean(x * x, axis=-1, keepdims=True) + EPS)
    return (x * r * g) * (1.0 + scale) + shift


def norm_mod(x, g, scale, shift, *, name):
    s = x.shape[0]
    t = _row_tile(s)

    def body(x_ref, g_ref, sc_ref, sh_ref, o_ref):
        o_ref[...] = _norm_mod_f(x_ref[...], g_ref[...], sc_ref[...], sh_ref[...]).astype(o_ref.dtype)

    vec = pl.BlockSpec((1, D), lambda i: (0, 0))
    return pl.pallas_call(
        body, grid=(s // t,), in_specs=[pl.BlockSpec((t, D), lambda i: (i, 0)), vec, vec, vec],
        out_specs=pl.BlockSpec((t, D), lambda i: (i, 0)), out_shape=_sds((s, D), MMT),
        compiler_params=_cparams(("parallel",)), name=name)(x, g, scale, shift)


def norm_mod_bwd(x, dh, dres, g, scale, shift, *, name):
    s = x.shape[0]
    t = _row_tile(s)

    def body(x_ref, dh_ref, dr_ref, g_ref, sc_ref, sh_ref, dx_ref, dg_ref, dsc_ref, dsh_ref):
        @pl.when(pl.program_id(0) == 0)
        def _():
            dg_ref[...] = jnp.zeros_like(dg_ref)
            dsc_ref[...] = jnp.zeros_like(dsc_ref)
            dsh_ref[...] = jnp.zeros_like(dsh_ref)

        _, vjp = jax.vjp(_norm_mod_f, x_ref[...], g_ref[...], sc_ref[...], sh_ref[...])
        dx, dg, dsc, dsh = vjp(dh_ref[...])
        dx_ref[...] = dr_ref[...] + dx
        dg_ref[...] += dg
        dsc_ref[...] += dsc
        dsh_ref[...] += dsh

    row = pl.BlockSpec((t, D), lambda i: (i, 0))
    vec = pl.BlockSpec((1, D), lambda i: (0, 0))
    return pl.pallas_call(
        body, grid=(s // t,), in_specs=[row, row, row, vec, vec, vec], out_specs=[row, vec, vec, vec],
        out_shape=[_sds((s, D), F32)] + [_sds((1, D), F32)] * 3,
        compiler_params=_cparams(("arbitrary",)), name=name)(x, dh, dres, g, scale, shift)


def gate_bwd(dx, y, gate, *, name):
    s = dx.shape[0]
    t = _row_tile(s)

    def body(dx_ref, y_ref, g_ref, dy_ref, dg_ref):
        @pl.when(pl.program_id(0) == 0)
        def _():
            dg_ref[...] = jnp.zeros_like(dg_ref)

        dxv = dx_ref[...]
        dy_ref[...] = (g_ref[...] * dxv).astype(dy_ref.dtype)
        dg_ref[...] += jnp.sum(dxv * y_ref[...], axis=0, keepdims=True)

    row = pl.BlockSpec((t, D), lambda i: (i, 0))
    vec = pl.BlockSpec((1, D), lambda i: (0, 0))
    return pl.pallas_call(
        body, grid=(s // t,), in_specs=[row, row, vec], out_specs=[row, vec],
        out_shape=[_sds((s, D), MMT), _sds((1, D), F32)],
        compiler_params=_cparams(("arbitrary",)), name=name)(dx, y, gate)


def loss_and_grad(xf, target, *, name):
    s = xf.shape[0]
    t = _row_tile(s)

    def body(x_ref, t_ref, l_ref, dx_ref):
        @pl.when(pl.program_id(0) == 0)
        def _():
            l_ref[...] = jnp.zeros_like(l_ref)

        e = x_ref[...] - t_ref[...]
        dx_ref[...] = e * (1.0 / D)
        l_ref[...] += 0.5 * jnp.sum(jnp.sum(e * e, axis=1, keepdims=True), axis=0, keepdims=True) * (1.0 / D)

    row = pl.BlockSpec((t, D), lambda i: (i, 0))
    return pl.pallas_call(
        body, grid=(s // t,), in_specs=[row, row], out_specs=[pl.BlockSpec((1, 1), lambda i: (0, 0)), row],
        out_shape=[_sds((1, 1), F32), _sds((s, D), F32)],
        compiler_params=_cparams(("arbitrary",)), name=name)(xf, target)


def _ret_consts():
    log_g = np.log1p(-np.exp2(-5.0 - np.arange(NH, dtype=np.float32))).astype(np.float32)
    idx = np.arange(CH, dtype=np.float32)
    d_intra = np.exp(np.abs(idx[:, None] - idx[None, :])[None] * log_g[:, None, None]).astype(np.float32)
    k_w = np.exp((CH - 1.0 - idx)[None, :] * log_g[:, None]).astype(np.float32)
    q_w = np.exp((idx + 1.0)[None, :] * log_g[:, None]).astype(np.float32)
    g_chunk = [float(v) for v in np.exp(np.float32(CH) * log_g).astype(np.float32)]
    bc = lambda a: np.ascontiguousarray(np.broadcast_to(a[:, :, None], (NH, CH, HD)))
    return jnp.asarray(d_intra), jnp.asarray(bc(k_w)), jnp.asarray(bc(q_w)), g_chunk


def _rope_tables(s):
    half = HD // 2
    inv_freq = (ROPE_BASE ** (-np.arange(half, dtype=np.float64) / half)).astype(np.float32)
    ang = (np.arange(s, dtype=np.float32)[:, None] * inv_freq[None, :]).astype(np.float64)
    cos, sin = np.cos(ang).astype(np.float32), np.sin(ang).astype(np.float32)
    return jnp.asarray(np.concatenate([cos, cos], axis=1)), jnp.asarray(np.concatenate([-sin, sin], axis=1))


def _ret_chunk(qs, ks, vs, rs, cos, sin, dintra, kw, qw, g_chunk):
    outs, rn = [], []
    for h in range(NH):
        q = qs[h] * cos + _swap_halves(qs[h]) * sin
        k = (ks[h] * cos + _swap_halves(ks[h]) * sin) * (HD ** -0.5)
        sc = _bdot_nt(q, k) * dintra[h]
        outs.append(_bdot(sc, vs[h]) + _bdot(q * qw[h], rs[h]))
        rn.append(g_chunk[h] * rs[h] + _bdot_tn(k * kw[h], vs[h]))
    return outs, rn


def _heads(x):
    return [x[:, h * HD:(h + 1) * HD] for h in range(NH)]


def _chunks_per_step(n):
    return 4 if n % 4 == 0 else 1


def retention_fwd(zz, cosf, sinf, *, name):
    s = zz.shape[0]
    n = s // CH
    nb = _chunks_per_step(n)
    rb = nb * CH
    dintra, kw, qw, g_chunk = _ret_consts()

    def body(q_ref, k_ref, v_ref, c_ref, s_ref, di_ref, kw_ref, qw_ref, o_ref, rp_ref, r_scr):
        @pl.when(pl.program_id(0) == 0)
        def _():
            r_scr[...] = jnp.zeros_like(r_scr)

        r = r_scr[...]
        consts = ([di_ref[h] for h in range(NH)], [kw_ref[h] for h in range(NH)], [qw_ref[h] for h in range(NH)], g_chunk)
        for i in range(nb):
            rows = slice(i * CH, (i + 1) * CH)
            rp_ref[i] = r
            outs, rn = _ret_chunk(_heads(q_ref[rows, :]), _heads(k_ref[rows, :]), _heads(v_ref[rows, :]),
                                  [r[h * HD:(h + 1) * HD] for h in range(NH)], c_ref[rows, :], s_ref[rows, :], *consts)
            o_ref[rows, :] = jnp.concatenate(outs, axis=1)
            r = jnp.concatenate(rn, axis=0)
        r_scr[...] = r

    col = lambda c: pl.BlockSpec((rb, BW), lambda i: (i, c // BW))
    tab = pl.BlockSpec((rb, HD), lambda i: (i, 0))
    cst = lambda shp: pl.BlockSpec(shp, lambda i: (0,) * len(shp))
    return pl.pallas_call(
        body, grid=(n // nb,),
        in_specs=[col(RQ), col(RK), col(RV), tab, tab, cst((NH, CH, CH)), cst((NH, CH, HD)), cst((NH, CH, HD))],
        out_specs=[pl.BlockSpec((rb, BW), lambda i: (i, 0)), pl.BlockSpec((nb, BW, HD), lambda i: (i, 0, 0))],
        out_shape=[_sds((s, BW), F32), _sds((n, BW, HD), F32)],
        scratch_shapes=[pltpu.VMEM((BW, HD), F32)],
        compiler_params=_cparams(("arbitrary",)), name=name)(zz, zz, zz, cosf, sinf, dintra, kw, qw)


def retention_bwd(zz, cosf, sinf, rprev, do, dzz, *, name):
    s = zz.shape[0]
    n = s // CH
    nb = _chunks_per_step(n)
    rb, steps = nb * CH, n // nb
    dintra, kw, qw, g_chunk = _ret_consts()

    def body(q_ref, k_ref, v_ref, c_ref, s_ref, di_ref, kw_ref, qw_ref, rp_ref, do_ref, dzz_ref, dz_ref, dr_scr):
        @pl.when(pl.program_id(0) == 0)
        def _():
            dr_scr[...] = jnp.zeros_like(dr_scr)

        dr = dr_scr[...]
        consts = dict(dintra=[di_ref[h] for h in range(NH)], kw=[kw_ref[h] for h in range(NH)],
                      qw=[qw_ref[h] for h in range(NH)], g_chunk=g_chunk)
        for i in reversed(range(nb)):
            rows = slice(i * CH, (i + 1) * CH)
            rprev_v = rp_ref[i]
            f = functools.partial(_ret_chunk, cos=c_ref[rows, :], sin=s_ref[rows, :], **consts)
            _, vjp = jax.vjp(f, _heads(q_ref[rows, :]), _heads(k_ref[rows, :]), _heads(v_ref[rows, :]),
                             [rprev_v[h * HD:(h + 1) * HD] for h in range(NH)])
            dq, dk, dv, drp = vjp((_heads(do_ref[rows, :]), [dr[h * HD:(h + 1) * HD] for h in range(NH)]))
            dz_ref[rows, :] = jnp.concatenate(dq + dk + dv, axis=1).astype(dz_ref.dtype)
            dr = jnp.concatenate(drp, axis=0)
        dr_scr[...] = dr

    col = lambda c: pl.BlockSpec((rb, BW), lambda i: (steps - 1 - i, c // BW))
    tab = pl.BlockSpec((rb, HD), lambda i: (steps - 1 - i, 0))
    cst = lambda shp: pl.BlockSpec(shp, lambda i: (0,) * len(shp))
    return pl.pallas_call(
        body, grid=(steps,),
        in_specs=[col(RQ), col(RK), col(RV), tab, tab, cst((NH, CH, CH)), cst((NH, CH, HD)), cst((NH, CH, HD)),
                  pl.BlockSpec((nb, BW, HD), lambda i: (steps - 1 - i, 0, 0)),
                  pl.BlockSpec((rb, BW), lambda i: (steps - 1 - i, 0)), pl.BlockSpec(memory_space=pl.ANY)],
        out_specs=pl.BlockSpec((rb, 3 * BW), lambda i: (steps - 1 - i, RQ // (3 * BW))),
        out_shape=_sds(dzz.shape, dzz.dtype), input_output_aliases={10: 0},
        scratch_shapes=[pltpu.VMEM((BW, HD), F32)],
        compiler_params=_cparams(("arbitrary",)), name=name)(zz, zz, zz, cosf, sinf, dintra, kw, qw, rprev, do, dzz)


GKW = NH * GDK


def _gla_consts():
    tri = np.tril(np.ones((CH, CH), np.float32))
    mask_t = np.zeros((BW, GKW), np.float32)
    for h in range(NH):
        mask_t[h * HD:(h + 1) * HD, h * GDK:(h + 1) * GDK] = 1.0
    return jnp.asarray(tri), jnp.asarray(mask_t)


def _gla_chunk(q, k, v, glr, w_a2, b_a, st, tri, mask_t):
    la = _log_sigmoid(_bdot(glr, w_a2) + b_a) * (1.0 / 16.0)
    bc = _dot(tri, la, HI)
    be = jnp.sum(la, axis=0, keepdims=True)
    kv_t = _bdot_tn(v, k * jnp.exp(be - bc)) * mask_t
    sn = jnp.exp(be) * st + kv_t
    return _bdot_nt(q * (GDK ** -0.5), sn), sn


def gla_fwd(zz, w_a2p, b_a, *, name):
    s = zz.shape[0]
    n = s // CH
    nb = _chunks_per_step(n)
    rb = nb * CH
    tri, mask_t = _gla_consts()

    def body(q_ref, k_ref, v_ref, lr_ref, w_ref, b_ref, tri_ref, m_ref, o_ref, sp_ref, st_scr):
        @pl.when(pl.program_id(0) == 0)
        def _():
            st_scr[...] = jnp.zeros_like(st_scr)

        st = st_scr[...]
        for i in range(nb):
            rows = slice(i * CH, (i + 1) * CH)
            sp_ref[i] = st
            o_ref[rows, :], st = _gla_chunk(q_ref[rows, :], k_ref[rows, :], v_ref[rows, :], lr_ref[rows, :], w_ref[...],
                                            b_ref[...], st, tri_ref[...], m_ref[...])
        st_scr[...] = st

    cst = lambda shp: pl.BlockSpec(shp, lambda i: (0,) * len(shp))
    return pl.pallas_call(
        body, grid=(n // nb,),
        in_specs=[pl.BlockSpec((rb, GKW), lambda i: (i, GQ // GKW)), pl.BlockSpec((rb, GKW), lambda i: (i, GK // GKW)),
                  pl.BlockSpec((rb, BW), lambda i: (i, GV // BW)), pl.BlockSpec((rb, HD), lambda i: (i, LR // HD)),
                  cst((HD, GKW)), cst((1, GKW)), cst((CH, CH)), cst((BW, GKW))],
        out_specs=[pl.BlockSpec((rb, BW), lambda i: (i, 0)), pl.BlockSpec((nb, BW, GKW), lambda i: (i, 0, 0))],
        out_shape=[_sds((s, BW), F32), _sds((n, BW, GKW), F32)],
        scratch_shapes=[pltpu.VMEM((BW, GKW), F32)],
        compiler_params=_cparams(("arbitrary",)), name=name)(zz, zz, zz, zz, w_a2p, b_a, tri, mask_t)


def gla_bwd(zz, w_a2p, b_a, sprev, do, dzz, *, name):
    s = zz.shape[0]
    n = s // CH
    nb = _chunks_per_step(n)
    rb, steps = nb * CH, n // nb
    tri, mask_t = _gla_consts()

    def body(q_ref, k_ref, v_ref, lr_ref, w_ref, b_ref, tri_ref, m_ref, sp_ref, do_ref, dzz_ref,
             dz_ref, dlr_ref, dw_ref, db_ref, ds_scr):
        @pl.when(pl.program_id(0) == 0)
        def _():
            ds_scr[...] = jnp.zeros_like(ds_scr)
            dw_ref[...] = jnp.zeros_like(dw_ref)
            db_ref[...] = jnp.zeros_like(db_ref)

        f = functools.partial(_gla_chunk, tri=tri_ref[...], mask_t=m_ref[...])
        ds, dw_sum, db_sum = ds_scr[...], jnp.zeros(dw_ref.shape, F32), jnp.zeros(db_ref.shape, F32)
        for i in reversed(range(nb)):
            rows = slice(i * CH, (i + 1) * CH)
            _, vjp = jax.vjp(f, q_ref[rows, :], k_ref[rows, :], v_ref[rows, :], lr_ref[rows, :], w_ref[...], b_ref[...],
                             sp_ref[i])
            dq, dk, dv, dlr, dw, db, ds = vjp((do_ref[rows, :], ds))
            dz_ref[rows, :] = jnp.concatenate([dq, dk, dv], axis=1).astype(dz_ref.dtype)
            dlr_ref[rows, :] = dlr.astype(dlr_ref.dtype)
            dw_sum, db_sum = dw_sum + dw, db_sum + db
        dw_ref[...] += dw_sum
        db_ref[...] += db_sum
        ds_scr[...] = ds

    cst = lambda shp: pl.BlockSpec(shp, lambda i: (0,) * len(shp))
    r = lambda i: steps - 1 - i
    return pl.pallas_call(
        body, grid=(steps,),
        in_specs=[pl.BlockSpec((rb, GKW), lambda i: (r(i), GQ // GKW)), pl.BlockSpec((rb, GKW), lambda i: (r(i), GK // GKW)),
                  pl.BlockSpec((rb, BW), lambda i: (r(i), GV // BW)), pl.BlockSpec((rb, HD), lambda i: (r(i), LR // HD)),
                  cst((HD, GKW)), cst((1, GKW)), cst((CH, CH)), cst((BW, GKW)),
                  pl.BlockSpec((nb, BW, GKW), lambda i: (r(i), 0, 0)), pl.BlockSpec((rb, BW), lambda i: (r(i), 0)),
                  pl.BlockSpec(memory_space=pl.ANY)],
        out_specs=[pl.BlockSpec((rb, 2 * GKW + BW), lambda i: (r(i), GQ // (2 * GKW + BW))),
                   pl.BlockSpec((rb, HD), lambda i: (r(i), 0)), cst((HD, GKW)), cst((1, GKW))],
        out_shape=[_sds(dzz.shape, dzz.dtype), _sds((s, HD), MMT), _sds((HD, GKW), F32), _sds((1, GKW), F32)],
        input_output_aliases={10: 0}, scratch_shapes=[pltpu.VMEM((BW, GKW), F32)],
        compiler_params=_cparams(("arbitrary",)), name=name)(zz, zz, zz, zz, w_a2p, b_a, tri, mask_t, sprev, do, dzz)


def _fox_pre_f(fqs, fks, ff, gq, gk, bf):
    def rms(x, g):
        return x * lax.rsqrt(jnp.mean(x * x, axis=-1, keepdims=True) + EPS) * g

    qn = [rms(x, gq) * (HD ** -0.5) for x in fqs]
    kn = [rms(x, gk) for x in fks]
    return qn, kn, _log_sigmoid(ff + bf)


def fox_pre(zz, gq, gk, bf, *, name):
    s = zz.shape[0]
    t = _row_tile(s)
    tri = jnp.asarray(np.tril(np.ones((t, t), np.float32)))

    def body(q_ref, k_ref, f_ref, gq_ref, gk_ref, b_ref, tri_ref, qn_ref, kn_ref, cum_ref, carry):
        @pl.when(pl.program_id(0) == 0)
        def _():
            carry[...] = jnp.zeros_like(carry)

        qn, kn, lf = _fox_pre_f(_heads(q_ref[...]), _heads(k_ref[...]), f_ref[...], gq_ref[...], gk_ref[...], b_ref[...])
        qn_ref[...] = jnp.concatenate(qn, axis=1).astype(qn_ref.dtype)
        kn_ref[...] = jnp.concatenate(kn, axis=1).astype(kn_ref.dtype)
        cum_ref[...] = _dot(tri_ref[...], lf, HI) + carry[...]
        carry[...] += jnp.sum(lf, axis=0, keepdims=True)

    vec = pl.BlockSpec((1, HD), lambda i: (0, 0))
    return pl.pallas_call(
        body, grid=(s // t,),
        in_specs=[pl.BlockSpec((t, BW), lambda i: (i, FQ // BW)), pl.BlockSpec((t, BW), lambda i: (i, FK // BW)),
                  pl.BlockSpec((t, HD), lambda i: (i, FF // HD)), vec, vec, vec, pl.BlockSpec((t, t), lambda i: (0, 0))],
        out_specs=[pl.BlockSpec((t, BW), lambda i: (i, 0)), pl.BlockSpec((t, BW), lambda i: (i, 0)),
                   pl.BlockSpec((t, HD), lambda i: (i, 0))],
        out_shape=[_sds((s, BW), MMT), _sds((s, BW), MMT), _sds((s, HD), F32)],
        scratch_shapes=[pltpu.VMEM((1, HD), F32)],
        compiler_params=_cparams(("arbitrary",)), name=name)(zz, zz, zz, gq, gk, bf, tri)


def fox_pre_bwd(zz, gq, gk, bf, dqn, dkn, dcum, dzz, *, name):
    s = zz.shape[0]
    t = _row_tile(s)
    nt = s // t
    triu = jnp.asarray(np.triu(np.ones((t, t), np.float32)))

    def body(q_ref, k_ref, f_ref, gq_ref, gk_ref, b_ref, tri_ref, dqn_ref, dkn_ref, dcum_ref, dzz_ref,
             dz_ref, dff_ref, dgq_ref, dgk_ref, db_ref, carry):
        @pl.when(pl.program_id(0) == 0)
        def _():
            carry[...] = jnp.zeros_like(carry)
            dgq_ref[...] = jnp.zeros_like(dgq_ref)
            dgk_ref[...] = jnp.zeros_like(dgk_ref)
            db_ref[...] = jnp.zeros_like(db_ref)

        dcum_v = dcum_ref[...]
        dlf = _dot(tri_ref[...], dcum_v, HI) + carry[...]
        carry[...] += jnp.sum(dcum_v, axis=0, keepdims=True)
        _, vjp = jax.vjp(_fox_pre_f, _heads(q_ref[...]), _heads(k_ref[...]), f_ref[...], gq_ref[...], gk_ref[...], b_ref[...])
        dq, dk, dff, dgq, dgk, db = vjp((_heads(dqn_ref[...]), _heads(dkn_ref[...]), dlf))
        dz_ref[...] = jnp.concatenate(dq + dk, axis=1).astype(dz_ref.dtype)
        dff_ref[...] = dff.astype(dff_ref.dtype)
        dgq_ref[...] += dgq
        dgk_ref[...] += dgk
        db_ref[...] += db

    r = lambda i: nt - 1 - i
    vec = pl.BlockSpec((1, HD), lambda i: (0, 0))
    return pl.pallas_call(
        body, grid=(nt,),
        in_specs=[pl.BlockSpec((t, BW), lambda i: (r(i), FQ // BW)), pl.BlockSpec((t, BW), lambda i: (r(i), FK // BW)),
                  pl.BlockSpec((t, HD), lambda i: (r(i), FF // HD)), vec, vec, vec, pl.BlockSpec((t, t), lambda i: (0, 0)),
                  pl.BlockSpec((t, BW), lambda i: (r(i), 0)), pl.BlockSpec((t, BW), lambda i: (r(i), 0)),
                  pl.BlockSpec((t, HD), lambda i: (r(i), 0)), pl.BlockSpec(memory_space=pl.ANY)],
        out_specs=[pl.BlockSpec((t, 2 * BW), lambda i: (r(i), FQ // (2 * BW))), pl.BlockSpec((t, HD), lambda i: (r(i), 0)),
                   vec, vec, vec],
        out_shape=[_sds(dzz.shape, dzz.dtype), _sds((s, HD), MMT), _sds((1, HD), F32), _sds((1, HD), F32), _sds((1, HD), F32)],
        input_output_aliases={10: 0}, scratch_shapes=[pltpu.VMEM((1, HD), F32)],
        compiler_params=_cparams(("arbitrary",)), name=name)(zz, zz, zz, gq, gk, bf, triu, dqn, dkn, dcum, dzz)


def _fox_blocks(s):
    return min(256, s), min(512, s)


NEG = -1e30


def fox_attn_fwd(qn, kn, zz, cum_col, cum_row, *, name):
    s = qn.shape[0]
    bq, bk = _fox_blocks(s)

    def body(q_ref, k_ref, v_ref, cc_ref, cr_ref, o_ref, lse_ref):
        qi = pl.program_id(1)
        q = q_ref[...]
        cq = cc_ref[...]
        rows = qi * bq + lax.broadcasted_iota(jnp.int32, (bq, bk), 0)
        cols0 = lax.broadcasted_iota(jnp.int32, (bq, bk), 1)

        def step(j, carry, on_diagonal):
            m, l, acc = carry
            off = pl.multiple_of(j * bk, bk)
            k = k_ref[pl.ds(off, bk), :]
            v = v_ref[pl.ds(off, bk), :].astype(MMT)
            sc = _dot_nt(q, k) + cq - cr_ref[pl.ds(j, 1), :]
            if on_diagonal:
                sc = jnp.where(rows >= cols0 + j * bk, sc, NEG)
            m_new = jnp.maximum(m, jnp.max(sc, axis=1, keepdims=True))
            alpha = jnp.exp(m - m_new)
            p = jnp.exp(sc - m_new)
            return m_new, alpha * l + jnp.sum(p, axis=1, keepdims=True), alpha * acc + _dot(p.astype(MMT), v)

        nfull, nk = (qi * bq + 1) // bk, ((qi + 1) * bq + bk - 1) // bk
        carry = (jnp.full((bq, 1), NEG, F32), jnp.zeros((bq, 1), F32), jnp.zeros((bq, HD), F32))
        carry = lax.fori_loop(0, nfull, functools.partial(step, on_diagonal=False), carry)
        m, l, acc = lax.fori_loop(nfull, nk, functools.partial(step, on_diagonal=True), carry)
        o_ref[...] = acc / l
        lse_ref[...] = m + jnp.log(l)

    return pl.pallas_call(
        body, grid=(NH, s // bq),
        in_specs=[pl.BlockSpec((bq, HD), lambda h, i: (i, h)), pl.BlockSpec((s, HD), lambda h, i: (0, h)),
                  pl.BlockSpec((s, HD), lambda h, i: (0, FV // HD + h)),
                  pl.BlockSpec((None, bq, 1), lambda h, i: (h, i, 0)), pl.BlockSpec((None, s // bk, bk), lambda h, i: (h, 0, 0))],
        out_specs=[pl.BlockSpec((bq, HD), lambda h, i: (i, h)), pl.BlockSpec((None, bq, 1), lambda h, i: (h, i, 0))],
        out_shape=[_sds((s, BW), F32), _sds((NH, s, 1), F32)],
        compiler_params=_cparams(("parallel", "parallel")), name=name)(qn, kn, zz, cum_col, cum_row)


def fox_attn_bwd(qn, kn, zz, cum_col, cum_row, lse, do, dzz, *, name):
    s = qn.shape[0]
    bq, bk = _fox_blocks(s)
    nkc = s // bk

    def body(q_ref, k_ref, v_ref, cc_ref, cr_ref, lse_ref, do_ref, dzz_ref, dq_ref, dk_ref, dv_ref, dc_ref,
             p_scr, dp_scr, dv_scr):
        qi = pl.program_id(1)

        @pl.when(qi == 0)
        def _():
            dk_ref[...] = jnp.zeros_like(dk_ref)
            dv_scr[...] = jnp.zeros_like(dv_scr)
            dc_ref[...] = jnp.zeros_like(dc_ref)

        q = q_ref[...]
        dob = do_ref[...].astype(MMT)
        cq = cc_ref[...]
        lse_v = lse_ref[...]
        rows = qi * bq + lax.broadcasted_iota(jnp.int32, (bq, bk), 0)
        cols0 = lax.broadcasted_iota(jnp.int32, (bq, bk), 1)
        nfull, nk = (qi * bq + 1) // bk, ((qi + 1) * bq + bk - 1) // bk

        def probs(j, delta, on_diagonal):
            off = pl.multiple_of(j * bk, bk)
            sc = _dot_nt(q, k_ref[pl.ds(off, bk), :]) + cq - cr_ref[pl.ds(j, 1), :]
            p = jnp.exp(sc - lse_v)
            if on_diagonal:
                p = jnp.where(rows >= cols0 + j * bk, p, 0.0)
            dp = _dot_nt(dob, v_ref[pl.ds(off, bk), :].astype(MMT))
            p_scr[j] = p
            dp_scr[j] = dp
            return delta + jnp.sum(p * dp, axis=1, keepdims=True)

        delta = lax.fori_loop(0, nfull, functools.partial(probs, on_diagonal=False), jnp.zeros((bq, 1), F32))
        delta = lax.fori_loop(nfull, nk, functools.partial(probs, on_diagonal=True), delta)

        def grads(j, dq):
            off = pl.multiple_of(j * bk, bk)
            p = p_scr[j]
            ds = p * (dp_scr[j] - delta)
            dsm = ds.astype(MMT)
            dv_scr[pl.ds(off, bk), :] += _dot_tn(p.astype(MMT), dob)
            dk_ref[pl.ds(off, bk), :] += _dot_tn(dsm, q)
            dc_ref[pl.ds(j, 1), :] -= jnp.sum(ds, axis=0, keepdims=True)
            return dq + _dot(dsm, k_ref[pl.ds(off, bk), :])

        dq_ref[...] = lax.fori_loop(0, nk, grads, jnp.zeros((bq, HD), F32))

        @pl.when(qi == pl.num_programs(1) - 1)
        def _():
            dv_ref[...] = dv_scr[...].astype(dv_ref.dtype)

    full = lambda c0=0: pl.BlockSpec((s, HD), lambda h, i: (0, c0 + h))
    blk = lambda: pl.BlockSpec((bq, HD), lambda h, i: (i, h))
    colv = lambda: pl.BlockSpec((None, bq, 1), lambda h, i: (h, i, 0))
    rowv = lambda: pl.BlockSpec((None, nkc, bk), lambda h, i: (h, 0, 0))
    return pl.pallas_call(
        body, grid=(NH, s // bq),
        in_specs=[blk(), full(), full(FV // HD), colv(), rowv(), colv(), blk(), pl.BlockSpec(memory_space=pl.ANY)],
        out_specs=[blk(), full(), full(FV // HD), rowv()],
        out_shape=[_sds((s, BW), F32), _sds((s, BW), F32), _sds(dzz.shape, dzz.dtype), _sds((NH, nkc, bk), F32)],
        input_output_aliases={7: 2},
        scratch_shapes=[pltpu.VMEM((nkc, bq, bk), F32), pltpu.VMEM((nkc, bq, bk), F32), pltpu.VMEM((s, HD), F32)],
        compiler_params=_cparams(("parallel", "arbitrary")), name=name)(qn, kn, zz, cum_col, cum_row, lse, do, dzz)


def _branch_f(rets, rgs, glas, ggs, ret_g, gla_g):
    out_r, out_g = [], []
    for h in range(NH):
        xc = rets[h] - jnp.mean(rets[h], axis=-1, keepdims=True)
        y = xc * lax.rsqrt(jnp.mean(xc * xc, axis=-1, keepdims=True) + EPS) * ret_g[h]
        out_r.append(_silu(rgs[h]) * y)
        x = glas[h]
        y = x * lax.rsqrt(jnp.mean(x * x, axis=-1, keepdims=True) + EPS) * gla_g
        out_g.append(_silu(ggs[h]) * y)
    return out_r, out_g


def _w_br_spec(layer):
    return pl.BlockSpec((None, 3, BW, D), lambda i: (layer, 0, 0, 0))


def mix_fwd(ret_raw, gla_raw, fox_o, zz, ret_g, gla_g, b_mg, w_br, *, name, layer):
    s = zz.shape[0]
    t = _row_tile(s)

    def body(r_ref, g_ref, f_ref, rg_ref, gg_ref, gp_ref, rgn_ref, ggn_ref, bmg_ref, w_ref, o_ref):
        rgn = rgn_ref[...]
        br_r, br_g = _branch_f(_heads(r_ref[...]), _heads(rg_ref[...]), _heads(g_ref[...]), _heads(gg_ref[...]),
                               _heads(rgn), ggn_ref[...])
        brs = [jnp.concatenate(br_r, axis=1), jnp.concatenate(br_g, axis=1), f_ref[...]]
        acc = jnp.zeros((t, D), F32)
        for b in range(3):
            gate = jax.nn.sigmoid(gp_ref[:, b * D:(b + 1) * D] + bmg_ref[:, b * D:(b + 1) * D])
            acc = acc + gate * _dot(brs[b].astype(MMT), w_ref[b])
        o_ref[...] = acc.astype(o_ref.dtype)

    row = lambda w, c=0: pl.BlockSpec((t, w), lambda i: (i, c // w))
    cst = lambda shp: pl.BlockSpec(shp, lambda i: (0,) * len(shp))
    return pl.pallas_call(
        body, grid=(s // t,),
        in_specs=[row(BW), row(BW), row(BW), row(BW, RG), row(BW, GG), row(3 * D, GP), cst((1, BW)), cst((1, HD)),
                  cst((1, 3 * D)), _w_br_spec(layer)],
        out_specs=row(D), out_shape=_sds((s, D), MMT),
        compiler_params=_cparams(("parallel",)), name=name)(ret_raw, gla_raw, fox_o, zz, zz, zz, ret_g, gla_g, b_mg, w_br)


def mix_bwd(ret_raw, gla_raw, fox_o, zz, ret_g, gla_g, b_mg, w_br, dmi, *, name, layer):
    s = zz.shape[0]
    t = _row_tile(s)

    def body(r_ref, g_ref, f_ref, rg_ref, gg_ref, gp_ref, rgn_ref, ggn_ref, bmg_ref, w_ref, dmi_ref,
             dr_ref, dg_ref, df_ref, dgp_ref, dw_ref, drgn_ref, dggn_ref, dbmg_ref):
        @pl.when(pl.program_id(0) == 0)
        def _():
            dw_ref[...] = jnp.zeros_like(dw_ref)
            drgn_ref[...] = jnp.zeros_like(drgn_ref)
            dggn_ref[...] = jnp.zeros_like(dggn_ref)
            dbmg_ref[...] = jnp.zeros_like(dbmg_ref)

        (br_r, br_g), vjp = jax.vjp(_branch_f, _heads(r_ref[...]), _heads(rg_ref[...]), _heads(g_ref[...]),
                                    _heads(gg_ref[...]), _heads(rgn_ref[...]), ggn_ref[...])
        brs = [jnp.concatenate(br_r, axis=1).astype(MMT), jnp.concatenate(br_g, axis=1).astype(MMT),
               f_ref[...].astype(MMT)]
        dmi_v = dmi_ref[...].astype(F32)
        dbr = []
        for b in range(3):
            w = w_ref[b]
            ybr = _dot(brs[b], w)
            gate = jax.nn.sigmoid(gp_ref[:, b * D:(b + 1) * D] + bmg_ref[:, b * D:(b + 1) * D])
            dgp = dmi_v * ybr * gate * (1.0 - gate)
            dgp_ref[:, b * D:(b + 1) * D] = dgp.astype(dgp_ref.dtype)
            dbmg_ref[:, b * D:(b + 1) * D] += jnp.sum(dgp, axis=0, keepdims=True)
            dy = (dmi_v * gate).astype(MMT)
            dw_ref[b] += _dot_tn(brs[b], dy)
            dbr.append(_dot_nt(dy, w))
        dr, drg, dg, dgg, drgn, dggn = vjp((_heads(dbr[0]), _heads(dbr[1])))
        dr_ref[...] = jnp.concatenate(dr, axis=1)
        dg_ref[...] = jnp.concatenate(dg, axis=1)
        df_ref[...] = dbr[2]
        dgp_ref[:, RG:RG + BW] = jnp.concatenate(drg, axis=1).astype(dgp_ref.dtype)
        dgp_ref[:, GG:GG + BW] = jnp.concatenate(dgg, axis=1).astype(dgp_ref.dtype)
        drgn_ref[...] += jnp.concatenate(drgn, axis=1)
        dggn_ref[...] += dggn

    row = lambda w, c=0: pl.BlockSpec((t, w), lambda i: (i, c // w))
    cst = lambda shp: pl.BlockSpec(shp, lambda i: (0,) * len(shp))
    return pl.pallas_call(
        body, grid=(s // t,),
        in_specs=[row(BW), row(BW), row(BW), row(BW, RG), row(BW, GG), row(3 * D, GP), cst((1, BW)), cst((1, HD)),
                  cst((1, 3 * D)), _w_br_spec(layer), row(D)],
        out_specs=[row(BW), row(BW), row(BW), row(FV), cst((3, BW, D)), cst((1, BW)), cst((1, HD)), cst((1, 3 * D))],
        out_shape=[_sds((s, BW), F32)] * 3 + [_sds((s, NZZ), MMT), _sds((3, BW, D), F32), _sds((1, BW), F32),
                                              _sds((1, HD), F32), _sds((1, 3 * D), F32)],
        compiler_params=_cparams(("arbitrary",)), name=name)(ret_raw, gla_raw, fox_o, zz, zz, zz, ret_g, gla_g, b_mg, w_br, dmi)


CT = 256


def _shift_down(x, k, rows):
    return jnp.where(rows >= k, pltpu.roll(x, k, 0), 0.0)


def _shift_up(x, k, rows, s):
    return jnp.where(rows < s - k, pltpu.roll(x, s - k, 0), 0.0)


def conv_fwd(ug, w_conv, b_conv, *, name):
    s = ug.shape[0]
    nt = DFF // CT

    def body(u_ref, g_ref, w_ref, b_ref, a_ref):
        u = u_ref[...]
        rows = lax.broadcasted_iota(jnp.int32, u.shape, 0)
        uc = b_ref[...] + w_ref[0:1, :] * _shift_down(u, 2, rows) + w_ref[1:2, :] * _shift_down(u, 1, rows) + w_ref[2:3, :] * u
        a_ref[...] = (_silu(uc) * g_ref[...]).astype(a_ref.dtype)

    return pl.pallas_call(
        body, grid=(nt,),
        in_specs=[pl.BlockSpec((s, CT), lambda j: (0, j)), pl.BlockSpec((s, CT), lambda j: (0, nt + j)),
                  pl.BlockSpec((3, CT), lambda j: (0, j)), pl.BlockSpec((1, CT), lambda j: (0, j))],
        out_specs=pl.BlockSpec((s, CT), lambda j: (0, j)), out_shape=_sds((s, DFF), MMT),
        compiler_params=_cparams(("parallel",)), name=name)(ug, ug, w_conv, b_conv)


def conv_bwd(ug, w_conv, b_conv, da, *, name):
    s = ug.shape[0]
    nt = DFF // CT

    def body(u_ref, g_ref, w_ref, b_ref, da_ref, du_ref, dg_ref, dw_ref, db_ref):
        u = u_ref[...]
        rows = lax.broadcasted_iota(jnp.int32, u.shape, 0)
        u2, u1 = _shift_down(u, 2, rows), _shift_down(u, 1, rows)
        uc = b_ref[...] + w_ref[0:1, :] * u2 + w_ref[1:2, :] * u1 + w_ref[2:3, :] * u
        sg = jax.nn.sigmoid(uc)
        da_v = da_ref[...]
        dg_ref[...] = (da_v * uc * sg).astype(dg_ref.dtype)
        duc = da_v * g_ref[...] * sg * (1.0 + uc * (1.0 - sg))
        du = w_ref[2:3, :] * duc + w_ref[1:2, :] * _shift_up(duc, 1, rows, s) + w_ref[0:1, :] * _shift_up(duc, 2, rows, s)
        du_ref[...] = du.astype(du_ref.dtype)
        dw_ref[0:1, :] = jnp.sum(duc * u2, axis=0, keepdims=True)
        dw_ref[1:2, :] = jnp.sum(duc * u1, axis=0, keepdims=True)
        dw_ref[2:3, :] = jnp.sum(duc * u, axis=0, keepdims=True)
        db_ref[...] = jnp.sum(duc, axis=0, keepdims=True)

    col = lambda: pl.BlockSpec((s, CT), lambda j: (0, j))
    return pl.pallas_call(
        body, grid=(nt,),
        in_specs=[col(), pl.BlockSpec((s, CT), lambda j: (0, nt + j)), pl.BlockSpec((3, CT), lambda j: (0, j)),
                  pl.BlockSpec((1, CT), lambda j: (0, j)), col()],
        out_specs=[col(), col(), pl.BlockSpec((3, CT), lambda j: (0, j)), pl.BlockSpec((1, CT), lambda j: (0, j))],
        out_shape=[_sds((s, DFF), MMT), _sds((s, DFF), MMT), _sds((3, DFF), F32), _sds((1, DFF), F32)],
        compiler_params=_cparams(("parallel",)), name=name)(ug, ug, w_conv, b_conv, da)


def place_tail(dzz, dlr, dff, *, name):
    s = dzz.shape[0]
    t = _row_tile(s)

    def body(a_ref, b_ref, z_ref, o_ref):
        o_ref[...] = jnp.concatenate([a_ref[...], b_ref[...]], axis=1)

    spec = pl.BlockSpec((t, HD), lambda i: (i, 0))
    return pl.pallas_call(
        body, grid=(s // t,), in_specs=[spec, spec, pl.BlockSpec(memory_space=pl.ANY)],
        out_specs=pl.BlockSpec((t, 2 * HD), lambda i: (i, LR // (2 * HD))), out_shape=_sds(dzz.shape, dzz.dtype),
        input_output_aliases={2: 0}, compiler_params=_cparams(("parallel",)), name=name)(dlr, dff, dzz)


def _tiles(s):
    return min(1024, s)


def layer_fwd(x, mod, p, cosf, sinf):
    s = x.shape[0]
    tm = _tiles(s)
    l = p["l"]
    shift1, scale1, gate1, shift2, scale2, gate2 = mod
    h = norm_mod(x, p["norm1_g"], scale1, shift1, name="norm_mod")
    zz = mm_nn(h, p["w1"], tm=tm, tn=768, out_dtype=F32, name="mm_w1", layer=l)
    ret_raw, rprev = retention_fwd(zz, cosf, sinf, name="ret_fwd")
    gla_raw, sprev = gla_fwd(zz, p["w_a2p"], p["b_gla_a"], name="gla_fwd")
    qn, kn, cum = fox_pre(zz, p["q_norm_g"], p["k_norm_g"], p["b_foxp"], name="fox_pre")
    bq, bk = _fox_blocks(s)
    cum_t = cum[:, :NH].T
    cum_col, cum_row = cum_t[:, :, None], cum_t.reshape(NH, s // bk, bk)
    fox_o, lse = fox_attn_fwd(qn, kn, zz, cum_col, cum_row, name="fox_fwd")
    if "later" in p:
        p = {**p, **p["later"](fox_o)}
        if "token" in p:
            gate1 = gate1 + p["token"][0, 0]
    mi = mix_fwd(ret_raw, gla_raw, fox_o, zz, p["ret_norm_g"], p["gla_norm_g"], p["b_mg"], p["w_br"], name="mix_fwd",
                 layer=l)
    x1, mixed = mm_nn_residual(mi, p["w_o"], x, gate1, tm=tm, tn=512, name="mm_wo", layer=l)
    h2 = norm_mod(x1, p["norm2_g"], scale2, shift2, name="norm_mod")
    ug = mm_nn(h2, p["w_up"], tm=tm, tn=512, out_dtype=F32, name="mm_wup", layer=l)
    a = conv_fwd(ug, p["w_conv"], p["b_conv"], name="conv_fwd")
    x2, y = mm_nn_residual(a, p["w_down"], x1, gate2, tm=tm, tn=512, name="mm_wdown", layer=l)
    saved = dict(x=x, h=h, zz=zz, ret_raw=ret_raw, rprev=rprev, gla_raw=gla_raw, sprev=sprev, qn=qn, kn=kn,
                 cum_col=cum_col, cum_row=cum_row, fox_o=fox_o, lse=lse, mi=mi, mixed=mixed, x1=x1, h2=h2, ug=ug, a=a, y=y)
    return x2, saved, p


def layer_bwd(dx2, mod, p, sv, cosf, sinf, stacks, slot):
    s = dx2.shape[0]
    tm = _tiles(s)
    l = p["l"]
    shift1, scale1, gate1, shift2, scale2, gate2 = mod
    g, stacks = {}, dict(stacks)
    dy, dgate2 = gate_bwd(dx2, sv["y"], gate2, name="gate_bwd")
    stacks["w_down"] = mm_tn(sv["a"], dy, tm=min(1408, DFF), tn=512, out_dtype=MMT, name="mm_dwdown",
                             stack=stacks["w_down"], layer=slot)
    da = mm_nt(dy, p["w_down"], tm=tm, tn=1408, out_dtype=F32, name="mm_da", layer=l)
    du, dg, g["w_conv"], g["b_conv"] = conv_bwd(sv["ug"], p["w_conv"], p["b_conv"], da, name="conv_bwd")
    stacks["w_up"] = mm_tn(sv["h2"], du, tm=D, tn=CT, out_dtype=MMT, name="mm_dwup_u", stack=stacks["w_up"], layer=slot)
    stacks["w_up"] = mm_tn(sv["h2"], dg, tm=D, tn=CT, out_dtype=MMT, name="mm_dwup_g", stack=stacks["w_up"], layer=slot,
                           out_col0=DFF)
    dh2 = mm_nt2(du, dg, p["w_up"], tm=min(512, s), tn=D, name="mm_dh2", layer=l)
    dx1, g["norm2_g"], dscale2, dshift2 = norm_mod_bwd(sv["x1"], dh2, dx2, p["norm2_g"], scale2, shift2, name="norm_mod_bwd")
    dmixed, dgate1 = gate_bwd(dx1, sv["mixed"], gate1, name="gate_bwd")
    stacks["w_o"] = mm_tn(sv["mi"], dmixed, tm=512, tn=512, out_dtype=MMT, name="mm_dwo", stack=stacks["w_o"], layer=slot)
    dmi = mm_nt(dmixed, p["w_o"], tm=tm, tn=512, out_dtype=MMT, name="mm_dmi", layer=l)
    zz = sv["zz"]
    (dret, dgla, dfox, dzz, g["w_br"], g["ret_norm_g"], g["gla_norm_g"], g["b_mg"]) = mix_bwd(
        sv["ret_raw"], sv["gla_raw"], sv["fox_o"], zz, p["ret_norm_g"], p["gla_norm_g"], p["b_mg"], p["w_br"], dmi,
        name="mix_bwd", layer=l)
    dqn, dkn, dzz, dcum_row = fox_attn_bwd(sv["qn"], sv["kn"], zz, sv["cum_col"], sv["cum_row"], sv["lse"], dfox, dzz,
                                           name="fox_bwd")
    dcum = jnp.pad(dcum_row.reshape(NH, s).T, ((0, 0), (0, HD - NH)))
    dzz, dff, g["q_norm_g"], g["k_norm_g"], g["b_foxp"] = fox_pre_bwd(
        zz, p["q_norm_g"], p["k_norm_g"], p["b_foxp"], dqn, dkn, dcum, dzz, name="fox_pre_bwd")
    dzz, dlr, g["w_a2p"], g["b_gla_a"] = gla_bwd(zz, p["w_a2p"], p["b_gla_a"], sv["sprev"], dgla, dzz, name="gla_bwd")
    dzz = retention_bwd(zz, cosf, sinf, sv["rprev"], dret, dzz, name="ret_bwd")
    dzz = place_tail(dzz, dlr, dff, name="place_tail")
    stacks["w_mg"] = mm_tn(sv["h"], dzz, tm=512, tn=768, out_dtype=MMT, name="mm_dwmg", ncols=WZ0, stack=stacks["w_mg"],
                           layer=slot)
    dwz = mm_tn(sv["h"], dzz, tm=512, tn=768, out_dtype=MMT, name="mm_dwz", col0=WZ0)
    stacks["w_in"] = unalign_dw_in(dwz, stacks["w_in"], slot)
    dh = mm_nt(dzz, p["w1"], tm=min(256, s), tn=D, out_dtype=F32, name="mm_dh", layer=l)
    dx, g["norm1_g"], dscale1, dshift1 = norm_mod_bwd(sv["x"], dh, dx1, p["norm1_g"], scale1, shift1, name="norm_mod_bwd")
    dmod = jnp.concatenate([dshift1, dscale1, dgate1, dshift2, dscale2, dgate2], axis=1)
    return dx, g, dmod, stacks


def _align_cols(w_in, w_mg):
    z = lambda n: jnp.zeros((w_in.shape[0], n), w_in.dtype)
    seg = lambda name: w_in[:, W_IN_COLS[name][0]:W_IN_COLS[name][1]]
    return jnp.concatenate([w_mg, seg("rg"), seg("gg"), seg("fv"), seg("rqkv"), seg("gqkv"), seg("fqk"), seg("lr"),
                            z(HD - GLR), seg("ff"), z(HD - NH)], axis=1)


def _unalign_cols(dwz):
    seg = lambda c0, name: dwz[:, c0 - WZ0:c0 - WZ0 + W_IN_COLS[name][1] - W_IN_COLS[name][0]]
    return jnp.concatenate([seg(RQ, "rqkv"), seg(RG, "rg"), seg(GQ, "gqkv"), seg(LR, "lr"), seg(GG, "gg"), seg(FQ, "fqk"),
                            seg(FV, "fv"), seg(FF, "ff")], axis=1)


def build_w1(w_in_sh, w_mg):
    nl = w_mg.shape[0]
    t = _row_tile(D)

    def body(s_ref, g_ref, o_ref):
        o_ref[...] = _align_cols(jnp.concatenate([s_ref[k] for k in range(4)], axis=1), g_ref[...])

    return pl.pallas_call(
        body, grid=(nl, D // t),
        in_specs=[pl.BlockSpec((None, 4, t, IN_W // 4), lambda l, i: (l, 0, i, 0)), pl.BlockSpec((None, t, WZ0), lambda l, i: (l, i, 0))],
        out_specs=pl.BlockSpec((None, t, NZZ), lambda l, i: (l, i, 0)), out_shape=_sds((nl, D, NZZ), w_mg.dtype),
        compiler_params=_cparams(("parallel", "parallel")), name="build_w1")(w_in_sh, w_mg)


def unalign_dw_in(dwz, stack, layer):
    t = _row_tile(D)

    def body(z_ref, s_ref, o_ref):
        w = _unalign_cols(z_ref[...])
        for k in range(4):
            o_ref[k] = w[:, k * (IN_W // 4):(k + 1) * (IN_W // 4)]

    return pl.pallas_call(
        body, grid=(D // t,),
        in_specs=[pl.BlockSpec((t, NZZ - WZ0), lambda i: (i, 0)), pl.BlockSpec(memory_space=pl.ANY)],
        out_specs=pl.BlockSpec((None, 4, t, IN_W // 4), lambda i: (layer, 0, i, 0)), out_shape=_sds(stack.shape, stack.dtype),
        input_output_aliases={1: 0}, compiler_params=_cparams(("parallel",)), name="unalign_dw_in")(dwz, stack)


LATE_WEIGHTS = ("w_br", "w_o", "w_up", "w_down")


def layer_params(w, big, l, later=None):
    row = lambda v: v[l][None, :]
    p = dict(
        l=0, norm1_g=row(w["norm1_g"]), norm2_g=row(w["norm2_g"]), w1=big["w1"],
        w_a2p=jnp.pad(w["w_gla_a2"][l], ((0, HD - GLR), (0, 0))), b_gla_a=row(w["b_gla_a"]),
        b_foxp=jnp.pad(row(w["b_fox_f"]), ((0, 0), (0, HD - NH))), ret_norm_g=row(w["ret_norm_g"]),
        gla_norm_g=row(w["gla_norm_g"]), q_norm_g=row(w["q_norm_g"]), k_norm_g=row(w["k_norm_g"]),
        b_mg=row(w["b_mg"]), w_conv=w["w_conv"][l], b_conv=row(w["b_conv"]))
    if later is None:
        p.update({n: big[n] for n in LATE_WEIGHTS})
    else:
        p["later"] = later
    return p


def layer_grads(g):
    vec = lambda v: v[0]
    return dict(
        norm1_g=vec(g["norm1_g"]), norm2_g=vec(g["norm2_g"]), w_gla_a2=g["w_a2p"][:GLR], b_gla_a=vec(g["b_gla_a"]),
        b_fox_f=g["b_foxp"][0, :NH], ret_norm_g=vec(g["ret_norm_g"]), gla_norm_g=vec(g["gla_norm_g"]),
        q_norm_g=vec(g["q_norm_g"]), k_norm_g=vec(g["k_norm_g"]), w_br=g["w_br"], b_mg=vec(g["b_mg"]),
        w_conv=g["w_conv"], b_conv=vec(g["b_conv"]))


def ada_mod(c_all, w_ada, b_ada):
    nl, _, n = w_ada.shape

    def body(c_ref, w_ref, b_ref, o_ref):
        o_ref[...] = _dot(_silu(c_ref[...]), w_ref[...], HI) + b_ref[...]

    return pl.pallas_call(
        body, grid=(nl,),
        in_specs=[pl.BlockSpec((8, D), lambda l: (0, 0)), pl.BlockSpec((None, D, n), lambda l: (l, 0, 0)),
                  pl.BlockSpec((None, 1, n), lambda l: (l, 0, 0))],
        out_specs=pl.BlockSpec((None, 8, n), lambda l: (l, 0, 0)), out_shape=_sds((nl, 8, n), F32),
        compiler_params=_cparams(("parallel",)), name="ada_mod")(c_all, w_ada, b_ada)


def ada_dw(c_all, dmod):
    nl, _, n = dmod.shape

    def body(c_ref, d_ref, o_ref):
        o_ref[...] = _dot_tn(_silu(c_ref[...]), d_ref[...], HI)

    return pl.pallas_call(
        body, grid=(nl,),
        in_specs=[pl.BlockSpec((8, D), lambda l: (0, 0)), pl.BlockSpec((None, 8, n), lambda l: (l, 0, 0))],
        out_specs=pl.BlockSpec((None, D, n), lambda l: (l, 0, 0)), out_shape=_sds((nl, D, n), F32),
        compiler_params=_cparams(("parallel",)), name="ada_dw")(c_all, dmod)


def sum_devices(g):
    def body(g_ref, o_ref):
        acc = g_ref[0]
        for d in range(1, 8):
            acc = acc + g_ref[d]
        o_ref[...] = acc

    return pl.pallas_call(body, out_shape=_sds(g.shape[1:], F32), name="sum_devices")(g)


def adamw(w, g, m, v, *, block, name, rows=None, into=None, with_grad=False):
    nd = w.ndim
    lo, hi = (0, w.shape[0]) if rows is None else rows
    grid = ((hi - lo) // block[0],) + tuple(w.shape[i] // block[i] for i in range(1, nd))
    first = lo // block[0]
    nout = 4 if with_grad else 3
    bc1 = 1.0 - ADAM_B1 ** ADAM_STEP
    bc2 = 1.0 - ADAM_B2 ** ADAM_STEP

    def body(w_ref, g_ref, m_ref, v_ref, *rest):
        d_ref, nm_ref, nv_ref = rest[-nout:][:3]
        gv = g_ref[...]
        if with_grad:
            rest[-1][...] = gv
        nm = ADAM_B1 * m_ref[...] + (1.0 - ADAM_B1) * gv
        nv = ADAM_B2 * v_ref[...] + (1.0 - ADAM_B2) * (gv * gv)
        nm_ref[...] = nm
        nv_ref[...] = nv
        d_ref[...] = -ADAM_LR * ((nm / bc1) / (jnp.sqrt(nv / bc2) + ADAM_EPS) + ADAM_WD * w_ref[...])

    spec = pl.BlockSpec(tuple(block), lambda i, *j: (first + i,) + j)
    given = [] if into is None else list(into)
    return pl.pallas_call(
        body, grid=grid, in_specs=[spec] * 4 + [pl.BlockSpec(memory_space=pl.ANY)] * len(given), out_specs=[spec] * nout,
        out_shape=[_sds(w.shape, F32)] * nout, input_output_aliases={4 + i: i for i in range(len(given))},
        compiler_params=_cparams(("parallel",) * nd), name=name)(w, g, m, v, *given)


MESH = pl.DeviceIdType.MESH
ANY = pl.BlockSpec(memory_space=pl.ANY)
VM = pl.BlockSpec(memory_space=pltpu.VMEM)


def _place():
    x, y, c = lax.axis_index("x"), lax.axis_index("y"), lax.axis_index("c")
    return x, y, c, [(1 - x, y), (x, 1 - y), (1 - x, 1 - y)]


def small_allgather(v, *, name):
    m_per, n = v.shape

    def body(x_ref, out_ref, send_sems, recv_sems, local_sem):
        x, y, c, chips = _place()
        me, sibling = (x, y, c), (x, y, 1 - c)

        def rows(px, py, pc):
            return out_ref.at[pl.ds((4 * px + 2 * py + pc) * m_per, m_per), :]

        def copy(k, block, to, src=None):
            return pltpu.make_async_remote_copy(
                src_ref=rows(*block) if src is None else src, dst_ref=rows(*block),
                send_sem=send_sems.at[k], recv_sem=recv_sems.at[k], device_id=to, device_id_type=MESH)

        mine = pltpu.make_async_copy(x_ref, rows(*me), local_sem)
        mine.start()
        first = [copy(0, me, sibling, src=x_ref)]
        first += [copy(1 + j, me, (*chip, c), src=x_ref) for j, chip in enumerate(chips)]
        for cp in first:
            cp.start()
        passed = [copy(4 + j, (*chip, c), sibling) for j, chip in enumerate(chips)]
        for j, chip in enumerate(chips):
            copy(1 + j, (*chip, c), me).wait_recv()
            passed[j].start()
        copy(0, sibling, me).wait_recv()
        for j, chip in enumerate(chips):
            copy(4 + j, (*chip, 1 - c), me).wait_recv()
        for cp in first + passed:
            cp.wait_send()
        mine.wait()

    return pl.pallas_call(
        body, out_shape=_sds((8 * m_per, n), v.dtype), in_specs=[VM], out_specs=VM,
        scratch_shapes=[pltpu.SemaphoreType.DMA((7,)), pltpu.SemaphoreType.DMA((7,)), pltpu.SemaphoreType.DMA],
        name=name)(v)


TENSORS = {
    "w_in": ("lead", None, (4, D, 1285), (1, D, 1285)),
    "w_mg": ("col", 768, (D, 3072), (512, 3072)),
    "w_br": ("col", 256, (3, BW, D), (3, BW, D)),
    "w_o": ("row", 256, (D, D), (D, D)),
    "w_up": ("col", 1408, (D, 5632), (256, 5632)),
    "w_down": ("row", 704, (DFF, D), (704, D)),
}
BIG = tuple(TENSORS)


def _shard_shape(name):
    kind, width, full, _ = TENSORS[name]
    if kind == "lead":
        return full[1:]
    return full[:-1] + (width,) if kind == "col" else (width,) + full[1:]


def _shard_view(ref, layers, name, k):
    kind, width, full, _ = TENSORS[name]
    if kind == "lead":
        return ref.at[layers, k]
    if kind == "row":
        return ref.at[layers, pl.ds(k * width, width)]
    return ref.at[(layers,) + (slice(None),) * (len(full) - 1) + (pl.ds(k * width, width),)]


def _remote(send_sems, recv_sems, k, src, dst, to):
    return pltpu.make_async_remote_copy(src_ref=src, dst_ref=dst, send_sem=send_sems.at[k], recv_sem=recv_sems.at[k],
                                        device_id=to, device_id_type=MESH)


def _dma_sems(n):
    return [pltpu.SemaphoreType.DMA((n,)), pltpu.SemaphoreType.DMA((n,))]


RS_GROUP = 2
HBM = pl.BlockSpec(memory_space=pltpu.HBM)
SEM = pl.BlockSpec(memory_space=pltpu.SEMAPHORE)
SPLIT_CALL = dict(compiler_params=pltpu.CompilerParams(has_side_effects=pltpu.SideEffectType.DATAFLOW_SIDE_EFFECTING))
PULL_SET = (("w_mg", "w_up", "w_o"), ("w_in", "w_br", "w_down"))
FIRST_NEEDED = ("w_in", "w_mg")
NEEDED_LATER = tuple(n for n in BIG if n not in FIRST_NEEDED)


def _in_hbm(a):
    return pltpu.with_memory_space_constraint(a, pltpu.HBM)


def _pull_sends(send_sems, recv_sems, p, o, layer, core, x, y, chips, names=BIG):
    return [_remote(send_sems, recv_sems, 3 * BIG.index(n) + j, p[n].at[layer], _shard_view(o[n], 0, n, 2 * x + y), (*chip, core))
            for n in PULL_SET[core] if n in names for j, chip in enumerate(chips)]


def _pull_arrivals(send_sems, recv_sems, o, core, x, y, chips, to, names=BIG):
    views = [(3 * BIG.index(n) + j, _shard_view(o[n], 0, n, 2 * chip[0] + chip[1]))
             for n in PULL_SET[core] if n in names for j, chip in enumerate(chips)]
    return [_remote(send_sems, recv_sems, k, v, v, to) for k, v in views]


def gather_start(shards, layer, *, name, after=None, names=BIG, lands=None):
    nt = len(BIG)
    first = [] if after is None else [after]

    def body(*refs):
        p, o = dict(zip(BIG, refs[:nt])), dict(zip(BIG, refs[nt:2 * nt]))
        ss, rs = refs[2 * nt + len(first)], refs[2 * nt + len(first) + 1]
        x, y, c, chips = _place()
        for core in (0, 1):
            @pl.when(c == core)
            def _():
                for cp in _pull_sends(ss, rs, p, o, layer, core, x, y, chips, names):
                    cp.start()
        refs[-1][...] = jnp.zeros_like(refs[-1])

    if lands is None:
        lands = [lax.empty((1,) + TENSORS[n][2], shards[n].dtype) for n in BIG]
    lands = [_in_hbm(a) for a in lands]
    outs = pl.pallas_call(
        body,
        out_shape=(pltpu.SemaphoreType.DMA((3 * nt,)), pltpu.SemaphoreType.DMA((3 * nt,)),
                   *[pltpu.HBM(a.shape, a.dtype) for a in lands], _sds((8, HD), F32)),
        in_specs=[HBM] * (2 * nt) + [ANY] * len(first), out_specs=(SEM, SEM, *[HBM] * nt, VM),
        input_output_aliases={nt + t: 2 + t for t in range(nt)}, name=name, **SPLIT_CALL)(
            *[_in_hbm(shards[n]) for n in BIG], *lands, *first)
    return outs[0], outs[1], outs[2:2 + nt], outs[-1]


def gather_wait(send_sems, recv_sems, shards, lands, after, layer, *, name, names=BIG):
    nt = len(BIG)

    def body(*refs):
        p, o = dict(zip(BIG, refs[:nt])), dict(zip(BIG, refs[nt:2 * nt]))
        ss, rs = refs[2 * nt], refs[2 * nt + 1]
        x, y, c, chips = _place()
        for core in (0, 1):
            @pl.when(c == core)
            def _():
                for cp in _pull_sends(ss, rs, p, o, layer, core, x, y, chips, names):
                    cp.wait_send()
                for cp in _pull_arrivals(ss, rs, o, core, x, y, chips, (x, y, core), names):
                    cp.wait_recv()

    return pl.pallas_call(
        body, out_shape=tuple(pltpu.HBM(a.shape, a.dtype) for a in lands),
        in_specs=[HBM] * (2 * nt) + [SEM, SEM] + [ANY] * len(after), out_specs=tuple([HBM] * nt),
        input_output_aliases={nt + t: t for t in range(nt)}, name=name, **SPLIT_CALL)(
            *[_in_hbm(shards[n]) for n in BIG], *lands, send_sems, recv_sems, *after)


def gather_forward(shards, lands, layer, *, name, names=BIG):
    nt = len(BIG)

    def body(*refs):
        p, o = dict(zip(BIG, refs[:nt])), dict(zip(BIG, refs[2 * nt:3 * nt]))
        ss, rs = refs[3 * nt:]
        x, y, c, chips = _place()
        for core in (0, 1):
            @pl.when(c == core)
            def _():
                me, sibling = (x, y, core), (x, y, 1 - core)
                sends = _pull_arrivals(ss, rs, o, core, x, y, chips, sibling, names)
                sends += [_remote(ss, rs, 3 * nt + t, p[n].at[layer], _shard_view(o[n], 0, n, 2 * x + y), sibling)
                          for t, n in enumerate(BIG) if n in names]
                for cp in sends:
                    cp.start()
                for cp in sends:
                    cp.wait_send()
                for cp in _pull_arrivals(ss, rs, o, 1 - core, x, y, chips, me, names):
                    cp.wait_recv()
                for t, n in enumerate(BIG):
                    if n in names:
                        own = _shard_view(o[n], 0, n, 2 * x + y)
                        _remote(ss, rs, 3 * nt + t, own, own, me).wait_recv()

    outs = pl.pallas_call(
        body, out_shape=[_sds(a.shape, a.dtype) for a in lands], in_specs=[ANY] * (2 * nt), out_specs=[ANY] * nt,
        input_output_aliases={nt + t: t for t in range(nt)}, scratch_shapes=_dma_sems(4 * nt), name=name)(
            *[shards[n] for n in BIG], *lands)
    return dict(zip(BIG, outs))


def _forward_copies(ss, rs, p, o, layer, core, x, y, chips):
    sibling, nt = (x, y, 1 - core), len(BIG)
    return _pull_arrivals(ss, rs, o, core, x, y, chips, sibling) + [
        _remote(ss, rs, 3 * nt + t, p[n].at[layer], _shard_view(o[n], 0, n, 2 * x + y), sibling) for t, n in enumerate(BIG)]


def _forward_arrivals(ss, rs, o, core, x, y, chips):
    me, nt = (x, y, core), len(BIG)
    own = [(3 * nt + t, _shard_view(o[n], 0, n, 2 * x + y)) for t, n in enumerate(BIG)]
    return _pull_arrivals(ss, rs, o, 1 - core, x, y, chips, me) + [_remote(ss, rs, k, v, v, me) for k, v in own]


def gather_forward_start(shards, lands, layer, *, name):
    nt = len(BIG)

    def body(*refs):
        p, o = dict(zip(BIG, refs[:nt])), dict(zip(BIG, refs[nt:2 * nt]))
        x, y, c, chips = _place()
        for core in (0, 1):
            @pl.when(c == core)
            def _():
                for cp in _forward_copies(refs[2 * nt], refs[2 * nt + 1], p, o, layer, core, x, y, chips):
                    cp.start()
        refs[-1][...] = jnp.zeros_like(refs[-1])

    lands = [_in_hbm(a) for a in lands]
    outs = pl.pallas_call(
        body,
        out_shape=(pltpu.SemaphoreType.DMA((4 * nt,)), pltpu.SemaphoreType.DMA((4 * nt,)),
                   *[pltpu.HBM(a.shape, a.dtype) for a in lands], _sds((8, HD), F32)),
        in_specs=[HBM] * (2 * nt), out_specs=(SEM, SEM, *[HBM] * nt, VM),
        input_output_aliases={nt + t: 2 + t for t in range(nt)}, name=name, **SPLIT_CALL)(
            *[_in_hbm(shards[n]) for n in BIG], *lands)
    return outs[0], outs[1], outs[2:2 + nt], outs[-1]


def gather_forward_wait(send_sems, recv_sems, shards, lands, after, layer, *, name):
    nt = len(BIG)

    def body(*refs):
        p, o = dict(zip(BIG, refs[:nt])), dict(zip(BIG, refs[nt:2 * nt]))
        ss, rs = refs[2 * nt], refs[2 * nt + 1]
        x, y, c, chips = _place()
        for core in (0, 1):
            @pl.when(c == core)
            def _():
                for cp in _forward_copies(ss, rs, p, o, layer, core, x, y, chips):
                    cp.wait_send()
                for cp in _forward_arrivals(ss, rs, o, core, x, y, chips):
                    cp.wait_recv()

    outs = pl.pallas_call(
        body, out_shape=tuple(pltpu.HBM(a.shape, a.dtype) for a in lands),
        in_specs=[HBM] * (2 * nt) + [SEM, SEM] + [ANY] * len(after), out_specs=tuple([HBM] * nt),
        input_output_aliases={nt + t: t for t in range(nt)}, name=name, **SPLIT_CALL)(
            *[_in_hbm(shards[n]) for n in BIG], *lands, send_sems, recv_sems, *after)
    return dict(zip(BIG, outs))


def pair_exchange(g, *, name):
    hh = g[BIG[0]].shape[0] // 2
    nt = len(BIG)

    def body(*refs):
        send_sems, recv_sems = refs[2 * nt:]
        x, y, c, _ = _place()
        copies = [_remote(send_sems, recv_sems, t, refs[t].at[pl.ds(hh * (1 - c), hh)], refs[nt + t], (x, y, 1 - c))
                  for t in range(nt)]
        for cp in copies:
            cp.start()
        for cp in copies:
            cp.wait()

    outs = pl.pallas_call(
        body, out_shape=[_sds((hh,) + g[n].shape[1:], g[n].dtype) for n in BIG], in_specs=[ANY] * nt, out_specs=[ANY] * nt,
        scratch_shapes=_dma_sems(nt), name=name)(*[g[n] for n in BIG])
    return dict(zip(BIG, outs))


def _chip_copies(send_sems, recv_sems, s_refs, land_refs, c, chips):
    hl = s_refs[0].shape[0]
    return [_remote(send_sems, recv_sems, 3 * t + j, _shard_view(s_refs[t], pl.ds(0, hl), n, 2 * chip[0] + chip[1]),
                    land_refs[t].at[j], (*chip, c))
            for t, n in enumerate(BIG) for j, chip in enumerate(chips)]


def _landing_shapes(s):
    hl = s[BIG[0]].shape[0]
    return [_sds((3, hl) + _shard_shape(n), s[n].dtype) for n in BIG]


def chip_exchange(s, *, name):
    nt = len(BIG)

    def body(*refs):
        send_sems, recv_sems = refs[2 * nt:]
        x, y, c, chips = _place()
        copies = _chip_copies(send_sems, recv_sems, refs[:nt], refs[nt:2 * nt], c, chips)
        for cp in copies:
            cp.start()
        for cp in copies:
            cp.wait()

    outs = pl.pallas_call(
        body, out_shape=_landing_shapes(s), in_specs=[ANY] * nt, out_specs=[ANY] * nt,
        scratch_shapes=_dma_sems(3 * nt), name=name)(*[s[n] for n in BIG])
    return dict(zip(BIG, outs))


def chip_exchange_start(s, *, name, after=None):
    nt = len(BIG)
    first = [] if after is None else [after]

    def body(*refs):
        o = refs[2 * nt + len(first):]
        x, y, c, chips = _place()
        for cp in _chip_copies(o[0], o[1], refs[:nt], refs[nt:2 * nt], c, chips):
            cp.start()
        refs[-1][...] = jnp.zeros_like(refs[-1])

    lands = [_in_hbm(lax.empty(d.shape, d.dtype)) for d in _landing_shapes(s)]
    srcs = [_in_hbm(s[n]) for n in BIG]
    outs = pl.pallas_call(
        body,
        out_shape=(pltpu.SemaphoreType.DMA((3 * nt,)), pltpu.SemaphoreType.DMA((3 * nt,)),
                   *[pltpu.HBM(a.shape, a.dtype) for a in srcs + lands], _sds((8, HD), F32)),
        in_specs=[HBM] * (2 * nt) + [ANY] * len(first), out_specs=(SEM, SEM, *[HBM] * (2 * nt), VM),
        input_output_aliases={t: 2 + t for t in range(2 * nt)}, name=name, **SPLIT_CALL)(*srcs, *lands, *first)
    return outs[0], outs[1], outs[2:2 + nt], outs[2 + nt:2 + 2 * nt], outs[-1]


def chip_exchange_wait(send_sems, recv_sems, srcs, lands, after, *, name):
    nt = len(BIG)

    def body(*refs):
        x, y, c, chips = _place()
        for cp in _chip_copies(refs[2 * nt], refs[2 * nt + 1], refs[:nt], refs[nt:2 * nt], c, chips):
            cp.wait_send()
            cp.wait_recv()

    outs = pl.pallas_call(
        body, out_shape=tuple(pltpu.HBM(a.shape, a.dtype) for a in list(srcs) + list(lands)),
        in_specs=[HBM] * (2 * nt) + [SEM, SEM] + [ANY] * len(after), out_specs=tuple([HBM] * (2 * nt)),
        input_output_aliases={t: t for t in range(2 * nt)}, name=name, **SPLIT_CALL)(
            *srcs, *lands, send_sems, recv_sems, *after)
    return dict(zip(BIG, outs[:nt])), dict(zip(BIG, outs[nt:]))


def pair_share(f, l0, hh, *, name):
    nt = len(BIG)

    def body(*refs):
        o = refs[nt:2 * nt]
        send_sems, recv_sems = refs[2 * nt:]
        x, y, c, _ = _place()
        mine, theirs = pl.ds(l0 + hh * c, hh), pl.ds(l0 + hh * (1 - c), hh)
        copies = [_remote(send_sems, recv_sems, t, o[t].at[mine], o[t].at[mine], (x, y, 1 - c)) for t in range(nt)]
        for cp in copies:
            cp.start()
        for t, cp in enumerate(copies):
            cp.wait_send()
            _remote(send_sems, recv_sems, t, o[t].at[theirs], o[t].at[theirs], (x, y, c)).wait_recv()

    outs = pl.pallas_call(
        body, out_shape=[_sds(f[n].shape, f[n].dtype) for n in BIG], in_specs=[ANY] * nt, out_specs=[ANY] * nt,
        input_output_aliases={t: t for t in range(nt)}, scratch_shapes=_dma_sems(nt), name=name)(*[f[n] for n in BIG])
    return dict(zip(BIG, outs))


def pair_add(g, r, idx, *, tensor, name):
    _, _, full, blk = TENSORS[tensor]
    hh = r.shape[0]

    def body(idx_ref, g_ref, r_ref, o_ref):
        o_ref[...] = (g_ref[...].astype(F32) + r_ref[...].astype(F32)).astype(o_ref.dtype)

    own = pl.BlockSpec((None,) + blk, lambda *a: (a[0],) + a[1:-1])
    return pl.pallas_call(
        body, out_shape=_sds(r.shape, r.dtype),
        grid_spec=pltpu.PrefetchScalarGridSpec(
            num_scalar_prefetch=1, grid=(hh,) + tuple(f // b for f, b in zip(full, blk)),
            in_specs=[pl.BlockSpec((None,) + blk, lambda *a: (hh * a[-1][0] + a[0],) + a[1:-1]), own], out_specs=own),
        compiler_params=_cparams(("parallel",) * (1 + len(full))), name=name)(idx, g, r)


def chip_add(s, r, idx, totals, l0, *, tensor, name):
    kind, width, full, _ = TENSORS[tensor]
    shard = _shard_shape(tensor)
    hh = s.shape[0]
    zeros = (0,) * len(shard)

    def body(idx_ref, s_ref, r0_ref, r1_ref, r2_ref, t_ref, o_ref):
        o_ref[...] = ((s_ref[...].astype(F32) + r0_ref[...].astype(F32)) + r1_ref[...].astype(F32)) + r2_ref[...].astype(F32)

    if kind == "lead":
        mine = pl.BlockSpec((None, None) + shard, lambda i, ix: (i, ix[1]) + zeros)
    elif kind == "row":
        mine = pl.BlockSpec((None,) + shard, lambda i, ix: (i, ix[1]) + zeros[1:])
    else:
        mine = pl.BlockSpec((None,) + shard, lambda i, ix: (i,) + zeros[1:] + (ix[1],))
    peer = lambda j: pl.BlockSpec((None, None) + shard, lambda i, ix: (j, i) + zeros)
    return pl.pallas_call(
        body, out_shape=_sds(totals.shape, F32),
        grid_spec=pltpu.PrefetchScalarGridSpec(
            num_scalar_prefetch=1, grid=(hh,), in_specs=[mine, peer(0), peer(1), peer(2), pl.BlockSpec(memory_space=pl.ANY)],
            out_specs=pl.BlockSpec((None,) + shard, lambda i, ix: (l0 + hh * ix[0] + i,) + zeros)),
        input_output_aliases={5: 0}, compiler_params=_cparams(("parallel",)), name=name)(idx, s, r, r, r, totals)


def _flat_rows(arrs):
    v = jnp.concatenate([a.reshape(-1) for a in arrs])
    n = -(-v.shape[0] // 1024) * 1024
    return jnp.pad(v, (0, n - v.shape[0])).reshape(n // HD, HD)


def _unflat(buf, shapes):
    v, out, o = buf.reshape(-1), [], 0
    for s in shapes:
        n = int(np.prod(s))
        out.append(v[o:o + n].reshape(s))
        o += n
    return out


WEIGHTS = ("norm1_g", "norm2_g", "w_ada", "b_ada", "w_in", "w_gla_a2", "b_gla_a", "b_fox_f", "ret_norm_g", "gla_norm_g",
           "q_norm_g", "k_norm_g", "w_br", "w_mg", "b_mg", "w_o", "w_up", "w_conv", "b_conv", "w_down")
REPLICATED = ("norm1_g", "norm2_g", "b_gla_a", "b_fox_f", "ret_norm_g", "gla_norm_g", "q_norm_g", "k_norm_g", "b_mg", "b_conv")
ADAM_BLOCKS = dict(w_ada=(1, 256, 1536), w_in=(1, 256, 1285), w_br=(1, 3, BW, 256), w_mg=(1, 512, 768), w_o=(2, 256, D),
                   w_up=(1, 256, 1408), w_down=(1, 352, D))
ALL_AXES = ("x", "y", "c")


def kernel(x, c, norm1_g, norm2_g, w_ada, b_ada, w_in, w_gla_a2, b_gla_a, b_fox_f, ret_norm_g, gla_norm_g, q_norm_g, k_norm_g, w_br, w_mg, b_mg, w_o, w_up, w_conv, b_conv, w_down, loss_target, m_norm1_g, m_norm2_g, m_w_ada, m_b_ada, m_w_in, m_w_gla_a2, m_b_gla_a, m_b_fox_f, m_ret_norm_g, m_gla_norm_g, m_q_norm_g, m_k_norm_g, m_w_br, m_w_mg, m_b_mg, m_w_o, m_w_up, m_w_conv, m_b_conv, m_w_down, v_norm1_g, v_norm2_g, v_w_ada, v_b_ada, v_w_in, v_w_gla_a2, v_b_gla_a, v_b_fox_f, v_ret_norm_g, v_gla_norm_g, v_q_norm_g, v_k_norm_g, v_w_br, v_w_mg, v_b_mg, v_w_o, v_w_up, v_w_conv, v_b_conv, v_w_down):
    w = dict(zip(WEIGHTS, (norm1_g, norm2_g, w_ada, b_ada, w_in, w_gla_a2, b_gla_a, b_fox_f, ret_norm_g, gla_norm_g,
                           q_norm_g, k_norm_g, w_br, w_mg, b_mg, w_o, w_up, w_conv, b_conv, w_down)))
    m = dict(zip(WEIGHTS, (m_norm1_g, m_norm2_g, m_w_ada, m_b_ada, m_w_in, m_w_gla_a2, m_b_gla_a, m_b_fox_f, m_ret_norm_g,
                           m_gla_norm_g, m_q_norm_g, m_k_norm_g, m_w_br, m_w_mg, m_b_mg, m_w_o, m_w_up, m_w_conv, m_b_conv,
                           m_w_down)))
    v = dict(zip(WEIGHTS, (v_norm1_g, v_norm2_g, v_w_ada, v_b_ada, v_w_in, v_w_gla_a2, v_b_gla_a, v_b_fox_f, v_ret_norm_g,
                           v_gla_norm_g, v_q_norm_g, v_k_norm_g, v_w_br, v_w_mg, v_b_mg, v_w_o, v_w_up, v_w_conv, v_b_conv,
                           v_w_down)))
    nl = norm1_g.shape[0]
    seq = x.shape[1]
    xi, yi, ci = lax.axis_index("x"), lax.axis_index("y"), lax.axis_index("c")
    k_me = 2 * xi + yi
    b_me = 4 * xi + 2 * yi + ci
    ada_n = w_ada.shape[2]
    a2_n, conv_n = w_gla_a2.shape[2], w_conv.shape[2]

    shards = [{n: w[n][:1].astype(MMT) for n in BIG}]

    def gather_finish(l, started, after):
        send_sems, recv_sems, lands, _ = started
        lands = gather_wait(send_sems, recv_sems, shards[l], lands, [after], 0, name=f"gather{l}_wait")
        big = gather_forward(shards[l], lands, 0, name=f"gather{l}_forward")
        big["w1"] = build_w1(big["w_in"], big["w_mg"])
        return big

    blk = _flat_rows([c, w_gla_a2, w_conv])
    g1 = small_allgather(blk, name="gather_small").reshape(8, blk.shape[0], HD)
    c_all = g1[:, :D // HD].reshape(8, D)
    by_chip = g1[0::2].reshape(4, -1)[:, D:]
    a2_sh, conv_sh = by_chip[:, :nl * GLR * a2_n], by_chip[:, nl * GLR * a2_n:nl * (GLR * a2_n + 3 * conv_n)]
    full_small = dict(
        w_gla_a2=a2_sh.reshape(4, nl, GLR, a2_n).transpose(1, 2, 0, 3).reshape(nl, GLR, 4 * a2_n),
        w_conv=conv_sh.reshape(4, nl, 3, conv_n).transpose(1, 2, 0, 3).reshape(nl, 3, 4 * conv_n))

    b_ada_sh = lax.dynamic_slice_in_dim(b_ada, k_me * ada_n, ada_n, axis=1)[:, None, :]
    mod_sh = ada_mod(c_all, w_ada, b_ada_sh)
    g2 = small_allgather(mod_sh.reshape(nl * 8, ada_n), name="gather_mod").reshape(4, 2, nl, 8, ada_n)[:, 0]
    mod_me = lax.dynamic_index_in_dim(g2, b_me, axis=2, keepdims=False).transpose(1, 0, 2).reshape(nl, 4 * ada_n)

    wsmall = {n: w[n] for n in REPLICATED}
    wsmall.update(full_small)
    mods = [[mod_me[l:l + 1, i * D:(i + 1) * D] for i in range(6)] for l in range(nl)]

    send_sems, recv_sems, lands, token = gather_start(shards[0], 0, name="gather0_start_first", after=mod_me,
                                                      names=FIRST_NEEDED)
    shards += [{n: (w[n][l:l + 1] + token[0, 0]).astype(MMT) for n in BIG} for l in range(1, nl)]
    cast_meanwhile = [a for sh in shards[1:] for a in sh.values()]
    lands = gather_wait(send_sems, recv_sems, shards[0], lands, [token, *cast_meanwhile], 0, name="gather0_wait_first",
                        names=FIRST_NEEDED)
    first = gather_forward(shards[0], lands, 0, name="gather0_forward_first", names=FIRST_NEEDED)
    later_sems = gather_start(shards[0], 0, name="gather0_start_later", names=NEEDED_LATER, lands=[first[n] for n in BIG])
    in_flight = dict(zip(BIG, later_sems[2]))
    big = {"w1": build_w1(in_flight["w_in"], in_flight["w_mg"])}

    def rest_of_layer0(after):
        rest = gather_wait(later_sems[0], later_sems[1], shards[0], later_sems[2], [after], 0, name="gather0_wait_later",
                           names=NEEDED_LATER)
        return gather_forward(shards[0], rest, 0, name="gather0_forward_later", names=NEEDED_LATER)

    cosf, sinf = _rope_tables(seq)
    xs, saved, params = x[0], [], []
    for l in range(nl):
        if l + 1 < nl:
            started = gather_start(shards[l + 1], 0, name=f"gather{l + 1}_start", after=xs if l else big["w1"])
            mods[l][1] = mods[l][1] + started[-1][0, 0]
        trade = []

        def trade_next_layer(after, l=l, started=started if l + 1 < nl else None, big=big):
            lands = gather_wait(started[0], started[1], shards[l + 1], started[2], [after], 0, name=f"gather{l + 1}_wait")
            trade.append(gather_forward_start(shards[l + 1], lands, 0, name=f"gather{l + 1}_forward_start"))
            return {**{n: big[n] for n in LATE_WEIGHTS}, "token": trade[0][-1]}

        later = rest_of_layer0 if l == 0 else trade_next_layer if l + 1 < nl else None
        xs, sv, p = layer_fwd(xs, mods[l], layer_params(wsmall, big, l, later=later), cosf, sinf)
        saved.append(sv)
        params.append(p)
        if trade:
            big = gather_forward_wait(trade[0][0], trade[0][1], shards[l + 1], trade[0][2], [xs], 0,
                                      name=f"gather{l + 1}_forward_wait")
            big["w1"] = build_w1(big["w_in"], big["w_mg"])
        elif l + 1 < nl:
            big = gather_finish(l + 1, started, xs)
    loss_part, dx = loss_and_grad(xs, loss_target[0], name="loss")
    loss = lax.psum(loss_part[0, 0], ALL_AXES)
    grads, dmods = [None] * nl, [None] * nl
    idx = jnp.stack([ci, k_me]).astype(jnp.int32)
    totals = {n: lax.empty((nl,) + _shard_shape(n), F32) for n in BIG}

    def finish_group(pending, after, totals, idx):
        group, send_sems, recv_sems, srcs, lands, _ = pending
        sums, from_chips = chip_exchange_wait(send_sems, recv_sems, srcs, lands, after, name=f"rs{group}_chip_exchange_wait")
        totals = {n: chip_add(sums[n], from_chips[n], idx, totals[n], RS_GROUP * group, tensor=n,
                              name=f"rs{group}_chip_add_{n}") for n in BIG}
        return pair_share(totals, RS_GROUP * group, RS_GROUP // 2, name=f"rs{group}_pair_share")

    pending = None
    for group in reversed(range(nl // RS_GROUP)):
        layers = range(RS_GROUP * group, RS_GROUP * (group + 1))
        stacks = {n: lax.empty((RS_GROUP,) + TENSORS[n][2], MMT) for n in BIG if n != "w_br"}
        if pending is not None:
            mods[layers[-1]][5] = mods[layers[-1]][5] + pending[-1][0, 0]
        for l in reversed(layers):
            dx, g, dmods[l], stacks = layer_bwd(dx, mods[l], params[l], saved[l], cosf, sinf, stacks, l - layers[0])
            grads[l] = layer_grads(g)
        stacks["w_br"] = jnp.stack([grads[l]["w_br"].astype(MMT) for l in layers])
        from_sibling = pair_exchange(stacks, name=f"rs{group}_pair_exchange")
        chip_sum = {n: pair_add(stacks[n], from_sibling[n], idx, tensor=n, name=f"rs{group}_pair_add_{n}") for n in BIG}
        if group > 0:
            if pending is not None:
                totals = finish_group(pending, [dx, *chip_sum.values()], totals, idx)
            pending = (group, *chip_exchange_start(chip_sum, name=f"rs{group}_chip_exchange_start"))

    small_names = REPLICATED + ("w_gla_a2", "w_conv")
    small_shapes = [(nl, 6 * D)] + [(nl,) + grads[0][n].shape for n in small_names]
    vec = _flat_rows([jnp.concatenate(dmods, axis=0)] + [jnp.stack([grads[l][n] for l in range(nl)]) for n in small_names])
    gs = small_allgather(vec, name="gather_small_grads")
    earlier = pending
    pending = (0, *chip_exchange_start(chip_sum, name="rs0_chip_exchange_start", after=gs))
    if earlier is not None:
        totals = finish_group(earlier, [dx, pending[-1]], totals, idx)
    gs = (gs + pending[-1][0, 0]).reshape(8, vec.shape[0], HD)
    summed = _unflat(sum_devices(gs), small_shapes)
    grad = dict(zip(small_names, summed[1:]))
    grad["b_ada"] = summed[0]
    grad["w_gla_a2"] = lax.dynamic_slice_in_dim(grad["w_gla_a2"], k_me * a2_n, a2_n, axis=2)
    grad["w_conv"] = lax.dynamic_slice_in_dim(grad["w_conv"], k_me * conv_n, conv_n, axis=2)
    dmod_all = gs[:, :nl * 6 * D // HD].reshape(8, nl, 6 * D)
    dmod_sh = lax.dynamic_slice_in_dim(dmod_all, k_me * ada_n, ada_n, axis=2).transpose(1, 0, 2)
    grad["w_ada"] = ada_dw(c_all, dmod_sh)

    delta, new_m, new_v = {}, {}, {}
    delta["w_ada"], new_m["w_ada"], new_v["w_ada"] = adamw(w["w_ada"], grad["w_ada"], m["w_ada"], v["w_ada"],
                                                          block=ADAM_BLOCKS["w_ada"], name="adamw_w_ada")
    rest = [n for n in WEIGHTS if n not in ADAM_BLOCKS]
    shapes = [w[n].shape for n in rest]
    flat = [_flat_rows([t[n] for n in rest]) for t in (w, grad, m, v)]
    outs = adamw(*flat, block=flat[0].shape, name="adamw_small")
    for t, o in zip((delta, new_m, new_v), outs):
        t.update(zip(rest, _unflat(o, shapes)))

    later = {n: adamw(w[n], totals[n], m[n], v[n], block=ADAM_BLOCKS[n], name="adamw_later_" + n, rows=(RS_GROUP, nl),
                      with_grad=True) for n in BIG} if nl > RS_GROUP else {}
    done_first = [outs[0], delta["w_ada"]] + [later[n][0] for n in later]
    totals = finish_group(pending, done_first, totals, idx)
    for n in BIG:
        delta[n], new_m[n], new_v[n], grad[n] = adamw(w[n], totals[n], m[n], v[n], block=ADAM_BLOCKS[n], name="adamw_" + n,
                                                      rows=(0, min(RS_GROUP, nl)), into=later.get(n), with_grad=True)

    return (loss, dx[None], *[grad[n] for n in WEIGHTS], *[delta[n] for n in WEIGHTS], *[new_m[n] for n in WEIGHTS],
            *[new_v[n] for n in WEIGHTS])
```

```python
import functools

import numpy as np
import jax
import jax.numpy as jnp
from jax import lax
from jax.experimental import pallas as pl
from jax.experimental.pallas import tpu as pltpu

F32 = jnp.float32
MMT = jnp.bfloat16
HI = lax.Precision.HIGHEST

D = 1024
DEPTH = 4
NH = 4
HD = 128
BW = NH * HD
CH = 64
GDK = 64
GLR = 16
DFF = 2816
EPS = 1e-6
ROPE_BASE = 10000.0

GP, RG, GG, FV, RQ, RK, RV, GQ, GK, GV, FQ, FK, LR, FF = (
    0, 3072, 3584, 4096, 4608, 5120, 5632, 6144, 6400, 6656, 7168, 7680, 8192, 8320)
NZZ = 8448
WZ0 = 3072
IN_W = 5140
W_IN_COLS = dict(rqkv=(0, 1536), rg=(1536, 2048), gqkv=(2048, 3072), lr=(3072, 3088), gg=(3088, 3600), fqk=(3600, 4624),
                 fv=(4624, 5136), ff=(5136, 5140))

VMEM_LIMIT = 56 * 1024 * 1024

ADAM_LR, ADAM_B1, ADAM_B2, ADAM_EPS, ADAM_WD, ADAM_STEP = 0.001, 0.9, 0.999, 1e-08, 0.01, 10


def _cparams(sem=None):
    return pltpu.CompilerParams(dimension_semantics=sem, vmem_limit_bytes=VMEM_LIMIT)


def _sds(shape, dtype):
    return jax.ShapeDtypeStruct(tuple(shape), dtype)


def _dot(a, b, precision=None):
    return lax.dot_general(a, b, (((1,), (0,)), ((), ())), precision=precision, preferred_element_type=F32)


def _dot_nt(a, b, precision=None):
    return lax.dot_general(a, b, (((1,), (1,)), ((), ())), precision=precision, preferred_element_type=F32)


def _dot_tn(a, b, precision=None):
    return lax.dot_general(a, b, (((0,), (0,)), ((), ())), precision=precision, preferred_element_type=F32)


def _silu(x):
    return x * jax.nn.sigmoid(x)


def _log_sigmoid(x):
    return jnp.minimum(x, 0.0) - jnp.log(1.0 + jnp.exp(jnp.minimum(x, -x)))


@jax.custom_vjp
def _swap_halves(x):
    return pltpu.roll(x, HD // 2, 1)


_swap_halves.defvjp(lambda x: (_swap_halves(x), None), lambda _, g: (_swap_halves(g),))


@jax.custom_vjp
def _bdot(a, b):
    return _dot(a.astype(MMT), b.astype(MMT))


@jax.custom_vjp
def _bdot_nt(a, b):
    return _dot_nt(a.astype(MMT), b.astype(MMT))


@jax.custom_vjp
def _bdot_tn(a, b):
    return _dot_tn(a.astype(MMT), b.astype(MMT))


_bdot.defvjp(lambda a, b: (_bdot(a, b), (a, b)), lambda r, g: (_bdot_nt(g, r[1]), _bdot_tn(r[0], g)))
_bdot_nt.defvjp(lambda a, b: (_bdot_nt(a, b), (a, b)), lambda r, g: (_bdot(g, r[1]), _bdot_tn(g, r[0])))
_bdot_tn.defvjp(lambda a, b: (_bdot_tn(a, b), (a, b)), lambda r, g: (_bdot_nt(r[1], g), _bdot(r[0], g)))


def _stacked(blk, idx, layer):
    if layer is None:
        return pl.BlockSpec(blk, idx)
    return pl.BlockSpec((None,) + blk, lambda i, j: (layer,) + idx(i, j))


def mm_nn(a, b, *, tm, tn, out_dtype, name, layer=None):
    m, k = a.shape
    n = b.shape[-1]

    def body(a_ref, b_ref, o_ref):
        o_ref[...] = _dot(a_ref[...], b_ref[...]).astype(o_ref.dtype)

    return pl.pallas_call(
        body, grid=(m // tm, n // tn),
        in_specs=[pl.BlockSpec((tm, k), lambda i, j: (i, 0)), _stacked((k, tn), lambda i, j: (0, j), layer)],
        out_specs=pl.BlockSpec((tm, tn), lambda i, j: (i, j)),
        out_shape=_sds((m, n), out_dtype), compiler_params=_cparams(("parallel", "parallel")), name=name)(a, b)


def mm_nn_residual(a, b, res, gate, *, tm, tn, name, layer=None):
    m, k = a.shape
    n = b.shape[-1]

    def body(a_ref, b_ref, r_ref, g_ref, x_ref, y_ref):
        acc = _dot(a_ref[...], b_ref[...])
        y_ref[...] = acc
        x_ref[...] = r_ref[...] + g_ref[...] * acc

    return pl.pallas_call(
        body, grid=(m // tm, n // tn),
        in_specs=[pl.BlockSpec((tm, k), lambda i, j: (i, 0)), _stacked((k, tn), lambda i, j: (0, j), layer),
                  pl.BlockSpec((tm, tn), lambda i, j: (i, j)), pl.BlockSpec((1, tn), lambda i, j: (0, j))],
        out_specs=[pl.BlockSpec((tm, tn), lambda i, j: (i, j)), pl.BlockSpec((tm, tn), lambda i, j: (i, j))],
        out_shape=[_sds((m, n), F32), _sds((m, n), F32)],
        compiler_params=_cparams(("parallel", "parallel")), name=name)(a, b, res, gate)


def mm_nt(a, b, *, tm, tn, out_dtype, name, layer=None):
    m, k = a.shape
    n = b.shape[-2]

    def body(a_ref, b_ref, o_ref):
        o_ref[...] = _dot_nt(a_ref[...], b_ref[...]).astype(o_ref.dtype)

    return pl.pallas_call(
        body, grid=(m // tm, n // tn),
        in_specs=[pl.BlockSpec((tm, k), lambda i, j: (i, 0)), _stacked((tn, k), lambda i, j: (j, 0), layer)],
        out_specs=pl.BlockSpec((tm, tn), lambda i, j: (i, j)),
        out_shape=_sds((m, n), out_dtype), compiler_params=_cparams(("parallel", "parallel")), name=name)(a, b)


def mm_nt2(a1, a2, b, *, tm, tn, name, layer):
    m, k1 = a1.shape
    k2 = a2.shape[1]
    n = b.shape[-2]

    def body(a1_ref, a2_ref, b_ref, o_ref):
        o_ref[...] = _dot_nt(a1_ref[...], b_ref[:, :k1]) + _dot_nt(a2_ref[...], b_ref[:, k1:])

    return pl.pallas_call(
        body, grid=(m // tm, n // tn),
        in_specs=[pl.BlockSpec((tm, k1), lambda i, j: (i, 0)), pl.BlockSpec((tm, k2), lambda i, j: (i, 0)),
                  _stacked((tn, k1 + k2), lambda i, j: (j, 0), layer)],
        out_specs=pl.BlockSpec((tm, tn), lambda i, j: (i, j)),
        out_shape=_sds((m, n), F32), compiler_params=_cparams(("parallel", "parallel")), name=name)(a1, a2, b)


def mm_tn(a, b, *, tm, tn, out_dtype, name, col0=0, ncols=None, stack=None, layer=None, out_col0=0):
    s, m = a.shape
    n = b.shape[1] - col0 if ncols is None else ncols
    c0, oc0 = col0 // tn, out_col0 // tn

    def body(a_ref, b_ref, *rest):
        o_ref = rest[-1]
        o_ref[...] = _dot_tn(a_ref[...], b_ref[...]).astype(o_ref.dtype)

    in_specs = [pl.BlockSpec((s, tm), lambda i, j: (0, i)), pl.BlockSpec((s, tn), lambda i, j: (0, c0 + j))]
    if stack is None:
        return pl.pallas_call(
            body, grid=(m // tm, n // tn), in_specs=in_specs, out_specs=pl.BlockSpec((tm, tn), lambda i, j: (i, j)),
            out_shape=_sds((m, n), out_dtype), compiler_params=_cparams(("parallel", "parallel")), name=name)(a, b)
    return pl.pallas_call(
        body, grid=(m // tm, n // tn), in_specs=in_specs + [pl.BlockSpec(memory_space=pl.ANY)],
        out_specs=pl.BlockSpec((None, tm, tn), lambda i, j: (layer, i, oc0 + j)),
        out_shape=_sds(stack.shape, stack.dtype), input_output_aliases={2: 0},
        compiler_params=_cparams(("parallel", "parallel")), name=name)(a, b, stack)


def _row_tile(s):
    return min(256, s)


def _norm_mod_f(x, g, scale, shift):
    r = lax.rsqrt(jnp.mean(x * x, axis=-1, keepdims=True) + EPS)
    return (x * r * g) * (1.0 + scale) + shift


def norm_mod(x, g, scale, shift, *, name):
    s = x.shape[0]
    t = _row_tile(s)

    def body(x_ref, g_ref, sc_ref, sh_ref, o_ref):
        o_ref[...] = _norm_mod_f(x_ref[...], g_ref[...], sc_ref[...], sh_ref[...]).astype(o_ref.dtype)

    vec = pl.BlockSpec((1, D), lambda i: (0, 0))
    return pl.pallas_call(
        body, grid=(s // t,), in_specs=[pl.BlockSpec((t, D), lambda i: (i, 0)), vec, vec, vec],
        out_specs=pl.BlockSpec((t, D), lambda i: (i, 0)), out_shape=_sds((s, D), MMT),
        compiler_params=_cparams(("parallel",)), name=name)(x, g, scale, shift)


def norm_mod_bwd(x, dh, dres, g, scale, shift, *, name):
    s = x.shape[0]
    t = _row_tile(s)

    def body(x_ref, dh_ref, dr_ref, g_ref, sc_ref, sh_ref, dx_ref, dg_ref, dsc_ref, dsh_ref):
        @pl.when(pl.program_id(0) == 0)
        def _():
            dg_ref[...] = jnp.zeros_like(dg_ref)
            dsc_ref[...] = jnp.zeros_like(dsc_ref)
            dsh_ref[...] = jnp.zeros_like(dsh_ref)

        _, vjp = jax.vjp(_norm_mod_f, x_ref[...], g_ref[...], sc_ref[...], sh_ref[...])
        dx, dg, dsc, dsh = vjp(dh_ref[...])
        dx_ref[...] = dr_ref[...] + dx
        dg_ref[...] += dg
        dsc_ref[...] += dsc
        dsh_ref[...] += dsh

    row = pl.BlockSpec((t, D), lambda i: (i, 0))
    vec = pl.BlockSpec((1, D), lambda i: (0, 0))
    return pl.pallas_call(
        body, grid=(s // t,), in_specs=[row, row, row, vec, vec, vec], out_specs=[row, vec, vec, vec],
        out_shape=[_sds((s, D), F32)] + [_sds((1, D), F32)] * 3,
        compiler_params=_cparams(("arbitrary",)), name=name)(x, dh, dres, g, scale, shift)


def gate_bwd(dx, y, gate, *, name):
    s = dx.shape[0]
    t = _row_tile(s)

    def body(dx_ref, y_ref, g_ref, dy_ref, dg_ref):
        @pl.when(pl.program_id(0) == 0)
        def _():
            dg_ref[...] = jnp.zeros_like(dg_ref)

        dxv = dx_ref[...]
        dy_ref[...] = (g_ref[...] * dxv).astype(dy_ref.dtype)
        dg_ref[...] += jnp.sum(dxv * y_ref[...], axis=0, keepdims=True)

    row = pl.BlockSpec((t, D), lambda i: (i, 0))
    vec = pl.BlockSpec((1, D), lambda i: (0, 0))
    return pl.pallas_call(
        body, grid=(s // t,), in_specs=[row, row, vec], out_specs=[row, vec],
        out_shape=[_sds((s, D), MMT), _sds((1, D), F32)],
        compiler_params=_cparams(("arbitrary",)), name=name)(dx, y, gate)


def loss_and_grad(xf, target, *, name):
    s = xf.shape[0]
    t = _row_tile(s)

    def body(x_ref, t_ref, l_ref, dx_ref):
        @pl.when(pl.program_id(0) == 0)
        def _():
            l_ref[...] = jnp.zeros_like(l_ref)

        e = x_ref[...] - t_ref[...]
        dx_ref[...] = e * (1.0 / D)
        l_ref[...] += 0.5 * jnp.sum(jnp.sum(e * e, axis=1, keepdims=True), axis=0, keepdims=True) * (1.0 / D)

    row = pl.BlockSpec((t, D), lambda i: (i, 0))
    return pl.pallas_call(
        body, grid=(s // t,), in_specs=[row, row], out_specs=[pl.BlockSpec((1, 1), lambda i: (0, 0)), row],
        out_shape=[_sds((1, 1), F32), _sds((s, D), F32)],
        compiler_params=_cparams(("arbitrary",)), name=name)(xf, target)


def _ret_consts():
    log_g = np.log1p(-np.exp2(-5.0 - np.arange(NH, dtype=np.float32))).astype(np.float32)
    idx = np.arange(CH, dtype=np.float32)
    d_intra = np.exp(np.abs(idx[:, None] - idx[None, :])[None] * log_g[:, None, None]).astype(np.float32)
    k_w = np.exp((CH - 1.0 - idx)[None, :] * log_g[:, None]).astype(np.float32)
    q_w = np.exp((idx + 1.0)[None, :] * log_g[:, None]).astype(np.float32)
    g_chunk = [float(v) for v in np.exp(np.float32(CH) * log_g).astype(np.float32)]
    bc = lambda a: np.ascontiguousarray(np.broadcast_to(a[:, :, None], (NH, CH, HD)))
    return jnp.asarray(d_intra), jnp.asarray(bc(k_w)), jnp.asarray(bc(q_w)), g_chunk


def _rope_tables(s):
    half = HD // 2
    inv_freq = (ROPE_BASE ** (-np.arange(half, dtype=np.float64) / half)).astype(np.float32)
    ang = (np.arange(s, dtype=np.float32)[:, None] * inv_freq[None, :]).astype(np.float64)
    cos, sin = np.cos(ang).astype(np.float32), np.sin(ang).astype(np.float32)
    return jnp.asarray(np.concatenate([cos, cos], axis=1)), jnp.asarray(np.concatenate([-sin, sin], axis=1))


def _ret_chunk(qs, ks, vs, rs, cos, sin, dintra, kw, qw, g_chunk):
    outs, rn = [], []
    for h in range(NH):
        q = qs[h] * cos + _swap_halves(qs[h]) * sin
        k = (ks[h] * cos + _swap_halves(ks[h]) * sin) * (HD ** -0.5)
        sc = _bdot_nt(q, k) * dintra[h]
        outs.append(_bdot(sc, vs[h]) + _bdot(q * qw[h], rs[h]))
        rn.append(g_chunk[h] * rs[h] + _bdot_tn(k * kw[h], vs[h]))
    return outs, rn


def _heads(x):
    return [x[:, h * HD:(h + 1) * HD] for h in range(NH)]


def _chunks_per_step(n):
    return 4 if n % 4 == 0 else 1


def retention_fwd(zz, cosf, sinf, *, name):
    s = zz.shape[0]
    n = s // CH
    nb = _chunks_per_step(n)
    rb = nb * CH
    dintra, kw, qw, g_chunk = _ret_consts()

    def body(q_ref, k_ref, v_ref, c_ref, s_ref, di_ref, kw_ref, qw_ref, o_ref, rp_ref, r_scr):
        @pl.when(pl.program_id(0) == 0)
        def _():
            r_scr[...] = jnp.zeros_like(r_scr)

        r = r_scr[...]
        consts = ([di_ref[h] for h in range(NH)], [kw_ref[h] for h in range(NH)], [qw_ref[h] for h in range(NH)], g_chunk)
        for i in range(nb):
            rows = slice(i * CH, (i + 1) * CH)
            rp_ref[i] = r
            outs, rn = _ret_chunk(_heads(q_ref[rows, :]), _heads(k_ref[rows, :]), _heads(v_ref[rows, :]),
                                  [r[h * HD:(h + 1) * HD] for h in range(NH)], c_ref[rows, :], s_ref[rows, :], *consts)
            o_ref[rows, :] = jnp.concatenate(outs, axis=1)
            r = jnp.concatenate(rn, axis=0)
        r_scr[...] = r

    col = lambda c: pl.BlockSpec((rb, BW), lambda i: (i, c // BW))
    tab = pl.BlockSpec((rb, HD), lambda i: (i, 0))
    cst = lambda shp: pl.BlockSpec(shp, lambda i: (0,) * len(shp))
    return pl.pallas_call(
        body, grid=(n // nb,),
        in_specs=[col(RQ), col(RK), col(RV), tab, tab, cst((NH, CH, CH)), cst((NH, CH, HD)), cst((NH, CH, HD))],
        out_specs=[pl.BlockSpec((rb, BW), lambda i: (i, 0)), pl.BlockSpec((nb, BW, HD), lambda i: (i, 0, 0))],
        out_shape=[_sds((s, BW), F32), _sds((n, BW, HD), F32)],
        scratch_shapes=[pltpu.VMEM((BW, HD), F32)],
        compiler_params=_cparams(("arbitrary",)), name=name)(zz, zz, zz, cosf, sinf, dintra, kw, qw)


def retention_bwd(zz, cosf, sinf, rprev, do, dzz, *, name):
    s = zz.shape[0]
    n = s // CH
    nb = _chunks_per_step(n)
    rb, steps = nb * CH, n // nb
    dintra, kw, qw, g_chunk = _ret_consts()

    def body(q_ref, k_ref, v_ref, c_ref, s_ref, di_ref, kw_ref, qw_ref, rp_ref, do_ref, dzz_ref, dz_ref, dr_scr):
        @pl.when(pl.program_id(0) == 0)
        def _():
            dr_scr[...] = jnp.zeros_like(dr_scr)

        dr = dr_scr[...]
        consts = dict(dintra=[di_ref[h] for h in range(NH)], kw=[kw_ref[h] for h in range(NH)],
                      qw=[qw_ref[h] for h in range(NH)], g_chunk=g_chunk)
        for i in reversed(range(nb)):
            rows = slice(i * CH, (i + 1) * CH)
            rprev_v = rp_ref[i]
            f = functools.partial(_ret_chunk, cos=c_ref[rows, :], sin=s_ref[rows, :], **consts)
            _, vjp = jax.vjp(f, _heads(q_ref[rows, :]), _heads(k_ref[rows, :]), _heads(v_ref[rows, :]),
                             [rprev_v[h * HD:(h + 1) * HD] for h in range(NH)])
            dq, dk, dv, drp = vjp((_heads(do_ref[rows, :]), [dr[h * HD:(h + 1) * HD] for h in range(NH)]))
            dz_ref[rows, :] = jnp.concatenate(dq + dk + dv, axis=1).astype(dz_ref.dtype)
            dr = jnp.concatenate(drp, axis=0)
        dr_scr[...] = dr

    col = lambda c: pl.BlockSpec((rb, BW), lambda i: (steps - 1 - i, c // BW))
    tab = pl.BlockSpec((rb, HD), lambda i: (steps - 1 - i, 0))
    cst = lambda shp: pl.BlockSpec(shp, lambda i: (0,) * len(shp))
    return pl.pallas_call(
        body, grid=(steps,),
        in_specs=[col(RQ), col(RK), col(RV), tab, tab, cst((NH, CH, CH)), cst((NH, CH, HD)), cst((NH, CH, HD)),
                  pl.BlockSpec((nb, BW, HD), lambda i: (steps - 1 - i, 0, 0)),
                  pl.BlockSpec((rb, BW), lambda i: (steps - 1 - i, 0)), pl.BlockSpec(memory_space=pl.ANY)],
        out_specs=pl.BlockSpec((rb, 3 * BW), lambda i: (steps - 1 - i, RQ // (3 * BW))),
        out_shape=_sds(dzz.shape, dzz.dtype), input_output_aliases={10: 0},
        scratch_shapes=[pltpu.VMEM((BW, HD), F32)],
        compiler_params=_cparams(("arbitrary",)), name=name)(zz, zz, zz, cosf, sinf, dintra, kw, qw, rprev, do, dzz)


GKW = NH * GDK


def _gla_consts():
    tri = np.tril(np.ones((CH, CH), np.float32))
    mask_t = np.zeros((BW, GKW), np.float32)
    for h in range(NH):
        mask_t[h * HD:(h + 1) * HD, h * GDK:(h + 1) * GDK] = 1.0
    return jnp.asarray(tri), jnp.asarray(mask_t)


def _gla_chunk(q, k, v, glr, w_a2, b_a, st, tri, mask_t):
    la = _log_sigmoid(_bdot(glr, w_a2) + b_a) * (1.0 / 16.0)
    bc = _dot(tri, la, HI)
    be = jnp.sum(la, axis=0, keepdims=True)
    kv_t = _bdot_tn(v, k * jnp.exp(be - bc)) * mask_t
    sn = jnp.exp(be) * st + kv_t
    return _bdot_nt(q * (GDK ** -0.5), sn), sn


def gla_fwd(zz, w_a2p, b_a, *, name):
    s = zz.shape[0]
    n = s // CH
    nb = _chunks_per_step(n)
    rb = nb * CH
    tri, mask_t = _gla_consts()

    def body(q_ref, k_ref, v_ref, lr_ref, w_ref, b_ref, tri_ref, m_ref, o_ref, sp_ref, st_scr):
        @pl.when(pl.program_id(0) == 0)
        def _():
            st_scr[...] = jnp.zeros_like(st_scr)

        st = st_scr[...]
        for i in range(nb):
            rows = slice(i * CH, (i + 1) * CH)
            sp_ref[i] = st
            o_ref[rows, :], st = _gla_chunk(q_ref[rows, :], k_ref[rows, :], v_ref[rows, :], lr_ref[rows, :], w_ref[...],
                                            b_ref[...], st, tri_ref[...], m_ref[...])
        st_scr[...] = st

    cst = lambda shp: pl.BlockSpec(shp, lambda i: (0,) * len(shp))
    return pl.pallas_call(
        body, grid=(n // nb,),
        in_specs=[pl.BlockSpec((rb, GKW), lambda i: (i, GQ // GKW)), pl.BlockSpec((rb, GKW), lambda i: (i, GK // GKW)),
                  pl.BlockSpec((rb, BW), lambda i: (i, GV // BW)), pl.BlockSpec((rb, HD), lambda i: (i, LR // HD)),
                  cst((HD, GKW)), cst((1, GKW)), cst((CH, CH)), cst((BW, GKW))],
        out_specs=[pl.BlockSpec((rb, BW), lambda i: (i, 0)), pl.BlockSpec((nb, BW, GKW), lambda i: (i, 0, 0))],
        out_shape=[_sds((s, BW), F32), _sds((n, BW, GKW), F32)],
        scratch_shapes=[pltpu.VMEM((BW, GKW), F32)],
        compiler_params=_cparams(("arbitrary",)), name=name)(zz, zz, zz, zz, w_a2p, b_a, tri, mask_t)


def gla_bwd(zz, w_a2p, b_a, sprev, do, dzz, *, name):
    s = zz.shape[0]
    n = s // CH
    nb = _chunks_per_step(n)
    rb, steps = nb * CH, n // nb
    tri, mask_t = _gla_consts()

    def body(q_ref, k_ref, v_ref, lr_ref, w_ref, b_ref, tri_ref, m_ref, sp_ref, do_ref, dzz_ref,
             dz_ref, dlr_ref, dw_ref, db_ref, ds_scr):
        @pl.when(pl.program_id(0) == 0)
        def _():
            ds_scr[...] = jnp.zeros_like(ds_scr)
            dw_ref[...] = jnp.zeros_like(dw_ref)
            db_ref[...] = jnp.zeros_like(db_ref)

        f = functools.partial(_gla_chunk, tri=tri_ref[...], mask_t=m_ref[...])
        ds, dw_sum, db_sum = ds_scr[...], jnp.zeros(dw_ref.shape, F32), jnp.zeros(db_ref.shape, F32)
        for i in reversed(range(nb)):
            rows = slice(i * CH, (i + 1) * CH)
            _, vjp = jax.vjp(f, q_ref[rows, :], k_ref[rows, :], v_ref[rows, :], lr_ref[rows, :], w_ref[...], b_ref[...],
                             sp_ref[i])
            dq, dk, dv, dlr, dw, db, ds = vjp((do_ref[rows, :], ds))
            dz_ref[rows, :] = jnp.concatenate([dq, dk, dv], axis=1).astype(dz_ref.dtype)
            dlr_ref[rows, :] = dlr.astype(dlr_ref.dtype)
            dw_sum, db_sum = dw_sum + dw, db_sum + db
        dw_ref[...] += dw_sum
        db_ref[...] += db_sum
        ds_scr[...] = ds

    cst = lambda shp: pl.BlockSpec(shp, lambda i: (0,) * len(shp))
    r = lambda i: steps - 1 - i
    return pl.pallas_call(
        body, grid=(steps,),
        in_specs=[pl.BlockSpec((rb, GKW), lambda i: (r(i), GQ // GKW)), pl.BlockSpec((rb, GKW), lambda i: (r(i), GK // GKW)),
                  pl.BlockSpec((rb, BW), lambda i: (r(i), GV // BW)), pl.BlockSpec((rb, HD), lambda i: (r(i), LR // HD)),
                  cst((HD, GKW)), cst((1, GKW)), cst((CH, CH)), cst((BW, GKW)),
                  pl.BlockSpec((nb, BW, GKW), lambda i: (r(i), 0, 0)), pl.BlockSpec((rb, BW), lambda i: (r(i), 0)),
                  pl.BlockSpec(memory_space=pl.ANY)],
        out_specs=[pl.BlockSpec((rb, 2 * GKW + BW), lambda i: (r(i), GQ // (2 * GKW + BW))),
                   pl.BlockSpec((rb, HD), lambda i: (r(i), 0)), cst((HD, GKW)), cst((1, GKW))],
        out_shape=[_sds(dzz.shape, dzz.dtype), _sds((s, HD), MMT), _sds((HD, GKW), F32), _sds((1, GKW), F32)],
        input_output_aliases={10: 0}, scratch_shapes=[pltpu.VMEM((BW, GKW), F32)],
        compiler_params=_cparams(("arbitrary",)), name=name)(zz, zz, zz, zz, w_a2p, b_a, tri, mask_t, sprev, do, dzz)


def _fox_pre_f(fqs, fks, ff, gq, gk, bf):
    def rms(x, g):
        return x * lax.rsqrt(jnp.mean(x * x, axis=-1, keepdims=True) + EPS) * g

    qn = [rms(x, gq) * (HD ** -0.5) for x in fqs]
    kn = [rms(x, gk) for x in fks]
    return qn, kn, _log_sigmoid(ff + bf)


def fox_pre(zz, gq, gk, bf, *, name):
    s = zz.shape[0]
    t = _row_tile(s)
    tri = jnp.asarray(np.tril(np.ones((t, t), np.float32)))

    def body(q_ref, k_ref, f_ref, gq_ref, gk_ref, b_ref, tri_ref, qn_ref, kn_ref, cum_ref, carry):
        @pl.when(pl.program_id(0) == 0)
        def _():
            carry[...] = jnp.zeros_like(carry)

        qn, kn, lf = _fox_pre_f(_heads(q_ref[...]), _heads(k_ref[...]), f_ref[...], gq_ref[...], gk_ref[...], b_ref[...])
        qn_ref[...] = jnp.concatenate(qn, axis=1).astype(qn_ref.dtype)
        kn_ref[...] = jnp.concatenate(kn, axis=1).astype(kn_ref.dtype)
        cum_ref[...] = _dot(tri_ref[...], lf, HI) + carry[...]
        carry[...] += jnp.sum(lf, axis=0, keepdims=True)

    vec = pl.BlockSpec((1, HD), lambda i: (0, 0))
    return pl.pallas_call(
        body, grid=(s // t,),
        in_specs=[pl.BlockSpec((t, BW), lambda i: (i, FQ // BW)), pl.BlockSpec((t, BW), lambda i: (i, FK // BW)),
                  pl.BlockSpec((t, HD), lambda i: (i, FF // HD)), vec, vec, vec, pl.BlockSpec((t, t), lambda i: (0, 0))],
        out_specs=[pl.BlockSpec((t, BW), lambda i: (i, 0)), pl.BlockSpec((t, BW), lambda i: (i, 0)),
                   pl.BlockSpec((t, HD), lambda i: (i, 0))],
        out_shape=[_sds((s, BW), MMT), _sds((s, BW), MMT), _sds((s, HD), F32)],
        scratch_shapes=[pltpu.VMEM((1, HD), F32)],
        compiler_params=_cparams(("arbitrary",)), name=name)(zz, zz, zz, gq, gk, bf, tri)


def fox_pre_bwd(zz, gq, gk, bf, dqn, dkn, dcum, dzz, *, name):
    s = zz.shape[0]
    t = _row_tile(s)
    nt = s // t
    triu = jnp.asarray(np.triu(np.ones((t, t), np.float32)))

    def body(q_ref, k_ref, f_ref, gq_ref, gk_ref, b_ref, tri_ref, dqn_ref, dkn_ref, dcum_ref, dzz_ref,
             dz_ref, dff_ref, dgq_ref, dgk_ref, db_ref, carry):
        @pl.when(pl.program_id(0) == 0)
        def _():
            carry[...] = jnp.zeros_like(carry)
            dgq_ref[...] = jnp.zeros_like(dgq_ref)
            dgk_ref[...] = jnp.zeros_like(dgk_ref)
            db_ref[...] = jnp.zeros_like(db_ref)

        dcum_v = dcum_ref[...]
        dlf = _dot(tri_ref[...], dcum_v, HI) + carry[...]
        carry[...] += jnp.sum(dcum_v, axis=0, keepdims=True)
        _, vjp = jax.vjp(_fox_pre_f, _heads(q_ref[...]), _heads(k_ref[...]), f_ref[...], gq_ref[...], gk_ref[...], b_ref[...])
        dq, dk, dff, dgq, dgk, db = vjp((_heads(dqn_ref[...]), _heads(dkn_ref[...]), dlf))
        dz_ref[...] = jnp.concatenate(dq + dk, axis=1).astype(dz_ref.dtype)
        dff_ref[...] = dff.astype(dff_ref.dtype)
        dgq_ref[...] += dgq
        dgk_ref[...] += dgk
        db_ref[...] += db

    r = lambda i: nt - 1 - i
    vec = pl.BlockSpec((1, HD), lambda i: (0, 0))
    return pl.pallas_call(
        body, grid=(nt,),
        in_specs=[pl.BlockSpec((t, BW), lambda i: (r(i), FQ // BW)), pl.BlockSpec((t, BW), lambda i: (r(i), FK // BW)),
                  pl.BlockSpec((t, HD), lambda i: (r(i), FF // HD)), vec, vec, vec, pl.BlockSpec((t, t), lambda i: (0, 0)),
                  pl.BlockSpec((t, BW), lambda i: (r(i), 0)), pl.BlockSpec((t, BW), lambda i: (r(i), 0)),
                  pl.BlockSpec((t, HD), lambda i: (r(i), 0)), pl.BlockSpec(memory_space=pl.ANY)],
        out_specs=[pl.BlockSpec((t, 2 * BW), lambda i: (r(i), FQ // (2 * BW))), pl.BlockSpec((t, HD), lambda i: (r(i), 0)),
                   vec, vec, vec],
        out_shape=[_sds(dzz.shape, dzz.dtype), _sds((s, HD), MMT), _sds((1, HD), F32), _sds((1, HD), F32), _sds((1, HD), F32)],
        input_output_aliases={10: 0}, scratch_shapes=[pltpu.VMEM((1, HD), F32)],
        compiler_params=_cparams(("arbitrary",)), name=name)(zz, zz, zz, gq, gk, bf, triu, dqn, dkn, dcum, dzz)


def _fox_blocks(s):
    return min(256, s), min(512, s)


NEG = -1e30


def fox_attn_fwd(qn, kn, zz, cum_col, cum_row, *, name):
    s = qn.shape[0]
    bq, bk = _fox_blocks(s)

    def body(q_ref, k_ref, v_ref, cc_ref, cr_ref, o_ref, lse_ref):
        qi = pl.program_id(1)
        q = q_ref[...]
        cq = cc_ref[...]
        rows = qi * bq + lax.broadcasted_iota(jnp.int32, (bq, bk), 0)
        cols0 = lax.broadcasted_iota(jnp.int32, (bq, bk), 1)

        def step(j, carry, on_diagonal):
            m, l, acc = carry
            off = pl.multiple_of(j * bk, bk)
            k = k_ref[pl.ds(off, bk), :]
            v = v_ref[pl.ds(off, bk), :].astype(MMT)
            sc = _dot_nt(q, k) + cq - cr_ref[pl.ds(j, 1), :]
            if on_diagonal:
                sc = jnp.where(rows >= cols0 + j * bk, sc, NEG)
            m_new = jnp.maximum(m, jnp.max(sc, axis=1, keepdims=True))
            alpha = jnp.exp(m - m_new)
            p = jnp.exp(sc - m_new)
            return m_new, alpha * l + jnp.sum(p, axis=1, keepdims=True), alpha * acc + _dot(p.astype(MMT), v)

        nfull, nk = (qi * bq + 1) // bk, ((qi + 1) * bq + bk - 1) // bk
        carry = (jnp.full((bq, 1), NEG, F32), jnp.zeros((bq, 1), F32), jnp.zeros((bq, HD), F32))
        carry = lax.fori_loop(0, nfull, functools.partial(step, on_diagonal=False), carry)
        m, l, acc = lax.fori_loop(nfull, nk, functools.partial(step, on_diagonal=True), carry)
        o_ref[...] = acc / l
        lse_ref[...] = m + jnp.log(l)

    return pl.pallas_call(
        body, grid=(NH, s // bq),
        in_specs=[pl.BlockSpec((bq, HD), lambda h, i: (i, h)), pl.BlockSpec((s, HD), lambda h, i: (0, h)),
                  pl.BlockSpec((s, HD), lambda h, i: (0, FV // HD + h)),
                  pl.BlockSpec((None, bq, 1), lambda h, i: (h, i, 0)), pl.BlockSpec((None, s // bk, bk), lambda h, i: (h, 0, 0))],
        out_specs=[pl.BlockSpec((bq, HD), lambda h, i: (i, h)), pl.BlockSpec((None, bq, 1), lambda h, i: (h, i, 0))],
        out_shape=[_sds((s, BW), F32), _sds((NH, s, 1), F32)],
        compiler_params=_cparams(("parallel", "parallel")), name=name)(qn, kn, zz, cum_col, cum_row)


def fox_attn_bwd(qn, kn, zz, cum_col, cum_row, lse, do, dzz, *, name):
    s = qn.shape[0]
    bq, bk = _fox_blocks(s)
    nkc = s // bk

    def body(q_ref, k_ref, v_ref, cc_ref, cr_ref, lse_ref, do_ref, dzz_ref, dq_ref, dk_ref, dv_ref, dc_ref,
             p_scr, dp_scr, dv_scr):
        qi = pl.program_id(1)

        @pl.when(qi == 0)
        def _():
            dk_ref[...] = jnp.zeros_like(dk_ref)
            dv_scr[...] = jnp.zeros_like(dv_scr)
            dc_ref[...] = jnp.zeros_like(dc_ref)

        q = q_ref[...]
        dob = do_ref[...].astype(MMT)
        cq = cc_ref[...]
        lse_v = lse_ref[...]
        rows = qi * bq + lax.broadcasted_iota(jnp.int32, (bq, bk), 0)
        cols0 = lax.broadcasted_iota(jnp.int32, (bq, bk), 1)
        nfull, nk = (qi * bq + 1) // bk, ((qi + 1) * bq + bk - 1) // bk

        def probs(j, delta, on_diagonal):
            off = pl.multiple_of(j * bk, bk)
            sc = _dot_nt(q, k_ref[pl.ds(off, bk), :]) + cq - cr_ref[pl.ds(j, 1), :]
            p = jnp.exp(sc - lse_v)
            if on_diagonal:
                p = jnp.where(rows >= cols0 + j * bk, p, 0.0)
            dp = _dot_nt(dob, v_ref[pl.ds(off, bk), :].astype(MMT))
            p_scr[j] = p
            dp_scr[j] = dp
            return delta + jnp.sum(p * dp, axis=1, keepdims=True)

        delta = lax.fori_loop(0, nfull, functools.partial(probs, on_diagonal=False), jnp.zeros((bq, 1), F32))
        delta = lax.fori_loop(nfull, nk, functools.partial(probs, on_diagonal=True), delta)

        def grads(j, dq):
            off = pl.multiple_of(j * bk, bk)
            p = p_scr[j]
            ds = p * (dp_scr[j] - delta)
            dsm = ds.astype(MMT)
            dv_scr[pl.ds(off, bk), :] += _dot_tn(p.astype(MMT), dob)
            dk_ref[pl.ds(off, bk), :] += _dot_tn(dsm, q)
            dc_ref[pl.ds(j, 1), :] -= jnp.sum(ds, axis=0, keepdims=True)
            return dq + _dot(dsm, k_ref[pl.ds(off, bk), :])

        dq_ref[...] = lax.fori_loop(0, nk, grads, jnp.zeros((bq, HD), F32))

        @pl.when(qi == pl.num_programs(1) - 1)
        def _():
            dv_ref[...] = dv_scr[...].astype(dv_ref.dtype)

    full = lambda c0=0: pl.BlockSpec((s, HD), lambda h, i: (0, c0 + h))
    blk = lambda: pl.BlockSpec((bq, HD), lambda h, i: (i, h))
    colv = lambda: pl.BlockSpec((None, bq, 1), lambda h, i: (h, i, 0))
    rowv = lambda: pl.BlockSpec((None, nkc, bk), lambda h, i: (h, 0, 0))
    return pl.pallas_call(
        body, grid=(NH, s // bq),
        in_specs=[blk(), full(), full(FV // HD), colv(), rowv(), colv(), blk(), pl.BlockSpec(memory_space=pl.ANY)],
        out_specs=[blk(), full(), full(FV // HD), rowv()],
        out_shape=[_sds((s, BW), F32), _sds((s, BW), F32), _sds(dzz.shape, dzz.dtype), _sds((NH, nkc, bk), F32)],
        input_output_aliases={7: 2},
        scratch_shapes=[pltpu.VMEM((nkc, bq, bk), F32), pltpu.VMEM((nkc, bq, bk), F32), pltpu.VMEM((s, HD), F32)],
        compiler_params=_cparams(("parallel", "arbitrary")), name=name)(qn, kn, zz, cum_col, cum_row, lse, do, dzz)


def _branch_f(rets, rgs, glas, ggs, ret_g, gla_g):
    out_r, out_g = [], []
    for h in range(NH):
        xc = rets[h] - jnp.mean(rets[h], axis=-1, keepdims=True)
        y = xc * lax.rsqrt(jnp.mean(xc * xc, axis=-1, keepdims=True) + EPS) * ret_g[h]
        out_r.append(_silu(rgs[h]) * y)
        x = glas[h]
        y = x * lax.rsqrt(jnp.mean(x * x, axis=-1, keepdims=True) + EPS) * gla_g
        out_g.append(_silu(ggs[h]) * y)
    return out_r, out_g


def _w_br_spec(layer):
    return pl.BlockSpec((None, 3, BW, D), lambda i: (layer, 0, 0, 0))


def mix_fwd(ret_raw, gla_raw, fox_o, zz, ret_g, gla_g, b_mg, w_br, *, name, layer):
    s = zz.shape[0]
    t = _row_tile(s)

    def body(r_ref, g_ref, f_ref, rg_ref, gg_ref, gp_ref, rgn_ref, ggn_ref, bmg_ref, w_ref, o_ref):
        rgn = rgn_ref[...]
        br_r, br_g = _branch_f(_heads(r_ref[...]), _heads(rg_ref[...]), _heads(g_ref[...]), _heads(gg_ref[...]),
                               _heads(rgn), ggn_ref[...])
        brs = [jnp.concatenate(br_r, axis=1), jnp.concatenate(br_g, axis=1), f_ref[...]]
        acc = jnp.zeros((t, D), F32)
        for b in range(3):
            gate = jax.nn.sigmoid(gp_ref[:, b * D:(b + 1) * D] + bmg_ref[:, b * D:(b + 1) * D])
            acc = acc + gate * _dot(brs[b].astype(MMT), w_ref[b])
        o_ref[...] = acc.astype(o_ref.dtype)

    row = lambda w, c=0: pl.BlockSpec((t, w), lambda i: (i, c // w))
    cst = lambda shp: pl.BlockSpec(shp, lambda i: (0,) * len(shp))
    return pl.pallas_call(
        body, grid=(s // t,),
        in_specs=[row(BW), row(BW), row(BW), row(BW, RG), row(BW, GG), row(3 * D, GP), cst((1, BW)), cst((1, HD)),
                  cst((1, 3 * D)), _w_br_spec(layer)],
        out_specs=row(D), out_shape=_sds((s, D), MMT),
        compiler_params=_cparams(("parallel",)), name=name)(ret_raw, gla_raw, fox_o, zz, zz, zz, ret_g, gla_g, b_mg, w_br)


def mix_bwd(ret_raw, gla_raw, fox_o, zz, ret_g, gla_g, b_mg, w_br, dmi, *, name, layer):
    s = zz.shape[0]
    t = _row_tile(s)

    def body(r_ref, g_ref, f_ref, rg_ref, gg_ref, gp_ref, rgn_ref, ggn_ref, bmg_ref, w_ref, dmi_ref,
             dr_ref, dg_ref, df_ref, dgp_ref, dw_ref, drgn_ref, dggn_ref, dbmg_ref):
        @pl.when(pl.program_id(0) == 0)
        def _():
            dw_ref[...] = jnp.zeros_like(dw_ref)
            drgn_ref[...] = jnp.zeros_like(drgn_ref)
            dggn_ref[...] = jnp.zeros_like(dggn_ref)
            dbmg_ref[...] = jnp.zeros_like(dbmg_ref)

        (br_r, br_g), vjp = jax.vjp(_branch_f, _heads(r_ref[...]), _heads(rg_ref[...]), _heads(g_ref[...]),
                                    _heads(gg_ref[...]), _heads(rgn_ref[...]), ggn_ref[...])
        brs = [jnp.concatenate(br_r, axis=1).astype(MMT), jnp.concatenate(br_g, axis=1).astype(MMT),
               f_ref[...].astype(MMT)]
        dmi_v = dmi_ref[...].astype(F32)
        dbr = []
        for b in range(3):
            w = w_ref[b]
            ybr = _dot(brs[b], w)
            gate = jax.nn.sigmoid(gp_ref[:, b * D:(b + 1) * D] + bmg_ref[:, b * D:(b + 1) * D])
            dgp = dmi_v * ybr * gate * (1.0 - gate)
            dgp_ref[:, b * D:(b + 1) * D] = dgp.astype(dgp_ref.dtype)
            dbmg_ref[:, b * D:(b + 1) * D] += jnp.sum(dgp, axis=0, keepdims=True)
            dy = (dmi_v * gate).astype(MMT)
            dw_ref[b] += _dot_tn(brs[b], dy)
            dbr.append(_dot_nt(dy, w))
        dr, drg, dg, dgg, drgn, dggn = vjp((_heads(dbr[0]), _heads(dbr[1])))
        dr_ref[...] = jnp.concatenate(dr, axis=1)
        dg_ref[...] = jnp.concatenate(dg, axis=1)
        df_ref[...] = dbr[2]
        dgp_ref[:, RG:RG + BW] = jnp.concatenate(drg, axis=1).astype(dgp_ref.dtype)
        dgp_ref[:, GG:GG + BW] = jnp.concatenate(dgg, axis=1).astype(dgp_ref.dtype)
        drgn_ref[...] += jnp.concatenate(drgn, axis=1)
        dggn_ref[...] += dggn

    row = lambda w, c=0: pl.BlockSpec((t, w), lambda i: (i, c // w))
    cst = lambda shp: pl.BlockSpec(shp, lambda i: (0,) * len(shp))
    return pl.pallas_call(
        body, grid=(s // t,),
        in_specs=[row(BW), row(BW), row(BW), row(BW, RG), row(BW, GG), row(3 * D, GP), cst((1, BW)), cst((1, HD)),
                  cst((1, 3 * D)), _w_br_spec(layer), row(D)],
        out_specs=[row(BW), row(BW), row(BW), row(FV), cst((3, BW, D)), cst((1, BW)), cst((1, HD)), cst((1, 3 * D))],
        out_shape=[_sds((s, BW), F32)] * 3 + [_sds((s, NZZ), MMT), _sds((3, BW, D), F32), _sds((1, BW), F32),
                                              _sds((1, HD), F32), _sds((1, 3 * D), F32)],
        compiler_params=_cparams(("arbitrary",)), name=name)(ret_raw, gla_raw, fox_o, zz, zz, zz, ret_g, gla_g, b_mg, w_br, dmi)


CT = 256


def _shift_down(x, k, rows):
    return jnp.where(rows >= k, pltpu.roll(x, k, 0), 0.0)


def _shift_up(x, k, rows, s):
    return jnp.where(rows < s - k, pltpu.roll(x, s - k, 0), 0.0)


def conv_fwd(ug, w_conv, b_conv, *, name):
    s = ug.shape[0]
    nt = DFF // CT

    def body(u_ref, g_ref, w_ref, b_ref, a_ref):
        u = u_ref[...]
        rows = lax.broadcasted_iota(jnp.int32, u.shape, 0)
        uc = b_ref[...] + w_ref[0:1, :] * _shift_down(u, 2, rows) + w_ref[1:2, :] * _shift_down(u, 1, rows) + w_ref[2:3, :] * u
        a_ref[...] = (_silu(uc) * g_ref[...]).astype(a_ref.dtype)

    return pl.pallas_call(
        body, grid=(nt,),
        in_specs=[pl.BlockSpec((s, CT), lambda j: (0, j)), pl.BlockSpec((s, CT), lambda j: (0, nt + j)),
                  pl.BlockSpec((3, CT), lambda j: (0, j)), pl.BlockSpec((1, CT), lambda j: (0, j))],
        out_specs=pl.BlockSpec((s, CT), lambda j: (0, j)), out_shape=_sds((s, DFF), MMT),
        compiler_params=_cparams(("parallel",)), name=name)(ug, ug, w_conv, b_conv)


def conv_bwd(ug, w_conv, b_conv, da, *, name):
    s = ug.shape[0]
    nt = DFF // CT

    def body(u_ref, g_ref, w_ref, b_ref, da_ref, du_ref, dg_ref, dw_ref, db_ref):
        u = u_ref[...]
        rows = lax.broadcasted_iota(jnp.int32, u.shape, 0)
        u2, u1 = _shift_down(u, 2, rows), _shift_down(u, 1, rows)
        uc = b_ref[...] + w_ref[0:1, :] * u2 + w_ref[1:2, :] * u1 + w_ref[2:3, :] * u
        sg = jax.nn.sigmoid(uc)
        da_v = da_ref[...]
        dg_ref[...] = (da_v * uc * sg).astype(dg_ref.dtype)
        duc = da_v * g_ref[...] * sg * (1.0 + uc * (1.0 - sg))
        du = w_ref[2:3, :] * duc + w_ref[1:2, :] * _shift_up(duc, 1, rows, s) + w_ref[0:1, :] * _shift_up(duc, 2, rows, s)
        du_ref[...] = du.astype(du_ref.dtype)
        dw_ref[0:1, :] = jnp.sum(duc * u2, axis=0, keepdims=True)
        dw_ref[1:2, :] = jnp.sum(duc * u1, axis=0, keepdims=True)
        dw_ref[2:3, :] = jnp.sum(duc * u, axis=0, keepdims=True)
        db_ref[...] = jnp.sum(duc, axis=0, keepdims=True)

    col = lambda: pl.BlockSpec((s, CT), lambda j: (0, j))
    return pl.pallas_call(
        body, grid=(nt,),
        in_specs=[col(), pl.BlockSpec((s, CT), lambda j: (0, nt + j)), pl.BlockSpec((3, CT), lambda j: (0, j)),
                  pl.BlockSpec((1, CT), lambda j: (0, j)), col()],
        out_specs=[col(), col(), pl.BlockSpec((3, CT), lambda j: (0, j)), pl.BlockSpec((1, CT), lambda j: (0, j))],
        out_shape=[_sds((s, DFF), MMT), _sds((s, DFF), MMT), _sds((3, DFF), F32), _sds((1, DFF), F32)],
        compiler_params=_cparams(("parallel",)), name=name)(ug, ug, w_conv, b_conv, da)


def place_tail(dzz, dlr, dff, *, name):
    s = dzz.shape[0]
    t = _row_tile(s)

    def body(a_ref, b_ref, z_ref, o_ref):
        o_ref[...] = jnp.concatenate([a_ref[...], b_ref[...]], axis=1)

    spec = pl.BlockSpec((t, HD), lambda i: (i, 0))
    return pl.pallas_call(
        body, grid=(s // t,), in_specs=[spec, spec, pl.BlockSpec(memory_space=pl.ANY)],
        out_specs=pl.BlockSpec((t, 2 * HD), lambda i: (i, LR // (2 * HD))), out_shape=_sds(dzz.shape, dzz.dtype),
        input_output_aliases={2: 0}, compiler_params=_cparams(("parallel",)), name=name)(dlr, dff, dzz)


def _tiles(s):
    return min(1024, s)


def layer_fwd(x, mod, p, cosf, sinf):
    s = x.shape[0]
    tm = _tiles(s)
    l = p["l"]
    shift1, scale1, gate1, shift2, scale2, gate2 = mod
    h = norm_mod(x, p["norm1_g"], scale1, shift1, name="norm_mod")
    zz = mm_nn(h, p["w1"], tm=tm, tn=768, out_dtype=F32, name="mm_w1", layer=l)
    ret_raw, rprev = retention_fwd(zz, cosf, sinf, name="ret_fwd")
    gla_raw, sprev = gla_fwd(zz, p["w_a2p"], p["b_gla_a"], name="gla_fwd")
    qn, kn, cum = fox_pre(zz, p["q_norm_g"], p["k_norm_g"], p["b_foxp"], name="fox_pre")
    bq, bk = _fox_blocks(s)
    cum_t = cum[:, :NH].T
    cum_col, cum_row = cum_t[:, :, None], cum_t.reshape(NH, s // bk, bk)
    fox_o, lse = fox_attn_fwd(qn, kn, zz, cum_col, cum_row, name="fox_fwd")
    if "later" in p:
        p = {**p, **p["later"](fox_o)}
    mi = mix_fwd(ret_raw, gla_raw, fox_o, zz, p["ret_norm_g"], p["gla_norm_g"], p["b_mg"], p["w_br"], name="mix_fwd",
                 layer=l)
    x1, mixed = mm_nn_residual(mi, p["w_o"], x, gate1, tm=tm, tn=512, name="mm_wo", layer=l)
    h2 = norm_mod(x1, p["norm2_g"], scale2, shift2, name="norm_mod")
    ug = mm_nn(h2, p["w_up"], tm=tm, tn=512, out_dtype=F32, name="mm_wup", layer=l)
    b_conv = p["b_conv"]
    if "late_in_layer" in p:
        b_conv = b_conv + p["late_in_layer"](ug)[0, 0]
    a = conv_fwd(ug, p["w_conv"], b_conv, name="conv_fwd")
    x2, y = mm_nn_residual(a, p["w_down"], x1, gate2, tm=tm, tn=512, name="mm_wdown", layer=l)
    saved = dict(x=x, h=h, zz=zz, ret_raw=ret_raw, rprev=rprev, gla_raw=gla_raw, sprev=sprev, qn=qn, kn=kn,
                 cum_col=cum_col, cum_row=cum_row, fox_o=fox_o, lse=lse, mi=mi, mixed=mixed, x1=x1, h2=h2, ug=ug, a=a, y=y)
    return x2, saved, p


def layer_bwd(dx2, mod, p, sv, cosf, sinf, stacks, slot):
    s = dx2.shape[0]
    tm = _tiles(s)
    l = p["l"]
    shift1, scale1, gate1, shift2, scale2, gate2 = mod
    g, stacks = {}, dict(stacks)
    dy, dgate2 = gate_bwd(dx2, sv["y"], gate2, name="gate_bwd")
    stacks["w_down"] = mm_tn(sv["a"], dy, tm=min(1408, DFF), tn=512, out_dtype=MMT, name="mm_dwdown",
                             stack=stacks["w_down"], layer=slot)
    da = mm_nt(dy, p["w_down"], tm=tm, tn=1408, out_dtype=F32, name="mm_da", layer=l)
    du, dg, g["w_conv"], g["b_conv"] = conv_bwd(sv["ug"], p["w_conv"], p["b_conv"], da, name="conv_bwd")
    stacks["w_up"] = mm_tn(sv["h2"], du, tm=D, tn=CT, out_dtype=MMT, name="mm_dwup_u", stack=stacks["w_up"], layer=slot)
    stacks["w_up"] = mm_tn(sv["h2"], dg, tm=D, tn=CT, out_dtype=MMT, name="mm_dwup_g", stack=stacks["w_up"], layer=slot,
                           out_col0=DFF)
    dh2 = mm_nt2(du, dg, p["w_up"], tm=min(512, s), tn=D, name="mm_dh2", layer=l)
    dx1, g["norm2_g"], dscale2, dshift2 = norm_mod_bwd(sv["x1"], dh2, dx2, p["norm2_g"], scale2, shift2, name="norm_mod_bwd")
    dmixed, dgate1 = gate_bwd(dx1, sv["mixed"], gate1, name="gate_bwd")
    stacks["w_o"] = mm_tn(sv["mi"], dmixed, tm=512, tn=512, out_dtype=MMT, name="mm_dwo", stack=stacks["w_o"], layer=slot)
    dmi = mm_nt(dmixed, p["w_o"], tm=tm, tn=512, out_dtype=MMT, name="mm_dmi", layer=l)
    zz = sv["zz"]
    (dret, dgla, dfox, dzz, g["w_br"], g["ret_norm_g"], g["gla_norm_g"], g["b_mg"]) = mix_bwd(
        sv["ret_raw"], sv["gla_raw"], sv["fox_o"], zz, p["ret_norm_g"], p["gla_norm_g"], p["b_mg"], p["w_br"], dmi,
        name="mix_bwd", layer=l)
    dqn, dkn, dzz, dcum_row = fox_attn_bwd(sv["qn"], sv["kn"], zz, sv["cum_col"], sv["cum_row"], sv["lse"], dfox, dzz,
                                           name="fox_bwd")
    dcum = jnp.pad(dcum_row.reshape(NH, s).T, ((0, 0), (0, HD - NH)))
    dzz, dff, g["q_norm_g"], g["k_norm_g"], g["b_foxp"] = fox_pre_bwd(
        zz, p["q_norm_g"], p["k_norm_g"], p["b_foxp"], dqn, dkn, dcum, dzz, name="fox_pre_bwd")
    dzz, dlr, g["w_a2p"], g["b_gla_a"] = gla_bwd(zz, p["w_a2p"], p["b_gla_a"], sv["sprev"], dgla, dzz, name="gla_bwd")
    dzz = retention_bwd(zz, cosf, sinf, sv["rprev"], dret, dzz, name="ret_bwd")
    dzz = place_tail(dzz, dlr, dff, name="place_tail")
    stacks["w_mg"] = mm_tn(sv["h"], dzz, tm=512, tn=768, out_dtype=MMT, name="mm_dwmg", ncols=WZ0, stack=stacks["w_mg"],
                           layer=slot)
    dwz = mm_tn(sv["h"], dzz, tm=512, tn=768, out_dtype=MMT, name="mm_dwz", col0=WZ0)
    stacks["w_in"] = unalign_dw_in(dwz, stacks["w_in"], slot)
    dh = mm_nt(dzz, p["w1"], tm=min(256, s), tn=D, out_dtype=F32, name="mm_dh", layer=l)
    dx, g["norm1_g"], dscale1, dshift1 = norm_mod_bwd(sv["x"], dh, dx1, p["norm1_g"], scale1, shift1, name="norm_mod_bwd")
    dmod = jnp.concatenate([dshift1, dscale1, dgate1, dshift2, dscale2, dgate2], axis=1)
    return dx, g, dmod, stacks


def _align_cols(w_in, w_mg):
    z = lambda n: jnp.zeros((w_in.shape[0], n), w_in.dtype)
    seg = lambda name: w_in[:, W_IN_COLS[name][0]:W_IN_COLS[name][1]]
    return jnp.concatenate([w_mg, seg("rg"), seg("gg"), seg("fv"), seg("rqkv"), seg("gqkv"), seg("fqk"), seg("lr"),
                            z(HD - GLR), seg("ff"), z(HD - NH)], axis=1)


def _unalign_cols(dwz):
    seg = lambda c0, name: dwz[:, c0 - WZ0:c0 - WZ0 + W_IN_COLS[name][1] - W_IN_COLS[name][0]]
    return jnp.concatenate([seg(RQ, "rqkv"), seg(RG, "rg"), seg(GQ, "gqkv"), seg(LR, "lr"), seg(GG, "gg"), seg(FQ, "fqk"),
                            seg(FV, "fv"), seg(FF, "ff")], axis=1)


def build_w1(w_in_sh, w_mg):
    nl = w_mg.shape[0]
    t = _row_tile(D)

    def body(s_ref, g_ref, o_ref):
        o_ref[...] = _align_cols(jnp.concatenate([s_ref[k] for k in range(4)], axis=1), g_ref[...])

    return pl.pallas_call(
        body, grid=(nl, D // t),
        in_specs=[pl.BlockSpec((None, 4, t, IN_W // 4), lambda l, i: (l, 0, i, 0)), pl.BlockSpec((None, t, WZ0), lambda l, i: (l, i, 0))],
        out_specs=pl.BlockSpec((None, t, NZZ), lambda l, i: (l, i, 0)), out_shape=_sds((nl, D, NZZ), w_mg.dtype),
        compiler_params=_cparams(("parallel", "parallel")), name="build_w1")(w_in_sh, w_mg)


def unalign_dw_in(dwz, stack, layer):
    t = _row_tile(D)

    def body(z_ref, s_ref, o_ref):
        w = _unalign_cols(z_ref[...])
        for k in range(4):
            o_ref[k] = w[:, k * (IN_W // 4):(k + 1) * (IN_W // 4)]

    return pl.pallas_call(
        body, grid=(D // t,),
        in_specs=[pl.BlockSpec((t, NZZ - WZ0), lambda i: (i, 0)), pl.BlockSpec(memory_space=pl.ANY)],
        out_specs=pl.BlockSpec((None, 4, t, IN_W // 4), lambda i: (layer, 0, i, 0)), out_shape=_sds(stack.shape, stack.dtype),
        input_output_aliases={1: 0}, compiler_params=_cparams(("parallel",)), name="unalign_dw_in")(dwz, stack)


LATE_WEIGHTS = ("w_br", "w_o", "w_up", "w_down")


def layer_params(w, big, l, later=None, late_in_layer=None):
    row = lambda v: v[l][None, :]
    p = dict(
        l=0, norm1_g=row(w["norm1_g"]), norm2_g=row(w["norm2_g"]), w1=big["w1"],
        w_a2p=jnp.pad(w["w_gla_a2"][l], ((0, HD - GLR), (0, 0))), b_gla_a=row(w["b_gla_a"]),
        b_foxp=jnp.pad(row(w["b_fox_f"]), ((0, 0), (0, HD - NH))), ret_norm_g=row(w["ret_norm_g"]),
        gla_norm_g=row(w["gla_norm_g"]), q_norm_g=row(w["q_norm_g"]), k_norm_g=row(w["k_norm_g"]),
        b_mg=row(w["b_mg"]), w_conv=w["w_conv"][l], b_conv=row(w["b_conv"]))
    if later is None:
        p.update({n: big[n] for n in LATE_WEIGHTS})
    else:
        p["later"] = later
    if late_in_layer is not None:
        p["late_in_layer"] = late_in_layer
    return p


def layer_grads(g):
    vec = lambda v: v[0]
    return dict(
        norm1_g=vec(g["norm1_g"]), norm2_g=vec(g["norm2_g"]), w_gla_a2=g["w_a2p"][:GLR], b_gla_a=vec(g["b_gla_a"]),
        b_fox_f=g["b_foxp"][0, :NH], ret_norm_g=vec(g["ret_norm_g"]), gla_norm_g=vec(g["gla_norm_g"]),
        q_norm_g=vec(g["q_norm_g"]), k_norm_g=vec(g["k_norm_g"]), w_br=g["w_br"], b_mg=vec(g["b_mg"]),
        w_conv=g["w_conv"], b_conv=vec(g["b_conv"]))


def ada_mod(c_all, w_ada, b_ada):
    nl, _, n = w_ada.shape

    def body(c_ref, w_ref, b_ref, o_ref):
        o_ref[...] = _dot(_silu(c_ref[...]), w_ref[...], HI) + b_ref[...]

    return pl.pallas_call(
        body, grid=(nl,),
        in_specs=[pl.BlockSpec((8, D), lambda l: (0, 0)), pl.BlockSpec((None, D, n), lambda l: (l, 0, 0)),
                  pl.BlockSpec((None, 1, n), lambda l: (l, 0, 0))],
        out_specs=pl.BlockSpec((None, 8, n), lambda l: (l, 0, 0)), out_shape=_sds((nl, 8, n), F32),
        compiler_params=_cparams(("parallel",)), name="ada_mod")(c_all, w_ada, b_ada)


def ada_dw(c_all, dmod):
    nl, _, n = dmod.shape

    def body(c_ref, d_ref, o_ref):
        o_ref[...] = _dot_tn(_silu(c_ref[...]), d_ref[...], HI)

    return pl.pallas_call(
        body, grid=(nl,),
        in_specs=[pl.BlockSpec((8, D), lambda l: (0, 0)), pl.BlockSpec((None, 8, n), lambda l: (l, 0, 0))],
        out_specs=pl.BlockSpec((None, D, n), lambda l: (l, 0, 0)), out_shape=_sds((nl, D, n), F32),
        compiler_params=_cparams(("parallel",)), name="ada_dw")(c_all, dmod)


def sum_devices(g):
    def body(g_ref, o_ref):
        acc = g_ref[0]
        for d in range(1, 8):
            acc = acc + g_ref[d]
        o_ref[...] = acc

    return pl.pallas_call(body, out_shape=_sds(g.shape[1:], F32), name="sum_devices")(g)


def adamw(w, g, m, v, *, block, name, rows=None, into=None, with_grad=False):
    nd = w.ndim
    lo, hi = (0, w.shape[0]) if rows is None else rows
    grid = ((hi - lo) // block[0],) + tuple(w.shape[i] // block[i] for i in range(1, nd))
    first = lo // block[0]
    nout = 4 if with_grad else 3
    bc1 = 1.0 - ADAM_B1 ** ADAM_STEP
    bc2 = 1.0 - ADAM_B2 ** ADAM_STEP

    def body(w_ref, g_ref, m_ref, v_ref, *rest):
        d_ref, nm_ref, nv_ref = rest[-nout:][:3]
        gv = g_ref[...]
        if with_grad:
            rest[-1][...] = gv
        nm = ADAM_B1 * m_ref[...] + (1.0 - ADAM_B1) * gv
        nv = ADAM_B2 * v_ref[...] + (1.0 - ADAM_B2) * (gv * gv)
        nm_ref[...] = nm
        nv_ref[...] = nv
        d_ref[...] = -ADAM_LR * ((nm / bc1) / (jnp.sqrt(nv / bc2) + ADAM_EPS) + ADAM_WD * w_ref[...])

    spec = pl.BlockSpec(tuple(block), lambda i, *j: (first + i,) + j)
    given = [] if into is None else list(into)
    return pl.pallas_call(
        body, grid=grid, in_specs=[spec] * 4 + [pl.BlockSpec(memory_space=pl.ANY)] * len(given), out_specs=[spec] * nout,
        out_shape=[_sds(w.shape, F32)] * nout, input_output_aliases={4 + i: i for i in range(len(given))},
        compiler_params=_cparams(("parallel",) * nd), name=name)(w, g, m, v, *given)


MESH = pl.DeviceIdType.MESH
ANY = pl.BlockSpec(memory_space=pl.ANY)
VM = pl.BlockSpec(memory_space=pltpu.VMEM)


def _place():
    x, y, c = lax.axis_index("x"), lax.axis_index("y"), lax.axis_index("c")
    return x, y, c, [(1 - x, y), (x, 1 - y), (1 - x, 1 - y)]


def small_allgather(v, *, name):
    m_per, n = v.shape

    def body(x_ref, out_ref, send_sems, recv_sems, local_sem):
        x, y, c, chips = _place()
        me, sibling = (x, y, c), (x, y, 1 - c)

        def rows(px, py, pc):
            return out_ref.at[pl.ds((4 * px + 2 * py + pc) * m_per, m_per), :]

        def copy(k, block, to, src=None):
            return pltpu.make_async_remote_copy(
                src_ref=rows(*block) if src is None else src, dst_ref=rows(*block),
                send_sem=send_sems.at[k], recv_sem=recv_sems.at[k], device_id=to, device_id_type=MESH)

        mine = pltpu.make_async_copy(x_ref, rows(*me), local_sem)
        mine.start()
        first = [copy(0, me, sibling, src=x_ref)]
        first += [copy(1 + j, me, (*chip, c), src=x_ref) for j, chip in enumerate(chips)]
        for cp in first:
            cp.start()
        passed = [copy(4 + j, (*chip, c), sibling) for j, chip in enumerate(chips)]
        for j, chip in enumerate(chips):
            copy(1 + j, (*chip, c), me).wait_recv()
            passed[j].start()
        copy(0, sibling, me).wait_recv()
        for j, chip in enumerate(chips):
            copy(4 + j, (*chip, 1 - c), me).wait_recv()
        for cp in first + passed:
            cp.wait_send()
        mine.wait()

    return pl.pallas_call(
        body, out_shape=_sds((8 * m_per, n), v.dtype), in_specs=[VM], out_specs=VM,
        scratch_shapes=[pltpu.SemaphoreType.DMA((7,)), pltpu.SemaphoreType.DMA((7,)), pltpu.SemaphoreType.DMA],
        name=name)(v)


TENSORS = {
    "w_in": ("lead", None, (4, D, 1285), (1, D, 1285)),
    "w_mg": ("col", 768, (D, 3072), (512, 3072)),
    "w_br": ("col", 256, (3, BW, D), (3, BW, D)),
    "w_o": ("row", 256, (D, D), (D, D)),
    "w_up": ("col", 1408, (D, 5632), (256, 5632)),
    "w_down": ("row", 704, (DFF, D), (704, D)),
}
BIG = tuple(TENSORS)


def _shard_shape(name):
    kind, width, full, _ = TENSORS[name]
    if kind == "lead":
        return full[1:]
    return full[:-1] + (width,) if kind == "col" else (width,) + full[1:]


def _shard_view(ref, layers, name, k):
    kind, width, full, _ = TENSORS[name]
    if kind == "lead":
        return ref.at[layers, k]
    if kind == "row":
        return ref.at[layers, pl.ds(k * width, width)]
    return ref.at[(layers,) + (slice(None),) * (len(full) - 1) + (pl.ds(k * width, width),)]


def _remote(send_sems, recv_sems, k, src, dst, to):
    return pltpu.make_async_remote_copy(src_ref=src, dst_ref=dst, send_sem=send_sems.at[k], recv_sem=recv_sems.at[k],
                                        device_id=to, device_id_type=MESH)


def _dma_sems(n):
    return [pltpu.SemaphoreType.DMA((n,)), pltpu.SemaphoreType.DMA((n,))]


RS_GROUP = 2
HBM = pl.BlockSpec(memory_space=pltpu.HBM)
SEM = pl.BlockSpec(memory_space=pltpu.SEMAPHORE)
SPLIT_CALL = dict(compiler_params=pltpu.CompilerParams(has_side_effects=pltpu.SideEffectType.DATAFLOW_SIDE_EFFECTING))
PULL_SET = (("w_mg", "w_up", "w_o"), ("w_in", "w_br", "w_down"))
FIRST_NEEDED = ("w_in", "w_mg")
NEEDED_LATER = tuple(n for n in BIG if n not in FIRST_NEEDED)


def _in_hbm(a):
    return pltpu.with_memory_space_constraint(a, pltpu.HBM)


def _pull_sends(send_sems, recv_sems, p, o, layer, core, x, y, chips, names=BIG):
    return [_remote(send_sems, recv_sems, 3 * BIG.index(n) + j, p[n].at[layer], _shard_view(o[n], 0, n, 2 * x + y), (*chip, core))
            for n in PULL_SET[core] if n in names for j, chip in enumerate(chips)]


def _pull_arrivals(send_sems, recv_sems, o, core, x, y, chips, to, names=BIG):
    views = [(3 * BIG.index(n) + j, _shard_view(o[n], 0, n, 2 * chip[0] + chip[1]))
             for n in PULL_SET[core] if n in names for j, chip in enumerate(chips)]
    return [_remote(send_sems, recv_sems, k, v, v, to) for k, v in views]


def gather_start(shards, layer, *, name, after=None, names=BIG, lands=None):
    nt = len(BIG)
    first = [] if after is None else [after]

    def body(*refs):
        p, o = dict(zip(BIG, refs[:nt])), dict(zip(BIG, refs[nt:2 * nt]))
        ss, rs = refs[2 * nt + len(first)], refs[2 * nt + len(first) + 1]
        x, y, c, chips = _place()
        for core in (0, 1):
            @pl.when(c == core)
            def _():
                for cp in _pull_sends(ss, rs, p, o, layer, core, x, y, chips, names):
                    cp.start()
        refs[-1][...] = jnp.zeros_like(refs[-1])

    if lands is None:
        lands = [lax.empty((1,) + TENSORS[n][2], shards[n].dtype) for n in BIG]
    lands = [_in_hbm(a) for a in lands]
    outs = pl.pallas_call(
        body,
        out_shape=(pltpu.SemaphoreType.DMA((3 * nt,)), pltpu.SemaphoreType.DMA((3 * nt,)),
                   *[pltpu.HBM(a.shape, a.dtype) for a in lands], _sds((8, HD), F32)),
        in_specs=[HBM] * (2 * nt) + [ANY] * len(first), out_specs=(SEM, SEM, *[HBM] * nt, VM),
        input_output_aliases={nt + t: 2 + t for t in range(nt)}, name=name, **SPLIT_CALL)(
            *[_in_hbm(shards[n]) for n in BIG], *lands, *first)
    return outs[0], outs[1], outs[2:2 + nt], outs[-1]


def gather_wait(send_sems, recv_sems, shards, lands, after, layer, *, name, names=BIG):
    nt = len(BIG)

    def body(*refs):
        p, o = dict(zip(BIG, refs[:nt])), dict(zip(BIG, refs[nt:2 * nt]))
        ss, rs = refs[2 * nt], refs[2 * nt + 1]
        x, y, c, chips = _place()
        for core in (0, 1):
            @pl.when(c == core)
            def _():
                for cp in _pull_sends(ss, rs, p, o, layer, core, x, y, chips, names):
                    cp.wait_send()
                for cp in _pull_arrivals(ss, rs, o, core, x, y, chips, (x, y, core), names):
                    cp.wait_recv()

    return pl.pallas_call(
        body, out_shape=tuple(pltpu.HBM(a.shape, a.dtype) for a in lands),
        in_specs=[HBM] * (2 * nt) + [SEM, SEM] + [ANY] * len(after), out_specs=tuple([HBM] * nt),
        input_output_aliases={nt + t: t for t in range(nt)}, name=name, **SPLIT_CALL)(
            *[_in_hbm(shards[n]) for n in BIG], *lands, send_sems, recv_sems, *after)


def gather_forward(shards, lands, layer, *, name, names=BIG):
    nt = len(BIG)

    def body(*refs):
        p, o = dict(zip(BIG, refs[:nt])), dict(zip(BIG, refs[2 * nt:3 * nt]))
        ss, rs = refs[3 * nt:]
        x, y, c, chips = _place()
        for core in (0, 1):
            @pl.when(c == core)
            def _():
                me, sibling = (x, y, core), (x, y, 1 - core)
                sends = _pull_arrivals(ss, rs, o, core, x, y, chips, sibling, names)
                sends += [_remote(ss, rs, 3 * nt + t, p[n].at[layer], _shard_view(o[n], 0, n, 2 * x + y), sibling)
                          for t, n in enumerate(BIG) if n in names]
                for cp in sends:
                    cp.start()
                for cp in sends:
                    cp.wait_send()
                for cp in _pull_arrivals(ss, rs, o, 1 - core, x, y, chips, me, names):
                    cp.wait_recv()
                for t, n in enumerate(BIG):
                    if n in names:
                        own = _shard_view(o[n], 0, n, 2 * x + y)
                        _remote(ss, rs, 3 * nt + t, own, own, me).wait_recv()

    outs = pl.pallas_call(
        body, out_shape=[_sds(a.shape, a.dtype) for a in lands], in_specs=[ANY] * (2 * nt), out_specs=[ANY] * nt,
        input_output_aliases={nt + t: t for t in range(nt)}, scratch_shapes=_dma_sems(4 * nt), name=name)(
            *[shards[n] for n in BIG], *lands)
    return dict(zip(BIG, outs))


def _forward_copies(ss, rs, p, o, layer, core, x, y, chips):
    sibling, nt = (x, y, 1 - core), len(BIG)
    return _pull_arrivals(ss, rs, o, core, x, y, chips, sibling) + [
        _remote(ss, rs, 3 * nt + t, p[n].at[layer], _shard_view(o[n], 0, n, 2 * x + y), sibling) for t, n in enumerate(BIG)]


def _forward_arrivals(ss, rs, o, core, x, y, chips):
    me, nt = (x, y, core), len(BIG)
    own = [(3 * nt + t, _shard_view(o[n], 0, n, 2 * x + y)) for t, n in enumerate(BIG)]
    return _pull_arrivals(ss, rs, o, 1 - core, x, y, chips, me) + [_remote(ss, rs, k, v, v, me) for k, v in own]


def gather_forward_start(shards, lands, layer, *, name):
    nt = len(BIG)

    def body(*refs):
        p, o = dict(zip(BIG, refs[:nt])), dict(zip(BIG, refs[nt:2 * nt]))
        x, y, c, chips = _place()
        for core in (0, 1):
            @pl.when(c == core)
            def _():
                for cp in _forward_copies(refs[2 * nt], refs[2 * nt + 1], p, o, layer, core, x, y, chips):
                    cp.start()
        refs[-1][...] = jnp.zeros_like(refs[-1])

    lands = [_in_hbm(a) for a in lands]
    outs = pl.pallas_call(
        body,
        out_shape=(pltpu.SemaphoreType.DMA((4 * nt,)), pltpu.SemaphoreType.DMA((4 * nt,)),
                   *[pltpu.HBM(a.shape, a.dtype) for a in lands], _sds((8, HD), F32)),
        in_specs=[HBM] * (2 * nt), out_specs=(SEM, SEM, *[HBM] * nt, VM),
        input_output_aliases={nt + t: 2 + t for t in range(nt)}, name=name, **SPLIT_CALL)(
            *[_in_hbm(shards[n]) for n in BIG], *lands)
    return outs[0], outs[1], outs[2:2 + nt], outs[-1]


def gather_forward_wait(send_sems, recv_sems, shards, lands, after, layer, *, name):
    nt = len(BIG)

    def body(*refs):
        p, o = dict(zip(BIG, refs[:nt])), dict(zip(BIG, refs[nt:2 * nt]))
        ss, rs = refs[2 * nt], refs[2 * nt + 1]
        x, y, c, chips = _place()
        for core in (0, 1):
            @pl.when(c == core)
            def _():
                for cp in _forward_copies(ss, rs, p, o, layer, core, x, y, chips):
                    cp.wait_send()
                for cp in _forward_arrivals(ss, rs, o, core, x, y, chips):
                    cp.wait_recv()

    outs = pl.pallas_call(
        body, out_shape=tuple(pltpu.HBM(a.shape, a.dtype) for a in lands),
        in_specs=[HBM] * (2 * nt) + [SEM, SEM] + [ANY] * len(after), out_specs=tuple([HBM] * nt),
        input_output_aliases={nt + t: t for t in range(nt)}, name=name, **SPLIT_CALL)(
            *[_in_hbm(shards[n]) for n in BIG], *lands, send_sems, recv_sems, *after)
    return dict(zip(BIG, outs))


def pair_exchange(g, *, name):
    hh = g[BIG[0]].shape[0] // 2
    nt = len(BIG)

    def body(*refs):
        send_sems, recv_sems = refs[2 * nt:]
        x, y, c, _ = _place()
        copies = [_remote(send_sems, recv_sems, t, refs[t].at[pl.ds(hh * (1 - c), hh)], refs[nt + t], (x, y, 1 - c))
                  for t in range(nt)]
        for cp in copies:
            cp.start()
        for cp in copies:
            cp.wait()

    outs = pl.pallas_call(
        body, out_shape=[_sds((hh,) + g[n].shape[1:], g[n].dtype) for n in BIG], in_specs=[ANY] * nt, out_specs=[ANY] * nt,
        scratch_shapes=_dma_sems(nt), name=name)(*[g[n] for n in BIG])
    return dict(zip(BIG, outs))


def _chip_copies(send_sems, recv_sems, s_refs, land_refs, c, chips):
    hl = s_refs[0].shape[0]
    return [_remote(send_sems, recv_sems, 3 * t + j, _shard_view(s_refs[t], pl.ds(0, hl), n, 2 * chip[0] + chip[1]),
                    land_refs[t].at[j], (*chip, c))
            for t, n in enumerate(BIG) for j, chip in enumerate(chips)]


def _landing_shapes(s):
    hl = s[BIG[0]].shape[0]
    return [_sds((3, hl) + _shard_shape(n), s[n].dtype) for n in BIG]


def chip_exchange(s, *, name):
    nt = len(BIG)

    def body(*refs):
        send_sems, recv_sems = refs[2 * nt:]
        x, y, c, chips = _place()
        copies = _chip_copies(send_sems, recv_sems, refs[:nt], refs[nt:2 * nt], c, chips)
        for cp in copies:
            cp.start()
        for cp in copies:
            cp.wait()

    outs = pl.pallas_call(
        body, out_shape=_landing_shapes(s), in_specs=[ANY] * nt, out_specs=[ANY] * nt,
        scratch_shapes=_dma_sems(3 * nt), name=name)(*[s[n] for n in BIG])
    return dict(zip(BIG, outs))


def chip_exchange_start(s, *, name, after=None):
    nt = len(BIG)
    first = [] if after is None else [after]

    def body(*refs):
        o = refs[2 * nt + len(first):]
        x, y, c, chips = _place()
        for cp in _chip_copies(o[0], o[1], refs[:nt], refs[nt:2 * nt], c, chips):
            cp.start()
        refs[-1][...] = jnp.zeros_like(refs[-1])

    lands = [_in_hbm(lax.empty(d.shape, d.dtype)) for d in _landing_shapes(s)]
    srcs = [_in_hbm(s[n]) for n in BIG]
    outs = pl.pallas_call(
        body,
        out_shape=(pltpu.SemaphoreType.DMA((3 * nt,)), pltpu.SemaphoreType.DMA((3 * nt,)),
                   *[pltpu.HBM(a.shape, a.dtype) for a in srcs + lands], _sds((8, HD), F32)),
        in_specs=[HBM] * (2 * nt) + [ANY] * len(first), out_specs=(SEM, SEM, *[HBM] * (2 * nt), VM),
        input_output_aliases={t: 2 + t for t in range(2 * nt)}, name=name, **SPLIT_CALL)(*srcs, *lands, *first)
    return outs[0], outs[1], outs[2:2 + nt], outs[2 + nt:2 + 2 * nt], outs[-1]


def chip_exchange_wait(send_sems, recv_sems, srcs, lands, after, *, name):
    nt = len(BIG)

    def body(*refs):
        x, y, c, chips = _place()
        for cp in _chip_copies(refs[2 * nt], refs[2 * nt + 1], refs[:nt], refs[nt:2 * nt], c, chips):
            cp.wait_send()
            cp.wait_recv()

    outs = pl.pallas_call(
        body, out_shape=tuple(pltpu.HBM(a.shape, a.dtype) for a in list(srcs) + list(lands)),
        in_specs=[HBM] * (2 * nt) + [SEM, SEM] + [ANY] * len(after), out_specs=tuple([HBM] * (2 * nt)),
        input_output_aliases={t: t for t in range(2 * nt)}, name=name, **SPLIT_CALL)(
            *srcs, *lands, send_sems, recv_sems, *after)
    return dict(zip(BIG, outs[:nt])), dict(zip(BIG, outs[nt:]))


def pair_share(f, l0, hh, *, name):
    nt = len(BIG)

    def body(*refs):
        o = refs[nt:2 * nt]
        send_sems, recv_sems = refs[2 * nt:]
        x, y, c, _ = _place()
        mine, theirs = pl.ds(l0 + hh * c, hh), pl.ds(l0 + hh * (1 - c), hh)
        copies = [_remote(send_sems, recv_sems, t, o[t].at[mine], o[t].at[mine], (x, y, 1 - c)) for t in range(nt)]
        for cp in copies:
            cp.start()
        for t, cp in enumerate(copies):
            cp.wait_send()
            _remote(send_sems, recv_sems, t, o[t].at[theirs], o[t].at[theirs], (x, y, c)).wait_recv()

    outs = pl.pallas_call(
        body, out_shape=[_sds(f[n].shape, f[n].dtype) for n in BIG], in_specs=[ANY] * nt, out_specs=[ANY] * nt,
        input_output_aliases={t: t for t in range(nt)}, scratch_shapes=_dma_sems(nt), name=name)(*[f[n] for n in BIG])
    return dict(zip(BIG, outs))


def pair_add(g, r, idx, *, tensor, name):
    _, _, full, blk = TENSORS[tensor]
    hh = r.shape[0]

    def body(idx_ref, g_ref, r_ref, o_ref):
        o_ref[...] = (g_ref[...].astype(F32) + r_ref[...].astype(F32)).astype(o_ref.dtype)

    own = pl.BlockSpec((None,) + blk, lambda *a: (a[0],) + a[1:-1])
    return pl.pallas_call(
        body, out_shape=_sds(r.shape, r.dtype),
        grid_spec=pltpu.PrefetchScalarGridSpec(
            num_scalar_prefetch=1, grid=(hh,) + tuple(f // b for f, b in zip(full, blk)),
            in_specs=[pl.BlockSpec((None,) + blk, lambda *a: (hh * a[-1][0] + a[0],) + a[1:-1]), own], out_specs=own),
        compiler_params=_cparams(("parallel",) * (1 + len(full))), name=name)(idx, g, r)


def chip_add(s, r, idx, totals, l0, *, tensor, name):
    kind, width, full, _ = TENSORS[tensor]
    shard = _shard_shape(tensor)
    hh = s.shape[0]
    zeros = (0,) * len(shard)

    def body(idx_ref, s_ref, r0_ref, r1_ref, r2_ref, t_ref, o_ref):
        o_ref[...] = ((s_ref[...].astype(F32) + r0_ref[...].astype(F32)) + r1_ref[...].astype(F32)) + r2_ref[...].astype(F32)

    if kind == "lead":
        mine = pl.BlockSpec((None, None) + shard, lambda i, ix: (i, ix[1]) + zeros)
    elif kind == "row":
        mine = pl.BlockSpec((None,) + shard, lambda i, ix: (i, ix[1]) + zeros[1:])
    else:
        mine = pl.BlockSpec((None,) + shard, lambda i, ix: (i,) + zeros[1:] + (ix[1],))
    peer = lambda j: pl.BlockSpec((None, None) + shard, lambda i, ix: (j, i) + zeros)
    return pl.pallas_call(
        body, out_shape=_sds(totals.shape, F32),
        grid_spec=pltpu.PrefetchScalarGridSpec(
            num_scalar_prefetch=1, grid=(hh,), in_specs=[mine, peer(0), peer(1), peer(2), pl.BlockSpec(memory_space=pl.ANY)],
            out_specs=pl.BlockSpec((None,) + shard, lambda i, ix: (l0 + hh * ix[0] + i,) + zeros)),
        input_output_aliases={5: 0}, compiler_params=_cparams(("parallel",)), name=name)(idx, s, r, r, r, totals)


def _flat_rows(arrs):
    v = jnp.concatenate([a.reshape(-1) for a in arrs])
    n = -(-v.shape[0] // 1024) * 1024
    return jnp.pad(v, (0, n - v.shape[0])).reshape(n // HD, HD)


def _unflat(buf, shapes):
    v, out, o = buf.reshape(-1), [], 0
    for s in shapes:
        n = int(np.prod(s))
        out.append(v[o:o + n].reshape(s))
        o += n
    return out


WEIGHTS = ("norm1_g", "norm2_g", "w_ada", "b_ada", "w_in", "w_gla_a2", "b_gla_a", "b_fox_f", "ret_norm_g", "gla_norm_g",
           "q_norm_g", "k_norm_g", "w_br", "w_mg", "b_mg", "w_o", "w_up", "w_conv", "b_conv", "w_down")
REPLICATED = ("norm1_g", "norm2_g", "b_gla_a", "b_fox_f", "ret_norm_g", "gla_norm_g", "q_norm_g", "k_norm_g", "b_mg", "b_conv")
ADAM_BLOCKS = dict(w_ada=(1, 256, 1536), w_in=(1, 256, 1285), w_br=(1, 3, BW, 256), w_mg=(1, 512, 768), w_o=(2, 256, D),
                   w_up=(1, 256, 1408), w_down=(1, 352, D))
ALL_AXES = ("x", "y", "c")


def kernel(x, c, norm1_g, norm2_g, w_ada, b_ada, w_in, w_gla_a2, b_gla_a, b_fox_f, ret_norm_g, gla_norm_g, q_norm_g, k_norm_g, w_br, w_mg, b_mg, w_o, w_up, w_conv, b_conv, w_down, loss_target, m_norm1_g, m_norm2_g, m_w_ada, m_b_ada, m_w_in, m_w_gla_a2, m_b_gla_a, m_b_fox_f, m_ret_norm_g, m_gla_norm_g, m_q_norm_g, m_k_norm_g, m_w_br, m_w_mg, m_b_mg, m_w_o, m_w_up, m_w_conv, m_b_conv, m_w_down, v_norm1_g, v_norm2_g, v_w_ada, v_b_ada, v_w_in, v_w_gla_a2, v_b_gla_a, v_b_fox_f, v_ret_norm_g, v_gla_norm_g, v_q_norm_g, v_k_norm_g, v_w_br, v_w_mg, v_b_mg, v_w_o, v_w_up, v_w_conv, v_b_conv, v_w_down):
    w = dict(zip(WEIGHTS, (norm1_g, norm2_g, w_ada, b_ada, w_in, w_gla_a2, b_gla_a, b_fox_f, ret_norm_g, gla_norm_g,
                           q_norm_g, k_norm_g, w_br, w_mg, b_mg, w_o, w_up, w_conv, b_conv, w_down)))
    m = dict(zip(WEIGHTS, (m_norm1_g, m_norm2_g, m_w_ada, m_b_ada, m_w_in, m_w_gla_a2, m_b_gla_a, m_b_fox_f, m_ret_norm_g,
                           m_gla_norm_g, m_q_norm_g, m_k_norm_g, m_w_br, m_w_mg, m_b_mg, m_w_o, m_w_up, m_w_conv, m_b_conv,
                           m_w_down)))
    v = dict(zip(WEIGHTS, (v_norm1_g, v_norm2_g, v_w_ada, v_b_ada, v_w_in, v_w_gla_a2, v_b_gla_a, v_b_fox_f, v_ret_norm_g,
                           v_gla_norm_g, v_q_norm_g, v_k_norm_g, v_w_br, v_w_mg, v_b_mg, v_w_o, v_w_up, v_w_conv, v_b_conv,
                           v_w_down)))
    nl = norm1_g.shape[0]
    seq = x.shape[1]
    xi, yi, ci = lax.axis_index("x"), lax.axis_index("y"), lax.axis_index("c")
    k_me = 2 * xi + yi
    b_me = 4 * xi + 2 * yi + ci
    ada_n = w_ada.shape[2]
    a2_n, conv_n = w_gla_a2.shape[2], w_conv.shape[2]

    shards = [{n: w[n][:1].astype(MMT) for n in BIG}]

    def gather_finish(l, started, after):
        send_sems, recv_sems, lands, _ = started
        lands = gather_wait(send_sems, recv_sems, shards[l], lands, [after], 0, name=f"gather{l}_wait")
        big = gather_forward(shards[l], lands, 0, name=f"gather{l}_forward")
        big["w1"] = build_w1(big["w_in"], big["w_mg"])
        return big

    blk = _flat_rows([c, w_gla_a2, w_conv])
    g1 = small_allgather(blk, name="gather_small").reshape(8, blk.shape[0], HD)
    c_all = g1[:, :D // HD].reshape(8, D)
    by_chip = g1[0::2].reshape(4, -1)[:, D:]
    a2_sh, conv_sh = by_chip[:, :nl * GLR * a2_n], by_chip[:, nl * GLR * a2_n:nl * (GLR * a2_n + 3 * conv_n)]
    full_small = dict(
        w_gla_a2=a2_sh.reshape(4, nl, GLR, a2_n).transpose(1, 2, 0, 3).reshape(nl, GLR, 4 * a2_n),
        w_conv=conv_sh.reshape(4, nl, 3, conv_n).transpose(1, 2, 0, 3).reshape(nl, 3, 4 * conv_n))

    b_ada_sh = lax.dynamic_slice_in_dim(b_ada, k_me * ada_n, ada_n, axis=1)[:, None, :]
    mod_sh = ada_mod(c_all, w_ada, b_ada_sh)
    g2 = small_allgather(mod_sh.reshape(nl * 8, ada_n), name="gather_mod").reshape(4, 2, nl, 8, ada_n)[:, 0]
    mod_me = lax.dynamic_index_in_dim(g2, b_me, axis=2, keepdims=False).transpose(1, 0, 2).reshape(nl, 4 * ada_n)

    wsmall = {n: w[n] for n in REPLICATED}
    wsmall.update(full_small)
    mods = [[mod_me[l:l + 1, i * D:(i + 1) * D] for i in range(6)] for l in range(nl)]

    send_sems, recv_sems, lands, token = gather_start(shards[0], 0, name="gather0_start_first", after=mod_me,
                                                      names=FIRST_NEEDED)
    shards += [{n: (w[n][l:l + 1] + token[0, 0]).astype(MMT) for n in BIG} for l in range(1, nl)]
    cast_meanwhile = [a for sh in shards[1:] for a in sh.values()]
    lands = gather_wait(send_sems, recv_sems, shards[0], lands, [token, *cast_meanwhile], 0, name="gather0_wait_first",
                        names=FIRST_NEEDED)
    first = gather_forward(shards[0], lands, 0, name="gather0_forward_first", names=FIRST_NEEDED)
    later_sems = gather_start(shards[0], 0, name="gather0_start_later", names=NEEDED_LATER, lands=[first[n] for n in BIG])
    in_flight = dict(zip(BIG, later_sems[2]))
    big = {"w1": build_w1(in_flight["w_in"], in_flight["w_mg"])}

    def rest_of_layer0(after):
        rest = gather_wait(later_sems[0], later_sems[1], shards[0], later_sems[2], [after], 0, name="gather0_wait_later",
                           names=NEEDED_LATER)
        return gather_forward(shards[0], rest, 0, name="gather0_forward_later", names=NEEDED_LATER)

    cosf, sinf = _rope_tables(seq)
    xs, saved, params = x[0], [], []
    for l in range(nl):
        if l + 1 < nl:
            started = gather_start(shards[l + 1], 0, name=f"gather{l + 1}_start", after=xs if l else big["w1"])
            mods[l][1] = mods[l][1] + started[-1][0, 0]
        trade = []

        def trade_next_layer(after, l=l, started=started if l + 1 < nl else None):
            lands = gather_wait(started[0], started[1], shards[l + 1], started[2], [after], 0, name=f"gather{l + 1}_wait")
            trade.append(gather_forward_start(shards[l + 1], lands, 0, name=f"gather{l + 1}_forward_start"))
            return trade[0][-1]

        xs, sv, p = layer_fwd(xs, mods[l], layer_params(wsmall, big, l, later=rest_of_layer0 if l == 0 else None,
                                                        late_in_layer=trade_next_layer if l + 1 < nl else None),
                              cosf, sinf)
        saved.append(sv)
        params.append(p)
        if trade:
            big = gather_forward_wait(trade[0][0], trade[0][1], shards[l + 1], trade[0][2], [xs], 0,
                                      name=f"gather{l + 1}_forward_wait")
            big["w1"] = build_w1(big["w_in"], big["w_mg"])
        elif l + 1 < nl:
            big = gather_finish(l + 1, started, xs)
    loss_part, dx = loss_and_grad(xs, loss_target[0], name="loss")
    loss = lax.psum(loss_part[0, 0], ALL_AXES)
    grads, dmods = [None] * nl, [None] * nl
    idx = jnp.stack([ci, k_me]).astype(jnp.int32)
    totals = {n: lax.empty((nl,) + _shard_shape(n), F32) for n in BIG}

    def finish_group(pending, after, totals, idx):
        group, send_sems, recv_sems, srcs, lands, _ = pending
        sums, from_chips = chip_exchange_wait(send_sems, recv_sems, srcs, lands, after, name=f"rs{group}_chip_exchange_wait")
        totals = {n: chip_add(sums[n], from_chips[n], idx, totals[n], RS_GROUP * group, tensor=n,
                              name=f"rs{group}_chip_add_{n}") for n in BIG}
        return pair_share(totals, RS_GROUP * group, RS_GROUP // 2, name=f"rs{group}_pair_share")

    pending = None
    for group in reversed(range(nl // RS_GROUP)):
        layers = range(RS_GROUP * group, RS_GROUP * (group + 1))
        stacks = {n: lax.empty((RS_GROUP,) + TENSORS[n][2], MMT) for n in BIG if n != "w_br"}
        if pending is not None:
            mods[layers[-1]][5] = mods[layers[-1]][5] + pending[-1][0, 0]
        for l in reversed(layers):
            dx, g, dmods[l], stacks = layer_bwd(dx, mods[l], params[l], saved[l], cosf, sinf, stacks, l - layers[0])
            grads[l] = layer_grads(g)
        stacks["w_br"] = jnp.stack([grads[l]["w_br"].astype(MMT) for l in layers])
        from_sibling = pair_exchange(stacks, name=f"rs{group}_pair_exchange")
        chip_sum = {n: pair_add(stacks[n], from_sibling[n], idx, tensor=n, name=f"rs{group}_pair_add_{n}") for n in BIG}
        if group > 0:
            if pending is not None:
                totals = finish_group(pending, [dx, *chip_sum.values()], totals, idx)
            pending = (group, *chip_exchange_start(chip_sum, name=f"rs{group}_chip_exchange_start"))

    small_names = REPLICATED + ("w_gla_a2", "w_conv")
    small_shapes = [(nl, 6 * D)] + [(nl,) + grads[0][n].shape for n in small_names]
    vec = _flat_rows([jnp.concatenate(dmods, axis=0)] + [jnp.stack([grads[l][n] for l in range(nl)]) for n in small_names])
    gs = small_allgather(vec, name="gather_small_grads")
    earlier = pending
    pending = (0, *chip_exchange_start(chip_sum, name="rs0_chip_exchange_start", after=gs))
    if earlier is not None:
        totals = finish_group(earlier, [dx, pending[-1]], totals, idx)
    gs = (gs + pending[-1][0, 0]).reshape(8, vec.shape[0], HD)
    summed = _unflat(sum_devices(gs), small_shapes)
    grad = dict(zip(small_names, summed[1:]))
    grad["b_ada"] = summed[0]
    grad["w_gla_a2"] = lax.dynamic_slice_in_dim(grad["w_gla_a2"], k_me * a2_n, a2_n, axis=2)
    grad["w_conv"] = lax.dynamic_slice_in_dim(grad["w_conv"], k_me * conv_n, conv_n, axis=2)
    dmod_all = gs[:, :nl * 6 * D // HD].reshape(8, nl, 6 * D)
    dmod_sh = lax.dynamic_slice_in_dim(dmod_all, k_me * ada_n, ada_n, axis=2).transpose(1, 0, 2)
    grad["w_ada"] = ada_dw(c_all, dmod_sh)

    delta, new_m, new_v = {}, {}, {}
    delta["w_ada"], new_m["w_ada"], new_v["w_ada"] = adamw(w["w_ada"], grad["w_ada"], m["w_ada"], v["w_ada"],
                                                          block=ADAM_BLOCKS["w_ada"], name="adamw_w_ada")
    rest = [n for n in WEIGHTS if n not in ADAM_BLOCKS]
    shapes = [w[n].shape for n in rest]
    flat = [_flat_rows([t[n] for n in rest]) for t in (w, grad, m, v)]
    outs = adamw(*flat, block=flat[0].shape, name="adamw_small")
    for t, o in zip((delta, new_m, new_v), outs):
        t.update(zip(rest, _unflat(o, shapes)))

    later = {n: adamw(w[n], totals[n], m[n], v[n], block=ADAM_BLOCKS[n], name="adamw_later_" + n, rows=(RS_GROUP, nl),
                      with_grad=True) for n in BIG} if nl > RS_GROUP else {}
    done_first = [outs[0], delta["w_ada"]] + [later[n][0] for n in later]
    totals = finish_group(pending, done_first, totals, idx)
    for n in BIG:
        delta[n], new_m[n], new_v[n], grad[n] = adamw(w[n], totals[n], m[n], v[n], block=ADAM_BLOCKS[n], name="adamw_" + n,
                                                      rows=(0, min(RS_GROUP, nl)), into=later.get(n), with_grad=True)

    return (loss, dx[None], *[grad[n] for n in WEIGHTS], *[delta[n] for n in WEIGHTS], *[new_m[n] for n in WEIGHTS],
            *[new_v[n] for n in WEIGHTS])
```

```python
import functools

import numpy as np
import jax
import jax.numpy as jnp
from jax import lax
from jax.experimental import pallas as pl
from jax.experimental.pallas import tpu as pltpu

F32 = jnp.float32
MMT = jnp.bfloat16
HI = lax.Precision.HIGHEST

D = 1024
DEPTH = 4
NH = 4
HD = 128
BW = NH * HD
CH = 64
GDK = 64
GLR = 16
DFF = 2816
EPS = 1e-6
ROPE_BASE = 10000.0

GP, RG, GG, FV, RQ, RK, RV, GQ, GK, GV, FQ, FK, LR, FF = (
    0, 3072, 3584, 4096, 4608, 5120, 5632, 6144, 6400, 6656, 7168, 7680, 8192, 8320)
NZZ = 8448
WZ0 = 3072
IN_W = 5140
W_IN_COLS = dict(rqkv=(0, 1536), rg=(1536, 2048), gqkv=(2048, 3072), lr=(3072, 3088), gg=(3088, 3600), fqk=(3600, 4624),
                 fv=(4624, 5136), ff=(5136, 5140))

VMEM_LIMIT = 56 * 1024 * 1024

ADAM_LR, ADAM_B1, ADAM_B2, ADAM_EPS, ADAM_WD, ADAM_STEP = 0.001, 0.9, 0.999, 1e-08, 0.01, 10


def _cparams(sem=None):
    return pltpu.CompilerParams(dimension_semantics=sem, vmem_limit_bytes=VMEM_LIMIT)


def _sds(shape, dtype):
    return jax.ShapeDtypeStruct(tuple(shape), dtype)


def _dot(a, b, precision=None):
    return lax.dot_general(a, b, (((1,), (0,)), ((), ())), precision=precision, preferred_element_type=F32)


def _dot_nt(a, b, precision=None):
    return lax.dot_general(a, b, (((1,), (1,)), ((), ())), precision=precision, preferred_element_type=F32)


def _dot_tn(a, b, precision=None):
    return lax.dot_general(a, b, (((0,), (0,)), ((), ())), precision=precision, preferred_element_type=F32)


def _silu(x):
    return x * jax.nn.sigmoid(x)


def _log_sigmoid(x):
    return jnp.minimum(x, 0.0) - jnp.log(1.0 + jnp.exp(jnp.minimum(x, -x)))


@jax.custom_vjp
def _swap_halves(x):
    return pltpu.roll(x, HD // 2, 1)


_swap_halves.defvjp(lambda x: (_swap_halves(x), None), lambda _, g: (_swap_halves(g),))


@jax.custom_vjp
def _bdot(a, b):
    return _dot(a.astype(MMT), b.astype(MMT))


@jax.custom_vjp
def _bdot_nt(a, b):
    return _dot_nt(a.astype(MMT), b.astype(MMT))


@jax.custom_vjp
def _bdot_tn(a, b):
    return _dot_tn(a.astype(MMT), b.astype(MMT))


_bdot.defvjp(lambda a, b: (_bdot(a, b), (a, b)), lambda r, g: (_bdot_nt(g, r[1]), _bdot_tn(r[0], g)))
_bdot_nt.defvjp(lambda a, b: (_bdot_nt(a, b), (a, b)), lambda r, g: (_bdot(g, r[1]), _bdot_tn(g, r[0])))
_bdot_tn.defvjp(lambda a, b: (_bdot_tn(a, b), (a, b)), lambda r, g: (_bdot_nt(r[1], g), _bdot(r[0], g)))


def _stacked(blk, idx, layer):
    if layer is None:
        return pl.BlockSpec(blk, idx)
    return pl.BlockSpec((None,) + blk, lambda i, j: (layer,) + idx(i, j))


def mm_nn(a, b, *, tm, tn, out_dtype, name, layer=None):
    m, k = a.shape
    n = b.shape[-1]

    def body(a_ref, b_ref, o_ref):
        o_ref[...] = _dot(a_ref[...], b_ref[...]).astype(o_ref.dtype)

    return pl.pallas_call(
        body, grid=(m // tm, n // tn),
        in_specs=[pl.BlockSpec((tm, k), lambda i, j: (i, 0)), _stacked((k, tn), lambda i, j: (0, j), layer)],
        out_specs=pl.BlockSpec((tm, tn), lambda i, j: (i, j)),
        out_shape=_sds((m, n), out_dtype), compiler_params=_cparams(("parallel", "parallel")), name=name)(a, b)


def mm_nn_residual(a, b, res, gate, *, tm, tn, name, layer=None):
    m, k = a.shape
    n = b.shape[-1]

    def body(a_ref, b_ref, r_ref, g_ref, x_ref, y_ref):
        acc = _dot(a_ref[...], b_ref[...])
        y_ref[...] = acc
        x_ref[...] = r_ref[...] + g_ref[...] * acc

    return pl.pallas_call(
        body, grid=(m // tm, n // tn),
        in_specs=[pl.BlockSpec((tm, k), lambda i, j: (i, 0)), _stacked((k, tn), lambda i, j: (0, j), layer),
                  pl.BlockSpec((tm, tn), lambda i, j: (i, j)), pl.BlockSpec((1, tn), lambda i, j: (0, j))],
        out_specs=[pl.BlockSpec((tm, tn), lambda i, j: (i, j)), pl.BlockSpec((tm, tn), lambda i, j: (i, j))],
        out_shape=[_sds((m, n), F32), _sds((m, n), F32)],
        compiler_params=_cparams(("parallel", "parallel")), name=name)(a, b, res, gate)


def mm_nt(a, b, *, tm, tn, out_dtype, name, layer=None):
    m, k = a.shape
    n = b.shape[-2]

    def body(a_ref, b_ref, o_ref):
        o_ref[...] = _dot_nt(a_ref[...], b_ref[...]).astype(o_ref.dtype)

    return pl.pallas_call(
        body, grid=(m // tm, n // tn),
        in_specs=[pl.BlockSpec((tm, k), lambda i, j: (i, 0)), _stacked((tn, k), lambda i, j: (j, 0), layer)],
        out_specs=pl.BlockSpec((tm, tn), lambda i, j: (i, j)),
        out_shape=_sds((m, n), out_dtype), compiler_params=_cparams(("parallel", "parallel")), name=name)(a, b)


def mm_nt2(a1, a2, b, *, tm, tn, name, layer):
    m, k1 = a1.shape
    k2 = a2.shape[1]
    n = b.shape[-2]

    def body(a1_ref, a2_ref, b_ref, o_ref):
        o_ref[...] = _dot_nt(a1_ref[...], b_ref[:, :k1]) + _dot_nt(a2_ref[...], b_ref[:, k1:])

    return pl.pallas_call(
        body, grid=(m // tm, n // tn),
        in_specs=[pl.BlockSpec((tm, k1), lambda i, j: (i, 0)), pl.BlockSpec((tm, k2), lambda i, j: (i, 0)),
                  _stacked((tn, k1 + k2), lambda i, j: (j, 0), layer)],
        out_specs=pl.BlockSpec((tm, tn), lambda i, j: (i, j)),
        out_shape=_sds((m, n), F32), compiler_params=_cparams(("parallel", "parallel")), name=name)(a1, a2, b)


def mm_tn(a, b, *, tm, tn, out_dtype, name, col0=0, ncols=None, stack=None, layer=None, out_col0=0):
    s, m = a.shape
    n = b.shape[1] - col0 if ncols is None else ncols
    c0, oc0 = col0 // tn, out_col0 // tn

    def body(a_ref, b_ref, *rest):
        o_ref = rest[-1]
        o_ref[...] = _dot_tn(a_ref[...], b_ref[...]).astype(o_ref.dtype)

    in_specs = [pl.BlockSpec((s, tm), lambda i, j: (0, i)), pl.BlockSpec((s, tn), lambda i, j: (0, c0 + j))]
    if stack is None:
        return pl.pallas_call(
            body, grid=(m // tm, n // tn), in_specs=in_specs, out_specs=pl.BlockSpec((tm, tn), lambda i, j: (i, j)),
            out_shape=_sds((m, n), out_dtype), compiler_params=_cparams(("parallel", "parallel")), name=name)(a, b)
    return pl.pallas_call(
        body, grid=(m // tm, n // tn), in_specs=in_specs + [pl.BlockSpec(memory_space=pl.ANY)],
        out_specs=pl.BlockSpec((None, tm, tn), lambda i, j: (layer, i, oc0 + j)),
        out_shape=_sds(stack.shape, stack.dtype), input_output_aliases={2: 0},
        compiler_params=_cparams(("parallel", "parallel")), name=name)(a, b, stack)


def _row_tile(s):
    return min(256, s)


def _wide_row_tile(s):
    return min(512, s)


def _norm_mod_f(x, g, scale, shift):
    r = lax.rsqrt(jnp.mean(x * x, axis=-1, keepdims=True) + EPS)
    return (x * r * g) * (1.0 + scale) + shift


def norm_mod(x, g, scale, shift, *, name):
    s = x.shape[0]
    t = _wide_row_tile(s)

    def body(x_ref, g_ref, sc_ref, sh_ref, o_ref):
        o_ref[...] = _norm_mod_f(x_ref[...], g_ref[...], sc_ref[...], sh_ref[...]).astype(o_ref.dtype)

    vec = pl.BlockSpec((1, D), lambda i: (0, 0))
    return pl.pallas_call(
        body, grid=(s // t,), in_specs=[pl.BlockSpec((t, D), lambda i: (i, 0)), vec, vec, vec],
        out_specs=pl.BlockSpec((t, D), lambda i: (i, 0)), out_shape=_sds((s, D), MMT),
        compiler_params=_cparams(("parallel",)), name=name)(x, g, scale, shift)


def norm_mod_bwd(x, dh, dres, g, scale, shift, *, name):
    s = x.shape[0]
    t = _wide_row_tile(s)

    def body(x_ref, dh_ref, dr_ref, g_ref, sc_ref, sh_ref, dx_ref, dg_ref, dsc_ref, dsh_ref):
        @pl.when(pl.program_id(0) == 0)
        def _():
            dg_ref[...] = jnp.zeros_like(dg_ref)
            dsc_ref[...] = jnp.zeros_like(dsc_ref)
            dsh_ref[...] = jnp.zeros_like(dsh_ref)

        _, vjp = jax.vjp(_norm_mod_f, x_ref[...], g_ref[...], sc_ref[...], sh_ref[...])
        dx, dg, dsc, dsh = vjp(dh_ref[...])
        dx_ref[...] = dr_ref[...] + dx
        dg_ref[...] += dg
        dsc_ref[...] += dsc
        dsh_ref[...] += dsh

    row = pl.BlockSpec((t, D), lambda i: (i, 0))
    vec = pl.BlockSpec((1, D), lambda i: (0, 0))
    return pl.pallas_call(
        body, grid=(s // t,), in_specs=[row, row, row, vec, vec, vec], out_specs=[row, vec, vec, vec],
        out_shape=[_sds((s, D), F32)] + [_sds((1, D), F32)] * 3,
        compiler_params=_cparams(("arbitrary",)), name=name)(x, dh, dres, g, scale, shift)


def gate_bwd(dx, y, gate, *, name):
    s = dx.shape[0]
    t = _wide_row_tile(s)

    def body(dx_ref, y_ref, g_ref, dy_ref, dg_ref):
        @pl.when(pl.program_id(0) == 0)
        def _():
            dg_ref[...] = jnp.zeros_like(dg_ref)

        dxv = dx_ref[...]
        dy_ref[...] = (g_ref[...] * dxv).astype(dy_ref.dtype)
        dg_ref[...] += jnp.sum(dxv * y_ref[...], axis=0, keepdims=True)

    row = pl.BlockSpec((t, D), lambda i: (i, 0))
    vec = pl.BlockSpec((1, D), lambda i: (0, 0))
    return pl.pallas_call(
        body, grid=(s // t,), in_specs=[row, row, vec], out_specs=[row, vec],
        out_shape=[_sds((s, D), MMT), _sds((1, D), F32)],
        compiler_params=_cparams(("arbitrary",)), name=name)(dx, y, gate)


def loss_and_grad(xf, target, *, name):
    s = xf.shape[0]
    t = _wide_row_tile(s)

    def body(x_ref, t_ref, l_ref, dx_ref):
        @pl.when(pl.program_id(0) == 0)
        def _():
            l_ref[...] = jnp.zeros_like(l_ref)

        e = x_ref[...] - t_ref[...]
        dx_ref[...] = e * (1.0 / D)
        l_ref[...] += 0.5 * jnp.sum(jnp.sum(e * e, axis=1, keepdims=True), axis=0, keepdims=True) * (1.0 / D)

    row = pl.BlockSpec((t, D), lambda i: (i, 0))
    return pl.pallas_call(
        body, grid=(s // t,), in_specs=[row, row], out_specs=[pl.BlockSpec((1, 1), lambda i: (0, 0)), row],
        out_shape=[_sds((1, 1), F32), _sds((s, D), F32)],
        compiler_params=_cparams(("arbitrary",)), name=name)(xf, target)


def _ret_consts():
    log_g = np.log1p(-np.exp2(-5.0 - np.arange(NH, dtype=np.float32))).astype(np.float32)
    idx = np.arange(CH, dtype=np.float32)
    d_intra = np.exp(np.abs(idx[:, None] - idx[None, :])[None] * log_g[:, None, None]).astype(np.float32)
    k_w = np.exp((CH - 1.0 - idx)[None, :] * log_g[:, None]).astype(np.float32)
    q_w = np.exp((idx + 1.0)[None, :] * log_g[:, None]).astype(np.float32)
    g_chunk = [float(v) for v in np.exp(np.float32(CH) * log_g).astype(np.float32)]
    bc = lambda a: np.ascontiguousarray(np.broadcast_to(a[:, :, None], (NH, CH, HD)))
    return jnp.asarray(d_intra), jnp.asarray(bc(k_w)), jnp.asarray(bc(q_w)), g_chunk


def _rope_tables(s):
    half = HD // 2
    inv_freq = (ROPE_BASE ** (-np.arange(half, dtype=np.float64) / half)).astype(np.float32)
    ang = (np.arange(s, dtype=np.float32)[:, None] * inv_freq[None, :]).astype(np.float64)
    cos, sin = np.cos(ang).astype(np.float32), np.sin(ang).astype(np.float32)
    return jnp.asarray(np.concatenate([cos, cos], axis=1)), jnp.asarray(np.concatenate([-sin, sin], axis=1))


def _ret_chunk(qs, ks, vs, rs, cos, sin, dintra, kw, qw, g_chunk):
    outs, rn = [], []
    for h in range(NH):
        q = qs[h] * cos + _swap_halves(qs[h]) * sin
        k = (ks[h] * cos + _swap_halves(ks[h]) * sin) * (HD ** -0.5)
        sc = _bdot_nt(q, k) * dintra[h]
        outs.append(_bdot(sc, vs[h]) + _bdot(q * qw[h], rs[h]))
        rn.append(g_chunk[h] * rs[h] + _bdot_tn(k * kw[h], vs[h]))
    return outs, rn


def _heads(x):
    return [x[:, h * HD:(h + 1) * HD] for h in range(NH)]


def _chunks_per_step(n):
    return 4 if n % 4 == 0 else 1


def retention_fwd(zz, cosf, sinf, *, name):
    s = zz.shape[0]
    n = s // CH
    nb = _chunks_per_step(n)
    rb = nb * CH
    dintra, kw, qw, g_chunk = _ret_consts()

    def body(q_ref, k_ref, v_ref, c_ref, s_ref, di_ref, kw_ref, qw_ref, o_ref, rp_ref, r_scr):
        @pl.when(pl.program_id(0) == 0)
        def _():
            r_scr[...] = jnp.zeros_like(r_scr)

        r = r_scr[...]
        consts = ([di_ref[h] for h in range(NH)], [kw_ref[h] for h in range(NH)], [qw_ref[h] for h in range(NH)], g_chunk)
        for i in range(nb):
            rows = slice(i * CH, (i + 1) * CH)
            rp_ref[i] = r
            outs, rn = _ret_chunk(_heads(q_ref[rows, :]), _heads(k_ref[rows, :]), _heads(v_ref[rows, :]),
                                  [r[h * HD:(h + 1) * HD] for h in range(NH)], c_ref[rows, :], s_ref[rows, :], *consts)
            o_ref[rows, :] = jnp.concatenate(outs, axis=1)
            r = jnp.concatenate(rn, axis=0)
        r_scr[...] = r

    col = lambda c: pl.BlockSpec((rb, BW), lambda i: (i, c // BW))
    tab = pl.BlockSpec((rb, HD), lambda i: (i, 0))
    cst = lambda shp: pl.BlockSpec(shp, lambda i: (0,) * len(shp))
    return pl.pallas_call(
        body, grid=(n // nb,),
        in_specs=[col(RQ), col(RK), col(RV), tab, tab, cst((NH, CH, CH)), cst((NH, CH, HD)), cst((NH, CH, HD))],
        out_specs=[pl.BlockSpec((rb, BW), lambda i: (i, 0)), pl.BlockSpec((nb, BW, HD), lambda i: (i, 0, 0))],
        out_shape=[_sds((s, BW), F32), _sds((n, BW, HD), F32)],
        scratch_shapes=[pltpu.VMEM((BW, HD), F32)],
        compiler_params=_cparams(("arbitrary",)), name=name)(zz, zz, zz, cosf, sinf, dintra, kw, qw)


def retention_bwd(zz, cosf, sinf, rprev, do, dzz, *, name):
    s = zz.shape[0]
    n = s // CH
    nb = _chunks_per_step(n)
    rb, steps = nb * CH, n // nb
    dintra, kw, qw, g_chunk = _ret_consts()

    def body(q_ref, k_ref, v_ref, c_ref, s_ref, di_ref, kw_ref, qw_ref, rp_ref, do_ref, dzz_ref, dz_ref, dr_scr):
        @pl.when(pl.program_id(0) == 0)
        def _():
            dr_scr[...] = jnp.zeros_like(dr_scr)

        dr = dr_scr[...]
        consts = dict(dintra=[di_ref[h] for h in range(NH)], kw=[kw_ref[h] for h in range(NH)],
                      qw=[qw_ref[h] for h in range(NH)], g_chunk=g_chunk)
        for i in reversed(range(nb)):
            rows = slice(i * CH, (i + 1) * CH)
            rprev_v = rp_ref[i]
            f = functools.partial(_ret_chunk, cos=c_ref[rows, :], sin=s_ref[rows, :], **consts)
            _, vjp = jax.vjp(f, _heads(q_ref[rows, :]), _heads(k_ref[rows, :]), _heads(v_ref[rows, :]),
                             [rprev_v[h * HD:(h + 1) * HD] for h in range(NH)])
            dq, dk, dv, drp = vjp((_heads(do_ref[rows, :]), [dr[h * HD:(h + 1) * HD] for h in range(NH)]))
            dz_ref[rows, :] = jnp.concatenate(dq + dk + dv, axis=1).astype(dz_ref.dtype)
            dr = jnp.concatenate(drp, axis=0)
        dr_scr[...] = dr

    col = lambda c: pl.BlockSpec((rb, BW), lambda i: (steps - 1 - i, c // BW))
    tab = pl.BlockSpec((rb, HD), lambda i: (steps - 1 - i, 0))
    cst = lambda shp: pl.BlockSpec(shp, lambda i: (0,) * len(shp))
    return pl.pallas_call(
        body, grid=(steps,),
        in_specs=[col(RQ), col(RK), col(RV), tab, tab, cst((NH, CH, CH)), cst((NH, CH, HD)), cst((NH, CH, HD)),
                  pl.BlockSpec((nb, BW, HD), lambda i: (steps - 1 - i, 0, 0)),
                  pl.BlockSpec((rb, BW), lambda i: (steps - 1 - i, 0)), pl.BlockSpec(memory_space=pl.ANY)],
        out_specs=pl.BlockSpec((rb, 3 * BW), lambda i: (steps - 1 - i, RQ // (3 * BW))),
        out_shape=_sds(dzz.shape, dzz.dtype), input_output_aliases={10: 0},
        scratch_shapes=[pltpu.VMEM((BW, HD), F32)],
        compiler_params=_cparams(("arbitrary",)), name=name)(zz, zz, zz, cosf, sinf, dintra, kw, qw, rprev, do, dzz)


GKW = NH * GDK


def _gla_consts():
    tri = np.tril(np.ones((CH, CH), np.float32))
    mask_t = np.zeros((BW, GKW), np.float32)
    for h in range(NH):
        mask_t[h * HD:(h + 1) * HD, h * GDK:(h + 1) * GDK] = 1.0
    return jnp.asarray(tri), jnp.asarray(mask_t)


def _gla_chunk(q, k, v, glr, w_a2, b_a, st, tri, mask_t):
    la = _log_sigmoid(_bdot(glr, w_a2) + b_a) * (1.0 / 16.0)
    bc = _dot(tri, la, HI)
    be = jnp.sum(la, axis=0, keepdims=True)
    kv_t = _bdot_tn(v, k * jnp.exp(be - bc)) * mask_t
    sn = jnp.exp(be) * st + kv_t
    return _bdot_nt(q * (GDK ** -0.5), sn), sn


def gla_fwd(zz, w_a2p, b_a, *, name):
    s = zz.shape[0]
    n = s // CH
    nb = _chunks_per_step(n)
    rb = nb * CH
    tri, mask_t = _gla_consts()

    def body(q_ref, k_ref, v_ref, lr_ref, w_ref, b_ref, tri_ref, m_ref, o_ref, sp_ref, st_scr):
        @pl.when(pl.program_id(0) == 0)
        def _():
            st_scr[...] = jnp.zeros_like(st_scr)

        st = st_scr[...]
        for i in range(nb):
            rows = slice(i * CH, (i + 1) * CH)
            sp_ref[i] = st
            o_ref[rows, :], st = _gla_chunk(q_ref[rows, :], k_ref[rows, :], v_ref[rows, :], lr_ref[rows, :], w_ref[...],
                                            b_ref[...], st, tri_ref[...], m_ref[...])
        st_scr[...] = st

    cst = lambda shp: pl.BlockSpec(shp, lambda i: (0,) * len(shp))
    return pl.pallas_call(
        body, grid=(n // nb,),
        in_specs=[pl.BlockSpec((rb, GKW), lambda i: (i, GQ // GKW)), pl.BlockSpec((rb, GKW), lambda i: (i, GK // GKW)),
                  pl.BlockSpec((rb, BW), lambda i: (i, GV // BW)), pl.BlockSpec((rb, HD), lambda i: (i, LR // HD)),
                  cst((HD, GKW)), cst((1, GKW)), cst((CH, CH)), cst((BW, GKW))],
        out_specs=[pl.BlockSpec((rb, BW), lambda i: (i, 0)), pl.BlockSpec((nb, BW, GKW), lambda i: (i, 0, 0))],
        out_shape=[_sds((s, BW), F32), _sds((n, BW, GKW), F32)],
        scratch_shapes=[pltpu.VMEM((BW, GKW), F32)],
        compiler_params=_cparams(("arbitrary",)), name=name)(zz, zz, zz, zz, w_a2p, b_a, tri, mask_t)


def gla_bwd(zz, w_a2p, b_a, sprev, do, dzz, *, name):
    s = zz.shape[0]
    n = s // CH
    nb = _chunks_per_step(n)
    rb, steps = nb * CH, n // nb
    tri, mask_t = _gla_consts()

    def body(q_ref, k_ref, v_ref, lr_ref, w_ref, b_ref, tri_ref, m_ref, sp_ref, do_ref, dzz_ref,
             dz_ref, dlr_ref, dw_ref, db_ref, ds_scr):
        @pl.when(pl.program_id(0) == 0)
        def _():
            ds_scr[...] = jnp.zeros_like(ds_scr)
            dw_ref[...] = jnp.zeros_like(dw_ref)
            db_ref[...] = jnp.zeros_like(db_ref)

        f = functools.partial(_gla_chunk, tri=tri_ref[...], mask_t=m_ref[...])
        ds, dw_sum, db_sum = ds_scr[...], jnp.zeros(dw_ref.shape, F32), jnp.zeros(db_ref.shape, F32)
        for i in reversed(range(nb)):
            rows = slice(i * CH, (i + 1) * CH)
            _, vjp = jax.vjp(f, q_ref[rows, :], k_ref[rows, :], v_ref[rows, :], lr_ref[rows, :], w_ref[...], b_ref[...],
                             sp_ref[i])
            dq, dk, dv, dlr, dw, db, ds = vjp((do_ref[rows, :], ds))
            dz_ref[rows, :] = jnp.concatenate([dq, dk, dv], axis=1).astype(dz_ref.dtype)
            dlr_ref[rows, :] = dlr.astype(dlr_ref.dtype)
            dw_sum, db_sum = dw_sum + dw, db_sum + db
        dw_ref[...] += dw_sum
        db_ref[...] += db_sum
        ds_scr[...] = ds

    cst = lambda shp: pl.BlockSpec(shp, lambda i: (0,) * len(shp))
    r = lambda i: steps - 1 - i
    return pl.pallas_call(
        body, grid=(steps,),
        in_specs=[pl.BlockSpec((rb, GKW), lambda i: (r(i), GQ // GKW)), pl.BlockSpec((rb, GKW), lambda i: (r(i), GK // GKW)),
                  pl.BlockSpec((rb, BW), lambda i: (r(i), GV // BW)), pl.BlockSpec((rb, HD), lambda i: (r(i), LR // HD)),
                  cst((HD, GKW)), cst((1, GKW)), cst((CH, CH)), cst((BW, GKW)),
                  pl.BlockSpec((nb, BW, GKW), lambda i: (r(i), 0, 0)), pl.BlockSpec((rb, BW), lambda i: (r(i), 0)),
                  pl.BlockSpec(memory_space=pl.ANY)],
        out_specs=[pl.BlockSpec((rb, 2 * GKW + BW), lambda i: (r(i), GQ // (2 * GKW + BW))),
                   pl.BlockSpec((rb, HD), lambda i: (r(i), 0)), cst((HD, GKW)), cst((1, GKW))],
        out_shape=[_sds(dzz.shape, dzz.dtype), _sds((s, HD), MMT), _sds((HD, GKW), F32), _sds((1, GKW), F32)],
        input_output_aliases={10: 0}, scratch_shapes=[pltpu.VMEM((BW, GKW), F32)],
        compiler_params=_cparams(("arbitrary",)), name=name)(zz, zz, zz, zz, w_a2p, b_a, tri, mask_t, sprev, do, dzz)


def _fox_pre_f(fqs, fks, ff, gq, gk, bf):
    def rms(x, g):
        return x * lax.rsqrt(jnp.mean(x * x, axis=-1, keepdims=True) + EPS) * g

    qn = [rms(x, gq) * (HD ** -0.5) for x in fqs]
    kn = [rms(x, gk) for x in fks]
    return qn, kn, _log_sigmoid(ff + bf)


def fox_pre(zz, gq, gk, bf, *, name):
    s = zz.shape[0]
    t = _row_tile(s)
    tri = jnp.asarray(np.tril(np.ones((t, t), np.float32)))

    def body(q_ref, k_ref, f_ref, gq_ref, gk_ref, b_ref, tri_ref, qn_ref, kn_ref, cum_ref, carry):
        @pl.when(pl.program_id(0) == 0)
        def _():
            carry[...] = jnp.zeros_like(carry)

        qn, kn, lf = _fox_pre_f(_heads(q_ref[...]), _heads(k_ref[...]), f_ref[...], gq_ref[...], gk_ref[...], b_ref[...])
        qn_ref[...] = jnp.concatenate(qn, axis=1).astype(qn_ref.dtype)
        kn_ref[...] = jnp.concatenate(kn, axis=1).astype(kn_ref.dtype)
        cum_ref[...] = _dot(tri_ref[...], lf, HI) + carry[...]
        carry[...] += jnp.sum(lf, axis=0, keepdims=True)

    vec = pl.BlockSpec((1, HD), lambda i: (0, 0))
    return pl.pallas_call(
        body, grid=(s // t,),
        in_specs=[pl.BlockSpec((t, BW), lambda i: (i, FQ // BW)), pl.BlockSpec((t, BW), lambda i: (i, FK // BW)),
                  pl.BlockSpec((t, HD), lambda i: (i, FF // HD)), vec, vec, vec, pl.BlockSpec((t, t), lambda i: (0, 0))],
        out_specs=[pl.BlockSpec((t, BW), lambda i: (i, 0)), pl.BlockSpec((t, BW), lambda i: (i, 0)),
                   pl.BlockSpec((t, HD), lambda i: (i, 0))],
        out_shape=[_sds((s, BW), MMT), _sds((s, BW), MMT), _sds((s, HD), F32)],
        scratch_shapes=[pltpu.VMEM((1, HD), F32)],
        compiler_params=_cparams(("arbitrary",)), name=name)(zz, zz, zz, gq, gk, bf, tri)


def fox_pre_bwd(zz, gq, gk, bf, dqn, dkn, dcum, dzz, *, name):
    s = zz.shape[0]
    t = _row_tile(s)
    nt = s // t
    triu = jnp.asarray(np.triu(np.ones((t, t), np.float32)))

    def body(q_ref, k_ref, f_ref, gq_ref, gk_ref, b_ref, tri_ref, dqn_ref, dkn_ref, dcum_ref, dzz_ref,
             dz_ref, dff_ref, dgq_ref, dgk_ref, db_ref, carry):
        @pl.when(pl.program_id(0) == 0)
        def _():
            carry[...] = jnp.zeros_like(carry)
            dgq_ref[...] = jnp.zeros_like(dgq_ref)
            dgk_ref[...] = jnp.zeros_like(dgk_ref)
            db_ref[...] = jnp.zeros_like(db_ref)

        dcum_v = dcum_ref[...]
        dlf = _dot(tri_ref[...], dcum_v, HI) + carry[...]
        carry[...] += jnp.sum(dcum_v, axis=0, keepdims=True)
        _, vjp = jax.vjp(_fox_pre_f, _heads(q_ref[...]), _heads(k_ref[...]), f_ref[...], gq_ref[...], gk_ref[...], b_ref[...])
        dq, dk, dff, dgq, dgk, db = vjp((_heads(dqn_ref[...]), _heads(dkn_ref[...]), dlf))
        dz_ref[...] = jnp.concatenate(dq + dk, axis=1).astype(dz_ref.dtype)
        dff_ref[...] = dff.astype(dff_ref.dtype)
        dgq_ref[...] += dgq
        dgk_ref[...] += dgk
        db_ref[...] += db

    r = lambda i: nt - 1 - i
    vec = pl.BlockSpec((1, HD), lambda i: (0, 0))
    return pl.pallas_call(
        body, grid=(nt,),
        in_specs=[pl.BlockSpec((t, BW), lambda i: (r(i), FQ // BW)), pl.BlockSpec((t, BW), lambda i: (r(i), FK // BW)),
                  pl.BlockSpec((t, HD), lambda i: (r(i), FF // HD)), vec, vec, vec, pl.BlockSpec((t, t), lambda i: (0, 0)),
                  pl.BlockSpec((t, BW), lambda i: (r(i), 0)), pl.BlockSpec((t, BW), lambda i: (r(i), 0)),
                  pl.BlockSpec((t, HD), lambda i: (r(i), 0)), pl.BlockSpec(memory_space=pl.ANY)],
        out_specs=[pl.BlockSpec((t, 2 * BW), lambda i: (r(i), FQ // (2 * BW))), pl.BlockSpec((t, HD), lambda i: (r(i), 0)),
                   vec, vec, vec],
        out_shape=[_sds(dzz.shape, dzz.dtype), _sds((s, HD), MMT), _sds((1, HD), F32), _sds((1, HD), F32), _sds((1, HD), F32)],
        input_output_aliases={10: 0}, scratch_shapes=[pltpu.VMEM((1, HD), F32)],
        compiler_params=_cparams(("arbitrary",)), name=name)(zz, zz, zz, gq, gk, bf, triu, dqn, dkn, dcum, dzz)


def _fox_blocks(s):
    return min(256, s), min(512, s)


NEG = -1e30


def fox_attn_fwd(qn, kn, zz, cum_col, cum_row, *, name):
    s = qn.shape[0]
    bq, bk = _fox_blocks(s)

    def body(q_ref, k_ref, v_ref, cc_ref, cr_ref, o_ref, lse_ref):
        qi = pl.program_id(1)
        q = q_ref[...]
        cq = cc_ref[...]
        rows = qi * bq + lax.broadcasted_iota(jnp.int32, (bq, bk), 0)
        cols0 = lax.broadcasted_iota(jnp.int32, (bq, bk), 1)

        def step(j, carry, on_diagonal):
            m, l, acc = carry
            off = pl.multiple_of(j * bk, bk)
            k = k_ref[pl.ds(off, bk), :]
            v = v_ref[pl.ds(off, bk), :].astype(MMT)
            sc = _dot_nt(q, k) + cq - cr_ref[pl.ds(j, 1), :]
            if on_diagonal:
                sc = jnp.where(rows >= cols0 + j * bk, sc, NEG)
            m_new = jnp.maximum(m, jnp.max(sc, axis=1, keepdims=True))
            alpha = jnp.exp(m - m_new)
            p = jnp.exp(sc - m_new)
            return m_new, alpha * l + jnp.sum(p, axis=1, keepdims=True), alpha * acc + _dot(p.astype(MMT), v)

        nfull, nk = (qi * bq + 1) // bk, ((qi + 1) * bq + bk - 1) // bk
        carry = (jnp.full((bq, 1), NEG, F32), jnp.zeros((bq, 1), F32), jnp.zeros((bq, HD), F32))
        carry = lax.fori_loop(0, nfull, functools.partial(step, on_diagonal=False), carry)
        m, l, acc = lax.fori_loop(nfull, nk, functools.partial(step, on_diagonal=True), carry)
        o_ref[...] = acc / l
        lse_ref[...] = m + jnp.log(l)

    return pl.pallas_call(
        body, grid=(NH, s // bq),
        in_specs=[pl.BlockSpec((bq, HD), lambda h, i: (i, h)), pl.BlockSpec((s, HD), lambda h, i: (0, h)),
                  pl.BlockSpec((s, HD), lambda h, i: (0, FV // HD + h)),
                  pl.BlockSpec((None, bq, 1), lambda h, i: (h, i, 0)), pl.BlockSpec((None, s // bk, bk), lambda h, i: (h, 0, 0))],
        out_specs=[pl.BlockSpec((bq, HD), lambda h, i: (i, h)), pl.BlockSpec((None, bq, 1), lambda h, i: (h, i, 0))],
        out_shape=[_sds((s, BW), F32), _sds((NH, s, 1), F32)],
        compiler_params=_cparams(("parallel", "parallel")), name=name)(qn, kn, zz, cum_col, cum_row)


def fox_attn_bwd(qn, kn, zz, cum_col, cum_row, lse, do, dzz, *, name):
    s = qn.shape[0]
    bq, bk = _fox_blocks(s)
    nkc = s // bk

    def body(q_ref, k_ref, v_ref, cc_ref, cr_ref, lse_ref, do_ref, dzz_ref, dq_ref, dk_ref, dv_ref, dc_ref,
             p_scr, dp_scr, dv_scr):
        qi = pl.program_id(1)

        @pl.when(qi == 0)
        def _():
            dk_ref[...] = jnp.zeros_like(dk_ref)
            dv_scr[...] = jnp.zeros_like(dv_scr)
            dc_ref[...] = jnp.zeros_like(dc_ref)

        q = q_ref[...]
        dob = do_ref[...].astype(MMT)
        cq = cc_ref[...]
        lse_v = lse_ref[...]
        rows = qi * bq + lax.broadcasted_iota(jnp.int32, (bq, bk), 0)
        cols0 = lax.broadcasted_iota(jnp.int32, (bq, bk), 1)
        nfull, nk = (qi * bq + 1) // bk, ((qi + 1) * bq + bk - 1) // bk

        def probs(j, delta, on_diagonal):
            off = pl.multiple_of(j * bk, bk)
            sc = _dot_nt(q, k_ref[pl.ds(off, bk), :]) + cq - cr_ref[pl.ds(j, 1), :]
            p = jnp.exp(sc - lse_v)
            if on_diagonal:
                p = jnp.where(rows >= cols0 + j * bk, p, 0.0)
            dp = _dot_nt(dob, v_ref[pl.ds(off, bk), :].astype(MMT))
            p_scr[j] = p
            dp_scr[j] = dp
            return delta + jnp.sum(p * dp, axis=1, keepdims=True)

        delta = lax.fori_loop(0, nfull, functools.partial(probs, on_diagonal=False), jnp.zeros((bq, 1), F32))
        delta = lax.fori_loop(nfull, nk, functools.partial(probs, on_diagonal=True), delta)

        def grads(j, dq):
            off = pl.multiple_of(j * bk, bk)
            p = p_scr[j]
            ds = p * (dp_scr[j] - delta)
            dsm = ds.astype(MMT)
            dv_scr[pl.ds(off, bk), :] += _dot_tn(p.astype(MMT), dob)
            dk_ref[pl.ds(off, bk), :] += _dot_tn(dsm, q)
            dc_ref[pl.ds(j, 1), :] -= jnp.sum(ds, axis=0, keepdims=True)
            return dq + _dot(dsm, k_ref[pl.ds(off, bk), :])

        dq_ref[...] = lax.fori_loop(0, nk, grads, jnp.zeros((bq, HD), F32))

        @pl.when(qi == pl.num_programs(1) - 1)
        def _():
            dv_ref[...] = dv_scr[...].astype(dv_ref.dtype)

    full = lambda c0=0: pl.BlockSpec((s, HD), lambda h, i: (0, c0 + h))
    blk = lambda: pl.BlockSpec((bq, HD), lambda h, i: (i, h))
    colv = lambda: pl.BlockSpec((None, bq, 1), lambda h, i: (h, i, 0))
    rowv = lambda: pl.BlockSpec((None, nkc, bk), lambda h, i: (h, 0, 0))
    return pl.pallas_call(
        body, grid=(NH, s // bq),
        in_specs=[blk(), full(), full(FV // HD), colv(), rowv(), colv(), blk(), pl.BlockSpec(memory_space=pl.ANY)],
        out_specs=[blk(), full(), full(FV // HD), rowv()],
        out_shape=[_sds((s, BW), F32), _sds((s, BW), F32), _sds(dzz.shape, dzz.dtype), _sds((NH, nkc, bk), F32)],
        input_output_aliases={7: 2},
        scratch_shapes=[pltpu.VMEM((nkc, bq, bk), F32), pltpu.VMEM((nkc, bq, bk), F32), pltpu.VMEM((s, HD), F32)],
        compiler_params=_cparams(("parallel", "arbitrary")), name=name)(qn, kn, zz, cum_col, cum_row, lse, do, dzz)


def _branch_f(rets, rgs, glas, ggs, ret_g, gla_g):
    out_r, out_g = [], []
    for h in range(NH):
        xc = rets[h] - jnp.mean(rets[h], axis=-1, keepdims=True)
        y = xc * lax.rsqrt(jnp.mean(xc * xc, axis=-1, keepdims=True) + EPS) * ret_g[h]
        out_r.append(_silu(rgs[h]) * y)
        x = glas[h]
        y = x * lax.rsqrt(jnp.mean(x * x, axis=-1, keepdims=True) + EPS) * gla_g
        out_g.append(_silu(ggs[h]) * y)
    return out_r, out_g


def _w_br_spec(layer):
    return pl.BlockSpec((None, 3, BW, D), lambda i: (layer, 0, 0, 0))


def mix_fwd(ret_raw, gla_raw, fox_o, zz, ret_g, gla_g, b_mg, w_br, *, name, layer):
    s = zz.shape[0]
    t = _row_tile(s)

    def body(r_ref, g_ref, f_ref, rg_ref, gg_ref, gp_ref, rgn_ref, ggn_ref, bmg_ref, w_ref, o_ref):
        rgn = rgn_ref[...]
        br_r, br_g = _branch_f(_heads(r_ref[...]), _heads(rg_ref[...]), _heads(g_ref[...]), _heads(gg_ref[...]),
                               _heads(rgn), ggn_ref[...])
        brs = [jnp.concatenate(br_r, axis=1), jnp.concatenate(br_g, axis=1), f_ref[...]]
        acc = jnp.zeros((t, D), F32)
        for b in range(3):
            gate = jax.nn.sigmoid(gp_ref[:, b * D:(b + 1) * D] + bmg_ref[:, b * D:(b + 1) * D])
            acc = acc + gate * _dot(brs[b].astype(MMT), w_ref[b])
        o_ref[...] = acc.astype(o_ref.dtype)

    row = lambda w, c=0: pl.BlockSpec((t, w), lambda i: (i, c // w))
    cst = lambda shp: pl.BlockSpec(shp, lambda i: (0,) * len(shp))
    return pl.pallas_call(
        body, grid=(s // t,),
        in_specs=[row(BW), row(BW), row(BW), row(BW, RG), row(BW, GG), row(3 * D, GP), cst((1, BW)), cst((1, HD)),
                  cst((1, 3 * D)), _w_br_spec(layer)],
        out_specs=row(D), out_shape=_sds((s, D), MMT),
        compiler_params=_cparams(("parallel",)), name=name)(ret_raw, gla_raw, fox_o, zz, zz, zz, ret_g, gla_g, b_mg, w_br)


def mix_bwd(ret_raw, gla_raw, fox_o, zz, ret_g, gla_g, b_mg, w_br, dmi, *, name, layer):
    s = zz.shape[0]
    t = _row_tile(s)

    def body(r_ref, g_ref, f_ref, rg_ref, gg_ref, gp_ref, rgn_ref, ggn_ref, bmg_ref, w_ref, dmi_ref,
             dr_ref, dg_ref, df_ref, dgp_ref, dw_ref, drgn_ref, dggn_ref, dbmg_ref):
        @pl.when(pl.program_id(0) == 0)
        def _():
            dw_ref[...] = jnp.zeros_like(dw_ref)
            drgn_ref[...] = jnp.zeros_like(drgn_ref)
            dggn_ref[...] = jnp.zeros_like(dggn_ref)
            dbmg_ref[...] = jnp.zeros_like(dbmg_ref)

        (br_r, br_g), vjp = jax.vjp(_branch_f, _heads(r_ref[...]), _heads(rg_ref[...]), _heads(g_ref[...]),
                                    _heads(gg_ref[...]), _heads(rgn_ref[...]), ggn_ref[...])
        brs = [jnp.concatenate(br_r, axis=1).astype(MMT), jnp.concatenate(br_g, axis=1).astype(MMT),
               f_ref[...].astype(MMT)]
        dmi_v = dmi_ref[...].astype(F32)
        dbr = []
        for b in range(3):
            w = w_ref[b]
            ybr = _dot(brs[b], w)
            gate = jax.nn.sigmoid(gp_ref[:, b * D:(b + 1) * D] + bmg_ref[:, b * D:(b + 1) * D])
            dgp = dmi_v * ybr * gate * (1.0 - gate)
            dgp_ref[:, b * D:(b + 1) * D] = dgp.astype(dgp_ref.dtype)
            dbmg_ref[:, b * D:(b + 1) * D] += jnp.sum(dgp, axis=0, keepdims=True)
            dy = (dmi_v * gate).astype(MMT)
            dw_ref[b] += _dot_tn(brs[b], dy)
            dbr.append(_dot_nt(dy, w))
        dr, drg, dg, dgg, drgn, dggn = vjp((_heads(dbr[0]), _heads(dbr[1])))
        dr_ref[...] = jnp.concatenate(dr, axis=1)
        dg_ref[...] = jnp.concatenate(dg, axis=1)
        df_ref[...] = dbr[2]
        dgp_ref[:, RG:RG + BW] = jnp.concatenate(drg, axis=1).astype(dgp_ref.dtype)
        dgp_ref[:, GG:GG + BW] = jnp.concatenate(dgg, axis=1).astype(dgp_ref.dtype)
        drgn_ref[...] += jnp.concatenate(drgn, axis=1)
        dggn_ref[...] += dggn

    row = lambda w, c=0: pl.BlockSpec((t, w), lambda i: (i, c // w))
    cst = lambda shp: pl.BlockSpec(shp, lambda i: (0,) * len(shp))
    return pl.pallas_call(
        body, grid=(s // t,),
        in_specs=[row(BW), row(BW), row(BW), row(BW, RG), row(BW, GG), row(3 * D, GP), cst((1, BW)), cst((1, HD)),
                  cst((1, 3 * D)), _w_br_spec(layer), row(D)],
        out_specs=[row(BW), row(BW), row(BW), row(FV), cst((3, BW, D)), cst((1, BW)), cst((1, HD)), cst((1, 3 * D))],
        out_shape=[_sds((s, BW), F32)] * 3 + [_sds((s, NZZ), MMT), _sds((3, BW, D), F32), _sds((1, BW), F32),
                                              _sds((1, HD), F32), _sds((1, 3 * D), F32)],
        compiler_params=_cparams(("arbitrary",)), name=name)(ret_raw, gla_raw, fox_o, zz, zz, zz, ret_g, gla_g, b_mg, w_br, dmi)


CT = 256


def _shift_down(x, k, rows):
    return jnp.where(rows >= k, pltpu.roll(x, k, 0), 0.0)


def _shift_up(x, k, rows, s):
    return jnp.where(rows < s - k, pltpu.roll(x, s - k, 0), 0.0)


def conv_fwd(ug, w_conv, b_conv, *, name):
    s = ug.shape[0]
    nt = DFF // CT

    def body(u_ref, g_ref, w_ref, b_ref, a_ref):
        u = u_ref[...]
        rows = lax.broadcasted_iota(jnp.int32, u.shape, 0)
        uc = b_ref[...] + w_ref[0:1, :] * _shift_down(u, 2, rows) + w_ref[1:2, :] * _shift_down(u, 1, rows) + w_ref[2:3, :] * u
        a_ref[...] = (_silu(uc) * g_ref[...]).astype(a_ref.dtype)

    return pl.pallas_call(
        body, grid=(nt,),
        in_specs=[pl.BlockSpec((s, CT), lambda j: (0, j)), pl.BlockSpec((s, CT), lambda j: (0, nt + j)),
                  pl.BlockSpec((3, CT), lambda j: (0, j)), pl.BlockSpec((1, CT), lambda j: (0, j))],
        out_specs=pl.BlockSpec((s, CT), lambda j: (0, j)), out_shape=_sds((s, DFF), MMT),
        compiler_params=_cparams(("parallel",)), name=name)(ug, ug, w_conv, b_conv)


def conv_bwd(ug, w_conv, b_conv, da, *, name):
    s = ug.shape[0]
    nt = DFF // CT

    def body(u_ref, g_ref, w_ref, b_ref, da_ref, du_ref, dg_ref, dw_ref, db_ref):
        u = u_ref[...]
        rows = lax.broadcasted_iota(jnp.int32, u.shape, 0)
        u2, u1 = _shift_down(u, 2, rows), _shift_down(u, 1, rows)
        uc = b_ref[...] + w_ref[0:1, :] * u2 + w_ref[1:2, :] * u1 + w_ref[2:3, :] * u
        sg = jax.nn.sigmoid(uc)
        da_v = da_ref[...]
        dg_ref[...] = (da_v * uc * sg).astype(dg_ref.dtype)
        duc = da_v * g_ref[...] * sg * (1.0 + uc * (1.0 - sg))
        du = w_ref[2:3, :] * duc + w_ref[1:2, :] * _shift_up(duc, 1, rows, s) + w_ref[0:1, :] * _shift_up(duc, 2, rows, s)
        du_ref[...] = du.astype(du_ref.dtype)
        dw_ref[0:1, :] = jnp.sum(duc * u2, axis=0, keepdims=True)
        dw_ref[1:2, :] = jnp.sum(duc * u1, axis=0, keepdims=True)
        dw_ref[2:3, :] = jnp.sum(duc * u, axis=0, keepdims=True)
        db_ref[...] = jnp.sum(duc, axis=0, keepdims=True)

    col = lambda: pl.BlockSpec((s, CT), lambda j: (0, j))
    return pl.pallas_call(
        body, grid=(nt,),
        in_specs=[col(), pl.BlockSpec((s, CT), lambda j: (0, nt + j)), pl.BlockSpec((3, CT), lambda j: (0, j)),
                  pl.BlockSpec((1, CT), lambda j: (0, j)), col()],
        out_specs=[col(), col(), pl.BlockSpec((3, CT), lambda j: (0, j)), pl.BlockSpec((1, CT), lambda j: (0, j))],
        out_shape=[_sds((s, DFF), MMT), _sds((s, DFF), MMT), _sds((3, DFF), F32), _sds((1, DFF), F32)],
        compiler_params=_cparams(("parallel",)), name=name)(ug, ug, w_conv, b_conv, da)


def place_tail(dzz, dlr, dff, *, name):
    s = dzz.shape[0]
    t = _row_tile(s)

    def body(a_ref, b_ref, z_ref, o_ref):
        o_ref[...] = jnp.concatenate([a_ref[...], b_ref[...]], axis=1)

    spec = pl.BlockSpec((t, HD), lambda i: (i, 0))
    return pl.pallas_call(
        body, grid=(s // t,), in_specs=[spec, spec, pl.BlockSpec(memory_space=pl.ANY)],
        out_specs=pl.BlockSpec((t, 2 * HD), lambda i: (i, LR // (2 * HD))), out_shape=_sds(dzz.shape, dzz.dtype),
        input_output_aliases={2: 0}, compiler_params=_cparams(("parallel",)), name=name)(dlr, dff, dzz)


def _tiles(s):
    return min(1024, s)


def layer_fwd(x, mod, p, cosf, sinf):
    s = x.shape[0]
    tm = _tiles(s)
    l = p["l"]
    shift1, scale1, gate1, shift2, scale2, gate2 = mod
    h = norm_mod(x, p["norm1_g"], scale1, shift1, name="norm_mod")
    zz = mm_nn(h, p["w1"], tm=tm, tn=768, out_dtype=F32, name="mm_w1", layer=l)
    ret_raw, rprev = retention_fwd(zz, cosf, sinf, name="ret_fwd")
    gla_raw, sprev = gla_fwd(zz, p["w_a2p"], p["b_gla_a"], name="gla_fwd")
    qn, kn, cum = fox_pre(zz, p["q_norm_g"], p["k_norm_g"], p["b_foxp"], name="fox_pre")
    bq, bk = _fox_blocks(s)
    cum_t = cum[:, :NH].T
    cum_col, cum_row = cum_t[:, :, None], cum_t.reshape(NH, s // bk, bk)
    fox_o, lse = fox_attn_fwd(qn, kn, zz, cum_col, cum_row, name="fox_fwd")
    if "later" in p:
        p = {**p, **p["later"](fox_o)}
    mi = mix_fwd(ret_raw, gla_raw, fox_o, zz, p["ret_norm_g"], p["gla_norm_g"], p["b_mg"], p["w_br"], name="mix_fwd",
                 layer=l)
    x1, mixed = mm_nn_residual(mi, p["w_o"], x, gate1, tm=tm, tn=512, name="mm_wo", layer=l)
    h2 = norm_mod(x1, p["norm2_g"], scale2, shift2, name="norm_mod")
    ug = mm_nn(h2, p["w_up"], tm=tm, tn=512, out_dtype=F32, name="mm_wup", layer=l)
    b_conv = p["b_conv"]
    if "late_in_layer" in p:
        b_conv = b_conv + p["late_in_layer"](ug)[0, 0]
    a = conv_fwd(ug, p["w_conv"], b_conv, name="conv_fwd")
    x2, y = mm_nn_residual(a, p["w_down"], x1, gate2, tm=tm, tn=512, name="mm_wdown", layer=l)
    saved = dict(x=x, h=h, zz=zz, ret_raw=ret_raw, rprev=rprev, gla_raw=gla_raw, sprev=sprev, qn=qn, kn=kn,
                 cum_col=cum_col, cum_row=cum_row, fox_o=fox_o, lse=lse, mi=mi, mixed=mixed, x1=x1, h2=h2, ug=ug, a=a, y=y)
    return x2, saved, p


def layer_bwd(dx2, mod, p, sv, cosf, sinf, stacks, slot):
    s = dx2.shape[0]
    tm = _tiles(s)
    l = p["l"]
    shift1, scale1, gate1, shift2, scale2, gate2 = mod
    g, stacks = {}, dict(stacks)
    dy, dgate2 = gate_bwd(dx2, sv["y"], gate2, name="gate_bwd")
    stacks["w_down"] = mm_tn(sv["a"], dy, tm=min(1408, DFF), tn=512, out_dtype=MMT, name="mm_dwdown",
                             stack=stacks["w_down"], layer=slot)
    da = mm_nt(dy, p["w_down"], tm=tm, tn=1408, out_dtype=F32, name="mm_da", layer=l)
    du, dg, g["w_conv"], g["b_conv"] = conv_bwd(sv["ug"], p["w_conv"], p["b_conv"], da, name="conv_bwd")
    stacks["w_up"] = mm_tn(sv["h2"], du, tm=D, tn=CT, out_dtype=MMT, name="mm_dwup_u", stack=stacks["w_up"], layer=slot)
    stacks["w_up"] = mm_tn(sv["h2"], dg, tm=D, tn=CT, out_dtype=MMT, name="mm_dwup_g", stack=stacks["w_up"], layer=slot,
                           out_col0=DFF)
    dh2 = mm_nt2(du, dg, p["w_up"], tm=min(512, s), tn=D, name="mm_dh2", layer=l)
    dx1, g["norm2_g"], dscale2, dshift2 = norm_mod_bwd(sv["x1"], dh2, dx2, p["norm2_g"], scale2, shift2, name="norm_mod_bwd")
    dmixed, dgate1 = gate_bwd(dx1, sv["mixed"], gate1, name="gate_bwd")
    stacks["w_o"] = mm_tn(sv["mi"], dmixed, tm=512, tn=512, out_dtype=MMT, name="mm_dwo", stack=stacks["w_o"], layer=slot)
    dmi = mm_nt(dmixed, p["w_o"], tm=tm, tn=512, out_dtype=MMT, name="mm_dmi", layer=l)
    zz = sv["zz"]
    (dret, dgla, dfox, dzz, g["w_br"], g["ret_norm_g"], g["gla_norm_g"], g["b_mg"]) = mix_bwd(
        sv["ret_raw"], sv["gla_raw"], sv["fox_o"], zz, p["ret_norm_g"], p["gla_norm_g"], p["b_mg"], p["w_br"], dmi,
        name="mix_bwd", layer=l)
    dqn, dkn, dzz, dcum_row = fox_attn_bwd(sv["qn"], sv["kn"], zz, sv["cum_col"], sv["cum_row"], sv["lse"], dfox, dzz,
                                           name="fox_bwd")
    dcum = jnp.pad(dcum_row.reshape(NH, s).T, ((0, 0), (0, HD - NH)))
    dzz, dff, g["q_norm_g"], g["k_norm_g"], g["b_foxp"] = fox_pre_bwd(
        zz, p["q_norm_g"], p["k_norm_g"], p["b_foxp"], dqn, dkn, dcum, dzz, name="fox_pre_bwd")
    dzz, dlr, g["w_a2p"], g["b_gla_a"] = gla_bwd(zz, p["w_a2p"], p["b_gla_a"], sv["sprev"], dgla, dzz, name="gla_bwd")
    dzz = retention_bwd(zz, cosf, sinf, sv["rprev"], dret, dzz, name="ret_bwd")
    dzz = place_tail(dzz, dlr, dff, name="place_tail")
    stacks["w_mg"] = mm_tn(sv["h"], dzz, tm=512, tn=768, out_dtype=MMT, name="mm_dwmg", ncols=WZ0, stack=stacks["w_mg"],
                           layer=slot)
    dwz = mm_tn(sv["h"], dzz, tm=512, tn=768, out_dtype=MMT, name="mm_dwz", col0=WZ0)
    stacks["w_in"] = unalign_dw_in(dwz, stacks["w_in"], slot)
    dh = mm_nt(dzz, p["w1"], tm=min(256, s), tn=D, out_dtype=F32, name="mm_dh", layer=l)
    dx, g["norm1_g"], dscale1, dshift1 = norm_mod_bwd(sv["x"], dh, dx1, p["norm1_g"], scale1, shift1, name="norm_mod_bwd")
    dmod = jnp.concatenate([dshift1, dscale1, dgate1, dshift2, dscale2, dgate2], axis=1)
    return dx, g, dmod, stacks


def _align_cols(w_in, w_mg):
    z = lambda n: jnp.zeros((w_in.shape[0], n), w_in.dtype)
    seg = lambda name: w_in[:, W_IN_COLS[name][0]:W_IN_COLS[name][1]]
    return jnp.concatenate([w_mg, seg("rg"), seg("gg"), seg("fv"), seg("rqkv"), seg("gqkv"), seg("fqk"), seg("lr"),
                            z(HD - GLR), seg("ff"), z(HD - NH)], axis=1)


def _unalign_cols(dwz):
    seg = lambda c0, name: dwz[:, c0 - WZ0:c0 - WZ0 + W_IN_COLS[name][1] - W_IN_COLS[name][0]]
    return jnp.concatenate([seg(RQ, "rqkv"), seg(RG, "rg"), seg(GQ, "gqkv"), seg(LR, "lr"), seg(GG, "gg"), seg(FQ, "fqk"),
                            seg(FV, "fv"), seg(FF, "ff")], axis=1)


def build_w1(w_in_sh, w_mg):
    nl = w_mg.shape[0]
    t = _row_tile(D)

    def body(s_ref, g_ref, o_ref):
        o_ref[...] = _align_cols(jnp.concatenate([s_ref[k] for k in range(4)], axis=1), g_ref[...])

    return pl.pallas_call(
        body, grid=(nl, D // t),
        in_specs=[pl.BlockSpec((None, 4, t, IN_W // 4), lambda l, i: (l, 0, i, 0)), pl.BlockSpec((None, t, WZ0), lambda l, i: (l, i, 0))],
        out_specs=pl.BlockSpec((None, t, NZZ), lambda l, i: (l, i, 0)), out_shape=_sds((nl, D, NZZ), w_mg.dtype),
        compiler_params=_cparams(("parallel", "parallel")), name="build_w1")(w_in_sh, w_mg)


def unalign_dw_in(dwz, stack, layer):
    t = _row_tile(D)

    def body(z_ref, s_ref, o_ref):
        w = _unalign_cols(z_ref[...])
        for k in range(4):
            o_ref[k] = w[:, k * (IN_W // 4):(k + 1) * (IN_W // 4)]

    return pl.pallas_call(
        body, grid=(D // t,),
        in_specs=[pl.BlockSpec((t, NZZ - WZ0), lambda i: (i, 0)), pl.BlockSpec(memory_space=pl.ANY)],
        out_specs=pl.BlockSpec((None, 4, t, IN_W // 4), lambda i: (layer, 0, i, 0)), out_shape=_sds(stack.shape, stack.dtype),
        input_output_aliases={1: 0}, compiler_params=_cparams(("parallel",)), name="unalign_dw_in")(dwz, stack)


LATE_WEIGHTS = ("w_br", "w_o", "w_up", "w_down")


def layer_params(w, big, l, later=None, late_in_layer=None):
    row = lambda v: v[l][None, :]
    p = dict(
        l=0, norm1_g=row(w["norm1_g"]), norm2_g=row(w["norm2_g"]), w1=big["w1"],
        w_a2p=jnp.pad(w["w_gla_a2"][l], ((0, HD - GLR), (0, 0))), b_gla_a=row(w["b_gla_a"]),
        b_foxp=jnp.pad(row(w["b_fox_f"]), ((0, 0), (0, HD - NH))), ret_norm_g=row(w["ret_norm_g"]),
        gla_norm_g=row(w["gla_norm_g"]), q_norm_g=row(w["q_norm_g"]), k_norm_g=row(w["k_norm_g"]),
        b_mg=row(w["b_mg"]), w_conv=w["w_conv"][l], b_conv=row(w["b_conv"]))
    if later is None:
        p.update({n: big[n] for n in LATE_WEIGHTS})
    else:
        p["later"] = later
    if late_in_layer is not None:
        p["late_in_layer"] = late_in_layer
    return p


def layer_grads(g):
    vec = lambda v: v[0]
    return dict(
        norm1_g=vec(g["norm1_g"]), norm2_g=vec(g["norm2_g"]), w_gla_a2=g["w_a2p"][:GLR], b_gla_a=vec(g["b_gla_a"]),
        b_fox_f=g["b_foxp"][0, :NH], ret_norm_g=vec(g["ret_norm_g"]), gla_norm_g=vec(g["gla_norm_g"]),
        q_norm_g=vec(g["q_norm_g"]), k_norm_g=vec(g["k_norm_g"]), w_br=g["w_br"], b_mg=vec(g["b_mg"]),
        w_conv=g["w_conv"], b_conv=vec(g["b_conv"]))


def ada_mod(c_all, w_ada, b_ada):
    nl, _, n = w_ada.shape

    def body(c_ref, w_ref, b_ref, o_ref):
        o_ref[...] = _dot(_silu(c_ref[...]), w_ref[...], HI) + b_ref[...]

    return pl.pallas_call(
        body, grid=(nl,),
        in_specs=[pl.BlockSpec((8, D), lambda l: (0, 0)), pl.BlockSpec((None, D, n), lambda l: (l, 0, 0)),
                  pl.BlockSpec((None, 1, n), lambda l: (l, 0, 0))],
        out_specs=pl.BlockSpec((None, 8, n), lambda l: (l, 0, 0)), out_shape=_sds((nl, 8, n), F32),
        compiler_params=_cparams(("parallel",)), name="ada_mod")(c_all, w_ada, b_ada)


def ada_dw(c_all, dmod):
    nl, _, n = dmod.shape

    def body(c_ref, d_ref, o_ref):
        o_ref[...] = _dot_tn(_silu(c_ref[...]), d_ref[...], HI)

    return pl.pallas_call(
        body, grid=(nl,),
        in_specs=[pl.BlockSpec((8, D), lambda l: (0, 0)), pl.BlockSpec((None, 8, n), lambda l: (l, 0, 0))],
        out_specs=pl.BlockSpec((None, D, n), lambda l: (l, 0, 0)), out_shape=_sds((nl, D, n), F32),
        compiler_params=_cparams(("parallel",)), name="ada_dw")(c_all, dmod)


def sum_devices(g):
    def body(g_ref, o_ref):
        acc = g_ref[0]
        for d in range(1, 8):
            acc = acc + g_ref[d]
        o_ref[...] = acc

    return pl.pallas_call(body, out_shape=_sds(g.shape[1:], F32), name="sum_devices")(g)


def adamw(w, g, m, v, *, block, name, rows=None, into=None, with_grad=False):
    nd = w.ndim
    lo, hi = (0, w.shape[0]) if rows is None else rows
    grid = ((hi - lo) // block[0],) + tuple(w.shape[i] // block[i] for i in range(1, nd))
    first = lo // block[0]
    nout = 4 if with_grad else 3
    bc1 = 1.0 - ADAM_B1 ** ADAM_STEP
    bc2 = 1.0 - ADAM_B2 ** ADAM_STEP

    def body(w_ref, g_ref, m_ref, v_ref, *rest):
        d_ref, nm_ref, nv_ref = rest[-nout:][:3]
        gv = g_ref[...]
        if with_grad:
            rest[-1][...] = gv
        nm = ADAM_B1 * m_ref[...] + (1.0 - ADAM_B1) * gv
        nv = ADAM_B2 * v_ref[...] + (1.0 - ADAM_B2) * (gv * gv)
        nm_ref[...] = nm
        nv_ref[...] = nv
        d_ref[...] = -ADAM_LR * ((nm / bc1) / (jnp.sqrt(nv / bc2) + ADAM_EPS) + ADAM_WD * w_ref[...])

    spec = pl.BlockSpec(tuple(block), lambda i, *j: (first + i,) + j)
    given = [] if into is None else list(into)
    return pl.pallas_call(
        body, grid=grid, in_specs=[spec] * 4 + [pl.BlockSpec(memory_space=pl.ANY)] * len(given), out_specs=[spec] * nout,
        out_shape=[_sds(w.shape, F32)] * nout, input_output_aliases={4 + i: i for i in range(len(given))},
        compiler_params=_cparams(("parallel",) * nd), name=name)(w, g, m, v, *given)


MESH = pl.DeviceIdType.MESH
ANY = pl.BlockSpec(memory_space=pl.ANY)
VM = pl.BlockSpec(memory_space=pltpu.VMEM)


def _place():
    x, y, c = lax.axis_index("x"), lax.axis_index("y"), lax.axis_index("c")
    return x, y, c, [(1 - x, y), (x, 1 - y), (1 - x, 1 - y)]


def small_allgather(v, *, name):
    m_per, n = v.shape

    def body(x_ref, out_ref, send_sems, recv_sems, local_sem):
        x, y, c, chips = _place()
        me, sibling = (x, y, c), (x, y, 1 - c)

        def rows(px, py, pc):
            return out_ref.at[pl.ds((4 * px + 2 * py + pc) * m_per, m_per), :]

        def copy(k, block, to, src=None):
            return pltpu.make_async_remote_copy(
                src_ref=rows(*block) if src is None else src, dst_ref=rows(*block),
                send_sem=send_sems.at[k], recv_sem=recv_sems.at[k], device_id=to, device_id_type=MESH)

        mine = pltpu.make_async_copy(x_ref, rows(*me), local_sem)
        mine.start()
        first = [copy(0, me, sibling, src=x_ref)]
        first += [copy(1 + j, me, (*chip, c), src=x_ref) for j, chip in enumerate(chips)]
        for cp in first:
            cp.start()
        passed = [copy(4 + j, (*chip, c), sibling) for j, chip in enumerate(chips)]
        for j, chip in enumerate(chips):
            copy(1 + j, (*chip, c), me).wait_recv()
            passed[j].start()
        copy(0, sibling, me).wait_recv()
        for j, chip in enumerate(chips):
            copy(4 + j, (*chip, 1 - c), me).wait_recv()
        for cp in first + passed:
            cp.wait_send()
        mine.wait()

    return pl.pallas_call(
        body, out_shape=_sds((8 * m_per, n), v.dtype), in_specs=[VM], out_specs=VM,
        scratch_shapes=[pltpu.SemaphoreType.DMA((7,)), pltpu.SemaphoreType.DMA((7,)), pltpu.SemaphoreType.DMA],
        name=name)(v)


TENSORS = {
    "w_in": ("lead", None, (4, D, 1285), (1, D, 1285)),
    "w_mg": ("col", 768, (D, 3072), (512, 3072)),
    "w_br": ("col", 256, (3, BW, D), (3, BW, D)),
    "w_o": ("row", 256, (D, D), (D, D)),
    "w_up": ("col", 1408, (D, 5632), (256, 5632)),
    "w_down": ("row", 704, (DFF, D), (704, D)),
}
BIG = tuple(TENSORS)


def _shard_shape(name):
    kind, width, full, _ = TENSORS[name]
    if kind == "lead":
        return full[1:]
    return full[:-1] + (width,) if kind == "col" else (width,) + full[1:]


def _shard_view(ref, layers, name, k):
    kind, width, full, _ = TENSORS[name]
    if kind == "lead":
        return ref.at[layers, k]
    if kind == "row":
        return ref.at[layers, pl.ds(k * width, width)]
    return ref.at[(layers,) + (slice(None),) * (len(full) - 1) + (pl.ds(k * width, width),)]


def _remote(send_sems, recv_sems, k, src, dst, to):
    return pltpu.make_async_remote_copy(src_ref=src, dst_ref=dst, send_sem=send_sems.at[k], recv_sem=recv_sems.at[k],
                                        device_id=to, device_id_type=MESH)


def _dma_sems(n):
    return [pltpu.SemaphoreType.DMA((n,)), pltpu.SemaphoreType.DMA((n,))]


RS_GROUP = 2
HBM = pl.BlockSpec(memory_space=pltpu.HBM)
SEM = pl.BlockSpec(memory_space=pltpu.SEMAPHORE)
SPLIT_CALL = dict(compiler_params=pltpu.CompilerParams(has_side_effects=pltpu.SideEffectType.DATAFLOW_SIDE_EFFECTING))
PULL_SET = (("w_mg", "w_up", "w_o"), ("w_in", "w_br", "w_down"))
FIRST_NEEDED = ("w_in", "w_mg")
NEEDED_LATER = tuple(n for n in BIG if n not in FIRST_NEEDED)


def _in_hbm(a):
    return pltpu.with_memory_space_constraint(a, pltpu.HBM)


def _pull_sends(send_sems, recv_sems, p, o, layer, core, x, y, chips, names=BIG):
    return [_remote(send_sems, recv_sems, 3 * BIG.index(n) + j, p[n].at[layer], _shard_view(o[n], 0, n, 2 * x + y), (*chip, core))
            for n in PULL_SET[core] if n in names for j, chip in enumerate(chips)]


def _pull_arrivals(send_sems, recv_sems, o, core, x, y, chips, to, names=BIG):
    views = [(3 * BIG.index(n) + j, _shard_view(o[n], 0, n, 2 * chip[0] + chip[1]))
             for n in PULL_SET[core] if n in names for j, chip in enumerate(chips)]
    return [_remote(send_sems, recv_sems, k, v, v, to) for k, v in views]


def gather_start(shards, layer, *, name, after=None, names=BIG, lands=None):
    nt = len(BIG)
    first = [] if after is None else [after]

    def body(*refs):
        p, o = dict(zip(BIG, refs[:nt])), dict(zip(BIG, refs[nt:2 * nt]))
        ss, rs = refs[2 * nt + len(first)], refs[2 * nt + len(first) + 1]
        x, y, c, chips = _place()
        for core in (0, 1):
            @pl.when(c == core)
            def _():
                for cp in _pull_sends(ss, rs, p, o, layer, core, x, y, chips, names):
                    cp.start()
        refs[-1][...] = jnp.zeros_like(refs[-1])

    if lands is None:
        lands = [lax.empty((1,) + TENSORS[n][2], shards[n].dtype) for n in BIG]
    lands = [_in_hbm(a) for a in lands]
    outs = pl.pallas_call(
        body,
        out_shape=(pltpu.SemaphoreType.DMA((3 * nt,)), pltpu.SemaphoreType.DMA((3 * nt,)),
                   *[pltpu.HBM(a.shape, a.dtype) for a in lands], _sds((8, HD), F32)),
        in_specs=[HBM] * (2 * nt) + [ANY] * len(first), out_specs=(SEM, SEM, *[HBM] * nt, VM),
        input_output_aliases={nt + t: 2 + t for t in range(nt)}, name=name, **SPLIT_CALL)(
            *[_in_hbm(shards[n]) for n in BIG], *lands, *first)
    return outs[0], outs[1], outs[2:2 + nt], outs[-1]


def gather_wait(send_sems, recv_sems, shards, lands, after, layer, *, name, names=BIG):
    nt = len(BIG)

    def body(*refs):
        p, o = dict(zip(BIG, refs[:nt])), dict(zip(BIG, refs[nt:2 * nt]))
        ss, rs = refs[2 * nt], refs[2 * nt + 1]
        x, y, c, chips = _place()
        for core in (0, 1):
            @pl.when(c == core)
            def _():
                for cp in _pull_sends(ss, rs, p, o, layer, core, x, y, chips, names):
                    cp.wait_send()
                for cp in _pull_arrivals(ss, rs, o, core, x, y, chips, (x, y, core), names):
                    cp.wait_recv()

    return pl.pallas_call(
        body, out_shape=tuple(pltpu.HBM(a.shape, a.dtype) for a in lands),
        in_specs=[HBM] * (2 * nt) + [SEM, SEM] + [ANY] * len(after), out_specs=tuple([HBM] * nt),
        input_output_aliases={nt + t: t for t in range(nt)}, name=name, **SPLIT_CALL)(
            *[_in_hbm(shards[n]) for n in BIG], *lands, send_sems, recv_sems, *after)


def gather_forward(shards, lands, layer, *, name, names=BIG):
    nt = len(BIG)

    def body(*refs):
        p, o = dict(zip(BIG, refs[:nt])), dict(zip(BIG, refs[2 * nt:3 * nt]))
        ss, rs = refs[3 * nt:]
        x, y, c, chips = _place()
        for core in (0, 1):
            @pl.when(c == core)
            def _():
                me, sibling = (x, y, core), (x, y, 1 - core)
                sends = _pull_arrivals(ss, rs, o, core, x, y, chips, sibling, names)
                sends += [_remote(ss, rs, 3 * nt + t, p[n].at[layer], _shard_view(o[n], 0, n, 2 * x + y), sibling)
                          for t, n in enumerate(BIG) if n in names]
                for cp in sends:
                    cp.start()
                for cp in sends:
                    cp.wait_send()
                for cp in _pull_arrivals(ss, rs, o, 1 - core, x, y, chips, me, names):
                    cp.wait_recv()
                for t, n in enumerate(BIG):
                    if n in names:
                        own = _shard_view(o[n], 0, n, 2 * x + y)
                        _remote(ss, rs, 3 * nt + t, own, own, me).wait_recv()

    outs = pl.pallas_call(
        body, out_shape=[_sds(a.shape, a.dtype) for a in lands], in_specs=[ANY] * (2 * nt), out_specs=[ANY] * nt,
        input_output_aliases={nt + t: t for t in range(nt)}, scratch_shapes=_dma_sems(4 * nt), name=name)(
            *[shards[n] for n in BIG], *lands)
    return dict(zip(BIG, outs))


def _forward_copies(ss, rs, p, o, layer, core, x, y, chips):
    sibling, nt = (x, y, 1 - core), len(BIG)
    return _pull_arrivals(ss, rs, o, core, x, y, chips, sibling) + [
        _remote(ss, rs, 3 * nt + t, p[n].at[layer], _shard_view(o[n], 0, n, 2 * x + y), sibling) for t, n in enumerate(BIG)]


def _forward_arrivals(ss, rs, o, core, x, y, chips):
    me, nt = (x, y, core), len(BIG)
    own = [(3 * nt + t, _shard_view(o[n], 0, n, 2 * x + y)) for t, n in enumerate(BIG)]
    return _pull_arrivals(ss, rs, o, 1 - core, x, y, chips, me) + [_remote(ss, rs, k, v, v, me) for k, v in own]


def gather_forward_start(shards, lands, layer, *, name):
    nt = len(BIG)

    def body(*refs):
        p, o = dict(zip(BIG, refs[:nt])), dict(zip(BIG, refs[nt:2 * nt]))
        x, y, c, chips = _place()
        for core in (0, 1):
            @pl.when(c == core)
            def _():
                for cp in _forward_copies(refs[2 * nt], refs[2 * nt + 1], p, o, layer, core, x, y, chips):
                    cp.start()
        refs[-1][...] = jnp.zeros_like(refs[-1])

    lands = [_in_hbm(a) for a in lands]
    outs = pl.pallas_call(
        body,
        out_shape=(pltpu.SemaphoreType.DMA((4 * nt,)), pltpu.SemaphoreType.DMA((4 * nt,)),
                   *[pltpu.HBM(a.shape, a.dtype) for a in lands], _sds((8, HD), F32)),
        in_specs=[HBM] * (2 * nt), out_specs=(SEM, SEM, *[HBM] * nt, VM),
        input_output_aliases={nt + t: 2 + t for t in range(nt)}, name=name, **SPLIT_CALL)(
            *[_in_hbm(shards[n]) for n in BIG], *lands)
    return outs[0], outs[1], outs[2:2 + nt], outs[-1]


def gather_forward_wait(send_sems, recv_sems, shards, lands, after, layer, *, name):
    nt = len(BIG)

    def body(*refs):
        p, o = dict(zip(BIG, refs[:nt])), dict(zip(BIG, refs[nt:2 * nt]))
        ss, rs = refs[2 * nt], refs[2 * nt + 1]
        x, y, c, chips = _place()
        for core in (0, 1):
            @pl.when(c == core)
            def _():
                for cp in _forward_copies(ss, rs, p, o, layer, core, x, y, chips):
                    cp.wait_send()
                for cp in _forward_arrivals(ss, rs, o, core, x, y, chips):
                    cp.wait_recv()

    outs = pl.pallas_call(
        body, out_shape=tuple(pltpu.HBM(a.shape, a.dtype) for a in lands),
        in_specs=[HBM] * (2 * nt) + [SEM, SEM] + [ANY] * len(after), out_specs=tuple([HBM] * nt),
        input_output_aliases={nt + t: t for t in range(nt)}, name=name, **SPLIT_CALL)(
            *[_in_hbm(shards[n]) for n in BIG], *lands, send_sems, recv_sems, *after)
    return dict(zip(BIG, outs))


def pair_exchange(g, *, name):
    hh = g[BIG[0]].shape[0] // 2
    nt = len(BIG)

    def body(*refs):
        send_sems, recv_sems = refs[2 * nt:]
        x, y, c, _ = _place()
        copies = [_remote(send_sems, recv_sems, t, refs[t].at[pl.ds(hh * (1 - c), hh)], refs[nt + t], (x, y, 1 - c))
                  for t in range(nt)]
        for cp in copies:
            cp.start()
        for cp in copies:
            cp.wait()

    outs = pl.pallas_call(
        body, out_shape=[_sds((hh,) + g[n].shape[1:], g[n].dtype) for n in BIG], in_specs=[ANY] * nt, out_specs=[ANY] * nt,
        scratch_shapes=_dma_sems(nt), name=name)(*[g[n] for n in BIG])
    return dict(zip(BIG, outs))


def _chip_copies(send_sems, recv_sems, s_refs, land_refs, c, chips):
    hl = s_refs[0].shape[0]
    return [_remote(send_sems, recv_sems, 3 * t + j, _shard_view(s_refs[t], pl.ds(0, hl), n, 2 * chip[0] + chip[1]),
                    land_refs[t].at[j], (*chip, c))
            for t, n in enumerate(BIG) for j, chip in enumerate(chips)]


def _landing_shapes(s):
    hl = s[BIG[0]].shape[0]
    return [_sds((3, hl) + _shard_shape(n), s[n].dtype) for n in BIG]


def chip_exchange(s, *, name):
    nt = len(BIG)

    def body(*refs):
        send_sems, recv_sems = refs[2 * nt:]
        x, y, c, chips = _place()
        copies = _chip_copies(send_sems, recv_sems, refs[:nt], refs[nt:2 * nt], c, chips)
        for cp in copies:
            cp.start()
        for cp in copies:
            cp.wait()

    outs = pl.pallas_call(
        body, out_shape=_landing_shapes(s), in_specs=[ANY] * nt, out_specs=[ANY] * nt,
        scratch_shapes=_dma_sems(3 * nt), name=name)(*[s[n] for n in BIG])
    return dict(zip(BIG, outs))


def chip_exchange_start(s, *, name, after=None):
    nt = len(BIG)
    first = [] if after is None else [after]

    def body(*refs):
        o = refs[2 * nt + len(first):]
        x, y, c, chips = _place()
        for cp in _chip_copies(o[0], o[1], refs[:nt], refs[nt:2 * nt], c, chips):
            cp.start()
        refs[-1][...] = jnp.zeros_like(refs[-1])

    lands = [_in_hbm(lax.empty(d.shape, d.dtype)) for d in _landing_shapes(s)]
    srcs = [_in_hbm(s[n]) for n in BIG]
    outs = pl.pallas_call(
        body,
        out_shape=(pltpu.SemaphoreType.DMA((3 * nt,)), pltpu.SemaphoreType.DMA((3 * nt,)),
                   *[pltpu.HBM(a.shape, a.dtype) for a in srcs + lands], _sds((8, HD), F32)),
        in_specs=[HBM] * (2 * nt) + [ANY] * len(first), out_specs=(SEM, SEM, *[HBM] * (2 * nt), VM),
        input_output_aliases={t: 2 + t for t in range(2 * nt)}, name=name, **SPLIT_CALL)(*srcs, *lands, *first)
    return outs[0], outs[1], outs[2:2 + nt], outs[2 + nt:2 + 2 * nt], outs[-1]


def chip_exchange_wait(send_sems, recv_sems, srcs, lands, after, *, name):
    nt = len(BIG)

    def body(*refs):
        x, y, c, chips = _place()
        for cp in _chip_copies(refs[2 * nt], refs[2 * nt + 1], refs[:nt], refs[nt:2 * nt], c, chips):
            cp.wait_send()
            cp.wait_recv()

    outs = pl.pallas_call(
        body, out_shape=tuple(pltpu.HBM(a.shape, a.dtype) for a in list(srcs) + list(lands)),
        in_specs=[HBM] * (2 * nt) + [SEM, SEM] + [ANY] * len(after), out_specs=tuple([HBM] * (2 * nt)),
        input_output_aliases={t: t for t in range(2 * nt)}, name=name, **SPLIT_CALL)(
            *srcs, *lands, send_sems, recv_sems, *after)
    return dict(zip(BIG, outs[:nt])), dict(zip(BIG, outs[nt:]))


def pair_share(f, l0, hh, *, name):
    nt = len(BIG)

    def body(*refs):
        o = refs[nt:2 * nt]
        send_sems, recv_sems = refs[2 * nt:]
        x, y, c, _ = _place()
        mine, theirs = pl.ds(l0 + hh * c, hh), pl.ds(l0 + hh * (1 - c), hh)
        copies = [_remote(send_sems, recv_sems, t, o[t].at[mine], o[t].at[mine], (x, y, 1 - c)) for t in range(nt)]
        for cp in copies:
            cp.start()
        for t, cp in enumerate(copies):
            cp.wait_send()
            _remote(send_sems, recv_sems, t, o[t].at[theirs], o[t].at[theirs], (x, y, c)).wait_recv()

    outs = pl.pallas_call(
        body, out_shape=[_sds(f[n].shape, f[n].dtype) for n in BIG], in_specs=[ANY] * nt, out_specs=[ANY] * nt,
        input_output_aliases={t: t for t in range(nt)}, scratch_shapes=_dma_sems(nt), name=name)(*[f[n] for n in BIG])
    return dict(zip(BIG, outs))


def pair_add(g, r, idx, *, tensor, name):
    _, _, full, blk = TENSORS[tensor]
    hh = r.shape[0]

    def body(idx_ref, g_ref, r_ref, o_ref):
        o_ref[...] = (g_ref[...].astype(F32) + r_ref[...].astype(F32)).astype(o_ref.dtype)

    own = pl.BlockSpec((None,) + blk, lambda *a: (a[0],) + a[1:-1])
    return pl.pallas_call(
        body, out_shape=_sds(r.shape, r.dtype),
        grid_spec=pltpu.PrefetchScalarGridSpec(
            num_scalar_prefetch=1, grid=(hh,) + tuple(f // b for f, b in zip(full, blk)),
            in_specs=[pl.BlockSpec((None,) + blk, lambda *a: (hh * a[-1][0] + a[0],) + a[1:-1]), own], out_specs=own),
        compiler_params=_cparams(("parallel",) * (1 + len(full))), name=name)(idx, g, r)


def chip_add(s, r, idx, totals, l0, *, tensor, name):
    kind, width, full, _ = TENSORS[tensor]
    shard = _shard_shape(tensor)
    hh = s.shape[0]
    zeros = (0,) * len(shard)

    def body(idx_ref, s_ref, r0_ref, r1_ref, r2_ref, t_ref, o_ref):
        o_ref[...] = ((s_ref[...].astype(F32) + r0_ref[...].astype(F32)) + r1_ref[...].astype(F32)) + r2_ref[...].astype(F32)

    if kind == "lead":
        mine = pl.BlockSpec((None, None) + shard, lambda i, ix: (i, ix[1]) + zeros)
    elif kind == "row":
        mine = pl.BlockSpec((None,) + shard, lambda i, ix: (i, ix[1]) + zeros[1:])
    else:
        mine = pl.BlockSpec((None,) + shard, lambda i, ix: (i,) + zeros[1:] + (ix[1],))
    peer = lambda j: pl.BlockSpec((None, None) + shard, lambda i, ix: (j, i) + zeros)
    return pl.pallas_call(
        body, out_shape=_sds(totals.shape, F32),
        grid_spec=pltpu.PrefetchScalarGridSpec(
            num_scalar_prefetch=1, grid=(hh,), in_specs=[mine, peer(0), peer(1), peer(2), pl.BlockSpec(memory_space=pl.ANY)],
            out_specs=pl.BlockSpec((None,) + shard, lambda i, ix: (l0 + hh * ix[0] + i,) + zeros)),
        input_output_aliases={5: 0}, compiler_params=_cparams(("parallel",)), name=name)(idx, s, r, r, r, totals)


def _flat_rows(arrs):
    v = jnp.concatenate([a.reshape(-1) for a in arrs])
    n = -(-v.shape[0] // 1024) * 1024
    return jnp.pad(v, (0, n - v.shape[0])).reshape(n // HD, HD)


def _unflat(buf, shapes):
    v, out, o = buf.reshape(-1), [], 0
    for s in shapes:
        n = int(np.prod(s))
        out.append(v[o:o + n].reshape(s))
        o += n
    return out


WEIGHTS = ("norm1_g", "norm2_g", "w_ada", "b_ada", "w_in", "w_gla_a2", "b_gla_a", "b_fox_f", "ret_norm_g", "gla_norm_g",
           "q_norm_g", "k_norm_g", "w_br", "w_mg", "b_mg", "w_o", "w_up", "w_conv", "b_conv", "w_down")
REPLICATED = ("norm1_g", "norm2_g", "b_gla_a", "b_fox_f", "ret_norm_g", "gla_norm_g", "q_norm_g", "k_norm_g", "b_mg", "b_conv")
ADAM_BLOCKS = dict(w_ada=(1, 256, 1536), w_in=(1, 256, 1285), w_br=(1, 3, BW, 256), w_mg=(1, 512, 768), w_o=(2, 256, D),
                   w_up=(1, 256, 1408), w_down=(1, 352, D))
ALL_AXES = ("x", "y", "c")


def kernel(x, c, norm1_g, norm2_g, w_ada, b_ada, w_in, w_gla_a2, b_gla_a, b_fox_f, ret_norm_g, gla_norm_g, q_norm_g, k_norm_g, w_br, w_mg, b_mg, w_o, w_up, w_conv, b_conv, w_down, loss_target, m_norm1_g, m_norm2_g, m_w_ada, m_b_ada, m_w_in, m_w_gla_a2, m_b_gla_a, m_b_fox_f, m_ret_norm_g, m_gla_norm_g, m_q_norm_g, m_k_norm_g, m_w_br, m_w_mg, m_b_mg, m_w_o, m_w_up, m_w_conv, m_b_conv, m_w_down, v_norm1_g, v_norm2_g, v_w_ada, v_b_ada, v_w_in, v_w_gla_a2, v_b_gla_a, v_b_fox_f, v_ret_norm_g, v_gla_norm_g, v_q_norm_g, v_k_norm_g, v_w_br, v_w_mg, v_b_mg, v_w_o, v_w_up, v_w_conv, v_b_conv, v_w_down):
    w = dict(zip(WEIGHTS, (norm1_g, norm2_g, w_ada, b_ada, w_in, w_gla_a2, b_gla_a, b_fox_f, ret_norm_g, gla_norm_g,
                           q_norm_g, k_norm_g, w_br, w_mg, b_mg, w_o, w_up, w_conv, b_conv, w_down)))
    m = dict(zip(WEIGHTS, (m_norm1_g, m_norm2_g, m_w_ada, m_b_ada, m_w_in, m_w_gla_a2, m_b_gla_a, m_b_fox_f, m_ret_norm_g,
                           m_gla_norm_g, m_q_norm_g, m_k_norm_g, m_w_br, m_w_mg, m_b_mg, m_w_o, m_w_up, m_w_conv, m_b_conv,
                           m_w_down)))
    v = dict(zip(WEIGHTS, (v_norm1_g, v_norm2_g, v_w_ada, v_b_ada, v_w_in, v_w_gla_a2, v_b_gla_a, v_b_fox_f, v_ret_norm_g,
                           v_gla_norm_g, v_q_norm_g, v_k_norm_g, v_w_br, v_w_mg, v_b_mg, v_w_o, v_w_up, v_w_conv, v_b_conv,
                           v_w_down)))
    nl = norm1_g.shape[0]
    seq = x.shape[1]
    xi, yi, ci = lax.axis_index("x"), lax.axis_index("y"), lax.axis_index("c")
    k_me = 2 * xi + yi
    b_me = 4 * xi + 2 * yi + ci
    ada_n = w_ada.shape[2]
    a2_n, conv_n = w_gla_a2.shape[2], w_conv.shape[2]

    shards = [{n: w[n][:1].astype(MMT) for n in BIG}]

    def gather_finish(l, started, after):
        send_sems, recv_sems, lands, _ = started
        lands = gather_wait(send_sems, recv_sems, shards[l], lands, [after], 0, name=f"gather{l}_wait")
        big = gather_forward(shards[l], lands, 0, name=f"gather{l}_forward")
        big["w1"] = build_w1(big["w_in"], big["w_mg"])
        return big

    blk = _flat_rows([c, w_gla_a2, w_conv])
    g1 = small_allgather(blk, name="gather_small").reshape(8, blk.shape[0], HD)
    c_all = g1[:, :D // HD].reshape(8, D)
    by_chip = g1[0::2].reshape(4, -1)[:, D:]
    a2_sh, conv_sh = by_chip[:, :nl * GLR * a2_n], by_chip[:, nl * GLR * a2_n:nl * (GLR * a2_n + 3 * conv_n)]
    full_small = dict(
        w_gla_a2=a2_sh.reshape(4, nl, GLR, a2_n).transpose(1, 2, 0, 3).reshape(nl, GLR, 4 * a2_n),
        w_conv=conv_sh.reshape(4, nl, 3, conv_n).transpose(1, 2, 0, 3).reshape(nl, 3, 4 * conv_n))

    b_ada_sh = lax.dynamic_slice_in_dim(b_ada, k_me * ada_n, ada_n, axis=1)[:, None, :]
    mod_sh = ada_mod(c_all, w_ada, b_ada_sh)
    g2 = small_allgather(mod_sh.reshape(nl * 8, ada_n), name="gather_mod").reshape(4, 2, nl, 8, ada_n)[:, 0]
    mod_me = lax.dynamic_index_in_dim(g2, b_me, axis=2, keepdims=False).transpose(1, 0, 2).reshape(nl, 4 * ada_n)

    wsmall = {n: w[n] for n in REPLICATED}
    wsmall.update(full_small)
    mods = [[mod_me[l:l + 1, i * D:(i + 1) * D] for i in range(6)] for l in range(nl)]

    send_sems, recv_sems, lands, token = gather_start(shards[0], 0, name="gather0_start_first", after=mod_me,
                                                      names=FIRST_NEEDED)
    shards += [{n: (w[n][l:l + 1] + token[0, 0]).astype(MMT) for n in BIG} for l in range(1, nl)]
    cast_meanwhile = [a for sh in shards[1:] for a in sh.values()]
    lands = gather_wait(send_sems, recv_sems, shards[0], lands, [token, *cast_meanwhile], 0, name="gather0_wait_first",
                        names=FIRST_NEEDED)
    first = gather_forward(shards[0], lands, 0, name="gather0_forward_first", names=FIRST_NEEDED)
    later_sems = gather_start(shards[0], 0, name="gather0_start_later", names=NEEDED_LATER, lands=[first[n] for n in BIG])
    in_flight = dict(zip(BIG, later_sems[2]))
    big = {"w1": build_w1(in_flight["w_in"], in_flight["w_mg"])}

    def rest_of_layer0(after):
        rest = gather_wait(later_sems[0], later_sems[1], shards[0], later_sems[2], [after], 0, name="gather0_wait_later",
                           names=NEEDED_LATER)
        return gather_forward(shards[0], rest, 0, name="gather0_forward_later", names=NEEDED_LATER)

    cosf, sinf = _rope_tables(seq)
    xs, saved, params = x[0], [], []
    for l in range(nl):
        if l + 1 < nl:
            started = gather_start(shards[l + 1], 0, name=f"gather{l + 1}_start", after=xs if l else big["w1"])
            mods[l][1] = mods[l][1] + started[-1][0, 0]
        trade = []

        def trade_next_layer(after, l=l, started=started if l + 1 < nl else None):
            lands = gather_wait(started[0], started[1], shards[l + 1], started[2], [after], 0, name=f"gather{l + 1}_wait")
            trade.append(gather_forward_start(shards[l + 1], lands, 0, name=f"gather{l + 1}_forward_start"))
            return trade[0][-1]

        xs, sv, p = layer_fwd(xs, mods[l], layer_params(wsmall, big, l, later=rest_of_layer0 if l == 0 else None,
                                                        late_in_layer=trade_next_layer if l + 1 < nl else None),
                              cosf, sinf)
        saved.append(sv)
        params.append(p)
        if trade:
            big = gather_forward_wait(trade[0][0], trade[0][1], shards[l + 1], trade[0][2], [xs], 0,
                                      name=f"gather{l + 1}_forward_wait")
            big["w1"] = build_w1(big["w_in"], big["w_mg"])
        elif l + 1 < nl:
            big = gather_finish(l + 1, started, xs)
    loss_part, dx = loss_and_grad(xs, loss_target[0], name="loss")
    loss = lax.psum(loss_part[0, 0], ALL_AXES)
    grads, dmods = [None] * nl, [None] * nl
    idx = jnp.stack([ci, k_me]).astype(jnp.int32)
    totals = {n: lax.empty((nl,) + _shard_shape(n), F32) for n in BIG}

    def finish_group(pending, after, totals, idx):
        group, send_sems, recv_sems, srcs, lands, _ = pending
        sums, from_chips = chip_exchange_wait(send_sems, recv_sems, srcs, lands, after, name=f"rs{group}_chip_exchange_wait")
        totals = {n: chip_add(sums[n], from_chips[n], idx, totals[n], RS_GROUP * group, tensor=n,
                              name=f"rs{group}_chip_add_{n}") for n in BIG}
        return pair_share(totals, RS_GROUP * group, RS_GROUP // 2, name=f"rs{group}_pair_share")

    pending = None
    for group in reversed(range(nl // RS_GROUP)):
        layers = range(RS_GROUP * group, RS_GROUP * (group + 1))
        stacks = {n: lax.empty((RS_GROUP,) + TENSORS[n][2], MMT) for n in BIG if n != "w_br"}
        if pending is not None:
            mods[layers[-1]][5] = mods[layers[-1]][5] + pending[-1][0, 0]
        for l in reversed(layers):
            dx, g, dmods[l], stacks = layer_bwd(dx, mods[l], params[l], saved[l], cosf, sinf, stacks, l - layers[0])
            grads[l] = layer_grads(g)
        stacks["w_br"] = jnp.stack([grads[l]["w_br"].astype(MMT) for l in layers])
        from_sibling = pair_exchange(stacks, name=f"rs{group}_pair_exchange")
        chip_sum = {n: pair_add(stacks[n], from_sibling[n], idx, tensor=n, name=f"rs{group}_pair_add_{n}") for n in BIG}
        if group > 0:
            if pending is not None:
                totals = finish_group(pending, [dx, *chip_sum.values()], totals, idx)
            pending = (group, *chip_exchange_start(chip_sum, name=f"rs{group}_chip_exchange_start"))

    small_names = REPLICATED + ("w_gla_a2", "w_conv")
    small_shapes = [(nl, 6 * D)] + [(nl,) + grads[0][n].shape for n in small_names]
    vec = _flat_rows([jnp.concatenate(dmods, axis=0)] + [jnp.stack([grads[l][n] for l in range(nl)]) for n in small_names])
    gs = small_allgather(vec, name="gather_small_grads")
    earlier = pending
    pending = (0, *chip_exchange_start(chip_sum, name="rs0_chip_exchange_start", after=gs))
    if earlier is not None:
        totals = finish_group(earlier, [dx, pending[-1]], totals, idx)
    gs = (gs + pending[-1][0, 0]).reshape(8, vec.shape[0], HD)
    summed = _unflat(sum_devices(gs), small_shapes)
    grad = dict(zip(small_names, summed[1:]))
    grad["b_ada"] = summed[0]
    grad["w_gla_a2"] = lax.dynamic_slice_in_dim(grad["w_gla_a2"], k_me * a2_n, a2_n, axis=2)
    grad["w_conv"] = lax.dynamic_slice_in_dim(grad["w_conv"], k_me * conv_n, conv_n, axis=2)
    dmod_all = gs[:, :nl * 6 * D // HD].reshape(8, nl, 6 * D)
    dmod_sh = lax.dynamic_slice_in_dim(dmod_all, k_me * ada_n, ada_n, axis=2).transpose(1, 0, 2)
    grad["w_ada"] = ada_dw(c_all, dmod_sh)

    delta, new_m, new_v = {}, {}, {}
    delta["w_ada"], new_m["w_ada"], new_v["w_ada"] = adamw(w["w_ada"], grad["w_ada"], m["w_ada"], v["w_ada"],
                                                          block=ADAM_BLOCKS["w_ada"], name="adamw_w_ada")
    rest = [n for n in WEIGHTS if n not in ADAM_BLOCKS]
    shapes = [w[n].shape for n in rest]
    flat = [_flat_rows([t[n] for n in rest]) for t in (w, grad, m, v)]
    outs = adamw(*flat, block=flat[0].shape, name="adamw_small")
    for t, o in zip((delta, new_m, new_v), outs):
        t.update(zip(rest, _unflat(o, shapes)))

    later = {n: adamw(w[n], totals[n], m[n], v[n], block=ADAM_BLOCKS[n], name="adamw_later_" + n, rows=(RS_GROUP, nl),
                      with_grad=True) for n in BIG} if nl > RS_GROUP else {}
    done_first = [outs[0], delta["w_ada"]] + [later[n][0] for n in later]
    totals = finish_group(pending, done_first, totals, idx)
    for n in BIG:
        delta[n], new_m[n], new_v[n], grad[n] = adamw(w[n], totals[n], m[n], v[n], block=ADAM_BLOCKS[n], name="adamw_" + n,
                                                      rows=(0, min(RS_GROUP, nl)), into=later.get(n), with_grad=True)

    return (loss, dx[None], *[grad[n] for n in WEIGHTS], *[delta[n] for n in WEIGHTS], *[new_m[n] for n in WEIGHTS],
            *[new_v[n] for n in WEIGHTS])
```

```python
import functools

import numpy as np
import jax
import jax.numpy as jnp
from jax import lax
from jax.experimental import pallas as pl
from jax.experimental.pallas import tpu as pltpu

F32 = jnp.float32
MMT = jnp.bfloat16
HI = lax.Precision.HIGHEST

D = 1024
DEPTH = 4
NH = 4
HD = 128
BW = NH * HD
CH = 64
GDK = 64
GLR = 16
DFF = 2816
EPS = 1e-6
ROPE_BASE = 10000.0

GP, RG, GG, FV, RQ, RK, RV, GQ, GK, GV, FQ, FK, LR, FF = (
    0, 3072, 3584, 4096, 4608, 5120, 5632, 6144, 6400, 6656, 7168, 7680, 8192, 8320)
NZZ = 8448
WZ0 = 3072
IN_W = 5140
W_IN_COLS = dict(rqkv=(0, 1536), rg=(1536, 2048), gqkv=(2048, 3072), lr=(3072, 3088), gg=(3088, 3600), fqk=(3600, 4624),
                 fv=(4624, 5136), ff=(5136, 5140))

VMEM_LIMIT = 56 * 1024 * 1024

ADAM_LR, ADAM_B1, ADAM_B2, ADAM_EPS, ADAM_WD, ADAM_STEP = 0.001, 0.9, 0.999, 1e-08, 0.01, 10


def _cparams(sem=None):
    return pltpu.CompilerParams(dimension_semantics=sem, vmem_limit_bytes=VMEM_LIMIT)


def _sds(shape, dtype):
    return jax.ShapeDtypeStruct(tuple(shape), dtype)


def _dot(a, b, precision=None):
    return lax.dot_general(a, b, (((1,), (0,)), ((), ())), precision=precision, preferred_element_type=F32)


def _dot_nt(a, b, precision=None):
    return lax.dot_general(a, b, (((1,), (1,)), ((), ())), precision=precision, preferred_element_type=F32)


def _dot_tn(a, b, precision=None):
    return lax.dot_general(a, b, (((0,), (0,)), ((), ())), precision=precision, preferred_element_type=F32)


def _silu(x):
    return x * jax.nn.sigmoid(x)


def _log_sigmoid(x):
    return jnp.minimum(x, 0.0) - jnp.log(1.0 + jnp.exp(jnp.minimum(x, -x)))


@jax.custom_vjp
def _swap_halves(x):
    return pltpu.roll(x, HD // 2, 1)


_swap_halves.defvjp(lambda x: (_swap_halves(x), None), lambda _, g: (_swap_halves(g),))


@jax.custom_vjp
def _bdot(a, b):
    return _dot(a.astype(MMT), b.astype(MMT))


@jax.custom_vjp
def _bdot_nt(a, b):
    return _dot_nt(a.astype(MMT), b.astype(MMT))


@jax.custom_vjp
def _bdot_tn(a, b):
    return _dot_tn(a.astype(MMT), b.astype(MMT))


_bdot.defvjp(lambda a, b: (_bdot(a, b), (a, b)), lambda r, g: (_bdot_nt(g, r[1]), _bdot_tn(r[0], g)))
_bdot_nt.defvjp(lambda a, b: (_bdot_nt(a, b), (a, b)), lambda r, g: (_bdot(g, r[1]), _bdot_tn(g, r[0])))
_bdot_tn.defvjp(lambda a, b: (_bdot_tn(a, b), (a, b)), lambda r, g: (_bdot_nt(r[1], g), _bdot(r[0], g)))


def _stacked(blk, idx, layer):
    if layer is None:
        return pl.BlockSpec(blk, idx)
    return pl.BlockSpec((None,) + blk, lambda i, j: (layer,) + idx(i, j))


def mm_nn(a, b, *, tm, tn, out_dtype, name, layer=None):
    m, k = a.shape
    n = b.shape[-1]

    def body(a_ref, b_ref, o_ref):
        o_ref[...] = _dot(a_ref[...], b_ref[...]).astype(o_ref.dtype)

    return pl.pallas_call(
        body, grid=(m // tm, n // tn),
        in_specs=[pl.BlockSpec((tm, k), lambda i, j: (i, 0)), _stacked((k, tn), lambda i, j: (0, j), layer)],
        out_specs=pl.BlockSpec((tm, tn), lambda i, j: (i, j)),
        out_shape=_sds((m, n), out_dtype), compiler_params=_cparams(("parallel", "parallel")), name=name)(a, b)


def mm_nn_residual(a, b, res, gate, *, tm, tn, name, layer=None):
    m, k = a.shape
    n = b.shape[-1]

    def body(a_ref, b_ref, r_ref, g_ref, x_ref, y_ref):
        acc = _dot(a_ref[...], b_ref[...])
        y_ref[...] = acc
        x_ref[...] = r_ref[...] + g_ref[...] * acc

    return pl.pallas_call(
        body, grid=(m // tm, n // tn),
        in_specs=[pl.BlockSpec((tm, k), lambda i, j: (i, 0)), _stacked((k, tn), lambda i, j: (0, j), layer),
                  pl.BlockSpec((tm, tn), lambda i, j: (i, j)), pl.BlockSpec((1, tn), lambda i, j: (0, j))],
        out_specs=[pl.BlockSpec((tm, tn), lambda i, j: (i, j)), pl.BlockSpec((tm, tn), lambda i, j: (i, j))],
        out_shape=[_sds((m, n), F32), _sds((m, n), F32)],
        compiler_params=_cparams(("parallel", "parallel")), name=name)(a, b, res, gate)


def mm_nt(a, b, *, tm, tn, out_dtype, name, layer=None):
    m, k = a.shape
    n = b.shape[-2]

    def body(a_ref, b_ref, o_ref):
        o_ref[...] = _dot_nt(a_ref[...], b_ref[...]).astype(o_ref.dtype)

    return pl.pallas_call(
        body, grid=(m // tm, n // tn),
        in_specs=[pl.BlockSpec((tm, k), lambda i, j: (i, 0)), _stacked((tn, k), lambda i, j: (j, 0), layer)],
        out_specs=pl.BlockSpec((tm, tn), lambda i, j: (i, j)),
        out_shape=_sds((m, n), out_dtype), compiler_params=_cparams(("parallel", "parallel")), name=name)(a, b)


def mm_nt2(a1, a2, b, *, tm, tn, name, layer):
    m, k1 = a1.shape
    k2 = a2.shape[1]
    n = b.shape[-2]

    def body(a1_ref, a2_ref, b_ref, o_ref):
        o_ref[...] = _dot_nt(a1_ref[...], b_ref[:, :k1]) + _dot_nt(a2_ref[...], b_ref[:, k1:])

    return pl.pallas_call(
        body, grid=(m // tm, n // tn),
        in_specs=[pl.BlockSpec((tm, k1), lambda i, j: (i, 0)), pl.BlockSpec((tm, k2), lambda i, j: (i, 0)),
                  _stacked((tn, k1 + k2), lambda i, j: (j, 0), layer)],
        out_specs=pl.BlockSpec((tm, tn), lambda i, j: (i, j)),
        out_shape=_sds((m, n), F32), compiler_params=_cparams(("parallel", "parallel")), name=name)(a1, a2, b)


def mm_tn(a, b, *, tm, tn, out_dtype, name, col0=0, ncols=None, stack=None, layer=None, out_col0=0):
    s, m = a.shape
    n = b.shape[1] - col0 if ncols is None else ncols
    c0, oc0 = col0 // tn, out_col0 // tn

    def body(a_ref, b_ref, *rest):
        o_ref = rest[-1]
        o_ref[...] = _dot_tn(a_ref[...], b_ref[...]).astype(o_ref.dtype)

    in_specs = [pl.BlockSpec((s, tm), lambda i, j: (0, i)), pl.BlockSpec((s, tn), lambda i, j: (0, c0 + j))]
    if stack is None:
        return pl.pallas_call(
            body, grid=(m // tm, n // tn), in_specs=in_specs, out_specs=pl.BlockSpec((tm, tn), lambda i, j: (i, j)),
            out_shape=_sds((m, n), out_dtype), compiler_params=_cparams(("parallel", "parallel")), name=name)(a, b)
    return pl.pallas_call(
        body, grid=(m // tm, n // tn), in_specs=in_specs + [pl.BlockSpec(memory_space=pl.ANY)],
        out_specs=pl.BlockSpec((None, tm, tn), lambda i, j: (layer, i, oc0 + j)),
        out_shape=_sds(stack.shape, stack.dtype), input_output_aliases={2: 0},
        compiler_params=_cparams(("parallel", "parallel")), name=name)(a, b, stack)


def _row_tile(s):
    return min(256, s)


def _wide_row_tile(s):
    return min(512, s)


def _norm_mod_f(x, g, scale, shift):
    r = lax.rsqrt(jnp.mean(x * x, axis=-1, keepdims=True) + EPS)
    return (x * r * g) * (1.0 + scale) + shift


def norm_mod(x, g, scale, shift, *, name):
    s = x.shape[0]
    t = _wide_row_tile(s)

    def body(x_ref, g_ref, sc_ref, sh_ref, o_ref):
        o_ref[...] = _norm_mod_f(x_ref[...], g_ref[...], sc_ref[...], sh_ref[...]).astype(o_ref.dtype)

    vec = pl.BlockSpec((1, D), lambda i: (0, 0))
    return pl.pallas_call(
        body, grid=(s // t,), in_specs=[pl.BlockSpec((t, D), lambda i: (i, 0)), vec, vec, vec],
        out_specs=pl.BlockSpec((t, D), lambda i: (i, 0)), out_shape=_sds((s, D), MMT),
        compiler_params=_cparams(("parallel",)), name=name)(x, g, scale, shift)


def norm_mod_bwd(x, dh, dres, g, scale, shift, *, name):
    s = x.shape[0]
    t = _wide_row_tile(s)

    def body(x_ref, dh_ref, dr_ref, g_ref, sc_ref, sh_ref, dx_ref, dg_ref, dsc_ref, dsh_ref):
        @pl.when(pl.program_id(0) == 0)
        def _():
            dg_ref[...] = jnp.zeros_like(dg_ref)
            dsc_ref[...] = jnp.zeros_like(dsc_ref)
            dsh_ref[...] = jnp.zeros_like(dsh_ref)

        _, vjp = jax.vjp(_norm_mod_f, x_ref[...], g_ref[...], sc_ref[...], sh_ref[...])
        dx, dg, dsc, dsh = vjp(dh_ref[...])
        dx_ref[...] = dr_ref[...] + dx
        dg_ref[...] += dg
        dsc_ref[...] += dsc
        dsh_ref[...] += dsh

    row = pl.BlockSpec((t, D), lambda i: (i, 0))
    vec = pl.BlockSpec((1, D), lambda i: (0, 0))
    return pl.pallas_call(
        body, grid=(s // t,), in_specs=[row, row, row, vec, vec, vec], out_specs=[row, vec, vec, vec],
        out_shape=[_sds((s, D), F32)] + [_sds((1, D), F32)] * 3,
        compiler_params=_cparams(("arbitrary",)), name=name)(x, dh, dres, g, scale, shift)


def norm_mod_gate_bwd(x, dh, dres, g, scale, shift, y, gate, *, name):
    s = x.shape[0]
    t = _wide_row_tile(s)

    def body(x_ref, dh_ref, dr_ref, g_ref, sc_ref, sh_ref, y_ref, gt_ref, dx_ref, dg_ref, dsc_ref, dsh_ref, dy_ref, dgt_ref):
        @pl.when(pl.program_id(0) == 0)
        def _():
            dg_ref[...] = jnp.zeros_like(dg_ref)
            dsc_ref[...] = jnp.zeros_like(dsc_ref)
            dsh_ref[...] = jnp.zeros_like(dsh_ref)
            dgt_ref[...] = jnp.zeros_like(dgt_ref)

        _, vjp = jax.vjp(_norm_mod_f, x_ref[...], g_ref[...], sc_ref[...], sh_ref[...])
        dx, dg, dsc, dsh = vjp(dh_ref[...])
        dxv = dr_ref[...] + dx
        dx_ref[...] = dxv
        dg_ref[...] += dg
        dsc_ref[...] += dsc
        dsh_ref[...] += dsh
        dy_ref[...] = (gt_ref[...] * dxv).astype(dy_ref.dtype)
        dgt_ref[...] += jnp.sum(dxv * y_ref[...], axis=0, keepdims=True)

    row = pl.BlockSpec((t, D), lambda i: (i, 0))
    vec = pl.BlockSpec((1, D), lambda i: (0, 0))
    return pl.pallas_call(
        body, grid=(s // t,), in_specs=[row, row, row, vec, vec, vec, row, vec], out_specs=[row, vec, vec, vec, row, vec],
        out_shape=[_sds((s, D), F32)] + [_sds((1, D), F32)] * 3 + [_sds((s, D), MMT), _sds((1, D), F32)],
        compiler_params=_cparams(("arbitrary",)), name=name)(x, dh, dres, g, scale, shift, y, gate)


def gate_bwd(dx, y, gate, *, name):
    s = dx.shape[0]
    t = _wide_row_tile(s)

    def body(dx_ref, y_ref, g_ref, dy_ref, dg_ref):
        @pl.when(pl.program_id(0) == 0)
        def _():
            dg_ref[...] = jnp.zeros_like(dg_ref)

        dxv = dx_ref[...]
        dy_ref[...] = (g_ref[...] * dxv).astype(dy_ref.dtype)
        dg_ref[...] += jnp.sum(dxv * y_ref[...], axis=0, keepdims=True)

    row = pl.BlockSpec((t, D), lambda i: (i, 0))
    vec = pl.BlockSpec((1, D), lambda i: (0, 0))
    return pl.pallas_call(
        body, grid=(s // t,), in_specs=[row, row, vec], out_specs=[row, vec],
        out_shape=[_sds((s, D), MMT), _sds((1, D), F32)],
        compiler_params=_cparams(("arbitrary",)), name=name)(dx, y, gate)


def loss_and_grad(xf, target, *, name):
    s = xf.shape[0]
    t = _wide_row_tile(s)

    def body(x_ref, t_ref, l_ref, dx_ref):
        @pl.when(pl.program_id(0) == 0)
        def _():
            l_ref[...] = jnp.zeros_like(l_ref)

        e = x_ref[...] - t_ref[...]
        dx_ref[...] = e * (1.0 / D)
        l_ref[...] += 0.5 * jnp.sum(jnp.sum(e * e, axis=1, keepdims=True), axis=0, keepdims=True) * (1.0 / D)

    row = pl.BlockSpec((t, D), lambda i: (i, 0))
    return pl.pallas_call(
        body, grid=(s // t,), in_specs=[row, row], out_specs=[pl.BlockSpec((1, 1), lambda i: (0, 0)), row],
        out_shape=[_sds((1, 1), F32), _sds((s, D), F32)],
        compiler_params=_cparams(("arbitrary",)), name=name)(xf, target)


def _ret_consts():
    log_g = np.log1p(-np.exp2(-5.0 - np.arange(NH, dtype=np.float32))).astype(np.float32)
    idx = np.arange(CH, dtype=np.float32)
    d_intra = np.exp(np.abs(idx[:, None] - idx[None, :])[None] * log_g[:, None, None]).astype(np.float32)
    k_w = np.exp((CH - 1.0 - idx)[None, :] * log_g[:, None]).astype(np.float32)
    q_w = np.exp((idx + 1.0)[None, :] * log_g[:, None]).astype(np.float32)
    g_chunk = [float(v) for v in np.exp(np.float32(CH) * log_g).astype(np.float32)]
    bc = lambda a: np.ascontiguousarray(np.broadcast_to(a[:, :, None], (NH, CH, HD)))
    return jnp.asarray(d_intra), jnp.asarray(bc(k_w)), jnp.asarray(bc(q_w)), g_chunk


def _rope_tables(s):
    half = HD // 2
    inv_freq = (ROPE_BASE ** (-np.arange(half, dtype=np.float64) / half)).astype(np.float32)
    ang = (np.arange(s, dtype=np.float32)[:, None] * inv_freq[None, :]).astype(np.float64)
    cos, sin = np.cos(ang).astype(np.float32), np.sin(ang).astype(np.float32)
    return jnp.asarray(np.concatenate([cos, cos], axis=1)), jnp.asarray(np.concatenate([-sin, sin], axis=1))


def _ret_chunk(qs, ks, vs, rs, cos, sin, dintra, kw, qw, g_chunk):
    outs, rn = [], []
    for h in range(NH):
        q = qs[h] * cos + _swap_halves(qs[h]) * sin
        k = (ks[h] * cos + _swap_halves(ks[h]) * sin) * (HD ** -0.5)
        sc = _bdot_nt(q, k) * dintra[h]
        outs.append(_bdot(sc, vs[h]) + _bdot(q * qw[h], rs[h]))
        rn.append(g_chunk[h] * rs[h] + _bdot_tn(k * kw[h], vs[h]))
    return outs, rn


def _heads(x):
    return [x[:, h * HD:(h + 1) * HD] for h in range(NH)]


def _chunks_per_step(n):
    return 4 if n % 4 == 0 else 1


def retention_fwd(zz, cosf, sinf, *, name):
    s = zz.shape[0]
    n = s // CH
    nb = _chunks_per_step(n)
    rb = nb * CH
    dintra, kw, qw, g_chunk = _ret_consts()

    def body(q_ref, k_ref, v_ref, c_ref, s_ref, di_ref, kw_ref, qw_ref, o_ref, rp_ref, r_scr):
        @pl.when(pl.program_id(0) == 0)
        def _():
            r_scr[...] = jnp.zeros_like(r_scr)

        r = r_scr[...]
        consts = ([di_ref[h] for h in range(NH)], [kw_ref[h] for h in range(NH)], [qw_ref[h] for h in range(NH)], g_chunk)
        for i in range(nb):
            rows = slice(i * CH, (i + 1) * CH)
            rp_ref[i] = r
            outs, rn = _ret_chunk(_heads(q_ref[rows, :]), _heads(k_ref[rows, :]), _heads(v_ref[rows, :]),
                                  [r[h * HD:(h + 1) * HD] for h in range(NH)], c_ref[rows, :], s_ref[rows, :], *consts)
            o_ref[rows, :] = jnp.concatenate(outs, axis=1)
            r = jnp.concatenate(rn, axis=0)
        r_scr[...] = r

    col = lambda c: pl.BlockSpec((rb, BW), lambda i: (i, c // BW))
    tab = pl.BlockSpec((rb, HD), lambda i: (i, 0))
    cst = lambda shp: pl.BlockSpec(shp, lambda i: (0,) * len(shp))
    return pl.pallas_call(
        body, grid=(n // nb,),
        in_specs=[col(RQ), col(RK), col(RV), tab, tab, cst((NH, CH, CH)), cst((NH, CH, HD)), cst((NH, CH, HD))],
        out_specs=[pl.BlockSpec((rb, BW), lambda i: (i, 0)), pl.BlockSpec((nb, BW, HD), lambda i: (i, 0, 0))],
        out_shape=[_sds((s, BW), F32), _sds((n, BW, HD), F32)],
        scratch_shapes=[pltpu.VMEM((BW, HD), F32)],
        compiler_params=_cparams(("arbitrary",)), name=name)(zz, zz, zz, cosf, sinf, dintra, kw, qw)


def retention_bwd(zz, cosf, sinf, rprev, do, dzz, *, name):
    s = zz.shape[0]
    n = s // CH
    nb = _chunks_per_step(n)
    rb, steps = nb * CH, n // nb
    dintra, kw, qw, g_chunk = _ret_consts()

    def body(q_ref, k_ref, v_ref, c_ref, s_ref, di_ref, kw_ref, qw_ref, rp_ref, do_ref, dzz_ref, dz_ref, dr_scr):
        @pl.when(pl.program_id(0) == 0)
        def _():
            dr_scr[...] = jnp.zeros_like(dr_scr)

        dr = dr_scr[...]
        consts = dict(dintra=[di_ref[h] for h in range(NH)], kw=[kw_ref[h] for h in range(NH)],
                      qw=[qw_ref[h] for h in range(NH)], g_chunk=g_chunk)
        for i in reversed(range(nb)):
            rows = slice(i * CH, (i + 1) * CH)
            rprev_v = rp_ref[i]
            f = functools.partial(_ret_chunk, cos=c_ref[rows, :], sin=s_ref[rows, :], **consts)
            _, vjp = jax.vjp(f, _heads(q_ref[rows, :]), _heads(k_ref[rows, :]), _heads(v_ref[rows, :]),
                             [rprev_v[h * HD:(h + 1) * HD] for h in range(NH)])
            dq, dk, dv, drp = vjp((_heads(do_ref[rows, :]), [dr[h * HD:(h + 1) * HD] for h in range(NH)]))
            dz_ref[rows, :] = jnp.concatenate(dq + dk + dv, axis=1).astype(dz_ref.dtype)
            dr = jnp.concatenate(drp, axis=0)
        dr_scr[...] = dr

    col = lambda c: pl.BlockSpec((rb, BW), lambda i: (steps - 1 - i, c // BW))
    tab = pl.BlockSpec((rb, HD), lambda i: (steps - 1 - i, 0))
    cst = lambda shp: pl.BlockSpec(shp, lambda i: (0,) * len(shp))
    return pl.pallas_call(
        body, grid=(steps,),
        in_specs=[col(RQ), col(RK), col(RV), tab, tab, cst((NH, CH, CH)), cst((NH, CH, HD)), cst((NH, CH, HD)),
                  pl.BlockSpec((nb, BW, HD), lambda i: (steps - 1 - i, 0, 0)),
                  pl.BlockSpec((rb, BW), lambda i: (steps - 1 - i, 0)), pl.BlockSpec(memory_space=pl.ANY)],
        out_specs=pl.BlockSpec((rb, 3 * BW), lambda i: (steps - 1 - i, RQ // (3 * BW))),
        out_shape=_sds(dzz.shape, dzz.dtype), input_output_aliases={10: 0},
        scratch_shapes=[pltpu.VMEM((BW, HD), F32)],
        compiler_params=_cparams(("arbitrary",)), name=name)(zz, zz, zz, cosf, sinf, dintra, kw, qw, rprev, do, dzz)


GKW = NH * GDK


def _gla_consts():
    tri = np.tril(np.ones((CH, CH), np.float32))
    mask_t = np.zeros((BW, GKW), np.float32)
    for h in range(NH):
        mask_t[h * HD:(h + 1) * HD, h * GDK:(h + 1) * GDK] = 1.0
    return jnp.asarray(tri), jnp.asarray(mask_t)


def _gla_chunk(q, k, v, glr, w_a2, b_a, st, tri, mask_t):
    la = _log_sigmoid(_bdot(glr, w_a2) + b_a) * (1.0 / 16.0)
    bc = _dot(tri, la, HI)
    be = jnp.sum(la, axis=0, keepdims=True)
    kv_t = _bdot_tn(v, k * jnp.exp(be - bc)) * mask_t
    sn = jnp.exp(be) * st + kv_t
    return _bdot_nt(q * (GDK ** -0.5), sn), sn


def gla_fwd(zz, w_a2p, b_a, *, name):
    s = zz.shape[0]
    n = s // CH
    nb = _chunks_per_step(n)
    rb = nb * CH
    tri, mask_t = _gla_consts()

    def body(q_ref, k_ref, v_ref, lr_ref, w_ref, b_ref, tri_ref, m_ref, o_ref, sp_ref, st_scr):
        @pl.when(pl.program_id(0) == 0)
        def _():
            st_scr[...] = jnp.zeros_like(st_scr)

        st = st_scr[...]
        for i in range(nb):
            rows = slice(i * CH, (i + 1) * CH)
            sp_ref[i] = st
            o_ref[rows, :], st = _gla_chunk(q_ref[rows, :], k_ref[rows, :], v_ref[rows, :], lr_ref[rows, :], w_ref[...],
                                            b_ref[...], st, tri_ref[...], m_ref[...])
        st_scr[...] = st

    cst = lambda shp: pl.BlockSpec(shp, lambda i: (0,) * len(shp))
    return pl.pallas_call(
        body, grid=(n // nb,),
        in_specs=[pl.BlockSpec((rb, GKW), lambda i: (i, GQ // GKW)), pl.BlockSpec((rb, GKW), lambda i: (i, GK // GKW)),
                  pl.BlockSpec((rb, BW), lambda i: (i, GV // BW)), pl.BlockSpec((rb, HD), lambda i: (i, LR // HD)),
                  cst((HD, GKW)), cst((1, GKW)), cst((CH, CH)), cst((BW, GKW))],
        out_specs=[pl.BlockSpec((rb, BW), lambda i: (i, 0)), pl.BlockSpec((nb, BW, GKW), lambda i: (i, 0, 0))],
        out_shape=[_sds((s, BW), F32), _sds((n, BW, GKW), F32)],
        scratch_shapes=[pltpu.VMEM((BW, GKW), F32)],
        compiler_params=_cparams(("arbitrary",)), name=name)(zz, zz, zz, zz, w_a2p, b_a, tri, mask_t)


def gla_bwd(zz, w_a2p, b_a, sprev, do, dzz, *, name):
    s = zz.shape[0]
    n = s // CH
    nb = _chunks_per_step(n)
    rb, steps = nb * CH, n // nb
    tri, mask_t = _gla_consts()

    def body(q_ref, k_ref, v_ref, lr_ref, w_ref, b_ref, tri_ref, m_ref, sp_ref, do_ref, dzz_ref,
             dz_ref, dlr_ref, dw_ref, db_ref, ds_scr):
        @pl.when(pl.program_id(0) == 0)
        def _():
            ds_scr[...] = jnp.zeros_like(ds_scr)
            dw_ref[...] = jnp.zeros_like(dw_ref)
            db_ref[...] = jnp.zeros_like(db_ref)

        f = functools.partial(_gla_chunk, tri=tri_ref[...], mask_t=m_ref[...])
        ds, dw_sum, db_sum = ds_scr[...], jnp.zeros(dw_ref.shape, F32), jnp.zeros(db_ref.shape, F32)
        for i in reversed(range(nb)):
            rows = slice(i * CH, (i + 1) * CH)
            _, vjp = jax.vjp(f, q_ref[rows, :], k_ref[rows, :], v_ref[rows, :], lr_ref[rows, :], w_ref[...], b_ref[...],
                             sp_ref[i])
            dq, dk, dv, dlr, dw, db, ds = vjp((do_ref[rows, :], ds))
            dz_ref[rows, :] = jnp.concatenate([dq, dk, dv], axis=1).astype(dz_ref.dtype)
            dlr_ref[rows, :] = dlr.astype(dlr_ref.dtype)
            dw_sum, db_sum = dw_sum + dw, db_sum + db
        dw_ref[...] += dw_sum
        db_ref[...] += db_sum
        ds_scr[...] = ds

    cst = lambda shp: pl.BlockSpec(shp, lambda i: (0,) * len(shp))
    r = lambda i: steps - 1 - i
    return pl.pallas_call(
        body, grid=(steps,),
        in_specs=[pl.BlockSpec((rb, GKW), lambda i: (r(i), GQ // GKW)), pl.BlockSpec((rb, GKW), lambda i: (r(i), GK // GKW)),
                  pl.BlockSpec((rb, BW), lambda i: (r(i), GV // BW)), pl.BlockSpec((rb, HD), lambda i: (r(i), LR // HD)),
                  cst((HD, GKW)), cst((1, GKW)), cst((CH, CH)), cst((BW, GKW)),
                  pl.BlockSpec((nb, BW, GKW), lambda i: (r(i), 0, 0)), pl.BlockSpec((rb, BW), lambda i: (r(i), 0)),
                  pl.BlockSpec(memory_space=pl.ANY)],
        out_specs=[pl.BlockSpec((rb, 2 * GKW + BW), lambda i: (r(i), GQ // (2 * GKW + BW))),
                   pl.BlockSpec((rb, HD), lambda i: (r(i), 0)), cst((HD, GKW)), cst((1, GKW))],
        out_shape=[_sds(dzz.shape, dzz.dtype), _sds((s, HD), MMT), _sds((HD, GKW), F32), _sds((1, GKW), F32)],
        input_output_aliases={10: 0}, scratch_shapes=[pltpu.VMEM((BW, GKW), F32)],
        compiler_params=_cparams(("arbitrary",)), name=name)(zz, zz, zz, zz, w_a2p, b_a, tri, mask_t, sprev, do, dzz)


def _fox_pre_f(fqs, fks, ff, gq, gk, bf):
    def rms(x, g):
        return x * lax.rsqrt(jnp.mean(x * x, axis=-1, keepdims=True) + EPS) * g

    qn = [rms(x, gq) * (HD ** -0.5) for x in fqs]
    kn = [rms(x, gk) for x in fks]
    return qn, kn, _log_sigmoid(ff + bf)


def fox_pre(zz, gq, gk, bf, *, name):
    s = zz.shape[0]
    t = _row_tile(s)
    tri = jnp.asarray(np.tril(np.ones((t, t), np.float32)))

    def body(q_ref, k_ref, f_ref, gq_ref, gk_ref, b_ref, tri_ref, qn_ref, kn_ref, cum_ref, carry):
        @pl.when(pl.program_id(0) == 0)
        def _():
            carry[...] = jnp.zeros_like(carry)

        qn, kn, lf = _fox_pre_f(_heads(q_ref[...]), _heads(k_ref[...]), f_ref[...], gq_ref[...], gk_ref[...], b_ref[...])
        qn_ref[...] = jnp.concatenate(qn, axis=1).astype(qn_ref.dtype)
        kn_ref[...] = jnp.concatenate(kn, axis=1).astype(kn_ref.dtype)
        cum_ref[...] = _dot(tri_ref[...], lf, HI) + carry[...]
        carry[...] += jnp.sum(lf, axis=0, keepdims=True)

    vec = pl.BlockSpec((1, HD), lambda i: (0, 0))
    return pl.pallas_call(
        body, grid=(s // t,),
        in_specs=[pl.BlockSpec((t, BW), lambda i: (i, FQ // BW)), pl.BlockSpec((t, BW), lambda i: (i, FK // BW)),
                  pl.BlockSpec((t, HD), lambda i: (i, FF // HD)), vec, vec, vec, pl.BlockSpec((t, t), lambda i: (0, 0))],
        out_specs=[pl.BlockSpec((t, BW), lambda i: (i, 0)), pl.BlockSpec((t, BW), lambda i: (i, 0)),
                   pl.BlockSpec((t, HD), lambda i: (i, 0))],
        out_shape=[_sds((s, BW), MMT), _sds((s, BW), MMT), _sds((s, HD), F32)],
        scratch_shapes=[pltpu.VMEM((1, HD), F32)],
        compiler_params=_cparams(("arbitrary",)), name=name)(zz, zz, zz, gq, gk, bf, tri)


def fox_pre_bwd(zz, gq, gk, bf, dqn, dkn, dcum, dzz, *, name):
    s = zz.shape[0]
    t = _row_tile(s)
    nt = s // t
    triu = jnp.asarray(np.triu(np.ones((t, t), np.float32)))

    def body(q_ref, k_ref, f_ref, gq_ref, gk_ref, b_ref, tri_ref, dqn_ref, dkn_ref, dcum_ref, dzz_ref,
             dz_ref, dff_ref, dgq_ref, dgk_ref, db_ref, carry):
        @pl.when(pl.program_id(0) == 0)
        def _():
            carry[...] = jnp.zeros_like(carry)
            dgq_ref[...] = jnp.zeros_like(dgq_ref)
            dgk_ref[...] = jnp.zeros_like(dgk_ref)
            db_ref[...] = jnp.zeros_like(db_ref)

        dcum_v = dcum_ref[...]
        dlf = _dot(tri_ref[...], dcum_v, HI) + carry[...]
        carry[...] += jnp.sum(dcum_v, axis=0, keepdims=True)
        _, vjp = jax.vjp(_fox_pre_f, _heads(q_ref[...]), _heads(k_ref[...]), f_ref[...], gq_ref[...], gk_ref[...], b_ref[...])
        dq, dk, dff, dgq, dgk, db = vjp((_heads(dqn_ref[...]), _heads(dkn_ref[...]), dlf))
        dz_ref[...] = jnp.concatenate(dq + dk, axis=1).astype(dz_ref.dtype)
        dff_ref[...] = dff.astype(dff_ref.dtype)
        dgq_ref[...] += dgq
        dgk_ref[...] += dgk
        db_ref[...] += db

    r = lambda i: nt - 1 - i
    vec = pl.BlockSpec((1, HD), lambda i: (0, 0))
    return pl.pallas_call(
        body, grid=(nt,),
        in_specs=[pl.BlockSpec((t, BW), lambda i: (r(i), FQ // BW)), pl.BlockSpec((t, BW), lambda i: (r(i), FK // BW)),
                  pl.BlockSpec((t, HD), lambda i: (r(i), FF // HD)), vec, vec, vec, pl.BlockSpec((t, t), lambda i: (0, 0)),
                  pl.BlockSpec((t, BW), lambda i: (r(i), 0)), pl.BlockSpec((t, BW), lambda i: (r(i), 0)),
                  pl.BlockSpec((t, HD), lambda i: (r(i), 0)), pl.BlockSpec(memory_space=pl.ANY)],
        out_specs=[pl.BlockSpec((t, 2 * BW), lambda i: (r(i), FQ // (2 * BW))), pl.BlockSpec((t, HD), lambda i: (r(i), 0)),
                   vec, vec, vec],
        out_shape=[_sds(dzz.shape, dzz.dtype), _sds((s, HD), MMT), _sds((1, HD), F32), _sds((1, HD), F32), _sds((1, HD), F32)],
        input_output_aliases={10: 0}, scratch_shapes=[pltpu.VMEM((1, HD), F32)],
        compiler_params=_cparams(("arbitrary",)), name=name)(zz, zz, zz, gq, gk, bf, triu, dqn, dkn, dcum, dzz)


def _fox_blocks(s):
    return min(256, s), min(512, s)


NEG = -1e30


def fox_attn_fwd(qn, kn, zz, cum_col, cum_row, *, name):
    s = qn.shape[0]
    bq, bk = _fox_blocks(s)

    def body(q_ref, k_ref, v_ref, cc_ref, cr_ref, o_ref, lse_ref):
        qi = pl.program_id(1)
        q = q_ref[...]
        cq = cc_ref[...]
        rows = qi * bq + lax.broadcasted_iota(jnp.int32, (bq, bk), 0)
        cols0 = lax.broadcasted_iota(jnp.int32, (bq, bk), 1)

        def step(j, carry, on_diagonal):
            m, l, acc = carry
            off = pl.multiple_of(j * bk, bk)
            k = k_ref[pl.ds(off, bk), :]
            v = v_ref[pl.ds(off, bk), :].astype(MMT)
            sc = _dot_nt(q, k) + cq - cr_ref[pl.ds(j, 1), :]
            if on_diagonal:
                sc = jnp.where(rows >= cols0 + j * bk, sc, NEG)
            m_new = jnp.maximum(m, jnp.max(sc, axis=1, keepdims=True))
            alpha = jnp.exp(m - m_new)
            p = jnp.exp(sc - m_new)
            return m_new, alpha * l + jnp.sum(p, axis=1, keepdims=True), alpha * acc + _dot(p.astype(MMT), v)

        nfull, nk = (qi * bq + 1) // bk, ((qi + 1) * bq + bk - 1) // bk
        carry = (jnp.full((bq, 1), NEG, F32), jnp.zeros((bq, 1), F32), jnp.zeros((bq, HD), F32))
        carry = lax.fori_loop(0, nfull, functools.partial(step, on_diagonal=False), carry)
        m, l, acc = lax.fori_loop(nfull, nk, functools.partial(step, on_diagonal=True), carry)
        o_ref[...] = acc / l
        lse_ref[...] = m + jnp.log(l)

    return pl.pallas_call(
        body, grid=(NH, s // bq),
        in_specs=[pl.BlockSpec((bq, HD), lambda h, i: (i, h)), pl.BlockSpec((s, HD), lambda h, i: (0, h)),
                  pl.BlockSpec((s, HD), lambda h, i: (0, FV // HD + h)),
                  pl.BlockSpec((None, bq, 1), lambda h, i: (h, i, 0)), pl.BlockSpec((None, s // bk, bk), lambda h, i: (h, 0, 0))],
        out_specs=[pl.BlockSpec((bq, HD), lambda h, i: (i, h)), pl.BlockSpec((None, bq, 1), lambda h, i: (h, i, 0))],
        out_shape=[_sds((s, BW), F32), _sds((NH, s, 1), F32)],
        compiler_params=_cparams(("parallel", "parallel")), name=name)(qn, kn, zz, cum_col, cum_row)


def fox_attn_bwd(qn, kn, zz, cum_col, cum_row, lse, do, dzz, *, name):
    s = qn.shape[0]
    bq, bk = _fox_blocks(s)
    nkc = s // bk

    def body(q_ref, k_ref, v_ref, cc_ref, cr_ref, lse_ref, do_ref, dzz_ref, dq_ref, dk_ref, dv_ref, dc_ref,
             p_scr, dp_scr, dv_scr):
        qi = pl.program_id(1)

        @pl.when(qi == 0)
        def _():
            dk_ref[...] = jnp.zeros_like(dk_ref)
            dv_scr[...] = jnp.zeros_like(dv_scr)
            dc_ref[...] = jnp.zeros_like(dc_ref)

        q = q_ref[...]
        dob = do_ref[...].astype(MMT)
        cq = cc_ref[...]
        lse_v = lse_ref[...]
        rows = qi * bq + lax.broadcasted_iota(jnp.int32, (bq, bk), 0)
        cols0 = lax.broadcasted_iota(jnp.int32, (bq, bk), 1)
        nfull, nk = (qi * bq + 1) // bk, ((qi + 1) * bq + bk - 1) // bk

        def probs(j, delta, on_diagonal):
            off = pl.multiple_of(j * bk, bk)
            sc = _dot_nt(q, k_ref[pl.ds(off, bk), :]) + cq - cr_ref[pl.ds(j, 1), :]
            p = jnp.exp(sc - lse_v)
            if on_diagonal:
                p = jnp.where(rows >= cols0 + j * bk, p, 0.0)
            dp = _dot_nt(dob, v_ref[pl.ds(off, bk), :].astype(MMT))
            p_scr[j] = p
            dp_scr[j] = dp
            return delta + jnp.sum(p * dp, axis=1, keepdims=True)

        delta = lax.fori_loop(0, nfull, functools.partial(probs, on_diagonal=False), jnp.zeros((bq, 1), F32))
        delta = lax.fori_loop(nfull, nk, functools.partial(probs, on_diagonal=True), delta)

        def grads(j, dq):
            off = pl.multiple_of(j * bk, bk)
            p = p_scr[j]
            ds = p * (dp_scr[j] - delta)
            dsm = ds.astype(MMT)
            dv_scr[pl.ds(off, bk), :] += _dot_tn(p.astype(MMT), dob)
            dk_ref[pl.ds(off, bk), :] += _dot_tn(dsm, q)
            dc_ref[pl.ds(j, 1), :] -= jnp.sum(ds, axis=0, keepdims=True)
            return dq + _dot(dsm, k_ref[pl.ds(off, bk), :])

        dq_ref[...] = lax.fori_loop(0, nk, grads, jnp.zeros((bq, HD), F32))

        @pl.when(qi == pl.num_programs(1) - 1)
        def _():
            dv_ref[...] = dv_scr[...].astype(dv_ref.dtype)

    full = lambda c0=0: pl.BlockSpec((s, HD), lambda h, i: (0, c0 + h))
    blk = lambda: pl.BlockSpec((bq, HD), lambda h, i: (i, h))
    colv = lambda: pl.BlockSpec((None, bq, 1), lambda h, i: (h, i, 0))
    rowv = lambda: pl.BlockSpec((None, nkc, bk), lambda h, i: (h, 0, 0))
    return pl.pallas_call(
        body, grid=(NH, s // bq),
        in_specs=[blk(), full(), full(FV // HD), colv(), rowv(), colv(), blk(), pl.BlockSpec(memory_space=pl.ANY)],
        out_specs=[blk(), full(), full(FV // HD), rowv()],
        out_shape=[_sds((s, BW), F32), _sds((s, BW), F32), _sds(dzz.shape, dzz.dtype), _sds((NH, nkc, bk), F32)],
        input_output_aliases={7: 2},
        scratch_shapes=[pltpu.VMEM((nkc, bq, bk), F32), pltpu.VMEM((nkc, bq, bk), F32), pltpu.VMEM((s, HD), F32)],
        compiler_params=_cparams(("parallel", "arbitrary")), name=name)(qn, kn, zz, cum_col, cum_row, lse, do, dzz)


def _branch_f(rets, rgs, glas, ggs, ret_g, gla_g):
    out_r, out_g = [], []
    for h in range(NH):
        xc = rets[h] - jnp.mean(rets[h], axis=-1, keepdims=True)
        y = xc * lax.rsqrt(jnp.mean(xc * xc, axis=-1, keepdims=True) + EPS) * ret_g[h]
        out_r.append(_silu(rgs[h]) * y)
        x = glas[h]
        y = x * lax.rsqrt(jnp.mean(x * x, axis=-1, keepdims=True) + EPS) * gla_g
        out_g.append(_silu(ggs[h]) * y)
    return out_r, out_g


def _w_br_spec(layer):
    return pl.BlockSpec((None, 3, BW, D), lambda i: (layer, 0, 0, 0))


def mix_fwd(ret_raw, gla_raw, fox_o, zz, ret_g, gla_g, b_mg, w_br, *, name, layer):
    s = zz.shape[0]
    t = _row_tile(s)

    def body(r_ref, g_ref, f_ref, rg_ref, gg_ref, gp_ref, rgn_ref, ggn_ref, bmg_ref, w_ref, o_ref):
        rgn = rgn_ref[...]
        br_r, br_g = _branch_f(_heads(r_ref[...]), _heads(rg_ref[...]), _heads(g_ref[...]), _heads(gg_ref[...]),
                               _heads(rgn), ggn_ref[...])
        brs = [jnp.concatenate(br_r, axis=1), jnp.concatenate(br_g, axis=1), f_ref[...]]
        acc = jnp.zeros((t, D), F32)
        for b in range(3):
            gate = jax.nn.sigmoid(gp_ref[:, b * D:(b + 1) * D] + bmg_ref[:, b * D:(b + 1) * D])
            acc = acc + gate * _dot(brs[b].astype(MMT), w_ref[b])
        o_ref[...] = acc.astype(o_ref.dtype)

    row = lambda w, c=0: pl.BlockSpec((t, w), lambda i: (i, c // w))
    cst = lambda shp: pl.BlockSpec(shp, lambda i: (0,) * len(shp))
    return pl.pallas_call(
        body, grid=(s // t,),
        in_specs=[row(BW), row(BW), row(BW), row(BW, RG), row(BW, GG), row(3 * D, GP), cst((1, BW)), cst((1, HD)),
                  cst((1, 3 * D)), _w_br_spec(layer)],
        out_specs=row(D), out_shape=_sds((s, D), MMT),
        compiler_params=_cparams(("parallel",)), name=name)(ret_raw, gla_raw, fox_o, zz, zz, zz, ret_g, gla_g, b_mg, w_br)


def mix_bwd(ret_raw, gla_raw, fox_o, zz, ret_g, gla_g, b_mg, w_br, dmi, *, name, layer):
    s = zz.shape[0]
    t = _row_tile(s)

    def body(r_ref, g_ref, f_ref, rg_ref, gg_ref, gp_ref, rgn_ref, ggn_ref, bmg_ref, w_ref, dmi_ref,
             dr_ref, dg_ref, df_ref, dgp_ref, dw_ref, drgn_ref, dggn_ref, dbmg_ref):
        @pl.when(pl.program_id(0) == 0)
        def _():
            dw_ref[...] = jnp.zeros_like(dw_ref)
            drgn_ref[...] = jnp.zeros_like(drgn_ref)
            dggn_ref[...] = jnp.zeros_like(dggn_ref)
            dbmg_ref[...] = jnp.zeros_like(dbmg_ref)

        (br_r, br_g), vjp = jax.vjp(_branch_f, _heads(r_ref[...]), _heads(rg_ref[...]), _heads(g_ref[...]),
                                    _heads(gg_ref[...]), _heads(rgn_ref[...]), ggn_ref[...])
        brs = [jnp.concatenate(br_r, axis=1).astype(MMT), jnp.concatenate(br_g, axis=1).astype(MMT),
               f_ref[...].astype(MMT)]
        dmi_v = dmi_ref[...].astype(F32)
        dbr = []
        for b in range(3):
            w = w_ref[b]
            ybr = _dot(brs[b], w)
            gate = jax.nn.sigmoid(gp_ref[:, b * D:(b + 1) * D] + bmg_ref[:, b * D:(b + 1) * D])
            dgp = dmi_v * ybr * gate * (1.0 - gate)
            dgp_ref[:, b * D:(b + 1) * D] = dgp.astype(dgp_ref.dtype)
            dbmg_ref[:, b * D:(b + 1) * D] += jnp.sum(dgp, axis=0, keepdims=True)
            dy = (dmi_v * gate).astype(MMT)
            dw_ref[b] += _dot_tn(brs[b], dy)
            dbr.append(_dot_nt(dy, w))
        dr, drg, dg, dgg, drgn, dggn = vjp((_heads(dbr[0]), _heads(dbr[1])))
        dr_ref[...] = jnp.concatenate(dr, axis=1)
        dg_ref[...] = jnp.concatenate(dg, axis=1)
        df_ref[...] = dbr[2]
        dgp_ref[:, RG:RG + BW] = jnp.concatenate(drg, axis=1).astype(dgp_ref.dtype)
        dgp_ref[:, GG:GG + BW] = jnp.concatenate(dgg, axis=1).astype(dgp_ref.dtype)
        drgn_ref[...] += jnp.concatenate(drgn, axis=1)
        dggn_ref[...] += dggn

    row = lambda w, c=0: pl.BlockSpec((t, w), lambda i: (i, c // w))
    cst = lambda shp: pl.BlockSpec(shp, lambda i: (0,) * len(shp))
    return pl.pallas_call(
        body, grid=(s // t,),
        in_specs=[row(BW), row(BW), row(BW), row(BW, RG), row(BW, GG), row(3 * D, GP), cst((1, BW)), cst((1, HD)),
                  cst((1, 3 * D)), _w_br_spec(layer), row(D)],
        out_specs=[row(BW), row(BW), row(BW), row(FV), cst((3, BW, D)), cst((1, BW)), cst((1, HD)), cst((1, 3 * D))],
        out_shape=[_sds((s, BW), F32)] * 3 + [_sds((s, NZZ), MMT), _sds((3, BW, D), F32), _sds((1, BW), F32),
                                              _sds((1, HD), F32), _sds((1, 3 * D), F32)],
        compiler_params=_cparams(("arbitrary",)), name=name)(ret_raw, gla_raw, fox_o, zz, zz, zz, ret_g, gla_g, b_mg, w_br, dmi)


CT = 256


def _shift_down(x, k, rows):
    return jnp.where(rows >= k, pltpu.roll(x, k, 0), 0.0)


def _shift_up(x, k, rows, s):
    return jnp.where(rows < s - k, pltpu.roll(x, s - k, 0), 0.0)


def conv_fwd(ug, w_conv, b_conv, *, name):
    s = ug.shape[0]
    nt = DFF // CT

    def body(u_ref, g_ref, w_ref, b_ref, a_ref):
        u = u_ref[...]
        rows = lax.broadcasted_iota(jnp.int32, u.shape, 0)
        uc = b_ref[...] + w_ref[0:1, :] * _shift_down(u, 2, rows) + w_ref[1:2, :] * _shift_down(u, 1, rows) + w_ref[2:3, :] * u
        a_ref[...] = (_silu(uc) * g_ref[...]).astype(a_ref.dtype)

    return pl.pallas_call(
        body, grid=(nt,),
        in_specs=[pl.BlockSpec((s, CT), lambda j: (0, j)), pl.BlockSpec((s, CT), lambda j: (0, nt + j)),
                  pl.BlockSpec((3, CT), lambda j: (0, j)), pl.BlockSpec((1, CT), lambda j: (0, j))],
        out_specs=pl.BlockSpec((s, CT), lambda j: (0, j)), out_shape=_sds((s, DFF), MMT),
        compiler_params=_cparams(("parallel",)), name=name)(ug, ug, w_conv, b_conv)


def conv_bwd(ug, w_conv, b_conv, da, *, name):
    s = ug.shape[0]
    nt = DFF // CT

    def body(u_ref, g_ref, w_ref, b_ref, da_ref, du_ref, dg_ref, dw_ref, db_ref):
        u = u_ref[...]
        rows = lax.broadcasted_iota(jnp.int32, u.shape, 0)
        u2, u1 = _shift_down(u, 2, rows), _shift_down(u, 1, rows)
        uc = b_ref[...] + w_ref[0:1, :] * u2 + w_ref[1:2, :] * u1 + w_ref[2:3, :] * u
        sg = jax.nn.sigmoid(uc)
        da_v = da_ref[...]
        dg_ref[...] = (da_v * uc * sg).astype(dg_ref.dtype)
        duc = da_v * g_ref[...] * sg * (1.0 + uc * (1.0 - sg))
        du = w_ref[2:3, :] * duc + w_ref[1:2, :] * _shift_up(duc, 1, rows, s) + w_ref[0:1, :] * _shift_up(duc, 2, rows, s)
        du_ref[...] = du.astype(du_ref.dtype)
        dw_ref[0:1, :] = jnp.sum(duc * u2, axis=0, keepdims=True)
        dw_ref[1:2, :] = jnp.sum(duc * u1, axis=0, keepdims=True)
        dw_ref[2:3, :] = jnp.sum(duc * u, axis=0, keepdims=True)
        db_ref[...] = jnp.sum(duc, axis=0, keepdims=True)

    col = lambda: pl.BlockSpec((s, CT), lambda j: (0, j))
    return pl.pallas_call(
        body, grid=(nt,),
        in_specs=[col(), pl.BlockSpec((s, CT), lambda j: (0, nt + j)), pl.BlockSpec((3, CT), lambda j: (0, j)),
                  pl.BlockSpec((1, CT), lambda j: (0, j)), col()],
        out_specs=[col(), col(), pl.BlockSpec((3, CT), lambda j: (0, j)), pl.BlockSpec((1, CT), lambda j: (0, j))],
        out_shape=[_sds((s, DFF), MMT), _sds((s, DFF), MMT), _sds((3, DFF), F32), _sds((1, DFF), F32)],
        compiler_params=_cparams(("parallel",)), name=name)(ug, ug, w_conv, b_conv, da)


def place_tail(dzz, dlr, dff, *, name):
    s = dzz.shape[0]
    t = _row_tile(s)

    def body(a_ref, b_ref, z_ref, o_ref):
        o_ref[...] = jnp.concatenate([a_ref[...], b_ref[...]], axis=1)

    spec = pl.BlockSpec((t, HD), lambda i: (i, 0))
    return pl.pallas_call(
        body, grid=(s // t,), in_specs=[spec, spec, pl.BlockSpec(memory_space=pl.ANY)],
        out_specs=pl.BlockSpec((t, 2 * HD), lambda i: (i, LR // (2 * HD))), out_shape=_sds(dzz.shape, dzz.dtype),
        input_output_aliases={2: 0}, compiler_params=_cparams(("parallel",)), name=name)(dlr, dff, dzz)


def _tiles(s):
    return min(1024, s)


def layer_fwd(x, mod, p, cosf, sinf):
    s = x.shape[0]
    tm = _tiles(s)
    l = p["l"]
    shift1, scale1, gate1, shift2, scale2, gate2 = mod
    h = norm_mod(x, p["norm1_g"], scale1, shift1, name="norm_mod")
    zz = mm_nn(h, p["w1"], tm=tm, tn=768, out_dtype=F32, name="mm_w1", layer=l)
    ret_raw, rprev = retention_fwd(zz, cosf, sinf, name="ret_fwd")
    gla_raw, sprev = gla_fwd(zz, p["w_a2p"], p["b_gla_a"], name="gla_fwd")
    qn, kn, cum = fox_pre(zz, p["q_norm_g"], p["k_norm_g"], p["b_foxp"], name="fox_pre")
    bq, bk = _fox_blocks(s)
    cum_t = cum[:, :NH].T
    cum_col, cum_row = cum_t[:, :, None], cum_t.reshape(NH, s // bk, bk)
    fox_o, lse = fox_attn_fwd(qn, kn, zz, cum_col, cum_row, name="fox_fwd")
    if "later" in p:
        p = {**p, **p["later"](fox_o)}
    mi = mix_fwd(ret_raw, gla_raw, fox_o, zz, p["ret_norm_g"], p["gla_norm_g"], p["b_mg"], p["w_br"], name="mix_fwd",
                 layer=l)
    x1, mixed = mm_nn_residual(mi, p["w_o"], x, gate1, tm=tm, tn=512, name="mm_wo", layer=l)
    h2 = norm_mod(x1, p["norm2_g"], scale2, shift2, name="norm_mod")
    ug = mm_nn(h2, p["w_up"], tm=tm, tn=512, out_dtype=F32, name="mm_wup", layer=l)
    b_conv = p["b_conv"]
    if "late_in_layer" in p:
        b_conv = b_conv + p["late_in_layer"](ug)[0, 0]
    a = conv_fwd(ug, p["w_conv"], b_conv, name="conv_fwd")
    x2, y = mm_nn_residual(a, p["w_down"], x1, gate2, tm=tm, tn=512, name="mm_wdown", layer=l)
    saved = dict(x=x, h=h, zz=zz, ret_raw=ret_raw, rprev=rprev, gla_raw=gla_raw, sprev=sprev, qn=qn, kn=kn,
                 cum_col=cum_col, cum_row=cum_row, fox_o=fox_o, lse=lse, mi=mi, mixed=mixed, x1=x1, h2=h2, ug=ug, a=a, y=y)
    return x2, saved, p


def layer_bwd(dx2, mod, p, sv, cosf, sinf, stacks, slot):
    s = dx2.shape[0]
    tm = _tiles(s)
    l = p["l"]
    shift1, scale1, gate1, shift2, scale2, gate2 = mod
    g, stacks = {}, dict(stacks)
    dy, dgate2 = gate_bwd(dx2, sv["y"], gate2, name="gate_bwd")
    stacks["w_down"] = mm_tn(sv["a"], dy, tm=min(1408, DFF), tn=512, out_dtype=MMT, name="mm_dwdown",
                             stack=stacks["w_down"], layer=slot)
    da = mm_nt(dy, p["w_down"], tm=tm, tn=1408, out_dtype=F32, name="mm_da", layer=l)
    du, dg, g["w_conv"], g["b_conv"] = conv_bwd(sv["ug"], p["w_conv"], p["b_conv"], da, name="conv_bwd")
    stacks["w_up"] = mm_tn(sv["h2"], du, tm=D, tn=CT, out_dtype=MMT, name="mm_dwup_u", stack=stacks["w_up"], layer=slot)
    stacks["w_up"] = mm_tn(sv["h2"], dg, tm=D, tn=CT, out_dtype=MMT, name="mm_dwup_g", stack=stacks["w_up"], layer=slot,
                           out_col0=DFF)
    dh2 = mm_nt2(du, dg, p["w_up"], tm=min(512, s), tn=D, name="mm_dh2", layer=l)
    dx1, g["norm2_g"], dscale2, dshift2, dmixed, dgate1 = norm_mod_gate_bwd(
        sv["x1"], dh2, dx2, p["norm2_g"], scale2, shift2, sv["mixed"], gate1, name="norm_mod_gate_bwd")
    stacks["w_o"] = mm_tn(sv["mi"], dmixed, tm=512, tn=512, out_dtype=MMT, name="mm_dwo", stack=stacks["w_o"], layer=slot)
    dmi = mm_nt(dmixed, p["w_o"], tm=tm, tn=512, out_dtype=MMT, name="mm_dmi", layer=l)
    zz = sv["zz"]
    (dret, dgla, dfox, dzz, g["w_br"], g["ret_norm_g"], g["gla_norm_g"], g["b_mg"]) = mix_bwd(
        sv["ret_raw"], sv["gla_raw"], sv["fox_o"], zz, p["ret_norm_g"], p["gla_norm_g"], p["b_mg"], p["w_br"], dmi,
        name="mix_bwd", layer=l)
    dqn, dkn, dzz, dcum_row = fox_attn_bwd(sv["qn"], sv["kn"], zz, sv["cum_col"], sv["cum_row"], sv["lse"], dfox, dzz,
                                           name="fox_bwd")
    dcum = jnp.pad(dcum_row.reshape(NH, s).T, ((0, 0), (0, HD - NH)))
    dzz, dff, g["q_norm_g"], g["k_norm_g"], g["b_foxp"] = fox_pre_bwd(
        zz, p["q_norm_g"], p["k_norm_g"], p["b_foxp"], dqn, dkn, dcum, dzz, name="fox_pre_bwd")
    dzz, dlr, g["w_a2p"], g["b_gla_a"] = gla_bwd(zz, p["w_a2p"], p["b_gla_a"], sv["sprev"], dgla, dzz, name="gla_bwd")
    dzz = retention_bwd(zz, cosf, sinf, sv["rprev"], dret, dzz, name="ret_bwd")
    dzz = place_tail(dzz, dlr, dff, name="place_tail")
    stacks["w_mg"] = mm_tn(sv["h"], dzz, tm=512, tn=768, out_dtype=MMT, name="mm_dwmg", ncols=WZ0, stack=stacks["w_mg"],
                           layer=slot)
    dwz = mm_tn(sv["h"], dzz, tm=512, tn=768, out_dtype=MMT, name="mm_dwz", col0=WZ0)
    stacks["w_in"] = unalign_dw_in(dwz, stacks["w_in"], slot)
    dh = mm_nt(dzz, p["w1"], tm=min(256, s), tn=D, out_dtype=F32, name="mm_dh", layer=l)
    dx, g["norm1_g"], dscale1, dshift1 = norm_mod_bwd(sv["x"], dh, dx1, p["norm1_g"], scale1, shift1, name="norm_mod_bwd")
    dmod = jnp.concatenate([dshift1, dscale1, dgate1, dshift2, dscale2, dgate2], axis=1)
    return dx, g, dmod, stacks


def _align_cols(w_in, w_mg):
    z = lambda n: jnp.zeros((w_in.shape[0], n), w_in.dtype)
    seg = lambda name: w_in[:, W_IN_COLS[name][0]:W_IN_COLS[name][1]]
    return jnp.concatenate([w_mg, seg("rg"), seg("gg"), seg("fv"), seg("rqkv"), seg("gqkv"), seg("fqk"), seg("lr"),
                            z(HD - GLR), seg("ff"), z(HD - NH)], axis=1)


def _unalign_cols(dwz):
    seg = lambda c0, name: dwz[:, c0 - WZ0:c0 - WZ0 + W_IN_COLS[name][1] - W_IN_COLS[name][0]]
    return jnp.concatenate([seg(RQ, "rqkv"), seg(RG, "rg"), seg(GQ, "gqkv"), seg(LR, "lr"), seg(GG, "gg"), seg(FQ, "fqk"),
                            seg(FV, "fv"), seg(FF, "ff")], axis=1)


def build_w1(w_in_sh, w_mg):
    nl = w_mg.shape[0]
    t = _row_tile(D)

    def body(s_ref, g_ref, o_ref):
        o_ref[...] = _align_cols(jnp.concatenate([s_ref[k] for k in range(4)], axis=1), g_ref[...])

    return pl.pallas_call(
        body, grid=(nl, D // t),
        in_specs=[pl.BlockSpec((None, 4, t, IN_W // 4), lambda l, i: (l, 0, i, 0)), pl.BlockSpec((None, t, WZ0), lambda l, i: (l, i, 0))],
        out_specs=pl.BlockSpec((None, t, NZZ), lambda l, i: (l, i, 0)), out_shape=_sds((nl, D, NZZ), w_mg.dtype),
        compiler_params=_cparams(("parallel", "parallel")), name="build_w1")(w_in_sh, w_mg)


def unalign_dw_in(dwz, stack, layer):
    t = _row_tile(D)

    def body(z_ref, s_ref, o_ref):
        w = _unalign_cols(z_ref[...])
        for k in range(4):
            o_ref[k] = w[:, k * (IN_W // 4):(k + 1) * (IN_W // 4)]

    return pl.pallas_call(
        body, grid=(D // t,),
        in_specs=[pl.BlockSpec((t, NZZ - WZ0), lambda i: (i, 0)), pl.BlockSpec(memory_space=pl.ANY)],
        out_specs=pl.BlockSpec((None, 4, t, IN_W // 4), lambda i: (layer, 0, i, 0)), out_shape=_sds(stack.shape, stack.dtype),
        input_output_aliases={1: 0}, compiler_params=_cparams(("parallel",)), name="unalign_dw_in")(dwz, stack)


LATE_WEIGHTS = ("w_br", "w_o", "w_up", "w_down")


def layer_params(w, big, l, later=None, late_in_layer=None):
    row = lambda v: v[l][None, :]
    p = dict(
        l=0, norm1_g=row(w["norm1_g"]), norm2_g=row(w["norm2_g"]), w1=big["w1"],
        w_a2p=jnp.pad(w["w_gla_a2"][l], ((0, HD - GLR), (0, 0))), b_gla_a=row(w["b_gla_a"]),
        b_foxp=jnp.pad(row(w["b_fox_f"]), ((0, 0), (0, HD - NH))), ret_norm_g=row(w["ret_norm_g"]),
        gla_norm_g=row(w["gla_norm_g"]), q_norm_g=row(w["q_norm_g"]), k_norm_g=row(w["k_norm_g"]),
        b_mg=row(w["b_mg"]), w_conv=w["w_conv"][l], b_conv=row(w["b_conv"]))
    if later is None:
        p.update({n: big[n] for n in LATE_WEIGHTS})
    else:
        p["later"] = later
    if late_in_layer is not None:
        p["late_in_layer"] = late_in_layer
    return p


def layer_grads(g):
    vec = lambda v: v[0]
    return dict(
        norm1_g=vec(g["norm1_g"]), norm2_g=vec(g["norm2_g"]), w_gla_a2=g["w_a2p"][:GLR], b_gla_a=vec(g["b_gla_a"]),
        b_fox_f=g["b_foxp"][0, :NH], ret_norm_g=vec(g["ret_norm_g"]), gla_norm_g=vec(g["gla_norm_g"]),
        q_norm_g=vec(g["q_norm_g"]), k_norm_g=vec(g["k_norm_g"]), w_br=g["w_br"], b_mg=vec(g["b_mg"]),
        w_conv=g["w_conv"], b_conv=vec(g["b_conv"]))


def ada_mod(c_all, w_ada, b_ada):
    nl, _, n = w_ada.shape

    def body(c_ref, w_ref, b_ref, o_ref):
        o_ref[...] = _dot(_silu(c_ref[...]), w_ref[...], HI) + b_ref[...]

    return pl.pallas_call(
        body, grid=(nl,),
        in_specs=[pl.BlockSpec((8, D), lambda l: (0, 0)), pl.BlockSpec((None, D, n), lambda l: (l, 0, 0)),
                  pl.BlockSpec((None, 1, n), lambda l: (l, 0, 0))],
        out_specs=pl.BlockSpec((None, 8, n), lambda l: (l, 0, 0)), out_shape=_sds((nl, 8, n), F32),
        compiler_params=_cparams(("parallel",)), name="ada_mod")(c_all, w_ada, b_ada)


def ada_dw(c_all, dmod):
    nl, _, n = dmod.shape

    def body(c_ref, d_ref, o_ref):
        o_ref[...] = _dot_tn(_silu(c_ref[...]), d_ref[...], HI)

    return pl.pallas_call(
        body, grid=(nl,),
        in_specs=[pl.BlockSpec((8, D), lambda l: (0, 0)), pl.BlockSpec((None, 8, n), lambda l: (l, 0, 0))],
        out_specs=pl.BlockSpec((None, D, n), lambda l: (l, 0, 0)), out_shape=_sds((nl, D, n), F32),
        compiler_params=_cparams(("parallel",)), name="ada_dw")(c_all, dmod)


def sum_devices(g):
    def body(g_ref, o_ref):
        acc = g_ref[0]
        for d in range(1, 8):
            acc = acc + g_ref[d]
        o_ref[...] = acc

    return pl.pallas_call(body, out_shape=_sds(g.shape[1:], F32), name="sum_devices")(g)


def adamw(w, g, m, v, *, block, name, rows=None, into=None, with_grad=False):
    nd = w.ndim
    lo, hi = (0, w.shape[0]) if rows is None else rows
    grid = ((hi - lo) // block[0],) + tuple(w.shape[i] // block[i] for i in range(1, nd))
    first = lo // block[0]
    nout = 4 if with_grad else 3
    bc1 = 1.0 - ADAM_B1 ** ADAM_STEP
    bc2 = 1.0 - ADAM_B2 ** ADAM_STEP

    def body(w_ref, g_ref, m_ref, v_ref, *rest):
        d_ref, nm_ref, nv_ref = rest[-nout:][:3]
        gv = g_ref[...]
        if with_grad:
            rest[-1][...] = gv
        nm = ADAM_B1 * m_ref[...] + (1.0 - ADAM_B1) * gv
        nv = ADAM_B2 * v_ref[...] + (1.0 - ADAM_B2) * (gv * gv)
        nm_ref[...] = nm
        nv_ref[...] = nv
        d_ref[...] = -ADAM_LR * ((nm / bc1) / (jnp.sqrt(nv / bc2) + ADAM_EPS) + ADAM_WD * w_ref[...])

    spec = pl.BlockSpec(tuple(block), lambda i, *j: (first + i,) + j)
    given = [] if into is None else list(into)
    return pl.pallas_call(
        body, grid=grid, in_specs=[spec] * 4 + [pl.BlockSpec(memory_space=pl.ANY)] * len(given), out_specs=[spec] * nout,
        out_shape=[_sds(w.shape, F32)] * nout, input_output_aliases={4 + i: i for i in range(len(given))},
        compiler_params=_cparams(("parallel",) * nd), name=name)(w, g, m, v, *given)


MESH = pl.DeviceIdType.MESH
ANY = pl.BlockSpec(memory_space=pl.ANY)
VM = pl.BlockSpec(memory_space=pltpu.VMEM)


def _place():
    x, y, c = lax.axis_index("x"), lax.axis_index("y"), lax.axis_index("c")
    return x, y, c, [(1 - x, y), (x, 1 - y), (1 - x, 1 - y)]


def small_allgather(v, *, name):
    m_per, n = v.shape

    def body(x_ref, out_ref, send_sems, recv_sems, local_sem):
        x, y, c, chips = _place()
        me, sibling = (x, y, c), (x, y, 1 - c)

        def rows(px, py, pc):
            return out_ref.at[pl.ds((4 * px + 2 * py + pc) * m_per, m_per), :]

        def copy(k, block, to, src=None):
            return pltpu.make_async_remote_copy(
                src_ref=rows(*block) if src is None else src, dst_ref=rows(*block),
                send_sem=send_sems.at[k], recv_sem=recv_sems.at[k], device_id=to, device_id_type=MESH)

        mine = pltpu.make_async_copy(x_ref, rows(*me), local_sem)
        mine.start()
        first = [copy(0, me, sibling, src=x_ref)]
        first += [copy(1 + j, me, (*chip, c), src=x_ref) for j, chip in enumerate(chips)]
        for cp in first:
            cp.start()
        passed = [copy(4 + j, (*chip, c), sibling) for j, chip in enumerate(chips)]
        for j, chip in enumerate(chips):
            copy(1 + j, (*chip, c), me).wait_recv()
            passed[j].start()
        copy(0, sibling, me).wait_recv()
        for j, chip in enumerate(chips):
            copy(4 + j, (*chip, 1 - c), me).wait_recv()
        for cp in first + passed:
            cp.wait_send()
        mine.wait()

    return pl.pallas_call(
        body, out_shape=_sds((8 * m_per, n), v.dtype), in_specs=[VM], out_specs=VM,
        scratch_shapes=[pltpu.SemaphoreType.DMA((7,)), pltpu.SemaphoreType.DMA((7,)), pltpu.SemaphoreType.DMA],
        name=name)(v)


TENSORS = {
    "w_in": ("lead", None, (4, D, 1285), (1, D, 1285)),
    "w_mg": ("col", 768, (D, 3072), (512, 3072)),
    "w_br": ("col", 256, (3, BW, D), (3, BW, D)),
    "w_o": ("row", 256, (D, D), (D, D)),
    "w_up": ("col", 1408, (D, 5632), (256, 5632)),
    "w_down": ("row", 704, (DFF, D), (704, D)),
}
BIG = tuple(TENSORS)


def _shard_shape(name):
    kind, width, full, _ = TENSORS[name]
    if kind == "lead":
        return full[1:]
    return full[:-1] + (width,) if kind == "col" else (width,) + full[1:]


def _shard_view(ref, layers, name, k):
    kind, width, full, _ = TENSORS[name]
    if kind == "lead":
        return ref.at[layers, k]
    if kind == "row":
        return ref.at[layers, pl.ds(k * width, width)]
    return ref.at[(layers,) + (slice(None),) * (len(full) - 1) + (pl.ds(k * width, width),)]


def _remote(send_sems, recv_sems, k, src, dst, to):
    return pltpu.make_async_remote_copy(src_ref=src, dst_ref=dst, send_sem=send_sems.at[k], recv_sem=recv_sems.at[k],
                                        device_id=to, device_id_type=MESH)


def _dma_sems(n):
    return [pltpu.SemaphoreType.DMA((n,)), pltpu.SemaphoreType.DMA((n,))]


RS_GROUP = 2
HBM = pl.BlockSpec(memory_space=pltpu.HBM)
SEM = pl.BlockSpec(memory_space=pltpu.SEMAPHORE)
SPLIT_CALL = dict(compiler_params=pltpu.CompilerParams(has_side_effects=pltpu.SideEffectType.DATAFLOW_SIDE_EFFECTING))
PULL_SET = (("w_mg", "w_up", "w_o"), ("w_in", "w_br", "w_down"))
FIRST_NEEDED = ("w_in", "w_mg")
NEEDED_LATER = tuple(n for n in BIG if n not in FIRST_NEEDED)


def _in_hbm(a):
    return pltpu.with_memory_space_constraint(a, pltpu.HBM)


def _pull_sends(send_sems, recv_sems, p, o, layer, core, x, y, chips, names=BIG):
    return [_remote(send_sems, recv_sems, 3 * BIG.index(n) + j, p[n].at[layer], _shard_view(o[n], 0, n, 2 * x + y), (*chip, core))
            for n in PULL_SET[core] if n in names for j, chip in enumerate(chips)]


def _pull_arrivals(send_sems, recv_sems, o, core, x, y, chips, to, names=BIG):
    views = [(3 * BIG.index(n) + j, _shard_view(o[n], 0, n, 2 * chip[0] + chip[1]))
             for n in PULL_SET[core] if n in names for j, chip in enumerate(chips)]
    return [_remote(send_sems, recv_sems, k, v, v, to) for k, v in views]


def gather_start(shards, layer, *, name, after=None, names=BIG, lands=None):
    nt = len(BIG)
    first = [] if after is None else [after]

    def body(*refs):
        p, o = dict(zip(BIG, refs[:nt])), dict(zip(BIG, refs[nt:2 * nt]))
        ss, rs = refs[2 * nt + len(first)], refs[2 * nt + len(first) + 1]
        x, y, c, chips = _place()
        for core in (0, 1):
            @pl.when(c == core)
            def _():
                for cp in _pull_sends(ss, rs, p, o, layer, core, x, y, chips, names):
                    cp.start()
        refs[-1][...] = jnp.zeros_like(refs[-1])

    if lands is None:
        lands = [lax.empty((1,) + TENSORS[n][2], shards[n].dtype) for n in BIG]
    lands = [_in_hbm(a) for a in lands]
    outs = pl.pallas_call(
        body,
        out_shape=(pltpu.SemaphoreType.DMA((3 * nt,)), pltpu.SemaphoreType.DMA((3 * nt,)),
                   *[pltpu.HBM(a.shape, a.dtype) for a in lands], _sds((8, HD), F32)),
        in_specs=[HBM] * (2 * nt) + [ANY] * len(first), out_specs=(SEM, SEM, *[HBM] * nt, VM),
        input_output_aliases={nt + t: 2 + t for t in range(nt)}, name=name, **SPLIT_CALL)(
            *[_in_hbm(shards[n]) for n in BIG], *lands, *first)
    return outs[0], outs[1], outs[2:2 + nt], outs[-1]


def gather_wait(send_sems, recv_sems, shards, lands, after, layer, *, name, names=BIG):
    nt = len(BIG)

    def body(*refs):
        p, o = dict(zip(BIG, refs[:nt])), dict(zip(BIG, refs[nt:2 * nt]))
        ss, rs = refs[2 * nt], refs[2 * nt + 1]
        x, y, c, chips = _place()
        for core in (0, 1):
            @pl.when(c == core)
            def _():
                for cp in _pull_sends(ss, rs, p, o, layer, core, x, y, chips, names):
                    cp.wait_send()
                for cp in _pull_arrivals(ss, rs, o, core, x, y, chips, (x, y, core), names):
                    cp.wait_recv()

    return pl.pallas_call(
        body, out_shape=tuple(pltpu.HBM(a.shape, a.dtype) for a in lands),
        in_specs=[HBM] * (2 * nt) + [SEM, SEM] + [ANY] * len(after), out_specs=tuple([HBM] * nt),
        input_output_aliases={nt + t: t for t in range(nt)}, name=name, **SPLIT_CALL)(
            *[_in_hbm(shards[n]) for n in BIG], *lands, send_sems, recv_sems, *after)


def gather_forward(shards, lands, layer, *, name, names=BIG):
    nt = len(BIG)

    def body(*refs):
        p, o = dict(zip(BIG, refs[:nt])), dict(zip(BIG, refs[2 * nt:3 * nt]))
        ss, rs = refs[3 * nt:]
        x, y, c, chips = _place()
        for core in (0, 1):
            @pl.when(c == core)
            def _():
                me, sibling = (x, y, core), (x, y, 1 - core)
                sends = _pull_arrivals(ss, rs, o, core, x, y, chips, sibling, names)
                sends += [_remote(ss, rs, 3 * nt + t, p[n].at[layer], _shard_view(o[n], 0, n, 2 * x + y), sibling)
                          for t, n in enumerate(BIG) if n in names]
                for cp in sends:
                    cp.start()
                for cp in sends:
                    cp.wait_send()
                for cp in _pull_arrivals(ss, rs, o, 1 - core, x, y, chips, me, names):
                    cp.wait_recv()
                for t, n in enumerate(BIG):
                    if n in names:
                        own = _shard_view(o[n], 0, n, 2 * x + y)
                        _remote(ss, rs, 3 * nt + t, own, own, me).wait_recv()

    outs = pl.pallas_call(
        body, out_shape=[_sds(a.shape, a.dtype) for a in lands], in_specs=[ANY] * (2 * nt), out_specs=[ANY] * nt,
        input_output_aliases={nt + t: t for t in range(nt)}, scratch_shapes=_dma_sems(4 * nt), name=name)(
            *[shards[n] for n in BIG], *lands)
    return dict(zip(BIG, outs))


def _forward_copies(ss, rs, p, o, layer, core, x, y, chips):
    sibling, nt = (x, y, 1 - core), len(BIG)
    return _pull_arrivals(ss, rs, o, core, x, y, chips, sibling) + [
        _remote(ss, rs, 3 * nt + t, p[n].at[layer], _shard_view(o[n], 0, n, 2 * x + y), sibling) for t, n in enumerate(BIG)]


def _forward_arrivals(ss, rs, o, core, x, y, chips):
    me, nt = (x, y, core), len(BIG)
    own = [(3 * nt + t, _shard_view(o[n], 0, n, 2 * x + y)) for t, n in enumerate(BIG)]
    return _pull_arrivals(ss, rs, o, 1 - core, x, y, chips, me) + [_remote(ss, rs, k, v, v, me) for k, v in own]


def gather_forward_start(shards, lands, layer, *, name):
    nt = len(BIG)

    def body(*refs):
        p, o = dict(zip(BIG, refs[:nt])), dict(zip(BIG, refs[nt:2 * nt]))
        x, y, c, chips = _place()
        for core in (0, 1):
            @pl.when(c == core)
            def _():
                for cp in _forward_copies(refs[2 * nt], refs[2 * nt + 1], p, o, layer, core, x, y, chips):
                    cp.start()
        refs[-1][...] = jnp.zeros_like(refs[-1])

    lands = [_in_hbm(a) for a in lands]
    outs = pl.pallas_call(
        body,
        out_shape=(pltpu.SemaphoreType.DMA((4 * nt,)), pltpu.SemaphoreType.DMA((4 * nt,)),
                   *[pltpu.HBM(a.shape, a.dtype) for a in lands], _sds((8, HD), F32)),
        in_specs=[HBM] * (2 * nt), out_specs=(SEM, SEM, *[HBM] * nt, VM),
        input_output_aliases={nt + t: 2 + t for t in range(nt)}, name=name, **SPLIT_CALL)(
            *[_in_hbm(shards[n]) for n in BIG], *lands)
    return outs[0], outs[1], outs[2:2 + nt], outs[-1]


def gather_forward_wait(send_sems, recv_sems, shards, lands, after, layer, *, name):
    nt = len(BIG)

    def body(*refs):
        p, o = dict(zip(BIG, refs[:nt])), dict(zip(BIG, refs[nt:2 * nt]))
        ss, rs = refs[2 * nt], refs[2 * nt + 1]
        x, y, c, chips = _place()
        for core in (0, 1):
            @pl.when(c == core)
            def _():
                for cp in _forward_copies(ss, rs, p, o, layer, core, x, y, chips):
                    cp.wait_send()
                for cp in _forward_arrivals(ss, rs, o, core, x, y, chips):
                    cp.wait_recv()

    outs = pl.pallas_call(
        body, out_shape=tuple(pltpu.HBM(a.shape, a.dtype) for a in lands),
        in_specs=[HBM] * (2 * nt) + [SEM, SEM] + [ANY] * len(after), out_specs=tuple([HBM] * nt),
        input_output_aliases={nt + t: t for t in range(nt)}, name=name, **SPLIT_CALL)(
            *[_in_hbm(shards[n]) for n in BIG], *lands, send_sems, recv_sems, *after)
    return dict(zip(BIG, outs))


def pair_exchange(g, *, name):
    hh = g[BIG[0]].shape[0] // 2
    nt = len(BIG)

    def body(*refs):
        send_sems, recv_sems = refs[2 * nt:]
        x, y, c, _ = _place()
        copies = [_remote(send_sems, recv_sems, t, refs[t].at[pl.ds(hh * (1 - c), hh)], refs[nt + t], (x, y, 1 - c))
                  for t in range(nt)]
        for cp in copies:
            cp.start()
        for cp in copies:
            cp.wait()

    outs = pl.pallas_call(
        body, out_shape=[_sds((hh,) + g[n].shape[1:], g[n].dtype) for n in BIG], in_specs=[ANY] * nt, out_specs=[ANY] * nt,
        scratch_shapes=_dma_sems(nt), name=name)(*[g[n] for n in BIG])
    return dict(zip(BIG, outs))


def _chip_copies(send_sems, recv_sems, s_refs, land_refs, c, chips):
    hl = s_refs[0].shape[0]
    return [_remote(send_sems, recv_sems, 3 * t + j, _shard_view(s_refs[t], pl.ds(0, hl), n, 2 * chip[0] + chip[1]),
                    land_refs[t].at[j], (*chip, c))
            for t, n in enumerate(BIG) for j, chip in enumerate(chips)]


def _landing_shapes(s):
    hl = s[BIG[0]].shape[0]
    return [_sds((3, hl) + _shard_shape(n), s[n].dtype) for n in BIG]


def chip_exchange(s, *, name):
    nt = len(BIG)

    def body(*refs):
        send_sems, recv_sems = refs[2 * nt:]
        x, y, c, chips = _place()
        copies = _chip_copies(send_sems, recv_sems, refs[:nt], refs[nt:2 * nt], c, chips)
        for cp in copies:
            cp.start()
        for cp in copies:
            cp.wait()

    outs = pl.pallas_call(
        body, out_shape=_landing_shapes(s), in_specs=[ANY] * nt, out_specs=[ANY] * nt,
        scratch_shapes=_dma_sems(3 * nt), name=name)(*[s[n] for n in BIG])
    return dict(zip(BIG, outs))


def chip_exchange_start(s, *, name, after=None):
    nt = len(BIG)
    first = [] if after is None else [after]

    def body(*refs):
        o = refs[2 * nt + len(first):]
        x, y, c, chips = _place()
        for cp in _chip_copies(o[0], o[1], refs[:nt], refs[nt:2 * nt], c, chips):
            cp.start()
        refs[-1][...] = jnp.zeros_like(refs[-1])

    lands = [_in_hbm(lax.empty(d.shape, d.dtype)) for d in _landing_shapes(s)]
    srcs = [_in_hbm(s[n]) for n in BIG]
    outs = pl.pallas_call(
        body,
        out_shape=(pltpu.SemaphoreType.DMA((3 * nt,)), pltpu.SemaphoreType.DMA((3 * nt,)),
                   *[pltpu.HBM(a.shape, a.dtype) for a in srcs + lands], _sds((8, HD), F32)),
        in_specs=[HBM] * (2 * nt) + [ANY] * len(first), out_specs=(SEM, SEM, *[HBM] * (2 * nt), VM),
        input_output_aliases={t: 2 + t for t in range(2 * nt)}, name=name, **SPLIT_CALL)(*srcs, *lands, *first)
    return outs[0], outs[1], outs[2:2 + nt], outs[2 + nt:2 + 2 * nt], outs[-1]


def chip_exchange_wait(send_sems, recv_sems, srcs, lands, after, *, name):
    nt = len(BIG)

    def body(*refs):
        x, y, c, chips = _place()
        for cp in _chip_copies(refs[2 * nt], refs[2 * nt + 1], refs[:nt], refs[nt:2 * nt], c, chips):
            cp.wait_send()
            cp.wait_recv()

    outs = pl.pallas_call(
        body, out_shape=tuple(pltpu.HBM(a.shape, a.dtype) for a in list(srcs) + list(lands)),
        in_specs=[HBM] * (2 * nt) + [SEM, SEM] + [ANY] * len(after), out_specs=tuple([HBM] * (2 * nt)),
        input_output_aliases={t: t for t in range(2 * nt)}, name=name, **SPLIT_CALL)(
            *srcs, *lands, send_sems, recv_sems, *after)
    return dict(zip(BIG, outs[:nt])), dict(zip(BIG, outs[nt:]))


def pair_share(f, l0, hh, *, name):
    nt = len(BIG)

    def body(*refs):
        o = refs[nt:2 * nt]
        send_sems, recv_sems = refs[2 * nt:]
        x, y, c, _ = _place()
        mine, theirs = pl.ds(l0 + hh * c, hh), pl.ds(l0 + hh * (1 - c), hh)
        copies = [_remote(send_sems, recv_sems, t, o[t].at[mine], o[t].at[mine], (x, y, 1 - c)) for t in range(nt)]
        for cp in copies:
            cp.start()
        for t, cp in enumerate(copies):
            cp.wait_send()
            _remote(send_sems, recv_sems, t, o[t].at[theirs], o[t].at[theirs], (x, y, c)).wait_recv()

    outs = pl.pallas_call(
        body, out_shape=[_sds(f[n].shape, f[n].dtype) for n in BIG], in_specs=[ANY] * nt, out_specs=[ANY] * nt,
        input_output_aliases={t: t for t in range(nt)}, scratch_shapes=_dma_sems(nt), name=name)(*[f[n] for n in BIG])
    return dict(zip(BIG, outs))


def pair_add(g, r, idx, *, tensor, name):
    _, _, full, blk = TENSORS[tensor]
    hh = r.shape[0]

    def body(idx_ref, g_ref, r_ref, o_ref):
        o_ref[...] = (g_ref[...].astype(F32) + r_ref[...].astype(F32)).astype(o_ref.dtype)

    own = pl.BlockSpec((None,) + blk, lambda *a: (a[0],) + a[1:-1])
    return pl.pallas_call(
        body, out_shape=_sds(r.shape, r.dtype),
        grid_spec=pltpu.PrefetchScalarGridSpec(
            num_scalar_prefetch=1, grid=(hh,) + tuple(f // b for f, b in zip(full, blk)),
            in_specs=[pl.BlockSpec((None,) + blk, lambda *a: (hh * a[-1][0] + a[0],) + a[1:-1]), own], out_specs=own),
        compiler_params=_cparams(("parallel",) * (1 + len(full))), name=name)(idx, g, r)


def chip_add(s, r, idx, totals, l0, *, tensor, name):
    kind, width, full, _ = TENSORS[tensor]
    shard = _shard_shape(tensor)
    hh = s.shape[0]
    zeros = (0,) * len(shard)

    def body(idx_ref, s_ref, r0_ref, r1_ref, r2_ref, t_ref, o_ref):
        o_ref[...] = ((s_ref[...].astype(F32) + r0_ref[...].astype(F32)) + r1_ref[...].astype(F32)) + r2_ref[...].astype(F32)

    if kind == "lead":
        mine = pl.BlockSpec((None, None) + shard, lambda i, ix: (i, ix[1]) + zeros)
    elif kind == "row":
        mine = pl.BlockSpec((None,) + shard, lambda i, ix: (i, ix[1]) + zeros[1:])
    else:
        mine = pl.BlockSpec((None,) + shard, lambda i, ix: (i,) + zeros[1:] + (ix[1],))
    peer = lambda j: pl.BlockSpec((None, None) + shard, lambda i, ix: (j, i) + zeros)
    return pl.pallas_call(
        body, out_shape=_sds(totals.shape, F32),
        grid_spec=pltpu.PrefetchScalarGridSpec(
            num_scalar_prefetch=1, grid=(hh,), in_specs=[mine, peer(0), peer(1), peer(2), pl.BlockSpec(memory_space=pl.ANY)],
            out_specs=pl.BlockSpec((None,) + shard, lambda i, ix: (l0 + hh * ix[0] + i,) + zeros)),
        input_output_aliases={5: 0}, compiler_params=_cparams(("parallel",)), name=name)(idx, s, r, r, r, totals)


def _flat_rows(arrs):
    v = jnp.concatenate([a.reshape(-1) for a in arrs])
    n = -(-v.shape[0] // 1024) * 1024
    return jnp.pad(v, (0, n - v.shape[0])).reshape(n // HD, HD)


def _unflat(buf, shapes):
    v, out, o = buf.reshape(-1), [], 0
    for s in shapes:
        n = int(np.prod(s))
        out.append(v[o:o + n].reshape(s))
        o += n
    return out


WEIGHTS = ("norm1_g", "norm2_g", "w_ada", "b_ada", "w_in", "w_gla_a2", "b_gla_a", "b_fox_f", "ret_norm_g", "gla_norm_g",
           "q_norm_g", "k_norm_g", "w_br", "w_mg", "b_mg", "w_o", "w_up", "w_conv", "b_conv", "w_down")
REPLICATED = ("norm1_g", "norm2_g", "b_gla_a", "b_fox_f", "ret_norm_g", "gla_norm_g", "q_norm_g", "k_norm_g", "b_mg", "b_conv")
ADAM_BLOCKS = dict(w_ada=(1, 256, 1536), w_in=(1, 256, 1285), w_br=(1, 3, BW, 256), w_mg=(1, 512, 768), w_o=(2, 256, D),
                   w_up=(1, 256, 1408), w_down=(1, 352, D))
ALL_AXES = ("x", "y", "c")


def kernel(x, c, norm1_g, norm2_g, w_ada, b_ada, w_in, w_gla_a2, b_gla_a, b_fox_f, ret_norm_g, gla_norm_g, q_norm_g, k_norm_g, w_br, w_mg, b_mg, w_o, w_up, w_conv, b_conv, w_down, loss_target, m_norm1_g, m_norm2_g, m_w_ada, m_b_ada, m_w_in, m_w_gla_a2, m_b_gla_a, m_b_fox_f, m_ret_norm_g, m_gla_norm_g, m_q_norm_g, m_k_norm_g, m_w_br, m_w_mg, m_b_mg, m_w_o, m_w_up, m_w_conv, m_b_conv, m_w_down, v_norm1_g, v_norm2_g, v_w_ada, v_b_ada, v_w_in, v_w_gla_a2, v_b_gla_a, v_b_fox_f, v_ret_norm_g, v_gla_norm_g, v_q_norm_g, v_k_norm_g, v_w_br, v_w_mg, v_b_mg, v_w_o, v_w_up, v_w_conv, v_b_conv, v_w_down):
    w = dict(zip(WEIGHTS, (norm1_g, norm2_g, w_ada, b_ada, w_in, w_gla_a2, b_gla_a, b_fox_f, ret_norm_g, gla_norm_g,
                           q_norm_g, k_norm_g, w_br, w_mg, b_mg, w_o, w_up, w_conv, b_conv, w_down)))
    m = dict(zip(WEIGHTS, (m_norm1_g, m_norm2_g, m_w_ada, m_b_ada, m_w_in, m_w_gla_a2, m_b_gla_a, m_b_fox_f, m_ret_norm_g,
                           m_gla_norm_g, m_q_norm_g, m_k_norm_g, m_w_br, m_w_mg, m_b_mg, m_w_o, m_w_up, m_w_conv, m_b_conv,
                           m_w_down)))
    v = dict(zip(WEIGHTS, (v_norm1_g, v_norm2_g, v_w_ada, v_b_ada, v_w_in, v_w_gla_a2, v_b_gla_a, v_b_fox_f, v_ret_norm_g,
                           v_gla_norm_g, v_q_norm_g, v_k_norm_g, v_w_br, v_w_mg, v_b_mg, v_w_o, v_w_up, v_w_conv, v_b_conv,
                           v_w_down)))
    nl = norm1_g.shape[0]
    seq = x.shape[1]
    xi, yi, ci = lax.axis_index("x"), lax.axis_index("y"), lax.axis_index("c")
    k_me = 2 * xi + yi
    b_me = 4 * xi + 2 * yi + ci
    ada_n = w_ada.shape[2]
    a2_n, conv_n = w_gla_a2.shape[2], w_conv.shape[2]

    shards = [{n: w[n][:1].astype(MMT) for n in BIG}]

    def gather_finish(l, started, after):
        send_sems, recv_sems, lands, _ = started
        lands = gather_wait(send_sems, recv_sems, shards[l], lands, [after], 0, name=f"gather{l}_wait")
        big = gather_forward(shards[l], lands, 0, name=f"gather{l}_forward")
        big["w1"] = build_w1(big["w_in"], big["w_mg"])
        return big

    blk = _flat_rows([c, w_gla_a2, w_conv])
    g1 = small_allgather(blk, name="gather_small").reshape(8, blk.shape[0], HD)
    c_all = g1[:, :D // HD].reshape(8, D)
    by_chip = g1[0::2].reshape(4, -1)[:, D:]
    a2_sh, conv_sh = by_chip[:, :nl * GLR * a2_n], by_chip[:, nl * GLR * a2_n:nl * (GLR * a2_n + 3 * conv_n)]
    full_small = dict(
        w_gla_a2=a2_sh.reshape(4, nl, GLR, a2_n).transpose(1, 2, 0, 3).reshape(nl, GLR, 4 * a2_n),
        w_conv=conv_sh.reshape(4, nl, 3, conv_n).transpose(1, 2, 0, 3).reshape(nl, 3, 4 * conv_n))

    b_ada_sh = lax.dynamic_slice_in_dim(b_ada, k_me * ada_n, ada_n, axis=1)[:, None, :]
    mod_sh = ada_mod(c_all, w_ada, b_ada_sh)
    g2 = small_allgather(mod_sh.reshape(nl * 8, ada_n), name="gather_mod").reshape(4, 2, nl, 8, ada_n)[:, 0]
    mod_me = lax.dynamic_index_in_dim(g2, b_me, axis=2, keepdims=False).transpose(1, 0, 2).reshape(nl, 4 * ada_n)

    wsmall = {n: w[n] for n in REPLICATED}
    wsmall.update(full_small)
    mods = [[mod_me[l:l + 1, i * D:(i + 1) * D] for i in range(6)] for l in range(nl)]

    send_sems, recv_sems, lands, token = gather_start(shards[0], 0, name="gather0_start_first", after=mod_me,
                                                      names=FIRST_NEEDED)
    shards += [{n: (w[n][l:l + 1] + token[0, 0]).astype(MMT) for n in BIG} for l in range(1, nl)]
    cast_meanwhile = [a for sh in shards[1:] for a in sh.values()]
    lands = gather_wait(send_sems, recv_sems, shards[0], lands, [token, *cast_meanwhile], 0, name="gather0_wait_first",
                        names=FIRST_NEEDED)
    first = gather_forward(shards[0], lands, 0, name="gather0_forward_first", names=FIRST_NEEDED)
    later_sems = gather_start(shards[0], 0, name="gather0_start_later", names=NEEDED_LATER, lands=[first[n] for n in BIG])
    in_flight = dict(zip(BIG, later_sems[2]))
    big = {"w1": build_w1(in_flight["w_in"], in_flight["w_mg"])}

    def rest_of_layer0(after):
        rest = gather_wait(later_sems[0], later_sems[1], shards[0], later_sems[2], [after], 0, name="gather0_wait_later",
                           names=NEEDED_LATER)
        return gather_forward(shards[0], rest, 0, name="gather0_forward_later", names=NEEDED_LATER)

    cosf, sinf = _rope_tables(seq)
    xs, saved, params = x[0], [], []
    for l in range(nl):
        if l + 1 < nl:
            started = gather_start(shards[l + 1], 0, name=f"gather{l + 1}_start", after=xs if l else big["w1"])
            mods[l][1] = mods[l][1] + started[-1][0, 0]
        trade = []

        def trade_next_layer(after, l=l, started=started if l + 1 < nl else None):
            lands = gather_wait(started[0], started[1], shards[l + 1], started[2], [after], 0, name=f"gather{l + 1}_wait")
            trade.append(gather_forward_start(shards[l + 1], lands, 0, name=f"gather{l + 1}_forward_start"))
            return trade[0][-1]

        xs, sv, p = layer_fwd(xs, mods[l], layer_params(wsmall, big, l, later=rest_of_layer0 if l == 0 else None,
                                                        late_in_layer=trade_next_layer if l + 1 < nl else None),
                              cosf, sinf)
        saved.append(sv)
        params.append(p)
        if trade:
            big = gather_forward_wait(trade[0][0], trade[0][1], shards[l + 1], trade[0][2], [xs], 0,
                                      name=f"gather{l + 1}_forward_wait")
            big["w1"] = build_w1(big["w_in"], big["w_mg"])
        elif l + 1 < nl:
            big = gather_finish(l + 1, started, xs)
    loss_part, dx = loss_and_grad(xs, loss_target[0], name="loss")
    loss = lax.psum(loss_part[0, 0], ALL_AXES)
    grads, dmods = [None] * nl, [None] * nl
    idx = jnp.stack([ci, k_me]).astype(jnp.int32)
    totals = {n: lax.empty((nl,) + _shard_shape(n), F32) for n in BIG}

    def finish_group(pending, after, totals, idx):
        group, send_sems, recv_sems, srcs, lands, _ = pending
        sums, from_chips = chip_exchange_wait(send_sems, recv_sems, srcs, lands, after, name=f"rs{group}_chip_exchange_wait")
        totals = {n: chip_add(sums[n], from_chips[n], idx, totals[n], RS_GROUP * group, tensor=n,
                              name=f"rs{group}_chip_add_{n}") for n in BIG}
        return pair_share(totals, RS_GROUP * group, RS_GROUP // 2, name=f"rs{group}_pair_share")

    pending = None
    for group in reversed(range(nl // RS_GROUP)):
        layers = range(RS_GROUP * group, RS_GROUP * (group + 1))
        stacks = {n: lax.empty((RS_GROUP,) + TENSORS[n][2], MMT) for n in BIG if n != "w_br"}
        if pending is not None:
            mods[layers[-1]][5] = mods[layers[-1]][5] + pending[-1][0, 0]
        for l in reversed(layers):
            dx, g, dmods[l], stacks = layer_bwd(dx, mods[l], params[l], saved[l], cosf, sinf, stacks, l - layers[0])
            grads[l] = layer_grads(g)
        stacks["w_br"] = jnp.stack([grads[l]["w_br"].astype(MMT) for l in layers])
        from_sibling = pair_exchange(stacks, name=f"rs{group}_pair_exchange")
        chip_sum = {n: pair_add(stacks[n], from_sibling[n], idx, tensor=n, name=f"rs{group}_pair_add_{n}") for n in BIG}
        if group > 0:
            if pending is not None:
                totals = finish_group(pending, [dx, *chip_sum.values()], totals, idx)
            pending = (group, *chip_exchange_start(chip_sum, name=f"rs{group}_chip_exchange_start"))

    small_names = REPLICATED + ("w_gla_a2", "w_conv")
    small_shapes = [(nl, 6 * D)] + [(nl,) + grads[0][n].shape for n in small_names]
    vec = _flat_rows([jnp.concatenate(dmods, axis=0)] + [jnp.stack([grads[l][n] for l in range(nl)]) for n in small_names])
    gs = small_allgather(vec, name="gather_small_grads")
    earlier = pending
    pending = (0, *chip_exchange_start(chip_sum, name="rs0_chip_exchange_start", after=gs))
    if earlier is not None:
        totals = finish_group(earlier, [dx, pending[-1]], totals, idx)
    gs = (gs + pending[-1][0, 0]).reshape(8, vec.shape[0], HD)
    summed = _unflat(sum_devices(gs), small_shapes)
    grad = dict(zip(small_names, summed[1:]))
    grad["b_ada"] = summed[0]
    grad["w_gla_a2"] = lax.dynamic_slice_in_dim(grad["w_gla_a2"], k_me * a2_n, a2_n, axis=2)
    grad["w_conv"] = lax.dynamic_slice_in_dim(grad["w_conv"], k_me * conv_n, conv_n, axis=2)
    dmod_all = gs[:, :nl * 6 * D // HD].reshape(8, nl, 6 * D)
    dmod_sh = lax.dynamic_slice_in_dim(dmod_all, k_me * ada_n, ada_n, axis=2).transpose(1, 0, 2)
    grad["w_ada"] = ada_dw(c_all, dmod_sh)

    delta, new_m, new_v = {}, {}, {}
    delta["w_ada"], new_m["w_ada"], new_v["w_ada"] = adamw(w["w_ada"], grad["w_ada"], m["w_ada"], v["w_ada"],
                                                          block=ADAM_BLOCKS["w_ada"], name="adamw_w_ada")
    rest = [n for n in WEIGHTS if n not in ADAM_BLOCKS]
    shapes = [w[n].shape for n in rest]
    flat = [_flat_rows([t[n] for n in rest]) for t in (w, grad, m, v)]
    outs = adamw(*flat, block=flat[0].shape, name="adamw_small")
    for t, o in zip((delta, new_m, new_v), outs):
        t.update(zip(rest, _unflat(o, shapes)))

    later = {n: adamw(w[n], totals[n], m[n], v[n], block=ADAM_BLOCKS[n], name="adamw_later_" + n, rows=(RS_GROUP, nl),
                      with_grad=True) for n in BIG} if nl > RS_GROUP else {}
    done_first = [outs[0], delta["w_ada"]] + [later[n][0] for n in later]
    totals = finish_group(pending, done_first, totals, idx)
    for n in BIG:
        delta[n], new_m[n], new_v[n], grad[n] = adamw(w[n], totals[n], m[n], v[n], block=ADAM_BLOCKS[n], name="adamw_" + n,
                                                      rows=(0, min(RS_GROUP, nl)), into=later.get(n), with_grad=True)

    return (loss, dx[None], *[grad[n] for n in WEIGHTS], *[delta[n] for n in WEIGHTS], *[new_m[n] for n in WEIGHTS],
            *[new_v[n] for n in WEIGHTS])
```
